```python
import jax, jax.numpy as jnp
from jax import lax
import numpy as np

D_MODEL = 1024
BATCH = 8
SEQ = 8192
DEPTH = 1

CHUNK = 64
D_MIX = D_MODEL
LRU_WIDTH = D_MIX // 2
LRU_HEADS = 8
LRU_HEAD_DIM = LRU_WIDTH // LRU_HEADS
CONV_WIDTH = 4
LRU_C = 8.0
POOL_WIDTH = D_MIX - LRU_WIDTH
POOL_WINDOWS = (2, 4, 8, 16)
POOL_GROUPS = len(POOL_WINDOWS)
POOL_GROUP_DIM = POOL_WIDTH // POOL_GROUPS
D_IN = 2 * LRU_WIDTH + POOL_WIDTH
D_FF = ((8 * D_MODEL // 3 + 255) // 256) * 256
EPS = 1e-6

kernel_name = "hybrid_rglru_multiscale_pool_swiglu"


def rmsnorm(x, g):
    xf = x.astype(jnp.float32)
    return xf * lax.rsqrt(jnp.mean(xf * xf, axis=-1, keepdims=True) + EPS) * g.astype(jnp.float32)


def causal_depthwise_conv(x, w, b):
    S = x.shape[1]
    xp = jnp.pad(x, ((0, 0), (CONV_WIDTH - 1, 0), (0, 0)))
    out = b
    for k in range(CONV_WIDTH):
        out = out + xp[:, k:k + S, :] * w[k]
    return out


def rg_lru(x, wa, ba, wx, bx, lam):
    B, S, _ = x.shape
    xh = x.reshape(B, S, LRU_HEADS, LRU_HEAD_DIM)
    r = jax.nn.sigmoid(jnp.einsum('bshi,hij->bshj', xh, wa).reshape(B, S, LRU_WIDTH) + ba)
    i = jax.nn.sigmoid(jnp.einsum('bshi,hij->bshj', xh, wx).reshape(B, S, LRU_WIDTH) + bx)
    log_a = -LRU_C * r * jax.nn.softplus(-lam)
    a = jnp.exp(log_a)
    mult = jnp.sqrt(jnp.maximum(-jnp.expm1(2.0 * log_a), 1e-12))
    b = mult * (i * x)

    def combine(left, right):
        a1, b1 = left
        a2, b2 = right
        return a1 * a2, a2 * b1 + b2

    _, h = lax.associative_scan(combine, (a, b), axis=1)
    return h


def trailing_mean(x, w):
    S = x.shape[1]
    cs = jnp.cumsum(jnp.pad(x, ((0, 0), (w, 0), (0, 0))), axis=1)
    window_sum = cs[:, w:, :] - cs[:, :S, :]
    count = jnp.minimum(jnp.arange(1, S + 1), w).astype(jnp.float32)[None, :, None]
    return window_sum / count


def pool_mixer(u, pool_w, pool_b, pool_scale):
    B, S, _ = u.shape
    ug = u.reshape(B, S, POOL_GROUPS, POOL_GROUP_DIM)
    pooled = jnp.stack(
        [trailing_mean(ug[:, :, g, :], w) - ug[:, :, g, :] for g, w in enumerate(POOL_WINDOWS)],
        axis=2)
    y = jnp.einsum('bsgc,gcd->bsgd', pooled, pool_w).reshape(B, S, POOL_WIDTH) + pool_b
    return y * pool_scale


def _fwd_setup_inputs(seed: int = 0) -> dict:
    key = jax.random.key(seed)
    ks = jax.random.split(key, 24)
    f32 = jnp.float32
    nrm = lambda k, shape, fan_in: jax.random.normal(k, shape, f32) * (fan_in ** -0.5)
    gain = lambda k, shape: 1.0 + 0.02 * jax.random.normal(k, shape, f32)
    small = lambda k, shape: 0.01 * jax.random.normal(k, shape, f32)
    u = jax.random.uniform(ks[8], (DEPTH, LRU_WIDTH), f32, minval=0.9, maxval=0.999)
    a0 = u ** (1.0 / LRU_C)
    lam = jnp.log(a0) - jnp.log1p(-a0)
    return {
        "x": jax.random.normal(ks[0], (BATCH, SEQ, D_MODEL), f32),
        "norm_mix_g": gain(ks[1], (DEPTH, D_MODEL)),
        "w_in": nrm(ks[2], (DEPTH, D_MODEL, D_IN), D_MODEL),
        "conv_w": nrm(ks[3], (DEPTH, CONV_WIDTH, LRU_WIDTH), CONV_WIDTH),
        "conv_b": small(ks[4], (DEPTH, LRU_WIDTH)),
        "gate_a_w": nrm(ks[5], (DEPTH, LRU_HEADS, LRU_HEAD_DIM, LRU_HEAD_DIM), LRU_HEAD_DIM),
        "gate_a_b": small(ks[6], (DEPTH, LRU_WIDTH)),
        "gate_x_w": nrm(ks[7], (DEPTH, LRU_HEADS, LRU_HEAD_DIM, LRU_HEAD_DIM), LRU_HEAD_DIM),
        "gate_x_b": small(ks[9], (DEPTH, LRU_WIDTH)),
        "lru_lambda": lam,
        "pool_w": nrm(ks[10], (DEPTH, POOL_GROUPS, POOL_GROUP_DIM, POOL_GROUP_DIM), POOL_GROUP_DIM),
        "pool_b": small(ks[11], (DEPTH, POOL_WIDTH)),
        "pool_scale": gain(ks[12], (DEPTH, POOL_WIDTH)),
        "norm_lru_g": gain(ks[13], (DEPTH, LRU_WIDTH)),
        "norm_pool_g": gain(ks[14], (DEPTH, POOL_WIDTH)),
        "w_out": nrm(ks[15], (DEPTH, D_MIX, D_MODEL), D_MIX),
        "norm_ffn_g": gain(ks[16], (DEPTH, D_MODEL)),
        "ffn_w1": nrm(ks[17], (DEPTH, D_MODEL, D_FF), D_MODEL),
        "ffn_w3": nrm(ks[18], (DEPTH, D_MODEL, D_FF), D_MODEL),
        "ffn_w2": nrm(ks[19], (DEPTH, D_FF, D_MODEL), D_FF),
        "final_norm_g": gain(ks[20], (D_MODEL,)),
    }


def _fwd_reference(x, norm_mix_g, w_in, conv_w, conv_b, gate_a_w, gate_a_b, gate_x_w, gate_x_b,
              lru_lambda, pool_w, pool_b, pool_scale, norm_lru_g, norm_pool_g, w_out,
              norm_ffn_g, ffn_w1, ffn_w3, ffn_w2, final_norm_g):
    out_dtype = x.dtype
    h_res = x.astype(jnp.float32)
    for l in range(DEPTH):
        h = rmsnorm(h_res, norm_mix_g[l])
        u = h @ w_in[l].astype(jnp.float32)
        u_lru = u[..., :LRU_WIDTH]
        u_gate = u[..., LRU_WIDTH:2 * LRU_WIDTH]
        u_pool = u[..., 2 * LRU_WIDTH:]
        xc = causal_depthwise_conv(u_lru, conv_w[l], conv_b[l])
        y_lru = rg_lru(xc, gate_a_w[l], gate_a_b[l], gate_x_w[l], gate_x_b[l], lru_lambda[l])
        y_lru = y_lru * jax.nn.gelu(u_gate, approximate=True)
        y_pool = pool_mixer(u_pool, pool_w[l], pool_b[l], pool_scale[l])
        y = jnp.concatenate([rmsnorm(y_lru, norm_lru_g[l]), rmsnorm(y_pool, norm_pool_g[l])], axis=-1)
        h_res = h_res + y @ w_out[l].astype(jnp.float32)
        h = rmsnorm(h_res, norm_ffn_g[l])
        ff = jax.nn.silu(h @ ffn_w1[l].astype(jnp.float32)) * (h @ ffn_w3[l].astype(jnp.float32))
        h_res = h_res + ff @ ffn_w2[l].astype(jnp.float32)
    return rmsnorm(h_res, final_norm_g).astype(out_dtype)


import jax as _jax
import jax.numpy as _jnp

TWIN_FORMAT = 'train_step'
FWD_PARAMS = ['x', 'norm_mix_g', 'w_in', 'conv_w', 'conv_b', 'gate_a_w', 'gate_a_b', 'gate_x_w', 'gate_x_b', 'lru_lambda', 'pool_w', 'pool_b', 'pool_scale', 'norm_lru_g', 'norm_pool_g', 'w_out', 'norm_ffn_g', 'ffn_w1', 'ffn_w3', 'ffn_w2', 'final_norm_g']
TWIN_WEIGHTS = ['norm_mix_g', 'w_in', 'conv_w', 'conv_b', 'gate_a_w', 'gate_a_b', 'gate_x_w', 'gate_x_b', 'lru_lambda', 'pool_w', 'pool_b', 'pool_scale', 'norm_lru_g', 'norm_pool_g', 'w_out', 'norm_ffn_g', 'ffn_w1', 'ffn_w3', 'ffn_w2', 'final_norm_g']
TWIN_DIFF_INPUT = 'x'
TWIN_INPUTS = ['x', 'norm_mix_g', 'w_in', 'conv_w', 'conv_b', 'gate_a_w', 'gate_a_b', 'gate_x_w', 'gate_x_b', 'lru_lambda', 'pool_w', 'pool_b', 'pool_scale', 'norm_lru_g', 'norm_pool_g', 'w_out', 'norm_ffn_g', 'ffn_w1', 'ffn_w3', 'ffn_w2', 'final_norm_g', 'loss_target', 'm_norm_mix_g', 'm_w_in', 'm_conv_w', 'm_conv_b', 'm_gate_a_w', 'm_gate_a_b', 'm_gate_x_w', 'm_gate_x_b', 'm_lru_lambda', 'm_pool_w', 'm_pool_b', 'm_pool_scale', 'm_norm_lru_g', 'm_norm_pool_g', 'm_w_out', 'm_norm_ffn_g', 'm_ffn_w1', 'm_ffn_w3', 'm_ffn_w2', 'm_final_norm_g', 'v_norm_mix_g', 'v_w_in', 'v_conv_w', 'v_conv_b', 'v_gate_a_w', 'v_gate_a_b', 'v_gate_x_w', 'v_gate_x_b', 'v_lru_lambda', 'v_pool_w', 'v_pool_b', 'v_pool_scale', 'v_norm_lru_g', 'v_norm_pool_g', 'v_w_out', 'v_norm_ffn_g', 'v_ffn_w1', 'v_ffn_w3', 'v_ffn_w2', 'v_final_norm_g']
TWIN_OUTPUTS = ['loss', 'grad_x', 'grad_norm_mix_g', 'grad_w_in', 'grad_conv_w', 'grad_conv_b', 'grad_gate_a_w', 'grad_gate_a_b', 'grad_gate_x_w', 'grad_gate_x_b', 'grad_lru_lambda', 'grad_pool_w', 'grad_pool_b', 'grad_pool_scale', 'grad_norm_lru_g', 'grad_norm_pool_g', 'grad_w_out', 'grad_norm_ffn_g', 'grad_ffn_w1', 'grad_ffn_w3', 'grad_ffn_w2', 'grad_final_norm_g', 'delta_norm_mix_g', 'delta_w_in', 'delta_conv_w', 'delta_conv_b', 'delta_gate_a_w', 'delta_gate_a_b', 'delta_gate_x_w', 'delta_gate_x_b', 'delta_lru_lambda', 'delta_pool_w', 'delta_pool_b', 'delta_pool_scale', 'delta_norm_lru_g', 'delta_norm_pool_g', 'delta_w_out', 'delta_norm_ffn_g', 'delta_ffn_w1', 'delta_ffn_w3', 'delta_ffn_w2', 'delta_final_norm_g', 'new_m_norm_mix_g', 'new_m_w_in', 'new_m_conv_w', 'new_m_conv_b', 'new_m_gate_a_w', 'new_m_gate_a_b', 'new_m_gate_x_w', 'new_m_gate_x_b', 'new_m_lru_lambda', 'new_m_pool_w', 'new_m_pool_b', 'new_m_pool_scale', 'new_m_norm_lru_g', 'new_m_norm_pool_g', 'new_m_w_out', 'new_m_norm_ffn_g', 'new_m_ffn_w1', 'new_m_ffn_w3', 'new_m_ffn_w2', 'new_m_final_norm_g', 'new_v_norm_mix_g', 'new_v_w_in', 'new_v_conv_w', 'new_v_conv_b', 'new_v_gate_a_w', 'new_v_gate_a_b', 'new_v_gate_x_w', 'new_v_gate_x_b', 'new_v_lru_lambda', 'new_v_pool_w', 'new_v_pool_b', 'new_v_pool_scale', 'new_v_norm_lru_g', 'new_v_norm_pool_g', 'new_v_w_out', 'new_v_norm_ffn_g', 'new_v_ffn_w1', 'new_v_ffn_w3', 'new_v_ffn_w2', 'new_v_final_norm_g']
TWIN_LEAF_KINDS = {'loss': 'loss', 'grad_x': 'grad_x', 'grad_norm_mix_g': 'grad_w', 'grad_w_in': 'grad_w', 'grad_conv_w': 'grad_w', 'grad_conv_b': 'grad_w', 'grad_gate_a_w': 'grad_w', 'grad_gate_a_b': 'grad_w', 'grad_gate_x_w': 'grad_w', 'grad_gate_x_b': 'grad_w', 'grad_lru_lambda': 'grad_w', 'grad_pool_w': 'grad_w', 'grad_pool_b': 'grad_w', 'grad_pool_scale': 'grad_w', 'grad_norm_lru_g': 'grad_w', 'grad_norm_pool_g': 'grad_w', 'grad_w_out': 'grad_w', 'grad_norm_ffn_g': 'grad_w', 'grad_ffn_w1': 'grad_w', 'grad_ffn_w3': 'grad_w', 'grad_ffn_w2': 'grad_w', 'grad_final_norm_g': 'grad_w', 'delta_norm_mix_g': 'delta_w', 'delta_w_in': 'delta_w', 'delta_conv_w': 'delta_w', 'delta_conv_b': 'delta_w', 'delta_gate_a_w': 'delta_w', 'delta_gate_a_b': 'delta_w', 'delta_gate_x_w': 'delta_w', 'delta_gate_x_b': 'delta_w', 'delta_lru_lambda': 'delta_w', 'delta_pool_w': 'delta_w', 'delta_pool_b': 'delta_w', 'delta_pool_scale': 'delta_w', 'delta_norm_lru_g': 'delta_w', 'delta_norm_pool_g': 'delta_w', 'delta_w_out': 'delta_w', 'delta_norm_ffn_g': 'delta_w', 'delta_ffn_w1': 'delta_w', 'delta_ffn_w3': 'delta_w', 'delta_ffn_w2': 'delta_w', 'delta_final_norm_g': 'delta_w', 'new_m_norm_mix_g': 'new_m', 'new_m_w_in': 'new_m', 'new_m_conv_w': 'new_m', 'new_m_conv_b': 'new_m', 'new_m_gate_a_w': 'new_m', 'new_m_gate_a_b': 'new_m', 'new_m_gate_x_w': 'new_m', 'new_m_gate_x_b': 'new_m', 'new_m_lru_lambda': 'new_m', 'new_m_pool_w': 'new_m', 'new_m_pool_b': 'new_m', 'new_m_pool_scale': 'new_m', 'new_m_norm_lru_g': 'new_m', 'new_m_norm_pool_g': 'new_m', 'new_m_w_out': 'new_m', 'new_m_norm_ffn_g': 'new_m', 'new_m_ffn_w1': 'new_m', 'new_m_ffn_w3': 'new_m', 'new_m_ffn_w2': 'new_m', 'new_m_final_norm_g': 'new_m', 'new_v_norm_mix_g': 'new_v', 'new_v_w_in': 'new_v', 'new_v_conv_w': 'new_v', 'new_v_conv_b': 'new_v', 'new_v_gate_a_w': 'new_v', 'new_v_gate_a_b': 'new_v', 'new_v_gate_x_w': 'new_v', 'new_v_gate_x_b': 'new_v', 'new_v_lru_lambda': 'new_v', 'new_v_pool_w': 'new_v', 'new_v_pool_b': 'new_v', 'new_v_pool_scale': 'new_v', 'new_v_norm_lru_g': 'new_v', 'new_v_norm_pool_g': 'new_v', 'new_v_w_out': 'new_v', 'new_v_norm_ffn_g': 'new_v', 'new_v_ffn_w1': 'new_v', 'new_v_ffn_w3': 'new_v', 'new_v_ffn_w2': 'new_v', 'new_v_final_norm_g': 'new_v'}


def _forward(args):
    return _fwd_reference(*[args[k] for k in FWD_PARAMS])


def _output_shape():
    def fwd():
        inp = _fwd_setup_inputs(0)
        return _fwd_reference(*[inp[k] for k in FWD_PARAMS])
    out = _jax.eval_shape(fwd)
    return out.shape, out.dtype

N_MICROBATCH = 1
ADAM_LR = 0.001
ADAM_B1 = 0.9
ADAM_B2 = 0.999
ADAM_EPS = 1e-08
ADAM_WD = 0.01
ADAM_STEP = 10
PER_EXAMPLE_BATCH_AXIS = {'x': 0, 'loss_target': 0}
SHARED_INPUTS = []
_WEIGHT_DTYPES = {'norm_mix_g': _jnp.float32, 'w_in': _jnp.float32, 'conv_w': _jnp.float32, 'conv_b': _jnp.float32, 'gate_a_w': _jnp.float32, 'gate_a_b': _jnp.float32, 'gate_x_w': _jnp.float32, 'gate_x_b': _jnp.float32, 'lru_lambda': _jnp.float32, 'pool_w': _jnp.float32, 'pool_b': _jnp.float32, 'pool_scale': _jnp.float32, 'norm_lru_g': _jnp.float32, 'norm_pool_g': _jnp.float32, 'w_out': _jnp.float32, 'norm_ffn_g': _jnp.float32, 'ffn_w1': _jnp.float32, 'ffn_w3': _jnp.float32, 'ffn_w2': _jnp.float32, 'final_norm_g': _jnp.float32}
MOMENT_SCALE = {'norm_mix_g': 2.318801e-01, 'w_in': 1.949170e-01, 'conv_w': 2.102766e-01, 'conv_b': 2.061732e+00, 'gate_a_w': 6.764250e-02, 'gate_a_b': 5.653477e-02, 'gate_x_w': 1.226029e-01, 'gate_x_b': 6.713023e-02, 'lru_lambda': 9.818805e-02, 'pool_w': 1.928289e-01, 'pool_b': 1.159232e+00, 'pool_scale': 1.909367e-01, 'norm_lru_g': 1.894650e-01, 'norm_pool_g': 1.915818e-01, 'w_out': 1.956449e-01, 'norm_ffn_g': 1.415238e-01, 'ffn_w1': 6.085022e-02, 'ffn_w3': 5.912117e-02, 'ffn_w2': 9.788785e-02, 'final_norm_g': 6.406648e+01}


def _to_microbatches(a, axis):
    t = _jnp.moveaxis(a, axis, 0)
    t = t.reshape((N_MICROBATCH, t.shape[0] // N_MICROBATCH) + t.shape[1:])
    return _jnp.moveaxis(t, 1, axis + 1)


def setup_inputs(seed: int = 0) -> dict:
    inp = _fwd_setup_inputs(seed)
    key = _jax.random.fold_in(_jax.random.key(seed), 7919)
    shape, _ = _output_shape()
    out = dict(inp)
    out["loss_target"] = _jax.random.normal(_jax.random.fold_in(key, 0), shape, _jnp.float32)
    for i, name in enumerate(TWIN_WEIGHTS):
        w = inp[name].astype(_jnp.float32)
        if MOMENT_SCALE is None:
            s = _jnp.sqrt(_jnp.mean(_jnp.square(w)) + 1e-30)
        else:
            s = MOMENT_SCALE[name]
        km, kv = _jax.random.split(_jax.random.fold_in(key, i + 1))
        out[name] = w
        out["m_" + name] = s * _jax.random.normal(km, w.shape, _jnp.float32)
        out["v_" + name] = (s * s) * _jax.random.uniform(kv, w.shape, _jnp.float32, 0.5, 1.5)
    if N_MICROBATCH > 1:
        for name, axis in PER_EXAMPLE_BATCH_AXIS.items():
            out[name] = _to_microbatches(out[name], axis)
    return {'x': out['x'], 'norm_mix_g': out['norm_mix_g'], 'w_in': out['w_in'], 'conv_w': out['conv_w'], 'conv_b': out['conv_b'], 'gate_a_w': out['gate_a_w'], 'gate_a_b': out['gate_a_b'], 'gate_x_w': out['gate_x_w'], 'gate_x_b': out['gate_x_b'], 'lru_lambda': out['lru_lambda'], 'pool_w': out['pool_w'], 'pool_b': out['pool_b'], 'pool_scale': out['pool_scale'], 'norm_lru_g': out['norm_lru_g'], 'norm_pool_g': out['norm_pool_g'], 'w_out': out['w_out'], 'norm_ffn_g': out['norm_ffn_g'], 'ffn_w1': out['ffn_w1'], 'ffn_w3': out['ffn_w3'], 'ffn_w2': out['ffn_w2'], 'final_norm_g': out['final_norm_g'], 'loss_target': out['loss_target'], 'm_norm_mix_g': out['m_norm_mix_g'], 'm_w_in': out['m_w_in'], 'm_conv_w': out['m_conv_w'], 'm_conv_b': out['m_conv_b'], 'm_gate_a_w': out['m_gate_a_w'], 'm_gate_a_b': out['m_gate_a_b'], 'm_gate_x_w': out['m_gate_x_w'], 'm_gate_x_b': out['m_gate_x_b'], 'm_lru_lambda': out['m_lru_lambda'], 'm_pool_w': out['m_pool_w'], 'm_pool_b': out['m_pool_b'], 'm_pool_scale': out['m_pool_scale'], 'm_norm_lru_g': out['m_norm_lru_g'], 'm_norm_pool_g': out['m_norm_pool_g'], 'm_w_out': out['m_w_out'], 'm_norm_ffn_g': out['m_norm_ffn_g'], 'm_ffn_w1': out['m_ffn_w1'], 'm_ffn_w3': out['m_ffn_w3'], 'm_ffn_w2': out['m_ffn_w2'], 'm_final_norm_g': out['m_final_norm_g'], 'v_norm_mix_g': out['v_norm_mix_g'], 'v_w_in': out['v_w_in'], 'v_conv_w': out['v_conv_w'], 'v_conv_b': out['v_conv_b'], 'v_gate_a_w': out['v_gate_a_w'], 'v_gate_a_b': out['v_gate_a_b'], 'v_gate_x_w': out['v_gate_x_w'], 'v_gate_x_b': out['v_gate_x_b'], 'v_lru_lambda': out['v_lru_lambda'], 'v_pool_w': out['v_pool_w'], 'v_pool_b': out['v_pool_b'], 'v_pool_scale': out['v_pool_scale'], 'v_norm_lru_g': out['v_norm_lru_g'], 'v_norm_pool_g': out['v_norm_pool_g'], 'v_w_out': out['v_w_out'], 'v_norm_ffn_g': out['v_norm_ffn_g'], 'v_ffn_w1': out['v_ffn_w1'], 'v_ffn_w3': out['v_ffn_w3'], 'v_ffn_w2': out['v_ffn_w2'], 'v_final_norm_g': out['v_final_norm_g']}


def _loss(weights, diff, rest, loss_target):
    with _jax.named_scope("forward"):
        args = {**rest, TWIN_DIFF_INPUT: diff, **{k: w.astype(_WEIGHT_DTYPES[k]) for k, w in weights.items()}}
        y = _forward(args)
    with _jax.named_scope("loss_head"):
        err = _jnp.square(y.astype(_jnp.float32) - loss_target)
        return 0.5 * _jnp.sum(_jnp.mean(err, axis=-1)) if err.ndim else 0.5 * err


def _adamw(w, g, m, v):
    m = ADAM_B1 * m + (1.0 - ADAM_B1) * g
    v = ADAM_B2 * v + (1.0 - ADAM_B2) * _jnp.square(g)
    m_hat = m / (1.0 - ADAM_B1 ** ADAM_STEP)
    v_hat = v / (1.0 - ADAM_B2 ** ADAM_STEP)
    delta = -ADAM_LR * (m_hat / (_jnp.sqrt(v_hat) + ADAM_EPS) + ADAM_WD * w)
    return delta, m, v


def reference(x, norm_mix_g, w_in, conv_w, conv_b, gate_a_w, gate_a_b, gate_x_w, gate_x_b, lru_lambda, pool_w, pool_b, pool_scale, norm_lru_g, norm_pool_g, w_out, norm_ffn_g, ffn_w1, ffn_w3, ffn_w2, final_norm_g, loss_target, m_norm_mix_g, m_w_in, m_conv_w, m_conv_b, m_gate_a_w, m_gate_a_b, m_gate_x_w, m_gate_x_b, m_lru_lambda, m_pool_w, m_pool_b, m_pool_scale, m_norm_lru_g, m_norm_pool_g, m_w_out, m_norm_ffn_g, m_ffn_w1, m_ffn_w3, m_ffn_w2, m_final_norm_g, v_norm_mix_g, v_w_in, v_conv_w, v_conv_b, v_gate_a_w, v_gate_a_b, v_gate_x_w, v_gate_x_b, v_lru_lambda, v_pool_w, v_pool_b, v_pool_scale, v_norm_lru_g, v_norm_pool_g, v_w_out, v_norm_ffn_g, v_ffn_w1, v_ffn_w3, v_ffn_w2, v_final_norm_g):
    given = dict(x=x, norm_mix_g=norm_mix_g, w_in=w_in, conv_w=conv_w, conv_b=conv_b, gate_a_w=gate_a_w, gate_a_b=gate_a_b, gate_x_w=gate_x_w, gate_x_b=gate_x_b, lru_lambda=lru_lambda, pool_w=pool_w, pool_b=pool_b, pool_scale=pool_scale, norm_lru_g=norm_lru_g, norm_pool_g=norm_pool_g, w_out=w_out, norm_ffn_g=norm_ffn_g, ffn_w1=ffn_w1, ffn_w3=ffn_w3, ffn_w2=ffn_w2, final_norm_g=final_norm_g, loss_target=loss_target, m_norm_mix_g=m_norm_mix_g, m_w_in=m_w_in, m_conv_w=m_conv_w, m_conv_b=m_conv_b, m_gate_a_w=m_gate_a_w, m_gate_a_b=m_gate_a_b, m_gate_x_w=m_gate_x_w, m_gate_x_b=m_gate_x_b, m_lru_lambda=m_lru_lambda, m_pool_w=m_pool_w, m_pool_b=m_pool_b, m_pool_scale=m_pool_scale, m_norm_lru_g=m_norm_lru_g, m_norm_pool_g=m_norm_pool_g, m_w_out=m_w_out, m_norm_ffn_g=m_norm_ffn_g, m_ffn_w1=m_ffn_w1, m_ffn_w3=m_ffn_w3, m_ffn_w2=m_ffn_w2, m_final_norm_g=m_final_norm_g, v_norm_mix_g=v_norm_mix_g, v_w_in=v_w_in, v_conv_w=v_conv_w, v_conv_b=v_conv_b, v_gate_a_w=v_gate_a_w, v_gate_a_b=v_gate_a_b, v_gate_x_w=v_gate_x_w, v_gate_x_b=v_gate_x_b, v_lru_lambda=v_lru_lambda, v_pool_w=v_pool_w, v_pool_b=v_pool_b, v_pool_scale=v_pool_scale, v_norm_lru_g=v_norm_lru_g, v_norm_pool_g=v_norm_pool_g, v_w_out=v_w_out, v_norm_ffn_g=v_norm_ffn_g, v_ffn_w1=v_ffn_w1, v_ffn_w3=v_ffn_w3, v_ffn_w2=v_ffn_w2, v_final_norm_g=v_final_norm_g)
    weights = {n: given[n] for n in TWIN_WEIGHTS}
    shared = {n: given[n] for n in SHARED_INPUTS}
    per_example = {n: given[n] for n in ['x']}
    grad_fn = _jax.value_and_grad(_loss, argnums=(0, 1))

    def one_microbatch(ex, loss_target):
        ex = dict(ex)
        diff = ex.pop(TWIN_DIFF_INPUT)
        return grad_fn(weights, diff, {**shared, **ex}, loss_target)

    if N_MICROBATCH == 1:
        loss, (grad_w, grad_x) = one_microbatch(per_example, given["loss_target"])
    else:
        def body(carry, xs):
            loss_sum, grad_sum = carry
            l_k, (gw_k, gx_k) = one_microbatch(xs[0], xs[1])
            with _jax.named_scope("update"):
                return (loss_sum + l_k, _jax.tree.map(_jnp.add, grad_sum, gw_k)), gx_k

        init = (_jnp.zeros((), _jnp.float32), _jax.tree.map(_jnp.zeros_like, weights))
        (loss, grad_w), grad_x = _jax.lax.scan(body, init, (per_example, given["loss_target"]))
    with _jax.named_scope("update"):
        delta_w, new_m, new_v = {}, {}, {}
        for n in TWIN_WEIGHTS:
            delta_w[n], new_m[n], new_v[n] = _adamw(weights[n], grad_w[n], given["m_" + n], given["v_" + n])
    return (loss, grad_x, *[grad_w[n] for n in TWIN_WEIGHTS], *[delta_w[n] for n in TWIN_WEIGHTS],
            *[new_m[n] for n in TWIN_WEIGHTS], *[new_v[n] for n in TWIN_WEIGHTS])
```

```python
import functools

import jax
import jax.numpy as jnp
from jax import lax
from jax.experimental import pallas as pl
from jax.experimental.pallas import tpu as pltpu

F32 = jnp.float32
BF16 = jnp.bfloat16

D_MODEL = 1024
LRU_W = 512
POOL_W = 512
D_IN = 1536
D_FF = 2816
POOL_WINDOWS = (2, 4, 8, 16)
EPS = 1e-6
LRU_C = 8.0
N_DEV = 8
HALO = 16

ADAM_LR = 0.001
ADAM_B1 = 0.9
ADAM_B2 = 0.999
ADAM_EPS = 1e-08
ADAM_WD = 0.01
ADAM_STEP = 10

ROW_CW, ROW_CB, ROW_BA, ROW_BX, ROW_LAM, ROW_PB, ROW_PS, ROW_GL, ROW_GP = 0, 4, 5, 6, 7, 8, 9, 10, 11
SMALL_ROWS = 280

NT = (((1,), (1,)), ((), ()))
TN = (((0,), (0,)), ((), ()))


def _sds(shape, dtype):
    return jax.ShapeDtypeStruct(shape, dtype)


def _sigmoid(x):
    return 1.0 / (1.0 + jnp.exp(-x))


def _gelu_parts(x):
    c = 0.7978845608028654
    inner = c * (x + 0.044715 * (x * x * x))
    th = jnp.tanh(inner)
    g = 0.5 * x * (1.0 + th)
    dg = 0.5 * (1.0 + th) + 0.5 * x * (1.0 - th * th) * (c * (1.0 + 3.0 * 0.044715 * (x * x)))
    return g, dg


def _one_minus_exp(x):
    series = -x * (1.0 + x * (0.5 + x * (1.0 / 6.0 + x * (1.0 / 24.0 + x * (1.0 / 120.0)))))
    return jnp.where(x > -0.1, series, 1.0 - jnp.exp(x))


def _rstd(x):
    return lax.rsqrt(jnp.mean(x * x, axis=-1, keepdims=True) + EPS)


def _rms_bwd(dy, xhat, rstd, gain):
    dxh = dy * gain
    dx = rstd * (dxh - xhat * jnp.mean(dxh * xhat, axis=-1, keepdims=True))
    return dx, jnp.sum(dy * xhat, axis=0, keepdims=True)


def _bd(xb, w_ref):
    return jnp.concatenate(
        [jnp.dot(xb[:, :256], w_ref[0], preferred_element_type=F32),
         jnp.dot(xb[:, 256:], w_ref[1], preferred_element_type=F32)], axis=1)


def _bd_t(xb, w_ref):
    return jnp.concatenate(
        [lax.dot_general(xb[:, :256], w_ref[0], NT, preferred_element_type=F32),
         lax.dot_general(xb[:, 256:], w_ref[1], NT, preferred_element_type=F32)], axis=1)


def _bd_grad(xb, db):
    return jnp.stack(
        [lax.dot_general(xb[:, :256], db[:, :256], TN, preferred_element_type=F32),
         lax.dot_general(xb[:, 256:], db[:, 256:], TN, preferred_element_type=F32)], axis=0)


def _mixer_pre(e_lru, e_pool, pv, wa_ref, wx_ref, wp_ref, tm, t0):
    xc = pv[ROW_CB:ROW_CB + 1, :]
    for k in range(4):
        xc = xc + e_lru[pl.ds(HALO - 3 + k, tm), :] * pv[ROW_CW + k:ROW_CW + k + 1, :]
    xcb = xc.astype(BF16)
    r = _sigmoid(_bd(xcb, wa_ref) + pv[ROW_BA:ROW_BA + 1, :])
    ig = _sigmoid(_bd(xcb, wx_ref) + pv[ROW_BX:ROW_BX + 1, :])
    z = -pv[ROW_LAM:ROW_LAM + 1, :]
    sp = jnp.maximum(z, 0.0) + jnp.log(1.0 + jnp.exp(-jnp.abs(z)))
    la = (-LRU_C * r) * sp
    a = jnp.exp(la)
    om = _one_minus_exp(2.0 * la)
    mult = jnp.sqrt(jnp.maximum(om, 1e-12))
    t = t0 + lax.broadcasted_iota(jnp.int32, (tm, 1), 0)
    parts, cnts = [], []
    for g, w in enumerate(POOL_WINDOWS):
        lanes = pl.ds(128 * g, 128)
        cur = e_pool[pl.ds(HALO, tm), lanes]
        s = cur
        for k in range(1, w):
            s = s + e_pool[pl.ds(HALO - k, tm), lanes]
        cnt = jnp.minimum(t + 1, w).astype(F32)
        cnts.append(cnt)
        parts.append(s / cnt - cur)
    pooled = jnp.concatenate(parts, axis=1)
    pooled_b = pooled.astype(BF16)
    zp = _bd(pooled_b, wp_ref) + pv[ROW_PB:ROW_PB + 1, :]
    return dict(xc=xc, xcb=xcb, r=r, ig=ig, sp=sp, a=a, om=om, mult=mult, pooled_b=pooled_b, zp=zp, cnts=cnts)


def _scan_tile(a_ref, b_ref, out_ref, carry, tm, reverse):
    row = lax.broadcasted_iota(jnp.int32, (8, LRU_W), 0)
    nblk = tm // 8

    def step(i, hin):
        blk = (nblk - 1 - i) if reverse else i
        r0 = pl.multiple_of(blk * 8, 8)
        av = a_ref[pl.ds(r0, 8), :]
        bv = b_ref[pl.ds(r0, 8), :]
        for d in (1, 2, 4):
            sh = (8 - d) if reverse else d
            a_s = pltpu.roll(av, sh, 0)
            b_s = pltpu.roll(bv, sh, 0)
            m = (row < 8 - d) if reverse else (row >= d)
            bv = jnp.where(m, av * b_s + bv, bv)
            av = jnp.where(m, av * a_s, av)
        hv = av * hin + bv
        out_ref[pl.ds(r0, 8), :] = hv
        edge = hv[0:1, :] if reverse else hv[7:8, :]
        return jnp.broadcast_to(edge, (8, LRU_W))

    return lax.fori_loop(0, nblk, step, carry)


def _mix_in(x, g_mix, w_in_b, tm):
    T = x.shape[0]

    def body(x_ref, g_ref, w_ref, u_ref, h_ref):
        xv = x_ref[...]
        h = (xv * _rstd(xv) * g_ref[...]).astype(BF16)
        h_ref[...] = h
        u_ref[...] = jnp.dot(h, w_ref[...], preferred_element_type=F32)

    return pl.pallas_call(
        body, name="mix_in", grid=(T // tm,),
        in_specs=[pl.BlockSpec((tm, D_MODEL), lambda i: (i, 0)),
                  pl.BlockSpec((1, D_MODEL), lambda i: (0, 0)),
                  pl.BlockSpec((D_MODEL, D_IN), lambda i: (0, 0))],
        out_specs=[pl.BlockSpec((tm, D_IN), lambda i: (i, 0)),
                   pl.BlockSpec((tm, D_MODEL), lambda i: (i, 0))],
        out_shape=[_sds((T, D_IN), F32), _sds((T, D_MODEL), BF16)],
        compiler_params=pltpu.CompilerParams(dimension_semantics=("parallel",)),
    )(x, g_mix, w_in_b)


def _mixer_fwd(u, pv, wa, wx, wp, tm):
    T = u.shape[0]

    def body(u_ref, pv_ref, wa_ref, wx_ref, wp_ref, y_ref, hs_ref, e_lru, e_pool, a_s, b_s, hc):
        i = pl.program_id(0)

        @pl.when(i == 0)
        def _():
            e_lru[pl.ds(0, HALO), :] = jnp.zeros((HALO, LRU_W), F32)
            e_pool[pl.ds(0, HALO), :] = jnp.zeros((HALO, POOL_W), F32)
            hc[...] = jnp.zeros((8, LRU_W), F32)

        e_lru[pl.ds(HALO, tm), :] = u_ref[:, 0:LRU_W]
        e_pool[pl.ds(HALO, tm), :] = u_ref[:, 2 * LRU_W:D_IN]
        pv = pv_ref[...]
        p = _mixer_pre(e_lru, e_pool, pv, wa_ref, wx_ref, wp_ref, tm, i * tm)
        a_s[...] = p["a"]
        b_s[...] = p["mult"] * (p["ig"] * p["xc"])
        hc[...] = _scan_tile(a_s, b_s, hs_ref, hc[...], tm, reverse=False)
        gl, _ = _gelu_parts(u_ref[:, LRU_W:2 * LRU_W])
        y_lru = hs_ref[...] * gl
        y_ref[:, 0:LRU_W] = (y_lru * _rstd(y_lru) * pv[ROW_GL:ROW_GL + 1, :]).astype(BF16)
        y_pool = p["zp"] * pv[ROW_PS:ROW_PS + 1, :]
        y_ref[:, LRU_W:D_MODEL] = (y_pool * _rstd(y_pool) * pv[ROW_GP:ROW_GP + 1, :]).astype(BF16)
        e_lru[pl.ds(0, HALO), :] = e_lru[pl.ds(tm, HALO), :]
        e_pool[pl.ds(0, HALO), :] = e_pool[pl.ds(tm, HALO), :]

    full = lambda shape: pl.BlockSpec(shape, lambda i: (0,) * len(shape))
    return pl.pallas_call(
        body, name="mixer_fwd", grid=(T // tm,),
        in_specs=[pl.BlockSpec((tm, D_IN), lambda i: (i, 0)), full((16, LRU_W)),
                  full((2, 256, 256)), full((2, 256, 256)), full((2, 256, 256))],
        out_specs=[pl.BlockSpec((tm, D_MODEL), lambda i: (i, 0)), pl.BlockSpec((tm, LRU_W), lambda i: (i, 0))],
        out_shape=[_sds((T, D_MODEL), BF16), _sds((T, LRU_W), F32)],
        scratch_shapes=[pltpu.VMEM((HALO + tm, LRU_W), F32), pltpu.VMEM((HALO + tm, POOL_W), F32),
                        pltpu.VMEM((tm, LRU_W), F32), pltpu.VMEM((tm, LRU_W), F32), pltpu.VMEM((8, LRU_W), F32)],
        compiler_params=pltpu.CompilerParams(dimension_semantics=("arbitrary",)),
    )(u, pv, wa, wx, wp)


def _ffn_fwd(x, y, w_out_b, g_ffn, w1_b, w3_b, w2_b, g_fin, tgt, tm, tn):
    T = x.shape[0]
    n_j = D_FF // tn

    def body(x_ref, y_ref, wo_ref, gf_ref, w1_ref, w3_ref, w2_ref, gfin_ref, tgt_ref,
             hres_ref, h2_ref, g_ref, v_ref, d3_ref, loss_ref, dgfin_ref, acc):
        i, j = pl.program_id(0), pl.program_id(1)

        @pl.when(j == 0)
        def _():
            hr = x_ref[...] + jnp.dot(y_ref[...], wo_ref[...], preferred_element_type=F32)
            hres_ref[...] = hr
            h2_ref[...] = (hr * _rstd(hr) * gf_ref[...]).astype(BF16)
            acc[...] = jnp.zeros((tm, D_MODEL), F32)

        @pl.when((j == 0) & (i == 0))
        def _():
            loss_ref[...] = jnp.zeros((8, 128), F32)
            dgfin_ref[...] = jnp.zeros((1, D_MODEL), F32)

        h2 = h2_ref[...]
        g = jnp.dot(h2, w1_ref[...], preferred_element_type=F32)
        v = jnp.dot(h2, w3_ref[...], preferred_element_type=F32)
        g_ref[...] = g.astype(BF16)
        v_ref[...] = v.astype(BF16)
        ff = ((g * _sigmoid(g)) * v).astype(BF16)
        acc[...] += jnp.dot(ff, w2_ref[...], preferred_element_type=F32)

        @pl.when(j == n_j - 1)
        def _():
            h3 = hres_ref[...] + acc[...]
            rstd = _rstd(h3)
            xh = h3 * rstd
            gfin = gfin_ref[...]
            err = xh * gfin - tgt_ref[...]
            loss_ref[...] += 0.5 * jnp.sum(jnp.mean(err * err, axis=-1, keepdims=True))
            dout = err * (1.0 / D_MODEL)
            dx, dgain = _rms_bwd(dout, xh, rstd, gfin)
            d3_ref[...] = dx
            dgfin_ref[...] += dgain

    row = lambda w: pl.BlockSpec((tm, w), lambda i, j: (i, 0))
    const = lambda shape: pl.BlockSpec(shape, lambda i, j: (0,) * len(shape))
    return pl.pallas_call(
        body, name="ffn_fwd", grid=(T // tm, n_j),
        in_specs=[row(D_MODEL), row(D_MODEL), const((D_MODEL, D_MODEL)), const((1, D_MODEL)),
                  pl.BlockSpec((D_MODEL, tn), lambda i, j: (0, j)), pl.BlockSpec((D_MODEL, tn), lambda i, j: (0, j)),
                  pl.BlockSpec((tn, D_MODEL), lambda i, j: (j, 0)), const((1, D_MODEL)), row(D_MODEL)],
        out_specs=[row(D_MODEL), row(D_MODEL),
                   pl.BlockSpec((tm, tn), lambda i, j: (i, j)), pl.BlockSpec((tm, tn), lambda i, j: (i, j)),
                   row(D_MODEL), const((8, 128)), const((1, D_MODEL))],
        out_shape=[_sds((T, D_MODEL), F32), _sds((T, D_MODEL), BF16), _sds((T, D_FF), BF16), _sds((T, D_FF), BF16),
                   _sds((T, D_MODEL), F32), _sds((8, 128), F32), _sds((1, D_MODEL), F32)],
        scratch_shapes=[pltpu.VMEM((tm, D_MODEL), F32)],
        compiler_params=pltpu.CompilerParams(dimension_semantics=("arbitrary", "arbitrary")),
    )(x, y, w_out_b, g_ffn, w1_b, w3_b, w2_b, g_fin, tgt)


def _ffn_bwd(d3, g, v, w1_b, w3_b, w2_b, hres, g_ffn, tm, tn):
    T = d3.shape[0]
    n_j = D_FF // tn

    def body(d3_ref, g_ref, v_ref, w1_ref, w3_ref, w2_ref, hres_ref, gf_ref,
             dg_ref, dv_ref, ff_ref, d2_ref, dgffn_ref, acc):
        i, j = pl.program_id(0), pl.program_id(1)

        @pl.when(j == 0)
        def _():
            acc[...] = jnp.zeros((tm, D_MODEL), F32)

        @pl.when((j == 0) & (i == 0))
        def _():
            dgffn_ref[...] = jnp.zeros((1, D_MODEL), F32)

        dff = lax.dot_general(d3_ref[...].astype(BF16), w2_ref[...], NT, preferred_element_type=F32)
        gv = g_ref[...].astype(F32)
        vv = v_ref[...].astype(F32)
        sg = _sigmoid(gv)
        sl = gv * sg
        dgb = (dff * vv * (sg * (1.0 + gv * (1.0 - sg)))).astype(BF16)
        dvb = (dff * sl).astype(BF16)
        dg_ref[...] = dgb
        dv_ref[...] = dvb
        ff_ref[...] = (sl * vv).astype(BF16)
        acc[...] += (lax.dot_general(dgb, w1_ref[...], NT, preferred_element_type=F32)
                     + lax.dot_general(dvb, w3_ref[...], NT, preferred_element_type=F32))

        @pl.when(j == n_j - 1)
        def _():
            hr = hres_ref[...]
            rstd = _rstd(hr)
            dx, dgain = _rms_bwd(acc[...], hr * rstd, rstd, gf_ref[...])
            d2_ref[...] = d3_ref[...] + dx
            dgffn_ref[...] += dgain

    row = lambda w: pl.BlockSpec((tm, w), lambda i, j: (i, 0))
    tile = pl.BlockSpec((tm, tn), lambda i, j: (i, j))
    const = lambda shape: pl.BlockSpec(shape, lambda i, j: (0,) * len(shape))
    return pl.pallas_call(
        body, name="ffn_bwd", grid=(T // tm, n_j),
        in_specs=[row(D_MODEL), tile, tile,
                  pl.BlockSpec((D_MODEL, tn), lambda i, j: (0, j)), pl.BlockSpec((D_MODEL, tn), lambda i, j: (0, j)),
                  pl.BlockSpec((tn, D_MODEL), lambda i, j: (j, 0)), row(D_MODEL), const((1, D_MODEL))],
        out_specs=[tile, tile, tile, row(D_MODEL), const((1, D_MODEL))],
        out_shape=[_sds((T, D_FF), BF16), _sds((T, D_FF), BF16), _sds((T, D_FF), BF16),
                   _sds((T, D_MODEL), F32), _sds((1, D_MODEL), F32)],
        scratch_shapes=[pltpu.VMEM((tm, D_MODEL), F32)],
        compiler_params=pltpu.CompilerParams(dimension_semantics=("arbitrary", "arbitrary")),
    )(d3, g, v, w1_b, w3_b, w2_b, hres, g_ffn)


def _at_b(a, b, name, tmm, tn, tk):
    T, M = a.shape
    N = b.shape[1]

    def body(a_ref, b_ref, o_ref):
        @pl.when(pl.program_id(2) == 0)
        def _():
            o_ref[...] = jnp.zeros((tmm, tn), F32)

        o_ref[...] += lax.dot_general(a_ref[...].astype(BF16), b_ref[...].astype(BF16), TN,
                                      preferred_element_type=F32)

    return pl.pallas_call(
        body, name=name, grid=(M // tmm, N // tn, T // tk),
        in_specs=[pl.BlockSpec((tk, tmm), lambda m, n, k: (k, m)), pl.BlockSpec((tk, tn), lambda m, n, k: (k, n))],
        out_specs=pl.BlockSpec((tmm, tn), lambda m, n, k: (m, n)),
        out_shape=_sds((M, N), F32),
        compiler_params=pltpu.CompilerParams(dimension_semantics=("parallel", "parallel", "arbitrary")),
    )(a, b)


def _mixer_bwd(d2, u, hs, pv, wa, wx, wp, w_out_b, tm):
    T = u.shape[0]
    n_t = T // tm

    def body(d2_ref, u_ref, uh_ref, hs_ref, hh_ref, pv_ref, wa_ref, wx_ref, wp_ref, wo_ref,
             du_ref, vacc_ref, dwa_ref, dwx_ref, dwp_ref,
             e_lru, e_pool, e_h, a_s, b_s, mu_s, f_x, f_p, mc, cx, cp):
        s = pl.program_id(0)
        it = n_t - 1 - s

        @pl.when(s == 0)
        def _():
            mc[...] = jnp.zeros((8, LRU_W), F32)
            cx[...] = jnp.zeros((8, LRU_W), F32)
            cp[...] = jnp.zeros((HALO, POOL_W), F32)
            vacc_ref[...] = jnp.zeros((16, LRU_W), F32)
            dwa_ref[...] = jnp.zeros((2, 256, 256), F32)
            dwx_ref[...] = jnp.zeros((2, 256, 256), F32)
            dwp_ref[...] = jnp.zeros((2, 256, 256), F32)

        keep = (it > 0).astype(F32)
        e_lru[pl.ds(0, HALO), :] = uh_ref[:, 0:LRU_W] * keep
        e_pool[pl.ds(0, HALO), :] = uh_ref[:, 2 * LRU_W:D_IN] * keep
        e_lru[pl.ds(HALO, tm), :] = u_ref[:, 0:LRU_W]
        e_pool[pl.ds(HALO, tm), :] = u_ref[:, 2 * LRU_W:D_IN]
        e_h[pl.ds(0, 8), :] = hh_ref[...] * keep
        e_h[pl.ds(8, tm), :] = hs_ref[...]
        pv = pv_ref[...]
        p = _mixer_pre(e_lru, e_pool, pv, wa_ref, wx_ref, wp_ref, tm, it * tm)
        a, xc, ig, r, mult = p["a"], p["xc"], p["ig"], p["r"], p["mult"]

        dyn = lax.dot_general(d2_ref[...].astype(BF16), wo_ref[...], NT, preferred_element_type=F32)

        h = hs_ref[...]
        ug = u_ref[:, LRU_W:2 * LRU_W]
        gl, dgl = _gelu_parts(ug)
        y_lru = h * gl
        rstd_l = _rstd(y_lru)
        dy_lru, d_gain_l = _rms_bwd(dyn[:, 0:LRU_W], y_lru * rstd_l, rstd_l, pv[ROW_GL:ROW_GL + 1, :])
        dh = dy_lru * gl
        du_ref[:, LRU_W:2 * LRU_W] = (dy_lru * h * dgl).astype(BF16)
        a_s[...] = a
        b_s[...] = a * dh
        mu_s[pl.ds(tm, 8), :] = mc[...]
        mc[...] = _scan_tile(a_s, b_s, mu_s, mc[...], tm, reverse=True)
        lam_t = dh + mu_s[pl.ds(1, tm), :]
        da = lam_t * e_h[pl.ds(7, tm), :]
        dmult = lam_t * (ig * xc)
        di = lam_t * (mult * xc)
        dxc = lam_t * (mult * ig)
        dla = da * a + jnp.where(p["om"] > 1e-12, dmult * (-(a * a) / mult), 0.0)
        dra = (dla * (-LRU_C * p["sp"])) * (r * (1.0 - r))
        dia = di * (ig * (1.0 - ig))
        drab = dra.astype(BF16)
        diab = dia.astype(BF16)
        dxc = dxc + _bd_t(drab, wa_ref) + _bd_t(diab, wx_ref)
        dwa_ref[...] += _bd_grad(p["xcb"], drab)
        dwx_ref[...] += _bd_grad(p["xcb"], diab)
        sig_neg_lam = _sigmoid(-pv[ROW_LAM:ROW_LAM + 1, :])
        d_lam = jnp.sum(dla * r, axis=0, keepdims=True) * (LRU_C * sig_neg_lam)

        f_x[pl.ds(0, tm), :] = dxc
        f_x[pl.ds(tm, 8), :] = cx[...]
        du_lru = jnp.zeros((tm, LRU_W), F32)
        d_cw = []
        for k in range(4):
            du_lru = du_lru + f_x[pl.ds(3 - k, tm), :] * pv[ROW_CW + k:ROW_CW + k + 1, :]
            d_cw.append(jnp.sum(dxc * e_lru[pl.ds(HALO - 3 + k, tm), :], axis=0, keepdims=True))
        du_ref[:, 0:LRU_W] = du_lru.astype(BF16)
        cx[...] = f_x[pl.ds(0, 8), :]

        zp = p["zp"]
        ps = pv[ROW_PS:ROW_PS + 1, :]
        y_pool = zp * ps
        rstd_p = _rstd(y_pool)
        dy_pool, d_gain_p = _rms_bwd(dyn[:, LRU_W:D_MODEL], y_pool * rstd_p, rstd_p, pv[ROW_GP:ROW_GP + 1, :])
        dz = dy_pool * ps
        dzb = dz.astype(BF16)
        dwp_ref[...] += _bd_grad(p["pooled_b"], dzb)
        dpooled = _bd_t(dzb, wp_ref)
        for g, w in enumerate(POOL_WINDOWS):
            f_p[pl.ds(0, tm), pl.ds(128 * g, 128)] = dpooled[:, 128 * g:128 * (g + 1)] / p["cnts"][g]
        f_p[pl.ds(tm, HALO), :] = cp[...]
        for g, w in enumerate(POOL_WINDOWS):
            lanes = pl.ds(128 * g, 128)
            acc = f_p[pl.ds(0, tm), lanes]
            for k in range(1, w):
                acc = acc + f_p[pl.ds(k, tm), lanes]
            du_ref[:, 2 * LRU_W + 128 * g:2 * LRU_W + 128 * (g + 1)] = (
                acc - dpooled[:, 128 * g:128 * (g + 1)]).astype(BF16)
        cp[...] = f_p[pl.ds(0, HALO), :]

        rows = d_cw + [
            jnp.sum(dxc, axis=0, keepdims=True),
            jnp.sum(dra, axis=0, keepdims=True),
            jnp.sum(dia, axis=0, keepdims=True),
            d_lam,
            jnp.sum(dz, axis=0, keepdims=True),
            jnp.sum(dy_pool * zp, axis=0, keepdims=True),
            d_gain_l, d_gain_p,
            jnp.zeros((4, LRU_W), F32),
        ]
        vacc_ref[...] += jnp.concatenate(rows, axis=0)

    rev = lambda w: pl.BlockSpec((tm, w), lambda s: (n_t - 1 - s, 0))
    full = lambda shape: pl.BlockSpec(shape, lambda s: (0,) * len(shape))
    return pl.pallas_call(
        body, name="mixer_bwd", grid=(n_t,),
        in_specs=[rev(D_MODEL), rev(D_IN),
                  pl.BlockSpec((HALO, D_IN), lambda s: (jnp.maximum((n_t - 1 - s) * (tm // HALO) - 1, 0), 0)),
                  rev(LRU_W),
                  pl.BlockSpec((8, LRU_W), lambda s: (jnp.maximum((n_t - 1 - s) * (tm // 8) - 1, 0), 0)),
                  full((16, LRU_W)), full((2, 256, 256)), full((2, 256, 256)), full((2, 256, 256)),
                  full((D_MODEL, D_MODEL))],
        out_specs=[rev(D_IN), full((16, LRU_W)), full((2, 256, 256)), full((2, 256, 256)), full((2, 256, 256))],
        out_shape=[_sds((T, D_IN), BF16), _sds((16, LRU_W), F32), _sds((2, 256, 256), F32),
                   _sds((2, 256, 256), F32), _sds((2, 256, 256), F32)],
        scratch_shapes=[pltpu.VMEM((HALO + tm, LRU_W), F32), pltpu.VMEM((HALO + tm, POOL_W), F32),
                        pltpu.VMEM((8 + tm, LRU_W), F32), pltpu.VMEM((tm, LRU_W), F32), pltpu.VMEM((tm, LRU_W), F32),
                        pltpu.VMEM((tm + 8, LRU_W), F32), pltpu.VMEM((tm + 8, LRU_W), F32),
                        pltpu.VMEM((tm + HALO, POOL_W), F32), pltpu.VMEM((8, LRU_W), F32),
                        pltpu.VMEM((8, LRU_W), F32), pltpu.VMEM((HALO, POOL_W), F32)],
        compiler_params=pltpu.CompilerParams(dimension_semantics=("arbitrary",)),
    )(d2, u, u, hs, hs, pv, wa, wx, wp, w_out_b)


def _mix_in_bwd(du, x, d2, w_in_b, g_mix, tm):
    T = x.shape[0]

    def body(du_ref, x_ref, d2_ref, w_ref, g_ref, dx_ref, dg_ref):
        @pl.when(pl.program_id(0) == 0)
        def _():
            dg_ref[...] = jnp.zeros((1, D_MODEL), F32)

        dh = lax.dot_general(du_ref[...], w_ref[...], NT, preferred_element_type=F32)
        xv = x_ref[...]
        rstd = _rstd(xv)
        dx, dgain = _rms_bwd(dh, xv * rstd, rstd, g_ref[...])
        dx_ref[...] = d2_ref[...] + dx
        dg_ref[...] += dgain

    row = lambda w: pl.BlockSpec((tm, w), lambda i: (i, 0))
    const = lambda shape: pl.BlockSpec(shape, lambda i: (0,) * len(shape))
    return pl.pallas_call(
        body, name="mix_in_bwd", grid=(T // tm,),
        in_specs=[row(D_IN), row(D_MODEL), row(D_MODEL), const((D_MODEL, D_IN)), const((1, D_MODEL))],
        out_specs=[row(D_MODEL), const((1, D_MODEL))],
        out_shape=[_sds((T, D_MODEL), F32), _sds((1, D_MODEL), F32)],
        compiler_params=pltpu.CompilerParams(dimension_semantics=("arbitrary",)),
    )(du, x, d2, w_in_b, g_mix)


def _block_diag(w, per):
    n, k, _ = w.shape
    out = jnp.zeros((n // per, per * k, per * k), w.dtype)
    for b in range(n):
        out = out.at[b // per, (b % per) * k:(b % per + 1) * k, (b % per) * k:(b % per + 1) * k].set(w[b])
    return out


def _diag_blocks(w, per):
    m, pk, _ = w.shape
    k = pk // per
    return jnp.stack([w[b // per, (b % per) * k:(b % per + 1) * k, (b % per) * k:(b % per + 1) * k]
                      for b in range(m * per)], axis=0)


def _local_step(x, tgt, w_in_b, w_out_b, w1_b, w3_b, w2_b, pv, wa_b, wx_b, wp_b, g_mix, g_ffn, g_fin,
                tm=512, tmx=256, tn=1408):
    u, h1 = _mix_in(x, g_mix, w_in_b, tm)
    y, hs = _mixer_fwd(u, pv, wa_b, wx_b, wp_b, tmx)
    hres, h2, g, v, d3, loss_acc, d_gfin = _ffn_fwd(x, y, w_out_b, g_ffn, w1_b, w3_b, w2_b, g_fin, tgt, tm, tn)
    dg, dv, ff, d2, d_gffn = _ffn_bwd(d3, g, v, w1_b, w3_b, w2_b, hres, g_ffn, tm, tn)
    tk = min(512, x.shape[0])
    d_w1 = _at_b(h2, dg, "grad_w1", D_MODEL, tn, tk)
    d_w3 = _at_b(h2, dv, "grad_w3", D_MODEL, tn, tk)
    d_w2 = _at_b(ff, d3, "grad_w2", tn, D_MODEL, tk)
    d_wout = _at_b(y, d2, "grad_w_out", D_MODEL, D_MODEL, tk)
    du, vacc, d_wa, d_wx, d_wp = _mixer_bwd(d2, u, hs, pv, wa_b, wx_b, wp_b, w_out_b, tmx)
    d_win = _at_b(h1, du, "grad_w_in", D_MODEL, D_IN, tk)
    grad_x, d_gmix = _mix_in_bwd(du, x, d2, w_in_b, g_mix, tm)
    return dict(loss=loss_acc[0, 0], grad_x=grad_x, w_in=d_win, w_out=d_wout, w1=d_w1, w3=d_w3, w2=d_w2,
                vacc=vacc, wa=d_wa, wx=d_wx, wp=d_wp, g_mix=d_gmix, g_ffn=d_gffn, g_fin=d_gfin)


MESH = pl.DeviceIdType.MESH
ANY = pl.BlockSpec(memory_space=pl.ANY)


def _place():
    x, y, c = lax.axis_index("x"), lax.axis_index("y"), lax.axis_index("c")
    chips = [(1 - x, y), (x, 1 - y), (1 - x, 1 - y)]
    return x, y, c, chips


def _all_gather(arrs, name):
    n = len(arrs)

    def body(*refs):
        ins, outs = refs[:n], refs[n:2 * n]
        send_sems, recv_sems, local_sems = refs[2 * n:]
        x, y, c, chips = _place()
        sibling = (x, y, 1 - c)

        def slot(a, px, py, pc):
            return outs[a].at[4 * px + 2 * py + pc]

        def copy(a, k, block, to, src=None):
            return pltpu.make_async_remote_copy(
                src_ref=slot(a, *block) if src is None else src, dst_ref=slot(a, *block),
                send_sem=send_sems.at[k, a], recv_sem=recv_sems.at[k, a], device_id=to, device_id_type=MESH)

        started = []
        for a in range(n):
            mine = pltpu.make_async_copy(ins[a], slot(a, x, y, c), local_sems.at[a])
            mine.start()
            started.append(mine)
        sends = []
        for a in range(n):
            sends.append(copy(a, 0, (x, y, c), sibling, src=ins[a]))
            sends += [copy(a, 1 + j, (x, y, c), (*chip, c), src=ins[a]) for j, chip in enumerate(chips)]
        for cp in sends:
            cp.start()
        for j, chip in enumerate(chips):
            for a in range(n):
                copy(a, 1 + j, (*chip, c), (x, y, c)).wait_recv()
                fwd = copy(a, 4 + j, (*chip, c), sibling)
                fwd.start()
                sends.append(fwd)
        for a in range(n):
            copy(a, 0, (x, y, 1 - c), (x, y, c)).wait_recv()
            for j, chip in enumerate(chips):
                copy(a, 4 + j, (*chip, 1 - c), (x, y, c)).wait_recv()
        for cp in sends:
            cp.wait_send()
        for mine in started:
            mine.wait()

    return pl.pallas_call(
        body, name=name,
        out_shape=[_sds((N_DEV,) + a.shape, a.dtype) for a in arrs],
        in_specs=[ANY] * n, out_specs=[ANY] * n,
        scratch_shapes=[pltpu.SemaphoreType.DMA((7, n)), pltpu.SemaphoreType.DMA((7, n)), pltpu.SemaphoreType.DMA((n,))],
    )(*arrs)


def _pair_exchange(arrs, name):
    n = len(arrs)

    def body(*refs):
        ins, outs = refs[:n], refs[n:2 * n]
        send_sems, recv_sems = refs[2 * n:]
        x, y, c, _ = _place()
        sibling = (x, y, 1 - c)
        sends = []
        for a in range(n):
            for j in range(4):
                cp = pltpu.make_async_remote_copy(
                    src_ref=ins[a].at[2 * j + (1 - c)], dst_ref=outs[a].at[j],
                    send_sem=send_sems.at[j, a], recv_sem=recv_sems.at[j, a], device_id=sibling, device_id_type=MESH)
                cp.start()
                sends.append(cp)
        for cp in sends:
            cp.wait()

    return pl.pallas_call(
        body, name=name,
        out_shape=[_sds((4,) + a.shape[1:], a.dtype) for a in arrs],
        in_specs=[ANY] * n, out_specs=[ANY] * n,
        scratch_shapes=[pltpu.SemaphoreType.DMA((4, n)), pltpu.SemaphoreType.DMA((4, n))],
    )(*arrs)


def _chip_exchange(arrs, name):
    n = len(arrs)

    def body(*refs):
        ins, outs = refs[:n], refs[n:2 * n]
        send_sems, recv_sems, local_sems = refs[2 * n:]
        x, y, c, chips = _place()
        me = 2 * x + y
        local = []
        for a in range(n):
            cp = pltpu.make_async_copy(ins[a].at[me], outs[a].at[me], local_sems.at[a])
            cp.start()
            local.append(cp)
        sends = []
        for a in range(n):
            for k, (px, py) in enumerate(chips):
                cp = pltpu.make_async_remote_copy(
                    src_ref=ins[a].at[2 * px + py], dst_ref=outs[a].at[me],
                    send_sem=send_sems.at[k, a], recv_sem=recv_sems.at[k, a], device_id=(px, py, c), device_id_type=MESH)
                cp.start()
                sends.append((cp, a, k, px, py))
        for cp, a, k, px, py in sends:
            cp.wait_send()
            pltpu.make_async_remote_copy(
                src_ref=ins[a].at[me], dst_ref=outs[a].at[2 * px + py],
                send_sem=send_sems.at[k, a], recv_sem=recv_sems.at[k, a], device_id=(px, py, c), device_id_type=MESH).wait_recv()
        for cp in local:
            cp.wait()

    return pl.pallas_call(
        body, name=name,
        out_shape=[_sds(a.shape, a.dtype) for a in arrs],
        in_specs=[ANY] * n, out_specs=[ANY] * n,
        scratch_shapes=[pltpu.SemaphoreType.DMA((3, n)), pltpu.SemaphoreType.DMA((3, n)), pltpu.SemaphoreType.DMA((n,))],
    )(*arrs)


def _pair_sum(g, r1, c_arr, name):
    _, _, R, C = g.shape
    tr = R if R <= 512 else 256

    def body(c_ref, g_ref, r_ref, o_ref):
        o_ref[...] = (g_ref[...] + r_ref[...]).astype(BF16)

    return pl.pallas_call(
        body, name=name,
        grid_spec=pltpu.PrefetchScalarGridSpec(
            num_scalar_prefetch=1, grid=(4, R // tr),
            in_specs=[pl.BlockSpec((None, None, tr, C), lambda j, i, c_ref: (j, c_ref[0], i, 0)),
                      pl.BlockSpec((None, tr, C), lambda j, i, c_ref: (j, i, 0))],
            out_specs=pl.BlockSpec((None, tr, C), lambda j, i, c_ref: (j, i, 0))),
        out_shape=_sds((4, R, C), BF16),
    )(c_arr, g, r1)


def _adamw(w, g, m, v):
    m = ADAM_B1 * m + (1.0 - ADAM_B1) * g
    v = ADAM_B2 * v + (1.0 - ADAM_B2) * (g * g)
    m_hat = m / (1.0 - ADAM_B1 ** ADAM_STEP)
    v_hat = v / (1.0 - ADAM_B2 ** ADAM_STEP)
    delta = -ADAM_LR * (m_hat / (jnp.sqrt(v_hat) + ADAM_EPS) + ADAM_WD * w)
    return delta, m, v


def _adam_shard(w, m, v, parts, name):
    R, C = w.shape
    tr = R if R <= 512 else 256

    def body(w_ref, m_ref, v_ref, p_ref, g_ref, d_ref, nm_ref, nv_ref):
        g = p_ref[0].astype(F32)
        for j in range(1, 4):
            g = g + p_ref[j].astype(F32)
        delta, nm, nv = _adamw(w_ref[...], g, m_ref[...], v_ref[...])
        g_ref[...] = g
        d_ref[...] = delta
        nm_ref[...] = nm
        nv_ref[...] = nv

    blk = pl.BlockSpec((tr, C), lambda i: (i, 0))
    return pl.pallas_call(
        body, name=name, grid=(R // tr,),
        in_specs=[blk, blk, blk, pl.BlockSpec((4, tr, C), lambda i: (0, i, 0))],
        out_specs=[blk] * 4, out_shape=[_sds((R, C), F32)] * 4,
        compiler_params=pltpu.CompilerParams(dimension_semantics=("parallel",)),
    )(w, m, v, parts)


def _adam_small(w, m, v, parts):
    def body(w_ref, m_ref, v_ref, p_ref, g_ref, d_ref, nm_ref, nv_ref):
        g = p_ref[0]
        for j in range(1, N_DEV):
            g = g + p_ref[j]
        delta, nm, nv = _adamw(w_ref[...], g, m_ref[...], v_ref[...])
        g_ref[...] = g
        d_ref[...] = delta
        nm_ref[...] = nm
        nv_ref[...] = nv

    return pl.pallas_call(body, name="adam_small", out_shape=[_sds((SMALL_ROWS, LRU_W), F32)] * 4)(w, m, v, parts)


def _pack_small(vec_rows, g_mix, g_ffn, g_fin, wa, wx, wp):
    return jnp.concatenate(
        [vec_rows, g_mix.reshape(2, LRU_W), g_ffn.reshape(2, LRU_W), g_fin.reshape(2, LRU_W), jnp.zeros((2, LRU_W), F32),
         wa.reshape(64, LRU_W), wx.reshape(64, LRU_W), wp.reshape(128, LRU_W)], axis=0)


def _vec_rows(conv_w_full, conv_b, ba, bx, lam, pb, ps, gl, gp):
    return jnp.concatenate([conv_w_full, conv_b, ba, bx, lam, pb, ps, gl, gp, jnp.zeros((4, LRU_W), F32)], axis=0)


def _unpack_small(p, col0):
    return dict(
        conv_w=lax.dynamic_slice(p[0:4], (0, col0), (4, 64)).reshape(1, 4, 64),
        conv_b=p[4:5], gate_a_b=p[5:6], gate_x_b=p[6:7], lru_lambda=p[7:8], pool_b=p[8:9], pool_scale=p[9:10],
        norm_lru_g=p[10:11], norm_pool_g=p[11:12],
        norm_mix_g=p[16:18].reshape(1, D_MODEL), norm_ffn_g=p[18:20].reshape(1, D_MODEL),
        final_norm_g=p[20:22].reshape(D_MODEL),
        gate_a_w=p[24:88].reshape(1, 8, 64, 64), gate_x_w=p[88:152].reshape(1, 8, 64, 64),
        pool_w=p[152:280].reshape(1, 4, 128, 128))


WEIGHT_ORDER = ['norm_mix_g', 'w_in', 'conv_w', 'conv_b', 'gate_a_w', 'gate_a_b', 'gate_x_w', 'gate_x_b', 'lru_lambda',
                'pool_w', 'pool_b', 'pool_scale', 'norm_lru_g', 'norm_pool_g', 'w_out', 'norm_ffn_g', 'ffn_w1', 'ffn_w3',
                'ffn_w2', 'final_norm_g']


def kernel(x, norm_mix_g, w_in, conv_w, conv_b, gate_a_w, gate_a_b, gate_x_w, gate_x_b, lru_lambda, pool_w, pool_b, pool_scale, norm_lru_g, norm_pool_g, w_out, norm_ffn_g, ffn_w1, ffn_w3, ffn_w2, final_norm_g, loss_target, m_norm_mix_g, m_w_in, m_conv_w, m_conv_b, m_gate_a_w, m_gate_a_b, m_gate_x_w, m_gate_x_b, m_lru_lambda, m_pool_w, m_pool_b, m_pool_scale, m_norm_lru_g, m_norm_pool_g, m_w_out, m_norm_ffn_g, m_ffn_w1, m_ffn_w3, m_ffn_w2, m_final_norm_g, v_norm_mix_g, v_w_in, v_conv_w, v_conv_b, v_gate_a_w, v_gate_a_b, v_gate_x_w, v_gate_x_b, v_lru_lambda, v_pool_w, v_pool_b, v_pool_scale, v_norm_lru_g, v_norm_pool_g, v_w_out, v_norm_ffn_g, v_ffn_w1, v_ffn_w3, v_ffn_w2, v_final_norm_g):
    ax, ay, ac = lax.axis_index("x"), lax.axis_index("y"), lax.axis_index("c")
    dev = 4 * ax + 2 * ay + ac
    col0 = 64 * dev

    gathered = _all_gather([w_in[0].astype(BF16), w_out[0].astype(BF16), ffn_w1[0].astype(BF16),
                            ffn_w3[0].astype(BF16), ffn_w2[0].astype(BF16), conv_w[0]], "gather_weights")
    cols = lambda a: a.transpose(1, 0, 2).reshape(a.shape[1], N_DEV * a.shape[2])
    rows = lambda a: a.reshape(N_DEV * a.shape[1], a.shape[2])
    w_in_b, w_out_b, w1_b, w3_b, w2_b = cols(gathered[0]), rows(gathered[1]), cols(gathered[2]), cols(gathered[3]), rows(gathered[4])
    conv_w_full = cols(gathered[5])

    pv = _vec_rows(conv_w_full, conv_b, gate_a_b, gate_x_b, lru_lambda, pool_b, pool_scale, norm_lru_g, norm_pool_g)
    wa_b = _block_diag(gate_a_w[0], 4).astype(BF16)
    wx_b = _block_diag(gate_x_w[0], 4).astype(BF16)
    wp_b = _block_diag(pool_w[0], 2).astype(BF16)
    L = _local_step(x[0], loss_target[0], w_in_b, w_out_b, w1_b, w3_b, w2_b, pv, wa_b, wx_b, wp_b,
                    norm_mix_g, norm_ffn_g, final_norm_g.reshape(1, D_MODEL))
    loss = lax.psum(L["loss"], ("x", "y", "c"))

    col_blocks = lambda a, s: a.reshape(a.shape[0], N_DEV, s).transpose(1, 0, 2)
    row_blocks = lambda a, s: a.reshape(N_DEV, s, a.shape[1])
    big = [col_blocks(L["w_in"], 192), row_blocks(L["w_out"], 128), col_blocks(L["w1"], 352),
           col_blocks(L["w3"], 352), row_blocks(L["w2"], 352)]
    from_sibling = _pair_exchange(big, "grads_to_sibling")
    c_arr = jnp.reshape(ac, (1,)).astype(jnp.int32)
    names = ["w_in", "w_out", "w1", "w3", "w2"]
    chip_sums = [_pair_sum(g.reshape((4, 2) + g.shape[1:]), r, c_arr, "pair_sum_" + nm)
                 for g, r, nm in zip(big, from_sibling, names)]
    parts = _chip_exchange(chip_sums, "grads_to_chips")

    small_local = _pack_small(L["vacc"], L["g_mix"], L["g_ffn"], L["g_fin"],
                              _diag_blocks(L["wa"], 4), _diag_blocks(L["wx"], 4), _diag_blocks(L["wp"], 2))
    small_parts = _all_gather([small_local], "gather_small_grads")[0]

    res = {}
    shard_w = dict(w_in=(w_in, m_w_in, v_w_in), w_out=(w_out, m_w_out, v_w_out), ffn_w1=(ffn_w1, m_ffn_w1, v_ffn_w1),
                   ffn_w3=(ffn_w3, m_ffn_w3, v_ffn_w3), ffn_w2=(ffn_w2, m_ffn_w2, v_ffn_w2))
    for (nm, (w, m, v)), p in zip(shard_w.items(), parts):
        outs = _adam_shard(w[0], m[0], v[0], p, "adam_" + nm)
        res[nm] = [o[None] for o in outs]

    def packed(cw, cb, wa, ba, wx, bx, lam, pw, pb, ps, gl, gp, gm, gf, gfin):
        cw_full = lax.dynamic_update_slice(jnp.zeros((4, LRU_W), F32), cw[0], (0, col0))
        return _pack_small(_vec_rows(cw_full, cb, ba, bx, lam, pb, ps, gl, gp), gm, gf, gfin, wa[0], wx[0], pw[0])

    sw = packed(conv_w, conv_b, gate_a_w, gate_a_b, gate_x_w, gate_x_b, lru_lambda, pool_w, pool_b, pool_scale,
                norm_lru_g, norm_pool_g, norm_mix_g, norm_ffn_g, final_norm_g)
    sm = packed(m_conv_w, m_conv_b, m_gate_a_w, m_gate_a_b, m_gate_x_w, m_gate_x_b, m_lru_lambda, m_pool_w, m_pool_b,
                m_pool_scale, m_norm_lru_g, m_norm_pool_g, m_norm_mix_g, m_norm_ffn_g, m_final_norm_g)
    sv = packed(v_conv_w, v_conv_b, v_gate_a_w, v_gate_a_b, v_gate_x_w, v_gate_x_b, v_lru_lambda, v_pool_w, v_pool_b,
                v_pool_scale, v_norm_lru_g, v_norm_pool_g, v_norm_mix_g, v_norm_ffn_g, v_final_norm_g)
    small_out = [_unpack_small(o, col0) for o in _adam_small(sw, sm, sv, small_parts)]
    for nm in small_out[0]:
        res[nm] = [o[nm] for o in small_out]

    out = [loss, L["grad_x"][None]]
    for kind in range(4):
        out += [res[nm][kind] for nm in WEIGHT_ORDER]
    return tuple(out)
```

```python
import jax
import jax.numpy as jnp
from jax import lax
from jax.experimental import pallas as pl
from jax.experimental.pallas import tpu as pltpu

F32 = jnp.float32
BF16 = jnp.bfloat16

D_MODEL = 1024
LRU_W = 512
POOL_W = 512
D_IN = 1536
D_FF = 2816
POOL_WINDOWS = (2, 4, 8, 16)
EPS = 1e-6
LRU_C = 8.0
N_DEV = 8
HALO = 16

ADAM_LR = 0.001
ADAM_B1 = 0.9
ADAM_B2 = 0.999
ADAM_EPS = 1e-08
ADAM_WD = 0.01
ADAM_STEP = 10

ROW_CW, ROW_CB, ROW_BA, ROW_BX, ROW_LAM, ROW_PB, ROW_PS, ROW_GL, ROW_GP = 0, 4, 5, 6, 7, 8, 9, 10, 11
SMALL_ROWS = 280

NT = (((1,), (1,)), ((), ()))
TN = (((0,), (0,)), ((), ()))


def _sds(shape, dtype):
    return jax.ShapeDtypeStruct(shape, dtype)


def _sigmoid(x):
    return 1.0 / (1.0 + jnp.exp(-x))


def _gelu_parts(x):
    c = 0.7978845608028654
    inner = c * (x + 0.044715 * (x * x * x))
    th = jnp.tanh(inner)
    g = 0.5 * x * (1.0 + th)
    dg = 0.5 * (1.0 + th) + 0.5 * x * (1.0 - th * th) * (c * (1.0 + 3.0 * 0.044715 * (x * x)))
    return g, dg


def _one_minus_exp(x):
    series = -x * (1.0 + x * (0.5 + x * (1.0 / 6.0 + x * (1.0 / 24.0 + x * (1.0 / 120.0)))))
    return jnp.where(x > -0.1, series, 1.0 - jnp.exp(x))


def _rstd(x):
    return lax.rsqrt(jnp.mean(x * x, axis=-1, keepdims=True) + EPS)


def _rms_bwd(dy, xhat, rstd, gain):
    dxh = dy * gain
    dx = rstd * (dxh - xhat * jnp.mean(dxh * xhat, axis=-1, keepdims=True))
    return dx, jnp.sum(dy * xhat, axis=0, keepdims=True)


def _bd(xb, w_ref):
    return jnp.concatenate(
        [jnp.dot(xb[:, :256], w_ref[0], preferred_element_type=F32),
         jnp.dot(xb[:, 256:], w_ref[1], preferred_element_type=F32)], axis=1)


def _bd_t(xb, w_ref):
    return jnp.concatenate(
        [lax.dot_general(xb[:, :256], w_ref[0], NT, preferred_element_type=F32),
         lax.dot_general(xb[:, 256:], w_ref[1], NT, preferred_element_type=F32)], axis=1)


def _bd_grad(xb, db):
    return jnp.stack(
        [lax.dot_general(xb[:, :256], db[:, :256], TN, preferred_element_type=F32),
         lax.dot_general(xb[:, 256:], db[:, 256:], TN, preferred_element_type=F32)], axis=0)


def _mixer_pre(e_lru, e_pool, pv, wa_ref, wx_ref, wp_ref, tm, t0):
    xc = pv[ROW_CB:ROW_CB + 1, :]
    for k in range(4):
        xc = xc + e_lru[pl.ds(HALO - 3 + k, tm), :] * pv[ROW_CW + k:ROW_CW + k + 1, :]
    xcb = xc.astype(BF16)
    r = _sigmoid(_bd(xcb, wa_ref) + pv[ROW_BA:ROW_BA + 1, :])
    ig = _sigmoid(_bd(xcb, wx_ref) + pv[ROW_BX:ROW_BX + 1, :])
    z = -pv[ROW_LAM:ROW_LAM + 1, :]
    sp = jnp.maximum(z, 0.0) + jnp.log(1.0 + jnp.exp(-jnp.abs(z)))
    la = (-LRU_C * r) * sp
    a = jnp.exp(la)
    om = _one_minus_exp(2.0 * la)
    mult = jnp.sqrt(jnp.maximum(om, 1e-12))
    t = t0 + lax.broadcasted_iota(jnp.int32, (tm, 1), 0)
    parts, cnts = [], []
    for g, w in enumerate(POOL_WINDOWS):
        lanes = pl.ds(128 * g, 128)
        cur = e_pool[pl.ds(HALO, tm), lanes]
        s = cur
        for k in range(1, w):
            s = s + e_pool[pl.ds(HALO - k, tm), lanes]
        cnt = jnp.minimum(t + 1, w).astype(F32)
        cnts.append(cnt)
        parts.append(s / cnt - cur)
    pooled = jnp.concatenate(parts, axis=1)
    pooled_b = pooled.astype(BF16)
    zp = _bd(pooled_b, wp_ref) + pv[ROW_PB:ROW_PB + 1, :]
    return dict(xc=xc, xcb=xcb, r=r, ig=ig, sp=sp, a=a, om=om, mult=mult, pooled_b=pooled_b, zp=zp, cnts=cnts)


def _scan_tile(a_ref, b_ref, out_ref, carry, tm, reverse):
    row = lax.broadcasted_iota(jnp.int32, (8, LRU_W), 0)
    nblk = tm // 8

    def step(i, hin):
        blk = (nblk - 1 - i) if reverse else i
        r0 = pl.multiple_of(blk * 8, 8)
        av = a_ref[pl.ds(r0, 8), :]
        bv = b_ref[pl.ds(r0, 8), :]
        for d in (1, 2, 4):
            sh = (8 - d) if reverse else d
            a_s = pltpu.roll(av, sh, 0)
            b_s = pltpu.roll(bv, sh, 0)
            m = (row < 8 - d) if reverse else (row >= d)
            bv = jnp.where(m, av * b_s + bv, bv)
            av = jnp.where(m, av * a_s, av)
        hv = av * hin + bv
        out_ref[pl.ds(r0, 8), :] = hv
        edge = hv[0:1, :] if reverse else hv[7:8, :]
        return jnp.broadcast_to(edge, (8, LRU_W))

    return lax.fori_loop(0, nblk, step, carry)


MESH = pl.DeviceIdType.MESH
ANY = pl.BlockSpec(memory_space=pl.ANY)


def _place():
    x, y, c = lax.axis_index("x"), lax.axis_index("y"), lax.axis_index("c")
    chips = [(1 - x, y), (x, 1 - y), (1 - x, 1 - y)]
    return x, y, c, chips


class _Gather:
    def __init__(self, ins, outs, send_sems, recv_sems, local_sems):
        self.ins, self.outs, self.n = ins, outs, len(ins)
        self.send_sems, self.recv_sems, self.local_sems = send_sems, recv_sems, local_sems

    @staticmethod
    def scratch(n):
        return [pltpu.SemaphoreType.DMA((7, n)), pltpu.SemaphoreType.DMA((7, n)), pltpu.SemaphoreType.DMA((n,))]

    def _slot(self, a, px, py, pc):
        return self.outs[a].at[4 * px + 2 * py + pc]

    def _copy(self, a, k, block, to, src=None):
        return pltpu.make_async_remote_copy(
            src_ref=self._slot(a, *block) if src is None else src, dst_ref=self._slot(a, *block),
            send_sem=self.send_sems.at[k, a], recv_sem=self.recv_sems.at[k, a], device_id=to, device_id_type=MESH)

    def _mine(self, a):
        x, y, c, _ = _place()
        return pltpu.make_async_copy(self.ins[a], self._slot(a, x, y, c), self.local_sems.at[a])

    def _first(self, a):
        x, y, c, chips = _place()
        me = (x, y, c)
        return ([self._copy(a, 0, me, (x, y, 1 - c), src=self.ins[a])]
                + [self._copy(a, 1 + j, me, (*chip, c), src=self.ins[a]) for j, chip in enumerate(chips)])

    def start(self):
        for a in range(self.n):
            self._mine(a).start()
        for a in range(self.n):
            for cp in self._first(a):
                cp.start()

    def finish(self):
        x, y, c, chips = _place()
        me, sibling = (x, y, c), (x, y, 1 - c)
        passed = []
        for j, chip in enumerate(chips):
            for a in range(self.n):
                self._copy(a, 1 + j, (*chip, c), me).wait_recv()
                fwd = self._copy(a, 4 + j, (*chip, c), sibling)
                fwd.start()
                passed.append(fwd)
        for a in range(self.n):
            self._copy(a, 0, (x, y, 1 - c), me).wait_recv()
            for j, chip in enumerate(chips):
                self._copy(a, 4 + j, (*chip, 1 - c), me).wait_recv()
        for a in range(self.n):
            for cp in self._first(a):
                cp.wait_send()
        for cp in passed:
            cp.wait_send()
        for a in range(self.n):
            self._mine(a).wait()


def _all_gather(arrs, name):
    n = len(arrs)

    def body(*refs):
        g = _Gather(refs[:n], refs[n:2 * n], *refs[2 * n:])
        g.start()
        g.finish()

    return pl.pallas_call(
        body, name=name,
        out_shape=[_sds((N_DEV,) + a.shape, a.dtype) for a in arrs],
        in_specs=[ANY] * n, out_specs=[ANY] * n, scratch_shapes=_Gather.scratch(n),
    )(*arrs)


def _pair_exchange(arrs, name):
    n = len(arrs)

    def body(*refs):
        ins, outs = refs[:n], refs[n:2 * n]
        send_sems, recv_sems = refs[2 * n:]
        x, y, c, _ = _place()
        sibling = (x, y, 1 - c)
        sends = []
        for a in range(n):
            for j in range(4):
                cp = pltpu.make_async_remote_copy(
                    src_ref=ins[a].at[2 * j + (1 - c)], dst_ref=outs[a].at[j],
                    send_sem=send_sems.at[j, a], recv_sem=recv_sems.at[j, a], device_id=sibling, device_id_type=MESH)
                cp.start()
                sends.append(cp)
        for cp in sends:
            cp.wait()

    return pl.pallas_call(
        body, name=name,
        out_shape=[_sds((4,) + a.shape[1:], a.dtype) for a in arrs],
        in_specs=[ANY] * n, out_specs=[ANY] * n,
        scratch_shapes=[pltpu.SemaphoreType.DMA((4, n)), pltpu.SemaphoreType.DMA((4, n))],
    )(*arrs)


class _ChipExchange:
    def __init__(self, ins, outs, send_sems, recv_sems, local_sems):
        self.ins, self.outs, self.n = ins, outs, len(ins)
        self.send_sems, self.recv_sems, self.local_sems = send_sems, recv_sems, local_sems

    @staticmethod
    def scratch(n):
        return [pltpu.SemaphoreType.DMA((3, n)), pltpu.SemaphoreType.DMA((3, n)), pltpu.SemaphoreType.DMA((n,))]

    def _local(self, a):
        x, y, _, _ = _place()
        me = 2 * x + y
        return pltpu.make_async_copy(self.ins[a].at[me], self.outs[a].at[me], self.local_sems.at[a])

    def _copies(self, a):
        x, y, c, chips = _place()
        me = 2 * x + y
        return [(pltpu.make_async_remote_copy(
                     src_ref=self.ins[a].at[2 * px + py], dst_ref=self.outs[a].at[me],
                     send_sem=self.send_sems.at[k, a], recv_sem=self.recv_sems.at[k, a],
                     device_id=(px, py, c), device_id_type=MESH),
                 pltpu.make_async_remote_copy(
                     src_ref=self.ins[a].at[me], dst_ref=self.outs[a].at[2 * px + py],
                     send_sem=self.send_sems.at[k, a], recv_sem=self.recv_sems.at[k, a],
                     device_id=(px, py, c), device_id_type=MESH))
                for k, (px, py) in enumerate(chips)]

    def start(self):
        for a in range(self.n):
            self._local(a).start()
        for a in range(self.n):
            for send, _ in self._copies(a):
                send.start()

    def finish(self):
        for a in range(self.n):
            for send, recv in self._copies(a):
                send.wait_send()
                recv.wait_recv()
        for a in range(self.n):
            self._local(a).wait()


def _chip_exchange(arrs, name):
    n = len(arrs)

    def body(*refs):
        e = _ChipExchange(refs[:n], refs[n:2 * n], *refs[2 * n:])
        e.start()
        e.finish()

    return pl.pallas_call(
        body, name=name, out_shape=[_sds(a.shape, a.dtype) for a in arrs],
        in_specs=[ANY] * n, out_specs=[ANY] * n, scratch_shapes=_ChipExchange.scratch(n),
    )(*arrs)


def _mix_in(x, g_mix, w_in_b, tm):
    T = x.shape[0]

    def body(x_ref, g_ref, w_ref, u_ref, h_ref):
        xv = x_ref[...]
        h = (xv * _rstd(xv) * g_ref[...]).astype(BF16)
        h_ref[...] = h
        u_ref[...] = jnp.dot(h, w_ref[...], preferred_element_type=F32)

    return pl.pallas_call(
        body, name="mix_in", grid=(T // tm,),
        in_specs=[pl.BlockSpec((tm, D_MODEL), lambda i: (i, 0)),
                  pl.BlockSpec((1, D_MODEL), lambda i: (0, 0)),
                  pl.BlockSpec((D_MODEL, D_IN), lambda i: (0, 0))],
        out_specs=[pl.BlockSpec((tm, D_IN), lambda i: (i, 0)),
                   pl.BlockSpec((tm, D_MODEL), lambda i: (i, 0))],
        out_shape=[_sds((T, D_IN), F32), _sds((T, D_MODEL), BF16)],
        compiler_params=pltpu.CompilerParams(dimension_semantics=("parallel",)),
    )(x, g_mix, w_in_b)


def _mixer_fwd(u, pv, wa, wx, wp, tm, shards=()):
    T = u.shape[0]
    n_s = len(shards)
    n_t = T // tm

    def body(u_ref, pv_ref, wa_ref, wx_ref, wp_ref, *rest):
        sh_in, rest = rest[:n_s], rest[n_s:]
        y_ref, hs_ref = rest[:2]
        sh_out, rest = rest[2:2 + n_s], rest[2 + n_s:]
        e_lru, e_pool, a_s, b_s, hc = rest[:5]
        gather = _Gather(sh_in, sh_out, *rest[5:]) if n_s else None
        i = pl.program_id(0)

        @pl.when(i == 0)
        def _():
            e_lru[pl.ds(0, HALO), :] = jnp.zeros((HALO, LRU_W), F32)
            e_pool[pl.ds(0, HALO), :] = jnp.zeros((HALO, POOL_W), F32)
            hc[...] = jnp.zeros((8, LRU_W), F32)
            if gather:
                gather.start()

        e_lru[pl.ds(HALO, tm), :] = u_ref[:, 0:LRU_W]
        e_pool[pl.ds(HALO, tm), :] = u_ref[:, 2 * LRU_W:D_IN]
        pv = pv_ref[...]
        p = _mixer_pre(e_lru, e_pool, pv, wa_ref, wx_ref, wp_ref, tm, i * tm)
        a_s[...] = p["a"]
        b_s[...] = p["mult"] * (p["ig"] * p["xc"])
        hc[...] = _scan_tile(a_s, b_s, hs_ref, hc[...], tm, reverse=False)
        gl, _ = _gelu_parts(u_ref[:, LRU_W:2 * LRU_W])
        y_lru = hs_ref[...] * gl
        y_ref[:, 0:LRU_W] = (y_lru * _rstd(y_lru) * pv[ROW_GL:ROW_GL + 1, :]).astype(BF16)
        y_pool = p["zp"] * pv[ROW_PS:ROW_PS + 1, :]
        y_ref[:, LRU_W:D_MODEL] = (y_pool * _rstd(y_pool) * pv[ROW_GP:ROW_GP + 1, :]).astype(BF16)
        e_lru[pl.ds(0, HALO), :] = e_lru[pl.ds(tm, HALO), :]
        e_pool[pl.ds(0, HALO), :] = e_pool[pl.ds(tm, HALO), :]

        if gather:
            @pl.when(i == n_t - 1)
            def _():
                gather.finish()

    full = lambda shape: pl.BlockSpec(shape, lambda i: (0,) * len(shape))
    outs = pl.pallas_call(
        body, name="mixer_fwd", grid=(n_t,),
        in_specs=[pl.BlockSpec((tm, D_IN), lambda i: (i, 0)), full((16, LRU_W)),
                  full((2, 256, 256)), full((2, 256, 256)), full((2, 256, 256))] + [ANY] * n_s,
        out_specs=[pl.BlockSpec((tm, D_MODEL), lambda i: (i, 0)), pl.BlockSpec((tm, LRU_W), lambda i: (i, 0))] + [ANY] * n_s,
        out_shape=[_sds((T, D_MODEL), BF16), _sds((T, LRU_W), F32)] + [_sds((N_DEV,) + a.shape, a.dtype) for a in shards],
        scratch_shapes=[pltpu.VMEM((HALO + tm, LRU_W), F32), pltpu.VMEM((HALO + tm, POOL_W), F32),
                        pltpu.VMEM((tm, LRU_W), F32), pltpu.VMEM((tm, LRU_W), F32), pltpu.VMEM((8, LRU_W), F32)]
        + (_Gather.scratch(n_s) if n_s else []),
        compiler_params=pltpu.CompilerParams(dimension_semantics=("arbitrary",)),
    )(u, pv, wa, wx, wp, *shards)
    return outs[0], outs[1], list(outs[2:])


def _ffn_fwd(x, y, w_out_b, g_ffn, w1_b, w3_b, w2_b, g_fin, tgt, tm, tn):
    T = x.shape[0]
    n_j = D_FF // tn

    def body(x_ref, y_ref, wo_ref, gf_ref, w1_ref, w3_ref, w2_ref, gfin_ref, tgt_ref,
             hres_ref, h2_ref, g_ref, v_ref, d3_ref, loss_ref, dgfin_ref, acc):
        i, j = pl.program_id(0), pl.program_id(1)

        @pl.when(j == 0)
        def _():
            hr = x_ref[...] + jnp.dot(y_ref[...], wo_ref[...], preferred_element_type=F32)
            hres_ref[...] = hr
            h2_ref[...] = (hr * _rstd(hr) * gf_ref[...]).astype(BF16)
            acc[...] = jnp.zeros((tm, D_MODEL), F32)

        @pl.when((j == 0) & (i == 0))
        def _():
            loss_ref[...] = jnp.zeros((8, 128), F32)
            dgfin_ref[...] = jnp.zeros((1, D_MODEL), F32)

        h2 = h2_ref[...]
        g = jnp.dot(h2, w1_ref[...], preferred_element_type=F32)
        v = jnp.dot(h2, w3_ref[...], preferred_element_type=F32)
        g_ref[...] = g.astype(BF16)
        v_ref[...] = v.astype(BF16)
        ff = ((g * _sigmoid(g)) * v).astype(BF16)
        acc[...] += jnp.dot(ff, w2_ref[...], preferred_element_type=F32)

        @pl.when(j == n_j - 1)
        def _():
            h3 = hres_ref[...] + acc[...]
            rstd = _rstd(h3)
            xh = h3 * rstd
            gfin = gfin_ref[...]
            err = xh * gfin - tgt_ref[...]
            loss_ref[...] += 0.5 * jnp.sum(jnp.mean(err * err, axis=-1, keepdims=True))
            dout = err * (1.0 / D_MODEL)
            dx, dgain = _rms_bwd(dout, xh, rstd, gfin)
            d3_ref[...] = dx
            dgfin_ref[...] += dgain

    row = lambda w: pl.BlockSpec((tm, w), lambda i, j: (i, 0))
    const = lambda shape: pl.BlockSpec(shape, lambda i, j: (0,) * len(shape))
    return pl.pallas_call(
        body, name="ffn_fwd", grid=(T // tm, n_j),
        in_specs=[row(D_MODEL), row(D_MODEL), const((D_MODEL, D_MODEL)), const((1, D_MODEL)),
                  pl.BlockSpec((D_MODEL, tn), lambda i, j: (0, j)), pl.BlockSpec((D_MODEL, tn), lambda i, j: (0, j)),
                  pl.BlockSpec((tn, D_MODEL), lambda i, j: (j, 0)), const((1, D_MODEL)), row(D_MODEL)],
        out_specs=[row(D_MODEL), row(D_MODEL),
                   pl.BlockSpec((tm, tn), lambda i, j: (i, j)), pl.BlockSpec((tm, tn), lambda i, j: (i, j)),
                   row(D_MODEL), const((8, 128)), const((1, D_MODEL))],
        out_shape=[_sds((T, D_MODEL), F32), _sds((T, D_MODEL), BF16), _sds((T, D_FF), BF16), _sds((T, D_FF), BF16),
                   _sds((T, D_MODEL), F32), _sds((8, 128), F32), _sds((1, D_MODEL), F32)],
        scratch_shapes=[pltpu.VMEM((tm, D_MODEL), F32)],
        compiler_params=pltpu.CompilerParams(dimension_semantics=("arbitrary", "arbitrary")),
    )(x, y, w_out_b, g_ffn, w1_b, w3_b, w2_b, g_fin, tgt)


def _ffn_bwd(d3, g, v, w1_b, w3_b, w2_b, hres, g_ffn, tm, tn):
    T = d3.shape[0]
    n_j = D_FF // tn

    def body(d3_ref, g_ref, v_ref, w1_ref, w3_ref, w2_ref, hres_ref, gf_ref,
             dg_ref, dv_ref, ff_ref, d2_ref, dgffn_ref, acc):
        i, j = pl.program_id(0), pl.program_id(1)

        @pl.when(j == 0)
        def _():
            acc[...] = jnp.zeros((tm, D_MODEL), F32)

        @pl.when((j == 0) & (i == 0))
        def _():
            dgffn_ref[...] = jnp.zeros((1, D_MODEL), F32)

        dff = lax.dot_general(d3_ref[...].astype(BF16), w2_ref[...], NT, preferred_element_type=F32)
        gv = g_ref[...].astype(F32)
        vv = v_ref[...].astype(F32)
        sg = _sigmoid(gv)
        sl = gv * sg
        dgb = (dff * vv * (sg * (1.0 + gv * (1.0 - sg)))).astype(BF16)
        dvb = (dff * sl).astype(BF16)
        dg_ref[...] = dgb
        dv_ref[...] = dvb
        ff_ref[...] = (sl * vv).astype(BF16)
        acc[...] += (lax.dot_general(dgb, w1_ref[...], NT, preferred_element_type=F32)
                     + lax.dot_general(dvb, w3_ref[...], NT, preferred_element_type=F32))

        @pl.when(j == n_j - 1)
        def _():
            hr = hres_ref[...]
            rstd = _rstd(hr)
            dx, dgain = _rms_bwd(acc[...], hr * rstd, rstd, gf_ref[...])
            d2_ref[...] = d3_ref[...] + dx
            dgffn_ref[...] += dgain

    row = lambda w: pl.BlockSpec((tm, w), lambda i, j: (i, 0))
    tile = pl.BlockSpec((tm, tn), lambda i, j: (i, j))
    const = lambda shape: pl.BlockSpec(shape, lambda i, j: (0,) * len(shape))
    return pl.pallas_call(
        body, name="ffn_bwd", grid=(T // tm, n_j),
        in_specs=[row(D_MODEL), tile, tile,
                  pl.BlockSpec((D_MODEL, tn), lambda i, j: (0, j)), pl.BlockSpec((D_MODEL, tn), lambda i, j: (0, j)),
                  pl.BlockSpec((tn, D_MODEL), lambda i, j: (j, 0)), row(D_MODEL), const((1, D_MODEL))],
        out_specs=[tile, tile, tile, row(D_MODEL), const((1, D_MODEL))],
        out_shape=[_sds((T, D_FF), BF16), _sds((T, D_FF), BF16), _sds((T, D_FF), BF16),
                   _sds((T, D_MODEL), F32), _sds((1, D_MODEL), F32)],
        scratch_shapes=[pltpu.VMEM((tm, D_MODEL), F32)],
        compiler_params=pltpu.CompilerParams(dimension_semantics=("arbitrary", "arbitrary")),
    )(d3, g, v, w1_b, w3_b, w2_b, hres, g_ffn)


def _at_b(a, b, name, tmm, tn, tk, col_block=None):
    T, M = a.shape
    N = b.shape[1]
    n_k = T // tk

    if col_block is None:
        def body(a_ref, b_ref, o_ref):
            @pl.when(pl.program_id(2) == 0)
            def _():
                o_ref[...] = jnp.zeros((tmm, tn), F32)

            o_ref[...] += lax.dot_general(a_ref[...].astype(BF16), b_ref[...].astype(BF16), TN,
                                          preferred_element_type=F32)

        out_spec = pl.BlockSpec((tmm, tn), lambda m, n, k: (m, n))
        out_shape, scratch = _sds((M, N), F32), []
    else:
        s = col_block
        per = tn // s

        def body(a_ref, b_ref, o_ref, acc):
            k = pl.program_id(2)

            @pl.when(k == 0)
            def _():
                acc[...] = jnp.zeros((M, tn), F32)

            acc[...] += lax.dot_general(a_ref[...].astype(BF16), b_ref[...].astype(BF16), TN,
                                        preferred_element_type=F32)

            @pl.when(k == n_k - 1)
            def _():
                for q in range(per):
                    o_ref[q] = acc[:, q * s:(q + 1) * s]

        out_spec = pl.BlockSpec((per, M, s), lambda m, n, k: (n, 0, 0))
        out_shape, scratch = _sds((N // s, M, s), F32), [pltpu.VMEM((M, tn), F32)]

    return pl.pallas_call(
        body, name=name, grid=(M // tmm, N // tn, n_k),
        in_specs=[pl.BlockSpec((tk, tmm), lambda m, n, k: (k, m)), pl.BlockSpec((tk, tn), lambda m, n, k: (k, n))],
        out_specs=out_spec, out_shape=out_shape, scratch_shapes=scratch,
        compiler_params=pltpu.CompilerParams(dimension_semantics=("parallel", "parallel", "arbitrary")),
    )(a, b)


def _cols_from_blocks(a, name):
    n, R, s = a.shape
    tr = 256

    def body(i_ref, o_ref):
        for d in range(n):
            o_ref[:, s * d:s * (d + 1)] = i_ref[d]

    return pl.pallas_call(
        body, name=name, grid=(R // tr,),
        in_specs=[pl.BlockSpec((n, tr, s), lambda i: (0, i, 0))], out_specs=pl.BlockSpec((tr, n * s), lambda i: (i, 0)),
        out_shape=_sds((R, n * s), a.dtype),
        compiler_params=pltpu.CompilerParams(dimension_semantics=("parallel",)),
    )(a)


def _mixer_bwd(d2, u, hs, pv, wa, wx, wp, w_out_b, tm, chip_sums=()):
    T = u.shape[0]
    n_t = T // tm
    n_x = len(chip_sums)

    def body(d2_ref, u_ref, uh_ref, hs_ref, hh_ref, pv_ref, wa_ref, wx_ref, wp_ref, wo_ref, *rest):
        x_in, rest = rest[:n_x], rest[n_x:]
        du_ref, vacc_ref, dwa_ref, dwx_ref, dwp_ref = rest[:5]
        x_out, rest = rest[5:5 + n_x], rest[5 + n_x:]
        e_lru, e_pool, e_h, a_s, b_s, mu_s, f_x, f_p, mc, cx, cp = rest[:11]
        exchange = _ChipExchange(x_in, x_out, *rest[11:]) if n_x else None
        s = pl.program_id(0)
        it = n_t - 1 - s

        @pl.when(s == 0)
        def _():
            mc[...] = jnp.zeros((8, LRU_W), F32)
            cx[...] = jnp.zeros((8, LRU_W), F32)
            cp[...] = jnp.zeros((HALO, POOL_W), F32)
            vacc_ref[...] = jnp.zeros((16, LRU_W), F32)
            dwa_ref[...] = jnp.zeros((2, 256, 256), F32)
            dwx_ref[...] = jnp.zeros((2, 256, 256), F32)
            dwp_ref[...] = jnp.zeros((2, 256, 256), F32)
            if exchange:
                exchange.start()

        first = it == 0
        e_lru[pl.ds(0, HALO), :] = jnp.where(first, 0.0, uh_ref[:, 0:LRU_W])
        e_pool[pl.ds(0, HALO), :] = jnp.where(first, 0.0, uh_ref[:, 2 * LRU_W:D_IN])
        e_lru[pl.ds(HALO, tm), :] = u_ref[:, 0:LRU_W]
        e_pool[pl.ds(HALO, tm), :] = u_ref[:, 2 * LRU_W:D_IN]
        e_h[pl.ds(0, 8), :] = jnp.where(first, 0.0, hh_ref[...])
        e_h[pl.ds(8, tm), :] = hs_ref[...]
        pv = pv_ref[...]
        p = _mixer_pre(e_lru, e_pool, pv, wa_ref, wx_ref, wp_ref, tm, it * tm)
        a, xc, ig, r, mult = p["a"], p["xc"], p["ig"], p["r"], p["mult"]

        dyn = lax.dot_general(d2_ref[...].astype(BF16), wo_ref[...], NT, preferred_element_type=F32)

        h = hs_ref[...]
        ug = u_ref[:, LRU_W:2 * LRU_W]
        gl, dgl = _gelu_parts(ug)
        y_lru = h * gl
        rstd_l = _rstd(y_lru)
        dy_lru, d_gain_l = _rms_bwd(dyn[:, 0:LRU_W], y_lru * rstd_l, rstd_l, pv[ROW_GL:ROW_GL + 1, :])
        dh = dy_lru * gl
        du_ref[:, LRU_W:2 * LRU_W] = (dy_lru * h * dgl).astype(BF16)
        a_s[...] = a
        b_s[...] = a * dh
        mu_s[pl.ds(tm, 8), :] = mc[...]
        mc[...] = _scan_tile(a_s, b_s, mu_s, mc[...], tm, reverse=True)
        lam_t = dh + mu_s[pl.ds(1, tm), :]
        da = lam_t * e_h[pl.ds(7, tm), :]
        dmult = lam_t * (ig * xc)
        di = lam_t * (mult * xc)
        dxc = lam_t * (mult * ig)
        dla = da * a + jnp.where(p["om"] > 1e-12, dmult * (-(a * a) / mult), 0.0)
        dra = (dla * (-LRU_C * p["sp"])) * (r * (1.0 - r))
        dia = di * (ig * (1.0 - ig))
        drab = dra.astype(BF16)
        diab = dia.astype(BF16)
        dxc = dxc + _bd_t(drab, wa_ref) + _bd_t(diab, wx_ref)
        dwa_ref[...] += _bd_grad(p["xcb"], drab)
        dwx_ref[...] += _bd_grad(p["xcb"], diab)
        sig_neg_lam = _sigmoid(-pv[ROW_LAM:ROW_LAM + 1, :])
        d_lam = jnp.sum(dla * r, axis=0, keepdims=True) * (LRU_C * sig_neg_lam)

        f_x[pl.ds(0, tm), :] = dxc
        f_x[pl.ds(tm, 8), :] = cx[...]
        du_lru = jnp.zeros((tm, LRU_W), F32)
        d_cw = []
        for k in range(4):
            du_lru = du_lru + f_x[pl.ds(3 - k, tm), :] * pv[ROW_CW + k:ROW_CW + k + 1, :]
            d_cw.append(jnp.sum(dxc * e_lru[pl.ds(HALO - 3 + k, tm), :], axis=0, keepdims=True))
        du_ref[:, 0:LRU_W] = du_lru.astype(BF16)
        cx[...] = f_x[pl.ds(0, 8), :]

        zp = p["zp"]
        ps = pv[ROW_PS:ROW_PS + 1, :]
        y_pool = zp * ps
        rstd_p = _rstd(y_pool)
        dy_pool, d_gain_p = _rms_bwd(dyn[:, LRU_W:D_MODEL], y_pool * rstd_p, rstd_p, pv[ROW_GP:ROW_GP + 1, :])
        dz = dy_pool * ps
        dzb = dz.astype(BF16)
        dwp_ref[...] += _bd_grad(p["pooled_b"], dzb)
        dpooled = _bd_t(dzb, wp_ref)
        for g, w in enumerate(POOL_WINDOWS):
            f_p[pl.ds(0, tm), pl.ds(128 * g, 128)] = dpooled[:, 128 * g:128 * (g + 1)] / p["cnts"][g]
        f_p[pl.ds(tm, HALO), :] = cp[...]
        for g, w in enumerate(POOL_WINDOWS):
            lanes = pl.ds(128 * g, 128)
            acc = f_p[pl.ds(0, tm), lanes]
            for k in range(1, w):
                acc = acc + f_p[pl.ds(k, tm), lanes]
            du_ref[:, 2 * LRU_W + 128 * g:2 * LRU_W + 128 * (g + 1)] = (
                acc - dpooled[:, 128 * g:128 * (g + 1)]).astype(BF16)
        cp[...] = f_p[pl.ds(0, HALO), :]

        rows = d_cw + [
            jnp.sum(dxc, axis=0, keepdims=True),
            jnp.sum(dra, axis=0, keepdims=True),
            jnp.sum(dia, axis=0, keepdims=True),
            d_lam,
            jnp.sum(dz, axis=0, keepdims=True),
            jnp.sum(dy_pool * zp, axis=0, keepdims=True),
            d_gain_l, d_gain_p,
            jnp.zeros((4, LRU_W), F32),
        ]
        vacc_ref[...] += jnp.concatenate(rows, axis=0)

        if exchange:
            @pl.when(s == n_t - 1)
            def _():
                exchange.finish()

    rev = lambda w: pl.BlockSpec((tm, w), lambda s: (n_t - 1 - s, 0))
    full = lambda shape: pl.BlockSpec(shape, lambda s: (0,) * len(shape))
    outs = pl.pallas_call(
        body, name="mixer_bwd", grid=(n_t,),
        in_specs=[rev(D_MODEL), rev(D_IN),
                  pl.BlockSpec((HALO, D_IN), lambda s: (jnp.maximum((n_t - 1 - s) * (tm // HALO) - 1, 0), 0)),
                  rev(LRU_W),
                  pl.BlockSpec((8, LRU_W), lambda s: (jnp.maximum((n_t - 1 - s) * (tm // 8) - 1, 0), 0)),
                  full((16, LRU_W)), full((2, 256, 256)), full((2, 256, 256)), full((2, 256, 256)),
                  full((D_MODEL, D_MODEL))] + [ANY] * n_x,
        out_specs=[rev(D_IN), full((16, LRU_W)), full((2, 256, 256)), full((2, 256, 256)), full((2, 256, 256))]
        + [ANY] * n_x,
        out_shape=[_sds((T, D_IN), BF16), _sds((16, LRU_W), F32), _sds((2, 256, 256), F32),
                   _sds((2, 256, 256), F32), _sds((2, 256, 256), F32)] + [_sds(a.shape, a.dtype) for a in chip_sums],
        scratch_shapes=[pltpu.VMEM((HALO + tm, LRU_W), F32), pltpu.VMEM((HALO + tm, POOL_W), F32),
                        pltpu.VMEM((8 + tm, LRU_W), F32), pltpu.VMEM((tm, LRU_W), F32), pltpu.VMEM((tm, LRU_W), F32),
                        pltpu.VMEM((tm + 8, LRU_W), F32), pltpu.VMEM((tm + 8, LRU_W), F32),
                        pltpu.VMEM((tm + HALO, POOL_W), F32), pltpu.VMEM((8, LRU_W), F32),
                        pltpu.VMEM((8, LRU_W), F32), pltpu.VMEM((HALO, POOL_W), F32)]
        + (_ChipExchange.scratch(n_x) if n_x else []),
        compiler_params=pltpu.CompilerParams(dimension_semantics=("arbitrary",)),
    )(d2, u, u, hs, hs, pv, wa, wx, wp, w_out_b, *chip_sums)
    return tuple(outs[:5]) + (list(outs[5:]),)


def _mix_in_bwd(du, x, d2, w_in_b, g_mix, tm, chip_sums=()):
    T = x.shape[0]
    n_t = T // tm
    n_x = len(chip_sums)

    def body(du_ref, x_ref, d2_ref, w_ref, g_ref, *rest):
        x_in, rest = rest[:n_x], rest[n_x:]
        dx_ref, dg_ref = rest[:2]
        exchange = _ChipExchange(x_in, rest[2:2 + n_x], *rest[2 + n_x:]) if n_x else None
        i = pl.program_id(0)

        @pl.when(i == 0)
        def _():
            dg_ref[...] = jnp.zeros((1, D_MODEL), F32)
            if exchange:
                exchange.start()

        dh = lax.dot_general(du_ref[...], w_ref[...], NT, preferred_element_type=F32)
        xv = x_ref[...]
        rstd = _rstd(xv)
        dx, dgain = _rms_bwd(dh, xv * rstd, rstd, g_ref[...])
        dx_ref[...] = d2_ref[...] + dx
        dg_ref[...] += dgain

        if exchange:
            @pl.when(i == n_t - 1)
            def _():
                exchange.finish()

    row = lambda w: pl.BlockSpec((tm, w), lambda i: (i, 0))
    const = lambda shape: pl.BlockSpec(shape, lambda i: (0,) * len(shape))
    outs = pl.pallas_call(
        body, name="mix_in_bwd", grid=(n_t,),
        in_specs=[row(D_IN), row(D_MODEL), row(D_MODEL), const((D_MODEL, D_IN)), const((1, D_MODEL))] + [ANY] * n_x,
        out_specs=[row(D_MODEL), const((1, D_MODEL))] + [ANY] * n_x,
        out_shape=[_sds((T, D_MODEL), F32), _sds((1, D_MODEL), F32)] + [_sds(a.shape, a.dtype) for a in chip_sums],
        scratch_shapes=_ChipExchange.scratch(n_x) if n_x else [],
        compiler_params=pltpu.CompilerParams(dimension_semantics=("arbitrary",)),
    )(du, x, d2, w_in_b, g_mix, *chip_sums)
    return outs[0], outs[1], list(outs[2:])


def _block_diag(w, per):
    n, k, _ = w.shape
    out = jnp.zeros((n // per, per * k, per * k), w.dtype)
    for b in range(n):
        out = out.at[b // per, (b % per) * k:(b % per + 1) * k, (b % per) * k:(b % per + 1) * k].set(w[b])
    return out


def _diag_blocks(w, per):
    m, pk, _ = w.shape
    k = pk // per
    return jnp.stack([w[b // per, (b % per) * k:(b % per + 1) * k, (b % per) * k:(b % per + 1) * k]
                      for b in range(m * per)], axis=0)


def _pair_sum(g, r1, c_arr, name):
    _, _, R, C = g.shape
    tr = R if R <= 512 else 256

    def body(c_ref, g_ref, r_ref, o_ref):
        o_ref[...] = (g_ref[...] + r_ref[...]).astype(BF16)

    return pl.pallas_call(
        body, name=name,
        grid_spec=pltpu.PrefetchScalarGridSpec(
            num_scalar_prefetch=1, grid=(4, R // tr),
            in_specs=[pl.BlockSpec((None, None, tr, C), lambda j, i, c_ref: (j, c_ref[0], i, 0)),
                      pl.BlockSpec((None, tr, C), lambda j, i, c_ref: (j, i, 0))],
            out_specs=pl.BlockSpec((None, tr, C), lambda j, i, c_ref: (j, i, 0))),
        out_shape=_sds((4, R, C), BF16),
    )(c_arr, g, r1)


def _adamw(w, g, m, v):
    m = ADAM_B1 * m + (1.0 - ADAM_B1) * g
    v = ADAM_B2 * v + (1.0 - ADAM_B2) * (g * g)
    m_hat = m / (1.0 - ADAM_B1 ** ADAM_STEP)
    v_hat = v / (1.0 - ADAM_B2 ** ADAM_STEP)
    delta = -ADAM_LR * (m_hat / (jnp.sqrt(v_hat) + ADAM_EPS) + ADAM_WD * w)
    return delta, m, v


def _adam_shard(w, m, v, parts, name):
    R, C = w.shape
    tr = R if R <= 512 else 256

    def body(w_ref, m_ref, v_ref, p_ref, g_ref, d_ref, nm_ref, nv_ref):
        g = p_ref[0].astype(F32)
        for j in range(1, 4):
            g = g + p_ref[j].astype(F32)
        delta, nm, nv = _adamw(w_ref[...], g, m_ref[...], v_ref[...])
        g_ref[...] = g
        d_ref[...] = delta
        nm_ref[...] = nm
        nv_ref[...] = nv

    blk = pl.BlockSpec((tr, C), lambda i: (i, 0))
    return pl.pallas_call(
        body, name=name, grid=(R // tr,),
        in_specs=[blk, blk, blk, pl.BlockSpec((4, tr, C), lambda i: (0, i, 0))],
        out_specs=[blk] * 4, out_shape=[_sds((R, C), F32)] * 4,
        compiler_params=pltpu.CompilerParams(dimension_semantics=("parallel",)),
    )(w, m, v, parts)


def _adam_small(w, m, v, parts):
    def body(w_ref, m_ref, v_ref, p_ref, g_ref, d_ref, nm_ref, nv_ref):
        g = p_ref[0]
        for j in range(1, N_DEV):
            g = g + p_ref[j]
        delta, nm, nv = _adamw(w_ref[...], g, m_ref[...], v_ref[...])
        g_ref[...] = g
        d_ref[...] = delta
        nm_ref[...] = nm
        nv_ref[...] = nv

    return pl.pallas_call(body, name="adam_small", out_shape=[_sds((SMALL_ROWS, LRU_W), F32)] * 4)(w, m, v, parts)


def _pack_small(vec_rows, g_mix, g_ffn, g_fin, wa, wx, wp):
    return jnp.concatenate(
        [vec_rows, g_mix.reshape(2, LRU_W), g_ffn.reshape(2, LRU_W), g_fin.reshape(2, LRU_W), jnp.zeros((2, LRU_W), F32),
         wa.reshape(64, LRU_W), wx.reshape(64, LRU_W), wp.reshape(128, LRU_W)], axis=0)


def _vec_rows(conv_w_full, conv_b, ba, bx, lam, pb, ps, gl, gp):
    return jnp.concatenate([conv_w_full, conv_b, ba, bx, lam, pb, ps, gl, gp, jnp.zeros((4, LRU_W), F32)], axis=0)


def _unpack_small(p, col0):
    return dict(
        conv_w=lax.dynamic_slice(p[0:4], (0, col0), (4, 64)).reshape(1, 4, 64),
        conv_b=p[4:5], gate_a_b=p[5:6], gate_x_b=p[6:7], lru_lambda=p[7:8], pool_b=p[8:9], pool_scale=p[9:10],
        norm_lru_g=p[10:11], norm_pool_g=p[11:12],
        norm_mix_g=p[16:18].reshape(1, D_MODEL), norm_ffn_g=p[18:20].reshape(1, D_MODEL),
        final_norm_g=p[20:22].reshape(D_MODEL),
        gate_a_w=p[24:88].reshape(1, 8, 64, 64), gate_x_w=p[88:152].reshape(1, 8, 64, 64),
        pool_w=p[152:280].reshape(1, 4, 128, 128))


WEIGHT_ORDER = ['norm_mix_g', 'w_in', 'conv_w', 'conv_b', 'gate_a_w', 'gate_a_b', 'gate_x_w', 'gate_x_b', 'lru_lambda',
                'pool_w', 'pool_b', 'pool_scale', 'norm_lru_g', 'norm_pool_g', 'w_out', 'norm_ffn_g', 'ffn_w1', 'ffn_w3',
                'ffn_w2', 'final_norm_g']


def kernel(x, norm_mix_g, w_in, conv_w, conv_b, gate_a_w, gate_a_b, gate_x_w, gate_x_b, lru_lambda, pool_w, pool_b, pool_scale, norm_lru_g, norm_pool_g, w_out, norm_ffn_g, ffn_w1, ffn_w3, ffn_w2, final_norm_g, loss_target, m_norm_mix_g, m_w_in, m_conv_w, m_conv_b, m_gate_a_w, m_gate_a_b, m_gate_x_w, m_gate_x_b, m_lru_lambda, m_pool_w, m_pool_b, m_pool_scale, m_norm_lru_g, m_norm_pool_g, m_w_out, m_norm_ffn_g, m_ffn_w1, m_ffn_w3, m_ffn_w2, m_final_norm_g, v_norm_mix_g, v_w_in, v_conv_w, v_conv_b, v_gate_a_w, v_gate_a_b, v_gate_x_w, v_gate_x_b, v_lru_lambda, v_pool_w, v_pool_b, v_pool_scale, v_norm_lru_g, v_norm_pool_g, v_w_out, v_norm_ffn_g, v_ffn_w1, v_ffn_w3, v_ffn_w2, v_final_norm_g):
    ax, ay, ac = lax.axis_index("x"), lax.axis_index("y"), lax.axis_index("c")
    dev = 4 * ax + 2 * ay + ac
    col0 = 64 * dev

    tm, tmx, tn, tk = 512, 256, 1408, 512
    xs, tgt = x[0], loss_target[0]
    g_fin = final_norm_g.reshape(1, D_MODEL)
    c_arr = jnp.reshape(ac, (1,)).astype(jnp.int32)

    def pair_sums(blocks, names):
        from_sibling = _pair_exchange(blocks, "grads_to_sibling_" + names[0])
        return [_pair_sum(g.reshape((4, 2) + g.shape[1:]), r, c_arr, "pair_sum_" + nm)
                for g, r, nm in zip(blocks, from_sibling, names)]

    g_in, g_conv = _all_gather([w_in[0].astype(BF16), conv_w[0]], "gather_w_in")
    w_in_b = _cols_from_blocks(g_in, "w_in_cols")
    conv_w_full = g_conv.transpose(1, 0, 2).reshape(4, LRU_W)
    pv = _vec_rows(conv_w_full, conv_b, gate_a_b, gate_x_b, lru_lambda, pool_b, pool_scale, norm_lru_g, norm_pool_g)
    wa_b = _block_diag(gate_a_w[0], 4).astype(BF16)
    wx_b = _block_diag(gate_x_w[0], 4).astype(BF16)
    wp_b = _block_diag(pool_w[0], 2).astype(BF16)

    u, h1 = _mix_in(xs, norm_mix_g, w_in_b, tm)
    y, hs, gathered = _mixer_fwd(u, pv, wa_b, wx_b, wp_b, tmx,
                                 shards=[w_out[0].astype(BF16), ffn_w1[0].astype(BF16), ffn_w3[0].astype(BF16),
                                         ffn_w2[0].astype(BF16)])
    w_out_b = gathered[0].reshape(D_MODEL, D_MODEL)
    w1_b = _cols_from_blocks(gathered[1], "w1_cols")
    w3_b = _cols_from_blocks(gathered[2], "w3_cols")
    w2_b = gathered[3].reshape(D_FF, D_MODEL)
    hres, h2, g, v, d3, loss_acc, d_gfin = _ffn_fwd(xs, y, w_out_b, norm_ffn_g, w1_b, w3_b, w2_b, g_fin, tgt, tm, tn)
    loss = lax.psum(loss_acc[0, 0], ("x", "y", "c"))

    dg, dv, ff, d2, d_gffn = _ffn_bwd(d3, g, v, w1_b, w3_b, w2_b, hres, norm_ffn_g, tm, tn)
    d_w1 = _at_b(h2, dg, "grad_w1", D_MODEL, tn, tk, col_block=352)
    d_w3 = _at_b(h2, dv, "grad_w3", D_MODEL, tn, tk, col_block=352)
    d_w2 = _at_b(ff, d3, "grad_w2", tn, D_MODEL, tk).reshape(N_DEV, 352, D_MODEL)
    ffn_sums = pair_sums([d_w1, d_w3, d_w2], ["w1", "w3", "w2"])
    d_wout = _at_b(y, d2, "grad_w_out", D_MODEL, D_MODEL, tk).reshape(N_DEV, 128, D_MODEL)
    du, vacc, d_wa, d_wx, d_wp, ffn_parts = _mixer_bwd(d2, u, hs, pv, wa_b, wx_b, wp_b, w_out_b, tmx, chip_sums=ffn_sums)
    d_win = _at_b(h1, du, "grad_w_in", D_MODEL, D_IN, tk, col_block=192)
    mix_sums = pair_sums([d_win, d_wout], ["w_in", "w_out"])
    grad_x, d_gmix, mix_parts = _mix_in_bwd(du, xs, d2, w_in_b, norm_mix_g, tm, chip_sums=mix_sums)
    parts = [mix_parts[0], mix_parts[1]] + ffn_parts

    small_local = _pack_small(vacc, d_gmix, d_gffn, d_gfin,
                              _diag_blocks(d_wa, 4), _diag_blocks(d_wx, 4), _diag_blocks(d_wp, 2))
    small_parts = _all_gather([small_local], "gather_small_grads")[0]

    res = {}
    shard_w = dict(w_in=(w_in, m_w_in, v_w_in), w_out=(w_out, m_w_out, v_w_out), ffn_w1=(ffn_w1, m_ffn_w1, v_ffn_w1),
                   ffn_w3=(ffn_w3, m_ffn_w3, v_ffn_w3), ffn_w2=(ffn_w2, m_ffn_w2, v_ffn_w2))
    for (nm, (w, m, v)), p in zip(shard_w.items(), parts):
        outs = _adam_shard(w[0], m[0], v[0], p, "adam_" + nm)
        res[nm] = [o[None] for o in outs]

    def packed(cw, cb, wa, ba, wx, bx, lam, pw, pb, ps, gl, gp, gm, gf, gfin):
        cw_full = lax.dynamic_update_slice(jnp.zeros((4, LRU_W), F32), cw[0], (0, col0))
        return _pack_small(_vec_rows(cw_full, cb, ba, bx, lam, pb, ps, gl, gp), gm, gf, gfin, wa[0], wx[0], pw[0])

    sw = packed(conv_w, conv_b, gate_a_w, gate_a_b, gate_x_w, gate_x_b, lru_lambda, pool_w, pool_b, pool_scale,
                norm_lru_g, norm_pool_g, norm_mix_g, norm_ffn_g, final_norm_g)
    sm = packed(m_conv_w, m_conv_b, m_gate_a_w, m_gate_a_b, m_gate_x_w, m_gate_x_b, m_lru_lambda, m_pool_w, m_pool_b,
                m_pool_scale, m_norm_lru_g, m_norm_pool_g, m_norm_mix_g, m_norm_ffn_g, m_final_norm_g)
    sv = packed(v_conv_w, v_conv_b, v_gate_a_w, v_gate_a_b, v_gate_x_w, v_gate_x_b, v_lru_lambda, v_pool_w, v_pool_b,
                v_pool_scale, v_norm_lru_g, v_norm_pool_g, v_norm_mix_g, v_norm_ffn_g, v_final_norm_g)
    small_out = [_unpack_small(o, col0) for o in _adam_small(sw, sm, sv, small_parts)]
    for nm in small_out[0]:
        res[nm] = [o[nm] for o in small_out]

    out = [loss, grad_x[None]]
    for kind in range(4):
        out += [res[nm][kind] for nm in WEIGHT_ORDER]
    return tuple(out)
```

```python
import jax
import jax.numpy as jnp
from jax import lax
from jax.experimental import pallas as pl
from jax.experimental.pallas import tpu as pltpu

F32 = jnp.float32
BF16 = jnp.bfloat16

D_MODEL = 1024
LRU_W = 512
POOL_W = 512
D_IN = 1536
D_FF = 2816
POOL_WINDOWS = (2, 4, 8, 16)
EPS = 1e-6
LRU_C = 8.0
N_DEV = 8
HALO = 16

ADAM_LR = 0.001
ADAM_B1 = 0.9
ADAM_B2 = 0.999
ADAM_EPS = 1e-08
ADAM_WD = 0.01
ADAM_STEP = 10

ROW_CW, ROW_CB, ROW_BA, ROW_BX, ROW_LAM, ROW_PB, ROW_PS, ROW_GL, ROW_GP = 0, 4, 5, 6, 7, 8, 9, 10, 11
SMALL_ROWS = 280

NT = (((1,), (1,)), ((), ()))
TN = (((0,), (0,)), ((), ()))


def _sds(shape, dtype):
    return jax.ShapeDtypeStruct(shape, dtype)


def _sigmoid(x):
    return 0.5 * jnp.tanh(0.5 * x) + 0.5


def _gelu_parts(x):
    c = 0.7978845608028654
    inner = c * (x + 0.044715 * (x * x * x))
    th = jnp.tanh(inner)
    g = 0.5 * x * (1.0 + th)
    dg = 0.5 * (1.0 + th) + 0.5 * x * (1.0 - th * th) * (c * (1.0 + 3.0 * 0.044715 * (x * x)))
    return g, dg


def _window_sum(ext, w, back):
    n = ext.shape[0]
    s, k = ext, 1
    while k < w:
        s = s + pltpu.roll(s, k if back else n - k, 0)
        k *= 2
    return s


def _rstd(x):
    return lax.rsqrt(jnp.mean(x * x, axis=-1, keepdims=True) + EPS)


def _rms_bwd(dy, xhat, rstd, gain):
    dxh = dy * gain
    dx = rstd * (dxh - xhat * jnp.mean(dxh * xhat, axis=-1, keepdims=True))
    return dx, jnp.sum(dy * xhat, axis=0, keepdims=True)


def _bd(xb, w_ref):
    return jnp.concatenate(
        [jnp.dot(xb[:, :256], w_ref[0], preferred_element_type=F32),
         jnp.dot(xb[:, 256:], w_ref[1], preferred_element_type=F32)], axis=1)


def _bd_t(xb, w_ref):
    return jnp.concatenate(
        [lax.dot_general(xb[:, :256], w_ref[0], NT, preferred_element_type=F32),
         lax.dot_general(xb[:, 256:], w_ref[1], NT, preferred_element_type=F32)], axis=1)


def _bd_grad(xb, db):
    return jnp.stack(
        [lax.dot_general(xb[:, :256], db[:, :256], TN, preferred_element_type=F32),
         lax.dot_general(xb[:, 256:], db[:, 256:], TN, preferred_element_type=F32)], axis=0)


def _mixer_pre(e_lru, e_pool, pv, wa_ref, wx_ref, wp_ref, tm, t0):
    xc = pv[ROW_CB:ROW_CB + 1, :]
    for k in range(4):
        xc = xc + e_lru[pl.ds(HALO - 3 + k, tm), :] * pv[ROW_CW + k:ROW_CW + k + 1, :]
    xcb = xc.astype(BF16)
    r = _sigmoid(_bd(xcb, wa_ref) + pv[ROW_BA:ROW_BA + 1, :])
    ig = _sigmoid(_bd(xcb, wx_ref) + pv[ROW_BX:ROW_BX + 1, :])
    z = -pv[ROW_LAM:ROW_LAM + 1, :]
    sp = jnp.maximum(z, 0.0) + jnp.log(1.0 + jnp.exp(-jnp.abs(z)))
    la = (-LRU_C * r) * sp
    a = jnp.exp(la)
    om = -jnp.tanh(la) * (1.0 + a * a)
    mult = jnp.sqrt(jnp.maximum(om, 1e-12))
    t = t0 + lax.broadcasted_iota(jnp.int32, (tm, 1), 0)
    parts, cnts = [], []
    for g, w in enumerate(POOL_WINDOWS):
        ext = e_pool[:, pl.ds(128 * g, 128)]
        s = _window_sum(ext, w, back=True)[HALO:, :]
        cnt = jnp.minimum(t + 1, w).astype(F32)
        cnts.append(cnt)
        parts.append(s / cnt - ext[HALO:, :])
    pooled = jnp.concatenate(parts, axis=1)
    pooled_b = pooled.astype(BF16)
    zp = _bd(pooled_b, wp_ref) + pv[ROW_PB:ROW_PB + 1, :]
    return dict(xc=xc, xcb=xcb, r=r, ig=ig, sp=sp, a=a, om=om, mult=mult, pooled_b=pooled_b, zp=zp, cnts=cnts)


def _scan_tile(a_ref, b_ref, out_ref, carry, tm, reverse):
    row = lax.broadcasted_iota(jnp.int32, (8, LRU_W), 0)
    nblk = tm // 8

    def step(i, hin):
        blk = (nblk - 1 - i) if reverse else i
        r0 = pl.multiple_of(blk * 8, 8)
        av = a_ref[pl.ds(r0, 8), :]
        bv = b_ref[pl.ds(r0, 8), :]
        for d in (1, 2, 4):
            sh = (8 - d) if reverse else d
            a_s = pltpu.roll(av, sh, 0)
            b_s = pltpu.roll(bv, sh, 0)
            m = (row < 8 - d) if reverse else (row >= d)
            bv = jnp.where(m, av * b_s + bv, bv)
            av = jnp.where(m, av * a_s, av)
        hv = av * hin + bv
        out_ref[pl.ds(r0, 8), :] = hv
        edge = hv[0:1, :] if reverse else hv[7:8, :]
        return jnp.broadcast_to(edge, (8, LRU_W))

    return lax.fori_loop(0, nblk, step, carry)


MESH = pl.DeviceIdType.MESH
ANY = pl.BlockSpec(memory_space=pl.ANY)


def _place():
    x, y, c = lax.axis_index("x"), lax.axis_index("y"), lax.axis_index("c")
    chips = [(1 - x, y), (x, 1 - y), (1 - x, 1 - y)]
    return x, y, c, chips


class _Gather:
    def __init__(self, ins, outs, send_sems, recv_sems, local_sems):
        self.ins, self.outs, self.n = ins, outs, len(ins)
        self.send_sems, self.recv_sems, self.local_sems = send_sems, recv_sems, local_sems

    @staticmethod
    def scratch(n):
        return [pltpu.SemaphoreType.DMA((7, n)), pltpu.SemaphoreType.DMA((7, n)), pltpu.SemaphoreType.DMA((n,))]

    def _slot(self, a, px, py, pc):
        return self.outs[a].at[4 * px + 2 * py + pc]

    def _copy(self, a, k, block, to, src=None):
        return pltpu.make_async_remote_copy(
            src_ref=self._slot(a, *block) if src is None else src, dst_ref=self._slot(a, *block),
            send_sem=self.send_sems.at[k, a], recv_sem=self.recv_sems.at[k, a], device_id=to, device_id_type=MESH)

    def _mine(self, a):
        x, y, c, _ = _place()
        return pltpu.make_async_copy(self.ins[a], self._slot(a, x, y, c), self.local_sems.at[a])

    def _first(self, a):
        x, y, c, chips = _place()
        me = (x, y, c)
        return ([self._copy(a, 0, me, (x, y, 1 - c), src=self.ins[a])]
                + [self._copy(a, 1 + j, me, (*chip, c), src=self.ins[a]) for j, chip in enumerate(chips)])

    def start(self):
        for a in range(self.n):
            self._mine(a).start()
        for a in range(self.n):
            for cp in self._first(a):
                cp.start()

    def finish(self):
        x, y, c, chips = _place()
        me, sibling = (x, y, c), (x, y, 1 - c)
        passed = []
        for j, chip in enumerate(chips):
            for a in range(self.n):
                self._copy(a, 1 + j, (*chip, c), me).wait_recv()
                fwd = self._copy(a, 4 + j, (*chip, c), sibling)
                fwd.start()
                passed.append(fwd)
        for a in range(self.n):
            self._copy(a, 0, (x, y, 1 - c), me).wait_recv()
            for j, chip in enumerate(chips):
                self._copy(a, 4 + j, (*chip, 1 - c), me).wait_recv()
        for a in range(self.n):
            for cp in self._first(a):
                cp.wait_send()
        for cp in passed:
            cp.wait_send()
        for a in range(self.n):
            self._mine(a).wait()


def _all_gather(arrs, name):
    n = len(arrs)

    def body(*refs):
        g = _Gather(refs[:n], refs[n:2 * n], *refs[2 * n:])
        g.start()
        g.finish()

    return pl.pallas_call(
        body, name=name,
        out_shape=[_sds((N_DEV,) + a.shape, a.dtype) for a in arrs],
        in_specs=[ANY] * n, out_specs=[ANY] * n, scratch_shapes=_Gather.scratch(n),
    )(*arrs)


def _pair_exchange(arrs, name):
    n = len(arrs)

    def body(*refs):
        ins, outs = refs[:n], refs[n:2 * n]
        send_sems, recv_sems = refs[2 * n:]
        x, y, c, _ = _place()
        sibling = (x, y, 1 - c)
        sends = []
        for a in range(n):
            for j in range(4):
                cp = pltpu.make_async_remote_copy(
                    src_ref=ins[a].at[2 * j + (1 - c)], dst_ref=outs[a].at[j],
                    send_sem=send_sems.at[j, a], recv_sem=recv_sems.at[j, a], device_id=sibling, device_id_type=MESH)
                cp.start()
                sends.append(cp)
        for cp in sends:
            cp.wait()

    return pl.pallas_call(
        body, name=name,
        out_shape=[_sds((4,) + a.shape[1:], a.dtype) for a in arrs],
        in_specs=[ANY] * n, out_specs=[ANY] * n,
        scratch_shapes=[pltpu.SemaphoreType.DMA((4, n)), pltpu.SemaphoreType.DMA((4, n))],
    )(*arrs)


class _ChipExchange:
    def __init__(self, ins, outs, send_sems, recv_sems, local_sems):
        self.ins, self.outs, self.n = ins, outs, len(ins)
        self.send_sems, self.recv_sems, self.local_sems = send_sems, recv_sems, local_sems

    @staticmethod
    def scratch(n):
        return [pltpu.SemaphoreType.DMA((3, n)), pltpu.SemaphoreType.DMA((3, n)), pltpu.SemaphoreType.DMA((n,))]

    def _local(self, a):
        x, y, _, _ = _place()
        me = 2 * x + y
        return pltpu.make_async_copy(self.ins[a].at[me], self.outs[a].at[me], self.local_sems.at[a])

    def _copies(self, a):
        x, y, c, chips = _place()
        me = 2 * x + y
        return [(pltpu.make_async_remote_copy(
                     src_ref=self.ins[a].at[2 * px + py], dst_ref=self.outs[a].at[me],
                     send_sem=self.send_sems.at[k, a], recv_sem=self.recv_sems.at[k, a],
                     device_id=(px, py, c), device_id_type=MESH),
                 pltpu.make_async_remote_copy(
                     src_ref=self.ins[a].at[me], dst_ref=self.outs[a].at[2 * px + py],
                     send_sem=self.send_sems.at[k, a], recv_sem=self.recv_sems.at[k, a],
                     device_id=(px, py, c), device_id_type=MESH))
                for k, (px, py) in enumerate(chips)]

    def start(self):
        for a in range(self.n):
            self._local(a).start()
        for a in range(self.n):
            for send, _ in self._copies(a):
                send.start()

    def finish(self):
        for a in range(self.n):
            for send, recv in self._copies(a):
                send.wait_send()
                recv.wait_recv()
        for a in range(self.n):
            self._local(a).wait()


def _chip_exchange(arrs, name):
    n = len(arrs)

    def body(*refs):
        e = _ChipExchange(refs[:n], refs[n:2 * n], *refs[2 * n:])
        e.start()
        e.finish()

    return pl.pallas_call(
        body, name=name, out_shape=[_sds(a.shape, a.dtype) for a in arrs],
        in_specs=[ANY] * n, out_specs=[ANY] * n, scratch_shapes=_ChipExchange.scratch(n),
    )(*arrs)


def _mix_in(x, g_mix, w_in_b, tm):
    T = x.shape[0]

    def body(x_ref, g_ref, w_ref, u_ref, h_ref):
        xv = x_ref[...]
        h = (xv * _rstd(xv) * g_ref[...]).astype(BF16)
        h_ref[...] = h
        u_ref[...] = jnp.dot(h, w_ref[...], preferred_element_type=F32)

    return pl.pallas_call(
        body, name="mix_in", grid=(T // tm,),
        in_specs=[pl.BlockSpec((tm, D_MODEL), lambda i: (i, 0)),
                  pl.BlockSpec((1, D_MODEL), lambda i: (0, 0)),
                  pl.BlockSpec((D_MODEL, D_IN), lambda i: (0, 0))],
        out_specs=[pl.BlockSpec((tm, D_IN), lambda i: (i, 0)),
                   pl.BlockSpec((tm, D_MODEL), lambda i: (i, 0))],
        out_shape=[_sds((T, D_IN), F32), _sds((T, D_MODEL), BF16)],
        compiler_params=pltpu.CompilerParams(dimension_semantics=("parallel",)),
    )(x, g_mix, w_in_b)


def _mixer_fwd(u, pv, wa, wx, wp, tm, shards=()):
    T = u.shape[0]
    n_s = len(shards)
    n_t = T // tm

    def body(u_ref, pv_ref, wa_ref, wx_ref, wp_ref, *rest):
        sh_in, rest = rest[:n_s], rest[n_s:]
        y_ref, hs_ref = rest[:2]
        sh_out, rest = rest[2:2 + n_s], rest[2 + n_s:]
        e_lru, e_pool, a_s, b_s, hc = rest[:5]
        gather = _Gather(sh_in, sh_out, *rest[5:]) if n_s else None
        i = pl.program_id(0)

        @pl.when(i == 0)
        def _():
            e_lru[pl.ds(0, HALO), :] = jnp.zeros((HALO, LRU_W), F32)
            e_pool[pl.ds(0, HALO), :] = jnp.zeros((HALO, POOL_W), F32)
            hc[...] = jnp.zeros((8, LRU_W), F32)
            if gather:
                gather.start()

        e_lru[pl.ds(HALO, tm), :] = u_ref[:, 0:LRU_W]
        e_pool[pl.ds(HALO, tm), :] = u_ref[:, 2 * LRU_W:D_IN]
        pv = pv_ref[...]
        p = _mixer_pre(e_lru, e_pool, pv, wa_ref, wx_ref, wp_ref, tm, i * tm)
        a_s[...] = p["a"]
        b_s[...] = p["mult"] * (p["ig"] * p["xc"])
        hc[...] = _scan_tile(a_s, b_s, hs_ref, hc[...], tm, reverse=False)
        gl, _ = _gelu_parts(u_ref[:, LRU_W:2 * LRU_W])
        y_lru = hs_ref[...] * gl
        y_ref[:, 0:LRU_W] = (y_lru * _rstd(y_lru) * pv[ROW_GL:ROW_GL + 1, :]).astype(BF16)
        y_pool = p["zp"] * pv[ROW_PS:ROW_PS + 1, :]
        y_ref[:, LRU_W:D_MODEL] = (y_pool * _rstd(y_pool) * pv[ROW_GP:ROW_GP + 1, :]).astype(BF16)
        e_lru[pl.ds(0, HALO), :] = e_lru[pl.ds(tm, HALO), :]
        e_pool[pl.ds(0, HALO), :] = e_pool[pl.ds(tm, HALO), :]

        if gather:
            @pl.when(i == n_t - 1)
            def _():
                gather.finish()

    full = lambda shape: pl.BlockSpec(shape, lambda i: (0,) * len(shape))
    outs = pl.pallas_call(
        body, name="mixer_fwd", grid=(n_t,),
        in_specs=[pl.BlockSpec((tm, D_IN), lambda i: (i, 0)), full((16, LRU_W)),
                  full((2, 256, 256)), full((2, 256, 256)), full((2, 256, 256))] + [ANY] * n_s,
        out_specs=[pl.BlockSpec((tm, D_MODEL), lambda i: (i, 0)), pl.BlockSpec((tm, LRU_W), lambda i: (i, 0))] + [ANY] * n_s,
        out_shape=[_sds((T, D_MODEL), BF16), _sds((T, LRU_W), F32)] + [_sds((N_DEV,) + a.shape, a.dtype) for a in shards],
        scratch_shapes=[pltpu.VMEM((HALO + tm, LRU_W), F32), pltpu.VMEM((HALO + tm, POOL_W), F32),
                        pltpu.VMEM((tm, LRU_W), F32), pltpu.VMEM((tm, LRU_W), F32), pltpu.VMEM((8, LRU_W), F32)]
        + (_Gather.scratch(n_s) if n_s else []),
        compiler_params=pltpu.CompilerParams(dimension_semantics=("arbitrary",)),
    )(u, pv, wa, wx, wp, *shards)
    return outs[0], outs[1], list(outs[2:])


def _ffn_fwd(x, y, w_out_b, g_ffn, w1_b, w3_b, w2_b, g_fin, tgt, tm, tn):
    T = x.shape[0]
    n_j = D_FF // tn

    def body(x_ref, y_ref, wo_ref, gf_ref, w1_ref, w3_ref, w2_ref, gfin_ref, tgt_ref,
             hres_ref, h2_ref, g_ref, v_ref, d3_ref, loss_ref, dgfin_ref, acc):
        i, j = pl.program_id(0), pl.program_id(1)

        @pl.when(j == 0)
        def _():
            hr = x_ref[...] + jnp.dot(y_ref[...], wo_ref[...], preferred_element_type=F32)
            hres_ref[...] = hr
            h2_ref[...] = (hr * _rstd(hr) * gf_ref[...]).astype(BF16)
            acc[...] = jnp.zeros((tm, D_MODEL), F32)

        @pl.when((j == 0) & (i == 0))
        def _():
            loss_ref[...] = jnp.zeros((8, 128), F32)
            dgfin_ref[...] = jnp.zeros((1, D_MODEL), F32)

        h2 = h2_ref[...]
        g = jnp.dot(h2, w1_ref[...], preferred_element_type=F32)
        v = jnp.dot(h2, w3_ref[...], preferred_element_type=F32)
        g_ref[...] = g.astype(BF16)
        v_ref[...] = v.astype(BF16)
        ff = ((g * _sigmoid(g)) * v).astype(BF16)
        acc[...] += jnp.dot(ff, w2_ref[...], preferred_element_type=F32)

        @pl.when(j == n_j - 1)
        def _():
            h3 = hres_ref[...] + acc[...]
            rstd = _rstd(h3)
            xh = h3 * rstd
            gfin = gfin_ref[...]
            err = xh * gfin - tgt_ref[...]
            loss_ref[...] += 0.5 * jnp.sum(jnp.mean(err * err, axis=-1, keepdims=True))
            dout = err * (1.0 / D_MODEL)
            dx, dgain = _rms_bwd(dout, xh, rstd, gfin)
            d3_ref[...] = dx
            dgfin_ref[...] += dgain

    row = lambda w: pl.BlockSpec((tm, w), lambda i, j: (i, 0))
    const = lambda shape: pl.BlockSpec(shape, lambda i, j: (0,) * len(shape))
    return pl.pallas_call(
        body, name="ffn_fwd", grid=(T // tm, n_j),
        in_specs=[row(D_MODEL), row(D_MODEL), const((D_MODEL, D_MODEL)), const((1, D_MODEL)),
                  pl.BlockSpec((D_MODEL, tn), lambda i, j: (0, j)), pl.BlockSpec((D_MODEL, tn), lambda i, j: (0, j)),
                  pl.BlockSpec((tn, D_MODEL), lambda i, j: (j, 0)), const((1, D_MODEL)), row(D_MODEL)],
        out_specs=[row(D_MODEL), row(D_MODEL),
                   pl.BlockSpec((tm, tn), lambda i, j: (i, j)), pl.BlockSpec((tm, tn), lambda i, j: (i, j)),
                   row(D_MODEL), const((8, 128)), const((1, D_MODEL))],
        out_shape=[_sds((T, D_MODEL), F32), _sds((T, D_MODEL), BF16), _sds((T, D_FF), BF16), _sds((T, D_FF), BF16),
                   _sds((T, D_MODEL), F32), _sds((8, 128), F32), _sds((1, D_MODEL), F32)],
        scratch_shapes=[pltpu.VMEM((tm, D_MODEL), F32)],
        compiler_params=pltpu.CompilerParams(dimension_semantics=("arbitrary", "arbitrary")),
    )(x, y, w_out_b, g_ffn, w1_b, w3_b, w2_b, g_fin, tgt)


def _ffn_bwd(d3, g, v, w1_b, w3_b, w2_b, hres, g_ffn, tm, tn):
    T = d3.shape[0]
    n_j = D_FF // tn

    def body(d3_ref, g_ref, v_ref, w1_ref, w3_ref, w2_ref, hres_ref, gf_ref,
             dg_ref, dv_ref, ff_ref, d2_ref, dgffn_ref, acc):
        i, j = pl.program_id(0), pl.program_id(1)

        @pl.when(j == 0)
        def _():
            acc[...] = jnp.zeros((tm, D_MODEL), F32)

        @pl.when((j == 0) & (i == 0))
        def _():
            dgffn_ref[...] = jnp.zeros((1, D_MODEL), F32)

        dff = lax.dot_general(d3_ref[...].astype(BF16), w2_ref[...], NT, preferred_element_type=F32)
        gv = g_ref[...].astype(F32)
        vv = v_ref[...].astype(F32)
        sg = _sigmoid(gv)
        sl = gv * sg
        dgb = (dff * vv * (sg * (1.0 + gv * (1.0 - sg)))).astype(BF16)
        dvb = (dff * sl).astype(BF16)
        dg_ref[...] = dgb
        dv_ref[...] = dvb
        ff_ref[...] = (sl * vv).astype(BF16)
        acc[...] += (lax.dot_general(dgb, w1_ref[...], NT, preferred_element_type=F32)
                     + lax.dot_general(dvb, w3_ref[...], NT, preferred_element_type=F32))

        @pl.when(j == n_j - 1)
        def _():
            hr = hres_ref[...]
            rstd = _rstd(hr)
            dx, dgain = _rms_bwd(acc[...], hr * rstd, rstd, gf_ref[...])
            d2_ref[...] = d3_ref[...] + dx
            dgffn_ref[...] += dgain

    row = lambda w: pl.BlockSpec((tm, w), lambda i, j: (i, 0))
    tile = pl.BlockSpec((tm, tn), lambda i, j: (i, j))
    const = lambda shape: pl.BlockSpec(shape, lambda i, j: (0,) * len(shape))
    return pl.pallas_call(
        body, name="ffn_bwd", grid=(T // tm, n_j),
        in_specs=[row(D_MODEL), tile, tile,
                  pl.BlockSpec((D_MODEL, tn), lambda i, j: (0, j)), pl.BlockSpec((D_MODEL, tn), lambda i, j: (0, j)),
                  pl.BlockSpec((tn, D_MODEL), lambda i, j: (j, 0)), row(D_MODEL), const((1, D_MODEL))],
        out_specs=[tile, tile, tile, row(D_MODEL), const((1, D_MODEL))],
        out_shape=[_sds((T, D_FF), BF16), _sds((T, D_FF), BF16), _sds((T, D_FF), BF16),
                   _sds((T, D_MODEL), F32), _sds((1, D_MODEL), F32)],
        scratch_shapes=[pltpu.VMEM((tm, D_MODEL), F32)],
        compiler_params=pltpu.CompilerParams(dimension_semantics=("arbitrary", "arbitrary")),
    )(d3, g, v, w1_b, w3_b, w2_b, hres, g_ffn)


def _at_b(a, b, name, tmm, tn, tk, col_block=None, gather=()):
    T, M = a.shape
    N = b.shape[1]
    n_m, n_n, n_k = M // tmm, N // tn, T // tk
    n_g = len(gather)

    def body(a_ref, b_ref, *rest):
        g_in, o_ref, rest = rest[:n_g], rest[n_g], rest[n_g + 1:]
        g_out, rest = rest[:n_g], rest[n_g:]
        m, n, k = pl.program_id(0), pl.program_id(1), pl.program_id(2)
        prod = lambda: lax.dot_general(a_ref[...].astype(BF16), b_ref[...].astype(BF16), TN,
                                       preferred_element_type=F32)
        if col_block is None:
            ag = _Gather(g_in, g_out, *rest) if n_g else None
        else:
            acc = rest[0]
            ag = _Gather(g_in, g_out, *rest[1:]) if n_g else None

        if ag:
            @pl.when((m == 0) & (n == 0) & (k == 0))
            def _():
                ag.start()

        if col_block is None:
            @pl.when(k == 0)
            def _():
                o_ref[...] = jnp.zeros((tmm, tn), F32)

            o_ref[...] += prod()
        else:
            @pl.when(k == 0)
            def _():
                acc[...] = jnp.zeros((M, tn), F32)

            acc[...] += prod()

            @pl.when(k == n_k - 1)
            def _():
                for q in range(tn // col_block):
                    o_ref[q] = acc[:, q * col_block:(q + 1) * col_block]

        if ag:
            @pl.when((m == n_m - 1) & (n == n_n - 1) & (k == n_k - 1))
            def _():
                ag.finish()

    if col_block is None:
        out_spec = pl.BlockSpec((tmm, tn), lambda m, n, k: (m, n))
        out_shape, scratch = _sds((M, N), F32), []
    else:
        out_spec = pl.BlockSpec((tn // col_block, M, col_block), lambda m, n, k: (n, 0, 0))
        out_shape, scratch = _sds((N // col_block, M, col_block), F32), [pltpu.VMEM((M, tn), F32)]

    outs = pl.pallas_call(
        body, name=name, grid=(n_m, n_n, n_k),
        in_specs=[pl.BlockSpec((tk, tmm), lambda m, n, k: (k, m)), pl.BlockSpec((tk, tn), lambda m, n, k: (k, n))]
        + [ANY] * n_g,
        out_specs=[out_spec] + [ANY] * n_g,
        out_shape=[out_shape] + [_sds((N_DEV,) + g.shape, g.dtype) for g in gather],
        scratch_shapes=scratch + (_Gather.scratch(n_g) if n_g else []),
        compiler_params=pltpu.CompilerParams(
            dimension_semantics=("arbitrary",) * 3 if n_g else ("parallel", "parallel", "arbitrary")),
    )(a, b, *gather)
    return (outs[0], list(outs[1:])) if n_g else outs[0]


def _cols_from_blocks(a, name):
    n, R, s = a.shape
    tr = 256

    def body(i_ref, o_ref):
        for d in range(n):
            o_ref[:, s * d:s * (d + 1)] = i_ref[d]

    return pl.pallas_call(
        body, name=name, grid=(R // tr,),
        in_specs=[pl.BlockSpec((n, tr, s), lambda i: (0, i, 0))], out_specs=pl.BlockSpec((tr, n * s), lambda i: (i, 0)),
        out_shape=_sds((R, n * s), a.dtype),
        compiler_params=pltpu.CompilerParams(dimension_semantics=("parallel",)),
    )(a)


def _mixer_bwd(d2, u, hs, pv, wa, wx, wp, w_out_b, tm, chip_sums=()):
    T = u.shape[0]
    n_t = T // tm
    n_x = len(chip_sums)

    def body(d2_ref, u_ref, uh_ref, hs_ref, hh_ref, pv_ref, wa_ref, wx_ref, wp_ref, wo_ref, *rest):
        x_in, rest = rest[:n_x], rest[n_x:]
        du_ref, vacc_ref, dwa_ref, dwx_ref, dwp_ref = rest[:5]
        x_out, rest = rest[5:5 + n_x], rest[5 + n_x:]
        e_lru, e_pool, e_h, a_s, b_s, mu_s, f_x, f_p, mc, cx, cp = rest[:11]
        exchange = _ChipExchange(x_in, x_out, *rest[11:]) if n_x else None
        s = pl.program_id(0)
        it = n_t - 1 - s

        @pl.when(s == 0)
        def _():
            mc[...] = jnp.zeros((8, LRU_W), F32)
            cx[...] = jnp.zeros((8, LRU_W), F32)
            cp[...] = jnp.zeros((HALO, POOL_W), F32)
            vacc_ref[...] = jnp.zeros((16, LRU_W), F32)
            dwa_ref[...] = jnp.zeros((2, 256, 256), F32)
            dwx_ref[...] = jnp.zeros((2, 256, 256), F32)
            dwp_ref[...] = jnp.zeros((2, 256, 256), F32)
            if exchange:
                exchange.start()

        first = it == 0
        e_lru[pl.ds(0, HALO), :] = jnp.where(first, 0.0, uh_ref[:, 0:LRU_W])
        e_pool[pl.ds(0, HALO), :] = jnp.where(first, 0.0, uh_ref[:, 2 * LRU_W:D_IN])
        e_lru[pl.ds(HALO, tm), :] = u_ref[:, 0:LRU_W]
        e_pool[pl.ds(HALO, tm), :] = u_ref[:, 2 * LRU_W:D_IN]
        e_h[pl.ds(0, 8), :] = jnp.where(first, 0.0, hh_ref[...])
        e_h[pl.ds(8, tm), :] = hs_ref[...]
        pv = pv_ref[...]
        p = _mixer_pre(e_lru, e_pool, pv, wa_ref, wx_ref, wp_ref, tm, it * tm)
        a, xc, ig, r, mult = p["a"], p["xc"], p["ig"], p["r"], p["mult"]

        dyn = lax.dot_general(d2_ref[...].astype(BF16), wo_ref[...], NT, preferred_element_type=F32)

        h = hs_ref[...]
        ug = u_ref[:, LRU_W:2 * LRU_W]
        gl, dgl = _gelu_parts(ug)
        y_lru = h * gl
        rstd_l = _rstd(y_lru)
        dy_lru, d_gain_l = _rms_bwd(dyn[:, 0:LRU_W], y_lru * rstd_l, rstd_l, pv[ROW_GL:ROW_GL + 1, :])
        dh = dy_lru * gl
        du_ref[:, LRU_W:2 * LRU_W] = (dy_lru * h * dgl).astype(BF16)
        a_s[...] = a
        b_s[...] = a * dh
        mu_s[pl.ds(tm, 8), :] = mc[...]
        mc[...] = _scan_tile(a_s, b_s, mu_s, mc[...], tm, reverse=True)
        lam_t = dh + mu_s[pl.ds(1, tm), :]
        da = lam_t * e_h[pl.ds(7, tm), :]
        dmult = lam_t * (ig * xc)
        di = lam_t * (mult * xc)
        dxc = lam_t * (mult * ig)
        dla = da * a + jnp.where(p["om"] > 1e-12, dmult * (-(a * a) / mult), 0.0)
        dra = (dla * (-LRU_C * p["sp"])) * (r * (1.0 - r))
        dia = di * (ig * (1.0 - ig))
        drab = dra.astype(BF16)
        diab = dia.astype(BF16)
        dxc = dxc + _bd_t(drab, wa_ref) + _bd_t(diab, wx_ref)
        dwa_ref[...] += _bd_grad(p["xcb"], drab)
        dwx_ref[...] += _bd_grad(p["xcb"], diab)
        sig_neg_lam = _sigmoid(-pv[ROW_LAM:ROW_LAM + 1, :])
        d_lam = jnp.sum(dla * r, axis=0, keepdims=True) * (LRU_C * sig_neg_lam)

        f_x[pl.ds(0, tm), :] = dxc
        f_x[pl.ds(tm, 8), :] = cx[...]
        du_lru = jnp.zeros((tm, LRU_W), F32)
        d_cw = []
        for k in range(4):
            du_lru = du_lru + f_x[pl.ds(3 - k, tm), :] * pv[ROW_CW + k:ROW_CW + k + 1, :]
            d_cw.append(jnp.sum(dxc * e_lru[pl.ds(HALO - 3 + k, tm), :], axis=0, keepdims=True))
        du_ref[:, 0:LRU_W] = du_lru.astype(BF16)
        cx[...] = f_x[pl.ds(0, 8), :]

        zp = p["zp"]
        ps = pv[ROW_PS:ROW_PS + 1, :]
        y_pool = zp * ps
        rstd_p = _rstd(y_pool)
        dy_pool, d_gain_p = _rms_bwd(dyn[:, LRU_W:D_MODEL], y_pool * rstd_p, rstd_p, pv[ROW_GP:ROW_GP + 1, :])
        dz = dy_pool * ps
        dzb = dz.astype(BF16)
        dwp_ref[...] += _bd_grad(p["pooled_b"], dzb)
        dpooled = _bd_t(dzb, wp_ref)
        for g, w in enumerate(POOL_WINDOWS):
            f_p[pl.ds(0, tm), pl.ds(128 * g, 128)] = dpooled[:, 128 * g:128 * (g + 1)] / p["cnts"][g]
        f_p[pl.ds(tm, HALO), :] = cp[...]
        for g, w in enumerate(POOL_WINDOWS):
            acc = _window_sum(f_p[:, pl.ds(128 * g, 128)], w, back=False)[0:tm, :]
            du_ref[:, 2 * LRU_W + 128 * g:2 * LRU_W + 128 * (g + 1)] = (
                acc - dpooled[:, 128 * g:128 * (g + 1)]).astype(BF16)
        cp[...] = f_p[pl.ds(0, HALO), :]

        rows = d_cw + [
            jnp.sum(dxc, axis=0, keepdims=True),
            jnp.sum(dra, axis=0, keepdims=True),
            jnp.sum(dia, axis=0, keepdims=True),
            d_lam,
            jnp.sum(dz, axis=0, keepdims=True),
            jnp.sum(dy_pool * zp, axis=0, keepdims=True),
            d_gain_l, d_gain_p,
            jnp.zeros((4, LRU_W), F32),
        ]
        vacc_ref[...] += jnp.concatenate(rows, axis=0)

        if exchange:
            @pl.when(s == n_t - 1)
            def _():
                exchange.finish()

    rev = lambda w: pl.BlockSpec((tm, w), lambda s: (n_t - 1 - s, 0))
    full = lambda shape: pl.BlockSpec(shape, lambda s: (0,) * len(shape))
    outs = pl.pallas_call(
        body, name="mixer_bwd", grid=(n_t,),
        in_specs=[rev(D_MODEL), rev(D_IN),
                  pl.BlockSpec((HALO, D_IN), lambda s: (jnp.maximum((n_t - 1 - s) * (tm // HALO) - 1, 0), 0)),
                  rev(LRU_W),
                  pl.BlockSpec((8, LRU_W), lambda s: (jnp.maximum((n_t - 1 - s) * (tm // 8) - 1, 0), 0)),
                  full((16, LRU_W)), full((2, 256, 256)), full((2, 256, 256)), full((2, 256, 256)),
                  full((D_MODEL, D_MODEL))] + [ANY] * n_x,
        out_specs=[rev(D_IN), full((16, LRU_W)), full((2, 256, 256)), full((2, 256, 256)), full((2, 256, 256))]
        + [ANY] * n_x,
        out_shape=[_sds((T, D_IN), BF16), _sds((16, LRU_W), F32), _sds((2, 256, 256), F32),
                   _sds((2, 256, 256), F32), _sds((2, 256, 256), F32)] + [_sds(a.shape, a.dtype) for a in chip_sums],
        scratch_shapes=[pltpu.VMEM((HALO + tm, LRU_W), F32), pltpu.VMEM((HALO + tm, POOL_W), F32),
                        pltpu.VMEM((8 + tm, LRU_W), F32), pltpu.VMEM((tm, LRU_W), F32), pltpu.VMEM((tm, LRU_W), F32),
                        pltpu.VMEM((tm + 8, LRU_W), F32), pltpu.VMEM((tm + 8, LRU_W), F32),
                        pltpu.VMEM((tm + HALO, POOL_W), F32), pltpu.VMEM((8, LRU_W), F32),
                        pltpu.VMEM((8, LRU_W), F32), pltpu.VMEM((HALO, POOL_W), F32)]
        + (_ChipExchange.scratch(n_x) if n_x else []),
        compiler_params=pltpu.CompilerParams(dimension_semantics=("arbitrary",)),
    )(d2, u, u, hs, hs, pv, wa, wx, wp, w_out_b, *chip_sums)
    return tuple(outs[:5]) + (list(outs[5:]),)


def _mix_in_bwd(du, x, d2, w_in_b, g_mix, tm, chip_sums=()):
    T = x.shape[0]
    n_t = T // tm
    n_x = len(chip_sums)

    def body(du_ref, x_ref, d2_ref, w_ref, g_ref, *rest):
        x_in, rest = rest[:n_x], rest[n_x:]
        dx_ref, dg_ref = rest[:2]
        exchange = _ChipExchange(x_in, rest[2:2 + n_x], *rest[2 + n_x:]) if n_x else None
        i = pl.program_id(0)

        @pl.when(i == 0)
        def _():
            dg_ref[...] = jnp.zeros((1, D_MODEL), F32)
            if exchange:
                exchange.start()

        dh = lax.dot_general(du_ref[...], w_ref[...], NT, preferred_element_type=F32)
        xv = x_ref[...]
        rstd = _rstd(xv)
        dx, dgain = _rms_bwd(dh, xv * rstd, rstd, g_ref[...])
        dx_ref[...] = d2_ref[...] + dx
        dg_ref[...] += dgain

        if exchange:
            @pl.when(i == n_t - 1)
            def _():
                exchange.finish()

    row = lambda w: pl.BlockSpec((tm, w), lambda i: (i, 0))
    const = lambda shape: pl.BlockSpec(shape, lambda i: (0,) * len(shape))
    outs = pl.pallas_call(
        body, name="mix_in_bwd", grid=(n_t,),
        in_specs=[row(D_IN), row(D_MODEL), row(D_MODEL), const((D_MODEL, D_IN)), const((1, D_MODEL))] + [ANY] * n_x,
        out_specs=[row(D_MODEL), const((1, D_MODEL))] + [ANY] * n_x,
        out_shape=[_sds((T, D_MODEL), F32), _sds((1, D_MODEL), F32)] + [_sds(a.shape, a.dtype) for a in chip_sums],
        scratch_shapes=_ChipExchange.scratch(n_x) if n_x else [],
        compiler_params=pltpu.CompilerParams(dimension_semantics=("arbitrary",)),
    )(du, x, d2, w_in_b, g_mix, *chip_sums)
    return outs[0], outs[1], list(outs[2:])


def _block_diag(w, per):
    n, k, _ = w.shape
    out = jnp.zeros((n // per, per * k, per * k), w.dtype)
    for b in range(n):
        out = out.at[b // per, (b % per) * k:(b % per + 1) * k, (b % per) * k:(b % per + 1) * k].set(w[b])
    return out


def _diag_blocks(w, per):
    m, pk, _ = w.shape
    k = pk // per
    return jnp.stack([w[b // per, (b % per) * k:(b % per + 1) * k, (b % per) * k:(b % per + 1) * k]
                      for b in range(m * per)], axis=0)


def _pair_sum(g, r1, c_arr, name):
    _, _, R, C = g.shape
    tr = R if R <= 512 else 256

    def body(c_ref, g_ref, r_ref, o_ref):
        o_ref[...] = (g_ref[...] + r_ref[...]).astype(BF16)

    return pl.pallas_call(
        body, name=name,
        grid_spec=pltpu.PrefetchScalarGridSpec(
            num_scalar_prefetch=1, grid=(4, R // tr),
            in_specs=[pl.BlockSpec((None, None, tr, C), lambda j, i, c_ref: (j, c_ref[0], i, 0)),
                      pl.BlockSpec((None, tr, C), lambda j, i, c_ref: (j, i, 0))],
            out_specs=pl.BlockSpec((None, tr, C), lambda j, i, c_ref: (j, i, 0))),
        out_shape=_sds((4, R, C), BF16),
    )(c_arr, g, r1)


def _adamw(w, g, m, v):
    m = ADAM_B1 * m + (1.0 - ADAM_B1) * g
    v = ADAM_B2 * v + (1.0 - ADAM_B2) * (g * g)
    m_hat = m / (1.0 - ADAM_B1 ** ADAM_STEP)
    v_hat = v / (1.0 - ADAM_B2 ** ADAM_STEP)
    delta = -ADAM_LR * (m_hat / (jnp.sqrt(v_hat) + ADAM_EPS) + ADAM_WD * w)
    return delta, m, v


def _adam_shard(w, m, v, parts, name):
    R, C = w.shape
    tr = R if R <= 512 else 256

    def body(w_ref, m_ref, v_ref, p_ref, g_ref, d_ref, nm_ref, nv_ref):
        g = p_ref[0].astype(F32)
        for j in range(1, 4):
            g = g + p_ref[j].astype(F32)
        delta, nm, nv = _adamw(w_ref[...], g, m_ref[...], v_ref[...])
        g_ref[...] = g
        d_ref[...] = delta
        nm_ref[...] = nm
        nv_ref[...] = nv

    blk = pl.BlockSpec((tr, C), lambda i: (i, 0))
    return pl.pallas_call(
        body, name=name, grid=(R // tr,),
        in_specs=[blk, blk, blk, pl.BlockSpec((4, tr, C), lambda i: (0, i, 0))],
        out_specs=[blk] * 4, out_shape=[_sds((R, C), F32)] * 4,
        compiler_params=pltpu.CompilerParams(dimension_semantics=("parallel",)),
    )(w, m, v, parts)


def _adam_small(w, m, v, parts):
    def body(w_ref, m_ref, v_ref, p_ref, g_ref, d_ref, nm_ref, nv_ref):
        g = p_ref[0]
        for j in range(1, N_DEV):
            g = g + p_ref[j]
        delta, nm, nv = _adamw(w_ref[...], g, m_ref[...], v_ref[...])
        g_ref[...] = g
        d_ref[...] = delta
        nm_ref[...] = nm
        nv_ref[...] = nv

    return pl.pallas_call(body, name="adam_small", out_shape=[_sds((SMALL_ROWS, LRU_W), F32)] * 4)(w, m, v, parts)


ROW_SPARE = 22


def _pack_small(vec_rows, g_mix, g_ffn, g_fin, spare, wa, wx, wp):
    return jnp.concatenate(
        [vec_rows, g_mix.reshape(2, LRU_W), g_ffn.reshape(2, LRU_W), g_fin.reshape(2, LRU_W), spare.reshape(2, LRU_W),
         wa.reshape(64, LRU_W), wx.reshape(64, LRU_W), wp.reshape(128, LRU_W)], axis=0)


def _vec_rows(conv_w_full, conv_b, ba, bx, lam, pb, ps, gl, gp):
    return jnp.concatenate([conv_w_full, conv_b, ba, bx, lam, pb, ps, gl, gp, jnp.zeros((4, LRU_W), F32)], axis=0)


def _unpack_small(p, col0):
    return dict(
        conv_w=lax.dynamic_slice(p[0:4], (0, col0), (4, 64)).reshape(1, 4, 64),
        conv_b=p[4:5], gate_a_b=p[5:6], gate_x_b=p[6:7], lru_lambda=p[7:8], pool_b=p[8:9], pool_scale=p[9:10],
        norm_lru_g=p[10:11], norm_pool_g=p[11:12],
        norm_mix_g=p[16:18].reshape(1, D_MODEL), norm_ffn_g=p[18:20].reshape(1, D_MODEL),
        final_norm_g=p[20:22].reshape(D_MODEL),
        gate_a_w=p[24:88].reshape(1, 8, 64, 64), gate_x_w=p[88:152].reshape(1, 8, 64, 64),
        pool_w=p[152:280].reshape(1, 4, 128, 128))


WEIGHT_ORDER = ['norm_mix_g', 'w_in', 'conv_w', 'conv_b', 'gate_a_w', 'gate_a_b', 'gate_x_w', 'gate_x_b', 'lru_lambda',
                'pool_w', 'pool_b', 'pool_scale', 'norm_lru_g', 'norm_pool_g', 'w_out', 'norm_ffn_g', 'ffn_w1', 'ffn_w3',
                'ffn_w2', 'final_norm_g']


def kernel(x, norm_mix_g, w_in, conv_w, conv_b, gate_a_w, gate_a_b, gate_x_w, gate_x_b, lru_lambda, pool_w, pool_b, pool_scale, norm_lru_g, norm_pool_g, w_out, norm_ffn_g, ffn_w1, ffn_w3, ffn_w2, final_norm_g, loss_target, m_norm_mix_g, m_w_in, m_conv_w, m_conv_b, m_gate_a_w, m_gate_a_b, m_gate_x_w, m_gate_x_b, m_lru_lambda, m_pool_w, m_pool_b, m_pool_scale, m_norm_lru_g, m_norm_pool_g, m_w_out, m_norm_ffn_g, m_ffn_w1, m_ffn_w3, m_ffn_w2, m_final_norm_g, v_norm_mix_g, v_w_in, v_conv_w, v_conv_b, v_gate_a_w, v_gate_a_b, v_gate_x_w, v_gate_x_b, v_lru_lambda, v_pool_w, v_pool_b, v_pool_scale, v_norm_lru_g, v_norm_pool_g, v_w_out, v_norm_ffn_g, v_ffn_w1, v_ffn_w3, v_ffn_w2, v_final_norm_g):
    ax, ay, ac = lax.axis_index("x"), lax.axis_index("y"), lax.axis_index("c")
    dev = 4 * ax + 2 * ay + ac
    col0 = 64 * dev

    tm, tmx, tn, tk = 512, 256, 1408, 512
    xs, tgt = x[0], loss_target[0]
    g_fin = final_norm_g.reshape(1, D_MODEL)
    c_arr = jnp.reshape(ac, (1,)).astype(jnp.int32)

    def pair_sums(blocks, names):
        from_sibling = _pair_exchange(blocks, "grads_to_sibling_" + names[0])
        return [_pair_sum(g.reshape((4, 2) + g.shape[1:]), r, c_arr, "pair_sum_" + nm)
                for g, r, nm in zip(blocks, from_sibling, names)]

    g_in, g_conv = _all_gather([w_in[0].astype(BF16), conv_w[0]], "gather_w_in")
    w_in_b = _cols_from_blocks(g_in, "w_in_cols")
    conv_w_full = g_conv.transpose(1, 0, 2).reshape(4, LRU_W)
    pv = _vec_rows(conv_w_full, conv_b, gate_a_b, gate_x_b, lru_lambda, pool_b, pool_scale, norm_lru_g, norm_pool_g)
    wa_b = _block_diag(gate_a_w[0], 4).astype(BF16)
    wx_b = _block_diag(gate_x_w[0], 4).astype(BF16)
    wp_b = _block_diag(pool_w[0], 2).astype(BF16)

    u, h1 = _mix_in(xs, norm_mix_g, w_in_b, tm)
    y, hs, gathered = _mixer_fwd(u, pv, wa_b, wx_b, wp_b, tmx,
                                 shards=[w_out[0].astype(BF16), ffn_w1[0].astype(BF16), ffn_w3[0].astype(BF16),
                                         ffn_w2[0].astype(BF16)])
    w_out_b = gathered[0].reshape(D_MODEL, D_MODEL)
    w1_b = _cols_from_blocks(gathered[1], "w1_cols")
    w3_b = _cols_from_blocks(gathered[2], "w3_cols")
    w2_b = gathered[3].reshape(D_FF, D_MODEL)
    hres, h2, g, v, d3, loss_acc, d_gfin = _ffn_fwd(xs, y, w_out_b, norm_ffn_g, w1_b, w3_b, w2_b, g_fin, tgt, tm, tn)

    dg, dv, ff, d2, d_gffn = _ffn_bwd(d3, g, v, w1_b, w3_b, w2_b, hres, norm_ffn_g, tm, tn)
    d_w1 = _at_b(h2, dg, "grad_w1", D_MODEL, tn, tk, col_block=352)
    d_w3 = _at_b(h2, dv, "grad_w3", D_MODEL, tn, tk, col_block=352)
    d_w2 = _at_b(ff, d3, "grad_w2", tn, D_MODEL, tk).reshape(N_DEV, 352, D_MODEL)
    d_wout = _at_b(y, d2, "grad_w_out", D_MODEL, D_MODEL, tk).reshape(N_DEV, 128, D_MODEL)
    early_sums = pair_sums([d_wout, d_w1, d_w3, d_w2], ["w_out", "w1", "w3", "w2"])
    du, vacc, d_wa, d_wx, d_wp, early_parts = _mixer_bwd(d2, u, hs, pv, wa_b, wx_b, wp_b, w_out_b, tmx,
                                                         chip_sums=early_sums)
    grad_x, d_gmix, _ = _mix_in_bwd(du, xs, d2, w_in_b, norm_mix_g, tm)
    small_local = _pack_small(vacc, d_gmix, d_gffn, d_gfin, loss_acc,
                              _diag_blocks(d_wa, 4), _diag_blocks(d_wx, 4), _diag_blocks(d_wp, 2))
    d_win, (small_parts,) = _at_b(h1, du, "grad_w_in", D_MODEL, D_IN, tk, col_block=192, gather=[small_local])
    win_parts = _chip_exchange(pair_sums([d_win], ["w_in"]), "grads_to_chips_w_in")
    parts = list(win_parts) + list(early_parts)

    res = {}
    shard_w = dict(w_in=(w_in, m_w_in, v_w_in), w_out=(w_out, m_w_out, v_w_out), ffn_w1=(ffn_w1, m_ffn_w1, v_ffn_w1),
                   ffn_w3=(ffn_w3, m_ffn_w3, v_ffn_w3), ffn_w2=(ffn_w2, m_ffn_w2, v_ffn_w2))
    for (nm, (w, m, v)), p in zip(shard_w.items(), parts):
        outs = _adam_shard(w[0], m[0], v[0], p, "adam_" + nm)
        res[nm] = [o[None] for o in outs]

    def packed(cw, cb, wa, ba, wx, bx, lam, pw, pb, ps, gl, gp, gm, gf, gfin):
        cw_full = lax.dynamic_update_slice(jnp.zeros((4, LRU_W), F32), cw[0], (0, col0))
        return _pack_small(_vec_rows(cw_full, cb, ba, bx, lam, pb, ps, gl, gp), gm, gf, gfin, jnp.zeros((8, 128), F32),
                           wa[0], wx[0], pw[0])

    sw = packed(conv_w, conv_b, gate_a_w, gate_a_b, gate_x_w, gate_x_b, lru_lambda, pool_w, pool_b, pool_scale,
                norm_lru_g, norm_pool_g, norm_mix_g, norm_ffn_g, final_norm_g)
    sm = packed(m_conv_w, m_conv_b, m_gate_a_w, m_gate_a_b, m_gate_x_w, m_gate_x_b, m_lru_lambda, m_pool_w, m_pool_b,
                m_pool_scale, m_norm_lru_g, m_norm_pool_g, m_norm_mix_g, m_norm_ffn_g, m_final_norm_g)
    sv = packed(v_conv_w, v_conv_b, v_gate_a_w, v_gate_a_b, v_gate_x_w, v_gate_x_b, v_lru_lambda, v_pool_w, v_pool_b,
                v_pool_scale, v_norm_lru_g, v_norm_pool_g, v_norm_mix_g, v_norm_ffn_g, v_final_norm_g)
    small_res = _adam_small(sw, sm, sv, small_parts)
    small_out = [_unpack_small(o, col0) for o in small_res]
    for nm in small_out[0]:
        res[nm] = [o[nm] for o in small_out]
    loss = small_res[0][ROW_SPARE, 0]

    out = [loss, grad_x[None]]
    for kind in range(4):
        out += [res[nm][kind] for nm in WEIGHT_ORDER]
    return tuple(out)
```

```python
import jax
import jax.numpy as jnp
from jax import lax
from jax.experimental import pallas as pl
from jax.experimental.pallas import tpu as pltpu

F32 = jnp.float32
BF16 = jnp.bfloat16

D_MODEL = 1024
LRU_W = 512
POOL_W = 512
D_IN = 1536
D_FF = 2816
POOL_WINDOWS = (2, 4, 8, 16)
EPS = 1e-6
LRU_C = 8.0
N_DEV = 8
HALO = 16

ADAM_LR = 0.001
ADAM_B1 = 0.9
ADAM_B2 = 0.999
ADAM_EPS = 1e-08
ADAM_WD = 0.01
ADAM_STEP = 10

ROW_CW, ROW_CB, ROW_BA, ROW_BX, ROW_LAM, ROW_PB, ROW_PS, ROW_GL, ROW_GP = 0, 4, 5, 6, 7, 8, 9, 10, 11
SMALL_ROWS = 280

NT = (((1,), (1,)), ((), ()))
TN = (((0,), (0,)), ((), ()))


def _sds(shape, dtype):
    return jax.ShapeDtypeStruct(shape, dtype)


def _sigmoid(x):
    return 0.5 * jnp.tanh(0.5 * x) + 0.5


def _gelu_parts(x):
    c = 0.7978845608028654
    inner = c * (x + 0.044715 * (x * x * x))
    th = jnp.tanh(inner)
    g = 0.5 * x * (1.0 + th)
    dg = 0.5 * (1.0 + th) + 0.5 * x * (1.0 - th * th) * (c * (1.0 + 3.0 * 0.044715 * (x * x)))
    return g, dg


def _window_sum(ext, w, back):
    n = ext.shape[0]
    s, k = ext, 1
    while k < w:
        s = s + pltpu.roll(s, k if back else n - k, 0)
        k *= 2
    return s


def _rstd(x):
    return lax.rsqrt(jnp.mean(x * x, axis=-1, keepdims=True) + EPS)


def _rms_bwd(dy, xhat, rstd, gain):
    dxh = dy * gain
    dx = rstd * (dxh - xhat * jnp.mean(dxh * xhat, axis=-1, keepdims=True))
    return dx, jnp.sum(dy * xhat, axis=0, keepdims=True)


def _bd(xb, w_ref):
    return jnp.concatenate(
        [jnp.dot(xb[:, :256], w_ref[0], preferred_element_type=F32),
         jnp.dot(xb[:, 256:], w_ref[1], preferred_element_type=F32)], axis=1)


def _bd_t(xb, w_ref):
    return jnp.concatenate(
        [lax.dot_general(xb[:, :256], w_ref[0], NT, preferred_element_type=F32),
         lax.dot_general(xb[:, 256:], w_ref[1], NT, preferred_element_type=F32)], axis=1)


def _bd_grad(xb, db):
    return jnp.stack(
        [lax.dot_general(xb[:, :256], db[:, :256], TN, preferred_element_type=F32),
         lax.dot_general(xb[:, 256:], db[:, 256:], TN, preferred_element_type=F32)], axis=0)


def _mixer_pre(e_lru, e_pool, pv, wa_ref, wx_ref, wp_ref, tm, t0):
    xc = pv[ROW_CB:ROW_CB + 1, :]
    for k in range(4):
        xc = xc + e_lru[pl.ds(HALO - 3 + k, tm), :] * pv[ROW_CW + k:ROW_CW + k + 1, :]
    xcb = xc.astype(BF16)
    r = _sigmoid(_bd(xcb, wa_ref) + pv[ROW_BA:ROW_BA + 1, :])
    ig = _sigmoid(_bd(xcb, wx_ref) + pv[ROW_BX:ROW_BX + 1, :])
    z = -pv[ROW_LAM:ROW_LAM + 1, :]
    sp = jnp.maximum(z, 0.0) + jnp.log(1.0 + jnp.exp(-jnp.abs(z)))
    la = (-LRU_C * r) * sp
    a = jnp.exp(la)
    om = -jnp.tanh(la) * (1.0 + a * a)
    mult = jnp.sqrt(jnp.maximum(om, 1e-12))
    t = t0 + lax.broadcasted_iota(jnp.int32, (tm, 1), 0)
    parts, cnts = [], []
    for g, w in enumerate(POOL_WINDOWS):
        ext = e_pool[:, pl.ds(128 * g, 128)]
        s = _window_sum(ext, w, back=True)[HALO:, :]
        cnt = jnp.minimum(t + 1, w).astype(F32)
        cnts.append(cnt)
        parts.append(s / cnt - ext[HALO:, :])
    pooled = jnp.concatenate(parts, axis=1)
    pooled_b = pooled.astype(BF16)
    zp = _bd(pooled_b, wp_ref) + pv[ROW_PB:ROW_PB + 1, :]
    return dict(xc=xc, xcb=xcb, r=r, ig=ig, sp=sp, a=a, om=om, mult=mult, pooled_b=pooled_b, zp=zp, cnts=cnts)


def _scan_tile(a_ref, b_ref, out_ref, carry, tm, reverse):
    row = lax.broadcasted_iota(jnp.int32, (8, LRU_W), 0)
    nblk = tm // 8

    def step(i, hin):
        blk = (nblk - 1 - i) if reverse else i
        r0 = pl.multiple_of(blk * 8, 8)
        av = a_ref[pl.ds(r0, 8), :]
        bv = b_ref[pl.ds(r0, 8), :]
        for d in (1, 2, 4):
            sh = (8 - d) if reverse else d
            a_s = pltpu.roll(av, sh, 0)
            b_s = pltpu.roll(bv, sh, 0)
            m = (row < 8 - d) if reverse else (row >= d)
            bv = jnp.where(m, av * b_s + bv, bv)
            av = jnp.where(m, av * a_s, av)
        hv = av * hin + bv
        out_ref[pl.ds(r0, 8), :] = hv
        edge = hv[0:1, :] if reverse else hv[7:8, :]
        return jnp.broadcast_to(edge, (8, LRU_W))

    return lax.fori_loop(0, nblk, step, carry)


MESH = pl.DeviceIdType.MESH
ANY = pl.BlockSpec(memory_space=pl.ANY)


def _place():
    x, y, c = lax.axis_index("x"), lax.axis_index("y"), lax.axis_index("c")
    chips = [(1 - x, y), (x, 1 - y), (1 - x, 1 - y)]
    return x, y, c, chips


class _Gather:
    def __init__(self, ins, outs, send_sems, recv_sems, local_sems):
        self.ins, self.outs, self.n = ins, outs, len(ins)
        self.send_sems, self.recv_sems, self.local_sems = send_sems, recv_sems, local_sems

    @staticmethod
    def scratch(n):
        return [pltpu.SemaphoreType.DMA((7, n)), pltpu.SemaphoreType.DMA((7, n)), pltpu.SemaphoreType.DMA((n,))]

    def _slot(self, a, px, py, pc):
        return self.outs[a].at[4 * px + 2 * py + pc]

    def _copy(self, a, k, block, to, src=None):
        return pltpu.make_async_remote_copy(
            src_ref=self._slot(a, *block) if src is None else src, dst_ref=self._slot(a, *block),
            send_sem=self.send_sems.at[k, a], recv_sem=self.recv_sems.at[k, a], device_id=to, device_id_type=MESH)

    def _mine(self, a):
        x, y, c, _ = _place()
        return pltpu.make_async_copy(self.ins[a], self._slot(a, x, y, c), self.local_sems.at[a])

    def _first(self, a):
        x, y, c, chips = _place()
        me = (x, y, c)
        return ([self._copy(a, 0, me, (x, y, 1 - c), src=self.ins[a])]
                + [self._copy(a, 1 + j, me, (*chip, c), src=self.ins[a]) for j, chip in enumerate(chips)])

    def start(self):
        for a in range(self.n):
            self._mine(a).start()
        for a in range(self.n):
            for cp in self._first(a):
                cp.start()

    def finish(self):
        x, y, c, chips = _place()
        me, sibling = (x, y, c), (x, y, 1 - c)
        passed = []
        for j, chip in enumerate(chips):
            for a in range(self.n):
                self._copy(a, 1 + j, (*chip, c), me).wait_recv()
                fwd = self._copy(a, 4 + j, (*chip, c), sibling)
                fwd.start()
                passed.append(fwd)
        for a in range(self.n):
            self._copy(a, 0, (x, y, 1 - c), me).wait_recv()
            for j, chip in enumerate(chips):
                self._copy(a, 4 + j, (*chip, 1 - c), me).wait_recv()
        for a in range(self.n):
            for cp in self._first(a):
                cp.wait_send()
        for cp in passed:
            cp.wait_send()
        for a in range(self.n):
            self._mine(a).wait()


def _all_gather(arrs, name):
    n = len(arrs)

    def body(*refs):
        g = _Gather(refs[:n], refs[n:2 * n], *refs[2 * n:])
        g.start()
        g.finish()

    return pl.pallas_call(
        body, name=name,
        out_shape=[_sds((N_DEV,) + a.shape, a.dtype) for a in arrs],
        in_specs=[ANY] * n, out_specs=[ANY] * n, scratch_shapes=_Gather.scratch(n),
    )(*arrs)


def _pair_exchange(arrs, name):
    n = len(arrs)

    def body(*refs):
        ins, outs = refs[:n], refs[n:2 * n]
        send_sems, recv_sems = refs[2 * n:]
        x, y, c, _ = _place()
        sibling = (x, y, 1 - c)
        sends = []
        for a in range(n):
            for j in range(4):
                cp = pltpu.make_async_remote_copy(
                    src_ref=ins[a].at[2 * j + (1 - c)], dst_ref=outs[a].at[j],
                    send_sem=send_sems.at[j, a], recv_sem=recv_sems.at[j, a], device_id=sibling, device_id_type=MESH)
                cp.start()
                sends.append(cp)
        for cp in sends:
            cp.wait()

    return pl.pallas_call(
        body, name=name,
        out_shape=[_sds((4,) + a.shape[1:], a.dtype) for a in arrs],
        in_specs=[ANY] * n, out_specs=[ANY] * n,
        scratch_shapes=[pltpu.SemaphoreType.DMA((4, n)), pltpu.SemaphoreType.DMA((4, n))],
    )(*arrs)


class _ChipExchange:
    def __init__(self, ins, outs, send_sems, recv_sems, local_sems):
        self.ins, self.outs, self.n = ins, outs, len(ins)
        self.send_sems, self.recv_sems, self.local_sems = send_sems, recv_sems, local_sems

    @staticmethod
    def scratch(n):
        return [pltpu.SemaphoreType.DMA((3, n)), pltpu.SemaphoreType.DMA((3, n)), pltpu.SemaphoreType.DMA((n,))]

    def _local(self, a):
        x, y, _, _ = _place()
        me = 2 * x + y
        return pltpu.make_async_copy(self.ins[a].at[me], self.outs[a].at[me], self.local_sems.at[a])

    def _copies(self, a):
        x, y, c, chips = _place()
        me = 2 * x + y
        return [(pltpu.make_async_remote_copy(
                     src_ref=self.ins[a].at[2 * px + py], dst_ref=self.outs[a].at[me],
                     send_sem=self.send_sems.at[k, a], recv_sem=self.recv_sems.at[k, a],
                     device_id=(px, py, c), device_id_type=MESH),
                 pltpu.make_async_remote_copy(
                     src_ref=self.ins[a].at[me], dst_ref=self.outs[a].at[2 * px + py],
                     send_sem=self.send_sems.at[k, a], recv_sem=self.recv_sems.at[k, a],
                     device_id=(px, py, c), device_id_type=MESH))
                for k, (px, py) in enumerate(chips)]

    def start(self):
        for a in range(self.n):
            self._local(a).start()
        for a in range(self.n):
            for send, _ in self._copies(a):
                send.start()

    def finish(self):
        for a in range(self.n):
            for send, recv in self._copies(a):
                send.wait_send()
                recv.wait_recv()
        for a in range(self.n):
            self._local(a).wait()


def _chip_exchange(arrs, name):
    n = len(arrs)

    def body(*refs):
        e = _ChipExchange(refs[:n], refs[n:2 * n], *refs[2 * n:])
        e.start()
        e.finish()

    return pl.pallas_call(
        body, name=name, out_shape=[_sds(a.shape, a.dtype) for a in arrs],
        in_specs=[ANY] * n, out_specs=[ANY] * n, scratch_shapes=_ChipExchange.scratch(n),
    )(*arrs)


def _mix_in(x, g_mix, w_in_t, tm, shards=()):
    T = x.shape[0]
    n_t = T // tm
    n_s = len(shards)

    def body(x_ref, g_ref, w_ref, *rest):
        sh_in, rest = rest[:n_s], rest[n_s:]
        u_ref, h_ref = rest[:2]
        gather = _Gather(sh_in, rest[2:2 + n_s], *rest[2 + n_s:]) if n_s else None
        i = pl.program_id(0)

        if gather:
            @pl.when(i == 0)
            def _():
                gather.start()

        xv = x_ref[...]
        h = (xv * _rstd(xv) * g_ref[...]).astype(BF16)
        h_ref[...] = h
        u_ref[...] = lax.dot_general(h, w_ref[...], NT, preferred_element_type=F32)

        if gather:
            @pl.when(i == n_t - 1)
            def _():
                gather.finish()

    outs = pl.pallas_call(
        body, name="mix_in", grid=(n_t,),
        in_specs=[pl.BlockSpec((tm, D_MODEL), lambda i: (i, 0)),
                  pl.BlockSpec((1, D_MODEL), lambda i: (0, 0)),
                  pl.BlockSpec((D_IN, D_MODEL), lambda i: (0, 0))] + [ANY] * n_s,
        out_specs=[pl.BlockSpec((tm, D_IN), lambda i: (i, 0)),
                   pl.BlockSpec((tm, D_MODEL), lambda i: (i, 0))] + [ANY] * n_s,
        out_shape=[_sds((T, D_IN), F32), _sds((T, D_MODEL), BF16)] + [_sds((N_DEV,) + a.shape, a.dtype) for a in shards],
        scratch_shapes=_Gather.scratch(n_s) if n_s else [],
        compiler_params=pltpu.CompilerParams(dimension_semantics=("arbitrary",)),
    )(x, g_mix, w_in_t, *shards)
    return outs[0], outs[1], list(outs[2:])


def _mixer_fwd(u, pv, wa, wx, wp, tm, shards=()):
    T = u.shape[0]
    n_s = len(shards)
    n_t = T // tm

    def body(u_ref, pv_ref, wa_ref, wx_ref, wp_ref, *rest):
        sh_in, rest = rest[:n_s], rest[n_s:]
        y_ref, hs_ref = rest[:2]
        sh_out, rest = rest[2:2 + n_s], rest[2 + n_s:]
        e_lru, e_pool, a_s, b_s, hc = rest[:5]
        gather = _Gather(sh_in, sh_out, *rest[5:]) if n_s else None
        i = pl.program_id(0)

        @pl.when(i == 0)
        def _():
            e_lru[pl.ds(0, HALO), :] = jnp.zeros((HALO, LRU_W), F32)
            e_pool[pl.ds(0, HALO), :] = jnp.zeros((HALO, POOL_W), F32)
            hc[...] = jnp.zeros((8, LRU_W), F32)
            if gather:
                gather.start()

        e_lru[pl.ds(HALO, tm), :] = u_ref[:, 0:LRU_W]
        e_pool[pl.ds(HALO, tm), :] = u_ref[:, 2 * LRU_W:D_IN]
        pv = pv_ref[...]
        p = _mixer_pre(e_lru, e_pool, pv, wa_ref, wx_ref, wp_ref, tm, i * tm)
        a_s[...] = p["a"]
        b_s[...] = p["mult"] * (p["ig"] * p["xc"])
        hc[...] = _scan_tile(a_s, b_s, hs_ref, hc[...], tm, reverse=False)
        gl, _ = _gelu_parts(u_ref[:, LRU_W:2 * LRU_W])
        y_lru = hs_ref[...] * gl
        y_ref[:, 0:LRU_W] = (y_lru * _rstd(y_lru) * pv[ROW_GL:ROW_GL + 1, :]).astype(BF16)
        y_pool = p["zp"] * pv[ROW_PS:ROW_PS + 1, :]
        y_ref[:, LRU_W:D_MODEL] = (y_pool * _rstd(y_pool) * pv[ROW_GP:ROW_GP + 1, :]).astype(BF16)
        e_lru[pl.ds(0, HALO), :] = e_lru[pl.ds(tm, HALO), :]
        e_pool[pl.ds(0, HALO), :] = e_pool[pl.ds(tm, HALO), :]

        if gather:
            @pl.when(i == n_t - 1)
            def _():
                gather.finish()

    full = lambda shape: pl.BlockSpec(shape, lambda i: (0,) * len(shape))
    outs = pl.pallas_call(
        body, name="mixer_fwd", grid=(n_t,),
        in_specs=[pl.BlockSpec((tm, D_IN), lambda i: (i, 0)), full((16, LRU_W)),
                  full((2, 256, 256)), full((2, 256, 256)), full((2, 256, 256))] + [ANY] * n_s,
        out_specs=[pl.BlockSpec((tm, D_MODEL), lambda i: (i, 0)), pl.BlockSpec((tm, LRU_W), lambda i: (i, 0))] + [ANY] * n_s,
        out_shape=[_sds((T, D_MODEL), BF16), _sds((T, LRU_W), F32)] + [_sds((N_DEV,) + a.shape, a.dtype) for a in shards],
        scratch_shapes=[pltpu.VMEM((HALO + tm, LRU_W), F32), pltpu.VMEM((HALO + tm, POOL_W), F32),
                        pltpu.VMEM((tm, LRU_W), F32), pltpu.VMEM((tm, LRU_W), F32), pltpu.VMEM((8, LRU_W), F32)]
        + (_Gather.scratch(n_s) if n_s else []),
        compiler_params=pltpu.CompilerParams(dimension_semantics=("arbitrary",)),
    )(u, pv, wa, wx, wp, *shards)
    return outs[0], outs[1], list(outs[2:])


def _ffn_fwd(x, y, w_out_b, g_ffn, w1_b, w3_b, w2_b, g_fin, tgt, tm, tn):
    T = x.shape[0]
    n_j = D_FF // tn

    def body(x_ref, y_ref, wo_ref, gf_ref, w1_ref, w3_ref, w2_ref, gfin_ref, tgt_ref,
             hres_ref, h2_ref, g_ref, v_ref, d3_ref, loss_ref, dgfin_ref, acc):
        i, j = pl.program_id(0), pl.program_id(1)

        @pl.when(j == 0)
        def _():
            hr = x_ref[...] + jnp.dot(y_ref[...], wo_ref[...], preferred_element_type=F32)
            hres_ref[...] = hr
            h2_ref[...] = (hr * _rstd(hr) * gf_ref[...]).astype(BF16)
            acc[...] = jnp.zeros((tm, D_MODEL), F32)

        @pl.when((j == 0) & (i == 0))
        def _():
            loss_ref[...] = jnp.zeros((8, 128), F32)
            dgfin_ref[...] = jnp.zeros((1, D_MODEL), F32)

        h2 = h2_ref[...]
        g = lax.dot_general(h2, w1_ref[...], NT, preferred_element_type=F32)
        v = lax.dot_general(h2, w3_ref[...], NT, preferred_element_type=F32)
        g_ref[...] = g.astype(BF16)
        v_ref[...] = v.astype(BF16)
        ff = ((g * _sigmoid(g)) * v).astype(BF16)
        acc[...] += jnp.dot(ff, w2_ref[...], preferred_element_type=F32)

        @pl.when(j == n_j - 1)
        def _():
            h3 = hres_ref[...] + acc[...]
            rstd = _rstd(h3)
            xh = h3 * rstd
            gfin = gfin_ref[...]
            err = xh * gfin - tgt_ref[...]
            loss_ref[...] += 0.5 * jnp.sum(jnp.mean(err * err, axis=-1, keepdims=True))
            dout = err * (1.0 / D_MODEL)
            dx, dgain = _rms_bwd(dout, xh, rstd, gfin)
            d3_ref[...] = dx
            dgfin_ref[...] += dgain

    row = lambda w: pl.BlockSpec((tm, w), lambda i, j: (i, 0))
    const = lambda shape: pl.BlockSpec(shape, lambda i, j: (0,) * len(shape))
    return pl.pallas_call(
        body, name="ffn_fwd", grid=(T // tm, n_j),
        in_specs=[row(D_MODEL), row(D_MODEL), const((D_MODEL, D_MODEL)), const((1, D_MODEL)),
                  pl.BlockSpec((tn, D_MODEL), lambda i, j: (j, 0)), pl.BlockSpec((tn, D_MODEL), lambda i, j: (j, 0)),
                  pl.BlockSpec((tn, D_MODEL), lambda i, j: (j, 0)), const((1, D_MODEL)), row(D_MODEL)],
        out_specs=[row(D_MODEL), row(D_MODEL),
                   pl.BlockSpec((tm, tn), lambda i, j: (i, j)), pl.BlockSpec((tm, tn), lambda i, j: (i, j)),
                   row(D_MODEL), const((8, 128)), const((1, D_MODEL))],
        out_shape=[_sds((T, D_MODEL), F32), _sds((T, D_MODEL), BF16), _sds((T, D_FF), BF16), _sds((T, D_FF), BF16),
                   _sds((T, D_MODEL), F32), _sds((8, 128), F32), _sds((1, D_MODEL), F32)],
        scratch_shapes=[pltpu.VMEM((tm, D_MODEL), F32)],
        compiler_params=pltpu.CompilerParams(dimension_semantics=("arbitrary", "arbitrary")),
    )(x, y, w_out_b, g_ffn, w1_b, w3_b, w2_b, g_fin, tgt)


def _ffn_bwd(d3, g, v, w1_b, w3_b, w2_b, hres, g_ffn, tm, tn):
    T = d3.shape[0]
    n_j = D_FF // tn

    def body(d3_ref, g_ref, v_ref, w1_ref, w3_ref, w2_ref, hres_ref, gf_ref,
             dg_ref, dv_ref, ff_ref, d2_ref, dgffn_ref, acc):
        i, j = pl.program_id(0), pl.program_id(1)

        @pl.when(j == 0)
        def _():
            acc[...] = jnp.zeros((tm, D_MODEL), F32)

        @pl.when((j == 0) & (i == 0))
        def _():
            dgffn_ref[...] = jnp.zeros((1, D_MODEL), F32)

        dff = lax.dot_general(d3_ref[...].astype(BF16), w2_ref[...], NT, preferred_element_type=F32)
        gv = g_ref[...].astype(F32)
        vv = v_ref[...].astype(F32)
        sg = _sigmoid(gv)
        sl = gv * sg
        dgb = (dff * vv * (sg * (1.0 + gv * (1.0 - sg)))).astype(BF16)
        dvb = (dff * sl).astype(BF16)
        dg_ref[...] = dgb
        dv_ref[...] = dvb
        ff_ref[...] = (sl * vv).astype(BF16)
        acc[...] += (jnp.dot(dgb, w1_ref[...], preferred_element_type=F32)
                     + jnp.dot(dvb, w3_ref[...], preferred_element_type=F32))

        @pl.when(j == n_j - 1)
        def _():
            hr = hres_ref[...]
            rstd = _rstd(hr)
            dx, dgain = _rms_bwd(acc[...], hr * rstd, rstd, gf_ref[...])
            d2_ref[...] = d3_ref[...] + dx
            dgffn_ref[...] += dgain

    row = lambda w: pl.BlockSpec((tm, w), lambda i, j: (i, 0))
    tile = pl.BlockSpec((tm, tn), lambda i, j: (i, j))
    const = lambda shape: pl.BlockSpec(shape, lambda i, j: (0,) * len(shape))
    return pl.pallas_call(
        body, name="ffn_bwd", grid=(T // tm, n_j),
        in_specs=[row(D_MODEL), tile, tile,
                  pl.BlockSpec((tn, D_MODEL), lambda i, j: (j, 0)), pl.BlockSpec((tn, D_MODEL), lambda i, j: (j, 0)),
                  pl.BlockSpec((tn, D_MODEL), lambda i, j: (j, 0)), row(D_MODEL), const((1, D_MODEL))],
        out_specs=[tile, tile, tile, row(D_MODEL), const((1, D_MODEL))],
        out_shape=[_sds((T, D_FF), BF16), _sds((T, D_FF), BF16), _sds((T, D_FF), BF16),
                   _sds((T, D_MODEL), F32), _sds((1, D_MODEL), F32)],
        scratch_shapes=[pltpu.VMEM((tm, D_MODEL), F32)],
        compiler_params=pltpu.CompilerParams(dimension_semantics=("arbitrary", "arbitrary")),
    )(d3, g, v, w1_b, w3_b, w2_b, hres, g_ffn)


def _at_b(a, b, name, tmm, tn, tk, gather=()):
    T, M = a.shape
    N = b.shape[1]
    n_m, n_n, n_k = M // tmm, N // tn, T // tk
    n_g = len(gather)

    def body(a_ref, b_ref, *rest):
        g_in, o_ref, rest = rest[:n_g], rest[n_g], rest[n_g + 1:]
        ag = _Gather(g_in, rest[:n_g], *rest[n_g:]) if n_g else None
        m, n, k = pl.program_id(0), pl.program_id(1), pl.program_id(2)

        if ag:
            @pl.when((m == 0) & (n == 0) & (k == 0))
            def _():
                ag.start()

        @pl.when(k == 0)
        def _():
            o_ref[...] = jnp.zeros((tmm, tn), F32)

        o_ref[...] += lax.dot_general(a_ref[...].astype(BF16), b_ref[...].astype(BF16), TN,
                                      preferred_element_type=F32)

        if ag:
            @pl.when((m == n_m - 1) & (n == n_n - 1) & (k == n_k - 1))
            def _():
                ag.finish()

    outs = pl.pallas_call(
        body, name=name, grid=(n_m, n_n, n_k),
        in_specs=[pl.BlockSpec((tk, tmm), lambda m, n, k: (k, m)), pl.BlockSpec((tk, tn), lambda m, n, k: (k, n))]
        + [ANY] * n_g,
        out_specs=[pl.BlockSpec((tmm, tn), lambda m, n, k: (m, n))] + [ANY] * n_g,
        out_shape=[_sds((M, N), F32)] + [_sds((N_DEV,) + g.shape, g.dtype) for g in gather],
        scratch_shapes=_Gather.scratch(n_g) if n_g else [],
        compiler_params=pltpu.CompilerParams(
            dimension_semantics=("arbitrary",) * 3 if n_g else ("parallel", "parallel", "arbitrary")),
    )(a, b, *gather)
    return (outs[0], list(outs[1:])) if n_g else outs[0]


def _mixer_bwd(d2, u, hs, pv, wa, wx, wp, w_out_b, tm, chip_sums=()):
    T = u.shape[0]
    n_t = T // tm
    n_x = len(chip_sums)

    def body(d2_ref, u_ref, uh_ref, hs_ref, hh_ref, pv_ref, wa_ref, wx_ref, wp_ref, wo_ref, *rest):
        x_in, rest = rest[:n_x], rest[n_x:]
        du_ref, vacc_ref, dwa_ref, dwx_ref, dwp_ref = rest[:5]
        x_out, rest = rest[5:5 + n_x], rest[5 + n_x:]
        e_lru, e_pool, e_h, a_s, b_s, mu_s, f_x, f_p, mc, cx, cp = rest[:11]
        exchange = _ChipExchange(x_in, x_out, *rest[11:]) if n_x else None
        s = pl.program_id(0)
        it = n_t - 1 - s

        @pl.when(s == 0)
        def _():
            mc[...] = jnp.zeros((8, LRU_W), F32)
            cx[...] = jnp.zeros((8, LRU_W), F32)
            cp[...] = jnp.zeros((HALO, POOL_W), F32)
            vacc_ref[...] = jnp.zeros((16, LRU_W), F32)
            dwa_ref[...] = jnp.zeros((2, 256, 256), F32)
            dwx_ref[...] = jnp.zeros((2, 256, 256), F32)
            dwp_ref[...] = jnp.zeros((2, 256, 256), F32)
            if exchange:
                exchange.start()

        first = it == 0
        e_lru[pl.ds(0, HALO), :] = jnp.where(first, 0.0, uh_ref[:, 0:LRU_W])
        e_pool[pl.ds(0, HALO), :] = jnp.where(first, 0.0, uh_ref[:, 2 * LRU_W:D_IN])
        e_lru[pl.ds(HALO, tm), :] = u_ref[:, 0:LRU_W]
        e_pool[pl.ds(HALO, tm), :] = u_ref[:, 2 * LRU_W:D_IN]
        e_h[pl.ds(0, 8), :] = jnp.where(first, 0.0, hh_ref[...])
        e_h[pl.ds(8, tm), :] = hs_ref[...]
        pv = pv_ref[...]
        p = _mixer_pre(e_lru, e_pool, pv, wa_ref, wx_ref, wp_ref, tm, it * tm)
        a, xc, ig, r, mult = p["a"], p["xc"], p["ig"], p["r"], p["mult"]

        dyn = lax.dot_general(d2_ref[...].astype(BF16), wo_ref[...], NT, preferred_element_type=F32)

        h = hs_ref[...]
        ug = u_ref[:, LRU_W:2 * LRU_W]
        gl, dgl = _gelu_parts(ug)
        y_lru = h * gl
        rstd_l = _rstd(y_lru)
        dy_lru, d_gain_l = _rms_bwd(dyn[:, 0:LRU_W], y_lru * rstd_l, rstd_l, pv[ROW_GL:ROW_GL + 1, :])
        dh = dy_lru * gl
        du_ref[:, LRU_W:2 * LRU_W] = (dy_lru * h * dgl).astype(BF16)
        a_s[...] = a
        b_s[...] = a * dh
        mu_s[pl.ds(tm, 8), :] = mc[...]
        mc[...] = _scan_tile(a_s, b_s, mu_s, mc[...], tm, reverse=True)
        lam_t = dh + mu_s[pl.ds(1, tm), :]
        da = lam_t * e_h[pl.ds(7, tm), :]
        dmult = lam_t * (ig * xc)
        di = lam_t * (mult * xc)
        dxc = lam_t * (mult * ig)
        dla = da * a + jnp.where(p["om"] > 1e-12, dmult * (-(a * a) / mult), 0.0)
        dra = (dla * (-LRU_C * p["sp"])) * (r * (1.0 - r))
        dia = di * (ig * (1.0 - ig))
        drab = dra.astype(BF16)
        diab = dia.astype(BF16)
        dxc = dxc + _bd_t(drab, wa_ref) + _bd_t(diab, wx_ref)
        dwa_ref[...] += _bd_grad(p["xcb"], drab)
        dwx_ref[...] += _bd_grad(p["xcb"], diab)
        sig_neg_lam = _sigmoid(-pv[ROW_LAM:ROW_LAM + 1, :])
        d_lam = jnp.sum(dla * r, axis=0, keepdims=True) * (LRU_C * sig_neg_lam)

        f_x[pl.ds(0, tm), :] = dxc
        f_x[pl.ds(tm, 8), :] = cx[...]
        du_lru = jnp.zeros((tm, LRU_W), F32)
        d_cw = []
        for k in range(4):
            du_lru = du_lru + f_x[pl.ds(3 - k, tm), :] * pv[ROW_CW + k:ROW_CW + k + 1, :]
            d_cw.append(jnp.sum(dxc * e_lru[pl.ds(HALO - 3 + k, tm), :], axis=0, keepdims=True))
        du_ref[:, 0:LRU_W] = du_lru.astype(BF16)
        cx[...] = f_x[pl.ds(0, 8), :]

        zp = p["zp"]
        ps = pv[ROW_PS:ROW_PS + 1, :]
        y_pool = zp * ps
        rstd_p = _rstd(y_pool)
        dy_pool, d_gain_p = _rms_bwd(dyn[:, LRU_W:D_MODEL], y_pool * rstd_p, rstd_p, pv[ROW_GP:ROW_GP + 1, :])
        dz = dy_pool * ps
        dzb = dz.astype(BF16)
        dwp_ref[...] += _bd_grad(p["pooled_b"], dzb)
        dpooled = _bd_t(dzb, wp_ref)
        for g, w in enumerate(POOL_WINDOWS):
            f_p[pl.ds(0, tm), pl.ds(128 * g, 128)] = dpooled[:, 128 * g:128 * (g + 1)] / p["cnts"][g]
        f_p[pl.ds(tm, HALO), :] = cp[...]
        for g, w in enumerate(POOL_WINDOWS):
            acc = _window_sum(f_p[:, pl.ds(128 * g, 128)], w, back=False)[0:tm, :]
            du_ref[:, 2 * LRU_W + 128 * g:2 * LRU_W + 128 * (g + 1)] = (
                acc - dpooled[:, 128 * g:128 * (g + 1)]).astype(BF16)
        cp[...] = f_p[pl.ds(0, HALO), :]

        rows = d_cw + [
            jnp.sum(dxc, axis=0, keepdims=True),
            jnp.sum(dra, axis=0, keepdims=True),
            jnp.sum(dia, axis=0, keepdims=True),
            d_lam,
            jnp.sum(dz, axis=0, keepdims=True),
            jnp.sum(dy_pool * zp, axis=0, keepdims=True),
            d_gain_l, d_gain_p,
            jnp.zeros((4, LRU_W), F32),
        ]
        vacc_ref[...] += jnp.concatenate(rows, axis=0)

        if exchange:
            @pl.when(s == n_t - 1)
            def _():
                exchange.finish()

    rev = lambda w: pl.BlockSpec((tm, w), lambda s: (n_t - 1 - s, 0))
    full = lambda shape: pl.BlockSpec(shape, lambda s: (0,) * len(shape))
    outs = pl.pallas_call(
        body, name="mixer_bwd", grid=(n_t,),
        in_specs=[rev(D_MODEL), rev(D_IN),
                  pl.BlockSpec((HALO, D_IN), lambda s: (jnp.maximum((n_t - 1 - s) * (tm // HALO) - 1, 0), 0)),
                  rev(LRU_W),
                  pl.BlockSpec((8, LRU_W), lambda s: (jnp.maximum((n_t - 1 - s) * (tm // 8) - 1, 0), 0)),
                  full((16, LRU_W)), full((2, 256, 256)), full((2, 256, 256)), full((2, 256, 256)),
                  full((D_MODEL, D_MODEL))] + [ANY] * n_x,
        out_specs=[rev(D_IN), full((16, LRU_W)), full((2, 256, 256)), full((2, 256, 256)), full((2, 256, 256))]
        + [ANY] * n_x,
        out_shape=[_sds((T, D_IN), BF16), _sds((16, LRU_W), F32), _sds((2, 256, 256), F32),
                   _sds((2, 256, 256), F32), _sds((2, 256, 256), F32)] + [_sds(a.shape, a.dtype) for a in chip_sums],
        scratch_shapes=[pltpu.VMEM((HALO + tm, LRU_W), F32), pltpu.VMEM((HALO + tm, POOL_W), F32),
                        pltpu.VMEM((8 + tm, LRU_W), F32), pltpu.VMEM((tm, LRU_W), F32), pltpu.VMEM((tm, LRU_W), F32),
                        pltpu.VMEM((tm + 8, LRU_W), F32), pltpu.VMEM((tm + 8, LRU_W), F32),
                        pltpu.VMEM((tm + HALO, POOL_W), F32), pltpu.VMEM((8, LRU_W), F32),
                        pltpu.VMEM((8, LRU_W), F32), pltpu.VMEM((HALO, POOL_W), F32)]
        + (_ChipExchange.scratch(n_x) if n_x else []),
        compiler_params=pltpu.CompilerParams(dimension_semantics=("arbitrary",)),
    )(d2, u, u, hs, hs, pv, wa, wx, wp, w_out_b, *chip_sums)
    return tuple(outs[:5]) + (list(outs[5:]),)


def _mix_in_bwd(du, x, d2, w_in_t, g_mix, tm, chip_sums=()):
    T = x.shape[0]
    n_t = T // tm
    n_x = len(chip_sums)

    def body(du_ref, x_ref, d2_ref, w_ref, g_ref, *rest):
        x_in, rest = rest[:n_x], rest[n_x:]
        dx_ref, dg_ref = rest[:2]
        exchange = _ChipExchange(x_in, rest[2:2 + n_x], *rest[2 + n_x:]) if n_x else None
        i = pl.program_id(0)

        @pl.when(i == 0)
        def _():
            dg_ref[...] = jnp.zeros((1, D_MODEL), F32)
            if exchange:
                exchange.start()

        dh = jnp.dot(du_ref[...], w_ref[...], preferred_element_type=F32)
        xv = x_ref[...]
        rstd = _rstd(xv)
        dx, dgain = _rms_bwd(dh, xv * rstd, rstd, g_ref[...])
        dx_ref[...] = d2_ref[...] + dx
        dg_ref[...] += dgain

        if exchange:
            @pl.when(i == n_t - 1)
            def _():
                exchange.finish()

    row = lambda w: pl.BlockSpec((tm, w), lambda i: (i, 0))
    const = lambda shape: pl.BlockSpec(shape, lambda i: (0,) * len(shape))
    outs = pl.pallas_call(
        body, name="mix_in_bwd", grid=(n_t,),
        in_specs=[row(D_IN), row(D_MODEL), row(D_MODEL), const((D_IN, D_MODEL)), const((1, D_MODEL))] + [ANY] * n_x,
        out_specs=[row(D_MODEL), const((1, D_MODEL))] + [ANY] * n_x,
        out_shape=[_sds((T, D_MODEL), F32), _sds((1, D_MODEL), F32)] + [_sds(a.shape, a.dtype) for a in chip_sums],
        scratch_shapes=_ChipExchange.scratch(n_x) if n_x else [],
        compiler_params=pltpu.CompilerParams(dimension_semantics=("arbitrary",)),
    )(du, x, d2, w_in_t, g_mix, *chip_sums)
    return outs[0], outs[1], list(outs[2:])


def _block_diag(w, per):
    n, k, _ = w.shape
    out = jnp.zeros((n // per, per * k, per * k), w.dtype)
    for b in range(n):
        out = out.at[b // per, (b % per) * k:(b % per + 1) * k, (b % per) * k:(b % per + 1) * k].set(w[b])
    return out


def _diag_blocks(w, per):
    m, pk, _ = w.shape
    k = pk // per
    return jnp.stack([w[b // per, (b % per) * k:(b % per + 1) * k, (b % per) * k:(b % per + 1) * k]
                      for b in range(m * per)], axis=0)


def _pair_sum(g, r1, c_arr, name):
    _, _, R, C = g.shape
    tr = R if R <= 512 else 256

    def body(c_ref, g_ref, r_ref, o_ref):
        o_ref[...] = (g_ref[...] + r_ref[...]).astype(BF16)

    return pl.pallas_call(
        body, name=name,
        grid_spec=pltpu.PrefetchScalarGridSpec(
            num_scalar_prefetch=1, grid=(4, R // tr),
            in_specs=[pl.BlockSpec((None, None, tr, C), lambda j, i, c_ref: (j, c_ref[0], i, 0)),
                      pl.BlockSpec((None, tr, C), lambda j, i, c_ref: (j, i, 0))],
            out_specs=pl.BlockSpec((None, tr, C), lambda j, i, c_ref: (j, i, 0))),
        out_shape=_sds((4, R, C), BF16),
    )(c_arr, g, r1)


def _adamw(w, g, m, v):
    m = ADAM_B1 * m + (1.0 - ADAM_B1) * g
    v = ADAM_B2 * v + (1.0 - ADAM_B2) * (g * g)
    m_hat = m / (1.0 - ADAM_B1 ** ADAM_STEP)
    v_hat = v / (1.0 - ADAM_B2 ** ADAM_STEP)
    delta = -ADAM_LR * (m_hat / (jnp.sqrt(v_hat) + ADAM_EPS) + ADAM_WD * w)
    return delta, m, v


def _adam_shard(w, m, v, parts, name):
    R, C = w.shape
    tr = R if R <= 512 else 256

    def body(w_ref, m_ref, v_ref, p_ref, g_ref, d_ref, nm_ref, nv_ref):
        g = p_ref[0].astype(F32)
        for j in range(1, 4):
            g = g + p_ref[j].astype(F32)
        delta, nm, nv = _adamw(w_ref[...], g, m_ref[...], v_ref[...])
        g_ref[...] = g
        d_ref[...] = delta
        nm_ref[...] = nm
        nv_ref[...] = nv

    blk = pl.BlockSpec((tr, C), lambda i: (i, 0))
    return pl.pallas_call(
        body, name=name, grid=(R // tr,),
        in_specs=[blk, blk, blk, pl.BlockSpec((4, tr, C), lambda i: (0, i, 0))],
        out_specs=[blk] * 4, out_shape=[_sds((R, C), F32)] * 4,
        compiler_params=pltpu.CompilerParams(dimension_semantics=("parallel",)),
    )(w, m, v, parts)


def _adam_small(w, m, v, parts):
    def body(w_ref, m_ref, v_ref, p_ref, g_ref, d_ref, nm_ref, nv_ref):
        g = p_ref[0]
        for j in range(1, N_DEV):
            g = g + p_ref[j]
        delta, nm, nv = _adamw(w_ref[...], g, m_ref[...], v_ref[...])
        g_ref[...] = g
        d_ref[...] = delta
        nm_ref[...] = nm
        nv_ref[...] = nv

    return pl.pallas_call(body, name="adam_small", out_shape=[_sds((SMALL_ROWS, LRU_W), F32)] * 4)(w, m, v, parts)


ROW_SPARE = 22


def _pack_small(vec_rows, g_mix, g_ffn, g_fin, spare, wa, wx, wp):
    return jnp.concatenate(
        [vec_rows, g_mix.reshape(2, LRU_W), g_ffn.reshape(2, LRU_W), g_fin.reshape(2, LRU_W), spare.reshape(2, LRU_W),
         wa.reshape(64, LRU_W), wx.reshape(64, LRU_W), wp.reshape(128, LRU_W)], axis=0)


def _vec_rows(conv_w_full, conv_b, ba, bx, lam, pb, ps, gl, gp):
    return jnp.concatenate([conv_w_full, conv_b, ba, bx, lam, pb, ps, gl, gp, jnp.zeros((4, LRU_W), F32)], axis=0)


def _unpack_small(p, col0):
    return dict(
        conv_w=lax.dynamic_slice(p[0:4], (0, col0), (4, 64)).reshape(1, 4, 64),
        conv_b=p[4:5], gate_a_b=p[5:6], gate_x_b=p[6:7], lru_lambda=p[7:8], pool_b=p[8:9], pool_scale=p[9:10],
        norm_lru_g=p[10:11], norm_pool_g=p[11:12],
        norm_mix_g=p[16:18].reshape(1, D_MODEL), norm_ffn_g=p[18:20].reshape(1, D_MODEL),
        final_norm_g=p[20:22].reshape(D_MODEL),
        gate_a_w=p[24:88].reshape(1, 8, 64, 64), gate_x_w=p[88:152].reshape(1, 8, 64, 64),
        pool_w=p[152:280].reshape(1, 4, 128, 128))


WEIGHT_ORDER = ['norm_mix_g', 'w_in', 'conv_w', 'conv_b', 'gate_a_w', 'gate_a_b', 'gate_x_w', 'gate_x_b', 'lru_lambda',
                'pool_w', 'pool_b', 'pool_scale', 'norm_lru_g', 'norm_pool_g', 'w_out', 'norm_ffn_g', 'ffn_w1', 'ffn_w3',
                'ffn_w2', 'final_norm_g']


def kernel(x, norm_mix_g, w_in, conv_w, conv_b, gate_a_w, gate_a_b, gate_x_w, gate_x_b, lru_lambda, pool_w, pool_b, pool_scale, norm_lru_g, norm_pool_g, w_out, norm_ffn_g, ffn_w1, ffn_w3, ffn_w2, final_norm_g, loss_target, m_norm_mix_g, m_w_in, m_conv_w, m_conv_b, m_gate_a_w, m_gate_a_b, m_gate_x_w, m_gate_x_b, m_lru_lambda, m_pool_w, m_pool_b, m_pool_scale, m_norm_lru_g, m_norm_pool_g, m_w_out, m_norm_ffn_g, m_ffn_w1, m_ffn_w3, m_ffn_w2, m_final_norm_g, v_norm_mix_g, v_w_in, v_conv_w, v_conv_b, v_gate_a_w, v_gate_a_b, v_gate_x_w, v_gate_x_b, v_lru_lambda, v_pool_w, v_pool_b, v_pool_scale, v_norm_lru_g, v_norm_pool_g, v_w_out, v_norm_ffn_g, v_ffn_w1, v_ffn_w3, v_ffn_w2, v_final_norm_g):
    ax, ay, ac = lax.axis_index("x"), lax.axis_index("y"), lax.axis_index("c")
    dev = 4 * ax + 2 * ay + ac
    col0 = 64 * dev

    tm, tmx, tn, tk = 512, 256, 1408, 512
    xs, tgt = x[0], loss_target[0]
    g_fin = final_norm_g.reshape(1, D_MODEL)
    c_arr = jnp.reshape(ac, (1,)).astype(jnp.int32)

    def pair_sums(blocks, names):
        from_sibling = _pair_exchange(blocks, "grads_to_sibling_" + names[0])
        return [_pair_sum(g.reshape((4, 2) + g.shape[1:]), r, c_arr, "pair_sum_" + nm)
                for g, r, nm in zip(blocks, from_sibling, names)]

    tr = lambda w: jnp.swapaxes(w[0], 0, 1)
    own = lambda w: w[0]
    bf = lambda a: a.astype(BF16)

    g_in, g_conv = _all_gather([bf(tr(w_in)), conv_w[0]], "gather_w_in")
    w_in_t = g_in.reshape(D_IN, D_MODEL)
    conv_w_full = g_conv.transpose(1, 0, 2).reshape(4, LRU_W)
    pv = _vec_rows(conv_w_full, conv_b, gate_a_b, gate_x_b, lru_lambda, pool_b, pool_scale, norm_lru_g, norm_pool_g)
    wa_b = bf(_block_diag(gate_a_w[0], 4))
    wx_b = bf(_block_diag(gate_x_w[0], 4))
    wp_b = bf(_block_diag(pool_w[0], 2))

    u, h1, (g_out, g_w1) = _mix_in(xs, norm_mix_g, w_in_t, tm, shards=[bf(own(w_out)), bf(tr(ffn_w1))])
    y, hs, (g_w3, g_w2) = _mixer_fwd(u, pv, wa_b, wx_b, wp_b, tmx, shards=[bf(tr(ffn_w3)), bf(own(ffn_w2))])
    w_out_b = g_out.reshape(D_MODEL, D_MODEL)
    w1_t, w3_t, w2_b = g_w1.reshape(D_FF, D_MODEL), g_w3.reshape(D_FF, D_MODEL), g_w2.reshape(D_FF, D_MODEL)
    hres, h2, g, v, d3, loss_acc, d_gfin = _ffn_fwd(xs, y, w_out_b, norm_ffn_g, w1_t, w3_t, w2_b, g_fin, tgt, tm, tn)

    dg, dv, ff, d2, d_gffn = _ffn_bwd(d3, g, v, w1_t, w3_t, w2_b, hres, norm_ffn_g, tm, tn)
    blocks = lambda a: a.reshape(N_DEV, a.shape[0] // N_DEV, a.shape[1])
    d_w1 = blocks(_at_b(dg, h2, "grad_w1", tn, D_MODEL, tk))
    d_w3 = blocks(_at_b(dv, h2, "grad_w3", tn, D_MODEL, tk))
    d_w2 = blocks(_at_b(ff, d3, "grad_w2", tn, D_MODEL, tk))
    d_wout = blocks(_at_b(y, d2, "grad_w_out", D_MODEL, D_MODEL, tk))
    early_sums = pair_sums([d_wout, d_w1, d_w3, d_w2], ["w_out", "w1", "w3", "w2"])
    du, vacc, d_wa, d_wx, d_wp, early_parts = _mixer_bwd(d2, u, hs, pv, wa_b, wx_b, wp_b, w_out_b, tmx,
                                                         chip_sums=early_sums)
    grad_x, d_gmix, _ = _mix_in_bwd(du, xs, d2, w_in_t, norm_mix_g, tm)
    small_local = _pack_small(vacc, d_gmix, d_gffn, d_gfin, loss_acc,
                              _diag_blocks(d_wa, 4), _diag_blocks(d_wx, 4), _diag_blocks(d_wp, 2))
    d_win, (small_parts,) = _at_b(du, h1, "grad_w_in", D_IN, D_MODEL, tk, gather=[small_local])
    win_parts = _chip_exchange(pair_sums([blocks(d_win)], ["w_in"]), "grads_to_chips_w_in")
    parts = list(win_parts) + list(early_parts)

    res = {}
    shard_w = dict(w_in=(w_in, m_w_in, v_w_in, tr), w_out=(w_out, m_w_out, v_w_out, own),
                   ffn_w1=(ffn_w1, m_ffn_w1, v_ffn_w1, tr), ffn_w3=(ffn_w3, m_ffn_w3, v_ffn_w3, tr),
                   ffn_w2=(ffn_w2, m_ffn_w2, v_ffn_w2, own))
    for (nm, (w, m, v, view)), p in zip(shard_w.items(), parts):
        outs = _adam_shard(view(w), view(m), view(v), p, "adam_" + nm)
        res[nm] = [(jnp.swapaxes(o, 0, 1) if view is tr else o)[None] for o in outs]

    def packed(cw, cb, wa, ba, wx, bx, lam, pw, pb, ps, gl, gp, gm, gf, gfin):
        cw_full = lax.dynamic_update_slice(jnp.zeros((4, LRU_W), F32), cw[0], (0, col0))
        return _pack_small(_vec_rows(cw_full, cb, ba, bx, lam, pb, ps, gl, gp), gm, gf, gfin, jnp.zeros((8, 128), F32),
                           wa[0], wx[0], pw[0])

    sw = packed(conv_w, conv_b, gate_a_w, gate_a_b, gate_x_w, gate_x_b, lru_lambda, pool_w, pool_b, pool_scale,
                norm_lru_g, norm_pool_g, norm_mix_g, norm_ffn_g, final_norm_g)
    sm = packed(m_conv_w, m_conv_b, m_gate_a_w, m_gate_a_b, m_gate_x_w, m_gate_x_b, m_lru_lambda, m_pool_w, m_pool_b,
                m_pool_scale, m_norm_lru_g, m_norm_pool_g, m_norm_mix_g, m_norm_ffn_g, m_final_norm_g)
    sv = packed(v_conv_w, v_conv_b, v_gate_a_w, v_gate_a_b, v_gate_x_w, v_gate_x_b, v_lru_lambda, v_pool_w, v_pool_b,
                v_pool_scale, v_norm_lru_g, v_norm_pool_g, v_norm_mix_g, v_norm_ffn_g, v_final_norm_g)
    small_res = _adam_small(sw, sm, sv, small_parts)
    small_out = [_unpack_small(o, col0) for o in small_res]
    for nm in small_out[0]:
        res[nm] = [o[nm] for o in small_out]
    loss = small_res[0][ROW_SPARE, 0]

    out = [loss, grad_x[None]]
    for kind in range(4):
        out += [res[nm][kind] for nm in WEIGHT_ORDER]
    return tuple(out)
```

```python
import jax
import jax.numpy as jnp
from jax import lax
from jax.experimental import pallas as pl
from jax.experimental.pallas import tpu as pltpu

F32 = jnp.float32
BF16 = jnp.bfloat16

D_MODEL = 1024
LRU_W = 512
POOL_W = 512
D_IN = 1536
D_FF = 2816
POOL_WINDOWS = (2, 4, 8, 16)
EPS = 1e-6
LRU_C = 8.0
N_DEV = 8
HALO = 16

ADAM_LR = 0.001
ADAM_B1 = 0.9
ADAM_B2 = 0.999
ADAM_EPS = 1e-08
ADAM_WD = 0.01
ADAM_STEP = 10

ROW_CW, ROW_CB, ROW_BA, ROW_BX, ROW_LAM, ROW_PB, ROW_PS, ROW_GL, ROW_GP = 0, 4, 5, 6, 7, 8, 9, 10, 11
SG_VEC, SG_WA, SG_WX, SG_WP, SG_ROWS = 0, 32, 160, 288, 544

NT = (((1,), (1,)), ((), ()))
TN = (((0,), (0,)), ((), ()))


def _sds(shape, dtype):
    return jax.ShapeDtypeStruct(shape, dtype)


def _sigmoid(x):
    return 0.5 * jnp.tanh(0.5 * x) + 0.5


def _gelu_parts(x):
    c = 0.7978845608028654
    inner = c * (x + 0.044715 * (x * x * x))
    th = jnp.tanh(inner)
    g = 0.5 * x * (1.0 + th)
    dg = 0.5 * (1.0 + th) + 0.5 * x * (1.0 - th * th) * (c * (1.0 + 3.0 * 0.044715 * (x * x)))
    return g, dg


def _window_sum(ext, w, back):
    n = ext.shape[0]
    s, k = ext, 1
    while k < w:
        s = s + pltpu.roll(s, k if back else n - k, 0)
        k *= 2
    return s


def _rstd(x):
    return lax.rsqrt(jnp.mean(x * x, axis=-1, keepdims=True) + EPS)


def _rms_bwd(dy, xhat, rstd, gain):
    dxh = dy * gain
    dx = rstd * (dxh - xhat * jnp.mean(dxh * xhat, axis=-1, keepdims=True))
    return dx, jnp.sum(dy * xhat, axis=0, keepdims=True)


def _bd(xb, w_ref):
    return jnp.concatenate(
        [jnp.dot(xb[:, :256], w_ref[0], preferred_element_type=F32),
         jnp.dot(xb[:, 256:], w_ref[1], preferred_element_type=F32)], axis=1)


def _bd_t(xb, w_ref):
    return jnp.concatenate(
        [lax.dot_general(xb[:, :256], w_ref[0], NT, preferred_element_type=F32),
         lax.dot_general(xb[:, 256:], w_ref[1], NT, preferred_element_type=F32)], axis=1)


def _bd_grad(xb, db):
    return jnp.stack(
        [lax.dot_general(xb[:, :256], db[:, :256], TN, preferred_element_type=F32),
         lax.dot_general(xb[:, 256:], db[:, 256:], TN, preferred_element_type=F32)], axis=0)


def _fill_block_diag(dst, src_ref):
    n, k, _ = src_ref.shape
    dst[...] = jnp.zeros(dst.shape, BF16)
    for b in range(n):
        p, q = divmod(b, 256 // k)
        dst[p, q * k:(q + 1) * k, q * k:(q + 1) * k] = src_ref[b].astype(BF16)


def _diag_pack(w, k):
    lane = lax.broadcasted_iota(jnp.int32, (k, 256), 1)
    out = w[0:k]
    for q in range(1, 256 // k):
        out = jnp.where(lane >= q * k, w[q * k:(q + 1) * k], out)
    return out


def _mixer_pre(e_lru, e_pool, pv, wa_ref, wx_ref, wp_ref, tm, t0):
    xc = pv[ROW_CB:ROW_CB + 1, :]
    for k in range(4):
        xc = xc + e_lru[pl.ds(HALO - 3 + k, tm), :] * pv[ROW_CW + k:ROW_CW + k + 1, :]
    xcb = xc.astype(BF16)
    r = _sigmoid(_bd(xcb, wa_ref) + pv[ROW_BA:ROW_BA + 1, :])
    ig = _sigmoid(_bd(xcb, wx_ref) + pv[ROW_BX:ROW_BX + 1, :])
    z = -pv[ROW_LAM:ROW_LAM + 1, :]
    sp = jnp.maximum(z, 0.0) + jnp.log(1.0 + jnp.exp(-jnp.abs(z)))
    la = (-LRU_C * r) * sp
    a = jnp.exp(la)
    om = -jnp.tanh(la) * (1.0 + a * a)
    mult = jnp.sqrt(jnp.maximum(om, 1e-12))
    t = t0 + lax.broadcasted_iota(jnp.int32, (tm, 1), 0)
    parts, cnts = [], []
    for g, w in enumerate(POOL_WINDOWS):
        ext = e_pool[:, pl.ds(128 * g, 128)]
        s = _window_sum(ext, w, back=True)[HALO:, :]
        cnt = jnp.minimum(t + 1, w).astype(F32)
        cnts.append(cnt)
        parts.append(s / cnt - ext[HALO:, :])
    pooled = jnp.concatenate(parts, axis=1)
    pooled_b = pooled.astype(BF16)
    zp = _bd(pooled_b, wp_ref) + pv[ROW_PB:ROW_PB + 1, :]
    return dict(xc=xc, xcb=xcb, r=r, ig=ig, sp=sp, a=a, om=om, mult=mult, pooled_b=pooled_b, zp=zp, cnts=cnts)


def _scan_tile(a_ref, b_ref, out_ref, carry, tm, reverse):
    row = lax.broadcasted_iota(jnp.int32, (8, LRU_W), 0)
    nblk = tm // 8

    def step(i, hin):
        blk = (nblk - 1 - i) if reverse else i
        r0 = pl.multiple_of(blk * 8, 8)
        av = a_ref[pl.ds(r0, 8), :]
        bv = b_ref[pl.ds(r0, 8), :]
        for d in (1, 2, 4):
            sh = (8 - d) if reverse else d
            a_s = pltpu.roll(av, sh, 0)
            b_s = pltpu.roll(bv, sh, 0)
            m = (row < 8 - d) if reverse else (row >= d)
            bv = jnp.where(m, av * b_s + bv, bv)
            av = jnp.where(m, av * a_s, av)
        hv = av * hin + bv
        out_ref[pl.ds(r0, 8), :] = hv
        edge = hv[0:1, :] if reverse else hv[7:8, :]
        return jnp.broadcast_to(edge, (8, LRU_W))

    return lax.fori_loop(0, nblk, step, carry)


MESH = pl.DeviceIdType.MESH
ANY = pl.BlockSpec(memory_space=pl.ANY)


def _place():
    x, y, c = lax.axis_index("x"), lax.axis_index("y"), lax.axis_index("c")
    chips = [(1 - x, y), (x, 1 - y), (1 - x, 1 - y)]
    return x, y, c, chips


class _Gather:
    def __init__(self, ins, outs, send_sems, recv_sems, local_sems):
        self.ins, self.outs, self.n = ins, outs, len(ins)
        self.send_sems, self.recv_sems, self.local_sems = send_sems, recv_sems, local_sems

    @staticmethod
    def scratch(n):
        return [pltpu.SemaphoreType.DMA((7, n)), pltpu.SemaphoreType.DMA((7, n)), pltpu.SemaphoreType.DMA((n,))]

    def _slot(self, a, px, py, pc):
        return self.outs[a].at[4 * px + 2 * py + pc]

    def _copy(self, a, k, block, to, src=None):
        return pltpu.make_async_remote_copy(
            src_ref=self._slot(a, *block) if src is None else src, dst_ref=self._slot(a, *block),
            send_sem=self.send_sems.at[k, a], recv_sem=self.recv_sems.at[k, a], device_id=to, device_id_type=MESH)

    def _mine(self, a):
        x, y, c, _ = _place()
        return pltpu.make_async_copy(self.ins[a], self._slot(a, x, y, c), self.local_sems.at[a])

    def _first(self, a):
        x, y, c, chips = _place()
        me = (x, y, c)
        return ([self._copy(a, 0, me, (x, y, 1 - c), src=self.ins[a])]
                + [self._copy(a, 1 + j, me, (*chip, c), src=self.ins[a]) for j, chip in enumerate(chips)])

    def start(self):
        for a in range(self.n):
            self._mine(a).start()
        for a in range(self.n):
            for cp in self._first(a):
                cp.start()

    def finish(self):
        x, y, c, chips = _place()
        me, sibling = (x, y, c), (x, y, 1 - c)
        passed = []
        for j, chip in enumerate(chips):
            for a in range(self.n):
                self._copy(a, 1 + j, (*chip, c), me).wait_recv()
                fwd = self._copy(a, 4 + j, (*chip, c), sibling)
                fwd.start()
                passed.append(fwd)
        for a in range(self.n):
            self._copy(a, 0, (x, y, 1 - c), me).wait_recv()
            for j, chip in enumerate(chips):
                self._copy(a, 4 + j, (*chip, 1 - c), me).wait_recv()
        for a in range(self.n):
            for cp in self._first(a):
                cp.wait_send()
        for cp in passed:
            cp.wait_send()
        for a in range(self.n):
            self._mine(a).wait()


def _all_gather(arrs, name):
    n = len(arrs)

    def body(*refs):
        g = _Gather(refs[:n], refs[n:2 * n], *refs[2 * n:])
        g.start()
        g.finish()

    return pl.pallas_call(
        body, name=name,
        out_shape=[_sds((N_DEV,) + a.shape, a.dtype) for a in arrs],
        in_specs=[ANY] * n, out_specs=[ANY] * n, scratch_shapes=_Gather.scratch(n),
    )(*arrs)


def _pair_exchange(arrs, name):
    n = len(arrs)

    def body(*refs):
        ins, outs = refs[:n], refs[n:2 * n]
        send_sems, recv_sems = refs[2 * n:]
        x, y, c, _ = _place()
        sibling = (x, y, 1 - c)
        sends = []
        for a in range(n):
            for j in range(4):
                cp = pltpu.make_async_remote_copy(
                    src_ref=ins[a].at[2 * j + (1 - c)], dst_ref=outs[a].at[j],
                    send_sem=send_sems.at[j, a], recv_sem=recv_sems.at[j, a], device_id=sibling, device_id_type=MESH)
                cp.start()
                sends.append(cp)
        for cp in sends:
            cp.wait()

    return pl.pallas_call(
        body, name=name,
        out_shape=[_sds((4,) + a.shape[1:], a.dtype) for a in arrs],
        in_specs=[ANY] * n, out_specs=[ANY] * n,
        scratch_shapes=[pltpu.SemaphoreType.DMA((4, n)), pltpu.SemaphoreType.DMA((4, n))],
    )(*arrs)


class _ChipExchange:
    def __init__(self, ins, outs, send_sems, recv_sems, local_sems):
        self.ins, self.outs, self.n = ins, outs, len(ins)
        self.send_sems, self.recv_sems, self.local_sems = send_sems, recv_sems, local_sems

    @staticmethod
    def scratch(n):
        return [pltpu.SemaphoreType.DMA((3, n)), pltpu.SemaphoreType.DMA((3, n)), pltpu.SemaphoreType.DMA((n,))]

    def _local(self, a):
        x, y, _, _ = _place()
        me = 2 * x + y
        return pltpu.make_async_copy(self.ins[a].at[me], self.outs[a].at[me], self.local_sems.at[a])

    def _copies(self, a):
        x, y, c, chips = _place()
        me = 2 * x + y
        return [(pltpu.make_async_remote_copy(
                     src_ref=self.ins[a].at[2 * px + py], dst_ref=self.outs[a].at[me],
                     send_sem=self.send_sems.at[k, a], recv_sem=self.recv_sems.at[k, a],
                     device_id=(px, py, c), device_id_type=MESH),
                 pltpu.make_async_remote_copy(
                     src_ref=self.ins[a].at[me], dst_ref=self.outs[a].at[2 * px + py],
                     send_sem=self.send_sems.at[k, a], recv_sem=self.recv_sems.at[k, a],
                     device_id=(px, py, c), device_id_type=MESH))
                for k, (px, py) in enumerate(chips)]

    def start(self):
        for a in range(self.n):
            self._local(a).start()
        for a in range(self.n):
            for send, _ in self._copies(a):
                send.start()

    def finish(self):
        for a in range(self.n):
            for send, recv in self._copies(a):
                send.wait_send()
                recv.wait_recv()
        for a in range(self.n):
            self._local(a).wait()


def _chip_exchange(arrs, name):
    n = len(arrs)

    def body(*refs):
        e = _ChipExchange(refs[:n], refs[n:2 * n], *refs[2 * n:])
        e.start()
        e.finish()

    return pl.pallas_call(
        body, name=name, out_shape=[_sds(a.shape, a.dtype) for a in arrs],
        in_specs=[ANY] * n, out_specs=[ANY] * n, scratch_shapes=_ChipExchange.scratch(n),
    )(*arrs)


def _mix_in(x, g_mix, w_in_t, tm, shards=()):
    T = x.shape[0]
    n_t = T // tm
    n_s = len(shards)

    def body(x_ref, g_ref, w_ref, *rest):
        sh_in, rest = rest[:n_s], rest[n_s:]
        u_ref, h_ref = rest[:2]
        gather = _Gather(sh_in, rest[2:2 + n_s], *rest[2 + n_s:]) if n_s else None
        i = pl.program_id(0)

        if gather:
            @pl.when(i == 0)
            def _():
                gather.start()

        xv = x_ref[...]
        h = (xv * _rstd(xv) * g_ref[...]).astype(BF16)
        h_ref[...] = h
        u_ref[...] = lax.dot_general(h, w_ref[...], NT, preferred_element_type=F32)

        if gather:
            @pl.when(i == n_t - 1)
            def _():
                gather.finish()

    outs = pl.pallas_call(
        body, name="mix_in", grid=(n_t,),
        in_specs=[pl.BlockSpec((tm, D_MODEL), lambda i: (i, 0)),
                  pl.BlockSpec((1, D_MODEL), lambda i: (0, 0)),
                  pl.BlockSpec((D_IN, D_MODEL), lambda i: (0, 0))] + [ANY] * n_s,
        out_specs=[pl.BlockSpec((tm, D_IN), lambda i: (i, 0)),
                   pl.BlockSpec((tm, D_MODEL), lambda i: (i, 0))] + [ANY] * n_s,
        out_shape=[_sds((T, D_IN), F32), _sds((T, D_MODEL), BF16)] + [_sds((N_DEV,) + a.shape, a.dtype) for a in shards],
        scratch_shapes=_Gather.scratch(n_s) if n_s else [],
        compiler_params=pltpu.CompilerParams(dimension_semantics=("arbitrary",)),
    )(x, g_mix, w_in_t, *shards)
    return outs[0], outs[1], list(outs[2:])


def _mixer_fwd(u, pv, wa, wx, wp, tm, shards=()):
    T = u.shape[0]
    n_s = len(shards)
    n_t = T // tm

    def body(u_ref, pv_ref, wa_in, wx_in, wp_in, *rest):
        sh_in, rest = rest[:n_s], rest[n_s:]
        y_ref, hs_ref = rest[:2]
        sh_out, rest = rest[2:2 + n_s], rest[2 + n_s:]
        e_lru, e_pool, a_s, b_s, hc, wa_ref, wx_ref, wp_ref = rest[:8]
        gather = _Gather(sh_in, sh_out, *rest[8:]) if n_s else None
        i = pl.program_id(0)

        @pl.when(i == 0)
        def _():
            if gather:
                gather.start()
            e_lru[pl.ds(0, HALO), :] = jnp.zeros((HALO, LRU_W), F32)
            e_pool[pl.ds(0, HALO), :] = jnp.zeros((HALO, POOL_W), F32)
            hc[...] = jnp.zeros((8, LRU_W), F32)
            _fill_block_diag(wa_ref, wa_in)
            _fill_block_diag(wx_ref, wx_in)
            _fill_block_diag(wp_ref, wp_in)

        e_lru[pl.ds(HALO, tm), :] = u_ref[:, 0:LRU_W]
        e_pool[pl.ds(HALO, tm), :] = u_ref[:, 2 * LRU_W:D_IN]
        pv = pv_ref[...]
        p = _mixer_pre(e_lru, e_pool, pv, wa_ref, wx_ref, wp_ref, tm, i * tm)
        a_s[...] = p["a"]
        b_s[...] = p["mult"] * (p["ig"] * p["xc"])
        hc[...] = _scan_tile(a_s, b_s, hs_ref, hc[...], tm, reverse=False)
        gl, _ = _gelu_parts(u_ref[:, LRU_W:2 * LRU_W])
        y_lru = hs_ref[...] * gl
        y_ref[:, 0:LRU_W] = (y_lru * _rstd(y_lru) * pv[ROW_GL:ROW_GL + 1, :]).astype(BF16)
        y_pool = p["zp"] * pv[ROW_PS:ROW_PS + 1, :]
        y_ref[:, LRU_W:D_MODEL] = (y_pool * _rstd(y_pool) * pv[ROW_GP:ROW_GP + 1, :]).astype(BF16)
        e_lru[pl.ds(0, HALO), :] = e_lru[pl.ds(tm, HALO), :]
        e_pool[pl.ds(0, HALO), :] = e_pool[pl.ds(tm, HALO), :]

        if gather:
            @pl.when(i == n_t - 1)
            def _():
                gather.finish()

    full = lambda shape: pl.BlockSpec(shape, lambda i: (0,) * len(shape))
    outs = pl.pallas_call(
        body, name="mixer_fwd", grid=(n_t,),
        in_specs=[pl.BlockSpec((tm, D_IN), lambda i: (i, 0)), full((16, LRU_W)),
                  full((8, 64, 64)), full((8, 64, 64)), full((4, 128, 128))] + [ANY] * n_s,
        out_specs=[pl.BlockSpec((tm, D_MODEL), lambda i: (i, 0)), pl.BlockSpec((tm, LRU_W), lambda i: (i, 0))] + [ANY] * n_s,
        out_shape=[_sds((T, D_MODEL), BF16), _sds((T, LRU_W), F32)] + [_sds((N_DEV,) + a.shape, a.dtype) for a in shards],
        scratch_shapes=[pltpu.VMEM((HALO + tm, LRU_W), F32), pltpu.VMEM((HALO + tm, POOL_W), F32),
                        pltpu.VMEM((tm, LRU_W), F32), pltpu.VMEM((tm, LRU_W), F32), pltpu.VMEM((8, LRU_W), F32)]
        + [pltpu.VMEM((2, 256, 256), BF16)] * 3 + (_Gather.scratch(n_s) if n_s else []),
        compiler_params=pltpu.CompilerParams(dimension_semantics=("arbitrary",)),
    )(u, pv, wa, wx, wp, *shards)
    return outs[0], outs[1], list(outs[2:])


def _ffn_fwd(x, y, w_out_b, g_ffn, w1_b, w3_b, w2_b, g_fin, tgt, tm, tn):
    T = x.shape[0]
    n_j = D_FF // tn

    def body(x_ref, y_ref, wo_ref, gf_ref, w1_ref, w3_ref, w2_ref, gfin_ref, tgt_ref,
             hres_ref, h2_ref, g_ref, v_ref, d3_ref, loss_ref, dgfin_ref, acc):
        i, j = pl.program_id(0), pl.program_id(1)

        @pl.when(j == 0)
        def _():
            hr = x_ref[...] + jnp.dot(y_ref[...], wo_ref[...], preferred_element_type=F32)
            hres_ref[...] = hr
            h2_ref[...] = (hr * _rstd(hr) * gf_ref[...]).astype(BF16)
            acc[...] = jnp.zeros((tm, D_MODEL), F32)

        @pl.when((j == 0) & (i == 0))
        def _():
            loss_ref[...] = jnp.zeros((8, 128), F32)
            dgfin_ref[...] = jnp.zeros((1, D_MODEL), F32)

        h2 = h2_ref[...]
        g = lax.dot_general(h2, w1_ref[...], NT, preferred_element_type=F32)
        v = lax.dot_general(h2, w3_ref[...], NT, preferred_element_type=F32)
        g_ref[...] = g.astype(BF16)
        v_ref[...] = v.astype(BF16)
        ff = ((g * _sigmoid(g)) * v).astype(BF16)
        acc[...] += jnp.dot(ff, w2_ref[...], preferred_element_type=F32)

        @pl.when(j == n_j - 1)
        def _():
            h3 = hres_ref[...] + acc[...]
            rstd = _rstd(h3)
            xh = h3 * rstd
            gfin = gfin_ref[...]
            err = xh * gfin - tgt_ref[...]
            loss_ref[...] += 0.5 * jnp.sum(jnp.mean(err * err, axis=-1, keepdims=True))
            dout = err * (1.0 / D_MODEL)
            dx, dgain = _rms_bwd(dout, xh, rstd, gfin)
            d3_ref[...] = dx
            dgfin_ref[...] += dgain

    row = lambda w: pl.BlockSpec((tm, w), lambda i, j: (i, 0))
    const = lambda shape: pl.BlockSpec(shape, lambda i, j: (0,) * len(shape))
    return pl.pallas_call(
        body, name="ffn_fwd", grid=(T // tm, n_j),
        in_specs=[row(D_MODEL), row(D_MODEL), const((D_MODEL, D_MODEL)), const((1, D_MODEL)),
                  pl.BlockSpec((tn, D_MODEL), lambda i, j: (j, 0)), pl.BlockSpec((tn, D_MODEL), lambda i, j: (j, 0)),
                  pl.BlockSpec((tn, D_MODEL), lambda i, j: (j, 0)), const((1, D_MODEL)), row(D_MODEL)],
        out_specs=[row(D_MODEL), row(D_MODEL),
                   pl.BlockSpec((tm, tn), lambda i, j: (i, j)), pl.BlockSpec((tm, tn), lambda i, j: (i, j)),
                   row(D_MODEL), const((8, 128)), const((1, D_MODEL))],
        out_shape=[_sds((T, D_MODEL), F32), _sds((T, D_MODEL), BF16), _sds((T, D_FF), BF16), _sds((T, D_FF), BF16),
                   _sds((T, D_MODEL), F32), _sds((8, 128), F32), _sds((1, D_MODEL), F32)],
        scratch_shapes=[pltpu.VMEM((tm, D_MODEL), F32)],
        compiler_params=pltpu.CompilerParams(dimension_semantics=("arbitrary", "arbitrary")),
    )(x, y, w_out_b, g_ffn, w1_b, w3_b, w2_b, g_fin, tgt)


def _ffn_bwd(d3, g, v, w1_b, w3_b, w2_b, hres, g_ffn, tm, tn):
    T = d3.shape[0]
    n_j = D_FF // tn

    def body(d3_ref, g_ref, v_ref, w1_ref, w3_ref, w2_ref, hres_ref, gf_ref,
             dg_ref, dv_ref, ff_ref, d2_ref, dgffn_ref, acc):
        i, j = pl.program_id(0), pl.program_id(1)

        @pl.when(j == 0)
        def _():
            acc[...] = jnp.zeros((tm, D_MODEL), F32)

        @pl.when((j == 0) & (i == 0))
        def _():
            dgffn_ref[...] = jnp.zeros((1, D_MODEL), F32)

        dff = lax.dot_general(d3_ref[...].astype(BF16), w2_ref[...], NT, preferred_element_type=F32)
        gv = g_ref[...].astype(F32)
        vv = v_ref[...].astype(F32)
        sg = _sigmoid(gv)
        sl = gv * sg
        dgb = (dff * vv * (sg * (1.0 + gv * (1.0 - sg)))).astype(BF16)
        dvb = (dff * sl).astype(BF16)
        dg_ref[...] = dgb
        dv_ref[...] = dvb
        ff_ref[...] = (sl * vv).astype(BF16)
        acc[...] += (jnp.dot(dgb, w1_ref[...], preferred_element_type=F32)
                     + jnp.dot(dvb, w3_ref[...], preferred_element_type=F32))

        @pl.when(j == n_j - 1)
        def _():
            hr = hres_ref[...]
            rstd = _rstd(hr)
            dx, dgain = _rms_bwd(acc[...], hr * rstd, rstd, gf_ref[...])
            d2_ref[...] = d3_ref[...] + dx
            dgffn_ref[...] += dgain

    row = lambda w: pl.BlockSpec((tm, w), lambda i, j: (i, 0))
    tile = pl.BlockSpec((tm, tn), lambda i, j: (i, j))
    const = lambda shape: pl.BlockSpec(shape, lambda i, j: (0,) * len(shape))
    return pl.pallas_call(
        body, name="ffn_bwd", grid=(T // tm, n_j),
        in_specs=[row(D_MODEL), tile, tile,
                  pl.BlockSpec((tn, D_MODEL), lambda i, j: (j, 0)), pl.BlockSpec((tn, D_MODEL), lambda i, j: (j, 0)),
                  pl.BlockSpec((tn, D_MODEL), lambda i, j: (j, 0)), row(D_MODEL), const((1, D_MODEL))],
        out_specs=[tile, tile, tile, row(D_MODEL), const((1, D_MODEL))],
        out_shape=[_sds((T, D_FF), BF16), _sds((T, D_FF), BF16), _sds((T, D_FF), BF16),
                   _sds((T, D_MODEL), F32), _sds((1, D_MODEL), F32)],
        scratch_shapes=[pltpu.VMEM((tm, D_MODEL), F32)],
        compiler_params=pltpu.CompilerParams(dimension_semantics=("arbitrary", "arbitrary")),
    )(d3, g, v, w1_b, w3_b, w2_b, hres, g_ffn)


def _at_b(a, b, name, tmm, tn, tk, gather=()):
    T, M = a.shape
    N = b.shape[1]
    n_m, n_n, n_k = M // tmm, N // tn, T // tk
    n_g = len(gather)

    def body(a_ref, b_ref, *rest):
        g_in, o_ref, rest = rest[:n_g], rest[n_g], rest[n_g + 1:]
        ag = _Gather(g_in, rest[:n_g], *rest[n_g:]) if n_g else None
        m, n, k = pl.program_id(0), pl.program_id(1), pl.program_id(2)

        if ag:
            @pl.when((m == 0) & (n == 0) & (k == 0))
            def _():
                ag.start()

        @pl.when(k == 0)
        def _():
            o_ref[...] = jnp.zeros((tmm, tn), F32)

        o_ref[...] += lax.dot_general(a_ref[...].astype(BF16), b_ref[...].astype(BF16), TN,
                                      preferred_element_type=F32)

        if ag:
            @pl.when((m == n_m - 1) & (n == n_n - 1) & (k == n_k - 1))
            def _():
                ag.finish()

    outs = pl.pallas_call(
        body, name=name, grid=(n_m, n_n, n_k),
        in_specs=[pl.BlockSpec((tk, tmm), lambda m, n, k: (k, m)), pl.BlockSpec((tk, tn), lambda m, n, k: (k, n))]
        + [ANY] * n_g,
        out_specs=[pl.BlockSpec((tmm, tn), lambda m, n, k: (m, n))] + [ANY] * n_g,
        out_shape=[_sds((M, N), F32)] + [_sds((N_DEV,) + g.shape, g.dtype) for g in gather],
        scratch_shapes=_Gather.scratch(n_g) if n_g else [],
        compiler_params=pltpu.CompilerParams(
            dimension_semantics=("arbitrary",) * 3 if n_g else ("parallel", "parallel", "arbitrary")),
    )(a, b, *gather)
    return (outs[0], list(outs[1:])) if n_g else outs[0]


def _mixer_bwd(d2, u, hs, pv, wa, wx, wp, w_out_b, tm, chip_sums=()):
    T = u.shape[0]
    n_t = T // tm
    n_x = len(chip_sums)

    def body(d2_ref, u_ref, uh_ref, hs_ref, hh_ref, pv_ref, wa_in, wx_in, wp_in, wo_ref, *rest):
        x_in, rest = rest[:n_x], rest[n_x:]
        du_ref, sg_ref = rest[:2]
        x_out, rest = rest[2:2 + n_x], rest[2 + n_x:]
        e_lru, e_pool, e_h, a_s, b_s, mu_s, f_x, f_p, mc, cx, cp = rest[:11]
        wa_ref, wx_ref, wp_ref, vacc_ref, dwa_ref, dwx_ref, dwp_ref = rest[11:18]
        exchange = _ChipExchange(x_in, x_out, *rest[18:]) if n_x else None
        s = pl.program_id(0)
        it = n_t - 1 - s

        @pl.when(s == 0)
        def _():
            if exchange:
                exchange.start()
            mc[...] = jnp.zeros((8, LRU_W), F32)
            cx[...] = jnp.zeros((8, LRU_W), F32)
            cp[...] = jnp.zeros((HALO, POOL_W), F32)
            vacc_ref[...] = jnp.zeros((16, LRU_W), F32)
            dwa_ref[...] = jnp.zeros((2, 256, 256), F32)
            dwx_ref[...] = jnp.zeros((2, 256, 256), F32)
            dwp_ref[...] = jnp.zeros((2, 256, 256), F32)
            _fill_block_diag(wa_ref, wa_in)
            _fill_block_diag(wx_ref, wx_in)
            _fill_block_diag(wp_ref, wp_in)

        first = it == 0
        e_lru[pl.ds(0, HALO), :] = jnp.where(first, 0.0, uh_ref[:, 0:LRU_W])
        e_pool[pl.ds(0, HALO), :] = jnp.where(first, 0.0, uh_ref[:, 2 * LRU_W:D_IN])
        e_lru[pl.ds(HALO, tm), :] = u_ref[:, 0:LRU_W]
        e_pool[pl.ds(HALO, tm), :] = u_ref[:, 2 * LRU_W:D_IN]
        e_h[pl.ds(0, 8), :] = jnp.where(first, 0.0, hh_ref[...])
        e_h[pl.ds(8, tm), :] = hs_ref[...]
        pv = pv_ref[...]
        p = _mixer_pre(e_lru, e_pool, pv, wa_ref, wx_ref, wp_ref, tm, it * tm)
        a, xc, ig, r, mult = p["a"], p["xc"], p["ig"], p["r"], p["mult"]

        dyn = lax.dot_general(d2_ref[...].astype(BF16), wo_ref[...], NT, preferred_element_type=F32)

        h = hs_ref[...]
        ug = u_ref[:, LRU_W:2 * LRU_W]
        gl, dgl = _gelu_parts(ug)
        y_lru = h * gl
        rstd_l = _rstd(y_lru)
        dy_lru, d_gain_l = _rms_bwd(dyn[:, 0:LRU_W], y_lru * rstd_l, rstd_l, pv[ROW_GL:ROW_GL + 1, :])
        dh = dy_lru * gl
        du_ref[:, LRU_W:2 * LRU_W] = (dy_lru * h * dgl).astype(BF16)
        a_s[...] = a
        b_s[...] = a * dh
        mu_s[pl.ds(tm, 8), :] = mc[...]
        mc[...] = _scan_tile(a_s, b_s, mu_s, mc[...], tm, reverse=True)
        lam_t = dh + mu_s[pl.ds(1, tm), :]
        da = lam_t * e_h[pl.ds(7, tm), :]
        dmult = lam_t * (ig * xc)
        di = lam_t * (mult * xc)
        dxc = lam_t * (mult * ig)
        dla = da * a + jnp.where(p["om"] > 1e-12, dmult * (-(a * a) / mult), 0.0)
        dra = (dla * (-LRU_C * p["sp"])) * (r * (1.0 - r))
        dia = di * (ig * (1.0 - ig))
        drab = dra.astype(BF16)
        diab = dia.astype(BF16)
        dxc = dxc + _bd_t(drab, wa_ref) + _bd_t(diab, wx_ref)
        dwa_ref[...] += _bd_grad(p["xcb"], drab)
        dwx_ref[...] += _bd_grad(p["xcb"], diab)
        sig_neg_lam = _sigmoid(-pv[ROW_LAM:ROW_LAM + 1, :])
        d_lam = jnp.sum(dla * r, axis=0, keepdims=True) * (LRU_C * sig_neg_lam)

        f_x[pl.ds(0, tm), :] = dxc
        f_x[pl.ds(tm, 8), :] = cx[...]
        du_lru = jnp.zeros((tm, LRU_W), F32)
        d_cw = []
        for k in range(4):
            du_lru = du_lru + f_x[pl.ds(3 - k, tm), :] * pv[ROW_CW + k:ROW_CW + k + 1, :]
            d_cw.append(jnp.sum(dxc * e_lru[pl.ds(HALO - 3 + k, tm), :], axis=0, keepdims=True))
        du_ref[:, 0:LRU_W] = du_lru.astype(BF16)
        cx[...] = f_x[pl.ds(0, 8), :]

        zp = p["zp"]
        ps = pv[ROW_PS:ROW_PS + 1, :]
        y_pool = zp * ps
        rstd_p = _rstd(y_pool)
        dy_pool, d_gain_p = _rms_bwd(dyn[:, LRU_W:D_MODEL], y_pool * rstd_p, rstd_p, pv[ROW_GP:ROW_GP + 1, :])
        dz = dy_pool * ps
        dzb = dz.astype(BF16)
        dwp_ref[...] += _bd_grad(p["pooled_b"], dzb)
        dpooled = _bd_t(dzb, wp_ref)
        for g, w in enumerate(POOL_WINDOWS):
            f_p[pl.ds(0, tm), pl.ds(128 * g, 128)] = dpooled[:, 128 * g:128 * (g + 1)] / p["cnts"][g]
        f_p[pl.ds(tm, HALO), :] = cp[...]
        for g, w in enumerate(POOL_WINDOWS):
            acc = _window_sum(f_p[:, pl.ds(128 * g, 128)], w, back=False)[0:tm, :]
            du_ref[:, 2 * LRU_W + 128 * g:2 * LRU_W + 128 * (g + 1)] = (
                acc - dpooled[:, 128 * g:128 * (g + 1)]).astype(BF16)
        cp[...] = f_p[pl.ds(0, HALO), :]

        rows = d_cw + [
            jnp.sum(dxc, axis=0, keepdims=True),
            jnp.sum(dra, axis=0, keepdims=True),
            jnp.sum(dia, axis=0, keepdims=True),
            d_lam,
            jnp.sum(dz, axis=0, keepdims=True),
            jnp.sum(dy_pool * zp, axis=0, keepdims=True),
            d_gain_l, d_gain_p,
            jnp.zeros((4, LRU_W), F32),
        ]
        vacc_ref[...] += jnp.concatenate(rows, axis=0)

        @pl.when(s == n_t - 1)
        def _():
            sg_ref[SG_VEC:SG_VEC + 16, :] = vacc_ref[:, 0:256]
            sg_ref[SG_VEC + 16:SG_VEC + 32, :] = vacc_ref[:, 256:512]
            for half in range(2):
                sg_ref[SG_WA + 64 * half:SG_WA + 64 * (half + 1), :] = _diag_pack(dwa_ref[half], 64)
                sg_ref[SG_WX + 64 * half:SG_WX + 64 * (half + 1), :] = _diag_pack(dwx_ref[half], 64)
                sg_ref[SG_WP + 128 * half:SG_WP + 128 * (half + 1), :] = _diag_pack(dwp_ref[half], 128)
            if exchange:
                exchange.finish()

    rev = lambda w: pl.BlockSpec((tm, w), lambda s: (n_t - 1 - s, 0))
    full = lambda shape: pl.BlockSpec(shape, lambda s: (0,) * len(shape))
    outs = pl.pallas_call(
        body, name="mixer_bwd", grid=(n_t,),
        in_specs=[rev(D_MODEL), rev(D_IN),
                  pl.BlockSpec((HALO, D_IN), lambda s: (jnp.maximum((n_t - 1 - s) * (tm // HALO) - 1, 0), 0)),
                  rev(LRU_W),
                  pl.BlockSpec((8, LRU_W), lambda s: (jnp.maximum((n_t - 1 - s) * (tm // 8) - 1, 0), 0)),
                  full((16, LRU_W)), full((8, 64, 64)), full((8, 64, 64)), full((4, 128, 128)),
                  full((D_MODEL, D_MODEL))] + [ANY] * n_x,
        out_specs=[rev(D_IN), full((SG_ROWS, 256))] + [ANY] * n_x,
        out_shape=[_sds((T, D_IN), BF16), _sds((SG_ROWS, 256), F32)] + [_sds(a.shape, a.dtype) for a in chip_sums],
        scratch_shapes=[pltpu.VMEM((HALO + tm, LRU_W), F32), pltpu.VMEM((HALO + tm, POOL_W), F32),
                        pltpu.VMEM((8 + tm, LRU_W), F32), pltpu.VMEM((tm, LRU_W), F32), pltpu.VMEM((tm, LRU_W), F32),
                        pltpu.VMEM((tm + 8, LRU_W), F32), pltpu.VMEM((tm + 8, LRU_W), F32),
                        pltpu.VMEM((tm + HALO, POOL_W), F32), pltpu.VMEM((8, LRU_W), F32),
                        pltpu.VMEM((8, LRU_W), F32), pltpu.VMEM((HALO, POOL_W), F32)]
        + [pltpu.VMEM((2, 256, 256), BF16)] * 3 + [pltpu.VMEM((16, LRU_W), F32)] + [pltpu.VMEM((2, 256, 256), F32)] * 3
        + (_ChipExchange.scratch(n_x) if n_x else []),
        compiler_params=pltpu.CompilerParams(dimension_semantics=("arbitrary",)),
    )(d2, u, u, hs, hs, pv, wa, wx, wp, w_out_b, *chip_sums)
    return outs[0], outs[1], list(outs[2:])


def _mix_in_bwd(du, x, d2, w_in_t, g_mix, tm, chip_sums=()):
    T = x.shape[0]
    n_t = T // tm
    n_x = len(chip_sums)

    def body(du_ref, x_ref, d2_ref, w_ref, g_ref, *rest):
        x_in, rest = rest[:n_x], rest[n_x:]
        dx_ref, dg_ref = rest[:2]
        exchange = _ChipExchange(x_in, rest[2:2 + n_x], *rest[2 + n_x:]) if n_x else None
        i = pl.program_id(0)

        @pl.when(i == 0)
        def _():
            dg_ref[...] = jnp.zeros((1, D_MODEL), F32)
            if exchange:
                exchange.start()

        dh = jnp.dot(du_ref[...], w_ref[...], preferred_element_type=F32)
        xv = x_ref[...]
        rstd = _rstd(xv)
        dx, dgain = _rms_bwd(dh, xv * rstd, rstd, g_ref[...])
        dx_ref[...] = d2_ref[...] + dx
        dg_ref[...] += dgain

        if exchange:
            @pl.when(i == n_t - 1)
            def _():
                exchange.finish()

    row = lambda w: pl.BlockSpec((tm, w), lambda i: (i, 0))
    const = lambda shape: pl.BlockSpec(shape, lambda i: (0,) * len(shape))
    outs = pl.pallas_call(
        body, name="mix_in_bwd", grid=(n_t,),
        in_specs=[row(D_IN), row(D_MODEL), row(D_MODEL), const((D_IN, D_MODEL)), const((1, D_MODEL))] + [ANY] * n_x,
        out_specs=[row(D_MODEL), const((1, D_MODEL))] + [ANY] * n_x,
        out_shape=[_sds((T, D_MODEL), F32), _sds((1, D_MODEL), F32)] + [_sds(a.shape, a.dtype) for a in chip_sums],
        scratch_shapes=_ChipExchange.scratch(n_x) if n_x else [],
        compiler_params=pltpu.CompilerParams(dimension_semantics=("arbitrary",)),
    )(du, x, d2, w_in_t, g_mix, *chip_sums)
    return outs[0], outs[1], list(outs[2:])


def _pair_sum(g, r1, c_arr, name):
    _, _, R, C = g.shape
    tr = R if R <= 512 else 256

    def body(c_ref, g_ref, r_ref, o_ref):
        o_ref[...] = (g_ref[...] + r_ref[...]).astype(BF16)

    return pl.pallas_call(
        body, name=name,
        grid_spec=pltpu.PrefetchScalarGridSpec(
            num_scalar_prefetch=1, grid=(4, R // tr),
            in_specs=[pl.BlockSpec((None, None, tr, C), lambda j, i, c_ref: (j, c_ref[0], i, 0)),
                      pl.BlockSpec((None, tr, C), lambda j, i, c_ref: (j, i, 0))],
            out_specs=pl.BlockSpec((None, tr, C), lambda j, i, c_ref: (j, i, 0))),
        out_shape=_sds((4, R, C), BF16),
    )(c_arr, g, r1)


def _adamw(w, g, m, v):
    m = ADAM_B1 * m + (1.0 - ADAM_B1) * g
    v = ADAM_B2 * v + (1.0 - ADAM_B2) * (g * g)
    m_hat = m / (1.0 - ADAM_B1 ** ADAM_STEP)
    v_hat = v / (1.0 - ADAM_B2 ** ADAM_STEP)
    delta = -ADAM_LR * (m_hat / (jnp.sqrt(v_hat) + ADAM_EPS) + ADAM_WD * w)
    return delta, m, v


def _adam_shard(w, m, v, parts, name):
    R, C = w.shape
    tr = R if R <= 512 else 256

    def body(w_ref, m_ref, v_ref, p_ref, g_ref, d_ref, nm_ref, nv_ref):
        g = p_ref[0].astype(F32)
        for j in range(1, 4):
            g = g + p_ref[j].astype(F32)
        delta, nm, nv = _adamw(w_ref[...], g, m_ref[...], v_ref[...])
        g_ref[...] = g
        d_ref[...] = delta
        nm_ref[...] = nm
        nv_ref[...] = nv

    blk = pl.BlockSpec((tr, C), lambda i: (i, 0))
    return pl.pallas_call(
        body, name=name, grid=(R // tr,),
        in_specs=[blk, blk, blk, pl.BlockSpec((4, tr, C), lambda i: (0, i, 0))],
        out_specs=[blk] * 4, out_shape=[_sds((R, C), F32)] * 4,
        compiler_params=pltpu.CompilerParams(dimension_semantics=("parallel",)),
    )(w, m, v, parts)


SMALL_PARAMS = [("norm_mix_g", (1, D_MODEL)), ("conv_w", (1, 4, 64)), ("conv_b", (1, LRU_W)),
                ("gate_a_w", (1, 8, 64, 64)), ("gate_a_b", (1, LRU_W)), ("gate_x_w", (1, 8, 64, 64)),
                ("gate_x_b", (1, LRU_W)), ("lru_lambda", (1, LRU_W)), ("pool_w", (1, 4, 128, 128)),
                ("pool_b", (1, POOL_W)), ("pool_scale", (1, POOL_W)), ("norm_lru_g", (1, LRU_W)),
                ("norm_pool_g", (1, POOL_W)), ("norm_ffn_g", (1, D_MODEL)), ("final_norm_g", (1, D_MODEL))]
VEC_ROW = dict(conv_b=ROW_CB, gate_a_b=ROW_BA, gate_x_b=ROW_BX, lru_lambda=ROW_LAM, pool_b=ROW_PB, pool_scale=ROW_PS,
               norm_lru_g=ROW_GL, norm_pool_g=ROW_GP)
WHOLE = (Ellipsis,)


def _unpack_mixer_grads(sg, dev):
    vec = jnp.concatenate([sg[SG_VEC:SG_VEC + 16], sg[SG_VEC + 16:SG_VEC + 32]], axis=1)
    out = {nm: [(WHOLE, vec[r:r + 1])] for nm, r in VEC_ROW.items()}
    own = jnp.zeros((4, 64), F32)
    for d in range(N_DEV):
        own = jnp.where(dev == d, vec[ROW_CW:ROW_CW + 4, 64 * d:64 * (d + 1)], own)
    out["conv_w"] = [((0,), own)]
    for nm, row0 in (("gate_a_w", SG_WA), ("gate_x_w", SG_WX)):
        out[nm] = [((0, b), sg[row0 + 64 * (b // 4):row0 + 64 * (b // 4 + 1), 64 * (b % 4):64 * (b % 4 + 1)])
                   for b in range(8)]
    out["pool_w"] = [((0, b), sg[SG_WP + 128 * (b // 2):SG_WP + 128 * (b // 2 + 1), 128 * (b % 2):128 * (b % 2 + 1)])
                     for b in range(4)]
    return out


def _adam_small(parts, w, m, v):
    names = [nm for nm, _ in SMALL_PARAMS]
    n = len(names)

    def body(sg_ref, gm_ref, gf_ref, gn_ref, ls_ref, *rest):
        w_refs, m_refs, v_refs, outs = rest[:n], rest[n:2 * n], rest[2 * n:3 * n], rest[3 * n:]
        dev = 4 * lax.axis_index("x") + 2 * lax.axis_index("y") + lax.axis_index("c")

        def total(ref):
            acc = ref[0]
            for d in range(1, N_DEV):
                acc = acc + ref[d]
            return acc

        pieces = _unpack_mixer_grads(total(sg_ref), dev)
        pieces["norm_mix_g"] = [(WHOLE, total(gm_ref))]
        pieces["norm_ffn_g"] = [(WHOLE, total(gf_ref))]
        pieces["final_norm_g"] = [(WHOLE, total(gn_ref))]
        for i, nm in enumerate(names):
            for idx, g in pieces[nm]:
                delta, new_m, new_v = _adamw(w_refs[i][idx], g, m_refs[i][idx], v_refs[i][idx])
                for kind, val in enumerate((g, delta, new_m, new_v)):
                    outs[4 * i + kind][idx] = val
        outs[4 * n][...] = total(ls_ref)

    shapes = [_sds(shape, F32) for _, shape in SMALL_PARAMS for _ in range(4)] + [_sds((8, 128), F32)]
    res = pl.pallas_call(body, name="adam_small", out_shape=shapes)(
        *parts, *[w[nm] for nm in names], *[m[nm] for nm in names], *[v[nm] for nm in names])
    return {nm: tuple(res[4 * i:4 * i + 4]) for i, nm in enumerate(names)}, res[4 * n][0, 0]


def _vec_rows(conv_w_full, conv_b, ba, bx, lam, pb, ps, gl, gp):
    return jnp.concatenate([conv_w_full, conv_b, ba, bx, lam, pb, ps, gl, gp, jnp.zeros((4, LRU_W), F32)], axis=0)


WEIGHT_ORDER = ['norm_mix_g', 'w_in', 'conv_w', 'conv_b', 'gate_a_w', 'gate_a_b', 'gate_x_w', 'gate_x_b', 'lru_lambda',
                'pool_w', 'pool_b', 'pool_scale', 'norm_lru_g', 'norm_pool_g', 'w_out', 'norm_ffn_g', 'ffn_w1', 'ffn_w3',
                'ffn_w2', 'final_norm_g']


def kernel(x, norm_mix_g, w_in, conv_w, conv_b, gate_a_w, gate_a_b, gate_x_w, gate_x_b, lru_lambda, pool_w, pool_b, pool_scale, norm_lru_g, norm_pool_g, w_out, norm_ffn_g, ffn_w1, ffn_w3, ffn_w2, final_norm_g, loss_target, m_norm_mix_g, m_w_in, m_conv_w, m_conv_b, m_gate_a_w, m_gate_a_b, m_gate_x_w, m_gate_x_b, m_lru_lambda, m_pool_w, m_pool_b, m_pool_scale, m_norm_lru_g, m_norm_pool_g, m_w_out, m_norm_ffn_g, m_ffn_w1, m_ffn_w3, m_ffn_w2, m_final_norm_g, v_norm_mix_g, v_w_in, v_conv_w, v_conv_b, v_gate_a_w, v_gate_a_b, v_gate_x_w, v_gate_x_b, v_lru_lambda, v_pool_w, v_pool_b, v_pool_scale, v_norm_lru_g, v_norm_pool_g, v_w_out, v_norm_ffn_g, v_ffn_w1, v_ffn_w3, v_ffn_w2, v_final_norm_g):
    ac = lax.axis_index("c")
    tm, tmx, tn, tk = 512, 256, 1408, 512
    xs, tgt = x[0], loss_target[0]
    g_fin = final_norm_g.reshape(1, D_MODEL)
    c_arr = jnp.reshape(ac, (1,)).astype(jnp.int32)

    def pair_sums(blocks, names):
        from_sibling = _pair_exchange(blocks, "grads_to_sibling_" + names[0])
        return [_pair_sum(g.reshape((4, 2) + g.shape[1:]), r, c_arr, "pair_sum_" + nm)
                for g, r, nm in zip(blocks, from_sibling, names)]

    tr = lambda w: jnp.swapaxes(w[0], 0, 1)
    own = lambda w: w[0]
    bf = lambda a: a.astype(BF16)

    g_in, g_conv = _all_gather([bf(tr(w_in)), conv_w[0]], "gather_w_in")
    w_in_t = g_in.reshape(D_IN, D_MODEL)
    conv_w_full = g_conv.transpose(1, 0, 2).reshape(4, LRU_W)
    pv = _vec_rows(conv_w_full, conv_b, gate_a_b, gate_x_b, lru_lambda, pool_b, pool_scale, norm_lru_g, norm_pool_g)
    wa, wx, wp = gate_a_w[0], gate_x_w[0], pool_w[0]

    u, h1, (g_out, g_w1) = _mix_in(xs, norm_mix_g, w_in_t, tm, shards=[bf(own(w_out)), bf(tr(ffn_w1))])
    y, hs, (g_w3, g_w2) = _mixer_fwd(u, pv, wa, wx, wp, tmx, shards=[bf(tr(ffn_w3)), bf(own(ffn_w2))])
    w_out_b = g_out.reshape(D_MODEL, D_MODEL)
    w1_t, w3_t, w2_b = g_w1.reshape(D_FF, D_MODEL), g_w3.reshape(D_FF, D_MODEL), g_w2.reshape(D_FF, D_MODEL)
    hres, h2, g, v, d3, loss_acc, d_gfin = _ffn_fwd(xs, y, w_out_b, norm_ffn_g, w1_t, w3_t, w2_b, g_fin, tgt, tm, tn)

    dg, dv, ff, d2, d_gffn = _ffn_bwd(d3, g, v, w1_t, w3_t, w2_b, hres, norm_ffn_g, tm, tn)
    blocks = lambda a: a.reshape(N_DEV, a.shape[0] // N_DEV, a.shape[1])
    d_w1 = blocks(_at_b(dg, h2, "grad_w1", tn, D_MODEL, tk))
    d_w3 = blocks(_at_b(dv, h2, "grad_w3", tn, D_MODEL, tk))
    d_w2 = blocks(_at_b(ff, d3, "grad_w2", tn, D_MODEL, tk))
    d_wout = blocks(_at_b(y, d2, "grad_w_out", D_MODEL, D_MODEL, tk))
    early_sums = pair_sums([d_wout, d_w1, d_w3, d_w2], ["w_out", "w1", "w3", "w2"])
    du, d_mixer, early_parts = _mixer_bwd(d2, u, hs, pv, wa, wx, wp, w_out_b, tmx, chip_sums=early_sums)
    grad_x, d_gmix, _ = _mix_in_bwd(du, xs, d2, w_in_t, norm_mix_g, tm)
    d_win, small_parts = _at_b(du, h1, "grad_w_in", D_IN, D_MODEL, tk,
                               gather=[d_mixer, d_gmix, d_gffn, d_gfin, loss_acc])
    win_parts = _chip_exchange(pair_sums([blocks(d_win)], ["w_in"]), "grads_to_chips_w_in")
    parts = list(win_parts) + list(early_parts)

    res = {}
    shard_w = dict(w_in=(w_in, m_w_in, v_w_in, tr), w_out=(w_out, m_w_out, v_w_out, own),
                   ffn_w1=(ffn_w1, m_ffn_w1, v_ffn_w1, tr), ffn_w3=(ffn_w3, m_ffn_w3, v_ffn_w3, tr),
                   ffn_w2=(ffn_w2, m_ffn_w2, v_ffn_w2, own))
    for (nm, (w, m, v, view)), p in zip(shard_w.items(), parts):
        outs = _adam_shard(view(w), view(m), view(v), p, "adam_" + nm)
        res[nm] = [(jnp.swapaxes(o, 0, 1) if view is tr else o)[None] for o in outs]

    row = lambda a: a.reshape(1, D_MODEL)
    small = lambda gm, cw, cb, wa_, ba, wx_, bx, lam, pw, pb, ps, gl, gp, gf, gn: dict(
        norm_mix_g=gm, conv_w=cw, conv_b=cb, gate_a_w=wa_, gate_a_b=ba, gate_x_w=wx_, gate_x_b=bx, lru_lambda=lam,
        pool_w=pw, pool_b=pb, pool_scale=ps, norm_lru_g=gl, norm_pool_g=gp, norm_ffn_g=gf, final_norm_g=row(gn))
    small_res, loss = _adam_small(
        small_parts,
        small(norm_mix_g, conv_w, conv_b, gate_a_w, gate_a_b, gate_x_w, gate_x_b, lru_lambda, pool_w, pool_b,
              pool_scale, norm_lru_g, norm_pool_g, norm_ffn_g, final_norm_g),
        small(m_norm_mix_g, m_conv_w, m_conv_b, m_gate_a_w, m_gate_a_b, m_gate_x_w, m_gate_x_b, m_lru_lambda, m_pool_w,
              m_pool_b, m_pool_scale, m_norm_lru_g, m_norm_pool_g, m_norm_ffn_g, m_final_norm_g),
        small(v_norm_mix_g, v_conv_w, v_conv_b, v_gate_a_w, v_gate_a_b, v_gate_x_w, v_gate_x_b, v_lru_lambda, v_pool_w,
              v_pool_b, v_pool_scale, v_norm_lru_g, v_norm_pool_g, v_norm_ffn_g, v_final_norm_g))
    for nm, outs in small_res.items():
        res[nm] = [o.reshape(D_MODEL) for o in outs] if nm == "final_norm_g" else list(outs)

    out = [loss, grad_x[None]]
    for kind in range(4):
        out += [res[nm][kind] for nm in WEIGHT_ORDER]
    return tuple(out)
```

```python
import jax
import jax.numpy as jnp
from jax import lax
from jax.experimental import pallas as pl
from jax.experimental.pallas import tpu as pltpu

F32 = jnp.float32
BF16 = jnp.bfloat16

D_MODEL = 1024
LRU_W = 512
POOL_W = 512
D_IN = 1536
D_FF = 2816
POOL_WINDOWS = (2, 4, 8, 16)
EPS = 1e-6
LRU_C = 8.0
N_DEV = 8
HALO = 16

ADAM_LR = 0.001
ADAM_B1 = 0.9
ADAM_B2 = 0.999
ADAM_EPS = 1e-08
ADAM_WD = 0.01
ADAM_STEP = 10

ROW_CW, ROW_CB, ROW_BA, ROW_BX, ROW_LAM, ROW_PB, ROW_PS, ROW_GL, ROW_GP = 0, 4, 5, 6, 7, 8, 9, 10, 11
SG_VEC, SG_WA, SG_WX, SG_WP, SG_ROWS = 0, 32, 160, 288, 544

NT = (((1,), (1,)), ((), ()))
TN = (((0,), (0,)), ((), ()))


def _sds(shape, dtype):
    return jax.ShapeDtypeStruct(shape, dtype)


def _sigmoid(x):
    return 0.5 * jnp.tanh(0.5 * x) + 0.5


def _gelu_parts(x):
    c = 0.7978845608028654
    inner = c * (x + 0.044715 * (x * x * x))
    th = jnp.tanh(inner)
    g = 0.5 * x * (1.0 + th)
    dg = 0.5 * (1.0 + th) + 0.5 * x * (1.0 - th * th) * (c * (1.0 + 3.0 * 0.044715 * (x * x)))
    return g, dg


def _window_sum(ext, w, back):
    n = ext.shape[0]
    s, k = ext, 1
    while k < w:
        s = s + pltpu.roll(s, k if back else n - k, 0)
        k *= 2
    return s


def _rstd(x):
    return lax.rsqrt(jnp.mean(x * x, axis=-1, keepdims=True) + EPS)


def _rms_bwd(dy, xhat, rstd, gain):
    dxh = dy * gain
    dx = rstd * (dxh - xhat * jnp.mean(dxh * xhat, axis=-1, keepdims=True))
    return dx, jnp.sum(dy * xhat, axis=0, keepdims=True)


def _bd(xb, w_ref):
    return jnp.concatenate(
        [jnp.dot(xb[:, :256], w_ref[0], preferred_element_type=F32),
         jnp.dot(xb[:, 256:], w_ref[1], preferred_element_type=F32)], axis=1)


def _bd_t(xb, w_ref):
    return jnp.concatenate(
        [lax.dot_general(xb[:, :256], w_ref[0], NT, preferred_element_type=F32),
         lax.dot_general(xb[:, 256:], w_ref[1], NT, preferred_element_type=F32)], axis=1)


def _bd_grad(xb, db):
    return jnp.stack(
        [lax.dot_general(xb[:, :256], db[:, :256], TN, preferred_element_type=F32),
         lax.dot_general(xb[:, 256:], db[:, 256:], TN, preferred_element_type=F32)], axis=0)


def _fill_block_diag(dst, src_ref):
    n, k, _ = src_ref.shape
    dst[...] = jnp.zeros(dst.shape, BF16)
    for b in range(n):
        p, q = divmod(b, 256 // k)
        dst[p, q * k:(q + 1) * k, q * k:(q + 1) * k] = src_ref[b].astype(BF16)


def _diag_pack(w, k):
    lane = lax.broadcasted_iota(jnp.int32, (k, 256), 1)
    out = w[0:k]
    for q in range(1, 256 // k):
        out = jnp.where(lane >= q * k, w[q * k:(q + 1) * k], out)
    return out


def _mixer_pre(e_lru, e_pool, pv, wa_ref, wx_ref, wp_ref, tm, t0):
    taps = [e_lru[pl.ds(HALO - 3 + k, tm), :] for k in range(4)]
    xc = pv[ROW_CB:ROW_CB + 1, :]
    for k in range(4):
        xc = xc + taps[k] * pv[ROW_CW + k:ROW_CW + k + 1, :]
    xcb = xc.astype(BF16)
    r = _sigmoid(_bd(xcb, wa_ref) + pv[ROW_BA:ROW_BA + 1, :])
    ig = _sigmoid(_bd(xcb, wx_ref) + pv[ROW_BX:ROW_BX + 1, :])
    z = -pv[ROW_LAM:ROW_LAM + 1, :]
    sp = jnp.maximum(z, 0.0) + jnp.log(1.0 + jnp.exp(-jnp.abs(z)))
    la = (-LRU_C * r) * sp
    a = jnp.exp(la)
    om = -jnp.tanh(la) * (1.0 + a * a)
    omc = jnp.maximum(om, 1e-12)
    rmult = lax.rsqrt(omc)
    mult = omc * rmult
    t = t0 + lax.broadcasted_iota(jnp.int32, (tm, 1), 0)
    parts, inv_cnts = [], []
    for g, w in enumerate(POOL_WINDOWS):
        ext = e_pool[:, pl.ds(128 * g, 128)]
        s = _window_sum(ext, w, back=True)[HALO:, :]
        inv_cnt = 1.0 / jnp.minimum(t + 1, w).astype(F32)
        inv_cnts.append(inv_cnt)
        parts.append(s * inv_cnt - ext[HALO:, :])
    pooled = jnp.concatenate(parts, axis=1)
    pooled_b = pooled.astype(BF16)
    zp = _bd(pooled_b, wp_ref) + pv[ROW_PB:ROW_PB + 1, :]
    return dict(taps=taps, xc=xc, xcb=xcb, r=r, ig=ig, sp=sp, a=a, om=om, mult=mult, rmult=rmult,
                pooled_b=pooled_b, zp=zp, inv_cnts=inv_cnts)


def _scan_tile(a_ref, b_ref, out_ref, carry, tm, reverse):
    row = lax.broadcasted_iota(jnp.int32, (8, LRU_W), 0)
    nblk = tm // 8

    def step(i, hin):
        blk = (nblk - 1 - i) if reverse else i
        r0 = pl.multiple_of(blk * 8, 8)
        av = a_ref[pl.ds(r0, 8), :]
        bv = b_ref[pl.ds(r0, 8), :]
        for d in (1, 2, 4):
            sh = (8 - d) if reverse else d
            a_s = pltpu.roll(av, sh, 0)
            b_s = pltpu.roll(bv, sh, 0)
            m = (row < 8 - d) if reverse else (row >= d)
            bv = jnp.where(m, av * b_s + bv, bv)
            av = jnp.where(m, av * a_s, av)
        hv = av * hin + bv
        out_ref[pl.ds(r0, 8), :] = hv
        edge = hv[0:1, :] if reverse else hv[7:8, :]
        return jnp.broadcast_to(edge, (8, LRU_W))

    return lax.fori_loop(0, nblk, step, carry)


MESH = pl.DeviceIdType.MESH
ANY = pl.BlockSpec(memory_space=pl.ANY)


def _place():
    x, y, c = lax.axis_index("x"), lax.axis_index("y"), lax.axis_index("c")
    chips = [(1 - x, y), (x, 1 - y), (1 - x, 1 - y)]
    return x, y, c, chips


class _Gather:
    def __init__(self, ins, outs, send_sems, recv_sems, local_sems):
        self.ins, self.outs, self.n = ins, outs, len(ins)
        self.send_sems, self.recv_sems, self.local_sems = send_sems, recv_sems, local_sems

    @staticmethod
    def scratch(n):
        return [pltpu.SemaphoreType.DMA((7, n)), pltpu.SemaphoreType.DMA((7, n)), pltpu.SemaphoreType.DMA((n,))]

    def _slot(self, a, px, py, pc):
        return self.outs[a].at[4 * px + 2 * py + pc]

    def _copy(self, a, k, block, to, src=None):
        return pltpu.make_async_remote_copy(
            src_ref=self._slot(a, *block) if src is None else src, dst_ref=self._slot(a, *block),
            send_sem=self.send_sems.at[k, a], recv_sem=self.recv_sems.at[k, a], device_id=to, device_id_type=MESH)

    def _mine(self, a):
        x, y, c, _ = _place()
        return pltpu.make_async_copy(self.ins[a], self._slot(a, x, y, c), self.local_sems.at[a])

    def _first(self, a):
        x, y, c, chips = _place()
        me = (x, y, c)
        return ([self._copy(a, 0, me, (x, y, 1 - c), src=self.ins[a])]
                + [self._copy(a, 1 + j, me, (*chip, c), src=self.ins[a]) for j, chip in enumerate(chips)])

    def start(self):
        for a in range(self.n):
            self._mine(a).start()
        for a in range(self.n):
            for cp in self._first(a):
                cp.start()

    def finish(self):
        x, y, c, chips = _place()
        me, sibling = (x, y, c), (x, y, 1 - c)
        passed = []
        for j, chip in enumerate(chips):
            for a in range(self.n):
                self._copy(a, 1 + j, (*chip, c), me).wait_recv()
                fwd = self._copy(a, 4 + j, (*chip, c), sibling)
                fwd.start()
                passed.append(fwd)
        for a in range(self.n):
            self._copy(a, 0, (x, y, 1 - c), me).wait_recv()
            for j, chip in enumerate(chips):
                self._copy(a, 4 + j, (*chip, 1 - c), me).wait_recv()
        for a in range(self.n):
            for cp in self._first(a):
                cp.wait_send()
        for cp in passed:
            cp.wait_send()
        for a in range(self.n):
            self._mine(a).wait()


def _all_gather(arrs, name):
    n = len(arrs)

    def body(*refs):
        g = _Gather(refs[:n], refs[n:2 * n], *refs[2 * n:])
        g.start()
        g.finish()

    return pl.pallas_call(
        body, name=name,
        out_shape=[_sds((N_DEV,) + a.shape, a.dtype) for a in arrs],
        in_specs=[ANY] * n, out_specs=[ANY] * n, scratch_shapes=_Gather.scratch(n),
    )(*arrs)


def _pair_exchange(arrs, name):
    n = len(arrs)

    def body(*refs):
        ins, outs = refs[:n], refs[n:2 * n]
        send_sems, recv_sems = refs[2 * n:]
        x, y, c, _ = _place()
        sibling = (x, y, 1 - c)
        sends = []
        for a in range(n):
            for j in range(4):
                cp = pltpu.make_async_remote_copy(
                    src_ref=ins[a].at[2 * j + (1 - c)], dst_ref=outs[a].at[j],
                    send_sem=send_sems.at[j, a], recv_sem=recv_sems.at[j, a], device_id=sibling, device_id_type=MESH)
                cp.start()
                sends.append(cp)
        for cp in sends:
            cp.wait()

    return pl.pallas_call(
        body, name=name,
        out_shape=[_sds((4,) + a.shape[1:], a.dtype) for a in arrs],
        in_specs=[ANY] * n, out_specs=[ANY] * n,
        scratch_shapes=[pltpu.SemaphoreType.DMA((4, n)), pltpu.SemaphoreType.DMA((4, n))],
    )(*arrs)


class _ChipExchange:
    def __init__(self, ins, outs, send_sems, recv_sems, local_sems):
        self.ins, self.outs, self.n = ins, outs, len(ins)
        self.send_sems, self.recv_sems, self.local_sems = send_sems, recv_sems, local_sems

    @staticmethod
    def scratch(n):
        return [pltpu.SemaphoreType.DMA((3, n)), pltpu.SemaphoreType.DMA((3, n)), pltpu.SemaphoreType.DMA((n,))]

    def _local(self, a):
        x, y, _, _ = _place()
        me = 2 * x + y
        return pltpu.make_async_copy(self.ins[a].at[me], self.outs[a].at[me], self.local_sems.at[a])

    def _copies(self, a):
        x, y, c, chips = _place()
        me = 2 * x + y
        return [(pltpu.make_async_remote_copy(
                     src_ref=self.ins[a].at[2 * px + py], dst_ref=self.outs[a].at[me],
                     send_sem=self.send_sems.at[k, a], recv_sem=self.recv_sems.at[k, a],
                     device_id=(px, py, c), device_id_type=MESH),
                 pltpu.make_async_remote_copy(
                     src_ref=self.ins[a].at[me], dst_ref=self.outs[a].at[2 * px + py],
                     send_sem=self.send_sems.at[k, a], recv_sem=self.recv_sems.at[k, a],
                     device_id=(px, py, c), device_id_type=MESH))
                for k, (px, py) in enumerate(chips)]

    def start(self):
        for a in range(self.n):
            self._local(a).start()
        for a in range(self.n):
            for send, _ in self._copies(a):
                send.start()

    def finish(self):
        for a in range(self.n):
            for send, recv in self._copies(a):
                send.wait_send()
                recv.wait_recv()
        for a in range(self.n):
            self._local(a).wait()


def _chip_exchange(arrs, name):
    n = len(arrs)

    def body(*refs):
        e = _ChipExchange(refs[:n], refs[n:2 * n], *refs[2 * n:])
        e.start()
        e.finish()

    return pl.pallas_call(
        body, name=name, out_shape=[_sds(a.shape, a.dtype) for a in arrs],
        in_specs=[ANY] * n, out_specs=[ANY] * n, scratch_shapes=_ChipExchange.scratch(n),
    )(*arrs)


def _mix_in(x, g_mix, w_in_t, tm, shards=()):
    T = x.shape[0]
    n_t = T // tm
    n_s = len(shards)

    def body(x_ref, g_ref, w_ref, *rest):
        sh_in, rest = rest[:n_s], rest[n_s:]
        u_ref, h_ref = rest[:2]
        gather = _Gather(sh_in, rest[2:2 + n_s], *rest[2 + n_s:]) if n_s else None
        i = pl.program_id(0)

        if gather:
            @pl.when(i == 0)
            def _():
                gather.start()

        xv = x_ref[...]
        h = (xv * _rstd(xv) * g_ref[...]).astype(BF16)
        h_ref[...] = h
        u_ref[...] = lax.dot_general(h, w_ref[...], NT, preferred_element_type=F32)

        if gather:
            @pl.when(i == n_t - 1)
            def _():
                gather.finish()

    outs = pl.pallas_call(
        body, name="mix_in", grid=(n_t,),
        in_specs=[pl.BlockSpec((tm, D_MODEL), lambda i: (i, 0)),
                  pl.BlockSpec((1, D_MODEL), lambda i: (0, 0)),
                  pl.BlockSpec((D_IN, D_MODEL), lambda i: (0, 0))] + [ANY] * n_s,
        out_specs=[pl.BlockSpec((tm, D_IN), lambda i: (i, 0)),
                   pl.BlockSpec((tm, D_MODEL), lambda i: (i, 0))] + [ANY] * n_s,
        out_shape=[_sds((T, D_IN), F32), _sds((T, D_MODEL), BF16)] + [_sds((N_DEV,) + a.shape, a.dtype) for a in shards],
        scratch_shapes=_Gather.scratch(n_s) if n_s else [],
        compiler_params=pltpu.CompilerParams(dimension_semantics=("arbitrary",)),
    )(x, g_mix, w_in_t, *shards)
    return outs[0], outs[1], list(outs[2:])


def _mixer_fwd(u, pv, wa, wx, wp, tm, shards=()):
    T = u.shape[0]
    n_s = len(shards)
    n_t = T // tm

    def body(u_ref, pv_ref, wa_in, wx_in, wp_in, *rest):
        sh_in, rest = rest[:n_s], rest[n_s:]
        y_ref, hs_ref = rest[:2]
        sh_out, rest = rest[2:2 + n_s], rest[2 + n_s:]
        e_lru, e_pool, a_s, b_s, hc, wa_ref, wx_ref, wp_ref = rest[:8]
        gather = _Gather(sh_in, sh_out, *rest[8:]) if n_s else None
        i = pl.program_id(0)

        @pl.when(i == 0)
        def _():
            if gather:
                gather.start()
            e_lru[pl.ds(0, HALO), :] = jnp.zeros((HALO, LRU_W), F32)
            e_pool[pl.ds(0, HALO), :] = jnp.zeros((HALO, POOL_W), F32)
            hc[...] = jnp.zeros((8, LRU_W), F32)
            _fill_block_diag(wa_ref, wa_in)
            _fill_block_diag(wx_ref, wx_in)
            _fill_block_diag(wp_ref, wp_in)

        e_lru[pl.ds(HALO, tm), :] = u_ref[:, 0:LRU_W]
        e_pool[pl.ds(HALO, tm), :] = u_ref[:, 2 * LRU_W:D_IN]
        pv = pv_ref[...]
        p = _mixer_pre(e_lru, e_pool, pv, wa_ref, wx_ref, wp_ref, tm, i * tm)
        a_s[...] = p["a"]
        b_s[...] = p["mult"] * (p["ig"] * p["xc"])
        hc[...] = _scan_tile(a_s, b_s, hs_ref, hc[...], tm, reverse=False)
        gl, _ = _gelu_parts(u_ref[:, LRU_W:2 * LRU_W])
        y_lru = hs_ref[...] * gl
        y_ref[:, 0:LRU_W] = (y_lru * _rstd(y_lru) * pv[ROW_GL:ROW_GL + 1, :]).astype(BF16)
        y_pool = p["zp"] * pv[ROW_PS:ROW_PS + 1, :]
        y_ref[:, LRU_W:D_MODEL] = (y_pool * _rstd(y_pool) * pv[ROW_GP:ROW_GP + 1, :]).astype(BF16)
        e_lru[pl.ds(0, HALO), :] = e_lru[pl.ds(tm, HALO), :]
        e_pool[pl.ds(0, HALO), :] = e_pool[pl.ds(tm, HALO), :]

        if gather:
            @pl.when(i == n_t - 1)
            def _():
                gather.finish()

    full = lambda shape: pl.BlockSpec(shape, lambda i: (0,) * len(shape))
    outs = pl.pallas_call(
        body, name="mixer_fwd", grid=(n_t,),
        in_specs=[pl.BlockSpec((tm, D_IN), lambda i: (i, 0)), full((16, LRU_W)),
                  full((8, 64, 64)), full((8, 64, 64)), full((4, 128, 128))] + [ANY] * n_s,
        out_specs=[pl.BlockSpec((tm, D_MODEL), lambda i: (i, 0)), pl.BlockSpec((tm, LRU_W), lambda i: (i, 0))] + [ANY] * n_s,
        out_shape=[_sds((T, D_MODEL), BF16), _sds((T, LRU_W), F32)] + [_sds((N_DEV,) + a.shape, a.dtype) for a in shards],
        scratch_shapes=[pltpu.VMEM((HALO + tm, LRU_W), F32), pltpu.VMEM((HALO + tm, POOL_W), F32),
                        pltpu.VMEM((tm, LRU_W), F32), pltpu.VMEM((tm, LRU_W), F32), pltpu.VMEM((8, LRU_W), F32)]
        + [pltpu.VMEM((2, 256, 256), BF16)] * 3 + (_Gather.scratch(n_s) if n_s else []),
        compiler_params=pltpu.CompilerParams(dimension_semantics=("arbitrary",)),
    )(u, pv, wa, wx, wp, *shards)
    return outs[0], outs[1], list(outs[2:])


def _ffn_fwd(x, y, w_out_b, g_ffn, w1_b, w3_b, w2_b, g_fin, tgt, tm, tn):
    T = x.shape[0]
    n_j = D_FF // tn

    def body(x_ref, y_ref, wo_ref, gf_ref, w1_ref, w3_ref, w2_ref, gfin_ref, tgt_ref,
             hres_ref, h2_ref, g_ref, v_ref, d3_ref, loss_ref, dgfin_ref, acc):
        i, j = pl.program_id(0), pl.program_id(1)

        @pl.when(j == 0)
        def _():
            hr = x_ref[...] + jnp.dot(y_ref[...], wo_ref[...], preferred_element_type=F32)
            hres_ref[...] = hr
            h2_ref[...] = (hr * _rstd(hr) * gf_ref[...]).astype(BF16)
            acc[...] = jnp.zeros((tm, D_MODEL), F32)

        @pl.when((j == 0) & (i == 0))
        def _():
            loss_ref[...] = jnp.zeros((8, 128), F32)
            dgfin_ref[...] = jnp.zeros((1, D_MODEL), F32)

        h2 = h2_ref[...]
        g = lax.dot_general(h2, w1_ref[...], NT, preferred_element_type=F32)
        v = lax.dot_general(h2, w3_ref[...], NT, preferred_element_type=F32)
        g_ref[...] = g.astype(BF16)
        v_ref[...] = v.astype(BF16)
        ff = ((g * _sigmoid(g)) * v).astype(BF16)
        acc[...] += jnp.dot(ff, w2_ref[...], preferred_element_type=F32)

        @pl.when(j == n_j - 1)
        def _():
            h3 = hres_ref[...] + acc[...]
            rstd = _rstd(h3)
            xh = h3 * rstd
            gfin = gfin_ref[...]
            err = xh * gfin - tgt_ref[...]
            loss_ref[...] += 0.5 * jnp.sum(jnp.mean(err * err, axis=-1, keepdims=True))
            dout = err * (1.0 / D_MODEL)
            dx, dgain = _rms_bwd(dout, xh, rstd, gfin)
            d3_ref[...] = dx
            dgfin_ref[...] += dgain

    row = lambda w: pl.BlockSpec((tm, w), lambda i, j: (i, 0))
    const = lambda shape: pl.BlockSpec(shape, lambda i, j: (0,) * len(shape))
    return pl.pallas_call(
        body, name="ffn_fwd", grid=(T // tm, n_j),
        in_specs=[row(D_MODEL), row(D_MODEL), const((D_MODEL, D_MODEL)), const((1, D_MODEL)),
                  pl.BlockSpec((tn, D_MODEL), lambda i, j: (j, 0)), pl.BlockSpec((tn, D_MODEL), lambda i, j: (j, 0)),
                  pl.BlockSpec((tn, D_MODEL), lambda i, j: (j, 0)), const((1, D_MODEL)), row(D_MODEL)],
        out_specs=[row(D_MODEL), row(D_MODEL),
                   pl.BlockSpec((tm, tn), lambda i, j: (i, j)), pl.BlockSpec((tm, tn), lambda i, j: (i, j)),
                   row(D_MODEL), const((8, 128)), const((1, D_MODEL))],
        out_shape=[_sds((T, D_MODEL), F32), _sds((T, D_MODEL), BF16), _sds((T, D_FF), BF16), _sds((T, D_FF), BF16),
                   _sds((T, D_MODEL), F32), _sds((8, 128), F32), _sds((1, D_MODEL), F32)],
        scratch_shapes=[pltpu.VMEM((tm, D_MODEL), F32)],
        compiler_params=pltpu.CompilerParams(dimension_semantics=("arbitrary", "arbitrary")),
    )(x, y, w_out_b, g_ffn, w1_b, w3_b, w2_b, g_fin, tgt)


def _ffn_bwd(d3, g, v, w1_b, w3_b, w2_b, hres, g_ffn, tm, tn):
    T = d3.shape[0]
    n_j = D_FF // tn

    def body(d3_ref, g_ref, v_ref, w1_ref, w3_ref, w2_ref, hres_ref, gf_ref,
             dg_ref, dv_ref, ff_ref, d2_ref, dgffn_ref, acc):
        i, j = pl.program_id(0), pl.program_id(1)

        @pl.when(j == 0)
        def _():
            acc[...] = jnp.zeros((tm, D_MODEL), F32)

        @pl.when((j == 0) & (i == 0))
        def _():
            dgffn_ref[...] = jnp.zeros((1, D_MODEL), F32)

        dff = lax.dot_general(d3_ref[...].astype(BF16), w2_ref[...], NT, preferred_element_type=F32)
        gv = g_ref[...].astype(F32)
        vv = v_ref[...].astype(F32)
        sg = _sigmoid(gv)
        sl = gv * sg
        dgb = (dff * vv * (sg * (1.0 + gv * (1.0 - sg)))).astype(BF16)
        dvb = (dff * sl).astype(BF16)
        dg_ref[...] = dgb
        dv_ref[...] = dvb
        ff_ref[...] = (sl * vv).astype(BF16)
        acc[...] += (jnp.dot(dgb, w1_ref[...], preferred_element_type=F32)
                     + jnp.dot(dvb, w3_ref[...], preferred_element_type=F32))

        @pl.when(j == n_j - 1)
        def _():
            hr = hres_ref[...]
            rstd = _rstd(hr)
            dx, dgain = _rms_bwd(acc[...], hr * rstd, rstd, gf_ref[...])
            d2_ref[...] = d3_ref[...] + dx
            dgffn_ref[...] += dgain

    row = lambda w: pl.BlockSpec((tm, w), lambda i, j: (i, 0))
    tile = pl.BlockSpec((tm, tn), lambda i, j: (i, j))
    const = lambda shape: pl.BlockSpec(shape, lambda i, j: (0,) * len(shape))
    return pl.pallas_call(
        body, name="ffn_bwd", grid=(T // tm, n_j),
        in_specs=[row(D_MODEL), tile, tile,
                  pl.BlockSpec((tn, D_MODEL), lambda i, j: (j, 0)), pl.BlockSpec((tn, D_MODEL), lambda i, j: (j, 0)),
                  pl.BlockSpec((tn, D_MODEL), lambda i, j: (j, 0)), row(D_MODEL), const((1, D_MODEL))],
        out_specs=[tile, tile, tile, row(D_MODEL), const((1, D_MODEL))],
        out_shape=[_sds((T, D_FF), BF16), _sds((T, D_FF), BF16), _sds((T, D_FF), BF16),
                   _sds((T, D_MODEL), F32), _sds((1, D_MODEL), F32)],
        scratch_shapes=[pltpu.VMEM((tm, D_MODEL), F32)],
        compiler_params=pltpu.CompilerParams(dimension_semantics=("arbitrary", "arbitrary")),
    )(d3, g, v, w1_b, w3_b, w2_b, hres, g_ffn)


def _at_b(a, b, name, tmm, tn, tk, gather=()):
    T, M = a.shape
    N = b.shape[1]
    n_m, n_n, n_k = M // tmm, N // tn, T // tk
    n_g = len(gather)

    def body(a_ref, b_ref, *rest):
        g_in, o_ref, rest = rest[:n_g], rest[n_g], rest[n_g + 1:]
        ag = _Gather(g_in, rest[:n_g], *rest[n_g:]) if n_g else None
        m, n, k = pl.program_id(0), pl.program_id(1), pl.program_id(2)

        if ag:
            @pl.when((m == 0) & (n == 0) & (k == 0))
            def _():
                ag.start()

        @pl.when(k == 0)
        def _():
            o_ref[...] = jnp.zeros((tmm, tn), F32)

        o_ref[...] += lax.dot_general(a_ref[...].astype(BF16), b_ref[...].astype(BF16), TN,
                                      preferred_element_type=F32)

        if ag:
            @pl.when((m == n_m - 1) & (n == n_n - 1) & (k == n_k - 1))
            def _():
                ag.finish()

    outs = pl.pallas_call(
        body, name=name, grid=(n_m, n_n, n_k),
        in_specs=[pl.BlockSpec((tk, tmm), lambda m, n, k: (k, m)), pl.BlockSpec((tk, tn), lambda m, n, k: (k, n))]
        + [ANY] * n_g,
        out_specs=[pl.BlockSpec((tmm, tn), lambda m, n, k: (m, n))] + [ANY] * n_g,
        out_shape=[_sds((M, N), F32)] + [_sds((N_DEV,) + g.shape, g.dtype) for g in gather],
        scratch_shapes=_Gather.scratch(n_g) if n_g else [],
        compiler_params=pltpu.CompilerParams(
            dimension_semantics=("arbitrary",) * 3 if n_g else ("parallel", "parallel", "arbitrary")),
    )(a, b, *gather)
    return (outs[0], list(outs[1:])) if n_g else outs[0]


def _mixer_bwd(d2, u, hs, pv, wa, wx, wp, w_out_b, tm, chip_sums=()):
    T = u.shape[0]
    n_t = T // tm
    n_x = len(chip_sums)

    def body(d2_ref, u_ref, uh_ref, hs_ref, hh_ref, pv_ref, wa_in, wx_in, wp_in, wo_ref, *rest):
        x_in, rest = rest[:n_x], rest[n_x:]
        du_ref, sg_ref = rest[:2]
        x_out, rest = rest[2:2 + n_x], rest[2 + n_x:]
        e_lru, e_pool, e_h, a_s, b_s, mu_s, f_x, f_p, mc, cx, cp = rest[:11]
        wa_ref, wx_ref, wp_ref, vacc_ref, dwa_ref, dwx_ref, dwp_ref = rest[11:18]
        exchange = _ChipExchange(x_in, x_out, *rest[18:]) if n_x else None
        s = pl.program_id(0)
        it = n_t - 1 - s

        @pl.when(s == 0)
        def _():
            if exchange:
                exchange.start()
            mc[...] = jnp.zeros((8, LRU_W), F32)
            cx[...] = jnp.zeros((8, LRU_W), F32)
            cp[...] = jnp.zeros((HALO, POOL_W), F32)
            vacc_ref[...] = jnp.zeros((16, LRU_W), F32)
            dwa_ref[...] = jnp.zeros((2, 256, 256), F32)
            dwx_ref[...] = jnp.zeros((2, 256, 256), F32)
            dwp_ref[...] = jnp.zeros((2, 256, 256), F32)
            _fill_block_diag(wa_ref, wa_in)
            _fill_block_diag(wx_ref, wx_in)
            _fill_block_diag(wp_ref, wp_in)

        first = it == 0
        e_lru[pl.ds(0, HALO), :] = jnp.where(first, 0.0, uh_ref[:, 0:LRU_W])
        e_pool[pl.ds(0, HALO), :] = jnp.where(first, 0.0, uh_ref[:, 2 * LRU_W:D_IN])
        e_lru[pl.ds(HALO, tm), :] = u_ref[:, 0:LRU_W]
        e_pool[pl.ds(HALO, tm), :] = u_ref[:, 2 * LRU_W:D_IN]
        e_h[pl.ds(0, 8), :] = jnp.where(first, 0.0, hh_ref[...])
        e_h[pl.ds(8, tm), :] = hs_ref[...]
        pv = pv_ref[...]
        p = _mixer_pre(e_lru, e_pool, pv, wa_ref, wx_ref, wp_ref, tm, it * tm)
        a, xc, ig, r, mult = p["a"], p["xc"], p["ig"], p["r"], p["mult"]

        dyn = lax.dot_general(d2_ref[...].astype(BF16), wo_ref[...], NT, preferred_element_type=F32)

        h = hs_ref[...]
        ug = u_ref[:, LRU_W:2 * LRU_W]
        gl, dgl = _gelu_parts(ug)
        y_lru = h * gl
        rstd_l = _rstd(y_lru)
        dy_lru, d_gain_l = _rms_bwd(dyn[:, 0:LRU_W], y_lru * rstd_l, rstd_l, pv[ROW_GL:ROW_GL + 1, :])
        dh = dy_lru * gl
        du_ref[:, LRU_W:2 * LRU_W] = (dy_lru * h * dgl).astype(BF16)
        a_s[...] = a
        b_s[...] = a * dh
        mu_s[pl.ds(tm, 8), :] = mc[...]
        mc[...] = _scan_tile(a_s, b_s, mu_s, mc[...], tm, reverse=True)
        lam_t = dh + mu_s[pl.ds(1, tm), :]
        da = lam_t * e_h[pl.ds(7, tm), :]
        dmult = lam_t * (ig * xc)
        di = lam_t * (mult * xc)
        dxc = lam_t * (mult * ig)
        dla = da * a - jnp.where(p["om"] > 1e-12, dmult * ((a * a) * p["rmult"]), 0.0)
        dra = (dla * (-LRU_C * p["sp"])) * (r * (1.0 - r))
        dia = di * (ig * (1.0 - ig))
        drab = dra.astype(BF16)
        diab = dia.astype(BF16)
        dxc = dxc + _bd_t(drab, wa_ref) + _bd_t(diab, wx_ref)
        dwa_ref[...] += _bd_grad(p["xcb"], drab)
        dwx_ref[...] += _bd_grad(p["xcb"], diab)
        sig_neg_lam = _sigmoid(-pv[ROW_LAM:ROW_LAM + 1, :])
        d_lam = jnp.sum(dla * r, axis=0, keepdims=True) * (LRU_C * sig_neg_lam)

        f_x[pl.ds(0, tm), :] = dxc
        f_x[pl.ds(tm, 8), :] = cx[...]
        du_lru = jnp.zeros((tm, LRU_W), F32)
        d_cw = []
        for k in range(4):
            du_lru = du_lru + f_x[pl.ds(3 - k, tm), :] * pv[ROW_CW + k:ROW_CW + k + 1, :]
            d_cw.append(jnp.sum(dxc * p["taps"][k], axis=0, keepdims=True))
        du_ref[:, 0:LRU_W] = du_lru.astype(BF16)
        cx[...] = f_x[pl.ds(0, 8), :]

        zp = p["zp"]
        ps = pv[ROW_PS:ROW_PS + 1, :]
        y_pool = zp * ps
        rstd_p = _rstd(y_pool)
        dy_pool, d_gain_p = _rms_bwd(dyn[:, LRU_W:D_MODEL], y_pool * rstd_p, rstd_p, pv[ROW_GP:ROW_GP + 1, :])
        dz = dy_pool * ps
        dzb = dz.astype(BF16)
        dwp_ref[...] += _bd_grad(p["pooled_b"], dzb)
        dpooled = _bd_t(dzb, wp_ref)
        for g, w in enumerate(POOL_WINDOWS):
            f_p[pl.ds(0, tm), pl.ds(128 * g, 128)] = dpooled[:, 128 * g:128 * (g + 1)] * p["inv_cnts"][g]
        f_p[pl.ds(tm, HALO), :] = cp[...]
        for g, w in enumerate(POOL_WINDOWS):
            acc = _window_sum(f_p[:, pl.ds(128 * g, 128)], w, back=False)[0:tm, :]
            du_ref[:, 2 * LRU_W + 128 * g:2 * LRU_W + 128 * (g + 1)] = (
                acc - dpooled[:, 128 * g:128 * (g + 1)]).astype(BF16)
        cp[...] = f_p[pl.ds(0, HALO), :]

        rows = d_cw + [
            jnp.sum(dxc, axis=0, keepdims=True),
            jnp.sum(dra, axis=0, keepdims=True),
            jnp.sum(dia, axis=0, keepdims=True),
            d_lam,
            jnp.sum(dz, axis=0, keepdims=True),
            jnp.sum(dy_pool * zp, axis=0, keepdims=True),
            d_gain_l, d_gain_p,
            jnp.zeros((4, LRU_W), F32),
        ]
        vacc_ref[...] += jnp.concatenate(rows, axis=0)

        @pl.when(s == n_t - 1)
        def _():
            sg_ref[SG_VEC:SG_VEC + 16, :] = vacc_ref[:, 0:256]
            sg_ref[SG_VEC + 16:SG_VEC + 32, :] = vacc_ref[:, 256:512]
            for half in range(2):
                sg_ref[SG_WA + 64 * half:SG_WA + 64 * (half + 1), :] = _diag_pack(dwa_ref[half], 64)
                sg_ref[SG_WX + 64 * half:SG_WX + 64 * (half + 1), :] = _diag_pack(dwx_ref[half], 64)
                sg_ref[SG_WP + 128 * half:SG_WP + 128 * (half + 1), :] = _diag_pack(dwp_ref[half], 128)
            if exchange:
                exchange.finish()

    rev = lambda w: pl.BlockSpec((tm, w), lambda s: (n_t - 1 - s, 0))
    full = lambda shape: pl.BlockSpec(shape, lambda s: (0,) * len(shape))
    outs = pl.pallas_call(
        body, name="mixer_bwd", grid=(n_t,),
        in_specs=[rev(D_MODEL), rev(D_IN),
                  pl.BlockSpec((HALO, D_IN), lambda s: (jnp.maximum((n_t - 1 - s) * (tm // HALO) - 1, 0), 0)),
                  rev(LRU_W),
                  pl.BlockSpec((8, LRU_W), lambda s: (jnp.maximum((n_t - 1 - s) * (tm // 8) - 1, 0), 0)),
                  full((16, LRU_W)), full((8, 64, 64)), full((8, 64, 64)), full((4, 128, 128)),
                  full((D_MODEL, D_MODEL))] + [ANY] * n_x,
        out_specs=[rev(D_IN), full((SG_ROWS, 256))] + [ANY] * n_x,
        out_shape=[_sds((T, D_IN), BF16), _sds((SG_ROWS, 256), F32)] + [_sds(a.shape, a.dtype) for a in chip_sums],
        scratch_shapes=[pltpu.VMEM((HALO + tm, LRU_W), F32), pltpu.VMEM((HALO + tm, POOL_W), F32),
                        pltpu.VMEM((8 + tm, LRU_W), F32), pltpu.VMEM((tm, LRU_W), F32), pltpu.VMEM((tm, LRU_W), F32),
                        pltpu.VMEM((tm + 8, LRU_W), F32), pltpu.VMEM((tm + 8, LRU_W), F32),
                        pltpu.VMEM((tm + HALO, POOL_W), F32), pltpu.VMEM((8, LRU_W), F32),
                        pltpu.VMEM((8, LRU_W), F32), pltpu.VMEM((HALO, POOL_W), F32)]
        + [pltpu.VMEM((2, 256, 256), BF16)] * 3 + [pltpu.VMEM((16, LRU_W), F32)] + [pltpu.VMEM((2, 256, 256), F32)] * 3
        + (_ChipExchange.scratch(n_x) if n_x else []),
        compiler_params=pltpu.CompilerParams(dimension_semantics=("arbitrary",)),
    )(d2, u, u, hs, hs, pv, wa, wx, wp, w_out_b, *chip_sums)
    return outs[0], outs[1], list(outs[2:])


def _mix_in_bwd(du, x, d2, w_in_t, g_mix, tm, chip_sums=()):
    T = x.shape[0]
    n_t = T // tm
    n_x = len(chip_sums)

    def body(du_ref, x_ref, d2_ref, w_ref, g_ref, *rest):
        x_in, rest = rest[:n_x], rest[n_x:]
        dx_ref, dg_ref = rest[:2]
        exchange = _ChipExchange(x_in, rest[2:2 + n_x], *rest[2 + n_x:]) if n_x else None
        i = pl.program_id(0)

        @pl.when(i == 0)
        def _():
            dg_ref[...] = jnp.zeros((1, D_MODEL), F32)
            if exchange:
                exchange.start()

        dh = jnp.dot(du_ref[...], w_ref[...], preferred_element_type=F32)
        xv = x_ref[...]
        rstd = _rstd(xv)
        dx, dgain = _rms_bwd(dh, xv * rstd, rstd, g_ref[...])
        dx_ref[...] = d2_ref[...] + dx
        dg_ref[...] += dgain

        if exchange:
            @pl.when(i == n_t - 1)
            def _():
                exchange.finish()

    row = lambda w: pl.BlockSpec((tm, w), lambda i: (i, 0))
    const = lambda shape: pl.BlockSpec(shape, lambda i: (0,) * len(shape))
    outs = pl.pallas_call(
        body, name="mix_in_bwd", grid=(n_t,),
        in_specs=[row(D_IN), row(D_MODEL), row(D_MODEL), const((D_IN, D_MODEL)), const((1, D_MODEL))] + [ANY] * n_x,
        out_specs=[row(D_MODEL), const((1, D_MODEL))] + [ANY] * n_x,
        out_shape=[_sds((T, D_MODEL), F32), _sds((1, D_MODEL), F32)] + [_sds(a.shape, a.dtype) for a in chip_sums],
        scratch_shapes=_ChipExchange.scratch(n_x) if n_x else [],
        compiler_params=pltpu.CompilerParams(dimension_semantics=("arbitrary",)),
    )(du, x, d2, w_in_t, g_mix, *chip_sums)
    return outs[0], outs[1], list(outs[2:])


def _pair_sum(g, r1, c_arr, name):
    _, _, R, C = g.shape
    tr = R if R <= 512 else 256

    def body(c_ref, g_ref, r_ref, o_ref):
        o_ref[...] = (g_ref[...] + r_ref[...]).astype(BF16)

    return pl.pallas_call(
        body, name=name,
        grid_spec=pltpu.PrefetchScalarGridSpec(
            num_scalar_prefetch=1, grid=(4, R // tr),
            in_specs=[pl.BlockSpec((None, None, tr, C), lambda j, i, c_ref: (j, c_ref[0], i, 0)),
                      pl.BlockSpec((None, tr, C), lambda j, i, c_ref: (j, i, 0))],
            out_specs=pl.BlockSpec((None, tr, C), lambda j, i, c_ref: (j, i, 0))),
        out_shape=_sds((4, R, C), BF16),
    )(c_arr, g, r1)


def _adamw(w, g, m, v):
    m = ADAM_B1 * m + (1.0 - ADAM_B1) * g
    v = ADAM_B2 * v + (1.0 - ADAM_B2) * (g * g)
    m_hat = m / (1.0 - ADAM_B1 ** ADAM_STEP)
    v_hat = v / (1.0 - ADAM_B2 ** ADAM_STEP)
    delta = -ADAM_LR * (m_hat / (jnp.sqrt(v_hat) + ADAM_EPS) + ADAM_WD * w)
    return delta, m, v


def _adam_shard(w, m, v, parts, name):
    R, C = w.shape
    tr = R if R <= 512 else 256

    def body(w_ref, m_ref, v_ref, p_ref, g_ref, d_ref, nm_ref, nv_ref):
        g = p_ref[0].astype(F32)
        for j in range(1, 4):
            g = g + p_ref[j].astype(F32)
        delta, nm, nv = _adamw(w_ref[...], g, m_ref[...], v_ref[...])
        g_ref[...] = g
        d_ref[...] = delta
        nm_ref[...] = nm
        nv_ref[...] = nv

    blk = pl.BlockSpec((tr, C), lambda i: (i, 0))
    return pl.pallas_call(
        body, name=name, grid=(R // tr,),
        in_specs=[blk, blk, blk, pl.BlockSpec((4, tr, C), lambda i: (0, i, 0))],
        out_specs=[blk] * 4, out_shape=[_sds((R, C), F32)] * 4,
        compiler_params=pltpu.CompilerParams(dimension_semantics=("parallel",)),
    )(w, m, v, parts)


SMALL_PARAMS = [("norm_mix_g", (1, D_MODEL)), ("conv_w", (1, 4, 64)), ("conv_b", (1, LRU_W)),
                ("gate_a_w", (1, 8, 64, 64)), ("gate_a_b", (1, LRU_W)), ("gate_x_w", (1, 8, 64, 64)),
                ("gate_x_b", (1, LRU_W)), ("lru_lambda", (1, LRU_W)), ("pool_w", (1, 4, 128, 128)),
                ("pool_b", (1, POOL_W)), ("pool_scale", (1, POOL_W)), ("norm_lru_g", (1, LRU_W)),
                ("norm_pool_g", (1, POOL_W)), ("norm_ffn_g", (1, D_MODEL)), ("final_norm_g", (1, D_MODEL))]
VEC_ROW = dict(conv_b=ROW_CB, gate_a_b=ROW_BA, gate_x_b=ROW_BX, lru_lambda=ROW_LAM, pool_b=ROW_PB, pool_scale=ROW_PS,
               norm_lru_g=ROW_GL, norm_pool_g=ROW_GP)
WHOLE = (Ellipsis,)


def _unpack_mixer_grads(sg, dev):
    vec = jnp.concatenate([sg[SG_VEC:SG_VEC + 16], sg[SG_VEC + 16:SG_VEC + 32]], axis=1)
    out = {nm: [(WHOLE, vec[r:r + 1])] for nm, r in VEC_ROW.items()}
    own = jnp.zeros((4, 64), F32)
    for d in range(N_DEV):
        own = jnp.where(dev == d, vec[ROW_CW:ROW_CW + 4, 64 * d:64 * (d + 1)], own)
    out["conv_w"] = [((0,), own)]
    for nm, row0 in (("gate_a_w", SG_WA), ("gate_x_w", SG_WX)):
        out[nm] = [((0, b), sg[row0 + 64 * (b // 4):row0 + 64 * (b // 4 + 1), 64 * (b % 4):64 * (b % 4 + 1)])
                   for b in range(8)]
    out["pool_w"] = [((0, b), sg[SG_WP + 128 * (b // 2):SG_WP + 128 * (b // 2 + 1), 128 * (b % 2):128 * (b % 2 + 1)])
                     for b in range(4)]
    return out


def _adam_small(parts, w, m, v):
    names = [nm for nm, _ in SMALL_PARAMS]
    n = len(names)

    def body(sg_ref, gm_ref, gf_ref, gn_ref, ls_ref, *rest):
        w_refs, m_refs, v_refs, outs = rest[:n], rest[n:2 * n], rest[2 * n:3 * n], rest[3 * n:]
        dev = 4 * lax.axis_index("x") + 2 * lax.axis_index("y") + lax.axis_index("c")

        def total(ref):
            acc = ref[0]
            for d in range(1, N_DEV):
                acc = acc + ref[d]
            return acc

        pieces = _unpack_mixer_grads(total(sg_ref), dev)
        pieces["norm_mix_g"] = [(WHOLE, total(gm_ref))]
        pieces["norm_ffn_g"] = [(WHOLE, total(gf_ref))]
        pieces["final_norm_g"] = [(WHOLE, total(gn_ref))]
        for i, nm in enumerate(names):
            for idx, g in pieces[nm]:
                delta, new_m, new_v = _adamw(w_refs[i][idx], g, m_refs[i][idx], v_refs[i][idx])
                for kind, val in enumerate((g, delta, new_m, new_v)):
                    outs[4 * i + kind][idx] = val
        outs[4 * n][...] = total(ls_ref)

    shapes = [_sds(shape, F32) for _, shape in SMALL_PARAMS for _ in range(4)] + [_sds((8, 128), F32)]
    res = pl.pallas_call(body, name="adam_small", out_shape=shapes)(
        *parts, *[w[nm] for nm in names], *[m[nm] for nm in names], *[v[nm] for nm in names])
    return {nm: tuple(res[4 * i:4 * i + 4]) for i, nm in enumerate(names)}, res[4 * n][0, 0]


def _vec_rows(conv_w_full, conv_b, ba, bx, lam, pb, ps, gl, gp):
    return jnp.concatenate([conv_w_full, conv_b, ba, bx, lam, pb, ps, gl, gp, jnp.zeros((4, LRU_W), F32)], axis=0)


WEIGHT_ORDER = ['norm_mix_g', 'w_in', 'conv_w', 'conv_b', 'gate_a_w', 'gate_a_b', 'gate_x_w', 'gate_x_b', 'lru_lambda',
                'pool_w', 'pool_b', 'pool_scale', 'norm_lru_g', 'norm_pool_g', 'w_out', 'norm_ffn_g', 'ffn_w1', 'ffn_w3',
                'ffn_w2', 'final_norm_g']


def kernel(x, norm_mix_g, w_in, conv_w, conv_b, gate_a_w, gate_a_b, gate_x_w, gate_x_b, lru_lambda, pool_w, pool_b, pool_scale, norm_lru_g, norm_pool_g, w_out, norm_ffn_g, ffn_w1, ffn_w3, ffn_w2, final_norm_g, loss_target, m_norm_mix_g, m_w_in, m_conv_w, m_conv_b, m_gate_a_w, m_gate_a_b, m_gate_x_w, m_gate_x_b, m_lru_lambda, m_pool_w, m_pool_b, m_pool_scale, m_norm_lru_g, m_norm_pool_g, m_w_out, m_norm_ffn_g, m_ffn_w1, m_ffn_w3, m_ffn_w2, m_final_norm_g, v_norm_mix_g, v_w_in, v_conv_w, v_conv_b, v_gate_a_w, v_gate_a_b, v_gate_x_w, v_gate_x_b, v_lru_lambda, v_pool_w, v_pool_b, v_pool_scale, v_norm_lru_g, v_norm_pool_g, v_w_out, v_norm_ffn_g, v_ffn_w1, v_ffn_w3, v_ffn_w2, v_final_norm_g):
    ac = lax.axis_index("c")
    tm, tmx, tn, tk = 512, 256, 1408, 1024
    xs, tgt = x[0], loss_target[0]
    g_fin = final_norm_g.reshape(1, D_MODEL)
    c_arr = jnp.reshape(ac, (1,)).astype(jnp.int32)

    def pair_sums(blocks, names):
        from_sibling = _pair_exchange(blocks, "grads_to_sibling_" + names[0])
        return [_pair_sum(g.reshape((4, 2) + g.shape[1:]), r, c_arr, "pair_sum_" + nm)
                for g, r, nm in zip(blocks, from_sibling, names)]

    tr = lambda w: jnp.swapaxes(w[0], 0, 1)
    own = lambda w: w[0]
    bf = lambda a: a.astype(BF16)

    g_in, g_conv = _all_gather([bf(tr(w_in)), conv_w[0]], "gather_w_in")
    w_in_t = g_in.reshape(D_IN, D_MODEL)
    conv_w_full = g_conv.transpose(1, 0, 2).reshape(4, LRU_W)
    pv = _vec_rows(conv_w_full, conv_b, gate_a_b, gate_x_b, lru_lambda, pool_b, pool_scale, norm_lru_g, norm_pool_g)
    wa, wx, wp = gate_a_w[0], gate_x_w[0], pool_w[0]

    u, h1, (g_out, g_w1) = _mix_in(xs, norm_mix_g, w_in_t, tm, shards=[bf(own(w_out)), bf(tr(ffn_w1))])
    y, hs, (g_w3, g_w2) = _mixer_fwd(u, pv, wa, wx, wp, tmx, shards=[bf(tr(ffn_w3)), bf(own(ffn_w2))])
    w_out_b = g_out.reshape(D_MODEL, D_MODEL)
    w1_t, w3_t, w2_b = g_w1.reshape(D_FF, D_MODEL), g_w3.reshape(D_FF, D_MODEL), g_w2.reshape(D_FF, D_MODEL)
    hres, h2, g, v, d3, loss_acc, d_gfin = _ffn_fwd(xs, y, w_out_b, norm_ffn_g, w1_t, w3_t, w2_b, g_fin, tgt, tm, tn)

    dg, dv, ff, d2, d_gffn = _ffn_bwd(d3, g, v, w1_t, w3_t, w2_b, hres, norm_ffn_g, tm, tn)
    blocks = lambda a: a.reshape(N_DEV, a.shape[0] // N_DEV, a.shape[1])
    d_w1 = blocks(_at_b(dg, h2, "grad_w1", tn, D_MODEL, tk))
    d_w3 = blocks(_at_b(dv, h2, "grad_w3", tn, D_MODEL, tk))
    d_w2 = blocks(_at_b(ff, d3, "grad_w2", tn, D_MODEL, tk))
    d_wout = blocks(_at_b(y, d2, "grad_w_out", D_MODEL, D_MODEL, tk))
    early_sums = pair_sums([d_wout, d_w1, d_w3, d_w2], ["w_out", "w1", "w3", "w2"])
    du, d_mixer, early_parts = _mixer_bwd(d2, u, hs, pv, wa, wx, wp, w_out_b, tmx, chip_sums=early_sums)
    grad_x, d_gmix, _ = _mix_in_bwd(du, xs, d2, w_in_t, norm_mix_g, tm)
    d_win, small_parts = _at_b(du, h1, "grad_w_in", D_IN, D_MODEL, tk,
                               gather=[d_mixer, d_gmix, d_gffn, d_gfin, loss_acc])
    win_parts = _chip_exchange(pair_sums([blocks(d_win)], ["w_in"]), "grads_to_chips_w_in")
    parts = list(win_parts) + list(early_parts)

    res = {}
    shard_w = dict(w_in=(w_in, m_w_in, v_w_in, tr), w_out=(w_out, m_w_out, v_w_out, own),
                   ffn_w1=(ffn_w1, m_ffn_w1, v_ffn_w1, tr), ffn_w3=(ffn_w3, m_ffn_w3, v_ffn_w3, tr),
                   ffn_w2=(ffn_w2, m_ffn_w2, v_ffn_w2, own))
    for (nm, (w, m, v, view)), p in zip(shard_w.items(), parts):
        outs = _adam_shard(view(w), view(m), view(v), p, "adam_" + nm)
        res[nm] = [(jnp.swapaxes(o, 0, 1) if view is tr else o)[None] for o in outs]

    row = lambda a: a.reshape(1, D_MODEL)
    small = lambda gm, cw, cb, wa_, ba, wx_, bx, lam, pw, pb, ps, gl, gp, gf, gn: dict(
        norm_mix_g=gm, conv_w=cw, conv_b=cb, gate_a_w=wa_, gate_a_b=ba, gate_x_w=wx_, gate_x_b=bx, lru_lambda=lam,
        pool_w=pw, pool_b=pb, pool_scale=ps, norm_lru_g=gl, norm_pool_g=gp, norm_ffn_g=gf, final_norm_g=row(gn))
    small_res, loss = _adam_small(
        small_parts,
        small(norm_mix_g, conv_w, conv_b, gate_a_w, gate_a_b, gate_x_w, gate_x_b, lru_lambda, pool_w, pool_b,
              pool_scale, norm_lru_g, norm_pool_g, norm_ffn_g, final_norm_g),
        small(m_norm_mix_g, m_conv_w, m_conv_b, m_gate_a_w, m_gate_a_b, m_gate_x_w, m_gate_x_b, m_lru_lambda, m_pool_w,
              m_pool_b, m_pool_scale, m_norm_lru_g, m_norm_pool_g, m_norm_ffn_g, m_final_norm_g),
        small(v_norm_mix_g, v_conv_w, v_conv_b, v_gate_a_w, v_gate_a_b, v_gate_x_w, v_gate_x_b, v_lru_lambda, v_pool_w,
              v_pool_b, v_pool_scale, v_norm_lru_g, v_norm_pool_g, v_norm_ffn_g, v_final_norm_g))
    for nm, outs in small_res.items():
        res[nm] = [o.reshape(D_MODEL) for o in outs] if nm == "final_norm_g" else list(outs)

    out = [loss, grad_x[None]]
    for kind in range(4):
        out += [res[nm][kind] for nm in WEIGHT_ORDER]
    return tuple(out)
```

```python
import jax
import jax.numpy as jnp
from jax import lax
from jax.experimental import pallas as pl
from jax.experimental.pallas import tpu as pltpu

F32 = jnp.float32
BF16 = jnp.bfloat16

D_MODEL = 1024
LRU_W = 512
POOL_W = 512
D_IN = 1536
D_FF = 2816
POOL_WINDOWS = (2, 4, 8, 16)
EPS = 1e-6
LRU_C = 8.0
N_DEV = 8
HALO = 16

ADAM_LR = 0.001
ADAM_B1 = 0.9
ADAM_B2 = 0.999
ADAM_EPS = 1e-08
ADAM_WD = 0.01
ADAM_STEP = 10

ROW_CW, ROW_CB, ROW_BA, ROW_BX, ROW_LAM, ROW_PB, ROW_PS, ROW_GL, ROW_GP = 0, 4, 5, 6, 7, 8, 9, 10, 11
SG_VEC, SG_WA, SG_WX, SG_WP, SG_ROWS = 0, 32, 160, 288, 544

NT = (((1,), (1,)), ((), ()))
TN = (((0,), (0,)), ((), ()))


def _sds(shape, dtype):
    return jax.ShapeDtypeStruct(shape, dtype)


def _sigmoid(x):
    return 0.5 * jnp.tanh(0.5 * x) + 0.5


def _gelu_parts(x):
    c = 0.7978845608028654
    inner = c * (x + 0.044715 * (x * x * x))
    th = jnp.tanh(inner)
    g = 0.5 * x * (1.0 + th)
    dg = 0.5 * (1.0 + th) + 0.5 * x * (1.0 - th * th) * (c * (1.0 + 3.0 * 0.044715 * (x * x)))
    return g, dg


def _window_sum(ext, w, back):
    n = ext.shape[0]
    s, k = ext, 1
    while k < w:
        s = s + pltpu.roll(s, k if back else n - k, 0)
        k *= 2
    return s


def _rstd(x):
    return lax.rsqrt(jnp.mean(x * x, axis=-1, keepdims=True) + EPS)


def _rms_bwd(dy, xhat, rstd, gain):
    dxh = dy * gain
    dx = rstd * (dxh - xhat * jnp.mean(dxh * xhat, axis=-1, keepdims=True))
    return dx, jnp.sum(dy * xhat, axis=0, keepdims=True)


def _bd(xb, w_ref):
    return jnp.concatenate(
        [jnp.dot(xb[:, :256], w_ref[0], preferred_element_type=F32),
         jnp.dot(xb[:, 256:], w_ref[1], preferred_element_type=F32)], axis=1)


def _bd_t(xb, w_ref):
    return jnp.concatenate(
        [lax.dot_general(xb[:, :256], w_ref[0], NT, preferred_element_type=F32),
         lax.dot_general(xb[:, 256:], w_ref[1], NT, preferred_element_type=F32)], axis=1)


def _bd_grad(xb, db):
    return jnp.stack(
        [lax.dot_general(xb[:, :256], db[:, :256], TN, preferred_element_type=F32),
         lax.dot_general(xb[:, 256:], db[:, 256:], TN, preferred_element_type=F32)], axis=0)


def _fill_block_diag(dst, src_ref):
    n, k, _ = src_ref.shape
    dst[...] = jnp.zeros(dst.shape, BF16)
    for b in range(n):
        p, q = divmod(b, 256 // k)
        dst[p, q * k:(q + 1) * k, q * k:(q + 1) * k] = src_ref[b].astype(BF16)


def _diag_pack(w, k):
    lane = lax.broadcasted_iota(jnp.int32, (k, 256), 1)
    out = w[0:k]
    for q in range(1, 256 // k):
        out = jnp.where(lane >= q * k, w[q * k:(q + 1) * k], out)
    return out


def _y_pos(b):
    return 4 * (b % 2) + b // 2


def _mixer_pre(e_lru, e_pool, pv, wa_ref, wx_ref, wp_ref, tm, t0):
    taps = [e_lru[pl.ds(HALO - 3 + k, tm), :] for k in range(4)]
    xc = pv[ROW_CB:ROW_CB + 1, :]
    for k in range(4):
        xc = xc + taps[k] * pv[ROW_CW + k:ROW_CW + k + 1, :]
    xcb = xc.astype(BF16)
    r = _sigmoid(_bd(xcb, wa_ref) + pv[ROW_BA:ROW_BA + 1, :])
    ig = _sigmoid(_bd(xcb, wx_ref) + pv[ROW_BX:ROW_BX + 1, :])
    z = -pv[ROW_LAM:ROW_LAM + 1, :]
    sp = jnp.maximum(z, 0.0) + jnp.log(1.0 + jnp.exp(-jnp.abs(z)))
    la = (-LRU_C * r) * sp
    a = jnp.exp(la)
    om = -jnp.tanh(la) * (1.0 + a * a)
    omc = jnp.maximum(om, 1e-12)
    rmult = lax.rsqrt(omc)
    mult = omc * rmult
    t = t0 + lax.broadcasted_iota(jnp.int32, (tm, 1), 0)
    parts, inv_cnts = [], []
    for g, w in enumerate(POOL_WINDOWS):
        ext = e_pool[:, pl.ds(128 * g, 128)]
        s = _window_sum(ext, w, back=True)[HALO:, :]
        inv_cnt = 1.0 / jnp.minimum(t + 1, w).astype(F32)
        inv_cnts.append(inv_cnt)
        parts.append(s * inv_cnt - ext[HALO:, :])
    pooled = jnp.concatenate(parts, axis=1)
    pooled_b = pooled.astype(BF16)
    zp = _bd(pooled_b, wp_ref) + pv[ROW_PB:ROW_PB + 1, :]
    return dict(taps=taps, xc=xc, xcb=xcb, r=r, ig=ig, sp=sp, a=a, om=om, mult=mult, rmult=rmult,
                pooled_b=pooled_b, zp=zp, inv_cnts=inv_cnts)


def _scan_tile(a_ref, b_ref, out_ref, carry, tm, reverse):
    row = lax.broadcasted_iota(jnp.int32, (8, LRU_W), 0)
    nblk = tm // 8

    def step(i, hin):
        blk = (nblk - 1 - i) if reverse else i
        r0 = pl.multiple_of(blk * 8, 8)
        av = a_ref[pl.ds(r0, 8), :]
        bv = b_ref[pl.ds(r0, 8), :]
        for d in (1, 2, 4):
            sh = (8 - d) if reverse else d
            a_s = pltpu.roll(av, sh, 0)
            b_s = pltpu.roll(bv, sh, 0)
            m = (row < 8 - d) if reverse else (row >= d)
            bv = jnp.where(m, av * b_s + bv, bv)
            av = jnp.where(m, av * a_s, av)
        hv = av * hin + bv
        out_ref[pl.ds(r0, 8), :] = hv
        edge = hv[0:1, :] if reverse else hv[7:8, :]
        return jnp.broadcast_to(edge, (8, LRU_W))

    return lax.fori_loop(0, nblk, step, carry)


MESH = pl.DeviceIdType.MESH
ANY = pl.BlockSpec(memory_space=pl.ANY)


def _place():
    x, y, c = lax.axis_index("x"), lax.axis_index("y"), lax.axis_index("c")
    chips = [(1 - x, y), (x, 1 - y), (1 - x, 1 - y)]
    return x, y, c, chips


class _Gather:
    def __init__(self, ins, outs, send_sems, recv_sems, local_sems, core_major=False):
        self.ins, self.outs, self.n = ins, outs, len(ins)
        self.send_sems, self.recv_sems, self.local_sems = send_sems, recv_sems, local_sems
        self.core_major = core_major

    @staticmethod
    def scratch(n):
        return [pltpu.SemaphoreType.DMA((7, n)), pltpu.SemaphoreType.DMA((7, n)), pltpu.SemaphoreType.DMA((n,))]

    def _slot(self, a, px, py, pc):
        return self.outs[a].at[4 * pc + 2 * px + py if self.core_major else 4 * px + 2 * py + pc]

    def _copy(self, a, k, block, to, src=None):
        return pltpu.make_async_remote_copy(
            src_ref=self._slot(a, *block) if src is None else src, dst_ref=self._slot(a, *block),
            send_sem=self.send_sems.at[k, a], recv_sem=self.recv_sems.at[k, a], device_id=to, device_id_type=MESH)

    def _mine(self, a):
        x, y, c, _ = _place()
        return pltpu.make_async_copy(self.ins[a], self._slot(a, x, y, c), self.local_sems.at[a])

    def _first(self, a):
        x, y, c, chips = _place()
        me = (x, y, c)
        return ([self._copy(a, 0, me, (x, y, 1 - c), src=self.ins[a])]
                + [self._copy(a, 1 + j, me, (*chip, c), src=self.ins[a]) for j, chip in enumerate(chips)])

    def start(self):
        for a in range(self.n):
            self._mine(a).start()
        for a in range(self.n):
            for cp in self._first(a):
                cp.start()

    def finish(self):
        x, y, c, chips = _place()
        me, sibling = (x, y, c), (x, y, 1 - c)
        passed = []
        for j, chip in enumerate(chips):
            for a in range(self.n):
                self._copy(a, 1 + j, (*chip, c), me).wait_recv()
                fwd = self._copy(a, 4 + j, (*chip, c), sibling)
                fwd.start()
                passed.append(fwd)
        for a in range(self.n):
            self._copy(a, 0, (x, y, 1 - c), me).wait_recv()
            for j, chip in enumerate(chips):
                self._copy(a, 4 + j, (*chip, 1 - c), me).wait_recv()
        for a in range(self.n):
            for cp in self._first(a):
                cp.wait_send()
        for cp in passed:
            cp.wait_send()
        for a in range(self.n):
            self._mine(a).wait()


def _all_gather(arrs, name):
    n = len(arrs)

    def body(*refs):
        g = _Gather(refs[:n], refs[n:2 * n], *refs[2 * n:])
        g.start()
        g.finish()

    return pl.pallas_call(
        body, name=name,
        out_shape=[_sds((N_DEV,) + a.shape, a.dtype) for a in arrs],
        in_specs=[ANY] * n, out_specs=[ANY] * n, scratch_shapes=_Gather.scratch(n),
    )(*arrs)


def _pair_exchange(arrs, name):
    n = len(arrs)

    def body(*refs):
        ins, outs = refs[:n], refs[n:2 * n]
        send_sems, recv_sems = refs[2 * n:]
        x, y, c, _ = _place()
        sibling = (x, y, 1 - c)
        sends = []
        for a in range(n):
            for j in range(4):
                cp = pltpu.make_async_remote_copy(
                    src_ref=ins[a].at[2 * j + (1 - c)], dst_ref=outs[a].at[j],
                    send_sem=send_sems.at[j, a], recv_sem=recv_sems.at[j, a], device_id=sibling, device_id_type=MESH)
                cp.start()
                sends.append(cp)
        for cp in sends:
            cp.wait()

    return pl.pallas_call(
        body, name=name,
        out_shape=[_sds((4,) + a.shape[1:], a.dtype) for a in arrs],
        in_specs=[ANY] * n, out_specs=[ANY] * n,
        scratch_shapes=[pltpu.SemaphoreType.DMA((4, n)), pltpu.SemaphoreType.DMA((4, n))],
    )(*arrs)


class _ChipExchange:
    def __init__(self, ins, outs, send_sems, recv_sems, local_sems):
        self.ins, self.outs, self.n = ins, outs, len(ins)
        self.send_sems, self.recv_sems, self.local_sems = send_sems, recv_sems, local_sems

    @staticmethod
    def scratch(n):
        return [pltpu.SemaphoreType.DMA((3, n)), pltpu.SemaphoreType.DMA((3, n)), pltpu.SemaphoreType.DMA((n,))]

    def _local(self, a):
        x, y, _, _ = _place()
        me = 2 * x + y
        return pltpu.make_async_copy(self.ins[a].at[me], self.outs[a].at[me], self.local_sems.at[a])

    def _copies(self, a):
        x, y, c, chips = _place()
        me = 2 * x + y
        return [(pltpu.make_async_remote_copy(
                     src_ref=self.ins[a].at[2 * px + py], dst_ref=self.outs[a].at[me],
                     send_sem=self.send_sems.at[k, a], recv_sem=self.recv_sems.at[k, a],
                     device_id=(px, py, c), device_id_type=MESH),
                 pltpu.make_async_remote_copy(
                     src_ref=self.ins[a].at[me], dst_ref=self.outs[a].at[2 * px + py],
                     send_sem=self.send_sems.at[k, a], recv_sem=self.recv_sems.at[k, a],
                     device_id=(px, py, c), device_id_type=MESH))
                for k, (px, py) in enumerate(chips)]

    def start(self):
        for a in range(self.n):
            self._local(a).start()
        for a in range(self.n):
            for send, _ in self._copies(a):
                send.start()

    def finish(self):
        for a in range(self.n):
            for send, recv in self._copies(a):
                send.wait_send()
                recv.wait_recv()
        for a in range(self.n):
            self._local(a).wait()


def _chip_exchange(arrs, name):
    n = len(arrs)

    def body(*refs):
        e = _ChipExchange(refs[:n], refs[n:2 * n], *refs[2 * n:])
        e.start()
        e.finish()

    return pl.pallas_call(
        body, name=name, out_shape=[_sds(a.shape, a.dtype) for a in arrs],
        in_specs=[ANY] * n, out_specs=[ANY] * n, scratch_shapes=_ChipExchange.scratch(n),
    )(*arrs)


def _mix_in(x, g_mix, w_in_t, tm, shards=()):
    T = x.shape[0]
    n_t = T // tm
    n_s = len(shards)

    def body(x_ref, g_ref, w_ref, *rest):
        sh_in, rest = rest[:n_s], rest[n_s:]
        u_ref, h_ref = rest[:2]
        gather = _Gather(sh_in, rest[2:2 + n_s], *rest[2 + n_s:], core_major=True) if n_s else None
        i = pl.program_id(0)

        if gather:
            @pl.when(i == 0)
            def _():
                gather.start()

        xv = x_ref[...]
        h = (xv * _rstd(xv) * g_ref[...]).astype(BF16)
        h_ref[...] = h
        u_ref[...] = lax.dot_general(h, w_ref[...], NT, preferred_element_type=F32)

        if gather:
            @pl.when(i == n_t - 1)
            def _():
                gather.finish()

    outs = pl.pallas_call(
        body, name="mix_in", grid=(n_t,),
        in_specs=[pl.BlockSpec((tm, D_MODEL), lambda i: (i, 0)),
                  pl.BlockSpec((1, D_MODEL), lambda i: (0, 0)),
                  pl.BlockSpec((D_IN, D_MODEL), lambda i: (0, 0))] + [ANY] * n_s,
        out_specs=[pl.BlockSpec((tm, D_IN), lambda i: (i, 0)),
                   pl.BlockSpec((tm, D_MODEL), lambda i: (i, 0))] + [ANY] * n_s,
        out_shape=[_sds((T, D_IN), F32), _sds((T, D_MODEL), BF16)] + [_sds((N_DEV,) + a.shape, a.dtype) for a in shards],
        scratch_shapes=_Gather.scratch(n_s) if n_s else [],
        compiler_params=pltpu.CompilerParams(dimension_semantics=("arbitrary",)),
    )(x, g_mix, w_in_t, *shards)
    return outs[0], outs[1], list(outs[2:])


def _mixer_fwd(u, pv, wa, wx, wp, tm, shards=()):
    T = u.shape[0]
    n_s = len(shards)
    n_t = T // tm

    def body(u_ref, pv_ref, wa_in, wx_in, wp_in, *rest):
        sh_in, rest = rest[:n_s], rest[n_s:]
        y_ref, hs_ref = rest[:2]
        sh_out, rest = rest[2:2 + n_s], rest[2 + n_s:]
        e_lru, e_pool, a_s, b_s, hc, wa_ref, wx_ref, wp_ref = rest[:8]
        gather = _Gather(sh_in, sh_out, *rest[8:], core_major=True) if n_s else None
        i = pl.program_id(0)

        @pl.when(i == 0)
        def _():
            if gather:
                gather.start()
            e_lru[pl.ds(0, HALO), :] = jnp.zeros((HALO, LRU_W), F32)
            e_pool[pl.ds(0, HALO), :] = jnp.zeros((HALO, POOL_W), F32)
            hc[...] = jnp.zeros((8, LRU_W), F32)
            _fill_block_diag(wa_ref, wa_in)
            _fill_block_diag(wx_ref, wx_in)
            _fill_block_diag(wp_ref, wp_in)

        e_lru[pl.ds(HALO, tm), :] = u_ref[:, 0:LRU_W]
        e_pool[pl.ds(HALO, tm), :] = u_ref[:, 2 * LRU_W:D_IN]
        pv = pv_ref[...]
        p = _mixer_pre(e_lru, e_pool, pv, wa_ref, wx_ref, wp_ref, tm, i * tm)
        a_s[...] = p["a"]
        b_s[...] = p["mult"] * (p["ig"] * p["xc"])
        hc[...] = _scan_tile(a_s, b_s, hs_ref, hc[...], tm, reverse=False)
        gl, _ = _gelu_parts(u_ref[:, LRU_W:2 * LRU_W])
        y_lru = hs_ref[...] * gl
        y_pool = p["zp"] * pv[ROW_PS:ROW_PS + 1, :]
        yn = jnp.concatenate([y_lru * _rstd(y_lru) * pv[ROW_GL:ROW_GL + 1, :],
                              y_pool * _rstd(y_pool) * pv[ROW_GP:ROW_GP + 1, :]], axis=1).astype(BF16)
        for b in range(N_DEV):
            y_ref[:, 128 * _y_pos(b):128 * (_y_pos(b) + 1)] = yn[:, 128 * b:128 * (b + 1)]
        e_lru[pl.ds(0, HALO), :] = e_lru[pl.ds(tm, HALO), :]
        e_pool[pl.ds(0, HALO), :] = e_pool[pl.ds(tm, HALO), :]

        if gather:
            @pl.when(i == n_t - 1)
            def _():
                gather.finish()

    full = lambda shape: pl.BlockSpec(shape, lambda i: (0,) * len(shape))
    outs = pl.pallas_call(
        body, name="mixer_fwd", grid=(n_t,),
        in_specs=[pl.BlockSpec((tm, D_IN), lambda i: (i, 0)), full((16, LRU_W)),
                  full((8, 64, 64)), full((8, 64, 64)), full((4, 128, 128))] + [ANY] * n_s,
        out_specs=[pl.BlockSpec((tm, D_MODEL), lambda i: (i, 0)), pl.BlockSpec((tm, LRU_W), lambda i: (i, 0))] + [ANY] * n_s,
        out_shape=[_sds((T, D_MODEL), BF16), _sds((T, LRU_W), F32)] + [_sds((N_DEV,) + a.shape, a.dtype) for a in shards],
        scratch_shapes=[pltpu.VMEM((HALO + tm, LRU_W), F32), pltpu.VMEM((HALO + tm, POOL_W), F32),
                        pltpu.VMEM((tm, LRU_W), F32), pltpu.VMEM((tm, LRU_W), F32), pltpu.VMEM((8, LRU_W), F32)]
        + [pltpu.VMEM((2, 256, 256), BF16)] * 3 + (_Gather.scratch(n_s) if n_s else []),
        compiler_params=pltpu.CompilerParams(dimension_semantics=("arbitrary",)),
    )(u, pv, wa, wx, wp, *shards)
    return outs[0], outs[1], list(outs[2:])


def _ffn_fwd(x, y, w_out_b, g_ffn, w1_b, w3_b, w2_b, g_fin, tgt, tm, tn):
    T = x.shape[0]
    n_j = D_FF // tn

    def body(x_ref, y_ref, wo_ref, gf_ref, w1_ref, w3_ref, w2_ref, gfin_ref, tgt_ref,
             hres_ref, h2_ref, g_ref, v_ref, d3_ref, loss_ref, dgfin_ref, acc):
        i, j = pl.program_id(0), pl.program_id(1)

        @pl.when(j == 0)
        def _():
            hr = x_ref[...] + jnp.dot(y_ref[...], wo_ref[...], preferred_element_type=F32)
            hres_ref[...] = hr
            h2_ref[...] = (hr * _rstd(hr) * gf_ref[...]).astype(BF16)
            acc[...] = jnp.zeros((tm, D_MODEL), F32)

        @pl.when((j == 0) & (i == 0))
        def _():
            loss_ref[...] = jnp.zeros((8, 128), F32)
            dgfin_ref[...] = jnp.zeros((1, D_MODEL), F32)

        h2 = h2_ref[...]
        g = lax.dot_general(h2, w1_ref[...], NT, preferred_element_type=F32)
        v = lax.dot_general(h2, w3_ref[...], NT, preferred_element_type=F32)
        g_ref[...] = g.astype(BF16)
        v_ref[...] = v.astype(BF16)
        ff = ((g * _sigmoid(g)) * v).astype(BF16)
        acc[...] += jnp.dot(ff, w2_ref[...], preferred_element_type=F32)

        @pl.when(j == n_j - 1)
        def _():
            h3 = hres_ref[...] + acc[...]
            rstd = _rstd(h3)
            xh = h3 * rstd
            gfin = gfin_ref[...]
            err = xh * gfin - tgt_ref[...]
            loss_ref[...] += 0.5 * jnp.sum(jnp.mean(err * err, axis=-1, keepdims=True))
            dout = err * (1.0 / D_MODEL)
            dx, dgain = _rms_bwd(dout, xh, rstd, gfin)
            d3_ref[...] = dx
            dgfin_ref[...] += dgain

    row = lambda w: pl.BlockSpec((tm, w), lambda i, j: (i, 0))
    const = lambda shape: pl.BlockSpec(shape, lambda i, j: (0,) * len(shape))
    return pl.pallas_call(
        body, name="ffn_fwd", grid=(T // tm, n_j),
        in_specs=[row(D_MODEL), row(D_MODEL), const((D_MODEL, D_MODEL)), const((1, D_MODEL)),
                  pl.BlockSpec((tn, D_MODEL), lambda i, j: (j, 0)), pl.BlockSpec((tn, D_MODEL), lambda i, j: (j, 0)),
                  pl.BlockSpec((tn, D_MODEL), lambda i, j: (j, 0)), const((1, D_MODEL)), row(D_MODEL)],
        out_specs=[row(D_MODEL), row(D_MODEL),
                   pl.BlockSpec((tm, tn), lambda i, j: (i, j)), pl.BlockSpec((tm, tn), lambda i, j: (i, j)),
                   row(D_MODEL), const((8, 128)), const((1, D_MODEL))],
        out_shape=[_sds((T, D_MODEL), F32), _sds((T, D_MODEL), BF16), _sds((T, D_FF), BF16), _sds((T, D_FF), BF16),
                   _sds((T, D_MODEL), F32), _sds((8, 128), F32), _sds((1, D_MODEL), F32)],
        scratch_shapes=[pltpu.VMEM((tm, D_MODEL), F32)],
        compiler_params=pltpu.CompilerParams(dimension_semantics=("arbitrary", "arbitrary")),
    )(x, y, w_out_b, g_ffn, w1_b, w3_b, w2_b, g_fin, tgt)


def _ffn_bwd(d3, g, v, w1_b, w3_b, w2_b, hres, g_ffn, tm, tn):
    T = d3.shape[0]
    n_j = D_FF // tn

    def body(d3_ref, g_ref, v_ref, w1_ref, w3_ref, w2_ref, hres_ref, gf_ref,
             dg_ref, dv_ref, ff_ref, d2_ref, dgffn_ref, acc):
        i, j = pl.program_id(0), pl.program_id(1)

        @pl.when(j == 0)
        def _():
            acc[...] = jnp.zeros((tm, D_MODEL), F32)

        @pl.when((j == 0) & (i == 0))
        def _():
            dgffn_ref[...] = jnp.zeros((1, D_MODEL), F32)

        dff = lax.dot_general(d3_ref[...].astype(BF16), w2_ref[...], NT, preferred_element_type=F32)
        gv = g_ref[...].astype(F32)
        vv = v_ref[...].astype(F32)
        sg = _sigmoid(gv)
        sl = gv * sg
        dgb = (dff * vv * (sg * (1.0 + gv * (1.0 - sg)))).astype(BF16)
        dvb = (dff * sl).astype(BF16)
        dg_ref[...] = dgb
        dv_ref[...] = dvb
        ff_ref[...] = (sl * vv).astype(BF16)
        acc[...] += (jnp.dot(dgb, w1_ref[...], preferred_element_type=F32)
                     + jnp.dot(dvb, w3_ref[...], preferred_element_type=F32))

        @pl.when(j == n_j - 1)
        def _():
            hr = hres_ref[...]
            rstd = _rstd(hr)
            dx, dgain = _rms_bwd(acc[...], hr * rstd, rstd, gf_ref[...])
            d2_ref[...] = d3_ref[...] + dx
            dgffn_ref[...] += dgain

    row = lambda w: pl.BlockSpec((tm, w), lambda i, j: (i, 0))
    tile = pl.BlockSpec((tm, tn), lambda i, j: (i, j))
    const = lambda shape: pl.BlockSpec(shape, lambda i, j: (0,) * len(shape))
    return pl.pallas_call(
        body, name="ffn_bwd", grid=(T // tm, n_j),
        in_specs=[row(D_MODEL), tile, tile,
                  pl.BlockSpec((tn, D_MODEL), lambda i, j: (j, 0)), pl.BlockSpec((tn, D_MODEL), lambda i, j: (j, 0)),
                  pl.BlockSpec((tn, D_MODEL), lambda i, j: (j, 0)), row(D_MODEL), const((1, D_MODEL))],
        out_specs=[tile, tile, tile, row(D_MODEL), const((1, D_MODEL))],
        out_shape=[_sds((T, D_FF), BF16), _sds((T, D_FF), BF16), _sds((T, D_FF), BF16),
                   _sds((T, D_MODEL), F32), _sds((1, D_MODEL), F32)],
        scratch_shapes=[pltpu.VMEM((tm, D_MODEL), F32)],
        compiler_params=pltpu.CompilerParams(dimension_semantics=("arbitrary", "arbitrary")),
    )(d3, g, v, w1_b, w3_b, w2_b, hres, g_ffn)


def _at_b(a, b, name, tmm, tn, tk, gather=()):
    T, M = a.shape
    N = b.shape[1]
    n_m, n_n, n_k = M // tmm, N // tn, T // tk
    n_g = len(gather)

    def body(a_ref, b_ref, *rest):
        g_in, o_ref, rest = rest[:n_g], rest[n_g], rest[n_g + 1:]
        ag = _Gather(g_in, rest[:n_g], *rest[n_g:]) if n_g else None
        m, n, k = pl.program_id(0), pl.program_id(1), pl.program_id(2)

        if ag:
            @pl.when((m == 0) & (n == 0) & (k == 0))
            def _():
                ag.start()

        @pl.when(k == 0)
        def _():
            o_ref[...] = jnp.zeros((tmm, tn), F32)

        o_ref[...] += lax.dot_general(a_ref[...].astype(BF16), b_ref[...].astype(BF16), TN,
                                      preferred_element_type=F32)

        if ag:
            @pl.when((m == n_m - 1) & (n == n_n - 1) & (k == n_k - 1))
            def _():
                ag.finish()

    outs = pl.pallas_call(
        body, name=name, grid=(n_m, n_n, n_k),
        in_specs=[pl.BlockSpec((tk, tmm), lambda m, n, k: (k, m)), pl.BlockSpec((tk, tn), lambda m, n, k: (k, n))]
        + [ANY] * n_g,
        out_specs=[pl.BlockSpec((tmm, tn), lambda m, n, k: (m, n))] + [ANY] * n_g,
        out_shape=[_sds((M, N), F32)] + [_sds((N_DEV,) + g.shape, g.dtype) for g in gather],
        scratch_shapes=_Gather.scratch(n_g) if n_g else [],
        compiler_params=pltpu.CompilerParams(
            dimension_semantics=("arbitrary",) * 3 if n_g else ("parallel", "parallel", "arbitrary")),
    )(a, b, *gather)
    return (outs[0], list(outs[1:])) if n_g else outs[0]


def _at_b_pair(a, b, c_arr, name, tk):
    T, M = a.shape
    N = b.shape[1]
    hm, n_k = M // 2, T // tk

    def body(c_ref, a_ref, b_ref, o_ref, acc, landed, send_sem, recv_sem):
        ph, k = pl.program_id(0), pl.program_id(1)
        def hand_over():
            x, y, c, _ = _place()
            return pltpu.make_async_remote_copy(
                src_ref=acc.at[0], dst_ref=landed, send_sem=send_sem, recv_sem=recv_sem,
                device_id=(x, y, 1 - c), device_id_type=MESH)

        prod = lax.dot_general(a_ref[...].astype(BF16), b_ref[...].astype(BF16), TN, preferred_element_type=F32)
        for half in range(2):
            @pl.when((ph == half) & (k == 0))
            def _():
                acc[half] = prod

            @pl.when((ph == half) & (k > 0))
            def _():
                acc[half] += prod

        @pl.when((ph == 0) & (k == n_k - 1))
        def _():
            hand_over().start()

        @pl.when((ph == 1) & (k == n_k - 1))
        def _():
            copy = hand_over()
            copy.wait_recv()
            o_ref[...] = (acc[1] + landed[...]).astype(BF16)
            copy.wait_send()

    return pl.pallas_call(
        body, name=name,
        grid_spec=pltpu.PrefetchScalarGridSpec(
            num_scalar_prefetch=1, grid=(2, n_k),
            in_specs=[pl.BlockSpec((tk, hm), lambda ph, k, c_ref: (k, (ph + 1 - c_ref[0]) % 2)),
                      pl.BlockSpec((tk, N), lambda ph, k, c_ref: (k, 0))],
            out_specs=pl.BlockSpec((hm, N), lambda ph, k, c_ref: (0, 0)),
            scratch_shapes=[pltpu.VMEM((2, hm, N), F32), pltpu.VMEM((hm, N), F32),
                            pltpu.SemaphoreType.DMA, pltpu.SemaphoreType.DMA]),
        out_shape=_sds((hm, N), BF16),
        compiler_params=pltpu.CompilerParams(dimension_semantics=("arbitrary", "arbitrary")),
    )(c_arr, a, b)


def _mixer_bwd(d2, u, hs, pv, wa, wx, wp, w_out_b, tm, chip_sums=()):
    T = u.shape[0]
    n_t = T // tm
    n_x = len(chip_sums)

    def body(d2_ref, u_ref, uh_ref, hs_ref, hh_ref, pv_ref, wa_in, wx_in, wp_in, wo_ref, *rest):
        x_in, rest = rest[:n_x], rest[n_x:]
        du_ref, sg_ref = rest[:2]
        x_out, rest = rest[2:2 + n_x], rest[2 + n_x:]
        e_lru, e_pool, e_h, a_s, b_s, mu_s, f_x, f_p, mc, cx, cp = rest[:11]
        wa_ref, wx_ref, wp_ref, vacc_ref, dwa_ref, dwx_ref, dwp_ref = rest[11:18]
        exchange = _ChipExchange(x_in, x_out, *rest[18:]) if n_x else None
        s = pl.program_id(0)
        it = n_t - 1 - s

        @pl.when(s == 0)
        def _():
            if exchange:
                exchange.start()
            mc[...] = jnp.zeros((8, LRU_W), F32)
            cx[...] = jnp.zeros((8, LRU_W), F32)
            cp[...] = jnp.zeros((HALO, POOL_W), F32)
            vacc_ref[...] = jnp.zeros((16, LRU_W), F32)
            dwa_ref[...] = jnp.zeros((2, 256, 256), F32)
            dwx_ref[...] = jnp.zeros((2, 256, 256), F32)
            dwp_ref[...] = jnp.zeros((2, 256, 256), F32)
            _fill_block_diag(wa_ref, wa_in)
            _fill_block_diag(wx_ref, wx_in)
            _fill_block_diag(wp_ref, wp_in)

        first = it == 0
        e_lru[pl.ds(0, HALO), :] = jnp.where(first, 0.0, uh_ref[:, 0:LRU_W])
        e_pool[pl.ds(0, HALO), :] = jnp.where(first, 0.0, uh_ref[:, 2 * LRU_W:D_IN])
        e_lru[pl.ds(HALO, tm), :] = u_ref[:, 0:LRU_W]
        e_pool[pl.ds(HALO, tm), :] = u_ref[:, 2 * LRU_W:D_IN]
        e_h[pl.ds(0, 8), :] = jnp.where(first, 0.0, hh_ref[...])
        e_h[pl.ds(8, tm), :] = hs_ref[...]
        pv = pv_ref[...]
        p = _mixer_pre(e_lru, e_pool, pv, wa_ref, wx_ref, wp_ref, tm, it * tm)
        a, xc, ig, r, mult = p["a"], p["xc"], p["ig"], p["r"], p["mult"]

        dyn = lax.dot_general(d2_ref[...].astype(BF16), wo_ref[...], NT, preferred_element_type=F32)
        dyn = jnp.concatenate([dyn[:, 128 * _y_pos(b):128 * (_y_pos(b) + 1)] for b in range(N_DEV)], axis=1)

        h = hs_ref[...]
        ug = u_ref[:, LRU_W:2 * LRU_W]
        gl, dgl = _gelu_parts(ug)
        y_lru = h * gl
        rstd_l = _rstd(y_lru)
        dy_lru, d_gain_l = _rms_bwd(dyn[:, 0:LRU_W], y_lru * rstd_l, rstd_l, pv[ROW_GL:ROW_GL + 1, :])
        dh = dy_lru * gl
        du_ref[:, LRU_W:2 * LRU_W] = (dy_lru * h * dgl).astype(BF16)
        a_s[...] = a
        b_s[...] = a * dh
        mu_s[pl.ds(tm, 8), :] = mc[...]
        mc[...] = _scan_tile(a_s, b_s, mu_s, mc[...], tm, reverse=True)
        lam_t = dh + mu_s[pl.ds(1, tm), :]
        da = lam_t * e_h[pl.ds(7, tm), :]
        dmult = lam_t * (ig * xc)
        di = lam_t * (mult * xc)
        dxc = lam_t * (mult * ig)
        dla = da * a - jnp.where(p["om"] > 1e-12, dmult * ((a * a) * p["rmult"]), 0.0)
        dra = (dla * (-LRU_C * p["sp"])) * (r * (1.0 - r))
        dia = di * (ig * (1.0 - ig))
        drab = dra.astype(BF16)
        diab = dia.astype(BF16)
        dxc = dxc + _bd_t(drab, wa_ref) + _bd_t(diab, wx_ref)
        dwa_ref[...] += _bd_grad(p["xcb"], drab)
        dwx_ref[...] += _bd_grad(p["xcb"], diab)
        sig_neg_lam = _sigmoid(-pv[ROW_LAM:ROW_LAM + 1, :])
        d_lam = jnp.sum(dla * r, axis=0, keepdims=True) * (LRU_C * sig_neg_lam)

        f_x[pl.ds(0, tm), :] = dxc
        f_x[pl.ds(tm, 8), :] = cx[...]
        du_lru = jnp.zeros((tm, LRU_W), F32)
        d_cw = []
        for k in range(4):
            du_lru = du_lru + f_x[pl.ds(3 - k, tm), :] * pv[ROW_CW + k:ROW_CW + k + 1, :]
            d_cw.append(jnp.sum(dxc * p["taps"][k], axis=0, keepdims=True))
        du_ref[:, 0:LRU_W] = du_lru.astype(BF16)
        cx[...] = f_x[pl.ds(0, 8), :]

        zp = p["zp"]
        ps = pv[ROW_PS:ROW_PS + 1, :]
        y_pool = zp * ps
        rstd_p = _rstd(y_pool)
        dy_pool, d_gain_p = _rms_bwd(dyn[:, LRU_W:D_MODEL], y_pool * rstd_p, rstd_p, pv[ROW_GP:ROW_GP + 1, :])
        dz = dy_pool * ps
        dzb = dz.astype(BF16)
        dwp_ref[...] += _bd_grad(p["pooled_b"], dzb)
        dpooled = _bd_t(dzb, wp_ref)
        for g, w in enumerate(POOL_WINDOWS):
            f_p[pl.ds(0, tm), pl.ds(128 * g, 128)] = dpooled[:, 128 * g:128 * (g + 1)] * p["inv_cnts"][g]
        f_p[pl.ds(tm, HALO), :] = cp[...]
        for g, w in enumerate(POOL_WINDOWS):
            acc = _window_sum(f_p[:, pl.ds(128 * g, 128)], w, back=False)[0:tm, :]
            du_ref[:, 2 * LRU_W + 128 * g:2 * LRU_W + 128 * (g + 1)] = (
                acc - dpooled[:, 128 * g:128 * (g + 1)]).astype(BF16)
        cp[...] = f_p[pl.ds(0, HALO), :]

        rows = d_cw + [
            jnp.sum(dxc, axis=0, keepdims=True),
            jnp.sum(dra, axis=0, keepdims=True),
            jnp.sum(dia, axis=0, keepdims=True),
            d_lam,
            jnp.sum(dz, axis=0, keepdims=True),
            jnp.sum(dy_pool * zp, axis=0, keepdims=True),
            d_gain_l, d_gain_p,
            jnp.zeros((4, LRU_W), F32),
        ]
        vacc_ref[...] += jnp.concatenate(rows, axis=0)

        @pl.when(s == n_t - 1)
        def _():
            sg_ref[SG_VEC:SG_VEC + 16, :] = vacc_ref[:, 0:256]
            sg_ref[SG_VEC + 16:SG_VEC + 32, :] = vacc_ref[:, 256:512]
            for half in range(2):
                sg_ref[SG_WA + 64 * half:SG_WA + 64 * (half + 1), :] = _diag_pack(dwa_ref[half], 64)
                sg_ref[SG_WX + 64 * half:SG_WX + 64 * (half + 1), :] = _diag_pack(dwx_ref[half], 64)
                sg_ref[SG_WP + 128 * half:SG_WP + 128 * (half + 1), :] = _diag_pack(dwp_ref[half], 128)
            if exchange:
                exchange.finish()

    rev = lambda w: pl.BlockSpec((tm, w), lambda s: (n_t - 1 - s, 0))
    full = lambda shape: pl.BlockSpec(shape, lambda s: (0,) * len(shape))
    outs = pl.pallas_call(
        body, name="mixer_bwd", grid=(n_t,),
        in_specs=[rev(D_MODEL), rev(D_IN),
                  pl.BlockSpec((HALO, D_IN), lambda s: (jnp.maximum((n_t - 1 - s) * (tm // HALO) - 1, 0), 0)),
                  rev(LRU_W),
                  pl.BlockSpec((8, LRU_W), lambda s: (jnp.maximum((n_t - 1 - s) * (tm // 8) - 1, 0), 0)),
                  full((16, LRU_W)), full((8, 64, 64)), full((8, 64, 64)), full((4, 128, 128)),
                  full((D_MODEL, D_MODEL))] + [ANY] * n_x,
        out_specs=[rev(D_IN), full((SG_ROWS, 256))] + [ANY] * n_x,
        out_shape=[_sds((T, D_IN), BF16), _sds((SG_ROWS, 256), F32)] + [_sds(a.shape, a.dtype) for a in chip_sums],
        scratch_shapes=[pltpu.VMEM((HALO + tm, LRU_W), F32), pltpu.VMEM((HALO + tm, POOL_W), F32),
                        pltpu.VMEM((8 + tm, LRU_W), F32), pltpu.VMEM((tm, LRU_W), F32), pltpu.VMEM((tm, LRU_W), F32),
                        pltpu.VMEM((tm + 8, LRU_W), F32), pltpu.VMEM((tm + 8, LRU_W), F32),
                        pltpu.VMEM((tm + HALO, POOL_W), F32), pltpu.VMEM((8, LRU_W), F32),
                        pltpu.VMEM((8, LRU_W), F32), pltpu.VMEM((HALO, POOL_W), F32)]
        + [pltpu.VMEM((2, 256, 256), BF16)] * 3 + [pltpu.VMEM((16, LRU_W), F32)] + [pltpu.VMEM((2, 256, 256), F32)] * 3
        + (_ChipExchange.scratch(n_x) if n_x else []),
        compiler_params=pltpu.CompilerParams(dimension_semantics=("arbitrary",)),
    )(d2, u, u, hs, hs, pv, wa, wx, wp, w_out_b, *chip_sums)
    return outs[0], outs[1], list(outs[2:])


def _mix_in_bwd(du, x, d2, w_in_t, g_mix, tm, chip_sums=()):
    T = x.shape[0]
    n_t = T // tm
    n_x = len(chip_sums)

    def body(du_ref, x_ref, d2_ref, w_ref, g_ref, *rest):
        x_in, rest = rest[:n_x], rest[n_x:]
        dx_ref, dg_ref = rest[:2]
        exchange = _ChipExchange(x_in, rest[2:2 + n_x], *rest[2 + n_x:]) if n_x else None
        i = pl.program_id(0)

        @pl.when(i == 0)
        def _():
            dg_ref[...] = jnp.zeros((1, D_MODEL), F32)
            if exchange:
                exchange.start()

        dh = jnp.dot(du_ref[...], w_ref[...], preferred_element_type=F32)
        xv = x_ref[...]
        rstd = _rstd(xv)
        dx, dgain = _rms_bwd(dh, xv * rstd, rstd, g_ref[...])
        dx_ref[...] = d2_ref[...] + dx
        dg_ref[...] += dgain

        if exchange:
            @pl.when(i == n_t - 1)
            def _():
                exchange.finish()

    row = lambda w: pl.BlockSpec((tm, w), lambda i: (i, 0))
    const = lambda shape: pl.BlockSpec(shape, lambda i: (0,) * len(shape))
    outs = pl.pallas_call(
        body, name="mix_in_bwd", grid=(n_t,),
        in_specs=[row(D_IN), row(D_MODEL), row(D_MODEL), const((D_IN, D_MODEL)), const((1, D_MODEL))] + [ANY] * n_x,
        out_specs=[row(D_MODEL), const((1, D_MODEL))] + [ANY] * n_x,
        out_shape=[_sds((T, D_MODEL), F32), _sds((1, D_MODEL), F32)] + [_sds(a.shape, a.dtype) for a in chip_sums],
        scratch_shapes=_ChipExchange.scratch(n_x) if n_x else [],
        compiler_params=pltpu.CompilerParams(dimension_semantics=("arbitrary",)),
    )(du, x, d2, w_in_t, g_mix, *chip_sums)
    return outs[0], outs[1], list(outs[2:])


def _pair_sum(g, r1, c_arr, name):
    _, _, R, C = g.shape
    tr = R if R <= 512 else 256

    def body(c_ref, g_ref, r_ref, o_ref):
        o_ref[...] = (g_ref[...] + r_ref[...]).astype(BF16)

    return pl.pallas_call(
        body, name=name,
        grid_spec=pltpu.PrefetchScalarGridSpec(
            num_scalar_prefetch=1, grid=(4, R // tr),
            in_specs=[pl.BlockSpec((None, None, tr, C), lambda j, i, c_ref: (j, c_ref[0], i, 0)),
                      pl.BlockSpec((None, tr, C), lambda j, i, c_ref: (j, i, 0))],
            out_specs=pl.BlockSpec((None, tr, C), lambda j, i, c_ref: (j, i, 0))),
        out_shape=_sds((4, R, C), BF16),
    )(c_arr, g, r1)


def _adamw(w, g, m, v):
    m = ADAM_B1 * m + (1.0 - ADAM_B1) * g
    v = ADAM_B2 * v + (1.0 - ADAM_B2) * (g * g)
    m_hat = m / (1.0 - ADAM_B1 ** ADAM_STEP)
    v_hat = v / (1.0 - ADAM_B2 ** ADAM_STEP)
    delta = -ADAM_LR * (m_hat / (jnp.sqrt(v_hat) + ADAM_EPS) + ADAM_WD * w)
    return delta, m, v


def _adam_shard(w, m, v, parts, name):
    R, C = w.shape
    tr = R if R <= 512 else 256

    def body(w_ref, m_ref, v_ref, p_ref, g_ref, d_ref, nm_ref, nv_ref):
        g = p_ref[0].astype(F32)
        for j in range(1, 4):
            g = g + p_ref[j].astype(F32)
        delta, nm, nv = _adamw(w_ref[...], g, m_ref[...], v_ref[...])
        g_ref[...] = g
        d_ref[...] = delta
        nm_ref[...] = nm
        nv_ref[...] = nv

    blk = pl.BlockSpec((tr, C), lambda i: (i, 0))
    return pl.pallas_call(
        body, name=name, grid=(R // tr,),
        in_specs=[blk, blk, blk, pl.BlockSpec((4, tr, C), lambda i: (0, i, 0))],
        out_specs=[blk] * 4, out_shape=[_sds((R, C), F32)] * 4,
        compiler_params=pltpu.CompilerParams(dimension_semantics=("parallel",)),
    )(w, m, v, parts)


SMALL_PARAMS = [("norm_mix_g", (1, D_MODEL)), ("conv_w", (1, 4, 64)), ("conv_b", (1, LRU_W)),
                ("gate_a_w", (1, 8, 64, 64)), ("gate_a_b", (1, LRU_W)), ("gate_x_w", (1, 8, 64, 64)),
                ("gate_x_b", (1, LRU_W)), ("lru_lambda", (1, LRU_W)), ("pool_w", (1, 4, 128, 128)),
                ("pool_b", (1, POOL_W)), ("pool_scale", (1, POOL_W)), ("norm_lru_g", (1, LRU_W)),
                ("norm_pool_g", (1, POOL_W)), ("norm_ffn_g", (1, D_MODEL)), ("final_norm_g", (1, D_MODEL))]
VEC_ROW = dict(conv_b=ROW_CB, gate_a_b=ROW_BA, gate_x_b=ROW_BX, lru_lambda=ROW_LAM, pool_b=ROW_PB, pool_scale=ROW_PS,
               norm_lru_g=ROW_GL, norm_pool_g=ROW_GP)
WHOLE = (Ellipsis,)


def _unpack_mixer_grads(sg, dev):
    vec = jnp.concatenate([sg[SG_VEC:SG_VEC + 16], sg[SG_VEC + 16:SG_VEC + 32]], axis=1)
    out = {nm: [(WHOLE, vec[r:r + 1])] for nm, r in VEC_ROW.items()}
    own = jnp.zeros((4, 64), F32)
    for d in range(N_DEV):
        own = jnp.where(dev == d, vec[ROW_CW:ROW_CW + 4, 64 * d:64 * (d + 1)], own)
    out["conv_w"] = [((0,), own)]
    for nm, row0 in (("gate_a_w", SG_WA), ("gate_x_w", SG_WX)):
        out[nm] = [((0, b), sg[row0 + 64 * (b // 4):row0 + 64 * (b // 4 + 1), 64 * (b % 4):64 * (b % 4 + 1)])
                   for b in range(8)]
    out["pool_w"] = [((0, b), sg[SG_WP + 128 * (b // 2):SG_WP + 128 * (b // 2 + 1), 128 * (b % 2):128 * (b % 2 + 1)])
                     for b in range(4)]
    return out


def _adam_small(parts, w, m, v):
    names = [nm for nm, _ in SMALL_PARAMS]
    n = len(names)

    def body(sg_ref, gm_ref, gf_ref, gn_ref, ls_ref, *rest):
        w_refs, m_refs, v_refs, outs = rest[:n], rest[n:2 * n], rest[2 * n:3 * n], rest[3 * n:]
        dev = 4 * lax.axis_index("x") + 2 * lax.axis_index("y") + lax.axis_index("c")

        def total(ref):
            acc = ref[0]
            for d in range(1, N_DEV):
                acc = acc + ref[d]
            return acc

        pieces = _unpack_mixer_grads(total(sg_ref), dev)
        pieces["norm_mix_g"] = [(WHOLE, total(gm_ref))]
        pieces["norm_ffn_g"] = [(WHOLE, total(gf_ref))]
        pieces["final_norm_g"] = [(WHOLE, total(gn_ref))]
        for i, nm in enumerate(names):
            for idx, g in pieces[nm]:
                delta, new_m, new_v = _adamw(w_refs[i][idx], g, m_refs[i][idx], v_refs[i][idx])
                for kind, val in enumerate((g, delta, new_m, new_v)):
                    outs[4 * i + kind][idx] = val
        outs[4 * n][...] = total(ls_ref)

    shapes = [_sds(shape, F32) for _, shape in SMALL_PARAMS for _ in range(4)] + [_sds((8, 128), F32)]
    res = pl.pallas_call(body, name="adam_small", out_shape=shapes)(
        *parts, *[w[nm] for nm in names], *[m[nm] for nm in names], *[v[nm] for nm in names])
    return {nm: tuple(res[4 * i:4 * i + 4]) for i, nm in enumerate(names)}, res[4 * n][0, 0]


def _vec_rows(conv_w_full, conv_b, ba, bx, lam, pb, ps, gl, gp):
    return jnp.concatenate([conv_w_full, conv_b, ba, bx, lam, pb, ps, gl, gp, jnp.zeros((4, LRU_W), F32)], axis=0)


WEIGHT_ORDER = ['norm_mix_g', 'w_in', 'conv_w', 'conv_b', 'gate_a_w', 'gate_a_b', 'gate_x_w', 'gate_x_b', 'lru_lambda',
                'pool_w', 'pool_b', 'pool_scale', 'norm_lru_g', 'norm_pool_g', 'w_out', 'norm_ffn_g', 'ffn_w1', 'ffn_w3',
                'ffn_w2', 'final_norm_g']


def kernel(x, norm_mix_g, w_in, conv_w, conv_b, gate_a_w, gate_a_b, gate_x_w, gate_x_b, lru_lambda, pool_w, pool_b, pool_scale, norm_lru_g, norm_pool_g, w_out, norm_ffn_g, ffn_w1, ffn_w3, ffn_w2, final_norm_g, loss_target, m_norm_mix_g, m_w_in, m_conv_w, m_conv_b, m_gate_a_w, m_gate_a_b, m_gate_x_w, m_gate_x_b, m_lru_lambda, m_pool_w, m_pool_b, m_pool_scale, m_norm_lru_g, m_norm_pool_g, m_w_out, m_norm_ffn_g, m_ffn_w1, m_ffn_w3, m_ffn_w2, m_final_norm_g, v_norm_mix_g, v_w_in, v_conv_w, v_conv_b, v_gate_a_w, v_gate_a_b, v_gate_x_w, v_gate_x_b, v_lru_lambda, v_pool_w, v_pool_b, v_pool_scale, v_norm_lru_g, v_norm_pool_g, v_w_out, v_norm_ffn_g, v_ffn_w1, v_ffn_w3, v_ffn_w2, v_final_norm_g):
    ac = lax.axis_index("c")
    tm, tmx, tn, tk = 512, 256, 1408, 1024
    xs, tgt = x[0], loss_target[0]
    g_fin = final_norm_g.reshape(1, D_MODEL)
    c_arr = jnp.reshape(ac, (1,)).astype(jnp.int32)

    def pair_sums(blocks, names):
        from_sibling = _pair_exchange(blocks, "grads_to_sibling_" + names[0])
        return [_pair_sum(g.reshape((4, 2) + g.shape[1:]), r, c_arr, "pair_sum_" + nm)
                for g, r, nm in zip(blocks, from_sibling, names)]

    tr = lambda w: jnp.swapaxes(w[0], 0, 1)
    own = lambda w: w[0]
    bf = lambda a: a.astype(BF16)

    g_in, g_conv = _all_gather([bf(tr(w_in)), conv_w[0]], "gather_w_in")
    w_in_t = g_in.reshape(D_IN, D_MODEL)
    conv_w_full = g_conv.transpose(1, 0, 2).reshape(4, LRU_W)
    pv = _vec_rows(conv_w_full, conv_b, gate_a_b, gate_x_b, lru_lambda, pool_b, pool_scale, norm_lru_g, norm_pool_g)
    wa, wx, wp = gate_a_w[0], gate_x_w[0], pool_w[0]

    u, h1, (g_out, g_w1) = _mix_in(xs, norm_mix_g, w_in_t, tm, shards=[bf(own(w_out)), bf(tr(ffn_w1))])
    y, hs, (g_w3, g_w2) = _mixer_fwd(u, pv, wa, wx, wp, tmx, shards=[bf(tr(ffn_w3)), bf(own(ffn_w2))])
    w_out_b = g_out.reshape(D_MODEL, D_MODEL)
    w1_t, w3_t, w2_b = g_w1.reshape(D_FF, D_MODEL), g_w3.reshape(D_FF, D_MODEL), g_w2.reshape(D_FF, D_MODEL)
    hres, h2, g, v, d3, loss_acc, d_gfin = _ffn_fwd(xs, y, w_out_b, norm_ffn_g, w1_t, w3_t, w2_b, g_fin, tgt, tm, tn)

    dg, dv, ff, d2, d_gffn = _ffn_bwd(d3, g, v, w1_t, w3_t, w2_b, hres, norm_ffn_g, tm, tn)
    blocks = lambda a: a.reshape(N_DEV, a.shape[0] // N_DEV, a.shape[1])
    chips = lambda a: a.reshape(4, a.shape[0] // 4, a.shape[1])
    early_sums = [chips(_at_b_pair(y, d2, c_arr, "grad_w_out", tk)), chips(_at_b_pair(dg, h2, c_arr, "grad_w1", tk)),
                  chips(_at_b_pair(dv, h2, c_arr, "grad_w3", tk)), chips(_at_b_pair(ff, d3, c_arr, "grad_w2", tk))]
    du, d_mixer, early_parts = _mixer_bwd(d2, u, hs, pv, wa, wx, wp, w_out_b, tmx, chip_sums=early_sums)
    grad_x, d_gmix, _ = _mix_in_bwd(du, xs, d2, w_in_t, norm_mix_g, tm)
    d_win, small_parts = _at_b(du, h1, "grad_w_in", D_IN, D_MODEL, tk,
                               gather=[d_mixer, d_gmix, d_gffn, d_gfin, loss_acc])
    win_parts = _chip_exchange(pair_sums([blocks(d_win)], ["w_in"]), "grads_to_chips_w_in")
    parts = list(win_parts) + list(early_parts)

    res = {}
    shard_w = dict(w_in=(w_in, m_w_in, v_w_in, tr), w_out=(w_out, m_w_out, v_w_out, own),
                   ffn_w1=(ffn_w1, m_ffn_w1, v_ffn_w1, tr), ffn_w3=(ffn_w3, m_ffn_w3, v_ffn_w3, tr),
                   ffn_w2=(ffn_w2, m_ffn_w2, v_ffn_w2, own))
    for (nm, (w, m, v, view)), p in zip(shard_w.items(), parts):
        outs = _adam_shard(view(w), view(m), view(v), p, "adam_" + nm)
        res[nm] = [(jnp.swapaxes(o, 0, 1) if view is tr else o)[None] for o in outs]

    row = lambda a: a.reshape(1, D_MODEL)
    small = lambda gm, cw, cb, wa_, ba, wx_, bx, lam, pw, pb, ps, gl, gp, gf, gn: dict(
        norm_mix_g=gm, conv_w=cw, conv_b=cb, gate_a_w=wa_, gate_a_b=ba, gate_x_w=wx_, gate_x_b=bx, lru_lambda=lam,
        pool_w=pw, pool_b=pb, pool_scale=ps, norm_lru_g=gl, norm_pool_g=gp, norm_ffn_g=gf, final_norm_g=row(gn))
    small_res, loss = _adam_small(
        small_parts,
        small(norm_mix_g, conv_w, conv_b, gate_a_w, gate_a_b, gate_x_w, gate_x_b, lru_lambda, pool_w, pool_b,
              pool_scale, norm_lru_g, norm_pool_g, norm_ffn_g, final_norm_g),
        small(m_norm_mix_g, m_conv_w, m_conv_b, m_gate_a_w, m_gate_a_b, m_gate_x_w, m_gate_x_b, m_lru_lambda, m_pool_w,
              m_pool_b, m_pool_scale, m_norm_lru_g, m_norm_pool_g, m_norm_ffn_g, m_final_norm_g),
        small(v_norm_mix_g, v_conv_w, v_conv_b, v_gate_a_w, v_gate_a_b, v_gate_x_w, v_gate_x_b, v_lru_lambda, v_pool_w,
              v_pool_b, v_pool_scale, v_norm_lru_g, v_norm_pool_g, v_norm_ffn_g, v_final_norm_g))
    for nm, outs in small_res.items():
        res[nm] = [o.reshape(D_MODEL) for o in outs] if nm == "final_norm_g" else list(outs)

    out = [loss, grad_x[None]]
    for kind in range(4):
        out += [res[nm][kind] for nm in WEIGHT_ORDER]
    return tuple(out)
```

```python
import jax
import jax.numpy as jnp
from jax import lax
from jax.experimental import pallas as pl
from jax.experimental.pallas import tpu as pltpu

F32 = jnp.float32
BF16 = jnp.bfloat16

D_MODEL = 1024
LRU_W = 512
POOL_W = 512
D_IN = 1536
D_FF = 2816
POOL_WINDOWS = (2, 4, 8, 16)
EPS = 1e-6
LRU_C = 8.0
N_DEV = 8
HALO = 16
SCAN_UNROLL = 4

ADAM_LR = 0.001
ADAM_B1 = 0.9
ADAM_B2 = 0.999
ADAM_EPS = 1e-08
ADAM_WD = 0.01
ADAM_STEP = 10

ROW_CW, ROW_CB, ROW_BA, ROW_BX, ROW_LAM, ROW_PB, ROW_PS, ROW_GL, ROW_GP = 0, 4, 5, 6, 7, 8, 9, 10, 11
SG_VEC, SG_WA, SG_WX, SG_WP, SG_ROWS = 0, 32, 160, 288, 544

NT = (((1,), (1,)), ((), ()))
TN = (((0,), (0,)), ((), ()))


def _sds(shape, dtype):
    return jax.ShapeDtypeStruct(shape, dtype)


def _sigmoid(x):
    return 0.5 * jnp.tanh(0.5 * x) + 0.5


def _gelu_parts(x):
    c = 0.7978845608028654
    inner = c * (x + 0.044715 * (x * x * x))
    th = jnp.tanh(inner)
    g = 0.5 * x * (1.0 + th)
    dg = 0.5 * (1.0 + th) + 0.5 * x * (1.0 - th * th) * (c * (1.0 + 3.0 * 0.044715 * (x * x)))
    return g, dg


def _window_sum(ext, w, back):
    n = ext.shape[0]
    s, k = ext, 1
    while k < w:
        s = s + pltpu.roll(s, k if back else n - k, 0)
        k *= 2
    return s


def _rstd(x):
    return lax.rsqrt(jnp.mean(x * x, axis=-1, keepdims=True) + EPS)


def _rms_bwd(dy, xhat, rstd, gain):
    dxh = dy * gain
    dx = rstd * (dxh - xhat * jnp.mean(dxh * xhat, axis=-1, keepdims=True))
    return dx, jnp.sum(dy * xhat, axis=0, keepdims=True)


def _bd(xb, w_ref):
    return jnp.concatenate(
        [jnp.dot(xb[:, :256], w_ref[0], preferred_element_type=F32),
         jnp.dot(xb[:, 256:], w_ref[1], preferred_element_type=F32)], axis=1)


def _bd_t(xb, w_ref):
    return jnp.concatenate(
        [lax.dot_general(xb[:, :256], w_ref[0], NT, preferred_element_type=F32),
         lax.dot_general(xb[:, 256:], w_ref[1], NT, preferred_element_type=F32)], axis=1)


def _bd_grad(xb, db):
    return jnp.stack(
        [lax.dot_general(xb[:, :256], db[:, :256], TN, preferred_element_type=F32),
         lax.dot_general(xb[:, 256:], db[:, 256:], TN, preferred_element_type=F32)], axis=0)


def _fill_block_diag(dst, src_ref):
    n, k, _ = src_ref.shape
    dst[...] = jnp.zeros(dst.shape, BF16)
    for b in range(n):
        p, q = divmod(b, 256 // k)
        dst[p, q * k:(q + 1) * k, q * k:(q + 1) * k] = src_ref[b].astype(BF16)


def _diag_pack(w, k):
    lane = lax.broadcasted_iota(jnp.int32, (k, 256), 1)
    out = w[0:k]
    for q in range(1, 256 // k):
        out = jnp.where(lane >= q * k, w[q * k:(q + 1) * k], out)
    return out


def _y_pos(b):
    return 4 * (b % 2) + b // 2


def _mixer_pre(e_lru, e_pool, pv, wa_ref, wx_ref, wp_ref, tm, t0):
    taps = [e_lru[pl.ds(HALO - 3 + k, tm), :] for k in range(4)]
    xc = pv[ROW_CB:ROW_CB + 1, :]
    for k in range(4):
        xc = xc + taps[k] * pv[ROW_CW + k:ROW_CW + k + 1, :]
    xcb = xc.astype(BF16)
    r = _sigmoid(_bd(xcb, wa_ref) + pv[ROW_BA:ROW_BA + 1, :])
    ig = _sigmoid(_bd(xcb, wx_ref) + pv[ROW_BX:ROW_BX + 1, :])
    z = -pv[ROW_LAM:ROW_LAM + 1, :]
    sp = jnp.maximum(z, 0.0) + jnp.log(1.0 + jnp.exp(-jnp.abs(z)))
    la = (-LRU_C * r) * sp
    a = jnp.exp(la)
    om = -jnp.tanh(la) * (1.0 + a * a)
    omc = jnp.maximum(om, 1e-12)
    rmult = lax.rsqrt(omc)
    mult = omc * rmult
    t = t0 + lax.broadcasted_iota(jnp.int32, (tm, 1), 0)
    parts, inv_cnts = [], []
    for g, w in enumerate(POOL_WINDOWS):
        ext = e_pool[:, pl.ds(128 * g, 128)]
        s = _window_sum(ext, w, back=True)[HALO:, :]
        inv_cnt = 1.0 / jnp.minimum(t + 1, w).astype(F32)
        inv_cnts.append(inv_cnt)
        parts.append(s * inv_cnt - ext[HALO:, :])
    pooled = jnp.concatenate(parts, axis=1)
    pooled_b = pooled.astype(BF16)
    zp = _bd(pooled_b, wp_ref) + pv[ROW_PB:ROW_PB + 1, :]
    return dict(taps=taps, xc=xc, xcb=xcb, r=r, ig=ig, sp=sp, a=a, om=om, mult=mult, rmult=rmult,
                pooled_b=pooled_b, zp=zp, inv_cnts=inv_cnts)


def _scan_tile(a_ref, b_ref, out_ref, carry, tm, reverse):
    row = lax.broadcasted_iota(jnp.int32, (8, LRU_W), 0)
    nblk = tm // 8

    def local_scan(blk):
        r0 = pl.multiple_of(blk * 8, 8)
        av = a_ref[pl.ds(r0, 8), :]
        bv = b_ref[pl.ds(r0, 8), :]
        for d in (1, 2, 4):
            sh = (8 - d) if reverse else d
            a_s = pltpu.roll(av, sh, 0)
            b_s = pltpu.roll(bv, sh, 0)
            m = (row < 8 - d) if reverse else (row >= d)
            bv = jnp.where(m, av * b_s + bv, bv)
            av = jnp.where(m, av * a_s, av)
        return r0, av, bv

    def step(i, hin):
        local = [local_scan((nblk - 1 - (i * SCAN_UNROLL + j)) if reverse else (i * SCAN_UNROLL + j))
                 for j in range(SCAN_UNROLL)]
        for r0, av, bv in local:
            hv = av * hin + bv
            out_ref[pl.ds(r0, 8), :] = hv
            hin = jnp.broadcast_to(hv[0:1, :] if reverse else hv[7:8, :], (8, LRU_W))
        return hin

    return lax.fori_loop(0, nblk // SCAN_UNROLL, step, carry)


MESH = pl.DeviceIdType.MESH
ANY = pl.BlockSpec(memory_space=pl.ANY)


def _place():
    x, y, c = lax.axis_index("x"), lax.axis_index("y"), lax.axis_index("c")
    chips = [(1 - x, y), (x, 1 - y), (1 - x, 1 - y)]
    return x, y, c, chips


class _Gather:
    def __init__(self, ins, outs, send_sems, recv_sems, local_sems, core_major=False):
        self.ins, self.outs, self.n = ins, outs, len(ins)
        self.send_sems, self.recv_sems, self.local_sems = send_sems, recv_sems, local_sems
        self.core_major = core_major

    @staticmethod
    def scratch(n):
        return [pltpu.SemaphoreType.DMA((7, n)), pltpu.SemaphoreType.DMA((7, n)), pltpu.SemaphoreType.DMA((n,))]

    def _slot(self, a, px, py, pc):
        return self.outs[a].at[4 * pc + 2 * px + py if self.core_major else 4 * px + 2 * py + pc]

    def _copy(self, a, k, block, to, src=None):
        return pltpu.make_async_remote_copy(
            src_ref=self._slot(a, *block) if src is None else src, dst_ref=self._slot(a, *block),
            send_sem=self.send_sems.at[k, a], recv_sem=self.recv_sems.at[k, a], device_id=to, device_id_type=MESH)

    def _mine(self, a):
        x, y, c, _ = _place()
        return pltpu.make_async_copy(self.ins[a], self._slot(a, x, y, c), self.local_sems.at[a])

    def _first(self, a):
        x, y, c, chips = _place()
        me = (x, y, c)
        return ([self._copy(a, 0, me, (x, y, 1 - c), src=self.ins[a])]
                + [self._copy(a, 1 + j, me, (*chip, c), src=self.ins[a]) for j, chip in enumerate(chips)])

    def start(self):
        for a in range(self.n):
            self._mine(a).start()
        for a in range(self.n):
            for cp in self._first(a):
                cp.start()

    def finish(self):
        x, y, c, chips = _place()
        me, sibling = (x, y, c), (x, y, 1 - c)
        passed = []
        for j, chip in enumerate(chips):
            for a in range(self.n):
                self._copy(a, 1 + j, (*chip, c), me).wait_recv()
                fwd = self._copy(a, 4 + j, (*chip, c), sibling)
                fwd.start()
                passed.append(fwd)
        for a in range(self.n):
            self._copy(a, 0, (x, y, 1 - c), me).wait_recv()
            for j, chip in enumerate(chips):
                self._copy(a, 4 + j, (*chip, 1 - c), me).wait_recv()
        for a in range(self.n):
            for cp in self._first(a):
                cp.wait_send()
        for cp in passed:
            cp.wait_send()
        for a in range(self.n):
            self._mine(a).wait()


def _all_gather(arrs, name):
    n = len(arrs)

    def body(*refs):
        g = _Gather(refs[:n], refs[n:2 * n], *refs[2 * n:])
        g.start()
        g.finish()

    return pl.pallas_call(
        body, name=name,
        out_shape=[_sds((N_DEV,) + a.shape, a.dtype) for a in arrs],
        in_specs=[ANY] * n, out_specs=[ANY] * n, scratch_shapes=_Gather.scratch(n),
    )(*arrs)


def _pair_exchange(arrs, name):
    n = len(arrs)

    def body(*refs):
        ins, outs = refs[:n], refs[n:2 * n]
        send_sems, recv_sems = refs[2 * n:]
        x, y, c, _ = _place()
        sibling = (x, y, 1 - c)
        sends = []
        for a in range(n):
            for j in range(4):
                cp = pltpu.make_async_remote_copy(
                    src_ref=ins[a].at[2 * j + (1 - c)], dst_ref=outs[a].at[j],
                    send_sem=send_sems.at[j, a], recv_sem=recv_sems.at[j, a], device_id=sibling, device_id_type=MESH)
                cp.start()
                sends.append(cp)
        for cp in sends:
            cp.wait()

    return pl.pallas_call(
        body, name=name,
        out_shape=[_sds((4,) + a.shape[1:], a.dtype) for a in arrs],
        in_specs=[ANY] * n, out_specs=[ANY] * n,
        scratch_shapes=[pltpu.SemaphoreType.DMA((4, n)), pltpu.SemaphoreType.DMA((4, n))],
    )(*arrs)


class _ChipExchange:
    def __init__(self, ins, outs, send_sems, recv_sems, local_sems):
        self.ins, self.outs, self.n = ins, outs, len(ins)
        self.send_sems, self.recv_sems, self.local_sems = send_sems, recv_sems, local_sems

    @staticmethod
    def scratch(n):
        return [pltpu.SemaphoreType.DMA((3, n)), pltpu.SemaphoreType.DMA((3, n)), pltpu.SemaphoreType.DMA((n,))]

    def _local(self, a):
        x, y, _, _ = _place()
        me = 2 * x + y
        return pltpu.make_async_copy(self.ins[a].at[me], self.outs[a].at[me], self.local_sems.at[a])

    def _copies(self, a):
        x, y, c, chips = _place()
        me = 2 * x + y
        return [(pltpu.make_async_remote_copy(
                     src_ref=self.ins[a].at[2 * px + py], dst_ref=self.outs[a].at[me],
                     send_sem=self.send_sems.at[k, a], recv_sem=self.recv_sems.at[k, a],
                     device_id=(px, py, c), device_id_type=MESH),
                 pltpu.make_async_remote_copy(
                     src_ref=self.ins[a].at[me], dst_ref=self.outs[a].at[2 * px + py],
                     send_sem=self.send_sems.at[k, a], recv_sem=self.recv_sems.at[k, a],
                     device_id=(px, py, c), device_id_type=MESH))
                for k, (px, py) in enumerate(chips)]

    def start(self):
        for a in range(self.n):
            self._local(a).start()
        for a in range(self.n):
            for send, _ in self._copies(a):
                send.start()

    def finish(self):
        for a in range(self.n):
            for send, recv in self._copies(a):
                send.wait_send()
                recv.wait_recv()
        for a in range(self.n):
            self._local(a).wait()


def _chip_exchange(arrs, name):
    n = len(arrs)

    def body(*refs):
        e = _ChipExchange(refs[:n], refs[n:2 * n], *refs[2 * n:])
        e.start()
        e.finish()

    return pl.pallas_call(
        body, name=name, out_shape=[_sds(a.shape, a.dtype) for a in arrs],
        in_specs=[ANY] * n, out_specs=[ANY] * n, scratch_shapes=_ChipExchange.scratch(n),
    )(*arrs)


def _mix_in(x, g_mix, w_in_t, tm, shards=()):
    T = x.shape[0]
    n_t = T // tm
    n_s = len(shards)

    def body(x_ref, g_ref, w_ref, *rest):
        sh_in, rest = rest[:n_s], rest[n_s:]
        u_ref, h_ref = rest[:2]
        gather = _Gather(sh_in, rest[2:2 + n_s], *rest[2 + n_s:], core_major=True) if n_s else None
        i = pl.program_id(0)

        if gather:
            @pl.when(i == 0)
            def _():
                gather.start()

        xv = x_ref[...]
        h = (xv * _rstd(xv) * g_ref[...]).astype(BF16)
        h_ref[...] = h
        u_ref[...] = lax.dot_general(h, w_ref[...], NT, preferred_element_type=F32)

        if gather:
            @pl.when(i == n_t - 1)
            def _():
                gather.finish()

    outs = pl.pallas_call(
        body, name="mix_in", grid=(n_t,),
        in_specs=[pl.BlockSpec((tm, D_MODEL), lambda i: (i, 0)),
                  pl.BlockSpec((1, D_MODEL), lambda i: (0, 0)),
                  pl.BlockSpec((D_IN, D_MODEL), lambda i: (0, 0))] + [ANY] * n_s,
        out_specs=[pl.BlockSpec((tm, D_IN), lambda i: (i, 0)),
                   pl.BlockSpec((tm, D_MODEL), lambda i: (i, 0))] + [ANY] * n_s,
        out_shape=[_sds((T, D_IN), F32), _sds((T, D_MODEL), BF16)] + [_sds((N_DEV,) + a.shape, a.dtype) for a in shards],
        scratch_shapes=_Gather.scratch(n_s) if n_s else [],
        compiler_params=pltpu.CompilerParams(dimension_semantics=("arbitrary",)),
    )(x, g_mix, w_in_t, *shards)
    return outs[0], outs[1], list(outs[2:])


def _mixer_fwd(u, pv, wa, wx, wp, tm, shards=()):
    T = u.shape[0]
    n_s = len(shards)
    n_t = T // tm

    def body(u_ref, pv_ref, wa_in, wx_in, wp_in, *rest):
        sh_in, rest = rest[:n_s], rest[n_s:]
        y_ref, hs_ref = rest[:2]
        sh_out, rest = rest[2:2 + n_s], rest[2 + n_s:]
        e_lru, e_pool, a_s, b_s, hc, wa_ref, wx_ref, wp_ref = rest[:8]
        gather = _Gather(sh_in, sh_out, *rest[8:], core_major=True) if n_s else None
        i = pl.program_id(0)

        @pl.when(i == 0)
        def _():
            if gather:
                gather.start()
            e_lru[pl.ds(0, HALO), :] = jnp.zeros((HALO, LRU_W), F32)
            e_pool[pl.ds(0, HALO), :] = jnp.zeros((HALO, POOL_W), F32)
            hc[...] = jnp.zeros((8, LRU_W), F32)
            _fill_block_diag(wa_ref, wa_in)
            _fill_block_diag(wx_ref, wx_in)
            _fill_block_diag(wp_ref, wp_in)

        e_lru[pl.ds(HALO, tm), :] = u_ref[:, 0:LRU_W]
        e_pool[pl.ds(HALO, tm), :] = u_ref[:, 2 * LRU_W:D_IN]
        pv = pv_ref[...]
        p = _mixer_pre(e_lru, e_pool, pv, wa_ref, wx_ref, wp_ref, tm, i * tm)
        a_s[...] = p["a"]
        b_s[...] = p["mult"] * (p["ig"] * p["xc"])
        hc[...] = _scan_tile(a_s, b_s, hs_ref, hc[...], tm, reverse=False)
        gl, _ = _gelu_parts(u_ref[:, LRU_W:2 * LRU_W])
        y_lru = hs_ref[...] * gl
        y_pool = p["zp"] * pv[ROW_PS:ROW_PS + 1, :]
        yn = jnp.concatenate([y_lru * _rstd(y_lru) * pv[ROW_GL:ROW_GL + 1, :],
                              y_pool * _rstd(y_pool) * pv[ROW_GP:ROW_GP + 1, :]], axis=1).astype(BF16)
        for b in range(N_DEV):
            y_ref[:, 128 * _y_pos(b):128 * (_y_pos(b) + 1)] = yn[:, 128 * b:128 * (b + 1)]
        e_lru[pl.ds(0, HALO), :] = e_lru[pl.ds(tm, HALO), :]
        e_pool[pl.ds(0, HALO), :] = e_pool[pl.ds(tm, HALO), :]

        if gather:
            @pl.when(i == n_t - 1)
            def _():
                gather.finish()

    full = lambda shape: pl.BlockSpec(shape, lambda i: (0,) * len(shape))
    outs = pl.pallas_call(
        body, name="mixer_fwd", grid=(n_t,),
        in_specs=[pl.BlockSpec((tm, D_IN), lambda i: (i, 0)), full((16, LRU_W)),
                  full((8, 64, 64)), full((8, 64, 64)), full((4, 128, 128))] + [ANY] * n_s,
        out_specs=[pl.BlockSpec((tm, D_MODEL), lambda i: (i, 0)), pl.BlockSpec((tm, LRU_W), lambda i: (i, 0))] + [ANY] * n_s,
        out_shape=[_sds((T, D_MODEL), BF16), _sds((T, LRU_W), F32)] + [_sds((N_DEV,) + a.shape, a.dtype) for a in shards],
        scratch_shapes=[pltpu.VMEM((HALO + tm, LRU_W), F32), pltpu.VMEM((HALO + tm, POOL_W), F32),
                        pltpu.VMEM((tm, LRU_W), F32), pltpu.VMEM((tm, LRU_W), F32), pltpu.VMEM((8, LRU_W), F32)]
        + [pltpu.VMEM((2, 256, 256), BF16)] * 3 + (_Gather.scratch(n_s) if n_s else []),
        compiler_params=pltpu.CompilerParams(dimension_semantics=("arbitrary",)),
    )(u, pv, wa, wx, wp, *shards)
    return outs[0], outs[1], list(outs[2:])


def _ffn_fwd(x, y, w_out_b, g_ffn, w1_b, w3_b, w2_b, g_fin, tgt, tm, tn):
    T = x.shape[0]
    n_j = D_FF // tn

    def body(x_ref, y_ref, wo_ref, gf_ref, w1_ref, w3_ref, w2_ref, gfin_ref, tgt_ref,
             hres_ref, h2_ref, g_ref, v_ref, d3_ref, loss_ref, dgfin_ref, acc):
        i, j = pl.program_id(0), pl.program_id(1)

        @pl.when(j == 0)
        def _():
            hr = x_ref[...] + jnp.dot(y_ref[...], wo_ref[...], preferred_element_type=F32)
            hres_ref[...] = hr
            h2_ref[...] = (hr * _rstd(hr) * gf_ref[...]).astype(BF16)
            acc[...] = jnp.zeros((tm, D_MODEL), F32)

        @pl.when((j == 0) & (i == 0))
        def _():
            loss_ref[...] = jnp.zeros((8, 128), F32)
            dgfin_ref[...] = jnp.zeros((1, D_MODEL), F32)

        h2 = h2_ref[...]
        g = lax.dot_general(h2, w1_ref[...], NT, preferred_element_type=F32)
        v = lax.dot_general(h2, w3_ref[...], NT, preferred_element_type=F32)
        g_ref[...] = g.astype(BF16)
        v_ref[...] = v.astype(BF16)
        ff = ((g * _sigmoid(g)) * v).astype(BF16)
        acc[...] += jnp.dot(ff, w2_ref[...], preferred_element_type=F32)

        @pl.when(j == n_j - 1)
        def _():
            h3 = hres_ref[...] + acc[...]
            rstd = _rstd(h3)
            xh = h3 * rstd
            gfin = gfin_ref[...]
            err = xh * gfin - tgt_ref[...]
            loss_ref[...] += 0.5 * jnp.sum(jnp.mean(err * err, axis=-1, keepdims=True))
            dout = err * (1.0 / D_MODEL)
            dx, dgain = _rms_bwd(dout, xh, rstd, gfin)
            d3_ref[...] = dx
            dgfin_ref[...] += dgain

    row = lambda w: pl.BlockSpec((tm, w), lambda i, j: (i, 0))
    const = lambda shape: pl.BlockSpec(shape, lambda i, j: (0,) * len(shape))
    return pl.pallas_call(
        body, name="ffn_fwd", grid=(T // tm, n_j),
        in_specs=[row(D_MODEL), row(D_MODEL), const((D_MODEL, D_MODEL)), const((1, D_MODEL)),
                  pl.BlockSpec((tn, D_MODEL), lambda i, j: (j, 0)), pl.BlockSpec((tn, D_MODEL), lambda i, j: (j, 0)),
                  pl.BlockSpec((tn, D_MODEL), lambda i, j: (j, 0)), const((1, D_MODEL)), row(D_MODEL)],
        out_specs=[row(D_MODEL), row(D_MODEL),
                   pl.BlockSpec((tm, tn), lambda i, j: (i, j)), pl.BlockSpec((tm, tn), lambda i, j: (i, j)),
                   row(D_MODEL), const((8, 128)), const((1, D_MODEL))],
        out_shape=[_sds((T, D_MODEL), F32), _sds((T, D_MODEL), BF16), _sds((T, D_FF), BF16), _sds((T, D_FF), BF16),
                   _sds((T, D_MODEL), F32), _sds((8, 128), F32), _sds((1, D_MODEL), F32)],
        scratch_shapes=[pltpu.VMEM((tm, D_MODEL), F32)],
        compiler_params=pltpu.CompilerParams(dimension_semantics=("arbitrary", "arbitrary")),
    )(x, y, w_out_b, g_ffn, w1_b, w3_b, w2_b, g_fin, tgt)


def _ffn_bwd(d3, g, v, w1_b, w3_b, w2_b, hres, g_ffn, tm, tn):
    T = d3.shape[0]
    n_j = D_FF // tn

    def body(d3_ref, g_ref, v_ref, w1_ref, w3_ref, w2_ref, hres_ref, gf_ref,
             dg_ref, dv_ref, ff_ref, d2_ref, dgffn_ref, acc):
        i, j = pl.program_id(0), pl.program_id(1)

        @pl.when(j == 0)
        def _():
            acc[...] = jnp.zeros((tm, D_MODEL), F32)

        @pl.when((j == 0) & (i == 0))
        def _():
            dgffn_ref[...] = jnp.zeros((1, D_MODEL), F32)

        dff = lax.dot_general(d3_ref[...].astype(BF16), w2_ref[...], NT, preferred_element_type=F32)
        gv = g_ref[...].astype(F32)
        vv = v_ref[...].astype(F32)
        sg = _sigmoid(gv)
        sl = gv * sg
        dgb = (dff * vv * (sg * (1.0 + gv * (1.0 - sg)))).astype(BF16)
        dvb = (dff * sl).astype(BF16)
        dg_ref[...] = dgb
        dv_ref[...] = dvb
        ff_ref[...] = (sl * vv).astype(BF16)
        acc[...] += (jnp.dot(dgb, w1_ref[...], preferred_element_type=F32)
                     + jnp.dot(dvb, w3_ref[...], preferred_element_type=F32))

        @pl.when(j == n_j - 1)
        def _():
            hr = hres_ref[...]
            rstd = _rstd(hr)
            dx, dgain = _rms_bwd(acc[...], hr * rstd, rstd, gf_ref[...])
            d2_ref[...] = d3_ref[...] + dx
            dgffn_ref[...] += dgain

    row = lambda w: pl.BlockSpec((tm, w), lambda i, j: (i, 0))
    tile = pl.BlockSpec((tm, tn), lambda i, j: (i, j))
    const = lambda shape: pl.BlockSpec(shape, lambda i, j: (0,) * len(shape))
    return pl.pallas_call(
        body, name="ffn_bwd", grid=(T // tm, n_j),
        in_specs=[row(D_MODEL), tile, tile,
                  pl.BlockSpec((tn, D_MODEL), lambda i, j: (j, 0)), pl.BlockSpec((tn, D_MODEL), lambda i, j: (j, 0)),
                  pl.BlockSpec((tn, D_MODEL), lambda i, j: (j, 0)), row(D_MODEL), const((1, D_MODEL))],
        out_specs=[tile, tile, tile, row(D_MODEL), const((1, D_MODEL))],
        out_shape=[_sds((T, D_FF), BF16), _sds((T, D_FF), BF16), _sds((T, D_FF), BF16),
                   _sds((T, D_MODEL), F32), _sds((1, D_MODEL), F32)],
        scratch_shapes=[pltpu.VMEM((tm, D_MODEL), F32)],
        compiler_params=pltpu.CompilerParams(dimension_semantics=("arbitrary", "arbitrary")),
    )(d3, g, v, w1_b, w3_b, w2_b, hres, g_ffn)


def _at_b(a, b, name, tmm, tn, tk, gather=()):
    T, M = a.shape
    N = b.shape[1]
    n_m, n_n, n_k = M // tmm, N // tn, T // tk
    n_g = len(gather)

    def body(a_ref, b_ref, *rest):
        g_in, o_ref, rest = rest[:n_g], rest[n_g], rest[n_g + 1:]
        ag = _Gather(g_in, rest[:n_g], *rest[n_g:]) if n_g else None
        m, n, k = pl.program_id(0), pl.program_id(1), pl.program_id(2)

        if ag:
            @pl.when((m == 0) & (n == 0) & (k == 0))
            def _():
                ag.start()

        @pl.when(k == 0)
        def _():
            o_ref[...] = jnp.zeros((tmm, tn), F32)

        o_ref[...] += lax.dot_general(a_ref[...].astype(BF16), b_ref[...].astype(BF16), TN,
                                      preferred_element_type=F32)

        if ag:
            @pl.when((m == n_m - 1) & (n == n_n - 1) & (k == n_k - 1))
            def _():
                ag.finish()

    outs = pl.pallas_call(
        body, name=name, grid=(n_m, n_n, n_k),
        in_specs=[pl.BlockSpec((tk, tmm), lambda m, n, k: (k, m)), pl.BlockSpec((tk, tn), lambda m, n, k: (k, n))]
        + [ANY] * n_g,
        out_specs=[pl.BlockSpec((tmm, tn), lambda m, n, k: (m, n))] + [ANY] * n_g,
        out_shape=[_sds((M, N), F32)] + [_sds((N_DEV,) + g.shape, g.dtype) for g in gather],
        scratch_shapes=_Gather.scratch(n_g) if n_g else [],
        compiler_params=pltpu.CompilerParams(
            dimension_semantics=("arbitrary",) * 3 if n_g else ("parallel", "parallel", "arbitrary")),
    )(a, b, *gather)
    return (outs[0], list(outs[1:])) if n_g else outs[0]


def _at_b_pair(a, b, c_arr, name, tk):
    T, M = a.shape
    N = b.shape[1]
    hm, n_k = M // 2, T // tk

    def body(c_ref, a_ref, b_ref, o_ref, acc, landed, send_sem, recv_sem):
        ph, k = pl.program_id(0), pl.program_id(1)
        def hand_over():
            x, y, c, _ = _place()
            return pltpu.make_async_remote_copy(
                src_ref=acc.at[0], dst_ref=landed, send_sem=send_sem, recv_sem=recv_sem,
                device_id=(x, y, 1 - c), device_id_type=MESH)

        prod = lax.dot_general(a_ref[...].astype(BF16), b_ref[...].astype(BF16), TN, preferred_element_type=F32)
        for half in range(2):
            @pl.when((ph == half) & (k == 0))
            def _():
                acc[half] = prod

            @pl.when((ph == half) & (k > 0))
            def _():
                acc[half] += prod

        @pl.when((ph == 0) & (k == n_k - 1))
        def _():
            hand_over().start()

        @pl.when((ph == 1) & (k == n_k - 1))
        def _():
            copy = hand_over()
            copy.wait_recv()
            o_ref[...] = (acc[1] + landed[...]).astype(BF16)
            copy.wait_send()

    return pl.pallas_call(
        body, name=name,
        grid_spec=pltpu.PrefetchScalarGridSpec(
            num_scalar_prefetch=1, grid=(2, n_k),
            in_specs=[pl.BlockSpec((tk, hm), lambda ph, k, c_ref: (k, (ph + 1 - c_ref[0]) % 2)),
                      pl.BlockSpec((tk, N), lambda ph, k, c_ref: (k, 0))],
            out_specs=pl.BlockSpec((hm, N), lambda ph, k, c_ref: (0, 0)),
            scratch_shapes=[pltpu.VMEM((2, hm, N), F32), pltpu.VMEM((hm, N), F32),
                            pltpu.SemaphoreType.DMA, pltpu.SemaphoreType.DMA]),
        out_shape=_sds((hm, N), BF16),
        compiler_params=pltpu.CompilerParams(dimension_semantics=("arbitrary", "arbitrary")),
    )(c_arr, a, b)


def _mixer_bwd(d2, u, hs, pv, wa, wx, wp, w_out_b, tm, chip_sums=()):
    T = u.shape[0]
    n_t = T // tm
    n_x = len(chip_sums)

    def body(d2_ref, u_ref, uh_ref, hs_ref, hh_ref, pv_ref, wa_in, wx_in, wp_in, wo_ref, *rest):
        x_in, rest = rest[:n_x], rest[n_x:]
        du_ref, sg_ref = rest[:2]
        x_out, rest = rest[2:2 + n_x], rest[2 + n_x:]
        e_lru, e_pool, e_h, a_s, b_s, mu_s, f_x, f_p, mc, cx, cp = rest[:11]
        wa_ref, wx_ref, wp_ref, vacc_ref, dwa_ref, dwx_ref, dwp_ref = rest[11:18]
        exchange = _ChipExchange(x_in, x_out, *rest[18:]) if n_x else None
        s = pl.program_id(0)
        it = n_t - 1 - s

        @pl.when(s == 0)
        def _():
            if exchange:
                exchange.start()
            mc[...] = jnp.zeros((8, LRU_W), F32)
            cx[...] = jnp.zeros((8, LRU_W), F32)
            cp[...] = jnp.zeros((HALO, POOL_W), F32)
            vacc_ref[...] = jnp.zeros((16, LRU_W), F32)
            dwa_ref[...] = jnp.zeros((2, 256, 256), F32)
            dwx_ref[...] = jnp.zeros((2, 256, 256), F32)
            dwp_ref[...] = jnp.zeros((2, 256, 256), F32)
            _fill_block_diag(wa_ref, wa_in)
            _fill_block_diag(wx_ref, wx_in)
            _fill_block_diag(wp_ref, wp_in)

        first = it == 0
        e_lru[pl.ds(0, HALO), :] = jnp.where(first, 0.0, uh_ref[:, 0:LRU_W])
        e_pool[pl.ds(0, HALO), :] = jnp.where(first, 0.0, uh_ref[:, 2 * LRU_W:D_IN])
        e_lru[pl.ds(HALO, tm), :] = u_ref[:, 0:LRU_W]
        e_pool[pl.ds(HALO, tm), :] = u_ref[:, 2 * LRU_W:D_IN]
        e_h[pl.ds(0, 8), :] = jnp.where(first, 0.0, hh_ref[...])
        e_h[pl.ds(8, tm), :] = hs_ref[...]
        pv = pv_ref[...]
        p = _mixer_pre(e_lru, e_pool, pv, wa_ref, wx_ref, wp_ref, tm, it * tm)
        a, xc, ig, r, mult = p["a"], p["xc"], p["ig"], p["r"], p["mult"]

        dyn = lax.dot_general(d2_ref[...].astype(BF16), wo_ref[...], NT, preferred_element_type=F32)
        dyn = jnp.concatenate([dyn[:, 128 * _y_pos(b):128 * (_y_pos(b) + 1)] for b in range(N_DEV)], axis=1)

        h = hs_ref[...]
        ug = u_ref[:, LRU_W:2 * LRU_W]
        gl, dgl = _gelu_parts(ug)
        y_lru = h * gl
        rstd_l = _rstd(y_lru)
        dy_lru, d_gain_l = _rms_bwd(dyn[:, 0:LRU_W], y_lru * rstd_l, rstd_l, pv[ROW_GL:ROW_GL + 1, :])
        dh = dy_lru * gl
        du_ref[:, LRU_W:2 * LRU_W] = (dy_lru * h * dgl).astype(BF16)
        a_s[...] = a
        b_s[...] = a * dh
        mu_s[pl.ds(tm, 8), :] = mc[...]
        mc[...] = _scan_tile(a_s, b_s, mu_s, mc[...], tm, reverse=True)
        lam_t = dh + mu_s[pl.ds(1, tm), :]
        da = lam_t * e_h[pl.ds(7, tm), :]
        dmult = lam_t * (ig * xc)
        di = lam_t * (mult * xc)
        dxc = lam_t * (mult * ig)
        dla = da * a - jnp.where(p["om"] > 1e-12, dmult * ((a * a) * p["rmult"]), 0.0)
        dra = (dla * (-LRU_C * p["sp"])) * (r * (1.0 - r))
        dia = di * (ig * (1.0 - ig))
        drab = dra.astype(BF16)
        diab = dia.astype(BF16)
        dxc = dxc + _bd_t(drab, wa_ref) + _bd_t(diab, wx_ref)
        dwa_ref[...] += _bd_grad(p["xcb"], drab)
        dwx_ref[...] += _bd_grad(p["xcb"], diab)
        sig_neg_lam = _sigmoid(-pv[ROW_LAM:ROW_LAM + 1, :])
        d_lam = jnp.sum(dla * r, axis=0, keepdims=True) * (LRU_C * sig_neg_lam)

        f_x[pl.ds(0, tm), :] = dxc
        f_x[pl.ds(tm, 8), :] = cx[...]
        du_lru = jnp.zeros((tm, LRU_W), F32)
        d_cw = []
        for k in range(4):
            du_lru = du_lru + f_x[pl.ds(3 - k, tm), :] * pv[ROW_CW + k:ROW_CW + k + 1, :]
            d_cw.append(jnp.sum(dxc * p["taps"][k], axis=0, keepdims=True))
        du_ref[:, 0:LRU_W] = du_lru.astype(BF16)
        cx[...] = f_x[pl.ds(0, 8), :]

        zp = p["zp"]
        ps = pv[ROW_PS:ROW_PS + 1, :]
        y_pool = zp * ps
        rstd_p = _rstd(y_pool)
        dy_pool, d_gain_p = _rms_bwd(dyn[:, LRU_W:D_MODEL], y_pool * rstd_p, rstd_p, pv[ROW_GP:ROW_GP + 1, :])
        dz = dy_pool * ps
        dzb = dz.astype(BF16)
        dwp_ref[...] += _bd_grad(p["pooled_b"], dzb)
        dpooled = _bd_t(dzb, wp_ref)
        for g, w in enumerate(POOL_WINDOWS):
            f_p[pl.ds(0, tm), pl.ds(128 * g, 128)] = dpooled[:, 128 * g:128 * (g + 1)] * p["inv_cnts"][g]
        f_p[pl.ds(tm, HALO), :] = cp[...]
        for g, w in enumerate(POOL_WINDOWS):
            acc = _window_sum(f_p[:, pl.ds(128 * g, 128)], w, back=False)[0:tm, :]
            du_ref[:, 2 * LRU_W + 128 * g:2 * LRU_W + 128 * (g + 1)] = (
                acc - dpooled[:, 128 * g:128 * (g + 1)]).astype(BF16)
        cp[...] = f_p[pl.ds(0, HALO), :]

        rows = d_cw + [
            jnp.sum(dxc, axis=0, keepdims=True),
            jnp.sum(dra, axis=0, keepdims=True),
            jnp.sum(dia, axis=0, keepdims=True),
            d_lam,
            jnp.sum(dz, axis=0, keepdims=True),
            jnp.sum(dy_pool * zp, axis=0, keepdims=True),
            d_gain_l, d_gain_p,
            jnp.zeros((4, LRU_W), F32),
        ]
        vacc_ref[...] += jnp.concatenate(rows, axis=0)

        @pl.when(s == n_t - 1)
        def _():
            sg_ref[SG_VEC:SG_VEC + 16, :] = vacc_ref[:, 0:256]
            sg_ref[SG_VEC + 16:SG_VEC + 32, :] = vacc_ref[:, 256:512]
            for half in range(2):
                sg_ref[SG_WA + 64 * half:SG_WA + 64 * (half + 1), :] = _diag_pack(dwa_ref[half], 64)
                sg_ref[SG_WX + 64 * half:SG_WX + 64 * (half + 1), :] = _diag_pack(dwx_ref[half], 64)
                sg_ref[SG_WP + 128 * half:SG_WP + 128 * (half + 1), :] = _diag_pack(dwp_ref[half], 128)
            if exchange:
                exchange.finish()

    rev = lambda w: pl.BlockSpec((tm, w), lambda s: (n_t - 1 - s, 0))
    full = lambda shape: pl.BlockSpec(shape, lambda s: (0,) * len(shape))
    outs = pl.pallas_call(
        body, name="mixer_bwd", grid=(n_t,),
        in_specs=[rev(D_MODEL), rev(D_IN),
                  pl.BlockSpec((HALO, D_IN), lambda s: (jnp.maximum((n_t - 1 - s) * (tm // HALO) - 1, 0), 0)),
                  rev(LRU_W),
                  pl.BlockSpec((8, LRU_W), lambda s: (jnp.maximum((n_t - 1 - s) * (tm // 8) - 1, 0), 0)),
                  full((16, LRU_W)), full((8, 64, 64)), full((8, 64, 64)), full((4, 128, 128)),
                  full((D_MODEL, D_MODEL))] + [ANY] * n_x,
        out_specs=[rev(D_IN), full((SG_ROWS, 256))] + [ANY] * n_x,
        out_shape=[_sds((T, D_IN), BF16), _sds((SG_ROWS, 256), F32)] + [_sds(a.shape, a.dtype) for a in chip_sums],
        scratch_shapes=[pltpu.VMEM((HALO + tm, LRU_W), F32), pltpu.VMEM((HALO + tm, POOL_W), F32),
                        pltpu.VMEM((8 + tm, LRU_W), F32), pltpu.VMEM((tm, LRU_W), F32), pltpu.VMEM((tm, LRU_W), F32),
                        pltpu.VMEM((tm + 8, LRU_W), F32), pltpu.VMEM((tm + 8, LRU_W), F32),
                        pltpu.VMEM((tm + HALO, POOL_W), F32), pltpu.VMEM((8, LRU_W), F32),
                        pltpu.VMEM((8, LRU_W), F32), pltpu.VMEM((HALO, POOL_W), F32)]
        + [pltpu.VMEM((2, 256, 256), BF16)] * 3 + [pltpu.VMEM((16, LRU_W), F32)] + [pltpu.VMEM((2, 256, 256), F32)] * 3
        + (_ChipExchange.scratch(n_x) if n_x else []),
        compiler_params=pltpu.CompilerParams(dimension_semantics=("arbitrary",)),
    )(d2, u, u, hs, hs, pv, wa, wx, wp, w_out_b, *chip_sums)
    return outs[0], outs[1], list(outs[2:])


def _mix_in_bwd(du, x, d2, w_in_t, g_mix, tm, chip_sums=()):
    T = x.shape[0]
    n_t = T // tm
    n_x = len(chip_sums)

    def body(du_ref, x_ref, d2_ref, w_ref, g_ref, *rest):
        x_in, rest = rest[:n_x], rest[n_x:]
        dx_ref, dg_ref = rest[:2]
        exchange = _ChipExchange(x_in, rest[2:2 + n_x], *rest[2 + n_x:]) if n_x else None
        i = pl.program_id(0)

        @pl.when(i == 0)
        def _():
            dg_ref[...] = jnp.zeros((1, D_MODEL), F32)
            if exchange:
                exchange.start()

        dh = jnp.dot(du_ref[...], w_ref[...], preferred_element_type=F32)
        xv = x_ref[...]
        rstd = _rstd(xv)
        dx, dgain = _rms_bwd(dh, xv * rstd, rstd, g_ref[...])
        dx_ref[...] = d2_ref[...] + dx
        dg_ref[...] += dgain

        if exchange:
            @pl.when(i == n_t - 1)
            def _():
                exchange.finish()

    row = lambda w: pl.BlockSpec((tm, w), lambda i: (i, 0))
    const = lambda shape: pl.BlockSpec(shape, lambda i: (0,) * len(shape))
    outs = pl.pallas_call(
        body, name="mix_in_bwd", grid=(n_t,),
        in_specs=[row(D_IN), row(D_MODEL), row(D_MODEL), const((D_IN, D_MODEL)), const((1, D_MODEL))] + [ANY] * n_x,
        out_specs=[row(D_MODEL), const((1, D_MODEL))] + [ANY] * n_x,
        out_shape=[_sds((T, D_MODEL), F32), _sds((1, D_MODEL), F32)] + [_sds(a.shape, a.dtype) for a in chip_sums],
        scratch_shapes=_ChipExchange.scratch(n_x) if n_x else [],
        compiler_params=pltpu.CompilerParams(dimension_semantics=("arbitrary",)),
    )(du, x, d2, w_in_t, g_mix, *chip_sums)
    return outs[0], outs[1], list(outs[2:])


def _pair_sum(g, r1, c_arr, name):
    _, _, R, C = g.shape
    tr = R if R <= 512 else 256

    def body(c_ref, g_ref, r_ref, o_ref):
        o_ref[...] = (g_ref[...] + r_ref[...]).astype(BF16)

    return pl.pallas_call(
        body, name=name,
        grid_spec=pltpu.PrefetchScalarGridSpec(
            num_scalar_prefetch=1, grid=(4, R // tr),
            in_specs=[pl.BlockSpec((None, None, tr, C), lambda j, i, c_ref: (j, c_ref[0], i, 0)),
                      pl.BlockSpec((None, tr, C), lambda j, i, c_ref: (j, i, 0))],
            out_specs=pl.BlockSpec((None, tr, C), lambda j, i, c_ref: (j, i, 0))),
        out_shape=_sds((4, R, C), BF16),
    )(c_arr, g, r1)


def _adamw(w, g, m, v):
    m = ADAM_B1 * m + (1.0 - ADAM_B1) * g
    v = ADAM_B2 * v + (1.0 - ADAM_B2) * (g * g)
    m_hat = m / (1.0 - ADAM_B1 ** ADAM_STEP)
    v_hat = v / (1.0 - ADAM_B2 ** ADAM_STEP)
    delta = -ADAM_LR * (m_hat / (jnp.sqrt(v_hat) + ADAM_EPS) + ADAM_WD * w)
    return delta, m, v


def _adam_shard(w, m, v, parts, name):
    R, C = w.shape
    tr = R if R <= 512 else 256

    def body(w_ref, m_ref, v_ref, p_ref, g_ref, d_ref, nm_ref, nv_ref):
        g = p_ref[0].astype(F32)
        for j in range(1, 4):
            g = g + p_ref[j].astype(F32)
        delta, nm, nv = _adamw(w_ref[...], g, m_ref[...], v_ref[...])
        g_ref[...] = g
        d_ref[...] = delta
        nm_ref[...] = nm
        nv_ref[...] = nv

    blk = pl.BlockSpec((tr, C), lambda i: (i, 0))
    return pl.pallas_call(
        body, name=name, grid=(R // tr,),
        in_specs=[blk, blk, blk, pl.BlockSpec((4, tr, C), lambda i: (0, i, 0))],
        out_specs=[blk] * 4, out_shape=[_sds((R, C), F32)] * 4,
        compiler_params=pltpu.CompilerParams(dimension_semantics=("parallel",)),
    )(w, m, v, parts)


SMALL_PARAMS = [("norm_mix_g", (1, D_MODEL)), ("conv_w", (1, 4, 64)), ("conv_b", (1, LRU_W)),
                ("gate_a_w", (1, 8, 64, 64)), ("gate_a_b", (1, LRU_W)), ("gate_x_w", (1, 8, 64, 64)),
                ("gate_x_b", (1, LRU_W)), ("lru_lambda", (1, LRU_W)), ("pool_w", (1, 4, 128, 128)),
                ("pool_b", (1, POOL_W)), ("pool_scale", (1, POOL_W)), ("norm_lru_g", (1, LRU_W)),
                ("norm_pool_g", (1, POOL_W)), ("norm_ffn_g", (1, D_MODEL)), ("final_norm_g", (1, D_MODEL))]
VEC_ROW = dict(conv_b=ROW_CB, gate_a_b=ROW_BA, gate_x_b=ROW_BX, lru_lambda=ROW_LAM, pool_b=ROW_PB, pool_scale=ROW_PS,
               norm_lru_g=ROW_GL, norm_pool_g=ROW_GP)
WHOLE = (Ellipsis,)


def _unpack_mixer_grads(sg, dev):
    vec = jnp.concatenate([sg[SG_VEC:SG_VEC + 16], sg[SG_VEC + 16:SG_VEC + 32]], axis=1)
    out = {nm: [(WHOLE, vec[r:r + 1])] for nm, r in VEC_ROW.items()}
    own = jnp.zeros((4, 64), F32)
    for d in range(N_DEV):
        own = jnp.where(dev == d, vec[ROW_CW:ROW_CW + 4, 64 * d:64 * (d + 1)], own)
    out["conv_w"] = [((0,), own)]
    for nm, row0 in (("gate_a_w", SG_WA), ("gate_x_w", SG_WX)):
        out[nm] = [((0, b), sg[row0 + 64 * (b // 4):row0 + 64 * (b // 4 + 1), 64 * (b % 4):64 * (b % 4 + 1)])
                   for b in range(8)]
    out["pool_w"] = [((0, b), sg[SG_WP + 128 * (b // 2):SG_WP + 128 * (b // 2 + 1), 128 * (b % 2):128 * (b % 2 + 1)])
                     for b in range(4)]
    return out


def _adam_small(parts, w, m, v):
    names = [nm for nm, _ in SMALL_PARAMS]
    n = len(names)

    def body(sg_ref, gm_ref, gf_ref, gn_ref, ls_ref, *rest):
        w_refs, m_refs, v_refs, outs = rest[:n], rest[n:2 * n], rest[2 * n:3 * n], rest[3 * n:]
        dev = 4 * lax.axis_index("x") + 2 * lax.axis_index("y") + lax.axis_index("c")

        def total(ref):
            acc = ref[0]
            for d in range(1, N_DEV):
                acc = acc + ref[d]
            return acc

        pieces = _unpack_mixer_grads(total(sg_ref), dev)
        pieces["norm_mix_g"] = [(WHOLE, total(gm_ref))]
        pieces["norm_ffn_g"] = [(WHOLE, total(gf_ref))]
        pieces["final_norm_g"] = [(WHOLE, total(gn_ref))]
        for i, nm in enumerate(names):
            for idx, g in pieces[nm]:
                delta, new_m, new_v = _adamw(w_refs[i][idx], g, m_refs[i][idx], v_refs[i][idx])
                for kind, val in enumerate((g, delta, new_m, new_v)):
                    outs[4 * i + kind][idx] = val
        outs[4 * n][...] = total(ls_ref)

    shapes = [_sds(shape, F32) for _, shape in SMALL_PARAMS for _ in range(4)] + [_sds((8, 128), F32)]
    res = pl.pallas_call(body, name="adam_small", out_shape=shapes)(
        *parts, *[w[nm] for nm in names], *[m[nm] for nm in names], *[v[nm] for nm in names])
    return {nm: tuple(res[4 * i:4 * i + 4]) for i, nm in enumerate(names)}, res[4 * n][0, 0]


def _vec_rows(conv_w_full, conv_b, ba, bx, lam, pb, ps, gl, gp):
    return jnp.concatenate([conv_w_full, conv_b, ba, bx, lam, pb, ps, gl, gp, jnp.zeros((4, LRU_W), F32)], axis=0)


WEIGHT_ORDER = ['norm_mix_g', 'w_in', 'conv_w', 'conv_b', 'gate_a_w', 'gate_a_b', 'gate_x_w', 'gate_x_b', 'lru_lambda',
                'pool_w', 'pool_b', 'pool_scale', 'norm_lru_g', 'norm_pool_g', 'w_out', 'norm_ffn_g', 'ffn_w1', 'ffn_w3',
                'ffn_w2', 'final_norm_g']


def kernel(x, norm_mix_g, w_in, conv_w, conv_b, gate_a_w, gate_a_b, gate_x_w, gate_x_b, lru_lambda, pool_w, pool_b, pool_scale, norm_lru_g, norm_pool_g, w_out, norm_ffn_g, ffn_w1, ffn_w3, ffn_w2, final_norm_g, loss_target, m_norm_mix_g, m_w_in, m_conv_w, m_conv_b, m_gate_a_w, m_gate_a_b, m_gate_x_w, m_gate_x_b, m_lru_lambda, m_pool_w, m_pool_b, m_pool_scale, m_norm_lru_g, m_norm_pool_g, m_w_out, m_norm_ffn_g, m_ffn_w1, m_ffn_w3, m_ffn_w2, m_final_norm_g, v_norm_mix_g, v_w_in, v_conv_w, v_conv_b, v_gate_a_w, v_gate_a_b, v_gate_x_w, v_gate_x_b, v_lru_lambda, v_pool_w, v_pool_b, v_pool_scale, v_norm_lru_g, v_norm_pool_g, v_w_out, v_norm_ffn_g, v_ffn_w1, v_ffn_w3, v_ffn_w2, v_final_norm_g):
    ac = lax.axis_index("c")
    tm, tmx, tn, tk = 512, 512, 1408, 1024
    xs, tgt = x[0], loss_target[0]
    g_fin = final_norm_g.reshape(1, D_MODEL)
    c_arr = jnp.reshape(ac, (1,)).astype(jnp.int32)

    def pair_sums(blocks, names):
        from_sibling = _pair_exchange(blocks, "grads_to_sibling_" + names[0])
        return [_pair_sum(g.reshape((4, 2) + g.shape[1:]), r, c_arr, "pair_sum_" + nm)
                for g, r, nm in zip(blocks, from_sibling, names)]

    tr = lambda w: jnp.swapaxes(w[0], 0, 1)
    own = lambda w: w[0]
    bf = lambda a: a.astype(BF16)

    g_in, g_conv = _all_gather([bf(tr(w_in)), conv_w[0]], "gather_w_in")
    w_in_t = g_in.reshape(D_IN, D_MODEL)
    conv_w_full = g_conv.transpose(1, 0, 2).reshape(4, LRU_W)
    pv = _vec_rows(conv_w_full, conv_b, gate_a_b, gate_x_b, lru_lambda, pool_b, pool_scale, norm_lru_g, norm_pool_g)
    wa, wx, wp = gate_a_w[0], gate_x_w[0], pool_w[0]

    u, h1, (g_out, g_w1) = _mix_in(xs, norm_mix_g, w_in_t, tm, shards=[bf(own(w_out)), bf(tr(ffn_w1))])
    y, hs, (g_w3, g_w2) = _mixer_fwd(u, pv, wa, wx, wp, tmx, shards=[bf(tr(ffn_w3)), bf(own(ffn_w2))])
    w_out_b = g_out.reshape(D_MODEL, D_MODEL)
    w1_t, w3_t, w2_b = g_w1.reshape(D_FF, D_MODEL), g_w3.reshape(D_FF, D_MODEL), g_w2.reshape(D_FF, D_MODEL)
    hres, h2, g, v, d3, loss_acc, d_gfin = _ffn_fwd(xs, y, w_out_b, norm_ffn_g, w1_t, w3_t, w2_b, g_fin, tgt, tm, tn)

    dg, dv, ff, d2, d_gffn = _ffn_bwd(d3, g, v, w1_t, w3_t, w2_b, hres, norm_ffn_g, tm, tn)
    blocks = lambda a: a.reshape(N_DEV, a.shape[0] // N_DEV, a.shape[1])
    chips = lambda a: a.reshape(4, a.shape[0] // 4, a.shape[1])
    early_sums = [chips(_at_b_pair(y, d2, c_arr, "grad_w_out", tk)), chips(_at_b_pair(dg, h2, c_arr, "grad_w1", tk)),
                  chips(_at_b_pair(dv, h2, c_arr, "grad_w3", tk)), chips(_at_b_pair(ff, d3, c_arr, "grad_w2", tk))]
    du, d_mixer, early_parts = _mixer_bwd(d2, u, hs, pv, wa, wx, wp, w_out_b, tmx, chip_sums=early_sums)
    grad_x, d_gmix, _ = _mix_in_bwd(du, xs, d2, w_in_t, norm_mix_g, tm)
    d_win, small_parts = _at_b(du, h1, "grad_w_in", D_IN, D_MODEL, tk,
                               gather=[d_mixer, d_gmix, d_gffn, d_gfin, loss_acc])
    win_parts = _chip_exchange(pair_sums([blocks(d_win)], ["w_in"]), "grads_to_chips_w_in")
    parts = list(win_parts) + list(early_parts)

    res = {}
    shard_w = dict(w_in=(w_in, m_w_in, v_w_in, tr), w_out=(w_out, m_w_out, v_w_out, own),
                   ffn_w1=(ffn_w1, m_ffn_w1, v_ffn_w1, tr), ffn_w3=(ffn_w3, m_ffn_w3, v_ffn_w3, tr),
                   ffn_w2=(ffn_w2, m_ffn_w2, v_ffn_w2, own))
    for (nm, (w, m, v, view)), p in zip(shard_w.items(), parts):
        outs = _adam_shard(view(w), view(m), view(v), p, "adam_" + nm)
        res[nm] = [(jnp.swapaxes(o, 0, 1) if view is tr else o)[None] for o in outs]

    row = lambda a: a.reshape(1, D_MODEL)
    small = lambda gm, cw, cb, wa_, ba, wx_, bx, lam, pw, pb, ps, gl, gp, gf, gn: dict(
        norm_mix_g=gm, conv_w=cw, conv_b=cb, gate_a_w=wa_, gate_a_b=ba, gate_x_w=wx_, gate_x_b=bx, lru_lambda=lam,
        pool_w=pw, pool_b=pb, pool_scale=ps, norm_lru_g=gl, norm_pool_g=gp, norm_ffn_g=gf, final_norm_g=row(gn))
    small_res, loss = _adam_small(
        small_parts,
        small(norm_mix_g, conv_w, conv_b, gate_a_w, gate_a_b, gate_x_w, gate_x_b, lru_lambda, pool_w, pool_b,
              pool_scale, norm_lru_g, norm_pool_g, norm_ffn_g, final_norm_g),
        small(m_norm_mix_g, m_conv_w, m_conv_b, m_gate_a_w, m_gate_a_b, m_gate_x_w, m_gate_x_b, m_lru_lambda, m_pool_w,
              m_pool_b, m_pool_scale, m_norm_lru_g, m_norm_pool_g, m_norm_ffn_g, m_final_norm_g),
        small(v_norm_mix_g, v_conv_w, v_conv_b, v_gate_a_w, v_gate_a_b, v_gate_x_w, v_gate_x_b, v_lru_lambda, v_pool_w,
              v_pool_b, v_pool_scale, v_norm_lru_g, v_norm_pool_g, v_norm_ffn_g, v_final_norm_g))
    for nm, outs in small_res.items():
        res[nm] = [o.reshape(D_MODEL) for o in outs] if nm == "final_norm_g" else list(outs)

    out = [loss, grad_x[None]]
    for kind in range(4):
        out += [res[nm][kind] for nm in WEIGHT_ORDER]
    return tuple(out)
```

```python
import jax
import jax.numpy as jnp
from jax import lax
from jax.experimental import pallas as pl
from jax.experimental.pallas import tpu as pltpu

F32 = jnp.float32
BF16 = jnp.bfloat16

D_MODEL = 1024
LRU_W = 512
POOL_W = 512
D_IN = 1536
D_FF = 2816
POOL_WINDOWS = (2, 4, 8, 16)
EPS = 1e-6
LRU_C = 8.0
N_DEV = 8
HALO = 16
SCAN_UNROLL = 4

ADAM_LR = 0.001
ADAM_B1 = 0.9
ADAM_B2 = 0.999
ADAM_EPS = 1e-08
ADAM_WD = 0.01
ADAM_STEP = 10

ROW_CW, ROW_CB, ROW_BA, ROW_BX, ROW_LAM, ROW_PB, ROW_PS, ROW_GL, ROW_GP = 0, 4, 5, 6, 7, 8, 9, 10, 11
SG_VEC, SG_WA, SG_WX, SG_WP, SG_ROWS = 0, 32, 160, 288, 544

NT = (((1,), (1,)), ((), ()))
TN = (((0,), (0,)), ((), ()))


def _sds(shape, dtype):
    return jax.ShapeDtypeStruct(shape, dtype)


def _sigmoid(x):
    return 0.5 * jnp.tanh(0.5 * x) + 0.5


def _gelu_parts(x):
    c = 0.7978845608028654
    inner = c * (x + 0.044715 * (x * x * x))
    th = jnp.tanh(inner)
    g = 0.5 * x * (1.0 + th)
    dg = 0.5 * (1.0 + th) + 0.5 * x * (1.0 - th * th) * (c * (1.0 + 3.0 * 0.044715 * (x * x)))
    return g, dg


def _window_sum(ext, w, back):
    n = ext.shape[0]
    s, k = ext, 1
    while k < w:
        s = s + pltpu.roll(s, k if back else n - k, 0)
        k *= 2
    return s


def _rstd(x):
    return lax.rsqrt(jnp.mean(x * x, axis=-1, keepdims=True) + EPS)


def _rms_bwd(dy, xhat, rstd, gain):
    dxh = dy * gain
    dx = rstd * (dxh - xhat * jnp.mean(dxh * xhat, axis=-1, keepdims=True))
    return dx, jnp.sum(dy * xhat, axis=0, keepdims=True)


def _bd(xb, w_ref):
    return jnp.concatenate(
        [jnp.dot(xb[:, :256], w_ref[0], preferred_element_type=F32),
         jnp.dot(xb[:, 256:], w_ref[1], preferred_element_type=F32)], axis=1)


def _bd_t(xb, w_ref):
    return jnp.concatenate(
        [lax.dot_general(xb[:, :256], w_ref[0], NT, preferred_element_type=F32),
         lax.dot_general(xb[:, 256:], w_ref[1], NT, preferred_element_type=F32)], axis=1)


def _bd_grad(xb, db):
    return jnp.stack(
        [lax.dot_general(xb[:, :256], db[:, :256], TN, preferred_element_type=F32),
         lax.dot_general(xb[:, 256:], db[:, 256:], TN, preferred_element_type=F32)], axis=0)


def _fill_block_diag(dst, src_ref):
    n, k, _ = src_ref.shape
    dst[...] = jnp.zeros(dst.shape, BF16)
    for b in range(n):
        p, q = divmod(b, 256 // k)
        dst[p, q * k:(q + 1) * k, q * k:(q + 1) * k] = src_ref[b].astype(BF16)


def _diag_pack(w, k):
    lane = lax.broadcasted_iota(jnp.int32, (k, 256), 1)
    out = w[0:k]
    for q in range(1, 256 // k):
        out = jnp.where(lane >= q * k, w[q * k:(q + 1) * k], out)
    return out


def _y_pos(b):
    return 4 * (b % 2) + b // 2


def _mixer_pre(e_lru, e_pool, pv, wa_ref, wx_ref, wp_ref, tm, t0):
    taps = [e_lru[pl.ds(HALO - 3 + k, tm), :] for k in range(4)]
    xc = pv[ROW_CB:ROW_CB + 1, :]
    for k in range(4):
        xc = xc + taps[k] * pv[ROW_CW + k:ROW_CW + k + 1, :]
    xcb = xc.astype(BF16)
    r = _sigmoid(_bd(xcb, wa_ref) + pv[ROW_BA:ROW_BA + 1, :])
    ig = _sigmoid(_bd(xcb, wx_ref) + pv[ROW_BX:ROW_BX + 1, :])
    z = -pv[ROW_LAM:ROW_LAM + 1, :]
    sp = jnp.maximum(z, 0.0) + jnp.log(1.0 + jnp.exp(-jnp.abs(z)))
    la = (-LRU_C * r) * sp
    a = jnp.exp(la)
    om = -jnp.tanh(la) * (1.0 + a * a)
    omc = jnp.maximum(om, 1e-12)
    rmult = lax.rsqrt(omc)
    mult = omc * rmult
    t = t0 + lax.broadcasted_iota(jnp.int32, (tm, 1), 0)
    parts, inv_cnts = [], []
    for g, w in enumerate(POOL_WINDOWS):
        ext = e_pool[:, pl.ds(128 * g, 128)]
        s = _window_sum(ext, w, back=True)[HALO:, :]
        inv_cnt = 1.0 / jnp.minimum(t + 1, w).astype(F32)
        inv_cnts.append(inv_cnt)
        parts.append(s * inv_cnt - ext[HALO:, :])
    pooled = jnp.concatenate(parts, axis=1)
    pooled_b = pooled.astype(BF16)
    zp = _bd(pooled_b, wp_ref) + pv[ROW_PB:ROW_PB + 1, :]
    return dict(taps=taps, xc=xc, xcb=xcb, r=r, ig=ig, sp=sp, a=a, om=om, mult=mult, rmult=rmult,
                pooled_b=pooled_b, zp=zp, inv_cnts=inv_cnts)


def _scan_tile(a_ref, b_ref, out_ref, carry, tm, reverse):
    row = lax.broadcasted_iota(jnp.int32, (8, LRU_W), 0)
    nblk = tm // 8

    def local_scan(blk):
        r0 = pl.multiple_of(blk * 8, 8)
        av = a_ref[pl.ds(r0, 8), :]
        bv = b_ref[pl.ds(r0, 8), :]
        for d in (1, 2, 4):
            sh = (8 - d) if reverse else d
            a_s = pltpu.roll(av, sh, 0)
            b_s = pltpu.roll(bv, sh, 0)
            m = (row < 8 - d) if reverse else (row >= d)
            bv = jnp.where(m, av * b_s + bv, bv)
            av = jnp.where(m, av * a_s, av)
        return r0, av, bv

    def step(i, hin):
        local = [local_scan((nblk - 1 - (i * SCAN_UNROLL + j)) if reverse else (i * SCAN_UNROLL + j))
                 for j in range(SCAN_UNROLL)]
        for r0, av, bv in local:
            hv = av * hin + bv
            out_ref[pl.ds(r0, 8), :] = hv
            hin = jnp.broadcast_to(hv[0:1, :] if reverse else hv[7:8, :], (8, LRU_W))
        return hin

    return lax.fori_loop(0, nblk // SCAN_UNROLL, step, carry)


MESH = pl.DeviceIdType.MESH
ANY = pl.BlockSpec(memory_space=pl.ANY)


def _place():
    x, y, c = lax.axis_index("x"), lax.axis_index("y"), lax.axis_index("c")
    chips = [(1 - x, y), (x, 1 - y), (1 - x, 1 - y)]
    return x, y, c, chips


class _Gather:
    def __init__(self, ins, outs, send_sems, recv_sems, local_sems, core_major=False):
        self.ins, self.outs, self.n = ins, outs, len(ins)
        self.send_sems, self.recv_sems, self.local_sems = send_sems, recv_sems, local_sems
        self.core_major = core_major

    @staticmethod
    def scratch(n):
        return [pltpu.SemaphoreType.DMA((7, n)), pltpu.SemaphoreType.DMA((7, n)), pltpu.SemaphoreType.DMA((n,))]

    def _slot(self, a, px, py, pc):
        return self.outs[a].at[4 * pc + 2 * px + py if self.core_major else 4 * px + 2 * py + pc]

    def _copy(self, a, k, block, to, src=None):
        return pltpu.make_async_remote_copy(
            src_ref=self._slot(a, *block) if src is None else src, dst_ref=self._slot(a, *block),
            send_sem=self.send_sems.at[k, a], recv_sem=self.recv_sems.at[k, a], device_id=to, device_id_type=MESH)

    def _mine(self, a):
        x, y, c, _ = _place()
        return pltpu.make_async_copy(self.ins[a], self._slot(a, x, y, c), self.local_sems.at[a])

    def _first(self, a):
        x, y, c, chips = _place()
        me = (x, y, c)
        return ([self._copy(a, 0, me, (x, y, 1 - c), src=self.ins[a])]
                + [self._copy(a, 1 + j, me, (*chip, c), src=self.ins[a]) for j, chip in enumerate(chips)])

    def start(self):
        for a in range(self.n):
            self._mine(a).start()
        for a in range(self.n):
            for cp in self._first(a):
                cp.start()

    def finish(self):
        x, y, c, chips = _place()
        me, sibling = (x, y, c), (x, y, 1 - c)
        passed = []
        for j, chip in enumerate(chips):
            for a in range(self.n):
                self._copy(a, 1 + j, (*chip, c), me).wait_recv()
                fwd = self._copy(a, 4 + j, (*chip, c), sibling)
                fwd.start()
                passed.append(fwd)
        for a in range(self.n):
            self._copy(a, 0, (x, y, 1 - c), me).wait_recv()
            for j, chip in enumerate(chips):
                self._copy(a, 4 + j, (*chip, 1 - c), me).wait_recv()
        for a in range(self.n):
            for cp in self._first(a):
                cp.wait_send()
        for cp in passed:
            cp.wait_send()
        for a in range(self.n):
            self._mine(a).wait()


def _all_gather(arrs, name):
    n = len(arrs)

    def body(*refs):
        g = _Gather(refs[:n], refs[n:2 * n], *refs[2 * n:])
        g.start()
        g.finish()

    return pl.pallas_call(
        body, name=name,
        out_shape=[_sds((N_DEV,) + a.shape, a.dtype) for a in arrs],
        in_specs=[ANY] * n, out_specs=[ANY] * n, scratch_shapes=_Gather.scratch(n),
    )(*arrs)


def _pair_exchange(arrs, name):
    n = len(arrs)

    def body(*refs):
        ins, outs = refs[:n], refs[n:2 * n]
        send_sems, recv_sems = refs[2 * n:]
        x, y, c, _ = _place()
        sibling = (x, y, 1 - c)
        sends = []
        for a in range(n):
            for j in range(4):
                cp = pltpu.make_async_remote_copy(
                    src_ref=ins[a].at[2 * j + (1 - c)], dst_ref=outs[a].at[j],
                    send_sem=send_sems.at[j, a], recv_sem=recv_sems.at[j, a], device_id=sibling, device_id_type=MESH)
                cp.start()
                sends.append(cp)
        for cp in sends:
            cp.wait()

    return pl.pallas_call(
        body, name=name,
        out_shape=[_sds((4,) + a.shape[1:], a.dtype) for a in arrs],
        in_specs=[ANY] * n, out_specs=[ANY] * n,
        scratch_shapes=[pltpu.SemaphoreType.DMA((4, n)), pltpu.SemaphoreType.DMA((4, n))],
    )(*arrs)


class _ChipExchange:
    def __init__(self, ins, outs, send_sems, recv_sems, local_sems):
        self.ins, self.outs, self.n = ins, outs, len(ins)
        self.send_sems, self.recv_sems, self.local_sems = send_sems, recv_sems, local_sems

    @staticmethod
    def scratch(n):
        return [pltpu.SemaphoreType.DMA((3, n)), pltpu.SemaphoreType.DMA((3, n)), pltpu.SemaphoreType.DMA((n,))]

    def _local(self, a):
        x, y, _, _ = _place()
        me = 2 * x + y
        return pltpu.make_async_copy(self.ins[a].at[me], self.outs[a].at[me], self.local_sems.at[a])

    def _copies(self, a):
        x, y, c, chips = _place()
        me = 2 * x + y
        return [(pltpu.make_async_remote_copy(
                     src_ref=self.ins[a].at[2 * px + py], dst_ref=self.outs[a].at[me],
                     send_sem=self.send_sems.at[k, a], recv_sem=self.recv_sems.at[k, a],
                     device_id=(px, py, c), device_id_type=MESH),
                 pltpu.make_async_remote_copy(
                     src_ref=self.ins[a].at[me], dst_ref=self.outs[a].at[2 * px + py],
                     send_sem=self.send_sems.at[k, a], recv_sem=self.recv_sems.at[k, a],
                     device_id=(px, py, c), device_id_type=MESH))
                for k, (px, py) in enumerate(chips)]

    def start(self):
        for a in range(self.n):
            self._local(a).start()
        for a in range(self.n):
            for send, _ in self._copies(a):
                send.start()

    def finish(self):
        for a in range(self.n):
            for send, recv in self._copies(a):
                send.wait_send()
                recv.wait_recv()
        for a in range(self.n):
            self._local(a).wait()


def _chip_exchange(arrs, name):
    n = len(arrs)

    def body(*refs):
        e = _ChipExchange(refs[:n], refs[n:2 * n], *refs[2 * n:])
        e.start()
        e.finish()

    return pl.pallas_call(
        body, name=name, out_shape=[_sds(a.shape, a.dtype) for a in arrs],
        in_specs=[ANY] * n, out_specs=[ANY] * n, scratch_shapes=_ChipExchange.scratch(n),
    )(*arrs)


def _mix_in(x, g_mix, w_in_t, tm, shards=()):
    T = x.shape[0]
    n_t = T // tm
    n_s = len(shards)

    def body(x_ref, g_ref, w_ref, *rest):
        sh_in, rest = rest[:n_s], rest[n_s:]
        u_ref, h_ref = rest[:2]
        gather = _Gather(sh_in, rest[2:2 + n_s], *rest[2 + n_s:], core_major=True) if n_s else None
        i = pl.program_id(0)

        if gather:
            @pl.when(i == 0)
            def _():
                gather.start()

        xv = x_ref[...]
        h = (xv * _rstd(xv) * g_ref[...]).astype(BF16)
        h_ref[...] = h
        u_ref[...] = lax.dot_general(h, w_ref[...], NT, preferred_element_type=F32)

        if gather:
            @pl.when(i == n_t - 1)
            def _():
                gather.finish()

    outs = pl.pallas_call(
        body, name="mix_in", grid=(n_t,),
        in_specs=[pl.BlockSpec((tm, D_MODEL), lambda i: (i, 0)),
                  pl.BlockSpec((1, D_MODEL), lambda i: (0, 0)),
                  pl.BlockSpec((D_IN, D_MODEL), lambda i: (0, 0))] + [ANY] * n_s,
        out_specs=[pl.BlockSpec((tm, D_IN), lambda i: (i, 0)),
                   pl.BlockSpec((tm, D_MODEL), lambda i: (i, 0))] + [ANY] * n_s,
        out_shape=[_sds((T, D_IN), F32), _sds((T, D_MODEL), BF16)] + [_sds((N_DEV,) + a.shape, a.dtype) for a in shards],
        scratch_shapes=_Gather.scratch(n_s) if n_s else [],
        compiler_params=pltpu.CompilerParams(dimension_semantics=("arbitrary",)),
    )(x, g_mix, w_in_t, *shards)
    return outs[0], outs[1], list(outs[2:])


def _mixer_fwd(u, x, pv, wa, wx, wp, w_out_b, g_ffn, tm, shards=()):
    T = u.shape[0]
    n_s = len(shards)
    n_t = T // tm

    def body(u_ref, x_ref, pv_ref, wa_in, wx_in, wp_in, wo_ref, gf_ref, *rest):
        sh_in, rest = rest[:n_s], rest[n_s:]
        y_ref, hs_ref, hres_ref, h2_ref = rest[:4]
        sh_out, rest = rest[4:4 + n_s], rest[4 + n_s:]
        e_lru, e_pool, a_s, b_s, hc, wa_ref, wx_ref, wp_ref = rest[:8]
        gather = _Gather(sh_in, sh_out, *rest[8:], core_major=True) if n_s else None
        i = pl.program_id(0)

        @pl.when(i == 0)
        def _():
            if gather:
                gather.start()
            e_lru[pl.ds(0, HALO), :] = jnp.zeros((HALO, LRU_W), F32)
            e_pool[pl.ds(0, HALO), :] = jnp.zeros((HALO, POOL_W), F32)
            hc[...] = jnp.zeros((8, LRU_W), F32)
            _fill_block_diag(wa_ref, wa_in)
            _fill_block_diag(wx_ref, wx_in)
            _fill_block_diag(wp_ref, wp_in)

        e_lru[pl.ds(HALO, tm), :] = u_ref[:, 0:LRU_W]
        e_pool[pl.ds(HALO, tm), :] = u_ref[:, 2 * LRU_W:D_IN]
        pv = pv_ref[...]
        p = _mixer_pre(e_lru, e_pool, pv, wa_ref, wx_ref, wp_ref, tm, i * tm)
        a_s[...] = p["a"]
        b_s[...] = p["mult"] * (p["ig"] * p["xc"])
        hc[...] = _scan_tile(a_s, b_s, hs_ref, hc[...], tm, reverse=False)
        gl, _ = _gelu_parts(u_ref[:, LRU_W:2 * LRU_W])
        y_lru = hs_ref[...] * gl
        y_pool = p["zp"] * pv[ROW_PS:ROW_PS + 1, :]
        yn = jnp.concatenate([y_lru * _rstd(y_lru) * pv[ROW_GL:ROW_GL + 1, :],
                              y_pool * _rstd(y_pool) * pv[ROW_GP:ROW_GP + 1, :]], axis=1).astype(BF16)
        for b in range(N_DEV):
            y_ref[:, 128 * _y_pos(b):128 * (_y_pos(b) + 1)] = yn[:, 128 * b:128 * (b + 1)]
        hr = x_ref[...] + jnp.dot(y_ref[...], wo_ref[...], preferred_element_type=F32)
        hres_ref[...] = hr
        h2_ref[...] = (hr * _rstd(hr) * gf_ref[...]).astype(BF16)
        e_lru[pl.ds(0, HALO), :] = e_lru[pl.ds(tm, HALO), :]
        e_pool[pl.ds(0, HALO), :] = e_pool[pl.ds(tm, HALO), :]

        if gather:
            @pl.when(i == n_t - 1)
            def _():
                gather.finish()

    full = lambda shape: pl.BlockSpec(shape, lambda i: (0,) * len(shape))
    row = lambda w: pl.BlockSpec((tm, w), lambda i: (i, 0))
    outs = pl.pallas_call(
        body, name="mixer_fwd", grid=(n_t,),
        in_specs=[row(D_IN), row(D_MODEL), full((16, LRU_W)), full((8, 64, 64)), full((8, 64, 64)), full((4, 128, 128)),
                  full((D_MODEL, D_MODEL)), full((1, D_MODEL))] + [ANY] * n_s,
        out_specs=[row(D_MODEL), row(LRU_W), row(D_MODEL), row(D_MODEL)] + [ANY] * n_s,
        out_shape=[_sds((T, D_MODEL), BF16), _sds((T, LRU_W), F32), _sds((T, D_MODEL), F32), _sds((T, D_MODEL), BF16)]
        + [_sds((N_DEV,) + a.shape, a.dtype) for a in shards],
        scratch_shapes=[pltpu.VMEM((HALO + tm, LRU_W), F32), pltpu.VMEM((HALO + tm, POOL_W), F32),
                        pltpu.VMEM((tm, LRU_W), F32), pltpu.VMEM((tm, LRU_W), F32), pltpu.VMEM((8, LRU_W), F32)]
        + [pltpu.VMEM((2, 256, 256), BF16)] * 3 + (_Gather.scratch(n_s) if n_s else []),
        compiler_params=pltpu.CompilerParams(dimension_semantics=("arbitrary",)),
    )(u, x, pv, wa, wx, wp, w_out_b, g_ffn, *shards)
    return outs[0], outs[1], outs[2], outs[3], list(outs[4:])


def _ffn_fwd(hres, h2, w1_b, w3_b, w2_b, g_fin, tgt, tm, tn):
    T = hres.shape[0]
    n_j = D_FF // tn

    def body(hres_ref, h2_ref, w1_ref, w3_ref, w2_ref, gfin_ref, tgt_ref,
             g_ref, v_ref, d3_ref, loss_ref, dgfin_ref, acc):
        i, j = pl.program_id(0), pl.program_id(1)

        @pl.when(j == 0)
        def _():
            acc[...] = jnp.zeros((tm, D_MODEL), F32)

        @pl.when((j == 0) & (i == 0))
        def _():
            loss_ref[...] = jnp.zeros((8, 128), F32)
            dgfin_ref[...] = jnp.zeros((1, D_MODEL), F32)

        h2 = h2_ref[...]
        g = lax.dot_general(h2, w1_ref[...], NT, preferred_element_type=F32)
        v = lax.dot_general(h2, w3_ref[...], NT, preferred_element_type=F32)
        g_ref[...] = g.astype(BF16)
        v_ref[...] = v.astype(BF16)
        ff = ((g * _sigmoid(g)) * v).astype(BF16)
        acc[...] += jnp.dot(ff, w2_ref[...], preferred_element_type=F32)

        @pl.when(j == n_j - 1)
        def _():
            h3 = hres_ref[...] + acc[...]
            rstd = _rstd(h3)
            xh = h3 * rstd
            gfin = gfin_ref[...]
            err = xh * gfin - tgt_ref[...]
            loss_ref[...] += 0.5 * jnp.sum(jnp.mean(err * err, axis=-1, keepdims=True))
            dout = err * (1.0 / D_MODEL)
            dx, dgain = _rms_bwd(dout, xh, rstd, gfin)
            d3_ref[...] = dx
            dgfin_ref[...] += dgain

    row = lambda w: pl.BlockSpec((tm, w), lambda i, j: (i, 0))
    const = lambda shape: pl.BlockSpec(shape, lambda i, j: (0,) * len(shape))
    return pl.pallas_call(
        body, name="ffn_fwd", grid=(T // tm, n_j),
        in_specs=[row(D_MODEL), row(D_MODEL),
                  pl.BlockSpec((tn, D_MODEL), lambda i, j: (j, 0)), pl.BlockSpec((tn, D_MODEL), lambda i, j: (j, 0)),
                  pl.BlockSpec((tn, D_MODEL), lambda i, j: (j, 0)), const((1, D_MODEL)), row(D_MODEL)],
        out_specs=[pl.BlockSpec((tm, tn), lambda i, j: (i, j)), pl.BlockSpec((tm, tn), lambda i, j: (i, j)),
                   row(D_MODEL), const((8, 128)), const((1, D_MODEL))],
        out_shape=[_sds((T, D_FF), BF16), _sds((T, D_FF), BF16),
                   _sds((T, D_MODEL), F32), _sds((8, 128), F32), _sds((1, D_MODEL), F32)],
        scratch_shapes=[pltpu.VMEM((tm, D_MODEL), F32)],
        compiler_params=pltpu.CompilerParams(dimension_semantics=("arbitrary", "arbitrary")),
    )(hres, h2, w1_b, w3_b, w2_b, g_fin, tgt)


def _ffn_bwd(d3, g, v, w1_b, w3_b, w2_b, hres, g_ffn, tm, tn):
    T = d3.shape[0]
    n_j = D_FF // tn

    def body(d3_ref, g_ref, v_ref, w1_ref, w3_ref, w2_ref, hres_ref, gf_ref,
             dg_ref, dv_ref, ff_ref, d2_ref, dgffn_ref, acc):
        i, j = pl.program_id(0), pl.program_id(1)

        @pl.when(j == 0)
        def _():
            acc[...] = jnp.zeros((tm, D_MODEL), F32)

        @pl.when((j == 0) & (i == 0))
        def _():
            dgffn_ref[...] = jnp.zeros((1, D_MODEL), F32)

        dff = lax.dot_general(d3_ref[...].astype(BF16), w2_ref[...], NT, preferred_element_type=F32)
        gv = g_ref[...].astype(F32)
        vv = v_ref[...].astype(F32)
        sg = _sigmoid(gv)
        sl = gv * sg
        dgb = (dff * vv * (sg * (1.0 + gv * (1.0 - sg)))).astype(BF16)
        dvb = (dff * sl).astype(BF16)
        dg_ref[...] = dgb
        dv_ref[...] = dvb
        ff_ref[...] = (sl * vv).astype(BF16)
        acc[...] += (jnp.dot(dgb, w1_ref[...], preferred_element_type=F32)
                     + jnp.dot(dvb, w3_ref[...], preferred_element_type=F32))

        @pl.when(j == n_j - 1)
        def _():
            hr = hres_ref[...]
            rstd = _rstd(hr)
            dx, dgain = _rms_bwd(acc[...], hr * rstd, rstd, gf_ref[...])
            d2_ref[...] = d3_ref[...] + dx
            dgffn_ref[...] += dgain

    row = lambda w: pl.BlockSpec((tm, w), lambda i, j: (i, 0))
    tile = pl.BlockSpec((tm, tn), lambda i, j: (i, j))
    const = lambda shape: pl.BlockSpec(shape, lambda i, j: (0,) * len(shape))
    return pl.pallas_call(
        body, name="ffn_bwd", grid=(T // tm, n_j),
        in_specs=[row(D_MODEL), tile, tile,
                  pl.BlockSpec((tn, D_MODEL), lambda i, j: (j, 0)), pl.BlockSpec((tn, D_MODEL), lambda i, j: (j, 0)),
                  pl.BlockSpec((tn, D_MODEL), lambda i, j: (j, 0)), row(D_MODEL), const((1, D_MODEL))],
        out_specs=[tile, tile, tile, row(D_MODEL), const((1, D_MODEL))],
        out_shape=[_sds((T, D_FF), BF16), _sds((T, D_FF), BF16), _sds((T, D_FF), BF16),
                   _sds((T, D_MODEL), F32), _sds((1, D_MODEL), F32)],
        scratch_shapes=[pltpu.VMEM((tm, D_MODEL), F32)],
        compiler_params=pltpu.CompilerParams(dimension_semantics=("arbitrary", "arbitrary")),
    )(d3, g, v, w1_b, w3_b, w2_b, hres, g_ffn)


def _at_b(a, b, name, tmm, tn, tk, gather=()):
    T, M = a.shape
    N = b.shape[1]
    n_m, n_n, n_k = M // tmm, N // tn, T // tk
    n_g = len(gather)

    def body(a_ref, b_ref, *rest):
        g_in, o_ref, rest = rest[:n_g], rest[n_g], rest[n_g + 1:]
        ag = _Gather(g_in, rest[:n_g], *rest[n_g:]) if n_g else None
        m, n, k = pl.program_id(0), pl.program_id(1), pl.program_id(2)

        if ag:
            @pl.when((m == 0) & (n == 0) & (k == 0))
            def _():
                ag.start()

        @pl.when(k == 0)
        def _():
            o_ref[...] = jnp.zeros((tmm, tn), F32)

        o_ref[...] += lax.dot_general(a_ref[...].astype(BF16), b_ref[...].astype(BF16), TN,
                                      preferred_element_type=F32)

        if ag:
            @pl.when((m == n_m - 1) & (n == n_n - 1) & (k == n_k - 1))
            def _():
                ag.finish()

    outs = pl.pallas_call(
        body, name=name, grid=(n_m, n_n, n_k),
        in_specs=[pl.BlockSpec((tk, tmm), lambda m, n, k: (k, m)), pl.BlockSpec((tk, tn), lambda m, n, k: (k, n))]
        + [ANY] * n_g,
        out_specs=[pl.BlockSpec((tmm, tn), lambda m, n, k: (m, n))] + [ANY] * n_g,
        out_shape=[_sds((M, N), F32)] + [_sds((N_DEV,) + g.shape, g.dtype) for g in gather],
        scratch_shapes=_Gather.scratch(n_g) if n_g else [],
        compiler_params=pltpu.CompilerParams(
            dimension_semantics=("arbitrary",) * 3 if n_g else ("parallel", "parallel", "arbitrary")),
    )(a, b, *gather)
    return (outs[0], list(outs[1:])) if n_g else outs[0]


def _at_b_pair(a, b, c_arr, name, tk):
    T, M = a.shape
    N = b.shape[1]
    hm, n_k = M // 2, T // tk

    def body(c_ref, a_ref, b_ref, o_ref, acc, landed, send_sem, recv_sem):
        ph, k = pl.program_id(0), pl.program_id(1)
        def hand_over():
            x, y, c, _ = _place()
            return pltpu.make_async_remote_copy(
                src_ref=acc.at[0], dst_ref=landed, send_sem=send_sem, recv_sem=recv_sem,
                device_id=(x, y, 1 - c), device_id_type=MESH)

        prod = lax.dot_general(a_ref[...].astype(BF16), b_ref[...].astype(BF16), TN, preferred_element_type=F32)
        for half in range(2):
            @pl.when((ph == half) & (k == 0))
            def _():
                acc[half] = prod

            @pl.when((ph == half) & (k > 0))
            def _():
                acc[half] += prod

        @pl.when((ph == 0) & (k == n_k - 1))
        def _():
            hand_over().start()

        @pl.when((ph == 1) & (k == n_k - 1))
        def _():
            copy = hand_over()
            copy.wait_recv()
            o_ref[...] = (acc[1] + landed[...]).astype(BF16)
            copy.wait_send()

    return pl.pallas_call(
        body, name=name,
        grid_spec=pltpu.PrefetchScalarGridSpec(
            num_scalar_prefetch=1, grid=(2, n_k),
            in_specs=[pl.BlockSpec((tk, hm), lambda ph, k, c_ref: (k, (ph + 1 - c_ref[0]) % 2)),
                      pl.BlockSpec((tk, N), lambda ph, k, c_ref: (k, 0))],
            out_specs=pl.BlockSpec((hm, N), lambda ph, k, c_ref: (0, 0)),
            scratch_shapes=[pltpu.VMEM((2, hm, N), F32), pltpu.VMEM((hm, N), F32),
                            pltpu.SemaphoreType.DMA, pltpu.SemaphoreType.DMA]),
        out_shape=_sds((hm, N), BF16),
        compiler_params=pltpu.CompilerParams(dimension_semantics=("arbitrary", "arbitrary")),
    )(c_arr, a, b)


def _mixer_bwd(d2, u, hs, pv, wa, wx, wp, w_out_b, tm, chip_sums=()):
    T = u.shape[0]
    n_t = T // tm
    n_x = len(chip_sums)

    def body(d2_ref, u_ref, uh_ref, hs_ref, hh_ref, pv_ref, wa_in, wx_in, wp_in, wo_ref, *rest):
        x_in, rest = rest[:n_x], rest[n_x:]
        du_ref, sg_ref = rest[:2]
        x_out, rest = rest[2:2 + n_x], rest[2 + n_x:]
        e_lru, e_pool, e_h, a_s, b_s, mu_s, f_x, f_p, mc, cx, cp = rest[:11]
        wa_ref, wx_ref, wp_ref, vacc_ref, dwa_ref, dwx_ref, dwp_ref = rest[11:18]
        exchange = _ChipExchange(x_in, x_out, *rest[18:]) if n_x else None
        s = pl.program_id(0)
        it = n_t - 1 - s

        @pl.when(s == 0)
        def _():
            if exchange:
                exchange.start()
            mc[...] = jnp.zeros((8, LRU_W), F32)
            cx[...] = jnp.zeros((8, LRU_W), F32)
            cp[...] = jnp.zeros((HALO, POOL_W), F32)
            vacc_ref[...] = jnp.zeros((16, LRU_W), F32)
            dwa_ref[...] = jnp.zeros((2, 256, 256), F32)
            dwx_ref[...] = jnp.zeros((2, 256, 256), F32)
            dwp_ref[...] = jnp.zeros((2, 256, 256), F32)
            _fill_block_diag(wa_ref, wa_in)
            _fill_block_diag(wx_ref, wx_in)
            _fill_block_diag(wp_ref, wp_in)

        first = it == 0
        e_lru[pl.ds(0, HALO), :] = jnp.where(first, 0.0, uh_ref[:, 0:LRU_W])
        e_pool[pl.ds(0, HALO), :] = jnp.where(first, 0.0, uh_ref[:, 2 * LRU_W:D_IN])
        e_lru[pl.ds(HALO, tm), :] = u_ref[:, 0:LRU_W]
        e_pool[pl.ds(HALO, tm), :] = u_ref[:, 2 * LRU_W:D_IN]
        e_h[pl.ds(0, 8), :] = jnp.where(first, 0.0, hh_ref[...])
        e_h[pl.ds(8, tm), :] = hs_ref[...]
        pv = pv_ref[...]
        p = _mixer_pre(e_lru, e_pool, pv, wa_ref, wx_ref, wp_ref, tm, it * tm)
        a, xc, ig, r, mult = p["a"], p["xc"], p["ig"], p["r"], p["mult"]

        dyn = lax.dot_general(d2_ref[...].astype(BF16), wo_ref[...], NT, preferred_element_type=F32)
        dyn = jnp.concatenate([dyn[:, 128 * _y_pos(b):128 * (_y_pos(b) + 1)] for b in range(N_DEV)], axis=1)

        h = hs_ref[...]
        ug = u_ref[:, LRU_W:2 * LRU_W]
        gl, dgl = _gelu_parts(ug)
        y_lru = h * gl
        rstd_l = _rstd(y_lru)
        dy_lru, d_gain_l = _rms_bwd(dyn[:, 0:LRU_W], y_lru * rstd_l, rstd_l, pv[ROW_GL:ROW_GL + 1, :])
        dh = dy_lru * gl
        du_ref[:, LRU_W:2 * LRU_W] = (dy_lru * h * dgl).astype(BF16)
        a_s[...] = a
        b_s[...] = a * dh
        mu_s[pl.ds(tm, 8), :] = mc[...]
        mc[...] = _scan_tile(a_s, b_s, mu_s, mc[...], tm, reverse=True)
        lam_t = dh + mu_s[pl.ds(1, tm), :]
        da = lam_t * e_h[pl.ds(7, tm), :]
        dmult = lam_t * (ig * xc)
        di = lam_t * (mult * xc)
        dxc = lam_t * (mult * ig)
        dla = da * a - jnp.where(p["om"] > 1e-12, dmult * ((a * a) * p["rmult"]), 0.0)
        dra = (dla * (-LRU_C * p["sp"])) * (r * (1.0 - r))
        dia = di * (ig * (1.0 - ig))
        drab = dra.astype(BF16)
        diab = dia.astype(BF16)
        dxc = dxc + _bd_t(drab, wa_ref) + _bd_t(diab, wx_ref)
        dwa_ref[...] += _bd_grad(p["xcb"], drab)
        dwx_ref[...] += _bd_grad(p["xcb"], diab)
        sig_neg_lam = _sigmoid(-pv[ROW_LAM:ROW_LAM + 1, :])
        d_lam = jnp.sum(dla * r, axis=0, keepdims=True) * (LRU_C * sig_neg_lam)

        f_x[pl.ds(0, tm), :] = dxc
        f_x[pl.ds(tm, 8), :] = cx[...]
        du_lru = jnp.zeros((tm, LRU_W), F32)
        d_cw = []
        for k in range(4):
            du_lru = du_lru + f_x[pl.ds(3 - k, tm), :] * pv[ROW_CW + k:ROW_CW + k + 1, :]
            d_cw.append(jnp.sum(dxc * p["taps"][k], axis=0, keepdims=True))
        du_ref[:, 0:LRU_W] = du_lru.astype(BF16)
        cx[...] = f_x[pl.ds(0, 8), :]

        zp = p["zp"]
        ps = pv[ROW_PS:ROW_PS + 1, :]
        y_pool = zp * ps
        rstd_p = _rstd(y_pool)
        dy_pool, d_gain_p = _rms_bwd(dyn[:, LRU_W:D_MODEL], y_pool * rstd_p, rstd_p, pv[ROW_GP:ROW_GP + 1, :])
        dz = dy_pool * ps
        dzb = dz.astype(BF16)
        dwp_ref[...] += _bd_grad(p["pooled_b"], dzb)
        dpooled = _bd_t(dzb, wp_ref)
        for g, w in enumerate(POOL_WINDOWS):
            f_p[pl.ds(0, tm), pl.ds(128 * g, 128)] = dpooled[:, 128 * g:128 * (g + 1)] * p["inv_cnts"][g]
        f_p[pl.ds(tm, HALO), :] = cp[...]
        for g, w in enumerate(POOL_WINDOWS):
            acc = _window_sum(f_p[:, pl.ds(128 * g, 128)], w, back=False)[0:tm, :]
            du_ref[:, 2 * LRU_W + 128 * g:2 * LRU_W + 128 * (g + 1)] = (
                acc - dpooled[:, 128 * g:128 * (g + 1)]).astype(BF16)
        cp[...] = f_p[pl.ds(0, HALO), :]

        rows = d_cw + [
            jnp.sum(dxc, axis=0, keepdims=True),
            jnp.sum(dra, axis=0, keepdims=True),
            jnp.sum(dia, axis=0, keepdims=True),
            d_lam,
            jnp.sum(dz, axis=0, keepdims=True),
            jnp.sum(dy_pool * zp, axis=0, keepdims=True),
            d_gain_l, d_gain_p,
            jnp.zeros((4, LRU_W), F32),
        ]
        vacc_ref[...] += jnp.concatenate(rows, axis=0)

        @pl.when(s == n_t - 1)
        def _():
            sg_ref[SG_VEC:SG_VEC + 16, :] = vacc_ref[:, 0:256]
            sg_ref[SG_VEC + 16:SG_VEC + 32, :] = vacc_ref[:, 256:512]
            for half in range(2):
                sg_ref[SG_WA + 64 * half:SG_WA + 64 * (half + 1), :] = _diag_pack(dwa_ref[half], 64)
                sg_ref[SG_WX + 64 * half:SG_WX + 64 * (half + 1), :] = _diag_pack(dwx_ref[half], 64)
                sg_ref[SG_WP + 128 * half:SG_WP + 128 * (half + 1), :] = _diag_pack(dwp_ref[half], 128)
            if exchange:
                exchange.finish()

    rev = lambda w: pl.BlockSpec((tm, w), lambda s: (n_t - 1 - s, 0))
    full = lambda shape: pl.BlockSpec(shape, lambda s: (0,) * len(shape))
    outs = pl.pallas_call(
        body, name="mixer_bwd", grid=(n_t,),
        in_specs=[rev(D_MODEL), rev(D_IN),
                  pl.BlockSpec((HALO, D_IN), lambda s: (jnp.maximum((n_t - 1 - s) * (tm // HALO) - 1, 0), 0)),
                  rev(LRU_W),
                  pl.BlockSpec((8, LRU_W), lambda s: (jnp.maximum((n_t - 1 - s) * (tm // 8) - 1, 0), 0)),
                  full((16, LRU_W)), full((8, 64, 64)), full((8, 64, 64)), full((4, 128, 128)),
                  full((D_MODEL, D_MODEL))] + [ANY] * n_x,
        out_specs=[rev(D_IN), full((SG_ROWS, 256))] + [ANY] * n_x,
        out_shape=[_sds((T, D_IN), BF16), _sds((SG_ROWS, 256), F32)] + [_sds(a.shape, a.dtype) for a in chip_sums],
        scratch_shapes=[pltpu.VMEM((HALO + tm, LRU_W), F32), pltpu.VMEM((HALO + tm, POOL_W), F32),
                        pltpu.VMEM((8 + tm, LRU_W), F32), pltpu.VMEM((tm, LRU_W), F32), pltpu.VMEM((tm, LRU_W), F32),
                        pltpu.VMEM((tm + 8, LRU_W), F32), pltpu.VMEM((tm + 8, LRU_W), F32),
                        pltpu.VMEM((tm + HALO, POOL_W), F32), pltpu.VMEM((8, LRU_W), F32),
                        pltpu.VMEM((8, LRU_W), F32), pltpu.VMEM((HALO, POOL_W), F32)]
        + [pltpu.VMEM((2, 256, 256), BF16)] * 3 + [pltpu.VMEM((16, LRU_W), F32)] + [pltpu.VMEM((2, 256, 256), F32)] * 3
        + (_ChipExchange.scratch(n_x) if n_x else []),
        compiler_params=pltpu.CompilerParams(dimension_semantics=("arbitrary",)),
    )(d2, u, u, hs, hs, pv, wa, wx, wp, w_out_b, *chip_sums)
    return outs[0], outs[1], list(outs[2:])


def _mix_in_bwd(du, x, d2, w_in_t, g_mix, tm, chip_sums=()):
    T = x.shape[0]
    n_t = T // tm
    n_x = len(chip_sums)

    def body(du_ref, x_ref, d2_ref, w_ref, g_ref, *rest):
        x_in, rest = rest[:n_x], rest[n_x:]
        dx_ref, dg_ref = rest[:2]
        exchange = _ChipExchange(x_in, rest[2:2 + n_x], *rest[2 + n_x:]) if n_x else None
        i = pl.program_id(0)

        @pl.when(i == 0)
        def _():
            dg_ref[...] = jnp.zeros((1, D_MODEL), F32)
            if exchange:
                exchange.start()

        dh = jnp.dot(du_ref[...], w_ref[...], preferred_element_type=F32)
        xv = x_ref[...]
        rstd = _rstd(xv)
        dx, dgain = _rms_bwd(dh, xv * rstd, rstd, g_ref[...])
        dx_ref[...] = d2_ref[...] + dx
        dg_ref[...] += dgain

        if exchange:
            @pl.when(i == n_t - 1)
            def _():
                exchange.finish()

    row = lambda w: pl.BlockSpec((tm, w), lambda i: (i, 0))
    const = lambda shape: pl.BlockSpec(shape, lambda i: (0,) * len(shape))
    outs = pl.pallas_call(
        body, name="mix_in_bwd", grid=(n_t,),
        in_specs=[row(D_IN), row(D_MODEL), row(D_MODEL), const((D_IN, D_MODEL)), const((1, D_MODEL))] + [ANY] * n_x,
        out_specs=[row(D_MODEL), const((1, D_MODEL))] + [ANY] * n_x,
        out_shape=[_sds((T, D_MODEL), F32), _sds((1, D_MODEL), F32)] + [_sds(a.shape, a.dtype) for a in chip_sums],
        scratch_shapes=_ChipExchange.scratch(n_x) if n_x else [],
        compiler_params=pltpu.CompilerParams(dimension_semantics=("arbitrary",)),
    )(du, x, d2, w_in_t, g_mix, *chip_sums)
    return outs[0], outs[1], list(outs[2:])


def _pair_sum(g, r1, c_arr, name):
    _, _, R, C = g.shape
    tr = R if R <= 512 else 256

    def body(c_ref, g_ref, r_ref, o_ref):
        o_ref[...] = (g_ref[...] + r_ref[...]).astype(BF16)

    return pl.pallas_call(
        body, name=name,
        grid_spec=pltpu.PrefetchScalarGridSpec(
            num_scalar_prefetch=1, grid=(4, R // tr),
            in_specs=[pl.BlockSpec((None, None, tr, C), lambda j, i, c_ref: (j, c_ref[0], i, 0)),
                      pl.BlockSpec((None, tr, C), lambda j, i, c_ref: (j, i, 0))],
            out_specs=pl.BlockSpec((None, tr, C), lambda j, i, c_ref: (j, i, 0))),
        out_shape=_sds((4, R, C), BF16),
    )(c_arr, g, r1)


def _adamw(w, g, m, v):
    m = ADAM_B1 * m + (1.0 - ADAM_B1) * g
    v = ADAM_B2 * v + (1.0 - ADAM_B2) * (g * g)
    m_hat = m / (1.0 - ADAM_B1 ** ADAM_STEP)
    v_hat = v / (1.0 - ADAM_B2 ** ADAM_STEP)
    delta = -ADAM_LR * (m_hat / (jnp.sqrt(v_hat) + ADAM_EPS) + ADAM_WD * w)
    return delta, m, v


def _adam_shard(w, m, v, parts, name):
    R, C = w.shape
    tr = R if R <= 512 else 256

    def body(w_ref, m_ref, v_ref, p_ref, g_ref, d_ref, nm_ref, nv_ref):
        g = p_ref[0].astype(F32)
        for j in range(1, 4):
            g = g + p_ref[j].astype(F32)
        delta, nm, nv = _adamw(w_ref[...], g, m_ref[...], v_ref[...])
        g_ref[...] = g
        d_ref[...] = delta
        nm_ref[...] = nm
        nv_ref[...] = nv

    blk = pl.BlockSpec((tr, C), lambda i: (i, 0))
    return pl.pallas_call(
        body, name=name, grid=(R // tr,),
        in_specs=[blk, blk, blk, pl.BlockSpec((4, tr, C), lambda i: (0, i, 0))],
        out_specs=[blk] * 4, out_shape=[_sds((R, C), F32)] * 4,
        compiler_params=pltpu.CompilerParams(dimension_semantics=("parallel",)),
    )(w, m, v, parts)


SMALL_PARAMS = [("norm_mix_g", (1, D_MODEL)), ("conv_w", (1, 4, 64)), ("conv_b", (1, LRU_W)),
                ("gate_a_w", (1, 8, 64, 64)), ("gate_a_b", (1, LRU_W)), ("gate_x_w", (1, 8, 64, 64)),
                ("gate_x_b", (1, LRU_W)), ("lru_lambda", (1, LRU_W)), ("pool_w", (1, 4, 128, 128)),
                ("pool_b", (1, POOL_W)), ("pool_scale", (1, POOL_W)), ("norm_lru_g", (1, LRU_W)),
                ("norm_pool_g", (1, POOL_W)), ("norm_ffn_g", (1, D_MODEL)), ("final_norm_g", (1, D_MODEL))]
VEC_ROW = dict(conv_b=ROW_CB, gate_a_b=ROW_BA, gate_x_b=ROW_BX, lru_lambda=ROW_LAM, pool_b=ROW_PB, pool_scale=ROW_PS,
               norm_lru_g=ROW_GL, norm_pool_g=ROW_GP)
WHOLE = (Ellipsis,)


def _unpack_mixer_grads(sg, dev):
    vec = jnp.concatenate([sg[SG_VEC:SG_VEC + 16], sg[SG_VEC + 16:SG_VEC + 32]], axis=1)
    out = {nm: [(WHOLE, vec[r:r + 1])] for nm, r in VEC_ROW.items()}
    own = jnp.zeros((4, 64), F32)
    for d in range(N_DEV):
        own = jnp.where(dev == d, vec[ROW_CW:ROW_CW + 4, 64 * d:64 * (d + 1)], own)
    out["conv_w"] = [((0,), own)]
    for nm, row0 in (("gate_a_w", SG_WA), ("gate_x_w", SG_WX)):
        out[nm] = [((0, b), sg[row0 + 64 * (b // 4):row0 + 64 * (b // 4 + 1), 64 * (b % 4):64 * (b % 4 + 1)])
                   for b in range(8)]
    out["pool_w"] = [((0, b), sg[SG_WP + 128 * (b // 2):SG_WP + 128 * (b // 2 + 1), 128 * (b % 2):128 * (b % 2 + 1)])
                     for b in range(4)]
    return out


def _adam_small(parts, w, m, v):
    names = [nm for nm, _ in SMALL_PARAMS]
    n = len(names)

    def body(sg_ref, gm_ref, gf_ref, gn_ref, ls_ref, *rest):
        w_refs, m_refs, v_refs, outs = rest[:n], rest[n:2 * n], rest[2 * n:3 * n], rest[3 * n:]
        dev = 4 * lax.axis_index("x") + 2 * lax.axis_index("y") + lax.axis_index("c")

        def total(ref):
            acc = ref[0]
            for d in range(1, N_DEV):
                acc = acc + ref[d]
            return acc

        pieces = _unpack_mixer_grads(total(sg_ref), dev)
        pieces["norm_mix_g"] = [(WHOLE, total(gm_ref))]
        pieces["norm_ffn_g"] = [(WHOLE, total(gf_ref))]
        pieces["final_norm_g"] = [(WHOLE, total(gn_ref))]
        for i, nm in enumerate(names):
            for idx, g in pieces[nm]:
                delta, new_m, new_v = _adamw(w_refs[i][idx], g, m_refs[i][idx], v_refs[i][idx])
                for kind, val in enumerate((g, delta, new_m, new_v)):
                    outs[4 * i + kind][idx] = val
        outs[4 * n][...] = total(ls_ref)

    shapes = [_sds(shape, F32) for _, shape in SMALL_PARAMS for _ in range(4)] + [_sds((8, 128), F32)]
    res = pl.pallas_call(body, name="adam_small", out_shape=shapes)(
        *parts, *[w[nm] for nm in names], *[m[nm] for nm in names], *[v[nm] for nm in names])
    return {nm: tuple(res[4 * i:4 * i + 4]) for i, nm in enumerate(names)}, res[4 * n][0, 0]


def _vec_rows(conv_w_full, conv_b, ba, bx, lam, pb, ps, gl, gp):
    return jnp.concatenate([conv_w_full, conv_b, ba, bx, lam, pb, ps, gl, gp, jnp.zeros((4, LRU_W), F32)], axis=0)


WEIGHT_ORDER = ['norm_mix_g', 'w_in', 'conv_w', 'conv_b', 'gate_a_w', 'gate_a_b', 'gate_x_w', 'gate_x_b', 'lru_lambda',
                'pool_w', 'pool_b', 'pool_scale', 'norm_lru_g', 'norm_pool_g', 'w_out', 'norm_ffn_g', 'ffn_w1', 'ffn_w3',
                'ffn_w2', 'final_norm_g']


def kernel(x, norm_mix_g, w_in, conv_w, conv_b, gate_a_w, gate_a_b, gate_x_w, gate_x_b, lru_lambda, pool_w, pool_b, pool_scale, norm_lru_g, norm_pool_g, w_out, norm_ffn_g, ffn_w1, ffn_w3, ffn_w2, final_norm_g, loss_target, m_norm_mix_g, m_w_in, m_conv_w, m_conv_b, m_gate_a_w, m_gate_a_b, m_gate_x_w, m_gate_x_b, m_lru_lambda, m_pool_w, m_pool_b, m_pool_scale, m_norm_lru_g, m_norm_pool_g, m_w_out, m_norm_ffn_g, m_ffn_w1, m_ffn_w3, m_ffn_w2, m_final_norm_g, v_norm_mix_g, v_w_in, v_conv_w, v_conv_b, v_gate_a_w, v_gate_a_b, v_gate_x_w, v_gate_x_b, v_lru_lambda, v_pool_w, v_pool_b, v_pool_scale, v_norm_lru_g, v_norm_pool_g, v_w_out, v_norm_ffn_g, v_ffn_w1, v_ffn_w3, v_ffn_w2, v_final_norm_g):
    ac = lax.axis_index("c")
    tm, tmx, tn, tk = 512, 512, 1408, 1024
    xs, tgt = x[0], loss_target[0]
    g_fin = final_norm_g.reshape(1, D_MODEL)
    c_arr = jnp.reshape(ac, (1,)).astype(jnp.int32)

    def pair_sums(blocks, names):
        from_sibling = _pair_exchange(blocks, "grads_to_sibling_" + names[0])
        return [_pair_sum(g.reshape((4, 2) + g.shape[1:]), r, c_arr, "pair_sum_" + nm)
                for g, r, nm in zip(blocks, from_sibling, names)]

    tr = lambda w: jnp.swapaxes(w[0], 0, 1)
    own = lambda w: w[0]
    bf = lambda a: a.astype(BF16)

    g_in, g_conv = _all_gather([bf(tr(w_in)), conv_w[0]], "gather_w_in")
    w_in_t = g_in.reshape(D_IN, D_MODEL)
    conv_w_full = g_conv.transpose(1, 0, 2).reshape(4, LRU_W)
    pv = _vec_rows(conv_w_full, conv_b, gate_a_b, gate_x_b, lru_lambda, pool_b, pool_scale, norm_lru_g, norm_pool_g)
    wa, wx, wp = gate_a_w[0], gate_x_w[0], pool_w[0]

    u, h1, (g_out, g_w1) = _mix_in(xs, norm_mix_g, w_in_t, tm, shards=[bf(own(w_out)), bf(tr(ffn_w1))])
    w_out_b = g_out.reshape(D_MODEL, D_MODEL)
    y, hs, hres, h2, (g_w3, g_w2) = _mixer_fwd(u, xs, pv, wa, wx, wp, w_out_b, norm_ffn_g, tmx,
                                               shards=[bf(tr(ffn_w3)), bf(own(ffn_w2))])
    w1_t, w3_t, w2_b = g_w1.reshape(D_FF, D_MODEL), g_w3.reshape(D_FF, D_MODEL), g_w2.reshape(D_FF, D_MODEL)
    g, v, d3, loss_acc, d_gfin = _ffn_fwd(hres, h2, w1_t, w3_t, w2_b, g_fin, tgt, tm, tn)

    dg, dv, ff, d2, d_gffn = _ffn_bwd(d3, g, v, w1_t, w3_t, w2_b, hres, norm_ffn_g, tm, tn)
    blocks = lambda a: a.reshape(N_DEV, a.shape[0] // N_DEV, a.shape[1])
    chips = lambda a: a.reshape(4, a.shape[0] // 4, a.shape[1])
    early_sums = [chips(_at_b_pair(y, d2, c_arr, "grad_w_out", tk)), chips(_at_b_pair(dg, h2, c_arr, "grad_w1", tk)),
                  chips(_at_b_pair(dv, h2, c_arr, "grad_w3", tk)), chips(_at_b_pair(ff, d3, c_arr, "grad_w2", tk))]
    du, d_mixer, early_parts = _mixer_bwd(d2, u, hs, pv, wa, wx, wp, w_out_b, tmx, chip_sums=early_sums)
    grad_x, d_gmix, _ = _mix_in_bwd(du, xs, d2, w_in_t, norm_mix_g, tm)
    d_win, small_parts = _at_b(du, h1, "grad_w_in", D_IN, D_MODEL, tk,
                               gather=[d_mixer, d_gmix, d_gffn, d_gfin, loss_acc])
    win_parts = _chip_exchange(pair_sums([blocks(d_win)], ["w_in"]), "grads_to_chips_w_in")
    parts = list(win_parts) + list(early_parts)

    res = {}
    shard_w = dict(w_in=(w_in, m_w_in, v_w_in, tr), w_out=(w_out, m_w_out, v_w_out, own),
                   ffn_w1=(ffn_w1, m_ffn_w1, v_ffn_w1, tr), ffn_w3=(ffn_w3, m_ffn_w3, v_ffn_w3, tr),
                   ffn_w2=(ffn_w2, m_ffn_w2, v_ffn_w2, own))
    for (nm, (w, m, v, view)), p in zip(shard_w.items(), parts):
        outs = _adam_shard(view(w), view(m), view(v), p, "adam_" + nm)
        res[nm] = [(jnp.swapaxes(o, 0, 1) if view is tr else o)[None] for o in outs]

    row = lambda a: a.reshape(1, D_MODEL)
    small = lambda gm, cw, cb, wa_, ba, wx_, bx, lam, pw, pb, ps, gl, gp, gf, gn: dict(
        norm_mix_g=gm, conv_w=cw, conv_b=cb, gate_a_w=wa_, gate_a_b=ba, gate_x_w=wx_, gate_x_b=bx, lru_lambda=lam,
        pool_w=pw, pool_b=pb, pool_scale=ps, norm_lru_g=gl, norm_pool_g=gp, norm_ffn_g=gf, final_norm_g=row(gn))
    small_res, loss = _adam_small(
        small_parts,
        small(norm_mix_g, conv_w, conv_b, gate_a_w, gate_a_b, gate_x_w, gate_x_b, lru_lambda, pool_w, pool_b,
              pool_scale, norm_lru_g, norm_pool_g, norm_ffn_g, final_norm_g),
        small(m_norm_mix_g, m_conv_w, m_conv_b, m_gate_a_w, m_gate_a_b, m_gate_x_w, m_gate_x_b, m_lru_lambda, m_pool_w,
              m_pool_b, m_pool_scale, m_norm_lru_g, m_norm_pool_g, m_norm_ffn_g, m_final_norm_g),
        small(v_norm_mix_g, v_conv_w, v_conv_b, v_gate_a_w, v_gate_a_b, v_gate_x_w, v_gate_x_b, v_lru_lambda, v_pool_w,
              v_pool_b, v_pool_scale, v_norm_lru_g, v_norm_pool_g, v_norm_ffn_g, v_final_norm_g))
    for nm, outs in small_res.items():
        res[nm] = [o.reshape(D_MODEL) for o in outs] if nm == "final_norm_g" else list(outs)

    out = [loss, grad_x[None]]
    for kind in range(4):
        out += [res[nm][kind] for nm in WEIGHT_ORDER]
    return tuple(out)
```

```python
import jax
import jax.numpy as jnp
from jax import lax
from jax.experimental import pallas as pl
from jax.experimental.pallas import tpu as pltpu

F32 = jnp.float32
BF16 = jnp.bfloat16

D_MODEL = 1024
LRU_W = 512
POOL_W = 512
D_IN = 1536
D_FF = 2816
POOL_WINDOWS = (2, 4, 8, 16)
EPS = 1e-6
LRU_C = 8.0
N_DEV = 8
HALO = 16
SCAN_UNROLL = 4

ADAM_LR = 0.001
ADAM_B1 = 0.9
ADAM_B2 = 0.999
ADAM_EPS = 1e-08
ADAM_WD = 0.01
ADAM_STEP = 10

ROW_CW, ROW_CB, ROW_BA, ROW_BX, ROW_LAM, ROW_PB, ROW_PS, ROW_GL, ROW_GP = 0, 4, 5, 6, 7, 8, 9, 10, 11
SG_VEC, SG_WA, SG_WX, SG_WP, SG_ROWS = 0, 32, 160, 288, 544

NT = (((1,), (1,)), ((), ()))
TN = (((0,), (0,)), ((), ()))


def _sds(shape, dtype):
    return jax.ShapeDtypeStruct(shape, dtype)


def _sigmoid(x):
    return 0.5 * jnp.tanh(0.5 * x) + 0.5


def _gelu_parts(x):
    c = 0.7978845608028654
    inner = c * (x + 0.044715 * (x * x * x))
    th = jnp.tanh(inner)
    g = 0.5 * x * (1.0 + th)
    dg = 0.5 * (1.0 + th) + 0.5 * x * (1.0 - th * th) * (c * (1.0 + 3.0 * 0.044715 * (x * x)))
    return g, dg


def _window_sum(ext, w, back):
    n = ext.shape[0]
    s, k = ext, 1
    while k < w:
        s = s + pltpu.roll(s, k if back else n - k, 0)
        k *= 2
    return s


def _rstd(x):
    return lax.rsqrt(jnp.mean(x * x, axis=-1, keepdims=True) + EPS)


def _rms_bwd(dy, xhat, rstd, gain):
    dxh = dy * gain
    dx = rstd * (dxh - xhat * jnp.mean(dxh * xhat, axis=-1, keepdims=True))
    return dx, jnp.sum(dy * xhat, axis=0, keepdims=True)


def _bd(xb, w_ref):
    return jnp.concatenate(
        [jnp.dot(xb[:, :256], w_ref[0], preferred_element_type=F32),
         jnp.dot(xb[:, 256:], w_ref[1], preferred_element_type=F32)], axis=1)


def _bd_t(xb, w_ref):
    return jnp.concatenate(
        [lax.dot_general(xb[:, :256], w_ref[0], NT, preferred_element_type=F32),
         lax.dot_general(xb[:, 256:], w_ref[1], NT, preferred_element_type=F32)], axis=1)


def _bd_grad(xb, db):
    return jnp.stack(
        [lax.dot_general(xb[:, :256], db[:, :256], TN, preferred_element_type=F32),
         lax.dot_general(xb[:, 256:], db[:, 256:], TN, preferred_element_type=F32)], axis=0)


def _fill_block_diag(dst, src_ref):
    n, k, _ = src_ref.shape
    dst[...] = jnp.zeros(dst.shape, BF16)
    for b in range(n):
        p, q = divmod(b, 256 // k)
        dst[p, q * k:(q + 1) * k, q * k:(q + 1) * k] = src_ref[b].astype(BF16)


def _diag_pack(w, k):
    lane = lax.broadcasted_iota(jnp.int32, (k, 256), 1)
    out = w[0:k]
    for q in range(1, 256 // k):
        out = jnp.where(lane >= q * k, w[q * k:(q + 1) * k], out)
    return out


def _y_pos(b):
    return 4 * (b % 2) + b // 2


def _mixer_pre(e_lru, e_pool, pv, wa_ref, wx_ref, wp_ref, tm, t0):
    taps = [e_lru[pl.ds(HALO - 3 + k, tm), :] for k in range(4)]
    xc = pv[ROW_CB:ROW_CB + 1, :]
    for k in range(4):
        xc = xc + taps[k] * pv[ROW_CW + k:ROW_CW + k + 1, :]
    xcb = xc.astype(BF16)
    r = _sigmoid(_bd(xcb, wa_ref) + pv[ROW_BA:ROW_BA + 1, :])
    ig = _sigmoid(_bd(xcb, wx_ref) + pv[ROW_BX:ROW_BX + 1, :])
    z = -pv[ROW_LAM:ROW_LAM + 1, :]
    sp = jnp.maximum(z, 0.0) + jnp.log(1.0 + jnp.exp(-jnp.abs(z)))
    la = (-LRU_C * r) * sp
    a = jnp.exp(la)
    om = -jnp.tanh(la) * (1.0 + a * a)
    omc = jnp.maximum(om, 1e-12)
    rmult = lax.rsqrt(omc)
    mult = omc * rmult
    t = t0 + lax.broadcasted_iota(jnp.int32, (tm, 1), 0)
    parts, inv_cnts = [], []
    for g, w in enumerate(POOL_WINDOWS):
        ext = e_pool[:, pl.ds(128 * g, 128)]
        s = _window_sum(ext, w, back=True)[HALO:, :]
        inv_cnt = 1.0 / jnp.minimum(t + 1, w).astype(F32)
        inv_cnts.append(inv_cnt)
        parts.append(s * inv_cnt - ext[HALO:, :])
    pooled = jnp.concatenate(parts, axis=1)
    pooled_b = pooled.astype(BF16)
    zp = _bd(pooled_b, wp_ref) + pv[ROW_PB:ROW_PB + 1, :]
    return dict(taps=taps, xc=xc, xcb=xcb, r=r, ig=ig, sp=sp, a=a, om=om, mult=mult, rmult=rmult,
                pooled_b=pooled_b, zp=zp, inv_cnts=inv_cnts)


def _scan_tile(a_ref, b_ref, out_ref, carry, tm, reverse):
    row = lax.broadcasted_iota(jnp.int32, (8, LRU_W), 0)
    nblk = tm // 8

    def local_scan(blk):
        r0 = pl.multiple_of(blk * 8, 8)
        av = a_ref[pl.ds(r0, 8), :]
        bv = b_ref[pl.ds(r0, 8), :]
        for d in (1, 2, 4):
            sh = (8 - d) if reverse else d
            a_s = pltpu.roll(av, sh, 0)
            b_s = pltpu.roll(bv, sh, 0)
            m = (row < 8 - d) if reverse else (row >= d)
            bv = jnp.where(m, av * b_s + bv, bv)
            av = jnp.where(m, av * a_s, av)
        return r0, av, bv

    def step(i, hin):
        local = [local_scan((nblk - 1 - (i * SCAN_UNROLL + j)) if reverse else (i * SCAN_UNROLL + j))
                 for j in range(SCAN_UNROLL)]
        for r0, av, bv in local:
            hv = av * hin + bv
            out_ref[pl.ds(r0, 8), :] = hv
            hin = jnp.broadcast_to(hv[0:1, :] if reverse else hv[7:8, :], (8, LRU_W))
        return hin

    return lax.fori_loop(0, nblk // SCAN_UNROLL, step, carry)


MESH = pl.DeviceIdType.MESH
ANY = pl.BlockSpec(memory_space=pl.ANY)


def _place():
    x, y, c = lax.axis_index("x"), lax.axis_index("y"), lax.axis_index("c")
    chips = [(1 - x, y), (x, 1 - y), (1 - x, 1 - y)]
    return x, y, c, chips


class _Gather:
    def __init__(self, ins, outs, send_sems, recv_sems, local_sems, core_major=False):
        self.ins, self.outs, self.n = ins, outs, len(ins)
        self.send_sems, self.recv_sems, self.local_sems = send_sems, recv_sems, local_sems
        self.core_major = core_major

    @staticmethod
    def scratch(n):
        return [pltpu.SemaphoreType.DMA((7, n)), pltpu.SemaphoreType.DMA((7, n)), pltpu.SemaphoreType.DMA((n,))]

    def _slot(self, a, px, py, pc):
        return self.outs[a].at[4 * pc + 2 * px + py if self.core_major else 4 * px + 2 * py + pc]

    def _copy(self, a, k, block, to, src=None):
        return pltpu.make_async_remote_copy(
            src_ref=self._slot(a, *block) if src is None else src, dst_ref=self._slot(a, *block),
            send_sem=self.send_sems.at[k, a], recv_sem=self.recv_sems.at[k, a], device_id=to, device_id_type=MESH)

    def _mine(self, a):
        x, y, c, _ = _place()
        return pltpu.make_async_copy(self.ins[a], self._slot(a, x, y, c), self.local_sems.at[a])

    def _first(self, a):
        x, y, c, chips = _place()
        me = (x, y, c)
        return ([self._copy(a, 0, me, (x, y, 1 - c), src=self.ins[a])]
                + [self._copy(a, 1 + j, me, (*chip, c), src=self.ins[a]) for j, chip in enumerate(chips)])

    def start(self):
        for a in range(self.n):
            self._mine(a).start()
        for a in range(self.n):
            for cp in self._first(a):
                cp.start()

    def finish(self):
        x, y, c, chips = _place()
        me, sibling = (x, y, c), (x, y, 1 - c)
        passed = []
        for j, chip in enumerate(chips):
            for a in range(self.n):
                self._copy(a, 1 + j, (*chip, c), me).wait_recv()
                fwd = self._copy(a, 4 + j, (*chip, c), sibling)
                fwd.start()
                passed.append(fwd)
        for a in range(self.n):
            self._copy(a, 0, (x, y, 1 - c), me).wait_recv()
            for j, chip in enumerate(chips):
                self._copy(a, 4 + j, (*chip, 1 - c), me).wait_recv()
        for a in range(self.n):
            for cp in self._first(a):
                cp.wait_send()
        for cp in passed:
            cp.wait_send()
        for a in range(self.n):
            self._mine(a).wait()


def _all_gather(arrs, name):
    n = len(arrs)

    def body(*refs):
        g = _Gather(refs[:n], refs[n:2 * n], *refs[2 * n:])
        g.start()
        g.finish()

    return pl.pallas_call(
        body, name=name,
        out_shape=[_sds((N_DEV,) + a.shape, a.dtype) for a in arrs],
        in_specs=[ANY] * n, out_specs=[ANY] * n, scratch_shapes=_Gather.scratch(n),
    )(*arrs)


def _pair_exchange(arrs, name):
    n = len(arrs)

    def body(*refs):
        ins, outs = refs[:n], refs[n:2 * n]
        send_sems, recv_sems = refs[2 * n:]
        x, y, c, _ = _place()
        sibling = (x, y, 1 - c)
        sends = []
        for a in range(n):
            for j in range(4):
                cp = pltpu.make_async_remote_copy(
                    src_ref=ins[a].at[2 * j + (1 - c)], dst_ref=outs[a].at[j],
                    send_sem=send_sems.at[j, a], recv_sem=recv_sems.at[j, a], device_id=sibling, device_id_type=MESH)
                cp.start()
                sends.append(cp)
        for cp in sends:
            cp.wait()

    return pl.pallas_call(
        body, name=name,
        out_shape=[_sds((4,) + a.shape[1:], a.dtype) for a in arrs],
        in_specs=[ANY] * n, out_specs=[ANY] * n,
        scratch_shapes=[pltpu.SemaphoreType.DMA((4, n)), pltpu.SemaphoreType.DMA((4, n))],
    )(*arrs)


class _ChipExchange:
    def __init__(self, ins, outs, send_sems, recv_sems, local_sems):
        self.ins, self.outs, self.n = ins, outs, len(ins)
        self.send_sems, self.recv_sems, self.local_sems = send_sems, recv_sems, local_sems

    @staticmethod
    def scratch(n):
        return [pltpu.SemaphoreType.DMA((3, n)), pltpu.SemaphoreType.DMA((3, n)), pltpu.SemaphoreType.DMA((n,))]

    def _local(self, a):
        x, y, _, _ = _place()
        me = 2 * x + y
        return pltpu.make_async_copy(self.ins[a].at[me], self.outs[a].at[me], self.local_sems.at[a])

    def _copies(self, a):
        x, y, c, chips = _place()
        me = 2 * x + y
        return [(pltpu.make_async_remote_copy(
                     src_ref=self.ins[a].at[2 * px + py], dst_ref=self.outs[a].at[me],
                     send_sem=self.send_sems.at[k, a], recv_sem=self.recv_sems.at[k, a],
                     device_id=(px, py, c), device_id_type=MESH),
                 pltpu.make_async_remote_copy(
                     src_ref=self.ins[a].at[me], dst_ref=self.outs[a].at[2 * px + py],
                     send_sem=self.send_sems.at[k, a], recv_sem=self.recv_sems.at[k, a],
                     device_id=(px, py, c), device_id_type=MESH))
                for k, (px, py) in enumerate(chips)]

    def start(self):
        for a in range(self.n):
            self._local(a).start()
        for a in range(self.n):
            for send, _ in self._copies(a):
                send.start()

    def finish(self):
        for a in range(self.n):
            for send, recv in self._copies(a):
                send.wait_send()
                recv.wait_recv()
        for a in range(self.n):
            self._local(a).wait()


def _chip_exchange(arrs, name):
    n = len(arrs)

    def body(*refs):
        e = _ChipExchange(refs[:n], refs[n:2 * n], *refs[2 * n:])
        e.start()
        e.finish()

    return pl.pallas_call(
        body, name=name, out_shape=[_sds(a.shape, a.dtype) for a in arrs],
        in_specs=[ANY] * n, out_specs=[ANY] * n, scratch_shapes=_ChipExchange.scratch(n),
    )(*arrs)


def _mix_in(x, g_mix, w_in_t, tm, shards=()):
    T = x.shape[0]
    n_t = T // tm
    n_s = len(shards)

    def body(x_ref, g_ref, w_ref, *rest):
        sh_in, rest = rest[:n_s], rest[n_s:]
        u_ref, h_ref = rest[:2]
        gather = _Gather(sh_in, rest[2:2 + n_s], *rest[2 + n_s:], core_major=True) if n_s else None
        i = pl.program_id(0)

        if gather:
            @pl.when(i == 0)
            def _():
                gather.start()

        xv = x_ref[...]
        h = (xv * _rstd(xv) * g_ref[...]).astype(BF16)
        h_ref[...] = h
        u_ref[...] = lax.dot_general(h, w_ref[...], NT, preferred_element_type=F32)

        if gather:
            @pl.when(i == n_t - 1)
            def _():
                gather.finish()

    outs = pl.pallas_call(
        body, name="mix_in", grid=(n_t,),
        in_specs=[pl.BlockSpec((tm, D_MODEL), lambda i: (i, 0)),
                  pl.BlockSpec((1, D_MODEL), lambda i: (0, 0)),
                  pl.BlockSpec((D_IN, D_MODEL), lambda i: (0, 0))] + [ANY] * n_s,
        out_specs=[pl.BlockSpec((tm, D_IN), lambda i: (i, 0)),
                   pl.BlockSpec((tm, D_MODEL), lambda i: (i, 0))] + [ANY] * n_s,
        out_shape=[_sds((T, D_IN), F32), _sds((T, D_MODEL), BF16)] + [_sds((N_DEV,) + a.shape, a.dtype) for a in shards],
        scratch_shapes=_Gather.scratch(n_s) if n_s else [],
        compiler_params=pltpu.CompilerParams(dimension_semantics=("arbitrary",)),
    )(x, g_mix, w_in_t, *shards)
    return outs[0], outs[1], list(outs[2:])


def _mixer_fwd(u, x, pv, wa, wx, wp, w_out_b, g_ffn, tm, shards=()):
    T = u.shape[0]
    n_s = len(shards)
    n_t = T // tm

    def body(u_ref, x_ref, pv_ref, wa_in, wx_in, wp_in, wo_ref, gf_ref, *rest):
        sh_in, rest = rest[:n_s], rest[n_s:]
        y_ref, hs_ref, hres_ref, h2_ref = rest[:4]
        sh_out, rest = rest[4:4 + n_s], rest[4 + n_s:]
        e_lru, e_pool, a_s, b_s, hc, wa_ref, wx_ref, wp_ref = rest[:8]
        gather = _Gather(sh_in, sh_out, *rest[8:], core_major=True) if n_s else None
        i = pl.program_id(0)

        @pl.when(i == 0)
        def _():
            if gather:
                gather.start()
            e_lru[pl.ds(0, HALO), :] = jnp.zeros((HALO, LRU_W), F32)
            e_pool[pl.ds(0, HALO), :] = jnp.zeros((HALO, POOL_W), F32)
            hc[...] = jnp.zeros((8, LRU_W), F32)
            _fill_block_diag(wa_ref, wa_in)
            _fill_block_diag(wx_ref, wx_in)
            _fill_block_diag(wp_ref, wp_in)

        e_lru[pl.ds(HALO, tm), :] = u_ref[:, 0:LRU_W]
        e_pool[pl.ds(HALO, tm), :] = u_ref[:, 2 * LRU_W:D_IN]
        pv = pv_ref[...]
        p = _mixer_pre(e_lru, e_pool, pv, wa_ref, wx_ref, wp_ref, tm, i * tm)
        a_s[...] = p["a"]
        b_s[...] = p["mult"] * (p["ig"] * p["xc"])
        hc[...] = _scan_tile(a_s, b_s, hs_ref, hc[...], tm, reverse=False)
        gl, _ = _gelu_parts(u_ref[:, LRU_W:2 * LRU_W])
        y_lru = hs_ref[...] * gl
        y_pool = p["zp"] * pv[ROW_PS:ROW_PS + 1, :]
        yn = jnp.concatenate([y_lru * _rstd(y_lru) * pv[ROW_GL:ROW_GL + 1, :],
                              y_pool * _rstd(y_pool) * pv[ROW_GP:ROW_GP + 1, :]], axis=1).astype(BF16)
        for b in range(N_DEV):
            y_ref[:, 128 * _y_pos(b):128 * (_y_pos(b) + 1)] = yn[:, 128 * b:128 * (b + 1)]
        hr = x_ref[...] + jnp.dot(y_ref[...], wo_ref[...], preferred_element_type=F32)
        hres_ref[...] = hr
        h2_ref[...] = (hr * _rstd(hr) * gf_ref[...]).astype(BF16)
        e_lru[pl.ds(0, HALO), :] = e_lru[pl.ds(tm, HALO), :]
        e_pool[pl.ds(0, HALO), :] = e_pool[pl.ds(tm, HALO), :]

        if gather:
            @pl.when(i == n_t - 1)
            def _():
                gather.finish()

    full = lambda shape: pl.BlockSpec(shape, lambda i: (0,) * len(shape))
    row = lambda w: pl.BlockSpec((tm, w), lambda i: (i, 0))
    outs = pl.pallas_call(
        body, name="mixer_fwd", grid=(n_t,),
        in_specs=[row(D_IN), row(D_MODEL), full((16, LRU_W)), full((8, 64, 64)), full((8, 64, 64)), full((4, 128, 128)),
                  full((D_MODEL, D_MODEL)), full((1, D_MODEL))] + [ANY] * n_s,
        out_specs=[row(D_MODEL), row(LRU_W), row(D_MODEL), row(D_MODEL)] + [ANY] * n_s,
        out_shape=[_sds((T, D_MODEL), BF16), _sds((T, LRU_W), F32), _sds((T, D_MODEL), F32), _sds((T, D_MODEL), BF16)]
        + [_sds((N_DEV,) + a.shape, a.dtype) for a in shards],
        scratch_shapes=[pltpu.VMEM((HALO + tm, LRU_W), F32), pltpu.VMEM((HALO + tm, POOL_W), F32),
                        pltpu.VMEM((tm, LRU_W), F32), pltpu.VMEM((tm, LRU_W), F32), pltpu.VMEM((8, LRU_W), F32)]
        + [pltpu.VMEM((2, 256, 256), BF16)] * 3 + (_Gather.scratch(n_s) if n_s else []),
        compiler_params=pltpu.CompilerParams(dimension_semantics=("arbitrary",)),
    )(u, x, pv, wa, wx, wp, w_out_b, g_ffn, *shards)
    return outs[0], outs[1], outs[2], outs[3], list(outs[4:])


def _ffn_fwd(hres, h2, w1_b, w3_b, w2_b, g_fin, tgt, tm, tn):
    T = hres.shape[0]
    n_j = D_FF // tn

    def body(hres_ref, h2_ref, w1_ref, w3_ref, w2_ref, gfin_ref, tgt_ref,
             g_ref, v_ref, d3_ref, loss_ref, dgfin_ref, acc):
        i, j = pl.program_id(0), pl.program_id(1)

        @pl.when(j == 0)
        def _():
            acc[...] = jnp.zeros((tm, D_MODEL), F32)

        @pl.when((j == 0) & (i == 0))
        def _():
            loss_ref[...] = jnp.zeros((8, 128), F32)
            dgfin_ref[...] = jnp.zeros((1, D_MODEL), F32)

        h2 = h2_ref[...]
        g = lax.dot_general(h2, w1_ref[...], NT, preferred_element_type=F32)
        v = lax.dot_general(h2, w3_ref[...], NT, preferred_element_type=F32)
        g_ref[...] = g.astype(BF16)
        v_ref[...] = v.astype(BF16)
        ff = ((g * _sigmoid(g)) * v).astype(BF16)
        acc[...] += jnp.dot(ff, w2_ref[...], preferred_element_type=F32)

        @pl.when(j == n_j - 1)
        def _():
            h3 = hres_ref[...] + acc[...]
            rstd = _rstd(h3)
            xh = h3 * rstd
            gfin = gfin_ref[...]
            err = xh * gfin - tgt_ref[...]
            loss_ref[...] += 0.5 * jnp.sum(jnp.mean(err * err, axis=-1, keepdims=True))
            dout = err * (1.0 / D_MODEL)
            dx, dgain = _rms_bwd(dout, xh, rstd, gfin)
            d3_ref[...] = dx
            dgfin_ref[...] += dgain

    row = lambda w: pl.BlockSpec((tm, w), lambda i, j: (i, 0))
    const = lambda shape: pl.BlockSpec(shape, lambda i, j: (0,) * len(shape))
    return pl.pallas_call(
        body, name="ffn_fwd", grid=(T // tm, n_j),
        in_specs=[row(D_MODEL), row(D_MODEL),
                  pl.BlockSpec((tn, D_MODEL), lambda i, j: (j, 0)), pl.BlockSpec((tn, D_MODEL), lambda i, j: (j, 0)),
                  pl.BlockSpec((tn, D_MODEL), lambda i, j: (j, 0)), const((1, D_MODEL)), row(D_MODEL)],
        out_specs=[pl.BlockSpec((tm, tn), lambda i, j: (i, j)), pl.BlockSpec((tm, tn), lambda i, j: (i, j)),
                   row(D_MODEL), const((8, 128)), const((1, D_MODEL))],
        out_shape=[_sds((T, D_FF), BF16), _sds((T, D_FF), BF16),
                   _sds((T, D_MODEL), F32), _sds((8, 128), F32), _sds((1, D_MODEL), F32)],
        scratch_shapes=[pltpu.VMEM((tm, D_MODEL), F32)],
        compiler_params=pltpu.CompilerParams(dimension_semantics=("arbitrary", "arbitrary")),
    )(hres, h2, w1_b, w3_b, w2_b, g_fin, tgt)


def _ffn_bwd(d3, g, v, w1_b, w3_b, w2_b, hres, g_ffn, tm, tn):
    T = d3.shape[0]
    n_j = D_FF // tn

    def body(d3_ref, g_ref, v_ref, w1_ref, w3_ref, w2_ref, hres_ref, gf_ref,
             dg_ref, dv_ref, ff_ref, d2_ref, dgffn_ref, acc):
        i, j = pl.program_id(0), pl.program_id(1)

        @pl.when(j == 0)
        def _():
            acc[...] = jnp.zeros((tm, D_MODEL), F32)

        @pl.when((j == 0) & (i == 0))
        def _():
            dgffn_ref[...] = jnp.zeros((1, D_MODEL), F32)

        dff = lax.dot_general(d3_ref[...].astype(BF16), w2_ref[...], NT, preferred_element_type=F32)
        gv = g_ref[...].astype(F32)
        vv = v_ref[...].astype(F32)
        sg = _sigmoid(gv)
        sl = gv * sg
        dgb = (dff * vv * (sg * (1.0 + gv * (1.0 - sg)))).astype(BF16)
        dvb = (dff * sl).astype(BF16)
        dg_ref[...] = dgb
        dv_ref[...] = dvb
        ff_ref[...] = (sl * vv).astype(BF16)
        acc[...] += (jnp.dot(dgb, w1_ref[...], preferred_element_type=F32)
                     + jnp.dot(dvb, w3_ref[...], preferred_element_type=F32))

        @pl.when(j == n_j - 1)
        def _():
            hr = hres_ref[...]
            rstd = _rstd(hr)
            dx, dgain = _rms_bwd(acc[...], hr * rstd, rstd, gf_ref[...])
            d2_ref[...] = d3_ref[...] + dx
            dgffn_ref[...] += dgain

    row = lambda w: pl.BlockSpec((tm, w), lambda i, j: (i, 0))
    tile = pl.BlockSpec((tm, tn), lambda i, j: (i, j))
    const = lambda shape: pl.BlockSpec(shape, lambda i, j: (0,) * len(shape))
    return pl.pallas_call(
        body, name="ffn_bwd", grid=(T // tm, n_j),
        in_specs=[row(D_MODEL), tile, tile,
                  pl.BlockSpec((tn, D_MODEL), lambda i, j: (j, 0)), pl.BlockSpec((tn, D_MODEL), lambda i, j: (j, 0)),
                  pl.BlockSpec((tn, D_MODEL), lambda i, j: (j, 0)), row(D_MODEL), const((1, D_MODEL))],
        out_specs=[tile, tile, tile, row(D_MODEL), const((1, D_MODEL))],
        out_shape=[_sds((T, D_FF), BF16), _sds((T, D_FF), BF16), _sds((T, D_FF), BF16),
                   _sds((T, D_MODEL), F32), _sds((1, D_MODEL), F32)],
        scratch_shapes=[pltpu.VMEM((tm, D_MODEL), F32)],
        compiler_params=pltpu.CompilerParams(dimension_semantics=("arbitrary", "arbitrary")),
    )(d3, g, v, w1_b, w3_b, w2_b, hres, g_ffn)


def _at_b(a, b, name, tmm, tn, tk, gather=()):
    T, M = a.shape
    N = b.shape[1]
    n_m, n_n, n_k = M // tmm, N // tn, T // tk
    n_g = len(gather)

    def body(a_ref, b_ref, *rest):
        g_in, o_ref, rest = rest[:n_g], rest[n_g], rest[n_g + 1:]
        ag = _Gather(g_in, rest[:n_g], *rest[n_g:]) if n_g else None
        m, n, k = pl.program_id(0), pl.program_id(1), pl.program_id(2)

        if ag:
            @pl.when((m == 0) & (n == 0) & (k == 0))
            def _():
                ag.start()

        @pl.when(k == 0)
        def _():
            o_ref[...] = jnp.zeros((tmm, tn), F32)

        o_ref[...] += lax.dot_general(a_ref[...].astype(BF16), b_ref[...].astype(BF16), TN,
                                      preferred_element_type=F32)

        if ag:
            @pl.when((m == n_m - 1) & (n == n_n - 1) & (k == n_k - 1))
            def _():
                ag.finish()

    outs = pl.pallas_call(
        body, name=name, grid=(n_m, n_n, n_k),
        in_specs=[pl.BlockSpec((tk, tmm), lambda m, n, k: (k, m)), pl.BlockSpec((tk, tn), lambda m, n, k: (k, n))]
        + [ANY] * n_g,
        out_specs=[pl.BlockSpec((tmm, tn), lambda m, n, k: (m, n))] + [ANY] * n_g,
        out_shape=[_sds((M, N), F32)] + [_sds((N_DEV,) + g.shape, g.dtype) for g in gather],
        scratch_shapes=_Gather.scratch(n_g) if n_g else [],
        compiler_params=pltpu.CompilerParams(
            dimension_semantics=("arbitrary",) * 3 if n_g else ("parallel", "parallel", "arbitrary")),
    )(a, b, *gather)
    return (outs[0], list(outs[1:])) if n_g else outs[0]


def _at_b_pair(a, b, c_arr, name, tk):
    T, M = a.shape
    N = b.shape[1]
    hm, n_k = M // 2, T // tk

    def body(c_ref, a_ref, b_ref, o_ref, acc, landed, send_sem, recv_sem):
        ph, k = pl.program_id(0), pl.program_id(1)
        def hand_over():
            x, y, c, _ = _place()
            return pltpu.make_async_remote_copy(
                src_ref=acc.at[0], dst_ref=landed, send_sem=send_sem, recv_sem=recv_sem,
                device_id=(x, y, 1 - c), device_id_type=MESH)

        prod = lax.dot_general(a_ref[...].astype(BF16), b_ref[...].astype(BF16), TN, preferred_element_type=F32)
        for half in range(2):
            @pl.when((ph == half) & (k == 0))
            def _():
                acc[half] = prod

            @pl.when((ph == half) & (k > 0))
            def _():
                acc[half] += prod

        @pl.when((ph == 0) & (k == n_k - 1))
        def _():
            hand_over().start()

        @pl.when((ph == 1) & (k == n_k - 1))
        def _():
            copy = hand_over()
            copy.wait_recv()
            o_ref[...] = (acc[1] + landed[...]).astype(BF16)
            copy.wait_send()

    return pl.pallas_call(
        body, name=name,
        grid_spec=pltpu.PrefetchScalarGridSpec(
            num_scalar_prefetch=1, grid=(2, n_k),
            in_specs=[pl.BlockSpec((tk, hm), lambda ph, k, c_ref: (k, (ph + 1 - c_ref[0]) % 2)),
                      pl.BlockSpec((tk, N), lambda ph, k, c_ref: (k, 0))],
            out_specs=pl.BlockSpec((hm, N), lambda ph, k, c_ref: (0, 0)),
            scratch_shapes=[pltpu.VMEM((2, hm, N), F32), pltpu.VMEM((hm, N), F32),
                            pltpu.SemaphoreType.DMA, pltpu.SemaphoreType.DMA]),
        out_shape=_sds((hm, N), BF16),
        compiler_params=pltpu.CompilerParams(dimension_semantics=("arbitrary", "arbitrary")),
    )(c_arr, a, b)


def _mixer_bwd(d2, u, hs, y, pv, wa, wx, wp, w_out_b, tm, chip_sums=()):
    T = u.shape[0]
    n_t = T // tm
    n_x = len(chip_sums)

    def body(d2_ref, u_ref, uh_ref, hs_ref, hh_ref, y_ref, pv_ref, wa_in, wx_in, wp_in, wo_ref, *rest):
        x_in, rest = rest[:n_x], rest[n_x:]
        du_ref, sg_ref, dwo_ref = rest[:3]
        x_out, rest = rest[3:3 + n_x], rest[3 + n_x:]
        e_lru, e_pool, e_h, a_s, b_s, mu_s, f_x, f_p, mc, cx, cp = rest[:11]
        wa_ref, wx_ref, wp_ref, vacc_ref, dwa_ref, dwx_ref, dwp_ref = rest[11:18]
        exchange = _ChipExchange(x_in, x_out, *rest[18:]) if n_x else None
        s = pl.program_id(0)
        it = n_t - 1 - s

        @pl.when(s == 0)
        def _():
            if exchange:
                exchange.start()
            mc[...] = jnp.zeros((8, LRU_W), F32)
            cx[...] = jnp.zeros((8, LRU_W), F32)
            cp[...] = jnp.zeros((HALO, POOL_W), F32)
            vacc_ref[...] = jnp.zeros((16, LRU_W), F32)
            dwa_ref[...] = jnp.zeros((2, 256, 256), F32)
            dwx_ref[...] = jnp.zeros((2, 256, 256), F32)
            dwp_ref[...] = jnp.zeros((2, 256, 256), F32)
            dwo_ref[...] = jnp.zeros((D_MODEL, D_MODEL), F32)
            _fill_block_diag(wa_ref, wa_in)
            _fill_block_diag(wx_ref, wx_in)
            _fill_block_diag(wp_ref, wp_in)

        first = it == 0
        e_lru[pl.ds(0, HALO), :] = jnp.where(first, 0.0, uh_ref[:, 0:LRU_W])
        e_pool[pl.ds(0, HALO), :] = jnp.where(first, 0.0, uh_ref[:, 2 * LRU_W:D_IN])
        e_lru[pl.ds(HALO, tm), :] = u_ref[:, 0:LRU_W]
        e_pool[pl.ds(HALO, tm), :] = u_ref[:, 2 * LRU_W:D_IN]
        e_h[pl.ds(0, 8), :] = jnp.where(first, 0.0, hh_ref[...])
        e_h[pl.ds(8, tm), :] = hs_ref[...]
        pv = pv_ref[...]
        p = _mixer_pre(e_lru, e_pool, pv, wa_ref, wx_ref, wp_ref, tm, it * tm)
        a, xc, ig, r, mult = p["a"], p["xc"], p["ig"], p["r"], p["mult"]

        d2b = d2_ref[...].astype(BF16)
        dwo_ref[...] += lax.dot_general(y_ref[...], d2b, TN, preferred_element_type=F32)
        dyn = lax.dot_general(d2b, wo_ref[...], NT, preferred_element_type=F32)
        dyn = jnp.concatenate([dyn[:, 128 * _y_pos(b):128 * (_y_pos(b) + 1)] for b in range(N_DEV)], axis=1)

        h = hs_ref[...]
        ug = u_ref[:, LRU_W:2 * LRU_W]
        gl, dgl = _gelu_parts(ug)
        y_lru = h * gl
        rstd_l = _rstd(y_lru)
        dy_lru, d_gain_l = _rms_bwd(dyn[:, 0:LRU_W], y_lru * rstd_l, rstd_l, pv[ROW_GL:ROW_GL + 1, :])
        dh = dy_lru * gl
        du_ref[:, LRU_W:2 * LRU_W] = (dy_lru * h * dgl).astype(BF16)
        a_s[...] = a
        b_s[...] = a * dh
        mu_s[pl.ds(tm, 8), :] = mc[...]
        mc[...] = _scan_tile(a_s, b_s, mu_s, mc[...], tm, reverse=True)
        lam_t = dh + mu_s[pl.ds(1, tm), :]
        da = lam_t * e_h[pl.ds(7, tm), :]
        dmult = lam_t * (ig * xc)
        di = lam_t * (mult * xc)
        dxc = lam_t * (mult * ig)
        dla = da * a - jnp.where(p["om"] > 1e-12, dmult * ((a * a) * p["rmult"]), 0.0)
        dra = (dla * (-LRU_C * p["sp"])) * (r * (1.0 - r))
        dia = di * (ig * (1.0 - ig))
        drab = dra.astype(BF16)
        diab = dia.astype(BF16)
        dxc = dxc + _bd_t(drab, wa_ref) + _bd_t(diab, wx_ref)
        dwa_ref[...] += _bd_grad(p["xcb"], drab)
        dwx_ref[...] += _bd_grad(p["xcb"], diab)
        sig_neg_lam = _sigmoid(-pv[ROW_LAM:ROW_LAM + 1, :])
        d_lam = jnp.sum(dla * r, axis=0, keepdims=True) * (LRU_C * sig_neg_lam)

        f_x[pl.ds(0, tm), :] = dxc
        f_x[pl.ds(tm, 8), :] = cx[...]
        du_lru = jnp.zeros((tm, LRU_W), F32)
        d_cw = []
        for k in range(4):
            du_lru = du_lru + f_x[pl.ds(3 - k, tm), :] * pv[ROW_CW + k:ROW_CW + k + 1, :]
            d_cw.append(jnp.sum(dxc * p["taps"][k], axis=0, keepdims=True))
        du_ref[:, 0:LRU_W] = du_lru.astype(BF16)
        cx[...] = f_x[pl.ds(0, 8), :]

        zp = p["zp"]
        ps = pv[ROW_PS:ROW_PS + 1, :]
        y_pool = zp * ps
        rstd_p = _rstd(y_pool)
        dy_pool, d_gain_p = _rms_bwd(dyn[:, LRU_W:D_MODEL], y_pool * rstd_p, rstd_p, pv[ROW_GP:ROW_GP + 1, :])
        dz = dy_pool * ps
        dzb = dz.astype(BF16)
        dwp_ref[...] += _bd_grad(p["pooled_b"], dzb)
        dpooled = _bd_t(dzb, wp_ref)
        for g, w in enumerate(POOL_WINDOWS):
            f_p[pl.ds(0, tm), pl.ds(128 * g, 128)] = dpooled[:, 128 * g:128 * (g + 1)] * p["inv_cnts"][g]
        f_p[pl.ds(tm, HALO), :] = cp[...]
        for g, w in enumerate(POOL_WINDOWS):
            acc = _window_sum(f_p[:, pl.ds(128 * g, 128)], w, back=False)[0:tm, :]
            du_ref[:, 2 * LRU_W + 128 * g:2 * LRU_W + 128 * (g + 1)] = (
                acc - dpooled[:, 128 * g:128 * (g + 1)]).astype(BF16)
        cp[...] = f_p[pl.ds(0, HALO), :]

        rows = d_cw + [
            jnp.sum(dxc, axis=0, keepdims=True),
            jnp.sum(dra, axis=0, keepdims=True),
            jnp.sum(dia, axis=0, keepdims=True),
            d_lam,
            jnp.sum(dz, axis=0, keepdims=True),
            jnp.sum(dy_pool * zp, axis=0, keepdims=True),
            d_gain_l, d_gain_p,
            jnp.zeros((4, LRU_W), F32),
        ]
        vacc_ref[...] += jnp.concatenate(rows, axis=0)

        @pl.when(s == n_t - 1)
        def _():
            sg_ref[SG_VEC:SG_VEC + 16, :] = vacc_ref[:, 0:256]
            sg_ref[SG_VEC + 16:SG_VEC + 32, :] = vacc_ref[:, 256:512]
            for half in range(2):
                sg_ref[SG_WA + 64 * half:SG_WA + 64 * (half + 1), :] = _diag_pack(dwa_ref[half], 64)
                sg_ref[SG_WX + 64 * half:SG_WX + 64 * (half + 1), :] = _diag_pack(dwx_ref[half], 64)
                sg_ref[SG_WP + 128 * half:SG_WP + 128 * (half + 1), :] = _diag_pack(dwp_ref[half], 128)
            if exchange:
                exchange.finish()

    rev = lambda w: pl.BlockSpec((tm, w), lambda s: (n_t - 1 - s, 0))
    full = lambda shape: pl.BlockSpec(shape, lambda s: (0,) * len(shape))
    outs = pl.pallas_call(
        body, name="mixer_bwd", grid=(n_t,),
        in_specs=[rev(D_MODEL), rev(D_IN),
                  pl.BlockSpec((HALO, D_IN), lambda s: (jnp.maximum((n_t - 1 - s) * (tm // HALO) - 1, 0), 0)),
                  rev(LRU_W),
                  pl.BlockSpec((8, LRU_W), lambda s: (jnp.maximum((n_t - 1 - s) * (tm // 8) - 1, 0), 0)),
                  rev(D_MODEL), full((16, LRU_W)), full((8, 64, 64)), full((8, 64, 64)), full((4, 128, 128)),
                  full((D_MODEL, D_MODEL))] + [ANY] * n_x,
        out_specs=[rev(D_IN), full((SG_ROWS, 256)), full((D_MODEL, D_MODEL))] + [ANY] * n_x,
        out_shape=[_sds((T, D_IN), BF16), _sds((SG_ROWS, 256), F32), _sds((D_MODEL, D_MODEL), F32)]
        + [_sds(a.shape, a.dtype) for a in chip_sums],
        scratch_shapes=[pltpu.VMEM((HALO + tm, LRU_W), F32), pltpu.VMEM((HALO + tm, POOL_W), F32),
                        pltpu.VMEM((8 + tm, LRU_W), F32), pltpu.VMEM((tm, LRU_W), F32), pltpu.VMEM((tm, LRU_W), F32),
                        pltpu.VMEM((tm + 8, LRU_W), F32), pltpu.VMEM((tm + 8, LRU_W), F32),
                        pltpu.VMEM((tm + HALO, POOL_W), F32), pltpu.VMEM((8, LRU_W), F32),
                        pltpu.VMEM((8, LRU_W), F32), pltpu.VMEM((HALO, POOL_W), F32)]
        + [pltpu.VMEM((2, 256, 256), BF16)] * 3 + [pltpu.VMEM((16, LRU_W), F32)] + [pltpu.VMEM((2, 256, 256), F32)] * 3
        + (_ChipExchange.scratch(n_x) if n_x else []),
        compiler_params=pltpu.CompilerParams(dimension_semantics=("arbitrary",)),
    )(d2, u, u, hs, hs, y, pv, wa, wx, wp, w_out_b, *chip_sums)
    return outs[0], outs[1], outs[2], list(outs[3:])


def _mix_in_bwd(du, x, d2, w_in_t, g_mix, tm, chip_sums=()):
    T = x.shape[0]
    n_t = T // tm
    n_x = len(chip_sums)

    def body(du_ref, x_ref, d2_ref, w_ref, g_ref, *rest):
        x_in, rest = rest[:n_x], rest[n_x:]
        dx_ref, dg_ref = rest[:2]
        exchange = _ChipExchange(x_in, rest[2:2 + n_x], *rest[2 + n_x:]) if n_x else None
        i = pl.program_id(0)

        @pl.when(i == 0)
        def _():
            dg_ref[...] = jnp.zeros((1, D_MODEL), F32)
            if exchange:
                exchange.start()

        dh = jnp.dot(du_ref[...], w_ref[...], preferred_element_type=F32)
        xv = x_ref[...]
        rstd = _rstd(xv)
        dx, dgain = _rms_bwd(dh, xv * rstd, rstd, g_ref[...])
        dx_ref[...] = d2_ref[...] + dx
        dg_ref[...] += dgain

        if exchange:
            @pl.when(i == n_t - 1)
            def _():
                exchange.finish()

    row = lambda w: pl.BlockSpec((tm, w), lambda i: (i, 0))
    const = lambda shape: pl.BlockSpec(shape, lambda i: (0,) * len(shape))
    outs = pl.pallas_call(
        body, name="mix_in_bwd", grid=(n_t,),
        in_specs=[row(D_IN), row(D_MODEL), row(D_MODEL), const((D_IN, D_MODEL)), const((1, D_MODEL))] + [ANY] * n_x,
        out_specs=[row(D_MODEL), const((1, D_MODEL))] + [ANY] * n_x,
        out_shape=[_sds((T, D_MODEL), F32), _sds((1, D_MODEL), F32)] + [_sds(a.shape, a.dtype) for a in chip_sums],
        scratch_shapes=_ChipExchange.scratch(n_x) if n_x else [],
        compiler_params=pltpu.CompilerParams(dimension_semantics=("arbitrary",)),
    )(du, x, d2, w_in_t, g_mix, *chip_sums)
    return outs[0], outs[1], list(outs[2:])


def _pair_sum(g, r1, c_arr, name):
    _, _, R, C = g.shape
    tr = R if R <= 512 else 256

    def body(c_ref, g_ref, r_ref, o_ref):
        o_ref[...] = (g_ref[...] + r_ref[...]).astype(BF16)

    return pl.pallas_call(
        body, name=name,
        grid_spec=pltpu.PrefetchScalarGridSpec(
            num_scalar_prefetch=1, grid=(4, R // tr),
            in_specs=[pl.BlockSpec((None, None, tr, C), lambda j, i, c_ref: (j, c_ref[0], i, 0)),
                      pl.BlockSpec((None, tr, C), lambda j, i, c_ref: (j, i, 0))],
            out_specs=pl.BlockSpec((None, tr, C), lambda j, i, c_ref: (j, i, 0))),
        out_shape=_sds((4, R, C), BF16),
    )(c_arr, g, r1)


def _adamw(w, g, m, v):
    m = ADAM_B1 * m + (1.0 - ADAM_B1) * g
    v = ADAM_B2 * v + (1.0 - ADAM_B2) * (g * g)
    m_hat = m / (1.0 - ADAM_B1 ** ADAM_STEP)
    v_hat = v / (1.0 - ADAM_B2 ** ADAM_STEP)
    delta = -ADAM_LR * (m_hat / (jnp.sqrt(v_hat) + ADAM_EPS) + ADAM_WD * w)
    return delta, m, v


def _adam_shard(w, m, v, parts, name):
    R, C = w.shape
    tr = R if R <= 512 else 256

    def body(w_ref, m_ref, v_ref, p_ref, g_ref, d_ref, nm_ref, nv_ref):
        g = p_ref[0].astype(F32)
        for j in range(1, 4):
            g = g + p_ref[j].astype(F32)
        delta, nm, nv = _adamw(w_ref[...], g, m_ref[...], v_ref[...])
        g_ref[...] = g
        d_ref[...] = delta
        nm_ref[...] = nm
        nv_ref[...] = nv

    blk = pl.BlockSpec((tr, C), lambda i: (i, 0))
    return pl.pallas_call(
        body, name=name, grid=(R // tr,),
        in_specs=[blk, blk, blk, pl.BlockSpec((4, tr, C), lambda i: (0, i, 0))],
        out_specs=[blk] * 4, out_shape=[_sds((R, C), F32)] * 4,
        compiler_params=pltpu.CompilerParams(dimension_semantics=("parallel",)),
    )(w, m, v, parts)


SMALL_PARAMS = [("norm_mix_g", (1, D_MODEL)), ("conv_w", (1, 4, 64)), ("conv_b", (1, LRU_W)),
                ("gate_a_w", (1, 8, 64, 64)), ("gate_a_b", (1, LRU_W)), ("gate_x_w", (1, 8, 64, 64)),
                ("gate_x_b", (1, LRU_W)), ("lru_lambda", (1, LRU_W)), ("pool_w", (1, 4, 128, 128)),
                ("pool_b", (1, POOL_W)), ("pool_scale", (1, POOL_W)), ("norm_lru_g", (1, LRU_W)),
                ("norm_pool_g", (1, POOL_W)), ("norm_ffn_g", (1, D_MODEL)), ("final_norm_g", (1, D_MODEL))]
VEC_ROW = dict(conv_b=ROW_CB, gate_a_b=ROW_BA, gate_x_b=ROW_BX, lru_lambda=ROW_LAM, pool_b=ROW_PB, pool_scale=ROW_PS,
               norm_lru_g=ROW_GL, norm_pool_g=ROW_GP)
WHOLE = (Ellipsis,)


def _unpack_mixer_grads(sg, dev):
    vec = jnp.concatenate([sg[SG_VEC:SG_VEC + 16], sg[SG_VEC + 16:SG_VEC + 32]], axis=1)
    out = {nm: [(WHOLE, vec[r:r + 1])] for nm, r in VEC_ROW.items()}
    own = jnp.zeros((4, 64), F32)
    for d in range(N_DEV):
        own = jnp.where(dev == d, vec[ROW_CW:ROW_CW + 4, 64 * d:64 * (d + 1)], own)
    out["conv_w"] = [((0,), own)]
    for nm, row0 in (("gate_a_w", SG_WA), ("gate_x_w", SG_WX)):
        out[nm] = [((0, b), sg[row0 + 64 * (b // 4):row0 + 64 * (b // 4 + 1), 64 * (b % 4):64 * (b % 4 + 1)])
                   for b in range(8)]
    out["pool_w"] = [((0, b), sg[SG_WP + 128 * (b // 2):SG_WP + 128 * (b // 2 + 1), 128 * (b % 2):128 * (b % 2 + 1)])
                     for b in range(4)]
    return out


def _adam_small(parts, w, m, v):
    names = [nm for nm, _ in SMALL_PARAMS]
    n = len(names)

    def body(sg_ref, gm_ref, gf_ref, gn_ref, ls_ref, *rest):
        w_refs, m_refs, v_refs, outs = rest[:n], rest[n:2 * n], rest[2 * n:3 * n], rest[3 * n:]
        dev = 4 * lax.axis_index("x") + 2 * lax.axis_index("y") + lax.axis_index("c")

        def total(ref):
            acc = ref[0]
            for d in range(1, N_DEV):
                acc = acc + ref[d]
            return acc

        pieces = _unpack_mixer_grads(total(sg_ref), dev)
        pieces["norm_mix_g"] = [(WHOLE, total(gm_ref))]
        pieces["norm_ffn_g"] = [(WHOLE, total(gf_ref))]
        pieces["final_norm_g"] = [(WHOLE, total(gn_ref))]
        for i, nm in enumerate(names):
            for idx, g in pieces[nm]:
                delta, new_m, new_v = _adamw(w_refs[i][idx], g, m_refs[i][idx], v_refs[i][idx])
                for kind, val in enumerate((g, delta, new_m, new_v)):
                    outs[4 * i + kind][idx] = val
        outs[4 * n][...] = total(ls_ref)

    shapes = [_sds(shape, F32) for _, shape in SMALL_PARAMS for _ in range(4)] + [_sds((8, 128), F32)]
    res = pl.pallas_call(body, name="adam_small", out_shape=shapes)(
        *parts, *[w[nm] for nm in names], *[m[nm] for nm in names], *[v[nm] for nm in names])
    return {nm: tuple(res[4 * i:4 * i + 4]) for i, nm in enumerate(names)}, res[4 * n][0, 0]


def _vec_rows(conv_w_full, conv_b, ba, bx, lam, pb, ps, gl, gp):
    return jnp.concatenate([conv_w_full, conv_b, ba, bx, lam, pb, ps, gl, gp, jnp.zeros((4, LRU_W), F32)], axis=0)


WEIGHT_ORDER = ['norm_mix_g', 'w_in', 'conv_w', 'conv_b', 'gate_a_w', 'gate_a_b', 'gate_x_w', 'gate_x_b', 'lru_lambda',
                'pool_w', 'pool_b', 'pool_scale', 'norm_lru_g', 'norm_pool_g', 'w_out', 'norm_ffn_g', 'ffn_w1', 'ffn_w3',
                'ffn_w2', 'final_norm_g']


def kernel(x, norm_mix_g, w_in, conv_w, conv_b, gate_a_w, gate_a_b, gate_x_w, gate_x_b, lru_lambda, pool_w, pool_b, pool_scale, norm_lru_g, norm_pool_g, w_out, norm_ffn_g, ffn_w1, ffn_w3, ffn_w2, final_norm_g, loss_target, m_norm_mix_g, m_w_in, m_conv_w, m_conv_b, m_gate_a_w, m_gate_a_b, m_gate_x_w, m_gate_x_b, m_lru_lambda, m_pool_w, m_pool_b, m_pool_scale, m_norm_lru_g, m_norm_pool_g, m_w_out, m_norm_ffn_g, m_ffn_w1, m_ffn_w3, m_ffn_w2, m_final_norm_g, v_norm_mix_g, v_w_in, v_conv_w, v_conv_b, v_gate_a_w, v_gate_a_b, v_gate_x_w, v_gate_x_b, v_lru_lambda, v_pool_w, v_pool_b, v_pool_scale, v_norm_lru_g, v_norm_pool_g, v_w_out, v_norm_ffn_g, v_ffn_w1, v_ffn_w3, v_ffn_w2, v_final_norm_g):
    ac = lax.axis_index("c")
    tm, tmx, tn, tk = 512, 512, 1408, 1024
    xs, tgt = x[0], loss_target[0]
    g_fin = final_norm_g.reshape(1, D_MODEL)
    c_arr = jnp.reshape(ac, (1,)).astype(jnp.int32)

    def pair_sums(blocks, names):
        from_sibling = _pair_exchange(blocks, "grads_to_sibling_" + names[0])
        return [_pair_sum(g.reshape((4, 2) + g.shape[1:]), r, c_arr, "pair_sum_" + nm)
                for g, r, nm in zip(blocks, from_sibling, names)]

    tr = lambda w: jnp.swapaxes(w[0], 0, 1)
    own = lambda w: w[0]
    bf = lambda a: a.astype(BF16)

    g_in, g_conv = _all_gather([bf(tr(w_in)), conv_w[0]], "gather_w_in")
    w_in_t = g_in.reshape(D_IN, D_MODEL)
    conv_w_full = g_conv.transpose(1, 0, 2).reshape(4, LRU_W)
    pv = _vec_rows(conv_w_full, conv_b, gate_a_b, gate_x_b, lru_lambda, pool_b, pool_scale, norm_lru_g, norm_pool_g)
    wa, wx, wp = gate_a_w[0], gate_x_w[0], pool_w[0]

    u, h1, (g_out, g_w1) = _mix_in(xs, norm_mix_g, w_in_t, tm, shards=[bf(own(w_out)), bf(tr(ffn_w1))])
    w_out_b = g_out.reshape(D_MODEL, D_MODEL)
    y, hs, hres, h2, (g_w3, g_w2) = _mixer_fwd(u, xs, pv, wa, wx, wp, w_out_b, norm_ffn_g, tmx,
                                               shards=[bf(tr(ffn_w3)), bf(own(ffn_w2))])
    w1_t, w3_t, w2_b = g_w1.reshape(D_FF, D_MODEL), g_w3.reshape(D_FF, D_MODEL), g_w2.reshape(D_FF, D_MODEL)
    g, v, d3, loss_acc, d_gfin = _ffn_fwd(hres, h2, w1_t, w3_t, w2_b, g_fin, tgt, tm, tn)

    dg, dv, ff, d2, d_gffn = _ffn_bwd(d3, g, v, w1_t, w3_t, w2_b, hres, norm_ffn_g, tm, tn)
    blocks = lambda a: a.reshape(N_DEV, a.shape[0] // N_DEV, a.shape[1])
    chips = lambda a: a.reshape(4, a.shape[0] // 4, a.shape[1])
    early_sums = [chips(_at_b_pair(dg, h2, c_arr, "grad_w1", tk)), chips(_at_b_pair(dv, h2, c_arr, "grad_w3", tk)),
                  chips(_at_b_pair(ff, d3, c_arr, "grad_w2", tk))]
    du, d_mixer, d_wout, early_parts = _mixer_bwd(d2, u, hs, y, pv, wa, wx, wp, w_out_b, tmx, chip_sums=early_sums)
    grad_x, d_gmix, _ = _mix_in_bwd(du, xs, d2, w_in_t, norm_mix_g, tm)
    d_win, small_parts = _at_b(du, h1, "grad_w_in", D_IN, D_MODEL, tk,
                               gather=[d_mixer, d_gmix, d_gffn, d_gfin, loss_acc])
    d_wout = d_wout.reshape(2, 4, 128, D_MODEL).transpose(1, 0, 2, 3).reshape(N_DEV, 128, D_MODEL)
    late_parts = _chip_exchange(pair_sums([blocks(d_win), d_wout], ["w_in", "w_out"]), "grads_to_chips_w_in")
    parts = list(late_parts) + list(early_parts)

    res = {}
    shard_w = dict(w_in=(w_in, m_w_in, v_w_in, tr), w_out=(w_out, m_w_out, v_w_out, own),
                   ffn_w1=(ffn_w1, m_ffn_w1, v_ffn_w1, tr), ffn_w3=(ffn_w3, m_ffn_w3, v_ffn_w3, tr),
                   ffn_w2=(ffn_w2, m_ffn_w2, v_ffn_w2, own))
    for (nm, (w, m, v, view)), p in zip(shard_w.items(), parts):
        outs = _adam_shard(view(w), view(m), view(v), p, "adam_" + nm)
        res[nm] = [(jnp.swapaxes(o, 0, 1) if view is tr else o)[None] for o in outs]

    row = lambda a: a.reshape(1, D_MODEL)
    small = lambda gm, cw, cb, wa_, ba, wx_, bx, lam, pw, pb, ps, gl, gp, gf, gn: dict(
        norm_mix_g=gm, conv_w=cw, conv_b=cb, gate_a_w=wa_, gate_a_b=ba, gate_x_w=wx_, gate_x_b=bx, lru_lambda=lam,
        pool_w=pw, pool_b=pb, pool_scale=ps, norm_lru_g=gl, norm_pool_g=gp, norm_ffn_g=gf, final_norm_g=row(gn))
    small_res, loss = _adam_small(
        small_parts,
        small(norm_mix_g, conv_w, conv_b, gate_a_w, gate_a_b, gate_x_w, gate_x_b, lru_lambda, pool_w, pool_b,
              pool_scale, norm_lru_g, norm_pool_g, norm_ffn_g, final_norm_g),
        small(m_norm_mix_g, m_conv_w, m_conv_b, m_gate_a_w, m_gate_a_b, m_gate_x_w, m_gate_x_b, m_lru_lambda, m_pool_w,
              m_pool_b, m_pool_scale, m_norm_lru_g, m_norm_pool_g, m_norm_ffn_g, m_final_norm_g),
        small(v_norm_mix_g, v_conv_w, v_conv_b, v_gate_a_w, v_gate_a_b, v_gate_x_w, v_gate_x_b, v_lru_lambda, v_pool_w,
              v_pool_b, v_pool_scale, v_norm_lru_g, v_norm_pool_g, v_norm_ffn_g, v_final_norm_g))
    for nm, outs in small_res.items():
        res[nm] = [o.reshape(D_MODEL) for o in outs] if nm == "final_norm_g" else list(outs)

    out = [loss, grad_x[None]]
    for kind in range(4):
        out += [res[nm][kind] for nm in WEIGHT_ORDER]
    return tuple(out)
```

```python
import jax
import jax.numpy as jnp
from jax import lax
from jax.experimental import pallas as pl
from jax.experimental.pallas import tpu as pltpu

F32 = jnp.float32
BF16 = jnp.bfloat16

D_MODEL = 1024
LRU_W = 512
POOL_W = 512
D_IN = 1536
D_FF = 2816
POOL_WINDOWS = (2, 4, 8, 16)
EPS = 1e-6
LRU_C = 8.0
N_DEV = 8
HALO = 16
SCAN_UNROLL = 4

ADAM_LR = 0.001
ADAM_B1 = 0.9
ADAM_B2 = 0.999
ADAM_EPS = 1e-08
ADAM_WD = 0.01
ADAM_STEP = 10

ROW_CW, ROW_CB, ROW_BA, ROW_BX, ROW_LAM, ROW_PB, ROW_PS, ROW_GL, ROW_GP = 0, 4, 5, 6, 7, 8, 9, 10, 11
SG_VEC, SG_WA, SG_WX, SG_WP, SG_ROWS = 0, 32, 160, 288, 544

NT = (((1,), (1,)), ((), ()))
TN = (((0,), (0,)), ((), ()))


def _sds(shape, dtype):
    return jax.ShapeDtypeStruct(shape, dtype)


def _sigmoid(x):
    return 0.5 * jnp.tanh(0.5 * x) + 0.5


def _gelu_parts(x):
    c = 0.7978845608028654
    inner = c * (x + 0.044715 * (x * x * x))
    th = jnp.tanh(inner)
    g = 0.5 * x * (1.0 + th)
    dg = 0.5 * (1.0 + th) + 0.5 * x * (1.0 - th * th) * (c * (1.0 + 3.0 * 0.044715 * (x * x)))
    return g, dg


def _window_sum(ext, w, back):
    n = ext.shape[0]
    s, k = ext, 1
    while k < w:
        s = s + pltpu.roll(s, k if back else n - k, 0)
        k *= 2
    return s


def _rstd(x):
    return lax.rsqrt(jnp.mean(x * x, axis=-1, keepdims=True) + EPS)


def _rms_bwd(dy, xhat, rstd, gain):
    dxh = dy * gain
    dx = rstd * (dxh - xhat * jnp.mean(dxh * xhat, axis=-1, keepdims=True))
    return dx, jnp.sum(dy * xhat, axis=0, keepdims=True)


def _bd(xb, w_ref):
    return jnp.concatenate(
        [jnp.dot(xb[:, :256], w_ref[0], preferred_element_type=F32),
         jnp.dot(xb[:, 256:], w_ref[1], preferred_element_type=F32)], axis=1)


def _bd_t(xb, w_ref):
    return jnp.concatenate(
        [lax.dot_general(xb[:, :256], w_ref[0], NT, preferred_element_type=F32),
         lax.dot_general(xb[:, 256:], w_ref[1], NT, preferred_element_type=F32)], axis=1)


def _bd_grad(xb, db):
    return jnp.stack(
        [lax.dot_general(xb[:, :256], db[:, :256], TN, preferred_element_type=F32),
         lax.dot_general(xb[:, 256:], db[:, 256:], TN, preferred_element_type=F32)], axis=0)


def _fill_block_diag(dst, src_ref):
    n, k, _ = src_ref.shape
    dst[...] = jnp.zeros(dst.shape, BF16)
    for b in range(n):
        p, q = divmod(b, 256 // k)
        dst[p, q * k:(q + 1) * k, q * k:(q + 1) * k] = src_ref[b].astype(BF16)


def _diag_pack(w, k):
    lane = lax.broadcasted_iota(jnp.int32, (k, 256), 1)
    out = w[0:k]
    for q in range(1, 256 // k):
        out = jnp.where(lane >= q * k, w[q * k:(q + 1) * k], out)
    return out


def _y_pos(b):
    return 4 * (b % 2) + b // 2


def _mixer_pre(e_lru, e_pool, pv, wa_ref, wx_ref, wp_ref, tm, t0):
    taps = [e_lru[pl.ds(HALO - 3 + k, tm), :] for k in range(4)]
    xc = pv[ROW_CB:ROW_CB + 1, :]
    for k in range(4):
        xc = xc + taps[k] * pv[ROW_CW + k:ROW_CW + k + 1, :]
    xcb = xc.astype(BF16)
    r = _sigmoid(_bd(xcb, wa_ref) + pv[ROW_BA:ROW_BA + 1, :])
    ig = _sigmoid(_bd(xcb, wx_ref) + pv[ROW_BX:ROW_BX + 1, :])
    z = -pv[ROW_LAM:ROW_LAM + 1, :]
    sp = jnp.maximum(z, 0.0) + jnp.log(1.0 + jnp.exp(-jnp.abs(z)))
    la = (-LRU_C * r) * sp
    a = jnp.exp(la)
    om = -jnp.tanh(la) * (1.0 + a * a)
    omc = jnp.maximum(om, 1e-12)
    rmult = lax.rsqrt(omc)
    mult = omc * rmult
    t = t0 + lax.broadcasted_iota(jnp.int32, (tm, 1), 0)
    parts, inv_cnts = [], []
    for g, w in enumerate(POOL_WINDOWS):
        ext = e_pool[:, pl.ds(128 * g, 128)]
        s = _window_sum(ext, w, back=True)[HALO:, :]
        inv_cnt = 1.0 / jnp.minimum(t + 1, w).astype(F32)
        inv_cnts.append(inv_cnt)
        parts.append(s * inv_cnt - ext[HALO:, :])
    pooled = jnp.concatenate(parts, axis=1)
    pooled_b = pooled.astype(BF16)
    zp = _bd(pooled_b, wp_ref) + pv[ROW_PB:ROW_PB + 1, :]
    return dict(taps=taps, xc=xc, xcb=xcb, r=r, ig=ig, sp=sp, a=a, om=om, mult=mult, rmult=rmult,
                pooled_b=pooled_b, zp=zp, inv_cnts=inv_cnts)


def _scan_tile(a_ref, b_ref, out_ref, carry, tm, reverse):
    row = lax.broadcasted_iota(jnp.int32, (8, LRU_W), 0)
    nblk = tm // 8

    def local_scan(blk):
        r0 = pl.multiple_of(blk * 8, 8)
        av = a_ref[pl.ds(r0, 8), :]
        bv = b_ref[pl.ds(r0, 8), :]
        for d in (1, 2, 4):
            sh = (8 - d) if reverse else d
            a_s = pltpu.roll(av, sh, 0)
            b_s = pltpu.roll(bv, sh, 0)
            m = (row < 8 - d) if reverse else (row >= d)
            bv = jnp.where(m, av * b_s + bv, bv)
            av = jnp.where(m, av * a_s, av)
        return r0, av, bv

    def step(i, hin):
        local = [local_scan((nblk - 1 - (i * SCAN_UNROLL + j)) if reverse else (i * SCAN_UNROLL + j))
                 for j in range(SCAN_UNROLL)]
        for r0, av, bv in local:
            hv = av * hin + bv
            out_ref[pl.ds(r0, 8), :] = hv
            hin = jnp.broadcast_to(hv[0:1, :] if reverse else hv[7:8, :], (8, LRU_W))
        return hin

    return lax.fori_loop(0, nblk // SCAN_UNROLL, step, carry)


MESH = pl.DeviceIdType.MESH
ANY = pl.BlockSpec(memory_space=pl.ANY)


def _place():
    x, y, c = lax.axis_index("x"), lax.axis_index("y"), lax.axis_index("c")
    chips = [(1 - x, y), (x, 1 - y), (1 - x, 1 - y)]
    return x, y, c, chips


class _Gather:
    def __init__(self, ins, outs, send_sems, recv_sems, local_sems, core_major=False):
        self.ins, self.outs, self.n = ins, outs, len(ins)
        self.send_sems, self.recv_sems, self.local_sems = send_sems, recv_sems, local_sems
        self.core_major = core_major

    @staticmethod
    def scratch(n):
        return [pltpu.SemaphoreType.DMA((7, n)), pltpu.SemaphoreType.DMA((7, n)), pltpu.SemaphoreType.DMA((n,))]

    def _slot(self, a, px, py, pc):
        return self.outs[a].at[4 * pc + 2 * px + py if self.core_major else 4 * px + 2 * py + pc]

    def _copy(self, a, k, block, to, src=None):
        return pltpu.make_async_remote_copy(
            src_ref=self._slot(a, *block) if src is None else src, dst_ref=self._slot(a, *block),
            send_sem=self.send_sems.at[k, a], recv_sem=self.recv_sems.at[k, a], device_id=to, device_id_type=MESH)

    def _mine(self, a):
        x, y, c, _ = _place()
        return pltpu.make_async_copy(self.ins[a], self._slot(a, x, y, c), self.local_sems.at[a])

    def _first(self, a):
        x, y, c, chips = _place()
        me = (x, y, c)
        return ([self._copy(a, 0, me, (x, y, 1 - c), src=self.ins[a])]
                + [self._copy(a, 1 + j, me, (*chip, c), src=self.ins[a]) for j, chip in enumerate(chips)])

    def start(self):
        for a in range(self.n):
            self._mine(a).start()
        for a in range(self.n):
            for cp in self._first(a):
                cp.start()

    def finish(self):
        x, y, c, chips = _place()
        me, sibling = (x, y, c), (x, y, 1 - c)
        passed = []
        for j, chip in enumerate(chips):
            for a in range(self.n):
                self._copy(a, 1 + j, (*chip, c), me).wait_recv()
                fwd = self._copy(a, 4 + j, (*chip, c), sibling)
                fwd.start()
                passed.append(fwd)
        for a in range(self.n):
            self._copy(a, 0, (x, y, 1 - c), me).wait_recv()
            for j, chip in enumerate(chips):
                self._copy(a, 4 + j, (*chip, 1 - c), me).wait_recv()
        for a in range(self.n):
            for cp in self._first(a):
                cp.wait_send()
        for cp in passed:
            cp.wait_send()
        for a in range(self.n):
            self._mine(a).wait()


def _all_gather(arrs, name):
    n = len(arrs)

    def body(*refs):
        g = _Gather(refs[:n], refs[n:2 * n], *refs[2 * n:])
        g.start()
        g.finish()

    return pl.pallas_call(
        body, name=name,
        out_shape=[_sds((N_DEV,) + a.shape, a.dtype) for a in arrs],
        in_specs=[ANY] * n, out_specs=[ANY] * n, scratch_shapes=_Gather.scratch(n),
    )(*arrs)


def _pair_exchange(arrs, name):
    n = len(arrs)

    def body(*refs):
        ins, outs = refs[:n], refs[n:2 * n]
        send_sems, recv_sems = refs[2 * n:]
        x, y, c, _ = _place()
        sibling = (x, y, 1 - c)
        sends = []
        for a in range(n):
            for j in range(4):
                cp = pltpu.make_async_remote_copy(
                    src_ref=ins[a].at[2 * j + (1 - c)], dst_ref=outs[a].at[j],
                    send_sem=send_sems.at[j, a], recv_sem=recv_sems.at[j, a], device_id=sibling, device_id_type=MESH)
                cp.start()
                sends.append(cp)
        for cp in sends:
            cp.wait()

    return pl.pallas_call(
        body, name=name,
        out_shape=[_sds((4,) + a.shape[1:], a.dtype) for a in arrs],
        in_specs=[ANY] * n, out_specs=[ANY] * n,
        scratch_shapes=[pltpu.SemaphoreType.DMA((4, n)), pltpu.SemaphoreType.DMA((4, n))],
    )(*arrs)


class _ChipExchange:
    def __init__(self, ins, outs, send_sems, recv_sems, local_sems):
        self.ins, self.outs, self.n = ins, outs, len(ins)
        self.send_sems, self.recv_sems, self.local_sems = send_sems, recv_sems, local_sems

    @staticmethod
    def scratch(n):
        return [pltpu.SemaphoreType.DMA((3, n)), pltpu.SemaphoreType.DMA((3, n)), pltpu.SemaphoreType.DMA((n,))]

    def _local(self, a):
        x, y, _, _ = _place()
        me = 2 * x + y
        return pltpu.make_async_copy(self.ins[a].at[me], self.outs[a].at[me], self.local_sems.at[a])

    def _copies(self, a):
        x, y, c, chips = _place()
        me = 2 * x + y
        return [(pltpu.make_async_remote_copy(
                     src_ref=self.ins[a].at[2 * px + py], dst_ref=self.outs[a].at[me],
                     send_sem=self.send_sems.at[k, a], recv_sem=self.recv_sems.at[k, a],
                     device_id=(px, py, c), device_id_type=MESH),
                 pltpu.make_async_remote_copy(
                     src_ref=self.ins[a].at[me], dst_ref=self.outs[a].at[2 * px + py],
                     send_sem=self.send_sems.at[k, a], recv_sem=self.recv_sems.at[k, a],
                     device_id=(px, py, c), device_id_type=MESH))
                for k, (px, py) in enumerate(chips)]

    def start(self):
        for a in range(self.n):
            self._local(a).start()
        for a in range(self.n):
            for send, _ in self._copies(a):
                send.start()

    def finish(self):
        for a in range(self.n):
            for send, recv in self._copies(a):
                send.wait_send()
                recv.wait_recv()
        for a in range(self.n):
            self._local(a).wait()


def _chip_exchange(arrs, name):
    n = len(arrs)

    def body(*refs):
        e = _ChipExchange(refs[:n], refs[n:2 * n], *refs[2 * n:])
        e.start()
        e.finish()

    return pl.pallas_call(
        body, name=name, out_shape=[_sds(a.shape, a.dtype) for a in arrs],
        in_specs=[ANY] * n, out_specs=[ANY] * n, scratch_shapes=_ChipExchange.scratch(n),
    )(*arrs)


def _mix_in(x, g_mix, w_in_t, tm, shards=()):
    T = x.shape[0]
    n_t = T // tm
    n_s = len(shards)

    def body(x_ref, g_ref, w_ref, *rest):
        sh_in, rest = rest[:n_s], rest[n_s:]
        u_ref, h_ref = rest[:2]
        gather = _Gather(sh_in, rest[2:2 + n_s], *rest[2 + n_s:], core_major=True) if n_s else None
        i = pl.program_id(0)

        if gather:
            @pl.when(i == 0)
            def _():
                gather.start()

        xv = x_ref[...]
        h = (xv * _rstd(xv) * g_ref[...]).astype(BF16)
        h_ref[...] = h
        u_ref[...] = lax.dot_general(h, w_ref[...], NT, preferred_element_type=F32)

        if gather:
            @pl.when(i == n_t - 1)
            def _():
                gather.finish()

    outs = pl.pallas_call(
        body, name="mix_in", grid=(n_t,),
        in_specs=[pl.BlockSpec((tm, D_MODEL), lambda i: (i, 0)),
                  pl.BlockSpec((1, D_MODEL), lambda i: (0, 0)),
                  pl.BlockSpec((D_IN, D_MODEL), lambda i: (0, 0))] + [ANY] * n_s,
        out_specs=[pl.BlockSpec((tm, D_IN), lambda i: (i, 0)),
                   pl.BlockSpec((tm, D_MODEL), lambda i: (i, 0))] + [ANY] * n_s,
        out_shape=[_sds((T, D_IN), F32), _sds((T, D_MODEL), BF16)] + [_sds((N_DEV,) + a.shape, a.dtype) for a in shards],
        scratch_shapes=_Gather.scratch(n_s) if n_s else [],
        compiler_params=pltpu.CompilerParams(dimension_semantics=("arbitrary",)),
    )(x, g_mix, w_in_t, *shards)
    return outs[0], outs[1], list(outs[2:])


def _mixer_fwd(u, x, pv, wa, wx, wp, w_out_b, g_ffn, tm, shards=()):
    T = u.shape[0]
    n_s = len(shards)
    n_t = T // tm

    def body(u_ref, x_ref, pv_ref, wa_in, wx_in, wp_in, wo_ref, gf_ref, *rest):
        sh_in, rest = rest[:n_s], rest[n_s:]
        y_ref, hs_ref, hres_ref, h2_ref = rest[:4]
        sh_out, rest = rest[4:4 + n_s], rest[4 + n_s:]
        e_lru, e_pool, a_s, b_s, hc, wa_ref, wx_ref, wp_ref = rest[:8]
        gather = _Gather(sh_in, sh_out, *rest[8:], core_major=True) if n_s else None
        i = pl.program_id(0)

        @pl.when(i == 0)
        def _():
            if gather:
                gather.start()
            e_lru[pl.ds(0, HALO), :] = jnp.zeros((HALO, LRU_W), F32)
            e_pool[pl.ds(0, HALO), :] = jnp.zeros((HALO, POOL_W), F32)
            hc[...] = jnp.zeros((8, LRU_W), F32)
            _fill_block_diag(wa_ref, wa_in)
            _fill_block_diag(wx_ref, wx_in)
            _fill_block_diag(wp_ref, wp_in)

        e_lru[pl.ds(HALO, tm), :] = u_ref[:, 0:LRU_W]
        e_pool[pl.ds(HALO, tm), :] = u_ref[:, 2 * LRU_W:D_IN]
        pv = pv_ref[...]
        p = _mixer_pre(e_lru, e_pool, pv, wa_ref, wx_ref, wp_ref, tm, i * tm)
        a_s[...] = p["a"]
        b_s[...] = p["mult"] * (p["ig"] * p["xc"])
        hc[...] = _scan_tile(a_s, b_s, hs_ref, hc[...], tm, reverse=False)
        gl, _ = _gelu_parts(u_ref[:, LRU_W:2 * LRU_W])
        y_lru = hs_ref[...] * gl
        y_pool = p["zp"] * pv[ROW_PS:ROW_PS + 1, :]
        yn = jnp.concatenate([y_lru * _rstd(y_lru) * pv[ROW_GL:ROW_GL + 1, :],
                              y_pool * _rstd(y_pool) * pv[ROW_GP:ROW_GP + 1, :]], axis=1).astype(BF16)
        for b in range(N_DEV):
            y_ref[:, 128 * _y_pos(b):128 * (_y_pos(b) + 1)] = yn[:, 128 * b:128 * (b + 1)]
        hr = x_ref[...] + jnp.dot(y_ref[...], wo_ref[...], preferred_element_type=F32)
        hres_ref[...] = hr
        h2_ref[...] = (hr * _rstd(hr) * gf_ref[...]).astype(BF16)
        e_lru[pl.ds(0, HALO), :] = e_lru[pl.ds(tm, HALO), :]
        e_pool[pl.ds(0, HALO), :] = e_pool[pl.ds(tm, HALO), :]

        if gather:
            @pl.when(i == n_t - 1)
            def _():
                gather.finish()

    full = lambda shape: pl.BlockSpec(shape, lambda i: (0,) * len(shape))
    row = lambda w: pl.BlockSpec((tm, w), lambda i: (i, 0))
    outs = pl.pallas_call(
        body, name="mixer_fwd", grid=(n_t,),
        in_specs=[row(D_IN), row(D_MODEL), full((16, LRU_W)), full((8, 64, 64)), full((8, 64, 64)), full((4, 128, 128)),
                  full((D_MODEL, D_MODEL)), full((1, D_MODEL))] + [ANY] * n_s,
        out_specs=[row(D_MODEL), row(LRU_W), row(D_MODEL), row(D_MODEL)] + [ANY] * n_s,
        out_shape=[_sds((T, D_MODEL), BF16), _sds((T, LRU_W), F32), _sds((T, D_MODEL), F32), _sds((T, D_MODEL), BF16)]
        + [_sds((N_DEV,) + a.shape, a.dtype) for a in shards],
        scratch_shapes=[pltpu.VMEM((HALO + tm, LRU_W), F32), pltpu.VMEM((HALO + tm, POOL_W), F32),
                        pltpu.VMEM((tm, LRU_W), F32), pltpu.VMEM((tm, LRU_W), F32), pltpu.VMEM((8, LRU_W), F32)]
        + [pltpu.VMEM((2, 256, 256), BF16)] * 3 + (_Gather.scratch(n_s) if n_s else []),
        compiler_params=pltpu.CompilerParams(dimension_semantics=("arbitrary",)),
    )(u, x, pv, wa, wx, wp, w_out_b, g_ffn, *shards)
    return outs[0], outs[1], outs[2], outs[3], list(outs[4:])


def _ffn_fwd(hres, h2, w1_b, w3_b, w2_b, g_fin, tgt, tm, tn):
    T = hres.shape[0]
    n_j = D_FF // tn

    def body(hres_ref, h2_ref, w1_ref, w3_ref, w2_ref, gfin_ref, tgt_ref,
             g_ref, v_ref, d3_ref, loss_ref, dgfin_ref, acc):
        i, j = pl.program_id(0), pl.program_id(1)

        @pl.when(j == 0)
        def _():
            acc[...] = jnp.zeros((tm, D_MODEL), F32)

        @pl.when((j == 0) & (i == 0))
        def _():
            loss_ref[...] = jnp.zeros((8, 128), F32)
            dgfin_ref[...] = jnp.zeros((1, D_MODEL), F32)

        h2 = h2_ref[...]
        g = lax.dot_general(h2, w1_ref[...], NT, preferred_element_type=F32)
        v = lax.dot_general(h2, w3_ref[...], NT, preferred_element_type=F32)
        g_ref[...] = g.astype(BF16)
        v_ref[...] = v.astype(BF16)
        ff = ((g * _sigmoid(g)) * v).astype(BF16)
        acc[...] += jnp.dot(ff, w2_ref[...], preferred_element_type=F32)

        @pl.when(j == n_j - 1)
        def _():
            h3 = hres_ref[...] + acc[...]
            rstd = _rstd(h3)
            xh = h3 * rstd
            gfin = gfin_ref[...]
            err = xh * gfin - tgt_ref[...]
            loss_ref[...] += 0.5 * jnp.sum(jnp.mean(err * err, axis=-1, keepdims=True))
            dout = err * (1.0 / D_MODEL)
            dx, dgain = _rms_bwd(dout, xh, rstd, gfin)
            d3_ref[...] = dx
            dgfin_ref[...] += dgain

    row = lambda w: pl.BlockSpec((tm, w), lambda i, j: (i, 0))
    const = lambda shape: pl.BlockSpec(shape, lambda i, j: (0,) * len(shape))
    return pl.pallas_call(
        body, name="ffn_fwd", grid=(T // tm, n_j),
        in_specs=[row(D_MODEL), row(D_MODEL),
                  pl.BlockSpec((tn, D_MODEL), lambda i, j: (j, 0)), pl.BlockSpec((tn, D_MODEL), lambda i, j: (j, 0)),
                  pl.BlockSpec((tn, D_MODEL), lambda i, j: (j, 0)), const((1, D_MODEL)), row(D_MODEL)],
        out_specs=[pl.BlockSpec((tm, tn), lambda i, j: (i, j)), pl.BlockSpec((tm, tn), lambda i, j: (i, j)),
                   row(D_MODEL), const((8, 128)), const((1, D_MODEL))],
        out_shape=[_sds((T, D_FF), BF16), _sds((T, D_FF), BF16),
                   _sds((T, D_MODEL), F32), _sds((8, 128), F32), _sds((1, D_MODEL), F32)],
        scratch_shapes=[pltpu.VMEM((tm, D_MODEL), F32)],
        compiler_params=pltpu.CompilerParams(dimension_semantics=("arbitrary", "arbitrary")),
    )(hres, h2, w1_b, w3_b, w2_b, g_fin, tgt)


def _ffn_bwd(d3, g, v, w1_b, w3_b, w2_b, hres, g_ffn, tm, tn):
    T = d3.shape[0]
    n_j = D_FF // tn

    def body(d3_ref, g_ref, v_ref, w1_ref, w3_ref, w2_ref, hres_ref, gf_ref,
             dg_ref, dv_ref, ff_ref, d2_ref, dgffn_ref, acc):
        i, j = pl.program_id(0), pl.program_id(1)

        @pl.when(j == 0)
        def _():
            acc[...] = jnp.zeros((tm, D_MODEL), F32)

        @pl.when((j == 0) & (i == 0))
        def _():
            dgffn_ref[...] = jnp.zeros((1, D_MODEL), F32)

        dff = lax.dot_general(d3_ref[...].astype(BF16), w2_ref[...], NT, preferred_element_type=F32)
        gv = g_ref[...].astype(F32)
        vv = v_ref[...].astype(F32)
        sg = _sigmoid(gv)
        sl = gv * sg
        dgb = (dff * vv * (sg * (1.0 + gv * (1.0 - sg)))).astype(BF16)
        dvb = (dff * sl).astype(BF16)
        dg_ref[...] = dgb
        dv_ref[...] = dvb
        ff_ref[...] = (sl * vv).astype(BF16)
        acc[...] += (jnp.dot(dgb, w1_ref[...], preferred_element_type=F32)
                     + jnp.dot(dvb, w3_ref[...], preferred_element_type=F32))

        @pl.when(j == n_j - 1)
        def _():
            hr = hres_ref[...]
            rstd = _rstd(hr)
            dx, dgain = _rms_bwd(acc[...], hr * rstd, rstd, gf_ref[...])
            d2_ref[...] = d3_ref[...] + dx
            dgffn_ref[...] += dgain

    row = lambda w: pl.BlockSpec((tm, w), lambda i, j: (i, 0))
    tile = pl.BlockSpec((tm, tn), lambda i, j: (i, j))
    const = lambda shape: pl.BlockSpec(shape, lambda i, j: (0,) * len(shape))
    return pl.pallas_call(
        body, name="ffn_bwd", grid=(T // tm, n_j),
        in_specs=[row(D_MODEL), tile, tile,
                  pl.BlockSpec((tn, D_MODEL), lambda i, j: (j, 0)), pl.BlockSpec((tn, D_MODEL), lambda i, j: (j, 0)),
                  pl.BlockSpec((tn, D_MODEL), lambda i, j: (j, 0)), row(D_MODEL), const((1, D_MODEL))],
        out_specs=[tile, tile, tile, row(D_MODEL), const((1, D_MODEL))],
        out_shape=[_sds((T, D_FF), BF16), _sds((T, D_FF), BF16), _sds((T, D_FF), BF16),
                   _sds((T, D_MODEL), F32), _sds((1, D_MODEL), F32)],
        scratch_shapes=[pltpu.VMEM((tm, D_MODEL), F32)],
        compiler_params=pltpu.CompilerParams(dimension_semantics=("arbitrary", "arbitrary")),
    )(d3, g, v, w1_b, w3_b, w2_b, hres, g_ffn)


def _at_b(a, b, name, tmm, tn, tk, gather=()):
    T, M = a.shape
    N = b.shape[1]
    n_m, n_n, n_k = M // tmm, N // tn, T // tk
    n_g = len(gather)

    def body(a_ref, b_ref, *rest):
        g_in, o_ref, rest = rest[:n_g], rest[n_g], rest[n_g + 1:]
        ag = _Gather(g_in, rest[:n_g], *rest[n_g:]) if n_g else None
        m, n, k = pl.program_id(0), pl.program_id(1), pl.program_id(2)

        if ag:
            @pl.when((m == 0) & (n == 0) & (k == 0))
            def _():
                ag.start()

        @pl.when(k == 0)
        def _():
            o_ref[...] = jnp.zeros((tmm, tn), F32)

        o_ref[...] += lax.dot_general(a_ref[...].astype(BF16), b_ref[...].astype(BF16), TN,
                                      preferred_element_type=F32)

        if ag:
            @pl.when((m == n_m - 1) & (n == n_n - 1) & (k == n_k - 1))
            def _():
                ag.finish()

    outs = pl.pallas_call(
        body, name=name, grid=(n_m, n_n, n_k),
        in_specs=[pl.BlockSpec((tk, tmm), lambda m, n, k: (k, m)), pl.BlockSpec((tk, tn), lambda m, n, k: (k, n))]
        + [ANY] * n_g,
        out_specs=[pl.BlockSpec((tmm, tn), lambda m, n, k: (m, n))] + [ANY] * n_g,
        out_shape=[_sds((M, N), F32)] + [_sds((N_DEV,) + g.shape, g.dtype) for g in gather],
        scratch_shapes=_Gather.scratch(n_g) if n_g else [],
        compiler_params=pltpu.CompilerParams(
            dimension_semantics=("arbitrary",) * 3 if n_g else ("parallel", "parallel", "arbitrary")),
    )(a, b, *gather)
    return (outs[0], list(outs[1:])) if n_g else outs[0]


def _at_b_pair(a, b, c_arr, name, tk):
    T, M = a.shape
    N = b.shape[1]
    hm, n_k = M // 2, T // tk

    def body(c_ref, a_ref, b_ref, o_ref, acc, landed, send_sem, recv_sem):
        ph, k = pl.program_id(0), pl.program_id(1)
        def hand_over():
            x, y, c, _ = _place()
            return pltpu.make_async_remote_copy(
                src_ref=acc.at[0], dst_ref=landed, send_sem=send_sem, recv_sem=recv_sem,
                device_id=(x, y, 1 - c), device_id_type=MESH)

        @pl.when(k == 0)
        def _():
            acc[ph] = jnp.zeros((hm, N), F32)

        acc[ph] += lax.dot_general(a_ref[...].astype(BF16), b_ref[...].astype(BF16), TN, preferred_element_type=F32)

        @pl.when((ph == 0) & (k == n_k - 1))
        def _():
            hand_over().start()

        @pl.when((ph == 1) & (k == n_k - 1))
        def _():
            copy = hand_over()
            copy.wait_recv()
            o_ref[...] = (acc[1] + landed[...]).astype(BF16)
            copy.wait_send()

    return pl.pallas_call(
        body, name=name,
        grid_spec=pltpu.PrefetchScalarGridSpec(
            num_scalar_prefetch=1, grid=(2, n_k),
            in_specs=[pl.BlockSpec((tk, hm), lambda ph, k, c_ref: (k, (ph + 1 - c_ref[0]) % 2)),
                      pl.BlockSpec((tk, N), lambda ph, k, c_ref: (k, 0))],
            out_specs=pl.BlockSpec((hm, N), lambda ph, k, c_ref: (0, 0)),
            scratch_shapes=[pltpu.VMEM((2, hm, N), F32), pltpu.VMEM((hm, N), F32),
                            pltpu.SemaphoreType.DMA, pltpu.SemaphoreType.DMA]),
        out_shape=_sds((hm, N), BF16),
        compiler_params=pltpu.CompilerParams(dimension_semantics=("arbitrary", "arbitrary")),
    )(c_arr, a, b)


def _mixer_bwd(d2, u, hs, pv, wa, wx, wp, w_out_b, tm, chip_sums=()):
    T = u.shape[0]
    n_t = T // tm
    n_x = len(chip_sums)

    def body(d2_ref, u_ref, uh_ref, hs_ref, hh_ref, pv_ref, wa_in, wx_in, wp_in, wo_ref, *rest):
        x_in, rest = rest[:n_x], rest[n_x:]
        du_ref, sg_ref = rest[:2]
        x_out, rest = rest[2:2 + n_x], rest[2 + n_x:]
        e_lru, e_pool, e_h, a_s, b_s, mu_s, f_x, f_p, mc, cx, cp = rest[:11]
        wa_ref, wx_ref, wp_ref, vacc_ref, dwa_ref, dwx_ref, dwp_ref = rest[11:18]
        exchange = _ChipExchange(x_in, x_out, *rest[18:]) if n_x else None
        s = pl.program_id(0)
        it = n_t - 1 - s

        @pl.when(s == 0)
        def _():
            if exchange:
                exchange.start()
            mc[...] = jnp.zeros((8, LRU_W), F32)
            cx[...] = jnp.zeros((8, LRU_W), F32)
            cp[...] = jnp.zeros((HALO, POOL_W), F32)
            vacc_ref[...] = jnp.zeros((16, LRU_W), F32)
            dwa_ref[...] = jnp.zeros((2, 256, 256), F32)
            dwx_ref[...] = jnp.zeros((2, 256, 256), F32)
            dwp_ref[...] = jnp.zeros((2, 256, 256), F32)
            _fill_block_diag(wa_ref, wa_in)
            _fill_block_diag(wx_ref, wx_in)
            _fill_block_diag(wp_ref, wp_in)

        first = it == 0
        e_lru[pl.ds(0, HALO), :] = jnp.where(first, 0.0, uh_ref[:, 0:LRU_W])
        e_pool[pl.ds(0, HALO), :] = jnp.where(first, 0.0, uh_ref[:, 2 * LRU_W:D_IN])
        e_lru[pl.ds(HALO, tm), :] = u_ref[:, 0:LRU_W]
        e_pool[pl.ds(HALO, tm), :] = u_ref[:, 2 * LRU_W:D_IN]
        e_h[pl.ds(0, 8), :] = jnp.where(first, 0.0, hh_ref[...])
        e_h[pl.ds(8, tm), :] = hs_ref[...]
        pv = pv_ref[...]
        p = _mixer_pre(e_lru, e_pool, pv, wa_ref, wx_ref, wp_ref, tm, it * tm)
        a, xc, ig, r, mult = p["a"], p["xc"], p["ig"], p["r"], p["mult"]

        dyn = lax.dot_general(d2_ref[...].astype(BF16), wo_ref[...], NT, preferred_element_type=F32)
        dyn = jnp.concatenate([dyn[:, 128 * _y_pos(b):128 * (_y_pos(b) + 1)] for b in range(N_DEV)], axis=1)

        h = hs_ref[...]
        ug = u_ref[:, LRU_W:2 * LRU_W]
        gl, dgl = _gelu_parts(ug)
        y_lru = h * gl
        rstd_l = _rstd(y_lru)
        dy_lru, d_gain_l = _rms_bwd(dyn[:, 0:LRU_W], y_lru * rstd_l, rstd_l, pv[ROW_GL:ROW_GL + 1, :])
        dh = dy_lru * gl
        du_ref[:, LRU_W:2 * LRU_W] = (dy_lru * h * dgl).astype(BF16)
        a_s[...] = a
        b_s[...] = a * dh
        mu_s[pl.ds(tm, 8), :] = mc[...]
        mc[...] = _scan_tile(a_s, b_s, mu_s, mc[...], tm, reverse=True)
        lam_t = dh + mu_s[pl.ds(1, tm), :]
        da = lam_t * e_h[pl.ds(7, tm), :]
        dmult = lam_t * (ig * xc)
        di = lam_t * (mult * xc)
        dxc = lam_t * (mult * ig)
        dla = da * a - jnp.where(p["om"] > 1e-12, dmult * ((a * a) * p["rmult"]), 0.0)
        dra = (dla * (-LRU_C * p["sp"])) * (r * (1.0 - r))
        dia = di * (ig * (1.0 - ig))
        drab = dra.astype(BF16)
        diab = dia.astype(BF16)
        dxc = dxc + _bd_t(drab, wa_ref) + _bd_t(diab, wx_ref)
        dwa_ref[...] += _bd_grad(p["xcb"], drab)
        dwx_ref[...] += _bd_grad(p["xcb"], diab)
        sig_neg_lam = _sigmoid(-pv[ROW_LAM:ROW_LAM + 1, :])
        d_lam = jnp.sum(dla * r, axis=0, keepdims=True) * (LRU_C * sig_neg_lam)

        f_x[pl.ds(0, tm), :] = dxc
        f_x[pl.ds(tm, 8), :] = cx[...]
        du_lru = jnp.zeros((tm, LRU_W), F32)
        d_cw = []
        for k in range(4):
            du_lru = du_lru + f_x[pl.ds(3 - k, tm), :] * pv[ROW_CW + k:ROW_CW + k + 1, :]
            d_cw.append(jnp.sum(dxc * p["taps"][k], axis=0, keepdims=True))
        du_ref[:, 0:LRU_W] = du_lru.astype(BF16)
        cx[...] = f_x[pl.ds(0, 8), :]

        zp = p["zp"]
        ps = pv[ROW_PS:ROW_PS + 1, :]
        y_pool = zp * ps
        rstd_p = _rstd(y_pool)
        dy_pool, d_gain_p = _rms_bwd(dyn[:, LRU_W:D_MODEL], y_pool * rstd_p, rstd_p, pv[ROW_GP:ROW_GP + 1, :])
        dz = dy_pool * ps
        dzb = dz.astype(BF16)
        dwp_ref[...] += _bd_grad(p["pooled_b"], dzb)
        dpooled = _bd_t(dzb, wp_ref)
        for g, w in enumerate(POOL_WINDOWS):
            f_p[pl.ds(0, tm), pl.ds(128 * g, 128)] = dpooled[:, 128 * g:128 * (g + 1)] * p["inv_cnts"][g]
        f_p[pl.ds(tm, HALO), :] = cp[...]
        for g, w in enumerate(POOL_WINDOWS):
            acc = _window_sum(f_p[:, pl.ds(128 * g, 128)], w, back=False)[0:tm, :]
            du_ref[:, 2 * LRU_W + 128 * g:2 * LRU_W + 128 * (g + 1)] = (
                acc - dpooled[:, 128 * g:128 * (g + 1)]).astype(BF16)
        cp[...] = f_p[pl.ds(0, HALO), :]

        rows = d_cw + [
            jnp.sum(dxc, axis=0, keepdims=True),
            jnp.sum(dra, axis=0, keepdims=True),
            jnp.sum(dia, axis=0, keepdims=True),
            d_lam,
            jnp.sum(dz, axis=0, keepdims=True),
            jnp.sum(dy_pool * zp, axis=0, keepdims=True),
            d_gain_l, d_gain_p,
            jnp.zeros((4, LRU_W), F32),
        ]
        vacc_ref[...] += jnp.concatenate(rows, axis=0)

        @pl.when(s == n_t - 1)
        def _():
            sg_ref[SG_VEC:SG_VEC + 16, :] = vacc_ref[:, 0:256]
            sg_ref[SG_VEC + 16:SG_VEC + 32, :] = vacc_ref[:, 256:512]
            for half in range(2):
                sg_ref[SG_WA + 64 * half:SG_WA + 64 * (half + 1), :] = _diag_pack(dwa_ref[half], 64)
                sg_ref[SG_WX + 64 * half:SG_WX + 64 * (half + 1), :] = _diag_pack(dwx_ref[half], 64)
                sg_ref[SG_WP + 128 * half:SG_WP + 128 * (half + 1), :] = _diag_pack(dwp_ref[half], 128)
            if exchange:
                exchange.finish()

    rev = lambda w: pl.BlockSpec((tm, w), lambda s: (n_t - 1 - s, 0))
    full = lambda shape: pl.BlockSpec(shape, lambda s: (0,) * len(shape))
    outs = pl.pallas_call(
        body, name="mixer_bwd", grid=(n_t,),
        in_specs=[rev(D_MODEL), rev(D_IN),
                  pl.BlockSpec((HALO, D_IN), lambda s: (jnp.maximum((n_t - 1 - s) * (tm // HALO) - 1, 0), 0)),
                  rev(LRU_W),
                  pl.BlockSpec((8, LRU_W), lambda s: (jnp.maximum((n_t - 1 - s) * (tm // 8) - 1, 0), 0)),
                  full((16, LRU_W)), full((8, 64, 64)), full((8, 64, 64)), full((4, 128, 128)),
                  full((D_MODEL, D_MODEL))] + [ANY] * n_x,
        out_specs=[rev(D_IN), full((SG_ROWS, 256))] + [ANY] * n_x,
        out_shape=[_sds((T, D_IN), BF16), _sds((SG_ROWS, 256), F32)] + [_sds(a.shape, a.dtype) for a in chip_sums],
        scratch_shapes=[pltpu.VMEM((HALO + tm, LRU_W), F32), pltpu.VMEM((HALO + tm, POOL_W), F32),
                        pltpu.VMEM((8 + tm, LRU_W), F32), pltpu.VMEM((tm, LRU_W), F32), pltpu.VMEM((tm, LRU_W), F32),
                        pltpu.VMEM((tm + 8, LRU_W), F32), pltpu.VMEM((tm + 8, LRU_W), F32),
                        pltpu.VMEM((tm + HALO, POOL_W), F32), pltpu.VMEM((8, LRU_W), F32),
                        pltpu.VMEM((8, LRU_W), F32), pltpu.VMEM((HALO, POOL_W), F32)]
        + [pltpu.VMEM((2, 256, 256), BF16)] * 3 + [pltpu.VMEM((16, LRU_W), F32)] + [pltpu.VMEM((2, 256, 256), F32)] * 3
        + (_ChipExchange.scratch(n_x) if n_x else []),
        compiler_params=pltpu.CompilerParams(dimension_semantics=("arbitrary",)),
    )(d2, u, u, hs, hs, pv, wa, wx, wp, w_out_b, *chip_sums)
    return outs[0], outs[1], list(outs[2:])


def _mix_in_bwd(du, x, d2, w_in_t, g_mix, tm, chip_sums=()):
    T = x.shape[0]
    n_t = T // tm
    n_x = len(chip_sums)

    def body(du_ref, x_ref, d2_ref, w_ref, g_ref, *rest):
        x_in, rest = rest[:n_x], rest[n_x:]
        dx_ref, dg_ref = rest[:2]
        exchange = _ChipExchange(x_in, rest[2:2 + n_x], *rest[2 + n_x:]) if n_x else None
        i = pl.program_id(0)

        @pl.when(i == 0)
        def _():
            dg_ref[...] = jnp.zeros((1, D_MODEL), F32)
            if exchange:
                exchange.start()

        dh = jnp.dot(du_ref[...], w_ref[...], preferred_element_type=F32)
        xv = x_ref[...]
        rstd = _rstd(xv)
        dx, dgain = _rms_bwd(dh, xv * rstd, rstd, g_ref[...])
        dx_ref[...] = d2_ref[...] + dx
        dg_ref[...] += dgain

        if exchange:
            @pl.when(i == n_t - 1)
            def _():
                exchange.finish()

    row = lambda w: pl.BlockSpec((tm, w), lambda i: (i, 0))
    const = lambda shape: pl.BlockSpec(shape, lambda i: (0,) * len(shape))
    outs = pl.pallas_call(
        body, name="mix_in_bwd", grid=(n_t,),
        in_specs=[row(D_IN), row(D_MODEL), row(D_MODEL), const((D_IN, D_MODEL)), const((1, D_MODEL))] + [ANY] * n_x,
        out_specs=[row(D_MODEL), const((1, D_MODEL))] + [ANY] * n_x,
        out_shape=[_sds((T, D_MODEL), F32), _sds((1, D_MODEL), F32)] + [_sds(a.shape, a.dtype) for a in chip_sums],
        scratch_shapes=_ChipExchange.scratch(n_x) if n_x else [],
        compiler_params=pltpu.CompilerParams(dimension_semantics=("arbitrary",)),
    )(du, x, d2, w_in_t, g_mix, *chip_sums)
    return outs[0], outs[1], list(outs[2:])


def _pair_sum(g, r1, c_arr, name):
    _, _, R, C = g.shape
    tr = R if R <= 512 else 256

    def body(c_ref, g_ref, r_ref, o_ref):
        o_ref[...] = (g_ref[...] + r_ref[...]).astype(BF16)

    return pl.pallas_call(
        body, name=name,
        grid_spec=pltpu.PrefetchScalarGridSpec(
            num_scalar_prefetch=1, grid=(4, R // tr),
            in_specs=[pl.BlockSpec((None, None, tr, C), lambda j, i, c_ref: (j, c_ref[0], i, 0)),
                      pl.BlockSpec((None, tr, C), lambda j, i, c_ref: (j, i, 0))],
            out_specs=pl.BlockSpec((None, tr, C), lambda j, i, c_ref: (j, i, 0))),
        out_shape=_sds((4, R, C), BF16),
    )(c_arr, g, r1)


def _adamw(w, g, m, v):
    m = ADAM_B1 * m + (1.0 - ADAM_B1) * g
    v = ADAM_B2 * v + (1.0 - ADAM_B2) * (g * g)
    m_hat = m / (1.0 - ADAM_B1 ** ADAM_STEP)
    v_hat = v / (1.0 - ADAM_B2 ** ADAM_STEP)
    delta = -ADAM_LR * (m_hat / (jnp.sqrt(v_hat) + ADAM_EPS) + ADAM_WD * w)
    return delta, m, v


def _adam_shard(w, m, v, parts, name):
    R, C = w.shape
    tr = R if R <= 512 else 256

    def body(w_ref, m_ref, v_ref, p_ref, g_ref, d_ref, nm_ref, nv_ref):
        g = p_ref[0].astype(F32)
        for j in range(1, 4):
            g = g + p_ref[j].astype(F32)
        delta, nm, nv = _adamw(w_ref[...], g, m_ref[...], v_ref[...])
        g_ref[...] = g
        d_ref[...] = delta
        nm_ref[...] = nm
        nv_ref[...] = nv

    blk = pl.BlockSpec((tr, C), lambda i: (i, 0))
    return pl.pallas_call(
        body, name=name, grid=(R // tr,),
        in_specs=[blk, blk, blk, pl.BlockSpec((4, tr, C), lambda i: (0, i, 0))],
        out_specs=[blk] * 4, out_shape=[_sds((R, C), F32)] * 4,
        compiler_params=pltpu.CompilerParams(dimension_semantics=("parallel",)),
    )(w, m, v, parts)


SMALL_PARAMS = [("norm_mix_g", (1, D_MODEL)), ("conv_w", (1, 4, 64)), ("conv_b", (1, LRU_W)),
                ("gate_a_w", (1, 8, 64, 64)), ("gate_a_b", (1, LRU_W)), ("gate_x_w", (1, 8, 64, 64)),
                ("gate_x_b", (1, LRU_W)), ("lru_lambda", (1, LRU_W)), ("pool_w", (1, 4, 128, 128)),
                ("pool_b", (1, POOL_W)), ("pool_scale", (1, POOL_W)), ("norm_lru_g", (1, LRU_W)),
                ("norm_pool_g", (1, POOL_W)), ("norm_ffn_g", (1, D_MODEL)), ("final_norm_g", (1, D_MODEL))]
VEC_ROW = dict(conv_b=ROW_CB, gate_a_b=ROW_BA, gate_x_b=ROW_BX, lru_lambda=ROW_LAM, pool_b=ROW_PB, pool_scale=ROW_PS,
               norm_lru_g=ROW_GL, norm_pool_g=ROW_GP)
WHOLE = (Ellipsis,)


def _unpack_mixer_grads(sg, dev):
    vec = jnp.concatenate([sg[SG_VEC:SG_VEC + 16], sg[SG_VEC + 16:SG_VEC + 32]], axis=1)
    out = {nm: [(WHOLE, vec[r:r + 1])] for nm, r in VEC_ROW.items()}
    own = jnp.zeros((4, 64), F32)
    for d in range(N_DEV):
        own = jnp.where(dev == d, vec[ROW_CW:ROW_CW + 4, 64 * d:64 * (d + 1)], own)
    out["conv_w"] = [((0,), own)]
    for nm, row0 in (("gate_a_w", SG_WA), ("gate_x_w", SG_WX)):
        out[nm] = [((0, b), sg[row0 + 64 * (b // 4):row0 + 64 * (b // 4 + 1), 64 * (b % 4):64 * (b % 4 + 1)])
                   for b in range(8)]
    out["pool_w"] = [((0, b), sg[SG_WP + 128 * (b // 2):SG_WP + 128 * (b // 2 + 1), 128 * (b % 2):128 * (b % 2 + 1)])
                     for b in range(4)]
    return out


def _adam_small(parts, w, m, v):
    names = [nm for nm, _ in SMALL_PARAMS]
    n = len(names)

    def body(sg_ref, gm_ref, gf_ref, gn_ref, ls_ref, *rest):
        w_refs, m_refs, v_refs, outs = rest[:n], rest[n:2 * n], rest[2 * n:3 * n], rest[3 * n:]
        dev = 4 * lax.axis_index("x") + 2 * lax.axis_index("y") + lax.axis_index("c")

        def total(ref):
            acc = ref[0]
            for d in range(1, N_DEV):
                acc = acc + ref[d]
            return acc

        pieces = _unpack_mixer_grads(total(sg_ref), dev)
        pieces["norm_mix_g"] = [(WHOLE, total(gm_ref))]
        pieces["norm_ffn_g"] = [(WHOLE, total(gf_ref))]
        pieces["final_norm_g"] = [(WHOLE, total(gn_ref))]
        for i, nm in enumerate(names):
            for idx, g in pieces[nm]:
                delta, new_m, new_v = _adamw(w_refs[i][idx], g, m_refs[i][idx], v_refs[i][idx])
                for kind, val in enumerate((g, delta, new_m, new_v)):
                    outs[4 * i + kind][idx] = val
        outs[4 * n][...] = total(ls_ref)

    shapes = [_sds(shape, F32) for _, shape in SMALL_PARAMS for _ in range(4)] + [_sds((8, 128), F32)]
    res = pl.pallas_call(body, name="adam_small", out_shape=shapes)(
        *parts, *[w[nm] for nm in names], *[m[nm] for nm in names], *[v[nm] for nm in names])
    return {nm: tuple(res[4 * i:4 * i + 4]) for i, nm in enumerate(names)}, res[4 * n][0, 0]


def _vec_rows(conv_w_full, conv_b, ba, bx, lam, pb, ps, gl, gp):
    return jnp.concatenate([conv_w_full, conv_b, ba, bx, lam, pb, ps, gl, gp, jnp.zeros((4, LRU_W), F32)], axis=0)


WEIGHT_ORDER = ['norm_mix_g', 'w_in', 'conv_w', 'conv_b', 'gate_a_w', 'gate_a_b', 'gate_x_w', 'gate_x_b', 'lru_lambda',
                'pool_w', 'pool_b', 'pool_scale', 'norm_lru_g', 'norm_pool_g', 'w_out', 'norm_ffn_g', 'ffn_w1', 'ffn_w3',
                'ffn_w2', 'final_norm_g']


def kernel(x, norm_mix_g, w_in, conv_w, conv_b, gate_a_w, gate_a_b, gate_x_w, gate_x_b, lru_lambda, pool_w, pool_b, pool_scale, norm_lru_g, norm_pool_g, w_out, norm_ffn_g, ffn_w1, ffn_w3, ffn_w2, final_norm_g, loss_target, m_norm_mix_g, m_w_in, m_conv_w, m_conv_b, m_gate_a_w, m_gate_a_b, m_gate_x_w, m_gate_x_b, m_lru_lambda, m_pool_w, m_pool_b, m_pool_scale, m_norm_lru_g, m_norm_pool_g, m_w_out, m_norm_ffn_g, m_ffn_w1, m_ffn_w3, m_ffn_w2, m_final_norm_g, v_norm_mix_g, v_w_in, v_conv_w, v_conv_b, v_gate_a_w, v_gate_a_b, v_gate_x_w, v_gate_x_b, v_lru_lambda, v_pool_w, v_pool_b, v_pool_scale, v_norm_lru_g, v_norm_pool_g, v_w_out, v_norm_ffn_g, v_ffn_w1, v_ffn_w3, v_ffn_w2, v_final_norm_g):
    ac = lax.axis_index("c")
    tm, tmx, tn, tk = 512, 512, 1408, 1024
    xs, tgt = x[0], loss_target[0]
    g_fin = final_norm_g.reshape(1, D_MODEL)
    c_arr = jnp.reshape(ac, (1,)).astype(jnp.int32)

    def pair_sums(blocks, names):
        from_sibling = _pair_exchange(blocks, "grads_to_sibling_" + names[0])
        return [_pair_sum(g.reshape((4, 2) + g.shape[1:]), r, c_arr, "pair_sum_" + nm)
                for g, r, nm in zip(blocks, from_sibling, names)]

    tr = lambda w: jnp.swapaxes(w[0], 0, 1)
    own = lambda w: w[0]
    bf = lambda a: a.astype(BF16)

    g_in, g_conv = _all_gather([bf(tr(w_in)), conv_w[0]], "gather_w_in")
    w_in_t = g_in.reshape(D_IN, D_MODEL)
    conv_w_full = g_conv.transpose(1, 0, 2).reshape(4, LRU_W)
    pv = _vec_rows(conv_w_full, conv_b, gate_a_b, gate_x_b, lru_lambda, pool_b, pool_scale, norm_lru_g, norm_pool_g)
    wa, wx, wp = gate_a_w[0], gate_x_w[0], pool_w[0]

    u, h1, (g_out, g_w1) = _mix_in(xs, norm_mix_g, w_in_t, tm, shards=[bf(own(w_out)), bf(tr(ffn_w1))])
    w_out_b = g_out.reshape(D_MODEL, D_MODEL)
    y, hs, hres, h2, (g_w3, g_w2) = _mixer_fwd(u, xs, pv, wa, wx, wp, w_out_b, norm_ffn_g, tmx,
                                               shards=[bf(tr(ffn_w3)), bf(own(ffn_w2))])
    w1_t, w3_t, w2_b = g_w1.reshape(D_FF, D_MODEL), g_w3.reshape(D_FF, D_MODEL), g_w2.reshape(D_FF, D_MODEL)
    g, v, d3, loss_acc, d_gfin = _ffn_fwd(hres, h2, w1_t, w3_t, w2_b, g_fin, tgt, tm, tn)

    dg, dv, ff, d2, d_gffn = _ffn_bwd(d3, g, v, w1_t, w3_t, w2_b, hres, norm_ffn_g, tm, tn)
    blocks = lambda a: a.reshape(N_DEV, a.shape[0] // N_DEV, a.shape[1])
    chips = lambda a: a.reshape(4, a.shape[0] // 4, a.shape[1])
    early_sums = [chips(_at_b_pair(y, d2, c_arr, "grad_w_out", tk)), chips(_at_b_pair(dg, h2, c_arr, "grad_w1", tk)),
                  chips(_at_b_pair(dv, h2, c_arr, "grad_w3", tk)), chips(_at_b_pair(ff, d3, c_arr, "grad_w2", tk))]
    du, d_mixer, early_parts = _mixer_bwd(d2, u, hs, pv, wa, wx, wp, w_out_b, tmx, chip_sums=early_sums)
    grad_x, d_gmix, _ = _mix_in_bwd(du, xs, d2, w_in_t, norm_mix_g, tm)
    d_win, small_parts = _at_b(du, h1, "grad_w_in", D_IN, D_MODEL, tk,
                               gather=[d_mixer, d_gmix, d_gffn, d_gfin, loss_acc])
    win_parts = _chip_exchange(pair_sums([blocks(d_win)], ["w_in"]), "grads_to_chips_w_in")
    parts = list(win_parts) + list(early_parts)

    res = {}
    shard_w = dict(w_in=(w_in, m_w_in, v_w_in, tr), w_out=(w_out, m_w_out, v_w_out, own),
                   ffn_w1=(ffn_w1, m_ffn_w1, v_ffn_w1, tr), ffn_w3=(ffn_w3, m_ffn_w3, v_ffn_w3, tr),
                   ffn_w2=(ffn_w2, m_ffn_w2, v_ffn_w2, own))
    for (nm, (w, m, v, view)), p in zip(shard_w.items(), parts):
        outs = _adam_shard(view(w), view(m), view(v), p, "adam_" + nm)
        res[nm] = [(jnp.swapaxes(o, 0, 1) if view is tr else o)[None] for o in outs]

    row = lambda a: a.reshape(1, D_MODEL)
    small = lambda gm, cw, cb, wa_, ba, wx_, bx, lam, pw, pb, ps, gl, gp, gf, gn: dict(
        norm_mix_g=gm, conv_w=cw, conv_b=cb, gate_a_w=wa_, gate_a_b=ba, gate_x_w=wx_, gate_x_b=bx, lru_lambda=lam,
        pool_w=pw, pool_b=pb, pool_scale=ps, norm_lru_g=gl, norm_pool_g=gp, norm_ffn_g=gf, final_norm_g=row(gn))
    small_res, loss = _adam_small(
        small_parts,
        small(norm_mix_g, conv_w, conv_b, gate_a_w, gate_a_b, gate_x_w, gate_x_b, lru_lambda, pool_w, pool_b,
              pool_scale, norm_lru_g, norm_pool_g, norm_ffn_g, final_norm_g),
        small(m_norm_mix_g, m_conv_w, m_conv_b, m_gate_a_w, m_gate_a_b, m_gate_x_w, m_gate_x_b, m_lru_lambda, m_pool_w,
              m_pool_b, m_pool_scale, m_norm_lru_g, m_norm_pool_g, m_norm_ffn_g, m_final_norm_g),
        small(v_norm_mix_g, v_conv_w, v_conv_b, v_gate_a_w, v_gate_a_b, v_gate_x_w, v_gate_x_b, v_lru_lambda, v_pool_w,
              v_pool_b, v_pool_scale, v_norm_lru_g, v_norm_pool_g, v_norm_ffn_g, v_final_norm_g))
    for nm, outs in small_res.items():
        res[nm] = [o.reshape(D_MODEL) for o in outs] if nm == "final_norm_g" else list(outs)

    out = [loss, grad_x[None]]
    for kind in range(4):
        out += [res[nm][kind] for nm in WEIGHT_ORDER]
    return tuple(out)
```

```python
import jax
import jax.numpy as jnp
from jax import lax
from jax.experimental import pallas as pl
from jax.experimental.pallas import tpu as pltpu

F32 = jnp.float32
BF16 = jnp.bfloat16

D_MODEL = 1024
LRU_W = 512
POOL_W = 512
D_IN = 1536
D_FF = 2816
POOL_WINDOWS = (2, 4, 8, 16)
EPS = 1e-6
LRU_C = 8.0
N_DEV = 8
HALO = 16
SCAN_UNROLL = 4

ADAM_LR = 0.001
ADAM_B1 = 0.9
ADAM_B2 = 0.999
ADAM_EPS = 1e-08
ADAM_WD = 0.01
ADAM_STEP = 10

ROW_CW, ROW_CB, ROW_BA, ROW_BX, ROW_LAM, ROW_PB, ROW_PS, ROW_GL, ROW_GP = 0, 4, 5, 6, 7, 8, 9, 10, 11
SG_VEC, SG_WA, SG_WX, SG_WP, SG_ROWS = 0, 32, 160, 288, 544

NT = (((1,), (1,)), ((), ()))
TN = (((0,), (0,)), ((), ()))


def _sds(shape, dtype):
    return jax.ShapeDtypeStruct(shape, dtype)


def _sigmoid(x):
    return 0.5 * jnp.tanh(0.5 * x) + 0.5


def _gelu_parts(x):
    c = 0.7978845608028654
    inner = c * (x + 0.044715 * (x * x * x))
    th = jnp.tanh(inner)
    g = 0.5 * x * (1.0 + th)
    dg = 0.5 * (1.0 + th) + 0.5 * x * (1.0 - th * th) * (c * (1.0 + 3.0 * 0.044715 * (x * x)))
    return g, dg


def _window_sum(ext, w, back):
    n = ext.shape[0]
    s, k = ext, 1
    while k < w:
        s = s + pltpu.roll(s, k if back else n - k, 0)
        k *= 2
    return s


def _rstd(x):
    return lax.rsqrt(jnp.mean(x * x, axis=-1, keepdims=True) + EPS)


def _rms_bwd(dy, xhat, rstd, gain):
    dxh = dy * gain
    dx = rstd * (dxh - xhat * jnp.mean(dxh * xhat, axis=-1, keepdims=True))
    return dx, jnp.sum(dy * xhat, axis=0, keepdims=True)


def _bd(xb, w_ref):
    return jnp.concatenate(
        [jnp.dot(xb[:, :256], w_ref[0], preferred_element_type=F32),
         jnp.dot(xb[:, 256:], w_ref[1], preferred_element_type=F32)], axis=1)


def _bd_t(xb, w_ref):
    return jnp.concatenate(
        [lax.dot_general(xb[:, :256], w_ref[0], NT, preferred_element_type=F32),
         lax.dot_general(xb[:, 256:], w_ref[1], NT, preferred_element_type=F32)], axis=1)


def _bd_grad(xb, db):
    return jnp.stack(
        [lax.dot_general(xb[:, :256], db[:, :256], TN, preferred_element_type=F32),
         lax.dot_general(xb[:, 256:], db[:, 256:], TN, preferred_element_type=F32)], axis=0)


def _fill_block_diag(dst, src_ref):
    n, k, _ = src_ref.shape
    dst[...] = jnp.zeros(dst.shape, BF16)
    for b in range(n):
        p, q = divmod(b, 256 // k)
        dst[p, q * k:(q + 1) * k, q * k:(q + 1) * k] = src_ref[b].astype(BF16)


def _diag_pack(w, k):
    lane = lax.broadcasted_iota(jnp.int32, (k, 256), 1)
    out = w[0:k]
    for q in range(1, 256 // k):
        out = jnp.where(lane >= q * k, w[q * k:(q + 1) * k], out)
    return out


def _y_pos(b):
    return 4 * (b % 2) + b // 2


def _mixer_pre(e_lru, e_pool, pv, wa_ref, wx_ref, wp_ref, tm, t0):
    taps = [e_lru[pl.ds(HALO - 3 + k, tm), :] for k in range(4)]
    xc = pv[ROW_CB:ROW_CB + 1, :]
    for k in range(4):
        xc = xc + taps[k] * pv[ROW_CW + k:ROW_CW + k + 1, :]
    xcb = xc.astype(BF16)
    r = _sigmoid(_bd(xcb, wa_ref) + pv[ROW_BA:ROW_BA + 1, :])
    ig = _sigmoid(_bd(xcb, wx_ref) + pv[ROW_BX:ROW_BX + 1, :])
    z = -pv[ROW_LAM:ROW_LAM + 1, :]
    sp = jnp.maximum(z, 0.0) + jnp.log(1.0 + jnp.exp(-jnp.abs(z)))
    la = (-LRU_C * r) * sp
    a = jnp.exp(la)
    om = -jnp.tanh(la) * (1.0 + a * a)
    omc = jnp.maximum(om, 1e-12)
    rmult = lax.rsqrt(omc)
    mult = omc * rmult
    t = t0 + lax.broadcasted_iota(jnp.int32, (tm, 1), 0)
    parts, inv_cnts = [], []
    for g, w in enumerate(POOL_WINDOWS):
        ext = e_pool[:, pl.ds(128 * g, 128)]
        s = _window_sum(ext, w, back=True)[HALO:, :]
        inv_cnt = 1.0 / jnp.minimum(t + 1, w).astype(F32)
        inv_cnts.append(inv_cnt)
        parts.append(s * inv_cnt - ext[HALO:, :])
    pooled = jnp.concatenate(parts, axis=1)
    pooled_b = pooled.astype(BF16)
    zp = _bd(pooled_b, wp_ref) + pv[ROW_PB:ROW_PB + 1, :]
    return dict(taps=taps, xc=xc, xcb=xcb, r=r, ig=ig, sp=sp, a=a, om=om, mult=mult, rmult=rmult,
                pooled_b=pooled_b, zp=zp, inv_cnts=inv_cnts)


def _scan_tile(a_ref, b_ref, out_ref, carry, tm, reverse):
    row = lax.broadcasted_iota(jnp.int32, (8, LRU_W), 0)
    nblk = tm // 8

    def local_scan(blk):
        r0 = pl.multiple_of(blk * 8, 8)
        av = a_ref[pl.ds(r0, 8), :]
        bv = b_ref[pl.ds(r0, 8), :]
        for d in (1, 2, 4):
            sh = (8 - d) if reverse else d
            a_s = pltpu.roll(av, sh, 0)
            b_s = pltpu.roll(bv, sh, 0)
            m = (row < 8 - d) if reverse else (row >= d)
            bv = jnp.where(m, av * b_s + bv, bv)
            av = jnp.where(m, av * a_s, av)
        return r0, av, bv

    def step(i, hin):
        local = [local_scan((nblk - 1 - (i * SCAN_UNROLL + j)) if reverse else (i * SCAN_UNROLL + j))
                 for j in range(SCAN_UNROLL)]
        for r0, av, bv in local:
            hv = av * hin + bv
            out_ref[pl.ds(r0, 8), :] = hv
            hin = jnp.broadcast_to(hv[0:1, :] if reverse else hv[7:8, :], (8, LRU_W))
        return hin

    return lax.fori_loop(0, nblk // SCAN_UNROLL, step, carry)


MESH = pl.DeviceIdType.MESH
ANY = pl.BlockSpec(memory_space=pl.ANY)


def _place():
    x, y, c = lax.axis_index("x"), lax.axis_index("y"), lax.axis_index("c")
    chips = [(1 - x, y), (x, 1 - y), (1 - x, 1 - y)]
    return x, y, c, chips


class _Gather:
    def __init__(self, ins, outs, send_sems, recv_sems, local_sems, core_major=False):
        self.ins, self.outs, self.n = ins, outs, len(ins)
        self.send_sems, self.recv_sems, self.local_sems = send_sems, recv_sems, local_sems
        self.core_major = core_major

    @staticmethod
    def scratch(n):
        return [pltpu.SemaphoreType.DMA((7, n)), pltpu.SemaphoreType.DMA((7, n)), pltpu.SemaphoreType.DMA((n,))]

    def _slot(self, a, px, py, pc):
        return self.outs[a].at[4 * pc + 2 * px + py if self.core_major else 4 * px + 2 * py + pc]

    def _copy(self, a, k, block, to, src=None):
        return pltpu.make_async_remote_copy(
            src_ref=self._slot(a, *block) if src is None else src, dst_ref=self._slot(a, *block),
            send_sem=self.send_sems.at[k, a], recv_sem=self.recv_sems.at[k, a], device_id=to, device_id_type=MESH)

    def _mine(self, a):
        x, y, c, _ = _place()
        return pltpu.make_async_copy(self.ins[a], self._slot(a, x, y, c), self.local_sems.at[a])

    def _first(self, a):
        x, y, c, chips = _place()
        me = (x, y, c)
        return ([self._copy(a, 0, me, (x, y, 1 - c), src=self.ins[a])]
                + [self._copy(a, 1 + j, me, (*chip, c), src=self.ins[a]) for j, chip in enumerate(chips)])

    def start(self):
        for a in range(self.n):
            self._mine(a).start()
        for a in range(self.n):
            for cp in self._first(a):
                cp.start()

    def finish(self):
        x, y, c, chips = _place()
        me, sibling = (x, y, c), (x, y, 1 - c)
        passed = []
        for j, chip in enumerate(chips):
            for a in range(self.n):
                self._copy(a, 1 + j, (*chip, c), me).wait_recv()
                fwd = self._copy(a, 4 + j, (*chip, c), sibling)
                fwd.start()
                passed.append(fwd)
        for a in range(self.n):
            self._copy(a, 0, (x, y, 1 - c), me).wait_recv()
            for j, chip in enumerate(chips):
                self._copy(a, 4 + j, (*chip, 1 - c), me).wait_recv()
        for a in range(self.n):
            for cp in self._first(a):
                cp.wait_send()
        for cp in passed:
            cp.wait_send()
        for a in range(self.n):
            self._mine(a).wait()


def _all_gather(arrs, name):
    n = len(arrs)

    def body(*refs):
        g = _Gather(refs[:n], refs[n:2 * n], *refs[2 * n:])
        g.start()
        g.finish()

    return pl.pallas_call(
        body, name=name,
        out_shape=[_sds((N_DEV,) + a.shape, a.dtype) for a in arrs],
        in_specs=[ANY] * n, out_specs=[ANY] * n, scratch_shapes=_Gather.scratch(n),
    )(*arrs)


def _pair_exchange(arrs, name):
    n = len(arrs)

    def body(*refs):
        ins, outs = refs[:n], refs[n:2 * n]
        send_sems, recv_sems = refs[2 * n:]
        x, y, c, _ = _place()
        sibling = (x, y, 1 - c)
        sends = []
        for a in range(n):
            for j in range(4):
                cp = pltpu.make_async_remote_copy(
                    src_ref=ins[a].at[2 * j + (1 - c)], dst_ref=outs[a].at[j],
                    send_sem=send_sems.at[j, a], recv_sem=recv_sems.at[j, a], device_id=sibling, device_id_type=MESH)
                cp.start()
                sends.append(cp)
        for cp in sends:
            cp.wait()

    return pl.pallas_call(
        body, name=name,
        out_shape=[_sds((4,) + a.shape[1:], a.dtype) for a in arrs],
        in_specs=[ANY] * n, out_specs=[ANY] * n,
        scratch_shapes=[pltpu.SemaphoreType.DMA((4, n)), pltpu.SemaphoreType.DMA((4, n))],
    )(*arrs)


class _ChipExchange:
    def __init__(self, ins, outs, send_sems, recv_sems, local_sems):
        self.ins, self.outs, self.n = ins, outs, len(ins)
        self.send_sems, self.recv_sems, self.local_sems = send_sems, recv_sems, local_sems

    @staticmethod
    def scratch(n):
        return [pltpu.SemaphoreType.DMA((3, n)), pltpu.SemaphoreType.DMA((3, n)), pltpu.SemaphoreType.DMA((n,))]

    def _local(self, a):
        x, y, _, _ = _place()
        me = 2 * x + y
        return pltpu.make_async_copy(self.ins[a].at[me], self.outs[a].at[me], self.local_sems.at[a])

    def _copies(self, a):
        x, y, c, chips = _place()
        me = 2 * x + y
        return [(pltpu.make_async_remote_copy(
                     src_ref=self.ins[a].at[2 * px + py], dst_ref=self.outs[a].at[me],
                     send_sem=self.send_sems.at[k, a], recv_sem=self.recv_sems.at[k, a],
                     device_id=(px, py, c), device_id_type=MESH),
                 pltpu.make_async_remote_copy(
                     src_ref=self.ins[a].at[me], dst_ref=self.outs[a].at[2 * px + py],
                     send_sem=self.send_sems.at[k, a], recv_sem=self.recv_sems.at[k, a],
                     device_id=(px, py, c), device_id_type=MESH))
                for k, (px, py) in enumerate(chips)]

    def start(self):
        for a in range(self.n):
            self._local(a).start()
        for a in range(self.n):
            for send, _ in self._copies(a):
                send.start()

    def finish(self):
        for a in range(self.n):
            for send, recv in self._copies(a):
                send.wait_send()
                recv.wait_recv()
        for a in range(self.n):
            self._local(a).wait()


def _chip_exchange(arrs, name):
    n = len(arrs)

    def body(*refs):
        e = _ChipExchange(refs[:n], refs[n:2 * n], *refs[2 * n:])
        e.start()
        e.finish()

    return pl.pallas_call(
        body, name=name, out_shape=[_sds(a.shape, a.dtype) for a in arrs],
        in_specs=[ANY] * n, out_specs=[ANY] * n, scratch_shapes=_ChipExchange.scratch(n),
    )(*arrs)


def _mix_in(x, g_mix, w_in_t, tm, shards=()):
    T = x.shape[0]
    n_t = T // tm
    n_s = len(shards)

    def body(x_ref, g_ref, w_ref, *rest):
        sh_in, rest = rest[:n_s], rest[n_s:]
        u_ref, h_ref = rest[:2]
        gather = _Gather(sh_in, rest[2:2 + n_s], *rest[2 + n_s:], core_major=True) if n_s else None
        i = pl.program_id(0)

        if gather:
            @pl.when(i == 0)
            def _():
                gather.start()

        xv = x_ref[...]
        h = (xv * _rstd(xv) * g_ref[...]).astype(BF16)
        h_ref[...] = h
        u_ref[...] = lax.dot_general(h, w_ref[...], NT, preferred_element_type=F32)

        if gather:
            @pl.when(i == n_t - 1)
            def _():
                gather.finish()

    outs = pl.pallas_call(
        body, name="mix_in", grid=(n_t,),
        in_specs=[pl.BlockSpec((tm, D_MODEL), lambda i: (i, 0)),
                  pl.BlockSpec((1, D_MODEL), lambda i: (0, 0)),
                  pl.BlockSpec((D_IN, D_MODEL), lambda i: (0, 0))] + [ANY] * n_s,
        out_specs=[pl.BlockSpec((tm, D_IN), lambda i: (i, 0)),
                   pl.BlockSpec((tm, D_MODEL), lambda i: (i, 0))] + [ANY] * n_s,
        out_shape=[_sds((T, D_IN), F32), _sds((T, D_MODEL), BF16)] + [_sds((N_DEV,) + a.shape, a.dtype) for a in shards],
        scratch_shapes=_Gather.scratch(n_s) if n_s else [],
        compiler_params=pltpu.CompilerParams(dimension_semantics=("arbitrary",)),
    )(x, g_mix, w_in_t, *shards)
    return outs[0], outs[1], list(outs[2:])


def _mixer_fwd(u, x, pv, wa, wx, wp, w_out_b, g_ffn, tm, shards=()):
    T = u.shape[0]
    n_s = len(shards)
    n_t = T // tm

    def body(u_ref, x_ref, pv_ref, wa_in, wx_in, wp_in, wo_ref, gf_ref, *rest):
        sh_in, rest = rest[:n_s], rest[n_s:]
        y_ref, hs_ref, hres_ref, h2_ref = rest[:4]
        sh_out, rest = rest[4:4 + n_s], rest[4 + n_s:]
        e_lru, e_pool, a_s, b_s, hc, wa_ref, wx_ref, wp_ref = rest[:8]
        gather = _Gather(sh_in, sh_out, *rest[8:], core_major=True) if n_s else None
        i = pl.program_id(0)

        @pl.when(i == 0)
        def _():
            if gather:
                gather.start()
            e_lru[pl.ds(0, HALO), :] = jnp.zeros((HALO, LRU_W), F32)
            e_pool[pl.ds(0, HALO), :] = jnp.zeros((HALO, POOL_W), F32)
            hc[...] = jnp.zeros((8, LRU_W), F32)
            _fill_block_diag(wa_ref, wa_in)
            _fill_block_diag(wx_ref, wx_in)
            _fill_block_diag(wp_ref, wp_in)

        e_lru[pl.ds(HALO, tm), :] = u_ref[:, 0:LRU_W]
        e_pool[pl.ds(HALO, tm), :] = u_ref[:, 2 * LRU_W:D_IN]
        pv = pv_ref[...]
        p = _mixer_pre(e_lru, e_pool, pv, wa_ref, wx_ref, wp_ref, tm, i * tm)
        a_s[...] = p["a"]
        b_s[...] = p["mult"] * (p["ig"] * p["xc"])
        hc[...] = _scan_tile(a_s, b_s, hs_ref, hc[...], tm, reverse=False)
        gl, _ = _gelu_parts(u_ref[:, LRU_W:2 * LRU_W])
        y_lru = hs_ref[...] * gl
        y_pool = p["zp"] * pv[ROW_PS:ROW_PS + 1, :]
        yn = jnp.concatenate([y_lru * _rstd(y_lru) * pv[ROW_GL:ROW_GL + 1, :],
                              y_pool * _rstd(y_pool) * pv[ROW_GP:ROW_GP + 1, :]], axis=1).astype(BF16)
        for b in range(N_DEV):
            y_ref[:, 128 * _y_pos(b):128 * (_y_pos(b) + 1)] = yn[:, 128 * b:128 * (b + 1)]
        hr = x_ref[...] + jnp.dot(y_ref[...], wo_ref[...], preferred_element_type=F32)
        hres_ref[...] = hr
        h2_ref[...] = (hr * _rstd(hr) * gf_ref[...]).astype(BF16)
        e_lru[pl.ds(0, HALO), :] = e_lru[pl.ds(tm, HALO), :]
        e_pool[pl.ds(0, HALO), :] = e_pool[pl.ds(tm, HALO), :]

        if gather:
            @pl.when(i == n_t - 1)
            def _():
                gather.finish()

    full = lambda shape: pl.BlockSpec(shape, lambda i: (0,) * len(shape))
    row = lambda w: pl.BlockSpec((tm, w), lambda i: (i, 0))
    outs = pl.pallas_call(
        body, name="mixer_fwd", grid=(n_t,),
        in_specs=[row(D_IN), row(D_MODEL), full((16, LRU_W)), full((8, 64, 64)), full((8, 64, 64)), full((4, 128, 128)),
                  full((D_MODEL, D_MODEL)), full((1, D_MODEL))] + [ANY] * n_s,
        out_specs=[row(D_MODEL), row(LRU_W), row(D_MODEL), row(D_MODEL)] + [ANY] * n_s,
        out_shape=[_sds((T, D_MODEL), BF16), _sds((T, LRU_W), F32), _sds((T, D_MODEL), F32), _sds((T, D_MODEL), BF16)]
        + [_sds((N_DEV,) + a.shape, a.dtype) for a in shards],
        scratch_shapes=[pltpu.VMEM((HALO + tm, LRU_W), F32), pltpu.VMEM((HALO + tm, POOL_W), F32),
                        pltpu.VMEM((tm, LRU_W), F32), pltpu.VMEM((tm, LRU_W), F32), pltpu.VMEM((8, LRU_W), F32)]
        + [pltpu.VMEM((2, 256, 256), BF16)] * 3 + (_Gather.scratch(n_s) if n_s else []),
        compiler_params=pltpu.CompilerParams(dimension_semantics=("arbitrary",)),
    )(u, x, pv, wa, wx, wp, w_out_b, g_ffn, *shards)
    return outs[0], outs[1], outs[2], outs[3], list(outs[4:])


def _ffn_fwd(hres, h2, w1_b, w3_b, w2_b, g_fin, tgt, tm, tn):
    T = hres.shape[0]
    n_j = D_FF // tn

    def body(hres_ref, h2_ref, w1_ref, w3_ref, w2_ref, gfin_ref, tgt_ref,
             g_ref, v_ref, d3_ref, loss_ref, dgfin_ref, acc):
        i, j = pl.program_id(0), pl.program_id(1)

        @pl.when(j == 0)
        def _():
            acc[...] = jnp.zeros((tm, D_MODEL), F32)

        @pl.when((j == 0) & (i == 0))
        def _():
            loss_ref[...] = jnp.zeros((8, 128), F32)
            dgfin_ref[...] = jnp.zeros((1, D_MODEL), F32)

        h2 = h2_ref[...]
        g = lax.dot_general(h2, w1_ref[...], NT, preferred_element_type=F32)
        v = lax.dot_general(h2, w3_ref[...], NT, preferred_element_type=F32)
        g_ref[...] = g.astype(BF16)
        v_ref[...] = v.astype(BF16)
        ff = ((g * _sigmoid(g)) * v).astype(BF16)
        acc[...] += jnp.dot(ff, w2_ref[...], preferred_element_type=F32)

        @pl.when(j == n_j - 1)
        def _():
            h3 = hres_ref[...] + acc[...]
            rstd = _rstd(h3)
            xh = h3 * rstd
            gfin = gfin_ref[...]
            err = xh * gfin - tgt_ref[...]
            loss_ref[...] += 0.5 * jnp.sum(jnp.mean(err * err, axis=-1, keepdims=True))
            dout = err * (1.0 / D_MODEL)
            dx, dgain = _rms_bwd(dout, xh, rstd, gfin)
            d3_ref[...] = dx
            dgfin_ref[...] += dgain

    row = lambda w: pl.BlockSpec((tm, w), lambda i, j: (i, 0))
    const = lambda shape: pl.BlockSpec(shape, lambda i, j: (0,) * len(shape))
    return pl.pallas_call(
        body, name="ffn_fwd", grid=(T // tm, n_j),
        in_specs=[row(D_MODEL), row(D_MODEL),
                  pl.BlockSpec((tn, D_MODEL), lambda i, j: (j, 0)), pl.BlockSpec((tn, D_MODEL), lambda i, j: (j, 0)),
                  pl.BlockSpec((tn, D_MODEL), lambda i, j: (j, 0)), const((1, D_MODEL)), row(D_MODEL)],
        out_specs=[pl.BlockSpec((tm, tn), lambda i, j: (i, j)), pl.BlockSpec((tm, tn), lambda i, j: (i, j)),
                   row(D_MODEL), const((8, 128)), const((1, D_MODEL))],
        out_shape=[_sds((T, D_FF), BF16), _sds((T, D_FF), BF16),
                   _sds((T, D_MODEL), F32), _sds((8, 128), F32), _sds((1, D_MODEL), F32)],
        scratch_shapes=[pltpu.VMEM((tm, D_MODEL), F32)],
        compiler_params=pltpu.CompilerParams(dimension_semantics=("arbitrary", "arbitrary")),
    )(hres, h2, w1_b, w3_b, w2_b, g_fin, tgt)


def _ffn_bwd(d3, g, v, w1_b, w3_b, w2_b, hres, g_ffn, tm, tn):
    T = d3.shape[0]
    n_j = D_FF // tn

    def body(d3_ref, g_ref, v_ref, w1_ref, w3_ref, w2_ref, hres_ref, gf_ref,
             dg_ref, dv_ref, ff_ref, d2_ref, dgffn_ref, acc):
        i, j = pl.program_id(0), pl.program_id(1)

        @pl.when(j == 0)
        def _():
            acc[...] = jnp.zeros((tm, D_MODEL), F32)

        @pl.when((j == 0) & (i == 0))
        def _():
            dgffn_ref[...] = jnp.zeros((1, D_MODEL), F32)

        dff = lax.dot_general(d3_ref[...].astype(BF16), w2_ref[...], NT, preferred_element_type=F32)
        gv = g_ref[...].astype(F32)
        vv = v_ref[...].astype(F32)
        sg = _sigmoid(gv)
        sl = gv * sg
        dgb = (dff * vv * (sg * (1.0 + gv * (1.0 - sg)))).astype(BF16)
        dvb = (dff * sl).astype(BF16)
        dg_ref[...] = dgb
        dv_ref[...] = dvb
        ff_ref[...] = (sl * vv).astype(BF16)
        acc[...] += (jnp.dot(dgb, w1_ref[...], preferred_element_type=F32)
                     + jnp.dot(dvb, w3_ref[...], preferred_element_type=F32))

        @pl.when(j == n_j - 1)
        def _():
            hr = hres_ref[...]
            rstd = _rstd(hr)
            dx, dgain = _rms_bwd(acc[...], hr * rstd, rstd, gf_ref[...])
            d2_ref[...] = d3_ref[...] + dx
            dgffn_ref[...] += dgain

    row = lambda w: pl.BlockSpec((tm, w), lambda i, j: (i, 0))
    tile = pl.BlockSpec((tm, tn), lambda i, j: (i, j))
    const = lambda shape: pl.BlockSpec(shape, lambda i, j: (0,) * len(shape))
    return pl.pallas_call(
        body, name="ffn_bwd", grid=(T // tm, n_j),
        in_specs=[row(D_MODEL), tile, tile,
                  pl.BlockSpec((tn, D_MODEL), lambda i, j: (j, 0)), pl.BlockSpec((tn, D_MODEL), lambda i, j: (j, 0)),
                  pl.BlockSpec((tn, D_MODEL), lambda i, j: (j, 0)), row(D_MODEL), const((1, D_MODEL))],
        out_specs=[tile, tile, tile, row(D_MODEL), const((1, D_MODEL))],
        out_shape=[_sds((T, D_FF), BF16), _sds((T, D_FF), BF16), _sds((T, D_FF), BF16),
                   _sds((T, D_MODEL), F32), _sds((1, D_MODEL), F32)],
        scratch_shapes=[pltpu.VMEM((tm, D_MODEL), F32)],
        compiler_params=pltpu.CompilerParams(dimension_semantics=("arbitrary", "arbitrary")),
    )(d3, g, v, w1_b, w3_b, w2_b, hres, g_ffn)


def _at_b(a, b, name, tmm, tn, tk, gather=()):
    T, M = a.shape
    N = b.shape[1]
    n_m, n_n, n_k = M // tmm, N // tn, T // tk
    n_g = len(gather)

    def body(a_ref, b_ref, *rest):
        g_in, o_ref, rest = rest[:n_g], rest[n_g], rest[n_g + 1:]
        ag = _Gather(g_in, rest[:n_g], *rest[n_g:]) if n_g else None
        m, n, k = pl.program_id(0), pl.program_id(1), pl.program_id(2)

        if ag:
            @pl.when((m == 0) & (n == 0) & (k == 0))
            def _():
                ag.start()

        @pl.when(k == 0)
        def _():
            o_ref[...] = jnp.zeros((tmm, tn), F32)

        o_ref[...] += lax.dot_general(a_ref[...].astype(BF16), b_ref[...].astype(BF16), TN,
                                      preferred_element_type=F32)

        if ag:
            @pl.when((m == n_m - 1) & (n == n_n - 1) & (k == n_k - 1))
            def _():
                ag.finish()

    outs = pl.pallas_call(
        body, name=name, grid=(n_m, n_n, n_k),
        in_specs=[pl.BlockSpec((tk, tmm), lambda m, n, k: (k, m)), pl.BlockSpec((tk, tn), lambda m, n, k: (k, n))]
        + [ANY] * n_g,
        out_specs=[pl.BlockSpec((tmm, tn), lambda m, n, k: (m, n))] + [ANY] * n_g,
        out_shape=[_sds((M, N), F32)] + [_sds((N_DEV,) + g.shape, g.dtype) for g in gather],
        scratch_shapes=_Gather.scratch(n_g) if n_g else [],
        compiler_params=pltpu.CompilerParams(
            dimension_semantics=("arbitrary",) * 3 if n_g else ("parallel", "parallel", "arbitrary")),
    )(a, b, *gather)
    return (outs[0], list(outs[1:])) if n_g else outs[0]


def _at_b_pair(a, b, c_arr, name, tk):
    T, M = a.shape
    N = b.shape[1]
    hm, n_k = M // 2, T // tk

    def body(c_ref, a_ref, b_ref, o_ref, acc, landed, send_sem, recv_sem):
        ph, k = pl.program_id(0), pl.program_id(1)
        def hand_over():
            x, y, c, _ = _place()
            return pltpu.make_async_remote_copy(
                src_ref=acc.at[0], dst_ref=landed, send_sem=send_sem, recv_sem=recv_sem,
                device_id=(x, y, 1 - c), device_id_type=MESH)

        @pl.when(k == 0)
        def _():
            acc[ph] = jnp.zeros((hm, N), F32)

        acc[ph] += lax.dot_general(a_ref[...].astype(BF16), b_ref[...].astype(BF16), TN, preferred_element_type=F32)

        @pl.when((ph == 0) & (k == n_k - 1))
        def _():
            hand_over().start()

        @pl.when((ph == 1) & (k == n_k - 1))
        def _():
            copy = hand_over()
            copy.wait_recv()
            o_ref[...] = (acc[1] + landed[...]).astype(BF16)
            copy.wait_send()

    return pl.pallas_call(
        body, name=name,
        grid_spec=pltpu.PrefetchScalarGridSpec(
            num_scalar_prefetch=1, grid=(2, n_k),
            in_specs=[pl.BlockSpec((tk, hm), lambda ph, k, c_ref: (k, (ph + 1 - c_ref[0]) % 2)),
                      pl.BlockSpec((tk, N), lambda ph, k, c_ref: (k, 0))],
            out_specs=pl.BlockSpec((hm, N), lambda ph, k, c_ref: (0, 0)),
            scratch_shapes=[pltpu.VMEM((2, hm, N), F32), pltpu.VMEM((hm, N), F32),
                            pltpu.SemaphoreType.DMA, pltpu.SemaphoreType.DMA]),
        out_shape=_sds((hm, N), BF16),
        compiler_params=pltpu.CompilerParams(dimension_semantics=("arbitrary", "arbitrary")),
    )(c_arr, a, b)


def _mixer_bwd(d2, u, hs, pv, wa, wx, wp, w_out_b, tm, chip_sums=()):
    T = u.shape[0]
    n_t = T // tm
    n_x = len(chip_sums)

    def body(d2_ref, u_ref, uh_ref, hs_ref, hh_ref, pv_ref, wa_in, wx_in, wp_in, wo_ref, *rest):
        x_in, rest = rest[:n_x], rest[n_x:]
        du_ref, sg_ref = rest[:2]
        x_out, rest = rest[2:2 + n_x], rest[2 + n_x:]
        e_lru, e_pool, e_h, a_s, b_s, mu_s, f_x, f_p, mc, cx, cp = rest[:11]
        wa_ref, wx_ref, wp_ref, vacc_ref, dwa_ref, dwx_ref, dwp_ref = rest[11:18]
        exchange = _ChipExchange(x_in, x_out, *rest[18:]) if n_x else None
        s = pl.program_id(0)
        it = n_t - 1 - s

        @pl.when(s == 0)
        def _():
            if exchange:
                exchange.start()
            mc[...] = jnp.zeros((8, LRU_W), F32)
            cx[...] = jnp.zeros((8, LRU_W), F32)
            cp[...] = jnp.zeros((HALO, POOL_W), F32)
            vacc_ref[...] = jnp.zeros((16, LRU_W), F32)
            dwa_ref[...] = jnp.zeros((2, 256, 256), F32)
            dwx_ref[...] = jnp.zeros((2, 256, 256), F32)
            dwp_ref[...] = jnp.zeros((2, 256, 256), F32)
            _fill_block_diag(wa_ref, wa_in)
            _fill_block_diag(wx_ref, wx_in)
            _fill_block_diag(wp_ref, wp_in)

        first = it == 0
        e_lru[pl.ds(0, HALO), :] = jnp.where(first, 0.0, uh_ref[:, 0:LRU_W])
        e_pool[pl.ds(0, HALO), :] = jnp.where(first, 0.0, uh_ref[:, 2 * LRU_W:D_IN])
        e_lru[pl.ds(HALO, tm), :] = u_ref[:, 0:LRU_W]
        e_pool[pl.ds(HALO, tm), :] = u_ref[:, 2 * LRU_W:D_IN]
        e_h[pl.ds(0, 8), :] = jnp.where(first, 0.0, hh_ref[...])
        e_h[pl.ds(8, tm), :] = hs_ref[...]
        pv = pv_ref[...]
        p = _mixer_pre(e_lru, e_pool, pv, wa_ref, wx_ref, wp_ref, tm, it * tm)
        a, xc, ig, r, mult = p["a"], p["xc"], p["ig"], p["r"], p["mult"]

        dyn = lax.dot_general(d2_ref[...].astype(BF16), wo_ref[...], NT, preferred_element_type=F32)
        dyn = jnp.concatenate([dyn[:, 128 * _y_pos(b):128 * (_y_pos(b) + 1)] for b in range(N_DEV)], axis=1)

        h = hs_ref[...]
        ug = u_ref[:, LRU_W:2 * LRU_W]
        gl, dgl = _gelu_parts(ug)
        y_lru = h * gl
        rstd_l = _rstd(y_lru)
        dy_lru, d_gain_l = _rms_bwd(dyn[:, 0:LRU_W], y_lru * rstd_l, rstd_l, pv[ROW_GL:ROW_GL + 1, :])
        dh = dy_lru * gl
        du_ref[:, LRU_W:2 * LRU_W] = (dy_lru * h * dgl).astype(BF16)
        a_s[...] = a
        b_s[...] = a * dh
        mu_s[pl.ds(tm, 8), :] = mc[...]
        mc[...] = _scan_tile(a_s, b_s, mu_s, mc[...], tm, reverse=True)
        lam_t = dh + mu_s[pl.ds(1, tm), :]
        da = lam_t * e_h[pl.ds(7, tm), :]
        dmult = lam_t * (ig * xc)
        di = lam_t * (mult * xc)
        dxc = lam_t * (mult * ig)
        dla = da * a - jnp.where(p["om"] > 1e-12, dmult * ((a * a) * p["rmult"]), 0.0)
        dra = (dla * (-LRU_C * p["sp"])) * (r * (1.0 - r))
        dia = di * (ig * (1.0 - ig))
        drab = dra.astype(BF16)
        diab = dia.astype(BF16)
        dxc = dxc + _bd_t(drab, wa_ref) + _bd_t(diab, wx_ref)
        dwa_ref[...] += _bd_grad(p["xcb"], drab)
        dwx_ref[...] += _bd_grad(p["xcb"], diab)
        sig_neg_lam = _sigmoid(-pv[ROW_LAM:ROW_LAM + 1, :])
        d_lam = jnp.sum(dla * r, axis=0, keepdims=True) * (LRU_C * sig_neg_lam)

        f_x[pl.ds(0, tm), :] = dxc
        f_x[pl.ds(tm, 8), :] = cx[...]
        du_lru = jnp.zeros((tm, LRU_W), F32)
        d_cw = []
        for k in range(4):
            du_lru = du_lru + f_x[pl.ds(3 - k, tm), :] * pv[ROW_CW + k:ROW_CW + k + 1, :]
            d_cw.append(jnp.sum(dxc * p["taps"][k], axis=0, keepdims=True))
        du_ref[:, 0:LRU_W] = du_lru.astype(BF16)
        cx[...] = f_x[pl.ds(0, 8), :]

        zp = p["zp"]
        ps = pv[ROW_PS:ROW_PS + 1, :]
        y_pool = zp * ps
        rstd_p = _rstd(y_pool)
        dy_pool, d_gain_p = _rms_bwd(dyn[:, LRU_W:D_MODEL], y_pool * rstd_p, rstd_p, pv[ROW_GP:ROW_GP + 1, :])
        dz = dy_pool * ps
        dzb = dz.astype(BF16)
        dwp_ref[...] += _bd_grad(p["pooled_b"], dzb)
        dpooled = _bd_t(dzb, wp_ref)
        for g, w in enumerate(POOL_WINDOWS):
            f_p[pl.ds(0, tm), pl.ds(128 * g, 128)] = dpooled[:, 128 * g:128 * (g + 1)] * p["inv_cnts"][g]
        f_p[pl.ds(tm, HALO), :] = cp[...]
        for g, w in enumerate(POOL_WINDOWS):
            acc = _window_sum(f_p[:, pl.ds(128 * g, 128)], w, back=False)[0:tm, :]
            du_ref[:, 2 * LRU_W + 128 * g:2 * LRU_W + 128 * (g + 1)] = (
                acc - dpooled[:, 128 * g:128 * (g + 1)]).astype(BF16)
        cp[...] = f_p[pl.ds(0, HALO), :]

        rows = d_cw + [
            jnp.sum(dxc, axis=0, keepdims=True),
            jnp.sum(dra, axis=0, keepdims=True),
            jnp.sum(dia, axis=0, keepdims=True),
            d_lam,
            jnp.sum(dz, axis=0, keepdims=True),
            jnp.sum(dy_pool * zp, axis=0, keepdims=True),
            d_gain_l, d_gain_p,
            jnp.zeros((4, LRU_W), F32),
        ]
        vacc_ref[...] += jnp.concatenate(rows, axis=0)

        @pl.when(s == n_t - 1)
        def _():
            sg_ref[SG_VEC:SG_VEC + 16, :] = vacc_ref[:, 0:256]
            sg_ref[SG_VEC + 16:SG_VEC + 32, :] = vacc_ref[:, 256:512]
            for half in range(2):
                sg_ref[SG_WA + 64 * half:SG_WA + 64 * (half + 1), :] = _diag_pack(dwa_ref[half], 64)
                sg_ref[SG_WX + 64 * half:SG_WX + 64 * (half + 1), :] = _diag_pack(dwx_ref[half], 64)
                sg_ref[SG_WP + 128 * half:SG_WP + 128 * (half + 1), :] = _diag_pack(dwp_ref[half], 128)
            if exchange:
                exchange.finish()

    rev = lambda w: pl.BlockSpec((tm, w), lambda s: (n_t - 1 - s, 0))
    full = lambda shape: pl.BlockSpec(shape, lambda s: (0,) * len(shape))
    outs = pl.pallas_call(
        body, name="mixer_bwd", grid=(n_t,),
        in_specs=[rev(D_MODEL), rev(D_IN),
                  pl.BlockSpec((HALO, D_IN), lambda s: (jnp.maximum((n_t - 1 - s) * (tm // HALO) - 1, 0), 0)),
                  rev(LRU_W),
                  pl.BlockSpec((8, LRU_W), lambda s: (jnp.maximum((n_t - 1 - s) * (tm // 8) - 1, 0), 0)),
                  full((16, LRU_W)), full((8, 64, 64)), full((8, 64, 64)), full((4, 128, 128)),
                  full((D_MODEL, D_MODEL))] + [ANY] * n_x,
        out_specs=[rev(D_IN), full((SG_ROWS, 256))] + [ANY] * n_x,
        out_shape=[_sds((T, D_IN), BF16), _sds((SG_ROWS, 256), F32)] + [_sds(a.shape, a.dtype) for a in chip_sums],
        scratch_shapes=[pltpu.VMEM((HALO + tm, LRU_W), F32), pltpu.VMEM((HALO + tm, POOL_W), F32),
                        pltpu.VMEM((8 + tm, LRU_W), F32), pltpu.VMEM((tm, LRU_W), F32), pltpu.VMEM((tm, LRU_W), F32),
                        pltpu.VMEM((tm + 8, LRU_W), F32), pltpu.VMEM((tm + 8, LRU_W), F32),
                        pltpu.VMEM((tm + HALO, POOL_W), F32), pltpu.VMEM((8, LRU_W), F32),
                        pltpu.VMEM((8, LRU_W), F32), pltpu.VMEM((HALO, POOL_W), F32)]
        + [pltpu.VMEM((2, 256, 256), BF16)] * 3 + [pltpu.VMEM((16, LRU_W), F32)] + [pltpu.VMEM((2, 256, 256), F32)] * 3
        + (_ChipExchange.scratch(n_x) if n_x else []),
        compiler_params=pltpu.CompilerParams(dimension_semantics=("arbitrary",)),
    )(d2, u, u, hs, hs, pv, wa, wx, wp, w_out_b, *chip_sums)
    return outs[0], outs[1], list(outs[2:])


def _mix_in_bwd(du, x, d2, w_in_t, g_mix, tm, chip_sums=()):
    T = x.shape[0]
    n_t = T // tm
    n_x = len(chip_sums)

    def body(du_ref, x_ref, d2_ref, w_ref, g_ref, *rest):
        x_in, rest = rest[:n_x], rest[n_x:]
        dx_ref, dg_ref = rest[:2]
        exchange = _ChipExchange(x_in, rest[2:2 + n_x], *rest[2 + n_x:]) if n_x else None
        i = pl.program_id(0)

        @pl.when(i == 0)
        def _():
            dg_ref[...] = jnp.zeros((1, D_MODEL), F32)
            if exchange:
                exchange.start()

        dh = jnp.dot(du_ref[...], w_ref[...], preferred_element_type=F32)
        xv = x_ref[...]
        rstd = _rstd(xv)
        dx, dgain = _rms_bwd(dh, xv * rstd, rstd, g_ref[...])
        dx_ref[...] = d2_ref[...] + dx
        dg_ref[...] += dgain

        if exchange:
            @pl.when(i == n_t - 1)
            def _():
                exchange.finish()

    row = lambda w: pl.BlockSpec((tm, w), lambda i: (i, 0))
    const = lambda shape: pl.BlockSpec(shape, lambda i: (0,) * len(shape))
    outs = pl.pallas_call(
        body, name="mix_in_bwd", grid=(n_t,),
        in_specs=[row(D_IN), row(D_MODEL), row(D_MODEL), const((D_IN, D_MODEL)), const((1, D_MODEL))] + [ANY] * n_x,
        out_specs=[row(D_MODEL), const((1, D_MODEL))] + [ANY] * n_x,
        out_shape=[_sds((T, D_MODEL), F32), _sds((1, D_MODEL), F32)] + [_sds(a.shape, a.dtype) for a in chip_sums],
        scratch_shapes=_ChipExchange.scratch(n_x) if n_x else [],
        compiler_params=pltpu.CompilerParams(dimension_semantics=("arbitrary",)),
    )(du, x, d2, w_in_t, g_mix, *chip_sums)
    return outs[0], outs[1], list(outs[2:])


def _pair_sum(g, r1, c_arr, name):
    _, _, R, C = g.shape
    tr = R if R <= 512 else 256

    def body(c_ref, g_ref, r_ref, o_ref):
        o_ref[...] = (g_ref[...] + r_ref[...]).astype(BF16)

    return pl.pallas_call(
        body, name=name,
        grid_spec=pltpu.PrefetchScalarGridSpec(
            num_scalar_prefetch=1, grid=(4, R // tr),
            in_specs=[pl.BlockSpec((None, None, tr, C), lambda j, i, c_ref: (j, c_ref[0], i, 0)),
                      pl.BlockSpec((None, tr, C), lambda j, i, c_ref: (j, i, 0))],
            out_specs=pl.BlockSpec((None, tr, C), lambda j, i, c_ref: (j, i, 0))),
        out_shape=_sds((4, R, C), BF16),
    )(c_arr, g, r1)


def _adamw(w, g, m, v):
    m = ADAM_B1 * m + (1.0 - ADAM_B1) * g
    v = ADAM_B2 * v + (1.0 - ADAM_B2) * (g * g)
    m_hat = m / (1.0 - ADAM_B1 ** ADAM_STEP)
    v_hat = v / (1.0 - ADAM_B2 ** ADAM_STEP)
    delta = -ADAM_LR * (m_hat / (jnp.sqrt(v_hat) + ADAM_EPS) + ADAM_WD * w)
    return delta, m, v


def _adam_shard(w, m, v, parts, name):
    R, C = w.shape
    tr = R if R <= 512 else 256

    def body(w_ref, m_ref, v_ref, p_ref, g_ref, d_ref, nm_ref, nv_ref):
        g = p_ref[0].astype(F32)
        for j in range(1, 4):
            g = g + p_ref[j].astype(F32)
        delta, nm, nv = _adamw(w_ref[...], g, m_ref[...], v_ref[...])
        g_ref[...] = g
        d_ref[...] = delta
        nm_ref[...] = nm
        nv_ref[...] = nv

    blk = pl.BlockSpec((tr, C), lambda i: (i, 0))
    return pl.pallas_call(
        body, name=name, grid=(R // tr,),
        in_specs=[blk, blk, blk, pl.BlockSpec((4, tr, C), lambda i: (0, i, 0))],
        out_specs=[blk] * 4, out_shape=[_sds((R, C), F32)] * 4,
        compiler_params=pltpu.CompilerParams(dimension_semantics=("parallel",)),
    )(w, m, v, parts)


SMALL_PARAMS = [("norm_mix_g", (1, D_MODEL)), ("conv_w", (1, 4, 64)), ("conv_b", (1, LRU_W)),
                ("gate_a_w", (1, 8, 64, 64)), ("gate_a_b", (1, LRU_W)), ("gate_x_w", (1, 8, 64, 64)),
                ("gate_x_b", (1, LRU_W)), ("lru_lambda", (1, LRU_W)), ("pool_w", (1, 4, 128, 128)),
                ("pool_b", (1, POOL_W)), ("pool_scale", (1, POOL_W)), ("norm_lru_g", (1, LRU_W)),
                ("norm_pool_g", (1, POOL_W)), ("norm_ffn_g", (1, D_MODEL)), ("final_norm_g", (1, D_MODEL))]
VEC_ROW = dict(conv_b=ROW_CB, gate_a_b=ROW_BA, gate_x_b=ROW_BX, lru_lambda=ROW_LAM, pool_b=ROW_PB, pool_scale=ROW_PS,
               norm_lru_g=ROW_GL, norm_pool_g=ROW_GP)
WHOLE = (Ellipsis,)


def _unpack_mixer_grads(sg, dev):
    vec = jnp.concatenate([sg[SG_VEC:SG_VEC + 16], sg[SG_VEC + 16:SG_VEC + 32]], axis=1)
    out = {nm: [(WHOLE, vec[r:r + 1])] for nm, r in VEC_ROW.items()}
    own = jnp.zeros((4, 64), F32)
    for d in range(N_DEV):
        own = jnp.where(dev == d, vec[ROW_CW:ROW_CW + 4, 64 * d:64 * (d + 1)], own)
    out["conv_w"] = [((0,), own)]
    for nm, row0 in (("gate_a_w", SG_WA), ("gate_x_w", SG_WX)):
        out[nm] = [((0, b), sg[row0 + 64 * (b // 4):row0 + 64 * (b // 4 + 1), 64 * (b % 4):64 * (b % 4 + 1)])
                   for b in range(8)]
    out["pool_w"] = [((0, b), sg[SG_WP + 128 * (b // 2):SG_WP + 128 * (b // 2 + 1), 128 * (b % 2):128 * (b % 2 + 1)])
                     for b in range(4)]
    return out


def _adam_small(parts, w, m, v):
    names = [nm for nm, _ in SMALL_PARAMS]
    n = len(names)

    def body(sg_ref, gm_ref, gf_ref, gn_ref, ls_ref, *rest):
        w_refs, m_refs, v_refs, outs = rest[:n], rest[n:2 * n], rest[2 * n:3 * n], rest[3 * n:]
        dev = 4 * lax.axis_index("x") + 2 * lax.axis_index("y") + lax.axis_index("c")

        def total(ref):
            acc = ref[0]
            for d in range(1, N_DEV):
                acc = acc + ref[d]
            return acc

        pieces = _unpack_mixer_grads(total(sg_ref), dev)
        pieces["norm_mix_g"] = [(WHOLE, total(gm_ref))]
        pieces["norm_ffn_g"] = [(WHOLE, total(gf_ref))]
        pieces["final_norm_g"] = [(WHOLE, total(gn_ref))]
        for i, nm in enumerate(names):
            for idx, g in pieces[nm]:
                delta, new_m, new_v = _adamw(w_refs[i][idx], g, m_refs[i][idx], v_refs[i][idx])
                for kind, val in enumerate((g, delta, new_m, new_v)):
                    outs[4 * i + kind][idx] = val
        outs[4 * n][...] = total(ls_ref)

    shapes = [_sds(shape, F32) for _, shape in SMALL_PARAMS for _ in range(4)] + [_sds((8, 128), F32)]
    res = pl.pallas_call(body, name="adam_small", out_shape=shapes)(
        *parts, *[w[nm] for nm in names], *[m[nm] for nm in names], *[v[nm] for nm in names])
    return {nm: tuple(res[4 * i:4 * i + 4]) for i, nm in enumerate(names)}, res[4 * n][0, 0]


def _vec_rows(conv_w_full, conv_b, ba, bx, lam, pb, ps, gl, gp):
    return jnp.concatenate([conv_w_full, conv_b, ba, bx, lam, pb, ps, gl, gp, jnp.zeros((4, LRU_W), F32)], axis=0)


WEIGHT_ORDER = ['norm_mix_g', 'w_in', 'conv_w', 'conv_b', 'gate_a_w', 'gate_a_b', 'gate_x_w', 'gate_x_b', 'lru_lambda',
                'pool_w', 'pool_b', 'pool_scale', 'norm_lru_g', 'norm_pool_g', 'w_out', 'norm_ffn_g', 'ffn_w1', 'ffn_w3',
                'ffn_w2', 'final_norm_g']


def kernel(x, norm_mix_g, w_in, conv_w, conv_b, gate_a_w, gate_a_b, gate_x_w, gate_x_b, lru_lambda, pool_w, pool_b, pool_scale, norm_lru_g, norm_pool_g, w_out, norm_ffn_g, ffn_w1, ffn_w3, ffn_w2, final_norm_g, loss_target, m_norm_mix_g, m_w_in, m_conv_w, m_conv_b, m_gate_a_w, m_gate_a_b, m_gate_x_w, m_gate_x_b, m_lru_lambda, m_pool_w, m_pool_b, m_pool_scale, m_norm_lru_g, m_norm_pool_g, m_w_out, m_norm_ffn_g, m_ffn_w1, m_ffn_w3, m_ffn_w2, m_final_norm_g, v_norm_mix_g, v_w_in, v_conv_w, v_conv_b, v_gate_a_w, v_gate_a_b, v_gate_x_w, v_gate_x_b, v_lru_lambda, v_pool_w, v_pool_b, v_pool_scale, v_norm_lru_g, v_norm_pool_g, v_w_out, v_norm_ffn_g, v_ffn_w1, v_ffn_w3, v_ffn_w2, v_final_norm_g):
    ac = lax.axis_index("c")
    tm, tmx, tn, tk = 512, 512, 1408, 1024
    xs, tgt = x[0], loss_target[0]
    g_fin = final_norm_g.reshape(1, D_MODEL)
    c_arr = jnp.reshape(ac, (1,)).astype(jnp.int32)

    def pair_sums(blocks, names):
        from_sibling = _pair_exchange(blocks, "grads_to_sibling_" + names[0])
        return [_pair_sum(g.reshape((4, 2) + g.shape[1:]), r, c_arr, "pair_sum_" + nm)
                for g, r, nm in zip(blocks, from_sibling, names)]

    tr = lambda w: jnp.swapaxes(w[0], 0, 1)
    own = lambda w: w[0]
    bf = lambda a: a.astype(BF16)

    g_in, g_conv = _all_gather([bf(tr(w_in)), conv_w[0]], "gather_w_in")
    w_in_t = g_in.reshape(D_IN, D_MODEL)
    conv_w_full = g_conv.transpose(1, 0, 2).reshape(4, LRU_W)
    pv = _vec_rows(conv_w_full, conv_b, gate_a_b, gate_x_b, lru_lambda, pool_b, pool_scale, norm_lru_g, norm_pool_g)
    wa, wx, wp = gate_a_w[0], gate_x_w[0], pool_w[0]

    u, h1, (g_out,) = _mix_in(xs, norm_mix_g, w_in_t, tm, shards=[bf(own(w_out))])
    w_out_b = g_out.reshape(D_MODEL, D_MODEL)
    y, hs, hres, h2, (g_w1, g_w3, g_w2) = _mixer_fwd(u, xs, pv, wa, wx, wp, w_out_b, norm_ffn_g, tmx,
                                                     shards=[bf(tr(ffn_w1)), bf(tr(ffn_w3)), bf(own(ffn_w2))])
    w1_t, w3_t, w2_b = g_w1.reshape(D_FF, D_MODEL), g_w3.reshape(D_FF, D_MODEL), g_w2.reshape(D_FF, D_MODEL)
    g, v, d3, loss_acc, d_gfin = _ffn_fwd(hres, h2, w1_t, w3_t, w2_b, g_fin, tgt, tm, tn)

    dg, dv, ff, d2, d_gffn = _ffn_bwd(d3, g, v, w1_t, w3_t, w2_b, hres, norm_ffn_g, tm, tn)
    blocks = lambda a: a.reshape(N_DEV, a.shape[0] // N_DEV, a.shape[1])
    chips = lambda a: a.reshape(4, a.shape[0] // 4, a.shape[1])
    early_sums = [chips(_at_b_pair(y, d2, c_arr, "grad_w_out", tk)), chips(_at_b_pair(dg, h2, c_arr, "grad_w1", tk)),
                  chips(_at_b_pair(dv, h2, c_arr, "grad_w3", tk)), chips(_at_b_pair(ff, d3, c_arr, "grad_w2", tk))]
    du, d_mixer, early_parts = _mixer_bwd(d2, u, hs, pv, wa, wx, wp, w_out_b, tmx, chip_sums=early_sums)
    grad_x, d_gmix, _ = _mix_in_bwd(du, xs, d2, w_in_t, norm_mix_g, tm)
    d_win, small_parts = _at_b(du, h1, "grad_w_in", D_IN, D_MODEL, tk,
                               gather=[d_mixer, d_gmix, d_gffn, d_gfin, loss_acc])
    win_parts = _chip_exchange(pair_sums([blocks(d_win)], ["w_in"]), "grads_to_chips_w_in")
    parts = list(win_parts) + list(early_parts)

    res = {}
    shard_w = dict(w_in=(w_in, m_w_in, v_w_in, tr), w_out=(w_out, m_w_out, v_w_out, own),
                   ffn_w1=(ffn_w1, m_ffn_w1, v_ffn_w1, tr), ffn_w3=(ffn_w3, m_ffn_w3, v_ffn_w3, tr),
                   ffn_w2=(ffn_w2, m_ffn_w2, v_ffn_w2, own))
    for (nm, (w, m, v, view)), p in zip(shard_w.items(), parts):
        outs = _adam_shard(view(w), view(m), view(v), p, "adam_" + nm)
        res[nm] = [(jnp.swapaxes(o, 0, 1) if view is tr else o)[None] for o in outs]

    row = lambda a: a.reshape(1, D_MODEL)
    small = lambda gm, cw, cb, wa_, ba, wx_, bx, lam, pw, pb, ps, gl, gp, gf, gn: dict(
        norm_mix_g=gm, conv_w=cw, conv_b=cb, gate_a_w=wa_, gate_a_b=ba, gate_x_w=wx_, gate_x_b=bx, lru_lambda=lam,
        pool_w=pw, pool_b=pb, pool_scale=ps, norm_lru_g=gl, norm_pool_g=gp, norm_ffn_g=gf, final_norm_g=row(gn))
    small_res, loss = _adam_small(
        small_parts,
        small(norm_mix_g, conv_w, conv_b, gate_a_w, gate_a_b, gate_x_w, gate_x_b, lru_lambda, pool_w, pool_b,
              pool_scale, norm_lru_g, norm_pool_g, norm_ffn_g, final_norm_g),
        small(m_norm_mix_g, m_conv_w, m_conv_b, m_gate_a_w, m_gate_a_b, m_gate_x_w, m_gate_x_b, m_lru_lambda, m_pool_w,
              m_pool_b, m_pool_scale, m_norm_lru_g, m_norm_pool_g, m_norm_ffn_g, m_final_norm_g),
        small(v_norm_mix_g, v_conv_w, v_conv_b, v_gate_a_w, v_gate_a_b, v_gate_x_w, v_gate_x_b, v_lru_lambda, v_pool_w,
              v_pool_b, v_pool_scale, v_norm_lru_g, v_norm_pool_g, v_norm_ffn_g, v_final_norm_g))
    for nm, outs in small_res.items():
        res[nm] = [o.reshape(D_MODEL) for o in outs] if nm == "final_norm_g" else list(outs)

    out = [loss, grad_x[None]]
    for kind in range(4):
        out += [res[nm][kind] for nm in WEIGHT_ORDER]
    return tuple(out)
```

```python
import jax
import jax.numpy as jnp
from jax import lax
from jax.experimental import pallas as pl
from jax.experimental.pallas import tpu as pltpu

F32 = jnp.float32
BF16 = jnp.bfloat16

D_MODEL = 1024
LRU_W = 512
POOL_W = 512
D_IN = 1536
D_FF = 2816
POOL_WINDOWS = (2, 4, 8, 16)
EPS = 1e-6
LRU_C = 8.0
N_DEV = 8
HALO = 16
SCAN_UNROLL = 4

ADAM_LR = 0.001
ADAM_B1 = 0.9
ADAM_B2 = 0.999
ADAM_EPS = 1e-08
ADAM_WD = 0.01
ADAM_STEP = 10

ROW_CW, ROW_CB, ROW_BA, ROW_BX, ROW_LAM, ROW_PB, ROW_PS, ROW_GL, ROW_GP = 0, 4, 5, 6, 7, 8, 9, 10, 11
SG_VEC, SG_WA, SG_WX, SG_WP, SG_ROWS = 0, 32, 160, 288, 544

NT = (((1,), (1,)), ((), ()))
TN = (((0,), (0,)), ((), ()))


def _sds(shape, dtype):
    return jax.ShapeDtypeStruct(shape, dtype)


def _sigmoid(x):
    return 0.5 * jnp.tanh(0.5 * x) + 0.5


def _gelu_parts(x):
    c = 0.7978845608028654
    inner = c * (x + 0.044715 * (x * x * x))
    th = jnp.tanh(inner)
    g = 0.5 * x * (1.0 + th)
    dg = 0.5 * (1.0 + th) + 0.5 * x * (1.0 - th * th) * (c * (1.0 + 3.0 * 0.044715 * (x * x)))
    return g, dg


def _window_sum(ext, w, back):
    n = ext.shape[0]
    s, k = ext, 1
    while k < w:
        s = s + pltpu.roll(s, k if back else n - k, 0)
        k *= 2
    return s


def _rstd(x):
    return lax.rsqrt(jnp.mean(x * x, axis=-1, keepdims=True) + EPS)


def _rms_bwd(dy, xhat, rstd, gain):
    dxh = dy * gain
    dx = rstd * (dxh - xhat * jnp.mean(dxh * xhat, axis=-1, keepdims=True))
    return dx, jnp.sum(dy * xhat, axis=0, keepdims=True)


def _bd(xb, w_ref):
    return jnp.concatenate(
        [jnp.dot(xb[:, :256], w_ref[0], preferred_element_type=F32),
         jnp.dot(xb[:, 256:], w_ref[1], preferred_element_type=F32)], axis=1)


def _bd_t(xb, w_ref):
    return jnp.concatenate(
        [lax.dot_general(xb[:, :256], w_ref[0], NT, preferred_element_type=F32),
         lax.dot_general(xb[:, 256:], w_ref[1], NT, preferred_element_type=F32)], axis=1)


def _bd_grad(xb, db):
    return jnp.stack(
        [lax.dot_general(xb[:, :256], db[:, :256], TN, preferred_element_type=F32),
         lax.dot_general(xb[:, 256:], db[:, 256:], TN, preferred_element_type=F32)], axis=0)


def _fill_block_diag(dst, src_ref):
    n, k, _ = src_ref.shape
    dst[...] = jnp.zeros(dst.shape, BF16)
    for b in range(n):
        p, q = divmod(b, 256 // k)
        dst[p, q * k:(q + 1) * k, q * k:(q + 1) * k] = src_ref[b].astype(BF16)


def _diag_pack(w, k):
    lane = lax.broadcasted_iota(jnp.int32, (k, 256), 1)
    out = w[0:k]
    for q in range(1, 256 // k):
        out = jnp.where(lane >= q * k, w[q * k:(q + 1) * k], out)
    return out


def _y_pos(b):
    return 4 * (b % 2) + b // 2


def _mixer_pre(e_lru, e_pool, pv, wa_ref, wx_ref, wp_ref, tm, t0, saved=None):
    z = -pv[ROW_LAM:ROW_LAM + 1, :]
    sp = jnp.maximum(z, 0.0) + jnp.log(1.0 + jnp.exp(-jnp.abs(z)))
    if saved is None:
        xc = pv[ROW_CB:ROW_CB + 1, :]
        for k in range(4):
            xc = xc + e_lru[pl.ds(HALO - 3 + k, tm), :] * pv[ROW_CW + k:ROW_CW + k + 1, :]
        xcb = xc.astype(BF16)
        r = _sigmoid(_bd(xcb, wa_ref) + pv[ROW_BA:ROW_BA + 1, :])
        ig = _sigmoid(_bd(xcb, wx_ref) + pv[ROW_BX:ROW_BX + 1, :])
        la = (-LRU_C * r) * sp
    else:
        xc, r, ig, la = (saved[:, LRU_W * q:LRU_W * (q + 1)] for q in range(4))
        xcb = xc.astype(BF16)
    a = jnp.exp(la)
    om = -jnp.tanh(la) * (1.0 + a * a)
    omc = jnp.maximum(om, 1e-12)
    rmult = lax.rsqrt(omc)
    mult = omc * rmult
    t = t0 + lax.broadcasted_iota(jnp.int32, (tm, 1), 0)
    parts, inv_cnts = [], []
    for g, w in enumerate(POOL_WINDOWS):
        ext = e_pool[:, pl.ds(128 * g, 128)]
        s = _window_sum(ext, w, back=True)[HALO:, :]
        inv_cnt = 1.0 / jnp.minimum(t + 1, w).astype(F32)
        inv_cnts.append(inv_cnt)
        parts.append(s * inv_cnt - ext[HALO:, :])
    pooled = jnp.concatenate(parts, axis=1)
    pooled_b = pooled.astype(BF16)
    zp = _bd(pooled_b, wp_ref) + pv[ROW_PB:ROW_PB + 1, :]
    return dict(xc=xc, xcb=xcb, r=r, ig=ig, la=la, sp=sp, a=a, om=om, mult=mult, rmult=rmult,
                pooled_b=pooled_b, zp=zp, inv_cnts=inv_cnts)


def _scan_tile(a_ref, b_ref, out_ref, carry, tm, reverse):
    row = lax.broadcasted_iota(jnp.int32, (8, LRU_W), 0)
    nblk = tm // 8

    def local_scan(blk):
        r0 = pl.multiple_of(blk * 8, 8)
        av = a_ref[pl.ds(r0, 8), :]
        bv = b_ref[pl.ds(r0, 8), :]
        for d in (1, 2, 4):
            sh = (8 - d) if reverse else d
            a_s = pltpu.roll(av, sh, 0)
            b_s = pltpu.roll(bv, sh, 0)
            m = (row < 8 - d) if reverse else (row >= d)
            bv = jnp.where(m, av * b_s + bv, bv)
            av = jnp.where(m, av * a_s, av)
        return r0, av, bv

    def step(i, hin):
        local = [local_scan((nblk - 1 - (i * SCAN_UNROLL + j)) if reverse else (i * SCAN_UNROLL + j))
                 for j in range(SCAN_UNROLL)]
        for r0, av, bv in local:
            hv = av * hin + bv
            out_ref[pl.ds(r0, 8), :] = hv
            hin = jnp.broadcast_to(hv[0:1, :] if reverse else hv[7:8, :], (8, LRU_W))
        return hin

    return lax.fori_loop(0, nblk // SCAN_UNROLL, step, carry)


MESH = pl.DeviceIdType.MESH
ANY = pl.BlockSpec(memory_space=pl.ANY)


def _place():
    x, y, c = lax.axis_index("x"), lax.axis_index("y"), lax.axis_index("c")
    chips = [(1 - x, y), (x, 1 - y), (1 - x, 1 - y)]
    return x, y, c, chips


class _Gather:
    def __init__(self, ins, outs, send_sems, recv_sems, local_sems, core_major=False):
        self.ins, self.outs, self.n = ins, outs, len(ins)
        self.send_sems, self.recv_sems, self.local_sems = send_sems, recv_sems, local_sems
        self.core_major = core_major

    @staticmethod
    def scratch(n):
        return [pltpu.SemaphoreType.DMA((7, n)), pltpu.SemaphoreType.DMA((7, n)), pltpu.SemaphoreType.DMA((n,))]

    def _slot(self, a, px, py, pc):
        return self.outs[a].at[4 * pc + 2 * px + py if self.core_major else 4 * px + 2 * py + pc]

    def _copy(self, a, k, block, to, src=None):
        return pltpu.make_async_remote_copy(
            src_ref=self._slot(a, *block) if src is None else src, dst_ref=self._slot(a, *block),
            send_sem=self.send_sems.at[k, a], recv_sem=self.recv_sems.at[k, a], device_id=to, device_id_type=MESH)

    def _mine(self, a):
        x, y, c, _ = _place()
        return pltpu.make_async_copy(self.ins[a], self._slot(a, x, y, c), self.local_sems.at[a])

    def _first(self, a):
        x, y, c, chips = _place()
        me = (x, y, c)
        return ([self._copy(a, 0, me, (x, y, 1 - c), src=self.ins[a])]
                + [self._copy(a, 1 + j, me, (*chip, c), src=self.ins[a]) for j, chip in enumerate(chips)])

    def start(self):
        for a in range(self.n):
            self._mine(a).start()
        for a in range(self.n):
            for cp in self._first(a):
                cp.start()

    def finish(self):
        x, y, c, chips = _place()
        me, sibling = (x, y, c), (x, y, 1 - c)
        passed = []
        for j, chip in enumerate(chips):
            for a in range(self.n):
                self._copy(a, 1 + j, (*chip, c), me).wait_recv()
                fwd = self._copy(a, 4 + j, (*chip, c), sibling)
                fwd.start()
                passed.append(fwd)
        for a in range(self.n):
            self._copy(a, 0, (x, y, 1 - c), me).wait_recv()
            for j, chip in enumerate(chips):
                self._copy(a, 4 + j, (*chip, 1 - c), me).wait_recv()
        for a in range(self.n):
            for cp in self._first(a):
                cp.wait_send()
        for cp in passed:
            cp.wait_send()
        for a in range(self.n):
            self._mine(a).wait()


def _all_gather(arrs, name):
    n = len(arrs)

    def body(*refs):
        g = _Gather(refs[:n], refs[n:2 * n], *refs[2 * n:])
        g.start()
        g.finish()

    return pl.pallas_call(
        body, name=name,
        out_shape=[_sds((N_DEV,) + a.shape, a.dtype) for a in arrs],
        in_specs=[ANY] * n, out_specs=[ANY] * n, scratch_shapes=_Gather.scratch(n),
    )(*arrs)


def _pair_exchange(arrs, name):
    n = len(arrs)

    def body(*refs):
        ins, outs = refs[:n], refs[n:2 * n]
        send_sems, recv_sems = refs[2 * n:]
        x, y, c, _ = _place()
        sibling = (x, y, 1 - c)
        sends = []
        for a in range(n):
            for j in range(4):
                cp = pltpu.make_async_remote_copy(
                    src_ref=ins[a].at[2 * j + (1 - c)], dst_ref=outs[a].at[j],
                    send_sem=send_sems.at[j, a], recv_sem=recv_sems.at[j, a], device_id=sibling, device_id_type=MESH)
                cp.start()
                sends.append(cp)
        for cp in sends:
            cp.wait()

    return pl.pallas_call(
        body, name=name,
        out_shape=[_sds((4,) + a.shape[1:], a.dtype) for a in arrs],
        in_specs=[ANY] * n, out_specs=[ANY] * n,
        scratch_shapes=[pltpu.SemaphoreType.DMA((4, n)), pltpu.SemaphoreType.DMA((4, n))],
    )(*arrs)


class _ChipExchange:
    def __init__(self, ins, outs, send_sems, recv_sems, local_sems):
        self.ins, self.outs, self.n = ins, outs, len(ins)
        self.send_sems, self.recv_sems, self.local_sems = send_sems, recv_sems, local_sems

    @staticmethod
    def scratch(n):
        return [pltpu.SemaphoreType.DMA((3, n)), pltpu.SemaphoreType.DMA((3, n)), pltpu.SemaphoreType.DMA((n,))]

    def _local(self, a):
        x, y, _, _ = _place()
        me = 2 * x + y
        return pltpu.make_async_copy(self.ins[a].at[me], self.outs[a].at[me], self.local_sems.at[a])

    def _copies(self, a):
        x, y, c, chips = _place()
        me = 2 * x + y
        return [(pltpu.make_async_remote_copy(
                     src_ref=self.ins[a].at[2 * px + py], dst_ref=self.outs[a].at[me],
                     send_sem=self.send_sems.at[k, a], recv_sem=self.recv_sems.at[k, a],
                     device_id=(px, py, c), device_id_type=MESH),
                 pltpu.make_async_remote_copy(
                     src_ref=self.ins[a].at[me], dst_ref=self.outs[a].at[2 * px + py],
                     send_sem=self.send_sems.at[k, a], recv_sem=self.recv_sems.at[k, a],
                     device_id=(px, py, c), device_id_type=MESH))
                for k, (px, py) in enumerate(chips)]

    def start(self):
        for a in range(self.n):
            self._local(a).start()
        for a in range(self.n):
            for send, _ in self._copies(a):
                send.start()

    def finish(self):
        for a in range(self.n):
            for send, recv in self._copies(a):
                send.wait_send()
                recv.wait_recv()
        for a in range(self.n):
            self._local(a).wait()


def _chip_exchange(arrs, name):
    n = len(arrs)

    def body(*refs):
        e = _ChipExchange(refs[:n], refs[n:2 * n], *refs[2 * n:])
        e.start()
        e.finish()

    return pl.pallas_call(
        body, name=name, out_shape=[_sds(a.shape, a.dtype) for a in arrs],
        in_specs=[ANY] * n, out_specs=[ANY] * n, scratch_shapes=_ChipExchange.scratch(n),
    )(*arrs)


def _mix_in(x, g_mix, w_in_t, tm, shards=()):
    T = x.shape[0]
    n_t = T // tm
    n_s = len(shards)

    def body(x_ref, g_ref, w_ref, *rest):
        sh_in, rest = rest[:n_s], rest[n_s:]
        u_ref, h_ref = rest[:2]
        gather = _Gather(sh_in, rest[2:2 + n_s], *rest[2 + n_s:], core_major=True) if n_s else None
        i = pl.program_id(0)

        if gather:
            @pl.when(i == 0)
            def _():
                gather.start()

        xv = x_ref[...]
        h = (xv * _rstd(xv) * g_ref[...]).astype(BF16)
        h_ref[...] = h
        u_ref[...] = lax.dot_general(h, w_ref[...], NT, preferred_element_type=F32)

        if gather:
            @pl.when(i == n_t - 1)
            def _():
                gather.finish()

    outs = pl.pallas_call(
        body, name="mix_in", grid=(n_t,),
        in_specs=[pl.BlockSpec((tm, D_MODEL), lambda i: (i, 0)),
                  pl.BlockSpec((1, D_MODEL), lambda i: (0, 0)),
                  pl.BlockSpec((D_IN, D_MODEL), lambda i: (0, 0))] + [ANY] * n_s,
        out_specs=[pl.BlockSpec((tm, D_IN), lambda i: (i, 0)),
                   pl.BlockSpec((tm, D_MODEL), lambda i: (i, 0))] + [ANY] * n_s,
        out_shape=[_sds((T, D_IN), F32), _sds((T, D_MODEL), BF16)] + [_sds((N_DEV,) + a.shape, a.dtype) for a in shards],
        scratch_shapes=_Gather.scratch(n_s) if n_s else [],
        compiler_params=pltpu.CompilerParams(dimension_semantics=("arbitrary",)),
    )(x, g_mix, w_in_t, *shards)
    return outs[0], outs[1], list(outs[2:])


def _mixer_fwd(u, x, pv, wa, wx, wp, w_out_b, g_ffn, tm, shards=()):
    T = u.shape[0]
    n_s = len(shards)
    n_t = T // tm

    def body(u_ref, x_ref, pv_ref, wa_in, wx_in, wp_in, wo_ref, gf_ref, *rest):
        sh_in, rest = rest[:n_s], rest[n_s:]
        y_ref, hs_ref, hres_ref, h2_ref, saved_ref = rest[:5]
        sh_out, rest = rest[5:5 + n_s], rest[5 + n_s:]
        e_lru, e_pool, a_s, b_s, hc, wa_ref, wx_ref, wp_ref = rest[:8]
        gather = _Gather(sh_in, sh_out, *rest[8:], core_major=True) if n_s else None
        i = pl.program_id(0)

        @pl.when(i == 0)
        def _():
            if gather:
                gather.start()
            e_lru[pl.ds(0, HALO), :] = jnp.zeros((HALO, LRU_W), F32)
            e_pool[pl.ds(0, HALO), :] = jnp.zeros((HALO, POOL_W), F32)
            hc[...] = jnp.zeros((8, LRU_W), F32)
            _fill_block_diag(wa_ref, wa_in)
            _fill_block_diag(wx_ref, wx_in)
            _fill_block_diag(wp_ref, wp_in)

        e_lru[pl.ds(HALO, tm), :] = u_ref[:, 0:LRU_W]
        e_pool[pl.ds(HALO, tm), :] = u_ref[:, 2 * LRU_W:D_IN]
        pv = pv_ref[...]
        p = _mixer_pre(e_lru, e_pool, pv, wa_ref, wx_ref, wp_ref, tm, i * tm)
        for q, name in enumerate(("xc", "r", "ig", "la")):
            saved_ref[:, LRU_W * q:LRU_W * (q + 1)] = p[name]
        a_s[...] = p["a"]
        b_s[...] = p["mult"] * (p["ig"] * p["xc"])
        hc[...] = _scan_tile(a_s, b_s, hs_ref, hc[...], tm, reverse=False)
        gl, _ = _gelu_parts(u_ref[:, LRU_W:2 * LRU_W])
        y_lru = hs_ref[...] * gl
        y_pool = p["zp"] * pv[ROW_PS:ROW_PS + 1, :]
        yn = jnp.concatenate([y_lru * _rstd(y_lru) * pv[ROW_GL:ROW_GL + 1, :],
                              y_pool * _rstd(y_pool) * pv[ROW_GP:ROW_GP + 1, :]], axis=1).astype(BF16)
        for b in range(N_DEV):
            y_ref[:, 128 * _y_pos(b):128 * (_y_pos(b) + 1)] = yn[:, 128 * b:128 * (b + 1)]
        hr = x_ref[...] + jnp.dot(y_ref[...], wo_ref[...], preferred_element_type=F32)
        hres_ref[...] = hr
        h2_ref[...] = (hr * _rstd(hr) * gf_ref[...]).astype(BF16)
        e_lru[pl.ds(0, HALO), :] = e_lru[pl.ds(tm, HALO), :]
        e_pool[pl.ds(0, HALO), :] = e_pool[pl.ds(tm, HALO), :]

        if gather:
            @pl.when(i == n_t - 1)
            def _():
                gather.finish()

    full = lambda shape: pl.BlockSpec(shape, lambda i: (0,) * len(shape))
    row = lambda w: pl.BlockSpec((tm, w), lambda i: (i, 0))
    outs = pl.pallas_call(
        body, name="mixer_fwd", grid=(n_t,),
        in_specs=[row(D_IN), row(D_MODEL), full((16, LRU_W)), full((8, 64, 64)), full((8, 64, 64)), full((4, 128, 128)),
                  full((D_MODEL, D_MODEL)), full((1, D_MODEL))] + [ANY] * n_s,
        out_specs=[row(D_MODEL), row(LRU_W), row(D_MODEL), row(D_MODEL), row(4 * LRU_W)] + [ANY] * n_s,
        out_shape=[_sds((T, D_MODEL), BF16), _sds((T, LRU_W), F32), _sds((T, D_MODEL), F32), _sds((T, D_MODEL), BF16),
                   _sds((T, 4 * LRU_W), F32)] + [_sds((N_DEV,) + a.shape, a.dtype) for a in shards],
        scratch_shapes=[pltpu.VMEM((HALO + tm, LRU_W), F32), pltpu.VMEM((HALO + tm, POOL_W), F32),
                        pltpu.VMEM((tm, LRU_W), F32), pltpu.VMEM((tm, LRU_W), F32), pltpu.VMEM((8, LRU_W), F32)]
        + [pltpu.VMEM((2, 256, 256), BF16)] * 3 + (_Gather.scratch(n_s) if n_s else []),
        compiler_params=pltpu.CompilerParams(dimension_semantics=("arbitrary",)),
    )(u, x, pv, wa, wx, wp, w_out_b, g_ffn, *shards)
    return outs[0], outs[1], outs[2], outs[3], outs[4], list(outs[5:])


def _ffn_fwd(hres, h2, w1_b, w3_b, w2_b, g_fin, tgt, tm, tn):
    T = hres.shape[0]
    n_j = D_FF // tn

    def body(hres_ref, h2_ref, w1_ref, w3_ref, w2_ref, gfin_ref, tgt_ref,
             g_ref, v_ref, d3_ref, loss_ref, dgfin_ref, acc):
        i, j = pl.program_id(0), pl.program_id(1)

        @pl.when(j == 0)
        def _():
            acc[...] = jnp.zeros((tm, D_MODEL), F32)

        @pl.when((j == 0) & (i == 0))
        def _():
            loss_ref[...] = jnp.zeros((8, 128), F32)
            dgfin_ref[...] = jnp.zeros((1, D_MODEL), F32)

        h2 = h2_ref[...]
        g = lax.dot_general(h2, w1_ref[...], NT, preferred_element_type=F32)
        v = lax.dot_general(h2, w3_ref[...], NT, preferred_element_type=F32)
        g_ref[...] = g.astype(BF16)
        v_ref[...] = v.astype(BF16)
        ff = ((g * _sigmoid(g)) * v).astype(BF16)
        acc[...] += jnp.dot(ff, w2_ref[...], preferred_element_type=F32)

        @pl.when(j == n_j - 1)
        def _():
            h3 = hres_ref[...] + acc[...]
            rstd = _rstd(h3)
            xh = h3 * rstd
            gfin = gfin_ref[...]
            err = xh * gfin - tgt_ref[...]
            loss_ref[...] += 0.5 * jnp.sum(jnp.mean(err * err, axis=-1, keepdims=True))
            dout = err * (1.0 / D_MODEL)
            dx, dgain = _rms_bwd(dout, xh, rstd, gfin)
            d3_ref[...] = dx
            dgfin_ref[...] += dgain

    row = lambda w: pl.BlockSpec((tm, w), lambda i, j: (i, 0))
    const = lambda shape: pl.BlockSpec(shape, lambda i, j: (0,) * len(shape))
    return pl.pallas_call(
        body, name="ffn_fwd", grid=(T // tm, n_j),
        in_specs=[row(D_MODEL), row(D_MODEL),
                  pl.BlockSpec((tn, D_MODEL), lambda i, j: (j, 0)), pl.BlockSpec((tn, D_MODEL), lambda i, j: (j, 0)),
                  pl.BlockSpec((tn, D_MODEL), lambda i, j: (j, 0)), const((1, D_MODEL)), row(D_MODEL)],
        out_specs=[pl.BlockSpec((tm, tn), lambda i, j: (i, j)), pl.BlockSpec((tm, tn), lambda i, j: (i, j)),
                   row(D_MODEL), const((8, 128)), const((1, D_MODEL))],
        out_shape=[_sds((T, D_FF), BF16), _sds((T, D_FF), BF16),
                   _sds((T, D_MODEL), F32), _sds((8, 128), F32), _sds((1, D_MODEL), F32)],
        scratch_shapes=[pltpu.VMEM((tm, D_MODEL), F32)],
        compiler_params=pltpu.CompilerParams(dimension_semantics=("arbitrary", "arbitrary")),
    )(hres, h2, w1_b, w3_b, w2_b, g_fin, tgt)


def _ffn_bwd(d3, g, v, w1_b, w3_b, w2_b, hres, g_ffn, tm, tn):
    T = d3.shape[0]
    n_j = D_FF // tn

    def body(d3_ref, g_ref, v_ref, w1_ref, w3_ref, w2_ref, hres_ref, gf_ref,
             dg_ref, dv_ref, ff_ref, d2_ref, dgffn_ref, acc):
        i, j = pl.program_id(0), pl.program_id(1)

        @pl.when(j == 0)
        def _():
            acc[...] = jnp.zeros((tm, D_MODEL), F32)

        @pl.when((j == 0) & (i == 0))
        def _():
            dgffn_ref[...] = jnp.zeros((1, D_MODEL), F32)

        dff = lax.dot_general(d3_ref[...].astype(BF16), w2_ref[...], NT, preferred_element_type=F32)
        gv = g_ref[...].astype(F32)
        vv = v_ref[...].astype(F32)
        sg = _sigmoid(gv)
        sl = gv * sg
        dgb = (dff * vv * (sg * (1.0 + gv * (1.0 - sg)))).astype(BF16)
        dvb = (dff * sl).astype(BF16)
        dg_ref[...] = dgb
        dv_ref[...] = dvb
        ff_ref[...] = (sl * vv).astype(BF16)
        acc[...] += (jnp.dot(dgb, w1_ref[...], preferred_element_type=F32)
                     + jnp.dot(dvb, w3_ref[...], preferred_element_type=F32))

        @pl.when(j == n_j - 1)
        def _():
            hr = hres_ref[...]
            rstd = _rstd(hr)
            dx, dgain = _rms_bwd(acc[...], hr * rstd, rstd, gf_ref[...])
            d2_ref[...] = d3_ref[...] + dx
            dgffn_ref[...] += dgain

    row = lambda w: pl.BlockSpec((tm, w), lambda i, j: (i, 0))
    tile = pl.BlockSpec((tm, tn), lambda i, j: (i, j))
    const = lambda shape: pl.BlockSpec(shape, lambda i, j: (0,) * len(shape))
    return pl.pallas_call(
        body, name="ffn_bwd", grid=(T // tm, n_j),
        in_specs=[row(D_MODEL), tile, tile,
                  pl.BlockSpec((tn, D_MODEL), lambda i, j: (j, 0)), pl.BlockSpec((tn, D_MODEL), lambda i, j: (j, 0)),
                  pl.BlockSpec((tn, D_MODEL), lambda i, j: (j, 0)), row(D_MODEL), const((1, D_MODEL))],
        out_specs=[tile, tile, tile, row(D_MODEL), const((1, D_MODEL))],
        out_shape=[_sds((T, D_FF), BF16), _sds((T, D_FF), BF16), _sds((T, D_FF), BF16),
                   _sds((T, D_MODEL), F32), _sds((1, D_MODEL), F32)],
        scratch_shapes=[pltpu.VMEM((tm, D_MODEL), F32)],
        compiler_params=pltpu.CompilerParams(dimension_semantics=("arbitrary", "arbitrary")),
    )(d3, g, v, w1_b, w3_b, w2_b, hres, g_ffn)


def _at_b(a, b, name, tmm, tn, tk, gather=()):
    T, M = a.shape
    N = b.shape[1]
    n_m, n_n, n_k = M // tmm, N // tn, T // tk
    n_g = len(gather)

    def body(a_ref, b_ref, *rest):
        g_in, o_ref, rest = rest[:n_g], rest[n_g], rest[n_g + 1:]
        ag = _Gather(g_in, rest[:n_g], *rest[n_g:]) if n_g else None
        m, n, k = pl.program_id(0), pl.program_id(1), pl.program_id(2)

        if ag:
            @pl.when((m == 0) & (n == 0) & (k == 0))
            def _():
                ag.start()

        @pl.when(k == 0)
        def _():
            o_ref[...] = jnp.zeros((tmm, tn), F32)

        o_ref[...] += lax.dot_general(a_ref[...].astype(BF16), b_ref[...].astype(BF16), TN,
                                      preferred_element_type=F32)

        if ag:
            @pl.when((m == n_m - 1) & (n == n_n - 1) & (k == n_k - 1))
            def _():
                ag.finish()

    outs = pl.pallas_call(
        body, name=name, grid=(n_m, n_n, n_k),
        in_specs=[pl.BlockSpec((tk, tmm), lambda m, n, k: (k, m)), pl.BlockSpec((tk, tn), lambda m, n, k: (k, n))]
        + [ANY] * n_g,
        out_specs=[pl.BlockSpec((tmm, tn), lambda m, n, k: (m, n))] + [ANY] * n_g,
        out_shape=[_sds((M, N), F32)] + [_sds((N_DEV,) + g.shape, g.dtype) for g in gather],
        scratch_shapes=_Gather.scratch(n_g) if n_g else [],
        compiler_params=pltpu.CompilerParams(
            dimension_semantics=("arbitrary",) * 3 if n_g else ("parallel", "parallel", "arbitrary")),
    )(a, b, *gather)
    return (outs[0], list(outs[1:])) if n_g else outs[0]


def _at_b_pair(a, b, c_arr, name, tk):
    T, M = a.shape
    N = b.shape[1]
    hm, n_k = M // 2, T // tk

    def body(c_ref, a_ref, b_ref, o_ref, acc, landed, send_sem, recv_sem):
        ph, k = pl.program_id(0), pl.program_id(1)
        def hand_over():
            x, y, c, _ = _place()
            return pltpu.make_async_remote_copy(
                src_ref=acc.at[0], dst_ref=landed, send_sem=send_sem, recv_sem=recv_sem,
                device_id=(x, y, 1 - c), device_id_type=MESH)

        @pl.when(k == 0)
        def _():
            acc[ph] = jnp.zeros((hm, N), F32)

        acc[ph] += lax.dot_general(a_ref[...].astype(BF16), b_ref[...].astype(BF16), TN, preferred_element_type=F32)

        @pl.when((ph == 0) & (k == n_k - 1))
        def _():
            hand_over().start()

        @pl.when((ph == 1) & (k == n_k - 1))
        def _():
            copy = hand_over()
            copy.wait_recv()
            o_ref[...] = (acc[1] + landed[...]).astype(BF16)
            copy.wait_send()

    return pl.pallas_call(
        body, name=name,
        grid_spec=pltpu.PrefetchScalarGridSpec(
            num_scalar_prefetch=1, grid=(2, n_k),
            in_specs=[pl.BlockSpec((tk, hm), lambda ph, k, c_ref: (k, (ph + 1 - c_ref[0]) % 2)),
                      pl.BlockSpec((tk, N), lambda ph, k, c_ref: (k, 0))],
            out_specs=pl.BlockSpec((hm, N), lambda ph, k, c_ref: (0, 0)),
            scratch_shapes=[pltpu.VMEM((2, hm, N), F32), pltpu.VMEM((hm, N), F32),
                            pltpu.SemaphoreType.DMA, pltpu.SemaphoreType.DMA]),
        out_shape=_sds((hm, N), BF16),
        compiler_params=pltpu.CompilerParams(dimension_semantics=("arbitrary", "arbitrary")),
    )(c_arr, a, b)


def _mixer_bwd(d2, u, hs, saved, pv, wa, wx, wp, w_out_b, tm, chip_sums=()):
    T = u.shape[0]
    n_t = T // tm
    n_x = len(chip_sums)

    def body(d2_ref, u_ref, uh_ref, hs_ref, hh_ref, saved_ref, pv_ref, wa_in, wx_in, wp_in, wo_ref, *rest):
        x_in, rest = rest[:n_x], rest[n_x:]
        du_ref, sg_ref = rest[:2]
        x_out, rest = rest[2:2 + n_x], rest[2 + n_x:]
        e_pool, e_h, a_s, b_s, mu_s, f_x, f_p, mc, cx, cp = rest[:10]
        wa_ref, wx_ref, wp_ref, vacc_ref, dwa_ref, dwx_ref, dwp_ref = rest[10:17]
        exchange = _ChipExchange(x_in, x_out, *rest[17:]) if n_x else None
        s = pl.program_id(0)
        it = n_t - 1 - s

        @pl.when(s == 0)
        def _():
            if exchange:
                exchange.start()
            mc[...] = jnp.zeros((8, LRU_W), F32)
            cx[...] = jnp.zeros((8, LRU_W), F32)
            cp[...] = jnp.zeros((HALO, POOL_W), F32)
            vacc_ref[...] = jnp.zeros((16, LRU_W), F32)
            dwa_ref[...] = jnp.zeros((2, 256, 256), F32)
            dwx_ref[...] = jnp.zeros((2, 256, 256), F32)
            dwp_ref[...] = jnp.zeros((2, 256, 256), F32)
            _fill_block_diag(wa_ref, wa_in)
            _fill_block_diag(wx_ref, wx_in)
            _fill_block_diag(wp_ref, wp_in)

        first = it == 0
        e_pool[pl.ds(0, HALO), :] = jnp.where(first, 0.0, uh_ref[...])
        e_pool[pl.ds(HALO, tm), :] = u_ref[:, 2 * LRU_W:D_IN]
        e_h[pl.ds(0, 8), :] = jnp.where(first, 0.0, hh_ref[...])
        e_h[pl.ds(8, tm), :] = hs_ref[...]
        pv = pv_ref[...]
        p = _mixer_pre(None, e_pool, pv, wa_ref, wx_ref, wp_ref, tm, it * tm, saved=saved_ref)
        a, xc, ig, r, mult = p["a"], p["xc"], p["ig"], p["r"], p["mult"]

        dyn = lax.dot_general(d2_ref[...].astype(BF16), wo_ref[...], NT, preferred_element_type=F32)
        dyn = jnp.concatenate([dyn[:, 128 * _y_pos(b):128 * (_y_pos(b) + 1)] for b in range(N_DEV)], axis=1)

        h = hs_ref[...]
        ug = u_ref[:, LRU_W:2 * LRU_W]
        gl, dgl = _gelu_parts(ug)
        y_lru = h * gl
        rstd_l = _rstd(y_lru)
        dy_lru, d_gain_l = _rms_bwd(dyn[:, 0:LRU_W], y_lru * rstd_l, rstd_l, pv[ROW_GL:ROW_GL + 1, :])
        dh = dy_lru * gl
        du_ref[:, LRU_W:2 * LRU_W] = (dy_lru * h * dgl).astype(BF16)
        a_s[...] = a
        b_s[...] = a * dh
        mu_s[pl.ds(tm, 8), :] = mc[...]
        mc[...] = _scan_tile(a_s, b_s, mu_s, mc[...], tm, reverse=True)
        lam_t = dh + mu_s[pl.ds(1, tm), :]
        da = lam_t * e_h[pl.ds(7, tm), :]
        dmult = lam_t * (ig * xc)
        di = lam_t * (mult * xc)
        dxc = lam_t * (mult * ig)
        dla = da * a - jnp.where(p["om"] > 1e-12, dmult * ((a * a) * p["rmult"]), 0.0)
        dra = (dla * (-LRU_C * p["sp"])) * (r * (1.0 - r))
        dia = di * (ig * (1.0 - ig))
        drab = dra.astype(BF16)
        diab = dia.astype(BF16)
        dxc = dxc + _bd_t(drab, wa_ref) + _bd_t(diab, wx_ref)
        dwa_ref[...] += _bd_grad(p["xcb"], drab)
        dwx_ref[...] += _bd_grad(p["xcb"], diab)
        sig_neg_lam = _sigmoid(-pv[ROW_LAM:ROW_LAM + 1, :])
        d_lam = jnp.sum(dla * r, axis=0, keepdims=True) * (LRU_C * sig_neg_lam)

        f_x[pl.ds(0, tm), :] = dxc
        f_x[pl.ds(tm, 8), :] = cx[...]
        du_lru = jnp.zeros((tm, LRU_W), F32)
        u_lru = u_ref[:, 0:LRU_W]
        d_cw = []
        for k in range(4):
            later = f_x[pl.ds(3 - k, tm), :]
            du_lru = du_lru + later * pv[ROW_CW + k:ROW_CW + k + 1, :]
            d_cw.append(jnp.sum(later * u_lru, axis=0, keepdims=True))
        du_ref[:, 0:LRU_W] = du_lru.astype(BF16)
        cx[...] = f_x[pl.ds(0, 8), :]

        zp = p["zp"]
        ps = pv[ROW_PS:ROW_PS + 1, :]
        y_pool = zp * ps
        rstd_p = _rstd(y_pool)
        dy_pool, d_gain_p = _rms_bwd(dyn[:, LRU_W:D_MODEL], y_pool * rstd_p, rstd_p, pv[ROW_GP:ROW_GP + 1, :])
        dz = dy_pool * ps
        dzb = dz.astype(BF16)
        dwp_ref[...] += _bd_grad(p["pooled_b"], dzb)
        dpooled = _bd_t(dzb, wp_ref)
        for g, w in enumerate(POOL_WINDOWS):
            f_p[pl.ds(0, tm), pl.ds(128 * g, 128)] = dpooled[:, 128 * g:128 * (g + 1)] * p["inv_cnts"][g]
        f_p[pl.ds(tm, HALO), :] = cp[...]
        for g, w in enumerate(POOL_WINDOWS):
            acc = _window_sum(f_p[:, pl.ds(128 * g, 128)], w, back=False)[0:tm, :]
            du_ref[:, 2 * LRU_W + 128 * g:2 * LRU_W + 128 * (g + 1)] = (
                acc - dpooled[:, 128 * g:128 * (g + 1)]).astype(BF16)
        cp[...] = f_p[pl.ds(0, HALO), :]

        rows = d_cw + [
            jnp.sum(dxc, axis=0, keepdims=True),
            jnp.sum(dra, axis=0, keepdims=True),
            jnp.sum(dia, axis=0, keepdims=True),
            d_lam,
            jnp.sum(dz, axis=0, keepdims=True),
            jnp.sum(dy_pool * zp, axis=0, keepdims=True),
            d_gain_l, d_gain_p,
            jnp.zeros((4, LRU_W), F32),
        ]
        vacc_ref[...] += jnp.concatenate(rows, axis=0)

        @pl.when(s == n_t - 1)
        def _():
            sg_ref[SG_VEC:SG_VEC + 16, :] = vacc_ref[:, 0:256]
            sg_ref[SG_VEC + 16:SG_VEC + 32, :] = vacc_ref[:, 256:512]
            for half in range(2):
                sg_ref[SG_WA + 64 * half:SG_WA + 64 * (half + 1), :] = _diag_pack(dwa_ref[half], 64)
                sg_ref[SG_WX + 64 * half:SG_WX + 64 * (half + 1), :] = _diag_pack(dwx_ref[half], 64)
                sg_ref[SG_WP + 128 * half:SG_WP + 128 * (half + 1), :] = _diag_pack(dwp_ref[half], 128)
            if exchange:
                exchange.finish()

    rev = lambda w: pl.BlockSpec((tm, w), lambda s: (n_t - 1 - s, 0))
    full = lambda shape: pl.BlockSpec(shape, lambda s: (0,) * len(shape))
    outs = pl.pallas_call(
        body, name="mixer_bwd", grid=(n_t,),
        in_specs=[rev(D_MODEL), rev(D_IN),
                  pl.BlockSpec((HALO, POOL_W), lambda s: (jnp.maximum((n_t - 1 - s) * (tm // HALO) - 1, 0), 2)),
                  rev(LRU_W),
                  pl.BlockSpec((8, LRU_W), lambda s: (jnp.maximum((n_t - 1 - s) * (tm // 8) - 1, 0), 0)),
                  rev(4 * LRU_W), full((16, LRU_W)), full((8, 64, 64)), full((8, 64, 64)), full((4, 128, 128)),
                  full((D_MODEL, D_MODEL))] + [ANY] * n_x,
        out_specs=[rev(D_IN), full((SG_ROWS, 256))] + [ANY] * n_x,
        out_shape=[_sds((T, D_IN), BF16), _sds((SG_ROWS, 256), F32)] + [_sds(a.shape, a.dtype) for a in chip_sums],
        scratch_shapes=[pltpu.VMEM((HALO + tm, POOL_W), F32),
                        pltpu.VMEM((8 + tm, LRU_W), F32), pltpu.VMEM((tm, LRU_W), F32), pltpu.VMEM((tm, LRU_W), F32),
                        pltpu.VMEM((tm + 8, LRU_W), F32), pltpu.VMEM((tm + 8, LRU_W), F32),
                        pltpu.VMEM((tm + HALO, POOL_W), F32), pltpu.VMEM((8, LRU_W), F32),
                        pltpu.VMEM((8, LRU_W), F32), pltpu.VMEM((HALO, POOL_W), F32)]
        + [pltpu.VMEM((2, 256, 256), BF16)] * 3 + [pltpu.VMEM((16, LRU_W), F32)] + [pltpu.VMEM((2, 256, 256), F32)] * 3
        + (_ChipExchange.scratch(n_x) if n_x else []),
        compiler_params=pltpu.CompilerParams(dimension_semantics=("arbitrary",)),
    )(d2, u, u, hs, hs, saved, pv, wa, wx, wp, w_out_b, *chip_sums)
    return outs[0], outs[1], list(outs[2:])


def _mix_in_bwd(du, x, d2, w_in_t, g_mix, tm, chip_sums=()):
    T = x.shape[0]
    n_t = T // tm
    n_x = len(chip_sums)

    def body(du_ref, x_ref, d2_ref, w_ref, g_ref, *rest):
        x_in, rest = rest[:n_x], rest[n_x:]
        dx_ref, dg_ref = rest[:2]
        exchange = _ChipExchange(x_in, rest[2:2 + n_x], *rest[2 + n_x:]) if n_x else None
        i = pl.program_id(0)

        @pl.when(i == 0)
        def _():
            dg_ref[...] = jnp.zeros((1, D_MODEL), F32)
            if exchange:
                exchange.start()

        dh = jnp.dot(du_ref[...], w_ref[...], preferred_element_type=F32)
        xv = x_ref[...]
        rstd = _rstd(xv)
        dx, dgain = _rms_bwd(dh, xv * rstd, rstd, g_ref[...])
        dx_ref[...] = d2_ref[...] + dx
        dg_ref[...] += dgain

        if exchange:
            @pl.when(i == n_t - 1)
            def _():
                exchange.finish()

    row = lambda w: pl.BlockSpec((tm, w), lambda i: (i, 0))
    const = lambda shape: pl.BlockSpec(shape, lambda i: (0,) * len(shape))
    outs = pl.pallas_call(
        body, name="mix_in_bwd", grid=(n_t,),
        in_specs=[row(D_IN), row(D_MODEL), row(D_MODEL), const((D_IN, D_MODEL)), const((1, D_MODEL))] + [ANY] * n_x,
        out_specs=[row(D_MODEL), const((1, D_MODEL))] + [ANY] * n_x,
        out_shape=[_sds((T, D_MODEL), F32), _sds((1, D_MODEL), F32)] + [_sds(a.shape, a.dtype) for a in chip_sums],
        scratch_shapes=_ChipExchange.scratch(n_x) if n_x else [],
        compiler_params=pltpu.CompilerParams(dimension_semantics=("arbitrary",)),
    )(du, x, d2, w_in_t, g_mix, *chip_sums)
    return outs[0], outs[1], list(outs[2:])


def _pair_sum(g, r1, c_arr, name):
    _, _, R, C = g.shape
    tr = R if R <= 512 else 256

    def body(c_ref, g_ref, r_ref, o_ref):
        o_ref[...] = (g_ref[...] + r_ref[...]).astype(BF16)

    return pl.pallas_call(
        body, name=name,
        grid_spec=pltpu.PrefetchScalarGridSpec(
            num_scalar_prefetch=1, grid=(4, R // tr),
            in_specs=[pl.BlockSpec((None, None, tr, C), lambda j, i, c_ref: (j, c_ref[0], i, 0)),
                      pl.BlockSpec((None, tr, C), lambda j, i, c_ref: (j, i, 0))],
            out_specs=pl.BlockSpec((None, tr, C), lambda j, i, c_ref: (j, i, 0))),
        out_shape=_sds((4, R, C), BF16),
    )(c_arr, g, r1)


def _adamw(w, g, m, v):
    m = ADAM_B1 * m + (1.0 - ADAM_B1) * g
    v = ADAM_B2 * v + (1.0 - ADAM_B2) * (g * g)
    m_hat = m / (1.0 - ADAM_B1 ** ADAM_STEP)
    v_hat = v / (1.0 - ADAM_B2 ** ADAM_STEP)
    delta = -ADAM_LR * (m_hat / (jnp.sqrt(v_hat) + ADAM_EPS) + ADAM_WD * w)
    return delta, m, v


def _adam_shard(w, m, v, parts, name):
    R, C = w.shape
    tr = R if R <= 512 else 256

    def body(w_ref, m_ref, v_ref, p_ref, g_ref, d_ref, nm_ref, nv_ref):
        g = p_ref[0].astype(F32)
        for j in range(1, 4):
            g = g + p_ref[j].astype(F32)
        delta, nm, nv = _adamw(w_ref[...], g, m_ref[...], v_ref[...])
        g_ref[...] = g
        d_ref[...] = delta
        nm_ref[...] = nm
        nv_ref[...] = nv

    blk = pl.BlockSpec((tr, C), lambda i: (i, 0))
    return pl.pallas_call(
        body, name=name, grid=(R // tr,),
        in_specs=[blk, blk, blk, pl.BlockSpec((4, tr, C), lambda i: (0, i, 0))],
        out_specs=[blk] * 4, out_shape=[_sds((R, C), F32)] * 4,
        compiler_params=pltpu.CompilerParams(dimension_semantics=("parallel",)),
    )(w, m, v, parts)


SMALL_PARAMS = [("norm_mix_g", (1, D_MODEL)), ("conv_w", (1, 4, 64)), ("conv_b", (1, LRU_W)),
                ("gate_a_w", (1, 8, 64, 64)), ("gate_a_b", (1, LRU_W)), ("gate_x_w", (1, 8, 64, 64)),
                ("gate_x_b", (1, LRU_W)), ("lru_lambda", (1, LRU_W)), ("pool_w", (1, 4, 128, 128)),
                ("pool_b", (1, POOL_W)), ("pool_scale", (1, POOL_W)), ("norm_lru_g", (1, LRU_W)),
                ("norm_pool_g", (1, POOL_W)), ("norm_ffn_g", (1, D_MODEL)), ("final_norm_g", (1, D_MODEL))]
VEC_ROW = dict(conv_b=ROW_CB, gate_a_b=ROW_BA, gate_x_b=ROW_BX, lru_lambda=ROW_LAM, pool_b=ROW_PB, pool_scale=ROW_PS,
               norm_lru_g=ROW_GL, norm_pool_g=ROW_GP)
WHOLE = (Ellipsis,)


def _unpack_mixer_grads(sg, dev):
    vec = jnp.concatenate([sg[SG_VEC:SG_VEC + 16], sg[SG_VEC + 16:SG_VEC + 32]], axis=1)
    out = {nm: [(WHOLE, vec[r:r + 1])] for nm, r in VEC_ROW.items()}
    own = jnp.zeros((4, 64), F32)
    for d in range(N_DEV):
        own = jnp.where(dev == d, vec[ROW_CW:ROW_CW + 4, 64 * d:64 * (d + 1)], own)
    out["conv_w"] = [((0,), own)]
    for nm, row0 in (("gate_a_w", SG_WA), ("gate_x_w", SG_WX)):
        out[nm] = [((0, b), sg[row0 + 64 * (b // 4):row0 + 64 * (b // 4 + 1), 64 * (b % 4):64 * (b % 4 + 1)])
                   for b in range(8)]
    out["pool_w"] = [((0, b), sg[SG_WP + 128 * (b // 2):SG_WP + 128 * (b // 2 + 1), 128 * (b % 2):128 * (b % 2 + 1)])
                     for b in range(4)]
    return out


def _adam_small(parts, w, m, v):
    names = [nm for nm, _ in SMALL_PARAMS]
    n = len(names)

    def body(sg_ref, gm_ref, gf_ref, gn_ref, ls_ref, *rest):
        w_refs, m_refs, v_refs, outs = rest[:n], rest[n:2 * n], rest[2 * n:3 * n], rest[3 * n:]
        dev = 4 * lax.axis_index("x") + 2 * lax.axis_index("y") + lax.axis_index("c")

        def total(ref):
            acc = ref[0]
            for d in range(1, N_DEV):
                acc = acc + ref[d]
            return acc

        pieces = _unpack_mixer_grads(total(sg_ref), dev)
        pieces["norm_mix_g"] = [(WHOLE, total(gm_ref))]
        pieces["norm_ffn_g"] = [(WHOLE, total(gf_ref))]
        pieces["final_norm_g"] = [(WHOLE, total(gn_ref))]
        for i, nm in enumerate(names):
            for idx, g in pieces[nm]:
                delta, new_m, new_v = _adamw(w_refs[i][idx], g, m_refs[i][idx], v_refs[i][idx])
                for kind, val in enumerate((g, delta, new_m, new_v)):
                    outs[4 * i + kind][idx] = val
        outs[4 * n][...] = total(ls_ref)

    shapes = [_sds(shape, F32) for _, shape in SMALL_PARAMS for _ in range(4)] + [_sds((8, 128), F32)]
    res = pl.pallas_call(body, name="adam_small", out_shape=shapes)(
        *parts, *[w[nm] for nm in names], *[m[nm] for nm in names], *[v[nm] for nm in names])
    return {nm: tuple(res[4 * i:4 * i + 4]) for i, nm in enumerate(names)}, res[4 * n][0, 0]


def _vec_rows(conv_w_full, conv_b, ba, bx, lam, pb, ps, gl, gp):
    return jnp.concatenate([conv_w_full, conv_b, ba, bx, lam, pb, ps, gl, gp, jnp.zeros((4, LRU_W), F32)], axis=0)


WEIGHT_ORDER = ['norm_mix_g', 'w_in', 'conv_w', 'conv_b', 'gate_a_w', 'gate_a_b', 'gate_x_w', 'gate_x_b', 'lru_lambda',
                'pool_w', 'pool_b', 'pool_scale', 'norm_lru_g', 'norm_pool_g', 'w_out', 'norm_ffn_g', 'ffn_w1', 'ffn_w3',
                'ffn_w2', 'final_norm_g']


def kernel(x, norm_mix_g, w_in, conv_w, conv_b, gate_a_w, gate_a_b, gate_x_w, gate_x_b, lru_lambda, pool_w, pool_b, pool_scale, norm_lru_g, norm_pool_g, w_out, norm_ffn_g, ffn_w1, ffn_w3, ffn_w2, final_norm_g, loss_target, m_norm_mix_g, m_w_in, m_conv_w, m_conv_b, m_gate_a_w, m_gate_a_b, m_gate_x_w, m_gate_x_b, m_lru_lambda, m_pool_w, m_pool_b, m_pool_scale, m_norm_lru_g, m_norm_pool_g, m_w_out, m_norm_ffn_g, m_ffn_w1, m_ffn_w3, m_ffn_w2, m_final_norm_g, v_norm_mix_g, v_w_in, v_conv_w, v_conv_b, v_gate_a_w, v_gate_a_b, v_gate_x_w, v_gate_x_b, v_lru_lambda, v_pool_w, v_pool_b, v_pool_scale, v_norm_lru_g, v_norm_pool_g, v_w_out, v_norm_ffn_g, v_ffn_w1, v_ffn_w3, v_ffn_w2, v_final_norm_g):
    ac = lax.axis_index("c")
    tm, tmx, tn, tk = 512, 512, 1408, 1024
    xs, tgt = x[0], loss_target[0]
    g_fin = final_norm_g.reshape(1, D_MODEL)
    c_arr = jnp.reshape(ac, (1,)).astype(jnp.int32)

    def pair_sums(blocks, names):
        from_sibling = _pair_exchange(blocks, "grads_to_sibling_" + names[0])
        return [_pair_sum(g.reshape((4, 2) + g.shape[1:]), r, c_arr, "pair_sum_" + nm)
                for g, r, nm in zip(blocks, from_sibling, names)]

    tr = lambda w: jnp.swapaxes(w[0], 0, 1)
    own = lambda w: w[0]
    bf = lambda a: a.astype(BF16)

    g_in, g_conv = _all_gather([bf(tr(w_in)), conv_w[0]], "gather_w_in")
    w_in_t = g_in.reshape(D_IN, D_MODEL)
    conv_w_full = g_conv.transpose(1, 0, 2).reshape(4, LRU_W)
    pv = _vec_rows(conv_w_full, conv_b, gate_a_b, gate_x_b, lru_lambda, pool_b, pool_scale, norm_lru_g, norm_pool_g)
    wa, wx, wp = gate_a_w[0], gate_x_w[0], pool_w[0]

    u, h1, (g_out, g_w1) = _mix_in(xs, norm_mix_g, w_in_t, tm, shards=[bf(own(w_out)), bf(tr(ffn_w1))])
    w_out_b = g_out.reshape(D_MODEL, D_MODEL)
    y, hs, hres, h2, saved, (g_w3, g_w2) = _mixer_fwd(u, xs, pv, wa, wx, wp, w_out_b, norm_ffn_g, tmx,
                                               shards=[bf(tr(ffn_w3)), bf(own(ffn_w2))])
    w1_t, w3_t, w2_b = g_w1.reshape(D_FF, D_MODEL), g_w3.reshape(D_FF, D_MODEL), g_w2.reshape(D_FF, D_MODEL)
    g, v, d3, loss_acc, d_gfin = _ffn_fwd(hres, h2, w1_t, w3_t, w2_b, g_fin, tgt, tm, tn)

    dg, dv, ff, d2, d_gffn = _ffn_bwd(d3, g, v, w1_t, w3_t, w2_b, hres, norm_ffn_g, tm, tn)
    blocks = lambda a: a.reshape(N_DEV, a.shape[0] // N_DEV, a.shape[1])
    chips = lambda a: a.reshape(4, a.shape[0] // 4, a.shape[1])
    early_sums = [chips(_at_b_pair(y, d2, c_arr, "grad_w_out", tk)), chips(_at_b_pair(dg, h2, c_arr, "grad_w1", tk)),
                  chips(_at_b_pair(dv, h2, c_arr, "grad_w3", tk)), chips(_at_b_pair(ff, d3, c_arr, "grad_w2", tk))]
    du, d_mixer, early_parts = _mixer_bwd(d2, u, hs, saved, pv, wa, wx, wp, w_out_b, tmx, chip_sums=early_sums)
    grad_x, d_gmix, _ = _mix_in_bwd(du, xs, d2, w_in_t, norm_mix_g, tm)
    d_win, small_parts = _at_b(du, h1, "grad_w_in", D_IN, D_MODEL, tk,
                               gather=[d_mixer, d_gmix, d_gffn, d_gfin, loss_acc])
    win_parts = _chip_exchange(pair_sums([blocks(d_win)], ["w_in"]), "grads_to_chips_w_in")
    parts = list(win_parts) + list(early_parts)

    res = {}
    shard_w = dict(w_in=(w_in, m_w_in, v_w_in, tr), w_out=(w_out, m_w_out, v_w_out, own),
                   ffn_w1=(ffn_w1, m_ffn_w1, v_ffn_w1, tr), ffn_w3=(ffn_w3, m_ffn_w3, v_ffn_w3, tr),
                   ffn_w2=(ffn_w2, m_ffn_w2, v_ffn_w2, own))
    for (nm, (w, m, v, view)), p in zip(shard_w.items(), parts):
        outs = _adam_shard(view(w), view(m), view(v), p, "adam_" + nm)
        res[nm] = [(jnp.swapaxes(o, 0, 1) if view is tr else o)[None] for o in outs]

    row = lambda a: a.reshape(1, D_MODEL)
    small = lambda gm, cw, cb, wa_, ba, wx_, bx, lam, pw, pb, ps, gl, gp, gf, gn: dict(
        norm_mix_g=gm, conv_w=cw, conv_b=cb, gate_a_w=wa_, gate_a_b=ba, gate_x_w=wx_, gate_x_b=bx, lru_lambda=lam,
        pool_w=pw, pool_b=pb, pool_scale=ps, norm_lru_g=gl, norm_pool_g=gp, norm_ffn_g=gf, final_norm_g=row(gn))
    small_res, loss = _adam_small(
        small_parts,
        small(norm_mix_g, conv_w, conv_b, gate_a_w, gate_a_b, gate_x_w, gate_x_b, lru_lambda, pool_w, pool_b,
              pool_scale, norm_lru_g, norm_pool_g, norm_ffn_g, final_norm_g),
        small(m_norm_mix_g, m_conv_w, m_conv_b, m_gate_a_w, m_gate_a_b, m_gate_x_w, m_gate_x_b, m_lru_lambda, m_pool_w,
              m_pool_b, m_pool_scale, m_norm_lru_g, m_norm_pool_g, m_norm_ffn_g, m_final_norm_g),
        small(v_norm_mix_g, v_conv_w, v_conv_b, v_gate_a_w, v_gate_a_b, v_gate_x_w, v_gate_x_b, v_lru_lambda, v_pool_w,
              v_pool_b, v_pool_scale, v_norm_lru_g, v_norm_pool_g, v_norm_ffn_g, v_final_norm_g))
    for nm, outs in small_res.items():
        res[nm] = [o.reshape(D_MODEL) for o in outs] if nm == "final_norm_g" else list(outs)

    out = [loss, grad_x[None]]
    for kind in range(4):
        out += [res[nm][kind] for nm in WEIGHT_ORDER]
    return tuple(out)
```

```python
import jax
import jax.numpy as jnp
from jax import lax
from jax.experimental import pallas as pl
from jax.experimental.pallas import tpu as pltpu

F32 = jnp.float32
BF16 = jnp.bfloat16

D_MODEL = 1024
LRU_W = 512
POOL_W = 512
D_IN = 1536
D_FF = 2816
POOL_WINDOWS = (2, 4, 8, 16)
EPS = 1e-6
LRU_C = 8.0
N_DEV = 8
HALO = 16
SCAN_UNROLL = 4

ADAM_LR = 0.001
ADAM_B1 = 0.9
ADAM_B2 = 0.999
ADAM_EPS = 1e-08
ADAM_WD = 0.01
ADAM_STEP = 10

ROW_CW, ROW_CB, ROW_BA, ROW_BX, ROW_LAM, ROW_PB, ROW_PS, ROW_GL, ROW_GP = 0, 4, 5, 6, 7, 8, 9, 10, 11
SG_VEC, SG_WA, SG_WX, SG_WP, SG_ROWS = 0, 32, 160, 288, 544

NT = (((1,), (1,)), ((), ()))
TN = (((0,), (0,)), ((), ()))


def _sds(shape, dtype):
    return jax.ShapeDtypeStruct(shape, dtype)


def _sigmoid(x):
    return 0.5 * jnp.tanh(0.5 * x) + 0.5


def _gelu_parts(x):
    c = 0.7978845608028654
    inner = c * (x + 0.044715 * (x * x * x))
    th = jnp.tanh(inner)
    g = 0.5 * x * (1.0 + th)
    dg = 0.5 * (1.0 + th) + 0.5 * x * (1.0 - th * th) * (c * (1.0 + 3.0 * 0.044715 * (x * x)))
    return g, dg


def _window_sum(ext, w, back):
    n = ext.shape[0]
    s, k = ext, 1
    while k < w:
        s = s + pltpu.roll(s, k if back else n - k, 0)
        k *= 2
    return s


def _rstd(x):
    return lax.rsqrt(jnp.mean(x * x, axis=-1, keepdims=True) + EPS)


def _rms_bwd(dy, xhat, rstd, gain):
    dxh = dy * gain
    dx = rstd * (dxh - xhat * jnp.mean(dxh * xhat, axis=-1, keepdims=True))
    return dx, jnp.sum(dy * xhat, axis=0, keepdims=True)


def _bd(xb, w_ref):
    return jnp.concatenate(
        [jnp.dot(xb[:, :256], w_ref[0], preferred_element_type=F32),
         jnp.dot(xb[:, 256:], w_ref[1], preferred_element_type=F32)], axis=1)


def _bd_t(xb, w_ref):
    return jnp.concatenate(
        [lax.dot_general(xb[:, :256], w_ref[0], NT, preferred_element_type=F32),
         lax.dot_general(xb[:, 256:], w_ref[1], NT, preferred_element_type=F32)], axis=1)


def _bd_grad(xb, db):
    return jnp.stack(
        [lax.dot_general(xb[:, :256], db[:, :256], TN, preferred_element_type=F32),
         lax.dot_general(xb[:, 256:], db[:, 256:], TN, preferred_element_type=F32)], axis=0)


def _fill_block_diag(dst, src_ref):
    n, k, _ = src_ref.shape
    dst[...] = jnp.zeros(dst.shape, BF16)
    for b in range(n):
        p, q = divmod(b, 256 // k)
        dst[p, q * k:(q + 1) * k, q * k:(q + 1) * k] = src_ref[b].astype(BF16)


def _diag_pack(w, k):
    lane = lax.broadcasted_iota(jnp.int32, (k, 256), 1)
    out = w[0:k]
    for q in range(1, 256 // k):
        out = jnp.where(lane >= q * k, w[q * k:(q + 1) * k], out)
    return out


def _y_pos(b):
    return 4 * (b % 2) + b // 2


def _mixer_pre(e_lru, e_pool, pv, wa_ref, wx_ref, wp_ref, tm, t0, saved=None):
    z = -pv[ROW_LAM:ROW_LAM + 1, :]
    sp = jnp.maximum(z, 0.0) + jnp.log(1.0 + jnp.exp(-jnp.abs(z)))
    if saved is None:
        xc = pv[ROW_CB:ROW_CB + 1, :]
        for k in range(4):
            xc = xc + e_lru[pl.ds(HALO - 3 + k, tm), :] * pv[ROW_CW + k:ROW_CW + k + 1, :]
        xcb = xc.astype(BF16)
        r = _sigmoid(_bd(xcb, wa_ref) + pv[ROW_BA:ROW_BA + 1, :])
        ig = _sigmoid(_bd(xcb, wx_ref) + pv[ROW_BX:ROW_BX + 1, :])
        la = (-LRU_C * r) * sp
    else:
        xc, r, ig, la = (saved[:, LRU_W * q:LRU_W * (q + 1)] for q in range(4))
        xcb = xc.astype(BF16)
    a = jnp.exp(la)
    om = -jnp.tanh(la) * (1.0 + a * a)
    omc = jnp.maximum(om, 1e-12)
    rmult = lax.rsqrt(omc)
    mult = omc * rmult
    t = t0 + lax.broadcasted_iota(jnp.int32, (tm, 1), 0)
    parts, inv_cnts = [], []
    for g, w in enumerate(POOL_WINDOWS):
        ext = e_pool[:, pl.ds(128 * g, 128)]
        s = _window_sum(ext, w, back=True)[HALO:, :]
        inv_cnt = 1.0 / jnp.minimum(t + 1, w).astype(F32)
        inv_cnts.append(inv_cnt)
        parts.append(s * inv_cnt - ext[HALO:, :])
    pooled = jnp.concatenate(parts, axis=1)
    pooled_b = pooled.astype(BF16)
    zp = _bd(pooled_b, wp_ref) + pv[ROW_PB:ROW_PB + 1, :]
    return dict(xc=xc, xcb=xcb, r=r, ig=ig, la=la, sp=sp, a=a, om=om, mult=mult, rmult=rmult,
                pooled_b=pooled_b, zp=zp, inv_cnts=inv_cnts)


def _scan_tile(a_ref, b_ref, out_ref, carry, tm, reverse):
    row = lax.broadcasted_iota(jnp.int32, (8, LRU_W), 0)
    nblk = tm // 8

    def local_scan(blk):
        r0 = pl.multiple_of(blk * 8, 8)
        av = a_ref[pl.ds(r0, 8), :]
        bv = b_ref[pl.ds(r0, 8), :]
        for d in (1, 2, 4):
            sh = (8 - d) if reverse else d
            a_s = pltpu.roll(av, sh, 0)
            b_s = pltpu.roll(bv, sh, 0)
            m = (row < 8 - d) if reverse else (row >= d)
            bv = jnp.where(m, av * b_s + bv, bv)
            av = jnp.where(m, av * a_s, av)
        return r0, av, bv

    def step(i, hin):
        local = [local_scan((nblk - 1 - (i * SCAN_UNROLL + j)) if reverse else (i * SCAN_UNROLL + j))
                 for j in range(SCAN_UNROLL)]
        for r0, av, bv in local:
            hv = av * hin + bv
            out_ref[pl.ds(r0, 8), :] = hv
            hin = jnp.broadcast_to(hv[0:1, :] if reverse else hv[7:8, :], (8, LRU_W))
        return hin

    return lax.fori_loop(0, nblk // SCAN_UNROLL, step, carry)


MESH = pl.DeviceIdType.MESH
ANY = pl.BlockSpec(memory_space=pl.ANY)


def _place():
    x, y, c = lax.axis_index("x"), lax.axis_index("y"), lax.axis_index("c")
    chips = [(1 - x, y), (x, 1 - y), (1 - x, 1 - y)]
    return x, y, c, chips


class _Gather:
    def __init__(self, ins, outs, send_sems, recv_sems, local_sems, core_major=False):
        self.ins, self.outs, self.n = ins, outs, len(ins)
        self.send_sems, self.recv_sems, self.local_sems = send_sems, recv_sems, local_sems
        self.core_major = core_major

    @staticmethod
    def scratch(n):
        return [pltpu.SemaphoreType.DMA((7, n)), pltpu.SemaphoreType.DMA((7, n)), pltpu.SemaphoreType.DMA((n,))]

    def _slot(self, a, px, py, pc):
        return self.outs[a].at[4 * pc + 2 * px + py if self.core_major else 4 * px + 2 * py + pc]

    def _copy(self, a, k, block, to, src=None):
        return pltpu.make_async_remote_copy(
            src_ref=self._slot(a, *block) if src is None else src, dst_ref=self._slot(a, *block),
            send_sem=self.send_sems.at[k, a], recv_sem=self.recv_sems.at[k, a], device_id=to, device_id_type=MESH)

    def _mine(self, a):
        x, y, c, _ = _place()
        return pltpu.make_async_copy(self.ins[a], self._slot(a, x, y, c), self.local_sems.at[a])

    def _first(self, a):
        x, y, c, chips = _place()
        me = (x, y, c)
        return ([self._copy(a, 0, me, (x, y, 1 - c), src=self.ins[a])]
                + [self._copy(a, 1 + j, me, (*chip, c), src=self.ins[a]) for j, chip in enumerate(chips)])

    def start(self):
        for a in range(self.n):
            self._mine(a).start()
        for a in range(self.n):
            for cp in self._first(a):
                cp.start()

    def finish(self):
        x, y, c, chips = _place()
        me, sibling = (x, y, c), (x, y, 1 - c)
        passed = []
        for j, chip in enumerate(chips):
            for a in range(self.n):
                self._copy(a, 1 + j, (*chip, c), me).wait_recv()
                fwd = self._copy(a, 4 + j, (*chip, c), sibling)
                fwd.start()
                passed.append(fwd)
        for a in range(self.n):
            self._copy(a, 0, (x, y, 1 - c), me).wait_recv()
            for j, chip in enumerate(chips):
                self._copy(a, 4 + j, (*chip, 1 - c), me).wait_recv()
        for a in range(self.n):
            for cp in self._first(a):
                cp.wait_send()
        for cp in passed:
            cp.wait_send()
        for a in range(self.n):
            self._mine(a).wait()


def _all_gather(arrs, name):
    n = len(arrs)

    def body(*refs):
        g = _Gather(refs[:n], refs[n:2 * n], *refs[2 * n:])
        g.start()
        g.finish()

    return pl.pallas_call(
        body, name=name,
        out_shape=[_sds((N_DEV,) + a.shape, a.dtype) for a in arrs],
        in_specs=[ANY] * n, out_specs=[ANY] * n, scratch_shapes=_Gather.scratch(n),
    )(*arrs)


def _pair_exchange(arrs, name):
    n = len(arrs)

    def body(*refs):
        ins, outs = refs[:n], refs[n:2 * n]
        send_sems, recv_sems = refs[2 * n:]
        x, y, c, _ = _place()
        sibling = (x, y, 1 - c)
        sends = []
        for a in range(n):
            for j in range(4):
                cp = pltpu.make_async_remote_copy(
                    src_ref=ins[a].at[2 * j + (1 - c)], dst_ref=outs[a].at[j],
                    send_sem=send_sems.at[j, a], recv_sem=recv_sems.at[j, a], device_id=sibling, device_id_type=MESH)
                cp.start()
                sends.append(cp)
        for cp in sends:
            cp.wait()

    return pl.pallas_call(
        body, name=name,
        out_shape=[_sds((4,) + a.shape[1:], a.dtype) for a in arrs],
        in_specs=[ANY] * n, out_specs=[ANY] * n,
        scratch_shapes=[pltpu.SemaphoreType.DMA((4, n)), pltpu.SemaphoreType.DMA((4, n))],
    )(*arrs)


class _ChipExchange:
    def __init__(self, ins, outs, send_sems, recv_sems, local_sems):
        self.ins, self.outs, self.n = ins, outs, len(ins)
        self.send_sems, self.recv_sems, self.local_sems = send_sems, recv_sems, local_sems

    @staticmethod
    def scratch(n):
        return [pltpu.SemaphoreType.DMA((3, n)), pltpu.SemaphoreType.DMA((3, n)), pltpu.SemaphoreType.DMA((n,))]

    def _local(self, a):
        x, y, _, _ = _place()
        me = 2 * x + y
        return pltpu.make_async_copy(self.ins[a].at[me], self.outs[a].at[me], self.local_sems.at[a])

    def _copies(self, a):
        x, y, c, chips = _place()
        me = 2 * x + y
        return [(pltpu.make_async_remote_copy(
                     src_ref=self.ins[a].at[2 * px + py], dst_ref=self.outs[a].at[me],
                     send_sem=self.send_sems.at[k, a], recv_sem=self.recv_sems.at[k, a],
                     device_id=(px, py, c), device_id_type=MESH),
                 pltpu.make_async_remote_copy(
                     src_ref=self.ins[a].at[me], dst_ref=self.outs[a].at[2 * px + py],
                     send_sem=self.send_sems.at[k, a], recv_sem=self.recv_sems.at[k, a],
                     device_id=(px, py, c), device_id_type=MESH))
                for k, (px, py) in enumerate(chips)]

    def start(self):
        for a in range(self.n):
            self._local(a).start()
        for a in range(self.n):
            for send, _ in self._copies(a):
                send.start()

    def finish(self):
        for a in range(self.n):
            for send, recv in self._copies(a):
                send.wait_send()
                recv.wait_recv()
        for a in range(self.n):
            self._local(a).wait()


def _chip_exchange(arrs, name):
    n = len(arrs)

    def body(*refs):
        e = _ChipExchange(refs[:n], refs[n:2 * n], *refs[2 * n:])
        e.start()
        e.finish()

    return pl.pallas_call(
        body, name=name, out_shape=[_sds(a.shape, a.dtype) for a in arrs],
        in_specs=[ANY] * n, out_specs=[ANY] * n, scratch_shapes=_ChipExchange.scratch(n),
    )(*arrs)


def _mix_in(x, g_mix, w_in_t, tm, shards=()):
    T = x.shape[0]
    n_t = T // tm
    n_s = len(shards)

    def body(x_ref, g_ref, w_ref, *rest):
        sh_in, rest = rest[:n_s], rest[n_s:]
        u_ref, h_ref = rest[:2]
        gather = _Gather(sh_in, rest[2:2 + n_s], *rest[2 + n_s:], core_major=True) if n_s else None
        i = pl.program_id(0)

        if gather:
            @pl.when(i == 0)
            def _():
                gather.start()

        xv = x_ref[...]
        h = (xv * _rstd(xv) * g_ref[...]).astype(BF16)
        h_ref[...] = h
        u_ref[...] = lax.dot_general(h, w_ref[...], NT, preferred_element_type=F32)

        if gather:
            @pl.when(i == n_t - 1)
            def _():
                gather.finish()

    outs = pl.pallas_call(
        body, name="mix_in", grid=(n_t,),
        in_specs=[pl.BlockSpec((tm, D_MODEL), lambda i: (i, 0)),
                  pl.BlockSpec((1, D_MODEL), lambda i: (0, 0)),
                  pl.BlockSpec((D_IN, D_MODEL), lambda i: (0, 0))] + [ANY] * n_s,
        out_specs=[pl.BlockSpec((tm, D_IN), lambda i: (i, 0)),
                   pl.BlockSpec((tm, D_MODEL), lambda i: (i, 0))] + [ANY] * n_s,
        out_shape=[_sds((T, D_IN), F32), _sds((T, D_MODEL), BF16)] + [_sds((N_DEV,) + a.shape, a.dtype) for a in shards],
        scratch_shapes=_Gather.scratch(n_s) if n_s else [],
        compiler_params=pltpu.CompilerParams(dimension_semantics=("arbitrary",)),
    )(x, g_mix, w_in_t, *shards)
    return outs[0], outs[1], list(outs[2:])


def _mixer_fwd(u, x, pv, wa, wx, wp, w_out_b, g_ffn, tm, shards=()):
    T = u.shape[0]
    n_s = len(shards)
    n_t = T // tm

    def body(u_ref, x_ref, pv_ref, wa_in, wx_in, wp_in, wo_ref, gf_ref, *rest):
        sh_in, rest = rest[:n_s], rest[n_s:]
        y_ref, hs_ref, hres_ref, h2_ref, saved_ref = rest[:5]
        sh_out, rest = rest[5:5 + n_s], rest[5 + n_s:]
        e_lru, e_pool, a_s, b_s, hc, wa_ref, wx_ref, wp_ref = rest[:8]
        gather = _Gather(sh_in, sh_out, *rest[8:], core_major=True) if n_s else None
        i = pl.program_id(0)

        @pl.when(i == 0)
        def _():
            if gather:
                gather.start()
            e_lru[pl.ds(0, HALO), :] = jnp.zeros((HALO, LRU_W), F32)
            e_pool[pl.ds(0, HALO), :] = jnp.zeros((HALO, POOL_W), F32)
            hc[...] = jnp.zeros((8, LRU_W), F32)
            _fill_block_diag(wa_ref, wa_in)
            _fill_block_diag(wx_ref, wx_in)
            _fill_block_diag(wp_ref, wp_in)

        e_lru[pl.ds(HALO, tm), :] = u_ref[:, 0:LRU_W]
        e_pool[pl.ds(HALO, tm), :] = u_ref[:, 2 * LRU_W:D_IN]
        pv = pv_ref[...]
        p = _mixer_pre(e_lru, e_pool, pv, wa_ref, wx_ref, wp_ref, tm, i * tm)
        for q, name in enumerate(("xc", "r", "ig", "la")):
            saved_ref[:, LRU_W * q:LRU_W * (q + 1)] = p[name]
        a_s[...] = p["a"]
        b_s[...] = p["mult"] * (p["ig"] * p["xc"])
        hc[...] = _scan_tile(a_s, b_s, hs_ref, hc[...], tm, reverse=False)
        gl, _ = _gelu_parts(u_ref[:, LRU_W:2 * LRU_W])
        y_lru = hs_ref[...] * gl
        y_pool = p["zp"] * pv[ROW_PS:ROW_PS + 1, :]
        yn = jnp.concatenate([y_lru * _rstd(y_lru) * pv[ROW_GL:ROW_GL + 1, :],
                              y_pool * _rstd(y_pool) * pv[ROW_GP:ROW_GP + 1, :]], axis=1).astype(BF16)
        for b in range(N_DEV):
            y_ref[:, 128 * _y_pos(b):128 * (_y_pos(b) + 1)] = yn[:, 128 * b:128 * (b + 1)]
        hr = x_ref[...] + jnp.dot(y_ref[...], wo_ref[...], preferred_element_type=F32)
        hres_ref[...] = hr
        h2_ref[...] = (hr * _rstd(hr) * gf_ref[...]).astype(BF16)
        e_lru[pl.ds(0, HALO), :] = e_lru[pl.ds(tm, HALO), :]
        e_pool[pl.ds(0, HALO), :] = e_pool[pl.ds(tm, HALO), :]

        if gather:
            @pl.when(i == n_t - 1)
            def _():
                gather.finish()

    full = lambda shape: pl.BlockSpec(shape, lambda i: (0,) * len(shape))
    row = lambda w: pl.BlockSpec((tm, w), lambda i: (i, 0))
    outs = pl.pallas_call(
        body, name="mixer_fwd", grid=(n_t,),
        in_specs=[row(D_IN), row(D_MODEL), full((16, LRU_W)), full((8, 64, 64)), full((8, 64, 64)), full((4, 128, 128)),
                  full((D_MODEL, D_MODEL)), full((1, D_MODEL))] + [ANY] * n_s,
        out_specs=[row(D_MODEL), row(LRU_W), row(D_MODEL), row(D_MODEL), row(4 * LRU_W)] + [ANY] * n_s,
        out_shape=[_sds((T, D_MODEL), BF16), _sds((T, LRU_W), F32), _sds((T, D_MODEL), F32), _sds((T, D_MODEL), BF16),
                   _sds((T, 4 * LRU_W), F32)] + [_sds((N_DEV,) + a.shape, a.dtype) for a in shards],
        scratch_shapes=[pltpu.VMEM((HALO + tm, LRU_W), F32), pltpu.VMEM((HALO + tm, POOL_W), F32),
                        pltpu.VMEM((tm, LRU_W), F32), pltpu.VMEM((tm, LRU_W), F32), pltpu.VMEM((8, LRU_W), F32)]
        + [pltpu.VMEM((2, 256, 256), BF16)] * 3 + (_Gather.scratch(n_s) if n_s else []),
        compiler_params=pltpu.CompilerParams(dimension_semantics=("arbitrary",)),
    )(u, x, pv, wa, wx, wp, w_out_b, g_ffn, *shards)
    return outs[0], outs[1], outs[2], outs[3], outs[4], list(outs[5:])


def _ffn_fwd(hres, h2, w1_b, w3_b, w2_b, g_fin, tgt, tm, tn):
    T = hres.shape[0]
    n_j = D_FF // tn

    def body(hres_ref, h2_ref, w1_ref, w3_ref, w2_ref, gfin_ref, tgt_ref,
             g_ref, v_ref, d3_ref, loss_ref, dgfin_ref, acc):
        i, j = pl.program_id(0), pl.program_id(1)

        @pl.when(j == 0)
        def _():
            acc[...] = jnp.zeros((tm, D_MODEL), F32)

        @pl.when((j == 0) & (i == 0))
        def _():
            loss_ref[...] = jnp.zeros((8, 128), F32)
            dgfin_ref[...] = jnp.zeros((1, D_MODEL), F32)

        h2 = h2_ref[...]
        g = lax.dot_general(h2, w1_ref[...], NT, preferred_element_type=F32)
        v = lax.dot_general(h2, w3_ref[...], NT, preferred_element_type=F32)
        g_ref[...] = g.astype(BF16)
        v_ref[...] = v.astype(BF16)
        ff = ((g * _sigmoid(g)) * v).astype(BF16)
        acc[...] += jnp.dot(ff, w2_ref[...], preferred_element_type=F32)

        @pl.when(j == n_j - 1)
        def _():
            h3 = hres_ref[...] + acc[...]
            rstd = _rstd(h3)
            xh = h3 * rstd
            gfin = gfin_ref[...]
            err = xh * gfin - tgt_ref[...]
            loss_ref[...] += 0.5 * jnp.sum(jnp.mean(err * err, axis=-1, keepdims=True))
            dout = err * (1.0 / D_MODEL)
            dx, dgain = _rms_bwd(dout, xh, rstd, gfin)
            d3_ref[...] = dx
            dgfin_ref[...] += dgain

    row = lambda w: pl.BlockSpec((tm, w), lambda i, j: (i, 0))
    const = lambda shape: pl.BlockSpec(shape, lambda i, j: (0,) * len(shape))
    return pl.pallas_call(
        body, name="ffn_fwd", grid=(T // tm, n_j),
        in_specs=[row(D_MODEL), row(D_MODEL),
                  pl.BlockSpec((tn, D_MODEL), lambda i, j: (j, 0)), pl.BlockSpec((tn, D_MODEL), lambda i, j: (j, 0)),
                  pl.BlockSpec((tn, D_MODEL), lambda i, j: (j, 0)), const((1, D_MODEL)), row(D_MODEL)],
        out_specs=[pl.BlockSpec((tm, tn), lambda i, j: (i, j)), pl.BlockSpec((tm, tn), lambda i, j: (i, j)),
                   row(D_MODEL), const((8, 128)), const((1, D_MODEL))],
        out_shape=[_sds((T, D_FF), BF16), _sds((T, D_FF), BF16),
                   _sds((T, D_MODEL), F32), _sds((8, 128), F32), _sds((1, D_MODEL), F32)],
        scratch_shapes=[pltpu.VMEM((tm, D_MODEL), F32)],
        compiler_params=pltpu.CompilerParams(dimension_semantics=("arbitrary", "arbitrary")),
    )(hres, h2, w1_b, w3_b, w2_b, g_fin, tgt)


def _ffn_bwd(d3, g, v, w1_b, w3_b, w2_b, hres, g_ffn, tm, tn):
    T = d3.shape[0]
    n_j = D_FF // tn

    def body(d3_ref, g_ref, v_ref, w1_ref, w3_ref, w2_ref, hres_ref, gf_ref,
             dg_ref, dv_ref, ff_ref, d2_ref, dgffn_ref, acc):
        i, j = pl.program_id(0), pl.program_id(1)

        @pl.when(j == 0)
        def _():
            acc[...] = jnp.zeros((tm, D_MODEL), F32)

        @pl.when((j == 0) & (i == 0))
        def _():
            dgffn_ref[...] = jnp.zeros((1, D_MODEL), F32)

        dff = lax.dot_general(d3_ref[...].astype(BF16), w2_ref[...], NT, preferred_element_type=F32)
        gv = g_ref[...].astype(F32)
        vv = v_ref[...].astype(F32)
        sg = _sigmoid(gv)
        sl = gv * sg
        dgb = (dff * vv * (sg * (1.0 + gv * (1.0 - sg)))).astype(BF16)
        dvb = (dff * sl).astype(BF16)
        dg_ref[...] = dgb
        dv_ref[...] = dvb
        ff_ref[...] = (sl * vv).astype(BF16)
        acc[...] += (jnp.dot(dgb, w1_ref[...], preferred_element_type=F32)
                     + jnp.dot(dvb, w3_ref[...], preferred_element_type=F32))

        @pl.when(j == n_j - 1)
        def _():
            hr = hres_ref[...]
            rstd = _rstd(hr)
            dx, dgain = _rms_bwd(acc[...], hr * rstd, rstd, gf_ref[...])
            d2_ref[...] = d3_ref[...] + dx
            dgffn_ref[...] += dgain

    row = lambda w: pl.BlockSpec((tm, w), lambda i, j: (i, 0))
    tile = pl.BlockSpec((tm, tn), lambda i, j: (i, j))
    const = lambda shape: pl.BlockSpec(shape, lambda i, j: (0,) * len(shape))
    return pl.pallas_call(
        body, name="ffn_bwd", grid=(T // tm, n_j),
        in_specs=[row(D_MODEL), tile, tile,
                  pl.BlockSpec((tn, D_MODEL), lambda i, j: (j, 0)), pl.BlockSpec((tn, D_MODEL), lambda i, j: (j, 0)),
                  pl.BlockSpec((tn, D_MODEL), lambda i, j: (j, 0)), row(D_MODEL), const((1, D_MODEL))],
        out_specs=[tile, tile, tile, row(D_MODEL), const((1, D_MODEL))],
        out_shape=[_sds((T, D_FF), BF16), _sds((T, D_FF), BF16), _sds((T, D_FF), BF16),
                   _sds((T, D_MODEL), F32), _sds((1, D_MODEL), F32)],
        scratch_shapes=[pltpu.VMEM((tm, D_MODEL), F32)],
        compiler_params=pltpu.CompilerParams(dimension_semantics=("arbitrary", "arbitrary")),
    )(d3, g, v, w1_b, w3_b, w2_b, hres, g_ffn)


def _at_b(a, b, name, tmm, tn, tk, gather=()):
    T, M = a.shape
    N = b.shape[1]
    n_m, n_n, n_k = M // tmm, N // tn, T // tk
    n_g = len(gather)

    def body(a_ref, b_ref, *rest):
        g_in, o_ref, rest = rest[:n_g], rest[n_g], rest[n_g + 1:]
        ag = _Gather(g_in, rest[:n_g], *rest[n_g:]) if n_g else None
        m, n, k = pl.program_id(0), pl.program_id(1), pl.program_id(2)

        if ag:
            @pl.when((m == 0) & (n == 0) & (k == 0))
            def _():
                ag.start()

        @pl.when(k == 0)
        def _():
            o_ref[...] = jnp.zeros((tmm, tn), F32)

        o_ref[...] += lax.dot_general(a_ref[...].astype(BF16), b_ref[...].astype(BF16), TN,
                                      preferred_element_type=F32)

        if ag:
            @pl.when((m == n_m - 1) & (n == n_n - 1) & (k == n_k - 1))
            def _():
                ag.finish()

    outs = pl.pallas_call(
        body, name=name, grid=(n_m, n_n, n_k),
        in_specs=[pl.BlockSpec((tk, tmm), lambda m, n, k: (k, m)), pl.BlockSpec((tk, tn), lambda m, n, k: (k, n))]
        + [ANY] * n_g,
        out_specs=[pl.BlockSpec((tmm, tn), lambda m, n, k: (m, n))] + [ANY] * n_g,
        out_shape=[_sds((M, N), F32)] + [_sds((N_DEV,) + g.shape, g.dtype) for g in gather],
        scratch_shapes=_Gather.scratch(n_g) if n_g else [],
        compiler_params=pltpu.CompilerParams(
            dimension_semantics=("arbitrary",) * 3 if n_g else ("parallel", "parallel", "arbitrary")),
    )(a, b, *gather)
    return (outs[0], list(outs[1:])) if n_g else outs[0]


def _at_b_pair(a, b, c_arr, name, tk):
    T, M = a.shape
    N = b.shape[1]
    hm, n_k = M // 2, T // tk

    def body(c_ref, a_ref, b_ref, o_ref, acc, landed, send_sem, recv_sem):
        ph, k = pl.program_id(0), pl.program_id(1)
        def hand_over():
            x, y, c, _ = _place()
            return pltpu.make_async_remote_copy(
                src_ref=acc.at[0], dst_ref=landed, send_sem=send_sem, recv_sem=recv_sem,
                device_id=(x, y, 1 - c), device_id_type=MESH)

        @pl.when(k == 0)
        def _():
            acc[ph] = jnp.zeros((hm, N), F32)

        acc[ph] += lax.dot_general(a_ref[...].astype(BF16), b_ref[...].astype(BF16), TN, preferred_element_type=F32)

        @pl.when((ph == 0) & (k == n_k - 1))
        def _():
            hand_over().start()

        @pl.when((ph == 1) & (k == n_k - 1))
        def _():
            copy = hand_over()
            copy.wait_recv()
            o_ref[...] = (acc[1] + landed[...]).astype(BF16)
            copy.wait_send()

    return pl.pallas_call(
        body, name=name,
        grid_spec=pltpu.PrefetchScalarGridSpec(
            num_scalar_prefetch=1, grid=(2, n_k),
            in_specs=[pl.BlockSpec((tk, hm), lambda ph, k, c_ref: (k, (ph + 1 - c_ref[0]) % 2)),
                      pl.BlockSpec((tk, N), lambda ph, k, c_ref: (k, 0))],
            out_specs=pl.BlockSpec((hm, N), lambda ph, k, c_ref: (0, 0)),
            scratch_shapes=[pltpu.VMEM((2, hm, N), F32), pltpu.VMEM((hm, N), F32),
                            pltpu.SemaphoreType.DMA, pltpu.SemaphoreType.DMA]),
        out_shape=_sds((hm, N), BF16),
        compiler_params=pltpu.CompilerParams(dimension_semantics=("arbitrary", "arbitrary")),
    )(c_arr, a, b)


def _mixer_bwd(d2, u, hs, saved, pv, wa, wx, wp, w_out_b, tm, chip_sums=()):
    T = u.shape[0]
    n_t = T // tm
    n_x = len(chip_sums)

    def body(d2_ref, u_ref, uh_ref, hs_ref, hh_ref, saved_ref, pv_ref, wa_in, wx_in, wp_in, wo_ref, *rest):
        x_in, rest = rest[:n_x], rest[n_x:]
        du_ref, sg_ref = rest[:2]
        x_out, rest = rest[2:2 + n_x], rest[2 + n_x:]
        e_pool, e_h, a_s, b_s, mu_s, f_x, f_p, mc, cx, cp = rest[:10]
        wa_ref, wx_ref, wp_ref, vacc_ref, dwa_ref, dwx_ref, dwp_ref = rest[10:17]
        exchange = _ChipExchange(x_in, x_out, *rest[17:]) if n_x else None
        s = pl.program_id(0)
        it = n_t - 1 - s

        @pl.when(s == 0)
        def _():
            if exchange:
                exchange.start()
            mc[...] = jnp.zeros((8, LRU_W), F32)
            cx[...] = jnp.zeros((8, LRU_W), F32)
            cp[...] = jnp.zeros((HALO, POOL_W), F32)
            vacc_ref[...] = jnp.zeros((16, LRU_W), F32)
            dwa_ref[...] = jnp.zeros((2, 256, 256), F32)
            dwx_ref[...] = jnp.zeros((2, 256, 256), F32)
            dwp_ref[...] = jnp.zeros((2, 256, 256), F32)
            _fill_block_diag(wa_ref, wa_in)
            _fill_block_diag(wx_ref, wx_in)
            _fill_block_diag(wp_ref, wp_in)

        first = it == 0
        e_pool[pl.ds(0, HALO), :] = jnp.where(first, 0.0, uh_ref[...])
        e_pool[pl.ds(HALO, tm), :] = u_ref[:, 2 * LRU_W:D_IN]
        e_h[pl.ds(0, 8), :] = jnp.where(first, 0.0, hh_ref[...])
        e_h[pl.ds(8, tm), :] = hs_ref[...]
        pv = pv_ref[...]
        p = _mixer_pre(None, e_pool, pv, wa_ref, wx_ref, wp_ref, tm, it * tm, saved=saved_ref)
        a, xc, ig, r, mult = p["a"], p["xc"], p["ig"], p["r"], p["mult"]

        dyn = lax.dot_general(d2_ref[...].astype(BF16), wo_ref[...], NT, preferred_element_type=F32)
        dyn = jnp.concatenate([dyn[:, 128 * _y_pos(b):128 * (_y_pos(b) + 1)] for b in range(N_DEV)], axis=1)

        h = hs_ref[...]
        ug = u_ref[:, LRU_W:2 * LRU_W]
        gl, dgl = _gelu_parts(ug)
        y_lru = h * gl
        rstd_l = _rstd(y_lru)
        dy_lru, d_gain_l = _rms_bwd(dyn[:, 0:LRU_W], y_lru * rstd_l, rstd_l, pv[ROW_GL:ROW_GL + 1, :])
        dh = dy_lru * gl
        du_ref[:, LRU_W:2 * LRU_W] = (dy_lru * h * dgl).astype(BF16)
        a_s[...] = a
        b_s[...] = a * dh
        mu_s[pl.ds(tm, 8), :] = mc[...]
        mc[...] = _scan_tile(a_s, b_s, mu_s, mc[...], tm, reverse=True)
        lam_t = dh + mu_s[pl.ds(1, tm), :]
        da = lam_t * e_h[pl.ds(7, tm), :]
        dmult = lam_t * (ig * xc)
        di = lam_t * (mult * xc)
        dxc = lam_t * (mult * ig)
        dla = da * a - jnp.where(p["om"] > 1e-12, dmult * ((a * a) * p["rmult"]), 0.0)
        dra = (dla * (-LRU_C * p["sp"])) * (r * (1.0 - r))
        dia = di * (ig * (1.0 - ig))
        drab = dra.astype(BF16)
        diab = dia.astype(BF16)
        dxc = dxc + _bd_t(drab, wa_ref) + _bd_t(diab, wx_ref)
        dwa_ref[...] += _bd_grad(p["xcb"], drab)
        dwx_ref[...] += _bd_grad(p["xcb"], diab)
        sig_neg_lam = _sigmoid(-pv[ROW_LAM:ROW_LAM + 1, :])
        d_lam = jnp.sum(dla * r, axis=0, keepdims=True) * (LRU_C * sig_neg_lam)

        f_x[pl.ds(0, tm), :] = dxc
        f_x[pl.ds(tm, 8), :] = cx[...]
        du_lru = jnp.zeros((tm, LRU_W), F32)
        u_lru = u_ref[:, 0:LRU_W]
        d_cw = []
        for k in range(4):
            later = f_x[pl.ds(3 - k, tm), :]
            du_lru = du_lru + later * pv[ROW_CW + k:ROW_CW + k + 1, :]
            d_cw.append(jnp.sum(later * u_lru, axis=0, keepdims=True))
        du_ref[:, 0:LRU_W] = du_lru.astype(BF16)
        cx[...] = f_x[pl.ds(0, 8), :]

        zp = p["zp"]
        ps = pv[ROW_PS:ROW_PS + 1, :]
        y_pool = zp * ps
        rstd_p = _rstd(y_pool)
        dy_pool, d_gain_p = _rms_bwd(dyn[:, LRU_W:D_MODEL], y_pool * rstd_p, rstd_p, pv[ROW_GP:ROW_GP + 1, :])
        dz = dy_pool * ps
        dzb = dz.astype(BF16)
        dwp_ref[...] += _bd_grad(p["pooled_b"], dzb)
        dpooled = _bd_t(dzb, wp_ref)
        for g, w in enumerate(POOL_WINDOWS):
            f_p[pl.ds(0, tm), pl.ds(128 * g, 128)] = dpooled[:, 128 * g:128 * (g + 1)] * p["inv_cnts"][g]
        f_p[pl.ds(tm, HALO), :] = cp[...]
        for g, w in enumerate(POOL_WINDOWS):
            acc = _window_sum(f_p[:, pl.ds(128 * g, 128)], w, back=False)[0:tm, :]
            du_ref[:, 2 * LRU_W + 128 * g:2 * LRU_W + 128 * (g + 1)] = (
                acc - dpooled[:, 128 * g:128 * (g + 1)]).astype(BF16)
        cp[...] = f_p[pl.ds(0, HALO), :]

        rows = d_cw + [
            jnp.sum(dxc, axis=0, keepdims=True),
            jnp.sum(dra, axis=0, keepdims=True),
            jnp.sum(dia, axis=0, keepdims=True),
            d_lam,
            jnp.sum(dz, axis=0, keepdims=True),
            jnp.sum(dy_pool * zp, axis=0, keepdims=True),
            d_gain_l, d_gain_p,
            jnp.zeros((4, LRU_W), F32),
        ]
        vacc_ref[...] += jnp.concatenate(rows, axis=0)

        @pl.when(s == n_t - 1)
        def _():
            sg_ref[SG_VEC:SG_VEC + 16, :] = vacc_ref[:, 0:256]
            sg_ref[SG_VEC + 16:SG_VEC + 32, :] = vacc_ref[:, 256:512]
            for half in range(2):
                sg_ref[SG_WA + 64 * half:SG_WA + 64 * (half + 1), :] = _diag_pack(dwa_ref[half], 64)
                sg_ref[SG_WX + 64 * half:SG_WX + 64 * (half + 1), :] = _diag_pack(dwx_ref[half], 64)
                sg_ref[SG_WP + 128 * half:SG_WP + 128 * (half + 1), :] = _diag_pack(dwp_ref[half], 128)
            if exchange:
                exchange.finish()

    rev = lambda w: pl.BlockSpec((tm, w), lambda s: (n_t - 1 - s, 0))
    full = lambda shape: pl.BlockSpec(shape, lambda s: (0,) * len(shape))
    outs = pl.pallas_call(
        body, name="mixer_bwd", grid=(n_t,),
        in_specs=[rev(D_MODEL), rev(D_IN),
                  pl.BlockSpec((HALO, POOL_W), lambda s: (jnp.maximum((n_t - 1 - s) * (tm // HALO) - 1, 0), 2)),
                  rev(LRU_W),
                  pl.BlockSpec((8, LRU_W), lambda s: (jnp.maximum((n_t - 1 - s) * (tm // 8) - 1, 0), 0)),
                  rev(4 * LRU_W), full((16, LRU_W)), full((8, 64, 64)), full((8, 64, 64)), full((4, 128, 128)),
                  full((D_MODEL, D_MODEL))] + [ANY] * n_x,
        out_specs=[rev(D_IN), full((SG_ROWS, 256))] + [ANY] * n_x,
        out_shape=[_sds((T, D_IN), BF16), _sds((SG_ROWS, 256), F32)] + [_sds(a.shape, a.dtype) for a in chip_sums],
        scratch_shapes=[pltpu.VMEM((HALO + tm, POOL_W), F32),
                        pltpu.VMEM((8 + tm, LRU_W), F32), pltpu.VMEM((tm, LRU_W), F32), pltpu.VMEM((tm, LRU_W), F32),
                        pltpu.VMEM((tm + 8, LRU_W), F32), pltpu.VMEM((tm + 8, LRU_W), F32),
                        pltpu.VMEM((tm + HALO, POOL_W), F32), pltpu.VMEM((8, LRU_W), F32),
                        pltpu.VMEM((8, LRU_W), F32), pltpu.VMEM((HALO, POOL_W), F32)]
        + [pltpu.VMEM((2, 256, 256), BF16)] * 3 + [pltpu.VMEM((16, LRU_W), F32)] + [pltpu.VMEM((2, 256, 256), F32)] * 3
        + (_ChipExchange.scratch(n_x) if n_x else []),
        compiler_params=pltpu.CompilerParams(dimension_semantics=("arbitrary",)),
    )(d2, u, u, hs, hs, saved, pv, wa, wx, wp, w_out_b, *chip_sums)
    return outs[0], outs[1], list(outs[2:])


def _mix_in_bwd(du, x, d2, w_in_t, g_mix, tm, chip_sums=()):
    T = x.shape[0]
    n_t = T // tm
    n_x = len(chip_sums)

    def body(du_ref, x_ref, d2_ref, w_ref, g_ref, *rest):
        x_in, rest = rest[:n_x], rest[n_x:]
        dx_ref, dg_ref = rest[:2]
        exchange = _ChipExchange(x_in, rest[2:2 + n_x], *rest[2 + n_x:]) if n_x else None
        i = pl.program_id(0)

        @pl.when(i == 0)
        def _():
            dg_ref[...] = jnp.zeros((1, D_MODEL), F32)
            if exchange:
                exchange.start()

        dh = jnp.dot(du_ref[...], w_ref[...], preferred_element_type=F32)
        xv = x_ref[...]
        rstd = _rstd(xv)
        dx, dgain = _rms_bwd(dh, xv * rstd, rstd, g_ref[...])
        dx_ref[...] = d2_ref[...] + dx
        dg_ref[...] += dgain

        if exchange:
            @pl.when(i == n_t - 1)
            def _():
                exchange.finish()

    row = lambda w: pl.BlockSpec((tm, w), lambda i: (i, 0))
    const = lambda shape: pl.BlockSpec(shape, lambda i: (0,) * len(shape))
    outs = pl.pallas_call(
        body, name="mix_in_bwd", grid=(n_t,),
        in_specs=[row(D_IN), row(D_MODEL), row(D_MODEL), const((D_IN, D_MODEL)), const((1, D_MODEL))] + [ANY] * n_x,
        out_specs=[row(D_MODEL), const((1, D_MODEL))] + [ANY] * n_x,
        out_shape=[_sds((T, D_MODEL), F32), _sds((1, D_MODEL), F32)] + [_sds(a.shape, a.dtype) for a in chip_sums],
        scratch_shapes=_ChipExchange.scratch(n_x) if n_x else [],
        compiler_params=pltpu.CompilerParams(dimension_semantics=("arbitrary",)),
    )(du, x, d2, w_in_t, g_mix, *chip_sums)
    return outs[0], outs[1], list(outs[2:])


def _pair_sum(g, r1, c_arr, name):
    _, _, R, C = g.shape
    tr = R if R <= 512 else 256

    def body(c_ref, g_ref, r_ref, o_ref):
        o_ref[...] = (g_ref[...] + r_ref[...]).astype(BF16)

    return pl.pallas_call(
        body, name=name,
        grid_spec=pltpu.PrefetchScalarGridSpec(
            num_scalar_prefetch=1, grid=(4, R // tr),
            in_specs=[pl.BlockSpec((None, None, tr, C), lambda j, i, c_ref: (j, c_ref[0], i, 0)),
                      pl.BlockSpec((None, tr, C), lambda j, i, c_ref: (j, i, 0))],
            out_specs=pl.BlockSpec((None, tr, C), lambda j, i, c_ref: (j, i, 0))),
        out_shape=_sds((4, R, C), BF16),
    )(c_arr, g, r1)


def _adamw(w, g, m, v):
    m = ADAM_B1 * m + (1.0 - ADAM_B1) * g
    v = ADAM_B2 * v + (1.0 - ADAM_B2) * (g * g)
    m_hat = m / (1.0 - ADAM_B1 ** ADAM_STEP)
    v_hat = v / (1.0 - ADAM_B2 ** ADAM_STEP)
    delta = -ADAM_LR * (m_hat / (jnp.sqrt(v_hat) + ADAM_EPS) + ADAM_WD * w)
    return delta, m, v


def _adam_shard(w, m, v, parts, name):
    R, C = w.shape
    tr = R if R <= 512 else 256

    def body(w_ref, m_ref, v_ref, p_ref, g_ref, d_ref, nm_ref, nv_ref):
        g = p_ref[0].astype(F32)
        for j in range(1, 4):
            g = g + p_ref[j].astype(F32)
        delta, nm, nv = _adamw(w_ref[...], g, m_ref[...], v_ref[...])
        g_ref[...] = g
        d_ref[...] = delta
        nm_ref[...] = nm
        nv_ref[...] = nv

    blk = pl.BlockSpec((tr, C), lambda i: (i, 0))
    return pl.pallas_call(
        body, name=name, grid=(R // tr,),
        in_specs=[blk, blk, blk, pl.BlockSpec((4, tr, C), lambda i: (0, i, 0))],
        out_specs=[blk] * 4, out_shape=[_sds((R, C), F32)] * 4,
        compiler_params=pltpu.CompilerParams(dimension_semantics=("parallel",)),
    )(w, m, v, parts)


SMALL_PARAMS = [("norm_mix_g", (1, D_MODEL)), ("conv_w", (1, 4, 64)), ("conv_b", (1, LRU_W)),
                ("gate_a_w", (1, 8, 64, 64)), ("gate_a_b", (1, LRU_W)), ("gate_x_w", (1, 8, 64, 64)),
                ("gate_x_b", (1, LRU_W)), ("lru_lambda", (1, LRU_W)), ("pool_w", (1, 4, 128, 128)),
                ("pool_b", (1, POOL_W)), ("pool_scale", (1, POOL_W)), ("norm_lru_g", (1, LRU_W)),
                ("norm_pool_g", (1, POOL_W)), ("norm_ffn_g", (1, D_MODEL)), ("final_norm_g", (1, D_MODEL))]
VEC_ROW = dict(conv_b=ROW_CB, gate_a_b=ROW_BA, gate_x_b=ROW_BX, lru_lambda=ROW_LAM, pool_b=ROW_PB, pool_scale=ROW_PS,
               norm_lru_g=ROW_GL, norm_pool_g=ROW_GP)
WHOLE = (Ellipsis,)


def _unpack_mixer_grads(sg, dev):
    vec = jnp.concatenate([sg[SG_VEC:SG_VEC + 16], sg[SG_VEC + 16:SG_VEC + 32]], axis=1)
    out = {nm: [(WHOLE, vec[r:r + 1])] for nm, r in VEC_ROW.items()}
    own = jnp.zeros((4, 64), F32)
    for d in range(N_DEV):
        own = jnp.where(dev == d, vec[ROW_CW:ROW_CW + 4, 64 * d:64 * (d + 1)], own)
    out["conv_w"] = [((0,), own)]
    for nm, row0 in (("gate_a_w", SG_WA), ("gate_x_w", SG_WX)):
        out[nm] = [((0, b), sg[row0 + 64 * (b // 4):row0 + 64 * (b // 4 + 1), 64 * (b % 4):64 * (b % 4 + 1)])
                   for b in range(8)]
    out["pool_w"] = [((0, b), sg[SG_WP + 128 * (b // 2):SG_WP + 128 * (b // 2 + 1), 128 * (b % 2):128 * (b % 2 + 1)])
                     for b in range(4)]
    return out


def _adam_small(parts, w, m, v):
    names = [nm for nm, _ in SMALL_PARAMS]
    n = len(names)

    def body(sg_ref, gm_ref, gf_ref, gn_ref, ls_ref, *rest):
        w_refs, m_refs, v_refs, outs = rest[:n], rest[n:2 * n], rest[2 * n:3 * n], rest[3 * n:]
        dev = 4 * lax.axis_index("x") + 2 * lax.axis_index("y") + lax.axis_index("c")

        def total(ref):
            acc = ref[0]
            for d in range(1, N_DEV):
                acc = acc + ref[d]
            return acc

        pieces = _unpack_mixer_grads(total(sg_ref), dev)
        pieces["norm_mix_g"] = [(WHOLE, total(gm_ref))]
        pieces["norm_ffn_g"] = [(WHOLE, total(gf_ref))]
        pieces["final_norm_g"] = [(WHOLE, total(gn_ref))]
        for i, nm in enumerate(names):
            for idx, g in pieces[nm]:
                delta, new_m, new_v = _adamw(w_refs[i][idx], g, m_refs[i][idx], v_refs[i][idx])
                for kind, val in enumerate((g, delta, new_m, new_v)):
                    outs[4 * i + kind][idx] = val
        outs[4 * n][...] = total(ls_ref)

    shapes = [_sds(shape, F32) for _, shape in SMALL_PARAMS for _ in range(4)] + [_sds((8, 128), F32)]
    res = pl.pallas_call(body, name="adam_small", out_shape=shapes)(
        *parts, *[w[nm] for nm in names], *[m[nm] for nm in names], *[v[nm] for nm in names])
    return {nm: tuple(res[4 * i:4 * i + 4]) for i, nm in enumerate(names)}, res[4 * n][0, 0]


def _vec_rows(conv_w_full, conv_b, ba, bx, lam, pb, ps, gl, gp):
    return jnp.concatenate([conv_w_full, conv_b, ba, bx, lam, pb, ps, gl, gp, jnp.zeros((4, LRU_W), F32)], axis=0)


WEIGHT_ORDER = ['norm_mix_g', 'w_in', 'conv_w', 'conv_b', 'gate_a_w', 'gate_a_b', 'gate_x_w', 'gate_x_b', 'lru_lambda',
                'pool_w', 'pool_b', 'pool_scale', 'norm_lru_g', 'norm_pool_g', 'w_out', 'norm_ffn_g', 'ffn_w1', 'ffn_w3',
                'ffn_w2', 'final_norm_g']


def kernel(x, norm_mix_g, w_in, conv_w, conv_b, gate_a_w, gate_a_b, gate_x_w, gate_x_b, lru_lambda, pool_w, pool_b, pool_scale, norm_lru_g, norm_pool_g, w_out, norm_ffn_g, ffn_w1, ffn_w3, ffn_w2, final_norm_g, loss_target, m_norm_mix_g, m_w_in, m_conv_w, m_conv_b, m_gate_a_w, m_gate_a_b, m_gate_x_w, m_gate_x_b, m_lru_lambda, m_pool_w, m_pool_b, m_pool_scale, m_norm_lru_g, m_norm_pool_g, m_w_out, m_norm_ffn_g, m_ffn_w1, m_ffn_w3, m_ffn_w2, m_final_norm_g, v_norm_mix_g, v_w_in, v_conv_w, v_conv_b, v_gate_a_w, v_gate_a_b, v_gate_x_w, v_gate_x_b, v_lru_lambda, v_pool_w, v_pool_b, v_pool_scale, v_norm_lru_g, v_norm_pool_g, v_w_out, v_norm_ffn_g, v_ffn_w1, v_ffn_w3, v_ffn_w2, v_final_norm_g):
    ac = lax.axis_index("c")
    tm, tmx, tn, tk = 512, 512, 1408, 1024
    tm_in = 1024
    xs, tgt = x[0], loss_target[0]
    g_fin = final_norm_g.reshape(1, D_MODEL)
    c_arr = jnp.reshape(ac, (1,)).astype(jnp.int32)

    def pair_sums(blocks, names):
        from_sibling = _pair_exchange(blocks, "grads_to_sibling_" + names[0])
        return [_pair_sum(g.reshape((4, 2) + g.shape[1:]), r, c_arr, "pair_sum_" + nm)
                for g, r, nm in zip(blocks, from_sibling, names)]

    tr = lambda w: jnp.swapaxes(w[0], 0, 1)
    own = lambda w: w[0]
    bf = lambda a: a.astype(BF16)

    g_in, g_conv = _all_gather([bf(tr(w_in)), conv_w[0]], "gather_w_in")
    w_in_t = g_in.reshape(D_IN, D_MODEL)
    conv_w_full = g_conv.transpose(1, 0, 2).reshape(4, LRU_W)
    pv = _vec_rows(conv_w_full, conv_b, gate_a_b, gate_x_b, lru_lambda, pool_b, pool_scale, norm_lru_g, norm_pool_g)
    wa, wx, wp = gate_a_w[0], gate_x_w[0], pool_w[0]

    u, h1, (g_out, g_w1) = _mix_in(xs, norm_mix_g, w_in_t, tm_in, shards=[bf(own(w_out)), bf(tr(ffn_w1))])
    w_out_b = g_out.reshape(D_MODEL, D_MODEL)
    y, hs, hres, h2, saved, (g_w3, g_w2) = _mixer_fwd(u, xs, pv, wa, wx, wp, w_out_b, norm_ffn_g, tmx,
                                               shards=[bf(tr(ffn_w3)), bf(own(ffn_w2))])
    w1_t, w3_t, w2_b = g_w1.reshape(D_FF, D_MODEL), g_w3.reshape(D_FF, D_MODEL), g_w2.reshape(D_FF, D_MODEL)
    g, v, d3, loss_acc, d_gfin = _ffn_fwd(hres, h2, w1_t, w3_t, w2_b, g_fin, tgt, tm, tn)

    dg, dv, ff, d2, d_gffn = _ffn_bwd(d3, g, v, w1_t, w3_t, w2_b, hres, norm_ffn_g, tm, tn)
    blocks = lambda a: a.reshape(N_DEV, a.shape[0] // N_DEV, a.shape[1])
    chips = lambda a: a.reshape(4, a.shape[0] // 4, a.shape[1])
    early_sums = [chips(_at_b_pair(y, d2, c_arr, "grad_w_out", tk)), chips(_at_b_pair(dg, h2, c_arr, "grad_w1", tk)),
                  chips(_at_b_pair(dv, h2, c_arr, "grad_w3", tk)), chips(_at_b_pair(ff, d3, c_arr, "grad_w2", tk))]
    du, d_mixer, early_parts = _mixer_bwd(d2, u, hs, saved, pv, wa, wx, wp, w_out_b, tmx, chip_sums=early_sums)
    grad_x, d_gmix, _ = _mix_in_bwd(du, xs, d2, w_in_t, norm_mix_g, tm_in)
    d_win, small_parts = _at_b(du, h1, "grad_w_in", D_IN, D_MODEL, tk,
                               gather=[d_mixer, d_gmix, d_gffn, d_gfin, loss_acc])
    win_parts = _chip_exchange(pair_sums([blocks(d_win)], ["w_in"]), "grads_to_chips_w_in")
    parts = list(win_parts) + list(early_parts)

    res = {}
    shard_w = dict(w_in=(w_in, m_w_in, v_w_in, tr), w_out=(w_out, m_w_out, v_w_out, own),
                   ffn_w1=(ffn_w1, m_ffn_w1, v_ffn_w1, tr), ffn_w3=(ffn_w3, m_ffn_w3, v_ffn_w3, tr),
                   ffn_w2=(ffn_w2, m_ffn_w2, v_ffn_w2, own))
    for (nm, (w, m, v, view)), p in zip(shard_w.items(), parts):
        outs = _adam_shard(view(w), view(m), view(v), p, "adam_" + nm)
        res[nm] = [(jnp.swapaxes(o, 0, 1) if view is tr else o)[None] for o in outs]

    row = lambda a: a.reshape(1, D_MODEL)
    small = lambda gm, cw, cb, wa_, ba, wx_, bx, lam, pw, pb, ps, gl, gp, gf, gn: dict(
        norm_mix_g=gm, conv_w=cw, conv_b=cb, gate_a_w=wa_, gate_a_b=ba, gate_x_w=wx_, gate_x_b=bx, lru_lambda=lam,
        pool_w=pw, pool_b=pb, pool_scale=ps, norm_lru_g=gl, norm_pool_g=gp, norm_ffn_g=gf, final_norm_g=row(gn))
    small_res, loss = _adam_small(
        small_parts,
        small(norm_mix_g, conv_w, conv_b, gate_a_w, gate_a_b, gate_x_w, gate_x_b, lru_lambda, pool_w, pool_b,
              pool_scale, norm_lru_g, norm_pool_g, norm_ffn_g, final_norm_g),
        small(m_norm_mix_g, m_conv_w, m_conv_b, m_gate_a_w, m_gate_a_b, m_gate_x_w, m_gate_x_b, m_lru_lambda, m_pool_w,
              m_pool_b, m_pool_scale, m_norm_lru_g, m_norm_pool_g, m_norm_ffn_g, m_final_norm_g),
        small(v_norm_mix_g, v_conv_w, v_conv_b, v_gate_a_w, v_gate_a_b, v_gate_x_w, v_gate_x_b, v_lru_lambda, v_pool_w,
              v_pool_b, v_pool_scale, v_norm_lru_g, v_norm_pool_g, v_norm_ffn_g, v_final_norm_g))
    for nm, outs in small_res.items():
        res[nm] = [o.reshape(D_MODEL) for o in outs] if nm == "final_norm_g" else list(outs)

    out = [loss, grad_x[None]]
    for kind in range(4):
        out += [res[nm][kind] for nm in WEIGHT_ORDER]
    return tuple(out)
```

```python
import jax
import jax.numpy as jnp
from jax import lax
from jax.experimental import pallas as pl
from jax.experimental.pallas import tpu as pltpu

F32 = jnp.float32
BF16 = jnp.bfloat16

D_MODEL = 1024
LRU_W = 512
POOL_W = 512
D_IN = 1536
D_FF = 2816
POOL_WINDOWS = (2, 4, 8, 16)
EPS = 1e-6
LRU_C = 8.0
N_DEV = 8
HALO = 16
SCAN_UNROLL = 4
ADAM_ROWS = 32

ADAM_LR = 0.001
ADAM_B1 = 0.9
ADAM_B2 = 0.999
ADAM_EPS = 1e-08
ADAM_WD = 0.01
ADAM_STEP = 10

ROW_CW, ROW_CB, ROW_BA, ROW_BX, ROW_LAM, ROW_PB, ROW_PS, ROW_GL, ROW_GP = 0, 4, 5, 6, 7, 8, 9, 10, 11
SG_VEC, SG_WA, SG_WX, SG_WP, SG_ROWS = 0, 32, 160, 288, 544

NT = (((1,), (1,)), ((), ()))
TN = (((0,), (0,)), ((), ()))


def _sds(shape, dtype):
    return jax.ShapeDtypeStruct(shape, dtype)


def _sigmoid(x):
    return 0.5 * jnp.tanh(0.5 * x) + 0.5


def _gelu_parts(x):
    c = 0.7978845608028654
    inner = c * (x + 0.044715 * (x * x * x))
    th = jnp.tanh(inner)
    g = 0.5 * x * (1.0 + th)
    dg = 0.5 * (1.0 + th) + 0.5 * x * (1.0 - th * th) * (c * (1.0 + 3.0 * 0.044715 * (x * x)))
    return g, dg


def _window_sum(ext, w, back):
    n = ext.shape[0]
    s, k = ext, 1
    while k < w:
        s = s + pltpu.roll(s, k if back else n - k, 0)
        k *= 2
    return s


def _rstd(x):
    return lax.rsqrt(jnp.mean(x * x, axis=-1, keepdims=True) + EPS)


def _rms_bwd(dy, xhat, rstd, gain):
    dxh = dy * gain
    dx = rstd * (dxh - xhat * jnp.mean(dxh * xhat, axis=-1, keepdims=True))
    return dx, jnp.sum(dy * xhat, axis=0, keepdims=True)


def _bd(xb, w_ref):
    return jnp.concatenate(
        [jnp.dot(xb[:, :256], w_ref[0], preferred_element_type=F32),
         jnp.dot(xb[:, 256:], w_ref[1], preferred_element_type=F32)], axis=1)


def _bd_t(xb, w_ref):
    return jnp.concatenate(
        [lax.dot_general(xb[:, :256], w_ref[0], NT, preferred_element_type=F32),
         lax.dot_general(xb[:, 256:], w_ref[1], NT, preferred_element_type=F32)], axis=1)


def _bd_grad(xb, db):
    return jnp.stack(
        [lax.dot_general(xb[:, :256], db[:, :256], TN, preferred_element_type=F32),
         lax.dot_general(xb[:, 256:], db[:, 256:], TN, preferred_element_type=F32)], axis=0)


def _fill_block_diag(dst, src_ref):
    n, k, _ = src_ref.shape
    dst[...] = jnp.zeros(dst.shape, BF16)
    for b in range(n):
        p, q = divmod(b, 256 // k)
        dst[p, q * k:(q + 1) * k, q * k:(q + 1) * k] = src_ref[b].astype(BF16)


def _diag_pack(w, k):
    lane = lax.broadcasted_iota(jnp.int32, (k, 256), 1)
    out = w[0:k]
    for q in range(1, 256 // k):
        out = jnp.where(lane >= q * k, w[q * k:(q + 1) * k], out)
    return out


def _y_pos(b):
    return 4 * (b % 2) + b // 2


def _softplus_neg_lambda(pv):
    z = -pv[ROW_LAM:ROW_LAM + 1, :]
    return jnp.maximum(z, 0.0) + jnp.log(1.0 + jnp.exp(-jnp.abs(z)))


def _lru_gates(e_lru, pv, wa_ref, wx_ref, tm):
    xc = pv[ROW_CB:ROW_CB + 1, :]
    for k in range(4):
        xc = xc + e_lru[pl.ds(HALO - 3 + k, tm), :] * pv[ROW_CW + k:ROW_CW + k + 1, :]
    xcb = xc.astype(BF16)
    r = _sigmoid(_bd(xcb, wa_ref) + pv[ROW_BA:ROW_BA + 1, :])
    ig = _sigmoid(_bd(xcb, wx_ref) + pv[ROW_BX:ROW_BX + 1, :])
    return xc, r, ig, (-LRU_C * r) * _softplus_neg_lambda(pv)


def _lru_decay(la):
    a = jnp.exp(la)
    om = -jnp.tanh(la) * (1.0 + a * a)
    omc = jnp.maximum(om, 1e-12)
    rmult = lax.rsqrt(omc)
    return a, om, omc * rmult, rmult


def _pool_pre(e_pool, pv, wp_ref, tm, t0):
    t = t0 + lax.broadcasted_iota(jnp.int32, (tm, 1), 0)
    parts, inv_cnts = [], []
    for g, w in enumerate(POOL_WINDOWS):
        ext = e_pool[:, pl.ds(128 * g, 128)]
        s = _window_sum(ext, w, back=True)[HALO:, :]
        inv_cnt = 1.0 / jnp.minimum(t + 1, w).astype(F32)
        inv_cnts.append(inv_cnt)
        parts.append(s * inv_cnt - ext[HALO:, :])
    pooled = jnp.concatenate(parts, axis=1)
    pooled_b = pooled.astype(BF16)
    zp = _bd(pooled_b, wp_ref) + pv[ROW_PB:ROW_PB + 1, :]
    return pooled_b, zp, inv_cnts


def _scan_tile(a_ref, b_ref, out_ref, carry, tm, reverse):
    row = lax.broadcasted_iota(jnp.int32, (8, LRU_W), 0)
    nblk = tm // 8

    def local_scan(blk):
        r0 = pl.multiple_of(blk * 8, 8)
        av = a_ref[pl.ds(r0, 8), :]
        bv = b_ref[pl.ds(r0, 8), :]
        for d in (1, 2, 4):
            sh = (8 - d) if reverse else d
            a_s = pltpu.roll(av, sh, 0)
            b_s = pltpu.roll(bv, sh, 0)
            m = (row < 8 - d) if reverse else (row >= d)
            bv = jnp.where(m, av * b_s + bv, bv)
            av = jnp.where(m, av * a_s, av)
        return r0, av, bv

    def step(i, hin):
        local = [local_scan((nblk - 1 - (i * SCAN_UNROLL + j)) if reverse else (i * SCAN_UNROLL + j))
                 for j in range(SCAN_UNROLL)]
        for r0, av, bv in local:
            hv = av * hin + bv
            out_ref[pl.ds(r0, 8), :] = hv
            hin = jnp.broadcast_to(hv[0:1, :] if reverse else hv[7:8, :], (8, LRU_W))
        return hin

    return lax.fori_loop(0, nblk // SCAN_UNROLL, step, carry)


MESH = pl.DeviceIdType.MESH
ANY = pl.BlockSpec(memory_space=pl.ANY)


def _place():
    x, y, c = lax.axis_index("x"), lax.axis_index("y"), lax.axis_index("c")
    chips = [(1 - x, y), (x, 1 - y), (1 - x, 1 - y)]
    return x, y, c, chips


class _Gather:
    def __init__(self, ins, outs, send_sems, recv_sems, local_sems, core_major=False):
        self.ins, self.outs, self.n = ins, outs, len(ins)
        self.send_sems, self.recv_sems, self.local_sems = send_sems, recv_sems, local_sems
        self.core_major = core_major

    @staticmethod
    def scratch(n):
        return [pltpu.SemaphoreType.DMA((7, n)), pltpu.SemaphoreType.DMA((7, n)), pltpu.SemaphoreType.DMA((n,))]

    def _slot(self, a, px, py, pc):
        return self.outs[a].at[4 * pc + 2 * px + py if self.core_major else 4 * px + 2 * py + pc]

    def _copy(self, a, k, block, to, src=None):
        return pltpu.make_async_remote_copy(
            src_ref=self._slot(a, *block) if src is None else src, dst_ref=self._slot(a, *block),
            send_sem=self.send_sems.at[k, a], recv_sem=self.recv_sems.at[k, a], device_id=to, device_id_type=MESH)

    def _mine(self, a):
        x, y, c, _ = _place()
        return pltpu.make_async_copy(self.ins[a], self._slot(a, x, y, c), self.local_sems.at[a])

    def _first(self, a):
        x, y, c, chips = _place()
        me = (x, y, c)
        return ([self._copy(a, 0, me, (x, y, 1 - c), src=self.ins[a])]
                + [self._copy(a, 1 + j, me, (*chip, c), src=self.ins[a]) for j, chip in enumerate(chips)])

    def start(self):
        for a in range(self.n):
            self._mine(a).start()
        for a in range(self.n):
            for cp in self._first(a):
                cp.start()

    def finish(self):
        x, y, c, chips = _place()
        me, sibling = (x, y, c), (x, y, 1 - c)
        passed = []
        for j, chip in enumerate(chips):
            for a in range(self.n):
                self._copy(a, 1 + j, (*chip, c), me).wait_recv()
                fwd = self._copy(a, 4 + j, (*chip, c), sibling)
                fwd.start()
                passed.append(fwd)
        for a in range(self.n):
            self._copy(a, 0, (x, y, 1 - c), me).wait_recv()
            for j, chip in enumerate(chips):
                self._copy(a, 4 + j, (*chip, 1 - c), me).wait_recv()
        for a in range(self.n):
            for cp in self._first(a):
                cp.wait_send()
        for cp in passed:
            cp.wait_send()
        for a in range(self.n):
            self._mine(a).wait()


def _all_gather(arrs, name):
    n = len(arrs)

    def body(*refs):
        g = _Gather(refs[:n], refs[n:2 * n], *refs[2 * n:])
        g.start()
        g.finish()

    return pl.pallas_call(
        body, name=name,
        out_shape=[_sds((N_DEV,) + a.shape, a.dtype) for a in arrs],
        in_specs=[ANY] * n, out_specs=[ANY] * n, scratch_shapes=_Gather.scratch(n),
    )(*arrs)


def _pair_exchange(arrs, name):
    n = len(arrs)

    def body(*refs):
        ins, outs = refs[:n], refs[n:2 * n]
        send_sems, recv_sems = refs[2 * n:]
        x, y, c, _ = _place()
        sibling = (x, y, 1 - c)
        sends = []
        for a in range(n):
            for j in range(4):
                cp = pltpu.make_async_remote_copy(
                    src_ref=ins[a].at[2 * j + (1 - c)], dst_ref=outs[a].at[j],
                    send_sem=send_sems.at[j, a], recv_sem=recv_sems.at[j, a], device_id=sibling, device_id_type=MESH)
                cp.start()
                sends.append(cp)
        for cp in sends:
            cp.wait()

    return pl.pallas_call(
        body, name=name,
        out_shape=[_sds((4,) + a.shape[1:], a.dtype) for a in arrs],
        in_specs=[ANY] * n, out_specs=[ANY] * n,
        scratch_shapes=[pltpu.SemaphoreType.DMA((4, n)), pltpu.SemaphoreType.DMA((4, n))],
    )(*arrs)


class _ChipExchange:
    def __init__(self, ins, outs, send_sems, recv_sems, local_sems):
        self.ins, self.outs, self.n = ins, outs, len(ins)
        self.send_sems, self.recv_sems, self.local_sems = send_sems, recv_sems, local_sems

    @staticmethod
    def scratch(n):
        return [pltpu.SemaphoreType.DMA((3, n)), pltpu.SemaphoreType.DMA((3, n)), pltpu.SemaphoreType.DMA((n,))]

    def _local(self, a):
        x, y, _, _ = _place()
        me = 2 * x + y
        return pltpu.make_async_copy(self.ins[a].at[me], self.outs[a].at[me], self.local_sems.at[a])

    def _copies(self, a):
        x, y, c, chips = _place()
        me = 2 * x + y
        return [(pltpu.make_async_remote_copy(
                     src_ref=self.ins[a].at[2 * px + py], dst_ref=self.outs[a].at[me],
                     send_sem=self.send_sems.at[k, a], recv_sem=self.recv_sems.at[k, a],
                     device_id=(px, py, c), device_id_type=MESH),
                 pltpu.make_async_remote_copy(
                     src_ref=self.ins[a].at[me], dst_ref=self.outs[a].at[2 * px + py],
                     send_sem=self.send_sems.at[k, a], recv_sem=self.recv_sems.at[k, a],
                     device_id=(px, py, c), device_id_type=MESH))
                for k, (px, py) in enumerate(chips)]

    def start(self):
        for a in range(self.n):
            self._local(a).start()
        for a in range(self.n):
            for send, _ in self._copies(a):
                send.start()

    def finish(self):
        for a in range(self.n):
            for send, recv in self._copies(a):
                send.wait_send()
                recv.wait_recv()
        for a in range(self.n):
            self._local(a).wait()


def _chip_exchange(arrs, name):
    n = len(arrs)

    def body(*refs):
        e = _ChipExchange(refs[:n], refs[n:2 * n], *refs[2 * n:])
        e.start()
        e.finish()

    return pl.pallas_call(
        body, name=name, out_shape=[_sds(a.shape, a.dtype) for a in arrs],
        in_specs=[ANY] * n, out_specs=[ANY] * n, scratch_shapes=_ChipExchange.scratch(n),
    )(*arrs)


def _mix_in(x, g_mix, w_in_t, tm, shards=()):
    T = x.shape[0]
    n_t = T // tm
    n_s = len(shards)

    def body(x_ref, g_ref, w_ref, *rest):
        sh_in, rest = rest[:n_s], rest[n_s:]
        u_ref, h_ref = rest[:2]
        gather = _Gather(sh_in, rest[2:2 + n_s], *rest[2 + n_s:], core_major=True) if n_s else None
        i = pl.program_id(0)

        if gather:
            @pl.when(i == 0)
            def _():
                gather.start()

        xv = x_ref[...]
        h = (xv * _rstd(xv) * g_ref[...]).astype(BF16)
        h_ref[...] = h
        u_ref[...] = lax.dot_general(h, w_ref[...], NT, preferred_element_type=F32)

        if gather:
            @pl.when(i == n_t - 1)
            def _():
                gather.finish()

    outs = pl.pallas_call(
        body, name="mix_in", grid=(n_t,),
        in_specs=[pl.BlockSpec((tm, D_MODEL), lambda i: (i, 0)),
                  pl.BlockSpec((1, D_MODEL), lambda i: (0, 0)),
                  pl.BlockSpec((D_IN, D_MODEL), lambda i: (0, 0))] + [ANY] * n_s,
        out_specs=[pl.BlockSpec((tm, D_IN), lambda i: (i, 0)),
                   pl.BlockSpec((tm, D_MODEL), lambda i: (i, 0))] + [ANY] * n_s,
        out_shape=[_sds((T, D_IN), F32), _sds((T, D_MODEL), BF16)] + [_sds((N_DEV,) + a.shape, a.dtype) for a in shards],
        scratch_shapes=_Gather.scratch(n_s) if n_s else [],
        compiler_params=pltpu.CompilerParams(dimension_semantics=("arbitrary",)),
    )(x, g_mix, w_in_t, *shards)
    return outs[0], outs[1], list(outs[2:])


def _mixer_fwd(u, x, pv, wa, wx, wp, w_out_b, g_ffn, tm, shards=()):
    T = u.shape[0]
    n_s = len(shards)
    n_t = T // tm

    def body(u_ref, x_ref, pv_ref, wa_in, wx_in, wp_in, wo_ref, gf_ref, *rest):
        sh_in, rest = rest[:n_s], rest[n_s:]
        y_ref, hs_ref, hres_ref, h2_ref, saved_ref = rest[:5]
        sh_out, rest = rest[5:5 + n_s], rest[5 + n_s:]
        e_lru, e_pool, a_s, b_s, hc, wa_ref, wx_ref, wp_ref = rest[:8]
        gather = _Gather(sh_in, sh_out, *rest[8:], core_major=True) if n_s else None
        i = pl.program_id(0)

        @pl.when(i == 0)
        def _():
            if gather:
                gather.start()
            e_lru[pl.ds(0, HALO), :] = jnp.zeros((HALO, LRU_W), F32)
            e_pool[pl.ds(0, HALO), :] = jnp.zeros((HALO, POOL_W), F32)
            hc[...] = jnp.zeros((8, LRU_W), F32)
            _fill_block_diag(wa_ref, wa_in)
            _fill_block_diag(wx_ref, wx_in)
            _fill_block_diag(wp_ref, wp_in)

        e_lru[pl.ds(HALO, tm), :] = u_ref[:, 0:LRU_W]
        e_pool[pl.ds(HALO, tm), :] = u_ref[:, 2 * LRU_W:D_IN]
        pv = pv_ref[...]
        xc, r, ig, la = _lru_gates(e_lru, pv, wa_ref, wx_ref, tm)
        for q, val in enumerate((xc, r, ig, la)):
            saved_ref[:, LRU_W * q:LRU_W * (q + 1)] = val
        a, _, mult, _ = _lru_decay(la)
        a_s[...] = a
        b_s[...] = mult * (ig * xc)
        hc[...] = _scan_tile(a_s, b_s, hs_ref, hc[...], tm, reverse=False)
        gl, _ = _gelu_parts(u_ref[:, LRU_W:2 * LRU_W])
        y_lru = hs_ref[...] * gl
        _, zp, _ = _pool_pre(e_pool, pv, wp_ref, tm, i * tm)
        y_pool = zp * pv[ROW_PS:ROW_PS + 1, :]
        yn = jnp.concatenate([y_lru * _rstd(y_lru) * pv[ROW_GL:ROW_GL + 1, :],
                              y_pool * _rstd(y_pool) * pv[ROW_GP:ROW_GP + 1, :]], axis=1).astype(BF16)
        for b in range(N_DEV):
            y_ref[:, 128 * _y_pos(b):128 * (_y_pos(b) + 1)] = yn[:, 128 * b:128 * (b + 1)]
        hr = x_ref[...] + jnp.dot(y_ref[...], wo_ref[...], preferred_element_type=F32)
        hres_ref[...] = hr
        h2_ref[...] = (hr * _rstd(hr) * gf_ref[...]).astype(BF16)
        e_lru[pl.ds(0, HALO), :] = e_lru[pl.ds(tm, HALO), :]
        e_pool[pl.ds(0, HALO), :] = e_pool[pl.ds(tm, HALO), :]

        if gather:
            @pl.when(i == n_t - 1)
            def _():
                gather.finish()

    full = lambda shape: pl.BlockSpec(shape, lambda i: (0,) * len(shape))
    row = lambda w: pl.BlockSpec((tm, w), lambda i: (i, 0))
    outs = pl.pallas_call(
        body, name="mixer_fwd", grid=(n_t,),
        in_specs=[row(D_IN), row(D_MODEL), full((16, LRU_W)), full((8, 64, 64)), full((8, 64, 64)), full((4, 128, 128)),
                  full((D_MODEL, D_MODEL)), full((1, D_MODEL))] + [ANY] * n_s,
        out_specs=[row(D_MODEL), row(LRU_W), row(D_MODEL), row(D_MODEL), row(4 * LRU_W)] + [ANY] * n_s,
        out_shape=[_sds((T, D_MODEL), BF16), _sds((T, LRU_W), F32), _sds((T, D_MODEL), F32), _sds((T, D_MODEL), BF16),
                   _sds((T, 4 * LRU_W), F32)] + [_sds((N_DEV,) + a.shape, a.dtype) for a in shards],
        scratch_shapes=[pltpu.VMEM((HALO + tm, LRU_W), F32), pltpu.VMEM((HALO + tm, POOL_W), F32),
                        pltpu.VMEM((tm, LRU_W), F32), pltpu.VMEM((tm, LRU_W), F32), pltpu.VMEM((8, LRU_W), F32)]
        + [pltpu.VMEM((2, 256, 256), BF16)] * 3 + (_Gather.scratch(n_s) if n_s else []),
        compiler_params=pltpu.CompilerParams(dimension_semantics=("arbitrary",)),
    )(u, x, pv, wa, wx, wp, w_out_b, g_ffn, *shards)
    return outs[0], outs[1], outs[2], outs[3], outs[4], list(outs[5:])


def _ffn_fwd(hres, h2, w1_b, w3_b, w2_b, g_fin, tgt, tm, tn):
    T = hres.shape[0]
    n_j = D_FF // tn

    def body(hres_ref, h2_ref, w1_ref, w3_ref, w2_ref, gfin_ref, tgt_ref,
             g_ref, v_ref, d3_ref, loss_ref, dgfin_ref, acc):
        i, j = pl.program_id(0), pl.program_id(1)

        @pl.when(j == 0)
        def _():
            acc[...] = jnp.zeros((tm, D_MODEL), F32)

        @pl.when((j == 0) & (i == 0))
        def _():
            loss_ref[...] = jnp.zeros((8, 128), F32)
            dgfin_ref[...] = jnp.zeros((1, D_MODEL), F32)

        h2 = h2_ref[...]
        g = lax.dot_general(h2, w1_ref[...], NT, preferred_element_type=F32)
        v = lax.dot_general(h2, w3_ref[...], NT, preferred_element_type=F32)
        g_ref[...] = g.astype(BF16)
        v_ref[...] = v.astype(BF16)
        ff = ((g * _sigmoid(g)) * v).astype(BF16)
        acc[...] += jnp.dot(ff, w2_ref[...], preferred_element_type=F32)

        @pl.when(j == n_j - 1)
        def _():
            h3 = hres_ref[...] + acc[...]
            rstd = _rstd(h3)
            xh = h3 * rstd
            gfin = gfin_ref[...]
            err = xh * gfin - tgt_ref[...]
            loss_ref[...] += 0.5 * jnp.sum(jnp.mean(err * err, axis=-1, keepdims=True))
            dout = err * (1.0 / D_MODEL)
            dx, dgain = _rms_bwd(dout, xh, rstd, gfin)
            d3_ref[...] = dx
            dgfin_ref[...] += dgain

    row = lambda w: pl.BlockSpec((tm, w), lambda i, j: (i, 0))
    const = lambda shape: pl.BlockSpec(shape, lambda i, j: (0,) * len(shape))
    return pl.pallas_call(
        body, name="ffn_fwd", grid=(T // tm, n_j),
        in_specs=[row(D_MODEL), row(D_MODEL),
                  pl.BlockSpec((tn, D_MODEL), lambda i, j: (j, 0)), pl.BlockSpec((tn, D_MODEL), lambda i, j: (j, 0)),
                  pl.BlockSpec((tn, D_MODEL), lambda i, j: (j, 0)), const((1, D_MODEL)), row(D_MODEL)],
        out_specs=[pl.BlockSpec((tm, tn), lambda i, j: (i, j)), pl.BlockSpec((tm, tn), lambda i, j: (i, j)),
                   row(D_MODEL), const((8, 128)), const((1, D_MODEL))],
        out_shape=[_sds((T, D_FF), BF16), _sds((T, D_FF), BF16),
                   _sds((T, D_MODEL), F32), _sds((8, 128), F32), _sds((1, D_MODEL), F32)],
        scratch_shapes=[pltpu.VMEM((tm, D_MODEL), F32)],
        compiler_params=pltpu.CompilerParams(dimension_semantics=("arbitrary", "arbitrary")),
    )(hres, h2, w1_b, w3_b, w2_b, g_fin, tgt)


def _ffn_bwd(d3, g, v, w1_b, w3_b, w2_b, hres, g_ffn, tm, tn):
    T = d3.shape[0]
    n_j = D_FF // tn

    def body(d3_ref, g_ref, v_ref, w1_ref, w3_ref, w2_ref, hres_ref, gf_ref,
             dg_ref, dv_ref, ff_ref, d2_ref, dgffn_ref, acc):
        i, j = pl.program_id(0), pl.program_id(1)

        @pl.when(j == 0)
        def _():
            acc[...] = jnp.zeros((tm, D_MODEL), F32)

        @pl.when((j == 0) & (i == 0))
        def _():
            dgffn_ref[...] = jnp.zeros((1, D_MODEL), F32)

        dff = lax.dot_general(d3_ref[...].astype(BF16), w2_ref[...], NT, preferred_element_type=F32)
        gv = g_ref[...].astype(F32)
        vv = v_ref[...].astype(F32)
        sg = _sigmoid(gv)
        sl = gv * sg
        dgb = (dff * vv * (sg * (1.0 + gv * (1.0 - sg)))).astype(BF16)
        dvb = (dff * sl).astype(BF16)
        dg_ref[...] = dgb
        dv_ref[...] = dvb
        ff_ref[...] = (sl * vv).astype(BF16)
        acc[...] += (jnp.dot(dgb, w1_ref[...], preferred_element_type=F32)
                     + jnp.dot(dvb, w3_ref[...], preferred_element_type=F32))

        @pl.when(j == n_j - 1)
        def _():
            hr = hres_ref[...]
            rstd = _rstd(hr)
            dx, dgain = _rms_bwd(acc[...], hr * rstd, rstd, gf_ref[...])
            d2_ref[...] = d3_ref[...] + dx
            dgffn_ref[...] += dgain

    row = lambda w: pl.BlockSpec((tm, w), lambda i, j: (i, 0))
    tile = pl.BlockSpec((tm, tn), lambda i, j: (i, j))
    const = lambda shape: pl.BlockSpec(shape, lambda i, j: (0,) * len(shape))
    return pl.pallas_call(
        body, name="ffn_bwd", grid=(T // tm, n_j),
        in_specs=[row(D_MODEL), tile, tile,
                  pl.BlockSpec((tn, D_MODEL), lambda i, j: (j, 0)), pl.BlockSpec((tn, D_MODEL), lambda i, j: (j, 0)),
                  pl.BlockSpec((tn, D_MODEL), lambda i, j: (j, 0)), row(D_MODEL), const((1, D_MODEL))],
        out_specs=[tile, tile, tile, row(D_MODEL), const((1, D_MODEL))],
        out_shape=[_sds((T, D_FF), BF16), _sds((T, D_FF), BF16), _sds((T, D_FF), BF16),
                   _sds((T, D_MODEL), F32), _sds((1, D_MODEL), F32)],
        scratch_shapes=[pltpu.VMEM((tm, D_MODEL), F32)],
        compiler_params=pltpu.CompilerParams(dimension_semantics=("arbitrary", "arbitrary")),
    )(d3, g, v, w1_b, w3_b, w2_b, hres, g_ffn)


def _at_b(a, b, name, tmm, tn, tk, gather=()):
    T, M = a.shape
    N = b.shape[1]
    n_m, n_n, n_k = M // tmm, N // tn, T // tk
    n_g = len(gather)

    def body(a_ref, b_ref, *rest):
        g_in, o_ref, rest = rest[:n_g], rest[n_g], rest[n_g + 1:]
        ag = _Gather(g_in, rest[:n_g], *rest[n_g:]) if n_g else None
        m, n, k = pl.program_id(0), pl.program_id(1), pl.program_id(2)

        if ag:
            @pl.when((m == 0) & (n == 0) & (k == 0))
            def _():
                ag.start()

        @pl.when(k == 0)
        def _():
            o_ref[...] = jnp.zeros((tmm, tn), F32)

        o_ref[...] += lax.dot_general(a_ref[...].astype(BF16), b_ref[...].astype(BF16), TN,
                                      preferred_element_type=F32)

        if ag:
            @pl.when((m == n_m - 1) & (n == n_n - 1) & (k == n_k - 1))
            def _():
                ag.finish()

    outs = pl.pallas_call(
        body, name=name, grid=(n_m, n_n, n_k),
        in_specs=[pl.BlockSpec((tk, tmm), lambda m, n, k: (k, m)), pl.BlockSpec((tk, tn), lambda m, n, k: (k, n))]
        + [ANY] * n_g,
        out_specs=[pl.BlockSpec((tmm, tn), lambda m, n, k: (m, n))] + [ANY] * n_g,
        out_shape=[_sds((M, N), F32)] + [_sds((N_DEV,) + g.shape, g.dtype) for g in gather],
        scratch_shapes=_Gather.scratch(n_g) if n_g else [],
        compiler_params=pltpu.CompilerParams(
            dimension_semantics=("arbitrary",) * 3 if n_g else ("parallel", "parallel", "arbitrary")),
    )(a, b, *gather)
    return (outs[0], list(outs[1:])) if n_g else outs[0]


def _at_b_pair(a, b, c_arr, name, tk):
    T, M = a.shape
    N = b.shape[1]
    hm, n_k = M // 2, T // tk

    def body(c_ref, a_ref, b_ref, o_ref, acc, landed, send_sem, recv_sem):
        ph, k = pl.program_id(0), pl.program_id(1)
        def hand_over():
            x, y, c, _ = _place()
            return pltpu.make_async_remote_copy(
                src_ref=acc.at[0], dst_ref=landed, send_sem=send_sem, recv_sem=recv_sem,
                device_id=(x, y, 1 - c), device_id_type=MESH)

        @pl.when(k == 0)
        def _():
            acc[ph] = jnp.zeros((hm, N), F32)

        acc[ph] += lax.dot_general(a_ref[...].astype(BF16), b_ref[...].astype(BF16), TN, preferred_element_type=F32)

        @pl.when((ph == 0) & (k == n_k - 1))
        def _():
            hand_over().start()

        @pl.when((ph == 1) & (k == n_k - 1))
        def _():
            copy = hand_over()
            copy.wait_recv()
            o_ref[...] = (acc[1] + landed[...]).astype(BF16)
            copy.wait_send()

    return pl.pallas_call(
        body, name=name,
        grid_spec=pltpu.PrefetchScalarGridSpec(
            num_scalar_prefetch=1, grid=(2, n_k),
            in_specs=[pl.BlockSpec((tk, hm), lambda ph, k, c_ref: (k, (ph + 1 - c_ref[0]) % 2)),
                      pl.BlockSpec((tk, N), lambda ph, k, c_ref: (k, 0))],
            out_specs=pl.BlockSpec((hm, N), lambda ph, k, c_ref: (0, 0)),
            scratch_shapes=[pltpu.VMEM((2, hm, N), F32), pltpu.VMEM((hm, N), F32),
                            pltpu.SemaphoreType.DMA, pltpu.SemaphoreType.DMA]),
        out_shape=_sds((hm, N), BF16),
        compiler_params=pltpu.CompilerParams(dimension_semantics=("arbitrary", "arbitrary")),
    )(c_arr, a, b)


def _mixer_bwd(d2, u, hs, saved, pv, wa, wx, wp, w_out_b, tm, chip_sums=()):
    T = u.shape[0]
    n_t = T // tm
    n_x = len(chip_sums)

    def body(d2_ref, u_ref, uh_ref, hs_ref, hh_ref, saved_ref, pv_ref, wa_in, wx_in, wp_in, wo_ref, *rest):
        x_in, rest = rest[:n_x], rest[n_x:]
        du_ref, sg_ref = rest[:2]
        x_out, rest = rest[2:2 + n_x], rest[2 + n_x:]
        e_pool, e_h, a_s, b_s, dh_s, mu_s, f_x, f_p, mc, cx, cp = rest[:11]
        wa_ref, wx_ref, wp_ref, vacc_ref, dwa_ref, dwx_ref, dwp_ref = rest[11:18]
        exchange = _ChipExchange(x_in, x_out, *rest[18:]) if n_x else None
        s = pl.program_id(0)
        it = n_t - 1 - s

        @pl.when(s == 0)
        def _():
            if exchange:
                exchange.start()
            mc[...] = jnp.zeros((8, LRU_W), F32)
            cx[...] = jnp.zeros((8, LRU_W), F32)
            cp[...] = jnp.zeros((HALO, POOL_W), F32)
            vacc_ref[...] = jnp.zeros((16, LRU_W), F32)
            dwa_ref[...] = jnp.zeros((2, 256, 256), F32)
            dwx_ref[...] = jnp.zeros((2, 256, 256), F32)
            dwp_ref[...] = jnp.zeros((2, 256, 256), F32)
            _fill_block_diag(wa_ref, wa_in)
            _fill_block_diag(wx_ref, wx_in)
            _fill_block_diag(wp_ref, wp_in)

        first = it == 0
        e_pool[pl.ds(0, HALO), :] = jnp.where(first, 0.0, uh_ref[...])
        e_pool[pl.ds(HALO, tm), :] = u_ref[:, 2 * LRU_W:D_IN]
        e_h[pl.ds(0, 8), :] = jnp.where(first, 0.0, hh_ref[...])
        e_h[pl.ds(8, tm), :] = hs_ref[...]
        pv = pv_ref[...]
        saved = lambda q: saved_ref[:, LRU_W * q:LRU_W * (q + 1)]

        dyn = lax.dot_general(d2_ref[...].astype(BF16), wo_ref[...], NT, preferred_element_type=F32)
        dyn = jnp.concatenate([dyn[:, 128 * _y_pos(b):128 * (_y_pos(b) + 1)] for b in range(N_DEV)], axis=1)

        h = hs_ref[...]
        ug = u_ref[:, LRU_W:2 * LRU_W]
        gl, dgl = _gelu_parts(ug)
        y_lru = h * gl
        rstd_l = _rstd(y_lru)
        dy_lru, d_gain_l = _rms_bwd(dyn[:, 0:LRU_W], y_lru * rstd_l, rstd_l, pv[ROW_GL:ROW_GL + 1, :])
        dh = dy_lru * gl
        du_ref[:, LRU_W:2 * LRU_W] = (dy_lru * h * dgl).astype(BF16)
        a_s[...] = jnp.exp(saved(3))
        b_s[...] = a_s[...] * dh
        dh_s[...] = dh
        mu_s[pl.ds(tm, 8), :] = mc[...]
        mc[...] = _scan_tile(a_s, b_s, mu_s, mc[...], tm, reverse=True)
        xc, r, ig = saved(0), saved(1), saved(2)
        a, om, mult, rmult = _lru_decay(saved(3))
        lam_t = dh_s[...] + mu_s[pl.ds(1, tm), :]
        da = lam_t * e_h[pl.ds(7, tm), :]
        dmult = lam_t * (ig * xc)
        di = lam_t * (mult * xc)
        dxc = lam_t * (mult * ig)
        dla = da * a - jnp.where(om > 1e-12, dmult * ((a * a) * rmult), 0.0)
        dra = (dla * (-LRU_C * _softplus_neg_lambda(pv))) * (r * (1.0 - r))
        dia = di * (ig * (1.0 - ig))
        drab = dra.astype(BF16)
        diab = dia.astype(BF16)
        xcb = xc.astype(BF16)
        dxc = dxc + _bd_t(drab, wa_ref) + _bd_t(diab, wx_ref)
        dwa_ref[...] += _bd_grad(xcb, drab)
        dwx_ref[...] += _bd_grad(xcb, diab)
        sig_neg_lam = _sigmoid(-pv[ROW_LAM:ROW_LAM + 1, :])
        d_lam = jnp.sum(dla * r, axis=0, keepdims=True) * (LRU_C * sig_neg_lam)

        f_x[pl.ds(0, tm), :] = dxc
        f_x[pl.ds(tm, 8), :] = cx[...]
        du_lru = jnp.zeros((tm, LRU_W), F32)
        u_lru = u_ref[:, 0:LRU_W]
        d_cw = []
        for k in range(4):
            later = f_x[pl.ds(3 - k, tm), :]
            du_lru = du_lru + later * pv[ROW_CW + k:ROW_CW + k + 1, :]
            d_cw.append(jnp.sum(later * u_lru, axis=0, keepdims=True))
        du_ref[:, 0:LRU_W] = du_lru.astype(BF16)
        cx[...] = f_x[pl.ds(0, 8), :]

        pooled_b, zp, inv_cnts = _pool_pre(e_pool, pv, wp_ref, tm, it * tm)
        ps = pv[ROW_PS:ROW_PS + 1, :]
        y_pool = zp * ps
        rstd_p = _rstd(y_pool)
        dy_pool, d_gain_p = _rms_bwd(dyn[:, LRU_W:D_MODEL], y_pool * rstd_p, rstd_p, pv[ROW_GP:ROW_GP + 1, :])
        dz = dy_pool * ps
        dzb = dz.astype(BF16)
        dwp_ref[...] += _bd_grad(pooled_b, dzb)
        dpooled = _bd_t(dzb, wp_ref)
        for g, w in enumerate(POOL_WINDOWS):
            f_p[pl.ds(0, tm), pl.ds(128 * g, 128)] = dpooled[:, 128 * g:128 * (g + 1)] * inv_cnts[g]
        f_p[pl.ds(tm, HALO), :] = cp[...]
        for g, w in enumerate(POOL_WINDOWS):
            acc = _window_sum(f_p[:, pl.ds(128 * g, 128)], w, back=False)[0:tm, :]
            du_ref[:, 2 * LRU_W + 128 * g:2 * LRU_W + 128 * (g + 1)] = (
                acc - dpooled[:, 128 * g:128 * (g + 1)]).astype(BF16)
        cp[...] = f_p[pl.ds(0, HALO), :]

        rows = d_cw + [
            jnp.sum(dxc, axis=0, keepdims=True),
            jnp.sum(dra, axis=0, keepdims=True),
            jnp.sum(dia, axis=0, keepdims=True),
            d_lam,
            jnp.sum(dz, axis=0, keepdims=True),
            jnp.sum(dy_pool * zp, axis=0, keepdims=True),
            d_gain_l, d_gain_p,
            jnp.zeros((4, LRU_W), F32),
        ]
        vacc_ref[...] += jnp.concatenate(rows, axis=0)

        @pl.when(s == n_t - 1)
        def _():
            sg_ref[SG_VEC:SG_VEC + 16, :] = vacc_ref[:, 0:256]
            sg_ref[SG_VEC + 16:SG_VEC + 32, :] = vacc_ref[:, 256:512]
            for half in range(2):
                sg_ref[SG_WA + 64 * half:SG_WA + 64 * (half + 1), :] = _diag_pack(dwa_ref[half], 64)
                sg_ref[SG_WX + 64 * half:SG_WX + 64 * (half + 1), :] = _diag_pack(dwx_ref[half], 64)
                sg_ref[SG_WP + 128 * half:SG_WP + 128 * (half + 1), :] = _diag_pack(dwp_ref[half], 128)
            if exchange:
                exchange.finish()

    rev = lambda w: pl.BlockSpec((tm, w), lambda s: (n_t - 1 - s, 0))
    full = lambda shape: pl.BlockSpec(shape, lambda s: (0,) * len(shape))
    outs = pl.pallas_call(
        body, name="mixer_bwd", grid=(n_t,),
        in_specs=[rev(D_MODEL), rev(D_IN),
                  pl.BlockSpec((HALO, POOL_W), lambda s: (jnp.maximum((n_t - 1 - s) * (tm // HALO) - 1, 0), 2)),
                  rev(LRU_W),
                  pl.BlockSpec((8, LRU_W), lambda s: (jnp.maximum((n_t - 1 - s) * (tm // 8) - 1, 0), 0)),
                  rev(4 * LRU_W), full((16, LRU_W)), full((8, 64, 64)), full((8, 64, 64)), full((4, 128, 128)),
                  full((D_MODEL, D_MODEL))] + [ANY] * n_x,
        out_specs=[rev(D_IN), full((SG_ROWS, 256))] + [ANY] * n_x,
        out_shape=[_sds((T, D_IN), BF16), _sds((SG_ROWS, 256), F32)] + [_sds(a.shape, a.dtype) for a in chip_sums],
        scratch_shapes=[pltpu.VMEM((HALO + tm, POOL_W), F32),
                        pltpu.VMEM((8 + tm, LRU_W), F32)] + [pltpu.VMEM((tm, LRU_W), F32)] * 3 + [
                        pltpu.VMEM((tm + 8, LRU_W), F32), pltpu.VMEM((tm + 8, LRU_W), F32),
                        pltpu.VMEM((tm + HALO, POOL_W), F32), pltpu.VMEM((8, LRU_W), F32),
                        pltpu.VMEM((8, LRU_W), F32), pltpu.VMEM((HALO, POOL_W), F32)]
        + [pltpu.VMEM((2, 256, 256), BF16)] * 3 + [pltpu.VMEM((16, LRU_W), F32)] + [pltpu.VMEM((2, 256, 256), F32)] * 3
        + (_ChipExchange.scratch(n_x) if n_x else []),
        compiler_params=pltpu.CompilerParams(dimension_semantics=("arbitrary",)),
    )(d2, u, u, hs, hs, saved, pv, wa, wx, wp, w_out_b, *chip_sums)
    return outs[0], outs[1], list(outs[2:])


def _mix_in_bwd(du, x, d2, w_in_t, g_mix, tm, chip_sums=()):
    T = x.shape[0]
    n_t = T // tm
    n_x = len(chip_sums)

    def body(du_ref, x_ref, d2_ref, w_ref, g_ref, *rest):
        x_in, rest = rest[:n_x], rest[n_x:]
        dx_ref, dg_ref = rest[:2]
        exchange = _ChipExchange(x_in, rest[2:2 + n_x], *rest[2 + n_x:]) if n_x else None
        i = pl.program_id(0)

        @pl.when(i == 0)
        def _():
            dg_ref[...] = jnp.zeros((1, D_MODEL), F32)
            if exchange:
                exchange.start()

        dh = jnp.dot(du_ref[...], w_ref[...], preferred_element_type=F32)
        xv = x_ref[...]
        rstd = _rstd(xv)
        dx, dgain = _rms_bwd(dh, xv * rstd, rstd, g_ref[...])
        dx_ref[...] = d2_ref[...] + dx
        dg_ref[...] += dgain

        if exchange:
            @pl.when(i == n_t - 1)
            def _():
                exchange.finish()

    row = lambda w: pl.BlockSpec((tm, w), lambda i: (i, 0))
    const = lambda shape: pl.BlockSpec(shape, lambda i: (0,) * len(shape))
    outs = pl.pallas_call(
        body, name="mix_in_bwd", grid=(n_t,),
        in_specs=[row(D_IN), row(D_MODEL), row(D_MODEL), const((D_IN, D_MODEL)), const((1, D_MODEL))] + [ANY] * n_x,
        out_specs=[row(D_MODEL), const((1, D_MODEL))] + [ANY] * n_x,
        out_shape=[_sds((T, D_MODEL), F32), _sds((1, D_MODEL), F32)] + [_sds(a.shape, a.dtype) for a in chip_sums],
        scratch_shapes=_ChipExchange.scratch(n_x) if n_x else [],
        compiler_params=pltpu.CompilerParams(dimension_semantics=("arbitrary",)),
    )(du, x, d2, w_in_t, g_mix, *chip_sums)
    return outs[0], outs[1], list(outs[2:])


def _pair_sum(g, r1, c_arr, name):
    _, _, R, C = g.shape
    tr = R if R <= 512 else 256

    def body(c_ref, g_ref, r_ref, o_ref):
        o_ref[...] = (g_ref[...] + r_ref[...]).astype(BF16)

    return pl.pallas_call(
        body, name=name,
        grid_spec=pltpu.PrefetchScalarGridSpec(
            num_scalar_prefetch=1, grid=(4, R // tr),
            in_specs=[pl.BlockSpec((None, None, tr, C), lambda j, i, c_ref: (j, c_ref[0], i, 0)),
                      pl.BlockSpec((None, tr, C), lambda j, i, c_ref: (j, i, 0))],
            out_specs=pl.BlockSpec((None, tr, C), lambda j, i, c_ref: (j, i, 0))),
        out_shape=_sds((4, R, C), BF16),
    )(c_arr, g, r1)


def _adamw(w, g, m, v):
    m = ADAM_B1 * m + (1.0 - ADAM_B1) * g
    v = ADAM_B2 * v + (1.0 - ADAM_B2) * (g * g)
    m_hat = m / (1.0 - ADAM_B1 ** ADAM_STEP)
    v_hat = v / (1.0 - ADAM_B2 ** ADAM_STEP)
    delta = -ADAM_LR * (m_hat / (jnp.sqrt(v_hat) + ADAM_EPS) + ADAM_WD * w)
    return delta, m, v


def _adam_shards(ws, ms, vs, parts):
    n = len(ws)
    n_blk = [w.shape[0] // ADAM_ROWS for w in ws]

    def body(*refs):
        w_refs, m_refs, v_refs, p_refs, outs = (refs[:n], refs[n:2 * n], refs[2 * n:3 * n], refs[3 * n:4 * n],
                                                refs[4 * n:])
        i = pl.program_id(0)
        for a in range(n):
            @pl.when(i < n_blk[a])
            def _(a=a):
                g = p_refs[a][0].astype(F32)
                for j in range(1, 4):
                    g = g + p_refs[a][j].astype(F32)
                delta, new_m, new_v = _adamw(w_refs[a][...], g, m_refs[a][...], v_refs[a][...])
                for kind, val in enumerate((g, delta, new_m, new_v)):
                    outs[4 * a + kind][...] = val

    blk = lambda a: pl.BlockSpec((ADAM_ROWS, D_MODEL), lambda i: (jnp.minimum(i, n_blk[a] - 1), 0))
    part_blk = lambda a: pl.BlockSpec((4, ADAM_ROWS, D_MODEL), lambda i: (0, jnp.minimum(i, n_blk[a] - 1), 0))
    res = pl.pallas_call(
        body, name="adam_shards", grid=(max(n_blk),),
        in_specs=[blk(a) for a in range(n)] * 3 + [part_blk(a) for a in range(n)],
        out_specs=[blk(a) for a in range(n) for _ in range(4)],
        out_shape=[_sds(w.shape, F32) for w in ws for _ in range(4)],
        compiler_params=pltpu.CompilerParams(dimension_semantics=("arbitrary",)),
    )(*ws, *ms, *vs, *parts)
    return [tuple(res[4 * a:4 * a + 4]) for a in range(n)]


SMALL_PARAMS = [("norm_mix_g", (1, D_MODEL)), ("conv_w", (1, 4, 64)), ("conv_b", (1, LRU_W)),
                ("gate_a_w", (1, 8, 64, 64)), ("gate_a_b", (1, LRU_W)), ("gate_x_w", (1, 8, 64, 64)),
                ("gate_x_b", (1, LRU_W)), ("lru_lambda", (1, LRU_W)), ("pool_w", (1, 4, 128, 128)),
                ("pool_b", (1, POOL_W)), ("pool_scale", (1, POOL_W)), ("norm_lru_g", (1, LRU_W)),
                ("norm_pool_g", (1, POOL_W)), ("norm_ffn_g", (1, D_MODEL)), ("final_norm_g", (1, D_MODEL))]
VEC_ROW = dict(conv_b=ROW_CB, gate_a_b=ROW_BA, gate_x_b=ROW_BX, lru_lambda=ROW_LAM, pool_b=ROW_PB, pool_scale=ROW_PS,
               norm_lru_g=ROW_GL, norm_pool_g=ROW_GP)
WHOLE = (Ellipsis,)


def _unpack_mixer_grads(sg, dev):
    vec = jnp.concatenate([sg[SG_VEC:SG_VEC + 16], sg[SG_VEC + 16:SG_VEC + 32]], axis=1)
    out = {nm: [(WHOLE, vec[r:r + 1])] for nm, r in VEC_ROW.items()}
    own = jnp.zeros((4, 64), F32)
    for d in range(N_DEV):
        own = jnp.where(dev == d, vec[ROW_CW:ROW_CW + 4, 64 * d:64 * (d + 1)], own)
    out["conv_w"] = [((0,), own)]
    for nm, row0 in (("gate_a_w", SG_WA), ("gate_x_w", SG_WX)):
        out[nm] = [((0, b), sg[row0 + 64 * (b // 4):row0 + 64 * (b // 4 + 1), 64 * (b % 4):64 * (b % 4 + 1)])
                   for b in range(8)]
    out["pool_w"] = [((0, b), sg[SG_WP + 128 * (b // 2):SG_WP + 128 * (b // 2 + 1), 128 * (b % 2):128 * (b % 2 + 1)])
                     for b in range(4)]
    return out


def _adam_small(parts, w, m, v):
    names = [nm for nm, _ in SMALL_PARAMS]
    n = len(names)

    def body(sg_ref, gm_ref, gf_ref, gn_ref, ls_ref, *rest):
        w_refs, m_refs, v_refs, outs = rest[:n], rest[n:2 * n], rest[2 * n:3 * n], rest[3 * n:]
        dev = 4 * lax.axis_index("x") + 2 * lax.axis_index("y") + lax.axis_index("c")

        def total(ref):
            acc = ref[0]
            for d in range(1, N_DEV):
                acc = acc + ref[d]
            return acc

        pieces = _unpack_mixer_grads(total(sg_ref), dev)
        pieces["norm_mix_g"] = [(WHOLE, total(gm_ref))]
        pieces["norm_ffn_g"] = [(WHOLE, total(gf_ref))]
        pieces["final_norm_g"] = [(WHOLE, total(gn_ref))]
        for i, nm in enumerate(names):
            for idx, g in pieces[nm]:
                delta, new_m, new_v = _adamw(w_refs[i][idx], g, m_refs[i][idx], v_refs[i][idx])
                for kind, val in enumerate((g, delta, new_m, new_v)):
                    outs[4 * i + kind][idx] = val
        outs[4 * n][...] = total(ls_ref)

    shapes = [_sds(shape, F32) for _, shape in SMALL_PARAMS for _ in range(4)] + [_sds((8, 128), F32)]
    res = pl.pallas_call(body, name="adam_small", out_shape=shapes)(
        *parts, *[w[nm] for nm in names], *[m[nm] for nm in names], *[v[nm] for nm in names])
    return {nm: tuple(res[4 * i:4 * i + 4]) for i, nm in enumerate(names)}, res[4 * n][0, 0]


def _vec_rows(conv_w_full, conv_b, ba, bx, lam, pb, ps, gl, gp):
    return jnp.concatenate([conv_w_full, conv_b, ba, bx, lam, pb, ps, gl, gp, jnp.zeros((4, LRU_W), F32)], axis=0)


WEIGHT_ORDER = ['norm_mix_g', 'w_in', 'conv_w', 'conv_b', 'gate_a_w', 'gate_a_b', 'gate_x_w', 'gate_x_b', 'lru_lambda',
                'pool_w', 'pool_b', 'pool_scale', 'norm_lru_g', 'norm_pool_g', 'w_out', 'norm_ffn_g', 'ffn_w1', 'ffn_w3',
                'ffn_w2', 'final_norm_g']


def kernel(x, norm_mix_g, w_in, conv_w, conv_b, gate_a_w, gate_a_b, gate_x_w, gate_x_b, lru_lambda, pool_w, pool_b, pool_scale, norm_lru_g, norm_pool_g, w_out, norm_ffn_g, ffn_w1, ffn_w3, ffn_w2, final_norm_g, loss_target, m_norm_mix_g, m_w_in, m_conv_w, m_conv_b, m_gate_a_w, m_gate_a_b, m_gate_x_w, m_gate_x_b, m_lru_lambda, m_pool_w, m_pool_b, m_pool_scale, m_norm_lru_g, m_norm_pool_g, m_w_out, m_norm_ffn_g, m_ffn_w1, m_ffn_w3, m_ffn_w2, m_final_norm_g, v_norm_mix_g, v_w_in, v_conv_w, v_conv_b, v_gate_a_w, v_gate_a_b, v_gate_x_w, v_gate_x_b, v_lru_lambda, v_pool_w, v_pool_b, v_pool_scale, v_norm_lru_g, v_norm_pool_g, v_w_out, v_norm_ffn_g, v_ffn_w1, v_ffn_w3, v_ffn_w2, v_final_norm_g):
    ac = lax.axis_index("c")
    tm, tmx, tn, tk = 512, 512, 1408, 1024
    tm_in = 1024
    xs, tgt = x[0], loss_target[0]
    g_fin = final_norm_g.reshape(1, D_MODEL)
    c_arr = jnp.reshape(ac, (1,)).astype(jnp.int32)

    def pair_sums(blocks, names):
        from_sibling = _pair_exchange(blocks, "grads_to_sibling_" + names[0])
        return [_pair_sum(g.reshape((4, 2) + g.shape[1:]), r, c_arr, "pair_sum_" + nm)
                for g, r, nm in zip(blocks, from_sibling, names)]

    tr = lambda w: jnp.swapaxes(w[0], 0, 1)
    own = lambda w: w[0]
    bf = lambda a: a.astype(BF16)

    g_in, g_conv = _all_gather([bf(tr(w_in)), conv_w[0]], "gather_w_in")
    w_in_t = g_in.reshape(D_IN, D_MODEL)
    conv_w_full = g_conv.transpose(1, 0, 2).reshape(4, LRU_W)
    pv = _vec_rows(conv_w_full, conv_b, gate_a_b, gate_x_b, lru_lambda, pool_b, pool_scale, norm_lru_g, norm_pool_g)
    wa, wx, wp = gate_a_w[0], gate_x_w[0], pool_w[0]

    u, h1, (g_out, g_w1) = _mix_in(xs, norm_mix_g, w_in_t, tm, shards=[bf(own(w_out)), bf(tr(ffn_w1))])
    w_out_b = g_out.reshape(D_MODEL, D_MODEL)
    y, hs, hres, h2, saved, (g_w3, g_w2) = _mixer_fwd(u, xs, pv, wa, wx, wp, w_out_b, norm_ffn_g, tmx,
                                               shards=[bf(tr(ffn_w3)), bf(own(ffn_w2))])
    w1_t, w3_t, w2_b = g_w1.reshape(D_FF, D_MODEL), g_w3.reshape(D_FF, D_MODEL), g_w2.reshape(D_FF, D_MODEL)
    g, v, d3, loss_acc, d_gfin = _ffn_fwd(hres, h2, w1_t, w3_t, w2_b, g_fin, tgt, tm, tn)

    dg, dv, ff, d2, d_gffn = _ffn_bwd(d3, g, v, w1_t, w3_t, w2_b, hres, norm_ffn_g, tm, tn)
    blocks = lambda a: a.reshape(N_DEV, a.shape[0] // N_DEV, a.shape[1])
    chips = lambda a: a.reshape(4, a.shape[0] // 4, a.shape[1])
    early_sums = [chips(_at_b_pair(y, d2, c_arr, "grad_w_out", tk)), chips(_at_b_pair(dg, h2, c_arr, "grad_w1", tk)),
                  chips(_at_b_pair(dv, h2, c_arr, "grad_w3", tk)), chips(_at_b_pair(ff, d3, c_arr, "grad_w2", tk))]
    du, d_mixer, early_parts = _mixer_bwd(d2, u, hs, saved, pv, wa, wx, wp, w_out_b, tmx, chip_sums=early_sums)
    grad_x, d_gmix, _ = _mix_in_bwd(du, xs, d2, w_in_t, norm_mix_g, tm_in)
    d_win, small_parts = _at_b(du, h1, "grad_w_in", D_IN, D_MODEL, tk,
                               gather=[d_mixer, d_gmix, d_gffn, d_gfin, loss_acc])
    win_parts = _chip_exchange(pair_sums([blocks(d_win)], ["w_in"]), "grads_to_chips_w_in")
    parts = list(win_parts) + list(early_parts)

    res = {}
    shard_w = dict(w_in=(w_in, m_w_in, v_w_in, tr), w_out=(w_out, m_w_out, v_w_out, own),
                   ffn_w1=(ffn_w1, m_ffn_w1, v_ffn_w1, tr), ffn_w3=(ffn_w3, m_ffn_w3, v_ffn_w3, tr),
                   ffn_w2=(ffn_w2, m_ffn_w2, v_ffn_w2, own))
    shard_res = _adam_shards([view(w) for w, _, _, view in shard_w.values()],
                             [view(m) for _, m, _, view in shard_w.values()],
                             [view(v) for _, _, v, view in shard_w.values()], parts)
    for (nm, (_, _, _, view)), outs in zip(shard_w.items(), shard_res):
        res[nm] = [(jnp.swapaxes(o, 0, 1) if view is tr else o)[None] for o in outs]

    row = lambda a: a.reshape(1, D_MODEL)
    small = lambda gm, cw, cb, wa_, ba, wx_, bx, lam, pw, pb, ps, gl, gp, gf, gn: dict(
        norm_mix_g=gm, conv_w=cw, conv_b=cb, gate_a_w=wa_, gate_a_b=ba, gate_x_w=wx_, gate_x_b=bx, lru_lambda=lam,
        pool_w=pw, pool_b=pb, pool_scale=ps, norm_lru_g=gl, norm_pool_g=gp, norm_ffn_g=gf, final_norm_g=row(gn))
    small_res, loss = _adam_small(
        small_parts,
        small(norm_mix_g, conv_w, conv_b, gate_a_w, gate_a_b, gate_x_w, gate_x_b, lru_lambda, pool_w, pool_b,
              pool_scale, norm_lru_g, norm_pool_g, norm_ffn_g, final_norm_g),
        small(m_norm_mix_g, m_conv_w, m_conv_b, m_gate_a_w, m_gate_a_b, m_gate_x_w, m_gate_x_b, m_lru_lambda, m_pool_w,
              m_pool_b, m_pool_scale, m_norm_lru_g, m_norm_pool_g, m_norm_ffn_g, m_final_norm_g),
        small(v_norm_mix_g, v_conv_w, v_conv_b, v_gate_a_w, v_gate_a_b, v_gate_x_w, v_gate_x_b, v_lru_lambda, v_pool_w,
              v_pool_b, v_pool_scale, v_norm_lru_g, v_norm_pool_g, v_norm_ffn_g, v_final_norm_g))
    for nm, outs in small_res.items():
        res[nm] = [o.reshape(D_MODEL) for o in outs] if nm == "final_norm_g" else list(outs)

    out = [loss, grad_x[None]]
    for kind in range(4):
        out += [res[nm][kind] for nm in WEIGHT_ORDER]
    return tuple(out)
```

```python
import jax
import jax.numpy as jnp
from jax import lax
from jax.experimental import pallas as pl
from jax.experimental.pallas import tpu as pltpu

F32 = jnp.float32
BF16 = jnp.bfloat16

D_MODEL = 1024
LRU_W = 512
POOL_W = 512
D_IN = 1536
D_FF = 2816
POOL_WINDOWS = (2, 4, 8, 16)
EPS = 1e-6
LRU_C = 8.0
N_DEV = 8
HALO = 16
SCAN_UNROLL = 4
ADAM_ROWS = 32

ADAM_LR = 0.001
ADAM_B1 = 0.9
ADAM_B2 = 0.999
ADAM_EPS = 1e-08
ADAM_WD = 0.01
ADAM_STEP = 10

ROW_CW, ROW_CB, ROW_BA, ROW_BX, ROW_LAM, ROW_PB, ROW_PS, ROW_GL, ROW_GP = 0, 4, 5, 6, 7, 8, 9, 10, 11
SG_VEC, SG_WA, SG_WX, SG_WP, SG_ROWS = 0, 32, 160, 288, 544

NT = (((1,), (1,)), ((), ()))
TN = (((0,), (0,)), ((), ()))


def _sds(shape, dtype):
    return jax.ShapeDtypeStruct(shape, dtype)


def _sigmoid(x):
    return 0.5 * jnp.tanh(0.5 * x) + 0.5


def _gelu_parts(x):
    c = 0.7978845608028654
    inner = c * (x + 0.044715 * (x * x * x))
    th = jnp.tanh(inner)
    g = 0.5 * x * (1.0 + th)
    dg = 0.5 * (1.0 + th) + 0.5 * x * (1.0 - th * th) * (c * (1.0 + 3.0 * 0.044715 * (x * x)))
    return g, dg


def _window_sum(ext, w, back):
    n = ext.shape[0]
    s, k = ext, 1
    while k < w:
        s = s + pltpu.roll(s, k if back else n - k, 0)
        k *= 2
    return s


def _rstd(x):
    return lax.rsqrt(jnp.mean(x * x, axis=-1, keepdims=True) + EPS)


def _rms_bwd(dy, xhat, rstd, gain):
    dxh = dy * gain
    dx = rstd * (dxh - xhat * jnp.mean(dxh * xhat, axis=-1, keepdims=True))
    return dx, jnp.sum(dy * xhat, axis=0, keepdims=True)


def _bd(xb, w_ref):
    return jnp.concatenate(
        [jnp.dot(xb[:, :256], w_ref[0], preferred_element_type=F32),
         jnp.dot(xb[:, 256:], w_ref[1], preferred_element_type=F32)], axis=1)


def _bd_t(xb, w_ref):
    return jnp.concatenate(
        [lax.dot_general(xb[:, :256], w_ref[0], NT, preferred_element_type=F32),
         lax.dot_general(xb[:, 256:], w_ref[1], NT, preferred_element_type=F32)], axis=1)


def _bd_grad(xb, db):
    return jnp.stack(
        [lax.dot_general(xb[:, :256], db[:, :256], TN, preferred_element_type=F32),
         lax.dot_general(xb[:, 256:], db[:, 256:], TN, preferred_element_type=F32)], axis=0)


def _fill_block_diag(dst, src_ref):
    n, k, _ = src_ref.shape
    dst[...] = jnp.zeros(dst.shape, BF16)
    for b in range(n):
        p, q = divmod(b, 256 // k)
        dst[p, q * k:(q + 1) * k, q * k:(q + 1) * k] = src_ref[b].astype(BF16)


def _diag_pack(w, k):
    lane = lax.broadcasted_iota(jnp.int32, (k, 256), 1)
    out = w[0:k]
    for q in range(1, 256 // k):
        out = jnp.where(lane >= q * k, w[q * k:(q + 1) * k], out)
    return out


def _y_pos(b):
    return 4 * (b % 2) + b // 2


def _softplus_neg_lambda(pv):
    z = -pv[ROW_LAM:ROW_LAM + 1, :]
    return jnp.maximum(z, 0.0) + jnp.log(1.0 + jnp.exp(-jnp.abs(z)))


def _lru_gates(e_lru, pv, wa_ref, wx_ref, tm):
    xc = pv[ROW_CB:ROW_CB + 1, :]
    for k in range(4):
        xc = xc + e_lru[pl.ds(HALO - 3 + k, tm), :] * pv[ROW_CW + k:ROW_CW + k + 1, :]
    xcb = xc.astype(BF16)
    r = _sigmoid(_bd(xcb, wa_ref) + pv[ROW_BA:ROW_BA + 1, :])
    ig = _sigmoid(_bd(xcb, wx_ref) + pv[ROW_BX:ROW_BX + 1, :])
    return xc, r, ig, (-LRU_C * r) * _softplus_neg_lambda(pv)


def _lru_decay(la):
    a = jnp.exp(la)
    om = -jnp.tanh(la) * (1.0 + a * a)
    omc = jnp.maximum(om, 1e-12)
    rmult = lax.rsqrt(omc)
    return a, om, omc * rmult, rmult


def _pool_pre(e_pool, pv, wp_ref, tm, t0):
    t = t0 + lax.broadcasted_iota(jnp.int32, (tm, 1), 0)
    parts, inv_cnts = [], []
    for g, w in enumerate(POOL_WINDOWS):
        ext = e_pool[:, pl.ds(128 * g, 128)]
        s = _window_sum(ext, w, back=True)[HALO:, :]
        inv_cnt = 1.0 / jnp.minimum(t + 1, w).astype(F32)
        inv_cnts.append(inv_cnt)
        parts.append(s * inv_cnt - ext[HALO:, :])
    pooled = jnp.concatenate(parts, axis=1)
    pooled_b = pooled.astype(BF16)
    zp = _bd(pooled_b, wp_ref) + pv[ROW_PB:ROW_PB + 1, :]
    return pooled_b, zp, inv_cnts


def _scan_tile(a_ref, b_ref, out_ref, carry, tm, reverse):
    row = lax.broadcasted_iota(jnp.int32, (8, LRU_W), 0)
    nblk = tm // 8

    def local_scan(blk):
        r0 = pl.multiple_of(blk * 8, 8)
        av = a_ref[pl.ds(r0, 8), :]
        bv = b_ref[pl.ds(r0, 8), :]
        for d in (1, 2, 4):
            sh = (8 - d) if reverse else d
            a_s = pltpu.roll(av, sh, 0)
            b_s = pltpu.roll(bv, sh, 0)
            m = (row < 8 - d) if reverse else (row >= d)
            bv = jnp.where(m, av * b_s + bv, bv)
            av = jnp.where(m, av * a_s, av)
        return r0, av, bv

    def step(i, hin):
        local = [local_scan((nblk - 1 - (i * SCAN_UNROLL + j)) if reverse else (i * SCAN_UNROLL + j))
                 for j in range(SCAN_UNROLL)]
        for r0, av, bv in local:
            hv = av * hin + bv
            out_ref[pl.ds(r0, 8), :] = hv
            hin = jnp.broadcast_to(hv[0:1, :] if reverse else hv[7:8, :], (8, LRU_W))
        return hin

    return lax.fori_loop(0, nblk // SCAN_UNROLL, step, carry)


MESH = pl.DeviceIdType.MESH
ANY = pl.BlockSpec(memory_space=pl.ANY)


def _place():
    x, y, c = lax.axis_index("x"), lax.axis_index("y"), lax.axis_index("c")
    chips = [(1 - x, y), (x, 1 - y), (1 - x, 1 - y)]
    return x, y, c, chips


class _Gather:
    def __init__(self, ins, outs, send_sems, recv_sems, local_sems, core_major=False):
        self.ins, self.outs, self.n = ins, outs, len(ins)
        self.send_sems, self.recv_sems, self.local_sems = send_sems, recv_sems, local_sems
        self.core_major = core_major

    @staticmethod
    def scratch(n):
        return [pltpu.SemaphoreType.DMA((7, n)), pltpu.SemaphoreType.DMA((7, n)), pltpu.SemaphoreType.DMA((n,))]

    def _slot(self, a, px, py, pc):
        return self.outs[a].at[4 * pc + 2 * px + py if self.core_major else 4 * px + 2 * py + pc]

    def _copy(self, a, k, block, to, src=None):
        return pltpu.make_async_remote_copy(
            src_ref=self._slot(a, *block) if src is None else src, dst_ref=self._slot(a, *block),
            send_sem=self.send_sems.at[k, a], recv_sem=self.recv_sems.at[k, a], device_id=to, device_id_type=MESH)

    def _mine(self, a):
        x, y, c, _ = _place()
        return pltpu.make_async_copy(self.ins[a], self._slot(a, x, y, c), self.local_sems.at[a])

    def _first(self, a):
        x, y, c, chips = _place()
        me = (x, y, c)
        return ([self._copy(a, 0, me, (x, y, 1 - c), src=self.ins[a])]
                + [self._copy(a, 1 + j, me, (*chip, c), src=self.ins[a]) for j, chip in enumerate(chips)])

    def start(self):
        for a in range(self.n):
            self._mine(a).start()
        for a in range(self.n):
            for cp in self._first(a):
                cp.start()

    def finish(self):
        x, y, c, chips = _place()
        me, sibling = (x, y, c), (x, y, 1 - c)
        passed = []
        for j, chip in enumerate(chips):
            for a in range(self.n):
                self._copy(a, 1 + j, (*chip, c), me).wait_recv()
                fwd = self._copy(a, 4 + j, (*chip, c), sibling)
                fwd.start()
                passed.append(fwd)
        for a in range(self.n):
            self._copy(a, 0, (x, y, 1 - c), me).wait_recv()
            for j, chip in enumerate(chips):
                self._copy(a, 4 + j, (*chip, 1 - c), me).wait_recv()
        for a in range(self.n):
            for cp in self._first(a):
                cp.wait_send()
        for cp in passed:
            cp.wait_send()
        for a in range(self.n):
            self._mine(a).wait()


def _all_gather(arrs, name):
    n = len(arrs)

    def body(*refs):
        g = _Gather(refs[:n], refs[n:2 * n], *refs[2 * n:])
        g.start()
        g.finish()

    return pl.pallas_call(
        body, name=name,
        out_shape=[_sds((N_DEV,) + a.shape, a.dtype) for a in arrs],
        in_specs=[ANY] * n, out_specs=[ANY] * n, scratch_shapes=_Gather.scratch(n),
    )(*arrs)


def _half_exchange(arr, name):
    def body(in_ref, out_ref, send_sems, recv_sems, local_sem):
        x, y, c, _ = _place()
        my_chip = 2 * x + y

        def send(j, wait):
            to_me = (c == x) & (y == j // 2) & (c == j % 2)

            @pl.when(to_me)
            def _():
                local = pltpu.make_async_copy(in_ref.at[j], out_ref.at[my_chip], local_sem)
                local.wait() if wait else local.start()

            @pl.when(jnp.logical_not(to_me))
            def _():
                remote = pltpu.make_async_remote_copy(
                    src_ref=in_ref.at[j], dst_ref=out_ref.at[my_chip], send_sem=send_sems.at[j],
                    recv_sem=recv_sems.at[my_chip], device_id=(c, j // 2, j % 2), device_id_type=MESH)
                remote.wait_send() if wait else remote.start()

        for j in range(4):
            send(j, wait=False)
        for j in range(4):
            send(j, wait=True)
        for k in range(4):
            from_me = (k // 2 == x) & (k % 2 == y) & (c == x)

            @pl.when(jnp.logical_not(from_me))
            def _():
                pltpu.make_async_remote_copy(
                    src_ref=in_ref.at[0], dst_ref=out_ref.at[k], send_sem=send_sems.at[0], recv_sem=recv_sems.at[k],
                    device_id=(k // 2, k % 2, x), device_id_type=MESH).wait_recv()

    return pl.pallas_call(
        body, name=name, out_shape=_sds(arr.shape, arr.dtype), in_specs=[ANY], out_specs=ANY,
        scratch_shapes=[pltpu.SemaphoreType.DMA((4,)), pltpu.SemaphoreType.DMA((4,)), pltpu.SemaphoreType.DMA],
    )(arr)


class _ChipExchange:
    def __init__(self, ins, outs, send_sems, recv_sems, local_sems):
        self.ins, self.outs, self.n = ins, outs, len(ins)
        self.send_sems, self.recv_sems, self.local_sems = send_sems, recv_sems, local_sems

    @staticmethod
    def scratch(n):
        return [pltpu.SemaphoreType.DMA((3, n)), pltpu.SemaphoreType.DMA((3, n)), pltpu.SemaphoreType.DMA((n,))]

    def _local(self, a):
        x, y, _, _ = _place()
        me = 2 * x + y
        return pltpu.make_async_copy(self.ins[a].at[me], self.outs[a].at[me], self.local_sems.at[a])

    def _copies(self, a):
        x, y, c, chips = _place()
        me = 2 * x + y
        return [(pltpu.make_async_remote_copy(
                     src_ref=self.ins[a].at[2 * px + py], dst_ref=self.outs[a].at[me],
                     send_sem=self.send_sems.at[k, a], recv_sem=self.recv_sems.at[k, a],
                     device_id=(px, py, c), device_id_type=MESH),
                 pltpu.make_async_remote_copy(
                     src_ref=self.ins[a].at[me], dst_ref=self.outs[a].at[2 * px + py],
                     send_sem=self.send_sems.at[k, a], recv_sem=self.recv_sems.at[k, a],
                     device_id=(px, py, c), device_id_type=MESH))
                for k, (px, py) in enumerate(chips)]

    def start(self):
        for a in range(self.n):
            self._local(a).start()
        for a in range(self.n):
            for send, _ in self._copies(a):
                send.start()

    def finish(self):
        for a in range(self.n):
            for send, recv in self._copies(a):
                send.wait_send()
                recv.wait_recv()
        for a in range(self.n):
            self._local(a).wait()


def _mix_in(x, g_mix, w_in_t, tm, shards=()):
    T = x.shape[0]
    n_t = T // tm
    n_s = len(shards)

    def body(x_ref, g_ref, w_ref, *rest):
        sh_in, rest = rest[:n_s], rest[n_s:]
        u_ref, h_ref = rest[:2]
        gather = _Gather(sh_in, rest[2:2 + n_s], *rest[2 + n_s:], core_major=True) if n_s else None
        i = pl.program_id(0)

        if gather:
            @pl.when(i == 0)
            def _():
                gather.start()

        xv = x_ref[...]
        h = (xv * _rstd(xv) * g_ref[...]).astype(BF16)
        h_ref[...] = h
        u_ref[...] = lax.dot_general(h, w_ref[...], NT, preferred_element_type=F32)

        if gather:
            @pl.when(i == n_t - 1)
            def _():
                gather.finish()

    outs = pl.pallas_call(
        body, name="mix_in", grid=(n_t,),
        in_specs=[pl.BlockSpec((tm, D_MODEL), lambda i: (i, 0)),
                  pl.BlockSpec((1, D_MODEL), lambda i: (0, 0)),
                  pl.BlockSpec((D_IN, D_MODEL), lambda i: (0, 0))] + [ANY] * n_s,
        out_specs=[pl.BlockSpec((tm, D_IN), lambda i: (i, 0)),
                   pl.BlockSpec((tm, D_MODEL), lambda i: (i, 0))] + [ANY] * n_s,
        out_shape=[_sds((T, D_IN), F32), _sds((T, D_MODEL), BF16)] + [_sds((N_DEV,) + a.shape, a.dtype) for a in shards],
        scratch_shapes=_Gather.scratch(n_s) if n_s else [],
        compiler_params=pltpu.CompilerParams(dimension_semantics=("arbitrary",)),
    )(x, g_mix, w_in_t, *shards)
    return outs[0], outs[1], list(outs[2:])


def _mixer_fwd(u, x, pv, wa, wx, wp, w_out_b, g_ffn, tm, shards=()):
    T = u.shape[0]
    n_s = len(shards)
    n_t = T // tm

    def body(u_ref, x_ref, pv_ref, wa_in, wx_in, wp_in, wo_ref, gf_ref, *rest):
        sh_in, rest = rest[:n_s], rest[n_s:]
        y_ref, hs_ref, hres_ref, h2_ref, saved_ref = rest[:5]
        sh_out, rest = rest[5:5 + n_s], rest[5 + n_s:]
        e_lru, e_pool, a_s, b_s, hc, wa_ref, wx_ref, wp_ref = rest[:8]
        gather = _Gather(sh_in, sh_out, *rest[8:], core_major=True) if n_s else None
        i = pl.program_id(0)

        @pl.when(i == 0)
        def _():
            if gather:
                gather.start()
            e_lru[pl.ds(0, HALO), :] = jnp.zeros((HALO, LRU_W), F32)
            e_pool[pl.ds(0, HALO), :] = jnp.zeros((HALO, POOL_W), F32)
            hc[...] = jnp.zeros((8, LRU_W), F32)
            _fill_block_diag(wa_ref, wa_in)
            _fill_block_diag(wx_ref, wx_in)
            _fill_block_diag(wp_ref, wp_in)

        e_lru[pl.ds(HALO, tm), :] = u_ref[:, 0:LRU_W]
        e_pool[pl.ds(HALO, tm), :] = u_ref[:, 2 * LRU_W:D_IN]
        pv = pv_ref[...]
        xc, r, ig, la = _lru_gates(e_lru, pv, wa_ref, wx_ref, tm)
        for q, val in enumerate((xc, r, ig, la)):
            saved_ref[:, LRU_W * q:LRU_W * (q + 1)] = val
        a, _, mult, _ = _lru_decay(la)
        a_s[...] = a
        b_s[...] = mult * (ig * xc)
        hc[...] = _scan_tile(a_s, b_s, hs_ref, hc[...], tm, reverse=False)
        gl, _ = _gelu_parts(u_ref[:, LRU_W:2 * LRU_W])
        y_lru = hs_ref[...] * gl
        _, zp, _ = _pool_pre(e_pool, pv, wp_ref, tm, i * tm)
        y_pool = zp * pv[ROW_PS:ROW_PS + 1, :]
        yn = jnp.concatenate([y_lru * _rstd(y_lru) * pv[ROW_GL:ROW_GL + 1, :],
                              y_pool * _rstd(y_pool) * pv[ROW_GP:ROW_GP + 1, :]], axis=1).astype(BF16)
        for b in range(N_DEV):
            y_ref[:, 128 * _y_pos(b):128 * (_y_pos(b) + 1)] = yn[:, 128 * b:128 * (b + 1)]
        hr = x_ref[...] + jnp.dot(y_ref[...], wo_ref[...], preferred_element_type=F32)
        hres_ref[...] = hr
        h2_ref[...] = (hr * _rstd(hr) * gf_ref[...]).astype(BF16)
        e_lru[pl.ds(0, HALO), :] = e_lru[pl.ds(tm, HALO), :]
        e_pool[pl.ds(0, HALO), :] = e_pool[pl.ds(tm, HALO), :]

        if gather:
            @pl.when(i == n_t - 1)
            def _():
                gather.finish()

    full = lambda shape: pl.BlockSpec(shape, lambda i: (0,) * len(shape))
    row = lambda w: pl.BlockSpec((tm, w), lambda i: (i, 0))
    outs = pl.pallas_call(
        body, name="mixer_fwd", grid=(n_t,),
        in_specs=[row(D_IN), row(D_MODEL), full((16, LRU_W)), full((8, 64, 64)), full((8, 64, 64)), full((4, 128, 128)),
                  full((D_MODEL, D_MODEL)), full((1, D_MODEL))] + [ANY] * n_s,
        out_specs=[row(D_MODEL), row(LRU_W), row(D_MODEL), row(D_MODEL), row(4 * LRU_W)] + [ANY] * n_s,
        out_shape=[_sds((T, D_MODEL), BF16), _sds((T, LRU_W), F32), _sds((T, D_MODEL), F32), _sds((T, D_MODEL), BF16),
                   _sds((T, 4 * LRU_W), F32)] + [_sds((N_DEV,) + a.shape, a.dtype) for a in shards],
        scratch_shapes=[pltpu.VMEM((HALO + tm, LRU_W), F32), pltpu.VMEM((HALO + tm, POOL_W), F32),
                        pltpu.VMEM((tm, LRU_W), F32), pltpu.VMEM((tm, LRU_W), F32), pltpu.VMEM((8, LRU_W), F32)]
        + [pltpu.VMEM((2, 256, 256), BF16)] * 3 + (_Gather.scratch(n_s) if n_s else []),
        compiler_params=pltpu.CompilerParams(dimension_semantics=("arbitrary",)),
    )(u, x, pv, wa, wx, wp, w_out_b, g_ffn, *shards)
    return outs[0], outs[1], outs[2], outs[3], outs[4], list(outs[5:])


def _ffn_fwd(hres, h2, w1_b, w3_b, w2_b, g_fin, tgt, tm, tn):
    T = hres.shape[0]
    n_j = D_FF // tn

    def body(hres_ref, h2_ref, w1_ref, w3_ref, w2_ref, gfin_ref, tgt_ref,
             g_ref, v_ref, d3_ref, loss_ref, dgfin_ref, acc):
        i, j = pl.program_id(0), pl.program_id(1)

        @pl.when(j == 0)
        def _():
            acc[...] = jnp.zeros((tm, D_MODEL), F32)

        @pl.when((j == 0) & (i == 0))
        def _():
            loss_ref[...] = jnp.zeros((8, 128), F32)
            dgfin_ref[...] = jnp.zeros((1, D_MODEL), F32)

        h2 = h2_ref[...]
        g = lax.dot_general(h2, w1_ref[...], NT, preferred_element_type=F32)
        v = lax.dot_general(h2, w3_ref[...], NT, preferred_element_type=F32)
        g_ref[...] = g.astype(BF16)
        v_ref[...] = v.astype(BF16)
        ff = ((g * _sigmoid(g)) * v).astype(BF16)
        acc[...] += jnp.dot(ff, w2_ref[...], preferred_element_type=F32)

        @pl.when(j == n_j - 1)
        def _():
            h3 = hres_ref[...] + acc[...]
            rstd = _rstd(h3)
            xh = h3 * rstd
            gfin = gfin_ref[...]
            err = xh * gfin - tgt_ref[...]
            loss_ref[...] += 0.5 * jnp.sum(jnp.mean(err * err, axis=-1, keepdims=True))
            dout = err * (1.0 / D_MODEL)
            dx, dgain = _rms_bwd(dout, xh, rstd, gfin)
            d3_ref[...] = dx
            dgfin_ref[...] += dgain

    row = lambda w: pl.BlockSpec((tm, w), lambda i, j: (i, 0))
    const = lambda shape: pl.BlockSpec(shape, lambda i, j: (0,) * len(shape))
    return pl.pallas_call(
        body, name="ffn_fwd", grid=(T // tm, n_j),
        in_specs=[row(D_MODEL), row(D_MODEL),
                  pl.BlockSpec((tn, D_MODEL), lambda i, j: (j, 0)), pl.BlockSpec((tn, D_MODEL), lambda i, j: (j, 0)),
                  pl.BlockSpec((tn, D_MODEL), lambda i, j: (j, 0)), const((1, D_MODEL)), row(D_MODEL)],
        out_specs=[pl.BlockSpec((tm, tn), lambda i, j: (i, j)), pl.BlockSpec((tm, tn), lambda i, j: (i, j)),
                   row(D_MODEL), const((8, 128)), const((1, D_MODEL))],
        out_shape=[_sds((T, D_FF), BF16), _sds((T, D_FF), BF16),
                   _sds((T, D_MODEL), F32), _sds((8, 128), F32), _sds((1, D_MODEL), F32)],
        scratch_shapes=[pltpu.VMEM((tm, D_MODEL), F32)],
        compiler_params=pltpu.CompilerParams(dimension_semantics=("arbitrary", "arbitrary")),
    )(hres, h2, w1_b, w3_b, w2_b, g_fin, tgt)


def _ffn_bwd(d3, g, v, w1_b, w3_b, w2_b, hres, g_ffn, tm, tn):
    T = d3.shape[0]
    n_j = D_FF // tn

    def body(d3_ref, g_ref, v_ref, w1_ref, w3_ref, w2_ref, hres_ref, gf_ref,
             dg_ref, dv_ref, ff_ref, d2_ref, dgffn_ref, acc):
        i, j = pl.program_id(0), pl.program_id(1)

        @pl.when(j == 0)
        def _():
            acc[...] = jnp.zeros((tm, D_MODEL), F32)

        @pl.when((j == 0) & (i == 0))
        def _():
            dgffn_ref[...] = jnp.zeros((1, D_MODEL), F32)

        dff = lax.dot_general(d3_ref[...].astype(BF16), w2_ref[...], NT, preferred_element_type=F32)
        gv = g_ref[...].astype(F32)
        vv = v_ref[...].astype(F32)
        sg = _sigmoid(gv)
        sl = gv * sg
        dgb = (dff * vv * (sg * (1.0 + gv * (1.0 - sg)))).astype(BF16)
        dvb = (dff * sl).astype(BF16)
        dg_ref[...] = dgb
        dv_ref[...] = dvb
        ff_ref[...] = (sl * vv).astype(BF16)
        acc[...] += (jnp.dot(dgb, w1_ref[...], preferred_element_type=F32)
                     + jnp.dot(dvb, w3_ref[...], preferred_element_type=F32))

        @pl.when(j == n_j - 1)
        def _():
            hr = hres_ref[...]
            rstd = _rstd(hr)
            dx, dgain = _rms_bwd(acc[...], hr * rstd, rstd, gf_ref[...])
            d2_ref[...] = d3_ref[...] + dx
            dgffn_ref[...] += dgain

    row = lambda w: pl.BlockSpec((tm, w), lambda i, j: (i, 0))
    tile = pl.BlockSpec((tm, tn), lambda i, j: (i, j))
    const = lambda shape: pl.BlockSpec(shape, lambda i, j: (0,) * len(shape))
    return pl.pallas_call(
        body, name="ffn_bwd", grid=(T // tm, n_j),
        in_specs=[row(D_MODEL), tile, tile,
                  pl.BlockSpec((tn, D_MODEL), lambda i, j: (j, 0)), pl.BlockSpec((tn, D_MODEL), lambda i, j: (j, 0)),
                  pl.BlockSpec((tn, D_MODEL), lambda i, j: (j, 0)), row(D_MODEL), const((1, D_MODEL))],
        out_specs=[tile, tile, tile, row(D_MODEL), const((1, D_MODEL))],
        out_shape=[_sds((T, D_FF), BF16), _sds((T, D_FF), BF16), _sds((T, D_FF), BF16),
                   _sds((T, D_MODEL), F32), _sds((1, D_MODEL), F32)],
        scratch_shapes=[pltpu.VMEM((tm, D_MODEL), F32)],
        compiler_params=pltpu.CompilerParams(dimension_semantics=("arbitrary", "arbitrary")),
    )(d3, g, v, w1_b, w3_b, w2_b, hres, g_ffn)


def _at_b_pair(a, b, c_arr, name, tk, gather=()):
    T, M = a.shape
    N = b.shape[1]
    hm, n_k = M // 2, T // tk
    n_g = len(gather)

    def body(c_ref, a_ref, b_ref, *rest):
        g_in, o_ref, rest = rest[:n_g], rest[n_g], rest[n_g + 1:]
        g_out, rest = rest[:n_g], rest[n_g:]
        acc, landed, send_sem, recv_sem = rest[:4]
        ag = _Gather(g_in, g_out, *rest[4:]) if n_g else None
        ph, k = pl.program_id(0), pl.program_id(1)

        def hand_over():
            x, y, c, _ = _place()
            return pltpu.make_async_remote_copy(
                src_ref=acc.at[0], dst_ref=landed, send_sem=send_sem, recv_sem=recv_sem,
                device_id=(x, y, 1 - c), device_id_type=MESH)

        if ag:
            @pl.when((ph == 0) & (k == 0))
            def _():
                ag.start()

        @pl.when(k == 0)
        def _():
            acc[ph] = jnp.zeros((hm, N), F32)

        acc[ph] += lax.dot_general(a_ref[...].astype(BF16), b_ref[...].astype(BF16), TN, preferred_element_type=F32)

        @pl.when((ph == 0) & (k == n_k - 1))
        def _():
            hand_over().start()

        @pl.when((ph == 1) & (k == n_k - 1))
        def _():
            copy = hand_over()
            copy.wait_recv()
            o_ref[...] = (acc[1] + landed[...]).astype(BF16)
            copy.wait_send()
            if ag:
                ag.finish()

    outs = pl.pallas_call(
        body, name=name,
        grid_spec=pltpu.PrefetchScalarGridSpec(
            num_scalar_prefetch=1, grid=(2, n_k),
            in_specs=[pl.BlockSpec((tk, hm), lambda ph, k, c_ref: (k, (ph + 1 - c_ref[0]) % 2)),
                      pl.BlockSpec((tk, N), lambda ph, k, c_ref: (k, 0))] + [ANY] * n_g,
            out_specs=[pl.BlockSpec((hm, N), lambda ph, k, c_ref: (0, 0))] + [ANY] * n_g,
            scratch_shapes=[pltpu.VMEM((2, hm, N), F32), pltpu.VMEM((hm, N), F32),
                            pltpu.SemaphoreType.DMA, pltpu.SemaphoreType.DMA] + (_Gather.scratch(n_g) if n_g else [])),
        out_shape=[_sds((hm, N), BF16)] + [_sds((N_DEV,) + g.shape, g.dtype) for g in gather],
        compiler_params=pltpu.CompilerParams(dimension_semantics=("arbitrary", "arbitrary")),
    )(c_arr, a, b, *gather)
    return (outs[0], list(outs[1:])) if n_g else outs[0]


def _mixer_bwd(d2, u, hs, saved, pv, wa, wx, wp, w_out_b, tm, chip_sums=()):
    T = u.shape[0]
    n_t = T // tm
    n_x = len(chip_sums)

    def body(d2_ref, u_ref, uh_ref, hs_ref, hh_ref, saved_ref, pv_ref, wa_in, wx_in, wp_in, wo_ref, *rest):
        x_in, rest = rest[:n_x], rest[n_x:]
        du_ref, sg_ref = rest[:2]
        x_out, rest = rest[2:2 + n_x], rest[2 + n_x:]
        e_pool, e_h, a_s, b_s, dh_s, mu_s, f_x, f_p, mc, cx, cp = rest[:11]
        wa_ref, wx_ref, wp_ref, vacc_ref, dwa_ref, dwx_ref, dwp_ref = rest[11:18]
        exchange = _ChipExchange(x_in, x_out, *rest[18:]) if n_x else None
        s = pl.program_id(0)
        it = n_t - 1 - s

        @pl.when(s == 0)
        def _():
            if exchange:
                exchange.start()
            mc[...] = jnp.zeros((8, LRU_W), F32)
            cx[...] = jnp.zeros((8, LRU_W), F32)
            cp[...] = jnp.zeros((HALO, POOL_W), F32)
            vacc_ref[...] = jnp.zeros((16, LRU_W), F32)
            dwa_ref[...] = jnp.zeros((2, 256, 256), F32)
            dwx_ref[...] = jnp.zeros((2, 256, 256), F32)
            dwp_ref[...] = jnp.zeros((2, 256, 256), F32)
            _fill_block_diag(wa_ref, wa_in)
            _fill_block_diag(wx_ref, wx_in)
            _fill_block_diag(wp_ref, wp_in)

        first = it == 0
        e_pool[pl.ds(0, HALO), :] = jnp.where(first, 0.0, uh_ref[...])
        e_pool[pl.ds(HALO, tm), :] = u_ref[:, 2 * LRU_W:D_IN]
        e_h[pl.ds(0, 8), :] = jnp.where(first, 0.0, hh_ref[...])
        e_h[pl.ds(8, tm), :] = hs_ref[...]
        pv = pv_ref[...]
        saved = lambda q: saved_ref[:, LRU_W * q:LRU_W * (q + 1)]

        dyn = lax.dot_general(d2_ref[...].astype(BF16), wo_ref[...], NT, preferred_element_type=F32)
        dyn = jnp.concatenate([dyn[:, 128 * _y_pos(b):128 * (_y_pos(b) + 1)] for b in range(N_DEV)], axis=1)

        h = hs_ref[...]
        ug = u_ref[:, LRU_W:2 * LRU_W]
        gl, dgl = _gelu_parts(ug)
        y_lru = h * gl
        rstd_l = _rstd(y_lru)
        dy_lru, d_gain_l = _rms_bwd(dyn[:, 0:LRU_W], y_lru * rstd_l, rstd_l, pv[ROW_GL:ROW_GL + 1, :])
        dh = dy_lru * gl
        du_ref[:, LRU_W:2 * LRU_W] = (dy_lru * h * dgl).astype(BF16)
        a_s[...] = jnp.exp(saved(3))
        b_s[...] = a_s[...] * dh
        dh_s[...] = dh
        mu_s[pl.ds(tm, 8), :] = mc[...]
        mc[...] = _scan_tile(a_s, b_s, mu_s, mc[...], tm, reverse=True)
        xc, r, ig = saved(0), saved(1), saved(2)
        a, om, mult, rmult = _lru_decay(saved(3))
        lam_t = dh_s[...] + mu_s[pl.ds(1, tm), :]
        da = lam_t * e_h[pl.ds(7, tm), :]
        dmult = lam_t * (ig * xc)
        di = lam_t * (mult * xc)
        dxc = lam_t * (mult * ig)
        dla = da * a - jnp.where(om > 1e-12, dmult * ((a * a) * rmult), 0.0)
        dra = (dla * (-LRU_C * _softplus_neg_lambda(pv))) * (r * (1.0 - r))
        dia = di * (ig * (1.0 - ig))
        drab = dra.astype(BF16)
        diab = dia.astype(BF16)
        xcb = xc.astype(BF16)
        dxc = dxc + _bd_t(drab, wa_ref) + _bd_t(diab, wx_ref)
        dwa_ref[...] += _bd_grad(xcb, drab)
        dwx_ref[...] += _bd_grad(xcb, diab)
        sig_neg_lam = _sigmoid(-pv[ROW_LAM:ROW_LAM + 1, :])
        d_lam = jnp.sum(dla * r, axis=0, keepdims=True) * (LRU_C * sig_neg_lam)

        f_x[pl.ds(0, tm), :] = dxc
        f_x[pl.ds(tm, 8), :] = cx[...]
        du_lru = jnp.zeros((tm, LRU_W), F32)
        u_lru = u_ref[:, 0:LRU_W]
        d_cw = []
        for k in range(4):
            later = f_x[pl.ds(3 - k, tm), :]
            du_lru = du_lru + later * pv[ROW_CW + k:ROW_CW + k + 1, :]
            d_cw.append(jnp.sum(later * u_lru, axis=0, keepdims=True))
        du_ref[:, 0:LRU_W] = du_lru.astype(BF16)
        cx[...] = f_x[pl.ds(0, 8), :]

        pooled_b, zp, inv_cnts = _pool_pre(e_pool, pv, wp_ref, tm, it * tm)
        ps = pv[ROW_PS:ROW_PS + 1, :]
        y_pool = zp * ps
        rstd_p = _rstd(y_pool)
        dy_pool, d_gain_p = _rms_bwd(dyn[:, LRU_W:D_MODEL], y_pool * rstd_p, rstd_p, pv[ROW_GP:ROW_GP + 1, :])
        dz = dy_pool * ps
        dzb = dz.astype(BF16)
        dwp_ref[...] += _bd_grad(pooled_b, dzb)
        dpooled = _bd_t(dzb, wp_ref)
        for g, w in enumerate(POOL_WINDOWS):
            f_p[pl.ds(0, tm), pl.ds(128 * g, 128)] = dpooled[:, 128 * g:128 * (g + 1)] * inv_cnts[g]
        f_p[pl.ds(tm, HALO), :] = cp[...]
        for g, w in enumerate(POOL_WINDOWS):
            acc = _window_sum(f_p[:, pl.ds(128 * g, 128)], w, back=False)[0:tm, :]
            du_ref[:, 2 * LRU_W + 128 * g:2 * LRU_W + 128 * (g + 1)] = (
                acc - dpooled[:, 128 * g:128 * (g + 1)]).astype(BF16)
        cp[...] = f_p[pl.ds(0, HALO), :]

        rows = d_cw + [
            jnp.sum(dxc, axis=0, keepdims=True),
            jnp.sum(dra, axis=0, keepdims=True),
            jnp.sum(dia, axis=0, keepdims=True),
            d_lam,
            jnp.sum(dz, axis=0, keepdims=True),
            jnp.sum(dy_pool * zp, axis=0, keepdims=True),
            d_gain_l, d_gain_p,
            jnp.zeros((4, LRU_W), F32),
        ]
        vacc_ref[...] += jnp.concatenate(rows, axis=0)

        @pl.when(s == n_t - 1)
        def _():
            sg_ref[SG_VEC:SG_VEC + 16, :] = vacc_ref[:, 0:256]
            sg_ref[SG_VEC + 16:SG_VEC + 32, :] = vacc_ref[:, 256:512]
            for half in range(2):
                sg_ref[SG_WA + 64 * half:SG_WA + 64 * (half + 1), :] = _diag_pack(dwa_ref[half], 64)
                sg_ref[SG_WX + 64 * half:SG_WX + 64 * (half + 1), :] = _diag_pack(dwx_ref[half], 64)
                sg_ref[SG_WP + 128 * half:SG_WP + 128 * (half + 1), :] = _diag_pack(dwp_ref[half], 128)
            if exchange:
                exchange.finish()

    rev = lambda w: pl.BlockSpec((tm, w), lambda s: (n_t - 1 - s, 0))
    full = lambda shape: pl.BlockSpec(shape, lambda s: (0,) * len(shape))
    outs = pl.pallas_call(
        body, name="mixer_bwd", grid=(n_t,),
        in_specs=[rev(D_MODEL), rev(D_IN),
                  pl.BlockSpec((HALO, POOL_W), lambda s: (jnp.maximum((n_t - 1 - s) * (tm // HALO) - 1, 0), 2)),
                  rev(LRU_W),
                  pl.BlockSpec((8, LRU_W), lambda s: (jnp.maximum((n_t - 1 - s) * (tm // 8) - 1, 0), 0)),
                  rev(4 * LRU_W), full((16, LRU_W)), full((8, 64, 64)), full((8, 64, 64)), full((4, 128, 128)),
                  full((D_MODEL, D_MODEL))] + [ANY] * n_x,
        out_specs=[rev(D_IN), full((SG_ROWS, 256))] + [ANY] * n_x,
        out_shape=[_sds((T, D_IN), BF16), _sds((SG_ROWS, 256), F32)] + [_sds(a.shape, a.dtype) for a in chip_sums],
        scratch_shapes=[pltpu.VMEM((HALO + tm, POOL_W), F32),
                        pltpu.VMEM((8 + tm, LRU_W), F32)] + [pltpu.VMEM((tm, LRU_W), F32)] * 3 + [
                        pltpu.VMEM((tm + 8, LRU_W), F32), pltpu.VMEM((tm + 8, LRU_W), F32),
                        pltpu.VMEM((tm + HALO, POOL_W), F32), pltpu.VMEM((8, LRU_W), F32),
                        pltpu.VMEM((8, LRU_W), F32), pltpu.VMEM((HALO, POOL_W), F32)]
        + [pltpu.VMEM((2, 256, 256), BF16)] * 3 + [pltpu.VMEM((16, LRU_W), F32)] + [pltpu.VMEM((2, 256, 256), F32)] * 3
        + (_ChipExchange.scratch(n_x) if n_x else []),
        compiler_params=pltpu.CompilerParams(dimension_semantics=("arbitrary",)),
    )(d2, u, u, hs, hs, saved, pv, wa, wx, wp, w_out_b, *chip_sums)
    return outs[0], outs[1], list(outs[2:])


def _mix_in_bwd(du, x, d2, w_in_t, g_mix, tm, chip_sums=()):
    T = x.shape[0]
    n_t = T // tm
    n_x = len(chip_sums)

    def body(du_ref, x_ref, d2_ref, w_ref, g_ref, *rest):
        x_in, rest = rest[:n_x], rest[n_x:]
        dx_ref, dg_ref = rest[:2]
        exchange = _ChipExchange(x_in, rest[2:2 + n_x], *rest[2 + n_x:]) if n_x else None
        i = pl.program_id(0)

        @pl.when(i == 0)
        def _():
            dg_ref[...] = jnp.zeros((1, D_MODEL), F32)
            if exchange:
                exchange.start()

        dh = jnp.dot(du_ref[...], w_ref[...], preferred_element_type=F32)
        xv = x_ref[...]
        rstd = _rstd(xv)
        dx, dgain = _rms_bwd(dh, xv * rstd, rstd, g_ref[...])
        dx_ref[...] = d2_ref[...] + dx
        dg_ref[...] += dgain

        if exchange:
            @pl.when(i == n_t - 1)
            def _():
                exchange.finish()

    row = lambda w: pl.BlockSpec((tm, w), lambda i: (i, 0))
    const = lambda shape: pl.BlockSpec(shape, lambda i: (0,) * len(shape))
    outs = pl.pallas_call(
        body, name="mix_in_bwd", grid=(n_t,),
        in_specs=[row(D_IN), row(D_MODEL), row(D_MODEL), const((D_IN, D_MODEL)), const((1, D_MODEL))] + [ANY] * n_x,
        out_specs=[row(D_MODEL), const((1, D_MODEL))] + [ANY] * n_x,
        out_shape=[_sds((T, D_MODEL), F32), _sds((1, D_MODEL), F32)] + [_sds(a.shape, a.dtype) for a in chip_sums],
        scratch_shapes=_ChipExchange.scratch(n_x) if n_x else [],
        compiler_params=pltpu.CompilerParams(dimension_semantics=("arbitrary",)),
    )(du, x, d2, w_in_t, g_mix, *chip_sums)
    return outs[0], outs[1], list(outs[2:])


def _adamw(w, g, m, v):
    m = ADAM_B1 * m + (1.0 - ADAM_B1) * g
    v = ADAM_B2 * v + (1.0 - ADAM_B2) * (g * g)
    m_hat = m / (1.0 - ADAM_B1 ** ADAM_STEP)
    v_hat = v / (1.0 - ADAM_B2 ** ADAM_STEP)
    delta = -ADAM_LR * (m_hat / (jnp.sqrt(v_hat) + ADAM_EPS) + ADAM_WD * w)
    return delta, m, v


def _adam_shards(ws, ms, vs, parts):
    n = len(ws)
    n_blk = [w.shape[0] // ADAM_ROWS for w in ws]

    def body(*refs):
        w_refs, m_refs, v_refs, p_refs, outs = (refs[:n], refs[n:2 * n], refs[2 * n:3 * n], refs[3 * n:4 * n],
                                                refs[4 * n:])
        i = pl.program_id(0)
        for a in range(n):
            @pl.when(i < n_blk[a])
            def _(a=a):
                g = p_refs[a][0].astype(F32)
                for j in range(1, 4):
                    g = g + p_refs[a][j].astype(F32)
                delta, new_m, new_v = _adamw(w_refs[a][...], g, m_refs[a][...], v_refs[a][...])
                for kind, val in enumerate((g, delta, new_m, new_v)):
                    outs[4 * a + kind][...] = val

    blk = lambda a: pl.BlockSpec((ADAM_ROWS, D_MODEL), lambda i: (jnp.minimum(i, n_blk[a] - 1), 0))
    part_blk = lambda a: pl.BlockSpec((4, ADAM_ROWS, D_MODEL), lambda i: (0, jnp.minimum(i, n_blk[a] - 1), 0))
    res = pl.pallas_call(
        body, name="adam_shards", grid=(max(n_blk),),
        in_specs=[blk(a) for a in range(n)] * 3 + [part_blk(a) for a in range(n)],
        out_specs=[blk(a) for a in range(n) for _ in range(4)],
        out_shape=[_sds(w.shape, F32) for w in ws for _ in range(4)],
        compiler_params=pltpu.CompilerParams(dimension_semantics=("arbitrary",)),
    )(*ws, *ms, *vs, *parts)
    return [tuple(res[4 * a:4 * a + 4]) for a in range(n)]


SMALL_PARAMS = [("norm_mix_g", (1, D_MODEL)), ("conv_w", (1, 4, 64)), ("conv_b", (1, LRU_W)),
                ("gate_a_w", (1, 8, 64, 64)), ("gate_a_b", (1, LRU_W)), ("gate_x_w", (1, 8, 64, 64)),
                ("gate_x_b", (1, LRU_W)), ("lru_lambda", (1, LRU_W)), ("pool_w", (1, 4, 128, 128)),
                ("pool_b", (1, POOL_W)), ("pool_scale", (1, POOL_W)), ("norm_lru_g", (1, LRU_W)),
                ("norm_pool_g", (1, POOL_W)), ("norm_ffn_g", (1, D_MODEL)), ("final_norm_g", (1, D_MODEL))]
VEC_ROW = dict(conv_b=ROW_CB, gate_a_b=ROW_BA, gate_x_b=ROW_BX, lru_lambda=ROW_LAM, pool_b=ROW_PB, pool_scale=ROW_PS,
               norm_lru_g=ROW_GL, norm_pool_g=ROW_GP)
WHOLE = (Ellipsis,)


def _unpack_mixer_grads(sg, dev):
    vec = jnp.concatenate([sg[SG_VEC:SG_VEC + 16], sg[SG_VEC + 16:SG_VEC + 32]], axis=1)
    out = {nm: [(WHOLE, vec[r:r + 1])] for nm, r in VEC_ROW.items()}
    own = jnp.zeros((4, 64), F32)
    for d in range(N_DEV):
        own = jnp.where(dev == d, vec[ROW_CW:ROW_CW + 4, 64 * d:64 * (d + 1)], own)
    out["conv_w"] = [((0,), own)]
    for nm, row0 in (("gate_a_w", SG_WA), ("gate_x_w", SG_WX)):
        out[nm] = [((0, b), sg[row0 + 64 * (b // 4):row0 + 64 * (b // 4 + 1), 64 * (b % 4):64 * (b % 4 + 1)])
                   for b in range(8)]
    out["pool_w"] = [((0, b), sg[SG_WP + 128 * (b // 2):SG_WP + 128 * (b // 2 + 1), 128 * (b % 2):128 * (b % 2 + 1)])
                     for b in range(4)]
    return out


def _adam_small(parts, w, m, v):
    names = [nm for nm, _ in SMALL_PARAMS]
    n = len(names)

    def body(sg_ref, gm_ref, gf_ref, gn_ref, ls_ref, *rest):
        w_refs, m_refs, v_refs, outs = rest[:n], rest[n:2 * n], rest[2 * n:3 * n], rest[3 * n:]
        dev = 4 * lax.axis_index("x") + 2 * lax.axis_index("y") + lax.axis_index("c")

        def total(ref):
            acc = ref[0]
            for d in range(1, N_DEV):
                acc = acc + ref[d]
            return acc

        pieces = _unpack_mixer_grads(total(sg_ref), dev)
        pieces["norm_mix_g"] = [(WHOLE, total(gm_ref))]
        pieces["norm_ffn_g"] = [(WHOLE, total(gf_ref))]
        pieces["final_norm_g"] = [(WHOLE, total(gn_ref))]
        for i, nm in enumerate(names):
            for idx, g in pieces[nm]:
                delta, new_m, new_v = _adamw(w_refs[i][idx], g, m_refs[i][idx], v_refs[i][idx])
                for kind, val in enumerate((g, delta, new_m, new_v)):
                    outs[4 * i + kind][idx] = val
        outs[4 * n][...] = total(ls_ref)

    shapes = [_sds(shape, F32) for _, shape in SMALL_PARAMS for _ in range(4)] + [_sds((8, 128), F32)]
    res = pl.pallas_call(body, name="adam_small", out_shape=shapes)(
        *parts, *[w[nm] for nm in names], *[m[nm] for nm in names], *[v[nm] for nm in names])
    return {nm: tuple(res[4 * i:4 * i + 4]) for i, nm in enumerate(names)}, res[4 * n][0, 0]


def _vec_rows(conv_w_full, conv_b, ba, bx, lam, pb, ps, gl, gp):
    return jnp.concatenate([conv_w_full, conv_b, ba, bx, lam, pb, ps, gl, gp, jnp.zeros((4, LRU_W), F32)], axis=0)


WEIGHT_ORDER = ['norm_mix_g', 'w_in', 'conv_w', 'conv_b', 'gate_a_w', 'gate_a_b', 'gate_x_w', 'gate_x_b', 'lru_lambda',
                'pool_w', 'pool_b', 'pool_scale', 'norm_lru_g', 'norm_pool_g', 'w_out', 'norm_ffn_g', 'ffn_w1', 'ffn_w3',
                'ffn_w2', 'final_norm_g']


def kernel(x, norm_mix_g, w_in, conv_w, conv_b, gate_a_w, gate_a_b, gate_x_w, gate_x_b, lru_lambda, pool_w, pool_b, pool_scale, norm_lru_g, norm_pool_g, w_out, norm_ffn_g, ffn_w1, ffn_w3, ffn_w2, final_norm_g, loss_target, m_norm_mix_g, m_w_in, m_conv_w, m_conv_b, m_gate_a_w, m_gate_a_b, m_gate_x_w, m_gate_x_b, m_lru_lambda, m_pool_w, m_pool_b, m_pool_scale, m_norm_lru_g, m_norm_pool_g, m_w_out, m_norm_ffn_g, m_ffn_w1, m_ffn_w3, m_ffn_w2, m_final_norm_g, v_norm_mix_g, v_w_in, v_conv_w, v_conv_b, v_gate_a_w, v_gate_a_b, v_gate_x_w, v_gate_x_b, v_lru_lambda, v_pool_w, v_pool_b, v_pool_scale, v_norm_lru_g, v_norm_pool_g, v_w_out, v_norm_ffn_g, v_ffn_w1, v_ffn_w3, v_ffn_w2, v_final_norm_g):
    ac = lax.axis_index("c")
    tm, tmx, tn, tk = 512, 512, 1408, 1024
    tm_in = 1024
    xs, tgt = x[0], loss_target[0]
    g_fin = final_norm_g.reshape(1, D_MODEL)
    c_arr = jnp.reshape(ac, (1,)).astype(jnp.int32)

    tr = lambda w: jnp.swapaxes(w[0], 0, 1)
    own = lambda w: w[0]
    bf = lambda a: a.astype(BF16)

    g_in, g_conv = _all_gather([bf(tr(w_in)), conv_w[0]], "gather_w_in")
    w_in_t = g_in.reshape(D_IN, D_MODEL)
    conv_w_full = g_conv.transpose(1, 0, 2).reshape(4, LRU_W)
    pv = _vec_rows(conv_w_full, conv_b, gate_a_b, gate_x_b, lru_lambda, pool_b, pool_scale, norm_lru_g, norm_pool_g)
    wa, wx, wp = gate_a_w[0], gate_x_w[0], pool_w[0]

    u, h1, (g_out, g_w1) = _mix_in(xs, norm_mix_g, w_in_t, tm, shards=[bf(own(w_out)), bf(tr(ffn_w1))])
    w_out_b = g_out.reshape(D_MODEL, D_MODEL)
    y, hs, hres, h2, saved, (g_w3, g_w2) = _mixer_fwd(u, xs, pv, wa, wx, wp, w_out_b, norm_ffn_g, tmx,
                                               shards=[bf(tr(ffn_w3)), bf(own(ffn_w2))])
    w1_t, w3_t, w2_b = g_w1.reshape(D_FF, D_MODEL), g_w3.reshape(D_FF, D_MODEL), g_w2.reshape(D_FF, D_MODEL)
    g, v, d3, loss_acc, d_gfin = _ffn_fwd(hres, h2, w1_t, w3_t, w2_b, g_fin, tgt, tm, tn)

    dg, dv, ff, d2, d_gffn = _ffn_bwd(d3, g, v, w1_t, w3_t, w2_b, hres, norm_ffn_g, tm, tn)
    chips = lambda a: a.reshape(4, a.shape[0] // 4, a.shape[1])
    early_sums = [chips(_at_b_pair(y, d2, c_arr, "grad_w_out", tk)), chips(_at_b_pair(dg, h2, c_arr, "grad_w1", tk)),
                  chips(_at_b_pair(dv, h2, c_arr, "grad_w3", tk)), chips(_at_b_pair(ff, d3, c_arr, "grad_w2", tk))]
    du, d_mixer, early_parts = _mixer_bwd(d2, u, hs, saved, pv, wa, wx, wp, w_out_b, tmx, chip_sums=early_sums)
    grad_x, d_gmix, _ = _mix_in_bwd(du, xs, d2, w_in_t, norm_mix_g, tm_in)
    d_win, small_parts = _at_b_pair(du, h1, c_arr, "grad_w_in", tk,
                                    gather=[d_mixer, d_gmix, d_gffn, d_gfin, loss_acc])
    parts = [_half_exchange(chips(d_win), "grads_to_chips_w_in")] + list(early_parts)

    res = {}
    shard_w = dict(w_in=(w_in, m_w_in, v_w_in, tr), w_out=(w_out, m_w_out, v_w_out, own),
                   ffn_w1=(ffn_w1, m_ffn_w1, v_ffn_w1, tr), ffn_w3=(ffn_w3, m_ffn_w3, v_ffn_w3, tr),
                   ffn_w2=(ffn_w2, m_ffn_w2, v_ffn_w2, own))
    shard_res = _adam_shards([view(w) for w, _, _, view in shard_w.values()],
                             [view(m) for _, m, _, view in shard_w.values()],
                             [view(v) for _, _, v, view in shard_w.values()], parts)
    for (nm, (_, _, _, view)), outs in zip(shard_w.items(), shard_res):
        res[nm] = [(jnp.swapaxes(o, 0, 1) if view is tr else o)[None] for o in outs]

    row = lambda a: a.reshape(1, D_MODEL)
    small = lambda gm, cw, cb, wa_, ba, wx_, bx, lam, pw, pb, ps, gl, gp, gf, gn: dict(
        norm_mix_g=gm, conv_w=cw, conv_b=cb, gate_a_w=wa_, gate_a_b=ba, gate_x_w=wx_, gate_x_b=bx, lru_lambda=lam,
        pool_w=pw, pool_b=pb, pool_scale=ps, norm_lru_g=gl, norm_pool_g=gp, norm_ffn_g=gf, final_norm_g=row(gn))
    small_res, loss = _adam_small(
        small_parts,
        small(norm_mix_g, conv_w, conv_b, gate_a_w, gate_a_b, gate_x_w, gate_x_b, lru_lambda, pool_w, pool_b,
              pool_scale, norm_lru_g, norm_pool_g, norm_ffn_g, final_norm_g),
        small(m_norm_mix_g, m_conv_w, m_conv_b, m_gate_a_w, m_gate_a_b, m_gate_x_w, m_gate_x_b, m_lru_lambda, m_pool_w,
              m_pool_b, m_pool_scale, m_norm_lru_g, m_norm_pool_g, m_norm_ffn_g, m_final_norm_g),
        small(v_norm_mix_g, v_conv_w, v_conv_b, v_gate_a_w, v_gate_a_b, v_gate_x_w, v_gate_x_b, v_lru_lambda, v_pool_w,
              v_pool_b, v_pool_scale, v_norm_lru_g, v_norm_pool_g, v_norm_ffn_g, v_final_norm_g))
    for nm, outs in small_res.items():
        res[nm] = [o.reshape(D_MODEL) for o in outs] if nm == "final_norm_g" else list(outs)

    out = [loss, grad_x[None]]
    for kind in range(4):
        out += [res[nm][kind] for nm in WEIGHT_ORDER]
    return tuple(out)
```

```python
import jax
import jax.numpy as jnp
from jax import lax
from jax.experimental import pallas as pl
from jax.experimental.pallas import tpu as pltpu

F32 = jnp.float32
BF16 = jnp.bfloat16

D_MODEL = 1024
LRU_W = 512
POOL_W = 512
D_IN = 1536
D_FF = 2816
POOL_WINDOWS = (2, 4, 8, 16)
EPS = 1e-6
LRU_C = 8.0
N_DEV = 8
HALO = 16
SCAN_UNROLL = 4
ADAM_ROWS = 32

ADAM_LR = 0.001
ADAM_B1 = 0.9
ADAM_B2 = 0.999
ADAM_EPS = 1e-08
ADAM_WD = 0.01
ADAM_STEP = 10

ROW_CW, ROW_CB, ROW_BA, ROW_BX, ROW_LAM, ROW_PB, ROW_PS, ROW_GL, ROW_GP = 0, 4, 5, 6, 7, 8, 9, 10, 11
SG_VEC, SG_WA, SG_WX, SG_WP, SG_ROWS = 0, 32, 160, 288, 544

NT = (((1,), (1,)), ((), ()))
TN = (((0,), (0,)), ((), ()))


def _sds(shape, dtype):
    return jax.ShapeDtypeStruct(shape, dtype)


def _sigmoid(x):
    return 0.5 * jnp.tanh(0.5 * x) + 0.5


def _gelu_parts(x):
    c = 0.7978845608028654
    inner = c * (x + 0.044715 * (x * x * x))
    th = jnp.tanh(inner)
    g = 0.5 * x * (1.0 + th)
    dg = 0.5 * (1.0 + th) + 0.5 * x * (1.0 - th * th) * (c * (1.0 + 3.0 * 0.044715 * (x * x)))
    return g, dg


def _window_sum(ext, w, back):
    n = ext.shape[0]
    s, k = ext, 1
    while k < w:
        s = s + pltpu.roll(s, k if back else n - k, 0)
        k *= 2
    return s


def _rstd(x):
    return lax.rsqrt(jnp.mean(x * x, axis=-1, keepdims=True) + EPS)


def _rms_bwd(dy, xhat, rstd, gain):
    dxh = dy * gain
    dx = rstd * (dxh - xhat * jnp.mean(dxh * xhat, axis=-1, keepdims=True))
    return dx, jnp.sum(dy * xhat, axis=0, keepdims=True)


def _bd(xb, w_ref):
    return jnp.concatenate(
        [jnp.dot(xb[:, :256], w_ref[0], preferred_element_type=F32),
         jnp.dot(xb[:, 256:], w_ref[1], preferred_element_type=F32)], axis=1)


def _bd_t(xb, w_ref):
    return jnp.concatenate(
        [lax.dot_general(xb[:, :256], w_ref[0], NT, preferred_element_type=F32),
         lax.dot_general(xb[:, 256:], w_ref[1], NT, preferred_element_type=F32)], axis=1)


def _bd_grad(xb, db):
    return jnp.stack(
        [lax.dot_general(xb[:, :256], db[:, :256], TN, preferred_element_type=F32),
         lax.dot_general(xb[:, 256:], db[:, 256:], TN, preferred_element_type=F32)], axis=0)


def _fill_block_diag(dst, src_ref):
    n, k, _ = src_ref.shape
    dst[...] = jnp.zeros(dst.shape, BF16)
    for b in range(n):
        p, q = divmod(b, 256 // k)
        dst[p, q * k:(q + 1) * k, q * k:(q + 1) * k] = src_ref[b].astype(BF16)


def _diag_pack(w, k):
    lane = lax.broadcasted_iota(jnp.int32, (k, 256), 1)
    out = w[0:k]
    for q in range(1, 256 // k):
        out = jnp.where(lane >= q * k, w[q * k:(q + 1) * k], out)
    return out


def _y_pos(b):
    return 4 * (b % 2) + b // 2


def _softplus_neg_lambda(pv):
    z = -pv[ROW_LAM:ROW_LAM + 1, :]
    return jnp.maximum(z, 0.0) + jnp.log(1.0 + jnp.exp(-jnp.abs(z)))


def _lru_gates(e_lru, pv, wa_ref, wx_ref, tm):
    xc = pv[ROW_CB:ROW_CB + 1, :]
    for k in range(4):
        xc = xc + e_lru[pl.ds(HALO - 3 + k, tm), :] * pv[ROW_CW + k:ROW_CW + k + 1, :]
    xcb = xc.astype(BF16)
    r = _sigmoid(_bd(xcb, wa_ref) + pv[ROW_BA:ROW_BA + 1, :])
    ig = _sigmoid(_bd(xcb, wx_ref) + pv[ROW_BX:ROW_BX + 1, :])
    return xc, r, ig, (-LRU_C * r) * _softplus_neg_lambda(pv)


def _lru_decay(la):
    a = jnp.exp(la)
    om = -jnp.tanh(la) * (1.0 + a * a)
    omc = jnp.maximum(om, 1e-12)
    rmult = lax.rsqrt(omc)
    return a, om, omc * rmult, rmult


def _pool_pre(e_pool, pv, wp_ref, tm, t0):
    t = t0 + lax.broadcasted_iota(jnp.int32, (tm, 1), 0)
    parts, inv_cnts = [], []
    for g, w in enumerate(POOL_WINDOWS):
        ext = e_pool[:, pl.ds(128 * g, 128)]
        s = _window_sum(ext, w, back=True)[HALO:, :]
        inv_cnt = 1.0 / jnp.minimum(t + 1, w).astype(F32)
        inv_cnts.append(inv_cnt)
        parts.append(s * inv_cnt - ext[HALO:, :])
    pooled = jnp.concatenate(parts, axis=1)
    pooled_b = pooled.astype(BF16)
    zp = _bd(pooled_b, wp_ref) + pv[ROW_PB:ROW_PB + 1, :]
    return pooled_b, zp, inv_cnts


def _scan_tile(a_ref, b_ref, out_ref, carry, tm, reverse):
    row = lax.broadcasted_iota(jnp.int32, (8, LRU_W), 0)
    nblk = tm // 8

    def local_scan(blk):
        r0 = pl.multiple_of(blk * 8, 8)
        av = a_ref[pl.ds(r0, 8), :]
        bv = b_ref[pl.ds(r0, 8), :]
        for d in (1, 2, 4):
            sh = (8 - d) if reverse else d
            a_s = pltpu.roll(av, sh, 0)
            b_s = pltpu.roll(bv, sh, 0)
            m = (row < 8 - d) if reverse else (row >= d)
            bv = jnp.where(m, av * b_s + bv, bv)
            av = jnp.where(m, av * a_s, av)
        return r0, av, bv

    def step(i, hin):
        local = [local_scan((nblk - 1 - (i * SCAN_UNROLL + j)) if reverse else (i * SCAN_UNROLL + j))
                 for j in range(SCAN_UNROLL)]
        for r0, av, bv in local:
            hv = av * hin + bv
            out_ref[pl.ds(r0, 8), :] = hv
            hin = jnp.broadcast_to(hv[0:1, :] if reverse else hv[7:8, :], (8, LRU_W))
        return hin

    return lax.fori_loop(0, nblk // SCAN_UNROLL, step, carry)


MESH = pl.DeviceIdType.MESH
ANY = pl.BlockSpec(memory_space=pl.ANY)


def _place():
    x, y, c = lax.axis_index("x"), lax.axis_index("y"), lax.axis_index("c")
    chips = [(1 - x, y), (x, 1 - y), (1 - x, 1 - y)]
    return x, y, c, chips


class _Gather:
    def __init__(self, ins, outs, send_sems, recv_sems, local_sems, core_major=False):
        self.ins, self.outs, self.n = ins, outs, len(ins)
        self.send_sems, self.recv_sems, self.local_sems = send_sems, recv_sems, local_sems
        self.core_major = core_major

    @staticmethod
    def scratch(n):
        return [pltpu.SemaphoreType.DMA((7, n)), pltpu.SemaphoreType.DMA((7, n)), pltpu.SemaphoreType.DMA((n,))]

    def _slot(self, a, px, py, pc):
        return self.outs[a].at[4 * pc + 2 * px + py if self.core_major else 4 * px + 2 * py + pc]

    def _copy(self, a, k, block, to, src=None):
        return pltpu.make_async_remote_copy(
            src_ref=self._slot(a, *block) if src is None else src, dst_ref=self._slot(a, *block),
            send_sem=self.send_sems.at[k, a], recv_sem=self.recv_sems.at[k, a], device_id=to, device_id_type=MESH)

    def _mine(self, a):
        x, y, c, _ = _place()
        return pltpu.make_async_copy(self.ins[a], self._slot(a, x, y, c), self.local_sems.at[a])

    def _first(self, a):
        x, y, c, chips = _place()
        me = (x, y, c)
        return ([self._copy(a, 0, me, (x, y, 1 - c), src=self.ins[a])]
                + [self._copy(a, 1 + j, me, (*chip, c), src=self.ins[a]) for j, chip in enumerate(chips)])

    def start(self):
        for a in range(self.n):
            self._mine(a).start()
        for a in range(self.n):
            for cp in self._first(a):
                cp.start()

    def finish(self):
        x, y, c, chips = _place()
        me, sibling = (x, y, c), (x, y, 1 - c)
        passed = []
        for j, chip in enumerate(chips):
            for a in range(self.n):
                self._copy(a, 1 + j, (*chip, c), me).wait_recv()
                fwd = self._copy(a, 4 + j, (*chip, c), sibling)
                fwd.start()
                passed.append(fwd)
        for a in range(self.n):
            self._copy(a, 0, (x, y, 1 - c), me).wait_recv()
            for j, chip in enumerate(chips):
                self._copy(a, 4 + j, (*chip, 1 - c), me).wait_recv()
        for a in range(self.n):
            for cp in self._first(a):
                cp.wait_send()
        for cp in passed:
            cp.wait_send()
        for a in range(self.n):
            self._mine(a).wait()


def _half_exchange(arr, name):
    def body(in_ref, out_ref, send_sems, recv_sems, local_sem):
        x, y, c, _ = _place()
        my_chip = 2 * x + y

        def send(j, wait):
            to_me = (c == x) & (y == j // 2) & (c == j % 2)

            @pl.when(to_me)
            def _():
                local = pltpu.make_async_copy(in_ref.at[j], out_ref.at[my_chip], local_sem)
                local.wait() if wait else local.start()

            @pl.when(jnp.logical_not(to_me))
            def _():
                remote = pltpu.make_async_remote_copy(
                    src_ref=in_ref.at[j], dst_ref=out_ref.at[my_chip], send_sem=send_sems.at[j],
                    recv_sem=recv_sems.at[my_chip], device_id=(c, j // 2, j % 2), device_id_type=MESH)
                remote.wait_send() if wait else remote.start()

        for j in range(4):
            send(j, wait=False)
        for j in range(4):
            send(j, wait=True)
        for k in range(4):
            from_me = (k // 2 == x) & (k % 2 == y) & (c == x)

            @pl.when(jnp.logical_not(from_me))
            def _():
                pltpu.make_async_remote_copy(
                    src_ref=in_ref.at[0], dst_ref=out_ref.at[k], send_sem=send_sems.at[0], recv_sem=recv_sems.at[k],
                    device_id=(k // 2, k % 2, x), device_id_type=MESH).wait_recv()

    return pl.pallas_call(
        body, name=name, out_shape=_sds(arr.shape, arr.dtype), in_specs=[ANY], out_specs=ANY,
        scratch_shapes=[pltpu.SemaphoreType.DMA((4,)), pltpu.SemaphoreType.DMA((4,)), pltpu.SemaphoreType.DMA],
    )(arr)


class _ChipExchange:
    def __init__(self, ins, outs, send_sems, recv_sems, local_sems):
        self.ins, self.outs, self.n = ins, outs, len(ins)
        self.send_sems, self.recv_sems, self.local_sems = send_sems, recv_sems, local_sems

    @staticmethod
    def scratch(n):
        return [pltpu.SemaphoreType.DMA((3, n)), pltpu.SemaphoreType.DMA((3, n)), pltpu.SemaphoreType.DMA((n,))]

    def _local(self, a):
        x, y, _, _ = _place()
        me = 2 * x + y
        return pltpu.make_async_copy(self.ins[a].at[me], self.outs[a].at[me], self.local_sems.at[a])

    def _copies(self, a):
        x, y, c, chips = _place()
        me = 2 * x + y
        return [(pltpu.make_async_remote_copy(
                     src_ref=self.ins[a].at[2 * px + py], dst_ref=self.outs[a].at[me],
                     send_sem=self.send_sems.at[k, a], recv_sem=self.recv_sems.at[k, a],
                     device_id=(px, py, c), device_id_type=MESH),
                 pltpu.make_async_remote_copy(
                     src_ref=self.ins[a].at[me], dst_ref=self.outs[a].at[2 * px + py],
                     send_sem=self.send_sems.at[k, a], recv_sem=self.recv_sems.at[k, a],
                     device_id=(px, py, c), device_id_type=MESH))
                for k, (px, py) in enumerate(chips)]

    def start(self):
        for a in range(self.n):
            self._local(a).start()
        for a in range(self.n):
            for send, _ in self._copies(a):
                send.start()

    def finish(self):
        for a in range(self.n):
            for send, recv in self._copies(a):
                send.wait_send()
                recv.wait_recv()
        for a in range(self.n):
            self._local(a).wait()


def _mix_in(x, g_mix, w_in_own, conv_w_own, tm, shards):
    T = x.shape[0]
    n_t = T // tm
    n_s = len(shards)

    def body(x_ref, g_ref, w_own, cw_own, *rest):
        sh_in, rest = rest[:n_s], rest[n_s:]
        u_ref, h_ref, w_hbm, cw_all = rest[:4]
        sh_out, rest = rest[4:4 + n_s], rest[4 + n_s:]
        w_all, keep_sem = rest[:2]
        first = _Gather([w_own, cw_own], [w_all, cw_all], *rest[2:5])
        later = _Gather(sh_in, sh_out, *rest[5:8], core_major=True)
        keep = pltpu.make_async_copy(w_all, w_hbm, keep_sem)
        i = pl.program_id(0)

        @pl.when(i == 0)
        def _():
            first.start()
            later.start()
            first.finish()
            keep.start()

        xv = x_ref[...]
        h = (xv * _rstd(xv) * g_ref[...]).astype(BF16)
        h_ref[...] = h
        u_ref[...] = lax.dot_general(h, w_all[...].reshape(D_IN, D_MODEL), NT, preferred_element_type=F32)

        @pl.when(i == n_t - 1)
        def _():
            later.finish()
            keep.wait()

    own = (D_IN // N_DEV, D_MODEL)
    outs = pl.pallas_call(
        body, name="mix_in", grid=(n_t,),
        in_specs=[pl.BlockSpec((tm, D_MODEL), lambda i: (i, 0)), pl.BlockSpec((1, D_MODEL), lambda i: (0, 0))]
        + [ANY] * (2 + n_s),
        out_specs=[pl.BlockSpec((tm, D_IN), lambda i: (i, 0)), pl.BlockSpec((tm, D_MODEL), lambda i: (i, 0))]
        + [ANY] * (2 + n_s),
        out_shape=[_sds((T, D_IN), F32), _sds((T, D_MODEL), BF16), _sds((N_DEV,) + own, BF16),
                   _sds((N_DEV,) + conv_w_own.shape, F32)] + [_sds((N_DEV,) + a.shape, a.dtype) for a in shards],
        scratch_shapes=[pltpu.VMEM((N_DEV,) + own, BF16), pltpu.SemaphoreType.DMA] + _Gather.scratch(2)
        + _Gather.scratch(n_s),
        compiler_params=pltpu.CompilerParams(dimension_semantics=("arbitrary",)),
    )(x, g_mix, w_in_own, conv_w_own, *shards)
    return outs[0], outs[1], outs[2], outs[3], list(outs[4:])


def _mixer_fwd(u, x, pv, wa, wx, wp, w_out_b, g_ffn, tm, shards=()):
    T = u.shape[0]
    n_s = len(shards)
    n_t = T // tm

    def body(u_ref, x_ref, pv_ref, wa_in, wx_in, wp_in, wo_ref, gf_ref, *rest):
        sh_in, rest = rest[:n_s], rest[n_s:]
        y_ref, hs_ref, hres_ref, h2_ref, saved_ref = rest[:5]
        sh_out, rest = rest[5:5 + n_s], rest[5 + n_s:]
        e_lru, e_pool, a_s, b_s, hc, wa_ref, wx_ref, wp_ref = rest[:8]
        gather = _Gather(sh_in, sh_out, *rest[8:], core_major=True) if n_s else None
        i = pl.program_id(0)

        @pl.when(i == 0)
        def _():
            if gather:
                gather.start()
            e_lru[pl.ds(0, HALO), :] = jnp.zeros((HALO, LRU_W), F32)
            e_pool[pl.ds(0, HALO), :] = jnp.zeros((HALO, POOL_W), F32)
            hc[...] = jnp.zeros((8, LRU_W), F32)
            _fill_block_diag(wa_ref, wa_in)
            _fill_block_diag(wx_ref, wx_in)
            _fill_block_diag(wp_ref, wp_in)

        e_lru[pl.ds(HALO, tm), :] = u_ref[:, 0:LRU_W]
        e_pool[pl.ds(HALO, tm), :] = u_ref[:, 2 * LRU_W:D_IN]
        pv = pv_ref[...]
        xc, r, ig, la = _lru_gates(e_lru, pv, wa_ref, wx_ref, tm)
        for q, val in enumerate((xc, r, ig, la)):
            saved_ref[:, LRU_W * q:LRU_W * (q + 1)] = val
        a, _, mult, _ = _lru_decay(la)
        a_s[...] = a
        b_s[...] = mult * (ig * xc)
        hc[...] = _scan_tile(a_s, b_s, hs_ref, hc[...], tm, reverse=False)
        gl, _ = _gelu_parts(u_ref[:, LRU_W:2 * LRU_W])
        y_lru = hs_ref[...] * gl
        _, zp, _ = _pool_pre(e_pool, pv, wp_ref, tm, i * tm)
        y_pool = zp * pv[ROW_PS:ROW_PS + 1, :]
        yn = jnp.concatenate([y_lru * _rstd(y_lru) * pv[ROW_GL:ROW_GL + 1, :],
                              y_pool * _rstd(y_pool) * pv[ROW_GP:ROW_GP + 1, :]], axis=1).astype(BF16)
        for b in range(N_DEV):
            y_ref[:, 128 * _y_pos(b):128 * (_y_pos(b) + 1)] = yn[:, 128 * b:128 * (b + 1)]
        hr = x_ref[...] + jnp.dot(y_ref[...], wo_ref[...], preferred_element_type=F32)
        hres_ref[...] = hr
        h2_ref[...] = (hr * _rstd(hr) * gf_ref[...]).astype(BF16)
        e_lru[pl.ds(0, HALO), :] = e_lru[pl.ds(tm, HALO), :]
        e_pool[pl.ds(0, HALO), :] = e_pool[pl.ds(tm, HALO), :]

        if gather:
            @pl.when(i == n_t - 1)
            def _():
                gather.finish()

    full = lambda shape: pl.BlockSpec(shape, lambda i: (0,) * len(shape))
    row = lambda w: pl.BlockSpec((tm, w), lambda i: (i, 0))
    outs = pl.pallas_call(
        body, name="mixer_fwd", grid=(n_t,),
        in_specs=[row(D_IN), row(D_MODEL), full((16, LRU_W)), full((8, 64, 64)), full((8, 64, 64)), full((4, 128, 128)),
                  full((D_MODEL, D_MODEL)), full((1, D_MODEL))] + [ANY] * n_s,
        out_specs=[row(D_MODEL), row(LRU_W), row(D_MODEL), row(D_MODEL), row(4 * LRU_W)] + [ANY] * n_s,
        out_shape=[_sds((T, D_MODEL), BF16), _sds((T, LRU_W), F32), _sds((T, D_MODEL), F32), _sds((T, D_MODEL), BF16),
                   _sds((T, 4 * LRU_W), F32)] + [_sds((N_DEV,) + a.shape, a.dtype) for a in shards],
        scratch_shapes=[pltpu.VMEM((HALO + tm, LRU_W), F32), pltpu.VMEM((HALO + tm, POOL_W), F32),
                        pltpu.VMEM((tm, LRU_W), F32), pltpu.VMEM((tm, LRU_W), F32), pltpu.VMEM((8, LRU_W), F32)]
        + [pltpu.VMEM((2, 256, 256), BF16)] * 3 + (_Gather.scratch(n_s) if n_s else []),
        compiler_params=pltpu.CompilerParams(dimension_semantics=("arbitrary",)),
    )(u, x, pv, wa, wx, wp, w_out_b, g_ffn, *shards)
    return outs[0], outs[1], outs[2], outs[3], outs[4], list(outs[5:])


def _ffn_fwd(hres, h2, w1_b, w3_b, w2_b, g_fin, tgt, tm, tn):
    T = hres.shape[0]
    n_j = D_FF // tn

    def body(hres_ref, h2_ref, w1_ref, w3_ref, w2_ref, gfin_ref, tgt_ref,
             g_ref, v_ref, d3_ref, loss_ref, dgfin_ref, acc):
        i, j = pl.program_id(0), pl.program_id(1)

        @pl.when(j == 0)
        def _():
            acc[...] = jnp.zeros((tm, D_MODEL), F32)

        @pl.when((j == 0) & (i == 0))
        def _():
            loss_ref[...] = jnp.zeros((8, 128), F32)
            dgfin_ref[...] = jnp.zeros((1, D_MODEL), F32)

        h2 = h2_ref[...]
        g = lax.dot_general(h2, w1_ref[...], NT, preferred_element_type=F32)
        v = lax.dot_general(h2, w3_ref[...], NT, preferred_element_type=F32)
        g_ref[...] = g.astype(BF16)
        v_ref[...] = v.astype(BF16)
        ff = ((g * _sigmoid(g)) * v).astype(BF16)
        acc[...] += jnp.dot(ff, w2_ref[...], preferred_element_type=F32)

        @pl.when(j == n_j - 1)
        def _():
            h3 = hres_ref[...] + acc[...]
            rstd = _rstd(h3)
            xh = h3 * rstd
            gfin = gfin_ref[...]
            err = xh * gfin - tgt_ref[...]
            loss_ref[...] += 0.5 * jnp.sum(jnp.mean(err * err, axis=-1, keepdims=True))
            dout = err * (1.0 / D_MODEL)
            dx, dgain = _rms_bwd(dout, xh, rstd, gfin)
            d3_ref[...] = dx
            dgfin_ref[...] += dgain

    row = lambda w: pl.BlockSpec((tm, w), lambda i, j: (i, 0))
    const = lambda shape: pl.BlockSpec(shape, lambda i, j: (0,) * len(shape))
    return pl.pallas_call(
        body, name="ffn_fwd", grid=(T // tm, n_j),
        in_specs=[row(D_MODEL), row(D_MODEL),
                  pl.BlockSpec((tn, D_MODEL), lambda i, j: (j, 0)), pl.BlockSpec((tn, D_MODEL), lambda i, j: (j, 0)),
                  pl.BlockSpec((tn, D_MODEL), lambda i, j: (j, 0)), const((1, D_MODEL)), row(D_MODEL)],
        out_specs=[pl.BlockSpec((tm, tn), lambda i, j: (i, j)), pl.BlockSpec((tm, tn), lambda i, j: (i, j)),
                   row(D_MODEL), const((8, 128)), const((1, D_MODEL))],
        out_shape=[_sds((T, D_FF), BF16), _sds((T, D_FF), BF16),
                   _sds((T, D_MODEL), F32), _sds((8, 128), F32), _sds((1, D_MODEL), F32)],
        scratch_shapes=[pltpu.VMEM((tm, D_MODEL), F32)],
        compiler_params=pltpu.CompilerParams(dimension_semantics=("arbitrary", "arbitrary")),
    )(hres, h2, w1_b, w3_b, w2_b, g_fin, tgt)


def _ffn_bwd(d3, g, v, w1_b, w3_b, w2_b, hres, g_ffn, tm, tn):
    T = d3.shape[0]
    n_j = D_FF // tn

    def body(d3_ref, g_ref, v_ref, w1_ref, w3_ref, w2_ref, hres_ref, gf_ref,
             dg_ref, dv_ref, ff_ref, d2_ref, dgffn_ref, acc):
        i, j = pl.program_id(0), pl.program_id(1)

        @pl.when(j == 0)
        def _():
            acc[...] = jnp.zeros((tm, D_MODEL), F32)

        @pl.when((j == 0) & (i == 0))
        def _():
            dgffn_ref[...] = jnp.zeros((1, D_MODEL), F32)

        dff = lax.dot_general(d3_ref[...].astype(BF16), w2_ref[...], NT, preferred_element_type=F32)
        gv = g_ref[...].astype(F32)
        vv = v_ref[...].astype(F32)
        sg = _sigmoid(gv)
        sl = gv * sg
        dgb = (dff * vv * (sg * (1.0 + gv * (1.0 - sg)))).astype(BF16)
        dvb = (dff * sl).astype(BF16)
        dg_ref[...] = dgb
        dv_ref[...] = dvb
        ff_ref[...] = (sl * vv).astype(BF16)
        acc[...] += (jnp.dot(dgb, w1_ref[...], preferred_element_type=F32)
                     + jnp.dot(dvb, w3_ref[...], preferred_element_type=F32))

        @pl.when(j == n_j - 1)
        def _():
            hr = hres_ref[...]
            rstd = _rstd(hr)
            dx, dgain = _rms_bwd(acc[...], hr * rstd, rstd, gf_ref[...])
            d2_ref[...] = d3_ref[...] + dx
            dgffn_ref[...] += dgain

    row = lambda w: pl.BlockSpec((tm, w), lambda i, j: (i, 0))
    tile = pl.BlockSpec((tm, tn), lambda i, j: (i, j))
    const = lambda shape: pl.BlockSpec(shape, lambda i, j: (0,) * len(shape))
    return pl.pallas_call(
        body, name="ffn_bwd", grid=(T // tm, n_j),
        in_specs=[row(D_MODEL), tile, tile,
                  pl.BlockSpec((tn, D_MODEL), lambda i, j: (j, 0)), pl.BlockSpec((tn, D_MODEL), lambda i, j: (j, 0)),
                  pl.BlockSpec((tn, D_MODEL), lambda i, j: (j, 0)), row(D_MODEL), const((1, D_MODEL))],
        out_specs=[tile, tile, tile, row(D_MODEL), const((1, D_MODEL))],
        out_shape=[_sds((T, D_FF), BF16), _sds((T, D_FF), BF16), _sds((T, D_FF), BF16),
                   _sds((T, D_MODEL), F32), _sds((1, D_MODEL), F32)],
        scratch_shapes=[pltpu.VMEM((tm, D_MODEL), F32)],
        compiler_params=pltpu.CompilerParams(dimension_semantics=("arbitrary", "arbitrary")),
    )(d3, g, v, w1_b, w3_b, w2_b, hres, g_ffn)


def _at_b_pair(a, b, c_arr, name, tk, gather=()):
    T, M = a.shape
    N = b.shape[1]
    hm, n_k = M // 2, T // tk
    n_g = len(gather)

    def body(c_ref, a_ref, b_ref, *rest):
        g_in, o_ref, rest = rest[:n_g], rest[n_g], rest[n_g + 1:]
        g_out, rest = rest[:n_g], rest[n_g:]
        acc, landed, send_sem, recv_sem = rest[:4]
        ag = _Gather(g_in, g_out, *rest[4:]) if n_g else None
        ph, k = pl.program_id(0), pl.program_id(1)

        def hand_over():
            x, y, c, _ = _place()
            return pltpu.make_async_remote_copy(
                src_ref=acc.at[0], dst_ref=landed, send_sem=send_sem, recv_sem=recv_sem,
                device_id=(x, y, 1 - c), device_id_type=MESH)

        if ag:
            @pl.when((ph == 0) & (k == 0))
            def _():
                ag.start()

        @pl.when(k == 0)
        def _():
            acc[ph] = jnp.zeros((hm, N), F32)

        acc[ph] += lax.dot_general(a_ref[...].astype(BF16), b_ref[...].astype(BF16), TN, preferred_element_type=F32)

        @pl.when((ph == 0) & (k == n_k - 1))
        def _():
            hand_over().start()

        @pl.when((ph == 1) & (k == n_k - 1))
        def _():
            copy = hand_over()
            copy.wait_recv()
            o_ref[...] = (acc[1] + landed[...]).astype(BF16)
            copy.wait_send()
            if ag:
                ag.finish()

    outs = pl.pallas_call(
        body, name=name,
        grid_spec=pltpu.PrefetchScalarGridSpec(
            num_scalar_prefetch=1, grid=(2, n_k),
            in_specs=[pl.BlockSpec((tk, hm), lambda ph, k, c_ref: (k, (ph + 1 - c_ref[0]) % 2)),
                      pl.BlockSpec((tk, N), lambda ph, k, c_ref: (k, 0))] + [ANY] * n_g,
            out_specs=[pl.BlockSpec((hm, N), lambda ph, k, c_ref: (0, 0))] + [ANY] * n_g,
            scratch_shapes=[pltpu.VMEM((2, hm, N), F32), pltpu.VMEM((hm, N), F32),
                            pltpu.SemaphoreType.DMA, pltpu.SemaphoreType.DMA] + (_Gather.scratch(n_g) if n_g else [])),
        out_shape=[_sds((hm, N), BF16)] + [_sds((N_DEV,) + g.shape, g.dtype) for g in gather],
        compiler_params=pltpu.CompilerParams(dimension_semantics=("arbitrary", "arbitrary")),
    )(c_arr, a, b, *gather)
    return (outs[0], list(outs[1:])) if n_g else outs[0]


def _mixer_bwd(d2, u, hs, saved, pv, wa, wx, wp, w_out_b, tm, chip_sums=()):
    T = u.shape[0]
    n_t = T // tm
    n_x = len(chip_sums)

    def body(d2_ref, u_ref, uh_ref, hs_ref, hh_ref, saved_ref, pv_ref, wa_in, wx_in, wp_in, wo_ref, *rest):
        x_in, rest = rest[:n_x], rest[n_x:]
        du_ref, sg_ref = rest[:2]
        x_out, rest = rest[2:2 + n_x], rest[2 + n_x:]
        e_pool, e_h, a_s, b_s, dh_s, mu_s, f_x, f_p, mc, cx, cp = rest[:11]
        wa_ref, wx_ref, wp_ref, vacc_ref, dwa_ref, dwx_ref, dwp_ref = rest[11:18]
        exchange = _ChipExchange(x_in, x_out, *rest[18:]) if n_x else None
        s = pl.program_id(0)
        it = n_t - 1 - s

        @pl.when(s == 0)
        def _():
            if exchange:
                exchange.start()
            mc[...] = jnp.zeros((8, LRU_W), F32)
            cx[...] = jnp.zeros((8, LRU_W), F32)
            cp[...] = jnp.zeros((HALO, POOL_W), F32)
            vacc_ref[...] = jnp.zeros((16, LRU_W), F32)
            dwa_ref[...] = jnp.zeros((2, 256, 256), F32)
            dwx_ref[...] = jnp.zeros((2, 256, 256), F32)
            dwp_ref[...] = jnp.zeros((2, 256, 256), F32)
            _fill_block_diag(wa_ref, wa_in)
            _fill_block_diag(wx_ref, wx_in)
            _fill_block_diag(wp_ref, wp_in)

        first = it == 0
        e_pool[pl.ds(0, HALO), :] = jnp.where(first, 0.0, uh_ref[...])
        e_pool[pl.ds(HALO, tm), :] = u_ref[:, 2 * LRU_W:D_IN]
        e_h[pl.ds(0, 8), :] = jnp.where(first, 0.0, hh_ref[...])
        e_h[pl.ds(8, tm), :] = hs_ref[...]
        pv = pv_ref[...]
        saved = lambda q: saved_ref[:, LRU_W * q:LRU_W * (q + 1)]

        dyn = lax.dot_general(d2_ref[...].astype(BF16), wo_ref[...], NT, preferred_element_type=F32)
        dyn = jnp.concatenate([dyn[:, 128 * _y_pos(b):128 * (_y_pos(b) + 1)] for b in range(N_DEV)], axis=1)

        h = hs_ref[...]
        ug = u_ref[:, LRU_W:2 * LRU_W]
        gl, dgl = _gelu_parts(ug)
        y_lru = h * gl
        rstd_l = _rstd(y_lru)
        dy_lru, d_gain_l = _rms_bwd(dyn[:, 0:LRU_W], y_lru * rstd_l, rstd_l, pv[ROW_GL:ROW_GL + 1, :])
        dh = dy_lru * gl
        du_ref[:, LRU_W:2 * LRU_W] = (dy_lru * h * dgl).astype(BF16)
        a_s[...] = jnp.exp(saved(3))
        b_s[...] = a_s[...] * dh
        dh_s[...] = dh
        mu_s[pl.ds(tm, 8), :] = mc[...]
        mc[...] = _scan_tile(a_s, b_s, mu_s, mc[...], tm, reverse=True)
        xc, r, ig = saved(0), saved(1), saved(2)
        a, om, mult, rmult = _lru_decay(saved(3))
        lam_t = dh_s[...] + mu_s[pl.ds(1, tm), :]
        da = lam_t * e_h[pl.ds(7, tm), :]
        dmult = lam_t * (ig * xc)
        di = lam_t * (mult * xc)
        dxc = lam_t * (mult * ig)
        dla = da * a - jnp.where(om > 1e-12, dmult * ((a * a) * rmult), 0.0)
        dra = (dla * (-LRU_C * _softplus_neg_lambda(pv))) * (r * (1.0 - r))
        dia = di * (ig * (1.0 - ig))
        drab = dra.astype(BF16)
        diab = dia.astype(BF16)
        xcb = xc.astype(BF16)
        dxc = dxc + _bd_t(drab, wa_ref) + _bd_t(diab, wx_ref)
        dwa_ref[...] += _bd_grad(xcb, drab)
        dwx_ref[...] += _bd_grad(xcb, diab)
        sig_neg_lam = _sigmoid(-pv[ROW_LAM:ROW_LAM + 1, :])
        d_lam = jnp.sum(dla * r, axis=0, keepdims=True) * (LRU_C * sig_neg_lam)

        f_x[pl.ds(0, tm), :] = dxc
        f_x[pl.ds(tm, 8), :] = cx[...]
        du_lru = jnp.zeros((tm, LRU_W), F32)
        u_lru = u_ref[:, 0:LRU_W]
        d_cw = []
        for k in range(4):
            later = f_x[pl.ds(3 - k, tm), :]
            du_lru = du_lru + later * pv[ROW_CW + k:ROW_CW + k + 1, :]
            d_cw.append(jnp.sum(later * u_lru, axis=0, keepdims=True))
        du_ref[:, 0:LRU_W] = du_lru.astype(BF16)
        cx[...] = f_x[pl.ds(0, 8), :]

        pooled_b, zp, inv_cnts = _pool_pre(e_pool, pv, wp_ref, tm, it * tm)
        ps = pv[ROW_PS:ROW_PS + 1, :]
        y_pool = zp * ps
        rstd_p = _rstd(y_pool)
        dy_pool, d_gain_p = _rms_bwd(dyn[:, LRU_W:D_MODEL], y_pool * rstd_p, rstd_p, pv[ROW_GP:ROW_GP + 1, :])
        dz = dy_pool * ps
        dzb = dz.astype(BF16)
        dwp_ref[...] += _bd_grad(pooled_b, dzb)
        dpooled = _bd_t(dzb, wp_ref)
        for g, w in enumerate(POOL_WINDOWS):
            f_p[pl.ds(0, tm), pl.ds(128 * g, 128)] = dpooled[:, 128 * g:128 * (g + 1)] * inv_cnts[g]
        f_p[pl.ds(tm, HALO), :] = cp[...]
        for g, w in enumerate(POOL_WINDOWS):
            acc = _window_sum(f_p[:, pl.ds(128 * g, 128)], w, back=False)[0:tm, :]
            du_ref[:, 2 * LRU_W + 128 * g:2 * LRU_W + 128 * (g + 1)] = (
                acc - dpooled[:, 128 * g:128 * (g + 1)]).astype(BF16)
        cp[...] = f_p[pl.ds(0, HALO), :]

        rows = d_cw + [
            jnp.sum(dxc, axis=0, keepdims=True),
            jnp.sum(dra, axis=0, keepdims=True),
            jnp.sum(dia, axis=0, keepdims=True),
            d_lam,
            jnp.sum(dz, axis=0, keepdims=True),
            jnp.sum(dy_pool * zp, axis=0, keepdims=True),
            d_gain_l, d_gain_p,
            jnp.zeros((4, LRU_W), F32),
        ]
        vacc_ref[...] += jnp.concatenate(rows, axis=0)

        @pl.when(s == n_t - 1)
        def _():
            sg_ref[SG_VEC:SG_VEC + 16, :] = vacc_ref[:, 0:256]
            sg_ref[SG_VEC + 16:SG_VEC + 32, :] = vacc_ref[:, 256:512]
            for half in range(2):
                sg_ref[SG_WA + 64 * half:SG_WA + 64 * (half + 1), :] = _diag_pack(dwa_ref[half], 64)
                sg_ref[SG_WX + 64 * half:SG_WX + 64 * (half + 1), :] = _diag_pack(dwx_ref[half], 64)
                sg_ref[SG_WP + 128 * half:SG_WP + 128 * (half + 1), :] = _diag_pack(dwp_ref[half], 128)
            if exchange:
                exchange.finish()

    rev = lambda w: pl.BlockSpec((tm, w), lambda s: (n_t - 1 - s, 0))
    full = lambda shape: pl.BlockSpec(shape, lambda s: (0,) * len(shape))
    outs = pl.pallas_call(
        body, name="mixer_bwd", grid=(n_t,),
        in_specs=[rev(D_MODEL), rev(D_IN),
                  pl.BlockSpec((HALO, POOL_W), lambda s: (jnp.maximum((n_t - 1 - s) * (tm // HALO) - 1, 0), 2)),
                  rev(LRU_W),
                  pl.BlockSpec((8, LRU_W), lambda s: (jnp.maximum((n_t - 1 - s) * (tm // 8) - 1, 0), 0)),
                  rev(4 * LRU_W), full((16, LRU_W)), full((8, 64, 64)), full((8, 64, 64)), full((4, 128, 128)),
                  full((D_MODEL, D_MODEL))] + [ANY] * n_x,
        out_specs=[rev(D_IN), full((SG_ROWS, 256))] + [ANY] * n_x,
        out_shape=[_sds((T, D_IN), BF16), _sds((SG_ROWS, 256), F32)] + [_sds(a.shape, a.dtype) for a in chip_sums],
        scratch_shapes=[pltpu.VMEM((HALO + tm, POOL_W), F32),
                        pltpu.VMEM((8 + tm, LRU_W), F32)] + [pltpu.VMEM((tm, LRU_W), F32)] * 3 + [
                        pltpu.VMEM((tm + 8, LRU_W), F32), pltpu.VMEM((tm + 8, LRU_W), F32),
                        pltpu.VMEM((tm + HALO, POOL_W), F32), pltpu.VMEM((8, LRU_W), F32),
                        pltpu.VMEM((8, LRU_W), F32), pltpu.VMEM((HALO, POOL_W), F32)]
        + [pltpu.VMEM((2, 256, 256), BF16)] * 3 + [pltpu.VMEM((16, LRU_W), F32)] + [pltpu.VMEM((2, 256, 256), F32)] * 3
        + (_ChipExchange.scratch(n_x) if n_x else []),
        compiler_params=pltpu.CompilerParams(dimension_semantics=("arbitrary",)),
    )(d2, u, u, hs, hs, saved, pv, wa, wx, wp, w_out_b, *chip_sums)
    return outs[0], outs[1], list(outs[2:])


def _mix_in_bwd(du, x, d2, w_in_t, g_mix, tm, chip_sums=()):
    T = x.shape[0]
    n_t = T // tm
    n_x = len(chip_sums)

    def body(du_ref, x_ref, d2_ref, w_ref, g_ref, *rest):
        x_in, rest = rest[:n_x], rest[n_x:]
        dx_ref, dg_ref = rest[:2]
        exchange = _ChipExchange(x_in, rest[2:2 + n_x], *rest[2 + n_x:]) if n_x else None
        i = pl.program_id(0)

        @pl.when(i == 0)
        def _():
            dg_ref[...] = jnp.zeros((1, D_MODEL), F32)
            if exchange:
                exchange.start()

        dh = jnp.dot(du_ref[...], w_ref[...], preferred_element_type=F32)
        xv = x_ref[...]
        rstd = _rstd(xv)
        dx, dgain = _rms_bwd(dh, xv * rstd, rstd, g_ref[...])
        dx_ref[...] = d2_ref[...] + dx
        dg_ref[...] += dgain

        if exchange:
            @pl.when(i == n_t - 1)
            def _():
                exchange.finish()

    row = lambda w: pl.BlockSpec((tm, w), lambda i: (i, 0))
    const = lambda shape: pl.BlockSpec(shape, lambda i: (0,) * len(shape))
    outs = pl.pallas_call(
        body, name="mix_in_bwd", grid=(n_t,),
        in_specs=[row(D_IN), row(D_MODEL), row(D_MODEL), const((D_IN, D_MODEL)), const((1, D_MODEL))] + [ANY] * n_x,
        out_specs=[row(D_MODEL), const((1, D_MODEL))] + [ANY] * n_x,
        out_shape=[_sds((T, D_MODEL), F32), _sds((1, D_MODEL), F32)] + [_sds(a.shape, a.dtype) for a in chip_sums],
        scratch_shapes=_ChipExchange.scratch(n_x) if n_x else [],
        compiler_params=pltpu.CompilerParams(dimension_semantics=("arbitrary",)),
    )(du, x, d2, w_in_t, g_mix, *chip_sums)
    return outs[0], outs[1], list(outs[2:])


def _adamw(w, g, m, v):
    m = ADAM_B1 * m + (1.0 - ADAM_B1) * g
    v = ADAM_B2 * v + (1.0 - ADAM_B2) * (g * g)
    m_hat = m / (1.0 - ADAM_B1 ** ADAM_STEP)
    v_hat = v / (1.0 - ADAM_B2 ** ADAM_STEP)
    delta = -ADAM_LR * (m_hat / (jnp.sqrt(v_hat) + ADAM_EPS) + ADAM_WD * w)
    return delta, m, v


def _adam_shards(ws, ms, vs, parts):
    n = len(ws)
    n_blk = [w.shape[0] // ADAM_ROWS for w in ws]

    def body(*refs):
        w_refs, m_refs, v_refs, p_refs, outs = (refs[:n], refs[n:2 * n], refs[2 * n:3 * n], refs[3 * n:4 * n],
                                                refs[4 * n:])
        i = pl.program_id(0)
        for a in range(n):
            @pl.when(i < n_blk[a])
            def _(a=a):
                g = p_refs[a][0].astype(F32)
                for j in range(1, 4):
                    g = g + p_refs[a][j].astype(F32)
                delta, new_m, new_v = _adamw(w_refs[a][...], g, m_refs[a][...], v_refs[a][...])
                for kind, val in enumerate((g, delta, new_m, new_v)):
                    outs[4 * a + kind][...] = val

    blk = lambda a: pl.BlockSpec((ADAM_ROWS, D_MODEL), lambda i: (jnp.minimum(i, n_blk[a] - 1), 0))
    part_blk = lambda a: pl.BlockSpec((4, ADAM_ROWS, D_MODEL), lambda i: (0, jnp.minimum(i, n_blk[a] - 1), 0))
    res = pl.pallas_call(
        body, name="adam_shards", grid=(max(n_blk),),
        in_specs=[blk(a) for a in range(n)] * 3 + [part_blk(a) for a in range(n)],
        out_specs=[blk(a) for a in range(n) for _ in range(4)],
        out_shape=[_sds(w.shape, F32) for w in ws for _ in range(4)],
        compiler_params=pltpu.CompilerParams(dimension_semantics=("arbitrary",)),
    )(*ws, *ms, *vs, *parts)
    return [tuple(res[4 * a:4 * a + 4]) for a in range(n)]


SMALL_PARAMS = [("norm_mix_g", (1, D_MODEL)), ("conv_w", (1, 4, 64)), ("conv_b", (1, LRU_W)),
                ("gate_a_w", (1, 8, 64, 64)), ("gate_a_b", (1, LRU_W)), ("gate_x_w", (1, 8, 64, 64)),
                ("gate_x_b", (1, LRU_W)), ("lru_lambda", (1, LRU_W)), ("pool_w", (1, 4, 128, 128)),
                ("pool_b", (1, POOL_W)), ("pool_scale", (1, POOL_W)), ("norm_lru_g", (1, LRU_W)),
                ("norm_pool_g", (1, POOL_W)), ("norm_ffn_g", (1, D_MODEL)), ("final_norm_g", (1, D_MODEL))]
VEC_ROW = dict(conv_b=ROW_CB, gate_a_b=ROW_BA, gate_x_b=ROW_BX, lru_lambda=ROW_LAM, pool_b=ROW_PB, pool_scale=ROW_PS,
               norm_lru_g=ROW_GL, norm_pool_g=ROW_GP)
WHOLE = (Ellipsis,)


def _unpack_mixer_grads(sg, dev):
    vec = jnp.concatenate([sg[SG_VEC:SG_VEC + 16], sg[SG_VEC + 16:SG_VEC + 32]], axis=1)
    out = {nm: [(WHOLE, vec[r:r + 1])] for nm, r in VEC_ROW.items()}
    own = jnp.zeros((4, 64), F32)
    for d in range(N_DEV):
        own = jnp.where(dev == d, vec[ROW_CW:ROW_CW + 4, 64 * d:64 * (d + 1)], own)
    out["conv_w"] = [((0,), own)]
    for nm, row0 in (("gate_a_w", SG_WA), ("gate_x_w", SG_WX)):
        out[nm] = [((0, b), sg[row0 + 64 * (b // 4):row0 + 64 * (b // 4 + 1), 64 * (b % 4):64 * (b % 4 + 1)])
                   for b in range(8)]
    out["pool_w"] = [((0, b), sg[SG_WP + 128 * (b // 2):SG_WP + 128 * (b // 2 + 1), 128 * (b % 2):128 * (b % 2 + 1)])
                     for b in range(4)]
    return out


def _adam_small(parts, w, m, v):
    names = [nm for nm, _ in SMALL_PARAMS]
    n = len(names)

    def body(sg_ref, gm_ref, gf_ref, gn_ref, ls_ref, *rest):
        w_refs, m_refs, v_refs, outs = rest[:n], rest[n:2 * n], rest[2 * n:3 * n], rest[3 * n:]
        dev = 4 * lax.axis_index("x") + 2 * lax.axis_index("y") + lax.axis_index("c")

        def total(ref):
            acc = ref[0]
            for d in range(1, N_DEV):
                acc = acc + ref[d]
            return acc

        pieces = _unpack_mixer_grads(total(sg_ref), dev)
        pieces["norm_mix_g"] = [(WHOLE, total(gm_ref))]
        pieces["norm_ffn_g"] = [(WHOLE, total(gf_ref))]
        pieces["final_norm_g"] = [(WHOLE, total(gn_ref))]
        for i, nm in enumerate(names):
            for idx, g in pieces[nm]:
                delta, new_m, new_v = _adamw(w_refs[i][idx], g, m_refs[i][idx], v_refs[i][idx])
                for kind, val in enumerate((g, delta, new_m, new_v)):
                    outs[4 * i + kind][idx] = val
        outs[4 * n][...] = total(ls_ref)

    shapes = [_sds(shape, F32) for _, shape in SMALL_PARAMS for _ in range(4)] + [_sds((8, 128), F32)]
    res = pl.pallas_call(body, name="adam_small", out_shape=shapes)(
        *parts, *[w[nm] for nm in names], *[m[nm] for nm in names], *[v[nm] for nm in names])
    return {nm: tuple(res[4 * i:4 * i + 4]) for i, nm in enumerate(names)}, res[4 * n][0, 0]


def _vec_rows(conv_w_full, conv_b, ba, bx, lam, pb, ps, gl, gp):
    return jnp.concatenate([conv_w_full, conv_b, ba, bx, lam, pb, ps, gl, gp, jnp.zeros((4, LRU_W), F32)], axis=0)


WEIGHT_ORDER = ['norm_mix_g', 'w_in', 'conv_w', 'conv_b', 'gate_a_w', 'gate_a_b', 'gate_x_w', 'gate_x_b', 'lru_lambda',
                'pool_w', 'pool_b', 'pool_scale', 'norm_lru_g', 'norm_pool_g', 'w_out', 'norm_ffn_g', 'ffn_w1', 'ffn_w3',
                'ffn_w2', 'final_norm_g']


def kernel(x, norm_mix_g, w_in, conv_w, conv_b, gate_a_w, gate_a_b, gate_x_w, gate_x_b, lru_lambda, pool_w, pool_b, pool_scale, norm_lru_g, norm_pool_g, w_out, norm_ffn_g, ffn_w1, ffn_w3, ffn_w2, final_norm_g, loss_target, m_norm_mix_g, m_w_in, m_conv_w, m_conv_b, m_gate_a_w, m_gate_a_b, m_gate_x_w, m_gate_x_b, m_lru_lambda, m_pool_w, m_pool_b, m_pool_scale, m_norm_lru_g, m_norm_pool_g, m_w_out, m_norm_ffn_g, m_ffn_w1, m_ffn_w3, m_ffn_w2, m_final_norm_g, v_norm_mix_g, v_w_in, v_conv_w, v_conv_b, v_gate_a_w, v_gate_a_b, v_gate_x_w, v_gate_x_b, v_lru_lambda, v_pool_w, v_pool_b, v_pool_scale, v_norm_lru_g, v_norm_pool_g, v_w_out, v_norm_ffn_g, v_ffn_w1, v_ffn_w3, v_ffn_w2, v_final_norm_g):
    ac = lax.axis_index("c")
    tm, tmx, tn, tk = 512, 512, 1408, 1024
    tm_in = 1024
    xs, tgt = x[0], loss_target[0]
    g_fin = final_norm_g.reshape(1, D_MODEL)
    c_arr = jnp.reshape(ac, (1,)).astype(jnp.int32)

    tr = lambda w: jnp.swapaxes(w[0], 0, 1)
    own = lambda w: w[0]
    bf = lambda a: a.astype(BF16)

    u, h1, g_in, g_conv, (g_out, g_w1) = _mix_in(xs, norm_mix_g, bf(tr(w_in)), conv_w[0], tm,
                                                 shards=[bf(own(w_out)), bf(tr(ffn_w1))])
    w_in_t = g_in.reshape(D_IN, D_MODEL)
    conv_w_full = g_conv.transpose(1, 0, 2).reshape(4, LRU_W)
    pv = _vec_rows(conv_w_full, conv_b, gate_a_b, gate_x_b, lru_lambda, pool_b, pool_scale, norm_lru_g, norm_pool_g)
    wa, wx, wp = gate_a_w[0], gate_x_w[0], pool_w[0]
    w_out_b = g_out.reshape(D_MODEL, D_MODEL)
    y, hs, hres, h2, saved, (g_w3, g_w2) = _mixer_fwd(u, xs, pv, wa, wx, wp, w_out_b, norm_ffn_g, tmx,
                                               shards=[bf(tr(ffn_w3)), bf(own(ffn_w2))])
    w1_t, w3_t, w2_b = g_w1.reshape(D_FF, D_MODEL), g_w3.reshape(D_FF, D_MODEL), g_w2.reshape(D_FF, D_MODEL)
    g, v, d3, loss_acc, d_gfin = _ffn_fwd(hres, h2, w1_t, w3_t, w2_b, g_fin, tgt, tm, tn)

    dg, dv, ff, d2, d_gffn = _ffn_bwd(d3, g, v, w1_t, w3_t, w2_b, hres, norm_ffn_g, tm, tn)
    chips = lambda a: a.reshape(4, a.shape[0] // 4, a.shape[1])
    early_sums = [chips(_at_b_pair(y, d2, c_arr, "grad_w_out", tk)), chips(_at_b_pair(dg, h2, c_arr, "grad_w1", tk)),
                  chips(_at_b_pair(dv, h2, c_arr, "grad_w3", tk)), chips(_at_b_pair(ff, d3, c_arr, "grad_w2", tk))]
    du, d_mixer, early_parts = _mixer_bwd(d2, u, hs, saved, pv, wa, wx, wp, w_out_b, tmx, chip_sums=early_sums)
    grad_x, d_gmix, _ = _mix_in_bwd(du, xs, d2, w_in_t, norm_mix_g, tm_in)
    d_win, small_parts = _at_b_pair(du, h1, c_arr, "grad_w_in", tk,
                                    gather=[d_mixer, d_gmix, d_gffn, d_gfin, loss_acc])
    parts = [_half_exchange(chips(d_win), "grads_to_chips_w_in")] + list(early_parts)

    res = {}
    shard_w = dict(w_in=(w_in, m_w_in, v_w_in, tr), w_out=(w_out, m_w_out, v_w_out, own),
                   ffn_w1=(ffn_w1, m_ffn_w1, v_ffn_w1, tr), ffn_w3=(ffn_w3, m_ffn_w3, v_ffn_w3, tr),
                   ffn_w2=(ffn_w2, m_ffn_w2, v_ffn_w2, own))
    shard_res = _adam_shards([view(w) for w, _, _, view in shard_w.values()],
                             [view(m) for _, m, _, view in shard_w.values()],
                             [view(v) for _, _, v, view in shard_w.values()], parts)
    for (nm, (_, _, _, view)), outs in zip(shard_w.items(), shard_res):
        res[nm] = [(jnp.swapaxes(o, 0, 1) if view is tr else o)[None] for o in outs]

    row = lambda a: a.reshape(1, D_MODEL)
    small = lambda gm, cw, cb, wa_, ba, wx_, bx, lam, pw, pb, ps, gl, gp, gf, gn: dict(
        norm_mix_g=gm, conv_w=cw, conv_b=cb, gate_a_w=wa_, gate_a_b=ba, gate_x_w=wx_, gate_x_b=bx, lru_lambda=lam,
        pool_w=pw, pool_b=pb, pool_scale=ps, norm_lru_g=gl, norm_pool_g=gp, norm_ffn_g=gf, final_norm_g=row(gn))
    small_res, loss = _adam_small(
        small_parts,
        small(norm_mix_g, conv_w, conv_b, gate_a_w, gate_a_b, gate_x_w, gate_x_b, lru_lambda, pool_w, pool_b,
              pool_scale, norm_lru_g, norm_pool_g, norm_ffn_g, final_norm_g),
        small(m_norm_mix_g, m_conv_w, m_conv_b, m_gate_a_w, m_gate_a_b, m_gate_x_w, m_gate_x_b, m_lru_lambda, m_pool_w,
              m_pool_b, m_pool_scale, m_norm_lru_g, m_norm_pool_g, m_norm_ffn_g, m_final_norm_g),
        small(v_norm_mix_g, v_conv_w, v_conv_b, v_gate_a_w, v_gate_a_b, v_gate_x_w, v_gate_x_b, v_lru_lambda, v_pool_w,
              v_pool_b, v_pool_scale, v_norm_lru_g, v_norm_pool_g, v_norm_ffn_g, v_final_norm_g))
    for nm, outs in small_res.items():
        res[nm] = [o.reshape(D_MODEL) for o in outs] if nm == "final_norm_g" else list(outs)

    out = [loss, grad_x[None]]
    for kind in range(4):
        out += [res[nm][kind] for nm in WEIGHT_ORDER]
    return tuple(out)
```

```python
import jax
import jax.numpy as jnp
from jax import lax
from jax.experimental import pallas as pl
from jax.experimental.pallas import tpu as pltpu

F32 = jnp.float32
BF16 = jnp.bfloat16

D_MODEL = 1024
LRU_W = 512
POOL_W = 512
D_IN = 1536
D_FF = 2816
POOL_WINDOWS = (2, 4, 8, 16)
EPS = 1e-6
LRU_C = 8.0
N_DEV = 8
HALO = 16
SCAN_UNROLL = 4
ADAM_ROWS = 32

ADAM_LR = 0.001
ADAM_B1 = 0.9
ADAM_B2 = 0.999
ADAM_EPS = 1e-08
ADAM_WD = 0.01
ADAM_STEP = 10

ROW_CW, ROW_CB, ROW_BA, ROW_BX, ROW_LAM, ROW_PB, ROW_PS, ROW_GL, ROW_GP = 0, 4, 5, 6, 7, 8, 9, 10, 11
SG_VEC, SG_WA, SG_WX, SG_WP, SG_ROWS = 0, 32, 160, 288, 544

NT = (((1,), (1,)), ((), ()))
TN = (((0,), (0,)), ((), ()))


def _sds(shape, dtype):
    return jax.ShapeDtypeStruct(shape, dtype)


def _sigmoid(x):
    return 0.5 * jnp.tanh(0.5 * x) + 0.5


def _gelu_parts(x):
    c = 0.7978845608028654
    inner = c * (x + 0.044715 * (x * x * x))
    th = jnp.tanh(inner)
    g = 0.5 * x * (1.0 + th)
    dg = 0.5 * (1.0 + th) + 0.5 * x * (1.0 - th * th) * (c * (1.0 + 3.0 * 0.044715 * (x * x)))
    return g, dg


def _window_sum(ext, w, back):
    n = ext.shape[0]
    s, k = ext, 1
    while k < w:
        s = s + pltpu.roll(s, k if back else n - k, 0)
        k *= 2
    return s


def _rstd(x):
    return lax.rsqrt(jnp.mean(x * x, axis=-1, keepdims=True) + EPS)


def _rms_bwd(dy, xhat, rstd, gain):
    dxh = dy * gain
    dx = rstd * (dxh - xhat * jnp.mean(dxh * xhat, axis=-1, keepdims=True))
    return dx, jnp.sum(dy * xhat, axis=0, keepdims=True)


def _bd(xb, w_ref):
    return jnp.concatenate(
        [jnp.dot(xb[:, :256], w_ref[0], preferred_element_type=F32),
         jnp.dot(xb[:, 256:], w_ref[1], preferred_element_type=F32)], axis=1)


def _bd_t(xb, w_ref):
    return jnp.concatenate(
        [lax.dot_general(xb[:, :256], w_ref[0], NT, preferred_element_type=F32),
         lax.dot_general(xb[:, 256:], w_ref[1], NT, preferred_element_type=F32)], axis=1)


def _bd_grad(xb, db):
    return jnp.stack(
        [lax.dot_general(xb[:, :256], db[:, :256], TN, preferred_element_type=F32),
         lax.dot_general(xb[:, 256:], db[:, 256:], TN, preferred_element_type=F32)], axis=0)


def _fill_block_diag(dst, src_ref):
    n, k, _ = src_ref.shape
    dst[...] = jnp.zeros(dst.shape, BF16)
    for b in range(n):
        p, q = divmod(b, 256 // k)
        dst[p, q * k:(q + 1) * k, q * k:(q + 1) * k] = src_ref[b].astype(BF16)


def _diag_pack(w, k):
    lane = lax.broadcasted_iota(jnp.int32, (k, 256), 1)
    out = w[0:k]
    for q in range(1, 256 // k):
        out = jnp.where(lane >= q * k, w[q * k:(q + 1) * k], out)
    return out


def _y_pos(b):
    return 4 * (b % 2) + b // 2


def _softplus_neg_lambda(pv):
    z = -pv[ROW_LAM:ROW_LAM + 1, :]
    return jnp.maximum(z, 0.0) + jnp.log(1.0 + jnp.exp(-jnp.abs(z)))


def _lru_gates(e_lru, pv, wa_ref, wx_ref, tm):
    xc = pv[ROW_CB:ROW_CB + 1, :]
    for k in range(4):
        xc = xc + e_lru[pl.ds(HALO - 3 + k, tm), :] * pv[ROW_CW + k:ROW_CW + k + 1, :]
    xcb = xc.astype(BF16)
    r = _sigmoid(_bd(xcb, wa_ref) + pv[ROW_BA:ROW_BA + 1, :])
    ig = _sigmoid(_bd(xcb, wx_ref) + pv[ROW_BX:ROW_BX + 1, :])
    return xc, r, ig, (-LRU_C * r) * _softplus_neg_lambda(pv)


def _lru_decay(la):
    a = jnp.exp(la)
    om = -jnp.tanh(la) * (1.0 + a * a)
    omc = jnp.maximum(om, 1e-12)
    rmult = lax.rsqrt(omc)
    return a, om, omc * rmult, rmult


def _pool_pre(e_pool, pv, wp_ref, tm, t0):
    t = t0 + lax.broadcasted_iota(jnp.int32, (tm, 1), 0)
    parts, inv_cnts = [], []
    for g, w in enumerate(POOL_WINDOWS):
        ext = e_pool[:, pl.ds(128 * g, 128)]
        s = _window_sum(ext, w, back=True)[HALO:, :]
        inv_cnt = 1.0 / jnp.minimum(t + 1, w).astype(F32)
        inv_cnts.append(inv_cnt)
        parts.append(s * inv_cnt - ext[HALO:, :])
    pooled = jnp.concatenate(parts, axis=1)
    pooled_b = pooled.astype(BF16)
    zp = _bd(pooled_b, wp_ref) + pv[ROW_PB:ROW_PB + 1, :]
    return pooled_b, zp, inv_cnts


def _scan_tile(a_ref, b_ref, out_ref, carry, tm, reverse):
    row = lax.broadcasted_iota(jnp.int32, (8, LRU_W), 0)
    nblk = tm // 8

    def local_scan(blk):
        r0 = pl.multiple_of(blk * 8, 8)
        av = a_ref[pl.ds(r0, 8), :]
        bv = b_ref[pl.ds(r0, 8), :]
        for d in (1, 2, 4):
            sh = (8 - d) if reverse else d
            a_s = pltpu.roll(av, sh, 0)
            b_s = pltpu.roll(bv, sh, 0)
            m = (row < 8 - d) if reverse else (row >= d)
            bv = jnp.where(m, av * b_s + bv, bv)
            av = jnp.where(m, av * a_s, av)
        return r0, av, bv

    def step(i, hin):
        local = [local_scan((nblk - 1 - (i * SCAN_UNROLL + j)) if reverse else (i * SCAN_UNROLL + j))
                 for j in range(SCAN_UNROLL)]
        for r0, av, bv in local:
            hv = av * hin + bv
            out_ref[pl.ds(r0, 8), :] = hv
            hin = jnp.broadcast_to(hv[0:1, :] if reverse else hv[7:8, :], (8, LRU_W))
        return hin

    return lax.fori_loop(0, nblk // SCAN_UNROLL, step, carry)


MESH = pl.DeviceIdType.MESH
ANY = pl.BlockSpec(memory_space=pl.ANY)


def _place():
    x, y, c = lax.axis_index("x"), lax.axis_index("y"), lax.axis_index("c")
    chips = [(1 - x, y), (x, 1 - y), (1 - x, 1 - y)]
    return x, y, c, chips


class _Gather:
    def __init__(self, ins, outs, send_sems, recv_sems, local_sems, core_major=False):
        self.ins, self.outs, self.n = ins, outs, len(ins)
        self.send_sems, self.recv_sems, self.local_sems = send_sems, recv_sems, local_sems
        self.core_major = core_major

    @staticmethod
    def scratch(n):
        return [pltpu.SemaphoreType.DMA((7, n)), pltpu.SemaphoreType.DMA((7, n)), pltpu.SemaphoreType.DMA((n,))]

    def _slot(self, a, px, py, pc):
        return self.outs[a].at[4 * pc + 2 * px + py if self.core_major else 4 * px + 2 * py + pc]

    def _copy(self, a, k, block, to, src=None):
        return pltpu.make_async_remote_copy(
            src_ref=self._slot(a, *block) if src is None else src, dst_ref=self._slot(a, *block),
            send_sem=self.send_sems.at[k, a], recv_sem=self.recv_sems.at[k, a], device_id=to, device_id_type=MESH)

    def _mine(self, a):
        x, y, c, _ = _place()
        return pltpu.make_async_copy(self.ins[a], self._slot(a, x, y, c), self.local_sems.at[a])

    def _first(self, a):
        x, y, c, chips = _place()
        me = (x, y, c)
        return ([self._copy(a, 0, me, (x, y, 1 - c), src=self.ins[a])]
                + [self._copy(a, 1 + j, me, (*chip, c), src=self.ins[a]) for j, chip in enumerate(chips)])

    def start(self):
        for a in range(self.n):
            self._mine(a).start()
        for a in range(self.n):
            for cp in self._first(a):
                cp.start()

    def finish(self):
        x, y, c, chips = _place()
        me, sibling = (x, y, c), (x, y, 1 - c)
        passed = []
        for j, chip in enumerate(chips):
            for a in range(self.n):
                self._copy(a, 1 + j, (*chip, c), me).wait_recv()
                fwd = self._copy(a, 4 + j, (*chip, c), sibling)
                fwd.start()
                passed.append(fwd)
        for a in range(self.n):
            self._copy(a, 0, (x, y, 1 - c), me).wait_recv()
            for j, chip in enumerate(chips):
                self._copy(a, 4 + j, (*chip, 1 - c), me).wait_recv()
        for a in range(self.n):
            for cp in self._first(a):
                cp.wait_send()
        for cp in passed:
            cp.wait_send()
        for a in range(self.n):
            self._mine(a).wait()


def _half_exchange(arr, name):
    def body(in_ref, out_ref, send_sems, recv_sems, local_sem):
        x, y, c, _ = _place()
        my_chip = 2 * x + y

        def send(j, wait):
            to_me = (c == x) & (y == j // 2) & (c == j % 2)

            @pl.when(to_me)
            def _():
                local = pltpu.make_async_copy(in_ref.at[j], out_ref.at[my_chip], local_sem)
                local.wait() if wait else local.start()

            @pl.when(jnp.logical_not(to_me))
            def _():
                remote = pltpu.make_async_remote_copy(
                    src_ref=in_ref.at[j], dst_ref=out_ref.at[my_chip], send_sem=send_sems.at[j],
                    recv_sem=recv_sems.at[my_chip], device_id=(c, j // 2, j % 2), device_id_type=MESH)
                remote.wait_send() if wait else remote.start()

        for j in range(4):
            send(j, wait=False)
        for j in range(4):
            send(j, wait=True)
        for k in range(4):
            from_me = (k // 2 == x) & (k % 2 == y) & (c == x)

            @pl.when(jnp.logical_not(from_me))
            def _():
                pltpu.make_async_remote_copy(
                    src_ref=in_ref.at[0], dst_ref=out_ref.at[k], send_sem=send_sems.at[0], recv_sem=recv_sems.at[k],
                    device_id=(k // 2, k % 2, x), device_id_type=MESH).wait_recv()

    return pl.pallas_call(
        body, name=name, out_shape=_sds(arr.shape, arr.dtype), in_specs=[ANY], out_specs=ANY,
        scratch_shapes=[pltpu.SemaphoreType.DMA((4,)), pltpu.SemaphoreType.DMA((4,)), pltpu.SemaphoreType.DMA],
    )(arr)


class _ChipExchange:
    def __init__(self, ins, outs, send_sems, recv_sems, local_sems):
        self.ins, self.outs, self.n = ins, outs, len(ins)
        self.send_sems, self.recv_sems, self.local_sems = send_sems, recv_sems, local_sems

    @staticmethod
    def scratch(n):
        return [pltpu.SemaphoreType.DMA((3, n)), pltpu.SemaphoreType.DMA((3, n)), pltpu.SemaphoreType.DMA((n,))]

    def _local(self, a):
        x, y, _, _ = _place()
        me = 2 * x + y
        return pltpu.make_async_copy(self.ins[a].at[me], self.outs[a].at[me], self.local_sems.at[a])

    def _copies(self, a):
        x, y, c, chips = _place()
        me = 2 * x + y
        return [(pltpu.make_async_remote_copy(
                     src_ref=self.ins[a].at[2 * px + py], dst_ref=self.outs[a].at[me],
                     send_sem=self.send_sems.at[k, a], recv_sem=self.recv_sems.at[k, a],
                     device_id=(px, py, c), device_id_type=MESH),
                 pltpu.make_async_remote_copy(
                     src_ref=self.ins[a].at[me], dst_ref=self.outs[a].at[2 * px + py],
                     send_sem=self.send_sems.at[k, a], recv_sem=self.recv_sems.at[k, a],
                     device_id=(px, py, c), device_id_type=MESH))
                for k, (px, py) in enumerate(chips)]

    def start(self):
        for a in range(self.n):
            self._local(a).start()
        for a in range(self.n):
            for send, _ in self._copies(a):
                send.start()

    def finish(self):
        for a in range(self.n):
            for send, recv in self._copies(a):
                send.wait_send()
                recv.wait_recv()
        for a in range(self.n):
            self._local(a).wait()


def _mix_in(x, g_mix, w_in_own, conv_w_own, tm, shards):
    T = x.shape[0]
    n_t = T // tm
    n_s = len(shards)

    def body(x_ref, g_ref, w_own, cw_own, *rest):
        sh_in, rest = rest[:n_s], rest[n_s:]
        u_ref, h_ref, w_hbm, cw_all = rest[:4]
        sh_out, rest = rest[4:4 + n_s], rest[4 + n_s:]
        w_all, keep_sem = rest[:2]
        first = _Gather([w_own, cw_own], [w_all, cw_all], *rest[2:5])
        later = _Gather(sh_in, sh_out, *rest[5:8], core_major=True)
        keep = pltpu.make_async_copy(w_all, w_hbm, keep_sem)
        i = pl.program_id(0)

        @pl.when(i == 0)
        def _():
            first.start()
            later.start()
            first.finish()
            keep.start()

        xv = x_ref[...]
        h = (xv * _rstd(xv) * g_ref[...]).astype(BF16)
        h_ref[...] = h
        u_ref[...] = lax.dot_general(h, w_all[...].reshape(D_IN, D_MODEL), NT, preferred_element_type=F32)

        @pl.when(i == n_t - 1)
        def _():
            later.finish()
            keep.wait()

    own = (D_IN // N_DEV, D_MODEL)
    outs = pl.pallas_call(
        body, name="mix_in", grid=(n_t,),
        in_specs=[pl.BlockSpec((tm, D_MODEL), lambda i: (i, 0)), pl.BlockSpec((1, D_MODEL), lambda i: (0, 0))]
        + [ANY] * (2 + n_s),
        out_specs=[pl.BlockSpec((tm, D_IN), lambda i: (i, 0)), pl.BlockSpec((tm, D_MODEL), lambda i: (i, 0))]
        + [ANY] * (2 + n_s),
        out_shape=[_sds((T, D_IN), F32), _sds((T, D_MODEL), BF16), _sds((N_DEV,) + own, BF16),
                   _sds((N_DEV,) + conv_w_own.shape, F32)] + [_sds((N_DEV,) + a.shape, a.dtype) for a in shards],
        scratch_shapes=[pltpu.VMEM((N_DEV,) + own, BF16), pltpu.SemaphoreType.DMA] + _Gather.scratch(2)
        + _Gather.scratch(n_s),
        compiler_params=pltpu.CompilerParams(dimension_semantics=("arbitrary",)),
    )(x, g_mix, w_in_own, conv_w_own, *shards)
    return outs[0], outs[1], outs[2], outs[3], list(outs[4:])


def _mixer_fwd(u, x, pv, wa, wx, wp, w_out_b, g_ffn, tm, shards=()):
    T = u.shape[0]
    n_s = len(shards)
    n_t = T // tm

    def body(u_ref, x_ref, pv_ref, wa_in, wx_in, wp_in, wo_ref, gf_ref, *rest):
        sh_in, rest = rest[:n_s], rest[n_s:]
        y_ref, hs_ref, hres_ref, h2_ref, saved_ref = rest[:5]
        sh_out, rest = rest[5:5 + n_s], rest[5 + n_s:]
        e_lru, e_pool, a_s, b_s, hc, wa_ref, wx_ref, wp_ref = rest[:8]
        gather = _Gather(sh_in, sh_out, *rest[8:], core_major=True) if n_s else None
        i = pl.program_id(0)

        @pl.when(i == 0)
        def _():
            if gather:
                gather.start()
            e_lru[pl.ds(0, HALO), :] = jnp.zeros((HALO, LRU_W), F32)
            e_pool[pl.ds(0, HALO), :] = jnp.zeros((HALO, POOL_W), F32)
            hc[...] = jnp.zeros((8, LRU_W), F32)
            _fill_block_diag(wa_ref, wa_in)
            _fill_block_diag(wx_ref, wx_in)
            _fill_block_diag(wp_ref, wp_in)

        e_lru[pl.ds(HALO, tm), :] = u_ref[:, 0:LRU_W]
        e_pool[pl.ds(HALO, tm), :] = u_ref[:, 2 * LRU_W:D_IN]
        pv = pv_ref[...]
        xc, r, ig, la = _lru_gates(e_lru, pv, wa_ref, wx_ref, tm)
        for q, val in enumerate((xc, r, ig, la)):
            saved_ref[:, LRU_W * q:LRU_W * (q + 1)] = val
        a, _, mult, _ = _lru_decay(la)
        a_s[...] = a
        b_s[...] = mult * (ig * xc)
        hc[...] = _scan_tile(a_s, b_s, hs_ref, hc[...], tm, reverse=False)
        gl, _ = _gelu_parts(u_ref[:, LRU_W:2 * LRU_W])
        y_lru = hs_ref[...] * gl
        _, zp, _ = _pool_pre(e_pool, pv, wp_ref, tm, i * tm)
        y_pool = zp * pv[ROW_PS:ROW_PS + 1, :]
        yn = jnp.concatenate([y_lru * _rstd(y_lru) * pv[ROW_GL:ROW_GL + 1, :],
                              y_pool * _rstd(y_pool) * pv[ROW_GP:ROW_GP + 1, :]], axis=1).astype(BF16)
        for b in range(N_DEV):
            y_ref[:, 128 * _y_pos(b):128 * (_y_pos(b) + 1)] = yn[:, 128 * b:128 * (b + 1)]
        hr = x_ref[...] + jnp.dot(y_ref[...], wo_ref[...], preferred_element_type=F32)
        hres_ref[...] = hr
        h2_ref[...] = (hr * _rstd(hr) * gf_ref[...]).astype(BF16)
        e_lru[pl.ds(0, HALO), :] = e_lru[pl.ds(tm, HALO), :]
        e_pool[pl.ds(0, HALO), :] = e_pool[pl.ds(tm, HALO), :]

        if gather:
            @pl.when(i == n_t - 1)
            def _():
                gather.finish()

    full = lambda shape: pl.BlockSpec(shape, lambda i: (0,) * len(shape))
    row = lambda w: pl.BlockSpec((tm, w), lambda i: (i, 0))
    outs = pl.pallas_call(
        body, name="mixer_fwd", grid=(n_t,),
        in_specs=[row(D_IN), row(D_MODEL), full((16, LRU_W)), full((8, 64, 64)), full((8, 64, 64)), full((4, 128, 128)),
                  full((D_MODEL, D_MODEL)), full((1, D_MODEL))] + [ANY] * n_s,
        out_specs=[row(D_MODEL), row(LRU_W), row(D_MODEL), row(D_MODEL), row(4 * LRU_W)] + [ANY] * n_s,
        out_shape=[_sds((T, D_MODEL), BF16), _sds((T, LRU_W), F32), _sds((T, D_MODEL), F32), _sds((T, D_MODEL), BF16),
                   _sds((T, 4 * LRU_W), F32)] + [_sds((N_DEV,) + a.shape, a.dtype) for a in shards],
        scratch_shapes=[pltpu.VMEM((HALO + tm, LRU_W), F32), pltpu.VMEM((HALO + tm, POOL_W), F32),
                        pltpu.VMEM((tm, LRU_W), F32), pltpu.VMEM((tm, LRU_W), F32), pltpu.VMEM((8, LRU_W), F32)]
        + [pltpu.VMEM((2, 256, 256), BF16)] * 3 + (_Gather.scratch(n_s) if n_s else []),
        compiler_params=pltpu.CompilerParams(dimension_semantics=("arbitrary",)),
    )(u, x, pv, wa, wx, wp, w_out_b, g_ffn, *shards)
    return outs[0], outs[1], outs[2], outs[3], outs[4], list(outs[5:])


def _ffn_fwd(hres, h2, w1_b, w3_b, w2_b, g_fin, tgt, tm, tn):
    T = hres.shape[0]
    n_j = D_FF // tn

    def body(hres_ref, h2_ref, w1_ref, w3_ref, w2_ref, gfin_ref, tgt_ref,
             g_ref, v_ref, d3_ref, loss_ref, dgfin_ref, acc):
        i, j = pl.program_id(0), pl.program_id(1)

        @pl.when(j == 0)
        def _():
            acc[...] = jnp.zeros((tm, D_MODEL), F32)

        @pl.when((j == 0) & (i == 0))
        def _():
            loss_ref[...] = jnp.zeros((8, 128), F32)
            dgfin_ref[...] = jnp.zeros((1, D_MODEL), F32)

        h2 = h2_ref[...]
        g = lax.dot_general(h2, w1_ref[...], NT, preferred_element_type=F32)
        v = lax.dot_general(h2, w3_ref[...], NT, preferred_element_type=F32)
        g_ref[...] = g.astype(BF16)
        v_ref[...] = v.astype(BF16)
        ff = ((g * _sigmoid(g)) * v).astype(BF16)
        acc[...] += jnp.dot(ff, w2_ref[...], preferred_element_type=F32)

        @pl.when(j == n_j - 1)
        def _():
            h3 = hres_ref[...] + acc[...]
            rstd = _rstd(h3)
            xh = h3 * rstd
            gfin = gfin_ref[...]
            err = xh * gfin - tgt_ref[...]
            loss_ref[...] += 0.5 * jnp.sum(jnp.mean(err * err, axis=-1, keepdims=True))
            dout = err * (1.0 / D_MODEL)
            dx, dgain = _rms_bwd(dout, xh, rstd, gfin)
            d3_ref[...] = dx
            dgfin_ref[...] += dgain

    row = lambda w: pl.BlockSpec((tm, w), lambda i, j: (i, 0))
    const = lambda shape: pl.BlockSpec(shape, lambda i, j: (0,) * len(shape))
    return pl.pallas_call(
        body, name="ffn_fwd", grid=(T // tm, n_j),
        in_specs=[row(D_MODEL), row(D_MODEL),
                  pl.BlockSpec((tn, D_MODEL), lambda i, j: (j, 0)), pl.BlockSpec((tn, D_MODEL), lambda i, j: (j, 0)),
                  pl.BlockSpec((tn, D_MODEL), lambda i, j: (j, 0)), const((1, D_MODEL)), row(D_MODEL)],
        out_specs=[pl.BlockSpec((tm, tn), lambda i, j: (i, j)), pl.BlockSpec((tm, tn), lambda i, j: (i, j)),
                   row(D_MODEL), const((8, 128)), const((1, D_MODEL))],
        out_shape=[_sds((T, D_FF), BF16), _sds((T, D_FF), BF16),
                   _sds((T, D_MODEL), F32), _sds((8, 128), F32), _sds((1, D_MODEL), F32)],
        scratch_shapes=[pltpu.VMEM((tm, D_MODEL), F32)],
        compiler_params=pltpu.CompilerParams(dimension_semantics=("arbitrary", "arbitrary")),
    )(hres, h2, w1_b, w3_b, w2_b, g_fin, tgt)


def _ffn_bwd(d3, g, v, w1_b, w3_b, w2_b, hres, g_ffn, tm, tn):
    T = d3.shape[0]
    n_j = D_FF // tn

    def body(d3_ref, g_ref, v_ref, w1_ref, w3_ref, w2_ref, hres_ref, gf_ref,
             dg_ref, dv_ref, ff_ref, d2_ref, dgffn_ref, acc):
        i, j = pl.program_id(0), pl.program_id(1)

        @pl.when(j == 0)
        def _():
            acc[...] = jnp.zeros((tm, D_MODEL), F32)

        @pl.when((j == 0) & (i == 0))
        def _():
            dgffn_ref[...] = jnp.zeros((1, D_MODEL), F32)

        dff = lax.dot_general(d3_ref[...].astype(BF16), w2_ref[...], NT, preferred_element_type=F32)
        gv = g_ref[...].astype(F32)
        vv = v_ref[...].astype(F32)
        sg = _sigmoid(gv)
        sl = gv * sg
        dgb = (dff * vv * (sg * (1.0 + gv * (1.0 - sg)))).astype(BF16)
        dvb = (dff * sl).astype(BF16)
        dg_ref[...] = dgb
        dv_ref[...] = dvb
        ff_ref[...] = (sl * vv).astype(BF16)
        acc[...] += (jnp.dot(dgb, w1_ref[...], preferred_element_type=F32)
                     + jnp.dot(dvb, w3_ref[...], preferred_element_type=F32))

        @pl.when(j == n_j - 1)
        def _():
            hr = hres_ref[...]
            rstd = _rstd(hr)
            dx, dgain = _rms_bwd(acc[...], hr * rstd, rstd, gf_ref[...])
            d2_ref[...] = d3_ref[...] + dx
            dgffn_ref[...] += dgain

    row = lambda w: pl.BlockSpec((tm, w), lambda i, j: (i, 0))
    tile = pl.BlockSpec((tm, tn), lambda i, j: (i, j))
    const = lambda shape: pl.BlockSpec(shape, lambda i, j: (0,) * len(shape))
    return pl.pallas_call(
        body, name="ffn_bwd", grid=(T // tm, n_j),
        in_specs=[row(D_MODEL), tile, tile,
                  pl.BlockSpec((tn, D_MODEL), lambda i, j: (j, 0)), pl.BlockSpec((tn, D_MODEL), lambda i, j: (j, 0)),
                  pl.BlockSpec((tn, D_MODEL), lambda i, j: (j, 0)), row(D_MODEL), const((1, D_MODEL))],
        out_specs=[tile, tile, tile, row(D_MODEL), const((1, D_MODEL))],
        out_shape=[_sds((T, D_FF), BF16), _sds((T, D_FF), BF16), _sds((T, D_FF), BF16),
                   _sds((T, D_MODEL), F32), _sds((1, D_MODEL), F32)],
        scratch_shapes=[pltpu.VMEM((tm, D_MODEL), F32)],
        compiler_params=pltpu.CompilerParams(dimension_semantics=("arbitrary", "arbitrary")),
    )(d3, g, v, w1_b, w3_b, w2_b, hres, g_ffn)


def _at_b_pair(a, b, c_arr, name, tk, gather=()):
    T, M = a.shape
    N = b.shape[1]
    hm, n_k = M // 2, T // tk
    n_g = len(gather)

    def body(c_ref, a_ref, b_ref, *rest):
        g_in, o_ref, rest = rest[:n_g], rest[n_g], rest[n_g + 1:]
        g_out, rest = rest[:n_g], rest[n_g:]
        acc, landed, send_sem, recv_sem = rest[:4]
        ag = _Gather(g_in, g_out, *rest[4:]) if n_g else None
        ph, k = pl.program_id(0), pl.program_id(1)

        def hand_over():
            x, y, c, _ = _place()
            return pltpu.make_async_remote_copy(
                src_ref=acc.at[0], dst_ref=landed, send_sem=send_sem, recv_sem=recv_sem,
                device_id=(x, y, 1 - c), device_id_type=MESH)

        if ag:
            @pl.when((ph == 0) & (k == 0))
            def _():
                ag.start()

        @pl.when(k == 0)
        def _():
            acc[ph] = jnp.zeros((hm, N), F32)

        acc[ph] += lax.dot_general(a_ref[...].astype(BF16), b_ref[...].astype(BF16), TN, preferred_element_type=F32)

        @pl.when((ph == 0) & (k == n_k - 1))
        def _():
            hand_over().start()

        @pl.when((ph == 1) & (k == n_k - 1))
        def _():
            copy = hand_over()
            copy.wait_recv()
            o_ref[...] = (acc[1] + landed[...]).astype(BF16)
            copy.wait_send()
            if ag:
                ag.finish()

    outs = pl.pallas_call(
        body, name=name,
        grid_spec=pltpu.PrefetchScalarGridSpec(
            num_scalar_prefetch=1, grid=(2, n_k),
            in_specs=[pl.BlockSpec((tk, hm), lambda ph, k, c_ref: (k, (ph + 1 - c_ref[0]) % 2)),
                      pl.BlockSpec((tk, N), lambda ph, k, c_ref: (k, 0))] + [ANY] * n_g,
            out_specs=[pl.BlockSpec((hm, N), lambda ph, k, c_ref: (0, 0))] + [ANY] * n_g,
            scratch_shapes=[pltpu.VMEM((2, hm, N), F32), pltpu.VMEM((hm, N), F32),
                            pltpu.SemaphoreType.DMA, pltpu.SemaphoreType.DMA] + (_Gather.scratch(n_g) if n_g else [])),
        out_shape=[_sds((hm, N), BF16)] + [_sds((N_DEV,) + g.shape, g.dtype) for g in gather],
        compiler_params=pltpu.CompilerParams(dimension_semantics=("arbitrary", "arbitrary")),
    )(c_arr, a, b, *gather)
    return (outs[0], list(outs[1:])) if n_g else outs[0]


def _mixer_bwd(d2, u, hs, saved, pv, wa, wx, wp, w_out_b, tm, chip_sums=()):
    T = u.shape[0]
    n_t = T // tm
    n_x = len(chip_sums)

    def body(d2_ref, u_ref, uh_ref, hs_ref, hh_ref, saved_ref, pv_ref, wa_in, wx_in, wp_in, wo_ref, *rest):
        x_in, rest = rest[:n_x], rest[n_x:]
        du_ref, sg_ref = rest[:2]
        x_out, rest = rest[2:2 + n_x], rest[2 + n_x:]
        e_pool, e_h, a_s, b_s, dh_s, mu_s, f_x, f_p, mc, cx, cp = rest[:11]
        wa_ref, wx_ref, wp_ref, vacc_ref, dwa_ref, dwx_ref, dwp_ref = rest[11:18]
        exchange = _ChipExchange(x_in, x_out, *rest[18:]) if n_x else None
        s = pl.program_id(0)
        it = n_t - 1 - s

        @pl.when(s == 0)
        def _():
            if exchange:
                exchange.start()
            mc[...] = jnp.zeros((8, LRU_W), F32)
            cx[...] = jnp.zeros((8, LRU_W), F32)
            cp[...] = jnp.zeros((HALO, POOL_W), F32)
            vacc_ref[...] = jnp.zeros((16, LRU_W), F32)
            dwa_ref[...] = jnp.zeros((2, 256, 256), F32)
            dwx_ref[...] = jnp.zeros((2, 256, 256), F32)
            dwp_ref[...] = jnp.zeros((2, 256, 256), F32)
            _fill_block_diag(wa_ref, wa_in)
            _fill_block_diag(wx_ref, wx_in)
            _fill_block_diag(wp_ref, wp_in)

        first = it == 0
        e_pool[pl.ds(0, HALO), :] = jnp.where(first, 0.0, uh_ref[...])
        e_pool[pl.ds(HALO, tm), :] = u_ref[:, 2 * LRU_W:D_IN]
        e_h[pl.ds(0, 8), :] = jnp.where(first, 0.0, hh_ref[...])
        e_h[pl.ds(8, tm), :] = hs_ref[...]
        pv = pv_ref[...]
        saved = lambda q: saved_ref[:, LRU_W * q:LRU_W * (q + 1)]

        dyn = lax.dot_general(d2_ref[...].astype(BF16), wo_ref[...], NT, preferred_element_type=F32)
        dyn = jnp.concatenate([dyn[:, 128 * _y_pos(b):128 * (_y_pos(b) + 1)] for b in range(N_DEV)], axis=1)

        h = hs_ref[...]
        ug = u_ref[:, LRU_W:2 * LRU_W]
        gl, dgl = _gelu_parts(ug)
        y_lru = h * gl
        rstd_l = _rstd(y_lru)
        dy_lru, d_gain_l = _rms_bwd(dyn[:, 0:LRU_W], y_lru * rstd_l, rstd_l, pv[ROW_GL:ROW_GL + 1, :])
        dh = dy_lru * gl
        du_ref[:, LRU_W:2 * LRU_W] = (dy_lru * h * dgl).astype(BF16)
        a_s[...] = jnp.exp(saved(3))
        b_s[...] = a_s[...] * dh
        dh_s[...] = dh
        mu_s[pl.ds(tm, 8), :] = mc[...]
        mc[...] = _scan_tile(a_s, b_s, mu_s, mc[...], tm, reverse=True)
        xc, r, ig = saved(0), saved(1), saved(2)
        a, om, mult, rmult = _lru_decay(saved(3))
        lam_t = dh_s[...] + mu_s[pl.ds(1, tm), :]
        da = lam_t * e_h[pl.ds(7, tm), :]
        dmult = lam_t * (ig * xc)
        di = lam_t * (mult * xc)
        dxc = lam_t * (mult * ig)
        dla = da * a - jnp.where(om > 1e-12, dmult * ((a * a) * rmult), 0.0)
        dra = (dla * (-LRU_C * _softplus_neg_lambda(pv))) * (r * (1.0 - r))
        dia = di * (ig * (1.0 - ig))
        drab = dra.astype(BF16)
        diab = dia.astype(BF16)
        xcb = xc.astype(BF16)
        dxc = dxc + _bd_t(drab, wa_ref) + _bd_t(diab, wx_ref)
        dwa_ref[...] += _bd_grad(xcb, drab)
        dwx_ref[...] += _bd_grad(xcb, diab)
        sig_neg_lam = _sigmoid(-pv[ROW_LAM:ROW_LAM + 1, :])
        d_lam = jnp.sum(dla * r, axis=0, keepdims=True) * (LRU_C * sig_neg_lam)

        f_x[pl.ds(0, tm), :] = dxc
        f_x[pl.ds(tm, 8), :] = cx[...]
        du_lru = jnp.zeros((tm, LRU_W), F32)
        u_lru = u_ref[:, 0:LRU_W]
        d_cw = []
        for k in range(4):
            later = f_x[pl.ds(3 - k, tm), :]
            du_lru = du_lru + later * pv[ROW_CW + k:ROW_CW + k + 1, :]
            d_cw.append(jnp.sum(later * u_lru, axis=0, keepdims=True))
        du_ref[:, 0:LRU_W] = du_lru.astype(BF16)
        cx[...] = f_x[pl.ds(0, 8), :]

        pooled_b, zp, inv_cnts = _pool_pre(e_pool, pv, wp_ref, tm, it * tm)
        ps = pv[ROW_PS:ROW_PS + 1, :]
        y_pool = zp * ps
        rstd_p = _rstd(y_pool)
        dy_pool, d_gain_p = _rms_bwd(dyn[:, LRU_W:D_MODEL], y_pool * rstd_p, rstd_p, pv[ROW_GP:ROW_GP + 1, :])
        dz = dy_pool * ps
        dzb = dz.astype(BF16)
        dwp_ref[...] += _bd_grad(pooled_b, dzb)
        dpooled = _bd_t(dzb, wp_ref)
        for g, w in enumerate(POOL_WINDOWS):
            f_p[pl.ds(0, tm), pl.ds(128 * g, 128)] = dpooled[:, 128 * g:128 * (g + 1)] * inv_cnts[g]
        f_p[pl.ds(tm, HALO), :] = cp[...]
        for g, w in enumerate(POOL_WINDOWS):
            acc = _window_sum(f_p[:, pl.ds(128 * g, 128)], w, back=False)[0:tm, :]
            du_ref[:, 2 * LRU_W + 128 * g:2 * LRU_W + 128 * (g + 1)] = (
                acc - dpooled[:, 128 * g:128 * (g + 1)]).astype(BF16)
        cp[...] = f_p[pl.ds(0, HALO), :]

        rows = d_cw + [
            jnp.sum(dxc, axis=0, keepdims=True),
            jnp.sum(dra, axis=0, keepdims=True),
            jnp.sum(dia, axis=0, keepdims=True),
            d_lam,
            jnp.sum(dz, axis=0, keepdims=True),
            jnp.sum(dy_pool * zp, axis=0, keepdims=True),
            d_gain_l, d_gain_p,
            jnp.zeros((4, LRU_W), F32),
        ]
        vacc_ref[...] += jnp.concatenate(rows, axis=0)

        @pl.when(s == n_t - 1)
        def _():
            sg_ref[SG_VEC:SG_VEC + 16, :] = vacc_ref[:, 0:256]
            sg_ref[SG_VEC + 16:SG_VEC + 32, :] = vacc_ref[:, 256:512]
            for half in range(2):
                sg_ref[SG_WA + 64 * half:SG_WA + 64 * (half + 1), :] = _diag_pack(dwa_ref[half], 64)
                sg_ref[SG_WX + 64 * half:SG_WX + 64 * (half + 1), :] = _diag_pack(dwx_ref[half], 64)
                sg_ref[SG_WP + 128 * half:SG_WP + 128 * (half + 1), :] = _diag_pack(dwp_ref[half], 128)
            if exchange:
                exchange.finish()

    rev = lambda w: pl.BlockSpec((tm, w), lambda s: (n_t - 1 - s, 0))
    full = lambda shape: pl.BlockSpec(shape, lambda s: (0,) * len(shape))
    outs = pl.pallas_call(
        body, name="mixer_bwd", grid=(n_t,),
        in_specs=[rev(D_MODEL), rev(D_IN),
                  pl.BlockSpec((HALO, POOL_W), lambda s: (jnp.maximum((n_t - 1 - s) * (tm // HALO) - 1, 0), 2)),
                  rev(LRU_W),
                  pl.BlockSpec((8, LRU_W), lambda s: (jnp.maximum((n_t - 1 - s) * (tm // 8) - 1, 0), 0)),
                  rev(4 * LRU_W), full((16, LRU_W)), full((8, 64, 64)), full((8, 64, 64)), full((4, 128, 128)),
                  full((D_MODEL, D_MODEL))] + [ANY] * n_x,
        out_specs=[rev(D_IN), full((SG_ROWS, 256))] + [ANY] * n_x,
        out_shape=[_sds((T, D_IN), BF16), _sds((SG_ROWS, 256), F32)] + [_sds(a.shape, a.dtype) for a in chip_sums],
        scratch_shapes=[pltpu.VMEM((HALO + tm, POOL_W), F32),
                        pltpu.VMEM((8 + tm, LRU_W), F32)] + [pltpu.VMEM((tm, LRU_W), F32)] * 3 + [
                        pltpu.VMEM((tm + 8, LRU_W), F32), pltpu.VMEM((tm + 8, LRU_W), F32),
                        pltpu.VMEM((tm + HALO, POOL_W), F32), pltpu.VMEM((8, LRU_W), F32),
                        pltpu.VMEM((8, LRU_W), F32), pltpu.VMEM((HALO, POOL_W), F32)]
        + [pltpu.VMEM((2, 256, 256), BF16)] * 3 + [pltpu.VMEM((16, LRU_W), F32)] + [pltpu.VMEM((2, 256, 256), F32)] * 3
        + (_ChipExchange.scratch(n_x) if n_x else []),
        compiler_params=pltpu.CompilerParams(dimension_semantics=("arbitrary",)),
    )(d2, u, u, hs, hs, saved, pv, wa, wx, wp, w_out_b, *chip_sums)
    return outs[0], outs[1], list(outs[2:])


def _mix_in_bwd(du, x, d2, w_in_t, g_mix, tm, chip_sums=()):
    T = x.shape[0]
    n_t = T // tm
    n_x = len(chip_sums)

    def body(du_ref, x_ref, d2_ref, w_ref, g_ref, *rest):
        x_in, rest = rest[:n_x], rest[n_x:]
        dx_ref, dg_ref = rest[:2]
        exchange = _ChipExchange(x_in, rest[2:2 + n_x], *rest[2 + n_x:]) if n_x else None
        i = pl.program_id(0)

        @pl.when(i == 0)
        def _():
            dg_ref[...] = jnp.zeros((1, D_MODEL), F32)
            if exchange:
                exchange.start()

        dh = jnp.dot(du_ref[...], w_ref[...], preferred_element_type=F32)
        xv = x_ref[...]
        rstd = _rstd(xv)
        dx, dgain = _rms_bwd(dh, xv * rstd, rstd, g_ref[...])
        dx_ref[...] = d2_ref[...] + dx
        dg_ref[...] += dgain

        if exchange:
            @pl.when(i == n_t - 1)
            def _():
                exchange.finish()

    row = lambda w: pl.BlockSpec((tm, w), lambda i: (i, 0))
    const = lambda shape: pl.BlockSpec(shape, lambda i: (0,) * len(shape))
    outs = pl.pallas_call(
        body, name="mix_in_bwd", grid=(n_t,),
        in_specs=[row(D_IN), row(D_MODEL), row(D_MODEL), const((D_IN, D_MODEL)), const((1, D_MODEL))] + [ANY] * n_x,
        out_specs=[row(D_MODEL), const((1, D_MODEL))] + [ANY] * n_x,
        out_shape=[_sds((T, D_MODEL), F32), _sds((1, D_MODEL), F32)] + [_sds(a.shape, a.dtype) for a in chip_sums],
        scratch_shapes=_ChipExchange.scratch(n_x) if n_x else [],
        compiler_params=pltpu.CompilerParams(dimension_semantics=("arbitrary",)),
    )(du, x, d2, w_in_t, g_mix, *chip_sums)
    return outs[0], outs[1], list(outs[2:])


def _adamw(w, g, m, v):
    m = ADAM_B1 * m + (1.0 - ADAM_B1) * g
    v = ADAM_B2 * v + (1.0 - ADAM_B2) * (g * g)
    m_hat = m / (1.0 - ADAM_B1 ** ADAM_STEP)
    v_hat = v / (1.0 - ADAM_B2 ** ADAM_STEP)
    delta = -ADAM_LR * (m_hat / (jnp.sqrt(v_hat) + ADAM_EPS) + ADAM_WD * w)
    return delta, m, v


def _adam_shards(ws, ms, vs, parts):
    n = len(ws)
    n_blk = [w.shape[0] // ADAM_ROWS for w in ws]

    def body(*refs):
        w_refs, m_refs, v_refs, p_refs, outs = (refs[:n], refs[n:2 * n], refs[2 * n:3 * n], refs[3 * n:4 * n],
                                                refs[4 * n:])
        i = pl.program_id(0)
        for a in range(n):
            @pl.when(i < n_blk[a])
            def _(a=a):
                g = p_refs[a][0].astype(F32)
                for j in range(1, 4):
                    g = g + p_refs[a][j].astype(F32)
                delta, new_m, new_v = _adamw(w_refs[a][...], g, m_refs[a][...], v_refs[a][...])
                for kind, val in enumerate((g, delta, new_m, new_v)):
                    outs[4 * a + kind][...] = val

    blk = lambda a: pl.BlockSpec((ADAM_ROWS, D_MODEL), lambda i: (jnp.minimum(i, n_blk[a] - 1), 0))
    part_blk = lambda a: pl.BlockSpec((4, ADAM_ROWS, D_MODEL), lambda i: (0, jnp.minimum(i, n_blk[a] - 1), 0))
    res = pl.pallas_call(
        body, name="adam_shards", grid=(max(n_blk),),
        in_specs=[blk(a) for a in range(n)] * 3 + [part_blk(a) for a in range(n)],
        out_specs=[blk(a) for a in range(n) for _ in range(4)],
        out_shape=[_sds(w.shape, F32) for w in ws for _ in range(4)],
        compiler_params=pltpu.CompilerParams(dimension_semantics=("arbitrary",)),
    )(*ws, *ms, *vs, *parts)
    return [tuple(res[4 * a:4 * a + 4]) for a in range(n)]


SMALL_PARAMS = [("norm_mix_g", (1, D_MODEL)), ("conv_w", (1, 4, 64)), ("conv_b", (1, LRU_W)),
                ("gate_a_w", (1, 8, 64, 64)), ("gate_a_b", (1, LRU_W)), ("gate_x_w", (1, 8, 64, 64)),
                ("gate_x_b", (1, LRU_W)), ("lru_lambda", (1, LRU_W)), ("pool_w", (1, 4, 128, 128)),
                ("pool_b", (1, POOL_W)), ("pool_scale", (1, POOL_W)), ("norm_lru_g", (1, LRU_W)),
                ("norm_pool_g", (1, POOL_W)), ("norm_ffn_g", (1, D_MODEL)), ("final_norm_g", (1, D_MODEL))]
VEC_ROW = dict(conv_b=ROW_CB, gate_a_b=ROW_BA, gate_x_b=ROW_BX, lru_lambda=ROW_LAM, pool_b=ROW_PB, pool_scale=ROW_PS,
               norm_lru_g=ROW_GL, norm_pool_g=ROW_GP)
WHOLE = (Ellipsis,)


def _unpack_mixer_grads(sg, dev):
    vec = jnp.concatenate([sg[SG_VEC:SG_VEC + 16], sg[SG_VEC + 16:SG_VEC + 32]], axis=1)
    out = {nm: [(WHOLE, vec[r:r + 1])] for nm, r in VEC_ROW.items()}
    own = jnp.zeros((4, 64), F32)
    for d in range(N_DEV):
        own = jnp.where(dev == d, vec[ROW_CW:ROW_CW + 4, 64 * d:64 * (d + 1)], own)
    out["conv_w"] = [((0,), own)]
    for nm, row0 in (("gate_a_w", SG_WA), ("gate_x_w", SG_WX)):
        out[nm] = [((0, b), sg[row0 + 64 * (b // 4):row0 + 64 * (b // 4 + 1), 64 * (b % 4):64 * (b % 4 + 1)])
                   for b in range(8)]
    out["pool_w"] = [((0, b), sg[SG_WP + 128 * (b // 2):SG_WP + 128 * (b // 2 + 1), 128 * (b % 2):128 * (b % 2 + 1)])
                     for b in range(4)]
    return out


def _adam_small(parts, w, m, v):
    names = [nm for nm, _ in SMALL_PARAMS]
    n = len(names)

    def body(sg_ref, gm_ref, gf_ref, gn_ref, ls_ref, *rest):
        w_refs, m_refs, v_refs, outs = rest[:n], rest[n:2 * n], rest[2 * n:3 * n], rest[3 * n:]
        dev = 4 * lax.axis_index("x") + 2 * lax.axis_index("y") + lax.axis_index("c")

        def total(ref):
            acc = ref[0]
            for d in range(1, N_DEV):
                acc = acc + ref[d]
            return acc

        pieces = _unpack_mixer_grads(total(sg_ref), dev)
        pieces["norm_mix_g"] = [(WHOLE, total(gm_ref))]
        pieces["norm_ffn_g"] = [(WHOLE, total(gf_ref))]
        pieces["final_norm_g"] = [(WHOLE, total(gn_ref))]
        for i, nm in enumerate(names):
            for idx, g in pieces[nm]:
                delta, new_m, new_v = _adamw(w_refs[i][idx], g, m_refs[i][idx], v_refs[i][idx])
                for kind, val in enumerate((g, delta, new_m, new_v)):
                    outs[4 * i + kind][idx] = val
        outs[4 * n][...] = total(ls_ref)

    shapes = [_sds(shape, F32) for _, shape in SMALL_PARAMS for _ in range(4)] + [_sds((8, 128), F32)]
    res = pl.pallas_call(body, name="adam_small", out_shape=shapes)(
        *parts, *[w[nm] for nm in names], *[m[nm] for nm in names], *[v[nm] for nm in names])
    return {nm: tuple(res[4 * i:4 * i + 4]) for i, nm in enumerate(names)}, res[4 * n][0, 0]


def _vec_rows(conv_w_full, conv_b, ba, bx, lam, pb, ps, gl, gp):
    return jnp.concatenate([conv_w_full, conv_b, ba, bx, lam, pb, ps, gl, gp, jnp.zeros((4, LRU_W), F32)], axis=0)


WEIGHT_ORDER = ['norm_mix_g', 'w_in', 'conv_w', 'conv_b', 'gate_a_w', 'gate_a_b', 'gate_x_w', 'gate_x_b', 'lru_lambda',
                'pool_w', 'pool_b', 'pool_scale', 'norm_lru_g', 'norm_pool_g', 'w_out', 'norm_ffn_g', 'ffn_w1', 'ffn_w3',
                'ffn_w2', 'final_norm_g']


def kernel(x, norm_mix_g, w_in, conv_w, conv_b, gate_a_w, gate_a_b, gate_x_w, gate_x_b, lru_lambda, pool_w, pool_b, pool_scale, norm_lru_g, norm_pool_g, w_out, norm_ffn_g, ffn_w1, ffn_w3, ffn_w2, final_norm_g, loss_target, m_norm_mix_g, m_w_in, m_conv_w, m_conv_b, m_gate_a_w, m_gate_a_b, m_gate_x_w, m_gate_x_b, m_lru_lambda, m_pool_w, m_pool_b, m_pool_scale, m_norm_lru_g, m_norm_pool_g, m_w_out, m_norm_ffn_g, m_ffn_w1, m_ffn_w3, m_ffn_w2, m_final_norm_g, v_norm_mix_g, v_w_in, v_conv_w, v_conv_b, v_gate_a_w, v_gate_a_b, v_gate_x_w, v_gate_x_b, v_lru_lambda, v_pool_w, v_pool_b, v_pool_scale, v_norm_lru_g, v_norm_pool_g, v_w_out, v_norm_ffn_g, v_ffn_w1, v_ffn_w3, v_ffn_w2, v_final_norm_g):
    ac = lax.axis_index("c")
    tm, tmx, tn, tk = 512, 512, 1408, 1024
    tm_in = 1024
    xs, tgt = x[0], loss_target[0]
    g_fin = final_norm_g.reshape(1, D_MODEL)
    c_arr = jnp.reshape(ac, (1,)).astype(jnp.int32)

    tr = lambda w: jnp.swapaxes(w[0], 0, 1)
    own = lambda w: w[0]
    bf = lambda a: a.astype(BF16)

    u, h1, g_in, g_conv, (g_out,) = _mix_in(xs, norm_mix_g, bf(tr(w_in)), conv_w[0], tm, shards=[bf(own(w_out))])
    w_in_t = g_in.reshape(D_IN, D_MODEL)
    conv_w_full = g_conv.transpose(1, 0, 2).reshape(4, LRU_W)
    pv = _vec_rows(conv_w_full, conv_b, gate_a_b, gate_x_b, lru_lambda, pool_b, pool_scale, norm_lru_g, norm_pool_g)
    wa, wx, wp = gate_a_w[0], gate_x_w[0], pool_w[0]
    w_out_b = g_out.reshape(D_MODEL, D_MODEL)
    y, hs, hres, h2, saved, (g_w1, g_w3, g_w2) = _mixer_fwd(
        u, xs, pv, wa, wx, wp, w_out_b, norm_ffn_g, tmx, shards=[bf(tr(ffn_w1)), bf(tr(ffn_w3)), bf(own(ffn_w2))])
    w1_t, w3_t, w2_b = g_w1.reshape(D_FF, D_MODEL), g_w3.reshape(D_FF, D_MODEL), g_w2.reshape(D_FF, D_MODEL)
    g, v, d3, loss_acc, d_gfin = _ffn_fwd(hres, h2, w1_t, w3_t, w2_b, g_fin, tgt, tm, tn)

    dg, dv, ff, d2, d_gffn = _ffn_bwd(d3, g, v, w1_t, w3_t, w2_b, hres, norm_ffn_g, tm, tn)
    chips = lambda a: a.reshape(4, a.shape[0] // 4, a.shape[1])
    early_sums = [chips(_at_b_pair(y, d2, c_arr, "grad_w_out", tk)), chips(_at_b_pair(dg, h2, c_arr, "grad_w1", tk)),
                  chips(_at_b_pair(dv, h2, c_arr, "grad_w3", tk)), chips(_at_b_pair(ff, d3, c_arr, "grad_w2", tk))]
    du, d_mixer, early_parts = _mixer_bwd(d2, u, hs, saved, pv, wa, wx, wp, w_out_b, tmx, chip_sums=early_sums)
    grad_x, d_gmix, _ = _mix_in_bwd(du, xs, d2, w_in_t, norm_mix_g, tm_in)
    d_win, small_parts = _at_b_pair(du, h1, c_arr, "grad_w_in", tk,
                                    gather=[d_mixer, d_gmix, d_gffn, d_gfin, loss_acc])
    parts = [_half_exchange(chips(d_win), "grads_to_chips_w_in")] + list(early_parts)

    res = {}
    shard_w = dict(w_in=(w_in, m_w_in, v_w_in, tr), w_out=(w_out, m_w_out, v_w_out, own),
                   ffn_w1=(ffn_w1, m_ffn_w1, v_ffn_w1, tr), ffn_w3=(ffn_w3, m_ffn_w3, v_ffn_w3, tr),
                   ffn_w2=(ffn_w2, m_ffn_w2, v_ffn_w2, own))
    shard_res = _adam_shards([view(w) for w, _, _, view in shard_w.values()],
                             [view(m) for _, m, _, view in shard_w.values()],
                             [view(v) for _, _, v, view in shard_w.values()], parts)
    for (nm, (_, _, _, view)), outs in zip(shard_w.items(), shard_res):
        res[nm] = [(jnp.swapaxes(o, 0, 1) if view is tr else o)[None] for o in outs]

    row = lambda a: a.reshape(1, D_MODEL)
    small = lambda gm, cw, cb, wa_, ba, wx_, bx, lam, pw, pb, ps, gl, gp, gf, gn: dict(
        norm_mix_g=gm, conv_w=cw, conv_b=cb, gate_a_w=wa_, gate_a_b=ba, gate_x_w=wx_, gate_x_b=bx, lru_lambda=lam,
        pool_w=pw, pool_b=pb, pool_scale=ps, norm_lru_g=gl, norm_pool_g=gp, norm_ffn_g=gf, final_norm_g=row(gn))
    small_res, loss = _adam_small(
        small_parts,
        small(norm_mix_g, conv_w, conv_b, gate_a_w, gate_a_b, gate_x_w, gate_x_b, lru_lambda, pool_w, pool_b,
              pool_scale, norm_lru_g, norm_pool_g, norm_ffn_g, final_norm_g),
        small(m_norm_mix_g, m_conv_w, m_conv_b, m_gate_a_w, m_gate_a_b, m_gate_x_w, m_gate_x_b, m_lru_lambda, m_pool_w,
              m_pool_b, m_pool_scale, m_norm_lru_g, m_norm_pool_g, m_norm_ffn_g, m_final_norm_g),
        small(v_norm_mix_g, v_conv_w, v_conv_b, v_gate_a_w, v_gate_a_b, v_gate_x_w, v_gate_x_b, v_lru_lambda, v_pool_w,
              v_pool_b, v_pool_scale, v_norm_lru_g, v_norm_pool_g, v_norm_ffn_g, v_final_norm_g))
    for nm, outs in small_res.items():
        res[nm] = [o.reshape(D_MODEL) for o in outs] if nm == "final_norm_g" else list(outs)

    out = [loss, grad_x[None]]
    for kind in range(4):
        out += [res[nm][kind] for nm in WEIGHT_ORDER]
    return tuple(out)
```

```python
import jax
import jax.numpy as jnp
from jax import lax
from jax.experimental import pallas as pl
from jax.experimental.pallas import tpu as pltpu

F32 = jnp.float32
BF16 = jnp.bfloat16

D_MODEL = 1024
LRU_W = 512
POOL_W = 512
D_IN = 1536
D_FF = 2816
POOL_WINDOWS = (2, 4, 8, 16)
EPS = 1e-6
LRU_C = 8.0
N_DEV = 8
HALO = 16
SCAN_UNROLL = 4
ADAM_ROWS = 32

ADAM_LR = 0.001
ADAM_B1 = 0.9
ADAM_B2 = 0.999
ADAM_EPS = 1e-08
ADAM_WD = 0.01
ADAM_STEP = 10

ROW_CW, ROW_CB, ROW_BA, ROW_BX, ROW_LAM, ROW_PB, ROW_PS, ROW_GL, ROW_GP = 0, 4, 5, 6, 7, 8, 9, 10, 11
SG_VEC, SG_WA, SG_WX, SG_WP, SG_ROWS = 0, 32, 160, 288, 544

NT = (((1,), (1,)), ((), ()))
TN = (((0,), (0,)), ((), ()))


def _sds(shape, dtype):
    return jax.ShapeDtypeStruct(shape, dtype)


def _sigmoid(x):
    return 0.5 * jnp.tanh(0.5 * x) + 0.5


def _gelu_parts(x):
    c = 0.7978845608028654
    inner = c * (x + 0.044715 * (x * x * x))
    th = jnp.tanh(inner)
    g = 0.5 * x * (1.0 + th)
    dg = 0.5 * (1.0 + th) + 0.5 * x * (1.0 - th * th) * (c * (1.0 + 3.0 * 0.044715 * (x * x)))
    return g, dg


def _window_sum(ext, w, back):
    n = ext.shape[0]
    s, k = ext, 1
    while k < w:
        s = s + pltpu.roll(s, k if back else n - k, 0)
        k *= 2
    return s


def _rstd(x):
    return lax.rsqrt(jnp.mean(x * x, axis=-1, keepdims=True) + EPS)


def _rms_bwd(dy, xhat, rstd, gain):
    dxh = dy * gain
    dx = rstd * (dxh - xhat * jnp.mean(dxh * xhat, axis=-1, keepdims=True))
    return dx, jnp.sum(dy * xhat, axis=0, keepdims=True)


def _bd(xb, w_ref):
    return jnp.concatenate(
        [jnp.dot(xb[:, :256], w_ref[0], preferred_element_type=F32),
         jnp.dot(xb[:, 256:], w_ref[1], preferred_element_type=F32)], axis=1)


def _bd_t(xb, w_ref):
    return jnp.concatenate(
        [lax.dot_general(xb[:, :256], w_ref[0], NT, preferred_element_type=F32),
         lax.dot_general(xb[:, 256:], w_ref[1], NT, preferred_element_type=F32)], axis=1)


def _bd_grad(xb, db):
    return jnp.stack(
        [lax.dot_general(xb[:, :256], db[:, :256], TN, preferred_element_type=F32),
         lax.dot_general(xb[:, 256:], db[:, 256:], TN, preferred_element_type=F32)], axis=0)


def _fill_block_diag(dst, src_ref):
    n, k, _ = src_ref.shape
    dst[...] = jnp.zeros(dst.shape, BF16)
    for b in range(n):
        p, q = divmod(b, 256 // k)
        dst[p, q * k:(q + 1) * k, q * k:(q + 1) * k] = src_ref[b].astype(BF16)


def _diag_pack(w, k):
    lane = lax.broadcasted_iota(jnp.int32, (k, 256), 1)
    out = w[0:k]
    for q in range(1, 256 // k):
        out = jnp.where(lane >= q * k, w[q * k:(q + 1) * k], out)
    return out


def _y_pos(b):
    return 4 * (b % 2) + b // 2


def _softplus_neg_lambda(pv):
    z = -pv[ROW_LAM:ROW_LAM + 1, :]
    return jnp.maximum(z, 0.0) + jnp.log(1.0 + jnp.exp(-jnp.abs(z)))


def _lru_gates(e_lru, pv, wa_ref, wx_ref, tm):
    xc = pv[ROW_CB:ROW_CB + 1, :]
    for k in range(4):
        xc = xc + e_lru[pl.ds(HALO - 3 + k, tm), :] * pv[ROW_CW + k:ROW_CW + k + 1, :]
    xcb = xc.astype(BF16)
    r = _sigmoid(_bd(xcb, wa_ref) + pv[ROW_BA:ROW_BA + 1, :])
    ig = _sigmoid(_bd(xcb, wx_ref) + pv[ROW_BX:ROW_BX + 1, :])
    return xc, r, ig, (-LRU_C * r) * _softplus_neg_lambda(pv)


def _lru_decay(la):
    a = jnp.exp(la)
    om = -jnp.tanh(la) * (1.0 + a * a)
    omc = jnp.maximum(om, 1e-12)
    rmult = lax.rsqrt(omc)
    return a, om, omc * rmult, rmult


def _pool_pre(e_pool, pv, wp_ref, tm, t0):
    t = t0 + lax.broadcasted_iota(jnp.int32, (tm, 1), 0)
    parts, inv_cnts = [], []
    for g, w in enumerate(POOL_WINDOWS):
        ext = e_pool[:, pl.ds(128 * g, 128)]
        s = _window_sum(ext, w, back=True)[HALO:, :]
        inv_cnt = 1.0 / jnp.minimum(t + 1, w).astype(F32)
        inv_cnts.append(inv_cnt)
        parts.append(s * inv_cnt - ext[HALO:, :])
    pooled = jnp.concatenate(parts, axis=1)
    pooled_b = pooled.astype(BF16)
    zp = _bd(pooled_b, wp_ref) + pv[ROW_PB:ROW_PB + 1, :]
    return pooled_b, zp, inv_cnts


def _scan_tile(a_ref, b_ref, out_ref, carry, tm, reverse):
    row = lax.broadcasted_iota(jnp.int32, (8, LRU_W), 0)
    nblk = tm // 8

    def local_scan(blk):
        r0 = pl.multiple_of(blk * 8, 8)
        av = a_ref[pl.ds(r0, 8), :]
        bv = b_ref[pl.ds(r0, 8), :]
        for d in (1, 2, 4):
            sh = (8 - d) if reverse else d
            a_s = pltpu.roll(av, sh, 0)
            b_s = pltpu.roll(bv, sh, 0)
            m = (row < 8 - d) if reverse else (row >= d)
            bv = jnp.where(m, av * b_s + bv, bv)
            av = jnp.where(m, av * a_s, av)
        return r0, av, bv

    def step(i, hin):
        local = [local_scan((nblk - 1 - (i * SCAN_UNROLL + j)) if reverse else (i * SCAN_UNROLL + j))
                 for j in range(SCAN_UNROLL)]
        for r0, av, bv in local:
            hv = av * hin + bv
            out_ref[pl.ds(r0, 8), :] = hv
            hin = jnp.broadcast_to(hv[0:1, :] if reverse else hv[7:8, :], (8, LRU_W))
        return hin

    return lax.fori_loop(0, nblk // SCAN_UNROLL, step, carry)


MESH = pl.DeviceIdType.MESH
ANY = pl.BlockSpec(memory_space=pl.ANY)


def _place():
    x, y, c = lax.axis_index("x"), lax.axis_index("y"), lax.axis_index("c")
    chips = [(1 - x, y), (x, 1 - y), (1 - x, 1 - y)]
    return x, y, c, chips


class _Gather:
    def __init__(self, ins, outs, send_sems, recv_sems, local_sems, core_major=False):
        self.ins, self.outs, self.n = ins, outs, len(ins)
        self.send_sems, self.recv_sems, self.local_sems = send_sems, recv_sems, local_sems
        self.core_major = core_major

    @staticmethod
    def scratch(n):
        return [pltpu.SemaphoreType.DMA((7, n)), pltpu.SemaphoreType.DMA((7, n)), pltpu.SemaphoreType.DMA((n,))]

    def _slot(self, a, px, py, pc):
        return self.outs[a].at[4 * pc + 2 * px + py if self.core_major else 4 * px + 2 * py + pc]

    def _copy(self, a, k, block, to, src=None):
        return pltpu.make_async_remote_copy(
            src_ref=self._slot(a, *block) if src is None else src, dst_ref=self._slot(a, *block),
            send_sem=self.send_sems.at[k, a], recv_sem=self.recv_sems.at[k, a], device_id=to, device_id_type=MESH)

    def _mine(self, a):
        x, y, c, _ = _place()
        return pltpu.make_async_copy(self.ins[a], self._slot(a, x, y, c), self.local_sems.at[a])

    def _first(self, a):
        x, y, c, chips = _place()
        me = (x, y, c)
        return ([self._copy(a, 0, me, (x, y, 1 - c), src=self.ins[a])]
                + [self._copy(a, 1 + j, me, (*chip, c), src=self.ins[a]) for j, chip in enumerate(chips)])

    def start(self):
        for a in range(self.n):
            self._mine(a).start()
        for a in range(self.n):
            for cp in self._first(a):
                cp.start()

    def finish(self):
        x, y, c, chips = _place()
        me, sibling = (x, y, c), (x, y, 1 - c)
        passed = []
        for j, chip in enumerate(chips):
            for a in range(self.n):
                self._copy(a, 1 + j, (*chip, c), me).wait_recv()
                fwd = self._copy(a, 4 + j, (*chip, c), sibling)
                fwd.start()
                passed.append(fwd)
        for a in range(self.n):
            self._copy(a, 0, (x, y, 1 - c), me).wait_recv()
            for j, chip in enumerate(chips):
                self._copy(a, 4 + j, (*chip, 1 - c), me).wait_recv()
        for a in range(self.n):
            for cp in self._first(a):
                cp.wait_send()
        for cp in passed:
            cp.wait_send()
        for a in range(self.n):
            self._mine(a).wait()


def _half_exchange(arr, name):
    def body(in_ref, out_ref, send_sems, recv_sems, local_sem):
        x, y, c, _ = _place()
        my_chip = 2 * x + y

        def send(j, wait):
            to_me = (c == x) & (y == j // 2) & (c == j % 2)

            @pl.when(to_me)
            def _():
                local = pltpu.make_async_copy(in_ref.at[j], out_ref.at[my_chip], local_sem)
                local.wait() if wait else local.start()

            @pl.when(jnp.logical_not(to_me))
            def _():
                remote = pltpu.make_async_remote_copy(
                    src_ref=in_ref.at[j], dst_ref=out_ref.at[my_chip], send_sem=send_sems.at[j],
                    recv_sem=recv_sems.at[my_chip], device_id=(c, j // 2, j % 2), device_id_type=MESH)
                remote.wait_send() if wait else remote.start()

        for j in range(4):
            send(j, wait=False)
        for j in range(4):
            send(j, wait=True)
        for k in range(4):
            from_me = (k // 2 == x) & (k % 2 == y) & (c == x)

            @pl.when(jnp.logical_not(from_me))
            def _():
                pltpu.make_async_remote_copy(
                    src_ref=in_ref.at[0], dst_ref=out_ref.at[k], send_sem=send_sems.at[0], recv_sem=recv_sems.at[k],
                    device_id=(k // 2, k % 2, x), device_id_type=MESH).wait_recv()

    return pl.pallas_call(
        body, name=name, out_shape=_sds(arr.shape, arr.dtype), in_specs=[ANY], out_specs=ANY,
        scratch_shapes=[pltpu.SemaphoreType.DMA((4,)), pltpu.SemaphoreType.DMA((4,)), pltpu.SemaphoreType.DMA],
    )(arr)


class _ChipExchange:
    def __init__(self, ins, outs, send_sems, recv_sems, local_sems):
        self.ins, self.outs, self.n = ins, outs, len(ins)
        self.send_sems, self.recv_sems, self.local_sems = send_sems, recv_sems, local_sems

    @staticmethod
    def scratch(n):
        return [pltpu.SemaphoreType.DMA((3, n)), pltpu.SemaphoreType.DMA((3, n)), pltpu.SemaphoreType.DMA((n,))]

    def _local(self, a):
        x, y, _, _ = _place()
        me = 2 * x + y
        return pltpu.make_async_copy(self.ins[a].at[me], self.outs[a].at[me], self.local_sems.at[a])

    def _copies(self, a):
        x, y, c, chips = _place()
        me = 2 * x + y
        return [(pltpu.make_async_remote_copy(
                     src_ref=self.ins[a].at[2 * px + py], dst_ref=self.outs[a].at[me],
                     send_sem=self.send_sems.at[k, a], recv_sem=self.recv_sems.at[k, a],
                     device_id=(px, py, c), device_id_type=MESH),
                 pltpu.make_async_remote_copy(
                     src_ref=self.ins[a].at[me], dst_ref=self.outs[a].at[2 * px + py],
                     send_sem=self.send_sems.at[k, a], recv_sem=self.recv_sems.at[k, a],
                     device_id=(px, py, c), device_id_type=MESH))
                for k, (px, py) in enumerate(chips)]

    def start(self):
        for a in range(self.n):
            self._local(a).start()
        for a in range(self.n):
            for send, _ in self._copies(a):
                send.start()

    def finish(self):
        for a in range(self.n):
            for send, recv in self._copies(a):
                send.wait_send()
                recv.wait_recv()
        for a in range(self.n):
            self._local(a).wait()


def _mix_in(x, g_mix, w_in_own, conv_w_own, tm, shards):
    T = x.shape[0]
    n_t = T // tm
    n_s = len(shards)

    def body(x_ref, g_ref, w_own, cw_own, *rest):
        sh_in, rest = rest[:n_s], rest[n_s:]
        u_ref, h_ref, w_hbm, cw_all = rest[:4]
        sh_out, rest = rest[4:4 + n_s], rest[4 + n_s:]
        w_all, keep_sem = rest[:2]
        first = _Gather([w_own, cw_own], [w_all, cw_all], *rest[2:5])
        later = _Gather(sh_in, sh_out, *rest[5:8], core_major=True)
        keep = pltpu.make_async_copy(w_all, w_hbm, keep_sem)
        i = pl.program_id(0)

        @pl.when(i == 0)
        def _():
            first.start()
            later.start()
            first.finish()
            keep.start()

        xv = x_ref[...]
        h = (xv * _rstd(xv) * g_ref[...]).astype(BF16)
        h_ref[...] = h
        u_ref[...] = lax.dot_general(h, w_all[...].reshape(D_IN, D_MODEL), NT, preferred_element_type=F32)

        @pl.when(i == n_t - 1)
        def _():
            later.finish()
            keep.wait()

    own = (D_IN // N_DEV, D_MODEL)
    outs = pl.pallas_call(
        body, name="mix_in", grid=(n_t,),
        in_specs=[pl.BlockSpec((tm, D_MODEL), lambda i: (i, 0)), pl.BlockSpec((1, D_MODEL), lambda i: (0, 0))]
        + [ANY] * (2 + n_s),
        out_specs=[pl.BlockSpec((tm, D_IN), lambda i: (i, 0)), pl.BlockSpec((tm, D_MODEL), lambda i: (i, 0))]
        + [ANY] * (2 + n_s),
        out_shape=[_sds((T, D_IN), F32), _sds((T, D_MODEL), BF16), _sds((N_DEV,) + own, BF16),
                   _sds((N_DEV,) + conv_w_own.shape, F32)] + [_sds((N_DEV,) + a.shape, a.dtype) for a in shards],
        scratch_shapes=[pltpu.VMEM((N_DEV,) + own, BF16), pltpu.SemaphoreType.DMA] + _Gather.scratch(2)
        + _Gather.scratch(n_s),
        compiler_params=pltpu.CompilerParams(dimension_semantics=("arbitrary",)),
    )(x, g_mix, w_in_own, conv_w_own, *shards)
    return outs[0], outs[1], outs[2], outs[3], list(outs[4:])


def _mixer_fwd(u, x, pv, wa, wx, wp, w_out_b, g_ffn, tm, shards=()):
    T = u.shape[0]
    n_s = len(shards)
    n_t = T // tm

    def body(u_ref, x_ref, pv_ref, wa_in, wx_in, wp_in, wo_ref, gf_ref, *rest):
        sh_in, rest = rest[:n_s], rest[n_s:]
        y_ref, hs_ref, hres_ref, h2_ref, saved_ref = rest[:5]
        sh_out, rest = rest[5:5 + n_s], rest[5 + n_s:]
        e_lru, e_pool, a_s, b_s, hc, wa_ref, wx_ref, wp_ref = rest[:8]
        gather = _Gather(sh_in, sh_out, *rest[8:], core_major=True) if n_s else None
        i = pl.program_id(0)

        @pl.when(i == 0)
        def _():
            if gather:
                gather.start()
            e_lru[pl.ds(0, HALO), :] = jnp.zeros((HALO, LRU_W), F32)
            e_pool[pl.ds(0, HALO), :] = jnp.zeros((HALO, POOL_W), F32)
            hc[...] = jnp.zeros((8, LRU_W), F32)
            _fill_block_diag(wa_ref, wa_in)
            _fill_block_diag(wx_ref, wx_in)
            _fill_block_diag(wp_ref, wp_in)

        e_lru[pl.ds(HALO, tm), :] = u_ref[:, 0:LRU_W]
        e_pool[pl.ds(HALO, tm), :] = u_ref[:, 2 * LRU_W:D_IN]
        pv = pv_ref[...]
        xc, r, ig, la = _lru_gates(e_lru, pv, wa_ref, wx_ref, tm)
        for q, val in enumerate((xc, r, ig, la)):
            saved_ref[:, LRU_W * q:LRU_W * (q + 1)] = val
        a, _, mult, _ = _lru_decay(la)
        a_s[...] = a
        b_s[...] = mult * (ig * xc)
        hc[...] = _scan_tile(a_s, b_s, hs_ref, hc[...], tm, reverse=False)
        gl, _ = _gelu_parts(u_ref[:, LRU_W:2 * LRU_W])
        y_lru = hs_ref[...] * gl
        _, zp, _ = _pool_pre(e_pool, pv, wp_ref, tm, i * tm)
        y_pool = zp * pv[ROW_PS:ROW_PS + 1, :]
        yn = jnp.concatenate([y_lru * _rstd(y_lru) * pv[ROW_GL:ROW_GL + 1, :],
                              y_pool * _rstd(y_pool) * pv[ROW_GP:ROW_GP + 1, :]], axis=1).astype(BF16)
        for b in range(N_DEV):
            y_ref[:, 128 * _y_pos(b):128 * (_y_pos(b) + 1)] = yn[:, 128 * b:128 * (b + 1)]
        hr = x_ref[...] + jnp.dot(y_ref[...], wo_ref[...], preferred_element_type=F32)
        hres_ref[...] = hr
        h2_ref[...] = (hr * _rstd(hr) * gf_ref[...]).astype(BF16)
        e_lru[pl.ds(0, HALO), :] = e_lru[pl.ds(tm, HALO), :]
        e_pool[pl.ds(0, HALO), :] = e_pool[pl.ds(tm, HALO), :]

        if gather:
            @pl.when(i == n_t - 1)
            def _():
                gather.finish()

    full = lambda shape: pl.BlockSpec(shape, lambda i: (0,) * len(shape))
    row = lambda w: pl.BlockSpec((tm, w), lambda i: (i, 0))
    outs = pl.pallas_call(
        body, name="mixer_fwd", grid=(n_t,),
        in_specs=[row(D_IN), row(D_MODEL), full((16, LRU_W)), full((8, 64, 64)), full((8, 64, 64)), full((4, 128, 128)),
                  full((D_MODEL, D_MODEL)), full((1, D_MODEL))] + [ANY] * n_s,
        out_specs=[row(D_MODEL), row(LRU_W), row(D_MODEL), row(D_MODEL), row(4 * LRU_W)] + [ANY] * n_s,
        out_shape=[_sds((T, D_MODEL), BF16), _sds((T, LRU_W), F32), _sds((T, D_MODEL), F32), _sds((T, D_MODEL), BF16),
                   _sds((T, 4 * LRU_W), F32)] + [_sds((N_DEV,) + a.shape, a.dtype) for a in shards],
        scratch_shapes=[pltpu.VMEM((HALO + tm, LRU_W), F32), pltpu.VMEM((HALO + tm, POOL_W), F32),
                        pltpu.VMEM((tm, LRU_W), F32), pltpu.VMEM((tm, LRU_W), F32), pltpu.VMEM((8, LRU_W), F32)]
        + [pltpu.VMEM((2, 256, 256), BF16)] * 3 + (_Gather.scratch(n_s) if n_s else []),
        compiler_params=pltpu.CompilerParams(dimension_semantics=("arbitrary",)),
    )(u, x, pv, wa, wx, wp, w_out_b, g_ffn, *shards)
    return outs[0], outs[1], outs[2], outs[3], outs[4], list(outs[5:])


def _ffn_fwd(hres, h2, w1_b, w3_b, w2_b, g_fin, tgt, tm, tn):
    T = hres.shape[0]
    n_j = D_FF // tn

    def body(hres_ref, h2_ref, w1_ref, w3_ref, w2_ref, gfin_ref, tgt_ref,
             g_ref, v_ref, d3_ref, loss_ref, dgfin_ref, acc):
        i, j = pl.program_id(0), pl.program_id(1)

        @pl.when(j == 0)
        def _():
            acc[...] = jnp.zeros((tm, D_MODEL), F32)

        @pl.when((j == 0) & (i == 0))
        def _():
            loss_ref[...] = jnp.zeros((8, 128), F32)
            dgfin_ref[...] = jnp.zeros((1, D_MODEL), F32)

        h2 = h2_ref[...]
        g = lax.dot_general(h2, w1_ref[...], NT, preferred_element_type=F32)
        v = lax.dot_general(h2, w3_ref[...], NT, preferred_element_type=F32)
        g_ref[...] = g.astype(BF16)
        v_ref[...] = v.astype(BF16)
        ff = ((g * _sigmoid(g)) * v).astype(BF16)
        acc[...] += jnp.dot(ff, w2_ref[...], preferred_element_type=F32)

        @pl.when(j == n_j - 1)
        def _():
            h3 = hres_ref[...] + acc[...]
            rstd = _rstd(h3)
            xh = h3 * rstd
            gfin = gfin_ref[...]
            err = xh * gfin - tgt_ref[...]
            loss_ref[...] += 0.5 * jnp.sum(jnp.mean(err * err, axis=-1, keepdims=True))
            dout = err * (1.0 / D_MODEL)
            dx, dgain = _rms_bwd(dout, xh, rstd, gfin)
            d3_ref[...] = dx
            dgfin_ref[...] += dgain

    row = lambda w: pl.BlockSpec((tm, w), lambda i, j: (i, 0))
    const = lambda shape: pl.BlockSpec(shape, lambda i, j: (0,) * len(shape))
    return pl.pallas_call(
        body, name="ffn_fwd", grid=(T // tm, n_j),
        in_specs=[row(D_MODEL), row(D_MODEL),
                  pl.BlockSpec((tn, D_MODEL), lambda i, j: (j, 0)), pl.BlockSpec((tn, D_MODEL), lambda i, j: (j, 0)),
                  pl.BlockSpec((tn, D_MODEL), lambda i, j: (j, 0)), const((1, D_MODEL)), row(D_MODEL)],
        out_specs=[pl.BlockSpec((tm, tn), lambda i, j: (i, j)), pl.BlockSpec((tm, tn), lambda i, j: (i, j)),
                   row(D_MODEL), const((8, 128)), const((1, D_MODEL))],
        out_shape=[_sds((T, D_FF), BF16), _sds((T, D_FF), BF16),
                   _sds((T, D_MODEL), F32), _sds((8, 128), F32), _sds((1, D_MODEL), F32)],
        scratch_shapes=[pltpu.VMEM((tm, D_MODEL), F32)],
        compiler_params=pltpu.CompilerParams(dimension_semantics=("arbitrary", "arbitrary")),
    )(hres, h2, w1_b, w3_b, w2_b, g_fin, tgt)


def _ffn_bwd(d3, g, v, w1_b, w3_b, w2_b, hres, g_ffn, tm, tn):
    T = d3.shape[0]
    n_j = D_FF // tn

    def body(d3_ref, g_ref, v_ref, w1_ref, w3_ref, w2_ref, hres_ref, gf_ref,
             dg_ref, dv_ref, ff_ref, d2_ref, dgffn_ref, acc):
        i, j = pl.program_id(0), pl.program_id(1)

        @pl.when(j == 0)
        def _():
            acc[...] = jnp.zeros((tm, D_MODEL), F32)

        @pl.when((j == 0) & (i == 0))
        def _():
            dgffn_ref[...] = jnp.zeros((1, D_MODEL), F32)

        dff = lax.dot_general(d3_ref[...].astype(BF16), w2_ref[...], NT, preferred_element_type=F32)
        gv = g_ref[...].astype(F32)
        vv = v_ref[...].astype(F32)
        sg = _sigmoid(gv)
        sl = gv * sg
        dgb = (dff * vv * (sg * (1.0 + gv * (1.0 - sg)))).astype(BF16)
        dvb = (dff * sl).astype(BF16)
        dg_ref[...] = dgb
        dv_ref[...] = dvb
        ff_ref[...] = (sl * vv).astype(BF16)
        acc[...] += (jnp.dot(dgb, w1_ref[...], preferred_element_type=F32)
                     + jnp.dot(dvb, w3_ref[...], preferred_element_type=F32))

        @pl.when(j == n_j - 1)
        def _():
            hr = hres_ref[...]
            rstd = _rstd(hr)
            dx, dgain = _rms_bwd(acc[...], hr * rstd, rstd, gf_ref[...])
            d2_ref[...] = d3_ref[...] + dx
            dgffn_ref[...] += dgain

    row = lambda w: pl.BlockSpec((tm, w), lambda i, j: (i, 0))
    tile = pl.BlockSpec((tm, tn), lambda i, j: (i, j))
    const = lambda shape: pl.BlockSpec(shape, lambda i, j: (0,) * len(shape))
    return pl.pallas_call(
        body, name="ffn_bwd", grid=(T // tm, n_j),
        in_specs=[row(D_MODEL), tile, tile,
                  pl.BlockSpec((tn, D_MODEL), lambda i, j: (j, 0)), pl.BlockSpec((tn, D_MODEL), lambda i, j: (j, 0)),
                  pl.BlockSpec((tn, D_MODEL), lambda i, j: (j, 0)), row(D_MODEL), const((1, D_MODEL))],
        out_specs=[tile, tile, tile, row(D_MODEL), const((1, D_MODEL))],
        out_shape=[_sds((T, D_FF), BF16), _sds((T, D_FF), BF16), _sds((T, D_FF), BF16),
                   _sds((T, D_MODEL), F32), _sds((1, D_MODEL), F32)],
        scratch_shapes=[pltpu.VMEM((tm, D_MODEL), F32)],
        compiler_params=pltpu.CompilerParams(dimension_semantics=("arbitrary", "arbitrary")),
    )(d3, g, v, w1_b, w3_b, w2_b, hres, g_ffn)


def _at_b_pair(a, b, c_arr, name, tk, gather=()):
    T, M = a.shape
    N = b.shape[1]
    hm, n_k = M // 2, T // tk
    n_g = len(gather)

    def body(c_ref, a_ref, b_ref, *rest):
        g_in, o_ref, rest = rest[:n_g], rest[n_g], rest[n_g + 1:]
        g_out, rest = rest[:n_g], rest[n_g:]
        acc, landed, send_sem, recv_sem = rest[:4]
        ag = _Gather(g_in, g_out, *rest[4:]) if n_g else None
        ph, k = pl.program_id(0), pl.program_id(1)

        def hand_over():
            x, y, c, _ = _place()
            return pltpu.make_async_remote_copy(
                src_ref=acc.at[0], dst_ref=landed, send_sem=send_sem, recv_sem=recv_sem,
                device_id=(x, y, 1 - c), device_id_type=MESH)

        if ag:
            @pl.when((ph == 0) & (k == 0))
            def _():
                ag.start()

        @pl.when(k == 0)
        def _():
            acc[ph] = jnp.zeros((hm, N), F32)

        acc[ph] += lax.dot_general(a_ref[...].astype(BF16), b_ref[...].astype(BF16), TN, preferred_element_type=F32)

        @pl.when((ph == 0) & (k == n_k - 1))
        def _():
            hand_over().start()

        @pl.when((ph == 1) & (k == n_k - 1))
        def _():
            copy = hand_over()
            copy.wait_recv()
            o_ref[...] = (acc[1] + landed[...]).astype(BF16)
            copy.wait_send()
            if ag:
                ag.finish()

    outs = pl.pallas_call(
        body, name=name,
        grid_spec=pltpu.PrefetchScalarGridSpec(
            num_scalar_prefetch=1, grid=(2, n_k),
            in_specs=[pl.BlockSpec((tk, hm), lambda ph, k, c_ref: (k, (ph + 1 - c_ref[0]) % 2)),
                      pl.BlockSpec((tk, N), lambda ph, k, c_ref: (k, 0))] + [ANY] * n_g,
            out_specs=[pl.BlockSpec((hm, N), lambda ph, k, c_ref: (0, 0))] + [ANY] * n_g,
            scratch_shapes=[pltpu.VMEM((2, hm, N), F32), pltpu.VMEM((hm, N), F32),
                            pltpu.SemaphoreType.DMA, pltpu.SemaphoreType.DMA] + (_Gather.scratch(n_g) if n_g else [])),
        out_shape=[_sds((hm, N), BF16)] + [_sds((N_DEV,) + g.shape, g.dtype) for g in gather],
        compiler_params=pltpu.CompilerParams(dimension_semantics=("arbitrary", "arbitrary")),
    )(c_arr, a, b, *gather)
    return (outs[0], list(outs[1:])) if n_g else outs[0]


def _mixer_bwd(d2, u, hs, saved, pv, wa, wx, wp, w_out_b, tm, chip_sums=()):
    T = u.shape[0]
    n_t = T // tm
    n_x = len(chip_sums)

    def body(d2_ref, u_ref, uh_ref, hs_ref, hh_ref, saved_ref, pv_ref, wa_in, wx_in, wp_in, wo_ref, *rest):
        x_in, rest = rest[:n_x], rest[n_x:]
        du_ref, sg_ref = rest[:2]
        x_out, rest = rest[2:2 + n_x], rest[2 + n_x:]
        e_pool, e_h, a_s, b_s, dh_s, mu_s, f_x, f_p, mc, cx, cp = rest[:11]
        wa_ref, wx_ref, wp_ref, vacc_ref, dwa_ref, dwx_ref, dwp_ref = rest[11:18]
        exchange = _ChipExchange(x_in, x_out, *rest[18:]) if n_x else None
        s = pl.program_id(0)
        it = n_t - 1 - s

        @pl.when(s == 0)
        def _():
            if exchange:
                exchange.start()
            mc[...] = jnp.zeros((8, LRU_W), F32)
            cx[...] = jnp.zeros((8, LRU_W), F32)
            cp[...] = jnp.zeros((HALO, POOL_W), F32)
            vacc_ref[...] = jnp.zeros((16, LRU_W), F32)
            dwa_ref[...] = jnp.zeros((2, 256, 256), F32)
            dwx_ref[...] = jnp.zeros((2, 256, 256), F32)
            dwp_ref[...] = jnp.zeros((2, 256, 256), F32)
            _fill_block_diag(wa_ref, wa_in)
            _fill_block_diag(wx_ref, wx_in)
            _fill_block_diag(wp_ref, wp_in)

        first = it == 0
        e_pool[pl.ds(0, HALO), :] = jnp.where(first, 0.0, uh_ref[...])
        e_pool[pl.ds(HALO, tm), :] = u_ref[:, 2 * LRU_W:D_IN]
        e_h[pl.ds(0, 8), :] = jnp.where(first, 0.0, hh_ref[...])
        e_h[pl.ds(8, tm), :] = hs_ref[...]
        pv = pv_ref[...]
        saved = lambda q: saved_ref[:, LRU_W * q:LRU_W * (q + 1)]

        dyn = lax.dot_general(d2_ref[...].astype(BF16), wo_ref[...], NT, preferred_element_type=F32)
        dyn = jnp.concatenate([dyn[:, 128 * _y_pos(b):128 * (_y_pos(b) + 1)] for b in range(N_DEV)], axis=1)

        h = hs_ref[...]
        ug = u_ref[:, LRU_W:2 * LRU_W]
        gl, dgl = _gelu_parts(ug)
        y_lru = h * gl
        rstd_l = _rstd(y_lru)
        dy_lru, d_gain_l = _rms_bwd(dyn[:, 0:LRU_W], y_lru * rstd_l, rstd_l, pv[ROW_GL:ROW_GL + 1, :])
        dh = dy_lru * gl
        du_ref[:, LRU_W:2 * LRU_W] = (dy_lru * h * dgl).astype(BF16)
        a_s[...] = jnp.exp(saved(3))
        b_s[...] = a_s[...] * dh
        dh_s[...] = dh
        mu_s[pl.ds(tm, 8), :] = mc[...]
        mc[...] = _scan_tile(a_s, b_s, mu_s, mc[...], tm, reverse=True)
        xc, r, ig = saved(0), saved(1), saved(2)
        a, om, mult, rmult = _lru_decay(saved(3))
        lam_t = dh_s[...] + mu_s[pl.ds(1, tm), :]
        da = lam_t * e_h[pl.ds(7, tm), :]
        dmult = lam_t * (ig * xc)
        di = lam_t * (mult * xc)
        dxc = lam_t * (mult * ig)
        dla = da * a - jnp.where(om > 1e-12, dmult * ((a * a) * rmult), 0.0)
        dra = (dla * (-LRU_C * _softplus_neg_lambda(pv))) * (r * (1.0 - r))
        dia = di * (ig * (1.0 - ig))
        drab = dra.astype(BF16)
        diab = dia.astype(BF16)
        xcb = xc.astype(BF16)
        dxc = dxc + _bd_t(drab, wa_ref) + _bd_t(diab, wx_ref)
        dwa_ref[...] += _bd_grad(xcb, drab)
        dwx_ref[...] += _bd_grad(xcb, diab)
        sig_neg_lam = _sigmoid(-pv[ROW_LAM:ROW_LAM + 1, :])
        d_lam = jnp.sum(dla * r, axis=0, keepdims=True) * (LRU_C * sig_neg_lam)

        f_x[pl.ds(0, tm), :] = dxc
        f_x[pl.ds(tm, 8), :] = cx[...]
        du_lru = jnp.zeros((tm, LRU_W), F32)
        u_lru = u_ref[:, 0:LRU_W]
        d_cw = []
        for k in range(4):
            later = f_x[pl.ds(3 - k, tm), :]
            du_lru = du_lru + later * pv[ROW_CW + k:ROW_CW + k + 1, :]
            d_cw.append(jnp.sum(later * u_lru, axis=0, keepdims=True))
        du_ref[:, 0:LRU_W] = du_lru.astype(BF16)
        cx[...] = f_x[pl.ds(0, 8), :]

        pooled_b, zp, inv_cnts = _pool_pre(e_pool, pv, wp_ref, tm, it * tm)
        ps = pv[ROW_PS:ROW_PS + 1, :]
        y_pool = zp * ps
        rstd_p = _rstd(y_pool)
        dy_pool, d_gain_p = _rms_bwd(dyn[:, LRU_W:D_MODEL], y_pool * rstd_p, rstd_p, pv[ROW_GP:ROW_GP + 1, :])
        dz = dy_pool * ps
        dzb = dz.astype(BF16)
        dwp_ref[...] += _bd_grad(pooled_b, dzb)
        dpooled = _bd_t(dzb, wp_ref)
        for g, w in enumerate(POOL_WINDOWS):
            f_p[pl.ds(0, tm), pl.ds(128 * g, 128)] = dpooled[:, 128 * g:128 * (g + 1)] * inv_cnts[g]
        f_p[pl.ds(tm, HALO), :] = cp[...]
        for g, w in enumerate(POOL_WINDOWS):
            acc = _window_sum(f_p[:, pl.ds(128 * g, 128)], w, back=False)[0:tm, :]
            du_ref[:, 2 * LRU_W + 128 * g:2 * LRU_W + 128 * (g + 1)] = (
                acc - dpooled[:, 128 * g:128 * (g + 1)]).astype(BF16)
        cp[...] = f_p[pl.ds(0, HALO), :]

        rows = d_cw + [
            jnp.sum(dxc, axis=0, keepdims=True),
            jnp.sum(dra, axis=0, keepdims=True),
            jnp.sum(dia, axis=0, keepdims=True),
            d_lam,
            jnp.sum(dz, axis=0, keepdims=True),
            jnp.sum(dy_pool * zp, axis=0, keepdims=True),
            d_gain_l, d_gain_p,
            jnp.zeros((4, LRU_W), F32),
        ]
        vacc_ref[...] += jnp.concatenate(rows, axis=0)

        @pl.when(s == n_t - 1)
        def _():
            sg_ref[SG_VEC:SG_VEC + 16, :] = vacc_ref[:, 0:256]
            sg_ref[SG_VEC + 16:SG_VEC + 32, :] = vacc_ref[:, 256:512]
            for half in range(2):
                sg_ref[SG_WA + 64 * half:SG_WA + 64 * (half + 1), :] = _diag_pack(dwa_ref[half], 64)
                sg_ref[SG_WX + 64 * half:SG_WX + 64 * (half + 1), :] = _diag_pack(dwx_ref[half], 64)
                sg_ref[SG_WP + 128 * half:SG_WP + 128 * (half + 1), :] = _diag_pack(dwp_ref[half], 128)
            if exchange:
                exchange.finish()

    rev = lambda w: pl.BlockSpec((tm, w), lambda s: (n_t - 1 - s, 0))
    full = lambda shape: pl.BlockSpec(shape, lambda s: (0,) * len(shape))
    outs = pl.pallas_call(
        body, name="mixer_bwd", grid=(n_t,),
        in_specs=[rev(D_MODEL), rev(D_IN),
                  pl.BlockSpec((HALO, POOL_W), lambda s: (jnp.maximum((n_t - 1 - s) * (tm // HALO) - 1, 0), 2)),
                  rev(LRU_W),
                  pl.BlockSpec((8, LRU_W), lambda s: (jnp.maximum((n_t - 1 - s) * (tm // 8) - 1, 0), 0)),
                  rev(4 * LRU_W), full((16, LRU_W)), full((8, 64, 64)), full((8, 64, 64)), full((4, 128, 128)),
                  full((D_MODEL, D_MODEL))] + [ANY] * n_x,
        out_specs=[rev(D_IN), full((SG_ROWS, 256))] + [ANY] * n_x,
        out_shape=[_sds((T, D_IN), BF16), _sds((SG_ROWS, 256), F32)] + [_sds(a.shape, a.dtype) for a in chip_sums],
        scratch_shapes=[pltpu.VMEM((HALO + tm, POOL_W), F32),
                        pltpu.VMEM((8 + tm, LRU_W), F32)] + [pltpu.VMEM((tm, LRU_W), F32)] * 3 + [
                        pltpu.VMEM((tm + 8, LRU_W), F32), pltpu.VMEM((tm + 8, LRU_W), F32),
                        pltpu.VMEM((tm + HALO, POOL_W), F32), pltpu.VMEM((8, LRU_W), F32),
                        pltpu.VMEM((8, LRU_W), F32), pltpu.VMEM((HALO, POOL_W), F32)]
        + [pltpu.VMEM((2, 256, 256), BF16)] * 3 + [pltpu.VMEM((16, LRU_W), F32)] + [pltpu.VMEM((2, 256, 256), F32)] * 3
        + (_ChipExchange.scratch(n_x) if n_x else []),
        compiler_params=pltpu.CompilerParams(dimension_semantics=("arbitrary",)),
    )(d2, u, u, hs, hs, saved, pv, wa, wx, wp, w_out_b, *chip_sums)
    return outs[0], outs[1], list(outs[2:])


def _mix_in_bwd(du, x, d2, w_in_t, g_mix, tm):
    T = x.shape[0]

    def body(du_ref, x_ref, d2_ref, w_ref, g_ref, dx_ref, dg_ref):
        @pl.when(pl.program_id(0) == 0)
        def _():
            dg_ref[...] = jnp.zeros((1, D_MODEL), F32)

        dh = jnp.dot(du_ref[...], w_ref[...], preferred_element_type=F32)
        xv = x_ref[...]
        rstd = _rstd(xv)
        dx, dgain = _rms_bwd(dh, xv * rstd, rstd, g_ref[...])
        dx_ref[...] = d2_ref[...] + dx
        dg_ref[...] += dgain

    row = lambda w: pl.BlockSpec((tm, w), lambda i: (i, 0))
    const = lambda shape: pl.BlockSpec(shape, lambda i: (0,) * len(shape))
    return pl.pallas_call(
        body, name="mix_in_bwd", grid=(T // tm,),
        in_specs=[row(D_IN), row(D_MODEL), row(D_MODEL), const((D_IN, D_MODEL)), const((1, D_MODEL))],
        out_specs=[row(D_MODEL), const((1, D_MODEL))],
        out_shape=[_sds((T, D_MODEL), F32), _sds((1, D_MODEL), F32)],
        compiler_params=pltpu.CompilerParams(dimension_semantics=("arbitrary",)),
    )(du, x, d2, w_in_t, g_mix)


def _adamw(w, g, m, v):
    m = ADAM_B1 * m + (1.0 - ADAM_B1) * g
    v = ADAM_B2 * v + (1.0 - ADAM_B2) * (g * g)
    m_hat = m / (1.0 - ADAM_B1 ** ADAM_STEP)
    v_hat = v / (1.0 - ADAM_B2 ** ADAM_STEP)
    delta = -ADAM_LR * (m_hat / (jnp.sqrt(v_hat) + ADAM_EPS) + ADAM_WD * w)
    return delta, m, v


def _adam_shards(ws, ms, vs, parts):
    n = len(ws)
    n_blk = [w.shape[0] // ADAM_ROWS for w in ws]

    def body(*refs):
        w_refs, m_refs, v_refs, p_refs, outs = (refs[:n], refs[n:2 * n], refs[2 * n:3 * n], refs[3 * n:4 * n],
                                                refs[4 * n:])
        i = pl.program_id(0)
        for a in range(n):
            @pl.when(i < n_blk[a])
            def _(a=a):
                g = p_refs[a][0].astype(F32)
                for j in range(1, 4):
                    g = g + p_refs[a][j].astype(F32)
                delta, new_m, new_v = _adamw(w_refs[a][...], g, m_refs[a][...], v_refs[a][...])
                for kind, val in enumerate((g, delta, new_m, new_v)):
                    outs[4 * a + kind][...] = val

    blk = lambda a: pl.BlockSpec((ADAM_ROWS, D_MODEL), lambda i: (jnp.minimum(i, n_blk[a] - 1), 0))
    part_blk = lambda a: pl.BlockSpec((4, ADAM_ROWS, D_MODEL), lambda i: (0, jnp.minimum(i, n_blk[a] - 1), 0))
    res = pl.pallas_call(
        body, name="adam_shards", grid=(max(n_blk),),
        in_specs=[blk(a) for a in range(n)] * 3 + [part_blk(a) for a in range(n)],
        out_specs=[blk(a) for a in range(n) for _ in range(4)],
        out_shape=[_sds(w.shape, F32) for w in ws for _ in range(4)],
        compiler_params=pltpu.CompilerParams(dimension_semantics=("arbitrary",)),
    )(*ws, *ms, *vs, *parts)
    return [tuple(res[4 * a:4 * a + 4]) for a in range(n)]


SMALL_PARAMS = [("norm_mix_g", (1, D_MODEL)), ("conv_w", (1, 4, 64)), ("conv_b", (1, LRU_W)),
                ("gate_a_w", (1, 8, 64, 64)), ("gate_a_b", (1, LRU_W)), ("gate_x_w", (1, 8, 64, 64)),
                ("gate_x_b", (1, LRU_W)), ("lru_lambda", (1, LRU_W)), ("pool_w", (1, 4, 128, 128)),
                ("pool_b", (1, POOL_W)), ("pool_scale", (1, POOL_W)), ("norm_lru_g", (1, LRU_W)),
                ("norm_pool_g", (1, POOL_W)), ("norm_ffn_g", (1, D_MODEL)), ("final_norm_g", (1, D_MODEL))]
VEC_ROW = dict(conv_b=ROW_CB, gate_a_b=ROW_BA, gate_x_b=ROW_BX, lru_lambda=ROW_LAM, pool_b=ROW_PB, pool_scale=ROW_PS,
               norm_lru_g=ROW_GL, norm_pool_g=ROW_GP)
WHOLE = (Ellipsis,)


def _unpack_mixer_grads(sg, dev):
    vec = jnp.concatenate([sg[SG_VEC:SG_VEC + 16], sg[SG_VEC + 16:SG_VEC + 32]], axis=1)
    out = {nm: [(WHOLE, vec[r:r + 1])] for nm, r in VEC_ROW.items()}
    own = jnp.zeros((4, 64), F32)
    for d in range(N_DEV):
        own = jnp.where(dev == d, vec[ROW_CW:ROW_CW + 4, 64 * d:64 * (d + 1)], own)
    out["conv_w"] = [((0,), own)]
    for nm, row0 in (("gate_a_w", SG_WA), ("gate_x_w", SG_WX)):
        out[nm] = [((0, b), sg[row0 + 64 * (b // 4):row0 + 64 * (b // 4 + 1), 64 * (b % 4):64 * (b % 4 + 1)])
                   for b in range(8)]
    out["pool_w"] = [((0, b), sg[SG_WP + 128 * (b // 2):SG_WP + 128 * (b // 2 + 1), 128 * (b % 2):128 * (b % 2 + 1)])
                     for b in range(4)]
    return out


def _adam_small(parts, w, m, v):
    names = [nm for nm, _ in SMALL_PARAMS]
    n = len(names)

    def body(sg_ref, gm_ref, gf_ref, gn_ref, ls_ref, *rest):
        w_refs, m_refs, v_refs, outs = rest[:n], rest[n:2 * n], rest[2 * n:3 * n], rest[3 * n:]
        dev = 4 * lax.axis_index("x") + 2 * lax.axis_index("y") + lax.axis_index("c")

        def total(ref):
            acc = ref[0]
            for d in range(1, N_DEV):
                acc = acc + ref[d]
            return acc

        pieces = _unpack_mixer_grads(total(sg_ref), dev)
        pieces["norm_mix_g"] = [(WHOLE, total(gm_ref))]
        pieces["norm_ffn_g"] = [(WHOLE, total(gf_ref))]
        pieces["final_norm_g"] = [(WHOLE, total(gn_ref))]
        for i, nm in enumerate(names):
            for idx, g in pieces[nm]:
                delta, new_m, new_v = _adamw(w_refs[i][idx], g, m_refs[i][idx], v_refs[i][idx])
                for kind, val in enumerate((g, delta, new_m, new_v)):
                    outs[4 * i + kind][idx] = val
        outs[4 * n][...] = total(ls_ref)

    shapes = [_sds(shape, F32) for _, shape in SMALL_PARAMS for _ in range(4)] + [_sds((8, 128), F32)]
    res = pl.pallas_call(body, name="adam_small", out_shape=shapes)(
        *parts, *[w[nm] for nm in names], *[m[nm] for nm in names], *[v[nm] for nm in names])
    return {nm: tuple(res[4 * i:4 * i + 4]) for i, nm in enumerate(names)}, res[4 * n][0, 0]


def _vec_rows(conv_w_full, conv_b, ba, bx, lam, pb, ps, gl, gp):
    return jnp.concatenate([conv_w_full, conv_b, ba, bx, lam, pb, ps, gl, gp, jnp.zeros((4, LRU_W), F32)], axis=0)


WEIGHT_ORDER = ['norm_mix_g', 'w_in', 'conv_w', 'conv_b', 'gate_a_w', 'gate_a_b', 'gate_x_w', 'gate_x_b', 'lru_lambda',
                'pool_w', 'pool_b', 'pool_scale', 'norm_lru_g', 'norm_pool_g', 'w_out', 'norm_ffn_g', 'ffn_w1', 'ffn_w3',
                'ffn_w2', 'final_norm_g']


def kernel(x, norm_mix_g, w_in, conv_w, conv_b, gate_a_w, gate_a_b, gate_x_w, gate_x_b, lru_lambda, pool_w, pool_b, pool_scale, norm_lru_g, norm_pool_g, w_out, norm_ffn_g, ffn_w1, ffn_w3, ffn_w2, final_norm_g, loss_target, m_norm_mix_g, m_w_in, m_conv_w, m_conv_b, m_gate_a_w, m_gate_a_b, m_gate_x_w, m_gate_x_b, m_lru_lambda, m_pool_w, m_pool_b, m_pool_scale, m_norm_lru_g, m_norm_pool_g, m_w_out, m_norm_ffn_g, m_ffn_w1, m_ffn_w3, m_ffn_w2, m_final_norm_g, v_norm_mix_g, v_w_in, v_conv_w, v_conv_b, v_gate_a_w, v_gate_a_b, v_gate_x_w, v_gate_x_b, v_lru_lambda, v_pool_w, v_pool_b, v_pool_scale, v_norm_lru_g, v_norm_pool_g, v_w_out, v_norm_ffn_g, v_ffn_w1, v_ffn_w3, v_ffn_w2, v_final_norm_g):
    ac = lax.axis_index("c")
    tm, tmx, tn, tk = 512, 512, 1408, 1024
    tm_in = 1024
    xs, tgt = x[0], loss_target[0]
    g_fin = final_norm_g.reshape(1, D_MODEL)
    c_arr = jnp.reshape(ac, (1,)).astype(jnp.int32)

    tr = lambda w: jnp.swapaxes(w[0], 0, 1)
    own = lambda w: w[0]
    bf = lambda a: a.astype(BF16)

    u, h1, g_in, g_conv, (g_out,) = _mix_in(xs, norm_mix_g, bf(tr(w_in)), conv_w[0], tm, shards=[bf(own(w_out))])
    w_in_t = g_in.reshape(D_IN, D_MODEL)
    conv_w_full = g_conv.transpose(1, 0, 2).reshape(4, LRU_W)
    pv = _vec_rows(conv_w_full, conv_b, gate_a_b, gate_x_b, lru_lambda, pool_b, pool_scale, norm_lru_g, norm_pool_g)
    wa, wx, wp = gate_a_w[0], gate_x_w[0], pool_w[0]
    w_out_b = g_out.reshape(D_MODEL, D_MODEL)
    y, hs, hres, h2, saved, (g_w1, g_w3, g_w2) = _mixer_fwd(
        u, xs, pv, wa, wx, wp, w_out_b, norm_ffn_g, tmx, shards=[bf(tr(ffn_w1)), bf(tr(ffn_w3)), bf(own(ffn_w2))])
    w1_t, w3_t, w2_b = g_w1.reshape(D_FF, D_MODEL), g_w3.reshape(D_FF, D_MODEL), g_w2.reshape(D_FF, D_MODEL)
    g, v, d3, loss_acc, d_gfin = _ffn_fwd(hres, h2, w1_t, w3_t, w2_b, g_fin, tgt, tm, tn)

    dg, dv, ff, d2, d_gffn = _ffn_bwd(d3, g, v, w1_t, w3_t, w2_b, hres, norm_ffn_g, tm, tn)
    chips = lambda a: a.reshape(4, a.shape[0] // 4, a.shape[1])
    early_sums = [chips(_at_b_pair(y, d2, c_arr, "grad_w_out", 2 * tk)), chips(_at_b_pair(dg, h2, c_arr, "grad_w1", tk)),
                  chips(_at_b_pair(dv, h2, c_arr, "grad_w3", tk)), chips(_at_b_pair(ff, d3, c_arr, "grad_w2", tk))]
    du, d_mixer, early_parts = _mixer_bwd(d2, u, hs, saved, pv, wa, wx, wp, w_out_b, tmx, chip_sums=early_sums)
    grad_x, d_gmix = _mix_in_bwd(du, xs, d2, w_in_t, norm_mix_g, tm_in)
    d_win, small_parts = _at_b_pair(du, h1, c_arr, "grad_w_in", tk,
                                    gather=[d_mixer, d_gmix, d_gffn, d_gfin, loss_acc])
    parts = [_half_exchange(chips(d_win), "grads_to_chips_w_in")] + list(early_parts)

    res = {}
    shard_w = dict(w_in=(w_in, m_w_in, v_w_in, tr), w_out=(w_out, m_w_out, v_w_out, own),
                   ffn_w1=(ffn_w1, m_ffn_w1, v_ffn_w1, tr), ffn_w3=(ffn_w3, m_ffn_w3, v_ffn_w3, tr),
                   ffn_w2=(ffn_w2, m_ffn_w2, v_ffn_w2, own))
    shard_res = _adam_shards([view(w) for w, _, _, view in shard_w.values()],
                             [view(m) for _, m, _, view in shard_w.values()],
                             [view(v) for _, _, v, view in shard_w.values()], parts)
    for (nm, (_, _, _, view)), outs in zip(shard_w.items(), shard_res):
        res[nm] = [(jnp.swapaxes(o, 0, 1) if view is tr else o)[None] for o in outs]

    row = lambda a: a.reshape(1, D_MODEL)
    small = lambda gm, cw, cb, wa_, ba, wx_, bx, lam, pw, pb, ps, gl, gp, gf, gn: dict(
        norm_mix_g=gm, conv_w=cw, conv_b=cb, gate_a_w=wa_, gate_a_b=ba, gate_x_w=wx_, gate_x_b=bx, lru_lambda=lam,
        pool_w=pw, pool_b=pb, pool_scale=ps, norm_lru_g=gl, norm_pool_g=gp, norm_ffn_g=gf, final_norm_g=row(gn))
    small_res, loss = _adam_small(
        small_parts,
        small(norm_mix_g, conv_w, conv_b, gate_a_w, gate_a_b, gate_x_w, gate_x_b, lru_lambda, pool_w, pool_b,
              pool_scale, norm_lru_g, norm_pool_g, norm_ffn_g, final_norm_g),
        small(m_norm_mix_g, m_conv_w, m_conv_b, m_gate_a_w, m_gate_a_b, m_gate_x_w, m_gate_x_b, m_lru_lambda, m_pool_w,
              m_pool_b, m_pool_scale, m_norm_lru_g, m_norm_pool_g, m_norm_ffn_g, m_final_norm_g),
        small(v_norm_mix_g, v_conv_w, v_conv_b, v_gate_a_w, v_gate_a_b, v_gate_x_w, v_gate_x_b, v_lru_lambda, v_pool_w,
              v_pool_b, v_pool_scale, v_norm_lru_g, v_norm_pool_g, v_norm_ffn_g, v_final_norm_g))
    for nm, outs in small_res.items():
        res[nm] = [o.reshape(D_MODEL) for o in outs] if nm == "final_norm_g" else list(outs)

    out = [loss, grad_x[None]]
    for kind in range(4):
        out += [res[nm][kind] for nm in WEIGHT_ORDER]
    return tuple(out)
```

```python
import jax
import jax.numpy as jnp
from jax import lax
from jax.experimental import pallas as pl
from jax.experimental.pallas import tpu as pltpu

F32 = jnp.float32
BF16 = jnp.bfloat16

D_MODEL = 1024
LRU_W = 512
POOL_W = 512
D_IN = 1536
D_FF = 2816
POOL_WINDOWS = (2, 4, 8, 16)
EPS = 1e-6
LRU_C = 8.0
N_DEV = 8
HALO = 16
SCAN_UNROLL = 4
ADAM_ROWS = 32

ADAM_LR = 0.001
ADAM_B1 = 0.9
ADAM_B2 = 0.999
ADAM_EPS = 1e-08
ADAM_WD = 0.01
ADAM_STEP = 10

ROW_CW, ROW_CB, ROW_BA, ROW_BX, ROW_LAM, ROW_PB, ROW_PS, ROW_GL, ROW_GP = 0, 4, 5, 6, 7, 8, 9, 10, 11
SG_VEC, SG_WA, SG_WX, SG_WP, SG_ROWS = 0, 32, 160, 288, 544

NT = (((1,), (1,)), ((), ()))
TN = (((0,), (0,)), ((), ()))


def _sds(shape, dtype):
    return jax.ShapeDtypeStruct(shape, dtype)


def _sigmoid(x):
    return 0.5 * jnp.tanh(0.5 * x) + 0.5


def _gelu_parts(x):
    c = 0.7978845608028654
    inner = c * (x + 0.044715 * (x * x * x))
    th = jnp.tanh(inner)
    g = 0.5 * x * (1.0 + th)
    dg = 0.5 * (1.0 + th) + 0.5 * x * (1.0 - th * th) * (c * (1.0 + 3.0 * 0.044715 * (x * x)))
    return g, dg


def _window_sum(ext, w, back):
    n = ext.shape[0]
    s, k = ext, 1
    while k < w:
        s = s + pltpu.roll(s, k if back else n - k, 0)
        k *= 2
    return s


def _rstd(x):
    return lax.rsqrt(jnp.mean(x * x, axis=-1, keepdims=True) + EPS)


def _rms_bwd(dy, xhat, rstd, gain):
    dxh = dy * gain
    dx = rstd * (dxh - xhat * jnp.mean(dxh * xhat, axis=-1, keepdims=True))
    return dx, jnp.sum(dy * xhat, axis=0, keepdims=True)


def _bd(xb, w_ref):
    return jnp.concatenate(
        [jnp.dot(xb[:, :256], w_ref[0], preferred_element_type=F32),
         jnp.dot(xb[:, 256:], w_ref[1], preferred_element_type=F32)], axis=1)


def _bd_t(xb, w_ref):
    return jnp.concatenate(
        [lax.dot_general(xb[:, :256], w_ref[0], NT, preferred_element_type=F32),
         lax.dot_general(xb[:, 256:], w_ref[1], NT, preferred_element_type=F32)], axis=1)


def _bd_grad(xb, db):
    return jnp.stack(
        [lax.dot_general(xb[:, :256], db[:, :256], TN, preferred_element_type=F32),
         lax.dot_general(xb[:, 256:], db[:, 256:], TN, preferred_element_type=F32)], axis=0)


def _fill_block_diag(dst, src_ref):
    n, k, _ = src_ref.shape
    dst[...] = jnp.zeros(dst.shape, BF16)
    for b in range(n):
        p, q = divmod(b, 256 // k)
        dst[p, q * k:(q + 1) * k, q * k:(q + 1) * k] = src_ref[b].astype(BF16)


def _diag_pack(w, k):
    lane = lax.broadcasted_iota(jnp.int32, (k, 256), 1)
    out = w[0:k]
    for q in range(1, 256 // k):
        out = jnp.where(lane >= q * k, w[q * k:(q + 1) * k], out)
    return out


def _y_pos(b):
    return 4 * (b % 2) + b // 2


def _softplus_neg_lambda(pv):
    z = -pv[ROW_LAM:ROW_LAM + 1, :]
    return jnp.maximum(z, 0.0) + jnp.log(1.0 + jnp.exp(-jnp.abs(z)))


def _lru_gates(e_lru, pv, wa_ref, wx_ref, tm):
    xc = pv[ROW_CB:ROW_CB + 1, :]
    for k in range(4):
        xc = xc + e_lru[pl.ds(HALO - 3 + k, tm), :] * pv[ROW_CW + k:ROW_CW + k + 1, :]
    xcb = xc.astype(BF16)
    r = _sigmoid(_bd(xcb, wa_ref) + pv[ROW_BA:ROW_BA + 1, :])
    ig = _sigmoid(_bd(xcb, wx_ref) + pv[ROW_BX:ROW_BX + 1, :])
    return xc, r, ig, (-LRU_C * r) * _softplus_neg_lambda(pv)


def _lru_decay(la):
    a = jnp.exp(la)
    om = -jnp.tanh(la) * (1.0 + a * a)
    omc = jnp.maximum(om, 1e-12)
    rmult = lax.rsqrt(omc)
    return a, om, omc * rmult, rmult


def _pool_pre(e_pool, pv, wp_ref, tm, t0):
    t = t0 + lax.broadcasted_iota(jnp.int32, (tm, 1), 0)
    parts, inv_cnts = [], []
    for g, w in enumerate(POOL_WINDOWS):
        ext = e_pool[:, pl.ds(128 * g, 128)]
        s = _window_sum(ext, w, back=True)[HALO:, :]
        inv_cnt = 1.0 / jnp.minimum(t + 1, w).astype(F32)
        inv_cnts.append(inv_cnt)
        parts.append(s * inv_cnt - ext[HALO:, :])
    pooled = jnp.concatenate(parts, axis=1)
    pooled_b = pooled.astype(BF16)
    zp = _bd(pooled_b, wp_ref) + pv[ROW_PB:ROW_PB + 1, :]
    return pooled_b, zp, inv_cnts


def _scan_tile(a_ref, b_ref, out_ref, carry, tm, reverse):
    row = lax.broadcasted_iota(jnp.int32, (8, LRU_W), 0)
    nblk = tm // 8

    def local_scan(blk):
        r0 = pl.multiple_of(blk * 8, 8)
        av = a_ref[pl.ds(r0, 8), :]
        bv = b_ref[pl.ds(r0, 8), :]
        for d in (1, 2, 4):
            sh = (8 - d) if reverse else d
            a_s = pltpu.roll(av, sh, 0)
            b_s = pltpu.roll(bv, sh, 0)
            m = (row < 8 - d) if reverse else (row >= d)
            bv = jnp.where(m, av * b_s + bv, bv)
            av = jnp.where(m, av * a_s, av)
        return r0, av, bv

    def step(i, hin):
        local = [local_scan((nblk - 1 - (i * SCAN_UNROLL + j)) if reverse else (i * SCAN_UNROLL + j))
                 for j in range(SCAN_UNROLL)]
        for r0, av, bv in local:
            hv = av * hin + bv
            out_ref[pl.ds(r0, 8), :] = hv
            hin = jnp.broadcast_to(hv[0:1, :] if reverse else hv[7:8, :], (8, LRU_W))
        return hin

    return lax.fori_loop(0, nblk // SCAN_UNROLL, step, carry)


MESH = pl.DeviceIdType.MESH
ANY = pl.BlockSpec(memory_space=pl.ANY)


def _place():
    x, y, c = lax.axis_index("x"), lax.axis_index("y"), lax.axis_index("c")
    chips = [(1 - x, y), (x, 1 - y), (1 - x, 1 - y)]
    return x, y, c, chips


class _Gather:
    def __init__(self, ins, outs, send_sems, recv_sems, local_sems, core_major=False):
        self.ins, self.outs, self.n = ins, outs, len(ins)
        self.send_sems, self.recv_sems, self.local_sems = send_sems, recv_sems, local_sems
        self.core_major = core_major

    @staticmethod
    def scratch(n):
        return [pltpu.SemaphoreType.DMA((7, n)), pltpu.SemaphoreType.DMA((7, n)), pltpu.SemaphoreType.DMA((n,))]

    def _slot(self, a, px, py, pc):
        return self.outs[a].at[4 * pc + 2 * px + py if self.core_major else 4 * px + 2 * py + pc]

    def _copy(self, a, k, block, to, src=None):
        return pltpu.make_async_remote_copy(
            src_ref=self._slot(a, *block) if src is None else src, dst_ref=self._slot(a, *block),
            send_sem=self.send_sems.at[k, a], recv_sem=self.recv_sems.at[k, a], device_id=to, device_id_type=MESH)

    def _mine(self, a):
        x, y, c, _ = _place()
        return pltpu.make_async_copy(self.ins[a], self._slot(a, x, y, c), self.local_sems.at[a])

    def _first(self, a):
        x, y, c, chips = _place()
        me = (x, y, c)
        return ([self._copy(a, 0, me, (x, y, 1 - c), src=self.ins[a])]
                + [self._copy(a, 1 + j, me, (*chip, c), src=self.ins[a]) for j, chip in enumerate(chips)])

    def start(self):
        for a in range(self.n):
            self._mine(a).start()
        for a in range(self.n):
            for cp in self._first(a):
                cp.start()

    def finish(self):
        x, y, c, chips = _place()
        me, sibling = (x, y, c), (x, y, 1 - c)
        passed = []
        for j, chip in enumerate(chips):
            for a in range(self.n):
                self._copy(a, 1 + j, (*chip, c), me).wait_recv()
                fwd = self._copy(a, 4 + j, (*chip, c), sibling)
                fwd.start()
                passed.append(fwd)
        for a in range(self.n):
            self._copy(a, 0, (x, y, 1 - c), me).wait_recv()
            for j, chip in enumerate(chips):
                self._copy(a, 4 + j, (*chip, 1 - c), me).wait_recv()
        for a in range(self.n):
            for cp in self._first(a):
                cp.wait_send()
        for cp in passed:
            cp.wait_send()
        for a in range(self.n):
            self._mine(a).wait()


def _half_exchange(arr, name):
    def body(in_ref, out_ref, send_sems, recv_sems, local_sem):
        x, y, c, _ = _place()
        my_chip = 2 * x + y

        def send(j, wait):
            to_me = (c == x) & (y == j // 2) & (c == j % 2)

            @pl.when(to_me)
            def _():
                local = pltpu.make_async_copy(in_ref.at[j], out_ref.at[my_chip], local_sem)
                local.wait() if wait else local.start()

            @pl.when(jnp.logical_not(to_me))
            def _():
                remote = pltpu.make_async_remote_copy(
                    src_ref=in_ref.at[j], dst_ref=out_ref.at[my_chip], send_sem=send_sems.at[j],
                    recv_sem=recv_sems.at[my_chip], device_id=(c, j // 2, j % 2), device_id_type=MESH)
                remote.wait_send() if wait else remote.start()

        for j in range(4):
            send(j, wait=False)
        for j in range(4):
            send(j, wait=True)
        for k in range(4):
            from_me = (k // 2 == x) & (k % 2 == y) & (c == x)

            @pl.when(jnp.logical_not(from_me))
            def _():
                pltpu.make_async_remote_copy(
                    src_ref=in_ref.at[0], dst_ref=out_ref.at[k], send_sem=send_sems.at[0], recv_sem=recv_sems.at[k],
                    device_id=(k // 2, k % 2, x), device_id_type=MESH).wait_recv()

    return pl.pallas_call(
        body, name=name, out_shape=_sds(arr.shape, arr.dtype), in_specs=[ANY], out_specs=ANY,
        scratch_shapes=[pltpu.SemaphoreType.DMA((4,)), pltpu.SemaphoreType.DMA((4,)), pltpu.SemaphoreType.DMA],
    )(arr)


class _ChipExchange:
    def __init__(self, ins, outs, send_sems, recv_sems, local_sems):
        self.ins, self.outs, self.n = ins, outs, len(ins)
        self.send_sems, self.recv_sems, self.local_sems = send_sems, recv_sems, local_sems

    @staticmethod
    def scratch(n):
        return [pltpu.SemaphoreType.DMA((3, n)), pltpu.SemaphoreType.DMA((3, n)), pltpu.SemaphoreType.DMA((n,))]

    def _local(self, a):
        x, y, _, _ = _place()
        me = 2 * x + y
        return pltpu.make_async_copy(self.ins[a].at[me], self.outs[a].at[me], self.local_sems.at[a])

    def _copies(self, a):
        x, y, c, chips = _place()
        me = 2 * x + y
        return [(pltpu.make_async_remote_copy(
                     src_ref=self.ins[a].at[2 * px + py], dst_ref=self.outs[a].at[me],
                     send_sem=self.send_sems.at[k, a], recv_sem=self.recv_sems.at[k, a],
                     device_id=(px, py, c), device_id_type=MESH),
                 pltpu.make_async_remote_copy(
                     src_ref=self.ins[a].at[me], dst_ref=self.outs[a].at[2 * px + py],
                     send_sem=self.send_sems.at[k, a], recv_sem=self.recv_sems.at[k, a],
                     device_id=(px, py, c), device_id_type=MESH))
                for k, (px, py) in enumerate(chips)]

    def start(self):
        for a in range(self.n):
            self._local(a).start()
        for a in range(self.n):
            for send, _ in self._copies(a):
                send.start()

    def finish(self):
        for a in range(self.n):
            for send, recv in self._copies(a):
                send.wait_send()
                recv.wait_recv()
        for a in range(self.n):
            self._local(a).wait()


def _mix_in(x, g_mix, w_in_own, conv_w_own, tm, shards):
    T = x.shape[0]
    n_t = T // tm
    n_s = len(shards)

    def body(x_ref, g_ref, w_own, cw_own, *rest):
        sh_in, rest = rest[:n_s], rest[n_s:]
        u_ref, h_ref, w_hbm, cw_all = rest[:4]
        sh_out, rest = rest[4:4 + n_s], rest[4 + n_s:]
        w_all, keep_sem = rest[:2]
        first = _Gather([w_own, cw_own], [w_all, cw_all], *rest[2:5])
        later = _Gather(sh_in, sh_out, *rest[5:8], core_major=True)
        keep = pltpu.make_async_copy(w_all, w_hbm, keep_sem)
        i = pl.program_id(0)

        @pl.when(i == 0)
        def _():
            first.start()
            later.start()
            first.finish()
            keep.start()

        xv = x_ref[...]
        h = (xv * _rstd(xv) * g_ref[...]).astype(BF16)
        h_ref[...] = h
        u_ref[...] = lax.dot_general(h, w_all[...].reshape(D_IN, D_MODEL), NT, preferred_element_type=F32)

        @pl.when(i == n_t - 1)
        def _():
            later.finish()
            keep.wait()

    own = (D_IN // N_DEV, D_MODEL)
    outs = pl.pallas_call(
        body, name="mix_in", grid=(n_t,),
        in_specs=[pl.BlockSpec((tm, D_MODEL), lambda i: (i, 0)), pl.BlockSpec((1, D_MODEL), lambda i: (0, 0))]
        + [ANY] * (2 + n_s),
        out_specs=[pl.BlockSpec((tm, D_IN), lambda i: (i, 0)), pl.BlockSpec((tm, D_MODEL), lambda i: (i, 0))]
        + [ANY] * (2 + n_s),
        out_shape=[_sds((T, D_IN), F32), _sds((T, D_MODEL), BF16), _sds((N_DEV,) + own, BF16),
                   _sds((N_DEV,) + conv_w_own.shape, F32)] + [_sds((N_DEV,) + a.shape, a.dtype) for a in shards],
        scratch_shapes=[pltpu.VMEM((N_DEV,) + own, BF16), pltpu.SemaphoreType.DMA] + _Gather.scratch(2)
        + _Gather.scratch(n_s),
        compiler_params=pltpu.CompilerParams(dimension_semantics=("arbitrary",)),
    )(x, g_mix, w_in_own, conv_w_own, *shards)
    return outs[0], outs[1], outs[2], outs[3], list(outs[4:])


def _mixer_fwd(u, x, pv, wa, wx, wp, w_out_b, g_ffn, tm, shards=()):
    T = u.shape[0]
    n_s = len(shards)
    n_t = T // tm

    def body(u_ref, x_ref, pv_ref, wa_in, wx_in, wp_in, wo_ref, gf_ref, *rest):
        sh_in, rest = rest[:n_s], rest[n_s:]
        y_ref, hs_ref, hres_ref, h2_ref, saved_ref = rest[:5]
        sh_out, rest = rest[5:5 + n_s], rest[5 + n_s:]
        e_lru, e_pool, a_s, b_s, hc, wa_ref, wx_ref, wp_ref = rest[:8]
        gather = _Gather(sh_in, sh_out, *rest[8:], core_major=True) if n_s else None
        i = pl.program_id(0)

        @pl.when(i == 0)
        def _():
            if gather:
                gather.start()
            e_lru[pl.ds(0, HALO), :] = jnp.zeros((HALO, LRU_W), F32)
            e_pool[pl.ds(0, HALO), :] = jnp.zeros((HALO, POOL_W), F32)
            hc[...] = jnp.zeros((8, LRU_W), F32)
            _fill_block_diag(wa_ref, wa_in)
            _fill_block_diag(wx_ref, wx_in)
            _fill_block_diag(wp_ref, wp_in)

        e_lru[pl.ds(HALO, tm), :] = u_ref[:, 0:LRU_W]
        e_pool[pl.ds(HALO, tm), :] = u_ref[:, 2 * LRU_W:D_IN]
        pv = pv_ref[...]
        xc, r, ig, la = _lru_gates(e_lru, pv, wa_ref, wx_ref, tm)
        for q, val in enumerate((xc, r, ig, la)):
            saved_ref[:, LRU_W * q:LRU_W * (q + 1)] = val
        a, _, mult, _ = _lru_decay(la)
        a_s[...] = a
        b_s[...] = mult * (ig * xc)
        hc[...] = _scan_tile(a_s, b_s, hs_ref, hc[...], tm, reverse=False)
        gl, _ = _gelu_parts(u_ref[:, LRU_W:2 * LRU_W])
        y_lru = hs_ref[...] * gl
        _, zp, _ = _pool_pre(e_pool, pv, wp_ref, tm, i * tm)
        y_pool = zp * pv[ROW_PS:ROW_PS + 1, :]
        yn = jnp.concatenate([y_lru * _rstd(y_lru) * pv[ROW_GL:ROW_GL + 1, :],
                              y_pool * _rstd(y_pool) * pv[ROW_GP:ROW_GP + 1, :]], axis=1).astype(BF16)
        for b in range(N_DEV):
            y_ref[:, 128 * _y_pos(b):128 * (_y_pos(b) + 1)] = yn[:, 128 * b:128 * (b + 1)]
        hr = x_ref[...] + jnp.dot(y_ref[...], wo_ref[...], preferred_element_type=F32)
        hres_ref[...] = hr
        h2_ref[...] = (hr * _rstd(hr) * gf_ref[...]).astype(BF16)
        e_lru[pl.ds(0, HALO), :] = e_lru[pl.ds(tm, HALO), :]
        e_pool[pl.ds(0, HALO), :] = e_pool[pl.ds(tm, HALO), :]

        if gather:
            @pl.when(i == n_t - 1)
            def _():
                gather.finish()

    full = lambda shape: pl.BlockSpec(shape, lambda i: (0,) * len(shape))
    row = lambda w: pl.BlockSpec((tm, w), lambda i: (i, 0))
    outs = pl.pallas_call(
        body, name="mixer_fwd", grid=(n_t,),
        in_specs=[row(D_IN), row(D_MODEL), full((16, LRU_W)), full((8, 64, 64)), full((8, 64, 64)), full((4, 128, 128)),
                  full((D_MODEL, D_MODEL)), full((1, D_MODEL))] + [ANY] * n_s,
        out_specs=[row(D_MODEL), row(LRU_W), row(D_MODEL), row(D_MODEL), row(4 * LRU_W)] + [ANY] * n_s,
        out_shape=[_sds((T, D_MODEL), BF16), _sds((T, LRU_W), F32), _sds((T, D_MODEL), F32), _sds((T, D_MODEL), BF16),
                   _sds((T, 4 * LRU_W), F32)] + [_sds((N_DEV,) + a.shape, a.dtype) for a in shards],
        scratch_shapes=[pltpu.VMEM((HALO + tm, LRU_W), F32), pltpu.VMEM((HALO + tm, POOL_W), F32),
                        pltpu.VMEM((tm, LRU_W), F32), pltpu.VMEM((tm, LRU_W), F32), pltpu.VMEM((8, LRU_W), F32)]
        + [pltpu.VMEM((2, 256, 256), BF16)] * 3 + (_Gather.scratch(n_s) if n_s else []),
        compiler_params=pltpu.CompilerParams(dimension_semantics=("arbitrary",)),
    )(u, x, pv, wa, wx, wp, w_out_b, g_ffn, *shards)
    return outs[0], outs[1], outs[2], outs[3], outs[4], list(outs[5:])


def _ffn_fwd(hres, h2, w1_b, w3_b, w2_b, g_fin, tgt, tm, tn):
    T = hres.shape[0]
    n_j = D_FF // tn

    def body(hres_ref, h2_ref, w1_ref, w3_ref, w2_ref, gfin_ref, tgt_ref,
             g_ref, v_ref, d3_ref, loss_ref, dgfin_ref, acc):
        i, j = pl.program_id(0), pl.program_id(1)

        @pl.when(j == 0)
        def _():
            acc[...] = jnp.zeros((tm, D_MODEL), F32)

        @pl.when((j == 0) & (i == 0))
        def _():
            loss_ref[...] = jnp.zeros((8, 128), F32)
            dgfin_ref[...] = jnp.zeros((1, D_MODEL), F32)

        h2 = h2_ref[...]
        g = lax.dot_general(h2, w1_ref[...], NT, preferred_element_type=F32)
        v = lax.dot_general(h2, w3_ref[...], NT, preferred_element_type=F32)
        g_ref[...] = g.astype(BF16)
        v_ref[...] = v.astype(BF16)
        ff = ((g * _sigmoid(g)) * v).astype(BF16)
        acc[...] += jnp.dot(ff, w2_ref[...], preferred_element_type=F32)

        @pl.when(j == n_j - 1)
        def _():
            h3 = hres_ref[...] + acc[...]
            rstd = _rstd(h3)
            xh = h3 * rstd
            gfin = gfin_ref[...]
            err = xh * gfin - tgt_ref[...]
            loss_ref[...] += 0.5 * jnp.sum(jnp.mean(err * err, axis=-1, keepdims=True))
            dout = err * (1.0 / D_MODEL)
            dx, dgain = _rms_bwd(dout, xh, rstd, gfin)
            d3_ref[...] = dx
            dgfin_ref[...] += dgain

    row = lambda w: pl.BlockSpec((tm, w), lambda i, j: (i, 0))
    const = lambda shape: pl.BlockSpec(shape, lambda i, j: (0,) * len(shape))
    return pl.pallas_call(
        body, name="ffn_fwd", grid=(T // tm, n_j),
        in_specs=[row(D_MODEL), row(D_MODEL),
                  pl.BlockSpec((tn, D_MODEL), lambda i, j: (j, 0)), pl.BlockSpec((tn, D_MODEL), lambda i, j: (j, 0)),
                  pl.BlockSpec((tn, D_MODEL), lambda i, j: (j, 0)), const((1, D_MODEL)), row(D_MODEL)],
        out_specs=[pl.BlockSpec((tm, tn), lambda i, j: (i, j)), pl.BlockSpec((tm, tn), lambda i, j: (i, j)),
                   row(D_MODEL), const((8, 128)), const((1, D_MODEL))],
        out_shape=[_sds((T, D_FF), BF16), _sds((T, D_FF), BF16),
                   _sds((T, D_MODEL), F32), _sds((8, 128), F32), _sds((1, D_MODEL), F32)],
        scratch_shapes=[pltpu.VMEM((tm, D_MODEL), F32)],
        compiler_params=pltpu.CompilerParams(dimension_semantics=("arbitrary", "arbitrary")),
    )(hres, h2, w1_b, w3_b, w2_b, g_fin, tgt)


def _ffn_bwd(d3, g, v, w1_b, w3_b, w2_b, hres, g_ffn, tm, tn):
    T = d3.shape[0]
    n_j = D_FF // tn

    def body(d3_ref, g_ref, v_ref, w1_ref, w3_ref, w2_ref, hres_ref, gf_ref,
             dg_ref, dv_ref, ff_ref, d2_ref, dgffn_ref, acc):
        i, j = pl.program_id(0), pl.program_id(1)

        @pl.when(j == 0)
        def _():
            acc[...] = jnp.zeros((tm, D_MODEL), F32)

        @pl.when((j == 0) & (i == 0))
        def _():
            dgffn_ref[...] = jnp.zeros((1, D_MODEL), F32)

        dff = lax.dot_general(d3_ref[...].astype(BF16), w2_ref[...], NT, preferred_element_type=F32)
        gv = g_ref[...].astype(F32)
        vv = v_ref[...].astype(F32)
        sg = _sigmoid(gv)
        sl = gv * sg
        dgb = (dff * vv * (sg * (1.0 + gv * (1.0 - sg)))).astype(BF16)
        dvb = (dff * sl).astype(BF16)
        dg_ref[...] = dgb
        dv_ref[...] = dvb
        ff_ref[...] = (sl * vv).astype(BF16)
        acc[...] += (jnp.dot(dgb, w1_ref[...], preferred_element_type=F32)
                     + jnp.dot(dvb, w3_ref[...], preferred_element_type=F32))

        @pl.when(j == n_j - 1)
        def _():
            hr = hres_ref[...]
            rstd = _rstd(hr)
            dx, dgain = _rms_bwd(acc[...], hr * rstd, rstd, gf_ref[...])
            d2_ref[...] = d3_ref[...] + dx
            dgffn_ref[...] += dgain

    row = lambda w: pl.BlockSpec((tm, w), lambda i, j: (i, 0))
    tile = pl.BlockSpec((tm, tn), lambda i, j: (i, j))
    const = lambda shape: pl.BlockSpec(shape, lambda i, j: (0,) * len(shape))
    return pl.pallas_call(
        body, name="ffn_bwd", grid=(T // tm, n_j),
        in_specs=[row(D_MODEL), tile, tile,
                  pl.BlockSpec((tn, D_MODEL), lambda i, j: (j, 0)), pl.BlockSpec((tn, D_MODEL), lambda i, j: (j, 0)),
                  pl.BlockSpec((tn, D_MODEL), lambda i, j: (j, 0)), row(D_MODEL), const((1, D_MODEL))],
        out_specs=[tile, tile, tile, row(D_MODEL), const((1, D_MODEL))],
        out_shape=[_sds((T, D_FF), BF16), _sds((T, D_FF), BF16), _sds((T, D_FF), BF16),
                   _sds((T, D_MODEL), F32), _sds((1, D_MODEL), F32)],
        scratch_shapes=[pltpu.VMEM((tm, D_MODEL), F32)],
        compiler_params=pltpu.CompilerParams(dimension_semantics=("arbitrary", "arbitrary")),
    )(d3, g, v, w1_b, w3_b, w2_b, hres, g_ffn)


def _at_b_pair(a, b, c_arr, name, tk, gather=()):
    T, M = a.shape
    N = b.shape[1]
    hm, n_k = M // 2, T // tk
    n_g = len(gather)

    def body(c_ref, a_ref, b_ref, *rest):
        g_in, o_ref, rest = rest[:n_g], rest[n_g], rest[n_g + 1:]
        g_out, rest = rest[:n_g], rest[n_g:]
        acc, landed, send_sem, recv_sem = rest[:4]
        ag = _Gather(g_in, g_out, *rest[4:]) if n_g else None
        ph, k = pl.program_id(0), pl.program_id(1)

        def hand_over():
            x, y, c, _ = _place()
            return pltpu.make_async_remote_copy(
                src_ref=acc.at[0], dst_ref=landed, send_sem=send_sem, recv_sem=recv_sem,
                device_id=(x, y, 1 - c), device_id_type=MESH)

        if ag:
            @pl.when((ph == 0) & (k == 0))
            def _():
                ag.start()

        @pl.when(k == 0)
        def _():
            acc[ph] = jnp.zeros((hm, N), F32)

        acc[ph] += lax.dot_general(a_ref[...].astype(BF16), b_ref[...].astype(BF16), TN, preferred_element_type=F32)

        @pl.when((ph == 0) & (k == n_k - 1))
        def _():
            hand_over().start()

        @pl.when((ph == 1) & (k == n_k - 1))
        def _():
            copy = hand_over()
            copy.wait_recv()
            o_ref[...] = (acc[1] + landed[...]).astype(BF16)
            copy.wait_send()
            if ag:
                ag.finish()

    outs = pl.pallas_call(
        body, name=name,
        grid_spec=pltpu.PrefetchScalarGridSpec(
            num_scalar_prefetch=1, grid=(2, n_k),
            in_specs=[pl.BlockSpec((tk, hm), lambda ph, k, c_ref: (k, (ph + 1 - c_ref[0]) % 2)),
                      pl.BlockSpec((tk, N), lambda ph, k, c_ref: (k, 0))] + [ANY] * n_g,
            out_specs=[pl.BlockSpec((hm, N), lambda ph, k, c_ref: (0, 0))] + [ANY] * n_g,
            scratch_shapes=[pltpu.VMEM((2, hm, N), F32), pltpu.VMEM((hm, N), F32),
                            pltpu.SemaphoreType.DMA, pltpu.SemaphoreType.DMA] + (_Gather.scratch(n_g) if n_g else [])),
        out_shape=[_sds((hm, N), BF16)] + [_sds((N_DEV,) + g.shape, g.dtype) for g in gather],
        compiler_params=pltpu.CompilerParams(dimension_semantics=("arbitrary", "arbitrary")),
    )(c_arr, a, b, *gather)
    return (outs[0], list(outs[1:])) if n_g else outs[0]


def _mixer_bwd(d2, u, hs, saved, pv, wa, wx, wp, w_out_b, tm, chip_sums=()):
    T = u.shape[0]
    n_t = T // tm
    n_x = len(chip_sums)

    def body(d2_ref, u_ref, uh_ref, hs_ref, hh_ref, saved_ref, pv_ref, wa_in, wx_in, wp_in, wo_ref, *rest):
        x_in, rest = rest[:n_x], rest[n_x:]
        du_ref, sg_ref = rest[:2]
        x_out, rest = rest[2:2 + n_x], rest[2 + n_x:]
        e_pool, e_h, a_s, b_s, dh_s, mu_s, f_x, f_p, mc, cx, cp = rest[:11]
        wa_ref, wx_ref, wp_ref, vacc_ref, dwa_ref, dwx_ref, dwp_ref = rest[11:18]
        exchange = _ChipExchange(x_in, x_out, *rest[18:]) if n_x else None
        s = pl.program_id(0)
        it = n_t - 1 - s

        @pl.when(s == 0)
        def _():
            if exchange:
                exchange.start()
            mc[...] = jnp.zeros((8, LRU_W), F32)
            cx[...] = jnp.zeros((8, LRU_W), F32)
            cp[...] = jnp.zeros((HALO, POOL_W), F32)
            vacc_ref[...] = jnp.zeros((16, LRU_W), F32)
            dwa_ref[...] = jnp.zeros((2, 256, 256), F32)
            dwx_ref[...] = jnp.zeros((2, 256, 256), F32)
            dwp_ref[...] = jnp.zeros((2, 256, 256), F32)
            _fill_block_diag(wa_ref, wa_in)
            _fill_block_diag(wx_ref, wx_in)
            _fill_block_diag(wp_ref, wp_in)

        first = it == 0
        e_pool[pl.ds(0, HALO), :] = jnp.where(first, 0.0, uh_ref[...])
        e_pool[pl.ds(HALO, tm), :] = u_ref[:, 2 * LRU_W:D_IN]
        e_h[pl.ds(0, 8), :] = jnp.where(first, 0.0, hh_ref[...])
        e_h[pl.ds(8, tm), :] = hs_ref[...]
        pv = pv_ref[...]
        saved = lambda q: saved_ref[:, LRU_W * q:LRU_W * (q + 1)]

        dyn = lax.dot_general(d2_ref[...].astype(BF16), wo_ref[...], NT, preferred_element_type=F32)
        dyn = jnp.concatenate([dyn[:, 128 * _y_pos(b):128 * (_y_pos(b) + 1)] for b in range(N_DEV)], axis=1)

        h = hs_ref[...]
        ug = u_ref[:, LRU_W:2 * LRU_W]
        gl, dgl = _gelu_parts(ug)
        y_lru = h * gl
        rstd_l = _rstd(y_lru)
        dy_lru, d_gain_l = _rms_bwd(dyn[:, 0:LRU_W], y_lru * rstd_l, rstd_l, pv[ROW_GL:ROW_GL + 1, :])
        dh = dy_lru * gl
        du_ref[:, LRU_W:2 * LRU_W] = (dy_lru * h * dgl).astype(BF16)
        a_s[...] = jnp.exp(saved(3))
        b_s[...] = a_s[...] * dh
        dh_s[...] = dh
        mu_s[pl.ds(tm, 8), :] = mc[...]
        mc[...] = _scan_tile(a_s, b_s, mu_s, mc[...], tm, reverse=True)
        xc, r, ig = saved(0), saved(1), saved(2)
        a, om, mult, rmult = _lru_decay(saved(3))
        lam_t = dh_s[...] + mu_s[pl.ds(1, tm), :]
        da = lam_t * e_h[pl.ds(7, tm), :]
        dmult = lam_t * (ig * xc)
        di = lam_t * (mult * xc)
        dxc = lam_t * (mult * ig)
        dla = da * a - jnp.where(om > 1e-12, dmult * ((a * a) * rmult), 0.0)
        dra = (dla * (-LRU_C * _softplus_neg_lambda(pv))) * (r * (1.0 - r))
        dia = di * (ig * (1.0 - ig))
        drab = dra.astype(BF16)
        diab = dia.astype(BF16)
        xcb = xc.astype(BF16)
        dxc = dxc + _bd_t(drab, wa_ref) + _bd_t(diab, wx_ref)
        dwa_ref[...] += _bd_grad(xcb, drab)
        dwx_ref[...] += _bd_grad(xcb, diab)
        sig_neg_lam = _sigmoid(-pv[ROW_LAM:ROW_LAM + 1, :])
        d_lam = jnp.sum(dla * r, axis=0, keepdims=True) * (LRU_C * sig_neg_lam)

        f_x[pl.ds(0, tm), :] = dxc
        f_x[pl.ds(tm, 8), :] = cx[...]
        du_lru = jnp.zeros((tm, LRU_W), F32)
        u_lru = u_ref[:, 0:LRU_W]
        d_cw = []
        for k in range(4):
            later = f_x[pl.ds(3 - k, tm), :]
            du_lru = du_lru + later * pv[ROW_CW + k:ROW_CW + k + 1, :]
            d_cw.append(jnp.sum(later * u_lru, axis=0, keepdims=True))
        du_ref[:, 0:LRU_W] = du_lru.astype(BF16)
        cx[...] = f_x[pl.ds(0, 8), :]

        pooled_b, zp, inv_cnts = _pool_pre(e_pool, pv, wp_ref, tm, it * tm)
        ps = pv[ROW_PS:ROW_PS + 1, :]
        y_pool = zp * ps
        rstd_p = _rstd(y_pool)
        dy_pool, d_gain_p = _rms_bwd(dyn[:, LRU_W:D_MODEL], y_pool * rstd_p, rstd_p, pv[ROW_GP:ROW_GP + 1, :])
        dz = dy_pool * ps
        dzb = dz.astype(BF16)
        dwp_ref[...] += _bd_grad(pooled_b, dzb)
        dpooled = _bd_t(dzb, wp_ref)
        for g, w in enumerate(POOL_WINDOWS):
            f_p[pl.ds(0, tm), pl.ds(128 * g, 128)] = dpooled[:, 128 * g:128 * (g + 1)] * inv_cnts[g]
        f_p[pl.ds(tm, HALO), :] = cp[...]
        for g, w in enumerate(POOL_WINDOWS):
            acc = _window_sum(f_p[:, pl.ds(128 * g, 128)], w, back=False)[0:tm, :]
            du_ref[:, 2 * LRU_W + 128 * g:2 * LRU_W + 128 * (g + 1)] = (
                acc - dpooled[:, 128 * g:128 * (g + 1)]).astype(BF16)
        cp[...] = f_p[pl.ds(0, HALO), :]

        rows = d_cw + [
            jnp.sum(dxc, axis=0, keepdims=True),
            jnp.sum(dra, axis=0, keepdims=True),
            jnp.sum(dia, axis=0, keepdims=True),
            d_lam,
            jnp.sum(dz, axis=0, keepdims=True),
            jnp.sum(dy_pool * zp, axis=0, keepdims=True),
            d_gain_l, d_gain_p,
            jnp.zeros((4, LRU_W), F32),
        ]
        vacc_ref[...] += jnp.concatenate(rows, axis=0)

        @pl.when(s == n_t - 1)
        def _():
            sg_ref[SG_VEC:SG_VEC + 16, :] = vacc_ref[:, 0:256]
            sg_ref[SG_VEC + 16:SG_VEC + 32, :] = vacc_ref[:, 256:512]
            for half in range(2):
                sg_ref[SG_WA + 64 * half:SG_WA + 64 * (half + 1), :] = _diag_pack(dwa_ref[half], 64)
                sg_ref[SG_WX + 64 * half:SG_WX + 64 * (half + 1), :] = _diag_pack(dwx_ref[half], 64)
                sg_ref[SG_WP + 128 * half:SG_WP + 128 * (half + 1), :] = _diag_pack(dwp_ref[half], 128)
            if exchange:
                exchange.finish()

    rev = lambda w: pl.BlockSpec((tm, w), lambda s: (n_t - 1 - s, 0))
    full = lambda shape: pl.BlockSpec(shape, lambda s: (0,) * len(shape))
    outs = pl.pallas_call(
        body, name="mixer_bwd", grid=(n_t,),
        in_specs=[rev(D_MODEL), rev(D_IN),
                  pl.BlockSpec((HALO, POOL_W), lambda s: (jnp.maximum((n_t - 1 - s) * (tm // HALO) - 1, 0), 2)),
                  rev(LRU_W),
                  pl.BlockSpec((8, LRU_W), lambda s: (jnp.maximum((n_t - 1 - s) * (tm // 8) - 1, 0), 0)),
                  rev(4 * LRU_W), full((16, LRU_W)), full((8, 64, 64)), full((8, 64, 64)), full((4, 128, 128)),
                  full((D_MODEL, D_MODEL))] + [ANY] * n_x,
        out_specs=[rev(D_IN), full((SG_ROWS, 256))] + [ANY] * n_x,
        out_shape=[_sds((T, D_IN), BF16), _sds((SG_ROWS, 256), F32)] + [_sds(a.shape, a.dtype) for a in chip_sums],
        scratch_shapes=[pltpu.VMEM((HALO + tm, POOL_W), F32),
                        pltpu.VMEM((8 + tm, LRU_W), F32)] + [pltpu.VMEM((tm, LRU_W), F32)] * 3 + [
                        pltpu.VMEM((tm + 8, LRU_W), F32), pltpu.VMEM((tm + 8, LRU_W), F32),
                        pltpu.VMEM((tm + HALO, POOL_W), F32), pltpu.VMEM((8, LRU_W), F32),
                        pltpu.VMEM((8, LRU_W), F32), pltpu.VMEM((HALO, POOL_W), F32)]
        + [pltpu.VMEM((2, 256, 256), BF16)] * 3 + [pltpu.VMEM((16, LRU_W), F32)] + [pltpu.VMEM((2, 256, 256), F32)] * 3
        + (_ChipExchange.scratch(n_x) if n_x else []),
        compiler_params=pltpu.CompilerParams(dimension_semantics=("arbitrary",)),
    )(d2, u, u, hs, hs, saved, pv, wa, wx, wp, w_out_b, *chip_sums)
    return outs[0], outs[1], list(outs[2:])


def _mix_in_bwd(du, x, d2, w_in_t, g_mix, tm):
    T = x.shape[0]

    def body(du_ref, x_ref, d2_ref, w_ref, g_ref, dx_ref, dg_ref):
        @pl.when(pl.program_id(0) == 0)
        def _():
            dg_ref[...] = jnp.zeros((1, D_MODEL), F32)

        dh = jnp.dot(du_ref[...], w_ref[...], preferred_element_type=F32)
        xv = x_ref[...]
        rstd = _rstd(xv)
        dx, dgain = _rms_bwd(dh, xv * rstd, rstd, g_ref[...])
        dx_ref[...] = d2_ref[...] + dx
        dg_ref[...] += dgain

    row = lambda w: pl.BlockSpec((tm, w), lambda i: (i, 0))
    const = lambda shape: pl.BlockSpec(shape, lambda i: (0,) * len(shape))
    return pl.pallas_call(
        body, name="mix_in_bwd", grid=(T // tm,),
        in_specs=[row(D_IN), row(D_MODEL), row(D_MODEL), const((D_IN, D_MODEL)), const((1, D_MODEL))],
        out_specs=[row(D_MODEL), const((1, D_MODEL))],
        out_shape=[_sds((T, D_MODEL), F32), _sds((1, D_MODEL), F32)],
        compiler_params=pltpu.CompilerParams(dimension_semantics=("arbitrary",)),
    )(du, x, d2, w_in_t, g_mix)


def _adamw(w, g, m, v):
    m = ADAM_B1 * m + (1.0 - ADAM_B1) * g
    v = ADAM_B2 * v + (1.0 - ADAM_B2) * (g * g)
    m_hat = m / (1.0 - ADAM_B1 ** ADAM_STEP)
    v_hat = v / (1.0 - ADAM_B2 ** ADAM_STEP)
    delta = -ADAM_LR * (m_hat / (jnp.sqrt(v_hat) + ADAM_EPS) + ADAM_WD * w)
    return delta, m, v


def _adam_shards(ws, ms, vs, parts):
    n = len(ws)
    n_blk = [w.shape[0] // ADAM_ROWS for w in ws]

    def body(*refs):
        w_refs, m_refs, v_refs, p_refs, outs = (refs[:n], refs[n:2 * n], refs[2 * n:3 * n], refs[3 * n:4 * n],
                                                refs[4 * n:])
        i = pl.program_id(0)
        for a in range(n):
            @pl.when(i < n_blk[a])
            def _(a=a):
                g = p_refs[a][0].astype(F32)
                for j in range(1, 4):
                    g = g + p_refs[a][j].astype(F32)
                delta, new_m, new_v = _adamw(w_refs[a][...], g, m_refs[a][...], v_refs[a][...])
                for kind, val in enumerate((g, delta, new_m, new_v)):
                    outs[4 * a + kind][...] = val

    blk = lambda a: pl.BlockSpec((ADAM_ROWS, D_MODEL), lambda i: (jnp.minimum(i, n_blk[a] - 1), 0))
    part_blk = lambda a: pl.BlockSpec((4, ADAM_ROWS, D_MODEL), lambda i: (0, jnp.minimum(i, n_blk[a] - 1), 0))
    res = pl.pallas_call(
        body, name="adam_shards", grid=(max(n_blk),),
        in_specs=[blk(a) for a in range(n)] * 3 + [part_blk(a) for a in range(n)],
        out_specs=[blk(a) for a in range(n) for _ in range(4)],
        out_shape=[_sds(w.shape, F32) for w in ws for _ in range(4)],
        compiler_params=pltpu.CompilerParams(dimension_semantics=("arbitrary",)),
    )(*ws, *ms, *vs, *parts)
    return [tuple(res[4 * a:4 * a + 4]) for a in range(n)]


SMALL_PARAMS = [("norm_mix_g", (1, D_MODEL)), ("conv_w", (1, 4, 64)), ("conv_b", (1, LRU_W)),
                ("gate_a_w", (1, 8, 64, 64)), ("gate_a_b", (1, LRU_W)), ("gate_x_w", (1, 8, 64, 64)),
                ("gate_x_b", (1, LRU_W)), ("lru_lambda", (1, LRU_W)), ("pool_w", (1, 4, 128, 128)),
                ("pool_b", (1, POOL_W)), ("pool_scale", (1, POOL_W)), ("norm_lru_g", (1, LRU_W)),
                ("norm_pool_g", (1, POOL_W)), ("norm_ffn_g", (1, D_MODEL)), ("final_norm_g", (1, D_MODEL))]
VEC_ROW = dict(conv_b=ROW_CB, gate_a_b=ROW_BA, gate_x_b=ROW_BX, lru_lambda=ROW_LAM, pool_b=ROW_PB, pool_scale=ROW_PS,
               norm_lru_g=ROW_GL, norm_pool_g=ROW_GP)
WHOLE = (Ellipsis,)


def _unpack_mixer_grads(sg, dev):
    vec = jnp.concatenate([sg[SG_VEC:SG_VEC + 16], sg[SG_VEC + 16:SG_VEC + 32]], axis=1)
    out = {nm: [(WHOLE, vec[r:r + 1])] for nm, r in VEC_ROW.items()}
    own = jnp.zeros((4, 64), F32)
    for d in range(N_DEV):
        own = jnp.where(dev == d, vec[ROW_CW:ROW_CW + 4, 64 * d:64 * (d + 1)], own)
    out["conv_w"] = [((0,), own)]
    for nm, row0 in (("gate_a_w", SG_WA), ("gate_x_w", SG_WX)):
        out[nm] = [((0, b), sg[row0 + 64 * (b // 4):row0 + 64 * (b // 4 + 1), 64 * (b % 4):64 * (b % 4 + 1)])
                   for b in range(8)]
    out["pool_w"] = [((0, b), sg[SG_WP + 128 * (b // 2):SG_WP + 128 * (b // 2 + 1), 128 * (b % 2):128 * (b % 2 + 1)])
                     for b in range(4)]
    return out


def _adam_small(parts, w, m, v):
    names = [nm for nm, _ in SMALL_PARAMS]
    n = len(names)

    def body(sg_ref, gm_ref, gf_ref, gn_ref, ls_ref, *rest):
        w_refs, m_refs, v_refs, outs = rest[:n], rest[n:2 * n], rest[2 * n:3 * n], rest[3 * n:]
        dev = 4 * lax.axis_index("x") + 2 * lax.axis_index("y") + lax.axis_index("c")

        def total(ref):
            acc = ref[0]
            for d in range(1, N_DEV):
                acc = acc + ref[d]
            return acc

        pieces = _unpack_mixer_grads(total(sg_ref), dev)
        pieces["norm_mix_g"] = [(WHOLE, total(gm_ref))]
        pieces["norm_ffn_g"] = [(WHOLE, total(gf_ref))]
        pieces["final_norm_g"] = [(WHOLE, total(gn_ref))]
        for i, nm in enumerate(names):
            for idx, g in pieces[nm]:
                delta, new_m, new_v = _adamw(w_refs[i][idx], g, m_refs[i][idx], v_refs[i][idx])
                for kind, val in enumerate((g, delta, new_m, new_v)):
                    outs[4 * i + kind][idx] = val
        outs[4 * n][...] = total(ls_ref)

    shapes = [_sds(shape, F32) for _, shape in SMALL_PARAMS for _ in range(4)] + [_sds((8, 128), F32)]
    res = pl.pallas_call(body, name="adam_small", out_shape=shapes)(
        *parts, *[w[nm] for nm in names], *[m[nm] for nm in names], *[v[nm] for nm in names])
    return {nm: tuple(res[4 * i:4 * i + 4]) for i, nm in enumerate(names)}, res[4 * n][0, 0]


def _vec_rows(conv_w_full, conv_b, ba, bx, lam, pb, ps, gl, gp):
    return jnp.concatenate([conv_w_full, conv_b, ba, bx, lam, pb, ps, gl, gp, jnp.zeros((4, LRU_W), F32)], axis=0)


WEIGHT_ORDER = ['norm_mix_g', 'w_in', 'conv_w', 'conv_b', 'gate_a_w', 'gate_a_b', 'gate_x_w', 'gate_x_b', 'lru_lambda',
                'pool_w', 'pool_b', 'pool_scale', 'norm_lru_g', 'norm_pool_g', 'w_out', 'norm_ffn_g', 'ffn_w1', 'ffn_w3',
                'ffn_w2', 'final_norm_g']


def kernel(x, norm_mix_g, w_in, conv_w, conv_b, gate_a_w, gate_a_b, gate_x_w, gate_x_b, lru_lambda, pool_w, pool_b, pool_scale, norm_lru_g, norm_pool_g, w_out, norm_ffn_g, ffn_w1, ffn_w3, ffn_w2, final_norm_g, loss_target, m_norm_mix_g, m_w_in, m_conv_w, m_conv_b, m_gate_a_w, m_gate_a_b, m_gate_x_w, m_gate_x_b, m_lru_lambda, m_pool_w, m_pool_b, m_pool_scale, m_norm_lru_g, m_norm_pool_g, m_w_out, m_norm_ffn_g, m_ffn_w1, m_ffn_w3, m_ffn_w2, m_final_norm_g, v_norm_mix_g, v_w_in, v_conv_w, v_conv_b, v_gate_a_w, v_gate_a_b, v_gate_x_w, v_gate_x_b, v_lru_lambda, v_pool_w, v_pool_b, v_pool_scale, v_norm_lru_g, v_norm_pool_g, v_w_out, v_norm_ffn_g, v_ffn_w1, v_ffn_w3, v_ffn_w2, v_final_norm_g):
    ac = lax.axis_index("c")
    tm, tmx, tn, tk = 512, 512, 1408, 1024
    tm_in = 1024
    xs, tgt = x[0], loss_target[0]
    g_fin = final_norm_g.reshape(1, D_MODEL)
    c_arr = jnp.reshape(ac, (1,)).astype(jnp.int32)

    tr = lambda w: jnp.swapaxes(w[0], 0, 1)
    own = lambda w: w[0]
    bf = lambda a: a.astype(BF16)

    u, h1, g_in, g_conv, (g_out,) = _mix_in(xs, norm_mix_g, bf(tr(w_in)), conv_w[0], tm, shards=[bf(own(w_out))])
    w_in_t = g_in.reshape(D_IN, D_MODEL)
    conv_w_full = g_conv.transpose(1, 0, 2).reshape(4, LRU_W)
    pv = _vec_rows(conv_w_full, conv_b, gate_a_b, gate_x_b, lru_lambda, pool_b, pool_scale, norm_lru_g, norm_pool_g)
    wa, wx, wp = gate_a_w[0], gate_x_w[0], pool_w[0]
    w_out_b = g_out.reshape(D_MODEL, D_MODEL)
    y, hs, hres, h2, saved, (g_w1, g_w3, g_w2) = _mixer_fwd(
        u, xs, pv, wa, wx, wp, w_out_b, norm_ffn_g, tmx, shards=[bf(tr(ffn_w1)), bf(tr(ffn_w3)), bf(own(ffn_w2))])
    w1_t, w3_t, w2_b = g_w1.reshape(D_FF, D_MODEL), g_w3.reshape(D_FF, D_MODEL), g_w2.reshape(D_FF, D_MODEL)
    g, v, d3, loss_acc, d_gfin = _ffn_fwd(hres, h2, w1_t, w3_t, w2_b, g_fin, tgt, tm, tn)

    dg, dv, ff, d2, d_gffn = _ffn_bwd(d3, g, v, w1_t, w3_t, w2_b, hres, norm_ffn_g, tm, tn)
    chips = lambda a: a.reshape(4, a.shape[0] // 4, a.shape[1])
    early_sums = [chips(_at_b_pair(y, d2, c_arr, "grad_w_out", 2 * tk)), chips(_at_b_pair(dg, h2, c_arr, "grad_w1", tk)),
                  chips(_at_b_pair(dv, h2, c_arr, "grad_w3", tk)), chips(_at_b_pair(ff, d3, c_arr, "grad_w2", tk))]
    du, d_mixer, early_parts = _mixer_bwd(d2, u, hs, saved, pv, wa, wx, wp, w_out_b, tmx, chip_sums=early_sums)
    grad_x, d_gmix = _mix_in_bwd(du, xs, d2, w_in_t, norm_mix_g, tm_in)
    d_win, small_parts = _at_b_pair(du, h1, c_arr, "grad_w_in", 2 * tk,
                                    gather=[d_mixer, d_gmix, d_gffn, d_gfin, loss_acc])
    parts = [_half_exchange(chips(d_win), "grads_to_chips_w_in")] + list(early_parts)

    res = {}
    shard_w = dict(w_in=(w_in, m_w_in, v_w_in, tr), w_out=(w_out, m_w_out, v_w_out, own),
                   ffn_w1=(ffn_w1, m_ffn_w1, v_ffn_w1, tr), ffn_w3=(ffn_w3, m_ffn_w3, v_ffn_w3, tr),
                   ffn_w2=(ffn_w2, m_ffn_w2, v_ffn_w2, own))
    shard_res = _adam_shards([view(w) for w, _, _, view in shard_w.values()],
                             [view(m) for _, m, _, view in shard_w.values()],
                             [view(v) for _, _, v, view in shard_w.values()], parts)
    for (nm, (_, _, _, view)), outs in zip(shard_w.items(), shard_res):
        res[nm] = [(jnp.swapaxes(o, 0, 1) if view is tr else o)[None] for o in outs]

    row = lambda a: a.reshape(1, D_MODEL)
    small = lambda gm, cw, cb, wa_, ba, wx_, bx, lam, pw, pb, ps, gl, gp, gf, gn: dict(
        norm_mix_g=gm, conv_w=cw, conv_b=cb, gate_a_w=wa_, gate_a_b=ba, gate_x_w=wx_, gate_x_b=bx, lru_lambda=lam,
        pool_w=pw, pool_b=pb, pool_scale=ps, norm_lru_g=gl, norm_pool_g=gp, norm_ffn_g=gf, final_norm_g=row(gn))
    small_res, loss = _adam_small(
        small_parts,
        small(norm_mix_g, conv_w, conv_b, gate_a_w, gate_a_b, gate_x_w, gate_x_b, lru_lambda, pool_w, pool_b,
              pool_scale, norm_lru_g, norm_pool_g, norm_ffn_g, final_norm_g),
        small(m_norm_mix_g, m_conv_w, m_conv_b, m_gate_a_w, m_gate_a_b, m_gate_x_w, m_gate_x_b, m_lru_lambda, m_pool_w,
              m_pool_b, m_pool_scale, m_norm_lru_g, m_norm_pool_g, m_norm_ffn_g, m_final_norm_g),
        small(v_norm_mix_g, v_conv_w, v_conv_b, v_gate_a_w, v_gate_a_b, v_gate_x_w, v_gate_x_b, v_lru_lambda, v_pool_w,
              v_pool_b, v_pool_scale, v_norm_lru_g, v_norm_pool_g, v_norm_ffn_g, v_final_norm_g))
    for nm, outs in small_res.items():
        res[nm] = [o.reshape(D_MODEL) for o in outs] if nm == "final_norm_g" else list(outs)

    out = [loss, grad_x[None]]
    for kind in range(4):
        out += [res[nm][kind] for nm in WEIGHT_ORDER]
    return tuple(out)
```

```python
import jax
import jax.numpy as jnp
from jax import lax
from jax.experimental import pallas as pl
from jax.experimental.pallas import tpu as pltpu

F32 = jnp.float32
BF16 = jnp.bfloat16

D_MODEL = 1024
LRU_W = 512
POOL_W = 512
D_IN = 1536
D_FF = 2816
POOL_WINDOWS = (2, 4, 8, 16)
EPS = 1e-6
LRU_C = 8.0
N_DEV = 8
HALO = 16
SCAN_UNROLL = 8
ADAM_ROWS = 32

ADAM_LR = 0.001
ADAM_B1 = 0.9
ADAM_B2 = 0.999
ADAM_EPS = 1e-08
ADAM_WD = 0.01
ADAM_STEP = 10

ROW_CW, ROW_CB, ROW_BA, ROW_BX, ROW_LAM, ROW_PB, ROW_PS, ROW_GL, ROW_GP = 0, 4, 5, 6, 7, 8, 9, 10, 11
SG_VEC, SG_WA, SG_WX, SG_WP, SG_ROWS = 0, 32, 160, 288, 544

NT = (((1,), (1,)), ((), ()))
TN = (((0,), (0,)), ((), ()))


def _sds(shape, dtype):
    return jax.ShapeDtypeStruct(shape, dtype)


def _sigmoid(x):
    return 0.5 * jnp.tanh(0.5 * x) + 0.5


def _gelu_parts(x):
    c = 0.7978845608028654
    inner = c * (x + 0.044715 * (x * x * x))
    th = jnp.tanh(inner)
    g = 0.5 * x * (1.0 + th)
    dg = 0.5 * (1.0 + th) + 0.5 * x * (1.0 - th * th) * (c * (1.0 + 3.0 * 0.044715 * (x * x)))
    return g, dg


def _window_sum(ext, w, back):
    n = ext.shape[0]
    s, k = ext, 1
    while k < w:
        s = s + pltpu.roll(s, k if back else n - k, 0)
        k *= 2
    return s


def _rstd(x):
    return lax.rsqrt(jnp.mean(x * x, axis=-1, keepdims=True) + EPS)


def _rms_bwd(dy, xhat, rstd, gain):
    dxh = dy * gain
    dx = rstd * (dxh - xhat * jnp.mean(dxh * xhat, axis=-1, keepdims=True))
    return dx, jnp.sum(dy * xhat, axis=0, keepdims=True)


def _bd(xb, w_ref):
    return jnp.concatenate(
        [jnp.dot(xb[:, :256], w_ref[0], preferred_element_type=F32),
         jnp.dot(xb[:, 256:], w_ref[1], preferred_element_type=F32)], axis=1)


def _bd_t(xb, w_ref):
    return jnp.concatenate(
        [lax.dot_general(xb[:, :256], w_ref[0], NT, preferred_element_type=F32),
         lax.dot_general(xb[:, 256:], w_ref[1], NT, preferred_element_type=F32)], axis=1)


def _bd_grad(xb, db):
    return jnp.stack(
        [lax.dot_general(xb[:, :256], db[:, :256], TN, preferred_element_type=F32),
         lax.dot_general(xb[:, 256:], db[:, 256:], TN, preferred_element_type=F32)], axis=0)


def _fill_block_diag(dst, src_ref):
    n, k, _ = src_ref.shape
    dst[...] = jnp.zeros(dst.shape, BF16)
    for b in range(n):
        p, q = divmod(b, 256 // k)
        dst[p, q * k:(q + 1) * k, q * k:(q + 1) * k] = src_ref[b].astype(BF16)


def _diag_pack(w, k):
    lane = lax.broadcasted_iota(jnp.int32, (k, 256), 1)
    out = w[0:k]
    for q in range(1, 256 // k):
        out = jnp.where(lane >= q * k, w[q * k:(q + 1) * k], out)
    return out


def _y_pos(b):
    return 4 * (b % 2) + b // 2


def _softplus_neg_lambda(pv):
    z = -pv[ROW_LAM:ROW_LAM + 1, :]
    return jnp.maximum(z, 0.0) + jnp.log(1.0 + jnp.exp(-jnp.abs(z)))


def _lru_gates(e_lru, pv, wa_ref, wx_ref, tm):
    xc = pv[ROW_CB:ROW_CB + 1, :]
    for k in range(4):
        xc = xc + e_lru[pl.ds(HALO - 3 + k, tm), :] * pv[ROW_CW + k:ROW_CW + k + 1, :]
    xcb = xc.astype(BF16)
    r = _sigmoid(_bd(xcb, wa_ref) + pv[ROW_BA:ROW_BA + 1, :])
    ig = _sigmoid(_bd(xcb, wx_ref) + pv[ROW_BX:ROW_BX + 1, :])
    return xc, r, ig, (-LRU_C * r) * _softplus_neg_lambda(pv)


def _lru_decay(la):
    a = jnp.exp(la)
    om = -jnp.tanh(la) * (1.0 + a * a)
    omc = jnp.maximum(om, 1e-12)
    rmult = lax.rsqrt(omc)
    return a, om, omc * rmult, rmult


def _over_count(v, w, inv_head):
    return jnp.concatenate([v[0:HALO] * inv_head, v[HALO:] * (1.0 / w)], axis=0)


def _pool_pre(e_pool, pv, wp_ref, tm, t0):
    t_head = t0 + lax.broadcasted_iota(jnp.int32, (HALO, 1), 0)
    parts, inv_heads = [], []
    for g, w in enumerate(POOL_WINDOWS):
        ext = e_pool[:, pl.ds(128 * g, 128)]
        s = _window_sum(ext, w, back=True)[HALO:, :]
        inv_head = 1.0 / jnp.minimum(t_head + 1, w).astype(F32)
        inv_heads.append(inv_head)
        parts.append(_over_count(s, w, inv_head) - ext[HALO:, :])
    pooled = jnp.concatenate(parts, axis=1)
    pooled_b = pooled.astype(BF16)
    zp = _bd(pooled_b, wp_ref) + pv[ROW_PB:ROW_PB + 1, :]
    return pooled_b, zp, inv_heads


def _scan_tile(a_ref, b_ref, out_ref, carry, tm, reverse):
    row = lax.broadcasted_iota(jnp.int32, (8, LRU_W), 0)
    nblk = tm // 8

    def local_scan(blk):
        r0 = pl.multiple_of(blk * 8, 8)
        av = a_ref[pl.ds(r0, 8), :]
        bv = b_ref[pl.ds(r0, 8), :]
        for d in (1, 2, 4):
            sh = (8 - d) if reverse else d
            a_s = pltpu.roll(av, sh, 0)
            b_s = pltpu.roll(bv, sh, 0)
            m = (row < 8 - d) if reverse else (row >= d)
            bv = jnp.where(m, av * b_s + bv, bv)
            av = jnp.where(m, av * a_s, av)
        return r0, av, bv

    def step(i, hin):
        local = [local_scan((nblk - 1 - (i * SCAN_UNROLL + j)) if reverse else (i * SCAN_UNROLL + j))
                 for j in range(SCAN_UNROLL)]
        for r0, av, bv in local:
            hv = av * hin + bv
            out_ref[pl.ds(r0, 8), :] = hv
            hin = jnp.broadcast_to(hv[0:1, :] if reverse else hv[7:8, :], (8, LRU_W))
        return hin

    return lax.fori_loop(0, nblk // SCAN_UNROLL, step, carry)


MESH = pl.DeviceIdType.MESH
ANY = pl.BlockSpec(memory_space=pl.ANY)


def _place():
    x, y, c = lax.axis_index("x"), lax.axis_index("y"), lax.axis_index("c")
    chips = [(1 - x, y), (x, 1 - y), (1 - x, 1 - y)]
    return x, y, c, chips


class _Gather:
    def __init__(self, ins, outs, send_sems, recv_sems, local_sems, core_major=False):
        self.ins, self.outs, self.n = ins, outs, len(ins)
        self.send_sems, self.recv_sems, self.local_sems = send_sems, recv_sems, local_sems
        self.core_major = core_major

    @staticmethod
    def scratch(n):
        return [pltpu.SemaphoreType.DMA((7, n)), pltpu.SemaphoreType.DMA((7, n)), pltpu.SemaphoreType.DMA((n,))]

    def _slot(self, a, px, py, pc):
        return self.outs[a].at[4 * pc + 2 * px + py if self.core_major else 4 * px + 2 * py + pc]

    def _copy(self, a, k, block, to, src=None):
        return pltpu.make_async_remote_copy(
            src_ref=self._slot(a, *block) if src is None else src, dst_ref=self._slot(a, *block),
            send_sem=self.send_sems.at[k, a], recv_sem=self.recv_sems.at[k, a], device_id=to, device_id_type=MESH)

    def _mine(self, a):
        x, y, c, _ = _place()
        return pltpu.make_async_copy(self.ins[a], self._slot(a, x, y, c), self.local_sems.at[a])

    def _first(self, a):
        x, y, c, chips = _place()
        me = (x, y, c)
        return ([self._copy(a, 0, me, (x, y, 1 - c), src=self.ins[a])]
                + [self._copy(a, 1 + j, me, (*chip, c), src=self.ins[a]) for j, chip in enumerate(chips)])

    def start(self):
        for a in range(self.n):
            self._mine(a).start()
        for a in range(self.n):
            for cp in self._first(a):
                cp.start()

    def finish(self):
        x, y, c, chips = _place()
        me, sibling = (x, y, c), (x, y, 1 - c)
        passed = []
        for j, chip in enumerate(chips):
            for a in range(self.n):
                self._copy(a, 1 + j, (*chip, c), me).wait_recv()
                fwd = self._copy(a, 4 + j, (*chip, c), sibling)
                fwd.start()
                passed.append(fwd)
        for a in range(self.n):
            self._copy(a, 0, (x, y, 1 - c), me).wait_recv()
            for j, chip in enumerate(chips):
                self._copy(a, 4 + j, (*chip, 1 - c), me).wait_recv()
        for a in range(self.n):
            for cp in self._first(a):
                cp.wait_send()
        for cp in passed:
            cp.wait_send()
        for a in range(self.n):
            self._mine(a).wait()


def _half_exchange(arr, name):
    def body(in_ref, out_ref, send_sems, recv_sems, local_sem):
        x, y, c, _ = _place()
        my_chip = 2 * x + y

        def send(j, wait):
            to_me = (c == x) & (y == j // 2) & (c == j % 2)

            @pl.when(to_me)
            def _():
                local = pltpu.make_async_copy(in_ref.at[j], out_ref.at[my_chip], local_sem)
                local.wait() if wait else local.start()

            @pl.when(jnp.logical_not(to_me))
            def _():
                remote = pltpu.make_async_remote_copy(
                    src_ref=in_ref.at[j], dst_ref=out_ref.at[my_chip], send_sem=send_sems.at[j],
                    recv_sem=recv_sems.at[my_chip], device_id=(c, j // 2, j % 2), device_id_type=MESH)
                remote.wait_send() if wait else remote.start()

        for j in range(4):
            send(j, wait=False)
        for j in range(4):
            send(j, wait=True)
        for k in range(4):
            from_me = (k // 2 == x) & (k % 2 == y) & (c == x)

            @pl.when(jnp.logical_not(from_me))
            def _():
                pltpu.make_async_remote_copy(
                    src_ref=in_ref.at[0], dst_ref=out_ref.at[k], send_sem=send_sems.at[0], recv_sem=recv_sems.at[k],
                    device_id=(k // 2, k % 2, x), device_id_type=MESH).wait_recv()

    return pl.pallas_call(
        body, name=name, out_shape=_sds(arr.shape, arr.dtype), in_specs=[ANY], out_specs=ANY,
        scratch_shapes=[pltpu.SemaphoreType.DMA((4,)), pltpu.SemaphoreType.DMA((4,)), pltpu.SemaphoreType.DMA],
    )(arr)


class _ChipExchange:
    def __init__(self, ins, outs, send_sems, recv_sems, local_sems):
        self.ins, self.outs, self.n = ins, outs, len(ins)
        self.send_sems, self.recv_sems, self.local_sems = send_sems, recv_sems, local_sems

    @staticmethod
    def scratch(n):
        return [pltpu.SemaphoreType.DMA((3, n)), pltpu.SemaphoreType.DMA((3, n)), pltpu.SemaphoreType.DMA((n,))]

    def _local(self, a):
        x, y, _, _ = _place()
        me = 2 * x + y
        return pltpu.make_async_copy(self.ins[a].at[me], self.outs[a].at[me], self.local_sems.at[a])

    def _copies(self, a):
        x, y, c, chips = _place()
        me = 2 * x + y
        return [(pltpu.make_async_remote_copy(
                     src_ref=self.ins[a].at[2 * px + py], dst_ref=self.outs[a].at[me],
                     send_sem=self.send_sems.at[k, a], recv_sem=self.recv_sems.at[k, a],
                     device_id=(px, py, c), device_id_type=MESH),
                 pltpu.make_async_remote_copy(
                     src_ref=self.ins[a].at[me], dst_ref=self.outs[a].at[2 * px + py],
                     send_sem=self.send_sems.at[k, a], recv_sem=self.recv_sems.at[k, a],
                     device_id=(px, py, c), device_id_type=MESH))
                for k, (px, py) in enumerate(chips)]

    def start(self):
        for a in range(self.n):
            self._local(a).start()
        for a in range(self.n):
            for send, _ in self._copies(a):
                send.start()

    def finish(self):
        for a in range(self.n):
            for send, recv in self._copies(a):
                send.wait_send()
                recv.wait_recv()
        for a in range(self.n):
            self._local(a).wait()


def _mix_in(x, g_mix, w_in_own, conv_w_own, tm, shards):
    T = x.shape[0]
    n_t = T // tm
    n_s = len(shards)

    def body(x_ref, g_ref, w_own, cw_own, *rest):
        sh_in, rest = rest[:n_s], rest[n_s:]
        u_ref, h_ref, w_hbm, cw_all = rest[:4]
        sh_out, rest = rest[4:4 + n_s], rest[4 + n_s:]
        w_all, keep_sem = rest[:2]
        first = _Gather([w_own, cw_own], [w_all, cw_all], *rest[2:5])
        later = _Gather(sh_in, sh_out, *rest[5:8], core_major=True)
        keep = pltpu.make_async_copy(w_all, w_hbm, keep_sem)
        i = pl.program_id(0)

        @pl.when(i == 0)
        def _():
            first.start()
            later.start()
            first.finish()
            keep.start()

        xv = x_ref[...]
        h = (xv * _rstd(xv) * g_ref[...]).astype(BF16)
        h_ref[...] = h
        u_ref[...] = lax.dot_general(h, w_all[...].reshape(D_IN, D_MODEL), NT, preferred_element_type=F32)

        @pl.when(i == n_t - 1)
        def _():
            later.finish()
            keep.wait()

    own = (D_IN // N_DEV, D_MODEL)
    outs = pl.pallas_call(
        body, name="mix_in", grid=(n_t,),
        in_specs=[pl.BlockSpec((tm, D_MODEL), lambda i: (i, 0)), pl.BlockSpec((1, D_MODEL), lambda i: (0, 0))]
        + [ANY] * (2 + n_s),
        out_specs=[pl.BlockSpec((tm, D_IN), lambda i: (i, 0)), pl.BlockSpec((tm, D_MODEL), lambda i: (i, 0))]
        + [ANY] * (2 + n_s),
        out_shape=[_sds((T, D_IN), F32), _sds((T, D_MODEL), BF16), _sds((N_DEV,) + own, BF16),
                   _sds((N_DEV,) + conv_w_own.shape, F32)] + [_sds((N_DEV,) + a.shape, a.dtype) for a in shards],
        scratch_shapes=[pltpu.VMEM((N_DEV,) + own, BF16), pltpu.SemaphoreType.DMA] + _Gather.scratch(2)
        + _Gather.scratch(n_s),
        compiler_params=pltpu.CompilerParams(dimension_semantics=("arbitrary",)),
    )(x, g_mix, w_in_own, conv_w_own, *shards)
    return outs[0], outs[1], outs[2], outs[3], list(outs[4:])


def _mixer_fwd(u, x, pv, wa, wx, wp, w_out_b, g_ffn, tm, shards=()):
    T = u.shape[0]
    n_s = len(shards)
    n_t = T // tm

    def body(u_ref, x_ref, pv_ref, wa_in, wx_in, wp_in, wo_ref, gf_ref, *rest):
        sh_in, rest = rest[:n_s], rest[n_s:]
        y_ref, hs_ref, hres_ref, h2_ref, saved_ref = rest[:5]
        sh_out, rest = rest[5:5 + n_s], rest[5 + n_s:]
        e_lru, e_pool, a_s, b_s, hc, wa_ref, wx_ref, wp_ref = rest[:8]
        gather = _Gather(sh_in, sh_out, *rest[8:], core_major=True) if n_s else None
        i = pl.program_id(0)

        @pl.when(i == 0)
        def _():
            if gather:
                gather.start()
            e_lru[pl.ds(0, HALO), :] = jnp.zeros((HALO, LRU_W), F32)
            e_pool[pl.ds(0, HALO), :] = jnp.zeros((HALO, POOL_W), F32)
            hc[...] = jnp.zeros((8, LRU_W), F32)
            _fill_block_diag(wa_ref, wa_in)
            _fill_block_diag(wx_ref, wx_in)
            _fill_block_diag(wp_ref, wp_in)

        e_lru[pl.ds(HALO, tm), :] = u_ref[:, 0:LRU_W]
        e_pool[pl.ds(HALO, tm), :] = u_ref[:, 2 * LRU_W:D_IN]
        pv = pv_ref[...]
        xc, r, ig, la = _lru_gates(e_lru, pv, wa_ref, wx_ref, tm)
        for q, val in enumerate((xc, r, ig, la)):
            saved_ref[:, LRU_W * q:LRU_W * (q + 1)] = val
        a, _, mult, _ = _lru_decay(la)
        a_s[...] = a
        b_s[...] = mult * (ig * xc)
        hc[...] = _scan_tile(a_s, b_s, hs_ref, hc[...], tm, reverse=False)
        gl, _ = _gelu_parts(u_ref[:, LRU_W:2 * LRU_W])
        y_lru = hs_ref[...] * gl
        _, zp, _ = _pool_pre(e_pool, pv, wp_ref, tm, i * tm)
        y_pool = zp * pv[ROW_PS:ROW_PS + 1, :]
        yn = jnp.concatenate([y_lru * _rstd(y_lru) * pv[ROW_GL:ROW_GL + 1, :],
                              y_pool * _rstd(y_pool) * pv[ROW_GP:ROW_GP + 1, :]], axis=1).astype(BF16)
        for b in range(N_DEV):
            y_ref[:, 128 * _y_pos(b):128 * (_y_pos(b) + 1)] = yn[:, 128 * b:128 * (b + 1)]
        hr = x_ref[...] + jnp.dot(y_ref[...], wo_ref[...], preferred_element_type=F32)
        hres_ref[...] = hr
        h2_ref[...] = (hr * _rstd(hr) * gf_ref[...]).astype(BF16)
        e_lru[pl.ds(0, HALO), :] = e_lru[pl.ds(tm, HALO), :]
        e_pool[pl.ds(0, HALO), :] = e_pool[pl.ds(tm, HALO), :]

        if gather:
            @pl.when(i == n_t - 1)
            def _():
                gather.finish()

    full = lambda shape: pl.BlockSpec(shape, lambda i: (0,) * len(shape))
    row = lambda w: pl.BlockSpec((tm, w), lambda i: (i, 0))
    outs = pl.pallas_call(
        body, name="mixer_fwd", grid=(n_t,),
        in_specs=[row(D_IN), row(D_MODEL), full((16, LRU_W)), full((8, 64, 64)), full((8, 64, 64)), full((4, 128, 128)),
                  full((D_MODEL, D_MODEL)), full((1, D_MODEL))] + [ANY] * n_s,
        out_specs=[row(D_MODEL), row(LRU_W), row(D_MODEL), row(D_MODEL), row(4 * LRU_W)] + [ANY] * n_s,
        out_shape=[_sds((T, D_MODEL), BF16), _sds((T, LRU_W), F32), _sds((T, D_MODEL), F32), _sds((T, D_MODEL), BF16),
                   _sds((T, 4 * LRU_W), F32)] + [_sds((N_DEV,) + a.shape, a.dtype) for a in shards],
        scratch_shapes=[pltpu.VMEM((HALO + tm, LRU_W), F32), pltpu.VMEM((HALO + tm, POOL_W), F32),
                        pltpu.VMEM((tm, LRU_W), F32), pltpu.VMEM((tm, LRU_W), F32), pltpu.VMEM((8, LRU_W), F32)]
        + [pltpu.VMEM((2, 256, 256), BF16)] * 3 + (_Gather.scratch(n_s) if n_s else []),
        compiler_params=pltpu.CompilerParams(dimension_semantics=("arbitrary",)),
    )(u, x, pv, wa, wx, wp, w_out_b, g_ffn, *shards)
    return outs[0], outs[1], outs[2], outs[3], outs[4], list(outs[5:])


def _ffn_fwd(hres, h2, w1_b, w3_b, w2_b, g_fin, tgt, tm, tn):
    T = hres.shape[0]
    n_j = D_FF // tn

    def body(hres_ref, h2_ref, w1_ref, w3_ref, w2_ref, gfin_ref, tgt_ref,
             g_ref, v_ref, d3_ref, loss_ref, dgfin_ref, acc):
        i, j = pl.program_id(0), pl.program_id(1)

        @pl.when(j == 0)
        def _():
            acc[...] = jnp.zeros((tm, D_MODEL), F32)

        @pl.when((j == 0) & (i == 0))
        def _():
            loss_ref[...] = jnp.zeros((8, 128), F32)
            dgfin_ref[...] = jnp.zeros((1, D_MODEL), F32)

        h2 = h2_ref[...]
        g = lax.dot_general(h2, w1_ref[...], NT, preferred_element_type=F32)
        v = lax.dot_general(h2, w3_ref[...], NT, preferred_element_type=F32)
        g_ref[...] = g.astype(BF16)
        v_ref[...] = v.astype(BF16)
        ff = ((g * _sigmoid(g)) * v).astype(BF16)
        acc[...] += jnp.dot(ff, w2_ref[...], preferred_element_type=F32)

        @pl.when(j == n_j - 1)
        def _():
            h3 = hres_ref[...] + acc[...]
            rstd = _rstd(h3)
            xh = h3 * rstd
            gfin = gfin_ref[...]
            err = xh * gfin - tgt_ref[...]
            loss_ref[...] += 0.5 * jnp.sum(jnp.mean(err * err, axis=-1, keepdims=True))
            dout = err * (1.0 / D_MODEL)
            dx, dgain = _rms_bwd(dout, xh, rstd, gfin)
            d3_ref[...] = dx
            dgfin_ref[...] += dgain

    row = lambda w: pl.BlockSpec((tm, w), lambda i, j: (i, 0))
    const = lambda shape: pl.BlockSpec(shape, lambda i, j: (0,) * len(shape))
    return pl.pallas_call(
        body, name="ffn_fwd", grid=(T // tm, n_j),
        in_specs=[row(D_MODEL), row(D_MODEL),
                  pl.BlockSpec((tn, D_MODEL), lambda i, j: (j, 0)), pl.BlockSpec((tn, D_MODEL), lambda i, j: (j, 0)),
                  pl.BlockSpec((tn, D_MODEL), lambda i, j: (j, 0)), const((1, D_MODEL)), row(D_MODEL)],
        out_specs=[pl.BlockSpec((tm, tn), lambda i, j: (i, j)), pl.BlockSpec((tm, tn), lambda i, j: (i, j)),
                   row(D_MODEL), const((8, 128)), const((1, D_MODEL))],
        out_shape=[_sds((T, D_FF), BF16), _sds((T, D_FF), BF16),
                   _sds((T, D_MODEL), F32), _sds((8, 128), F32), _sds((1, D_MODEL), F32)],
        scratch_shapes=[pltpu.VMEM((tm, D_MODEL), F32)],
        compiler_params=pltpu.CompilerParams(dimension_semantics=("arbitrary", "arbitrary")),
    )(hres, h2, w1_b, w3_b, w2_b, g_fin, tgt)


def _ffn_bwd(d3, g, v, w1_b, w3_b, w2_b, hres, g_ffn, tm, tn):
    T = d3.shape[0]
    n_j = D_FF // tn

    def body(d3_ref, g_ref, v_ref, w1_ref, w3_ref, w2_ref, hres_ref, gf_ref,
             dg_ref, dv_ref, ff_ref, d2_ref, dgffn_ref, acc):
        i, j = pl.program_id(0), pl.program_id(1)

        @pl.when(j == 0)
        def _():
            acc[...] = jnp.zeros((tm, D_MODEL), F32)

        @pl.when((j == 0) & (i == 0))
        def _():
            dgffn_ref[...] = jnp.zeros((1, D_MODEL), F32)

        dff = lax.dot_general(d3_ref[...].astype(BF16), w2_ref[...], NT, preferred_element_type=F32)
        gv = g_ref[...].astype(F32)
        vv = v_ref[...].astype(F32)
        sg = _sigmoid(gv)
        sl = gv * sg
        dgb = (dff * vv * (sg * (1.0 + gv * (1.0 - sg)))).astype(BF16)
        dvb = (dff * sl).astype(BF16)
        dg_ref[...] = dgb
        dv_ref[...] = dvb
        ff_ref[...] = (sl * vv).astype(BF16)
        acc[...] += (jnp.dot(dgb, w1_ref[...], preferred_element_type=F32)
                     + jnp.dot(dvb, w3_ref[...], preferred_element_type=F32))

        @pl.when(j == n_j - 1)
        def _():
            hr = hres_ref[...]
            rstd = _rstd(hr)
            dx, dgain = _rms_bwd(acc[...], hr * rstd, rstd, gf_ref[...])
            d2_ref[...] = d3_ref[...] + dx
            dgffn_ref[...] += dgain

    row = lambda w: pl.BlockSpec((tm, w), lambda i, j: (i, 0))
    tile = pl.BlockSpec((tm, tn), lambda i, j: (i, j))
    const = lambda shape: pl.BlockSpec(shape, lambda i, j: (0,) * len(shape))
    return pl.pallas_call(
        body, name="ffn_bwd", grid=(T // tm, n_j),
        in_specs=[row(D_MODEL), tile, tile,
                  pl.BlockSpec((tn, D_MODEL), lambda i, j: (j, 0)), pl.BlockSpec((tn, D_MODEL), lambda i, j: (j, 0)),
                  pl.BlockSpec((tn, D_MODEL), lambda i, j: (j, 0)), row(D_MODEL), const((1, D_MODEL))],
        out_specs=[tile, tile, tile, row(D_MODEL), const((1, D_MODEL))],
        out_shape=[_sds((T, D_FF), BF16), _sds((T, D_FF), BF16), _sds((T, D_FF), BF16),
                   _sds((T, D_MODEL), F32), _sds((1, D_MODEL), F32)],
        scratch_shapes=[pltpu.VMEM((tm, D_MODEL), F32)],
        compiler_params=pltpu.CompilerParams(dimension_semantics=("arbitrary", "arbitrary")),
    )(d3, g, v, w1_b, w3_b, w2_b, hres, g_ffn)


def _at_b_pair(a, b, c_arr, name, tk, gather=()):
    T, M = a.shape
    N = b.shape[1]
    hm, n_k = M // 2, T // tk
    n_g = len(gather)

    def body(c_ref, a_ref, b_ref, *rest):
        g_in, o_ref, rest = rest[:n_g], rest[n_g], rest[n_g + 1:]
        g_out, rest = rest[:n_g], rest[n_g:]
        acc, landed, send_sem, recv_sem = rest[:4]
        ag = _Gather(g_in, g_out, *rest[4:]) if n_g else None
        ph, k = pl.program_id(0), pl.program_id(1)

        def hand_over():
            x, y, c, _ = _place()
            return pltpu.make_async_remote_copy(
                src_ref=acc.at[0], dst_ref=landed, send_sem=send_sem, recv_sem=recv_sem,
                device_id=(x, y, 1 - c), device_id_type=MESH)

        if ag:
            @pl.when((ph == 0) & (k == 0))
            def _():
                ag.start()

        @pl.when(k == 0)
        def _():
            acc[ph] = jnp.zeros((hm, N), F32)

        acc[ph] += lax.dot_general(a_ref[...].astype(BF16), b_ref[...].astype(BF16), TN, preferred_element_type=F32)

        @pl.when((ph == 0) & (k == n_k - 1))
        def _():
            hand_over().start()

        @pl.when((ph == 1) & (k == n_k - 1))
        def _():
            copy = hand_over()
            copy.wait_recv()
            o_ref[...] = (acc[1] + landed[...]).astype(BF16)
            copy.wait_send()
            if ag:
                ag.finish()

    outs = pl.pallas_call(
        body, name=name,
        grid_spec=pltpu.PrefetchScalarGridSpec(
            num_scalar_prefetch=1, grid=(2, n_k),
            in_specs=[pl.BlockSpec((tk, hm), lambda ph, k, c_ref: (k, (ph + 1 - c_ref[0]) % 2)),
                      pl.BlockSpec((tk, N), lambda ph, k, c_ref: (k, 0))] + [ANY] * n_g,
            out_specs=[pl.BlockSpec((hm, N), lambda ph, k, c_ref: (0, 0))] + [ANY] * n_g,
            scratch_shapes=[pltpu.VMEM((2, hm, N), F32), pltpu.VMEM((hm, N), F32),
                            pltpu.SemaphoreType.DMA, pltpu.SemaphoreType.DMA] + (_Gather.scratch(n_g) if n_g else [])),
        out_shape=[_sds((hm, N), BF16)] + [_sds((N_DEV,) + g.shape, g.dtype) for g in gather],
        compiler_params=pltpu.CompilerParams(dimension_semantics=("arbitrary", "arbitrary")),
    )(c_arr, a, b, *gather)
    return (outs[0], list(outs[1:])) if n_g else outs[0]


def _mixer_bwd(d2, u, hs, saved, pv, wa, wx, wp, w_out_b, tm, chip_sums=()):
    T = u.shape[0]
    n_t = T // tm
    n_x = len(chip_sums)

    def body(d2_ref, u_ref, uh_ref, hs_ref, hh_ref, saved_ref, pv_ref, wa_in, wx_in, wp_in, wo_ref, *rest):
        x_in, rest = rest[:n_x], rest[n_x:]
        du_ref, sg_ref = rest[:2]
        x_out, rest = rest[2:2 + n_x], rest[2 + n_x:]
        e_pool, e_h, a_s, b_s, dh_s, mu_s, f_x, f_p, mc, cx, cp = rest[:11]
        wa_ref, wx_ref, wp_ref, vacc_ref, dwa_ref, dwx_ref, dwp_ref = rest[11:18]
        exchange = _ChipExchange(x_in, x_out, *rest[18:]) if n_x else None
        s = pl.program_id(0)
        it = n_t - 1 - s

        @pl.when(s == 0)
        def _():
            if exchange:
                exchange.start()
            mc[...] = jnp.zeros((8, LRU_W), F32)
            cx[...] = jnp.zeros((8, LRU_W), F32)
            cp[...] = jnp.zeros((HALO, POOL_W), F32)
            vacc_ref[...] = jnp.zeros((16, LRU_W), F32)
            dwa_ref[...] = jnp.zeros((2, 256, 256), F32)
            dwx_ref[...] = jnp.zeros((2, 256, 256), F32)
            dwp_ref[...] = jnp.zeros((2, 256, 256), F32)
            _fill_block_diag(wa_ref, wa_in)
            _fill_block_diag(wx_ref, wx_in)
            _fill_block_diag(wp_ref, wp_in)

        first = it == 0
        e_pool[pl.ds(0, HALO), :] = jnp.where(first, 0.0, uh_ref[...])
        e_pool[pl.ds(HALO, tm), :] = u_ref[:, 2 * LRU_W:D_IN]
        e_h[pl.ds(0, 8), :] = jnp.where(first, 0.0, hh_ref[...])
        e_h[pl.ds(8, tm), :] = hs_ref[...]
        pv = pv_ref[...]
        saved = lambda q: saved_ref[:, LRU_W * q:LRU_W * (q + 1)]

        dyn = lax.dot_general(d2_ref[...].astype(BF16), wo_ref[...], NT, preferred_element_type=F32)
        dyn = jnp.concatenate([dyn[:, 128 * _y_pos(b):128 * (_y_pos(b) + 1)] for b in range(N_DEV)], axis=1)

        h = hs_ref[...]
        ug = u_ref[:, LRU_W:2 * LRU_W]
        gl, dgl = _gelu_parts(ug)
        y_lru = h * gl
        rstd_l = _rstd(y_lru)
        dy_lru, d_gain_l = _rms_bwd(dyn[:, 0:LRU_W], y_lru * rstd_l, rstd_l, pv[ROW_GL:ROW_GL + 1, :])
        dh = dy_lru * gl
        du_ref[:, LRU_W:2 * LRU_W] = (dy_lru * h * dgl).astype(BF16)
        a_s[...] = jnp.exp(saved(3))
        b_s[...] = a_s[...] * dh
        dh_s[...] = dh
        mu_s[pl.ds(tm, 8), :] = mc[...]
        mc[...] = _scan_tile(a_s, b_s, mu_s, mc[...], tm, reverse=True)
        xc, r, ig = saved(0), saved(1), saved(2)
        a, om, mult, rmult = _lru_decay(saved(3))
        lam_t = dh_s[...] + mu_s[pl.ds(1, tm), :]
        da = lam_t * e_h[pl.ds(7, tm), :]
        dmult = lam_t * (ig * xc)
        di = lam_t * (mult * xc)
        dxc = lam_t * (mult * ig)
        dla = da * a - jnp.where(om > 1e-12, dmult * ((a * a) * rmult), 0.0)
        dra = (dla * (-LRU_C * _softplus_neg_lambda(pv))) * (r * (1.0 - r))
        dia = di * (ig * (1.0 - ig))
        drab = dra.astype(BF16)
        diab = dia.astype(BF16)
        xcb = xc.astype(BF16)
        dxc = dxc + _bd_t(drab, wa_ref) + _bd_t(diab, wx_ref)
        dwa_ref[...] += _bd_grad(xcb, drab)
        dwx_ref[...] += _bd_grad(xcb, diab)
        sig_neg_lam = _sigmoid(-pv[ROW_LAM:ROW_LAM + 1, :])
        d_lam = jnp.sum(dla * r, axis=0, keepdims=True) * (LRU_C * sig_neg_lam)

        f_x[pl.ds(0, tm), :] = dxc
        f_x[pl.ds(tm, 8), :] = cx[...]
        du_lru = jnp.zeros((tm, LRU_W), F32)
        u_lru = u_ref[:, 0:LRU_W]
        d_cw = []
        for k in range(4):
            later = f_x[pl.ds(3 - k, tm), :]
            du_lru = du_lru + later * pv[ROW_CW + k:ROW_CW + k + 1, :]
            d_cw.append(jnp.sum(later * u_lru, axis=0, keepdims=True))
        du_ref[:, 0:LRU_W] = du_lru.astype(BF16)
        cx[...] = f_x[pl.ds(0, 8), :]

        pooled_b, zp, inv_cnts = _pool_pre(e_pool, pv, wp_ref, tm, it * tm)
        ps = pv[ROW_PS:ROW_PS + 1, :]
        y_pool = zp * ps
        rstd_p = _rstd(y_pool)
        dy_pool, d_gain_p = _rms_bwd(dyn[:, LRU_W:D_MODEL], y_pool * rstd_p, rstd_p, pv[ROW_GP:ROW_GP + 1, :])
        dz = dy_pool * ps
        dzb = dz.astype(BF16)
        dwp_ref[...] += _bd_grad(pooled_b, dzb)
        dpooled = _bd_t(dzb, wp_ref)
        for g, w in enumerate(POOL_WINDOWS):
            f_p[pl.ds(0, tm), pl.ds(128 * g, 128)] = _over_count(dpooled[:, 128 * g:128 * (g + 1)], w, inv_cnts[g])
        f_p[pl.ds(tm, HALO), :] = cp[...]
        for g, w in enumerate(POOL_WINDOWS):
            acc = _window_sum(f_p[:, pl.ds(128 * g, 128)], w, back=False)[0:tm, :]
            du_ref[:, 2 * LRU_W + 128 * g:2 * LRU_W + 128 * (g + 1)] = (
                acc - dpooled[:, 128 * g:128 * (g + 1)]).astype(BF16)
        cp[...] = f_p[pl.ds(0, HALO), :]

        rows = d_cw + [
            jnp.sum(dxc, axis=0, keepdims=True),
            jnp.sum(dra, axis=0, keepdims=True),
            jnp.sum(dia, axis=0, keepdims=True),
            d_lam,
            jnp.sum(dz, axis=0, keepdims=True),
            jnp.sum(dy_pool * zp, axis=0, keepdims=True),
            d_gain_l, d_gain_p,
            jnp.zeros((4, LRU_W), F32),
        ]
        vacc_ref[...] += jnp.concatenate(rows, axis=0)

        @pl.when(s == n_t - 1)
        def _():
            sg_ref[SG_VEC:SG_VEC + 16, :] = vacc_ref[:, 0:256]
            sg_ref[SG_VEC + 16:SG_VEC + 32, :] = vacc_ref[:, 256:512]
            for half in range(2):
                sg_ref[SG_WA + 64 * half:SG_WA + 64 * (half + 1), :] = _diag_pack(dwa_ref[half], 64)
                sg_ref[SG_WX + 64 * half:SG_WX + 64 * (half + 1), :] = _diag_pack(dwx_ref[half], 64)
                sg_ref[SG_WP + 128 * half:SG_WP + 128 * (half + 1), :] = _diag_pack(dwp_ref[half], 128)
            if exchange:
                exchange.finish()

    rev = lambda w: pl.BlockSpec((tm, w), lambda s: (n_t - 1 - s, 0))
    full = lambda shape: pl.BlockSpec(shape, lambda s: (0,) * len(shape))
    outs = pl.pallas_call(
        body, name="mixer_bwd", grid=(n_t,),
        in_specs=[rev(D_MODEL), rev(D_IN),
                  pl.BlockSpec((HALO, POOL_W), lambda s: (jnp.maximum((n_t - 1 - s) * (tm // HALO) - 1, 0), 2)),
                  rev(LRU_W),
                  pl.BlockSpec((8, LRU_W), lambda s: (jnp.maximum((n_t - 1 - s) * (tm // 8) - 1, 0), 0)),
                  rev(4 * LRU_W), full((16, LRU_W)), full((8, 64, 64)), full((8, 64, 64)), full((4, 128, 128)),
                  full((D_MODEL, D_MODEL))] + [ANY] * n_x,
        out_specs=[rev(D_IN), full((SG_ROWS, 256))] + [ANY] * n_x,
        out_shape=[_sds((T, D_IN), BF16), _sds((SG_ROWS, 256), F32)] + [_sds(a.shape, a.dtype) for a in chip_sums],
        scratch_shapes=[pltpu.VMEM((HALO + tm, POOL_W), F32),
                        pltpu.VMEM((8 + tm, LRU_W), F32)] + [pltpu.VMEM((tm, LRU_W), F32)] * 3 + [
                        pltpu.VMEM((tm + 8, LRU_W), F32), pltpu.VMEM((tm + 8, LRU_W), F32),
                        pltpu.VMEM((tm + HALO, POOL_W), F32), pltpu.VMEM((8, LRU_W), F32),
                        pltpu.VMEM((8, LRU_W), F32), pltpu.VMEM((HALO, POOL_W), F32)]
        + [pltpu.VMEM((2, 256, 256), BF16)] * 3 + [pltpu.VMEM((16, LRU_W), F32)] + [pltpu.VMEM((2, 256, 256), F32)] * 3
        + (_ChipExchange.scratch(n_x) if n_x else []),
        compiler_params=pltpu.CompilerParams(dimension_semantics=("arbitrary",)),
    )(d2, u, u, hs, hs, saved, pv, wa, wx, wp, w_out_b, *chip_sums)
    return outs[0], outs[1], list(outs[2:])


def _mix_in_bwd(du, x, d2, w_in_t, g_mix, tm):
    T = x.shape[0]

    def body(du_ref, x_ref, d2_ref, w_ref, g_ref, dx_ref, dg_ref):
        @pl.when(pl.program_id(0) == 0)
        def _():
            dg_ref[...] = jnp.zeros((1, D_MODEL), F32)

        dh = jnp.dot(du_ref[...], w_ref[...], preferred_element_type=F32)
        xv = x_ref[...]
        rstd = _rstd(xv)
        dx, dgain = _rms_bwd(dh, xv * rstd, rstd, g_ref[...])
        dx_ref[...] = d2_ref[...] + dx
        dg_ref[...] += dgain

    row = lambda w: pl.BlockSpec((tm, w), lambda i: (i, 0))
    const = lambda shape: pl.BlockSpec(shape, lambda i: (0,) * len(shape))
    return pl.pallas_call(
        body, name="mix_in_bwd", grid=(T // tm,),
        in_specs=[row(D_IN), row(D_MODEL), row(D_MODEL), const((D_IN, D_MODEL)), const((1, D_MODEL))],
        out_specs=[row(D_MODEL), const((1, D_MODEL))],
        out_shape=[_sds((T, D_MODEL), F32), _sds((1, D_MODEL), F32)],
        compiler_params=pltpu.CompilerParams(dimension_semantics=("arbitrary",)),
    )(du, x, d2, w_in_t, g_mix)


def _adamw(w, g, m, v):
    m = ADAM_B1 * m + (1.0 - ADAM_B1) * g
    v = ADAM_B2 * v + (1.0 - ADAM_B2) * (g * g)
    m_hat = m / (1.0 - ADAM_B1 ** ADAM_STEP)
    v_hat = v / (1.0 - ADAM_B2 ** ADAM_STEP)
    delta = -ADAM_LR * (m_hat / (jnp.sqrt(v_hat) + ADAM_EPS) + ADAM_WD * w)
    return delta, m, v


def _adam_shards(ws, ms, vs, parts):
    n = len(ws)
    n_blk = [w.shape[0] // ADAM_ROWS for w in ws]

    def body(*refs):
        w_refs, m_refs, v_refs, p_refs, outs = (refs[:n], refs[n:2 * n], refs[2 * n:3 * n], refs[3 * n:4 * n],
                                                refs[4 * n:])
        i = pl.program_id(0)
        for a in range(n):
            @pl.when(i < n_blk[a])
            def _(a=a):
                g = p_refs[a][0].astype(F32)
                for j in range(1, 4):
                    g = g + p_refs[a][j].astype(F32)
                delta, new_m, new_v = _adamw(w_refs[a][...], g, m_refs[a][...], v_refs[a][...])
                for kind, val in enumerate((g, delta, new_m, new_v)):
                    outs[4 * a + kind][...] = val

    blk = lambda a: pl.BlockSpec((ADAM_ROWS, D_MODEL), lambda i: (jnp.minimum(i, n_blk[a] - 1), 0))
    part_blk = lambda a: pl.BlockSpec((4, ADAM_ROWS, D_MODEL), lambda i: (0, jnp.minimum(i, n_blk[a] - 1), 0))
    res = pl.pallas_call(
        body, name="adam_shards", grid=(max(n_blk),),
        in_specs=[blk(a) for a in range(n)] * 3 + [part_blk(a) for a in range(n)],
        out_specs=[blk(a) for a in range(n) for _ in range(4)],
        out_shape=[_sds(w.shape, F32) for w in ws for _ in range(4)],
        compiler_params=pltpu.CompilerParams(dimension_semantics=("arbitrary",)),
    )(*ws, *ms, *vs, *parts)
    return [tuple(res[4 * a:4 * a + 4]) for a in range(n)]


SMALL_PARAMS = [("norm_mix_g", (1, D_MODEL)), ("conv_w", (1, 4, 64)), ("conv_b", (1, LRU_W)),
                ("gate_a_w", (1, 8, 64, 64)), ("gate_a_b", (1, LRU_W)), ("gate_x_w", (1, 8, 64, 64)),
                ("gate_x_b", (1, LRU_W)), ("lru_lambda", (1, LRU_W)), ("pool_w", (1, 4, 128, 128)),
                ("pool_b", (1, POOL_W)), ("pool_scale", (1, POOL_W)), ("norm_lru_g", (1, LRU_W)),
                ("norm_pool_g", (1, POOL_W)), ("norm_ffn_g", (1, D_MODEL)), ("final_norm_g", (1, D_MODEL))]
VEC_ROW = dict(conv_b=ROW_CB, gate_a_b=ROW_BA, gate_x_b=ROW_BX, lru_lambda=ROW_LAM, pool_b=ROW_PB, pool_scale=ROW_PS,
               norm_lru_g=ROW_GL, norm_pool_g=ROW_GP)
WHOLE = (Ellipsis,)


def _unpack_mixer_grads(sg, dev):
    vec = jnp.concatenate([sg[SG_VEC:SG_VEC + 16], sg[SG_VEC + 16:SG_VEC + 32]], axis=1)
    out = {nm: [(WHOLE, vec[r:r + 1])] for nm, r in VEC_ROW.items()}
    own = jnp.zeros((4, 64), F32)
    for d in range(N_DEV):
        own = jnp.where(dev == d, vec[ROW_CW:ROW_CW + 4, 64 * d:64 * (d + 1)], own)
    out["conv_w"] = [((0,), own)]
    for nm, row0 in (("gate_a_w", SG_WA), ("gate_x_w", SG_WX)):
        out[nm] = [((0, b), sg[row0 + 64 * (b // 4):row0 + 64 * (b // 4 + 1), 64 * (b % 4):64 * (b % 4 + 1)])
                   for b in range(8)]
    out["pool_w"] = [((0, b), sg[SG_WP + 128 * (b // 2):SG_WP + 128 * (b // 2 + 1), 128 * (b % 2):128 * (b % 2 + 1)])
                     for b in range(4)]
    return out


def _adam_small(parts, w, m, v):
    names = [nm for nm, _ in SMALL_PARAMS]
    n = len(names)

    def body(sg_ref, gm_ref, gf_ref, gn_ref, ls_ref, *rest):
        w_refs, m_refs, v_refs, outs = rest[:n], rest[n:2 * n], rest[2 * n:3 * n], rest[3 * n:]
        dev = 4 * lax.axis_index("x") + 2 * lax.axis_index("y") + lax.axis_index("c")

        def total(ref):
            acc = ref[0]
            for d in range(1, N_DEV):
                acc = acc + ref[d]
            return acc

        pieces = _unpack_mixer_grads(total(sg_ref), dev)
        pieces["norm_mix_g"] = [(WHOLE, total(gm_ref))]
        pieces["norm_ffn_g"] = [(WHOLE, total(gf_ref))]
        pieces["final_norm_g"] = [(WHOLE, total(gn_ref))]
        for i, nm in enumerate(names):
            for idx, g in pieces[nm]:
                delta, new_m, new_v = _adamw(w_refs[i][idx], g, m_refs[i][idx], v_refs[i][idx])
                for kind, val in enumerate((g, delta, new_m, new_v)):
                    outs[4 * i + kind][idx] = val
        outs[4 * n][...] = total(ls_ref)

    shapes = [_sds(shape, F32) for _, shape in SMALL_PARAMS for _ in range(4)] + [_sds((8, 128), F32)]
    res = pl.pallas_call(body, name="adam_small", out_shape=shapes)(
        *parts, *[w[nm] for nm in names], *[m[nm] for nm in names], *[v[nm] for nm in names])
    return {nm: tuple(res[4 * i:4 * i + 4]) for i, nm in enumerate(names)}, res[4 * n][0, 0]


def _vec_rows(conv_w_full, conv_b, ba, bx, lam, pb, ps, gl, gp):
    return jnp.concatenate([conv_w_full, conv_b, ba, bx, lam, pb, ps, gl, gp, jnp.zeros((4, LRU_W), F32)], axis=0)


WEIGHT_ORDER = ['norm_mix_g', 'w_in', 'conv_w', 'conv_b', 'gate_a_w', 'gate_a_b', 'gate_x_w', 'gate_x_b', 'lru_lambda',
                'pool_w', 'pool_b', 'pool_scale', 'norm_lru_g', 'norm_pool_g', 'w_out', 'norm_ffn_g', 'ffn_w1', 'ffn_w3',
                'ffn_w2', 'final_norm_g']


def kernel(x, norm_mix_g, w_in, conv_w, conv_b, gate_a_w, gate_a_b, gate_x_w, gate_x_b, lru_lambda, pool_w, pool_b, pool_scale, norm_lru_g, norm_pool_g, w_out, norm_ffn_g, ffn_w1, ffn_w3, ffn_w2, final_norm_g, loss_target, m_norm_mix_g, m_w_in, m_conv_w, m_conv_b, m_gate_a_w, m_gate_a_b, m_gate_x_w, m_gate_x_b, m_lru_lambda, m_pool_w, m_pool_b, m_pool_scale, m_norm_lru_g, m_norm_pool_g, m_w_out, m_norm_ffn_g, m_ffn_w1, m_ffn_w3, m_ffn_w2, m_final_norm_g, v_norm_mix_g, v_w_in, v_conv_w, v_conv_b, v_gate_a_w, v_gate_a_b, v_gate_x_w, v_gate_x_b, v_lru_lambda, v_pool_w, v_pool_b, v_pool_scale, v_norm_lru_g, v_norm_pool_g, v_w_out, v_norm_ffn_g, v_ffn_w1, v_ffn_w3, v_ffn_w2, v_final_norm_g):
    ac = lax.axis_index("c")
    tm, tmx, tn, tk = 512, 512, 1408, 1024
    tm_in = 1024
    xs, tgt = x[0], loss_target[0]
    g_fin = final_norm_g.reshape(1, D_MODEL)
    c_arr = jnp.reshape(ac, (1,)).astype(jnp.int32)

    tr = lambda w: jnp.swapaxes(w[0], 0, 1)
    own = lambda w: w[0]
    bf = lambda a: a.astype(BF16)

    u, h1, g_in, g_conv, (g_out,) = _mix_in(xs, norm_mix_g, bf(tr(w_in)), conv_w[0], tm, shards=[bf(own(w_out))])
    w_in_t = g_in.reshape(D_IN, D_MODEL)
    conv_w_full = g_conv.transpose(1, 0, 2).reshape(4, LRU_W)
    pv = _vec_rows(conv_w_full, conv_b, gate_a_b, gate_x_b, lru_lambda, pool_b, pool_scale, norm_lru_g, norm_pool_g)
    wa, wx, wp = gate_a_w[0], gate_x_w[0], pool_w[0]
    w_out_b = g_out.reshape(D_MODEL, D_MODEL)
    y, hs, hres, h2, saved, (g_w1, g_w3, g_w2) = _mixer_fwd(
        u, xs, pv, wa, wx, wp, w_out_b, norm_ffn_g, tmx, shards=[bf(tr(ffn_w1)), bf(tr(ffn_w3)), bf(own(ffn_w2))])
    w1_t, w3_t, w2_b = g_w1.reshape(D_FF, D_MODEL), g_w3.reshape(D_FF, D_MODEL), g_w2.reshape(D_FF, D_MODEL)
    g, v, d3, loss_acc, d_gfin = _ffn_fwd(hres, h2, w1_t, w3_t, w2_b, g_fin, tgt, tm, tn)

    dg, dv, ff, d2, d_gffn = _ffn_bwd(d3, g, v, w1_t, w3_t, w2_b, hres, norm_ffn_g, tm, tn)
    chips = lambda a: a.reshape(4, a.shape[0] // 4, a.shape[1])
    early_sums = [chips(_at_b_pair(y, d2, c_arr, "grad_w_out", 2 * tk)), chips(_at_b_pair(dg, h2, c_arr, "grad_w1", tk)),
                  chips(_at_b_pair(dv, h2, c_arr, "grad_w3", tk)), chips(_at_b_pair(ff, d3, c_arr, "grad_w2", tk))]
    du, d_mixer, early_parts = _mixer_bwd(d2, u, hs, saved, pv, wa, wx, wp, w_out_b, tmx, chip_sums=early_sums)
    grad_x, d_gmix = _mix_in_bwd(du, xs, d2, w_in_t, norm_mix_g, tm_in)
    d_win, small_parts = _at_b_pair(du, h1, c_arr, "grad_w_in", 2 * tk,
                                    gather=[d_mixer, d_gmix, d_gffn, d_gfin, loss_acc])
    parts = [_half_exchange(chips(d_win), "grads_to_chips_w_in")] + list(early_parts)

    res = {}
    shard_w = dict(w_in=(w_in, m_w_in, v_w_in, tr), w_out=(w_out, m_w_out, v_w_out, own),
                   ffn_w1=(ffn_w1, m_ffn_w1, v_ffn_w1, tr), ffn_w3=(ffn_w3, m_ffn_w3, v_ffn_w3, tr),
                   ffn_w2=(ffn_w2, m_ffn_w2, v_ffn_w2, own))
    shard_res = _adam_shards([view(w) for w, _, _, view in shard_w.values()],
                             [view(m) for _, m, _, view in shard_w.values()],
                             [view(v) for _, _, v, view in shard_w.values()], parts)
    for (nm, (_, _, _, view)), outs in zip(shard_w.items(), shard_res):
        res[nm] = [(jnp.swapaxes(o, 0, 1) if view is tr else o)[None] for o in outs]

    row = lambda a: a.reshape(1, D_MODEL)
    small = lambda gm, cw, cb, wa_, ba, wx_, bx, lam, pw, pb, ps, gl, gp, gf, gn: dict(
        norm_mix_g=gm, conv_w=cw, conv_b=cb, gate_a_w=wa_, gate_a_b=ba, gate_x_w=wx_, gate_x_b=bx, lru_lambda=lam,
        pool_w=pw, pool_b=pb, pool_scale=ps, norm_lru_g=gl, norm_pool_g=gp, norm_ffn_g=gf, final_norm_g=row(gn))
    small_res, loss = _adam_small(
        small_parts,
        small(norm_mix_g, conv_w, conv_b, gate_a_w, gate_a_b, gate_x_w, gate_x_b, lru_lambda, pool_w, pool_b,
              pool_scale, norm_lru_g, norm_pool_g, norm_ffn_g, final_norm_g),
        small(m_norm_mix_g, m_conv_w, m_conv_b, m_gate_a_w, m_gate_a_b, m_gate_x_w, m_gate_x_b, m_lru_lambda, m_pool_w,
              m_pool_b, m_pool_scale, m_norm_lru_g, m_norm_pool_g, m_norm_ffn_g, m_final_norm_g),
        small(v_norm_mix_g, v_conv_w, v_conv_b, v_gate_a_w, v_gate_a_b, v_gate_x_w, v_gate_x_b, v_lru_lambda, v_pool_w,
              v_pool_b, v_pool_scale, v_norm_lru_g, v_norm_pool_g, v_norm_ffn_g, v_final_norm_g))
    for nm, outs in small_res.items():
        res[nm] = [o.reshape(D_MODEL) for o in outs] if nm == "final_norm_g" else list(outs)

    out = [loss, grad_x[None]]
    for kind in range(4):
        out += [res[nm][kind] for nm in WEIGHT_ORDER]
    return tuple(out)
```

```python
import functools

import jax
import jax.numpy as jnp
from jax import lax
from jax.experimental import pallas as pl
from jax.experimental.pallas import tpu as pltpu

F32 = jnp.float32
BF16 = jnp.bfloat16

D_MODEL = 1024
LRU_W = 512
POOL_W = 512
D_IN = 1536
D_FF = 2816
POOL_WINDOWS = (2, 4, 8, 16)
EPS = 1e-6
LRU_C = 8.0
N_DEV = 8
HALO = 16
SCAN_UNROLL = 8
ADAM_ROWS = 32

ADAM_LR = 0.001
ADAM_B1 = 0.9
ADAM_B2 = 0.999
ADAM_EPS = 1e-08
ADAM_WD = 0.01
ADAM_STEP = 10

ROW_CW, ROW_CB, ROW_BA, ROW_BX, ROW_LAM, ROW_PB, ROW_PS, ROW_GL, ROW_GP = 0, 4, 5, 6, 7, 8, 9, 10, 11
SG_VEC, SG_WA, SG_WX, SG_WP, SG_ROWS = 0, 32, 160, 288, 544

NT = (((1,), (1,)), ((), ()))
TN = (((0,), (0,)), ((), ()))


def _sds(shape, dtype):
    return jax.ShapeDtypeStruct(shape, dtype)


def _sigmoid(x):
    return 0.5 * jnp.tanh(0.5 * x) + 0.5


def _gelu_parts(x):
    c = 0.7978845608028654
    inner = c * (x + 0.044715 * (x * x * x))
    th = jnp.tanh(inner)
    g = 0.5 * x * (1.0 + th)
    dg = 0.5 * (1.0 + th) + 0.5 * x * (1.0 - th * th) * (c * (1.0 + 3.0 * 0.044715 * (x * x)))
    return g, dg


def _window_sum(ext, w, back):
    n = ext.shape[0]
    s, k = ext, 1
    while k < w:
        s = s + pltpu.roll(s, k if back else n - k, 0)
        k *= 2
    return s


def _rstd(x):
    return lax.rsqrt(jnp.mean(x * x, axis=-1, keepdims=True) + EPS)


def _rms_bwd(dy, xhat, rstd, gain):
    dxh = dy * gain
    dx = rstd * (dxh - xhat * jnp.mean(dxh * xhat, axis=-1, keepdims=True))
    return dx, jnp.sum(dy * xhat, axis=0, keepdims=True)


def _bd(xb, w_ref):
    return jnp.concatenate(
        [jnp.dot(xb[:, :256], w_ref[0], preferred_element_type=F32),
         jnp.dot(xb[:, 256:], w_ref[1], preferred_element_type=F32)], axis=1)


def _bd_t(xb, w_ref):
    return jnp.concatenate(
        [lax.dot_general(xb[:, :256], w_ref[0], NT, preferred_element_type=F32),
         lax.dot_general(xb[:, 256:], w_ref[1], NT, preferred_element_type=F32)], axis=1)


def _bd_grad(xb, db):
    return jnp.stack(
        [lax.dot_general(xb[:, :256], db[:, :256], TN, preferred_element_type=F32),
         lax.dot_general(xb[:, 256:], db[:, 256:], TN, preferred_element_type=F32)], axis=0)


def _fill_block_diag(dst, src_ref):
    n, k, _ = src_ref.shape
    dst[...] = jnp.zeros(dst.shape, BF16)
    for b in range(n):
        p, q = divmod(b, 256 // k)
        dst[p, q * k:(q + 1) * k, q * k:(q + 1) * k] = src_ref[b].astype(BF16)


def _diag_pack(w, k):
    lane = lax.broadcasted_iota(jnp.int32, (k, 256), 1)
    out = w[0:k]
    for q in range(1, 256 // k):
        out = jnp.where(lane >= q * k, w[q * k:(q + 1) * k], out)
    return out


def _y_pos(b):
    return 4 * (b % 2) + b // 2


def _softplus_neg_lambda(pv):
    z = -pv[ROW_LAM:ROW_LAM + 1, :]
    return jnp.maximum(z, 0.0) + jnp.log(1.0 + jnp.exp(-jnp.abs(z)))


def _lru_gates(e_lru, pv, wa_ref, wx_ref, tm):
    xc = pv[ROW_CB:ROW_CB + 1, :]
    for k in range(4):
        xc = xc + e_lru[pl.ds(HALO - 3 + k, tm), :] * pv[ROW_CW + k:ROW_CW + k + 1, :]
    xcb = xc.astype(BF16)
    r = _sigmoid(_bd(xcb, wa_ref) + pv[ROW_BA:ROW_BA + 1, :])
    ig = _sigmoid(_bd(xcb, wx_ref) + pv[ROW_BX:ROW_BX + 1, :])
    return xc, r, ig, (-LRU_C * r) * _softplus_neg_lambda(pv)


def _lru_decay(la):
    a = jnp.exp(la)
    om = -jnp.tanh(la) * (1.0 + a * a)
    omc = jnp.maximum(om, 1e-12)
    rmult = lax.rsqrt(omc)
    return a, om, omc * rmult, rmult


def _over_count(v, w, inv_head):
    return jnp.concatenate([v[0:HALO] * inv_head, v[HALO:] * (1.0 / w)], axis=0)


def _pool_pre(e_pool, pv, wp_ref, tm, t0):
    t_head = t0 + lax.broadcasted_iota(jnp.int32, (HALO, 1), 0)
    parts, inv_heads = [], []
    for g, w in enumerate(POOL_WINDOWS):
        ext = e_pool[:, pl.ds(128 * g, 128)]
        s = _window_sum(ext, w, back=True)[HALO:, :]
        inv_head = 1.0 / jnp.minimum(t_head + 1, w).astype(F32)
        inv_heads.append(inv_head)
        parts.append(_over_count(s, w, inv_head) - ext[HALO:, :])
    pooled = jnp.concatenate(parts, axis=1)
    pooled_b = pooled.astype(BF16)
    zp = _bd(pooled_b, wp_ref) + pv[ROW_PB:ROW_PB + 1, :]
    return pooled_b, zp, inv_heads


def _scan_tile(a_ref, b_ref, out_ref, carry, tm, reverse):
    row = lax.broadcasted_iota(jnp.int32, (8, LRU_W), 0)
    nblk = tm // 8

    def local_scan(blk):
        r0 = pl.multiple_of(blk * 8, 8)
        av = a_ref[pl.ds(r0, 8), :]
        bv = b_ref[pl.ds(r0, 8), :]
        for d in (1, 2, 4):
            sh = (8 - d) if reverse else d
            a_s = pltpu.roll(av, sh, 0)
            b_s = pltpu.roll(bv, sh, 0)
            m = (row < 8 - d) if reverse else (row >= d)
            bv = jnp.where(m, av * b_s + bv, bv)
            av = jnp.where(m, av * a_s, av)
        return r0, av, bv

    def step(i, hin):
        local = [local_scan((nblk - 1 - (i * SCAN_UNROLL + j)) if reverse else (i * SCAN_UNROLL + j))
                 for j in range(SCAN_UNROLL)]
        for r0, av, bv in local:
            hv = av * hin + bv
            out_ref[pl.ds(r0, 8), :] = hv
            hin = jnp.broadcast_to(hv[0:1, :] if reverse else hv[7:8, :], (8, LRU_W))
        return hin

    return lax.fori_loop(0, nblk // SCAN_UNROLL, step, carry)


MESH = pl.DeviceIdType.MESH
ANY = pl.BlockSpec(memory_space=pl.ANY)


def _place():
    x, y, c = lax.axis_index("x"), lax.axis_index("y"), lax.axis_index("c")
    chips = [(1 - x, y), (x, 1 - y), (1 - x, 1 - y)]
    return x, y, c, chips


class _Gather:
    def __init__(self, ins, outs, send_sems, recv_sems, local_sems, core_major=False):
        self.ins, self.outs, self.n = ins, outs, len(ins)
        self.send_sems, self.recv_sems, self.local_sems = send_sems, recv_sems, local_sems
        self.core_major = core_major

    @staticmethod
    def scratch(n):
        return [pltpu.SemaphoreType.DMA((7, n)), pltpu.SemaphoreType.DMA((7, n)), pltpu.SemaphoreType.DMA((n,))]

    def _slot(self, a, px, py, pc):
        return self.outs[a].at[4 * pc + 2 * px + py if self.core_major else 4 * px + 2 * py + pc]

    def _copy(self, a, k, block, to, src=None):
        return pltpu.make_async_remote_copy(
            src_ref=self._slot(a, *block) if src is None else src, dst_ref=self._slot(a, *block),
            send_sem=self.send_sems.at[k, a], recv_sem=self.recv_sems.at[k, a], device_id=to, device_id_type=MESH)

    def _mine(self, a):
        x, y, c, _ = _place()
        return pltpu.make_async_copy(self.ins[a], self._slot(a, x, y, c), self.local_sems.at[a])

    def _first(self, a):
        x, y, c, chips = _place()
        me = (x, y, c)
        return ([self._copy(a, 0, me, (x, y, 1 - c), src=self.ins[a])]
                + [self._copy(a, 1 + j, me, (*chip, c), src=self.ins[a]) for j, chip in enumerate(chips)])

    def start(self):
        for a in range(self.n):
            self._mine(a).start()
        for a in range(self.n):
            for cp in self._first(a):
                cp.start()

    def finish(self):
        x, y, c, chips = _place()
        me, sibling = (x, y, c), (x, y, 1 - c)
        passed = []
        for j, chip in enumerate(chips):
            for a in range(self.n):
                self._copy(a, 1 + j, (*chip, c), me).wait_recv()
                fwd = self._copy(a, 4 + j, (*chip, c), sibling)
                fwd.start()
                passed.append(fwd)
        for a in range(self.n):
            self._copy(a, 0, (x, y, 1 - c), me).wait_recv()
            for j, chip in enumerate(chips):
                self._copy(a, 4 + j, (*chip, 1 - c), me).wait_recv()
        for a in range(self.n):
            for cp in self._first(a):
                cp.wait_send()
        for cp in passed:
            cp.wait_send()
        for a in range(self.n):
            self._mine(a).wait()


def _half_exchange(arr, name):
    def body(in_ref, out_ref, send_sems, recv_sems, local_sem):
        x, y, c, _ = _place()
        my_chip = 2 * x + y

        def send(j, wait):
            to_me = (c == x) & (y == j // 2) & (c == j % 2)

            @pl.when(to_me)
            def _():
                local = pltpu.make_async_copy(in_ref.at[j], out_ref.at[my_chip], local_sem)
                local.wait() if wait else local.start()

            @pl.when(jnp.logical_not(to_me))
            def _():
                remote = pltpu.make_async_remote_copy(
                    src_ref=in_ref.at[j], dst_ref=out_ref.at[my_chip], send_sem=send_sems.at[j],
                    recv_sem=recv_sems.at[my_chip], device_id=(c, j // 2, j % 2), device_id_type=MESH)
                remote.wait_send() if wait else remote.start()

        for j in range(4):
            send(j, wait=False)
        for j in range(4):
            send(j, wait=True)
        for k in range(4):
            from_me = (k // 2 == x) & (k % 2 == y) & (c == x)

            @pl.when(jnp.logical_not(from_me))
            def _():
                pltpu.make_async_remote_copy(
                    src_ref=in_ref.at[0], dst_ref=out_ref.at[k], send_sem=send_sems.at[0], recv_sem=recv_sems.at[k],
                    device_id=(k // 2, k % 2, x), device_id_type=MESH).wait_recv()

    return pl.pallas_call(
        body, name=name, out_shape=_sds(arr.shape, arr.dtype), in_specs=[ANY], out_specs=ANY,
        scratch_shapes=[pltpu.SemaphoreType.DMA((4,)), pltpu.SemaphoreType.DMA((4,)), pltpu.SemaphoreType.DMA],
    )(arr)


class _ChipExchange:
    def __init__(self, ins, outs, send_sems, recv_sems, local_sems):
        self.ins, self.outs, self.n = ins, outs, len(ins)
        self.send_sems, self.recv_sems, self.local_sems = send_sems, recv_sems, local_sems

    @staticmethod
    def scratch(n):
        return [pltpu.SemaphoreType.DMA((3, n)), pltpu.SemaphoreType.DMA((3, n)), pltpu.SemaphoreType.DMA((n,))]

    def _local(self, a):
        x, y, _, _ = _place()
        me = 2 * x + y
        return pltpu.make_async_copy(self.ins[a].at[me], self.outs[a].at[me], self.local_sems.at[a])

    def _copies(self, a):
        x, y, c, chips = _place()
        me = 2 * x + y
        return [(pltpu.make_async_remote_copy(
                     src_ref=self.ins[a].at[2 * px + py], dst_ref=self.outs[a].at[me],
                     send_sem=self.send_sems.at[k, a], recv_sem=self.recv_sems.at[k, a],
                     device_id=(px, py, c), device_id_type=MESH),
                 pltpu.make_async_remote_copy(
                     src_ref=self.ins[a].at[me], dst_ref=self.outs[a].at[2 * px + py],
                     send_sem=self.send_sems.at[k, a], recv_sem=self.recv_sems.at[k, a],
                     device_id=(px, py, c), device_id_type=MESH))
                for k, (px, py) in enumerate(chips)]

    def start(self):
        for a in range(self.n):
            self._local(a).start()
        for a in range(self.n):
            for send, _ in self._copies(a):
                send.start()

    def finish(self):
        for a in range(self.n):
            for send, recv in self._copies(a):
                send.wait_send()
                recv.wait_recv()
        for a in range(self.n):
            self._local(a).wait()


def _gathering(body, n_steps, n_s, core_major):
    def wrapped(*refs, n_in, n_out):
        ins, sh_in = refs[:n_in], refs[n_in:n_in + n_s]
        outs, sh_out = refs[n_in + n_s:n_in + n_s + n_out], refs[n_in + n_s + n_out:n_in + 2 * n_s + n_out]
        rest = refs[n_in + 2 * n_s + n_out:]
        gather = _Gather(sh_in, sh_out, *rest[len(rest) - 3:], core_major=core_major)
        i = pl.program_id(0)

        @pl.when(i == 0)
        def _():
            gather.start()

        body(*ins, *outs, *rest[:len(rest) - 3])

        @pl.when(i == n_steps - 1)
        def _():
            gather.finish()

    return wrapped


def _norm_in(x, g_mix, shards, tm):
    T = x.shape[0]
    n_t = T // tm
    n_s = len(shards)

    def norm(x_ref, g_ref, h_ref):
        xv = x_ref[...]
        h_ref[...] = (xv * _rstd(xv) * g_ref[...]).astype(BF16)

    outs = pl.pallas_call(
        functools.partial(_gathering(norm, n_t, n_s, core_major=False), n_in=2, n_out=1), name="norm_in", grid=(n_t,),
        in_specs=[pl.BlockSpec((tm, D_MODEL), lambda i: (i, 0)), pl.BlockSpec((1, D_MODEL), lambda i: (0, 0))]
        + [ANY] * n_s,
        out_specs=[pl.BlockSpec((tm, D_MODEL), lambda i: (i, 0))] + [ANY] * n_s,
        out_shape=[_sds((T, D_MODEL), BF16)] + [_sds((N_DEV,) + a.shape, a.dtype) for a in shards],
        scratch_shapes=_Gather.scratch(n_s),
        compiler_params=pltpu.CompilerParams(dimension_semantics=("arbitrary",)),
    )(x, g_mix, *shards)
    return outs[0], list(outs[1:])


def _mix_in(h1, w_in_t, tm, shards):
    T = h1.shape[0]
    n_t = T // tm
    n_s = len(shards)

    def project(h_ref, w_ref, u_ref):
        u_ref[...] = lax.dot_general(h_ref[...], w_ref[...], NT, preferred_element_type=F32)

    outs = pl.pallas_call(
        functools.partial(_gathering(project, n_t, n_s, core_major=True), n_in=2, n_out=1), name="mix_in", grid=(n_t,),
        in_specs=[pl.BlockSpec((tm, D_MODEL), lambda i: (i, 0)), pl.BlockSpec((D_IN, D_MODEL), lambda i: (0, 0))]
        + [ANY] * n_s,
        out_specs=[pl.BlockSpec((tm, D_IN), lambda i: (i, 0))] + [ANY] * n_s,
        out_shape=[_sds((T, D_IN), F32)] + [_sds((N_DEV,) + a.shape, a.dtype) for a in shards],
        scratch_shapes=_Gather.scratch(n_s),
        compiler_params=pltpu.CompilerParams(dimension_semantics=("arbitrary",)),
    )(h1, w_in_t, *shards)
    return outs[0], list(outs[1:])


def _mixer_fwd(u, x, pv, wa, wx, wp, w_out_b, g_ffn, tm, shards=()):
    T = u.shape[0]
    n_s = len(shards)
    n_t = T // tm

    def body(u_ref, x_ref, pv_ref, wa_in, wx_in, wp_in, wo_ref, gf_ref, *rest):
        sh_in, rest = rest[:n_s], rest[n_s:]
        y_ref, hs_ref, hres_ref, h2_ref, saved_ref = rest[:5]
        sh_out, rest = rest[5:5 + n_s], rest[5 + n_s:]
        e_lru, e_pool, a_s, b_s, hc, wa_ref, wx_ref, wp_ref = rest[:8]
        gather = _Gather(sh_in, sh_out, *rest[8:], core_major=True) if n_s else None
        i = pl.program_id(0)

        @pl.when(i == 0)
        def _():
            if gather:
                gather.start()
            e_lru[pl.ds(0, HALO), :] = jnp.zeros((HALO, LRU_W), F32)
            e_pool[pl.ds(0, HALO), :] = jnp.zeros((HALO, POOL_W), F32)
            hc[...] = jnp.zeros((8, LRU_W), F32)
            _fill_block_diag(wa_ref, wa_in)
            _fill_block_diag(wx_ref, wx_in)
            _fill_block_diag(wp_ref, wp_in)

        e_lru[pl.ds(HALO, tm), :] = u_ref[:, 0:LRU_W]
        e_pool[pl.ds(HALO, tm), :] = u_ref[:, 2 * LRU_W:D_IN]
        pv = pv_ref[...]
        xc, r, ig, la = _lru_gates(e_lru, pv, wa_ref, wx_ref, tm)
        for q, val in enumerate((xc, r, ig, la)):
            saved_ref[:, LRU_W * q:LRU_W * (q + 1)] = val
        a, _, mult, _ = _lru_decay(la)
        a_s[...] = a
        b_s[...] = mult * (ig * xc)
        hc[...] = _scan_tile(a_s, b_s, hs_ref, hc[...], tm, reverse=False)
        gl, _ = _gelu_parts(u_ref[:, LRU_W:2 * LRU_W])
        y_lru = hs_ref[...] * gl
        _, zp, _ = _pool_pre(e_pool, pv, wp_ref, tm, i * tm)
        y_pool = zp * pv[ROW_PS:ROW_PS + 1, :]
        yn = jnp.concatenate([y_lru * _rstd(y_lru) * pv[ROW_GL:ROW_GL + 1, :],
                              y_pool * _rstd(y_pool) * pv[ROW_GP:ROW_GP + 1, :]], axis=1).astype(BF16)
        for b in range(N_DEV):
            y_ref[:, 128 * _y_pos(b):128 * (_y_pos(b) + 1)] = yn[:, 128 * b:128 * (b + 1)]
        hr = x_ref[...] + jnp.dot(y_ref[...], wo_ref[...], preferred_element_type=F32)
        hres_ref[...] = hr
        h2_ref[...] = (hr * _rstd(hr) * gf_ref[...]).astype(BF16)
        e_lru[pl.ds(0, HALO), :] = e_lru[pl.ds(tm, HALO), :]
        e_pool[pl.ds(0, HALO), :] = e_pool[pl.ds(tm, HALO), :]

        if gather:
            @pl.when(i == n_t - 1)
            def _():
                gather.finish()

    full = lambda shape: pl.BlockSpec(shape, lambda i: (0,) * len(shape))
    row = lambda w: pl.BlockSpec((tm, w), lambda i: (i, 0))
    outs = pl.pallas_call(
        body, name="mixer_fwd", grid=(n_t,),
        in_specs=[row(D_IN), row(D_MODEL), full((16, LRU_W)), full((8, 64, 64)), full((8, 64, 64)), full((4, 128, 128)),
                  full((D_MODEL, D_MODEL)), full((1, D_MODEL))] + [ANY] * n_s,
        out_specs=[row(D_MODEL), row(LRU_W), row(D_MODEL), row(D_MODEL), row(4 * LRU_W)] + [ANY] * n_s,
        out_shape=[_sds((T, D_MODEL), BF16), _sds((T, LRU_W), F32), _sds((T, D_MODEL), F32), _sds((T, D_MODEL), BF16),
                   _sds((T, 4 * LRU_W), F32)] + [_sds((N_DEV,) + a.shape, a.dtype) for a in shards],
        scratch_shapes=[pltpu.VMEM((HALO + tm, LRU_W), F32), pltpu.VMEM((HALO + tm, POOL_W), F32),
                        pltpu.VMEM((tm, LRU_W), F32), pltpu.VMEM((tm, LRU_W), F32), pltpu.VMEM((8, LRU_W), F32)]
        + [pltpu.VMEM((2, 256, 256), BF16)] * 3 + (_Gather.scratch(n_s) if n_s else []),
        compiler_params=pltpu.CompilerParams(dimension_semantics=("arbitrary",)),
    )(u, x, pv, wa, wx, wp, w_out_b, g_ffn, *shards)
    return outs[0], outs[1], outs[2], outs[3], outs[4], list(outs[5:])


def _ffn_fwd(hres, h2, w1_b, w3_b, w2_b, g_fin, tgt, tm, tn):
    T = hres.shape[0]
    n_j = D_FF // tn

    def body(hres_ref, h2_ref, w1_ref, w3_ref, w2_ref, gfin_ref, tgt_ref,
             g_ref, v_ref, d3_ref, loss_ref, dgfin_ref, acc):
        i, j = pl.program_id(0), pl.program_id(1)

        @pl.when(j == 0)
        def _():
            acc[...] = jnp.zeros((tm, D_MODEL), F32)

        @pl.when((j == 0) & (i == 0))
        def _():
            loss_ref[...] = jnp.zeros((8, 128), F32)
            dgfin_ref[...] = jnp.zeros((1, D_MODEL), F32)

        h2 = h2_ref[...]
        g = lax.dot_general(h2, w1_ref[...], NT, preferred_element_type=F32)
        v = lax.dot_general(h2, w3_ref[...], NT, preferred_element_type=F32)
        g_ref[...] = g.astype(BF16)
        v_ref[...] = v.astype(BF16)
        ff = ((g * _sigmoid(g)) * v).astype(BF16)
        acc[...] += jnp.dot(ff, w2_ref[...], preferred_element_type=F32)

        @pl.when(j == n_j - 1)
        def _():
            h3 = hres_ref[...] + acc[...]
            rstd = _rstd(h3)
            xh = h3 * rstd
            gfin = gfin_ref[...]
            err = xh * gfin - tgt_ref[...]
            loss_ref[...] += 0.5 * jnp.sum(jnp.mean(err * err, axis=-1, keepdims=True))
            dout = err * (1.0 / D_MODEL)
            dx, dgain = _rms_bwd(dout, xh, rstd, gfin)
            d3_ref[...] = dx
            dgfin_ref[...] += dgain

    row = lambda w: pl.BlockSpec((tm, w), lambda i, j: (i, 0))
    const = lambda shape: pl.BlockSpec(shape, lambda i, j: (0,) * len(shape))
    return pl.pallas_call(
        body, name="ffn_fwd", grid=(T // tm, n_j),
        in_specs=[row(D_MODEL), row(D_MODEL),
                  pl.BlockSpec((tn, D_MODEL), lambda i, j: (j, 0)), pl.BlockSpec((tn, D_MODEL), lambda i, j: (j, 0)),
                  pl.BlockSpec((tn, D_MODEL), lambda i, j: (j, 0)), const((1, D_MODEL)), row(D_MODEL)],
        out_specs=[pl.BlockSpec((tm, tn), lambda i, j: (i, j)), pl.BlockSpec((tm, tn), lambda i, j: (i, j)),
                   row(D_MODEL), const((8, 128)), const((1, D_MODEL))],
        out_shape=[_sds((T, D_FF), BF16), _sds((T, D_FF), BF16),
                   _sds((T, D_MODEL), F32), _sds((8, 128), F32), _sds((1, D_MODEL), F32)],
        scratch_shapes=[pltpu.VMEM((tm, D_MODEL), F32)],
        compiler_params=pltpu.CompilerParams(dimension_semantics=("arbitrary", "arbitrary")),
    )(hres, h2, w1_b, w3_b, w2_b, g_fin, tgt)


def _ffn_bwd(d3, g, v, w1_b, w3_b, w2_b, hres, g_ffn, tm, tn):
    T = d3.shape[0]
    n_j = D_FF // tn

    def body(d3_ref, g_ref, v_ref, w1_ref, w3_ref, w2_ref, hres_ref, gf_ref,
             dg_ref, dv_ref, ff_ref, d2_ref, dgffn_ref, acc):
        i, j = pl.program_id(0), pl.program_id(1)

        @pl.when(j == 0)
        def _():
            acc[...] = jnp.zeros((tm, D_MODEL), F32)

        @pl.when((j == 0) & (i == 0))
        def _():
            dgffn_ref[...] = jnp.zeros((1, D_MODEL), F32)

        dff = lax.dot_general(d3_ref[...].astype(BF16), w2_ref[...], NT, preferred_element_type=F32)
        gv = g_ref[...].astype(F32)
        vv = v_ref[...].astype(F32)
        sg = _sigmoid(gv)
        sl = gv * sg
        dgb = (dff * vv * (sg * (1.0 + gv * (1.0 - sg)))).astype(BF16)
        dvb = (dff * sl).astype(BF16)
        dg_ref[...] = dgb
        dv_ref[...] = dvb
        ff_ref[...] = (sl * vv).astype(BF16)
        acc[...] += (jnp.dot(dgb, w1_ref[...], preferred_element_type=F32)
                     + jnp.dot(dvb, w3_ref[...], preferred_element_type=F32))

        @pl.when(j == n_j - 1)
        def _():
            hr = hres_ref[...]
            rstd = _rstd(hr)
            dx, dgain = _rms_bwd(acc[...], hr * rstd, rstd, gf_ref[...])
            d2_ref[...] = d3_ref[...] + dx
            dgffn_ref[...] += dgain

    row = lambda w: pl.BlockSpec((tm, w), lambda i, j: (i, 0))
    tile = pl.BlockSpec((tm, tn), lambda i, j: (i, j))
    const = lambda shape: pl.BlockSpec(shape, lambda i, j: (0,) * len(shape))
    return pl.pallas_call(
        body, name="ffn_bwd", grid=(T // tm, n_j),
        in_specs=[row(D_MODEL), tile, tile,
                  pl.BlockSpec((tn, D_MODEL), lambda i, j: (j, 0)), pl.BlockSpec((tn, D_MODEL), lambda i, j: (j, 0)),
                  pl.BlockSpec((tn, D_MODEL), lambda i, j: (j, 0)), row(D_MODEL), const((1, D_MODEL))],
        out_specs=[tile, tile, tile, row(D_MODEL), const((1, D_MODEL))],
        out_shape=[_sds((T, D_FF), BF16), _sds((T, D_FF), BF16), _sds((T, D_FF), BF16),
                   _sds((T, D_MODEL), F32), _sds((1, D_MODEL), F32)],
        scratch_shapes=[pltpu.VMEM((tm, D_MODEL), F32)],
        compiler_params=pltpu.CompilerParams(dimension_semantics=("arbitrary", "arbitrary")),
    )(d3, g, v, w1_b, w3_b, w2_b, hres, g_ffn)


def _at_b_pair(a, b, c_arr, name, tk, gather=()):
    T, M = a.shape
    N = b.shape[1]
    hm, n_k = M // 2, T // tk
    n_g = len(gather)

    def body(c_ref, a_ref, b_ref, *rest):
        g_in, o_ref, rest = rest[:n_g], rest[n_g], rest[n_g + 1:]
        g_out, rest = rest[:n_g], rest[n_g:]
        acc, landed, send_sem, recv_sem = rest[:4]
        ag = _Gather(g_in, g_out, *rest[4:]) if n_g else None
        ph, k = pl.program_id(0), pl.program_id(1)

        def hand_over():
            x, y, c, _ = _place()
            return pltpu.make_async_remote_copy(
                src_ref=acc.at[0], dst_ref=landed, send_sem=send_sem, recv_sem=recv_sem,
                device_id=(x, y, 1 - c), device_id_type=MESH)

        if ag:
            @pl.when((ph == 0) & (k == 0))
            def _():
                ag.start()

        @pl.when(k == 0)
        def _():
            acc[ph] = jnp.zeros((hm, N), F32)

        acc[ph] += lax.dot_general(a_ref[...].astype(BF16), b_ref[...].astype(BF16), TN, preferred_element_type=F32)

        @pl.when((ph == 0) & (k == n_k - 1))
        def _():
            hand_over().start()

        @pl.when((ph == 1) & (k == n_k - 1))
        def _():
            copy = hand_over()
            copy.wait_recv()
            o_ref[...] = (acc[1] + landed[...]).astype(BF16)
            copy.wait_send()
            if ag:
                ag.finish()

    outs = pl.pallas_call(
        body, name=name,
        grid_spec=pltpu.PrefetchScalarGridSpec(
            num_scalar_prefetch=1, grid=(2, n_k),
            in_specs=[pl.BlockSpec((tk, hm), lambda ph, k, c_ref: (k, (ph + 1 - c_ref[0]) % 2)),
                      pl.BlockSpec((tk, N), lambda ph, k, c_ref: (k, 0))] + [ANY] * n_g,
            out_specs=[pl.BlockSpec((hm, N), lambda ph, k, c_ref: (0, 0))] + [ANY] * n_g,
            scratch_shapes=[pltpu.VMEM((2, hm, N), F32), pltpu.VMEM((hm, N), F32),
                            pltpu.SemaphoreType.DMA, pltpu.SemaphoreType.DMA] + (_Gather.scratch(n_g) if n_g else [])),
        out_shape=[_sds((hm, N), BF16)] + [_sds((N_DEV,) + g.shape, g.dtype) for g in gather],
        compiler_params=pltpu.CompilerParams(dimension_semantics=("arbitrary", "arbitrary")),
    )(c_arr, a, b, *gather)
    return (outs[0], list(outs[1:])) if n_g else outs[0]


def _mixer_bwd(d2, u, hs, saved, pv, wa, wx, wp, w_out_b, tm, chip_sums=()):
    T = u.shape[0]
    n_t = T // tm
    n_x = len(chip_sums)

    def body(d2_ref, u_ref, uh_ref, hs_ref, hh_ref, saved_ref, pv_ref, wa_in, wx_in, wp_in, wo_ref, *rest):
        x_in, rest = rest[:n_x], rest[n_x:]
        du_ref, sg_ref = rest[:2]
        x_out, rest = rest[2:2 + n_x], rest[2 + n_x:]
        e_pool, e_h, a_s, b_s, dh_s, mu_s, f_x, f_p, mc, cx, cp = rest[:11]
        wa_ref, wx_ref, wp_ref, vacc_ref, dwa_ref, dwx_ref, dwp_ref = rest[11:18]
        exchange = _ChipExchange(x_in, x_out, *rest[18:]) if n_x else None
        s = pl.program_id(0)
        it = n_t - 1 - s

        @pl.when(s == 0)
        def _():
            if exchange:
                exchange.start()
            mc[...] = jnp.zeros((8, LRU_W), F32)
            cx[...] = jnp.zeros((8, LRU_W), F32)
            cp[...] = jnp.zeros((HALO, POOL_W), F32)
            vacc_ref[...] = jnp.zeros((16, LRU_W), F32)
            dwa_ref[...] = jnp.zeros((2, 256, 256), F32)
            dwx_ref[...] = jnp.zeros((2, 256, 256), F32)
            dwp_ref[...] = jnp.zeros((2, 256, 256), F32)
            _fill_block_diag(wa_ref, wa_in)
            _fill_block_diag(wx_ref, wx_in)
            _fill_block_diag(wp_ref, wp_in)

        first = it == 0
        e_pool[pl.ds(0, HALO), :] = jnp.where(first, 0.0, uh_ref[...])
        e_pool[pl.ds(HALO, tm), :] = u_ref[:, 2 * LRU_W:D_IN]
        e_h[pl.ds(0, 8), :] = jnp.where(first, 0.0, hh_ref[...])
        e_h[pl.ds(8, tm), :] = hs_ref[...]
        pv = pv_ref[...]
        saved = lambda q: saved_ref[:, LRU_W * q:LRU_W * (q + 1)]

        dyn = lax.dot_general(d2_ref[...].astype(BF16), wo_ref[...], NT, preferred_element_type=F32)
        dyn = jnp.concatenate([dyn[:, 128 * _y_pos(b):128 * (_y_pos(b) + 1)] for b in range(N_DEV)], axis=1)

        h = hs_ref[...]
        ug = u_ref[:, LRU_W:2 * LRU_W]
        gl, dgl = _gelu_parts(ug)
        y_lru = h * gl
        rstd_l = _rstd(y_lru)
        dy_lru, d_gain_l = _rms_bwd(dyn[:, 0:LRU_W], y_lru * rstd_l, rstd_l, pv[ROW_GL:ROW_GL + 1, :])
        dh = dy_lru * gl
        du_ref[:, LRU_W:2 * LRU_W] = (dy_lru * h * dgl).astype(BF16)
        a_s[...] = jnp.exp(saved(3))
        b_s[...] = a_s[...] * dh
        dh_s[...] = dh
        mu_s[pl.ds(tm, 8), :] = mc[...]
        mc[...] = _scan_tile(a_s, b_s, mu_s, mc[...], tm, reverse=True)
        xc, r, ig = saved(0), saved(1), saved(2)
        a, om, mult, rmult = _lru_decay(saved(3))
        lam_t = dh_s[...] + mu_s[pl.ds(1, tm), :]
        da = lam_t * e_h[pl.ds(7, tm), :]
        dmult = lam_t * (ig * xc)
        di = lam_t * (mult * xc)
        dxc = lam_t * (mult * ig)
        dla = da * a - jnp.where(om > 1e-12, dmult * ((a * a) * rmult), 0.0)
        dra = (dla * (-LRU_C * _softplus_neg_lambda(pv))) * (r * (1.0 - r))
        dia = di * (ig * (1.0 - ig))
        drab = dra.astype(BF16)
        diab = dia.astype(BF16)
        xcb = xc.astype(BF16)
        dxc = dxc + _bd_t(drab, wa_ref) + _bd_t(diab, wx_ref)
        dwa_ref[...] += _bd_grad(xcb, drab)
        dwx_ref[...] += _bd_grad(xcb, diab)
        sig_neg_lam = _sigmoid(-pv[ROW_LAM:ROW_LAM + 1, :])
        d_lam = jnp.sum(dla * r, axis=0, keepdims=True) * (LRU_C * sig_neg_lam)

        f_x[pl.ds(0, tm), :] = dxc
        f_x[pl.ds(tm, 8), :] = cx[...]
        du_lru = jnp.zeros((tm, LRU_W), F32)
        u_lru = u_ref[:, 0:LRU_W]
        d_cw = []
        for k in range(4):
            later = f_x[pl.ds(3 - k, tm), :]
            du_lru = du_lru + later * pv[ROW_CW + k:ROW_CW + k + 1, :]
            d_cw.append(jnp.sum(later * u_lru, axis=0, keepdims=True))
        du_ref[:, 0:LRU_W] = du_lru.astype(BF16)
        cx[...] = f_x[pl.ds(0, 8), :]

        pooled_b, zp, inv_cnts = _pool_pre(e_pool, pv, wp_ref, tm, it * tm)
        ps = pv[ROW_PS:ROW_PS + 1, :]
        y_pool = zp * ps
        rstd_p = _rstd(y_pool)
        dy_pool, d_gain_p = _rms_bwd(dyn[:, LRU_W:D_MODEL], y_pool * rstd_p, rstd_p, pv[ROW_GP:ROW_GP + 1, :])
        dz = dy_pool * ps
        dzb = dz.astype(BF16)
        dwp_ref[...] += _bd_grad(pooled_b, dzb)
        dpooled = _bd_t(dzb, wp_ref)
        for g, w in enumerate(POOL_WINDOWS):
            f_p[pl.ds(0, tm), pl.ds(128 * g, 128)] = _over_count(dpooled[:, 128 * g:128 * (g + 1)], w, inv_cnts[g])
        f_p[pl.ds(tm, HALO), :] = cp[...]
        for g, w in enumerate(POOL_WINDOWS):
            acc = _window_sum(f_p[:, pl.ds(128 * g, 128)], w, back=False)[0:tm, :]
            du_ref[:, 2 * LRU_W + 128 * g:2 * LRU_W + 128 * (g + 1)] = (
                acc - dpooled[:, 128 * g:128 * (g + 1)]).astype(BF16)
        cp[...] = f_p[pl.ds(0, HALO), :]

        rows = d_cw + [
            jnp.sum(dxc, axis=0, keepdims=True),
            jnp.sum(dra, axis=0, keepdims=True),
            jnp.sum(dia, axis=0, keepdims=True),
            d_lam,
            jnp.sum(dz, axis=0, keepdims=True),
            jnp.sum(dy_pool * zp, axis=0, keepdims=True),
            d_gain_l, d_gain_p,
            jnp.zeros((4, LRU_W), F32),
        ]
        vacc_ref[...] += jnp.concatenate(rows, axis=0)

        @pl.when(s == n_t - 1)
        def _():
            sg_ref[SG_VEC:SG_VEC + 16, :] = vacc_ref[:, 0:256]
            sg_ref[SG_VEC + 16:SG_VEC + 32, :] = vacc_ref[:, 256:512]
            for half in range(2):
                sg_ref[SG_WA + 64 * half:SG_WA + 64 * (half + 1), :] = _diag_pack(dwa_ref[half], 64)
                sg_ref[SG_WX + 64 * half:SG_WX + 64 * (half + 1), :] = _diag_pack(dwx_ref[half], 64)
                sg_ref[SG_WP + 128 * half:SG_WP + 128 * (half + 1), :] = _diag_pack(dwp_ref[half], 128)
            if exchange:
                exchange.finish()

    rev = lambda w: pl.BlockSpec((tm, w), lambda s: (n_t - 1 - s, 0))
    full = lambda shape: pl.BlockSpec(shape, lambda s: (0,) * len(shape))
    outs = pl.pallas_call(
        body, name="mixer_bwd", grid=(n_t,),
        in_specs=[rev(D_MODEL), rev(D_IN),
                  pl.BlockSpec((HALO, POOL_W), lambda s: (jnp.maximum((n_t - 1 - s) * (tm // HALO) - 1, 0), 2)),
                  rev(LRU_W),
                  pl.BlockSpec((8, LRU_W), lambda s: (jnp.maximum((n_t - 1 - s) * (tm // 8) - 1, 0), 0)),
                  rev(4 * LRU_W), full((16, LRU_W)), full((8, 64, 64)), full((8, 64, 64)), full((4, 128, 128)),
                  full((D_MODEL, D_MODEL))] + [ANY] * n_x,
        out_specs=[rev(D_IN), full((SG_ROWS, 256))] + [ANY] * n_x,
        out_shape=[_sds((T, D_IN), BF16), _sds((SG_ROWS, 256), F32)] + [_sds(a.shape, a.dtype) for a in chip_sums],
        scratch_shapes=[pltpu.VMEM((HALO + tm, POOL_W), F32),
                        pltpu.VMEM((8 + tm, LRU_W), F32)] + [pltpu.VMEM((tm, LRU_W), F32)] * 3 + [
                        pltpu.VMEM((tm + 8, LRU_W), F32), pltpu.VMEM((tm + 8, LRU_W), F32),
                        pltpu.VMEM((tm + HALO, POOL_W), F32), pltpu.VMEM((8, LRU_W), F32),
                        pltpu.VMEM((8, LRU_W), F32), pltpu.VMEM((HALO, POOL_W), F32)]
        + [pltpu.VMEM((2, 256, 256), BF16)] * 3 + [pltpu.VMEM((16, LRU_W), F32)] + [pltpu.VMEM((2, 256, 256), F32)] * 3
        + (_ChipExchange.scratch(n_x) if n_x else []),
        compiler_params=pltpu.CompilerParams(dimension_semantics=("arbitrary",)),
    )(d2, u, u, hs, hs, saved, pv, wa, wx, wp, w_out_b, *chip_sums)
    return outs[0], outs[1], list(outs[2:])


def _mix_in_bwd(du, x, d2, w_in_t, g_mix, tm):
    T = x.shape[0]

    def body(du_ref, x_ref, d2_ref, w_ref, g_ref, dx_ref, dg_ref):
        @pl.when(pl.program_id(0) == 0)
        def _():
            dg_ref[...] = jnp.zeros((1, D_MODEL), F32)

        dh = jnp.dot(du_ref[...], w_ref[...], preferred_element_type=F32)
        xv = x_ref[...]
        rstd = _rstd(xv)
        dx, dgain = _rms_bwd(dh, xv * rstd, rstd, g_ref[...])
        dx_ref[...] = d2_ref[...] + dx
        dg_ref[...] += dgain

    row = lambda w: pl.BlockSpec((tm, w), lambda i: (i, 0))
    const = lambda shape: pl.BlockSpec(shape, lambda i: (0,) * len(shape))
    return pl.pallas_call(
        body, name="mix_in_bwd", grid=(T // tm,),
        in_specs=[row(D_IN), row(D_MODEL), row(D_MODEL), const((D_IN, D_MODEL)), const((1, D_MODEL))],
        out_specs=[row(D_MODEL), const((1, D_MODEL))],
        out_shape=[_sds((T, D_MODEL), F32), _sds((1, D_MODEL), F32)],
        compiler_params=pltpu.CompilerParams(dimension_semantics=("arbitrary",)),
    )(du, x, d2, w_in_t, g_mix)


def _adamw(w, g, m, v):
    m = ADAM_B1 * m + (1.0 - ADAM_B1) * g
    v = ADAM_B2 * v + (1.0 - ADAM_B2) * (g * g)
    m_hat = m / (1.0 - ADAM_B1 ** ADAM_STEP)
    v_hat = v / (1.0 - ADAM_B2 ** ADAM_STEP)
    delta = -ADAM_LR * (m_hat / (jnp.sqrt(v_hat) + ADAM_EPS) + ADAM_WD * w)
    return delta, m, v


def _adam_shards(ws, ms, vs, parts):
    n = len(ws)
    n_blk = [w.shape[0] // ADAM_ROWS for w in ws]

    def body(*refs):
        w_refs, m_refs, v_refs, p_refs, outs = (refs[:n], refs[n:2 * n], refs[2 * n:3 * n], refs[3 * n:4 * n],
                                                refs[4 * n:])
        i = pl.program_id(0)
        for a in range(n):
            @pl.when(i < n_blk[a])
            def _(a=a):
                g = p_refs[a][0].astype(F32)
                for j in range(1, 4):
                    g = g + p_refs[a][j].astype(F32)
                delta, new_m, new_v = _adamw(w_refs[a][...], g, m_refs[a][...], v_refs[a][...])
                for kind, val in enumerate((g, delta, new_m, new_v)):
                    outs[4 * a + kind][...] = val

    blk = lambda a: pl.BlockSpec((ADAM_ROWS, D_MODEL), lambda i: (jnp.minimum(i, n_blk[a] - 1), 0))
    part_blk = lambda a: pl.BlockSpec((4, ADAM_ROWS, D_MODEL), lambda i: (0, jnp.minimum(i, n_blk[a] - 1), 0))
    res = pl.pallas_call(
        body, name="adam_shards", grid=(max(n_blk),),
        in_specs=[blk(a) for a in range(n)] * 3 + [part_blk(a) for a in range(n)],
        out_specs=[blk(a) for a in range(n) for _ in range(4)],
        out_shape=[_sds(w.shape, F32) for w in ws for _ in range(4)],
        compiler_params=pltpu.CompilerParams(dimension_semantics=("arbitrary",)),
    )(*ws, *ms, *vs, *parts)
    return [tuple(res[4 * a:4 * a + 4]) for a in range(n)]


SMALL_PARAMS = [("norm_mix_g", (1, D_MODEL)), ("conv_w", (1, 4, 64)), ("conv_b", (1, LRU_W)),
                ("gate_a_w", (1, 8, 64, 64)), ("gate_a_b", (1, LRU_W)), ("gate_x_w", (1, 8, 64, 64)),
                ("gate_x_b", (1, LRU_W)), ("lru_lambda", (1, LRU_W)), ("pool_w", (1, 4, 128, 128)),
                ("pool_b", (1, POOL_W)), ("pool_scale", (1, POOL_W)), ("norm_lru_g", (1, LRU_W)),
                ("norm_pool_g", (1, POOL_W)), ("norm_ffn_g", (1, D_MODEL)), ("final_norm_g", (1, D_MODEL))]
VEC_ROW = dict(conv_b=ROW_CB, gate_a_b=ROW_BA, gate_x_b=ROW_BX, lru_lambda=ROW_LAM, pool_b=ROW_PB, pool_scale=ROW_PS,
               norm_lru_g=ROW_GL, norm_pool_g=ROW_GP)
WHOLE = (Ellipsis,)


def _unpack_mixer_grads(sg, dev):
    vec = jnp.concatenate([sg[SG_VEC:SG_VEC + 16], sg[SG_VEC + 16:SG_VEC + 32]], axis=1)
    out = {nm: [(WHOLE, vec[r:r + 1])] for nm, r in VEC_ROW.items()}
    own = jnp.zeros((4, 64), F32)
    for d in range(N_DEV):
        own = jnp.where(dev == d, vec[ROW_CW:ROW_CW + 4, 64 * d:64 * (d + 1)], own)
    out["conv_w"] = [((0,), own)]
    for nm, row0 in (("gate_a_w", SG_WA), ("gate_x_w", SG_WX)):
        out[nm] = [((0, b), sg[row0 + 64 * (b // 4):row0 + 64 * (b // 4 + 1), 64 * (b % 4):64 * (b % 4 + 1)])
                   for b in range(8)]
    out["pool_w"] = [((0, b), sg[SG_WP + 128 * (b // 2):SG_WP + 128 * (b // 2 + 1), 128 * (b % 2):128 * (b % 2 + 1)])
                     for b in range(4)]
    return out


def _adam_small(parts, w, m, v):
    names = [nm for nm, _ in SMALL_PARAMS]
    n = len(names)

    def body(sg_ref, gm_ref, gf_ref, gn_ref, ls_ref, *rest):
        w_refs, m_refs, v_refs, outs = rest[:n], rest[n:2 * n], rest[2 * n:3 * n], rest[3 * n:]
        dev = 4 * lax.axis_index("x") + 2 * lax.axis_index("y") + lax.axis_index("c")

        def total(ref):
            acc = ref[0]
            for d in range(1, N_DEV):
                acc = acc + ref[d]
            return acc

        pieces = _unpack_mixer_grads(total(sg_ref), dev)
        pieces["norm_mix_g"] = [(WHOLE, total(gm_ref))]
        pieces["norm_ffn_g"] = [(WHOLE, total(gf_ref))]
        pieces["final_norm_g"] = [(WHOLE, total(gn_ref))]
        for i, nm in enumerate(names):
            for idx, g in pieces[nm]:
                delta, new_m, new_v = _adamw(w_refs[i][idx], g, m_refs[i][idx], v_refs[i][idx])
                for kind, val in enumerate((g, delta, new_m, new_v)):
                    outs[4 * i + kind][idx] = val
        outs[4 * n][...] = total(ls_ref)

    shapes = [_sds(shape, F32) for _, shape in SMALL_PARAMS for _ in range(4)] + [_sds((8, 128), F32)]
    res = pl.pallas_call(body, name="adam_small", out_shape=shapes)(
        *parts, *[w[nm] for nm in names], *[m[nm] for nm in names], *[v[nm] for nm in names])
    return {nm: tuple(res[4 * i:4 * i + 4]) for i, nm in enumerate(names)}, res[4 * n][0, 0]


def _vec_rows(conv_w_full, conv_b, ba, bx, lam, pb, ps, gl, gp):
    return jnp.concatenate([conv_w_full, conv_b, ba, bx, lam, pb, ps, gl, gp, jnp.zeros((4, LRU_W), F32)], axis=0)


WEIGHT_ORDER = ['norm_mix_g', 'w_in', 'conv_w', 'conv_b', 'gate_a_w', 'gate_a_b', 'gate_x_w', 'gate_x_b', 'lru_lambda',
                'pool_w', 'pool_b', 'pool_scale', 'norm_lru_g', 'norm_pool_g', 'w_out', 'norm_ffn_g', 'ffn_w1', 'ffn_w3',
                'ffn_w2', 'final_norm_g']


def kernel(x, norm_mix_g, w_in, conv_w, conv_b, gate_a_w, gate_a_b, gate_x_w, gate_x_b, lru_lambda, pool_w, pool_b, pool_scale, norm_lru_g, norm_pool_g, w_out, norm_ffn_g, ffn_w1, ffn_w3, ffn_w2, final_norm_g, loss_target, m_norm_mix_g, m_w_in, m_conv_w, m_conv_b, m_gate_a_w, m_gate_a_b, m_gate_x_w, m_gate_x_b, m_lru_lambda, m_pool_w, m_pool_b, m_pool_scale, m_norm_lru_g, m_norm_pool_g, m_w_out, m_norm_ffn_g, m_ffn_w1, m_ffn_w3, m_ffn_w2, m_final_norm_g, v_norm_mix_g, v_w_in, v_conv_w, v_conv_b, v_gate_a_w, v_gate_a_b, v_gate_x_w, v_gate_x_b, v_lru_lambda, v_pool_w, v_pool_b, v_pool_scale, v_norm_lru_g, v_norm_pool_g, v_w_out, v_norm_ffn_g, v_ffn_w1, v_ffn_w3, v_ffn_w2, v_final_norm_g):
    ac = lax.axis_index("c")
    tm, tmx, tn, tk = 512, 512, 1408, 1024
    tm_in = 1024
    xs, tgt = x[0], loss_target[0]
    g_fin = final_norm_g.reshape(1, D_MODEL)
    c_arr = jnp.reshape(ac, (1,)).astype(jnp.int32)

    tr = lambda w: jnp.swapaxes(w[0], 0, 1)
    own = lambda w: w[0]
    bf = lambda a: a.astype(BF16)

    h1, (g_in, g_conv) = _norm_in(xs, norm_mix_g, [bf(tr(w_in)), conv_w[0]], tm_in)
    w_in_t = g_in.reshape(D_IN, D_MODEL)
    u, (g_out,) = _mix_in(h1, w_in_t, tm, shards=[bf(own(w_out))])
    conv_w_full = g_conv.transpose(1, 0, 2).reshape(4, LRU_W)
    pv = _vec_rows(conv_w_full, conv_b, gate_a_b, gate_x_b, lru_lambda, pool_b, pool_scale, norm_lru_g, norm_pool_g)
    wa, wx, wp = gate_a_w[0], gate_x_w[0], pool_w[0]
    w_out_b = g_out.reshape(D_MODEL, D_MODEL)
    y, hs, hres, h2, saved, (g_w1, g_w3, g_w2) = _mixer_fwd(
        u, xs, pv, wa, wx, wp, w_out_b, norm_ffn_g, tmx, shards=[bf(tr(ffn_w1)), bf(tr(ffn_w3)), bf(own(ffn_w2))])
    w1_t, w3_t, w2_b = g_w1.reshape(D_FF, D_MODEL), g_w3.reshape(D_FF, D_MODEL), g_w2.reshape(D_FF, D_MODEL)
    g, v, d3, loss_acc, d_gfin = _ffn_fwd(hres, h2, w1_t, w3_t, w2_b, g_fin, tgt, tm, tn)

    dg, dv, ff, d2, d_gffn = _ffn_bwd(d3, g, v, w1_t, w3_t, w2_b, hres, norm_ffn_g, tm, tn)
    chips = lambda a: a.reshape(4, a.shape[0] // 4, a.shape[1])
    early_sums = [chips(_at_b_pair(y, d2, c_arr, "grad_w_out", 2 * tk)), chips(_at_b_pair(dg, h2, c_arr, "grad_w1", tk)),
                  chips(_at_b_pair(dv, h2, c_arr, "grad_w3", tk)), chips(_at_b_pair(ff, d3, c_arr, "grad_w2", tk))]
    du, d_mixer, early_parts = _mixer_bwd(d2, u, hs, saved, pv, wa, wx, wp, w_out_b, tmx, chip_sums=early_sums)
    grad_x, d_gmix = _mix_in_bwd(du, xs, d2, w_in_t, norm_mix_g, tm_in)
    d_win, small_parts = _at_b_pair(du, h1, c_arr, "grad_w_in", 2 * tk,
                                    gather=[d_mixer, d_gmix, d_gffn, d_gfin, loss_acc])
    parts = [_half_exchange(chips(d_win), "grads_to_chips_w_in")] + list(early_parts)

    res = {}
    shard_w = dict(w_in=(w_in, m_w_in, v_w_in, tr), w_out=(w_out, m_w_out, v_w_out, own),
                   ffn_w1=(ffn_w1, m_ffn_w1, v_ffn_w1, tr), ffn_w3=(ffn_w3, m_ffn_w3, v_ffn_w3, tr),
                   ffn_w2=(ffn_w2, m_ffn_w2, v_ffn_w2, own))
    shard_res = _adam_shards([view(w) for w, _, _, view in shard_w.values()],
                             [view(m) for _, m, _, view in shard_w.values()],
                             [view(v) for _, _, v, view in shard_w.values()], parts)
    for (nm, (_, _, _, view)), outs in zip(shard_w.items(), shard_res):
        res[nm] = [(jnp.swapaxes(o, 0, 1) if view is tr else o)[None] for o in outs]

    row = lambda a: a.reshape(1, D_MODEL)
    small = lambda gm, cw, cb, wa_, ba, wx_, bx, lam, pw, pb, ps, gl, gp, gf, gn: dict(
        norm_mix_g=gm, conv_w=cw, conv_b=cb, gate_a_w=wa_, gate_a_b=ba, gate_x_w=wx_, gate_x_b=bx, lru_lambda=lam,
        pool_w=pw, pool_b=pb, pool_scale=ps, norm_lru_g=gl, norm_pool_g=gp, norm_ffn_g=gf, final_norm_g=row(gn))
    small_res, loss = _adam_small(
        small_parts,
        small(norm_mix_g, conv_w, conv_b, gate_a_w, gate_a_b, gate_x_w, gate_x_b, lru_lambda, pool_w, pool_b,
              pool_scale, norm_lru_g, norm_pool_g, norm_ffn_g, final_norm_g),
        small(m_norm_mix_g, m_conv_w, m_conv_b, m_gate_a_w, m_gate_a_b, m_gate_x_w, m_gate_x_b, m_lru_lambda, m_pool_w,
              m_pool_b, m_pool_scale, m_norm_lru_g, m_norm_pool_g, m_norm_ffn_g, m_final_norm_g),
        small(v_norm_mix_g, v_conv_w, v_conv_b, v_gate_a_w, v_gate_a_b, v_gate_x_w, v_gate_x_b, v_lru_lambda, v_pool_w,
              v_pool_b, v_pool_scale, v_norm_lru_g, v_norm_pool_g, v_norm_ffn_g, v_final_norm_g))
    for nm, outs in small_res.items():
        res[nm] = [o.reshape(D_MODEL) for o in outs] if nm == "final_norm_g" else list(outs)

    out = [loss, grad_x[None]]
    for kind in range(4):
        out += [res[nm][kind] for nm in WEIGHT_ORDER]
    return tuple(out)
```

```python
import functools

import jax
import jax.numpy as jnp
from jax import lax
from jax.experimental import pallas as pl
from jax.experimental.pallas import tpu as pltpu

F32 = jnp.float32
BF16 = jnp.bfloat16

D_MODEL = 1024
LRU_W = 512
POOL_W = 512
D_IN = 1536
D_FF = 2816
POOL_WINDOWS = (2, 4, 8, 16)
EPS = 1e-6
LRU_C = 8.0
N_DEV = 8
HALO = 16
SCAN_UNROLL = 8
ADAM_ROWS = 32
FF_CHUNKS = ((0, 1536), (1536, 2816))

ADAM_LR = 0.001
ADAM_B1 = 0.9
ADAM_B2 = 0.999
ADAM_EPS = 1e-08
ADAM_WD = 0.01
ADAM_STEP = 10

ROW_CW, ROW_CB, ROW_BA, ROW_BX, ROW_LAM, ROW_PB, ROW_PS, ROW_GL, ROW_GP = 0, 4, 5, 6, 7, 8, 9, 10, 11
SG_VEC, SG_WA, SG_WX, SG_WP, SG_ROWS = 0, 32, 160, 288, 544

NT = (((1,), (1,)), ((), ()))
TN = (((0,), (0,)), ((), ()))


def _sds(shape, dtype):
    return jax.ShapeDtypeStruct(shape, dtype)


def _sigmoid(x):
    return 0.5 * jnp.tanh(0.5 * x) + 0.5


def _gelu_parts(x):
    c = 0.7978845608028654
    inner = c * (x + 0.044715 * (x * x * x))
    th = jnp.tanh(inner)
    g = 0.5 * x * (1.0 + th)
    dg = 0.5 * (1.0 + th) + 0.5 * x * (1.0 - th * th) * (c * (1.0 + 3.0 * 0.044715 * (x * x)))
    return g, dg


def _window_sum(ext, w, back):
    n = ext.shape[0]
    s, k = ext, 1
    while k < w:
        s = s + pltpu.roll(s, k if back else n - k, 0)
        k *= 2
    return s


def _rstd(x):
    return lax.rsqrt(jnp.mean(x * x, axis=-1, keepdims=True) + EPS)


def _rms_bwd(dy, xhat, rstd, gain):
    dxh = dy * gain
    dx = rstd * (dxh - xhat * jnp.mean(dxh * xhat, axis=-1, keepdims=True))
    return dx, jnp.sum(dy * xhat, axis=0, keepdims=True)


def _bd(xb, w_ref):
    return jnp.concatenate(
        [jnp.dot(xb[:, :256], w_ref[0], preferred_element_type=F32),
         jnp.dot(xb[:, 256:], w_ref[1], preferred_element_type=F32)], axis=1)


def _bd_t(xb, w_ref):
    return jnp.concatenate(
        [lax.dot_general(xb[:, :256], w_ref[0], NT, preferred_element_type=F32),
         lax.dot_general(xb[:, 256:], w_ref[1], NT, preferred_element_type=F32)], axis=1)


def _bd_grad(xb, db):
    return jnp.stack(
        [lax.dot_general(xb[:, :256], db[:, :256], TN, preferred_element_type=F32),
         lax.dot_general(xb[:, 256:], db[:, 256:], TN, preferred_element_type=F32)], axis=0)


def _fill_block_diag(dst, src_ref):
    n, k, _ = src_ref.shape
    dst[...] = jnp.zeros(dst.shape, BF16)
    for b in range(n):
        p, q = divmod(b, 256 // k)
        dst[p, q * k:(q + 1) * k, q * k:(q + 1) * k] = src_ref[b].astype(BF16)


def _diag_pack(w, k):
    lane = lax.broadcasted_iota(jnp.int32, (k, 256), 1)
    out = w[0:k]
    for q in range(1, 256 // k):
        out = jnp.where(lane >= q * k, w[q * k:(q + 1) * k], out)
    return out


def _y_pos(b):
    return 4 * (b % 2) + b // 2


def _softplus_neg_lambda(pv):
    z = -pv[ROW_LAM:ROW_LAM + 1, :]
    return jnp.maximum(z, 0.0) + jnp.log(1.0 + jnp.exp(-jnp.abs(z)))


def _lru_gates(e_lru, pv, wa_ref, wx_ref, tm):
    xc = pv[ROW_CB:ROW_CB + 1, :]
    for k in range(4):
        xc = xc + e_lru[pl.ds(HALO - 3 + k, tm), :] * pv[ROW_CW + k:ROW_CW + k + 1, :]
    xcb = xc.astype(BF16)
    r = _sigmoid(_bd(xcb, wa_ref) + pv[ROW_BA:ROW_BA + 1, :])
    ig = _sigmoid(_bd(xcb, wx_ref) + pv[ROW_BX:ROW_BX + 1, :])
    return xc, r, ig, (-LRU_C * r) * _softplus_neg_lambda(pv)


def _lru_decay(la):
    a = jnp.exp(la)
    om = -jnp.tanh(la) * (1.0 + a * a)
    omc = jnp.maximum(om, 1e-12)
    rmult = lax.rsqrt(omc)
    return a, om, omc * rmult, rmult


def _over_count(v, w, inv_head):
    return jnp.concatenate([v[0:HALO] * inv_head, v[HALO:] * (1.0 / w)], axis=0)


def _pool_pre(e_pool, pv, wp_ref, tm, t0):
    t_head = t0 + lax.broadcasted_iota(jnp.int32, (HALO, 1), 0)
    parts, inv_heads = [], []
    for g, w in enumerate(POOL_WINDOWS):
        ext = e_pool[:, pl.ds(128 * g, 128)]
        s = _window_sum(ext, w, back=True)[HALO:, :]
        inv_head = 1.0 / jnp.minimum(t_head + 1, w).astype(F32)
        inv_heads.append(inv_head)
        parts.append(_over_count(s, w, inv_head) - ext[HALO:, :])
    pooled = jnp.concatenate(parts, axis=1)
    pooled_b = pooled.astype(BF16)
    zp = _bd(pooled_b, wp_ref) + pv[ROW_PB:ROW_PB + 1, :]
    return pooled_b, zp, inv_heads


def _scan_tile(a_ref, b_ref, out_ref, carry, tm, reverse):
    row = lax.broadcasted_iota(jnp.int32, (8, LRU_W), 0)
    nblk = tm // 8

    def local_scan(blk):
        r0 = pl.multiple_of(blk * 8, 8)
        av = a_ref[pl.ds(r0, 8), :]
        bv = b_ref[pl.ds(r0, 8), :]
        for d in (1, 2, 4):
            sh = (8 - d) if reverse else d
            a_s = pltpu.roll(av, sh, 0)
            b_s = pltpu.roll(bv, sh, 0)
            m = (row < 8 - d) if reverse else (row >= d)
            bv = jnp.where(m, av * b_s + bv, bv)
            av = jnp.where(m, av * a_s, av)
        return r0, av, bv

    def step(i, hin):
        local = [local_scan((nblk - 1 - (i * SCAN_UNROLL + j)) if reverse else (i * SCAN_UNROLL + j))
                 for j in range(SCAN_UNROLL)]
        for r0, av, bv in local:
            hv = av * hin + bv
            out_ref[pl.ds(r0, 8), :] = hv
            hin = jnp.broadcast_to(hv[0:1, :] if reverse else hv[7:8, :], (8, LRU_W))
        return hin

    return lax.fori_loop(0, nblk // SCAN_UNROLL, step, carry)


MESH = pl.DeviceIdType.MESH
ANY = pl.BlockSpec(memory_space=pl.ANY)


def _place():
    x, y, c = lax.axis_index("x"), lax.axis_index("y"), lax.axis_index("c")
    chips = [(1 - x, y), (x, 1 - y), (1 - x, 1 - y)]
    return x, y, c, chips


class _Gather:
    def __init__(self, ins, outs, send_sems, recv_sems, local_sems, core_major=False):
        self.ins, self.outs, self.n = ins, outs, len(ins)
        self.send_sems, self.recv_sems, self.local_sems = send_sems, recv_sems, local_sems
        self.core_major = core_major

    @staticmethod
    def scratch(n):
        return [pltpu.SemaphoreType.DMA((7, n)), pltpu.SemaphoreType.DMA((7, n)), pltpu.SemaphoreType.DMA((n,))]

    def _slot(self, a, px, py, pc):
        return self.outs[a].at[4 * pc + 2 * px + py if self.core_major else 4 * px + 2 * py + pc]

    def _copy(self, a, k, block, to, src=None):
        return pltpu.make_async_remote_copy(
            src_ref=self._slot(a, *block) if src is None else src, dst_ref=self._slot(a, *block),
            send_sem=self.send_sems.at[k, a], recv_sem=self.recv_sems.at[k, a], device_id=to, device_id_type=MESH)

    def _mine(self, a):
        x, y, c, _ = _place()
        return pltpu.make_async_copy(self.ins[a], self._slot(a, x, y, c), self.local_sems.at[a])

    def _first(self, a):
        x, y, c, chips = _place()
        me = (x, y, c)
        return ([self._copy(a, 0, me, (x, y, 1 - c), src=self.ins[a])]
                + [self._copy(a, 1 + j, me, (*chip, c), src=self.ins[a]) for j, chip in enumerate(chips)])

    def start(self):
        for a in range(self.n):
            self._mine(a).start()
        for a in range(self.n):
            for cp in self._first(a):
                cp.start()

    def finish(self):
        x, y, c, chips = _place()
        me, sibling = (x, y, c), (x, y, 1 - c)
        passed = []
        for j, chip in enumerate(chips):
            for a in range(self.n):
                self._copy(a, 1 + j, (*chip, c), me).wait_recv()
                fwd = self._copy(a, 4 + j, (*chip, c), sibling)
                fwd.start()
                passed.append(fwd)
        for a in range(self.n):
            self._copy(a, 0, (x, y, 1 - c), me).wait_recv()
            for j, chip in enumerate(chips):
                self._copy(a, 4 + j, (*chip, 1 - c), me).wait_recv()
        for a in range(self.n):
            for cp in self._first(a):
                cp.wait_send()
        for cp in passed:
            cp.wait_send()
        for a in range(self.n):
            self._mine(a).wait()


def _half_exchange(arr, name):
    def body(in_ref, out_ref, send_sems, recv_sems, local_sem):
        x, y, c, _ = _place()
        my_chip = 2 * x + y

        def send(j, wait):
            to_me = (c == x) & (y == j // 2) & (c == j % 2)

            @pl.when(to_me)
            def _():
                local = pltpu.make_async_copy(in_ref.at[j], out_ref.at[my_chip], local_sem)
                local.wait() if wait else local.start()

            @pl.when(jnp.logical_not(to_me))
            def _():
                remote = pltpu.make_async_remote_copy(
                    src_ref=in_ref.at[j], dst_ref=out_ref.at[my_chip], send_sem=send_sems.at[j],
                    recv_sem=recv_sems.at[my_chip], device_id=(c, j // 2, j % 2), device_id_type=MESH)
                remote.wait_send() if wait else remote.start()

        for j in range(4):
            send(j, wait=False)
        for j in range(4):
            send(j, wait=True)
        for k in range(4):
            from_me = (k // 2 == x) & (k % 2 == y) & (c == x)

            @pl.when(jnp.logical_not(from_me))
            def _():
                pltpu.make_async_remote_copy(
                    src_ref=in_ref.at[0], dst_ref=out_ref.at[k], send_sem=send_sems.at[0], recv_sem=recv_sems.at[k],
                    device_id=(k // 2, k % 2, x), device_id_type=MESH).wait_recv()

    return pl.pallas_call(
        body, name=name, out_shape=_sds(arr.shape, arr.dtype), in_specs=[ANY], out_specs=ANY,
        scratch_shapes=[pltpu.SemaphoreType.DMA((4,)), pltpu.SemaphoreType.DMA((4,)), pltpu.SemaphoreType.DMA],
    )(arr)


class _ChipExchange:
    def __init__(self, ins, outs, send_sems, recv_sems, local_sems):
        self.ins, self.outs, self.n = ins, outs, len(ins)
        self.send_sems, self.recv_sems, self.local_sems = send_sems, recv_sems, local_sems

    @staticmethod
    def scratch(n):
        return [pltpu.SemaphoreType.DMA((3, n)), pltpu.SemaphoreType.DMA((3, n)), pltpu.SemaphoreType.DMA((n,))]

    def _local(self, a):
        x, y, _, _ = _place()
        me = 2 * x + y
        return pltpu.make_async_copy(self.ins[a].at[me], self.outs[a].at[me], self.local_sems.at[a])

    def _copies(self, a):
        x, y, c, chips = _place()
        me = 2 * x + y
        return [(pltpu.make_async_remote_copy(
                     src_ref=self.ins[a].at[2 * px + py], dst_ref=self.outs[a].at[me],
                     send_sem=self.send_sems.at[k, a], recv_sem=self.recv_sems.at[k, a],
                     device_id=(px, py, c), device_id_type=MESH),
                 pltpu.make_async_remote_copy(
                     src_ref=self.ins[a].at[me], dst_ref=self.outs[a].at[2 * px + py],
                     send_sem=self.send_sems.at[k, a], recv_sem=self.recv_sems.at[k, a],
                     device_id=(px, py, c), device_id_type=MESH))
                for k, (px, py) in enumerate(chips)]

    def start(self):
        for a in range(self.n):
            self._local(a).start()
        for a in range(self.n):
            for send, _ in self._copies(a):
                send.start()

    def finish(self):
        for a in range(self.n):
            for send, recv in self._copies(a):
                send.wait_send()
                recv.wait_recv()
        for a in range(self.n):
            self._local(a).wait()


def _gathering(body, n_steps, n_s, core_major):
    def wrapped(*refs, n_in, n_out):
        ins, sh_in = refs[:n_in], refs[n_in:n_in + n_s]
        outs, sh_out = refs[n_in + n_s:n_in + n_s + n_out], refs[n_in + n_s + n_out:n_in + 2 * n_s + n_out]
        rest = refs[n_in + 2 * n_s + n_out:]
        gather = _Gather(sh_in, sh_out, *rest[len(rest) - 3:], core_major=core_major)
        i = pl.program_id(0)

        @pl.when(i == 0)
        def _():
            gather.start()

        body(*ins, *outs, *rest[:len(rest) - 3])

        @pl.when(i == n_steps - 1)
        def _():
            gather.finish()

    return wrapped


def _norm_in(x, g_mix, shards, tm):
    T = x.shape[0]
    n_t = T // tm
    n_s = len(shards)

    def norm(x_ref, g_ref, h_ref):
        xv = x_ref[...]
        h_ref[...] = (xv * _rstd(xv) * g_ref[...]).astype(BF16)

    outs = pl.pallas_call(
        functools.partial(_gathering(norm, n_t, n_s, core_major=False), n_in=2, n_out=1), name="norm_in", grid=(n_t,),
        in_specs=[pl.BlockSpec((tm, D_MODEL), lambda i: (i, 0)), pl.BlockSpec((1, D_MODEL), lambda i: (0, 0))]
        + [ANY] * n_s,
        out_specs=[pl.BlockSpec((tm, D_MODEL), lambda i: (i, 0))] + [ANY] * n_s,
        out_shape=[_sds((T, D_MODEL), BF16)] + [_sds((N_DEV,) + a.shape, a.dtype) for a in shards],
        scratch_shapes=_Gather.scratch(n_s),
        compiler_params=pltpu.CompilerParams(dimension_semantics=("arbitrary",)),
    )(x, g_mix, *shards)
    return outs[0], list(outs[1:])


def _mix_in(h1, w_in_t, tm, shards):
    T = h1.shape[0]
    n_t = T // tm
    n_s = len(shards)

    def project(h_ref, w_ref, u_ref):
        u_ref[...] = lax.dot_general(h_ref[...], w_ref[...], NT, preferred_element_type=F32)

    outs = pl.pallas_call(
        functools.partial(_gathering(project, n_t, n_s, core_major=True), n_in=2, n_out=1), name="mix_in", grid=(n_t,),
        in_specs=[pl.BlockSpec((tm, D_MODEL), lambda i: (i, 0)), pl.BlockSpec((D_IN, D_MODEL), lambda i: (0, 0))]
        + [ANY] * n_s,
        out_specs=[pl.BlockSpec((tm, D_IN), lambda i: (i, 0))] + [ANY] * n_s,
        out_shape=[_sds((T, D_IN), F32)] + [_sds((N_DEV,) + a.shape, a.dtype) for a in shards],
        scratch_shapes=_Gather.scratch(n_s),
        compiler_params=pltpu.CompilerParams(dimension_semantics=("arbitrary",)),
    )(h1, w_in_t, *shards)
    return outs[0], list(outs[1:])


def _mixer_fwd(u, x, pv, wa, wx, wp, w_out_b, g_ffn, tm, shards=()):
    T = u.shape[0]
    n_s = len(shards)
    n_t = T // tm

    def body(u_ref, x_ref, pv_ref, wa_in, wx_in, wp_in, wo_ref, gf_ref, *rest):
        sh_in, rest = rest[:n_s], rest[n_s:]
        y_ref, hs_ref, hres_ref, h2_ref, saved_ref = rest[:5]
        sh_out, rest = rest[5:5 + n_s], rest[5 + n_s:]
        e_lru, e_pool, a_s, b_s, hc, wa_ref, wx_ref, wp_ref = rest[:8]
        gather = _Gather(sh_in, sh_out, *rest[8:], core_major=True) if n_s else None
        i = pl.program_id(0)

        @pl.when(i == 0)
        def _():
            if gather:
                gather.start()
            e_lru[pl.ds(0, HALO), :] = jnp.zeros((HALO, LRU_W), F32)
            e_pool[pl.ds(0, HALO), :] = jnp.zeros((HALO, POOL_W), F32)
            hc[...] = jnp.zeros((8, LRU_W), F32)
            _fill_block_diag(wa_ref, wa_in)
            _fill_block_diag(wx_ref, wx_in)
            _fill_block_diag(wp_ref, wp_in)

        e_lru[pl.ds(HALO, tm), :] = u_ref[:, 0:LRU_W]
        e_pool[pl.ds(HALO, tm), :] = u_ref[:, 2 * LRU_W:D_IN]
        pv = pv_ref[...]
        xc, r, ig, la = _lru_gates(e_lru, pv, wa_ref, wx_ref, tm)
        for q, val in enumerate((xc, r, ig, la)):
            saved_ref[:, LRU_W * q:LRU_W * (q + 1)] = val
        a, _, mult, _ = _lru_decay(la)
        a_s[...] = a
        b_s[...] = mult * (ig * xc)
        hc[...] = _scan_tile(a_s, b_s, hs_ref, hc[...], tm, reverse=False)
        gl, _ = _gelu_parts(u_ref[:, LRU_W:2 * LRU_W])
        y_lru = hs_ref[...] * gl
        _, zp, _ = _pool_pre(e_pool, pv, wp_ref, tm, i * tm)
        y_pool = zp * pv[ROW_PS:ROW_PS + 1, :]
        yn = jnp.concatenate([y_lru * _rstd(y_lru) * pv[ROW_GL:ROW_GL + 1, :],
                              y_pool * _rstd(y_pool) * pv[ROW_GP:ROW_GP + 1, :]], axis=1).astype(BF16)
        for b in range(N_DEV):
            y_ref[:, 128 * _y_pos(b):128 * (_y_pos(b) + 1)] = yn[:, 128 * b:128 * (b + 1)]
        hr = x_ref[...] + jnp.dot(y_ref[...], wo_ref[...], preferred_element_type=F32)
        hres_ref[...] = hr
        h2_ref[...] = (hr * _rstd(hr) * gf_ref[...]).astype(BF16)
        e_lru[pl.ds(0, HALO), :] = e_lru[pl.ds(tm, HALO), :]
        e_pool[pl.ds(0, HALO), :] = e_pool[pl.ds(tm, HALO), :]

        if gather:
            @pl.when(i == n_t - 1)
            def _():
                gather.finish()

    full = lambda shape: pl.BlockSpec(shape, lambda i: (0,) * len(shape))
    row = lambda w: pl.BlockSpec((tm, w), lambda i: (i, 0))
    outs = pl.pallas_call(
        body, name="mixer_fwd", grid=(n_t,),
        in_specs=[row(D_IN), row(D_MODEL), full((16, LRU_W)), full((8, 64, 64)), full((8, 64, 64)), full((4, 128, 128)),
                  full((D_MODEL, D_MODEL)), full((1, D_MODEL))] + [ANY] * n_s,
        out_specs=[row(D_MODEL), row(LRU_W), row(D_MODEL), row(D_MODEL), row(4 * LRU_W)] + [ANY] * n_s,
        out_shape=[_sds((T, D_MODEL), BF16), _sds((T, LRU_W), F32), _sds((T, D_MODEL), F32), _sds((T, D_MODEL), BF16),
                   _sds((T, 4 * LRU_W), F32)] + [_sds((N_DEV,) + a.shape, a.dtype) for a in shards],
        scratch_shapes=[pltpu.VMEM((HALO + tm, LRU_W), F32), pltpu.VMEM((HALO + tm, POOL_W), F32),
                        pltpu.VMEM((tm, LRU_W), F32), pltpu.VMEM((tm, LRU_W), F32), pltpu.VMEM((8, LRU_W), F32)]
        + [pltpu.VMEM((2, 256, 256), BF16)] * 3 + (_Gather.scratch(n_s) if n_s else []),
        compiler_params=pltpu.CompilerParams(dimension_semantics=("arbitrary",)),
    )(u, x, pv, wa, wx, wp, w_out_b, g_ffn, *shards)
    return outs[0], outs[1], outs[2], outs[3], outs[4], list(outs[5:])


def _ffn_fwd(hres, h2, w1_b, w3_b, w2_b, g_fin, tgt, tm):
    T = hres.shape[0]

    def body(hres_ref, h2_ref, w1_ref, w3_ref, w2_ref, gfin_ref, tgt_ref, g_ref, v_ref, d3_ref, loss_ref, dgfin_ref):
        @pl.when(pl.program_id(0) == 0)
        def _():
            loss_ref[...] = jnp.zeros((8, 128), F32)
            dgfin_ref[...] = jnp.zeros((1, D_MODEL), F32)

        h2 = h2_ref[...]
        h3 = hres_ref[...]
        for lo, hi in FF_CHUNKS:
            g = lax.dot_general(h2, w1_ref[lo:hi, :], NT, preferred_element_type=F32)
            v = lax.dot_general(h2, w3_ref[lo:hi, :], NT, preferred_element_type=F32)
            g_ref[:, lo:hi] = g.astype(BF16)
            v_ref[:, lo:hi] = v.astype(BF16)
            ff = ((g * _sigmoid(g)) * v).astype(BF16)
            h3 = h3 + jnp.dot(ff, w2_ref[lo:hi, :], preferred_element_type=F32)

        rstd = _rstd(h3)
        xh = h3 * rstd
        gfin = gfin_ref[...]
        err = xh * gfin - tgt_ref[...]
        loss_ref[...] += 0.5 * jnp.sum(jnp.mean(err * err, axis=-1, keepdims=True))
        dx, dgain = _rms_bwd(err * (1.0 / D_MODEL), xh, rstd, gfin)
        d3_ref[...] = dx
        dgfin_ref[...] += dgain

    row = lambda w: pl.BlockSpec((tm, w), lambda i: (i, 0))
    const = lambda shape: pl.BlockSpec(shape, lambda i: (0,) * len(shape))
    weight = pl.BlockSpec((D_FF, D_MODEL), lambda i: (0, 0), pipeline_mode=pl.Buffered(1))
    return pl.pallas_call(
        body, name="ffn_fwd", grid=(T // tm,),
        in_specs=[row(D_MODEL), row(D_MODEL), weight, weight, weight, const((1, D_MODEL)), row(D_MODEL)],
        out_specs=[row(D_FF), row(D_FF), row(D_MODEL), const((8, 128)), const((1, D_MODEL))],
        out_shape=[_sds((T, D_FF), BF16), _sds((T, D_FF), BF16),
                   _sds((T, D_MODEL), F32), _sds((8, 128), F32), _sds((1, D_MODEL), F32)],
        compiler_params=pltpu.CompilerParams(dimension_semantics=("arbitrary",)),
    )(hres, h2, w1_b, w3_b, w2_b, g_fin, tgt)


def _ffn_bwd(d3, g, v, w1_b, w3_b, w2_b, hres, g_ffn, tm):
    T = d3.shape[0]

    def body(d3_ref, g_ref, v_ref, w1_ref, w3_ref, w2_ref, hres_ref, gf_ref, dg_ref, dv_ref, ff_ref, d2_ref, dgffn_ref):
        @pl.when(pl.program_id(0) == 0)
        def _():
            dgffn_ref[...] = jnp.zeros((1, D_MODEL), F32)

        d3 = d3_ref[...]
        d3b = d3.astype(BF16)
        dh2 = jnp.zeros((tm, D_MODEL), F32)
        for lo, hi in FF_CHUNKS:
            dff = lax.dot_general(d3b, w2_ref[lo:hi, :], NT, preferred_element_type=F32)
            gv = g_ref[:, lo:hi].astype(F32)
            vv = v_ref[:, lo:hi].astype(F32)
            sg = _sigmoid(gv)
            sl = gv * sg
            dgb = (dff * vv * (sg * (1.0 + gv * (1.0 - sg)))).astype(BF16)
            dvb = (dff * sl).astype(BF16)
            dg_ref[:, lo:hi] = dgb
            dv_ref[:, lo:hi] = dvb
            ff_ref[:, lo:hi] = (sl * vv).astype(BF16)
            dh2 = dh2 + (jnp.dot(dgb, w1_ref[lo:hi, :], preferred_element_type=F32)
                         + jnp.dot(dvb, w3_ref[lo:hi, :], preferred_element_type=F32))

        hr = hres_ref[...]
        rstd = _rstd(hr)
        dx, dgain = _rms_bwd(dh2, hr * rstd, rstd, gf_ref[...])
        d2_ref[...] = d3 + dx
        dgffn_ref[...] += dgain

    row = lambda w: pl.BlockSpec((tm, w), lambda i: (i, 0))
    const = lambda shape: pl.BlockSpec(shape, lambda i: (0,) * len(shape))
    weight = pl.BlockSpec((D_FF, D_MODEL), lambda i: (0, 0), pipeline_mode=pl.Buffered(1))
    return pl.pallas_call(
        body, name="ffn_bwd", grid=(T // tm,),
        in_specs=[row(D_MODEL), row(D_FF), row(D_FF), weight, weight, weight, row(D_MODEL), const((1, D_MODEL))],
        out_specs=[row(D_FF), row(D_FF), row(D_FF), row(D_MODEL), const((1, D_MODEL))],
        out_shape=[_sds((T, D_FF), BF16), _sds((T, D_FF), BF16), _sds((T, D_FF), BF16),
                   _sds((T, D_MODEL), F32), _sds((1, D_MODEL), F32)],
        compiler_params=pltpu.CompilerParams(dimension_semantics=("arbitrary",)),
    )(d3, g, v, w1_b, w3_b, w2_b, hres, g_ffn)


def _at_b_pair(a, b, c_arr, name, tk, gather=()):
    T, M = a.shape
    N = b.shape[1]
    hm, n_k = M // 2, T // tk
    n_g = len(gather)

    def body(c_ref, a_ref, b_ref, *rest):
        g_in, o_ref, rest = rest[:n_g], rest[n_g], rest[n_g + 1:]
        g_out, rest = rest[:n_g], rest[n_g:]
        acc, landed, send_sem, recv_sem = rest[:4]
        ag = _Gather(g_in, g_out, *rest[4:]) if n_g else None
        ph, k = pl.program_id(0), pl.program_id(1)

        def hand_over():
            x, y, c, _ = _place()
            return pltpu.make_async_remote_copy(
                src_ref=acc.at[0], dst_ref=landed, send_sem=send_sem, recv_sem=recv_sem,
                device_id=(x, y, 1 - c), device_id_type=MESH)

        if ag:
            @pl.when((ph == 0) & (k == 0))
            def _():
                ag.start()

        @pl.when(k == 0)
        def _():
            acc[ph] = jnp.zeros((hm, N), F32)

        acc[ph] += lax.dot_general(a_ref[...].astype(BF16), b_ref[...].astype(BF16), TN, preferred_element_type=F32)

        @pl.when((ph == 0) & (k == n_k - 1))
        def _():
            hand_over().start()

        @pl.when((ph == 1) & (k == n_k - 1))
        def _():
            copy = hand_over()
            copy.wait_recv()
            o_ref[...] = (acc[1] + landed[...]).astype(BF16)
            copy.wait_send()
            if ag:
                ag.finish()

    outs = pl.pallas_call(
        body, name=name,
        grid_spec=pltpu.PrefetchScalarGridSpec(
            num_scalar_prefetch=1, grid=(2, n_k),
            in_specs=[pl.BlockSpec((tk, hm), lambda ph, k, c_ref: (k, (ph + 1 - c_ref[0]) % 2)),
                      pl.BlockSpec((tk, N), lambda ph, k, c_ref: (k, 0))] + [ANY] * n_g,
            out_specs=[pl.BlockSpec((hm, N), lambda ph, k, c_ref: (0, 0))] + [ANY] * n_g,
            scratch_shapes=[pltpu.VMEM((2, hm, N), F32), pltpu.VMEM((hm, N), F32),
                            pltpu.SemaphoreType.DMA, pltpu.SemaphoreType.DMA] + (_Gather.scratch(n_g) if n_g else [])),
        out_shape=[_sds((hm, N), BF16)] + [_sds((N_DEV,) + g.shape, g.dtype) for g in gather],
        compiler_params=pltpu.CompilerParams(dimension_semantics=("arbitrary", "arbitrary")),
    )(c_arr, a, b, *gather)
    return (outs[0], list(outs[1:])) if n_g else outs[0]


def _mixer_bwd(d2, u, hs, saved, pv, wa, wx, wp, w_out_b, tm, chip_sums=()):
    T = u.shape[0]
    n_t = T // tm
    n_x = len(chip_sums)

    def body(d2_ref, u_ref, uh_ref, hs_ref, hh_ref, saved_ref, pv_ref, wa_in, wx_in, wp_in, wo_ref, *rest):
        x_in, rest = rest[:n_x], rest[n_x:]
        du_ref, sg_ref = rest[:2]
        x_out, rest = rest[2:2 + n_x], rest[2 + n_x:]
        e_pool, e_h, a_s, b_s, dh_s, mu_s, f_x, f_p, mc, cx, cp = rest[:11]
        wa_ref, wx_ref, wp_ref, vacc_ref, dwa_ref, dwx_ref, dwp_ref = rest[11:18]
        exchange = _ChipExchange(x_in, x_out, *rest[18:]) if n_x else None
        s = pl.program_id(0)
        it = n_t - 1 - s

        @pl.when(s == 0)
        def _():
            if exchange:
                exchange.start()
            mc[...] = jnp.zeros((8, LRU_W), F32)
            cx[...] = jnp.zeros((8, LRU_W), F32)
            cp[...] = jnp.zeros((HALO, POOL_W), F32)
            vacc_ref[...] = jnp.zeros((16, LRU_W), F32)
            dwa_ref[...] = jnp.zeros((2, 256, 256), F32)
            dwx_ref[...] = jnp.zeros((2, 256, 256), F32)
            dwp_ref[...] = jnp.zeros((2, 256, 256), F32)
            _fill_block_diag(wa_ref, wa_in)
            _fill_block_diag(wx_ref, wx_in)
            _fill_block_diag(wp_ref, wp_in)

        first = it == 0
        e_pool[pl.ds(0, HALO), :] = jnp.where(first, 0.0, uh_ref[...])
        e_pool[pl.ds(HALO, tm), :] = u_ref[:, 2 * LRU_W:D_IN]
        e_h[pl.ds(0, 8), :] = jnp.where(first, 0.0, hh_ref[...])
        e_h[pl.ds(8, tm), :] = hs_ref[...]
        pv = pv_ref[...]
        saved = lambda q: saved_ref[:, LRU_W * q:LRU_W * (q + 1)]

        dyn = lax.dot_general(d2_ref[...].astype(BF16), wo_ref[...], NT, preferred_element_type=F32)
        dyn = jnp.concatenate([dyn[:, 128 * _y_pos(b):128 * (_y_pos(b) + 1)] for b in range(N_DEV)], axis=1)

        h = hs_ref[...]
        ug = u_ref[:, LRU_W:2 * LRU_W]
        gl, dgl = _gelu_parts(ug)
        y_lru = h * gl
        rstd_l = _rstd(y_lru)
        dy_lru, d_gain_l = _rms_bwd(dyn[:, 0:LRU_W], y_lru * rstd_l, rstd_l, pv[ROW_GL:ROW_GL + 1, :])
        dh = dy_lru * gl
        du_ref[:, LRU_W:2 * LRU_W] = (dy_lru * h * dgl).astype(BF16)
        a_s[...] = jnp.exp(saved(3))
        b_s[...] = a_s[...] * dh
        dh_s[...] = dh
        mu_s[pl.ds(tm, 8), :] = mc[...]
        mc[...] = _scan_tile(a_s, b_s, mu_s, mc[...], tm, reverse=True)
        xc, r, ig = saved(0), saved(1), saved(2)
        a, om, mult, rmult = _lru_decay(saved(3))
        lam_t = dh_s[...] + mu_s[pl.ds(1, tm), :]
        da = lam_t * e_h[pl.ds(7, tm), :]
        dmult = lam_t * (ig * xc)
        di = lam_t * (mult * xc)
        dxc = lam_t * (mult * ig)
        dla = da * a - jnp.where(om > 1e-12, dmult * ((a * a) * rmult), 0.0)
        dra = (dla * (-LRU_C * _softplus_neg_lambda(pv))) * (r * (1.0 - r))
        dia = di * (ig * (1.0 - ig))
        drab = dra.astype(BF16)
        diab = dia.astype(BF16)
        xcb = xc.astype(BF16)
        dxc = dxc + _bd_t(drab, wa_ref) + _bd_t(diab, wx_ref)
        dwa_ref[...] += _bd_grad(xcb, drab)
        dwx_ref[...] += _bd_grad(xcb, diab)
        sig_neg_lam = _sigmoid(-pv[ROW_LAM:ROW_LAM + 1, :])
        d_lam = jnp.sum(dla * r, axis=0, keepdims=True) * (LRU_C * sig_neg_lam)

        f_x[pl.ds(0, tm), :] = dxc
        f_x[pl.ds(tm, 8), :] = cx[...]
        du_lru = jnp.zeros((tm, LRU_W), F32)
        u_lru = u_ref[:, 0:LRU_W]
        d_cw = []
        for k in range(4):
            later = f_x[pl.ds(3 - k, tm), :]
            du_lru = du_lru + later * pv[ROW_CW + k:ROW_CW + k + 1, :]
            d_cw.append(jnp.sum(later * u_lru, axis=0, keepdims=True))
        du_ref[:, 0:LRU_W] = du_lru.astype(BF16)
        cx[...] = f_x[pl.ds(0, 8), :]

        pooled_b, zp, inv_cnts = _pool_pre(e_pool, pv, wp_ref, tm, it * tm)
        ps = pv[ROW_PS:ROW_PS + 1, :]
        y_pool = zp * ps
        rstd_p = _rstd(y_pool)
        dy_pool, d_gain_p = _rms_bwd(dyn[:, LRU_W:D_MODEL], y_pool * rstd_p, rstd_p, pv[ROW_GP:ROW_GP + 1, :])
        dz = dy_pool * ps
        dzb = dz.astype(BF16)
        dwp_ref[...] += _bd_grad(pooled_b, dzb)
        dpooled = _bd_t(dzb, wp_ref)
        for g, w in enumerate(POOL_WINDOWS):
            f_p[pl.ds(0, tm), pl.ds(128 * g, 128)] = _over_count(dpooled[:, 128 * g:128 * (g + 1)], w, inv_cnts[g])
        f_p[pl.ds(tm, HALO), :] = cp[...]
        for g, w in enumerate(POOL_WINDOWS):
            acc = _window_sum(f_p[:, pl.ds(128 * g, 128)], w, back=False)[0:tm, :]
            du_ref[:, 2 * LRU_W + 128 * g:2 * LRU_W + 128 * (g + 1)] = (
                acc - dpooled[:, 128 * g:128 * (g + 1)]).astype(BF16)
        cp[...] = f_p[pl.ds(0, HALO), :]

        rows = d_cw + [
            jnp.sum(dxc, axis=0, keepdims=True),
            jnp.sum(dra, axis=0, keepdims=True),
            jnp.sum(dia, axis=0, keepdims=True),
            d_lam,
            jnp.sum(dz, axis=0, keepdims=True),
            jnp.sum(dy_pool * zp, axis=0, keepdims=True),
            d_gain_l, d_gain_p,
            jnp.zeros((4, LRU_W), F32),
        ]
        vacc_ref[...] += jnp.concatenate(rows, axis=0)

        @pl.when(s == n_t - 1)
        def _():
            sg_ref[SG_VEC:SG_VEC + 16, :] = vacc_ref[:, 0:256]
            sg_ref[SG_VEC + 16:SG_VEC + 32, :] = vacc_ref[:, 256:512]
            for half in range(2):
                sg_ref[SG_WA + 64 * half:SG_WA + 64 * (half + 1), :] = _diag_pack(dwa_ref[half], 64)
                sg_ref[SG_WX + 64 * half:SG_WX + 64 * (half + 1), :] = _diag_pack(dwx_ref[half], 64)
                sg_ref[SG_WP + 128 * half:SG_WP + 128 * (half + 1), :] = _diag_pack(dwp_ref[half], 128)
            if exchange:
                exchange.finish()

    rev = lambda w: pl.BlockSpec((tm, w), lambda s: (n_t - 1 - s, 0))
    full = lambda shape: pl.BlockSpec(shape, lambda s: (0,) * len(shape))
    outs = pl.pallas_call(
        body, name="mixer_bwd", grid=(n_t,),
        in_specs=[rev(D_MODEL), rev(D_IN),
                  pl.BlockSpec((HALO, POOL_W), lambda s: (jnp.maximum((n_t - 1 - s) * (tm // HALO) - 1, 0), 2)),
                  rev(LRU_W),
                  pl.BlockSpec((8, LRU_W), lambda s: (jnp.maximum((n_t - 1 - s) * (tm // 8) - 1, 0), 0)),
                  rev(4 * LRU_W), full((16, LRU_W)), full((8, 64, 64)), full((8, 64, 64)), full((4, 128, 128)),
                  full((D_MODEL, D_MODEL))] + [ANY] * n_x,
        out_specs=[rev(D_IN), full((SG_ROWS, 256))] + [ANY] * n_x,
        out_shape=[_sds((T, D_IN), BF16), _sds((SG_ROWS, 256), F32)] + [_sds(a.shape, a.dtype) for a in chip_sums],
        scratch_shapes=[pltpu.VMEM((HALO + tm, POOL_W), F32),
                        pltpu.VMEM((8 + tm, LRU_W), F32)] + [pltpu.VMEM((tm, LRU_W), F32)] * 3 + [
                        pltpu.VMEM((tm + 8, LRU_W), F32), pltpu.VMEM((tm + 8, LRU_W), F32),
                        pltpu.VMEM((tm + HALO, POOL_W), F32), pltpu.VMEM((8, LRU_W), F32),
                        pltpu.VMEM((8, LRU_W), F32), pltpu.VMEM((HALO, POOL_W), F32)]
        + [pltpu.VMEM((2, 256, 256), BF16)] * 3 + [pltpu.VMEM((16, LRU_W), F32)] + [pltpu.VMEM((2, 256, 256), F32)] * 3
        + (_ChipExchange.scratch(n_x) if n_x else []),
        compiler_params=pltpu.CompilerParams(dimension_semantics=("arbitrary",)),
    )(d2, u, u, hs, hs, saved, pv, wa, wx, wp, w_out_b, *chip_sums)
    return outs[0], outs[1], list(outs[2:])


def _mix_in_bwd(du, x, d2, w_in_t, g_mix, tm):
    T = x.shape[0]

    def body(du_ref, x_ref, d2_ref, w_ref, g_ref, dx_ref, dg_ref):
        @pl.when(pl.program_id(0) == 0)
        def _():
            dg_ref[...] = jnp.zeros((1, D_MODEL), F32)

        dh = jnp.dot(du_ref[...], w_ref[...], preferred_element_type=F32)
        xv = x_ref[...]
        rstd = _rstd(xv)
        dx, dgain = _rms_bwd(dh, xv * rstd, rstd, g_ref[...])
        dx_ref[...] = d2_ref[...] + dx
        dg_ref[...] += dgain

    row = lambda w: pl.BlockSpec((tm, w), lambda i: (i, 0))
    const = lambda shape: pl.BlockSpec(shape, lambda i: (0,) * len(shape))
    return pl.pallas_call(
        body, name="mix_in_bwd", grid=(T // tm,),
        in_specs=[row(D_IN), row(D_MODEL), row(D_MODEL), const((D_IN, D_MODEL)), const((1, D_MODEL))],
        out_specs=[row(D_MODEL), const((1, D_MODEL))],
        out_shape=[_sds((T, D_MODEL), F32), _sds((1, D_MODEL), F32)],
        compiler_params=pltpu.CompilerParams(dimension_semantics=("arbitrary",)),
    )(du, x, d2, w_in_t, g_mix)


def _adamw(w, g, m, v):
    m = ADAM_B1 * m + (1.0 - ADAM_B1) * g
    v = ADAM_B2 * v + (1.0 - ADAM_B2) * (g * g)
    m_hat = m / (1.0 - ADAM_B1 ** ADAM_STEP)
    v_hat = v / (1.0 - ADAM_B2 ** ADAM_STEP)
    delta = -ADAM_LR * (m_hat / (jnp.sqrt(v_hat) + ADAM_EPS) + ADAM_WD * w)
    return delta, m, v


def _adam_shards(ws, ms, vs, parts):
    n = len(ws)
    n_blk = [w.shape[0] // ADAM_ROWS for w in ws]

    def body(*refs):
        w_refs, m_refs, v_refs, p_refs, outs = (refs[:n], refs[n:2 * n], refs[2 * n:3 * n], refs[3 * n:4 * n],
                                                refs[4 * n:])
        i = pl.program_id(0)
        for a in range(n):
            @pl.when(i < n_blk[a])
            def _(a=a):
                g = p_refs[a][0].astype(F32)
                for j in range(1, 4):
                    g = g + p_refs[a][j].astype(F32)
                delta, new_m, new_v = _adamw(w_refs[a][...], g, m_refs[a][...], v_refs[a][...])
                for kind, val in enumerate((g, delta, new_m, new_v)):
                    outs[4 * a + kind][...] = val

    blk = lambda a: pl.BlockSpec((ADAM_ROWS, D_MODEL), lambda i: (jnp.minimum(i, n_blk[a] - 1), 0))
    part_blk = lambda a: pl.BlockSpec((4, ADAM_ROWS, D_MODEL), lambda i: (0, jnp.minimum(i, n_blk[a] - 1), 0))
    res = pl.pallas_call(
        body, name="adam_shards", grid=(max(n_blk),),
        in_specs=[blk(a) for a in range(n)] * 3 + [part_blk(a) for a in range(n)],
        out_specs=[blk(a) for a in range(n) for _ in range(4)],
        out_shape=[_sds(w.shape, F32) for w in ws for _ in range(4)],
        compiler_params=pltpu.CompilerParams(dimension_semantics=("arbitrary",)),
    )(*ws, *ms, *vs, *parts)
    return [tuple(res[4 * a:4 * a + 4]) for a in range(n)]


SMALL_PARAMS = [("norm_mix_g", (1, D_MODEL)), ("conv_w", (1, 4, 64)), ("conv_b", (1, LRU_W)),
                ("gate_a_w", (1, 8, 64, 64)), ("gate_a_b", (1, LRU_W)), ("gate_x_w", (1, 8, 64, 64)),
                ("gate_x_b", (1, LRU_W)), ("lru_lambda", (1, LRU_W)), ("pool_w", (1, 4, 128, 128)),
                ("pool_b", (1, POOL_W)), ("pool_scale", (1, POOL_W)), ("norm_lru_g", (1, LRU_W)),
                ("norm_pool_g", (1, POOL_W)), ("norm_ffn_g", (1, D_MODEL)), ("final_norm_g", (1, D_MODEL))]
VEC_ROW = dict(conv_b=ROW_CB, gate_a_b=ROW_BA, gate_x_b=ROW_BX, lru_lambda=ROW_LAM, pool_b=ROW_PB, pool_scale=ROW_PS,
               norm_lru_g=ROW_GL, norm_pool_g=ROW_GP)
WHOLE = (Ellipsis,)


def _unpack_mixer_grads(sg, dev):
    vec = jnp.concatenate([sg[SG_VEC:SG_VEC + 16], sg[SG_VEC + 16:SG_VEC + 32]], axis=1)
    out = {nm: [(WHOLE, vec[r:r + 1])] for nm, r in VEC_ROW.items()}
    own = jnp.zeros((4, 64), F32)
    for d in range(N_DEV):
        own = jnp.where(dev == d, vec[ROW_CW:ROW_CW + 4, 64 * d:64 * (d + 1)], own)
    out["conv_w"] = [((0,), own)]
    for nm, row0 in (("gate_a_w", SG_WA), ("gate_x_w", SG_WX)):
        out[nm] = [((0, b), sg[row0 + 64 * (b // 4):row0 + 64 * (b // 4 + 1), 64 * (b % 4):64 * (b % 4 + 1)])
                   for b in range(8)]
    out["pool_w"] = [((0, b), sg[SG_WP + 128 * (b // 2):SG_WP + 128 * (b // 2 + 1), 128 * (b % 2):128 * (b % 2 + 1)])
                     for b in range(4)]
    return out


def _adam_small(parts, w, m, v):
    names = [nm for nm, _ in SMALL_PARAMS]
    n = len(names)

    def body(sg_ref, gm_ref, gf_ref, gn_ref, ls_ref, *rest):
        w_refs, m_refs, v_refs, outs = rest[:n], rest[n:2 * n], rest[2 * n:3 * n], rest[3 * n:]
        dev = 4 * lax.axis_index("x") + 2 * lax.axis_index("y") + lax.axis_index("c")

        def total(ref):
            acc = ref[0]
            for d in range(1, N_DEV):
                acc = acc + ref[d]
            return acc

        pieces = _unpack_mixer_grads(total(sg_ref), dev)
        pieces["norm_mix_g"] = [(WHOLE, total(gm_ref))]
        pieces["norm_ffn_g"] = [(WHOLE, total(gf_ref))]
        pieces["final_norm_g"] = [(WHOLE, total(gn_ref))]
        for i, nm in enumerate(names):
            for idx, g in pieces[nm]:
                delta, new_m, new_v = _adamw(w_refs[i][idx], g, m_refs[i][idx], v_refs[i][idx])
                for kind, val in enumerate((g, delta, new_m, new_v)):
                    outs[4 * i + kind][idx] = val
        outs[4 * n][...] = total(ls_ref)

    shapes = [_sds(shape, F32) for _, shape in SMALL_PARAMS for _ in range(4)] + [_sds((8, 128), F32)]
    res = pl.pallas_call(body, name="adam_small", out_shape=shapes)(
        *parts, *[w[nm] for nm in names], *[m[nm] for nm in names], *[v[nm] for nm in names])
    return {nm: tuple(res[4 * i:4 * i + 4]) for i, nm in enumerate(names)}, res[4 * n][0, 0]


def _vec_rows(conv_w_full, conv_b, ba, bx, lam, pb, ps, gl, gp):
    return jnp.concatenate([conv_w_full, conv_b, ba, bx, lam, pb, ps, gl, gp, jnp.zeros((4, LRU_W), F32)], axis=0)


WEIGHT_ORDER = ['norm_mix_g', 'w_in', 'conv_w', 'conv_b', 'gate_a_w', 'gate_a_b', 'gate_x_w', 'gate_x_b', 'lru_lambda',
                'pool_w', 'pool_b', 'pool_scale', 'norm_lru_g', 'norm_pool_g', 'w_out', 'norm_ffn_g', 'ffn_w1', 'ffn_w3',
                'ffn_w2', 'final_norm_g']


def kernel(x, norm_mix_g, w_in, conv_w, conv_b, gate_a_w, gate_a_b, gate_x_w, gate_x_b, lru_lambda, pool_w, pool_b, pool_scale, norm_lru_g, norm_pool_g, w_out, norm_ffn_g, ffn_w1, ffn_w3, ffn_w2, final_norm_g, loss_target, m_norm_mix_g, m_w_in, m_conv_w, m_conv_b, m_gate_a_w, m_gate_a_b, m_gate_x_w, m_gate_x_b, m_lru_lambda, m_pool_w, m_pool_b, m_pool_scale, m_norm_lru_g, m_norm_pool_g, m_w_out, m_norm_ffn_g, m_ffn_w1, m_ffn_w3, m_ffn_w2, m_final_norm_g, v_norm_mix_g, v_w_in, v_conv_w, v_conv_b, v_gate_a_w, v_gate_a_b, v_gate_x_w, v_gate_x_b, v_lru_lambda, v_pool_w, v_pool_b, v_pool_scale, v_norm_lru_g, v_norm_pool_g, v_w_out, v_norm_ffn_g, v_ffn_w1, v_ffn_w3, v_ffn_w2, v_final_norm_g):
    ac = lax.axis_index("c")
    tm, tmx, tn, tk = 512, 512, 1408, 1024
    tm_in = 1024
    xs, tgt = x[0], loss_target[0]
    g_fin = final_norm_g.reshape(1, D_MODEL)
    c_arr = jnp.reshape(ac, (1,)).astype(jnp.int32)

    tr = lambda w: jnp.swapaxes(w[0], 0, 1)
    own = lambda w: w[0]
    bf = lambda a: a.astype(BF16)

    h1, (g_in, g_conv) = _norm_in(xs, norm_mix_g, [bf(tr(w_in)), conv_w[0]], tm_in)
    w_in_t = g_in.reshape(D_IN, D_MODEL)
    u, (g_out,) = _mix_in(h1, w_in_t, tm, shards=[bf(own(w_out))])
    conv_w_full = g_conv.transpose(1, 0, 2).reshape(4, LRU_W)
    pv = _vec_rows(conv_w_full, conv_b, gate_a_b, gate_x_b, lru_lambda, pool_b, pool_scale, norm_lru_g, norm_pool_g)
    wa, wx, wp = gate_a_w[0], gate_x_w[0], pool_w[0]
    w_out_b = g_out.reshape(D_MODEL, D_MODEL)
    y, hs, hres, h2, saved, (g_w1, g_w3, g_w2) = _mixer_fwd(
        u, xs, pv, wa, wx, wp, w_out_b, norm_ffn_g, tmx, shards=[bf(tr(ffn_w1)), bf(tr(ffn_w3)), bf(own(ffn_w2))])
    w1_t, w3_t, w2_b = g_w1.reshape(D_FF, D_MODEL), g_w3.reshape(D_FF, D_MODEL), g_w2.reshape(D_FF, D_MODEL)
    g, v, d3, loss_acc, d_gfin = _ffn_fwd(hres, h2, w1_t, w3_t, w2_b, g_fin, tgt, tm)

    dg, dv, ff, d2, d_gffn = _ffn_bwd(d3, g, v, w1_t, w3_t, w2_b, hres, norm_ffn_g, tm // 2)
    chips = lambda a: a.reshape(4, a.shape[0] // 4, a.shape[1])
    early_sums = [chips(_at_b_pair(y, d2, c_arr, "grad_w_out", 2 * tk)), chips(_at_b_pair(dg, h2, c_arr, "grad_w1", tk)),
                  chips(_at_b_pair(dv, h2, c_arr, "grad_w3", tk)), chips(_at_b_pair(ff, d3, c_arr, "grad_w2", tk))]
    du, d_mixer, early_parts = _mixer_bwd(d2, u, hs, saved, pv, wa, wx, wp, w_out_b, tmx, chip_sums=early_sums)
    grad_x, d_gmix = _mix_in_bwd(du, xs, d2, w_in_t, norm_mix_g, tm_in)
    d_win, small_parts = _at_b_pair(du, h1, c_arr, "grad_w_in", 2 * tk,
                                    gather=[d_mixer, d_gmix, d_gffn, d_gfin, loss_acc])
    parts = [_half_exchange(chips(d_win), "grads_to_chips_w_in")] + list(early_parts)

    res = {}
    shard_w = dict(w_in=(w_in, m_w_in, v_w_in, tr), w_out=(w_out, m_w_out, v_w_out, own),
                   ffn_w1=(ffn_w1, m_ffn_w1, v_ffn_w1, tr), ffn_w3=(ffn_w3, m_ffn_w3, v_ffn_w3, tr),
                   ffn_w2=(ffn_w2, m_ffn_w2, v_ffn_w2, own))
    shard_res = _adam_shards([view(w) for w, _, _, view in shard_w.values()],
                             [view(m) for _, m, _, view in shard_w.values()],
                             [view(v) for _, _, v, view in shard_w.values()], parts)
    for (nm, (_, _, _, view)), outs in zip(shard_w.items(), shard_res):
        res[nm] = [(jnp.swapaxes(o, 0, 1) if view is tr else o)[None] for o in outs]

    row = lambda a: a.reshape(1, D_MODEL)
    small = lambda gm, cw, cb, wa_, ba, wx_, bx, lam, pw, pb, ps, gl, gp, gf, gn: dict(
        norm_mix_g=gm, conv_w=cw, conv_b=cb, gate_a_w=wa_, gate_a_b=ba, gate_x_w=wx_, gate_x_b=bx, lru_lambda=lam,
        pool_w=pw, pool_b=pb, pool_scale=ps, norm_lru_g=gl, norm_pool_g=gp, norm_ffn_g=gf, final_norm_g=row(gn))
    small_res, loss = _adam_small(
        small_parts,
        small(norm_mix_g, conv_w, conv_b, gate_a_w, gate_a_b, gate_x_w, gate_x_b, lru_lambda, pool_w, pool_b,
              pool_scale, norm_lru_g, norm_pool_g, norm_ffn_g, final_norm_g),
        small(m_norm_mix_g, m_conv_w, m_conv_b, m_gate_a_w, m_gate_a_b, m_gate_x_w, m_gate_x_b, m_lru_lambda, m_pool_w,
              m_pool_b, m_pool_scale, m_norm_lru_g, m_norm_pool_g, m_norm_ffn_g, m_final_norm_g),
        small(v_norm_mix_g, v_conv_w, v_conv_b, v_gate_a_w, v_gate_a_b, v_gate_x_w, v_gate_x_b, v_lru_lambda, v_pool_w,
              v_pool_b, v_pool_scale, v_norm_lru_g, v_norm_pool_g, v_norm_ffn_g, v_final_norm_g))
    for nm, outs in small_res.items():
        res[nm] = [o.reshape(D_MODEL) for o in outs] if nm == "final_norm_g" else list(outs)

    out = [loss, grad_x[None]]
    for kind in range(4):
        out += [res[nm][kind] for nm in WEIGHT_ORDER]
    return tuple(out)
```

```python
import functools

import jax
import jax.numpy as jnp
from jax import lax
from jax.experimental import pallas as pl
from jax.experimental.pallas import tpu as pltpu

F32 = jnp.float32
BF16 = jnp.bfloat16

D_MODEL = 1024
LRU_W = 512
POOL_W = 512
D_IN = 1536
D_FF = 2816
POOL_WINDOWS = (2, 4, 8, 16)
EPS = 1e-6
LRU_C = 8.0
N_DEV = 8
HALO = 16
SCAN_UNROLL = 8
ADAM_ROWS = 32
FF_CHUNKS = ((0, 1536), (1536, 2816))

ADAM_LR = 0.001
ADAM_B1 = 0.9
ADAM_B2 = 0.999
ADAM_EPS = 1e-08
ADAM_WD = 0.01
ADAM_STEP = 10

ROW_CW, ROW_CB, ROW_BA, ROW_BX, ROW_LAM, ROW_PB, ROW_PS, ROW_GL, ROW_GP = 0, 4, 5, 6, 7, 8, 9, 10, 11
SG_VEC, SG_WA, SG_WX, SG_WP, SG_ROWS = 0, 32, 160, 288, 544

NT = (((1,), (1,)), ((), ()))
TN = (((0,), (0,)), ((), ()))


def _sds(shape, dtype):
    return jax.ShapeDtypeStruct(shape, dtype)


def _sigmoid(x):
    return 0.5 * jnp.tanh(0.5 * x) + 0.5


def _gelu_parts(x):
    c = 0.7978845608028654
    inner = c * (x + 0.044715 * (x * x * x))
    th = jnp.tanh(inner)
    g = 0.5 * x * (1.0 + th)
    dg = 0.5 * (1.0 + th) + 0.5 * x * (1.0 - th * th) * (c * (1.0 + 3.0 * 0.044715 * (x * x)))
    return g, dg


def _window_sum(ext, w, back):
    n = ext.shape[0]
    s, k = ext, 1
    while k < w:
        s = s + pltpu.roll(s, k if back else n - k, 0)
        k *= 2
    return s


def _rstd(x):
    return lax.rsqrt(jnp.mean(x * x, axis=-1, keepdims=True) + EPS)


def _rms_bwd(dy, xhat, rstd, gain):
    dxh = dy * gain
    dx = rstd * (dxh - xhat * jnp.mean(dxh * xhat, axis=-1, keepdims=True))
    return dx, jnp.sum(dy * xhat, axis=0, keepdims=True)


def _bd(xb, w_ref):
    return jnp.concatenate(
        [jnp.dot(xb[:, :256], w_ref[0], preferred_element_type=F32),
         jnp.dot(xb[:, 256:], w_ref[1], preferred_element_type=F32)], axis=1)


def _bd_t(xb, w_ref):
    return jnp.concatenate(
        [lax.dot_general(xb[:, :256], w_ref[0], NT, preferred_element_type=F32),
         lax.dot_general(xb[:, 256:], w_ref[1], NT, preferred_element_type=F32)], axis=1)


def _bd_grad(xb, db):
    return jnp.stack(
        [lax.dot_general(xb[:, :256], db[:, :256], TN, preferred_element_type=F32),
         lax.dot_general(xb[:, 256:], db[:, 256:], TN, preferred_element_type=F32)], axis=0)


def _fill_block_diag(dst, src_ref):
    n, k, _ = src_ref.shape
    dst[...] = jnp.zeros(dst.shape, BF16)
    for b in range(n):
        p, q = divmod(b, 256 // k)
        dst[p, q * k:(q + 1) * k, q * k:(q + 1) * k] = src_ref[b].astype(BF16)


def _diag_pack(w, k):
    lane = lax.broadcasted_iota(jnp.int32, (k, 256), 1)
    out = w[0:k]
    for q in range(1, 256 // k):
        out = jnp.where(lane >= q * k, w[q * k:(q + 1) * k], out)
    return out


def _y_pos(b):
    return 4 * (b % 2) + b // 2


def _softplus_neg_lambda(pv):
    z = -pv[ROW_LAM:ROW_LAM + 1, :]
    return jnp.maximum(z, 0.0) + jnp.log(1.0 + jnp.exp(-jnp.abs(z)))


def _lru_gates(e_lru, pv, wa_ref, wx_ref, tm):
    xc = pv[ROW_CB:ROW_CB + 1, :]
    for k in range(4):
        xc = xc + e_lru[pl.ds(HALO - 3 + k, tm), :] * pv[ROW_CW + k:ROW_CW + k + 1, :]
    xcb = xc.astype(BF16)
    r = _sigmoid(_bd(xcb, wa_ref) + pv[ROW_BA:ROW_BA + 1, :])
    ig = _sigmoid(_bd(xcb, wx_ref) + pv[ROW_BX:ROW_BX + 1, :])
    return xc, r, ig, (-LRU_C * r) * _softplus_neg_lambda(pv)


def _lru_decay(la):
    a = jnp.exp(la)
    om = -jnp.tanh(la) * (1.0 + a * a)
    omc = jnp.maximum(om, 1e-12)
    rmult = lax.rsqrt(omc)
    return a, om, omc * rmult, rmult


def _over_count(v, w, inv_head):
    return jnp.concatenate([v[0:HALO] * inv_head, v[HALO:] * (1.0 / w)], axis=0)


def _pool_pre(e_pool, pv, wp_ref, tm, t0):
    t_head = t0 + lax.broadcasted_iota(jnp.int32, (HALO, 1), 0)
    parts, inv_heads = [], []
    for g, w in enumerate(POOL_WINDOWS):
        ext = e_pool[:, pl.ds(128 * g, 128)]
        s = _window_sum(ext, w, back=True)[HALO:, :]
        inv_head = 1.0 / jnp.minimum(t_head + 1, w).astype(F32)
        inv_heads.append(inv_head)
        parts.append(_over_count(s, w, inv_head) - ext[HALO:, :])
    pooled = jnp.concatenate(parts, axis=1)
    pooled_b = pooled.astype(BF16)
    zp = _bd(pooled_b, wp_ref) + pv[ROW_PB:ROW_PB + 1, :]
    return pooled_b, zp, inv_heads


def _scan_tile(a_ref, b_ref, out_ref, carry, tm, reverse):
    row = lax.broadcasted_iota(jnp.int32, (8, LRU_W), 0)
    nblk = tm // 8

    def local_scan(blk):
        r0 = pl.multiple_of(blk * 8, 8)
        av = a_ref[pl.ds(r0, 8), :]
        bv = b_ref[pl.ds(r0, 8), :]
        for d in (1, 2, 4):
            sh = (8 - d) if reverse else d
            a_s = pltpu.roll(av, sh, 0)
            b_s = pltpu.roll(bv, sh, 0)
            m = (row < 8 - d) if reverse else (row >= d)
            bv = jnp.where(m, av * b_s + bv, bv)
            av = jnp.where(m, av * a_s, av)
        return r0, av, bv

    def step(i, hin):
        local = [local_scan((nblk - 1 - (i * SCAN_UNROLL + j)) if reverse else (i * SCAN_UNROLL + j))
                 for j in range(SCAN_UNROLL)]
        for r0, av, bv in local:
            hv = av * hin + bv
            out_ref[pl.ds(r0, 8), :] = hv
            hin = jnp.broadcast_to(hv[0:1, :] if reverse else hv[7:8, :], (8, LRU_W))
        return hin

    return lax.fori_loop(0, nblk // SCAN_UNROLL, step, carry)


MESH = pl.DeviceIdType.MESH
ANY = pl.BlockSpec(memory_space=pl.ANY)


def _place():
    x, y, c = lax.axis_index("x"), lax.axis_index("y"), lax.axis_index("c")
    chips = [(1 - x, y), (x, 1 - y), (1 - x, 1 - y)]
    return x, y, c, chips


class _Gather:
    def __init__(self, ins, outs, send_sems, recv_sems, local_sems, core_major=False):
        self.ins, self.outs, self.n = ins, outs, len(ins)
        self.send_sems, self.recv_sems, self.local_sems = send_sems, recv_sems, local_sems
        self.core_major = core_major

    @staticmethod
    def scratch(n):
        return [pltpu.SemaphoreType.DMA((7, n)), pltpu.SemaphoreType.DMA((7, n)), pltpu.SemaphoreType.DMA((n,))]

    def _slot(self, a, px, py, pc):
        return self.outs[a].at[4 * pc + 2 * px + py if self.core_major else 4 * px + 2 * py + pc]

    def _copy(self, a, k, block, to, src=None):
        return pltpu.make_async_remote_copy(
            src_ref=self._slot(a, *block) if src is None else src, dst_ref=self._slot(a, *block),
            send_sem=self.send_sems.at[k, a], recv_sem=self.recv_sems.at[k, a], device_id=to, device_id_type=MESH)

    def _mine(self, a):
        x, y, c, _ = _place()
        return pltpu.make_async_copy(self.ins[a], self._slot(a, x, y, c), self.local_sems.at[a])

    def _first(self, a):
        x, y, c, chips = _place()
        me = (x, y, c)
        return ([self._copy(a, 0, me, (x, y, 1 - c), src=self.ins[a])]
                + [self._copy(a, 1 + j, me, (*chip, c), src=self.ins[a]) for j, chip in enumerate(chips)])

    def start(self):
        for a in range(self.n):
            self._mine(a).start()
        for a in range(self.n):
            for cp in self._first(a):
                cp.start()

    def finish(self):
        x, y, c, chips = _place()
        me, sibling = (x, y, c), (x, y, 1 - c)
        passed = []
        for j, chip in enumerate(chips):
            for a in range(self.n):
                self._copy(a, 1 + j, (*chip, c), me).wait_recv()
                fwd = self._copy(a, 4 + j, (*chip, c), sibling)
                fwd.start()
                passed.append(fwd)
        for a in range(self.n):
            self._copy(a, 0, (x, y, 1 - c), me).wait_recv()
            for j, chip in enumerate(chips):
                self._copy(a, 4 + j, (*chip, 1 - c), me).wait_recv()
        for a in range(self.n):
            for cp in self._first(a):
                cp.wait_send()
        for cp in passed:
            cp.wait_send()
        for a in range(self.n):
            self._mine(a).wait()


def _half_exchange(arr, name):
    def body(in_ref, out_ref, send_sems, recv_sems, local_sem):
        x, y, c, _ = _place()
        my_chip = 2 * x + y

        def send(j, wait):
            to_me = (c == x) & (y == j // 2) & (c == j % 2)

            @pl.when(to_me)
            def _():
                local = pltpu.make_async_copy(in_ref.at[j], out_ref.at[my_chip], local_sem)
                local.wait() if wait else local.start()

            @pl.when(jnp.logical_not(to_me))
            def _():
                remote = pltpu.make_async_remote_copy(
                    src_ref=in_ref.at[j], dst_ref=out_ref.at[my_chip], send_sem=send_sems.at[j],
                    recv_sem=recv_sems.at[my_chip], device_id=(c, j // 2, j % 2), device_id_type=MESH)
                remote.wait_send() if wait else remote.start()

        for j in range(4):
            send(j, wait=False)
        for j in range(4):
            send(j, wait=True)
        for k in range(4):
            from_me = (k // 2 == x) & (k % 2 == y) & (c == x)

            @pl.when(jnp.logical_not(from_me))
            def _():
                pltpu.make_async_remote_copy(
                    src_ref=in_ref.at[0], dst_ref=out_ref.at[k], send_sem=send_sems.at[0], recv_sem=recv_sems.at[k],
                    device_id=(k // 2, k % 2, x), device_id_type=MESH).wait_recv()

    return pl.pallas_call(
        body, name=name, out_shape=_sds(arr.shape, arr.dtype), in_specs=[ANY], out_specs=ANY,
        scratch_shapes=[pltpu.SemaphoreType.DMA((4,)), pltpu.SemaphoreType.DMA((4,)), pltpu.SemaphoreType.DMA],
    )(arr)


class _ChipExchange:
    def __init__(self, ins, outs, send_sems, recv_sems, local_sems):
        self.ins, self.outs, self.n = ins, outs, len(ins)
        self.send_sems, self.recv_sems, self.local_sems = send_sems, recv_sems, local_sems

    @staticmethod
    def scratch(n):
        return [pltpu.SemaphoreType.DMA((3, n)), pltpu.SemaphoreType.DMA((3, n)), pltpu.SemaphoreType.DMA((n,))]

    def _local(self, a):
        x, y, _, _ = _place()
        me = 2 * x + y
        return pltpu.make_async_copy(self.ins[a].at[me], self.outs[a].at[me], self.local_sems.at[a])

    def _copies(self, a):
        x, y, c, chips = _place()
        me = 2 * x + y
        return [(pltpu.make_async_remote_copy(
                     src_ref=self.ins[a].at[2 * px + py], dst_ref=self.outs[a].at[me],
                     send_sem=self.send_sems.at[k, a], recv_sem=self.recv_sems.at[k, a],
                     device_id=(px, py, c), device_id_type=MESH),
                 pltpu.make_async_remote_copy(
                     src_ref=self.ins[a].at[me], dst_ref=self.outs[a].at[2 * px + py],
                     send_sem=self.send_sems.at[k, a], recv_sem=self.recv_sems.at[k, a],
                     device_id=(px, py, c), device_id_type=MESH))
                for k, (px, py) in enumerate(chips)]

    def start(self):
        for a in range(self.n):
            self._local(a).start()
        for a in range(self.n):
            for send, _ in self._copies(a):
                send.start()

    def finish(self):
        for a in range(self.n):
            for send, recv in self._copies(a):
                send.wait_send()
                recv.wait_recv()
        for a in range(self.n):
            self._local(a).wait()


def _gathering(body, n_steps, n_s, core_major):
    def wrapped(*refs, n_in, n_out):
        ins, sh_in = refs[:n_in], refs[n_in:n_in + n_s]
        outs, sh_out = refs[n_in + n_s:n_in + n_s + n_out], refs[n_in + n_s + n_out:n_in + 2 * n_s + n_out]
        rest = refs[n_in + 2 * n_s + n_out:]
        gather = _Gather(sh_in, sh_out, *rest[len(rest) - 3:], core_major=core_major)
        i = pl.program_id(0)

        @pl.when(i == 0)
        def _():
            gather.start()

        body(*ins, *outs, *rest[:len(rest) - 3])

        @pl.when(i == n_steps - 1)
        def _():
            gather.finish()

    return wrapped


def _norm_in(x, g_mix, shards, tm):
    T = x.shape[0]
    n_t = T // tm
    n_s = len(shards)

    def norm(x_ref, g_ref, h_ref):
        xv = x_ref[...]
        h_ref[...] = (xv * _rstd(xv) * g_ref[...]).astype(BF16)

    outs = pl.pallas_call(
        functools.partial(_gathering(norm, n_t, n_s, core_major=False), n_in=2, n_out=1), name="norm_in", grid=(n_t,),
        in_specs=[pl.BlockSpec((tm, D_MODEL), lambda i: (i, 0)), pl.BlockSpec((1, D_MODEL), lambda i: (0, 0))]
        + [ANY] * n_s,
        out_specs=[pl.BlockSpec((tm, D_MODEL), lambda i: (i, 0))] + [ANY] * n_s,
        out_shape=[_sds((T, D_MODEL), BF16)] + [_sds((N_DEV,) + a.shape, a.dtype) for a in shards],
        scratch_shapes=_Gather.scratch(n_s),
        compiler_params=pltpu.CompilerParams(dimension_semantics=("arbitrary",)),
    )(x, g_mix, *shards)
    return outs[0], list(outs[1:])


def _mix_in(h1, w_in_t, tm, shards):
    T = h1.shape[0]
    n_t = T // tm
    n_s = len(shards)

    def project(h_ref, w_ref, u_ref):
        u_ref[...] = lax.dot_general(h_ref[...], w_ref[...], NT, preferred_element_type=F32)

    outs = pl.pallas_call(
        functools.partial(_gathering(project, n_t, n_s, core_major=True), n_in=2, n_out=1), name="mix_in", grid=(n_t,),
        in_specs=[pl.BlockSpec((tm, D_MODEL), lambda i: (i, 0)), pl.BlockSpec((D_IN, D_MODEL), lambda i: (0, 0))]
        + [ANY] * n_s,
        out_specs=[pl.BlockSpec((tm, D_IN), lambda i: (i, 0))] + [ANY] * n_s,
        out_shape=[_sds((T, D_IN), F32)] + [_sds((N_DEV,) + a.shape, a.dtype) for a in shards],
        scratch_shapes=_Gather.scratch(n_s),
        compiler_params=pltpu.CompilerParams(dimension_semantics=("arbitrary",)),
    )(h1, w_in_t, *shards)
    return outs[0], list(outs[1:])


def _mixer_fwd(u, x, pv, wa, wx, wp, w_out_b, g_ffn, tm, shards=()):
    T = u.shape[0]
    n_s = len(shards)
    n_t = T // tm

    def body(u_ref, x_ref, pv_ref, wa_in, wx_in, wp_in, wo_ref, gf_ref, *rest):
        sh_in, rest = rest[:n_s], rest[n_s:]
        y_ref, hs_ref, hres_ref, h2_ref, saved_ref = rest[:5]
        sh_out, rest = rest[5:5 + n_s], rest[5 + n_s:]
        e_lru, e_pool, a_s, b_s, hc, wa_ref, wx_ref, wp_ref = rest[:8]
        gather = _Gather(sh_in, sh_out, *rest[8:], core_major=True) if n_s else None
        i = pl.program_id(0)

        @pl.when(i == 0)
        def _():
            if gather:
                gather.start()
            e_lru[pl.ds(0, HALO), :] = jnp.zeros((HALO, LRU_W), F32)
            e_pool[pl.ds(0, HALO), :] = jnp.zeros((HALO, POOL_W), F32)
            hc[...] = jnp.zeros((8, LRU_W), F32)
            _fill_block_diag(wa_ref, wa_in)
            _fill_block_diag(wx_ref, wx_in)
            _fill_block_diag(wp_ref, wp_in)

        e_lru[pl.ds(HALO, tm), :] = u_ref[:, 0:LRU_W]
        e_pool[pl.ds(HALO, tm), :] = u_ref[:, 2 * LRU_W:D_IN]
        pv = pv_ref[...]
        xc, r, ig, la = _lru_gates(e_lru, pv, wa_ref, wx_ref, tm)
        for q, val in enumerate((xc, r, ig, la)):
            saved_ref[:, LRU_W * q:LRU_W * (q + 1)] = val
        a, _, mult, _ = _lru_decay(la)
        a_s[...] = a
        b_s[...] = mult * (ig * xc)
        hc[...] = _scan_tile(a_s, b_s, hs_ref, hc[...], tm, reverse=False)
        gl, _ = _gelu_parts(u_ref[:, LRU_W:2 * LRU_W])
        y_lru = hs_ref[...] * gl
        _, zp, _ = _pool_pre(e_pool, pv, wp_ref, tm, i * tm)
        y_pool = zp * pv[ROW_PS:ROW_PS + 1, :]
        yn = jnp.concatenate([y_lru * _rstd(y_lru) * pv[ROW_GL:ROW_GL + 1, :],
                              y_pool * _rstd(y_pool) * pv[ROW_GP:ROW_GP + 1, :]], axis=1).astype(BF16)
        for b in range(N_DEV):
            y_ref[:, 128 * _y_pos(b):128 * (_y_pos(b) + 1)] = yn[:, 128 * b:128 * (b + 1)]
        hr = x_ref[...] + jnp.dot(y_ref[...], wo_ref[...], preferred_element_type=F32)
        hres_ref[...] = hr
        h2_ref[...] = (hr * _rstd(hr) * gf_ref[...]).astype(BF16)
        e_lru[pl.ds(0, HALO), :] = e_lru[pl.ds(tm, HALO), :]
        e_pool[pl.ds(0, HALO), :] = e_pool[pl.ds(tm, HALO), :]

        if gather:
            @pl.when(i == n_t - 1)
            def _():
                gather.finish()

    full = lambda shape: pl.BlockSpec(shape, lambda i: (0,) * len(shape))
    row = lambda w: pl.BlockSpec((tm, w), lambda i: (i, 0))
    outs = pl.pallas_call(
        body, name="mixer_fwd", grid=(n_t,),
        in_specs=[row(D_IN), row(D_MODEL), full((16, LRU_W)), full((8, 64, 64)), full((8, 64, 64)), full((4, 128, 128)),
                  full((D_MODEL, D_MODEL)), full((1, D_MODEL))] + [ANY] * n_s,
        out_specs=[row(D_MODEL), row(LRU_W), row(D_MODEL), row(D_MODEL), row(4 * LRU_W)] + [ANY] * n_s,
        out_shape=[_sds((T, D_MODEL), BF16), _sds((T, LRU_W), F32), _sds((T, D_MODEL), F32), _sds((T, D_MODEL), BF16),
                   _sds((T, 4 * LRU_W), F32)] + [_sds((N_DEV,) + a.shape, a.dtype) for a in shards],
        scratch_shapes=[pltpu.VMEM((HALO + tm, LRU_W), F32), pltpu.VMEM((HALO + tm, POOL_W), F32),
                        pltpu.VMEM((tm, LRU_W), F32), pltpu.VMEM((tm, LRU_W), F32), pltpu.VMEM((8, LRU_W), F32)]
        + [pltpu.VMEM((2, 256, 256), BF16)] * 3 + (_Gather.scratch(n_s) if n_s else []),
        compiler_params=pltpu.CompilerParams(dimension_semantics=("arbitrary",)),
    )(u, x, pv, wa, wx, wp, w_out_b, g_ffn, *shards)
    return outs[0], outs[1], outs[2], outs[3], outs[4], list(outs[5:])


def _ffn_fwd(hres, h2, w1_b, w3_b, w2_b, g_fin, tgt, tm):
    T = hres.shape[0]

    def body(hres_ref, h2_ref, w1_ref, w3_ref, w2_ref, gfin_ref, tgt_ref, g_ref, v_ref, d3_ref, loss_ref, dgfin_ref):
        @pl.when(pl.program_id(0) == 0)
        def _():
            loss_ref[...] = jnp.zeros((8, 128), F32)
            dgfin_ref[...] = jnp.zeros((1, D_MODEL), F32)

        h2 = h2_ref[...]
        h3 = hres_ref[...]
        for lo, hi in FF_CHUNKS:
            g = lax.dot_general(h2, w1_ref[lo:hi, :], NT, preferred_element_type=F32)
            v = lax.dot_general(h2, w3_ref[lo:hi, :], NT, preferred_element_type=F32)
            g_ref[:, lo:hi] = g.astype(BF16)
            v_ref[:, lo:hi] = v.astype(BF16)
            ff = ((g * _sigmoid(g)) * v).astype(BF16)
            h3 = h3 + jnp.dot(ff, w2_ref[lo:hi, :], preferred_element_type=F32)

        rstd = _rstd(h3)
        xh = h3 * rstd
        gfin = gfin_ref[...]
        err = xh * gfin - tgt_ref[...]
        loss_ref[...] += 0.5 * jnp.sum(jnp.mean(err * err, axis=-1, keepdims=True))
        dx, dgain = _rms_bwd(err * (1.0 / D_MODEL), xh, rstd, gfin)
        d3_ref[...] = dx
        dgfin_ref[...] += dgain

    row = lambda w: pl.BlockSpec((tm, w), lambda i: (i, 0))
    const = lambda shape: pl.BlockSpec(shape, lambda i: (0,) * len(shape))
    weight = pl.BlockSpec((D_FF, D_MODEL), lambda i: (0, 0), pipeline_mode=pl.Buffered(1))
    return pl.pallas_call(
        body, name="ffn_fwd", grid=(T // tm,),
        in_specs=[row(D_MODEL), row(D_MODEL), weight, weight, weight, const((1, D_MODEL)), row(D_MODEL)],
        out_specs=[row(D_FF), row(D_FF), row(D_MODEL), const((8, 128)), const((1, D_MODEL))],
        out_shape=[_sds((T, D_FF), BF16), _sds((T, D_FF), BF16),
                   _sds((T, D_MODEL), F32), _sds((8, 128), F32), _sds((1, D_MODEL), F32)],
        compiler_params=pltpu.CompilerParams(dimension_semantics=("arbitrary",)),
    )(hres, h2, w1_b, w3_b, w2_b, g_fin, tgt)


def _ffn_bwd(d3, g, v, w1_b, w3_b, w2_b, hres, g_ffn, tm):
    T = d3.shape[0]

    def body(d3_ref, g_ref, v_ref, w1_ref, w3_ref, w2_ref, hres_ref, gf_ref, dg_ref, dv_ref, ff_ref, d2_ref, dgffn_ref):
        @pl.when(pl.program_id(0) == 0)
        def _():
            dgffn_ref[...] = jnp.zeros((1, D_MODEL), F32)

        d3 = d3_ref[...]
        d3b = d3.astype(BF16)
        dh2 = jnp.zeros((tm, D_MODEL), F32)
        for lo, hi in FF_CHUNKS:
            dff = lax.dot_general(d3b, w2_ref[lo:hi, :], NT, preferred_element_type=F32)
            gv = g_ref[:, lo:hi].astype(F32)
            vv = v_ref[:, lo:hi].astype(F32)
            sg = _sigmoid(gv)
            sl = gv * sg
            dgb = (dff * vv * (sg * (1.0 + gv * (1.0 - sg)))).astype(BF16)
            dvb = (dff * sl).astype(BF16)
            dg_ref[:, lo:hi] = dgb
            dv_ref[:, lo:hi] = dvb
            ff_ref[:, lo:hi] = (sl * vv).astype(BF16)
            dh2 = dh2 + (jnp.dot(dgb, w1_ref[lo:hi, :], preferred_element_type=F32)
                         + jnp.dot(dvb, w3_ref[lo:hi, :], preferred_element_type=F32))

        hr = hres_ref[...]
        rstd = _rstd(hr)
        dx, dgain = _rms_bwd(dh2, hr * rstd, rstd, gf_ref[...])
        d2_ref[...] = d3 + dx
        dgffn_ref[...] += dgain

    row = lambda w: pl.BlockSpec((tm, w), lambda i: (i, 0))
    const = lambda shape: pl.BlockSpec(shape, lambda i: (0,) * len(shape))
    weight = pl.BlockSpec((D_FF, D_MODEL), lambda i: (0, 0), pipeline_mode=pl.Buffered(1))
    return pl.pallas_call(
        body, name="ffn_bwd", grid=(T // tm,),
        in_specs=[row(D_MODEL), row(D_FF), row(D_FF), weight, weight, weight, row(D_MODEL), const((1, D_MODEL))],
        out_specs=[row(D_FF), row(D_FF), row(D_FF), row(D_MODEL), const((1, D_MODEL))],
        out_shape=[_sds((T, D_FF), BF16), _sds((T, D_FF), BF16), _sds((T, D_FF), BF16),
                   _sds((T, D_MODEL), F32), _sds((1, D_MODEL), F32)],
        compiler_params=pltpu.CompilerParams(dimension_semantics=("arbitrary",)),
    )(d3, g, v, w1_b, w3_b, w2_b, hres, g_ffn)


def _at_b_pair(a, b, c_arr, name, tk, gather=()):
    T, M = a.shape
    N = b.shape[1]
    hm, n_k = M // 2, T // tk
    n_g = len(gather)

    def body(c_ref, a_ref, b_ref, *rest):
        g_in, o_ref, rest = rest[:n_g], rest[n_g], rest[n_g + 1:]
        g_out, rest = rest[:n_g], rest[n_g:]
        acc, landed, send_sem, recv_sem = rest[:4]
        ag = _Gather(g_in, g_out, *rest[4:]) if n_g else None
        ph, k = pl.program_id(0), pl.program_id(1)

        def hand_over():
            x, y, c, _ = _place()
            return pltpu.make_async_remote_copy(
                src_ref=acc.at[0], dst_ref=landed, send_sem=send_sem, recv_sem=recv_sem,
                device_id=(x, y, 1 - c), device_id_type=MESH)

        if ag:
            @pl.when((ph == 0) & (k == 0))
            def _():
                ag.start()

        @pl.when(k == 0)
        def _():
            acc[ph] = jnp.zeros((hm, N), F32)

        acc[ph] += lax.dot_general(a_ref[...].astype(BF16), b_ref[...].astype(BF16), TN, preferred_element_type=F32)

        @pl.when((ph == 0) & (k == n_k - 1))
        def _():
            hand_over().start()

        @pl.when((ph == 1) & (k == n_k - 1))
        def _():
            copy = hand_over()
            copy.wait_recv()
            o_ref[...] = (acc[1] + landed[...]).astype(BF16)
            copy.wait_send()
            if ag:
                ag.finish()

    outs = pl.pallas_call(
        body, name=name,
        grid_spec=pltpu.PrefetchScalarGridSpec(
            num_scalar_prefetch=1, grid=(2, n_k),
            in_specs=[pl.BlockSpec((tk, hm), lambda ph, k, c_ref: (k, (ph + 1 - c_ref[0]) % 2)),
                      pl.BlockSpec((tk, N), lambda ph, k, c_ref: (k, 0))] + [ANY] * n_g,
            out_specs=[pl.BlockSpec((hm, N), lambda ph, k, c_ref: (0, 0))] + [ANY] * n_g,
            scratch_shapes=[pltpu.VMEM((2, hm, N), F32), pltpu.VMEM((hm, N), F32),
                            pltpu.SemaphoreType.DMA, pltpu.SemaphoreType.DMA] + (_Gather.scratch(n_g) if n_g else [])),
        out_shape=[_sds((hm, N), BF16)] + [_sds((N_DEV,) + g.shape, g.dtype) for g in gather],
        compiler_params=pltpu.CompilerParams(dimension_semantics=("arbitrary", "arbitrary")),
    )(c_arr, a, b, *gather)
    return (outs[0], list(outs[1:])) if n_g else outs[0]


def _mixer_bwd(d2, u, hs, saved, pv, wa, wx, wp, w_out_b, tm, chip_sums=()):
    T = u.shape[0]
    n_t = T // tm
    n_x = len(chip_sums)

    def body(d2_ref, u_ref, uh_ref, hs_ref, hh_ref, saved_ref, pv_ref, wa_in, wx_in, wp_in, wo_ref, *rest):
        x_in, rest = rest[:n_x], rest[n_x:]
        du_ref, sg_ref = rest[:2]
        x_out, rest = rest[2:2 + n_x], rest[2 + n_x:]
        e_pool, e_h, a_s, b_s, dh_s, mu_s, f_x, f_p, mc, cx, cp = rest[:11]
        wa_ref, wx_ref, wp_ref, vacc_ref, dwa_ref, dwx_ref, dwp_ref = rest[11:18]
        exchange = _ChipExchange(x_in, x_out, *rest[18:]) if n_x else None
        s = pl.program_id(0)
        it = n_t - 1 - s

        @pl.when(s == 0)
        def _():
            if exchange:
                exchange.start()
            mc[...] = jnp.zeros((8, LRU_W), F32)
            cx[...] = jnp.zeros((8, LRU_W), F32)
            cp[...] = jnp.zeros((HALO, POOL_W), F32)
            vacc_ref[...] = jnp.zeros((16, LRU_W), F32)
            dwa_ref[...] = jnp.zeros((2, 256, 256), F32)
            dwx_ref[...] = jnp.zeros((2, 256, 256), F32)
            dwp_ref[...] = jnp.zeros((2, 256, 256), F32)
            _fill_block_diag(wa_ref, wa_in)
            _fill_block_diag(wx_ref, wx_in)
            _fill_block_diag(wp_ref, wp_in)

        first = it == 0
        e_pool[pl.ds(0, HALO), :] = jnp.where(first, 0.0, uh_ref[...])
        e_pool[pl.ds(HALO, tm), :] = u_ref[:, 2 * LRU_W:D_IN]
        e_h[pl.ds(0, 8), :] = jnp.where(first, 0.0, hh_ref[...])
        e_h[pl.ds(8, tm), :] = hs_ref[...]
        pv = pv_ref[...]
        saved = lambda q: saved_ref[:, LRU_W * q:LRU_W * (q + 1)]

        dyn = lax.dot_general(d2_ref[...].astype(BF16), wo_ref[...], NT, preferred_element_type=F32)
        dyn = jnp.concatenate([dyn[:, 128 * _y_pos(b):128 * (_y_pos(b) + 1)] for b in range(N_DEV)], axis=1)

        h = hs_ref[...]
        ug = u_ref[:, LRU_W:2 * LRU_W]
        gl, dgl = _gelu_parts(ug)
        y_lru = h * gl
        rstd_l = _rstd(y_lru)
        dy_lru, d_gain_l = _rms_bwd(dyn[:, 0:LRU_W], y_lru * rstd_l, rstd_l, pv[ROW_GL:ROW_GL + 1, :])
        dh = dy_lru * gl
        du_ref[:, LRU_W:2 * LRU_W] = (dy_lru * h * dgl).astype(BF16)
        a_s[...] = jnp.exp(saved(3))
        b_s[...] = a_s[...] * dh
        dh_s[...] = dh
        mu_s[pl.ds(tm, 8), :] = mc[...]
        mc[...] = _scan_tile(a_s, b_s, mu_s, mc[...], tm, reverse=True)
        xc, r, ig = saved(0), saved(1), saved(2)
        a, om, mult, rmult = _lru_decay(saved(3))
        lam_t = dh_s[...] + mu_s[pl.ds(1, tm), :]
        da = lam_t * e_h[pl.ds(7, tm), :]
        dmult = lam_t * (ig * xc)
        di = lam_t * (mult * xc)
        dxc = lam_t * (mult * ig)
        dla = da * a - jnp.where(om > 1e-12, dmult * ((a * a) * rmult), 0.0)
        dra = (dla * (-LRU_C * _softplus_neg_lambda(pv))) * (r * (1.0 - r))
        dia = di * (ig * (1.0 - ig))
        drab = dra.astype(BF16)
        diab = dia.astype(BF16)
        xcb = xc.astype(BF16)
        dxc = dxc + _bd_t(drab, wa_ref) + _bd_t(diab, wx_ref)
        dwa_ref[...] += _bd_grad(xcb, drab)
        dwx_ref[...] += _bd_grad(xcb, diab)
        sig_neg_lam = _sigmoid(-pv[ROW_LAM:ROW_LAM + 1, :])
        d_lam = jnp.sum(dla * r, axis=0, keepdims=True) * (LRU_C * sig_neg_lam)

        f_x[pl.ds(0, tm), :] = dxc
        f_x[pl.ds(tm, 8), :] = cx[...]
        du_lru = jnp.zeros((tm, LRU_W), F32)
        u_lru = u_ref[:, 0:LRU_W]
        d_cw = []
        for k in range(4):
            later = f_x[pl.ds(3 - k, tm), :]
            du_lru = du_lru + later * pv[ROW_CW + k:ROW_CW + k + 1, :]
            d_cw.append(jnp.sum(later * u_lru, axis=0, keepdims=True))
        du_ref[:, 0:LRU_W] = du_lru.astype(BF16)
        cx[...] = f_x[pl.ds(0, 8), :]

        pooled_b, zp, inv_cnts = _pool_pre(e_pool, pv, wp_ref, tm, it * tm)
        ps = pv[ROW_PS:ROW_PS + 1, :]
        y_pool = zp * ps
        rstd_p = _rstd(y_pool)
        dy_pool, d_gain_p = _rms_bwd(dyn[:, LRU_W:D_MODEL], y_pool * rstd_p, rstd_p, pv[ROW_GP:ROW_GP + 1, :])
        dz = dy_pool * ps
        dzb = dz.astype(BF16)
        dwp_ref[...] += _bd_grad(pooled_b, dzb)
        dpooled = _bd_t(dzb, wp_ref)
        for g, w in enumerate(POOL_WINDOWS):
            f_p[pl.ds(0, tm), pl.ds(128 * g, 128)] = _over_count(dpooled[:, 128 * g:128 * (g + 1)], w, inv_cnts[g])
        f_p[pl.ds(tm, HALO), :] = cp[...]
        for g, w in enumerate(POOL_WINDOWS):
            acc = _window_sum(f_p[:, pl.ds(128 * g, 128)], w, back=False)[0:tm, :]
            du_ref[:, 2 * LRU_W + 128 * g:2 * LRU_W + 128 * (g + 1)] = (
                acc - dpooled[:, 128 * g:128 * (g + 1)]).astype(BF16)
        cp[...] = f_p[pl.ds(0, HALO), :]

        rows = d_cw + [
            jnp.sum(dxc, axis=0, keepdims=True),
            jnp.sum(dra, axis=0, keepdims=True),
            jnp.sum(dia, axis=0, keepdims=True),
            d_lam,
            jnp.sum(dz, axis=0, keepdims=True),
            jnp.sum(dy_pool * zp, axis=0, keepdims=True),
            d_gain_l, d_gain_p,
            jnp.zeros((4, LRU_W), F32),
        ]
        vacc_ref[...] += jnp.concatenate(rows, axis=0)

        @pl.when(s == n_t - 1)
        def _():
            sg_ref[SG_VEC:SG_VEC + 16, :] = vacc_ref[:, 0:256]
            sg_ref[SG_VEC + 16:SG_VEC + 32, :] = vacc_ref[:, 256:512]
            for half in range(2):
                sg_ref[SG_WA + 64 * half:SG_WA + 64 * (half + 1), :] = _diag_pack(dwa_ref[half], 64)
                sg_ref[SG_WX + 64 * half:SG_WX + 64 * (half + 1), :] = _diag_pack(dwx_ref[half], 64)
                sg_ref[SG_WP + 128 * half:SG_WP + 128 * (half + 1), :] = _diag_pack(dwp_ref[half], 128)
            if exchange:
                exchange.finish()

    rev = lambda w: pl.BlockSpec((tm, w), lambda s: (n_t - 1 - s, 0))
    full = lambda shape: pl.BlockSpec(shape, lambda s: (0,) * len(shape))
    outs = pl.pallas_call(
        body, name="mixer_bwd", grid=(n_t,),
        in_specs=[rev(D_MODEL), rev(D_IN),
                  pl.BlockSpec((HALO, POOL_W), lambda s: (jnp.maximum((n_t - 1 - s) * (tm // HALO) - 1, 0), 2)),
                  rev(LRU_W),
                  pl.BlockSpec((8, LRU_W), lambda s: (jnp.maximum((n_t - 1 - s) * (tm // 8) - 1, 0), 0)),
                  rev(4 * LRU_W), full((16, LRU_W)), full((8, 64, 64)), full((8, 64, 64)), full((4, 128, 128)),
                  full((D_MODEL, D_MODEL))] + [ANY] * n_x,
        out_specs=[rev(D_IN), full((SG_ROWS, 256))] + [ANY] * n_x,
        out_shape=[_sds((T, D_IN), BF16), _sds((SG_ROWS, 256), F32)] + [_sds(a.shape, a.dtype) for a in chip_sums],
        scratch_shapes=[pltpu.VMEM((HALO + tm, POOL_W), F32),
                        pltpu.VMEM((8 + tm, LRU_W), F32)] + [pltpu.VMEM((tm, LRU_W), F32)] * 3 + [
                        pltpu.VMEM((tm + 8, LRU_W), F32), pltpu.VMEM((tm + 8, LRU_W), F32),
                        pltpu.VMEM((tm + HALO, POOL_W), F32), pltpu.VMEM((8, LRU_W), F32),
                        pltpu.VMEM((8, LRU_W), F32), pltpu.VMEM((HALO, POOL_W), F32)]
        + [pltpu.VMEM((2, 256, 256), BF16)] * 3 + [pltpu.VMEM((16, LRU_W), F32)] + [pltpu.VMEM((2, 256, 256), F32)] * 3
        + (_ChipExchange.scratch(n_x) if n_x else []),
        compiler_params=pltpu.CompilerParams(dimension_semantics=("arbitrary",)),
    )(d2, u, u, hs, hs, saved, pv, wa, wx, wp, w_out_b, *chip_sums)
    return outs[0], outs[1], list(outs[2:])


def _mix_in_bwd(du, x, d2, w_in_t, g_mix, tm):
    T = x.shape[0]

    def body(du_ref, x_ref, d2_ref, w_ref, g_ref, dx_ref, dg_ref):
        @pl.when(pl.program_id(0) == 0)
        def _():
            dg_ref[...] = jnp.zeros((1, D_MODEL), F32)

        dh = jnp.dot(du_ref[...], w_ref[...], preferred_element_type=F32)
        xv = x_ref[...]
        rstd = _rstd(xv)
        dx, dgain = _rms_bwd(dh, xv * rstd, rstd, g_ref[...])
        dx_ref[...] = d2_ref[...] + dx
        dg_ref[...] += dgain

    row = lambda w: pl.BlockSpec((tm, w), lambda i: (i, 0))
    const = lambda shape: pl.BlockSpec(shape, lambda i: (0,) * len(shape))
    return pl.pallas_call(
        body, name="mix_in_bwd", grid=(T // tm,),
        in_specs=[row(D_IN), row(D_MODEL), row(D_MODEL), const((D_IN, D_MODEL)), const((1, D_MODEL))],
        out_specs=[row(D_MODEL), const((1, D_MODEL))],
        out_shape=[_sds((T, D_MODEL), F32), _sds((1, D_MODEL), F32)],
        compiler_params=pltpu.CompilerParams(dimension_semantics=("arbitrary",)),
    )(du, x, d2, w_in_t, g_mix)


def _adamw(w, g, m, v):
    m = ADAM_B1 * m + (1.0 - ADAM_B1) * g
    v = ADAM_B2 * v + (1.0 - ADAM_B2) * (g * g)
    m_hat = m / (1.0 - ADAM_B1 ** ADAM_STEP)
    v_hat = v / (1.0 - ADAM_B2 ** ADAM_STEP)
    delta = -ADAM_LR * (m_hat / (jnp.sqrt(v_hat) + ADAM_EPS) + ADAM_WD * w)
    return delta, m, v


def _adam_shards(ws, ms, vs, parts):
    n = len(ws)
    n_blk = [w.shape[0] // ADAM_ROWS for w in ws]

    def body(*refs):
        w_refs, m_refs, v_refs, p_refs, outs = (refs[:n], refs[n:2 * n], refs[2 * n:3 * n], refs[3 * n:4 * n],
                                                refs[4 * n:])
        i = pl.program_id(0)
        for a in range(n):
            @pl.when(i < n_blk[a])
            def _(a=a):
                g = p_refs[a][0].astype(F32)
                for j in range(1, 4):
                    g = g + p_refs[a][j].astype(F32)
                delta, new_m, new_v = _adamw(w_refs[a][...], g, m_refs[a][...], v_refs[a][...])
                for kind, val in enumerate((g, delta, new_m, new_v)):
                    outs[4 * a + kind][...] = val

    blk = lambda a: pl.BlockSpec((ADAM_ROWS, D_MODEL), lambda i: (jnp.minimum(i, n_blk[a] - 1), 0))
    part_blk = lambda a: pl.BlockSpec((4, ADAM_ROWS, D_MODEL), lambda i: (0, jnp.minimum(i, n_blk[a] - 1), 0))
    res = pl.pallas_call(
        body, name="adam_shards", grid=(max(n_blk),),
        in_specs=[blk(a) for a in range(n)] * 3 + [part_blk(a) for a in range(n)],
        out_specs=[blk(a) for a in range(n) for _ in range(4)],
        out_shape=[_sds(w.shape, F32) for w in ws for _ in range(4)],
        compiler_params=pltpu.CompilerParams(dimension_semantics=("arbitrary",)),
    )(*ws, *ms, *vs, *parts)
    return [tuple(res[4 * a:4 * a + 4]) for a in range(n)]


SMALL_PARAMS = [("norm_mix_g", (1, D_MODEL)), ("conv_w", (1, 4, 64)), ("conv_b", (1, LRU_W)),
                ("gate_a_w", (1, 8, 64, 64)), ("gate_a_b", (1, LRU_W)), ("gate_x_w", (1, 8, 64, 64)),
                ("gate_x_b", (1, LRU_W)), ("lru_lambda", (1, LRU_W)), ("pool_w", (1, 4, 128, 128)),
                ("pool_b", (1, POOL_W)), ("pool_scale", (1, POOL_W)), ("norm_lru_g", (1, LRU_W)),
                ("norm_pool_g", (1, POOL_W)), ("norm_ffn_g", (1, D_MODEL)), ("final_norm_g", (1, D_MODEL))]
VEC_ROW = dict(conv_b=ROW_CB, gate_a_b=ROW_BA, gate_x_b=ROW_BX, lru_lambda=ROW_LAM, pool_b=ROW_PB, pool_scale=ROW_PS,
               norm_lru_g=ROW_GL, norm_pool_g=ROW_GP)
WHOLE = (Ellipsis,)


def _unpack_mixer_grads(sg, dev):
    vec = jnp.concatenate([sg[SG_VEC:SG_VEC + 16], sg[SG_VEC + 16:SG_VEC + 32]], axis=1)
    out = {nm: [(WHOLE, vec[r:r + 1])] for nm, r in VEC_ROW.items()}
    own = jnp.zeros((4, 64), F32)
    for d in range(N_DEV):
        own = jnp.where(dev == d, vec[ROW_CW:ROW_CW + 4, 64 * d:64 * (d + 1)], own)
    out["conv_w"] = [((0,), own)]
    for nm, row0 in (("gate_a_w", SG_WA), ("gate_x_w", SG_WX)):
        out[nm] = [((0, b), sg[row0 + 64 * (b // 4):row0 + 64 * (b // 4 + 1), 64 * (b % 4):64 * (b % 4 + 1)])
                   for b in range(8)]
    out["pool_w"] = [((0, b), sg[SG_WP + 128 * (b // 2):SG_WP + 128 * (b // 2 + 1), 128 * (b % 2):128 * (b % 2 + 1)])
                     for b in range(4)]
    return out


def _adam_small(parts, w, m, v):
    names = [nm for nm, _ in SMALL_PARAMS]
    n = len(names)

    def body(sg_ref, gm_ref, gf_ref, gn_ref, ls_ref, *rest):
        w_refs, m_refs, v_refs, outs = rest[:n], rest[n:2 * n], rest[2 * n:3 * n], rest[3 * n:]
        dev = 4 * lax.axis_index("x") + 2 * lax.axis_index("y") + lax.axis_index("c")

        def total(ref):
            acc = ref[0]
            for d in range(1, N_DEV):
                acc = acc + ref[d]
            return acc

        pieces = _unpack_mixer_grads(total(sg_ref), dev)
        pieces["norm_mix_g"] = [(WHOLE, total(gm_ref))]
        pieces["norm_ffn_g"] = [(WHOLE, total(gf_ref))]
        pieces["final_norm_g"] = [(WHOLE, total(gn_ref))]
        for i, nm in enumerate(names):
            for idx, g in pieces[nm]:
                delta, new_m, new_v = _adamw(w_refs[i][idx], g, m_refs[i][idx], v_refs[i][idx])
                for kind, val in enumerate((g, delta, new_m, new_v)):
                    outs[4 * i + kind][idx] = val
        outs[4 * n][...] = total(ls_ref)

    shapes = [_sds(shape, F32) for _, shape in SMALL_PARAMS for _ in range(4)] + [_sds((8, 128), F32)]
    res = pl.pallas_call(body, name="adam_small", out_shape=shapes)(
        *parts, *[w[nm] for nm in names], *[m[nm] for nm in names], *[v[nm] for nm in names])
    return {nm: tuple(res[4 * i:4 * i + 4]) for i, nm in enumerate(names)}, res[4 * n][0, 0]


def _vec_rows(conv_w_full, conv_b, ba, bx, lam, pb, ps, gl, gp):
    return jnp.concatenate([conv_w_full, conv_b, ba, bx, lam, pb, ps, gl, gp, jnp.zeros((4, LRU_W), F32)], axis=0)


WEIGHT_ORDER = ['norm_mix_g', 'w_in', 'conv_w', 'conv_b', 'gate_a_w', 'gate_a_b', 'gate_x_w', 'gate_x_b', 'lru_lambda',
                'pool_w', 'pool_b', 'pool_scale', 'norm_lru_g', 'norm_pool_g', 'w_out', 'norm_ffn_g', 'ffn_w1', 'ffn_w3',
                'ffn_w2', 'final_norm_g']


def kernel(x, norm_mix_g, w_in, conv_w, conv_b, gate_a_w, gate_a_b, gate_x_w, gate_x_b, lru_lambda, pool_w, pool_b, pool_scale, norm_lru_g, norm_pool_g, w_out, norm_ffn_g, ffn_w1, ffn_w3, ffn_w2, final_norm_g, loss_target, m_norm_mix_g, m_w_in, m_conv_w, m_conv_b, m_gate_a_w, m_gate_a_b, m_gate_x_w, m_gate_x_b, m_lru_lambda, m_pool_w, m_pool_b, m_pool_scale, m_norm_lru_g, m_norm_pool_g, m_w_out, m_norm_ffn_g, m_ffn_w1, m_ffn_w3, m_ffn_w2, m_final_norm_g, v_norm_mix_g, v_w_in, v_conv_w, v_conv_b, v_gate_a_w, v_gate_a_b, v_gate_x_w, v_gate_x_b, v_lru_lambda, v_pool_w, v_pool_b, v_pool_scale, v_norm_lru_g, v_norm_pool_g, v_w_out, v_norm_ffn_g, v_ffn_w1, v_ffn_w3, v_ffn_w2, v_final_norm_g):
    ac = lax.axis_index("c")
    tm, tmx, tn, tk = 512, 512, 1408, 1024
    tm_in = 1024
    xs, tgt = x[0], loss_target[0]
    g_fin = final_norm_g.reshape(1, D_MODEL)
    c_arr = jnp.reshape(ac, (1,)).astype(jnp.int32)

    tr = lambda w: jnp.swapaxes(w[0], 0, 1)
    own = lambda w: w[0]
    bf = lambda a: a.astype(BF16)

    h1, (g_in, g_conv) = _norm_in(xs, norm_mix_g, [bf(tr(w_in)), conv_w[0]], tm_in)
    w_in_t = g_in.reshape(D_IN, D_MODEL)
    u, (g_out, g_w1) = _mix_in(h1, w_in_t, tm, shards=[bf(own(w_out)), bf(tr(ffn_w1))])
    conv_w_full = g_conv.transpose(1, 0, 2).reshape(4, LRU_W)
    pv = _vec_rows(conv_w_full, conv_b, gate_a_b, gate_x_b, lru_lambda, pool_b, pool_scale, norm_lru_g, norm_pool_g)
    wa, wx, wp = gate_a_w[0], gate_x_w[0], pool_w[0]
    w_out_b = g_out.reshape(D_MODEL, D_MODEL)
    y, hs, hres, h2, saved, (g_w3, g_w2) = _mixer_fwd(
        u, xs, pv, wa, wx, wp, w_out_b, norm_ffn_g, tmx, shards=[bf(tr(ffn_w3)), bf(own(ffn_w2))])
    w1_t, w3_t, w2_b = g_w1.reshape(D_FF, D_MODEL), g_w3.reshape(D_FF, D_MODEL), g_w2.reshape(D_FF, D_MODEL)
    g, v, d3, loss_acc, d_gfin = _ffn_fwd(hres, h2, w1_t, w3_t, w2_b, g_fin, tgt, tm)

    dg, dv, ff, d2, d_gffn = _ffn_bwd(d3, g, v, w1_t, w3_t, w2_b, hres, norm_ffn_g, tm // 2)
    chips = lambda a: a.reshape(4, a.shape[0] // 4, a.shape[1])
    early_sums = [chips(_at_b_pair(y, d2, c_arr, "grad_w_out", 2 * tk)), chips(_at_b_pair(dg, h2, c_arr, "grad_w1", tk)),
                  chips(_at_b_pair(dv, h2, c_arr, "grad_w3", tk)), chips(_at_b_pair(ff, d3, c_arr, "grad_w2", tk))]
    du, d_mixer, early_parts = _mixer_bwd(d2, u, hs, saved, pv, wa, wx, wp, w_out_b, tmx, chip_sums=early_sums)
    grad_x, d_gmix = _mix_in_bwd(du, xs, d2, w_in_t, norm_mix_g, tm_in)
    d_win, small_parts = _at_b_pair(du, h1, c_arr, "grad_w_in", 2 * tk,
                                    gather=[d_mixer, d_gmix, d_gffn, d_gfin, loss_acc])
    parts = [_half_exchange(chips(d_win), "grads_to_chips_w_in")] + list(early_parts)

    res = {}
    shard_w = dict(w_in=(w_in, m_w_in, v_w_in, tr), w_out=(w_out, m_w_out, v_w_out, own),
                   ffn_w1=(ffn_w1, m_ffn_w1, v_ffn_w1, tr), ffn_w3=(ffn_w3, m_ffn_w3, v_ffn_w3, tr),
                   ffn_w2=(ffn_w2, m_ffn_w2, v_ffn_w2, own))
    shard_res = _adam_shards([view(w) for w, _, _, view in shard_w.values()],
                             [view(m) for _, m, _, view in shard_w.values()],
                             [view(v) for _, _, v, view in shard_w.values()], parts)
    for (nm, (_, _, _, view)), outs in zip(shard_w.items(), shard_res):
        res[nm] = [(jnp.swapaxes(o, 0, 1) if view is tr else o)[None] for o in outs]

    row = lambda a: a.reshape(1, D_MODEL)
    small = lambda gm, cw, cb, wa_, ba, wx_, bx, lam, pw, pb, ps, gl, gp, gf, gn: dict(
        norm_mix_g=gm, conv_w=cw, conv_b=cb, gate_a_w=wa_, gate_a_b=ba, gate_x_w=wx_, gate_x_b=bx, lru_lambda=lam,
        pool_w=pw, pool_b=pb, pool_scale=ps, norm_lru_g=gl, norm_pool_g=gp, norm_ffn_g=gf, final_norm_g=row(gn))
    small_res, loss = _adam_small(
        small_parts,
        small(norm_mix_g, conv_w, conv_b, gate_a_w, gate_a_b, gate_x_w, gate_x_b, lru_lambda, pool_w, pool_b,
              pool_scale, norm_lru_g, norm_pool_g, norm_ffn_g, final_norm_g),
        small(m_norm_mix_g, m_conv_w, m_conv_b, m_gate_a_w, m_gate_a_b, m_gate_x_w, m_gate_x_b, m_lru_lambda, m_pool_w,
              m_pool_b, m_pool_scale, m_norm_lru_g, m_norm_pool_g, m_norm_ffn_g, m_final_norm_g),
        small(v_norm_mix_g, v_conv_w, v_conv_b, v_gate_a_w, v_gate_a_b, v_gate_x_w, v_gate_x_b, v_lru_lambda, v_pool_w,
              v_pool_b, v_pool_scale, v_norm_lru_g, v_norm_pool_g, v_norm_ffn_g, v_final_norm_g))
    for nm, outs in small_res.items():
        res[nm] = [o.reshape(D_MODEL) for o in outs] if nm == "final_norm_g" else list(outs)

    out = [loss, grad_x[None]]
    for kind in range(4):
        out += [res[nm][kind] for nm in WEIGHT_ORDER]
    return tuple(out)
```

```python
import functools

import jax
import jax.numpy as jnp
from jax import lax
from jax.experimental import pallas as pl
from jax.experimental.pallas import tpu as pltpu

F32 = jnp.float32
BF16 = jnp.bfloat16

D_MODEL = 1024
LRU_W = 512
POOL_W = 512
D_IN = 1536
D_FF = 2816
POOL_WINDOWS = (2, 4, 8, 16)
EPS = 1e-6
LRU_C = 8.0
N_DEV = 8
HALO = 16
SCAN_UNROLL = 8
ADAM_ROWS = 32
FF_CHUNKS = ((0, 1536), (1536, 2816))

ADAM_LR = 0.001
ADAM_B1 = 0.9
ADAM_B2 = 0.999
ADAM_EPS = 1e-08
ADAM_WD = 0.01
ADAM_STEP = 10

ROW_CW, ROW_CB, ROW_BA, ROW_BX, ROW_LAM, ROW_PB, ROW_PS, ROW_GL, ROW_GP = 0, 4, 5, 6, 7, 8, 9, 10, 11
SG_VEC, SG_WA, SG_WX, SG_WP, SG_ROWS = 0, 32, 160, 288, 544

NT = (((1,), (1,)), ((), ()))
TN = (((0,), (0,)), ((), ()))


def _sds(shape, dtype):
    return jax.ShapeDtypeStruct(shape, dtype)


def _sigmoid(x):
    return 0.5 * jnp.tanh(0.5 * x) + 0.5


def _gelu_parts(x):
    c = 0.7978845608028654
    inner = c * (x + 0.044715 * (x * x * x))
    th = jnp.tanh(inner)
    g = 0.5 * x * (1.0 + th)
    dg = 0.5 * (1.0 + th) + 0.5 * x * (1.0 - th * th) * (c * (1.0 + 3.0 * 0.044715 * (x * x)))
    return g, dg


def _window_sum(ext, w, back):
    n = ext.shape[0]
    s, k = ext, 1
    while k < w:
        s = s + pltpu.roll(s, k if back else n - k, 0)
        k *= 2
    return s


def _rstd(x):
    return lax.rsqrt(jnp.mean(x * x, axis=-1, keepdims=True) + EPS)


def _rms_bwd(dy, xhat, rstd, gain):
    dxh = dy * gain
    dx = rstd * (dxh - xhat * jnp.mean(dxh * xhat, axis=-1, keepdims=True))
    return dx, jnp.sum(dy * xhat, axis=0, keepdims=True)


def _bd(xb, w_ref):
    return jnp.concatenate(
        [jnp.dot(xb[:, :256], w_ref[0], preferred_element_type=F32),
         jnp.dot(xb[:, 256:], w_ref[1], preferred_element_type=F32)], axis=1)


def _bd_t(xb, w_ref):
    return jnp.concatenate(
        [lax.dot_general(xb[:, :256], w_ref[0], NT, preferred_element_type=F32),
         lax.dot_general(xb[:, 256:], w_ref[1], NT, preferred_element_type=F32)], axis=1)


def _bd_grad(xb, db):
    return jnp.stack(
        [lax.dot_general(xb[:, :256], db[:, :256], TN, preferred_element_type=F32),
         lax.dot_general(xb[:, 256:], db[:, 256:], TN, preferred_element_type=F32)], axis=0)


def _fill_block_diag(dst, src_ref):
    n, k, _ = src_ref.shape
    dst[...] = jnp.zeros(dst.shape, BF16)
    for b in range(n):
        p, q = divmod(b, 256 // k)
        dst[p, q * k:(q + 1) * k, q * k:(q + 1) * k] = src_ref[b].astype(BF16)


def _diag_pack(w, k):
    lane = lax.broadcasted_iota(jnp.int32, (k, 256), 1)
    out = w[0:k]
    for q in range(1, 256 // k):
        out = jnp.where(lane >= q * k, w[q * k:(q + 1) * k], out)
    return out


def _y_pos(b):
    return 4 * (b % 2) + b // 2


def _softplus_neg_lambda(pv):
    z = -pv[ROW_LAM:ROW_LAM + 1, :]
    return jnp.maximum(z, 0.0) + jnp.log(1.0 + jnp.exp(-jnp.abs(z)))


def _lru_gates(e_lru, pv, wa_ref, wx_ref, tm):
    xc = pv[ROW_CB:ROW_CB + 1, :]
    for k in range(4):
        xc = xc + e_lru[pl.ds(HALO - 3 + k, tm), :] * pv[ROW_CW + k:ROW_CW + k + 1, :]
    xcb = xc.astype(BF16)
    r = _sigmoid(_bd(xcb, wa_ref) + pv[ROW_BA:ROW_BA + 1, :])
    ig = _sigmoid(_bd(xcb, wx_ref) + pv[ROW_BX:ROW_BX + 1, :])
    return xc, r, ig, (-LRU_C * r) * _softplus_neg_lambda(pv)


def _lru_decay(la):
    a = jnp.exp(la)
    om = -jnp.tanh(la) * (1.0 + a * a)
    omc = jnp.maximum(om, 1e-12)
    rmult = lax.rsqrt(omc)
    return a, om, omc * rmult, rmult


def _over_count(v, w, inv_head):
    return jnp.concatenate([v[0:HALO] * inv_head, v[HALO:] * (1.0 / w)], axis=0)


def _pool_pre(e_pool, pv, wp_ref, tm, t0):
    t_head = t0 + lax.broadcasted_iota(jnp.int32, (HALO, 1), 0)
    parts, inv_heads = [], []
    for g, w in enumerate(POOL_WINDOWS):
        ext = e_pool[:, pl.ds(128 * g, 128)]
        s = _window_sum(ext, w, back=True)[HALO:, :]
        inv_head = 1.0 / jnp.minimum(t_head + 1, w).astype(F32)
        inv_heads.append(inv_head)
        parts.append(_over_count(s, w, inv_head) - ext[HALO:, :])
    pooled = jnp.concatenate(parts, axis=1)
    pooled_b = pooled.astype(BF16)
    zp = _bd(pooled_b, wp_ref) + pv[ROW_PB:ROW_PB + 1, :]
    return pooled_b, zp, inv_heads


def _scan_tile(a_ref, b_ref, out_ref, carry, tm, reverse):
    row = lax.broadcasted_iota(jnp.int32, (8, LRU_W), 0)
    nblk = tm // 8

    def local_scan(blk):
        r0 = pl.multiple_of(blk * 8, 8)
        av = a_ref[pl.ds(r0, 8), :]
        bv = b_ref[pl.ds(r0, 8), :]
        for d in (1, 2, 4):
            sh = (8 - d) if reverse else d
            a_s = pltpu.roll(av, sh, 0)
            b_s = pltpu.roll(bv, sh, 0)
            m = (row < 8 - d) if reverse else (row >= d)
            bv = jnp.where(m, av * b_s + bv, bv)
            av = jnp.where(m, av * a_s, av)
        return r0, av, bv

    def step(i, hin):
        local = [local_scan((nblk - 1 - (i * SCAN_UNROLL + j)) if reverse else (i * SCAN_UNROLL + j))
                 for j in range(SCAN_UNROLL)]
        for r0, av, bv in local:
            hv = av * hin + bv
            out_ref[pl.ds(r0, 8), :] = hv
            hin = jnp.broadcast_to(hv[0:1, :] if reverse else hv[7:8, :], (8, LRU_W))
        return hin

    return lax.fori_loop(0, nblk // SCAN_UNROLL, step, carry)


MESH = pl.DeviceIdType.MESH
ANY = pl.BlockSpec(memory_space=pl.ANY)


def _place():
    x, y, c = lax.axis_index("x"), lax.axis_index("y"), lax.axis_index("c")
    chips = [(1 - x, y), (x, 1 - y), (1 - x, 1 - y)]
    return x, y, c, chips


class _Gather:
    def __init__(self, ins, outs, send_sems, recv_sems, local_sems, core_major=False):
        self.ins, self.outs, self.n = ins, outs, len(ins)
        self.send_sems, self.recv_sems, self.local_sems = send_sems, recv_sems, local_sems
        self.core_major = core_major

    @staticmethod
    def scratch(n):
        return [pltpu.SemaphoreType.DMA((7, n)), pltpu.SemaphoreType.DMA((7, n)), pltpu.SemaphoreType.DMA((n,))]

    def _slot(self, a, px, py, pc):
        return self.outs[a].at[4 * pc + 2 * px + py if self.core_major else 4 * px + 2 * py + pc]

    def _copy(self, a, k, block, to, src=None):
        return pltpu.make_async_remote_copy(
            src_ref=self._slot(a, *block) if src is None else src, dst_ref=self._slot(a, *block),
            send_sem=self.send_sems.at[k, a], recv_sem=self.recv_sems.at[k, a], device_id=to, device_id_type=MESH)

    def _mine(self, a):
        x, y, c, _ = _place()
        return pltpu.make_async_copy(self.ins[a], self._slot(a, x, y, c), self.local_sems.at[a])

    def _first(self, a):
        x, y, c, chips = _place()
        me = (x, y, c)
        return ([self._copy(a, 0, me, (x, y, 1 - c), src=self.ins[a])]
                + [self._copy(a, 1 + j, me, (*chip, c), src=self.ins[a]) for j, chip in enumerate(chips)])

    def start(self):
        for a in range(self.n):
            self._mine(a).start()
        for a in range(self.n):
            for cp in self._first(a):
                cp.start()

    def finish(self):
        x, y, c, chips = _place()
        me, sibling = (x, y, c), (x, y, 1 - c)
        passed = []
        for j, chip in enumerate(chips):
            for a in range(self.n):
                self._copy(a, 1 + j, (*chip, c), me).wait_recv()
                fwd = self._copy(a, 4 + j, (*chip, c), sibling)
                fwd.start()
                passed.append(fwd)
        for a in range(self.n):
            self._copy(a, 0, (x, y, 1 - c), me).wait_recv()
            for j, chip in enumerate(chips):
                self._copy(a, 4 + j, (*chip, 1 - c), me).wait_recv()
        for a in range(self.n):
            for cp in self._first(a):
                cp.wait_send()
        for cp in passed:
            cp.wait_send()
        for a in range(self.n):
            self._mine(a).wait()


def _half_exchange(arr, name):
    def body(in_ref, out_ref, send_sems, recv_sems, local_sem):
        x, y, c, _ = _place()
        my_chip = 2 * x + y

        def send(j, wait):
            to_me = (c == x) & (y == j // 2) & (c == j % 2)

            @pl.when(to_me)
            def _():
                local = pltpu.make_async_copy(in_ref.at[j], out_ref.at[my_chip], local_sem)
                local.wait() if wait else local.start()

            @pl.when(jnp.logical_not(to_me))
            def _():
                remote = pltpu.make_async_remote_copy(
                    src_ref=in_ref.at[j], dst_ref=out_ref.at[my_chip], send_sem=send_sems.at[j],
                    recv_sem=recv_sems.at[my_chip], device_id=(c, j // 2, j % 2), device_id_type=MESH)
                remote.wait_send() if wait else remote.start()

        for j in range(4):
            send(j, wait=False)
        for j in range(4):
            send(j, wait=True)
        for k in range(4):
            from_me = (k // 2 == x) & (k % 2 == y) & (c == x)

            @pl.when(jnp.logical_not(from_me))
            def _():
                pltpu.make_async_remote_copy(
                    src_ref=in_ref.at[0], dst_ref=out_ref.at[k], send_sem=send_sems.at[0], recv_sem=recv_sems.at[k],
                    device_id=(k // 2, k % 2, x), device_id_type=MESH).wait_recv()

    return pl.pallas_call(
        body, name=name, out_shape=_sds(arr.shape, arr.dtype), in_specs=[ANY], out_specs=ANY,
        scratch_shapes=[pltpu.SemaphoreType.DMA((4,)), pltpu.SemaphoreType.DMA((4,)), pltpu.SemaphoreType.DMA],
    )(arr)


class _ChipExchange:
    def __init__(self, ins, outs, send_sems, recv_sems, local_sems):
        self.ins, self.outs, self.n = ins, outs, len(ins)
        self.send_sems, self.recv_sems, self.local_sems = send_sems, recv_sems, local_sems

    @staticmethod
    def scratch(n):
        return [pltpu.SemaphoreType.DMA((3, n)), pltpu.SemaphoreType.DMA((3, n)), pltpu.SemaphoreType.DMA((n,))]

    def _local(self, a):
        x, y, _, _ = _place()
        me = 2 * x + y
        return pltpu.make_async_copy(self.ins[a].at[me], self.outs[a].at[me], self.local_sems.at[a])

    def _copies(self, a):
        x, y, c, chips = _place()
        me = 2 * x + y
        return [(pltpu.make_async_remote_copy(
                     src_ref=self.ins[a].at[2 * px + py], dst_ref=self.outs[a].at[me],
                     send_sem=self.send_sems.at[k, a], recv_sem=self.recv_sems.at[k, a],
                     device_id=(px, py, c), device_id_type=MESH),
                 pltpu.make_async_remote_copy(
                     src_ref=self.ins[a].at[me], dst_ref=self.outs[a].at[2 * px + py],
                     send_sem=self.send_sems.at[k, a], recv_sem=self.recv_sems.at[k, a],
                     device_id=(px, py, c), device_id_type=MESH))
                for k, (px, py) in enumerate(chips)]

    def start(self):
        for a in range(self.n):
            self._local(a).start()
        for a in range(self.n):
            for send, _ in self._copies(a):
                send.start()

    def finish(self):
        for a in range(self.n):
            for send, recv in self._copies(a):
                send.wait_send()
                recv.wait_recv()
        for a in range(self.n):
            self._local(a).wait()


def _gathering(body, n_steps, n_s, core_major):
    def wrapped(*refs, n_in, n_out):
        ins, sh_in = refs[:n_in], refs[n_in:n_in + n_s]
        outs, sh_out = refs[n_in + n_s:n_in + n_s + n_out], refs[n_in + n_s + n_out:n_in + 2 * n_s + n_out]
        rest = refs[n_in + 2 * n_s + n_out:]
        gather = _Gather(sh_in, sh_out, *rest[len(rest) - 3:], core_major=core_major)
        i = pl.program_id(0)

        @pl.when(i == 0)
        def _():
            gather.start()

        body(*ins, *outs, *rest[:len(rest) - 3])

        @pl.when(i == n_steps - 1)
        def _():
            gather.finish()

    return wrapped


def _norm_in(x, g_mix, shards, tm):
    T = x.shape[0]
    n_t = T // tm
    n_s = len(shards)

    def norm(x_ref, g_ref, h_ref):
        xv = x_ref[...]
        h_ref[...] = (xv * _rstd(xv) * g_ref[...]).astype(BF16)

    outs = pl.pallas_call(
        functools.partial(_gathering(norm, n_t, n_s, core_major=False), n_in=2, n_out=1), name="norm_in", grid=(n_t,),
        in_specs=[pl.BlockSpec((tm, D_MODEL), lambda i: (i, 0)), pl.BlockSpec((1, D_MODEL), lambda i: (0, 0))]
        + [ANY] * n_s,
        out_specs=[pl.BlockSpec((tm, D_MODEL), lambda i: (i, 0))] + [ANY] * n_s,
        out_shape=[_sds((T, D_MODEL), BF16)] + [_sds((N_DEV,) + a.shape, a.dtype) for a in shards],
        scratch_shapes=_Gather.scratch(n_s),
        compiler_params=pltpu.CompilerParams(dimension_semantics=("arbitrary",)),
    )(x, g_mix, *shards)
    return outs[0], list(outs[1:])


def _mix_in(h1, w_in_t, tm, shards):
    T = h1.shape[0]
    n_t = T // tm
    n_s = len(shards)

    def project(h_ref, w_ref, u_ref):
        u_ref[...] = lax.dot_general(h_ref[...], w_ref[...], NT, preferred_element_type=F32)

    outs = pl.pallas_call(
        functools.partial(_gathering(project, n_t, n_s, core_major=True), n_in=2, n_out=1), name="mix_in", grid=(n_t,),
        in_specs=[pl.BlockSpec((tm, D_MODEL), lambda i: (i, 0)), pl.BlockSpec((D_IN, D_MODEL), lambda i: (0, 0))]
        + [ANY] * n_s,
        out_specs=[pl.BlockSpec((tm, D_IN), lambda i: (i, 0))] + [ANY] * n_s,
        out_shape=[_sds((T, D_IN), F32)] + [_sds((N_DEV,) + a.shape, a.dtype) for a in shards],
        scratch_shapes=_Gather.scratch(n_s),
        compiler_params=pltpu.CompilerParams(dimension_semantics=("arbitrary",)),
    )(h1, w_in_t, *shards)
    return outs[0], list(outs[1:])


def _mixer_fwd(u, x, pv, wa, wx, wp, w_out_b, g_ffn, tm, shards=()):
    T = u.shape[0]
    n_s = len(shards)
    n_t = T // tm

    def body(u_ref, x_ref, pv_ref, wa_in, wx_in, wp_in, wo_ref, gf_ref, *rest):
        sh_in, rest = rest[:n_s], rest[n_s:]
        y_ref, hs_ref, hres_ref, h2_ref, saved_ref = rest[:5]
        sh_out, rest = rest[5:5 + n_s], rest[5 + n_s:]
        e_lru, e_pool, a_s, b_s, hc, wa_ref, wx_ref, wp_ref = rest[:8]
        gather = _Gather(sh_in, sh_out, *rest[8:], core_major=True) if n_s else None
        i = pl.program_id(0)

        @pl.when(i == 0)
        def _():
            if gather:
                gather.start()
            e_lru[pl.ds(0, HALO), :] = jnp.zeros((HALO, LRU_W), F32)
            e_pool[pl.ds(0, HALO), :] = jnp.zeros((HALO, POOL_W), F32)
            hc[...] = jnp.zeros((8, LRU_W), F32)
            _fill_block_diag(wa_ref, wa_in)
            _fill_block_diag(wx_ref, wx_in)
            _fill_block_diag(wp_ref, wp_in)

        e_lru[pl.ds(HALO, tm), :] = u_ref[:, 0:LRU_W]
        e_pool[pl.ds(HALO, tm), :] = u_ref[:, 2 * LRU_W:D_IN]
        pv = pv_ref[...]
        xc, r, ig, la = _lru_gates(e_lru, pv, wa_ref, wx_ref, tm)
        for q, val in enumerate((xc, r, ig, la)):
            saved_ref[:, LRU_W * q:LRU_W * (q + 1)] = val
        a, _, mult, _ = _lru_decay(la)
        a_s[...] = a
        b_s[...] = mult * (ig * xc)
        hc[...] = _scan_tile(a_s, b_s, hs_ref, hc[...], tm, reverse=False)
        gl, _ = _gelu_parts(u_ref[:, LRU_W:2 * LRU_W])
        y_lru = hs_ref[...] * gl
        _, zp, _ = _pool_pre(e_pool, pv, wp_ref, tm, i * tm)
        y_pool = zp * pv[ROW_PS:ROW_PS + 1, :]
        yn = jnp.concatenate([y_lru * _rstd(y_lru) * pv[ROW_GL:ROW_GL + 1, :],
                              y_pool * _rstd(y_pool) * pv[ROW_GP:ROW_GP + 1, :]], axis=1).astype(BF16)
        for b in range(N_DEV):
            y_ref[:, 128 * _y_pos(b):128 * (_y_pos(b) + 1)] = yn[:, 128 * b:128 * (b + 1)]
        hr = x_ref[...] + jnp.dot(y_ref[...], wo_ref[...], preferred_element_type=F32)
        hres_ref[...] = hr
        h2_ref[...] = (hr * _rstd(hr) * gf_ref[...]).astype(BF16)
        e_lru[pl.ds(0, HALO), :] = e_lru[pl.ds(tm, HALO), :]
        e_pool[pl.ds(0, HALO), :] = e_pool[pl.ds(tm, HALO), :]

        if gather:
            @pl.when(i == n_t - 1)
            def _():
                gather.finish()

    full = lambda shape: pl.BlockSpec(shape, lambda i: (0,) * len(shape))
    row = lambda w: pl.BlockSpec((tm, w), lambda i: (i, 0))
    outs = pl.pallas_call(
        body, name="mixer_fwd", grid=(n_t,),
        in_specs=[row(D_IN), row(D_MODEL), full((16, LRU_W)), full((8, 64, 64)), full((8, 64, 64)), full((4, 128, 128)),
                  full((D_MODEL, D_MODEL)), full((1, D_MODEL))] + [ANY] * n_s,
        out_specs=[row(D_MODEL), row(LRU_W), row(D_MODEL), row(D_MODEL), row(4 * LRU_W)] + [ANY] * n_s,
        out_shape=[_sds((T, D_MODEL), BF16), _sds((T, LRU_W), F32), _sds((T, D_MODEL), F32), _sds((T, D_MODEL), BF16),
                   _sds((T, 4 * LRU_W), F32)] + [_sds((N_DEV,) + a.shape, a.dtype) for a in shards],
        scratch_shapes=[pltpu.VMEM((HALO + tm, LRU_W), F32), pltpu.VMEM((HALO + tm, POOL_W), F32),
                        pltpu.VMEM((tm, LRU_W), F32), pltpu.VMEM((tm, LRU_W), F32), pltpu.VMEM((8, LRU_W), F32)]
        + [pltpu.VMEM((2, 256, 256), BF16)] * 3 + (_Gather.scratch(n_s) if n_s else []),
        compiler_params=pltpu.CompilerParams(dimension_semantics=("arbitrary",)),
    )(u, x, pv, wa, wx, wp, w_out_b, g_ffn, *shards)
    return outs[0], outs[1], outs[2], outs[3], outs[4], list(outs[5:])


def _ffn_fwd(hres, h2, w1_b, w3_b, w2_b, g_fin, tgt, tm):
    T = hres.shape[0]

    def body(hres_ref, h2_ref, w1_ref, w3_ref, w2_ref, gfin_ref, tgt_ref,
             g_ref, v_ref, ff_ref, d3_ref, loss_ref, dgfin_ref):
        @pl.when(pl.program_id(0) == 0)
        def _():
            loss_ref[...] = jnp.zeros((8, 128), F32)
            dgfin_ref[...] = jnp.zeros((1, D_MODEL), F32)

        h2 = h2_ref[...]
        h3 = hres_ref[...]
        for lo, hi in FF_CHUNKS:
            g = lax.dot_general(h2, w1_ref[lo:hi, :], NT, preferred_element_type=F32)
            v = lax.dot_general(h2, w3_ref[lo:hi, :], NT, preferred_element_type=F32)
            g_ref[:, lo:hi] = g.astype(BF16)
            v_ref[:, lo:hi] = v.astype(BF16)
            ff = ((g * _sigmoid(g)) * v).astype(BF16)
            ff_ref[:, lo:hi] = ff
            h3 = h3 + jnp.dot(ff, w2_ref[lo:hi, :], preferred_element_type=F32)

        rstd = _rstd(h3)
        xh = h3 * rstd
        gfin = gfin_ref[...]
        err = xh * gfin - tgt_ref[...]
        loss_ref[...] += 0.5 * jnp.sum(jnp.mean(err * err, axis=-1, keepdims=True))
        dx, dgain = _rms_bwd(err * (1.0 / D_MODEL), xh, rstd, gfin)
        d3_ref[...] = dx
        dgfin_ref[...] += dgain

    row = lambda w: pl.BlockSpec((tm, w), lambda i: (i, 0))
    const = lambda shape: pl.BlockSpec(shape, lambda i: (0,) * len(shape))
    weight = pl.BlockSpec((D_FF, D_MODEL), lambda i: (0, 0), pipeline_mode=pl.Buffered(1))
    return pl.pallas_call(
        body, name="ffn_fwd", grid=(T // tm,),
        in_specs=[row(D_MODEL), row(D_MODEL), weight, weight, weight, const((1, D_MODEL)), row(D_MODEL)],
        out_specs=[row(D_FF), row(D_FF), row(D_FF), row(D_MODEL), const((8, 128)), const((1, D_MODEL))],
        out_shape=[_sds((T, D_FF), BF16), _sds((T, D_FF), BF16), _sds((T, D_FF), BF16),
                   _sds((T, D_MODEL), F32), _sds((8, 128), F32), _sds((1, D_MODEL), F32)],
        compiler_params=pltpu.CompilerParams(dimension_semantics=("arbitrary",)),
    )(hres, h2, w1_b, w3_b, w2_b, g_fin, tgt)


def _ffn_bwd(d3, g, v, w1_b, w3_b, w2_b, hres, g_ffn, tm):
    T = d3.shape[0]

    def body(d3_ref, g_ref, v_ref, w1_ref, w3_ref, w2_ref, hres_ref, gf_ref, dg_ref, dv_ref, d2_ref, dgffn_ref):
        @pl.when(pl.program_id(0) == 0)
        def _():
            dgffn_ref[...] = jnp.zeros((1, D_MODEL), F32)

        d3 = d3_ref[...]
        d3b = d3.astype(BF16)
        dh2 = jnp.zeros((tm, D_MODEL), F32)
        for lo, hi in FF_CHUNKS:
            dff = lax.dot_general(d3b, w2_ref[lo:hi, :], NT, preferred_element_type=F32)
            gv = g_ref[:, lo:hi].astype(F32)
            vv = v_ref[:, lo:hi].astype(F32)
            sg = _sigmoid(gv)
            sl = gv * sg
            dgb = (dff * vv * (sg * (1.0 + gv * (1.0 - sg)))).astype(BF16)
            dvb = (dff * sl).astype(BF16)
            dg_ref[:, lo:hi] = dgb
            dv_ref[:, lo:hi] = dvb
            dh2 = dh2 + (jnp.dot(dgb, w1_ref[lo:hi, :], preferred_element_type=F32)
                         + jnp.dot(dvb, w3_ref[lo:hi, :], preferred_element_type=F32))

        hr = hres_ref[...]
        rstd = _rstd(hr)
        dx, dgain = _rms_bwd(dh2, hr * rstd, rstd, gf_ref[...])
        d2_ref[...] = d3 + dx
        dgffn_ref[...] += dgain

    row = lambda w: pl.BlockSpec((tm, w), lambda i: (i, 0))
    const = lambda shape: pl.BlockSpec(shape, lambda i: (0,) * len(shape))
    weight = pl.BlockSpec((D_FF, D_MODEL), lambda i: (0, 0), pipeline_mode=pl.Buffered(1))
    return pl.pallas_call(
        body, name="ffn_bwd", grid=(T // tm,),
        in_specs=[row(D_MODEL), row(D_FF), row(D_FF), weight, weight, weight, row(D_MODEL), const((1, D_MODEL))],
        out_specs=[row(D_FF), row(D_FF), row(D_MODEL), const((1, D_MODEL))],
        out_shape=[_sds((T, D_FF), BF16), _sds((T, D_FF), BF16), _sds((T, D_MODEL), F32), _sds((1, D_MODEL), F32)],
        compiler_params=pltpu.CompilerParams(dimension_semantics=("arbitrary",)),
    )(d3, g, v, w1_b, w3_b, w2_b, hres, g_ffn)


def _at_b_pair(a, b, c_arr, name, tk, gather=()):
    T, M = a.shape
    N = b.shape[1]
    hm, n_k = M // 2, T // tk
    n_g = len(gather)

    def body(c_ref, a_ref, b_ref, *rest):
        g_in, o_ref, rest = rest[:n_g], rest[n_g], rest[n_g + 1:]
        g_out, rest = rest[:n_g], rest[n_g:]
        acc, landed, send_sem, recv_sem = rest[:4]
        ag = _Gather(g_in, g_out, *rest[4:]) if n_g else None
        ph, k = pl.program_id(0), pl.program_id(1)

        def hand_over():
            x, y, c, _ = _place()
            return pltpu.make_async_remote_copy(
                src_ref=acc.at[0], dst_ref=landed, send_sem=send_sem, recv_sem=recv_sem,
                device_id=(x, y, 1 - c), device_id_type=MESH)

        if ag:
            @pl.when((ph == 0) & (k == 0))
            def _():
                ag.start()

        @pl.when(k == 0)
        def _():
            acc[ph] = jnp.zeros((hm, N), F32)

        acc[ph] += lax.dot_general(a_ref[...].astype(BF16), b_ref[...].astype(BF16), TN, preferred_element_type=F32)

        @pl.when((ph == 0) & (k == n_k - 1))
        def _():
            hand_over().start()

        @pl.when((ph == 1) & (k == n_k - 1))
        def _():
            copy = hand_over()
            copy.wait_recv()
            o_ref[...] = (acc[1] + landed[...]).astype(BF16)
            copy.wait_send()
            if ag:
                ag.finish()

    outs = pl.pallas_call(
        body, name=name,
        grid_spec=pltpu.PrefetchScalarGridSpec(
            num_scalar_prefetch=1, grid=(2, n_k),
            in_specs=[pl.BlockSpec((tk, hm), lambda ph, k, c_ref: (k, (ph + 1 - c_ref[0]) % 2)),
                      pl.BlockSpec((tk, N), lambda ph, k, c_ref: (k, 0))] + [ANY] * n_g,
            out_specs=[pl.BlockSpec((hm, N), lambda ph, k, c_ref: (0, 0))] + [ANY] * n_g,
            scratch_shapes=[pltpu.VMEM((2, hm, N), F32), pltpu.VMEM((hm, N), F32),
                            pltpu.SemaphoreType.DMA, pltpu.SemaphoreType.DMA] + (_Gather.scratch(n_g) if n_g else [])),
        out_shape=[_sds((hm, N), BF16)] + [_sds((N_DEV,) + g.shape, g.dtype) for g in gather],
        compiler_params=pltpu.CompilerParams(dimension_semantics=("arbitrary", "arbitrary")),
    )(c_arr, a, b, *gather)
    return (outs[0], list(outs[1:])) if n_g else outs[0]


def _mixer_bwd(d2, u, hs, saved, pv, wa, wx, wp, w_out_b, tm, chip_sums=()):
    T = u.shape[0]
    n_t = T // tm
    n_x = len(chip_sums)

    def body(d2_ref, u_ref, uh_ref, hs_ref, hh_ref, saved_ref, pv_ref, wa_in, wx_in, wp_in, wo_ref, *rest):
        x_in, rest = rest[:n_x], rest[n_x:]
        du_ref, sg_ref = rest[:2]
        x_out, rest = rest[2:2 + n_x], rest[2 + n_x:]
        e_pool, e_h, a_s, b_s, dh_s, mu_s, f_x, f_p, mc, cx, cp = rest[:11]
        wa_ref, wx_ref, wp_ref, vacc_ref, dwa_ref, dwx_ref, dwp_ref = rest[11:18]
        exchange = _ChipExchange(x_in, x_out, *rest[18:]) if n_x else None
        s = pl.program_id(0)
        it = n_t - 1 - s

        @pl.when(s == 0)
        def _():
            if exchange:
                exchange.start()
            mc[...] = jnp.zeros((8, LRU_W), F32)
            cx[...] = jnp.zeros((8, LRU_W), F32)
            cp[...] = jnp.zeros((HALO, POOL_W), F32)
            vacc_ref[...] = jnp.zeros((16, LRU_W), F32)
            dwa_ref[...] = jnp.zeros((2, 256, 256), F32)
            dwx_ref[...] = jnp.zeros((2, 256, 256), F32)
            dwp_ref[...] = jnp.zeros((2, 256, 256), F32)
            _fill_block_diag(wa_ref, wa_in)
            _fill_block_diag(wx_ref, wx_in)
            _fill_block_diag(wp_ref, wp_in)

        first = it == 0
        e_pool[pl.ds(0, HALO), :] = jnp.where(first, 0.0, uh_ref[...])
        e_pool[pl.ds(HALO, tm), :] = u_ref[:, 2 * LRU_W:D_IN]
        e_h[pl.ds(0, 8), :] = jnp.where(first, 0.0, hh_ref[...])
        e_h[pl.ds(8, tm), :] = hs_ref[...]
        pv = pv_ref[...]
        saved = lambda q: saved_ref[:, LRU_W * q:LRU_W * (q + 1)]

        dyn = lax.dot_general(d2_ref[...].astype(BF16), wo_ref[...], NT, preferred_element_type=F32)
        dyn = jnp.concatenate([dyn[:, 128 * _y_pos(b):128 * (_y_pos(b) + 1)] for b in range(N_DEV)], axis=1)

        h = hs_ref[...]
        ug = u_ref[:, LRU_W:2 * LRU_W]
        gl, dgl = _gelu_parts(ug)
        y_lru = h * gl
        rstd_l = _rstd(y_lru)
        dy_lru, d_gain_l = _rms_bwd(dyn[:, 0:LRU_W], y_lru * rstd_l, rstd_l, pv[ROW_GL:ROW_GL + 1, :])
        dh = dy_lru * gl
        du_ref[:, LRU_W:2 * LRU_W] = (dy_lru * h * dgl).astype(BF16)
        a_s[...] = jnp.exp(saved(3))
        b_s[...] = a_s[...] * dh
        dh_s[...] = dh
        mu_s[pl.ds(tm, 8), :] = mc[...]
        mc[...] = _scan_tile(a_s, b_s, mu_s, mc[...], tm, reverse=True)
        xc, r, ig = saved(0), saved(1), saved(2)
        a, om, mult, rmult = _lru_decay(saved(3))
        lam_t = dh_s[...] + mu_s[pl.ds(1, tm), :]
        da = lam_t * e_h[pl.ds(7, tm), :]
        dmult = lam_t * (ig * xc)
        di = lam_t * (mult * xc)
        dxc = lam_t * (mult * ig)
        dla = da * a - jnp.where(om > 1e-12, dmult * ((a * a) * rmult), 0.0)
        dra = (dla * (-LRU_C * _softplus_neg_lambda(pv))) * (r * (1.0 - r))
        dia = di * (ig * (1.0 - ig))
        drab = dra.astype(BF16)
        diab = dia.astype(BF16)
        xcb = xc.astype(BF16)
        dxc = dxc + _bd_t(drab, wa_ref) + _bd_t(diab, wx_ref)
        dwa_ref[...] += _bd_grad(xcb, drab)
        dwx_ref[...] += _bd_grad(xcb, diab)
        sig_neg_lam = _sigmoid(-pv[ROW_LAM:ROW_LAM + 1, :])
        d_lam = jnp.sum(dla * r, axis=0, keepdims=True) * (LRU_C * sig_neg_lam)

        f_x[pl.ds(0, tm), :] = dxc
        f_x[pl.ds(tm, 8), :] = cx[...]
        du_lru = jnp.zeros((tm, LRU_W), F32)
        u_lru = u_ref[:, 0:LRU_W]
        d_cw = []
        for k in range(4):
            later = f_x[pl.ds(3 - k, tm), :]
            du_lru = du_lru + later * pv[ROW_CW + k:ROW_CW + k + 1, :]
            d_cw.append(jnp.sum(later * u_lru, axis=0, keepdims=True))
        du_ref[:, 0:LRU_W] = du_lru.astype(BF16)
        cx[...] = f_x[pl.ds(0, 8), :]

        pooled_b, zp, inv_cnts = _pool_pre(e_pool, pv, wp_ref, tm, it * tm)
        ps = pv[ROW_PS:ROW_PS + 1, :]
        y_pool = zp * ps
        rstd_p = _rstd(y_pool)
        dy_pool, d_gain_p = _rms_bwd(dyn[:, LRU_W:D_MODEL], y_pool * rstd_p, rstd_p, pv[ROW_GP:ROW_GP + 1, :])
        dz = dy_pool * ps
        dzb = dz.astype(BF16)
        dwp_ref[...] += _bd_grad(pooled_b, dzb)
        dpooled = _bd_t(dzb, wp_ref)
        for g, w in enumerate(POOL_WINDOWS):
            f_p[pl.ds(0, tm), pl.ds(128 * g, 128)] = _over_count(dpooled[:, 128 * g:128 * (g + 1)], w, inv_cnts[g])
        f_p[pl.ds(tm, HALO), :] = cp[...]
        for g, w in enumerate(POOL_WINDOWS):
            acc = _window_sum(f_p[:, pl.ds(128 * g, 128)], w, back=False)[0:tm, :]
            du_ref[:, 2 * LRU_W + 128 * g:2 * LRU_W + 128 * (g + 1)] = (
                acc - dpooled[:, 128 * g:128 * (g + 1)]).astype(BF16)
        cp[...] = f_p[pl.ds(0, HALO), :]

        rows = d_cw + [
            jnp.sum(dxc, axis=0, keepdims=True),
            jnp.sum(dra, axis=0, keepdims=True),
            jnp.sum(dia, axis=0, keepdims=True),
            d_lam,
            jnp.sum(dz, axis=0, keepdims=True),
            jnp.sum(dy_pool * zp, axis=0, keepdims=True),
            d_gain_l, d_gain_p,
            jnp.zeros((4, LRU_W), F32),
        ]
        vacc_ref[...] += jnp.concatenate(rows, axis=0)

        @pl.when(s == n_t - 1)
        def _():
            sg_ref[SG_VEC:SG_VEC + 16, :] = vacc_ref[:, 0:256]
            sg_ref[SG_VEC + 16:SG_VEC + 32, :] = vacc_ref[:, 256:512]
            for half in range(2):
                sg_ref[SG_WA + 64 * half:SG_WA + 64 * (half + 1), :] = _diag_pack(dwa_ref[half], 64)
                sg_ref[SG_WX + 64 * half:SG_WX + 64 * (half + 1), :] = _diag_pack(dwx_ref[half], 64)
                sg_ref[SG_WP + 128 * half:SG_WP + 128 * (half + 1), :] = _diag_pack(dwp_ref[half], 128)
            if exchange:
                exchange.finish()

    rev = lambda w: pl.BlockSpec((tm, w), lambda s: (n_t - 1 - s, 0))
    full = lambda shape: pl.BlockSpec(shape, lambda s: (0,) * len(shape))
    outs = pl.pallas_call(
        body, name="mixer_bwd", grid=(n_t,),
        in_specs=[rev(D_MODEL), rev(D_IN),
                  pl.BlockSpec((HALO, POOL_W), lambda s: (jnp.maximum((n_t - 1 - s) * (tm // HALO) - 1, 0), 2)),
                  rev(LRU_W),
                  pl.BlockSpec((8, LRU_W), lambda s: (jnp.maximum((n_t - 1 - s) * (tm // 8) - 1, 0), 0)),
                  rev(4 * LRU_W), full((16, LRU_W)), full((8, 64, 64)), full((8, 64, 64)), full((4, 128, 128)),
                  full((D_MODEL, D_MODEL))] + [ANY] * n_x,
        out_specs=[rev(D_IN), full((SG_ROWS, 256))] + [ANY] * n_x,
        out_shape=[_sds((T, D_IN), BF16), _sds((SG_ROWS, 256), F32)] + [_sds(a.shape, a.dtype) for a in chip_sums],
        scratch_shapes=[pltpu.VMEM((HALO + tm, POOL_W), F32),
                        pltpu.VMEM((8 + tm, LRU_W), F32)] + [pltpu.VMEM((tm, LRU_W), F32)] * 3 + [
                        pltpu.VMEM((tm + 8, LRU_W), F32), pltpu.VMEM((tm + 8, LRU_W), F32),
                        pltpu.VMEM((tm + HALO, POOL_W), F32), pltpu.VMEM((8, LRU_W), F32),
                        pltpu.VMEM((8, LRU_W), F32), pltpu.VMEM((HALO, POOL_W), F32)]
        + [pltpu.VMEM((2, 256, 256), BF16)] * 3 + [pltpu.VMEM((16, LRU_W), F32)] + [pltpu.VMEM((2, 256, 256), F32)] * 3
        + (_ChipExchange.scratch(n_x) if n_x else []),
        compiler_params=pltpu.CompilerParams(dimension_semantics=("arbitrary",)),
    )(d2, u, u, hs, hs, saved, pv, wa, wx, wp, w_out_b, *chip_sums)
    return outs[0], outs[1], list(outs[2:])


def _mix_in_bwd(du, x, d2, w_in_t, g_mix, tm):
    T = x.shape[0]

    def body(du_ref, x_ref, d2_ref, w_ref, g_ref, dx_ref, dg_ref):
        @pl.when(pl.program_id(0) == 0)
        def _():
            dg_ref[...] = jnp.zeros((1, D_MODEL), F32)

        dh = jnp.dot(du_ref[...], w_ref[...], preferred_element_type=F32)
        xv = x_ref[...]
        rstd = _rstd(xv)
        dx, dgain = _rms_bwd(dh, xv * rstd, rstd, g_ref[...])
        dx_ref[...] = d2_ref[...] + dx
        dg_ref[...] += dgain

    row = lambda w: pl.BlockSpec((tm, w), lambda i: (i, 0))
    const = lambda shape: pl.BlockSpec(shape, lambda i: (0,) * len(shape))
    return pl.pallas_call(
        body, name="mix_in_bwd", grid=(T // tm,),
        in_specs=[row(D_IN), row(D_MODEL), row(D_MODEL), const((D_IN, D_MODEL)), const((1, D_MODEL))],
        out_specs=[row(D_MODEL), const((1, D_MODEL))],
        out_shape=[_sds((T, D_MODEL), F32), _sds((1, D_MODEL), F32)],
        compiler_params=pltpu.CompilerParams(dimension_semantics=("arbitrary",)),
    )(du, x, d2, w_in_t, g_mix)


def _adamw(w, g, m, v):
    m = ADAM_B1 * m + (1.0 - ADAM_B1) * g
    v = ADAM_B2 * v + (1.0 - ADAM_B2) * (g * g)
    m_hat = m / (1.0 - ADAM_B1 ** ADAM_STEP)
    v_hat = v / (1.0 - ADAM_B2 ** ADAM_STEP)
    delta = -ADAM_LR * (m_hat / (jnp.sqrt(v_hat) + ADAM_EPS) + ADAM_WD * w)
    return delta, m, v


def _adam_shards(ws, ms, vs, parts):
    n = len(ws)
    n_blk = [w.shape[0] // ADAM_ROWS for w in ws]

    def body(*refs):
        w_refs, m_refs, v_refs, p_refs, outs = (refs[:n], refs[n:2 * n], refs[2 * n:3 * n], refs[3 * n:4 * n],
                                                refs[4 * n:])
        i = pl.program_id(0)
        for a in range(n):
            @pl.when(i < n_blk[a])
            def _(a=a):
                g = p_refs[a][0].astype(F32)
                for j in range(1, 4):
                    g = g + p_refs[a][j].astype(F32)
                delta, new_m, new_v = _adamw(w_refs[a][...], g, m_refs[a][...], v_refs[a][...])
                for kind, val in enumerate((g, delta, new_m, new_v)):
                    outs[4 * a + kind][...] = val

    blk = lambda a: pl.BlockSpec((ADAM_ROWS, D_MODEL), lambda i: (jnp.minimum(i, n_blk[a] - 1), 0))
    part_blk = lambda a: pl.BlockSpec((4, ADAM_ROWS, D_MODEL), lambda i: (0, jnp.minimum(i, n_blk[a] - 1), 0))
    res = pl.pallas_call(
        body, name="adam_shards", grid=(max(n_blk),),
        in_specs=[blk(a) for a in range(n)] * 3 + [part_blk(a) for a in range(n)],
        out_specs=[blk(a) for a in range(n) for _ in range(4)],
        out_shape=[_sds(w.shape, F32) for w in ws for _ in range(4)],
        compiler_params=pltpu.CompilerParams(dimension_semantics=("arbitrary",)),
    )(*ws, *ms, *vs, *parts)
    return [tuple(res[4 * a:4 * a + 4]) for a in range(n)]


SMALL_PARAMS = [("norm_mix_g", (1, D_MODEL)), ("conv_w", (1, 4, 64)), ("conv_b", (1, LRU_W)),
                ("gate_a_w", (1, 8, 64, 64)), ("gate_a_b", (1, LRU_W)), ("gate_x_w", (1, 8, 64, 64)),
                ("gate_x_b", (1, LRU_W)), ("lru_lambda", (1, LRU_W)), ("pool_w", (1, 4, 128, 128)),
                ("pool_b", (1, POOL_W)), ("pool_scale", (1, POOL_W)), ("norm_lru_g", (1, LRU_W)),
                ("norm_pool_g", (1, POOL_W)), ("norm_ffn_g", (1, D_MODEL)), ("final_norm_g", (1, D_MODEL))]
VEC_ROW = dict(conv_b=ROW_CB, gate_a_b=ROW_BA, gate_x_b=ROW_BX, lru_lambda=ROW_LAM, pool_b=ROW_PB, pool_scale=ROW_PS,
               norm_lru_g=ROW_GL, norm_pool_g=ROW_GP)
WHOLE = (Ellipsis,)


def _unpack_mixer_grads(sg, dev):
    vec = jnp.concatenate([sg[SG_VEC:SG_VEC + 16], sg[SG_VEC + 16:SG_VEC + 32]], axis=1)
    out = {nm: [(WHOLE, vec[r:r + 1])] for nm, r in VEC_ROW.items()}
    own = jnp.zeros((4, 64), F32)
    for d in range(N_DEV):
        own = jnp.where(dev == d, vec[ROW_CW:ROW_CW + 4, 64 * d:64 * (d + 1)], own)
    out["conv_w"] = [((0,), own)]
    for nm, row0 in (("gate_a_w", SG_WA), ("gate_x_w", SG_WX)):
        out[nm] = [((0, b), sg[row0 + 64 * (b // 4):row0 + 64 * (b // 4 + 1), 64 * (b % 4):64 * (b % 4 + 1)])
                   for b in range(8)]
    out["pool_w"] = [((0, b), sg[SG_WP + 128 * (b // 2):SG_WP + 128 * (b // 2 + 1), 128 * (b % 2):128 * (b % 2 + 1)])
                     for b in range(4)]
    return out


def _adam_small(parts, w, m, v):
    names = [nm for nm, _ in SMALL_PARAMS]
    n = len(names)

    def body(sg_ref, gm_ref, gf_ref, gn_ref, ls_ref, *rest):
        w_refs, m_refs, v_refs, outs = rest[:n], rest[n:2 * n], rest[2 * n:3 * n], rest[3 * n:]
        dev = 4 * lax.axis_index("x") + 2 * lax.axis_index("y") + lax.axis_index("c")

        def total(ref):
            acc = ref[0]
            for d in range(1, N_DEV):
                acc = acc + ref[d]
            return acc

        pieces = _unpack_mixer_grads(total(sg_ref), dev)
        pieces["norm_mix_g"] = [(WHOLE, total(gm_ref))]
        pieces["norm_ffn_g"] = [(WHOLE, total(gf_ref))]
        pieces["final_norm_g"] = [(WHOLE, total(gn_ref))]
        for i, nm in enumerate(names):
            for idx, g in pieces[nm]:
                delta, new_m, new_v = _adamw(w_refs[i][idx], g, m_refs[i][idx], v_refs[i][idx])
                for kind, val in enumerate((g, delta, new_m, new_v)):
                    outs[4 * i + kind][idx] = val
        outs[4 * n][...] = total(ls_ref)

    shapes = [_sds(shape, F32) for _, shape in SMALL_PARAMS for _ in range(4)] + [_sds((8, 128), F32)]
    res = pl.pallas_call(body, name="adam_small", out_shape=shapes)(
        *parts, *[w[nm] for nm in names], *[m[nm] for nm in names], *[v[nm] for nm in names])
    return {nm: tuple(res[4 * i:4 * i + 4]) for i, nm in enumerate(names)}, res[4 * n][0, 0]


def _vec_rows(conv_w_full, conv_b, ba, bx, lam, pb, ps, gl, gp):
    return jnp.concatenate([conv_w_full, conv_b, ba, bx, lam, pb, ps, gl, gp, jnp.zeros((4, LRU_W), F32)], axis=0)


WEIGHT_ORDER = ['norm_mix_g', 'w_in', 'conv_w', 'conv_b', 'gate_a_w', 'gate_a_b', 'gate_x_w', 'gate_x_b', 'lru_lambda',
                'pool_w', 'pool_b', 'pool_scale', 'norm_lru_g', 'norm_pool_g', 'w_out', 'norm_ffn_g', 'ffn_w1', 'ffn_w3',
                'ffn_w2', 'final_norm_g']


def kernel(x, norm_mix_g, w_in, conv_w, conv_b, gate_a_w, gate_a_b, gate_x_w, gate_x_b, lru_lambda, pool_w, pool_b, pool_scale, norm_lru_g, norm_pool_g, w_out, norm_ffn_g, ffn_w1, ffn_w3, ffn_w2, final_norm_g, loss_target, m_norm_mix_g, m_w_in, m_conv_w, m_conv_b, m_gate_a_w, m_gate_a_b, m_gate_x_w, m_gate_x_b, m_lru_lambda, m_pool_w, m_pool_b, m_pool_scale, m_norm_lru_g, m_norm_pool_g, m_w_out, m_norm_ffn_g, m_ffn_w1, m_ffn_w3, m_ffn_w2, m_final_norm_g, v_norm_mix_g, v_w_in, v_conv_w, v_conv_b, v_gate_a_w, v_gate_a_b, v_gate_x_w, v_gate_x_b, v_lru_lambda, v_pool_w, v_pool_b, v_pool_scale, v_norm_lru_g, v_norm_pool_g, v_w_out, v_norm_ffn_g, v_ffn_w1, v_ffn_w3, v_ffn_w2, v_final_norm_g):
    ac = lax.axis_index("c")
    tm, tmx, tk = 512, 512, 1024
    tm_in = 1024
    xs, tgt = x[0], loss_target[0]
    g_fin = final_norm_g.reshape(1, D_MODEL)
    c_arr = jnp.reshape(ac, (1,)).astype(jnp.int32)

    tr = lambda w: jnp.swapaxes(w[0], 0, 1)
    own = lambda w: w[0]
    bf = lambda a: a.astype(BF16)

    h1, (g_in, g_conv) = _norm_in(xs, norm_mix_g, [bf(tr(w_in)), conv_w[0]], tm_in)
    w_in_t = g_in.reshape(D_IN, D_MODEL)
    u, (g_out,) = _mix_in(h1, w_in_t, tm, shards=[bf(own(w_out))])
    conv_w_full = g_conv.transpose(1, 0, 2).reshape(4, LRU_W)
    pv = _vec_rows(conv_w_full, conv_b, gate_a_b, gate_x_b, lru_lambda, pool_b, pool_scale, norm_lru_g, norm_pool_g)
    wa, wx, wp = gate_a_w[0], gate_x_w[0], pool_w[0]
    w_out_b = g_out.reshape(D_MODEL, D_MODEL)
    y, hs, hres, h2, saved, (g_w1, g_w3, g_w2) = _mixer_fwd(
        u, xs, pv, wa, wx, wp, w_out_b, norm_ffn_g, tmx, shards=[bf(tr(ffn_w1)), bf(tr(ffn_w3)), bf(own(ffn_w2))])
    w1_t, w3_t, w2_b = g_w1.reshape(D_FF, D_MODEL), g_w3.reshape(D_FF, D_MODEL), g_w2.reshape(D_FF, D_MODEL)
    g, v, ff, d3, loss_acc, d_gfin = _ffn_fwd(hres, h2, w1_t, w3_t, w2_b, g_fin, tgt, tm)

    dg, dv, d2, d_gffn = _ffn_bwd(d3, g, v, w1_t, w3_t, w2_b, hres, norm_ffn_g, tm // 2)
    chips = lambda a: a.reshape(4, a.shape[0] // 4, a.shape[1])
    early_sums = [chips(_at_b_pair(y, d2, c_arr, "grad_w_out", 2 * tk)), chips(_at_b_pair(dg, h2, c_arr, "grad_w1", tk)),
                  chips(_at_b_pair(dv, h2, c_arr, "grad_w3", tk)), chips(_at_b_pair(ff, d3, c_arr, "grad_w2", tk))]
    du, d_mixer, early_parts = _mixer_bwd(d2, u, hs, saved, pv, wa, wx, wp, w_out_b, tmx, chip_sums=early_sums)
    grad_x, d_gmix = _mix_in_bwd(du, xs, d2, w_in_t, norm_mix_g, tm_in)
    d_win, small_parts = _at_b_pair(du, h1, c_arr, "grad_w_in", 2 * tk,
                                    gather=[d_mixer, d_gmix, d_gffn, d_gfin, loss_acc])
    parts = [_half_exchange(chips(d_win), "grads_to_chips_w_in")] + list(early_parts)

    res = {}
    shard_w = dict(w_in=(w_in, m_w_in, v_w_in, tr), w_out=(w_out, m_w_out, v_w_out, own),
                   ffn_w1=(ffn_w1, m_ffn_w1, v_ffn_w1, tr), ffn_w3=(ffn_w3, m_ffn_w3, v_ffn_w3, tr),
                   ffn_w2=(ffn_w2, m_ffn_w2, v_ffn_w2, own))
    shard_res = _adam_shards([view(w) for w, _, _, view in shard_w.values()],
                             [view(m) for _, m, _, view in shard_w.values()],
                             [view(v) for _, _, v, view in shard_w.values()], parts)
    for (nm, (_, _, _, view)), outs in zip(shard_w.items(), shard_res):
        res[nm] = [(jnp.swapaxes(o, 0, 1) if view is tr else o)[None] for o in outs]

    row = lambda a: a.reshape(1, D_MODEL)
    small = lambda gm, cw, cb, wa_, ba, wx_, bx, lam, pw, pb, ps, gl, gp, gf, gn: dict(
        norm_mix_g=gm, conv_w=cw, conv_b=cb, gate_a_w=wa_, gate_a_b=ba, gate_x_w=wx_, gate_x_b=bx, lru_lambda=lam,
        pool_w=pw, pool_b=pb, pool_scale=ps, norm_lru_g=gl, norm_pool_g=gp, norm_ffn_g=gf, final_norm_g=row(gn))
    small_res, loss = _adam_small(
        small_parts,
        small(norm_mix_g, conv_w, conv_b, gate_a_w, gate_a_b, gate_x_w, gate_x_b, lru_lambda, pool_w, pool_b,
              pool_scale, norm_lru_g, norm_pool_g, norm_ffn_g, final_norm_g),
        small(m_norm_mix_g, m_conv_w, m_conv_b, m_gate_a_w, m_gate_a_b, m_gate_x_w, m_gate_x_b, m_lru_lambda, m_pool_w,
              m_pool_b, m_pool_scale, m_norm_lru_g, m_norm_pool_g, m_norm_ffn_g, m_final_norm_g),
        small(v_norm_mix_g, v_conv_w, v_conv_b, v_gate_a_w, v_gate_a_b, v_gate_x_w, v_gate_x_b, v_lru_lambda, v_pool_w,
              v_pool_b, v_pool_scale, v_norm_lru_g, v_norm_pool_g, v_norm_ffn_g, v_final_norm_g))
    for nm, outs in small_res.items():
        res[nm] = [o.reshape(D_MODEL) for o in outs] if nm == "final_norm_g" else list(outs)

    out = [loss, grad_x[None]]
    for kind in range(4):
        out += [res[nm][kind] for nm in WEIGHT_ORDER]
    return tuple(out)
```

```python
import functools

import jax
import jax.numpy as jnp
from jax import lax
from jax.experimental import pallas as pl
from jax.experimental.pallas import tpu as pltpu

F32 = jnp.float32
BF16 = jnp.bfloat16

D_MODEL = 1024
LRU_W = 512
POOL_W = 512
D_IN = 1536
D_FF = 2816
POOL_WINDOWS = (2, 4, 8, 16)
EPS = 1e-6
LRU_C = 8.0
N_DEV = 8
HALO = 16
SCAN_UNROLL = 8
ADAM_ROWS = 32
FF_CHUNKS = ((0, 1536), (1536, 2816))
GRAD_W_IN_COLS = ((768, 0), (256, 3))

ADAM_LR = 0.001
ADAM_B1 = 0.9
ADAM_B2 = 0.999
ADAM_EPS = 1e-08
ADAM_WD = 0.01
ADAM_STEP = 10

ROW_CW, ROW_CB, ROW_BA, ROW_BX, ROW_LAM, ROW_PB, ROW_PS, ROW_GL, ROW_GP = 0, 4, 5, 6, 7, 8, 9, 10, 11
SG_VEC, SG_WA, SG_WX, SG_WP, SG_ROWS = 0, 32, 160, 288, 544

NT = (((1,), (1,)), ((), ()))
TN = (((0,), (0,)), ((), ()))


def _sds(shape, dtype):
    return jax.ShapeDtypeStruct(shape, dtype)


def _sigmoid(x):
    return 0.5 * jnp.tanh(0.5 * x) + 0.5


def _gelu_parts(x):
    c = 0.7978845608028654
    inner = c * (x + 0.044715 * (x * x * x))
    th = jnp.tanh(inner)
    g = 0.5 * x * (1.0 + th)
    dg = 0.5 * (1.0 + th) + 0.5 * x * (1.0 - th * th) * (c * (1.0 + 3.0 * 0.044715 * (x * x)))
    return g, dg


def _window_sum(ext, w, back):
    n = ext.shape[0]
    s, k = ext, 1
    while k < w:
        s = s + pltpu.roll(s, k if back else n - k, 0)
        k *= 2
    return s


def _rstd(x):
    return lax.rsqrt(jnp.mean(x * x, axis=-1, keepdims=True) + EPS)


def _rms_bwd(dy, xhat, rstd, gain):
    dxh = dy * gain
    dx = rstd * (dxh - xhat * jnp.mean(dxh * xhat, axis=-1, keepdims=True))
    return dx, jnp.sum(dy * xhat, axis=0, keepdims=True)


def _bd(xb, w_ref):
    return jnp.concatenate(
        [jnp.dot(xb[:, :256], w_ref[0], preferred_element_type=F32),
         jnp.dot(xb[:, 256:], w_ref[1], preferred_element_type=F32)], axis=1)


def _bd_t(xb, w_ref):
    return jnp.concatenate(
        [lax.dot_general(xb[:, :256], w_ref[0], NT, preferred_element_type=F32),
         lax.dot_general(xb[:, 256:], w_ref[1], NT, preferred_element_type=F32)], axis=1)


def _bd_grad(xb, db):
    return jnp.stack(
        [lax.dot_general(xb[:, :256], db[:, :256], TN, preferred_element_type=F32),
         lax.dot_general(xb[:, 256:], db[:, 256:], TN, preferred_element_type=F32)], axis=0)


def _fill_block_diag(dst, src_ref):
    n, k, _ = src_ref.shape
    dst[...] = jnp.zeros(dst.shape, BF16)
    for b in range(n):
        p, q = divmod(b, 256 // k)
        dst[p, q * k:(q + 1) * k, q * k:(q + 1) * k] = src_ref[b].astype(BF16)


def _diag_pack(w, k):
    lane = lax.broadcasted_iota(jnp.int32, (k, 256), 1)
    out = w[0:k]
    for q in range(1, 256 // k):
        out = jnp.where(lane >= q * k, w[q * k:(q + 1) * k], out)
    return out


def _y_pos(b):
    return 4 * (b % 2) + b // 2


def _softplus_neg_lambda(pv):
    z = -pv[ROW_LAM:ROW_LAM + 1, :]
    return jnp.maximum(z, 0.0) + jnp.log(1.0 + jnp.exp(-jnp.abs(z)))


def _lru_gates(e_lru, pv, wa_ref, wx_ref, tm):
    xc = pv[ROW_CB:ROW_CB + 1, :]
    for k in range(4):
        xc = xc + e_lru[pl.ds(HALO - 3 + k, tm), :] * pv[ROW_CW + k:ROW_CW + k + 1, :]
    xcb = xc.astype(BF16)
    r = _sigmoid(_bd(xcb, wa_ref) + pv[ROW_BA:ROW_BA + 1, :])
    ig = _sigmoid(_bd(xcb, wx_ref) + pv[ROW_BX:ROW_BX + 1, :])
    return xc, r, ig, (-LRU_C * r) * _softplus_neg_lambda(pv)


def _lru_decay(la):
    a = jnp.exp(la)
    om = -jnp.tanh(la) * (1.0 + a * a)
    omc = jnp.maximum(om, 1e-12)
    rmult = lax.rsqrt(omc)
    return a, om, omc * rmult, rmult


def _over_count(v, w, inv_head):
    return jnp.concatenate([v[0:HALO] * inv_head, v[HALO:] * (1.0 / w)], axis=0)


def _pool_pre(e_pool, pv, wp_ref, tm, t0):
    t_head = t0 + lax.broadcasted_iota(jnp.int32, (HALO, 1), 0)
    parts, inv_heads = [], []
    for g, w in enumerate(POOL_WINDOWS):
        ext = e_pool[:, pl.ds(128 * g, 128)]
        s = _window_sum(ext, w, back=True)[HALO:, :]
        inv_head = 1.0 / jnp.minimum(t_head + 1, w).astype(F32)
        inv_heads.append(inv_head)
        parts.append(_over_count(s, w, inv_head) - ext[HALO:, :])
    pooled = jnp.concatenate(parts, axis=1)
    pooled_b = pooled.astype(BF16)
    zp = _bd(pooled_b, wp_ref) + pv[ROW_PB:ROW_PB + 1, :]
    return pooled_b, zp, inv_heads


def _scan_tile(a_ref, b_ref, out_ref, carry, tm, reverse):
    row = lax.broadcasted_iota(jnp.int32, (8, LRU_W), 0)
    nblk = tm // 8

    def local_scan(blk):
        r0 = pl.multiple_of(blk * 8, 8)
        av = a_ref[pl.ds(r0, 8), :]
        bv = b_ref[pl.ds(r0, 8), :]
        for d in (1, 2, 4):
            sh = (8 - d) if reverse else d
            a_s = pltpu.roll(av, sh, 0)
            b_s = pltpu.roll(bv, sh, 0)
            m = (row < 8 - d) if reverse else (row >= d)
            bv = jnp.where(m, av * b_s + bv, bv)
            av = jnp.where(m, av * a_s, av)
        return r0, av, bv

    def step(i, hin):
        local = [local_scan((nblk - 1 - (i * SCAN_UNROLL + j)) if reverse else (i * SCAN_UNROLL + j))
                 for j in range(SCAN_UNROLL)]
        for r0, av, bv in local:
            hv = av * hin + bv
            out_ref[pl.ds(r0, 8), :] = hv
            hin = jnp.broadcast_to(hv[0:1, :] if reverse else hv[7:8, :], (8, LRU_W))
        return hin

    return lax.fori_loop(0, nblk // SCAN_UNROLL, step, carry)


MESH = pl.DeviceIdType.MESH
ANY = pl.BlockSpec(memory_space=pl.ANY)


def _place():
    x, y, c = lax.axis_index("x"), lax.axis_index("y"), lax.axis_index("c")
    chips = [(1 - x, y), (x, 1 - y), (1 - x, 1 - y)]
    return x, y, c, chips


class _Gather:
    def __init__(self, ins, outs, send_sems, recv_sems, local_sems, core_major=False):
        self.ins, self.outs, self.n = ins, outs, len(ins)
        self.send_sems, self.recv_sems, self.local_sems = send_sems, recv_sems, local_sems
        self.core_major = core_major

    @staticmethod
    def scratch(n):
        return [pltpu.SemaphoreType.DMA((7, n)), pltpu.SemaphoreType.DMA((7, n)), pltpu.SemaphoreType.DMA((n,))]

    def _slot(self, a, px, py, pc):
        return self.outs[a].at[4 * pc + 2 * px + py if self.core_major else 4 * px + 2 * py + pc]

    def _copy(self, a, k, block, to, src=None):
        return pltpu.make_async_remote_copy(
            src_ref=self._slot(a, *block) if src is None else src, dst_ref=self._slot(a, *block),
            send_sem=self.send_sems.at[k, a], recv_sem=self.recv_sems.at[k, a], device_id=to, device_id_type=MESH)

    def _mine(self, a):
        x, y, c, _ = _place()
        return pltpu.make_async_copy(self.ins[a], self._slot(a, x, y, c), self.local_sems.at[a])

    def _first(self, a):
        x, y, c, chips = _place()
        me = (x, y, c)
        return ([self._copy(a, 0, me, (x, y, 1 - c), src=self.ins[a])]
                + [self._copy(a, 1 + j, me, (*chip, c), src=self.ins[a]) for j, chip in enumerate(chips)])

    def start(self):
        for a in range(self.n):
            self._mine(a).start()
        for a in range(self.n):
            for cp in self._first(a):
                cp.start()

    def finish(self):
        x, y, c, chips = _place()
        me, sibling = (x, y, c), (x, y, 1 - c)
        passed = []
        for j, chip in enumerate(chips):
            for a in range(self.n):
                self._copy(a, 1 + j, (*chip, c), me).wait_recv()
                fwd = self._copy(a, 4 + j, (*chip, c), sibling)
                fwd.start()
                passed.append(fwd)
        for a in range(self.n):
            self._copy(a, 0, (x, y, 1 - c), me).wait_recv()
            for j, chip in enumerate(chips):
                self._copy(a, 4 + j, (*chip, 1 - c), me).wait_recv()
        for a in range(self.n):
            for cp in self._first(a):
                cp.wait_send()
        for cp in passed:
            cp.wait_send()
        for a in range(self.n):
            self._mine(a).wait()


class _HalfExchange:
    def __init__(self, in_ref, out_ref, send_sems, recv_sems, local_sem):
        self.in_ref, self.out_ref = in_ref, out_ref
        self.send_sems, self.recv_sems, self.local_sem = send_sems, recv_sems, local_sem

    @staticmethod
    def scratch():
        return [pltpu.SemaphoreType.DMA((4,)), pltpu.SemaphoreType.DMA((4,)), pltpu.SemaphoreType.DMA]

    def _send(self, j, wait):
        x, y, c, _ = _place()
        my_chip = 2 * x + y
        to_me = (c == x) & (y == j // 2) & (c == j % 2)

        @pl.when(to_me)
        def _():
            local = pltpu.make_async_copy(self.in_ref.at[j], self.out_ref.at[my_chip], self.local_sem)
            local.wait() if wait else local.start()

        @pl.when(jnp.logical_not(to_me))
        def _():
            remote = pltpu.make_async_remote_copy(
                src_ref=self.in_ref.at[j], dst_ref=self.out_ref.at[my_chip], send_sem=self.send_sems.at[j],
                recv_sem=self.recv_sems.at[my_chip], device_id=(c, j // 2, j % 2), device_id_type=MESH)
            remote.wait_send() if wait else remote.start()

    def start(self):
        for j in range(4):
            self._send(j, wait=False)

    def finish(self):
        x, y, c, _ = _place()
        for j in range(4):
            self._send(j, wait=True)
        for k in range(4):
            from_me = (k // 2 == x) & (k % 2 == y) & (c == x)

            @pl.when(jnp.logical_not(from_me))
            def _():
                pltpu.make_async_remote_copy(
                    src_ref=self.in_ref.at[0], dst_ref=self.out_ref.at[k], send_sem=self.send_sems.at[0],
                    recv_sem=self.recv_sems.at[k], device_id=(k // 2, k % 2, x), device_id_type=MESH).wait_recv()


def _half_exchange(arr, name):
    def body(in_ref, out_ref, send_sems, recv_sems, local_sem):
        exchange = _HalfExchange(in_ref, out_ref, send_sems, recv_sems, local_sem)
        exchange.start()
        exchange.finish()

    return pl.pallas_call(
        body, name=name, out_shape=_sds(arr.shape, arr.dtype), in_specs=[ANY], out_specs=ANY,
        scratch_shapes=_HalfExchange.scratch(),
    )(arr)


class _ChipExchange:
    def __init__(self, ins, outs, send_sems, recv_sems, local_sems):
        self.ins, self.outs, self.n = ins, outs, len(ins)
        self.send_sems, self.recv_sems, self.local_sems = send_sems, recv_sems, local_sems

    @staticmethod
    def scratch(n):
        return [pltpu.SemaphoreType.DMA((3, n)), pltpu.SemaphoreType.DMA((3, n)), pltpu.SemaphoreType.DMA((n,))]

    def _local(self, a):
        x, y, _, _ = _place()
        me = 2 * x + y
        return pltpu.make_async_copy(self.ins[a].at[me], self.outs[a].at[me], self.local_sems.at[a])

    def _copies(self, a):
        x, y, c, chips = _place()
        me = 2 * x + y
        return [(pltpu.make_async_remote_copy(
                     src_ref=self.ins[a].at[2 * px + py], dst_ref=self.outs[a].at[me],
                     send_sem=self.send_sems.at[k, a], recv_sem=self.recv_sems.at[k, a],
                     device_id=(px, py, c), device_id_type=MESH),
                 pltpu.make_async_remote_copy(
                     src_ref=self.ins[a].at[me], dst_ref=self.outs[a].at[2 * px + py],
                     send_sem=self.send_sems.at[k, a], recv_sem=self.recv_sems.at[k, a],
                     device_id=(px, py, c), device_id_type=MESH))
                for k, (px, py) in enumerate(chips)]

    def start(self):
        for a in range(self.n):
            self._local(a).start()
        for a in range(self.n):
            for send, _ in self._copies(a):
                send.start()

    def finish(self):
        for a in range(self.n):
            for send, recv in self._copies(a):
                send.wait_send()
                recv.wait_recv()
        for a in range(self.n):
            self._local(a).wait()


def _gathering(body, n_steps, n_s, core_major):
    def wrapped(*refs, n_in, n_out):
        ins, sh_in = refs[:n_in], refs[n_in:n_in + n_s]
        outs, sh_out = refs[n_in + n_s:n_in + n_s + n_out], refs[n_in + n_s + n_out:n_in + 2 * n_s + n_out]
        rest = refs[n_in + 2 * n_s + n_out:]
        gather = _Gather(sh_in, sh_out, *rest[len(rest) - 3:], core_major=core_major)
        i = pl.program_id(0)

        @pl.when(i == 0)
        def _():
            gather.start()

        body(*ins, *outs, *rest[:len(rest) - 3])

        @pl.when(i == n_steps - 1)
        def _():
            gather.finish()

    return wrapped


def _norm_in(x, g_mix, shards, tm):
    T = x.shape[0]
    n_t = T // tm
    n_s = len(shards)

    def norm(x_ref, g_ref, h_ref):
        xv = x_ref[...]
        h_ref[...] = (xv * _rstd(xv) * g_ref[...]).astype(BF16)

    outs = pl.pallas_call(
        functools.partial(_gathering(norm, n_t, n_s, core_major=False), n_in=2, n_out=1), name="norm_in", grid=(n_t,),
        in_specs=[pl.BlockSpec((tm, D_MODEL), lambda i: (i, 0)), pl.BlockSpec((1, D_MODEL), lambda i: (0, 0))]
        + [ANY] * n_s,
        out_specs=[pl.BlockSpec((tm, D_MODEL), lambda i: (i, 0))] + [ANY] * n_s,
        out_shape=[_sds((T, D_MODEL), BF16)] + [_sds((N_DEV,) + a.shape, a.dtype) for a in shards],
        scratch_shapes=_Gather.scratch(n_s),
        compiler_params=pltpu.CompilerParams(dimension_semantics=("arbitrary",)),
    )(x, g_mix, *shards)
    return outs[0], list(outs[1:])


def _mix_in(h1, w_in_t, tm, shards):
    T = h1.shape[0]
    n_t = T // tm
    n_s = len(shards)

    def project(h_ref, w_ref, u_ref):
        u_ref[...] = lax.dot_general(h_ref[...], w_ref[...], NT, preferred_element_type=F32)

    outs = pl.pallas_call(
        functools.partial(_gathering(project, n_t, n_s, core_major=True), n_in=2, n_out=1), name="mix_in", grid=(n_t,),
        in_specs=[pl.BlockSpec((tm, D_MODEL), lambda i: (i, 0)), pl.BlockSpec((D_IN, D_MODEL), lambda i: (0, 0))]
        + [ANY] * n_s,
        out_specs=[pl.BlockSpec((tm, D_IN), lambda i: (i, 0))] + [ANY] * n_s,
        out_shape=[_sds((T, D_IN), F32)] + [_sds((N_DEV,) + a.shape, a.dtype) for a in shards],
        scratch_shapes=_Gather.scratch(n_s),
        compiler_params=pltpu.CompilerParams(dimension_semantics=("arbitrary",)),
    )(h1, w_in_t, *shards)
    return outs[0], list(outs[1:])


def _mixer_fwd(u, x, pv, wa, wx, wp, w_out_b, g_ffn, tm, shards=()):
    T = u.shape[0]
    n_s = len(shards)
    n_t = T // tm

    def body(u_ref, x_ref, pv_ref, wa_in, wx_in, wp_in, wo_ref, gf_ref, *rest):
        sh_in, rest = rest[:n_s], rest[n_s:]
        y_ref, hs_ref, hres_ref, h2_ref, saved_ref = rest[:5]
        sh_out, rest = rest[5:5 + n_s], rest[5 + n_s:]
        e_lru, e_pool, a_s, b_s, hc, wa_ref, wx_ref, wp_ref = rest[:8]
        gather = _Gather(sh_in, sh_out, *rest[8:], core_major=True) if n_s else None
        i = pl.program_id(0)

        @pl.when(i == 0)
        def _():
            if gather:
                gather.start()
            e_lru[pl.ds(0, HALO), :] = jnp.zeros((HALO, LRU_W), F32)
            e_pool[pl.ds(0, HALO), :] = jnp.zeros((HALO, POOL_W), F32)
            hc[...] = jnp.zeros((8, LRU_W), F32)
            _fill_block_diag(wa_ref, wa_in)
            _fill_block_diag(wx_ref, wx_in)
            _fill_block_diag(wp_ref, wp_in)

        e_lru[pl.ds(HALO, tm), :] = u_ref[:, 0:LRU_W]
        e_pool[pl.ds(HALO, tm), :] = u_ref[:, 2 * LRU_W:D_IN]
        pv = pv_ref[...]
        xc, r, ig, la = _lru_gates(e_lru, pv, wa_ref, wx_ref, tm)
        for q, val in enumerate((xc, r, ig, la)):
            saved_ref[:, LRU_W * q:LRU_W * (q + 1)] = val
        a, _, mult, _ = _lru_decay(la)
        a_s[...] = a
        b_s[...] = mult * (ig * xc)
        hc[...] = _scan_tile(a_s, b_s, hs_ref, hc[...], tm, reverse=False)
        gl, _ = _gelu_parts(u_ref[:, LRU_W:2 * LRU_W])
        y_lru = hs_ref[...] * gl
        _, zp, _ = _pool_pre(e_pool, pv, wp_ref, tm, i * tm)
        y_pool = zp * pv[ROW_PS:ROW_PS + 1, :]
        yn = jnp.concatenate([y_lru * _rstd(y_lru) * pv[ROW_GL:ROW_GL + 1, :],
                              y_pool * _rstd(y_pool) * pv[ROW_GP:ROW_GP + 1, :]], axis=1).astype(BF16)
        for b in range(N_DEV):
            y_ref[:, 128 * _y_pos(b):128 * (_y_pos(b) + 1)] = yn[:, 128 * b:128 * (b + 1)]
        hr = x_ref[...] + jnp.dot(y_ref[...], wo_ref[...], preferred_element_type=F32)
        hres_ref[...] = hr
        h2_ref[...] = (hr * _rstd(hr) * gf_ref[...]).astype(BF16)
        e_lru[pl.ds(0, HALO), :] = e_lru[pl.ds(tm, HALO), :]
        e_pool[pl.ds(0, HALO), :] = e_pool[pl.ds(tm, HALO), :]

        if gather:
            @pl.when(i == n_t - 1)
            def _():
                gather.finish()

    full = lambda shape: pl.BlockSpec(shape, lambda i: (0,) * len(shape))
    row = lambda w: pl.BlockSpec((tm, w), lambda i: (i, 0))
    outs = pl.pallas_call(
        body, name="mixer_fwd", grid=(n_t,),
        in_specs=[row(D_IN), row(D_MODEL), full((16, LRU_W)), full((8, 64, 64)), full((8, 64, 64)), full((4, 128, 128)),
                  full((D_MODEL, D_MODEL)), full((1, D_MODEL))] + [ANY] * n_s,
        out_specs=[row(D_MODEL), row(LRU_W), row(D_MODEL), row(D_MODEL), row(4 * LRU_W)] + [ANY] * n_s,
        out_shape=[_sds((T, D_MODEL), BF16), _sds((T, LRU_W), F32), _sds((T, D_MODEL), F32), _sds((T, D_MODEL), BF16),
                   _sds((T, 4 * LRU_W), F32)] + [_sds((N_DEV,) + a.shape, a.dtype) for a in shards],
        scratch_shapes=[pltpu.VMEM((HALO + tm, LRU_W), F32), pltpu.VMEM((HALO + tm, POOL_W), F32),
                        pltpu.VMEM((tm, LRU_W), F32), pltpu.VMEM((tm, LRU_W), F32), pltpu.VMEM((8, LRU_W), F32)]
        + [pltpu.VMEM((2, 256, 256), BF16)] * 3 + (_Gather.scratch(n_s) if n_s else []),
        compiler_params=pltpu.CompilerParams(dimension_semantics=("arbitrary",)),
    )(u, x, pv, wa, wx, wp, w_out_b, g_ffn, *shards)
    return outs[0], outs[1], outs[2], outs[3], outs[4], list(outs[5:])


def _ffn_fwd(hres, h2, w1_b, w3_b, w2_b, g_fin, tgt, tm):
    T = hres.shape[0]

    def body(hres_ref, h2_ref, w1_ref, w3_ref, w2_ref, gfin_ref, tgt_ref,
             g_ref, v_ref, ff_ref, d3_ref, loss_ref, dgfin_ref):
        @pl.when(pl.program_id(0) == 0)
        def _():
            loss_ref[...] = jnp.zeros((8, 128), F32)
            dgfin_ref[...] = jnp.zeros((1, D_MODEL), F32)

        h2 = h2_ref[...]
        h3 = hres_ref[...]
        for lo, hi in FF_CHUNKS:
            g = lax.dot_general(h2, w1_ref[lo:hi, :], NT, preferred_element_type=F32)
            v = lax.dot_general(h2, w3_ref[lo:hi, :], NT, preferred_element_type=F32)
            g_ref[:, lo:hi] = g.astype(BF16)
            v_ref[:, lo:hi] = v.astype(BF16)
            ff = ((g * _sigmoid(g)) * v).astype(BF16)
            ff_ref[:, lo:hi] = ff
            h3 = h3 + jnp.dot(ff, w2_ref[lo:hi, :], preferred_element_type=F32)

        rstd = _rstd(h3)
        xh = h3 * rstd
        gfin = gfin_ref[...]
        err = xh * gfin - tgt_ref[...]
        loss_ref[...] += 0.5 * jnp.sum(jnp.mean(err * err, axis=-1, keepdims=True))
        dx, dgain = _rms_bwd(err * (1.0 / D_MODEL), xh, rstd, gfin)
        d3_ref[...] = dx
        dgfin_ref[...] += dgain

    row = lambda w: pl.BlockSpec((tm, w), lambda i: (i, 0))
    const = lambda shape: pl.BlockSpec(shape, lambda i: (0,) * len(shape))
    weight = pl.BlockSpec((D_FF, D_MODEL), lambda i: (0, 0), pipeline_mode=pl.Buffered(1))
    return pl.pallas_call(
        body, name="ffn_fwd", grid=(T // tm,),
        in_specs=[row(D_MODEL), row(D_MODEL), weight, weight, weight, const((1, D_MODEL)), row(D_MODEL)],
        out_specs=[row(D_FF), row(D_FF), row(D_FF), row(D_MODEL), const((8, 128)), const((1, D_MODEL))],
        out_shape=[_sds((T, D_FF), BF16), _sds((T, D_FF), BF16), _sds((T, D_FF), BF16),
                   _sds((T, D_MODEL), F32), _sds((8, 128), F32), _sds((1, D_MODEL), F32)],
        compiler_params=pltpu.CompilerParams(dimension_semantics=("arbitrary",)),
    )(hres, h2, w1_b, w3_b, w2_b, g_fin, tgt)


def _ffn_bwd(d3, g, v, w1_b, w3_b, w2_b, hres, g_ffn, tm):
    T = d3.shape[0]

    def body(d3_ref, g_ref, v_ref, w1_ref, w3_ref, w2_ref, hres_ref, gf_ref, dg_ref, dv_ref, d2_ref, dgffn_ref):
        @pl.when(pl.program_id(0) == 0)
        def _():
            dgffn_ref[...] = jnp.zeros((1, D_MODEL), F32)

        d3 = d3_ref[...]
        d3b = d3.astype(BF16)
        dh2 = jnp.zeros((tm, D_MODEL), F32)
        for lo, hi in FF_CHUNKS:
            dff = lax.dot_general(d3b, w2_ref[lo:hi, :], NT, preferred_element_type=F32)
            gv = g_ref[:, lo:hi].astype(F32)
            vv = v_ref[:, lo:hi].astype(F32)
            sg = _sigmoid(gv)
            sl = gv * sg
            dgb = (dff * vv * (sg * (1.0 + gv * (1.0 - sg)))).astype(BF16)
            dvb = (dff * sl).astype(BF16)
            dg_ref[:, lo:hi] = dgb
            dv_ref[:, lo:hi] = dvb
            dh2 = dh2 + (jnp.dot(dgb, w1_ref[lo:hi, :], preferred_element_type=F32)
                         + jnp.dot(dvb, w3_ref[lo:hi, :], preferred_element_type=F32))

        hr = hres_ref[...]
        rstd = _rstd(hr)
        dx, dgain = _rms_bwd(dh2, hr * rstd, rstd, gf_ref[...])
        d2_ref[...] = d3 + dx
        dgffn_ref[...] += dgain

    row = lambda w: pl.BlockSpec((tm, w), lambda i: (i, 0))
    const = lambda shape: pl.BlockSpec(shape, lambda i: (0,) * len(shape))
    weight = pl.BlockSpec((D_FF, D_MODEL), lambda i: (0, 0), pipeline_mode=pl.Buffered(1))
    return pl.pallas_call(
        body, name="ffn_bwd", grid=(T // tm,),
        in_specs=[row(D_MODEL), row(D_FF), row(D_FF), weight, weight, weight, row(D_MODEL), const((1, D_MODEL))],
        out_specs=[row(D_FF), row(D_FF), row(D_MODEL), const((1, D_MODEL))],
        out_shape=[_sds((T, D_FF), BF16), _sds((T, D_FF), BF16), _sds((T, D_MODEL), F32), _sds((1, D_MODEL), F32)],
        compiler_params=pltpu.CompilerParams(dimension_semantics=("arbitrary",)),
    )(d3, g, v, w1_b, w3_b, w2_b, hres, g_ffn)


def _at_b_pair(a, b, c_arr, name, tk, gather=(), cols=None, exchange=None):
    T, M = a.shape
    N, col = cols if cols else (b.shape[1], 0)
    hm, n_k = M // 2, T // tk
    n_g = len(gather)
    moved = list(gather) + ([] if exchange is None else [exchange])
    n_m = len(moved)

    def body(c_ref, a_ref, b_ref, *rest):
        m_in, o_ref, rest = rest[:n_m], rest[n_m], rest[n_m + 1:]
        m_out, rest = rest[:n_m], rest[n_m:]
        acc, landed, send_sem, recv_sem = rest[:4]
        movers = []
        if n_g:
            movers.append(_Gather(m_in[:n_g], m_out[:n_g], *rest[4:7]))
        if n_m > n_g:
            movers.append(_HalfExchange(m_in[n_g], m_out[n_g], *rest[len(rest) - 3:]))
        ph, k = pl.program_id(0), pl.program_id(1)

        def hand_over():
            x, y, c, _ = _place()
            return pltpu.make_async_remote_copy(
                src_ref=acc.at[0], dst_ref=landed, send_sem=send_sem, recv_sem=recv_sem,
                device_id=(x, y, 1 - c), device_id_type=MESH)

        if movers:
            @pl.when((ph == 0) & (k == 0))
            def _():
                for mover in movers:
                    mover.start()

        @pl.when(k == 0)
        def _():
            acc[ph] = jnp.zeros((hm, N), F32)

        acc[ph] += lax.dot_general(a_ref[...].astype(BF16), b_ref[...].astype(BF16), TN, preferred_element_type=F32)

        @pl.when((ph == 0) & (k == n_k - 1))
        def _():
            hand_over().start()

        @pl.when((ph == 1) & (k == n_k - 1))
        def _():
            copy = hand_over()
            copy.wait_recv()
            o_ref[...] = (acc[1] + landed[...]).astype(BF16)
            copy.wait_send()
            for mover in movers:
                mover.finish()

    outs = pl.pallas_call(
        body, name=name,
        grid_spec=pltpu.PrefetchScalarGridSpec(
            num_scalar_prefetch=1, grid=(2, n_k),
            in_specs=[pl.BlockSpec((tk, hm), lambda ph, k, c_ref: (k, (ph + 1 - c_ref[0]) % 2)),
                      pl.BlockSpec((tk, N), lambda ph, k, c_ref: (k, col))] + [ANY] * n_m,
            out_specs=[pl.BlockSpec((hm, N), lambda ph, k, c_ref: (0, 0))] + [ANY] * n_m,
            scratch_shapes=[pltpu.VMEM((2, hm, N), F32), pltpu.VMEM((hm, N), F32),
                            pltpu.SemaphoreType.DMA, pltpu.SemaphoreType.DMA] + (_Gather.scratch(n_g) if n_g else [])
                           + (_HalfExchange.scratch() if n_m > n_g else [])),
        out_shape=[_sds((hm, N), BF16)] + [_sds((N_DEV,) + g.shape, g.dtype) for g in gather]
                  + [_sds(m.shape, m.dtype) for m in moved[n_g:]],
        compiler_params=pltpu.CompilerParams(dimension_semantics=("arbitrary", "arbitrary")),
    )(c_arr, a, b, *moved)
    return (outs[0], list(outs[1:])) if n_m else outs[0]


def _mixer_bwd(d2, u, hs, saved, pv, wa, wx, wp, w_out_b, tm, chip_sums=()):
    T = u.shape[0]
    n_t = T // tm
    n_x = len(chip_sums)

    def body(d2_ref, u_ref, uh_ref, hs_ref, hh_ref, saved_ref, pv_ref, wa_in, wx_in, wp_in, wo_ref, *rest):
        x_in, rest = rest[:n_x], rest[n_x:]
        du_ref, sg_ref = rest[:2]
        x_out, rest = rest[2:2 + n_x], rest[2 + n_x:]
        e_pool, e_h, a_s, b_s, dh_s, mu_s, f_x, f_p, mc, cx, cp = rest[:11]
        wa_ref, wx_ref, wp_ref, vacc_ref, dwa_ref, dwx_ref, dwp_ref = rest[11:18]
        exchange = _ChipExchange(x_in, x_out, *rest[18:]) if n_x else None
        s = pl.program_id(0)
        it = n_t - 1 - s

        @pl.when(s == 0)
        def _():
            if exchange:
                exchange.start()
            mc[...] = jnp.zeros((8, LRU_W), F32)
            cx[...] = jnp.zeros((8, LRU_W), F32)
            cp[...] = jnp.zeros((HALO, POOL_W), F32)
            vacc_ref[...] = jnp.zeros((16, LRU_W), F32)
            dwa_ref[...] = jnp.zeros((2, 256, 256), F32)
            dwx_ref[...] = jnp.zeros((2, 256, 256), F32)
            dwp_ref[...] = jnp.zeros((2, 256, 256), F32)
            _fill_block_diag(wa_ref, wa_in)
            _fill_block_diag(wx_ref, wx_in)
            _fill_block_diag(wp_ref, wp_in)

        first = it == 0
        e_pool[pl.ds(0, HALO), :] = jnp.where(first, 0.0, uh_ref[...])
        e_pool[pl.ds(HALO, tm), :] = u_ref[:, 2 * LRU_W:D_IN]
        e_h[pl.ds(0, 8), :] = jnp.where(first, 0.0, hh_ref[...])
        e_h[pl.ds(8, tm), :] = hs_ref[...]
        pv = pv_ref[...]
        saved = lambda q: saved_ref[:, LRU_W * q:LRU_W * (q + 1)]

        dyn = lax.dot_general(d2_ref[...].astype(BF16), wo_ref[...], NT, preferred_element_type=F32)
        dyn = jnp.concatenate([dyn[:, 128 * _y_pos(b):128 * (_y_pos(b) + 1)] for b in range(N_DEV)], axis=1)

        h = hs_ref[...]
        ug = u_ref[:, LRU_W:2 * LRU_W]
        gl, dgl = _gelu_parts(ug)
        y_lru = h * gl
        rstd_l = _rstd(y_lru)
        dy_lru, d_gain_l = _rms_bwd(dyn[:, 0:LRU_W], y_lru * rstd_l, rstd_l, pv[ROW_GL:ROW_GL + 1, :])
        dh = dy_lru * gl
        du_ref[:, LRU_W:2 * LRU_W] = (dy_lru * h * dgl).astype(BF16)
        a_s[...] = jnp.exp(saved(3))
        b_s[...] = a_s[...] * dh
        dh_s[...] = dh
        mu_s[pl.ds(tm, 8), :] = mc[...]
        mc[...] = _scan_tile(a_s, b_s, mu_s, mc[...], tm, reverse=True)
        xc, r, ig = saved(0), saved(1), saved(2)
        a, om, mult, rmult = _lru_decay(saved(3))
        lam_t = dh_s[...] + mu_s[pl.ds(1, tm), :]
        da = lam_t * e_h[pl.ds(7, tm), :]
        dmult = lam_t * (ig * xc)
        di = lam_t * (mult * xc)
        dxc = lam_t * (mult * ig)
        dla = da * a - jnp.where(om > 1e-12, dmult * ((a * a) * rmult), 0.0)
        dra = (dla * (-LRU_C * _softplus_neg_lambda(pv))) * (r * (1.0 - r))
        dia = di * (ig * (1.0 - ig))
        drab = dra.astype(BF16)
        diab = dia.astype(BF16)
        xcb = xc.astype(BF16)
        dxc = dxc + _bd_t(drab, wa_ref) + _bd_t(diab, wx_ref)
        dwa_ref[...] += _bd_grad(xcb, drab)
        dwx_ref[...] += _bd_grad(xcb, diab)
        sig_neg_lam = _sigmoid(-pv[ROW_LAM:ROW_LAM + 1, :])
        d_lam = jnp.sum(dla * r, axis=0, keepdims=True) * (LRU_C * sig_neg_lam)

        f_x[pl.ds(0, tm), :] = dxc
        f_x[pl.ds(tm, 8), :] = cx[...]
        du_lru = jnp.zeros((tm, LRU_W), F32)
        u_lru = u_ref[:, 0:LRU_W]
        d_cw = []
        for k in range(4):
            later = f_x[pl.ds(3 - k, tm), :]
            du_lru = du_lru + later * pv[ROW_CW + k:ROW_CW + k + 1, :]
            d_cw.append(jnp.sum(later * u_lru, axis=0, keepdims=True))
        du_ref[:, 0:LRU_W] = du_lru.astype(BF16)
        cx[...] = f_x[pl.ds(0, 8), :]

        pooled_b, zp, inv_cnts = _pool_pre(e_pool, pv, wp_ref, tm, it * tm)
        ps = pv[ROW_PS:ROW_PS + 1, :]
        y_pool = zp * ps
        rstd_p = _rstd(y_pool)
        dy_pool, d_gain_p = _rms_bwd(dyn[:, LRU_W:D_MODEL], y_pool * rstd_p, rstd_p, pv[ROW_GP:ROW_GP + 1, :])
        dz = dy_pool * ps
        dzb = dz.astype(BF16)
        dwp_ref[...] += _bd_grad(pooled_b, dzb)
        dpooled = _bd_t(dzb, wp_ref)
        for g, w in enumerate(POOL_WINDOWS):
            f_p[pl.ds(0, tm), pl.ds(128 * g, 128)] = _over_count(dpooled[:, 128 * g:128 * (g + 1)], w, inv_cnts[g])
        f_p[pl.ds(tm, HALO), :] = cp[...]
        for g, w in enumerate(POOL_WINDOWS):
            acc = _window_sum(f_p[:, pl.ds(128 * g, 128)], w, back=False)[0:tm, :]
            du_ref[:, 2 * LRU_W + 128 * g:2 * LRU_W + 128 * (g + 1)] = (
                acc - dpooled[:, 128 * g:128 * (g + 1)]).astype(BF16)
        cp[...] = f_p[pl.ds(0, HALO), :]

        rows = d_cw + [
            jnp.sum(dxc, axis=0, keepdims=True),
            jnp.sum(dra, axis=0, keepdims=True),
            jnp.sum(dia, axis=0, keepdims=True),
            d_lam,
            jnp.sum(dz, axis=0, keepdims=True),
            jnp.sum(dy_pool * zp, axis=0, keepdims=True),
            d_gain_l, d_gain_p,
            jnp.zeros((4, LRU_W), F32),
        ]
        vacc_ref[...] += jnp.concatenate(rows, axis=0)

        @pl.when(s == n_t - 1)
        def _():
            sg_ref[SG_VEC:SG_VEC + 16, :] = vacc_ref[:, 0:256]
            sg_ref[SG_VEC + 16:SG_VEC + 32, :] = vacc_ref[:, 256:512]
            for half in range(2):
                sg_ref[SG_WA + 64 * half:SG_WA + 64 * (half + 1), :] = _diag_pack(dwa_ref[half], 64)
                sg_ref[SG_WX + 64 * half:SG_WX + 64 * (half + 1), :] = _diag_pack(dwx_ref[half], 64)
                sg_ref[SG_WP + 128 * half:SG_WP + 128 * (half + 1), :] = _diag_pack(dwp_ref[half], 128)
            if exchange:
                exchange.finish()

    rev = lambda w: pl.BlockSpec((tm, w), lambda s: (n_t - 1 - s, 0))
    full = lambda shape: pl.BlockSpec(shape, lambda s: (0,) * len(shape))
    outs = pl.pallas_call(
        body, name="mixer_bwd", grid=(n_t,),
        in_specs=[rev(D_MODEL), rev(D_IN),
                  pl.BlockSpec((HALO, POOL_W), lambda s: (jnp.maximum((n_t - 1 - s) * (tm // HALO) - 1, 0), 2)),
                  rev(LRU_W),
                  pl.BlockSpec((8, LRU_W), lambda s: (jnp.maximum((n_t - 1 - s) * (tm // 8) - 1, 0), 0)),
                  rev(4 * LRU_W), full((16, LRU_W)), full((8, 64, 64)), full((8, 64, 64)), full((4, 128, 128)),
                  full((D_MODEL, D_MODEL))] + [ANY] * n_x,
        out_specs=[rev(D_IN), full((SG_ROWS, 256))] + [ANY] * n_x,
        out_shape=[_sds((T, D_IN), BF16), _sds((SG_ROWS, 256), F32)] + [_sds(a.shape, a.dtype) for a in chip_sums],
        scratch_shapes=[pltpu.VMEM((HALO + tm, POOL_W), F32),
                        pltpu.VMEM((8 + tm, LRU_W), F32)] + [pltpu.VMEM((tm, LRU_W), F32)] * 3 + [
                        pltpu.VMEM((tm + 8, LRU_W), F32), pltpu.VMEM((tm + 8, LRU_W), F32),
                        pltpu.VMEM((tm + HALO, POOL_W), F32), pltpu.VMEM((8, LRU_W), F32),
                        pltpu.VMEM((8, LRU_W), F32), pltpu.VMEM((HALO, POOL_W), F32)]
        + [pltpu.VMEM((2, 256, 256), BF16)] * 3 + [pltpu.VMEM((16, LRU_W), F32)] + [pltpu.VMEM((2, 256, 256), F32)] * 3
        + (_ChipExchange.scratch(n_x) if n_x else []),
        compiler_params=pltpu.CompilerParams(dimension_semantics=("arbitrary",)),
    )(d2, u, u, hs, hs, saved, pv, wa, wx, wp, w_out_b, *chip_sums)
    return outs[0], outs[1], list(outs[2:])


def _mix_in_bwd(du, x, d2, w_in_t, g_mix, tm):
    T = x.shape[0]

    def body(du_ref, x_ref, d2_ref, w_ref, g_ref, dx_ref, dg_ref):
        @pl.when(pl.program_id(0) == 0)
        def _():
            dg_ref[...] = jnp.zeros((1, D_MODEL), F32)

        dh = jnp.dot(du_ref[...], w_ref[...], preferred_element_type=F32)
        xv = x_ref[...]
        rstd = _rstd(xv)
        dx, dgain = _rms_bwd(dh, xv * rstd, rstd, g_ref[...])
        dx_ref[...] = d2_ref[...] + dx
        dg_ref[...] += dgain

    row = lambda w: pl.BlockSpec((tm, w), lambda i: (i, 0))
    const = lambda shape: pl.BlockSpec(shape, lambda i: (0,) * len(shape))
    return pl.pallas_call(
        body, name="mix_in_bwd", grid=(T // tm,),
        in_specs=[row(D_IN), row(D_MODEL), row(D_MODEL), const((D_IN, D_MODEL)), const((1, D_MODEL))],
        out_specs=[row(D_MODEL), const((1, D_MODEL))],
        out_shape=[_sds((T, D_MODEL), F32), _sds((1, D_MODEL), F32)],
        compiler_params=pltpu.CompilerParams(dimension_semantics=("arbitrary",)),
    )(du, x, d2, w_in_t, g_mix)


def _adamw(w, g, m, v):
    m = ADAM_B1 * m + (1.0 - ADAM_B1) * g
    v = ADAM_B2 * v + (1.0 - ADAM_B2) * (g * g)
    m_hat = m / (1.0 - ADAM_B1 ** ADAM_STEP)
    v_hat = v / (1.0 - ADAM_B2 ** ADAM_STEP)
    delta = -ADAM_LR * (m_hat / (jnp.sqrt(v_hat) + ADAM_EPS) + ADAM_WD * w)
    return delta, m, v


def _adam_shards(ws, ms, vs, parts):
    n = len(ws)
    n_blk = [w.shape[0] // ADAM_ROWS for w in ws]
    first = [sum(len(p) for p in parts[:a]) for a in range(n + 1)]

    def body(*refs):
        w_refs, m_refs, v_refs, p_refs, outs = (refs[:n], refs[n:2 * n], refs[2 * n:3 * n],
                                                refs[3 * n:3 * n + first[n]], refs[3 * n + first[n]:])
        i = pl.program_id(0)
        for a in range(n):
            @pl.when(i < n_blk[a])
            def _(a=a):
                pieces = []
                for p_ref in p_refs[first[a]:first[a + 1]]:
                    piece = p_ref[0].astype(F32)
                    for j in range(1, 4):
                        piece = piece + p_ref[j].astype(F32)
                    pieces.append(piece)
                g = pieces[0] if len(pieces) == 1 else jnp.concatenate(pieces, axis=1)
                delta, new_m, new_v = _adamw(w_refs[a][...], g, m_refs[a][...], v_refs[a][...])
                for kind, val in enumerate((g, delta, new_m, new_v)):
                    outs[4 * a + kind][...] = val

    blk = lambda a: pl.BlockSpec((ADAM_ROWS, D_MODEL), lambda i: (jnp.minimum(i, n_blk[a] - 1), 0))
    part_blk = lambda a, p: pl.BlockSpec((4, ADAM_ROWS, p.shape[2]), lambda i: (0, jnp.minimum(i, n_blk[a] - 1), 0))
    res = pl.pallas_call(
        body, name="adam_shards", grid=(max(n_blk),),
        in_specs=[blk(a) for a in range(n)] * 3 + [part_blk(a, p) for a in range(n) for p in parts[a]],
        out_specs=[blk(a) for a in range(n) for _ in range(4)],
        out_shape=[_sds(w.shape, F32) for w in ws for _ in range(4)],
        compiler_params=pltpu.CompilerParams(dimension_semantics=("arbitrary",)),
    )(*ws, *ms, *vs, *[p for ps in parts for p in ps])
    return [tuple(res[4 * a:4 * a + 4]) for a in range(n)]


SMALL_PARAMS = [("norm_mix_g", (1, D_MODEL)), ("conv_w", (1, 4, 64)), ("conv_b", (1, LRU_W)),
                ("gate_a_w", (1, 8, 64, 64)), ("gate_a_b", (1, LRU_W)), ("gate_x_w", (1, 8, 64, 64)),
                ("gate_x_b", (1, LRU_W)), ("lru_lambda", (1, LRU_W)), ("pool_w", (1, 4, 128, 128)),
                ("pool_b", (1, POOL_W)), ("pool_scale", (1, POOL_W)), ("norm_lru_g", (1, LRU_W)),
                ("norm_pool_g", (1, POOL_W)), ("norm_ffn_g", (1, D_MODEL)), ("final_norm_g", (1, D_MODEL))]
VEC_ROW = dict(conv_b=ROW_CB, gate_a_b=ROW_BA, gate_x_b=ROW_BX, lru_lambda=ROW_LAM, pool_b=ROW_PB, pool_scale=ROW_PS,
               norm_lru_g=ROW_GL, norm_pool_g=ROW_GP)
WHOLE = (Ellipsis,)


def _unpack_mixer_grads(sg, dev):
    vec = jnp.concatenate([sg[SG_VEC:SG_VEC + 16], sg[SG_VEC + 16:SG_VEC + 32]], axis=1)
    out = {nm: [(WHOLE, vec[r:r + 1])] for nm, r in VEC_ROW.items()}
    own = jnp.zeros((4, 64), F32)
    for d in range(N_DEV):
        own = jnp.where(dev == d, vec[ROW_CW:ROW_CW + 4, 64 * d:64 * (d + 1)], own)
    out["conv_w"] = [((0,), own)]
    for nm, row0 in (("gate_a_w", SG_WA), ("gate_x_w", SG_WX)):
        out[nm] = [((0, b), sg[row0 + 64 * (b // 4):row0 + 64 * (b // 4 + 1), 64 * (b % 4):64 * (b % 4 + 1)])
                   for b in range(8)]
    out["pool_w"] = [((0, b), sg[SG_WP + 128 * (b // 2):SG_WP + 128 * (b // 2 + 1), 128 * (b % 2):128 * (b % 2 + 1)])
                     for b in range(4)]
    return out


def _adam_small(parts, w, m, v):
    names = [nm for nm, _ in SMALL_PARAMS]
    n = len(names)

    def body(sg_ref, gm_ref, gf_ref, gn_ref, ls_ref, *rest):
        w_refs, m_refs, v_refs, outs = rest[:n], rest[n:2 * n], rest[2 * n:3 * n], rest[3 * n:]
        dev = 4 * lax.axis_index("x") + 2 * lax.axis_index("y") + lax.axis_index("c")

        def total(ref):
            acc = ref[0]
            for d in range(1, N_DEV):
                acc = acc + ref[d]
            return acc

        pieces = _unpack_mixer_grads(total(sg_ref), dev)
        pieces["norm_mix_g"] = [(WHOLE, total(gm_ref))]
        pieces["norm_ffn_g"] = [(WHOLE, total(gf_ref))]
        pieces["final_norm_g"] = [(WHOLE, total(gn_ref))]
        for i, nm in enumerate(names):
            for idx, g in pieces[nm]:
                delta, new_m, new_v = _adamw(w_refs[i][idx], g, m_refs[i][idx], v_refs[i][idx])
                for kind, val in enumerate((g, delta, new_m, new_v)):
                    outs[4 * i + kind][idx] = val
        outs[4 * n][...] = total(ls_ref)

    shapes = [_sds(shape, F32) for _, shape in SMALL_PARAMS for _ in range(4)] + [_sds((8, 128), F32)]
    res = pl.pallas_call(body, name="adam_small", out_shape=shapes)(
        *parts, *[w[nm] for nm in names], *[m[nm] for nm in names], *[v[nm] for nm in names])
    return {nm: tuple(res[4 * i:4 * i + 4]) for i, nm in enumerate(names)}, res[4 * n][0, 0]


def _vec_rows(conv_w_full, conv_b, ba, bx, lam, pb, ps, gl, gp):
    return jnp.concatenate([conv_w_full, conv_b, ba, bx, lam, pb, ps, gl, gp, jnp.zeros((4, LRU_W), F32)], axis=0)


WEIGHT_ORDER = ['norm_mix_g', 'w_in', 'conv_w', 'conv_b', 'gate_a_w', 'gate_a_b', 'gate_x_w', 'gate_x_b', 'lru_lambda',
                'pool_w', 'pool_b', 'pool_scale', 'norm_lru_g', 'norm_pool_g', 'w_out', 'norm_ffn_g', 'ffn_w1', 'ffn_w3',
                'ffn_w2', 'final_norm_g']


def kernel(x, norm_mix_g, w_in, conv_w, conv_b, gate_a_w, gate_a_b, gate_x_w, gate_x_b, lru_lambda, pool_w, pool_b, pool_scale, norm_lru_g, norm_pool_g, w_out, norm_ffn_g, ffn_w1, ffn_w3, ffn_w2, final_norm_g, loss_target, m_norm_mix_g, m_w_in, m_conv_w, m_conv_b, m_gate_a_w, m_gate_a_b, m_gate_x_w, m_gate_x_b, m_lru_lambda, m_pool_w, m_pool_b, m_pool_scale, m_norm_lru_g, m_norm_pool_g, m_w_out, m_norm_ffn_g, m_ffn_w1, m_ffn_w3, m_ffn_w2, m_final_norm_g, v_norm_mix_g, v_w_in, v_conv_w, v_conv_b, v_gate_a_w, v_gate_a_b, v_gate_x_w, v_gate_x_b, v_lru_lambda, v_pool_w, v_pool_b, v_pool_scale, v_norm_lru_g, v_norm_pool_g, v_w_out, v_norm_ffn_g, v_ffn_w1, v_ffn_w3, v_ffn_w2, v_final_norm_g):
    ac = lax.axis_index("c")
    tm, tmx, tk = 512, 512, 1024
    tm_in = 1024
    xs, tgt = x[0], loss_target[0]
    g_fin = final_norm_g.reshape(1, D_MODEL)
    c_arr = jnp.reshape(ac, (1,)).astype(jnp.int32)

    tr = lambda w: jnp.swapaxes(w[0], 0, 1)
    own = lambda w: w[0]
    bf = lambda a: a.astype(BF16)

    h1, (g_in, g_conv) = _norm_in(xs, norm_mix_g, [bf(tr(w_in)), conv_w[0]], tm_in)
    w_in_t = g_in.reshape(D_IN, D_MODEL)
    u, (g_out,) = _mix_in(h1, w_in_t, tm, shards=[bf(own(w_out))])
    conv_w_full = g_conv.transpose(1, 0, 2).reshape(4, LRU_W)
    pv = _vec_rows(conv_w_full, conv_b, gate_a_b, gate_x_b, lru_lambda, pool_b, pool_scale, norm_lru_g, norm_pool_g)
    wa, wx, wp = gate_a_w[0], gate_x_w[0], pool_w[0]
    w_out_b = g_out.reshape(D_MODEL, D_MODEL)
    y, hs, hres, h2, saved, (g_w1, g_w3, g_w2) = _mixer_fwd(
        u, xs, pv, wa, wx, wp, w_out_b, norm_ffn_g, tmx, shards=[bf(tr(ffn_w1)), bf(tr(ffn_w3)), bf(own(ffn_w2))])
    w1_t, w3_t, w2_b = g_w1.reshape(D_FF, D_MODEL), g_w3.reshape(D_FF, D_MODEL), g_w2.reshape(D_FF, D_MODEL)
    g, v, ff, d3, loss_acc, d_gfin = _ffn_fwd(hres, h2, w1_t, w3_t, w2_b, g_fin, tgt, tm)

    dg, dv, d2, d_gffn = _ffn_bwd(d3, g, v, w1_t, w3_t, w2_b, hres, norm_ffn_g, tm // 2)
    chips = lambda a: a.reshape(4, a.shape[0] // 4, a.shape[1])
    early_sums = [chips(_at_b_pair(y, d2, c_arr, "grad_w_out", 2 * tk)), chips(_at_b_pair(dg, h2, c_arr, "grad_w1", tk)),
                  chips(_at_b_pair(dv, h2, c_arr, "grad_w3", tk)), chips(_at_b_pair(ff, d3, c_arr, "grad_w2", tk))]
    du, d_mixer, early_parts = _mixer_bwd(d2, u, hs, saved, pv, wa, wx, wp, w_out_b, tmx, chip_sums=early_sums)
    grad_x, d_gmix = _mix_in_bwd(du, xs, d2, w_in_t, norm_mix_g, tm_in)
    (wide, narrow), tkw = GRAD_W_IN_COLS, 2 * tk
    d_win_wide, small_parts = _at_b_pair(du, h1, c_arr, "grad_w_in_wide", tkw, cols=wide,
                                         gather=[d_mixer, d_gmix, d_gffn, d_gfin, loss_acc])
    d_win_narrow, (win_wide,) = _at_b_pair(du, h1, c_arr, "grad_w_in_narrow", tkw, cols=narrow,
                                           exchange=chips(d_win_wide))
    win_narrow = _half_exchange(chips(d_win_narrow), "grads_to_chips_w_in")
    parts = [[win_wide, win_narrow]] + [[p] for p in early_parts]

    res = {}
    shard_w = dict(w_in=(w_in, m_w_in, v_w_in, tr), w_out=(w_out, m_w_out, v_w_out, own),
                   ffn_w1=(ffn_w1, m_ffn_w1, v_ffn_w1, tr), ffn_w3=(ffn_w3, m_ffn_w3, v_ffn_w3, tr),
                   ffn_w2=(ffn_w2, m_ffn_w2, v_ffn_w2, own))
    shard_res = _adam_shards([view(w) for w, _, _, view in shard_w.values()],
                             [view(m) for _, m, _, view in shard_w.values()],
                             [view(v) for _, _, v, view in shard_w.values()], parts)
    for (nm, (_, _, _, view)), outs in zip(shard_w.items(), shard_res):
        res[nm] = [(jnp.swapaxes(o, 0, 1) if view is tr else o)[None] for o in outs]

    row = lambda a: a.reshape(1, D_MODEL)
    small = lambda gm, cw, cb, wa_, ba, wx_, bx, lam, pw, pb, ps, gl, gp, gf, gn: dict(
        norm_mix_g=gm, conv_w=cw, conv_b=cb, gate_a_w=wa_, gate_a_b=ba, gate_x_w=wx_, gate_x_b=bx, lru_lambda=lam,
        pool_w=pw, pool_b=pb, pool_scale=ps, norm_lru_g=gl, norm_pool_g=gp, norm_ffn_g=gf, final_norm_g=row(gn))
    small_res, loss = _adam_small(
        small_parts,
        small(norm_mix_g, conv_w, conv_b, gate_a_w, gate_a_b, gate_x_w, gate_x_b, lru_lambda, pool_w, pool_b,
              pool_scale, norm_lru_g, norm_pool_g, norm_ffn_g, final_norm_g),
        small(m_norm_mix_g, m_conv_w, m_conv_b, m_gate_a_w, m_gate_a_b, m_gate_x_w, m_gate_x_b, m_lru_lambda, m_pool_w,
              m_pool_b, m_pool_scale, m_norm_lru_g, m_norm_pool_g, m_norm_ffn_g, m_final_norm_g),
        small(v_norm_mix_g, v_conv_w, v_conv_b, v_gate_a_w, v_gate_a_b, v_gate_x_w, v_gate_x_b, v_lru_lambda, v_pool_w,
              v_pool_b, v_pool_scale, v_norm_lru_g, v_norm_pool_g, v_norm_ffn_g, v_final_norm_g))
    for nm, outs in small_res.items():
        res[nm] = [o.reshape(D_MODEL) for o in outs] if nm == "final_norm_g" else list(outs)

    out = [loss, grad_x[None]]
    for kind in range(4):
        out += [res[nm][kind] for nm in WEIGHT_ORDER]
    return tuple(out)
```

```python
import functools

import jax
import jax.numpy as jnp
from jax import lax
from jax.experimental import pallas as pl
from jax.experimental.pallas import tpu as pltpu

F32 = jnp.float32
BF16 = jnp.bfloat16

D_MODEL = 1024
LRU_W = 512
POOL_W = 512
D_IN = 1536
D_FF = 2816
POOL_WINDOWS = (2, 4, 8, 16)
EPS = 1e-6
LRU_C = 8.0
N_DEV = 8
HALO = 16
SCAN_UNROLL = 8
ADAM_ROWS = 32
FF_CHUNKS = ((0, 1536), (1536, 2816))
RELAY_SPLIT_ROWS = 32

ADAM_LR = 0.001
ADAM_B1 = 0.9
ADAM_B2 = 0.999
ADAM_EPS = 1e-08
ADAM_WD = 0.01
ADAM_STEP = 10

ROW_CW, ROW_CB, ROW_BA, ROW_BX, ROW_LAM, ROW_PB, ROW_PS, ROW_GL, ROW_GP = 0, 4, 5, 6, 7, 8, 9, 10, 11
SG_VEC, SG_WA, SG_WX, SG_WP, SG_ROWS = 0, 32, 160, 288, 544

NT = (((1,), (1,)), ((), ()))
TN = (((0,), (0,)), ((), ()))


def _sds(shape, dtype):
    return jax.ShapeDtypeStruct(shape, dtype)


def _sigmoid(x):
    return 0.5 * jnp.tanh(0.5 * x) + 0.5


def _gelu_parts(x):
    c = 0.7978845608028654
    inner = c * (x + 0.044715 * (x * x * x))
    th = jnp.tanh(inner)
    g = 0.5 * x * (1.0 + th)
    dg = 0.5 * (1.0 + th) + 0.5 * x * (1.0 - th * th) * (c * (1.0 + 3.0 * 0.044715 * (x * x)))
    return g, dg


def _window_sum(ext, w, back):
    n = ext.shape[0]
    s, k = ext, 1
    while k < w:
        s = s + pltpu.roll(s, k if back else n - k, 0)
        k *= 2
    return s


def _rstd(x):
    return lax.rsqrt(jnp.mean(x * x, axis=-1, keepdims=True) + EPS)


def _rms_bwd(dy, xhat, rstd, gain):
    dxh = dy * gain
    dx = rstd * (dxh - xhat * jnp.mean(dxh * xhat, axis=-1, keepdims=True))
    return dx, jnp.sum(dy * xhat, axis=0, keepdims=True)


def _bd(xb, w_ref):
    return jnp.concatenate(
        [jnp.dot(xb[:, :256], w_ref[0], preferred_element_type=F32),
         jnp.dot(xb[:, 256:], w_ref[1], preferred_element_type=F32)], axis=1)


def _bd_t(xb, w_ref):
    return jnp.concatenate(
        [lax.dot_general(xb[:, :256], w_ref[0], NT, preferred_element_type=F32),
         lax.dot_general(xb[:, 256:], w_ref[1], NT, preferred_element_type=F32)], axis=1)


def _bd_grad(xb, db):
    return jnp.stack(
        [lax.dot_general(xb[:, :256], db[:, :256], TN, preferred_element_type=F32),
         lax.dot_general(xb[:, 256:], db[:, 256:], TN, preferred_element_type=F32)], axis=0)


def _fill_block_diag(dst, src_ref):
    n, k, _ = src_ref.shape
    dst[...] = jnp.zeros(dst.shape, BF16)
    for b in range(n):
        p, q = divmod(b, 256 // k)
        dst[p, q * k:(q + 1) * k, q * k:(q + 1) * k] = src_ref[b].astype(BF16)


def _diag_pack(w, k):
    lane = lax.broadcasted_iota(jnp.int32, (k, 256), 1)
    out = w[0:k]
    for q in range(1, 256 // k):
        out = jnp.where(lane >= q * k, w[q * k:(q + 1) * k], out)
    return out


def _y_pos(b):
    return 4 * (b % 2) + b // 2


def _softplus_neg_lambda(pv):
    z = -pv[ROW_LAM:ROW_LAM + 1, :]
    return jnp.maximum(z, 0.0) + jnp.log(1.0 + jnp.exp(-jnp.abs(z)))


def _lru_gates(e_lru, pv, wa_ref, wx_ref, tm):
    xc = pv[ROW_CB:ROW_CB + 1, :]
    for k in range(4):
        xc = xc + e_lru[pl.ds(HALO - 3 + k, tm), :] * pv[ROW_CW + k:ROW_CW + k + 1, :]
    xcb = xc.astype(BF16)
    r = _sigmoid(_bd(xcb, wa_ref) + pv[ROW_BA:ROW_BA + 1, :])
    ig = _sigmoid(_bd(xcb, wx_ref) + pv[ROW_BX:ROW_BX + 1, :])
    return xc, r, ig, (-LRU_C * r) * _softplus_neg_lambda(pv)


def _lru_decay(la):
    a = jnp.exp(la)
    om = -jnp.tanh(la) * (1.0 + a * a)
    omc = jnp.maximum(om, 1e-12)
    rmult = lax.rsqrt(omc)
    return a, om, omc * rmult, rmult


def _over_count(v, w, inv_head):
    return jnp.concatenate([v[0:HALO] * inv_head, v[HALO:] * (1.0 / w)], axis=0)


def _pool_pre(e_pool, pv, wp_ref, tm, t0):
    t_head = t0 + lax.broadcasted_iota(jnp.int32, (HALO, 1), 0)
    parts, inv_heads = [], []
    for g, w in enumerate(POOL_WINDOWS):
        ext = e_pool[:, pl.ds(128 * g, 128)]
        s = _window_sum(ext, w, back=True)[HALO:, :]
        inv_head = 1.0 / jnp.minimum(t_head + 1, w).astype(F32)
        inv_heads.append(inv_head)
        parts.append(_over_count(s, w, inv_head) - ext[HALO:, :])
    pooled = jnp.concatenate(parts, axis=1)
    pooled_b = pooled.astype(BF16)
    zp = _bd(pooled_b, wp_ref) + pv[ROW_PB:ROW_PB + 1, :]
    return pooled_b, zp, inv_heads


def _scan_tile(a_ref, b_ref, out_ref, carry, tm, reverse):
    row = lax.broadcasted_iota(jnp.int32, (8, LRU_W), 0)
    nblk = tm // 8

    def local_scan(blk):
        r0 = pl.multiple_of(blk * 8, 8)
        av = a_ref[pl.ds(r0, 8), :]
        bv = b_ref[pl.ds(r0, 8), :]
        for d in (1, 2, 4):
            sh = (8 - d) if reverse else d
            a_s = pltpu.roll(av, sh, 0)
            b_s = pltpu.roll(bv, sh, 0)
            m = (row < 8 - d) if reverse else (row >= d)
            bv = jnp.where(m, av * b_s + bv, bv)
            av = jnp.where(m, av * a_s, av)
        return r0, av, bv

    def step(i, hin):
        local = [local_scan((nblk - 1 - (i * SCAN_UNROLL + j)) if reverse else (i * SCAN_UNROLL + j))
                 for j in range(SCAN_UNROLL)]
        for r0, av, bv in local:
            hv = av * hin + bv
            out_ref[pl.ds(r0, 8), :] = hv
            hin = jnp.broadcast_to(hv[0:1, :] if reverse else hv[7:8, :], (8, LRU_W))
        return hin

    return lax.fori_loop(0, nblk // SCAN_UNROLL, step, carry)


MESH = pl.DeviceIdType.MESH
ANY = pl.BlockSpec(memory_space=pl.ANY)


def _place():
    x, y, c = lax.axis_index("x"), lax.axis_index("y"), lax.axis_index("c")
    chips = [(1 - x, y), (x, 1 - y), (1 - x, 1 - y)]
    return x, y, c, chips


class _Gather:
    def __init__(self, ins, outs, send_sems, recv_sems, local_sems, core_major=False):
        self.ins, self.outs, self.n = ins, outs, len(ins)
        self.send_sems, self.recv_sems, self.local_sems = send_sems, recv_sems, local_sems
        self.core_major = core_major

    @staticmethod
    def scratch(n):
        return [pltpu.SemaphoreType.DMA((8, n)), pltpu.SemaphoreType.DMA((8, n)), pltpu.SemaphoreType.DMA((n,))]

    def _slot(self, a, px, py, pc):
        return self.outs[a].at[4 * pc + 2 * px + py if self.core_major else 4 * px + 2 * py + pc]

    def _half(self, a, h):
        rows = self.ins[a].shape[0]
        if rows % RELAY_SPLIT_ROWS:
            return None if h else (0, rows)
        return (h * (rows // 2), rows // 2)

    def _copy(self, a, k, block, to, src=None, rows=None):
        dst = self._slot(a, *block)
        src = dst if src is None else src
        if rows is not None:
            src, dst = src.at[pl.ds(*rows)], dst.at[pl.ds(*rows)]
        return pltpu.make_async_remote_copy(
            src_ref=src, dst_ref=dst, send_sem=self.send_sems.at[k, a], recv_sem=self.recv_sems.at[k, a],
            device_id=to, device_id_type=MESH)

    def _mine(self, a):
        x, y, c, _ = _place()
        return pltpu.make_async_copy(self.ins[a], self._slot(a, x, y, c), self.local_sems.at[a])

    def _first(self, a):
        x, y, c, chips = _place()
        me = (x, y, c)
        return ([self._copy(a, 0, me, (x, y, 1 - c), src=self.ins[a])]
                + [self._copy(a, 1 + j, me, (*chip, c), src=self.ins[a]) for j, chip in enumerate(chips[:2])])

    def _passed_on(self, a, h):
        x, y, c, chips = _place()
        block = (*chips[h], c)
        out = [self._copy(a, 4 + h, block, (x, y, 1 - c))]
        if self._half(a, h) is not None:
            out.append(self._copy(a, (3, 7)[h], block, (*chips[1 - h], c), rows=self._half(a, h)))
        return out

    def start(self):
        for a in range(self.n):
            self._mine(a).start()
        for a in range(self.n):
            for cp in self._first(a):
                cp.start()

    def relay(self):
        x, y, c, chips = _place()
        for h in range(2):
            for a in range(self.n):
                self._copy(a, 1 + h, (*chips[h], c), (x, y, c)).wait_recv()
                for cp in self._passed_on(a, h):
                    cp.start()

    def finish(self):
        x, y, c, chips = _place()
        me, sibling = (x, y, c), (x, y, 1 - c)
        passed = []
        for a in range(self.n):
            for h in range(2):
                if self._half(a, h) is not None:
                    self._copy(a, (3, 7)[h], (*chips[2], c), me, rows=self._half(a, h)).wait_recv()
            fwd = self._copy(a, 6, (*chips[2], c), sibling)
            fwd.start()
            passed.append(fwd)
        for a in range(self.n):
            self._copy(a, 0, (x, y, 1 - c), me).wait_recv()
            for j, chip in enumerate(chips):
                self._copy(a, 4 + j, (*chip, 1 - c), me).wait_recv()
        for a in range(self.n):
            for cp in self._first(a) + self._passed_on(a, 0) + self._passed_on(a, 1):
                cp.wait_send()
        for cp in passed:
            cp.wait_send()
        for a in range(self.n):
            self._mine(a).wait()


def _half_exchange(arr, name):
    def body(in_ref, out_ref, send_sems, recv_sems, local_sem):
        x, y, c, _ = _place()
        my_chip = 2 * x + y

        def send(j, wait):
            to_me = (c == x) & (y == j // 2) & (c == j % 2)

            @pl.when(to_me)
            def _():
                local = pltpu.make_async_copy(in_ref.at[j], out_ref.at[my_chip], local_sem)
                local.wait() if wait else local.start()

            @pl.when(jnp.logical_not(to_me))
            def _():
                remote = pltpu.make_async_remote_copy(
                    src_ref=in_ref.at[j], dst_ref=out_ref.at[my_chip], send_sem=send_sems.at[j],
                    recv_sem=recv_sems.at[my_chip], device_id=(c, j // 2, j % 2), device_id_type=MESH)
                remote.wait_send() if wait else remote.start()

        for j in range(4):
            send(j, wait=False)
        for j in range(4):
            send(j, wait=True)
        for k in range(4):
            from_me = (k // 2 == x) & (k % 2 == y) & (c == x)

            @pl.when(jnp.logical_not(from_me))
            def _():
                pltpu.make_async_remote_copy(
                    src_ref=in_ref.at[0], dst_ref=out_ref.at[k], send_sem=send_sems.at[0], recv_sem=recv_sems.at[k],
                    device_id=(k // 2, k % 2, x), device_id_type=MESH).wait_recv()

    return pl.pallas_call(
        body, name=name, out_shape=_sds(arr.shape, arr.dtype), in_specs=[ANY], out_specs=ANY,
        scratch_shapes=[pltpu.SemaphoreType.DMA((4,)), pltpu.SemaphoreType.DMA((4,)), pltpu.SemaphoreType.DMA],
    )(arr)


class _ChipExchange:
    def __init__(self, ins, outs, send_sems, recv_sems, local_sems):
        self.ins, self.outs, self.n = ins, outs, len(ins)
        self.send_sems, self.recv_sems, self.local_sems = send_sems, recv_sems, local_sems

    @staticmethod
    def scratch(n):
        return [pltpu.SemaphoreType.DMA((3, n)), pltpu.SemaphoreType.DMA((3, n)), pltpu.SemaphoreType.DMA((n,))]

    def _local(self, a):
        x, y, _, _ = _place()
        me = 2 * x + y
        return pltpu.make_async_copy(self.ins[a].at[me], self.outs[a].at[me], self.local_sems.at[a])

    def _copies(self, a):
        x, y, c, chips = _place()
        me = 2 * x + y
        return [(pltpu.make_async_remote_copy(
                     src_ref=self.ins[a].at[2 * px + py], dst_ref=self.outs[a].at[me],
                     send_sem=self.send_sems.at[k, a], recv_sem=self.recv_sems.at[k, a],
                     device_id=(px, py, c), device_id_type=MESH),
                 pltpu.make_async_remote_copy(
                     src_ref=self.ins[a].at[me], dst_ref=self.outs[a].at[2 * px + py],
                     send_sem=self.send_sems.at[k, a], recv_sem=self.recv_sems.at[k, a],
                     device_id=(px, py, c), device_id_type=MESH))
                for k, (px, py) in enumerate(chips)]

    def start(self):
        for a in range(self.n):
            self._local(a).start()
        for a in range(self.n):
            for send, _ in self._copies(a):
                send.start()

    def finish(self):
        for a in range(self.n):
            for send, recv in self._copies(a):
                send.wait_send()
                recv.wait_recv()
        for a in range(self.n):
            self._local(a).wait()


def _gathering(body, n_steps, n_s, core_major):
    def wrapped(*refs, n_in, n_out):
        ins, sh_in = refs[:n_in], refs[n_in:n_in + n_s]
        outs, sh_out = refs[n_in + n_s:n_in + n_s + n_out], refs[n_in + n_s + n_out:n_in + 2 * n_s + n_out]
        rest = refs[n_in + 2 * n_s + n_out:]
        gather = _Gather(sh_in, sh_out, *rest[len(rest) - 3:], core_major=core_major)
        i = pl.program_id(0)

        @pl.when(i == 0)
        def _():
            gather.start()

        @pl.when(i == n_steps // 2)
        def _():
            gather.relay()

        body(*ins, *outs, *rest[:len(rest) - 3])

        @pl.when(i == n_steps - 1)
        def _():
            gather.finish()

    return wrapped


def _norm_in(x, g_mix, shards, tm):
    T = x.shape[0]
    n_t = T // tm
    n_s = len(shards)

    def norm(x_ref, g_ref, h_ref):
        xv = x_ref[...]
        h_ref[...] = (xv * _rstd(xv) * g_ref[...]).astype(BF16)

    outs = pl.pallas_call(
        functools.partial(_gathering(norm, n_t, n_s, core_major=False), n_in=2, n_out=1), name="norm_in", grid=(n_t,),
        in_specs=[pl.BlockSpec((tm, D_MODEL), lambda i: (i, 0)), pl.BlockSpec((1, D_MODEL), lambda i: (0, 0))]
        + [ANY] * n_s,
        out_specs=[pl.BlockSpec((tm, D_MODEL), lambda i: (i, 0))] + [ANY] * n_s,
        out_shape=[_sds((T, D_MODEL), BF16)] + [_sds((N_DEV,) + a.shape, a.dtype) for a in shards],
        scratch_shapes=_Gather.scratch(n_s),
        compiler_params=pltpu.CompilerParams(dimension_semantics=("arbitrary",)),
    )(x, g_mix, *shards)
    return outs[0], list(outs[1:])


def _mix_in(h1, w_in_t, tm, shards):
    T = h1.shape[0]
    n_t = T // tm
    n_s = len(shards)

    def project(h_ref, w_ref, u_ref):
        u_ref[...] = lax.dot_general(h_ref[...], w_ref[...], NT, preferred_element_type=F32)

    outs = pl.pallas_call(
        functools.partial(_gathering(project, n_t, n_s, core_major=True), n_in=2, n_out=1), name="mix_in", grid=(n_t,),
        in_specs=[pl.BlockSpec((tm, D_MODEL), lambda i: (i, 0)), pl.BlockSpec((D_IN, D_MODEL), lambda i: (0, 0))]
        + [ANY] * n_s,
        out_specs=[pl.BlockSpec((tm, D_IN), lambda i: (i, 0))] + [ANY] * n_s,
        out_shape=[_sds((T, D_IN), F32)] + [_sds((N_DEV,) + a.shape, a.dtype) for a in shards],
        scratch_shapes=_Gather.scratch(n_s),
        compiler_params=pltpu.CompilerParams(dimension_semantics=("arbitrary",)),
    )(h1, w_in_t, *shards)
    return outs[0], list(outs[1:])


def _mixer_fwd(u, x, pv, wa, wx, wp, w_out_b, g_ffn, tm, shards=()):
    T = u.shape[0]
    n_s = len(shards)
    n_t = T // tm

    def body(u_ref, x_ref, pv_ref, wa_in, wx_in, wp_in, wo_ref, gf_ref, *rest):
        sh_in, rest = rest[:n_s], rest[n_s:]
        y_ref, hs_ref, hres_ref, h2_ref, saved_ref = rest[:5]
        sh_out, rest = rest[5:5 + n_s], rest[5 + n_s:]
        e_lru, e_pool, a_s, b_s, hc, wa_ref, wx_ref, wp_ref = rest[:8]
        gather = _Gather(sh_in, sh_out, *rest[8:], core_major=True) if n_s else None
        i = pl.program_id(0)

        @pl.when(i == 0)
        def _():
            if gather:
                gather.start()
            e_lru[pl.ds(0, HALO), :] = jnp.zeros((HALO, LRU_W), F32)
            e_pool[pl.ds(0, HALO), :] = jnp.zeros((HALO, POOL_W), F32)
            hc[...] = jnp.zeros((8, LRU_W), F32)
            _fill_block_diag(wa_ref, wa_in)
            _fill_block_diag(wx_ref, wx_in)
            _fill_block_diag(wp_ref, wp_in)

        if gather:
            @pl.when(i == (2 * n_t) // 3)
            def _():
                gather.relay()

        e_lru[pl.ds(HALO, tm), :] = u_ref[:, 0:LRU_W]
        e_pool[pl.ds(HALO, tm), :] = u_ref[:, 2 * LRU_W:D_IN]
        pv = pv_ref[...]
        xc, r, ig, la = _lru_gates(e_lru, pv, wa_ref, wx_ref, tm)
        for q, val in enumerate((xc, r, ig, la)):
            saved_ref[:, LRU_W * q:LRU_W * (q + 1)] = val
        a, _, mult, _ = _lru_decay(la)
        a_s[...] = a
        b_s[...] = mult * (ig * xc)
        hc[...] = _scan_tile(a_s, b_s, hs_ref, hc[...], tm, reverse=False)
        gl, _ = _gelu_parts(u_ref[:, LRU_W:2 * LRU_W])
        y_lru = hs_ref[...] * gl
        _, zp, _ = _pool_pre(e_pool, pv, wp_ref, tm, i * tm)
        y_pool = zp * pv[ROW_PS:ROW_PS + 1, :]
        yn = jnp.concatenate([y_lru * _rstd(y_lru) * pv[ROW_GL:ROW_GL + 1, :],
                              y_pool * _rstd(y_pool) * pv[ROW_GP:ROW_GP + 1, :]], axis=1).astype(BF16)
        for b in range(N_DEV):
            y_ref[:, 128 * _y_pos(b):128 * (_y_pos(b) + 1)] = yn[:, 128 * b:128 * (b + 1)]
        hr = x_ref[...] + jnp.dot(y_ref[...], wo_ref[...], preferred_element_type=F32)
        hres_ref[...] = hr
        h2_ref[...] = (hr * _rstd(hr) * gf_ref[...]).astype(BF16)
        e_lru[pl.ds(0, HALO), :] = e_lru[pl.ds(tm, HALO), :]
        e_pool[pl.ds(0, HALO), :] = e_pool[pl.ds(tm, HALO), :]

        if gather:
            @pl.when(i == n_t - 1)
            def _():
                gather.finish()

    full = lambda shape: pl.BlockSpec(shape, lambda i: (0,) * len(shape))
    row = lambda w: pl.BlockSpec((tm, w), lambda i: (i, 0))
    outs = pl.pallas_call(
        body, name="mixer_fwd", grid=(n_t,),
        in_specs=[row(D_IN), row(D_MODEL), full((16, LRU_W)), full((8, 64, 64)), full((8, 64, 64)), full((4, 128, 128)),
                  full((D_MODEL, D_MODEL)), full((1, D_MODEL))] + [ANY] * n_s,
        out_specs=[row(D_MODEL), row(LRU_W), row(D_MODEL), row(D_MODEL), row(4 * LRU_W)] + [ANY] * n_s,
        out_shape=[_sds((T, D_MODEL), BF16), _sds((T, LRU_W), F32), _sds((T, D_MODEL), F32), _sds((T, D_MODEL), BF16),
                   _sds((T, 4 * LRU_W), F32)] + [_sds((N_DEV,) + a.shape, a.dtype) for a in shards],
        scratch_shapes=[pltpu.VMEM((HALO + tm, LRU_W), F32), pltpu.VMEM((HALO + tm, POOL_W), F32),
                        pltpu.VMEM((tm, LRU_W), F32), pltpu.VMEM((tm, LRU_W), F32), pltpu.VMEM((8, LRU_W), F32)]
        + [pltpu.VMEM((2, 256, 256), BF16)] * 3 + (_Gather.scratch(n_s) if n_s else []),
        compiler_params=pltpu.CompilerParams(dimension_semantics=("arbitrary",)),
    )(u, x, pv, wa, wx, wp, w_out_b, g_ffn, *shards)
    return outs[0], outs[1], outs[2], outs[3], outs[4], list(outs[5:])


def _ffn_fwd(hres, h2, w1_b, w3_b, w2_b, g_fin, tgt, tm):
    T = hres.shape[0]

    def body(hres_ref, h2_ref, w1_ref, w3_ref, w2_ref, gfin_ref, tgt_ref,
             g_ref, v_ref, ff_ref, d3_ref, loss_ref, dgfin_ref):
        @pl.when(pl.program_id(0) == 0)
        def _():
            loss_ref[...] = jnp.zeros((8, 128), F32)
            dgfin_ref[...] = jnp.zeros((1, D_MODEL), F32)

        h2 = h2_ref[...]
        h3 = hres_ref[...]
        for lo, hi in FF_CHUNKS:
            g = lax.dot_general(h2, w1_ref[lo:hi, :], NT, preferred_element_type=F32)
            v = lax.dot_general(h2, w3_ref[lo:hi, :], NT, preferred_element_type=F32)
            g_ref[:, lo:hi] = g.astype(BF16)
            v_ref[:, lo:hi] = v.astype(BF16)
            ff = ((g * _sigmoid(g)) * v).astype(BF16)
            ff_ref[:, lo:hi] = ff
            h3 = h3 + jnp.dot(ff, w2_ref[lo:hi, :], preferred_element_type=F32)

        rstd = _rstd(h3)
        xh = h3 * rstd
        gfin = gfin_ref[...]
        err = xh * gfin - tgt_ref[...]
        loss_ref[...] += 0.5 * jnp.sum(jnp.mean(err * err, axis=-1, keepdims=True))
        dx, dgain = _rms_bwd(err * (1.0 / D_MODEL), xh, rstd, gfin)
        d3_ref[...] = dx
        dgfin_ref[...] += dgain

    row = lambda w: pl.BlockSpec((tm, w), lambda i: (i, 0))
    const = lambda shape: pl.BlockSpec(shape, lambda i: (0,) * len(shape))
    weight = pl.BlockSpec((D_FF, D_MODEL), lambda i: (0, 0), pipeline_mode=pl.Buffered(1))
    return pl.pallas_call(
        body, name="ffn_fwd", grid=(T // tm,),
        in_specs=[row(D_MODEL), row(D_MODEL), weight, weight, weight, const((1, D_MODEL)), row(D_MODEL)],
        out_specs=[row(D_FF), row(D_FF), row(D_FF), row(D_MODEL), const((8, 128)), const((1, D_MODEL))],
        out_shape=[_sds((T, D_FF), BF16), _sds((T, D_FF), BF16), _sds((T, D_FF), BF16),
                   _sds((T, D_MODEL), F32), _sds((8, 128), F32), _sds((1, D_MODEL), F32)],
        compiler_params=pltpu.CompilerParams(dimension_semantics=("arbitrary",)),
    )(hres, h2, w1_b, w3_b, w2_b, g_fin, tgt)


def _ffn_bwd(d3, g, v, w1_b, w3_b, w2_b, hres, g_ffn, tm):
    T = d3.shape[0]

    def body(d3_ref, g_ref, v_ref, w1_ref, w3_ref, w2_ref, hres_ref, gf_ref, dg_ref, dv_ref, d2_ref, dgffn_ref):
        @pl.when(pl.program_id(0) == 0)
        def _():
            dgffn_ref[...] = jnp.zeros((1, D_MODEL), F32)

        d3 = d3_ref[...]
        d3b = d3.astype(BF16)
        dh2 = jnp.zeros((tm, D_MODEL), F32)
        for lo, hi in FF_CHUNKS:
            dff = lax.dot_general(d3b, w2_ref[lo:hi, :], NT, preferred_element_type=F32)
            gv = g_ref[:, lo:hi].astype(F32)
            vv = v_ref[:, lo:hi].astype(F32)
            sg = _sigmoid(gv)
            sl = gv * sg
            dgb = (dff * vv * (sg * (1.0 + gv * (1.0 - sg)))).astype(BF16)
            dvb = (dff * sl).astype(BF16)
            dg_ref[:, lo:hi] = dgb
            dv_ref[:, lo:hi] = dvb
            dh2 = dh2 + (jnp.dot(dgb, w1_ref[lo:hi, :], preferred_element_type=F32)
                         + jnp.dot(dvb, w3_ref[lo:hi, :], preferred_element_type=F32))

        hr = hres_ref[...]
        rstd = _rstd(hr)
        dx, dgain = _rms_bwd(dh2, hr * rstd, rstd, gf_ref[...])
        d2_ref[...] = d3 + dx
        dgffn_ref[...] += dgain

    row = lambda w: pl.BlockSpec((tm, w), lambda i: (i, 0))
    const = lambda shape: pl.BlockSpec(shape, lambda i: (0,) * len(shape))
    weight = pl.BlockSpec((D_FF, D_MODEL), lambda i: (0, 0), pipeline_mode=pl.Buffered(1))
    return pl.pallas_call(
        body, name="ffn_bwd", grid=(T // tm,),
        in_specs=[row(D_MODEL), row(D_FF), row(D_FF), weight, weight, weight, row(D_MODEL), const((1, D_MODEL))],
        out_specs=[row(D_FF), row(D_FF), row(D_MODEL), const((1, D_MODEL))],
        out_shape=[_sds((T, D_FF), BF16), _sds((T, D_FF), BF16), _sds((T, D_MODEL), F32), _sds((1, D_MODEL), F32)],
        compiler_params=pltpu.CompilerParams(dimension_semantics=("arbitrary",)),
    )(d3, g, v, w1_b, w3_b, w2_b, hres, g_ffn)


def _at_b_pair(a, b, c_arr, name, tk, gather=()):
    T, M = a.shape
    N = b.shape[1]
    hm, n_k = M // 2, T // tk
    n_g = len(gather)

    def body(c_ref, a_ref, b_ref, *rest):
        g_in, o_ref, rest = rest[:n_g], rest[n_g], rest[n_g + 1:]
        g_out, rest = rest[:n_g], rest[n_g:]
        acc, landed, send_sem, recv_sem = rest[:4]
        ag = _Gather(g_in, g_out, *rest[4:]) if n_g else None
        ph, k = pl.program_id(0), pl.program_id(1)

        def hand_over():
            x, y, c, _ = _place()
            return pltpu.make_async_remote_copy(
                src_ref=acc.at[0], dst_ref=landed, send_sem=send_sem, recv_sem=recv_sem,
                device_id=(x, y, 1 - c), device_id_type=MESH)

        if ag:
            @pl.when((ph == 0) & (k == 0))
            def _():
                ag.start()

            @pl.when((ph == 1) & (k == 0))
            def _():
                ag.relay()

        @pl.when(k == 0)
        def _():
            acc[ph] = jnp.zeros((hm, N), F32)

        acc[ph] += lax.dot_general(a_ref[...].astype(BF16), b_ref[...].astype(BF16), TN, preferred_element_type=F32)

        @pl.when((ph == 0) & (k == n_k - 1))
        def _():
            hand_over().start()

        @pl.when((ph == 1) & (k == n_k - 1))
        def _():
            copy = hand_over()
            copy.wait_recv()
            o_ref[...] = (acc[1] + landed[...]).astype(BF16)
            copy.wait_send()
            if ag:
                ag.finish()

    outs = pl.pallas_call(
        body, name=name,
        grid_spec=pltpu.PrefetchScalarGridSpec(
            num_scalar_prefetch=1, grid=(2, n_k),
            in_specs=[pl.BlockSpec((tk, hm), lambda ph, k, c_ref: (k, (ph + 1 - c_ref[0]) % 2)),
                      pl.BlockSpec((tk, N), lambda ph, k, c_ref: (k, 0))] + [ANY] * n_g,
            out_specs=[pl.BlockSpec((hm, N), lambda ph, k, c_ref: (0, 0))] + [ANY] * n_g,
            scratch_shapes=[pltpu.VMEM((2, hm, N), F32), pltpu.VMEM((hm, N), F32),
                            pltpu.SemaphoreType.DMA, pltpu.SemaphoreType.DMA] + (_Gather.scratch(n_g) if n_g else [])),
        out_shape=[_sds((hm, N), BF16)] + [_sds((N_DEV,) + g.shape, g.dtype) for g in gather],
        compiler_params=pltpu.CompilerParams(dimension_semantics=("arbitrary", "arbitrary")),
    )(c_arr, a, b, *gather)
    return (outs[0], list(outs[1:])) if n_g else outs[0]


def _mixer_bwd(d2, u, hs, saved, pv, wa, wx, wp, w_out_b, tm, chip_sums=()):
    T = u.shape[0]
    n_t = T // tm
    n_x = len(chip_sums)

    def body(d2_ref, u_ref, uh_ref, hs_ref, hh_ref, saved_ref, pv_ref, wa_in, wx_in, wp_in, wo_ref, *rest):
        x_in, rest = rest[:n_x], rest[n_x:]
        du_ref, sg_ref = rest[:2]
        x_out, rest = rest[2:2 + n_x], rest[2 + n_x:]
        e_pool, e_h, a_s, b_s, dh_s, mu_s, f_x, f_p, mc, cx, cp = rest[:11]
        wa_ref, wx_ref, wp_ref, vacc_ref, dwa_ref, dwx_ref, dwp_ref = rest[11:18]
        exchange = _ChipExchange(x_in, x_out, *rest[18:]) if n_x else None
        s = pl.program_id(0)
        it = n_t - 1 - s

        @pl.when(s == 0)
        def _():
            if exchange:
                exchange.start()
            mc[...] = jnp.zeros((8, LRU_W), F32)
            cx[...] = jnp.zeros((8, LRU_W), F32)
            cp[...] = jnp.zeros((HALO, POOL_W), F32)
            vacc_ref[...] = jnp.zeros((16, LRU_W), F32)
            dwa_ref[...] = jnp.zeros((2, 256, 256), F32)
            dwx_ref[...] = jnp.zeros((2, 256, 256), F32)
            dwp_ref[...] = jnp.zeros((2, 256, 256), F32)
            _fill_block_diag(wa_ref, wa_in)
            _fill_block_diag(wx_ref, wx_in)
            _fill_block_diag(wp_ref, wp_in)

        first = it == 0
        e_pool[pl.ds(0, HALO), :] = jnp.where(first, 0.0, uh_ref[...])
        e_pool[pl.ds(HALO, tm), :] = u_ref[:, 2 * LRU_W:D_IN]
        e_h[pl.ds(0, 8), :] = jnp.where(first, 0.0, hh_ref[...])
        e_h[pl.ds(8, tm), :] = hs_ref[...]
        pv = pv_ref[...]
        saved = lambda q: saved_ref[:, LRU_W * q:LRU_W * (q + 1)]

        dyn = lax.dot_general(d2_ref[...].astype(BF16), wo_ref[...], NT, preferred_element_type=F32)
        dyn = jnp.concatenate([dyn[:, 128 * _y_pos(b):128 * (_y_pos(b) + 1)] for b in range(N_DEV)], axis=1)

        h = hs_ref[...]
        ug = u_ref[:, LRU_W:2 * LRU_W]
        gl, dgl = _gelu_parts(ug)
        y_lru = h * gl
        rstd_l = _rstd(y_lru)
        dy_lru, d_gain_l = _rms_bwd(dyn[:, 0:LRU_W], y_lru * rstd_l, rstd_l, pv[ROW_GL:ROW_GL + 1, :])
        dh = dy_lru * gl
        du_ref[:, LRU_W:2 * LRU_W] = (dy_lru * h * dgl).astype(BF16)
        a_s[...] = jnp.exp(saved(3))
        b_s[...] = a_s[...] * dh
        dh_s[...] = dh
        mu_s[pl.ds(tm, 8), :] = mc[...]
        mc[...] = _scan_tile(a_s, b_s, mu_s, mc[...], tm, reverse=True)
        xc, r, ig = saved(0), saved(1), saved(2)
        a, om, mult, rmult = _lru_decay(saved(3))
        lam_t = dh_s[...] + mu_s[pl.ds(1, tm), :]
        da = lam_t * e_h[pl.ds(7, tm), :]
        dmult = lam_t * (ig * xc)
        di = lam_t * (mult * xc)
        dxc = lam_t * (mult * ig)
        dla = da * a - jnp.where(om > 1e-12, dmult * ((a * a) * rmult), 0.0)
        dra = (dla * (-LRU_C * _softplus_neg_lambda(pv))) * (r * (1.0 - r))
        dia = di * (ig * (1.0 - ig))
        drab = dra.astype(BF16)
        diab = dia.astype(BF16)
        xcb = xc.astype(BF16)
        dxc = dxc + _bd_t(drab, wa_ref) + _bd_t(diab, wx_ref)
        dwa_ref[...] += _bd_grad(xcb, drab)
        dwx_ref[...] += _bd_grad(xcb, diab)
        sig_neg_lam = _sigmoid(-pv[ROW_LAM:ROW_LAM + 1, :])
        d_lam = jnp.sum(dla * r, axis=0, keepdims=True) * (LRU_C * sig_neg_lam)

        f_x[pl.ds(0, tm), :] = dxc
        f_x[pl.ds(tm, 8), :] = cx[...]
        du_lru = jnp.zeros((tm, LRU_W), F32)
        u_lru = u_ref[:, 0:LRU_W]
        d_cw = []
        for k in range(4):
            later = f_x[pl.ds(3 - k, tm), :]
            du_lru = du_lru + later * pv[ROW_CW + k:ROW_CW + k + 1, :]
            d_cw.append(jnp.sum(later * u_lru, axis=0, keepdims=True))
        du_ref[:, 0:LRU_W] = du_lru.astype(BF16)
        cx[...] = f_x[pl.ds(0, 8), :]

        pooled_b, zp, inv_cnts = _pool_pre(e_pool, pv, wp_ref, tm, it * tm)
        ps = pv[ROW_PS:ROW_PS + 1, :]
        y_pool = zp * ps
        rstd_p = _rstd(y_pool)
        dy_pool, d_gain_p = _rms_bwd(dyn[:, LRU_W:D_MODEL], y_pool * rstd_p, rstd_p, pv[ROW_GP:ROW_GP + 1, :])
        dz = dy_pool * ps
        dzb = dz.astype(BF16)
        dwp_ref[...] += _bd_grad(pooled_b, dzb)
        dpooled = _bd_t(dzb, wp_ref)
        for g, w in enumerate(POOL_WINDOWS):
            f_p[pl.ds(0, tm), pl.ds(128 * g, 128)] = _over_count(dpooled[:, 128 * g:128 * (g + 1)], w, inv_cnts[g])
        f_p[pl.ds(tm, HALO), :] = cp[...]
        for g, w in enumerate(POOL_WINDOWS):
            acc = _window_sum(f_p[:, pl.ds(128 * g, 128)], w, back=False)[0:tm, :]
            du_ref[:, 2 * LRU_W + 128 * g:2 * LRU_W + 128 * (g + 1)] = (
                acc - dpooled[:, 128 * g:128 * (g + 1)]).astype(BF16)
        cp[...] = f_p[pl.ds(0, HALO), :]

        rows = d_cw + [
            jnp.sum(dxc, axis=0, keepdims=True),
            jnp.sum(dra, axis=0, keepdims=True),
            jnp.sum(dia, axis=0, keepdims=True),
            d_lam,
            jnp.sum(dz, axis=0, keepdims=True),
            jnp.sum(dy_pool * zp, axis=0, keepdims=True),
            d_gain_l, d_gain_p,
            jnp.zeros((4, LRU_W), F32),
        ]
        vacc_ref[...] += jnp.concatenate(rows, axis=0)

        @pl.when(s == n_t - 1)
        def _():
            sg_ref[SG_VEC:SG_VEC + 16, :] = vacc_ref[:, 0:256]
            sg_ref[SG_VEC + 16:SG_VEC + 32, :] = vacc_ref[:, 256:512]
            for half in range(2):
                sg_ref[SG_WA + 64 * half:SG_WA + 64 * (half + 1), :] = _diag_pack(dwa_ref[half], 64)
                sg_ref[SG_WX + 64 * half:SG_WX + 64 * (half + 1), :] = _diag_pack(dwx_ref[half], 64)
                sg_ref[SG_WP + 128 * half:SG_WP + 128 * (half + 1), :] = _diag_pack(dwp_ref[half], 128)
            if exchange:
                exchange.finish()

    rev = lambda w: pl.BlockSpec((tm, w), lambda s: (n_t - 1 - s, 0))
    full = lambda shape: pl.BlockSpec(shape, lambda s: (0,) * len(shape))
    outs = pl.pallas_call(
        body, name="mixer_bwd", grid=(n_t,),
        in_specs=[rev(D_MODEL), rev(D_IN),
                  pl.BlockSpec((HALO, POOL_W), lambda s: (jnp.maximum((n_t - 1 - s) * (tm // HALO) - 1, 0), 2)),
                  rev(LRU_W),
                  pl.BlockSpec((8, LRU_W), lambda s: (jnp.maximum((n_t - 1 - s) * (tm // 8) - 1, 0), 0)),
                  rev(4 * LRU_W), full((16, LRU_W)), full((8, 64, 64)), full((8, 64, 64)), full((4, 128, 128)),
                  full((D_MODEL, D_MODEL))] + [ANY] * n_x,
        out_specs=[rev(D_IN), full((SG_ROWS, 256))] + [ANY] * n_x,
        out_shape=[_sds((T, D_IN), BF16), _sds((SG_ROWS, 256), F32)] + [_sds(a.shape, a.dtype) for a in chip_sums],
        scratch_shapes=[pltpu.VMEM((HALO + tm, POOL_W), F32),
                        pltpu.VMEM((8 + tm, LRU_W), F32)] + [pltpu.VMEM((tm, LRU_W), F32)] * 3 + [
                        pltpu.VMEM((tm + 8, LRU_W), F32), pltpu.VMEM((tm + 8, LRU_W), F32),
                        pltpu.VMEM((tm + HALO, POOL_W), F32), pltpu.VMEM((8, LRU_W), F32),
                        pltpu.VMEM((8, LRU_W), F32), pltpu.VMEM((HALO, POOL_W), F32)]
        + [pltpu.VMEM((2, 256, 256), BF16)] * 3 + [pltpu.VMEM((16, LRU_W), F32)] + [pltpu.VMEM((2, 256, 256), F32)] * 3
        + (_ChipExchange.scratch(n_x) if n_x else []),
        compiler_params=pltpu.CompilerParams(dimension_semantics=("arbitrary",)),
    )(d2, u, u, hs, hs, saved, pv, wa, wx, wp, w_out_b, *chip_sums)
    return outs[0], outs[1], list(outs[2:])


def _mix_in_bwd(du, x, d2, w_in_t, g_mix, tm):
    T = x.shape[0]

    def body(du_ref, x_ref, d2_ref, w_ref, g_ref, dx_ref, dg_ref):
        @pl.when(pl.program_id(0) == 0)
        def _():
            dg_ref[...] = jnp.zeros((1, D_MODEL), F32)

        dh = jnp.dot(du_ref[...], w_ref[...], preferred_element_type=F32)
        xv = x_ref[...]
        rstd = _rstd(xv)
        dx, dgain = _rms_bwd(dh, xv * rstd, rstd, g_ref[...])
        dx_ref[...] = d2_ref[...] + dx
        dg_ref[...] += dgain

    row = lambda w: pl.BlockSpec((tm, w), lambda i: (i, 0))
    const = lambda shape: pl.BlockSpec(shape, lambda i: (0,) * len(shape))
    return pl.pallas_call(
        body, name="mix_in_bwd", grid=(T // tm,),
        in_specs=[row(D_IN), row(D_MODEL), row(D_MODEL), const((D_IN, D_MODEL)), const((1, D_MODEL))],
        out_specs=[row(D_MODEL), const((1, D_MODEL))],
        out_shape=[_sds((T, D_MODEL), F32), _sds((1, D_MODEL), F32)],
        compiler_params=pltpu.CompilerParams(dimension_semantics=("arbitrary",)),
    )(du, x, d2, w_in_t, g_mix)


def _adamw(w, g, m, v):
    m = ADAM_B1 * m + (1.0 - ADAM_B1) * g
    v = ADAM_B2 * v + (1.0 - ADAM_B2) * (g * g)
    m_hat = m / (1.0 - ADAM_B1 ** ADAM_STEP)
    v_hat = v / (1.0 - ADAM_B2 ** ADAM_STEP)
    delta = -ADAM_LR * (m_hat / (jnp.sqrt(v_hat) + ADAM_EPS) + ADAM_WD * w)
    return delta, m, v


def _adam_shards(ws, ms, vs, parts):
    n = len(ws)
    n_blk = [w.shape[0] // ADAM_ROWS for w in ws]

    def body(*refs):
        w_refs, m_refs, v_refs, p_refs, outs = (refs[:n], refs[n:2 * n], refs[2 * n:3 * n], refs[3 * n:4 * n],
                                                refs[4 * n:])
        i = pl.program_id(0)
        for a in range(n):
            @pl.when(i < n_blk[a])
            def _(a=a):
                g = p_refs[a][0].astype(F32)
                for j in range(1, 4):
                    g = g + p_refs[a][j].astype(F32)
                delta, new_m, new_v = _adamw(w_refs[a][...], g, m_refs[a][...], v_refs[a][...])
                for kind, val in enumerate((g, delta, new_m, new_v)):
                    outs[4 * a + kind][...] = val

    blk = lambda a: pl.BlockSpec((ADAM_ROWS, D_MODEL), lambda i: (jnp.minimum(i, n_blk[a] - 1), 0))
    part_blk = lambda a: pl.BlockSpec((4, ADAM_ROWS, D_MODEL), lambda i: (0, jnp.minimum(i, n_blk[a] - 1), 0))
    res = pl.pallas_call(
        body, name="adam_shards", grid=(max(n_blk),),
        in_specs=[blk(a) for a in range(n)] * 3 + [part_blk(a) for a in range(n)],
        out_specs=[blk(a) for a in range(n) for _ in range(4)],
        out_shape=[_sds(w.shape, F32) for w in ws for _ in range(4)],
        compiler_params=pltpu.CompilerParams(dimension_semantics=("arbitrary",)),
    )(*ws, *ms, *vs, *parts)
    return [tuple(res[4 * a:4 * a + 4]) for a in range(n)]


SMALL_PARAMS = [("norm_mix_g", (1, D_MODEL)), ("conv_w", (1, 4, 64)), ("conv_b", (1, LRU_W)),
                ("gate_a_w", (1, 8, 64, 64)), ("gate_a_b", (1, LRU_W)), ("gate_x_w", (1, 8, 64, 64)),
                ("gate_x_b", (1, LRU_W)), ("lru_lambda", (1, LRU_W)), ("pool_w", (1, 4, 128, 128)),
                ("pool_b", (1, POOL_W)), ("pool_scale", (1, POOL_W)), ("norm_lru_g", (1, LRU_W)),
                ("norm_pool_g", (1, POOL_W)), ("norm_ffn_g", (1, D_MODEL)), ("final_norm_g", (1, D_MODEL))]
VEC_ROW = dict(conv_b=ROW_CB, gate_a_b=ROW_BA, gate_x_b=ROW_BX, lru_lambda=ROW_LAM, pool_b=ROW_PB, pool_scale=ROW_PS,
               norm_lru_g=ROW_GL, norm_pool_g=ROW_GP)
WHOLE = (Ellipsis,)


def _unpack_mixer_grads(sg, dev):
    vec = jnp.concatenate([sg[SG_VEC:SG_VEC + 16], sg[SG_VEC + 16:SG_VEC + 32]], axis=1)
    out = {nm: [(WHOLE, vec[r:r + 1])] for nm, r in VEC_ROW.items()}
    own = jnp.zeros((4, 64), F32)
    for d in range(N_DEV):
        own = jnp.where(dev == d, vec[ROW_CW:ROW_CW + 4, 64 * d:64 * (d + 1)], own)
    out["conv_w"] = [((0,), own)]
    for nm, row0 in (("gate_a_w", SG_WA), ("gate_x_w", SG_WX)):
        out[nm] = [((0, b), sg[row0 + 64 * (b // 4):row0 + 64 * (b // 4 + 1), 64 * (b % 4):64 * (b % 4 + 1)])
                   for b in range(8)]
    out["pool_w"] = [((0, b), sg[SG_WP + 128 * (b // 2):SG_WP + 128 * (b // 2 + 1), 128 * (b % 2):128 * (b % 2 + 1)])
                     for b in range(4)]
    return out


def _adam_small(parts, w, m, v):
    names = [nm for nm, _ in SMALL_PARAMS]
    n = len(names)

    def body(sg_ref, gm_ref, gf_ref, gn_ref, ls_ref, *rest):
        w_refs, m_refs, v_refs, outs = rest[:n], rest[n:2 * n], rest[2 * n:3 * n], rest[3 * n:]
        dev = 4 * lax.axis_index("x") + 2 * lax.axis_index("y") + lax.axis_index("c")

        def total(ref):
            acc = ref[0]
            for d in range(1, N_DEV):
                acc = acc + ref[d]
            return acc

        pieces = _unpack_mixer_grads(total(sg_ref), dev)
        pieces["norm_mix_g"] = [(WHOLE, total(gm_ref))]
        pieces["norm_ffn_g"] = [(WHOLE, total(gf_ref))]
        pieces["final_norm_g"] = [(WHOLE, total(gn_ref))]
        for i, nm in enumerate(names):
            for idx, g in pieces[nm]:
                delta, new_m, new_v = _adamw(w_refs[i][idx], g, m_refs[i][idx], v_refs[i][idx])
                for kind, val in enumerate((g, delta, new_m, new_v)):
                    outs[4 * i + kind][idx] = val
        outs[4 * n][...] = total(ls_ref)

    shapes = [_sds(shape, F32) for _, shape in SMALL_PARAMS for _ in range(4)] + [_sds((8, 128), F32)]
    res = pl.pallas_call(body, name="adam_small", out_shape=shapes)(
        *parts, *[w[nm] for nm in names], *[m[nm] for nm in names], *[v[nm] for nm in names])
    return {nm: tuple(res[4 * i:4 * i + 4]) for i, nm in enumerate(names)}, res[4 * n][0, 0]


def _vec_rows(conv_w_full, conv_b, ba, bx, lam, pb, ps, gl, gp):
    return jnp.concatenate([conv_w_full, conv_b, ba, bx, lam, pb, ps, gl, gp, jnp.zeros((4, LRU_W), F32)], axis=0)


WEIGHT_ORDER = ['norm_mix_g', 'w_in', 'conv_w', 'conv_b', 'gate_a_w', 'gate_a_b', 'gate_x_w', 'gate_x_b', 'lru_lambda',
                'pool_w', 'pool_b', 'pool_scale', 'norm_lru_g', 'norm_pool_g', 'w_out', 'norm_ffn_g', 'ffn_w1', 'ffn_w3',
                'ffn_w2', 'final_norm_g']


def kernel(x, norm_mix_g, w_in, conv_w, conv_b, gate_a_w, gate_a_b, gate_x_w, gate_x_b, lru_lambda, pool_w, pool_b, pool_scale, norm_lru_g, norm_pool_g, w_out, norm_ffn_g, ffn_w1, ffn_w3, ffn_w2, final_norm_g, loss_target, m_norm_mix_g, m_w_in, m_conv_w, m_conv_b, m_gate_a_w, m_gate_a_b, m_gate_x_w, m_gate_x_b, m_lru_lambda, m_pool_w, m_pool_b, m_pool_scale, m_norm_lru_g, m_norm_pool_g, m_w_out, m_norm_ffn_g, m_ffn_w1, m_ffn_w3, m_ffn_w2, m_final_norm_g, v_norm_mix_g, v_w_in, v_conv_w, v_conv_b, v_gate_a_w, v_gate_a_b, v_gate_x_w, v_gate_x_b, v_lru_lambda, v_pool_w, v_pool_b, v_pool_scale, v_norm_lru_g, v_norm_pool_g, v_w_out, v_norm_ffn_g, v_ffn_w1, v_ffn_w3, v_ffn_w2, v_final_norm_g):
    ac = lax.axis_index("c")
    tm, tmx, tk = 512, 512, 1024
    tm_in = 1024
    xs, tgt = x[0], loss_target[0]
    g_fin = final_norm_g.reshape(1, D_MODEL)
    c_arr = jnp.reshape(ac, (1,)).astype(jnp.int32)

    tr = lambda w: jnp.swapaxes(w[0], 0, 1)
    own = lambda w: w[0]
    bf = lambda a: a.astype(BF16)

    h1, (g_in, g_conv) = _norm_in(xs, norm_mix_g, [bf(tr(w_in)), conv_w[0]], tm_in)
    w_in_t = g_in.reshape(D_IN, D_MODEL)
    u, (g_out,) = _mix_in(h1, w_in_t, tm, shards=[bf(own(w_out))])
    conv_w_full = g_conv.transpose(1, 0, 2).reshape(4, LRU_W)
    pv = _vec_rows(conv_w_full, conv_b, gate_a_b, gate_x_b, lru_lambda, pool_b, pool_scale, norm_lru_g, norm_pool_g)
    wa, wx, wp = gate_a_w[0], gate_x_w[0], pool_w[0]
    w_out_b = g_out.reshape(D_MODEL, D_MODEL)
    y, hs, hres, h2, saved, (g_w1, g_w3, g_w2) = _mixer_fwd(
        u, xs, pv, wa, wx, wp, w_out_b, norm_ffn_g, tmx, shards=[bf(tr(ffn_w1)), bf(tr(ffn_w3)), bf(own(ffn_w2))])
    w1_t, w3_t, w2_b = g_w1.reshape(D_FF, D_MODEL), g_w3.reshape(D_FF, D_MODEL), g_w2.reshape(D_FF, D_MODEL)
    g, v, ff, d3, loss_acc, d_gfin = _ffn_fwd(hres, h2, w1_t, w3_t, w2_b, g_fin, tgt, tm)

    dg, dv, d2, d_gffn = _ffn_bwd(d3, g, v, w1_t, w3_t, w2_b, hres, norm_ffn_g, tm // 2)
    chips = lambda a: a.reshape(4, a.shape[0] // 4, a.shape[1])
    early_sums = [chips(_at_b_pair(y, d2, c_arr, "grad_w_out", 2 * tk)), chips(_at_b_pair(dg, h2, c_arr, "grad_w1", tk)),
                  chips(_at_b_pair(dv, h2, c_arr, "grad_w3", tk)), chips(_at_b_pair(ff, d3, c_arr, "grad_w2", tk))]
    du, d_mixer, early_parts = _mixer_bwd(d2, u, hs, saved, pv, wa, wx, wp, w_out_b, tmx, chip_sums=early_sums)
    grad_x, d_gmix = _mix_in_bwd(du, xs, d2, w_in_t, norm_mix_g, tm_in)
    d_win, small_parts = _at_b_pair(du, h1, c_arr, "grad_w_in", 2 * tk,
                                    gather=[d_mixer, d_gmix, d_gffn, d_gfin, loss_acc])
    parts = [_half_exchange(chips(d_win), "grads_to_chips_w_in")] + list(early_parts)

    res = {}
    shard_w = dict(w_in=(w_in, m_w_in, v_w_in, tr), w_out=(w_out, m_w_out, v_w_out, own),
                   ffn_w1=(ffn_w1, m_ffn_w1, v_ffn_w1, tr), ffn_w3=(ffn_w3, m_ffn_w3, v_ffn_w3, tr),
                   ffn_w2=(ffn_w2, m_ffn_w2, v_ffn_w2, own))
    shard_res = _adam_shards([view(w) for w, _, _, view in shard_w.values()],
                             [view(m) for _, m, _, view in shard_w.values()],
                             [view(v) for _, _, v, view in shard_w.values()], parts)
    for (nm, (_, _, _, view)), outs in zip(shard_w.items(), shard_res):
        res[nm] = [(jnp.swapaxes(o, 0, 1) if view is tr else o)[None] for o in outs]

    row = lambda a: a.reshape(1, D_MODEL)
    small = lambda gm, cw, cb, wa_, ba, wx_, bx, lam, pw, pb, ps, gl, gp, gf, gn: dict(
        norm_mix_g=gm, conv_w=cw, conv_b=cb, gate_a_w=wa_, gate_a_b=ba, gate_x_w=wx_, gate_x_b=bx, lru_lambda=lam,
        pool_w=pw, pool_b=pb, pool_scale=ps, norm_lru_g=gl, norm_pool_g=gp, norm_ffn_g=gf, final_norm_g=row(gn))
    small_res, loss = _adam_small(
        small_parts,
        small(norm_mix_g, conv_w, conv_b, gate_a_w, gate_a_b, gate_x_w, gate_x_b, lru_lambda, pool_w, pool_b,
              pool_scale, norm_lru_g, norm_pool_g, norm_ffn_g, final_norm_g),
        small(m_norm_mix_g, m_conv_w, m_conv_b, m_gate_a_w, m_gate_a_b, m_gate_x_w, m_gate_x_b, m_lru_lambda, m_pool_w,
              m_pool_b, m_pool_scale, m_norm_lru_g, m_norm_pool_g, m_norm_ffn_g, m_final_norm_g),
        small(v_norm_mix_g, v_conv_w, v_conv_b, v_gate_a_w, v_gate_a_b, v_gate_x_w, v_gate_x_b, v_lru_lambda, v_pool_w,
              v_pool_b, v_pool_scale, v_norm_lru_g, v_norm_pool_g, v_norm_ffn_g, v_final_norm_g))
    for nm, outs in small_res.items():
        res[nm] = [o.reshape(D_MODEL) for o in outs] if nm == "final_norm_g" else list(outs)

    out = [loss, grad_x[None]]
    for kind in range(4):
        out += [res[nm][kind] for nm in WEIGHT_ORDER]
    return tuple(out)
```

```python
import functools

import jax
import jax.numpy as jnp
from jax import lax
from jax.experimental import pallas as pl
from jax.experimental.pallas import tpu as pltpu

F32 = jnp.float32
BF16 = jnp.bfloat16

D_MODEL = 1024
LRU_W = 512
POOL_W = 512
D_IN = 1536
D_FF = 2816
POOL_WINDOWS = (2, 4, 8, 16)
EPS = 1e-6
LRU_C = 8.0
N_DEV = 8
HALO = 16
SCAN_UNROLL = 8
ADAM_ROWS = 32
FF_CHUNKS = ((0, 1536), (1536, 2816))
RELAY_SPLIT_ROWS = 32

ADAM_LR = 0.001
ADAM_B1 = 0.9
ADAM_B2 = 0.999
ADAM_EPS = 1e-08
ADAM_WD = 0.01
ADAM_STEP = 10

ROW_CW, ROW_CB, ROW_BA, ROW_BX, ROW_LAM, ROW_PB, ROW_PS, ROW_GL, ROW_GP = 0, 4, 5, 6, 7, 8, 9, 10, 11
SG_VEC, SG_WA, SG_WX, SG_WP, SG_ROWS = 0, 32, 160, 288, 544

NT = (((1,), (1,)), ((), ()))
TN = (((0,), (0,)), ((), ()))


def _sds(shape, dtype):
    return jax.ShapeDtypeStruct(shape, dtype)


def _sigmoid(x):
    return 0.5 * jnp.tanh(0.5 * x) + 0.5


def _gelu_parts(x):
    c = 0.7978845608028654
    inner = c * (x + 0.044715 * (x * x * x))
    th = jnp.tanh(inner)
    g = 0.5 * x * (1.0 + th)
    dg = 0.5 * (1.0 + th) + 0.5 * x * (1.0 - th * th) * (c * (1.0 + 3.0 * 0.044715 * (x * x)))
    return g, dg


def _window_sum(ext, w, back):
    n = ext.shape[0]
    s, k = ext, 1
    while k < w:
        s = s + pltpu.roll(s, k if back else n - k, 0)
        k *= 2
    return s


def _rstd(x):
    return lax.rsqrt(jnp.mean(x * x, axis=-1, keepdims=True) + EPS)


def _rms_bwd(dy, xhat, rstd, gain):
    dxh = dy * gain
    dx = rstd * (dxh - xhat * jnp.mean(dxh * xhat, axis=-1, keepdims=True))
    return dx, jnp.sum(dy * xhat, axis=0, keepdims=True)


def _bd(xb, w_ref):
    return jnp.concatenate(
        [jnp.dot(xb[:, :256], w_ref[0], preferred_element_type=F32),
         jnp.dot(xb[:, 256:], w_ref[1], preferred_element_type=F32)], axis=1)


def _bd_t(xb, w_ref):
    return jnp.concatenate(
        [lax.dot_general(xb[:, :256], w_ref[0], NT, preferred_element_type=F32),
         lax.dot_general(xb[:, 256:], w_ref[1], NT, preferred_element_type=F32)], axis=1)


def _bd_grad(xb, db):
    return jnp.stack(
        [lax.dot_general(xb[:, :256], db[:, :256], TN, preferred_element_type=F32),
         lax.dot_general(xb[:, 256:], db[:, 256:], TN, preferred_element_type=F32)], axis=0)


def _fill_block_diag(dst, src_ref):
    n, k, _ = src_ref.shape
    dst[...] = jnp.zeros(dst.shape, BF16)
    for b in range(n):
        p, q = divmod(b, 256 // k)
        dst[p, q * k:(q + 1) * k, q * k:(q + 1) * k] = src_ref[b].astype(BF16)


def _diag_pack(w, k):
    lane = lax.broadcasted_iota(jnp.int32, (k, 256), 1)
    out = w[0:k]
    for q in range(1, 256 // k):
        out = jnp.where(lane >= q * k, w[q * k:(q + 1) * k], out)
    return out


def _y_pos(b):
    return 4 * (b % 2) + b // 2


def _softplus_neg_lambda(pv):
    z = -pv[ROW_LAM:ROW_LAM + 1, :]
    return jnp.maximum(z, 0.0) + jnp.log(1.0 + jnp.exp(-jnp.abs(z)))


def _lru_gates(e_lru, pv, wa_ref, wx_ref, tm):
    xc = pv[ROW_CB:ROW_CB + 1, :]
    for k in range(4):
        xc = xc + e_lru[pl.ds(HALO - 3 + k, tm), :] * pv[ROW_CW + k:ROW_CW + k + 1, :]
    xcb = xc.astype(BF16)
    r = _sigmoid(_bd(xcb, wa_ref) + pv[ROW_BA:ROW_BA + 1, :])
    ig = _sigmoid(_bd(xcb, wx_ref) + pv[ROW_BX:ROW_BX + 1, :])
    return xc, r, ig, (-LRU_C * r) * _softplus_neg_lambda(pv)


def _lru_decay(la):
    a = jnp.exp(la)
    om = -jnp.tanh(la) * (1.0 + a * a)
    omc = jnp.maximum(om, 1e-12)
    rmult = lax.rsqrt(omc)
    return a, om, omc * rmult, rmult


def _over_count(v, w, inv_head):
    return jnp.concatenate([v[0:HALO] * inv_head, v[HALO:] * (1.0 / w)], axis=0)


def _pool_pre(e_pool, pv, wp_ref, tm, t0):
    t_head = t0 + lax.broadcasted_iota(jnp.int32, (HALO, 1), 0)
    parts, inv_heads = [], []
    for g, w in enumerate(POOL_WINDOWS):
        ext = e_pool[:, pl.ds(128 * g, 128)]
        s = _window_sum(ext, w, back=True)[HALO:, :]
        inv_head = 1.0 / jnp.minimum(t_head + 1, w).astype(F32)
        inv_heads.append(inv_head)
        parts.append(_over_count(s, w, inv_head) - ext[HALO:, :])
    pooled = jnp.concatenate(parts, axis=1)
    pooled_b = pooled.astype(BF16)
    zp = _bd(pooled_b, wp_ref) + pv[ROW_PB:ROW_PB + 1, :]
    return pooled_b, zp, inv_heads


def _scan_tile(a_ref, b_ref, out_ref, carry, tm, reverse):
    row = lax.broadcasted_iota(jnp.int32, (8, LRU_W), 0)
    nblk = tm // 8

    def local_scan(blk):
        r0 = pl.multiple_of(blk * 8, 8)
        av = a_ref[pl.ds(r0, 8), :]
        bv = b_ref[pl.ds(r0, 8), :]
        for d in (1, 2, 4):
            sh = (8 - d) if reverse else d
            a_s = pltpu.roll(av, sh, 0)
            b_s = pltpu.roll(bv, sh, 0)
            m = (row < 8 - d) if reverse else (row >= d)
            bv = jnp.where(m, av * b_s + bv, bv)
            av = jnp.where(m, av * a_s, av)
        return r0, av, bv

    def step(i, hin):
        local = [local_scan((nblk - 1 - (i * SCAN_UNROLL + j)) if reverse else (i * SCAN_UNROLL + j))
                 for j in range(SCAN_UNROLL)]
        for r0, av, bv in local:
            hv = av * hin + bv
            out_ref[pl.ds(r0, 8), :] = hv
            hin = jnp.broadcast_to(hv[0:1, :] if reverse else hv[7:8, :], (8, LRU_W))
        return hin

    return lax.fori_loop(0, nblk // SCAN_UNROLL, step, carry)


MESH = pl.DeviceIdType.MESH
ANY = pl.BlockSpec(memory_space=pl.ANY)


def _place():
    x, y, c = lax.axis_index("x"), lax.axis_index("y"), lax.axis_index("c")
    chips = [(1 - x, y), (x, 1 - y), (1 - x, 1 - y)]
    return x, y, c, chips


class _Gather:
    def __init__(self, ins, outs, send_sems, recv_sems, local_sems, core_major=False):
        self.ins, self.outs, self.n = ins, outs, len(ins)
        self.send_sems, self.recv_sems, self.local_sems = send_sems, recv_sems, local_sems
        self.core_major = core_major

    @staticmethod
    def scratch(n):
        return [pltpu.SemaphoreType.DMA((8, n)), pltpu.SemaphoreType.DMA((8, n)), pltpu.SemaphoreType.DMA((n,))]

    def _slot(self, a, px, py, pc):
        return self.outs[a].at[4 * pc + 2 * px + py if self.core_major else 4 * px + 2 * py + pc]

    def _half(self, a, h):
        rows = self.ins[a].shape[0]
        if rows % RELAY_SPLIT_ROWS:
            return None if h else (0, rows)
        return (h * (rows // 2), rows // 2)

    def _copy(self, a, k, block, to, src=None, rows=None):
        dst = self._slot(a, *block)
        src = dst if src is None else src
        if rows is not None:
            src, dst = src.at[pl.ds(*rows)], dst.at[pl.ds(*rows)]
        return pltpu.make_async_remote_copy(
            src_ref=src, dst_ref=dst, send_sem=self.send_sems.at[k, a], recv_sem=self.recv_sems.at[k, a],
            device_id=to, device_id_type=MESH)

    def _mine(self, a):
        x, y, c, _ = _place()
        return pltpu.make_async_copy(self.ins[a], self._slot(a, x, y, c), self.local_sems.at[a])

    def _first(self, a):
        x, y, c, chips = _place()
        me = (x, y, c)
        return ([self._copy(a, 0, me, (x, y, 1 - c), src=self.ins[a])]
                + [self._copy(a, 1 + j, me, (*chip, c), src=self.ins[a]) for j, chip in enumerate(chips[:2])])

    def _passed_on(self, a, h):
        x, y, c, chips = _place()
        block = (*chips[h], c)
        out = [self._copy(a, 4 + h, block, (x, y, 1 - c))]
        if self._half(a, h) is not None:
            out.append(self._copy(a, (3, 7)[h], block, (*chips[1 - h], c), rows=self._half(a, h)))
        return out

    def start(self):
        for a in range(self.n):
            self._mine(a).start()
        for a in range(self.n):
            for cp in self._first(a):
                cp.start()

    def relay(self):
        x, y, c, chips = _place()
        for h in range(2):
            for a in range(self.n):
                self._copy(a, 1 + h, (*chips[h], c), (x, y, c)).wait_recv()
                for cp in self._passed_on(a, h):
                    cp.start()

    def finish(self):
        x, y, c, chips = _place()
        me, sibling = (x, y, c), (x, y, 1 - c)
        passed = []
        for a in range(self.n):
            for h in range(2):
                if self._half(a, h) is not None:
                    self._copy(a, (3, 7)[h], (*chips[2], c), me, rows=self._half(a, h)).wait_recv()
            fwd = self._copy(a, 6, (*chips[2], c), sibling)
            fwd.start()
            passed.append(fwd)
        for a in range(self.n):
            self._copy(a, 0, (x, y, 1 - c), me).wait_recv()
            for j, chip in enumerate(chips):
                self._copy(a, 4 + j, (*chip, 1 - c), me).wait_recv()
        for a in range(self.n):
            for cp in self._first(a) + self._passed_on(a, 0) + self._passed_on(a, 1):
                cp.wait_send()
        for cp in passed:
            cp.wait_send()
        for a in range(self.n):
            self._mine(a).wait()


def _half_exchange(arr, name):
    def body(in_ref, out_ref, send_sems, recv_sems, local_sem):
        x, y, c, _ = _place()
        my_chip = 2 * x + y

        def send(j, wait):
            to_me = (c == x) & (y == j // 2) & (c == j % 2)

            @pl.when(to_me)
            def _():
                local = pltpu.make_async_copy(in_ref.at[j], out_ref.at[my_chip], local_sem)
                local.wait() if wait else local.start()

            @pl.when(jnp.logical_not(to_me))
            def _():
                remote = pltpu.make_async_remote_copy(
                    src_ref=in_ref.at[j], dst_ref=out_ref.at[my_chip], send_sem=send_sems.at[j],
                    recv_sem=recv_sems.at[my_chip], device_id=(c, j // 2, j % 2), device_id_type=MESH)
                remote.wait_send() if wait else remote.start()

        for j in range(4):
            send(j, wait=False)
        for j in range(4):
            send(j, wait=True)
        for k in range(4):
            from_me = (k // 2 == x) & (k % 2 == y) & (c == x)

            @pl.when(jnp.logical_not(from_me))
            def _():
                pltpu.make_async_remote_copy(
                    src_ref=in_ref.at[0], dst_ref=out_ref.at[k], send_sem=send_sems.at[0], recv_sem=recv_sems.at[k],
                    device_id=(k // 2, k % 2, x), device_id_type=MESH).wait_recv()

    return pl.pallas_call(
        body, name=name, out_shape=_sds(arr.shape, arr.dtype), in_specs=[ANY], out_specs=ANY,
        scratch_shapes=[pltpu.SemaphoreType.DMA((4,)), pltpu.SemaphoreType.DMA((4,)), pltpu.SemaphoreType.DMA],
    )(arr)


class _ChipExchange:
    def __init__(self, ins, outs, send_sems, recv_sems, local_sems):
        self.ins, self.outs, self.n = ins, outs, len(ins)
        self.send_sems, self.recv_sems, self.local_sems = send_sems, recv_sems, local_sems

    @staticmethod
    def scratch(n):
        return [pltpu.SemaphoreType.DMA((3, n)), pltpu.SemaphoreType.DMA((3, n)), pltpu.SemaphoreType.DMA((n,))]

    def _local(self, a):
        x, y, _, _ = _place()
        me = 2 * x + y
        return pltpu.make_async_copy(self.ins[a].at[me], self.outs[a].at[me], self.local_sems.at[a])

    def _copies(self, a):
        x, y, c, chips = _place()
        me = 2 * x + y
        return [(pltpu.make_async_remote_copy(
                     src_ref=self.ins[a].at[2 * px + py], dst_ref=self.outs[a].at[me],
                     send_sem=self.send_sems.at[k, a], recv_sem=self.recv_sems.at[k, a],
                     device_id=(px, py, c), device_id_type=MESH),
                 pltpu.make_async_remote_copy(
                     src_ref=self.ins[a].at[me], dst_ref=self.outs[a].at[2 * px + py],
                     send_sem=self.send_sems.at[k, a], recv_sem=self.recv_sems.at[k, a],
                     device_id=(px, py, c), device_id_type=MESH))
                for k, (px, py) in enumerate(chips)]

    def start(self):
        for a in range(self.n):
            self._local(a).start()
        for a in range(self.n):
            for send, _ in self._copies(a):
                send.start()

    def finish(self):
        for a in range(self.n):
            for send, recv in self._copies(a):
                send.wait_send()
                recv.wait_recv()
        for a in range(self.n):
            self._local(a).wait()


def _gathering(body, n_steps, n_s, core_major):
    def wrapped(*refs, n_in, n_out):
        ins, sh_in = refs[:n_in], refs[n_in:n_in + n_s]
        outs, sh_out = refs[n_in + n_s:n_in + n_s + n_out], refs[n_in + n_s + n_out:n_in + 2 * n_s + n_out]
        rest = refs[n_in + 2 * n_s + n_out:]
        gather = _Gather(sh_in, sh_out, *rest[len(rest) - 3:], core_major=core_major)
        i = pl.program_id(0)

        @pl.when(i == 0)
        def _():
            gather.start()

        @pl.when(i == n_steps // 2)
        def _():
            gather.relay()

        body(*ins, *outs, *rest[:len(rest) - 3])

        @pl.when(i == n_steps - 1)
        def _():
            gather.finish()

    return wrapped


def _norm_in(x, g_mix, shards, tm):
    T = x.shape[0]
    n_t = T // tm
    n_s = len(shards)

    def norm(x_ref, g_ref, h_ref):
        xv = x_ref[...]
        h_ref[...] = (xv * _rstd(xv) * g_ref[...]).astype(BF16)

    outs = pl.pallas_call(
        functools.partial(_gathering(norm, n_t, n_s, core_major=False), n_in=2, n_out=1), name="norm_in", grid=(n_t,),
        in_specs=[pl.BlockSpec((tm, D_MODEL), lambda i: (i, 0)), pl.BlockSpec((1, D_MODEL), lambda i: (0, 0))]
        + [ANY] * n_s,
        out_specs=[pl.BlockSpec((tm, D_MODEL), lambda i: (i, 0))] + [ANY] * n_s,
        out_shape=[_sds((T, D_MODEL), BF16)] + [_sds((N_DEV,) + a.shape, a.dtype) for a in shards],
        scratch_shapes=_Gather.scratch(n_s),
        compiler_params=pltpu.CompilerParams(dimension_semantics=("arbitrary",)),
    )(x, g_mix, *shards)
    return outs[0], list(outs[1:])


def _mix_in(h1, w_in_t, tm, shards):
    T = h1.shape[0]
    n_t = T // tm
    n_s = len(shards)

    def project(h_ref, w_ref, u_ref):
        u_ref[...] = lax.dot_general(h_ref[...], w_ref[...], NT, preferred_element_type=F32)

    outs = pl.pallas_call(
        functools.partial(_gathering(project, n_t, n_s, core_major=True), n_in=2, n_out=1), name="mix_in", grid=(n_t,),
        in_specs=[pl.BlockSpec((tm, D_MODEL), lambda i: (i, 0)), pl.BlockSpec((D_IN, D_MODEL), lambda i: (0, 0))]
        + [ANY] * n_s,
        out_specs=[pl.BlockSpec((tm, D_IN), lambda i: (i, 0))] + [ANY] * n_s,
        out_shape=[_sds((T, D_IN), F32)] + [_sds((N_DEV,) + a.shape, a.dtype) for a in shards],
        scratch_shapes=_Gather.scratch(n_s),
        compiler_params=pltpu.CompilerParams(dimension_semantics=("arbitrary",)),
    )(h1, w_in_t, *shards)
    return outs[0], list(outs[1:])


def _mixer_fwd(u, x, pv, wa, wx, wp, w_out_b, g_ffn, tm, shards=()):
    T = u.shape[0]
    n_s = len(shards)
    n_t = T // tm

    def body(u_ref, x_ref, pv_ref, wa_in, wx_in, wp_in, wo_ref, gf_ref, *rest):
        sh_in, rest = rest[:n_s], rest[n_s:]
        y_ref, hs_ref, hres_ref, h2_ref, saved_ref = rest[:5]
        sh_out, rest = rest[5:5 + n_s], rest[5 + n_s:]
        e_lru, e_pool, a_s, b_s, hc, wa_ref, wx_ref, wp_ref = rest[:8]
        gather = _Gather(sh_in, sh_out, *rest[8:], core_major=True) if n_s else None
        i = pl.program_id(0)

        @pl.when(i == 0)
        def _():
            if gather:
                gather.start()
            e_lru[pl.ds(0, HALO), :] = jnp.zeros((HALO, LRU_W), F32)
            e_pool[pl.ds(0, HALO), :] = jnp.zeros((HALO, POOL_W), F32)
            hc[...] = jnp.zeros((8, LRU_W), F32)
            _fill_block_diag(wa_ref, wa_in)
            _fill_block_diag(wx_ref, wx_in)
            _fill_block_diag(wp_ref, wp_in)

        if gather:
            @pl.when(i == (2 * n_t) // 3)
            def _():
                gather.relay()

        e_lru[pl.ds(HALO, tm), :] = u_ref[:, 0:LRU_W]
        e_pool[pl.ds(HALO, tm), :] = u_ref[:, 2 * LRU_W:D_IN]
        pv = pv_ref[...]
        xc, r, ig, la = _lru_gates(e_lru, pv, wa_ref, wx_ref, tm)
        for q, val in enumerate((xc, r, ig, la)):
            saved_ref[:, LRU_W * q:LRU_W * (q + 1)] = val
        a, _, mult, _ = _lru_decay(la)
        a_s[...] = a
        b_s[...] = mult * (ig * xc)
        hc[...] = _scan_tile(a_s, b_s, hs_ref, hc[...], tm, reverse=False)
        gl, _ = _gelu_parts(u_ref[:, LRU_W:2 * LRU_W])
        y_lru = hs_ref[...] * gl
        _, zp, _ = _pool_pre(e_pool, pv, wp_ref, tm, i * tm)
        y_pool = zp * pv[ROW_PS:ROW_PS + 1, :]
        yn = jnp.concatenate([y_lru * _rstd(y_lru) * pv[ROW_GL:ROW_GL + 1, :],
                              y_pool * _rstd(y_pool) * pv[ROW_GP:ROW_GP + 1, :]], axis=1).astype(BF16)
        for b in range(N_DEV):
            y_ref[:, 128 * _y_pos(b):128 * (_y_pos(b) + 1)] = yn[:, 128 * b:128 * (b + 1)]
        hr = x_ref[...] + jnp.dot(y_ref[...], wo_ref[...], preferred_element_type=F32)
        hres_ref[...] = hr
        h2_ref[...] = (hr * _rstd(hr) * gf_ref[...]).astype(BF16)
        e_lru[pl.ds(0, HALO), :] = e_lru[pl.ds(tm, HALO), :]
        e_pool[pl.ds(0, HALO), :] = e_pool[pl.ds(tm, HALO), :]

        if gather:
            @pl.when(i == n_t - 1)
            def _():
                gather.finish()

    full = lambda shape: pl.BlockSpec(shape, lambda i: (0,) * len(shape))
    row = lambda w: pl.BlockSpec((tm, w), lambda i: (i, 0))
    outs = pl.pallas_call(
        body, name="mixer_fwd", grid=(n_t,),
        in_specs=[row(D_IN), row(D_MODEL), full((16, LRU_W)), full((8, 64, 64)), full((8, 64, 64)), full((4, 128, 128)),
                  full((D_MODEL, D_MODEL)), full((1, D_MODEL))] + [ANY] * n_s,
        out_specs=[row(D_MODEL), row(LRU_W), row(D_MODEL), row(D_MODEL), row(4 * LRU_W)] + [ANY] * n_s,
        out_shape=[_sds((T, D_MODEL), BF16), _sds((T, LRU_W), F32), _sds((T, D_MODEL), F32), _sds((T, D_MODEL), BF16),
                   _sds((T, 4 * LRU_W), F32)] + [_sds((N_DEV,) + a.shape, a.dtype) for a in shards],
        scratch_shapes=[pltpu.VMEM((HALO + tm, LRU_W), F32), pltpu.VMEM((HALO + tm, POOL_W), F32),
                        pltpu.VMEM((tm, LRU_W), F32), pltpu.VMEM((tm, LRU_W), F32), pltpu.VMEM((8, LRU_W), F32)]
        + [pltpu.VMEM((2, 256, 256), BF16)] * 3 + (_Gather.scratch(n_s) if n_s else []),
        compiler_params=pltpu.CompilerParams(dimension_semantics=("arbitrary",)),
    )(u, x, pv, wa, wx, wp, w_out_b, g_ffn, *shards)
    return outs[0], outs[1], outs[2], outs[3], outs[4], list(outs[5:])


def _ffn_fwd(hres, h2, w1_b, w3_b, w2_b, g_fin, tgt, tm):
    T = hres.shape[0]

    def body(hres_ref, h2_ref, w1_ref, w3_ref, w2_ref, gfin_ref, tgt_ref,
             g_ref, v_ref, ff_ref, d3_ref, loss_ref, dgfin_ref):
        @pl.when(pl.program_id(0) == 0)
        def _():
            loss_ref[...] = jnp.zeros((8, 128), F32)
            dgfin_ref[...] = jnp.zeros((1, D_MODEL), F32)

        h2 = h2_ref[...]
        h3 = hres_ref[...]
        for lo, hi in FF_CHUNKS:
            g = lax.dot_general(h2, w1_ref[lo:hi, :], NT, preferred_element_type=F32)
            v = lax.dot_general(h2, w3_ref[lo:hi, :], NT, preferred_element_type=F32)
            g_ref[:, lo:hi] = g.astype(BF16)
            v_ref[:, lo:hi] = v.astype(BF16)
            ff = ((g * _sigmoid(g)) * v).astype(BF16)
            ff_ref[:, lo:hi] = ff
            h3 = h3 + jnp.dot(ff, w2_ref[lo:hi, :], preferred_element_type=F32)

        rstd = _rstd(h3)
        xh = h3 * rstd
        gfin = gfin_ref[...]
        err = xh * gfin - tgt_ref[...]
        loss_ref[...] += 0.5 * jnp.sum(jnp.mean(err * err, axis=-1, keepdims=True))
        dx, dgain = _rms_bwd(err * (1.0 / D_MODEL), xh, rstd, gfin)
        d3_ref[...] = dx
        dgfin_ref[...] += dgain

    row = lambda w: pl.BlockSpec((tm, w), lambda i: (i, 0))
    const = lambda shape: pl.BlockSpec(shape, lambda i: (0,) * len(shape))
    weight = pl.BlockSpec((D_FF, D_MODEL), lambda i: (0, 0), pipeline_mode=pl.Buffered(1))
    return pl.pallas_call(
        body, name="ffn_fwd", grid=(T // tm,),
        in_specs=[row(D_MODEL), row(D_MODEL), weight, weight, weight, const((1, D_MODEL)), row(D_MODEL)],
        out_specs=[row(D_FF), row(D_FF), row(D_FF), row(D_MODEL), const((8, 128)), const((1, D_MODEL))],
        out_shape=[_sds((T, D_FF), BF16), _sds((T, D_FF), BF16), _sds((T, D_FF), BF16),
                   _sds((T, D_MODEL), F32), _sds((8, 128), F32), _sds((1, D_MODEL), F32)],
        compiler_params=pltpu.CompilerParams(dimension_semantics=("arbitrary",)),
    )(hres, h2, w1_b, w3_b, w2_b, g_fin, tgt)


def _ffn_bwd(d3, g, v, w1_b, w3_b, w2_b, hres, g_ffn, tm):
    T = d3.shape[0]

    def body(d3_ref, g_ref, v_ref, w1_ref, w3_ref, w2_ref, hres_ref, gf_ref, dg_ref, dv_ref, d2_ref, dgffn_ref):
        @pl.when(pl.program_id(0) == 0)
        def _():
            dgffn_ref[...] = jnp.zeros((1, D_MODEL), F32)

        d3 = d3_ref[...]
        d3b = d3.astype(BF16)
        dh2 = jnp.zeros((tm, D_MODEL), F32)
        for lo, hi in FF_CHUNKS:
            dff = lax.dot_general(d3b, w2_ref[lo:hi, :], NT, preferred_element_type=F32)
            gv = g_ref[:, lo:hi].astype(F32)
            vv = v_ref[:, lo:hi].astype(F32)
            sg = _sigmoid(gv)
            sl = gv * sg
            dgb = (dff * vv * (sg * (1.0 + gv * (1.0 - sg)))).astype(BF16)
            dvb = (dff * sl).astype(BF16)
            dg_ref[:, lo:hi] = dgb
            dv_ref[:, lo:hi] = dvb
            dh2 = dh2 + (jnp.dot(dgb, w1_ref[lo:hi, :], preferred_element_type=F32)
                         + jnp.dot(dvb, w3_ref[lo:hi, :], preferred_element_type=F32))

        hr = hres_ref[...]
        rstd = _rstd(hr)
        dx, dgain = _rms_bwd(dh2, hr * rstd, rstd, gf_ref[...])
        d2_ref[...] = d3 + dx
        dgffn_ref[...] += dgain

    row = lambda w: pl.BlockSpec((tm, w), lambda i: (i, 0))
    const = lambda shape: pl.BlockSpec(shape, lambda i: (0,) * len(shape))
    weight = pl.BlockSpec((D_FF, D_MODEL), lambda i: (0, 0), pipeline_mode=pl.Buffered(1))
    return pl.pallas_call(
        body, name="ffn_bwd", grid=(T // tm,),
        in_specs=[row(D_MODEL), row(D_FF), row(D_FF), weight, weight, weight, row(D_MODEL), const((1, D_MODEL))],
        out_specs=[row(D_FF), row(D_FF), row(D_MODEL), const((1, D_MODEL))],
        out_shape=[_sds((T, D_FF), BF16), _sds((T, D_FF), BF16), _sds((T, D_MODEL), F32), _sds((1, D_MODEL), F32)],
        compiler_params=pltpu.CompilerParams(dimension_semantics=("arbitrary",)),
    )(d3, g, v, w1_b, w3_b, w2_b, hres, g_ffn)


def _at_b_pair(a, b, c_arr, name, tk, gather=()):
    T, M = a.shape
    N = b.shape[1]
    hm, n_k = M // 2, T // tk
    n_g = len(gather)

    def body(c_ref, a_ref, b_ref, *rest):
        g_in, o_ref, rest = rest[:n_g], rest[n_g], rest[n_g + 1:]
        g_out, rest = rest[:n_g], rest[n_g:]
        acc, landed, send_sem, recv_sem = rest[:4]
        ag = _Gather(g_in, g_out, *rest[4:]) if n_g else None
        ph, k = pl.program_id(0), pl.program_id(1)

        def hand_over():
            x, y, c, _ = _place()
            return pltpu.make_async_remote_copy(
                src_ref=acc.at[0], dst_ref=landed, send_sem=send_sem, recv_sem=recv_sem,
                device_id=(x, y, 1 - c), device_id_type=MESH)

        if ag:
            @pl.when((ph == 0) & (k == 0))
            def _():
                ag.start()

            @pl.when((ph == 1) & (k == 0))
            def _():
                ag.relay()

        @pl.when(k == 0)
        def _():
            acc[ph] = jnp.zeros((hm, N), F32)

        acc[ph] += lax.dot_general(a_ref[...].astype(BF16), b_ref[...].astype(BF16), TN, preferred_element_type=F32)

        @pl.when((ph == 0) & (k == n_k - 1))
        def _():
            hand_over().start()

        @pl.when((ph == 1) & (k == n_k - 1))
        def _():
            copy = hand_over()
            copy.wait_recv()
            o_ref[...] = (acc[1] + landed[...]).astype(BF16)
            copy.wait_send()
            if ag:
                ag.finish()

    outs = pl.pallas_call(
        body, name=name,
        grid_spec=pltpu.PrefetchScalarGridSpec(
            num_scalar_prefetch=1, grid=(2, n_k),
            in_specs=[pl.BlockSpec((tk, hm), lambda ph, k, c_ref: (k, (ph + 1 - c_ref[0]) % 2)),
                      pl.BlockSpec((tk, N), lambda ph, k, c_ref: (k, 0))] + [ANY] * n_g,
            out_specs=[pl.BlockSpec((hm, N), lambda ph, k, c_ref: (0, 0))] + [ANY] * n_g,
            scratch_shapes=[pltpu.VMEM((2, hm, N), F32), pltpu.VMEM((hm, N), F32),
                            pltpu.SemaphoreType.DMA, pltpu.SemaphoreType.DMA] + (_Gather.scratch(n_g) if n_g else [])),
        out_shape=[_sds((hm, N), BF16)] + [_sds((N_DEV,) + g.shape, g.dtype) for g in gather],
        compiler_params=pltpu.CompilerParams(dimension_semantics=("arbitrary", "arbitrary")),
    )(c_arr, a, b, *gather)
    return (outs[0], list(outs[1:])) if n_g else outs[0]


def _mixer_bwd(d2, u, hs, saved, pv, wa, wx, wp, w_out_b, tm, chip_sums=()):
    T = u.shape[0]
    n_t = T // tm
    n_x = len(chip_sums)

    def body(d2_ref, u_ref, uh_ref, hs_ref, hh_ref, saved_ref, pv_ref, wa_in, wx_in, wp_in, wo_ref, *rest):
        x_in, rest = rest[:n_x], rest[n_x:]
        du_ref, sg_ref = rest[:2]
        x_out, rest = rest[2:2 + n_x], rest[2 + n_x:]
        e_pool, e_h, a_s, b_s, dh_s, mu_s, f_x, f_p, mc, cx, cp = rest[:11]
        wa_ref, wx_ref, wp_ref, vacc_ref, dwa_ref, dwx_ref, dwp_ref = rest[11:18]
        exchange = _ChipExchange(x_in, x_out, *rest[18:]) if n_x else None
        s = pl.program_id(0)
        it = n_t - 1 - s

        @pl.when(s == 0)
        def _():
            if exchange:
                exchange.start()
            mc[...] = jnp.zeros((8, LRU_W), F32)
            cx[...] = jnp.zeros((8, LRU_W), F32)
            cp[...] = jnp.zeros((HALO, POOL_W), F32)
            vacc_ref[...] = jnp.zeros((16, LRU_W), F32)
            dwa_ref[...] = jnp.zeros((2, 256, 256), F32)
            dwx_ref[...] = jnp.zeros((2, 256, 256), F32)
            dwp_ref[...] = jnp.zeros((2, 256, 256), F32)
            _fill_block_diag(wa_ref, wa_in)
            _fill_block_diag(wx_ref, wx_in)
            _fill_block_diag(wp_ref, wp_in)

        first = it == 0
        e_pool[pl.ds(0, HALO), :] = jnp.where(first, 0.0, uh_ref[...])
        e_pool[pl.ds(HALO, tm), :] = u_ref[:, 2 * LRU_W:D_IN]
        e_h[pl.ds(0, 8), :] = jnp.where(first, 0.0, hh_ref[...])
        e_h[pl.ds(8, tm), :] = hs_ref[...]
        pv = pv_ref[...]
        saved = lambda q: saved_ref[:, LRU_W * q:LRU_W * (q + 1)]

        dyn = lax.dot_general(d2_ref[...].astype(BF16), wo_ref[...], NT, preferred_element_type=F32)
        dyn = jnp.concatenate([dyn[:, 128 * _y_pos(b):128 * (_y_pos(b) + 1)] for b in range(N_DEV)], axis=1)

        h = hs_ref[...]
        ug = u_ref[:, LRU_W:2 * LRU_W]
        gl, dgl = _gelu_parts(ug)
        y_lru = h * gl
        rstd_l = _rstd(y_lru)
        dy_lru, d_gain_l = _rms_bwd(dyn[:, 0:LRU_W], y_lru * rstd_l, rstd_l, pv[ROW_GL:ROW_GL + 1, :])
        dh = dy_lru * gl
        du_ref[:, LRU_W:2 * LRU_W] = (dy_lru * h * dgl).astype(BF16)
        a_s[...] = jnp.exp(saved(3))
        b_s[...] = a_s[...] * dh
        dh_s[...] = dh
        mu_s[pl.ds(tm, 8), :] = mc[...]
        mc[...] = _scan_tile(a_s, b_s, mu_s, mc[...], tm, reverse=True)
        xc, r, ig = saved(0), saved(1), saved(2)
        a, om, mult, rmult = _lru_decay(saved(3))
        lam_t = dh_s[...] + mu_s[pl.ds(1, tm), :]
        da = lam_t * e_h[pl.ds(7, tm), :]
        dmult = lam_t * (ig * xc)
        di = lam_t * (mult * xc)
        dxc = lam_t * (mult * ig)
        dla = da * a - jnp.where(om > 1e-12, dmult * ((a * a) * rmult), 0.0)
        dra = (dla * (-LRU_C * _softplus_neg_lambda(pv))) * (r * (1.0 - r))
        dia = di * (ig * (1.0 - ig))
        drab = dra.astype(BF16)
        diab = dia.astype(BF16)
        xcb = xc.astype(BF16)
        dxc = dxc + _bd_t(drab, wa_ref) + _bd_t(diab, wx_ref)
        dwa_ref[...] += _bd_grad(xcb, drab)
        dwx_ref[...] += _bd_grad(xcb, diab)
        sig_neg_lam = _sigmoid(-pv[ROW_LAM:ROW_LAM + 1, :])
        d_lam = jnp.sum(dla * r, axis=0, keepdims=True) * (LRU_C * sig_neg_lam)

        f_x[pl.ds(0, tm), :] = dxc
        f_x[pl.ds(tm, 8), :] = cx[...]
        du_lru = jnp.zeros((tm, LRU_W), F32)
        u_lru = u_ref[:, 0:LRU_W]
        d_cw = []
        for k in range(4):
            later = f_x[pl.ds(3 - k, tm), :]
            du_lru = du_lru + later * pv[ROW_CW + k:ROW_CW + k + 1, :]
            d_cw.append(jnp.sum(later * u_lru, axis=0, keepdims=True))
        du_ref[:, 0:LRU_W] = du_lru.astype(BF16)
        cx[...] = f_x[pl.ds(0, 8), :]

        pooled_b, zp, inv_cnts = _pool_pre(e_pool, pv, wp_ref, tm, it * tm)
        ps = pv[ROW_PS:ROW_PS + 1, :]
        y_pool = zp * ps
        rstd_p = _rstd(y_pool)
        dy_pool, d_gain_p = _rms_bwd(dyn[:, LRU_W:D_MODEL], y_pool * rstd_p, rstd_p, pv[ROW_GP:ROW_GP + 1, :])
        dz = dy_pool * ps
        dzb = dz.astype(BF16)
        dwp_ref[...] += _bd_grad(pooled_b, dzb)
        dpooled = _bd_t(dzb, wp_ref)
        for g, w in enumerate(POOL_WINDOWS):
            f_p[pl.ds(0, tm), pl.ds(128 * g, 128)] = _over_count(dpooled[:, 128 * g:128 * (g + 1)], w, inv_cnts[g])
        f_p[pl.ds(tm, HALO), :] = cp[...]
        for g, w in enumerate(POOL_WINDOWS):
            acc = _window_sum(f_p[:, pl.ds(128 * g, 128)], w, back=False)[0:tm, :]
            du_ref[:, 2 * LRU_W + 128 * g:2 * LRU_W + 128 * (g + 1)] = (
                acc - dpooled[:, 128 * g:128 * (g + 1)]).astype(BF16)
        cp[...] = f_p[pl.ds(0, HALO), :]

        rows = d_cw + [
            jnp.sum(dxc, axis=0, keepdims=True),
            jnp.sum(dra, axis=0, keepdims=True),
            jnp.sum(dia, axis=0, keepdims=True),
            d_lam,
            jnp.sum(dz, axis=0, keepdims=True),
            jnp.sum(dy_pool * zp, axis=0, keepdims=True),
            d_gain_l, d_gain_p,
            jnp.zeros((4, LRU_W), F32),
        ]
        vacc_ref[...] += jnp.concatenate(rows, axis=0)

        @pl.when(s == n_t - 1)
        def _():
            sg_ref[SG_VEC:SG_VEC + 16, :] = vacc_ref[:, 0:256]
            sg_ref[SG_VEC + 16:SG_VEC + 32, :] = vacc_ref[:, 256:512]
            for half in range(2):
                sg_ref[SG_WA + 64 * half:SG_WA + 64 * (half + 1), :] = _diag_pack(dwa_ref[half], 64)
                sg_ref[SG_WX + 64 * half:SG_WX + 64 * (half + 1), :] = _diag_pack(dwx_ref[half], 64)
                sg_ref[SG_WP + 128 * half:SG_WP + 128 * (half + 1), :] = _diag_pack(dwp_ref[half], 128)
            if exchange:
                exchange.finish()

    rev = lambda w: pl.BlockSpec((tm, w), lambda s: (n_t - 1 - s, 0))
    full = lambda shape: pl.BlockSpec(shape, lambda s: (0,) * len(shape))
    outs = pl.pallas_call(
        body, name="mixer_bwd", grid=(n_t,),
        in_specs=[rev(D_MODEL), rev(D_IN),
                  pl.BlockSpec((HALO, POOL_W), lambda s: (jnp.maximum((n_t - 1 - s) * (tm // HALO) - 1, 0), 2)),
                  rev(LRU_W),
                  pl.BlockSpec((8, LRU_W), lambda s: (jnp.maximum((n_t - 1 - s) * (tm // 8) - 1, 0), 0)),
                  rev(4 * LRU_W), full((16, LRU_W)), full((8, 64, 64)), full((8, 64, 64)), full((4, 128, 128)),
                  full((D_MODEL, D_MODEL))] + [ANY] * n_x,
        out_specs=[rev(D_IN), full((SG_ROWS, 256))] + [ANY] * n_x,
        out_shape=[_sds((T, D_IN), BF16), _sds((SG_ROWS, 256), F32)] + [_sds(a.shape, a.dtype) for a in chip_sums],
        scratch_shapes=[pltpu.VMEM((HALO + tm, POOL_W), F32),
                        pltpu.VMEM((8 + tm, LRU_W), F32)] + [pltpu.VMEM((tm, LRU_W), F32)] * 3 + [
                        pltpu.VMEM((tm + 8, LRU_W), F32), pltpu.VMEM((tm + 8, LRU_W), F32),
                        pltpu.VMEM((tm + HALO, POOL_W), F32), pltpu.VMEM((8, LRU_W), F32),
                        pltpu.VMEM((8, LRU_W), F32), pltpu.VMEM((HALO, POOL_W), F32)]
        + [pltpu.VMEM((2, 256, 256), BF16)] * 3 + [pltpu.VMEM((16, LRU_W), F32)] + [pltpu.VMEM((2, 256, 256), F32)] * 3
        + (_ChipExchange.scratch(n_x) if n_x else []),
        compiler_params=pltpu.CompilerParams(dimension_semantics=("arbitrary",)),
    )(d2, u, u, hs, hs, saved, pv, wa, wx, wp, w_out_b, *chip_sums)
    return outs[0], outs[1], list(outs[2:])


def _mix_in_bwd(du, x, d2, w_in_t, g_mix, tm):
    T = x.shape[0]

    def body(du_ref, x_ref, d2_ref, w_ref, g_ref, dx_ref, dg_ref):
        @pl.when(pl.program_id(0) == 0)
        def _():
            dg_ref[...] = jnp.zeros((1, D_MODEL), F32)

        dh = jnp.dot(du_ref[...], w_ref[...], preferred_element_type=F32)
        xv = x_ref[...]
        rstd = _rstd(xv)
        dx, dgain = _rms_bwd(dh, xv * rstd, rstd, g_ref[...])
        dx_ref[...] = d2_ref[...] + dx
        dg_ref[...] += dgain

    row = lambda w: pl.BlockSpec((tm, w), lambda i: (i, 0))
    const = lambda shape: pl.BlockSpec(shape, lambda i: (0,) * len(shape))
    return pl.pallas_call(
        body, name="mix_in_bwd", grid=(T // tm,),
        in_specs=[row(D_IN), row(D_MODEL), row(D_MODEL), const((D_IN, D_MODEL)), const((1, D_MODEL))],
        out_specs=[row(D_MODEL), const((1, D_MODEL))],
        out_shape=[_sds((T, D_MODEL), F32), _sds((1, D_MODEL), F32)],
        compiler_params=pltpu.CompilerParams(dimension_semantics=("arbitrary",)),
    )(du, x, d2, w_in_t, g_mix)


def _adamw(w, g, m, v):
    m = ADAM_B1 * m + (1.0 - ADAM_B1) * g
    v = ADAM_B2 * v + (1.0 - ADAM_B2) * (g * g)
    m_hat = m / (1.0 - ADAM_B1 ** ADAM_STEP)
    v_hat = v / (1.0 - ADAM_B2 ** ADAM_STEP)
    delta = -ADAM_LR * (m_hat / (jnp.sqrt(v_hat) + ADAM_EPS) + ADAM_WD * w)
    return delta, m, v


def _adam_shards(ws, ms, vs, parts):
    n = len(ws)
    n_blk = [w.shape[0] // ADAM_ROWS for w in ws]

    def body(*refs):
        w_refs, m_refs, v_refs, p_refs, outs = (refs[:n], refs[n:2 * n], refs[2 * n:3 * n], refs[3 * n:4 * n],
                                                refs[4 * n:])
        i = pl.program_id(0)
        for a in range(n):
            @pl.when(i < n_blk[a])
            def _(a=a):
                g = p_refs[a][0].astype(F32)
                for j in range(1, 4):
                    g = g + p_refs[a][j].astype(F32)
                delta, new_m, new_v = _adamw(w_refs[a][...], g, m_refs[a][...], v_refs[a][...])
                for kind, val in enumerate((g, delta, new_m, new_v)):
                    outs[4 * a + kind][...] = val

    blk = lambda a: pl.BlockSpec((ADAM_ROWS, D_MODEL), lambda i: (jnp.minimum(i, n_blk[a] - 1), 0))
    part_blk = lambda a: pl.BlockSpec((4, ADAM_ROWS, D_MODEL), lambda i: (0, jnp.minimum(i, n_blk[a] - 1), 0))
    res = pl.pallas_call(
        body, name="adam_shards", grid=(max(n_blk),),
        in_specs=[blk(a) for a in range(n)] * 3 + [part_blk(a) for a in range(n)],
        out_specs=[blk(a) for a in range(n) for _ in range(4)],
        out_shape=[_sds(w.shape, F32) for w in ws for _ in range(4)],
        compiler_params=pltpu.CompilerParams(dimension_semantics=("arbitrary",)),
    )(*ws, *ms, *vs, *parts)
    return [tuple(res[4 * a:4 * a + 4]) for a in range(n)]


SMALL_PARAMS = [("norm_mix_g", (1, D_MODEL)), ("conv_w", (1, 4, 64)), ("conv_b", (1, LRU_W)),
                ("gate_a_w", (1, 8, 64, 64)), ("gate_a_b", (1, LRU_W)), ("gate_x_w", (1, 8, 64, 64)),
                ("gate_x_b", (1, LRU_W)), ("lru_lambda", (1, LRU_W)), ("pool_w", (1, 4, 128, 128)),
                ("pool_b", (1, POOL_W)), ("pool_scale", (1, POOL_W)), ("norm_lru_g", (1, LRU_W)),
                ("norm_pool_g", (1, POOL_W)), ("norm_ffn_g", (1, D_MODEL)), ("final_norm_g", (1, D_MODEL))]
VEC_ROW = dict(conv_b=ROW_CB, gate_a_b=ROW_BA, gate_x_b=ROW_BX, lru_lambda=ROW_LAM, pool_b=ROW_PB, pool_scale=ROW_PS,
               norm_lru_g=ROW_GL, norm_pool_g=ROW_GP)
WHOLE = (Ellipsis,)


def _unpack_mixer_grads(sg, dev):
    vec = jnp.concatenate([sg[SG_VEC:SG_VEC + 16], sg[SG_VEC + 16:SG_VEC + 32]], axis=1)
    out = {nm: [(WHOLE, vec[r:r + 1])] for nm, r in VEC_ROW.items()}
    own = jnp.zeros((4, 64), F32)
    for d in range(N_DEV):
        own = jnp.where(dev == d, vec[ROW_CW:ROW_CW + 4, 64 * d:64 * (d + 1)], own)
    out["conv_w"] = [((0,), own)]
    for nm, row0 in (("gate_a_w", SG_WA), ("gate_x_w", SG_WX)):
        out[nm] = [((0, b), sg[row0 + 64 * (b // 4):row0 + 64 * (b // 4 + 1), 64 * (b % 4):64 * (b % 4 + 1)])
                   for b in range(8)]
    out["pool_w"] = [((0, b), sg[SG_WP + 128 * (b // 2):SG_WP + 128 * (b // 2 + 1), 128 * (b % 2):128 * (b % 2 + 1)])
                     for b in range(4)]
    return out


def _adam_small(parts, w, m, v):
    names = [nm for nm, _ in SMALL_PARAMS]
    n = len(names)

    def body(sg_ref, gm_ref, gf_ref, gn_ref, ls_ref, *rest):
        w_refs, m_refs, v_refs, outs = rest[:n], rest[n:2 * n], rest[2 * n:3 * n], rest[3 * n:]
        dev = 4 * lax.axis_index("x") + 2 * lax.axis_index("y") + lax.axis_index("c")

        def total(ref):
            acc = ref[0]
            for d in range(1, N_DEV):
                acc = acc + ref[d]
            return acc

        pieces = _unpack_mixer_grads(total(sg_ref), dev)
        pieces["norm_mix_g"] = [(WHOLE, total(gm_ref))]
        pieces["norm_ffn_g"] = [(WHOLE, total(gf_ref))]
        pieces["final_norm_g"] = [(WHOLE, total(gn_ref))]
        for i, nm in enumerate(names):
            for idx, g in pieces[nm]:
                delta, new_m, new_v = _adamw(w_refs[i][idx], g, m_refs[i][idx], v_refs[i][idx])
                for kind, val in enumerate((g, delta, new_m, new_v)):
                    outs[4 * i + kind][idx] = val
        outs[4 * n][...] = total(ls_ref)

    shapes = [_sds(shape, F32) for _, shape in SMALL_PARAMS for _ in range(4)] + [_sds((8, 128), F32)]
    res = pl.pallas_call(body, name="adam_small", out_shape=shapes)(
        *parts, *[w[nm] for nm in names], *[m[nm] for nm in names], *[v[nm] for nm in names])
    return {nm: tuple(res[4 * i:4 * i + 4]) for i, nm in enumerate(names)}, res[4 * n][0, 0]


def _vec_rows(conv_w_full, conv_b, ba, bx, lam, pb, ps, gl, gp):
    return jnp.concatenate([conv_w_full, conv_b, ba, bx, lam, pb, ps, gl, gp, jnp.zeros((4, LRU_W), F32)], axis=0)


WEIGHT_ORDER = ['norm_mix_g', 'w_in', 'conv_w', 'conv_b', 'gate_a_w', 'gate_a_b', 'gate_x_w', 'gate_x_b', 'lru_lambda',
                'pool_w', 'pool_b', 'pool_scale', 'norm_lru_g', 'norm_pool_g', 'w_out', 'norm_ffn_g', 'ffn_w1', 'ffn_w3',
                'ffn_w2', 'final_norm_g']


def kernel(x, norm_mix_g, w_in, conv_w, conv_b, gate_a_w, gate_a_b, gate_x_w, gate_x_b, lru_lambda, pool_w, pool_b, pool_scale, norm_lru_g, norm_pool_g, w_out, norm_ffn_g, ffn_w1, ffn_w3, ffn_w2, final_norm_g, loss_target, m_norm_mix_g, m_w_in, m_conv_w, m_conv_b, m_gate_a_w, m_gate_a_b, m_gate_x_w, m_gate_x_b, m_lru_lambda, m_pool_w, m_pool_b, m_pool_scale, m_norm_lru_g, m_norm_pool_g, m_w_out, m_norm_ffn_g, m_ffn_w1, m_ffn_w3, m_ffn_w2, m_final_norm_g, v_norm_mix_g, v_w_in, v_conv_w, v_conv_b, v_gate_a_w, v_gate_a_b, v_gate_x_w, v_gate_x_b, v_lru_lambda, v_pool_w, v_pool_b, v_pool_scale, v_norm_lru_g, v_norm_pool_g, v_w_out, v_norm_ffn_g, v_ffn_w1, v_ffn_w3, v_ffn_w2, v_final_norm_g):
    ac = lax.axis_index("c")
    tm, tmx, tk = 512, 512, 1024
    tm_in = 1024
    xs, tgt = x[0], loss_target[0]
    g_fin = final_norm_g.reshape(1, D_MODEL)
    c_arr = jnp.reshape(ac, (1,)).astype(jnp.int32)

    tr = lambda w: jnp.swapaxes(w[0], 0, 1)
    own = lambda w: w[0]
    bf = lambda a: a.astype(BF16)

    h1, (g_in, g_conv) = _norm_in(xs, norm_mix_g, [bf(tr(w_in)), conv_w[0]], tm_in)
    w_in_t = g_in.reshape(D_IN, D_MODEL)
    u, (g_out,) = _mix_in(h1, w_in_t, tm, shards=[bf(own(w_out))])
    conv_w_full = g_conv.transpose(1, 0, 2).reshape(4, LRU_W)
    pv = _vec_rows(conv_w_full, conv_b, gate_a_b, gate_x_b, lru_lambda, pool_b, pool_scale, norm_lru_g, norm_pool_g)
    wa, wx, wp = gate_a_w[0], gate_x_w[0], pool_w[0]
    w_out_b = g_out.reshape(D_MODEL, D_MODEL)
    y, hs, hres, h2, saved, (g_w1, g_w3, g_w2) = _mixer_fwd(
        u, xs, pv, wa, wx, wp, w_out_b, norm_ffn_g, tmx, shards=[bf(tr(ffn_w1)), bf(tr(ffn_w3)), bf(own(ffn_w2))])
    w1_t, w3_t, w2_b = g_w1.reshape(D_FF, D_MODEL), g_w3.reshape(D_FF, D_MODEL), g_w2.reshape(D_FF, D_MODEL)
    g, v, ff, d3, loss_acc, d_gfin = _ffn_fwd(hres, h2, w1_t, w3_t, w2_b, g_fin, tgt, tm)

    dg, dv, d2, d_gffn = _ffn_bwd(d3, g, v, w1_t, w3_t, w2_b, hres, norm_ffn_g, tm)
    chips = lambda a: a.reshape(4, a.shape[0] // 4, a.shape[1])
    early_sums = [chips(_at_b_pair(y, d2, c_arr, "grad_w_out", 2 * tk)), chips(_at_b_pair(dg, h2, c_arr, "grad_w1", tk)),
                  chips(_at_b_pair(dv, h2, c_arr, "grad_w3", tk)), chips(_at_b_pair(ff, d3, c_arr, "grad_w2", tk))]
    du, d_mixer, early_parts = _mixer_bwd(d2, u, hs, saved, pv, wa, wx, wp, w_out_b, tmx, chip_sums=early_sums)
    grad_x, d_gmix = _mix_in_bwd(du, xs, d2, w_in_t, norm_mix_g, tm_in)
    d_win, small_parts = _at_b_pair(du, h1, c_arr, "grad_w_in", 2 * tk,
                                    gather=[d_mixer, d_gmix, d_gffn, d_gfin, loss_acc])
    parts = [_half_exchange(chips(d_win), "grads_to_chips_w_in")] + list(early_parts)

    res = {}
    shard_w = dict(w_in=(w_in, m_w_in, v_w_in, tr), w_out=(w_out, m_w_out, v_w_out, own),
                   ffn_w1=(ffn_w1, m_ffn_w1, v_ffn_w1, tr), ffn_w3=(ffn_w3, m_ffn_w3, v_ffn_w3, tr),
                   ffn_w2=(ffn_w2, m_ffn_w2, v_ffn_w2, own))
    shard_res = _adam_shards([view(w) for w, _, _, view in shard_w.values()],
                             [view(m) for _, m, _, view in shard_w.values()],
                             [view(v) for _, _, v, view in shard_w.values()], parts)
    for (nm, (_, _, _, view)), outs in zip(shard_w.items(), shard_res):
        res[nm] = [(jnp.swapaxes(o, 0, 1) if view is tr else o)[None] for o in outs]

    row = lambda a: a.reshape(1, D_MODEL)
    small = lambda gm, cw, cb, wa_, ba, wx_, bx, lam, pw, pb, ps, gl, gp, gf, gn: dict(
        norm_mix_g=gm, conv_w=cw, conv_b=cb, gate_a_w=wa_, gate_a_b=ba, gate_x_w=wx_, gate_x_b=bx, lru_lambda=lam,
        pool_w=pw, pool_b=pb, pool_scale=ps, norm_lru_g=gl, norm_pool_g=gp, norm_ffn_g=gf, final_norm_g=row(gn))
    small_res, loss = _adam_small(
        small_parts,
        small(norm_mix_g, conv_w, conv_b, gate_a_w, gate_a_b, gate_x_w, gate_x_b, lru_lambda, pool_w, pool_b,
              pool_scale, norm_lru_g, norm_pool_g, norm_ffn_g, final_norm_g),
        small(m_norm_mix_g, m_conv_w, m_conv_b, m_gate_a_w, m_gate_a_b, m_gate_x_w, m_gate_x_b, m_lru_lambda, m_pool_w,
              m_pool_b, m_pool_scale, m_norm_lru_g, m_norm_pool_g, m_norm_ffn_g, m_final_norm_g),
        small(v_norm_mix_g, v_conv_w, v_conv_b, v_gate_a_w, v_gate_a_b, v_gate_x_w, v_gate_x_b, v_lru_lambda, v_pool_w,
              v_pool_b, v_pool_scale, v_norm_lru_g, v_norm_pool_g, v_norm_ffn_g, v_final_norm_g))
    for nm, outs in small_res.items():
        res[nm] = [o.reshape(D_MODEL) for o in outs] if nm == "final_norm_g" else list(outs)

    out = [loss, grad_x[None]]
    for kind in range(4):
        out += [res[nm][kind] for nm in WEIGHT_ORDER]
    return tuple(out)
```

```python
import functools

import jax
import jax.numpy as jnp
from jax import lax
from jax.experimental import pallas as pl
from jax.experimental.pallas import tpu as pltpu

F32 = jnp.float32
BF16 = jnp.bfloat16

D_MODEL = 1024
LRU_W = 512
POOL_W = 512
D_IN = 1536
D_FF = 2816
POOL_WINDOWS = (2, 4, 8, 16)
EPS = 1e-6
LRU_C = 8.0
N_DEV = 8
HALO = 16
SCAN_UNROLL = 8
ADAM_ROWS = 32
FF_CHUNKS = ((0, 1536), (1536, 2816))
RELAY_SPLIT_ROWS = 32

ADAM_LR = 0.001
ADAM_B1 = 0.9
ADAM_B2 = 0.999
ADAM_EPS = 1e-08
ADAM_WD = 0.01
ADAM_STEP = 10

ROW_CW, ROW_CB, ROW_BA, ROW_BX, ROW_LAM, ROW_PB, ROW_PS, ROW_GL, ROW_GP = 0, 4, 5, 6, 7, 8, 9, 10, 11
SG_VEC, SG_WA, SG_WX, SG_WP, SG_ROWS = 0, 32, 160, 288, 544

NT = (((1,), (1,)), ((), ()))
TN = (((0,), (0,)), ((), ()))


def _sds(shape, dtype):
    return jax.ShapeDtypeStruct(shape, dtype)


def _sigmoid(x):
    return 0.5 * jnp.tanh(0.5 * x) + 0.5


def _gelu_parts(x):
    c = 0.7978845608028654
    inner = c * (x + 0.044715 * (x * x * x))
    th = jnp.tanh(inner)
    g = 0.5 * x * (1.0 + th)
    dg = 0.5 * (1.0 + th) + 0.5 * x * (1.0 - th * th) * (c * (1.0 + 3.0 * 0.044715 * (x * x)))
    return g, dg


def _window_sum(ext, w, back):
    n = ext.shape[0]
    s, k = ext, 1
    while k < w:
        s = s + pltpu.roll(s, k if back else n - k, 0)
        k *= 2
    return s


def _rstd(x):
    return lax.rsqrt(jnp.mean(x * x, axis=-1, keepdims=True) + EPS)


def _rms_bwd(dy, xhat, rstd, gain):
    dxh = dy * gain
    dx = rstd * (dxh - xhat * jnp.mean(dxh * xhat, axis=-1, keepdims=True))
    return dx, jnp.sum(dy * xhat, axis=0, keepdims=True)


def _bd(xb, w_ref):
    return jnp.concatenate(
        [jnp.dot(xb[:, :256], w_ref[0], preferred_element_type=F32),
         jnp.dot(xb[:, 256:], w_ref[1], preferred_element_type=F32)], axis=1)


def _bd_t(xb, w_ref):
    return jnp.concatenate(
        [lax.dot_general(xb[:, :256], w_ref[0], NT, preferred_element_type=F32),
         lax.dot_general(xb[:, 256:], w_ref[1], NT, preferred_element_type=F32)], axis=1)


def _bd_grad(xb, db):
    return jnp.stack(
        [lax.dot_general(xb[:, :256], db[:, :256], TN, preferred_element_type=F32),
         lax.dot_general(xb[:, 256:], db[:, 256:], TN, preferred_element_type=F32)], axis=0)


def _fill_block_diag(dst, src_ref):
    n, k, _ = src_ref.shape
    dst[...] = jnp.zeros(dst.shape, BF16)
    for b in range(n):
        p, q = divmod(b, 256 // k)
        dst[p, q * k:(q + 1) * k, q * k:(q + 1) * k] = src_ref[b].astype(BF16)


def _diag_pack(w, k):
    lane = lax.broadcasted_iota(jnp.int32, (k, 256), 1)
    out = w[0:k]
    for q in range(1, 256 // k):
        out = jnp.where(lane >= q * k, w[q * k:(q + 1) * k], out)
    return out


def _y_pos(b):
    return 4 * (b % 2) + b // 2


def _softplus_neg_lambda(pv):
    z = -pv[ROW_LAM:ROW_LAM + 1, :]
    return jnp.maximum(z, 0.0) + jnp.log(1.0 + jnp.exp(-jnp.abs(z)))


def _lru_gates(e_lru, pv, wa_ref, wx_ref, tm):
    xc = pv[ROW_CB:ROW_CB + 1, :]
    for k in range(4):
        xc = xc + e_lru[pl.ds(HALO - 3 + k, tm), :] * pv[ROW_CW + k:ROW_CW + k + 1, :]
    xcb = xc.astype(BF16)
    r = _sigmoid(_bd(xcb, wa_ref) + pv[ROW_BA:ROW_BA + 1, :])
    ig = _sigmoid(_bd(xcb, wx_ref) + pv[ROW_BX:ROW_BX + 1, :])
    return xc, r, ig, (-LRU_C * r) * _softplus_neg_lambda(pv)


def _lru_decay(la):
    a = jnp.exp(la)
    om = -jnp.tanh(la) * (1.0 + a * a)
    omc = jnp.maximum(om, 1e-12)
    rmult = lax.rsqrt(omc)
    return a, om, omc * rmult, rmult


def _over_count(v, w, inv_head):
    return jnp.concatenate([v[0:HALO] * inv_head, v[HALO:] * (1.0 / w)], axis=0)


def _pool_pre(e_pool, pv, wp_ref, tm, t0):
    t_head = t0 + lax.broadcasted_iota(jnp.int32, (HALO, 1), 0)
    parts, inv_heads = [], []
    for g, w in enumerate(POOL_WINDOWS):
        ext = e_pool[:, pl.ds(128 * g, 128)]
        s = _window_sum(ext, w, back=True)[HALO:, :]
        inv_head = 1.0 / jnp.minimum(t_head + 1, w).astype(F32)
        inv_heads.append(inv_head)
        parts.append(_over_count(s, w, inv_head) - ext[HALO:, :])
    pooled = jnp.concatenate(parts, axis=1)
    pooled_b = pooled.astype(BF16)
    zp = _bd(pooled_b, wp_ref) + pv[ROW_PB:ROW_PB + 1, :]
    return pooled_b, zp, inv_heads


def _scan_tile(a_ref, b_ref, out_ref, carry, tm, reverse):
    row = lax.broadcasted_iota(jnp.int32, (8, LRU_W), 0)
    nblk = tm // 8

    def local_scan(blk):
        r0 = pl.multiple_of(blk * 8, 8)
        av = a_ref[pl.ds(r0, 8), :]
        bv = b_ref[pl.ds(r0, 8), :]
        for d in (1, 2, 4):
            sh = (8 - d) if reverse else d
            a_s = pltpu.roll(av, sh, 0)
            b_s = pltpu.roll(bv, sh, 0)
            m = (row < 8 - d) if reverse else (row >= d)
            bv = jnp.where(m, av * b_s + bv, bv)
            av = jnp.where(m, av * a_s, av)
        return r0, av, bv

    def step(i, hin):
        local = [local_scan((nblk - 1 - (i * SCAN_UNROLL + j)) if reverse else (i * SCAN_UNROLL + j))
                 for j in range(SCAN_UNROLL)]
        for r0, av, bv in local:
            hv = av * hin + bv
            out_ref[pl.ds(r0, 8), :] = hv
            hin = jnp.broadcast_to(hv[0:1, :] if reverse else hv[7:8, :], (8, LRU_W))
        return hin

    return lax.fori_loop(0, nblk // SCAN_UNROLL, step, carry)


MESH = pl.DeviceIdType.MESH
ANY = pl.BlockSpec(memory_space=pl.ANY)


def _place():
    x, y, c = lax.axis_index("x"), lax.axis_index("y"), lax.axis_index("c")
    chips = [(1 - x, y), (x, 1 - y), (1 - x, 1 - y)]
    return x, y, c, chips


class _Gather:
    def __init__(self, ins, outs, send_sems, recv_sems, local_sems, core_major=False):
        self.ins, self.outs, self.n = ins, outs, len(ins)
        self.send_sems, self.recv_sems, self.local_sems = send_sems, recv_sems, local_sems
        self.core_major = core_major

    @staticmethod
    def scratch(n):
        return [pltpu.SemaphoreType.DMA((8, n)), pltpu.SemaphoreType.DMA((8, n)), pltpu.SemaphoreType.DMA((n,))]

    def _slot(self, a, px, py, pc):
        return self.outs[a].at[4 * pc + 2 * px + py if self.core_major else 4 * px + 2 * py + pc]

    def _half(self, a, h):
        rows = self.ins[a].shape[0]
        if rows % RELAY_SPLIT_ROWS:
            return None if h else (0, rows)
        return (h * (rows // 2), rows // 2)

    def _copy(self, a, k, block, to, src=None, rows=None):
        dst = self._slot(a, *block)
        src = dst if src is None else src
        if rows is not None:
            src, dst = src.at[pl.ds(*rows)], dst.at[pl.ds(*rows)]
        return pltpu.make_async_remote_copy(
            src_ref=src, dst_ref=dst, send_sem=self.send_sems.at[k, a], recv_sem=self.recv_sems.at[k, a],
            device_id=to, device_id_type=MESH)

    def _mine(self, a):
        x, y, c, _ = _place()
        return pltpu.make_async_copy(self.ins[a], self._slot(a, x, y, c), self.local_sems.at[a])

    def _first(self, a):
        x, y, c, chips = _place()
        me = (x, y, c)
        return ([self._copy(a, 0, me, (x, y, 1 - c), src=self.ins[a])]
                + [self._copy(a, 1 + j, me, (*chip, c), src=self.ins[a]) for j, chip in enumerate(chips[:2])])

    def _passed_on(self, a, h):
        x, y, c, chips = _place()
        block = (*chips[h], c)
        out = [self._copy(a, 4 + h, block, (x, y, 1 - c))]
        if self._half(a, h) is not None:
            out.append(self._copy(a, (3, 7)[h], block, (*chips[1 - h], c), rows=self._half(a, h)))
        return out

    def start(self):
        for a in range(self.n):
            self._mine(a).start()
        for a in range(self.n):
            for cp in self._first(a):
                cp.start()

    def relay(self):
        x, y, c, chips = _place()
        for h in range(2):
            for a in range(self.n):
                self._copy(a, 1 + h, (*chips[h], c), (x, y, c)).wait_recv()
                for cp in self._passed_on(a, h):
                    cp.start()

    def finish(self):
        x, y, c, chips = _place()
        me, sibling = (x, y, c), (x, y, 1 - c)
        passed = []
        for a in range(self.n):
            for h in range(2):
                if self._half(a, h) is not None:
                    self._copy(a, (3, 7)[h], (*chips[2], c), me, rows=self._half(a, h)).wait_recv()
            fwd = self._copy(a, 6, (*chips[2], c), sibling)
            fwd.start()
            passed.append(fwd)
        for a in range(self.n):
            self._copy(a, 0, (x, y, 1 - c), me).wait_recv()
            for j, chip in enumerate(chips):
                self._copy(a, 4 + j, (*chip, 1 - c), me).wait_recv()
        for a in range(self.n):
            for cp in self._first(a) + self._passed_on(a, 0) + self._passed_on(a, 1):
                cp.wait_send()
        for cp in passed:
            cp.wait_send()
        for a in range(self.n):
            self._mine(a).wait()


def _half_exchange(arr, name):
    def body(in_ref, out_ref, send_sems, recv_sems, local_sem):
        x, y, c, _ = _place()
        my_chip = 2 * x + y

        def send(j, wait):
            to_me = (c == x) & (y == j // 2) & (c == j % 2)

            @pl.when(to_me)
            def _():
                local = pltpu.make_async_copy(in_ref.at[j], out_ref.at[my_chip], local_sem)
                local.wait() if wait else local.start()

            @pl.when(jnp.logical_not(to_me))
            def _():
                remote = pltpu.make_async_remote_copy(
                    src_ref=in_ref.at[j], dst_ref=out_ref.at[my_chip], send_sem=send_sems.at[j],
                    recv_sem=recv_sems.at[my_chip], device_id=(c, j // 2, j % 2), device_id_type=MESH)
                remote.wait_send() if wait else remote.start()

        for j in range(4):
            send(j, wait=False)
        for j in range(4):
            send(j, wait=True)
        for k in range(4):
            from_me = (k // 2 == x) & (k % 2 == y) & (c == x)

            @pl.when(jnp.logical_not(from_me))
            def _():
                pltpu.make_async_remote_copy(
                    src_ref=in_ref.at[0], dst_ref=out_ref.at[k], send_sem=send_sems.at[0], recv_sem=recv_sems.at[k],
                    device_id=(k // 2, k % 2, x), device_id_type=MESH).wait_recv()

    return pl.pallas_call(
        body, name=name, out_shape=_sds(arr.shape, arr.dtype), in_specs=[ANY], out_specs=ANY,
        scratch_shapes=[pltpu.SemaphoreType.DMA((4,)), pltpu.SemaphoreType.DMA((4,)), pltpu.SemaphoreType.DMA],
    )(arr)


class _ChipExchange:
    def __init__(self, ins, outs, send_sems, recv_sems, local_sems):
        self.ins, self.outs, self.n = ins, outs, len(ins)
        self.send_sems, self.recv_sems, self.local_sems = send_sems, recv_sems, local_sems

    @staticmethod
    def scratch(n):
        return [pltpu.SemaphoreType.DMA((3, n)), pltpu.SemaphoreType.DMA((3, n)), pltpu.SemaphoreType.DMA((n,))]

    def _local(self, a):
        x, y, _, _ = _place()
        me = 2 * x + y
        return pltpu.make_async_copy(self.ins[a].at[me], self.outs[a].at[me], self.local_sems.at[a])

    def _copies(self, a):
        x, y, c, chips = _place()
        me = 2 * x + y
        return [(pltpu.make_async_remote_copy(
                     src_ref=self.ins[a].at[2 * px + py], dst_ref=self.outs[a].at[me],
                     send_sem=self.send_sems.at[k, a], recv_sem=self.recv_sems.at[k, a],
                     device_id=(px, py, c), device_id_type=MESH),
                 pltpu.make_async_remote_copy(
                     src_ref=self.ins[a].at[me], dst_ref=self.outs[a].at[2 * px + py],
                     send_sem=self.send_sems.at[k, a], recv_sem=self.recv_sems.at[k, a],
                     device_id=(px, py, c), device_id_type=MESH))
                for k, (px, py) in enumerate(chips)]

    def start(self):
        for a in range(self.n):
            self._local(a).start()
        for a in range(self.n):
            for send, _ in self._copies(a):
                send.start()

    def finish(self):
        for a in range(self.n):
            for send, recv in self._copies(a):
                send.wait_send()
                recv.wait_recv()
        for a in range(self.n):
            self._local(a).wait()


def _gathering(body, n_steps, n_s, core_major):
    def wrapped(*refs, n_in, n_out):
        ins, sh_in = refs[:n_in], refs[n_in:n_in + n_s]
        outs, sh_out = refs[n_in + n_s:n_in + n_s + n_out], refs[n_in + n_s + n_out:n_in + 2 * n_s + n_out]
        rest = refs[n_in + 2 * n_s + n_out:]
        gather = _Gather(sh_in, sh_out, *rest[len(rest) - 3:], core_major=core_major)
        i = pl.program_id(0)

        @pl.when(i == 0)
        def _():
            gather.start()

        @pl.when(i == n_steps // 2)
        def _():
            gather.relay()

        body(*ins, *outs, *rest[:len(rest) - 3])

        @pl.when(i == n_steps - 1)
        def _():
            gather.finish()

    return wrapped


def _norm_in(x, g_mix, shards, tm):
    T = x.shape[0]
    n_t = T // tm
    n_s = len(shards)

    def norm(x_ref, g_ref, h_ref):
        xv = x_ref[...]
        h_ref[...] = (xv * _rstd(xv) * g_ref[...]).astype(BF16)

    outs = pl.pallas_call(
        functools.partial(_gathering(norm, n_t, n_s, core_major=False), n_in=2, n_out=1), name="norm_in", grid=(n_t,),
        in_specs=[pl.BlockSpec((tm, D_MODEL), lambda i: (i, 0)), pl.BlockSpec((1, D_MODEL), lambda i: (0, 0))]
        + [ANY] * n_s,
        out_specs=[pl.BlockSpec((tm, D_MODEL), lambda i: (i, 0))] + [ANY] * n_s,
        out_shape=[_sds((T, D_MODEL), BF16)] + [_sds((N_DEV,) + a.shape, a.dtype) for a in shards],
        scratch_shapes=_Gather.scratch(n_s),
        compiler_params=pltpu.CompilerParams(dimension_semantics=("arbitrary",)),
    )(x, g_mix, *shards)
    return outs[0], list(outs[1:])


def _mix_in(h1, w_in_t, tm, shards):
    T = h1.shape[0]
    n_t = T // tm
    n_s = len(shards)

    def project(h_ref, w_ref, u_ref):
        u_ref[...] = lax.dot_general(h_ref[...], w_ref[...], NT, preferred_element_type=F32)

    outs = pl.pallas_call(
        functools.partial(_gathering(project, n_t, n_s, core_major=True), n_in=2, n_out=1), name="mix_in", grid=(n_t,),
        in_specs=[pl.BlockSpec((tm, D_MODEL), lambda i: (i, 0)), pl.BlockSpec((D_IN, D_MODEL), lambda i: (0, 0))]
        + [ANY] * n_s,
        out_specs=[pl.BlockSpec((tm, D_IN), lambda i: (i, 0))] + [ANY] * n_s,
        out_shape=[_sds((T, D_IN), F32)] + [_sds((N_DEV,) + a.shape, a.dtype) for a in shards],
        scratch_shapes=_Gather.scratch(n_s),
        compiler_params=pltpu.CompilerParams(dimension_semantics=("arbitrary",)),
    )(h1, w_in_t, *shards)
    return outs[0], list(outs[1:])


def _mixer_fwd(u, x, pv, wa, wx, wp, w_out_b, g_ffn, tm, shards=()):
    T = u.shape[0]
    n_s = len(shards)
    n_t = T // tm

    def body(u_ref, x_ref, pv_ref, wa_in, wx_in, wp_in, wo_ref, gf_ref, *rest):
        sh_in, rest = rest[:n_s], rest[n_s:]
        y_ref, hs_ref, hres_ref, h2_ref, saved_ref = rest[:5]
        sh_out, rest = rest[5:5 + n_s], rest[5 + n_s:]
        e_lru, e_pool, a_s, b_s, hc, wa_ref, wx_ref, wp_ref = rest[:8]
        gather = _Gather(sh_in, sh_out, *rest[8:], core_major=True) if n_s else None
        i = pl.program_id(0)

        @pl.when(i == 0)
        def _():
            if gather:
                gather.start()
            e_lru[pl.ds(0, HALO), :] = jnp.zeros((HALO, LRU_W), F32)
            e_pool[pl.ds(0, HALO), :] = jnp.zeros((HALO, POOL_W), F32)
            hc[...] = jnp.zeros((8, LRU_W), F32)
            _fill_block_diag(wa_ref, wa_in)
            _fill_block_diag(wx_ref, wx_in)
            _fill_block_diag(wp_ref, wp_in)

        if gather:
            @pl.when(i == (2 * n_t) // 3)
            def _():
                gather.relay()

        e_lru[pl.ds(HALO, tm), :] = u_ref[:, 0:LRU_W]
        e_pool[pl.ds(HALO, tm), :] = u_ref[:, 2 * LRU_W:D_IN]
        pv = pv_ref[...]
        xc, r, ig, la = _lru_gates(e_lru, pv, wa_ref, wx_ref, tm)
        for q, val in enumerate((xc, r, ig, la)):
            saved_ref[:, LRU_W * q:LRU_W * (q + 1)] = val
        a, _, mult, _ = _lru_decay(la)
        a_s[...] = a
        b_s[...] = mult * (ig * xc)
        hc[...] = _scan_tile(a_s, b_s, hs_ref, hc[...], tm, reverse=False)
        gl, _ = _gelu_parts(u_ref[:, LRU_W:2 * LRU_W])
        y_lru = hs_ref[...] * gl
        _, zp, _ = _pool_pre(e_pool, pv, wp_ref, tm, i * tm)
        y_pool = zp * pv[ROW_PS:ROW_PS + 1, :]
        yn = jnp.concatenate([y_lru * _rstd(y_lru) * pv[ROW_GL:ROW_GL + 1, :],
                              y_pool * _rstd(y_pool) * pv[ROW_GP:ROW_GP + 1, :]], axis=1).astype(BF16)
        for b in range(N_DEV):
            y_ref[:, 128 * _y_pos(b):128 * (_y_pos(b) + 1)] = yn[:, 128 * b:128 * (b + 1)]
        hr = x_ref[...] + jnp.dot(y_ref[...], wo_ref[...], preferred_element_type=F32)
        hres_ref[...] = hr
        h2_ref[...] = (hr * _rstd(hr) * gf_ref[...]).astype(BF16)
        e_lru[pl.ds(0, HALO), :] = e_lru[pl.ds(tm, HALO), :]
        e_pool[pl.ds(0, HALO), :] = e_pool[pl.ds(tm, HALO), :]

        if gather:
            @pl.when(i == n_t - 1)
            def _():
                gather.finish()

    full = lambda shape: pl.BlockSpec(shape, lambda i: (0,) * len(shape))
    row = lambda w: pl.BlockSpec((tm, w), lambda i: (i, 0))
    outs = pl.pallas_call(
        body, name="mixer_fwd", grid=(n_t,),
        in_specs=[row(D_IN), row(D_MODEL), full((16, LRU_W)), full((8, 64, 64)), full((8, 64, 64)), full((4, 128, 128)),
                  full((D_MODEL, D_MODEL)), full((1, D_MODEL))] + [ANY] * n_s,
        out_specs=[row(D_MODEL), row(LRU_W), row(D_MODEL), row(D_MODEL), row(4 * LRU_W)] + [ANY] * n_s,
        out_shape=[_sds((T, D_MODEL), BF16), _sds((T, LRU_W), F32), _sds((T, D_MODEL), F32), _sds((T, D_MODEL), BF16),
                   _sds((T, 4 * LRU_W), F32)] + [_sds((N_DEV,) + a.shape, a.dtype) for a in shards],
        scratch_shapes=[pltpu.VMEM((HALO + tm, LRU_W), F32), pltpu.VMEM((HALO + tm, POOL_W), F32),
                        pltpu.VMEM((tm, LRU_W), F32), pltpu.VMEM((tm, LRU_W), F32), pltpu.VMEM((8, LRU_W), F32)]
        + [pltpu.VMEM((2, 256, 256), BF16)] * 3 + (_Gather.scratch(n_s) if n_s else []),
        compiler_params=pltpu.CompilerParams(dimension_semantics=("arbitrary",)),
    )(u, x, pv, wa, wx, wp, w_out_b, g_ffn, *shards)
    return outs[0], outs[1], outs[2], outs[3], outs[4], list(outs[5:])


def _ffn_fwd(hres, h2, w1_b, w3_b, w2_b, g_fin, tgt, tm):
    T = hres.shape[0]

    def body(hres_ref, h2_ref, w1_ref, w3_ref, w2_ref, gfin_ref, tgt_ref,
             g_ref, v_ref, ff_ref, d3_ref, loss_ref, dgfin_ref):
        @pl.when(pl.program_id(0) == 0)
        def _():
            loss_ref[...] = jnp.zeros((8, 128), F32)
            dgfin_ref[...] = jnp.zeros((1, D_MODEL), F32)

        h2 = h2_ref[...]
        h3 = hres_ref[...]
        for lo, hi in FF_CHUNKS:
            g = lax.dot_general(h2, w1_ref[lo:hi, :], NT, preferred_element_type=F32)
            v = lax.dot_general(h2, w3_ref[lo:hi, :], NT, preferred_element_type=F32)
            g_ref[:, lo:hi] = g.astype(BF16)
            v_ref[:, lo:hi] = v.astype(BF16)
            ff = ((g * _sigmoid(g)) * v).astype(BF16)
            ff_ref[:, lo:hi] = ff
            h3 = h3 + jnp.dot(ff, w2_ref[lo:hi, :], preferred_element_type=F32)

        rstd = _rstd(h3)
        xh = h3 * rstd
        gfin = gfin_ref[...]
        err = xh * gfin - tgt_ref[...]
        loss_ref[...] += 0.5 * jnp.sum(jnp.mean(err * err, axis=-1, keepdims=True))
        dx, dgain = _rms_bwd(err * (1.0 / D_MODEL), xh, rstd, gfin)
        d3_ref[...] = dx
        dgfin_ref[...] += dgain

    row = lambda w: pl.BlockSpec((tm, w), lambda i: (i, 0))
    const = lambda shape: pl.BlockSpec(shape, lambda i: (0,) * len(shape))
    weight = pl.BlockSpec((D_FF, D_MODEL), lambda i: (0, 0), pipeline_mode=pl.Buffered(1))
    return pl.pallas_call(
        body, name="ffn_fwd", grid=(T // tm,),
        in_specs=[row(D_MODEL), row(D_MODEL), weight, weight, weight, const((1, D_MODEL)), row(D_MODEL)],
        out_specs=[row(D_FF), row(D_FF), row(D_FF), row(D_MODEL), const((8, 128)), const((1, D_MODEL))],
        out_shape=[_sds((T, D_FF), BF16), _sds((T, D_FF), BF16), _sds((T, D_FF), BF16),
                   _sds((T, D_MODEL), F32), _sds((8, 128), F32), _sds((1, D_MODEL), F32)],
        compiler_params=pltpu.CompilerParams(dimension_semantics=("arbitrary",)),
    )(hres, h2, w1_b, w3_b, w2_b, g_fin, tgt)


def _ffn_bwd(d3, g, v, w1_b, w3_b, w2_b, hres, g_ffn, tm):
    T = d3.shape[0]

    def body(d3_ref, g_ref, v_ref, w1_ref, w3_ref, w2_ref, hres_ref, gf_ref, dg_ref, dv_ref, d2_ref, dgffn_ref):
        @pl.when(pl.program_id(0) == 0)
        def _():
            dgffn_ref[...] = jnp.zeros((1, D_MODEL), F32)

        d3 = d3_ref[...]
        d3b = d3.astype(BF16)
        dh2 = jnp.zeros((tm, D_MODEL), F32)
        for lo, hi in FF_CHUNKS:
            dff = lax.dot_general(d3b, w2_ref[lo:hi, :], NT, preferred_element_type=F32)
            gv = g_ref[:, lo:hi].astype(F32)
            vv = v_ref[:, lo:hi].astype(F32)
            sg = _sigmoid(gv)
            sl = gv * sg
            dgb = (dff * vv * (sg * (1.0 + gv * (1.0 - sg)))).astype(BF16)
            dvb = (dff * sl).astype(BF16)
            dg_ref[:, lo:hi] = dgb
            dv_ref[:, lo:hi] = dvb
            dh2 = dh2 + (jnp.dot(dgb, w1_ref[lo:hi, :], preferred_element_type=F32)
                         + jnp.dot(dvb, w3_ref[lo:hi, :], preferred_element_type=F32))

        hr = hres_ref[...]
        rstd = _rstd(hr)
        dx, dgain = _rms_bwd(dh2, hr * rstd, rstd, gf_ref[...])
        d2_ref[...] = d3 + dx
        dgffn_ref[...] += dgain

    row = lambda w: pl.BlockSpec((tm, w), lambda i: (i, 0))
    const = lambda shape: pl.BlockSpec(shape, lambda i: (0,) * len(shape))
    weight = pl.BlockSpec((D_FF, D_MODEL), lambda i: (0, 0), pipeline_mode=pl.Buffered(1))
    return pl.pallas_call(
        body, name="ffn_bwd", grid=(T // tm,),
        in_specs=[row(D_MODEL), row(D_FF), row(D_FF), weight, weight, weight, row(D_MODEL), const((1, D_MODEL))],
        out_specs=[row(D_FF), row(D_FF), row(D_MODEL), const((1, D_MODEL))],
        out_shape=[_sds((T, D_FF), BF16), _sds((T, D_FF), BF16), _sds((T, D_MODEL), F32), _sds((1, D_MODEL), F32)],
        compiler_params=pltpu.CompilerParams(dimension_semantics=("arbitrary",)),
    )(d3, g, v, w1_b, w3_b, w2_b, hres, g_ffn)


def _at_b_pair(a, b, c_arr, name, tk, gather=()):
    T, M = a.shape
    N = b.shape[1]
    hm, n_k = M // 2, T // tk
    n_g = len(gather)

    def body(c_ref, a_ref, b_ref, *rest):
        g_in, o_ref, rest = rest[:n_g], rest[n_g], rest[n_g + 1:]
        g_out, rest = rest[:n_g], rest[n_g:]
        acc, landed, send_sem, recv_sem = rest[:4]
        ag = _Gather(g_in, g_out, *rest[4:]) if n_g else None
        ph, k = pl.program_id(0), pl.program_id(1)

        def hand_over():
            x, y, c, _ = _place()
            return pltpu.make_async_remote_copy(
                src_ref=acc.at[0], dst_ref=landed, send_sem=send_sem, recv_sem=recv_sem,
                device_id=(x, y, 1 - c), device_id_type=MESH)

        if ag:
            @pl.when((ph == 0) & (k == 0))
            def _():
                ag.start()

            @pl.when((ph == 1) & (k == 0))
            def _():
                ag.relay()

        @pl.when(k == 0)
        def _():
            acc[ph] = jnp.zeros((hm, N), F32)

        acc[ph] += lax.dot_general(a_ref[...].astype(BF16), b_ref[...].astype(BF16), TN, preferred_element_type=F32)

        @pl.when((ph == 0) & (k == n_k - 1))
        def _():
            hand_over().start()

        @pl.when((ph == 1) & (k == n_k - 1))
        def _():
            copy = hand_over()
            copy.wait_recv()
            o_ref[...] = (acc[1] + landed[...]).astype(BF16)
            copy.wait_send()
            if ag:
                ag.finish()

    outs = pl.pallas_call(
        body, name=name,
        grid_spec=pltpu.PrefetchScalarGridSpec(
            num_scalar_prefetch=1, grid=(2, n_k),
            in_specs=[pl.BlockSpec((tk, hm), lambda ph, k, c_ref: (k, (ph + 1 - c_ref[0]) % 2)),
                      pl.BlockSpec((tk, N), lambda ph, k, c_ref: (k, 0))] + [ANY] * n_g,
            out_specs=[pl.BlockSpec((hm, N), lambda ph, k, c_ref: (0, 0))] + [ANY] * n_g,
            scratch_shapes=[pltpu.VMEM((2, hm, N), F32), pltpu.VMEM((hm, N), F32),
                            pltpu.SemaphoreType.DMA, pltpu.SemaphoreType.DMA] + (_Gather.scratch(n_g) if n_g else [])),
        out_shape=[_sds((hm, N), BF16)] + [_sds((N_DEV,) + g.shape, g.dtype) for g in gather],
        compiler_params=pltpu.CompilerParams(dimension_semantics=("arbitrary", "arbitrary")),
    )(c_arr, a, b, *gather)
    return (outs[0], list(outs[1:])) if n_g else outs[0]


def _mixer_bwd(d2, u, hs, saved, pv, wa, wx, wp, w_out_b, tm, chip_sums=()):
    T = u.shape[0]
    n_t = T // tm
    n_x = len(chip_sums)

    def body(d2_ref, u_ref, uh_ref, hs_ref, hh_ref, saved_ref, pv_ref, wa_in, wx_in, wp_in, wo_ref, *rest):
        x_in, rest = rest[:n_x], rest[n_x:]
        du_ref, sg_ref = rest[:2]
        x_out, rest = rest[2:2 + n_x], rest[2 + n_x:]
        e_pool, e_h, a_s, b_s, dh_s, mu_s, f_x, f_p, mc, cx, cp = rest[:11]
        wa_ref, wx_ref, wp_ref, vacc_ref, dwa_ref, dwx_ref, dwp_ref = rest[11:18]
        exchange = _ChipExchange(x_in, x_out, *rest[18:]) if n_x else None
        s = pl.program_id(0)
        it = n_t - 1 - s

        @pl.when(s == 0)
        def _():
            if exchange:
                exchange.start()
            mc[...] = jnp.zeros((8, LRU_W), F32)
            cx[...] = jnp.zeros((8, LRU_W), F32)
            cp[...] = jnp.zeros((HALO, POOL_W), F32)
            vacc_ref[...] = jnp.zeros((16, LRU_W), F32)
            dwa_ref[...] = jnp.zeros((2, 256, 256), F32)
            dwx_ref[...] = jnp.zeros((2, 256, 256), F32)
            dwp_ref[...] = jnp.zeros((2, 256, 256), F32)
            _fill_block_diag(wa_ref, wa_in)
            _fill_block_diag(wx_ref, wx_in)
            _fill_block_diag(wp_ref, wp_in)

        first = it == 0
        e_pool[pl.ds(0, HALO), :] = jnp.where(first, 0.0, uh_ref[...])
        e_pool[pl.ds(HALO, tm), :] = u_ref[:, 2 * LRU_W:D_IN]
        e_h[pl.ds(0, 8), :] = jnp.where(first, 0.0, hh_ref[...])
        e_h[pl.ds(8, tm), :] = hs_ref[...]
        pv = pv_ref[...]
        saved = lambda q: saved_ref[:, LRU_W * q:LRU_W * (q + 1)]

        dyn = lax.dot_general(d2_ref[...].astype(BF16), wo_ref[...], NT, preferred_element_type=F32)
        dyn = jnp.concatenate([dyn[:, 128 * _y_pos(b):128 * (_y_pos(b) + 1)] for b in range(N_DEV)], axis=1)

        h = hs_ref[...]
        ug = u_ref[:, LRU_W:2 * LRU_W]
        gl, dgl = _gelu_parts(ug)
        y_lru = h * gl
        rstd_l = _rstd(y_lru)
        dy_lru, d_gain_l = _rms_bwd(dyn[:, 0:LRU_W], y_lru * rstd_l, rstd_l, pv[ROW_GL:ROW_GL + 1, :])
        dh = dy_lru * gl
        du_ref[:, LRU_W:2 * LRU_W] = (dy_lru * h * dgl).astype(BF16)
        a_s[...] = jnp.exp(saved(3))
        b_s[...] = a_s[...] * dh
        dh_s[...] = dh
        mu_s[pl.ds(tm, 8), :] = mc[...]
        mc[...] = _scan_tile(a_s, b_s, mu_s, mc[...], tm, reverse=True)
        xc, r, ig = saved(0), saved(1), saved(2)
        a, om, mult, rmult = _lru_decay(saved(3))
        lam_t = dh_s[...] + mu_s[pl.ds(1, tm), :]
        da = lam_t * e_h[pl.ds(7, tm), :]
        dmult = lam_t * (ig * xc)
        di = lam_t * (mult * xc)
        dxc = lam_t * (mult * ig)
        dla = da * a - jnp.where(om > 1e-12, dmult * ((a * a) * rmult), 0.0)
        dra = (dla * (-LRU_C * _softplus_neg_lambda(pv))) * (r * (1.0 - r))
        dia = di * (ig * (1.0 - ig))
        drab = dra.astype(BF16)
        diab = dia.astype(BF16)
        xcb = xc.astype(BF16)
        dxc = dxc + _bd_t(drab, wa_ref) + _bd_t(diab, wx_ref)
        dwa_ref[...] += _bd_grad(xcb, drab)
        dwx_ref[...] += _bd_grad(xcb, diab)
        sig_neg_lam = _sigmoid(-pv[ROW_LAM:ROW_LAM + 1, :])
        d_lam = jnp.sum(dla * r, axis=0, keepdims=True) * (LRU_C * sig_neg_lam)

        f_x[pl.ds(0, tm), :] = dxc
        f_x[pl.ds(tm, 8), :] = cx[...]
        du_lru = jnp.zeros((tm, LRU_W), F32)
        u_lru = u_ref[:, 0:LRU_W]
        d_cw = []
        for k in range(4):
            later = f_x[pl.ds(3 - k, tm), :]
            du_lru = du_lru + later * pv[ROW_CW + k:ROW_CW + k + 1, :]
            d_cw.append(jnp.sum(later * u_lru, axis=0, keepdims=True))
        du_ref[:, 0:LRU_W] = du_lru.astype(BF16)
        cx[...] = f_x[pl.ds(0, 8), :]

        pooled_b, zp, inv_cnts = _pool_pre(e_pool, pv, wp_ref, tm, it * tm)
        ps = pv[ROW_PS:ROW_PS + 1, :]
        y_pool = zp * ps
        rstd_p = _rstd(y_pool)
        dy_pool, d_gain_p = _rms_bwd(dyn[:, LRU_W:D_MODEL], y_pool * rstd_p, rstd_p, pv[ROW_GP:ROW_GP + 1, :])
        dz = dy_pool * ps
        dzb = dz.astype(BF16)
        dwp_ref[...] += _bd_grad(pooled_b, dzb)
        dpooled = _bd_t(dzb, wp_ref)
        for g, w in enumerate(POOL_WINDOWS):
            f_p[pl.ds(0, tm), pl.ds(128 * g, 128)] = _over_count(dpooled[:, 128 * g:128 * (g + 1)], w, inv_cnts[g])
        f_p[pl.ds(tm, HALO), :] = cp[...]
        for g, w in enumerate(POOL_WINDOWS):
            acc = _window_sum(f_p[:, pl.ds(128 * g, 128)], w, back=False)[0:tm, :]
            du_ref[:, 2 * LRU_W + 128 * g:2 * LRU_W + 128 * (g + 1)] = (
                acc - dpooled[:, 128 * g:128 * (g + 1)]).astype(BF16)
        cp[...] = f_p[pl.ds(0, HALO), :]

        rows = d_cw + [
            jnp.sum(dxc, axis=0, keepdims=True),
            jnp.sum(dra, axis=0, keepdims=True),
            jnp.sum(dia, axis=0, keepdims=True),
            d_lam,
            jnp.sum(dz, axis=0, keepdims=True),
            jnp.sum(dy_pool * zp, axis=0, keepdims=True),
            d_gain_l, d_gain_p,
            jnp.zeros((4, LRU_W), F32),
        ]
        vacc_ref[...] += jnp.concatenate(rows, axis=0)

        @pl.when(s == n_t - 1)
        def _():
            sg_ref[SG_VEC:SG_VEC + 16, :] = vacc_ref[:, 0:256]
            sg_ref[SG_VEC + 16:SG_VEC + 32, :] = vacc_ref[:, 256:512]
            for half in range(2):
                sg_ref[SG_WA + 64 * half:SG_WA + 64 * (half + 1), :] = _diag_pack(dwa_ref[half], 64)
                sg_ref[SG_WX + 64 * half:SG_WX + 64 * (half + 1), :] = _diag_pack(dwx_ref[half], 64)
                sg_ref[SG_WP + 128 * half:SG_WP + 128 * (half + 1), :] = _diag_pack(dwp_ref[half], 128)
            if exchange:
                exchange.finish()

    rev = lambda w: pl.BlockSpec((tm, w), lambda s: (n_t - 1 - s, 0))
    full = lambda shape: pl.BlockSpec(shape, lambda s: (0,) * len(shape))
    outs = pl.pallas_call(
        body, name="mixer_bwd", grid=(n_t,),
        in_specs=[rev(D_MODEL), rev(D_IN),
                  pl.BlockSpec((HALO, POOL_W), lambda s: (jnp.maximum((n_t - 1 - s) * (tm // HALO) - 1, 0), 2)),
                  rev(LRU_W),
                  pl.BlockSpec((8, LRU_W), lambda s: (jnp.maximum((n_t - 1 - s) * (tm // 8) - 1, 0), 0)),
                  rev(4 * LRU_W), full((16, LRU_W)), full((8, 64, 64)), full((8, 64, 64)), full((4, 128, 128)),
                  full((D_MODEL, D_MODEL))] + [ANY] * n_x,
        out_specs=[rev(D_IN), full((SG_ROWS, 256))] + [ANY] * n_x,
        out_shape=[_sds((T, D_IN), BF16), _sds((SG_ROWS, 256), F32)] + [_sds(a.shape, a.dtype) for a in chip_sums],
        scratch_shapes=[pltpu.VMEM((HALO + tm, POOL_W), F32),
                        pltpu.VMEM((8 + tm, LRU_W), F32)] + [pltpu.VMEM((tm, LRU_W), F32)] * 3 + [
                        pltpu.VMEM((tm + 8, LRU_W), F32), pltpu.VMEM((tm + 8, LRU_W), F32),
                        pltpu.VMEM((tm + HALO, POOL_W), F32), pltpu.VMEM((8, LRU_W), F32),
                        pltpu.VMEM((8, LRU_W), F32), pltpu.VMEM((HALO, POOL_W), F32)]
        + [pltpu.VMEM((2, 256, 256), BF16)] * 3 + [pltpu.VMEM((16, LRU_W), F32)] + [pltpu.VMEM((2, 256, 256), F32)] * 3
        + (_ChipExchange.scratch(n_x) if n_x else []),
        compiler_params=pltpu.CompilerParams(dimension_semantics=("arbitrary",)),
    )(d2, u, u, hs, hs, saved, pv, wa, wx, wp, w_out_b, *chip_sums)
    return outs[0], outs[1], list(outs[2:])


def _mix_in_bwd(du, x, d2, w_in_t, g_mix, tm):
    T = x.shape[0]

    def body(du_ref, x_ref, d2_ref, w_ref, g_ref, dx_ref, dg_ref):
        @pl.when(pl.program_id(0) == 0)
        def _():
            dg_ref[...] = jnp.zeros((1, D_MODEL), F32)

        dh = jnp.dot(du_ref[...], w_ref[...], preferred_element_type=F32)
        xv = x_ref[...]
        rstd = _rstd(xv)
        dx, dgain = _rms_bwd(dh, xv * rstd, rstd, g_ref[...])
        dx_ref[...] = d2_ref[...] + dx
        dg_ref[...] += dgain

    row = lambda w: pl.BlockSpec((tm, w), lambda i: (i, 0))
    const = lambda shape: pl.BlockSpec(shape, lambda i: (0,) * len(shape))
    return pl.pallas_call(
        body, name="mix_in_bwd", grid=(T // tm,),
        in_specs=[row(D_IN), row(D_MODEL), row(D_MODEL), const((D_IN, D_MODEL)), const((1, D_MODEL))],
        out_specs=[row(D_MODEL), const((1, D_MODEL))],
        out_shape=[_sds((T, D_MODEL), F32), _sds((1, D_MODEL), F32)],
        compiler_params=pltpu.CompilerParams(dimension_semantics=("arbitrary",)),
    )(du, x, d2, w_in_t, g_mix)


def _adamw(w, g, m, v):
    m = ADAM_B1 * m + (1.0 - ADAM_B1) * g
    v = ADAM_B2 * v + (1.0 - ADAM_B2) * (g * g)
    m_hat = m / (1.0 - ADAM_B1 ** ADAM_STEP)
    v_hat = v / (1.0 - ADAM_B2 ** ADAM_STEP)
    delta = -ADAM_LR * (m_hat / (jnp.sqrt(v_hat) + ADAM_EPS) + ADAM_WD * w)
    return delta, m, v


def _adam_shards(ws, ms, vs, parts):
    n = len(ws)
    n_blk = [w.shape[0] // ADAM_ROWS for w in ws]

    def body(*refs):
        w_refs, m_refs, v_refs, p_refs, outs = (refs[:n], refs[n:2 * n], refs[2 * n:3 * n], refs[3 * n:4 * n],
                                                refs[4 * n:])
        i = pl.program_id(0)
        for a in range(n):
            @pl.when(i < n_blk[a])
            def _(a=a):
                g = p_refs[a][0].astype(F32)
                for j in range(1, 4):
                    g = g + p_refs[a][j].astype(F32)
                delta, new_m, new_v = _adamw(w_refs[a][...], g, m_refs[a][...], v_refs[a][...])
                for kind, val in enumerate((g, delta, new_m, new_v)):
                    outs[4 * a + kind][...] = val

    blk = lambda a: pl.BlockSpec((ADAM_ROWS, D_MODEL), lambda i: (jnp.minimum(i, n_blk[a] - 1), 0))
    part_blk = lambda a: pl.BlockSpec((4, ADAM_ROWS, D_MODEL), lambda i: (0, jnp.minimum(i, n_blk[a] - 1), 0))
    res = pl.pallas_call(
        body, name="adam_shards", grid=(max(n_blk),),
        in_specs=[blk(a) for a in range(n)] * 3 + [part_blk(a) for a in range(n)],
        out_specs=[blk(a) for a in range(n) for _ in range(4)],
        out_shape=[_sds(w.shape, F32) for w in ws for _ in range(4)],
        compiler_params=pltpu.CompilerParams(dimension_semantics=("arbitrary",)),
    )(*ws, *ms, *vs, *parts)
    return [tuple(res[4 * a:4 * a + 4]) for a in range(n)]


SMALL_PARAMS = [("norm_mix_g", (1, D_MODEL)), ("conv_w", (1, 4, 64)), ("conv_b", (1, LRU_W)),
                ("gate_a_w", (1, 8, 64, 64)), ("gate_a_b", (1, LRU_W)), ("gate_x_w", (1, 8, 64, 64)),
                ("gate_x_b", (1, LRU_W)), ("lru_lambda", (1, LRU_W)), ("pool_w", (1, 4, 128, 128)),
                ("pool_b", (1, POOL_W)), ("pool_scale", (1, POOL_W)), ("norm_lru_g", (1, LRU_W)),
                ("norm_pool_g", (1, POOL_W)), ("norm_ffn_g", (1, D_MODEL)), ("final_norm_g", (1, D_MODEL))]
VEC_ROW = dict(conv_b=ROW_CB, gate_a_b=ROW_BA, gate_x_b=ROW_BX, lru_lambda=ROW_LAM, pool_b=ROW_PB, pool_scale=ROW_PS,
               norm_lru_g=ROW_GL, norm_pool_g=ROW_GP)
WHOLE = (Ellipsis,)


def _unpack_mixer_grads(sg, dev):
    vec = jnp.concatenate([sg[SG_VEC:SG_VEC + 16], sg[SG_VEC + 16:SG_VEC + 32]], axis=1)
    out = {nm: [(WHOLE, vec[r:r + 1])] for nm, r in VEC_ROW.items()}
    own = jnp.zeros((4, 64), F32)
    for d in range(N_DEV):
        own = jnp.where(dev == d, vec[ROW_CW:ROW_CW + 4, 64 * d:64 * (d + 1)], own)
    out["conv_w"] = [((0,), own)]
    for nm, row0 in (("gate_a_w", SG_WA), ("gate_x_w", SG_WX)):
        out[nm] = [((0, b), sg[row0 + 64 * (b // 4):row0 + 64 * (b // 4 + 1), 64 * (b % 4):64 * (b % 4 + 1)])
                   for b in range(8)]
    out["pool_w"] = [((0, b), sg[SG_WP + 128 * (b // 2):SG_WP + 128 * (b // 2 + 1), 128 * (b % 2):128 * (b % 2 + 1)])
                     for b in range(4)]
    return out


def _adam_small(parts, w, m, v):
    names = [nm for nm, _ in SMALL_PARAMS]
    n = len(names)

    def body(sg_ref, gm_ref, gf_ref, gn_ref, ls_ref, *rest):
        w_refs, m_refs, v_refs, outs = rest[:n], rest[n:2 * n], rest[2 * n:3 * n], rest[3 * n:]
        dev = 4 * lax.axis_index("x") + 2 * lax.axis_index("y") + lax.axis_index("c")

        def total(ref):
            acc = ref[0]
            for d in range(1, N_DEV):
                acc = acc + ref[d]
            return acc

        pieces = _unpack_mixer_grads(total(sg_ref), dev)
        pieces["norm_mix_g"] = [(WHOLE, total(gm_ref))]
        pieces["norm_ffn_g"] = [(WHOLE, total(gf_ref))]
        pieces["final_norm_g"] = [(WHOLE, total(gn_ref))]
        for i, nm in enumerate(names):
            for idx, g in pieces[nm]:
                delta, new_m, new_v = _adamw(w_refs[i][idx], g, m_refs[i][idx], v_refs[i][idx])
                for kind, val in enumerate((g, delta, new_m, new_v)):
                    outs[4 * i + kind][idx] = val
        outs[4 * n][...] = total(ls_ref)

    shapes = [_sds(shape, F32) for _, shape in SMALL_PARAMS for _ in range(4)] + [_sds((8, 128), F32)]
    res = pl.pallas_call(body, name="adam_small", out_shape=shapes)(
        *parts, *[w[nm] for nm in names], *[m[nm] for nm in names], *[v[nm] for nm in names])
    return {nm: tuple(res[4 * i:4 * i + 4]) for i, nm in enumerate(names)}, res[4 * n][0, 0]


def _vec_rows(conv_w_full, conv_b, ba, bx, lam, pb, ps, gl, gp):
    return jnp.concatenate([conv_w_full, conv_b, ba, bx, lam, pb, ps, gl, gp, jnp.zeros((4, LRU_W), F32)], axis=0)


WEIGHT_ORDER = ['norm_mix_g', 'w_in', 'conv_w', 'conv_b', 'gate_a_w', 'gate_a_b', 'gate_x_w', 'gate_x_b', 'lru_lambda',
                'pool_w', 'pool_b', 'pool_scale', 'norm_lru_g', 'norm_pool_g', 'w_out', 'norm_ffn_g', 'ffn_w1', 'ffn_w3',
                'ffn_w2', 'final_norm_g']


def kernel(x, norm_mix_g, w_in, conv_w, conv_b, gate_a_w, gate_a_b, gate_x_w, gate_x_b, lru_lambda, pool_w, pool_b, pool_scale, norm_lru_g, norm_pool_g, w_out, norm_ffn_g, ffn_w1, ffn_w3, ffn_w2, final_norm_g, loss_target, m_norm_mix_g, m_w_in, m_conv_w, m_conv_b, m_gate_a_w, m_gate_a_b, m_gate_x_w, m_gate_x_b, m_lru_lambda, m_pool_w, m_pool_b, m_pool_scale, m_norm_lru_g, m_norm_pool_g, m_w_out, m_norm_ffn_g, m_ffn_w1, m_ffn_w3, m_ffn_w2, m_final_norm_g, v_norm_mix_g, v_w_in, v_conv_w, v_conv_b, v_gate_a_w, v_gate_a_b, v_gate_x_w, v_gate_x_b, v_lru_lambda, v_pool_w, v_pool_b, v_pool_scale, v_norm_lru_g, v_norm_pool_g, v_w_out, v_norm_ffn_g, v_ffn_w1, v_ffn_w3, v_ffn_w2, v_final_norm_g):
    ac = lax.axis_index("c")
    tm, tmx, tk = 512, 512, 1024
    tm_in = 1024
    xs, tgt = x[0], loss_target[0]
    g_fin = final_norm_g.reshape(1, D_MODEL)
    c_arr = jnp.reshape(ac, (1,)).astype(jnp.int32)

    tr = lambda w: jnp.swapaxes(w[0], 0, 1)
    own = lambda w: w[0]
    bf = lambda a: a.astype(BF16)

    h1, (g_in, g_conv) = _norm_in(xs, norm_mix_g, [bf(tr(w_in)), conv_w[0]], tm_in)
    w_in_t = g_in.reshape(D_IN, D_MODEL)
    u, (g_out,) = _mix_in(h1, w_in_t, tm_in, shards=[bf(own(w_out))])
    conv_w_full = g_conv.transpose(1, 0, 2).reshape(4, LRU_W)
    pv = _vec_rows(conv_w_full, conv_b, gate_a_b, gate_x_b, lru_lambda, pool_b, pool_scale, norm_lru_g, norm_pool_g)
    wa, wx, wp = gate_a_w[0], gate_x_w[0], pool_w[0]
    w_out_b = g_out.reshape(D_MODEL, D_MODEL)
    y, hs, hres, h2, saved, (g_w1, g_w3, g_w2) = _mixer_fwd(
        u, xs, pv, wa, wx, wp, w_out_b, norm_ffn_g, tmx, shards=[bf(tr(ffn_w1)), bf(tr(ffn_w3)), bf(own(ffn_w2))])
    w1_t, w3_t, w2_b = g_w1.reshape(D_FF, D_MODEL), g_w3.reshape(D_FF, D_MODEL), g_w2.reshape(D_FF, D_MODEL)
    g, v, ff, d3, loss_acc, d_gfin = _ffn_fwd(hres, h2, w1_t, w3_t, w2_b, g_fin, tgt, tm)

    dg, dv, d2, d_gffn = _ffn_bwd(d3, g, v, w1_t, w3_t, w2_b, hres, norm_ffn_g, tm)
    chips = lambda a: a.reshape(4, a.shape[0] // 4, a.shape[1])
    early_sums = [chips(_at_b_pair(y, d2, c_arr, "grad_w_out", 2 * tk)), chips(_at_b_pair(dg, h2, c_arr, "grad_w1", 2 * tk)),
                  chips(_at_b_pair(dv, h2, c_arr, "grad_w3", 2 * tk)), chips(_at_b_pair(ff, d3, c_arr, "grad_w2", 2 * tk))]
    du, d_mixer, early_parts = _mixer_bwd(d2, u, hs, saved, pv, wa, wx, wp, w_out_b, tmx, chip_sums=early_sums)
    grad_x, d_gmix = _mix_in_bwd(du, xs, d2, w_in_t, norm_mix_g, tm_in)
    d_win, small_parts = _at_b_pair(du, h1, c_arr, "grad_w_in", 2 * tk,
                                    gather=[d_mixer, d_gmix, d_gffn, d_gfin, loss_acc])
    parts = [_half_exchange(chips(d_win), "grads_to_chips_w_in")] + list(early_parts)

    res = {}
    shard_w = dict(w_in=(w_in, m_w_in, v_w_in, tr), w_out=(w_out, m_w_out, v_w_out, own),
                   ffn_w1=(ffn_w1, m_ffn_w1, v_ffn_w1, tr), ffn_w3=(ffn_w3, m_ffn_w3, v_ffn_w3, tr),
                   ffn_w2=(ffn_w2, m_ffn_w2, v_ffn_w2, own))
    shard_res = _adam_shards([view(w) for w, _, _, view in shard_w.values()],
                             [view(m) for _, m, _, view in shard_w.values()],
                             [view(v) for _, _, v, view in shard_w.values()], parts)
    for (nm, (_, _, _, view)), outs in zip(shard_w.items(), shard_res):
        res[nm] = [(jnp.swapaxes(o, 0, 1) if view is tr else o)[None] for o in outs]

    row = lambda a: a.reshape(1, D_MODEL)
    small = lambda gm, cw, cb, wa_, ba, wx_, bx, lam, pw, pb, ps, gl, gp, gf, gn: dict(
        norm_mix_g=gm, conv_w=cw, conv_b=cb, gate_a_w=wa_, gate_a_b=ba, gate_x_w=wx_, gate_x_b=bx, lru_lambda=lam,
        pool_w=pw, pool_b=pb, pool_scale=ps, norm_lru_g=gl, norm_pool_g=gp, norm_ffn_g=gf, final_norm_g=row(gn))
    small_res, loss = _adam_small(
        small_parts,
        small(norm_mix_g, conv_w, conv_b, gate_a_w, gate_a_b, gate_x_w, gate_x_b, lru_lambda, pool_w, pool_b,
              pool_scale, norm_lru_g, norm_pool_g, norm_ffn_g, final_norm_g),
        small(m_norm_mix_g, m_conv_w, m_conv_b, m_gate_a_w, m_gate_a_b, m_gate_x_w, m_gate_x_b, m_lru_lambda, m_pool_w,
              m_pool_b, m_pool_scale, m_norm_lru_g, m_norm_pool_g, m_norm_ffn_g, m_final_norm_g),
        small(v_norm_mix_g, v_conv_w, v_conv_b, v_gate_a_w, v_gate_a_b, v_gate_x_w, v_gate_x_b, v_lru_lambda, v_pool_w,
              v_pool_b, v_pool_scale, v_norm_lru_g, v_norm_pool_g, v_norm_ffn_g, v_final_norm_g))
    for nm, outs in small_res.items():
        res[nm] = [o.reshape(D_MODEL) for o in outs] if nm == "final_norm_g" else list(outs)

    out = [loss, grad_x[None]]
    for kind in range(4):
        out += [res[nm][kind] for nm in WEIGHT_ORDER]
    return tuple(out)
```

```python
import functools

import jax
import jax.numpy as jnp
from jax import lax
from jax.experimental import pallas as pl
from jax.experimental.pallas import tpu as pltpu

F32 = jnp.float32
BF16 = jnp.bfloat16

D_MODEL = 1024
LRU_W = 512
POOL_W = 512
D_IN = 1536
D_FF = 2816
POOL_WINDOWS = (2, 4, 8, 16)
EPS = 1e-6
LRU_C = 8.0
N_DEV = 8
HALO = 16
SCAN_UNROLL = 8
ADAM_ROWS = 32
FF_CHUNKS = ((0, 1536), (1536, 2816))
RELAY_SPLIT_ROWS = 32

ADAM_LR = 0.001
ADAM_B1 = 0.9
ADAM_B2 = 0.999
ADAM_EPS = 1e-08
ADAM_WD = 0.01
ADAM_STEP = 10

ROW_CW, ROW_CB, ROW_BA, ROW_BX, ROW_LAM, ROW_PB, ROW_PS, ROW_GL, ROW_GP = 0, 4, 5, 6, 7, 8, 9, 10, 11
SG_VEC, SG_WA, SG_WX, SG_WP, SG_ROWS = 0, 32, 160, 288, 544

NT = (((1,), (1,)), ((), ()))
TN = (((0,), (0,)), ((), ()))


def _sds(shape, dtype):
    return jax.ShapeDtypeStruct(shape, dtype)


def _sigmoid(x):
    return 0.5 * jnp.tanh(0.5 * x) + 0.5


def _gelu_parts(x):
    c = 0.7978845608028654
    inner = c * (x + 0.044715 * (x * x * x))
    th = jnp.tanh(inner)
    g = 0.5 * x * (1.0 + th)
    dg = 0.5 * (1.0 + th) + 0.5 * x * (1.0 - th * th) * (c * (1.0 + 3.0 * 0.044715 * (x * x)))
    return g, dg


def _window_sum(ext, w, back):
    n = ext.shape[0]
    s, k = ext, 1
    while k < w:
        s = s + pltpu.roll(s, k if back else n - k, 0)
        k *= 2
    return s


def _rstd(x):
    return lax.rsqrt(jnp.mean(x * x, axis=-1, keepdims=True) + EPS)


def _rms_bwd(dy, xhat, rstd, gain):
    dxh = dy * gain
    dx = rstd * (dxh - xhat * jnp.mean(dxh * xhat, axis=-1, keepdims=True))
    return dx, jnp.sum(dy * xhat, axis=0, keepdims=True)


def _bd(xb, w_ref):
    return jnp.concatenate(
        [jnp.dot(xb[:, :256], w_ref[0], preferred_element_type=F32),
         jnp.dot(xb[:, 256:], w_ref[1], preferred_element_type=F32)], axis=1)


def _bd_t(xb, w_ref):
    return jnp.concatenate(
        [lax.dot_general(xb[:, :256], w_ref[0], NT, preferred_element_type=F32),
         lax.dot_general(xb[:, 256:], w_ref[1], NT, preferred_element_type=F32)], axis=1)


def _bd_grad(xb, db):
    return jnp.stack(
        [lax.dot_general(xb[:, :256], db[:, :256], TN, preferred_element_type=F32),
         lax.dot_general(xb[:, 256:], db[:, 256:], TN, preferred_element_type=F32)], axis=0)


def _fill_block_diag(dst, src_ref):
    n, k, _ = src_ref.shape
    dst[...] = jnp.zeros(dst.shape, BF16)
    for b in range(n):
        p, q = divmod(b, 256 // k)
        dst[p, q * k:(q + 1) * k, q * k:(q + 1) * k] = src_ref[b].astype(BF16)


def _diag_pack(w, k):
    lane = lax.broadcasted_iota(jnp.int32, (k, 256), 1)
    out = w[0:k]
    for q in range(1, 256 // k):
        out = jnp.where(lane >= q * k, w[q * k:(q + 1) * k], out)
    return out


def _y_pos(b):
    return 4 * (b % 2) + b // 2


def _softplus_neg_lambda(pv):
    z = -pv[ROW_LAM:ROW_LAM + 1, :]
    return jnp.maximum(z, 0.0) + jnp.log(1.0 + jnp.exp(-jnp.abs(z)))


def _lru_gates(e_lru, pv, wa_ref, wx_ref, tm):
    xc = pv[ROW_CB:ROW_CB + 1, :]
    for k in range(4):
        xc = xc + e_lru[pl.ds(HALO - 3 + k, tm), :] * pv[ROW_CW + k:ROW_CW + k + 1, :]
    xcb = xc.astype(BF16)
    r = _sigmoid(_bd(xcb, wa_ref) + pv[ROW_BA:ROW_BA + 1, :])
    ig = _sigmoid(_bd(xcb, wx_ref) + pv[ROW_BX:ROW_BX + 1, :])
    return xc, r, ig, (-LRU_C * r) * _softplus_neg_lambda(pv)


def _lru_decay(la):
    a = jnp.exp(la)
    om = -jnp.tanh(la) * (1.0 + a * a)
    omc = jnp.maximum(om, 1e-12)
    rmult = lax.rsqrt(omc)
    return a, om, omc * rmult, rmult


def _over_count(v, w, inv_head):
    return jnp.concatenate([v[0:HALO] * inv_head, v[HALO:] * (1.0 / w)], axis=0)


def _pool_pre(e_pool, pv, wp_ref, tm, t0):
    t_head = t0 + lax.broadcasted_iota(jnp.int32, (HALO, 1), 0)
    parts, inv_heads = [], []
    for g, w in enumerate(POOL_WINDOWS):
        ext = e_pool[:, pl.ds(128 * g, 128)]
        s = _window_sum(ext, w, back=True)[HALO:, :]
        inv_head = 1.0 / jnp.minimum(t_head + 1, w).astype(F32)
        inv_heads.append(inv_head)
        parts.append(_over_count(s, w, inv_head) - ext[HALO:, :])
    pooled = jnp.concatenate(parts, axis=1)
    pooled_b = pooled.astype(BF16)
    zp = _bd(pooled_b, wp_ref) + pv[ROW_PB:ROW_PB + 1, :]
    return pooled_b, zp, inv_heads


def _scan_tile(a_ref, b_ref, out_ref, carry, tm, reverse):
    row = lax.broadcasted_iota(jnp.int32, (8, LRU_W), 0)
    nblk = tm // 8

    def local_scan(blk):
        r0 = pl.multiple_of(blk * 8, 8)
        av = a_ref[pl.ds(r0, 8), :]
        bv = b_ref[pl.ds(r0, 8), :]
        for d in (1, 2, 4):
            sh = (8 - d) if reverse else d
            a_s = pltpu.roll(av, sh, 0)
            b_s = pltpu.roll(bv, sh, 0)
            m = (row < 8 - d) if reverse else (row >= d)
            bv = jnp.where(m, av * b_s + bv, bv)
            av = jnp.where(m, av * a_s, av)
        return r0, av, bv

    def step(i, hin):
        local = [local_scan((nblk - 1 - (i * SCAN_UNROLL + j)) if reverse else (i * SCAN_UNROLL + j))
                 for j in range(SCAN_UNROLL)]
        for r0, av, bv in local:
            hv = av * hin + bv
            out_ref[pl.ds(r0, 8), :] = hv
            hin = jnp.broadcast_to(hv[0:1, :] if reverse else hv[7:8, :], (8, LRU_W))
        return hin

    return lax.fori_loop(0, nblk // SCAN_UNROLL, step, carry)


MESH = pl.DeviceIdType.MESH
ANY = pl.BlockSpec(memory_space=pl.ANY)


def _place():
    x, y, c = lax.axis_index("x"), lax.axis_index("y"), lax.axis_index("c")
    chips = [(1 - x, y), (x, 1 - y), (1 - x, 1 - y)]
    return x, y, c, chips


class _Gather:
    def __init__(self, ins, outs, send_sems, recv_sems, local_sems, core_major=False):
        self.ins, self.outs, self.n = ins, outs, len(ins)
        self.send_sems, self.recv_sems, self.local_sems = send_sems, recv_sems, local_sems
        self.core_major = core_major

    @staticmethod
    def scratch(n):
        return [pltpu.SemaphoreType.DMA((8, n)), pltpu.SemaphoreType.DMA((8, n)), pltpu.SemaphoreType.DMA((n,))]

    def _slot(self, a, px, py, pc):
        return self.outs[a].at[4 * pc + 2 * px + py if self.core_major else 4 * px + 2 * py + pc]

    def _half(self, a, h):
        rows = self.ins[a].shape[0]
        if rows % RELAY_SPLIT_ROWS:
            return None if h else (0, rows)
        return (h * (rows // 2), rows // 2)

    def _copy(self, a, k, block, to, src=None, rows=None):
        dst = self._slot(a, *block)
        src = dst if src is None else src
        if rows is not None:
            src, dst = src.at[pl.ds(*rows)], dst.at[pl.ds(*rows)]
        return pltpu.make_async_remote_copy(
            src_ref=src, dst_ref=dst, send_sem=self.send_sems.at[k, a], recv_sem=self.recv_sems.at[k, a],
            device_id=to, device_id_type=MESH)

    def _mine(self, a):
        x, y, c, _ = _place()
        return pltpu.make_async_copy(self.ins[a], self._slot(a, x, y, c), self.local_sems.at[a])

    def _first(self, a):
        x, y, c, chips = _place()
        me = (x, y, c)
        return ([self._copy(a, 0, me, (x, y, 1 - c), src=self.ins[a])]
                + [self._copy(a, 1 + j, me, (*chip, c), src=self.ins[a]) for j, chip in enumerate(chips[:2])])

    def _passed_on(self, a, h):
        x, y, c, chips = _place()
        block = (*chips[h], c)
        out = [self._copy(a, 4 + h, block, (x, y, 1 - c))]
        if self._half(a, h) is not None:
            out.append(self._copy(a, (3, 7)[h], block, (*chips[1 - h], c), rows=self._half(a, h)))
        return out

    def start(self):
        for a in range(self.n):
            self._mine(a).start()
        for a in range(self.n):
            for cp in self._first(a):
                cp.start()

    def relay(self):
        x, y, c, chips = _place()
        for h in range(2):
            for a in range(self.n):
                self._copy(a, 1 + h, (*chips[h], c), (x, y, c)).wait_recv()
                for cp in self._passed_on(a, h):
                    cp.start()

    def finish(self):
        x, y, c, chips = _place()
        me, sibling = (x, y, c), (x, y, 1 - c)
        passed = []
        for a in range(self.n):
            for h in range(2):
                if self._half(a, h) is not None:
                    self._copy(a, (3, 7)[h], (*chips[2], c), me, rows=self._half(a, h)).wait_recv()
            fwd = self._copy(a, 6, (*chips[2], c), sibling)
            fwd.start()
            passed.append(fwd)
        for a in range(self.n):
            self._copy(a, 0, (x, y, 1 - c), me).wait_recv()
            for j, chip in enumerate(chips):
                self._copy(a, 4 + j, (*chip, 1 - c), me).wait_recv()
        for a in range(self.n):
            for cp in self._first(a) + self._passed_on(a, 0) + self._passed_on(a, 1):
                cp.wait_send()
        for cp in passed:
            cp.wait_send()
        for a in range(self.n):
            self._mine(a).wait()


def _half_exchange(arr, name):
    def body(in_ref, out_ref, send_sems, recv_sems, local_sem):
        x, y, c, _ = _place()
        my_chip = 2 * x + y

        def send(j, wait):
            to_me = (c == x) & (y == j // 2) & (c == j % 2)

            @pl.when(to_me)
            def _():
                local = pltpu.make_async_copy(in_ref.at[j], out_ref.at[my_chip], local_sem)
                local.wait() if wait else local.start()

            @pl.when(jnp.logical_not(to_me))
            def _():
                remote = pltpu.make_async_remote_copy(
                    src_ref=in_ref.at[j], dst_ref=out_ref.at[my_chip], send_sem=send_sems.at[j],
                    recv_sem=recv_sems.at[my_chip], device_id=(c, j // 2, j % 2), device_id_type=MESH)
                remote.wait_send() if wait else remote.start()

        for j in range(4):
            send(j, wait=False)
        for j in range(4):
            send(j, wait=True)
        for k in range(4):
            from_me = (k // 2 == x) & (k % 2 == y) & (c == x)

            @pl.when(jnp.logical_not(from_me))
            def _():
                pltpu.make_async_remote_copy(
                    src_ref=in_ref.at[0], dst_ref=out_ref.at[k], send_sem=send_sems.at[0], recv_sem=recv_sems.at[k],
                    device_id=(k // 2, k % 2, x), device_id_type=MESH).wait_recv()

    return pl.pallas_call(
        body, name=name, out_shape=_sds(arr.shape, arr.dtype), in_specs=[ANY], out_specs=ANY,
        scratch_shapes=[pltpu.SemaphoreType.DMA((4,)), pltpu.SemaphoreType.DMA((4,)), pltpu.SemaphoreType.DMA],
    )(arr)


class _ChipExchange:
    def __init__(self, ins, outs, send_sems, recv_sems, local_sems):
        self.ins, self.outs, self.n = ins, outs, len(ins)
        self.send_sems, self.recv_sems, self.local_sems = send_sems, recv_sems, local_sems

    @staticmethod
    def scratch(n):
        return [pltpu.SemaphoreType.DMA((3, n)), pltpu.SemaphoreType.DMA((3, n)), pltpu.SemaphoreType.DMA((n,))]

    def _local(self, a):
        x, y, _, _ = _place()
        me = 2 * x + y
        return pltpu.make_async_copy(self.ins[a].at[me], self.outs[a].at[me], self.local_sems.at[a])

    def _copies(self, a):
        x, y, c, chips = _place()
        me = 2 * x + y
        return [(pltpu.make_async_remote_copy(
                     src_ref=self.ins[a].at[2 * px + py], dst_ref=self.outs[a].at[me],
                     send_sem=self.send_sems.at[k, a], recv_sem=self.recv_sems.at[k, a],
                     device_id=(px, py, c), device_id_type=MESH),
                 pltpu.make_async_remote_copy(
                     src_ref=self.ins[a].at[me], dst_ref=self.outs[a].at[2 * px + py],
                     send_sem=self.send_sems.at[k, a], recv_sem=self.recv_sems.at[k, a],
                     device_id=(px, py, c), device_id_type=MESH))
                for k, (px, py) in enumerate(chips)]

    def start(self):
        for a in range(self.n):
            self._local(a).start()
        for a in range(self.n):
            for send, _ in self._copies(a):
                send.start()

    def finish(self):
        for a in range(self.n):
            for send, recv in self._copies(a):
                send.wait_send()
                recv.wait_recv()
        for a in range(self.n):
            self._local(a).wait()


def _gathering(body, n_steps, n_s, core_major):
    def wrapped(*refs, n_in, n_out):
        ins, sh_in = refs[:n_in], refs[n_in:n_in + n_s]
        outs, sh_out = refs[n_in + n_s:n_in + n_s + n_out], refs[n_in + n_s + n_out:n_in + 2 * n_s + n_out]
        rest = refs[n_in + 2 * n_s + n_out:]
        gather = _Gather(sh_in, sh_out, *rest[len(rest) - 3:], core_major=core_major)
        i = pl.program_id(0)

        @pl.when(i == 0)
        def _():
            gather.start()

        @pl.when(i == n_steps // 2)
        def _():
            gather.relay()

        body(*ins, *outs, *rest[:len(rest) - 3])

        @pl.when(i == n_steps - 1)
        def _():
            gather.finish()

    return wrapped


def _norm_in(x, g_mix, shards, tm):
    T = x.shape[0]
    n_t = T // tm
    n_s = len(shards)

    def norm(x_ref, g_ref, h_ref):
        xv = x_ref[...]
        h_ref[...] = (xv * _rstd(xv) * g_ref[...]).astype(BF16)

    outs = pl.pallas_call(
        functools.partial(_gathering(norm, n_t, n_s, core_major=False), n_in=2, n_out=1), name="norm_in", grid=(n_t,),
        in_specs=[pl.BlockSpec((tm, D_MODEL), lambda i: (i, 0)), pl.BlockSpec((1, D_MODEL), lambda i: (0, 0))]
        + [ANY] * n_s,
        out_specs=[pl.BlockSpec((tm, D_MODEL), lambda i: (i, 0))] + [ANY] * n_s,
        out_shape=[_sds((T, D_MODEL), BF16)] + [_sds((N_DEV,) + a.shape, a.dtype) for a in shards],
        scratch_shapes=_Gather.scratch(n_s),
        compiler_params=pltpu.CompilerParams(dimension_semantics=("arbitrary",)),
    )(x, g_mix, *shards)
    return outs[0], list(outs[1:])


def _mix_in(h1, w_in_t, tm, shards):
    T = h1.shape[0]
    n_t = T // tm
    n_s = len(shards)

    def project(h_ref, w_ref, u_ref):
        u_ref[...] = lax.dot_general(h_ref[...], w_ref[...], NT, preferred_element_type=F32)

    outs = pl.pallas_call(
        functools.partial(_gathering(project, n_t, n_s, core_major=True), n_in=2, n_out=1), name="mix_in", grid=(n_t,),
        in_specs=[pl.BlockSpec((tm, D_MODEL), lambda i: (i, 0)), pl.BlockSpec((D_IN, D_MODEL), lambda i: (0, 0))]
        + [ANY] * n_s,
        out_specs=[pl.BlockSpec((tm, D_IN), lambda i: (i, 0))] + [ANY] * n_s,
        out_shape=[_sds((T, D_IN), F32)] + [_sds((N_DEV,) + a.shape, a.dtype) for a in shards],
        scratch_shapes=_Gather.scratch(n_s),
        compiler_params=pltpu.CompilerParams(dimension_semantics=("arbitrary",)),
    )(h1, w_in_t, *shards)
    return outs[0], list(outs[1:])


def _mixer_fwd(u, x, pv, wa, wx, wp, w_out_b, g_ffn, tm, shards=()):
    T = u.shape[0]
    n_s = len(shards)
    n_t = T // tm

    def body(u_ref, x_ref, pv_ref, wa_in, wx_in, wp_in, wo_ref, gf_ref, *rest):
        sh_in, rest = rest[:n_s], rest[n_s:]
        y_ref, hs_ref, hres_ref, h2_ref, saved_ref = rest[:5]
        sh_out, rest = rest[5:5 + n_s], rest[5 + n_s:]
        e_lru, e_pool, a_s, b_s, hc, wa_ref, wx_ref, wp_ref = rest[:8]
        gather = _Gather(sh_in, sh_out, *rest[8:], core_major=True) if n_s else None
        i = pl.program_id(0)

        @pl.when(i == 0)
        def _():
            if gather:
                gather.start()
            e_lru[pl.ds(0, HALO), :] = jnp.zeros((HALO, LRU_W), F32)
            e_pool[pl.ds(0, HALO), :] = jnp.zeros((HALO, POOL_W), F32)
            hc[...] = jnp.zeros((8, LRU_W), F32)
            _fill_block_diag(wa_ref, wa_in)
            _fill_block_diag(wx_ref, wx_in)
            _fill_block_diag(wp_ref, wp_in)

        if gather:
            @pl.when(i == (2 * n_t) // 3)
            def _():
                gather.relay()

        e_lru[pl.ds(HALO, tm), :] = u_ref[:, 0:LRU_W]
        e_pool[pl.ds(HALO, tm), :] = u_ref[:, 2 * LRU_W:D_IN]
        pv = pv_ref[...]
        xc, r, ig, la = _lru_gates(e_lru, pv, wa_ref, wx_ref, tm)
        for q, val in enumerate((xc, r, ig, la)):
            saved_ref[:, LRU_W * q:LRU_W * (q + 1)] = val
        a, _, mult, _ = _lru_decay(la)
        a_s[...] = a
        b_s[...] = mult * (ig * xc)
        hc[...] = _scan_tile(a_s, b_s, hs_ref, hc[...], tm, reverse=False)
        gl, _ = _gelu_parts(u_ref[:, LRU_W:2 * LRU_W])
        y_lru = hs_ref[...] * gl
        _, zp, _ = _pool_pre(e_pool, pv, wp_ref, tm, i * tm)
        y_pool = zp * pv[ROW_PS:ROW_PS + 1, :]
        yn = jnp.concatenate([y_lru * _rstd(y_lru) * pv[ROW_GL:ROW_GL + 1, :],
                              y_pool * _rstd(y_pool) * pv[ROW_GP:ROW_GP + 1, :]], axis=1).astype(BF16)
        for b in range(N_DEV):
            y_ref[:, 128 * _y_pos(b):128 * (_y_pos(b) + 1)] = yn[:, 128 * b:128 * (b + 1)]
        hr = x_ref[...] + jnp.dot(y_ref[...], wo_ref[...], preferred_element_type=F32)
        hres_ref[...] = hr
        h2_ref[...] = (hr * _rstd(hr) * gf_ref[...]).astype(BF16)
        e_lru[pl.ds(0, HALO), :] = e_lru[pl.ds(tm, HALO), :]
        e_pool[pl.ds(0, HALO), :] = e_pool[pl.ds(tm, HALO), :]

        if gather:
            @pl.when(i == n_t - 1)
            def _():
                gather.finish()

    full = lambda shape: pl.BlockSpec(shape, lambda i: (0,) * len(shape))
    row = lambda w: pl.BlockSpec((tm, w), lambda i: (i, 0))
    outs = pl.pallas_call(
        body, name="mixer_fwd", grid=(n_t,),
        in_specs=[row(D_IN), row(D_MODEL), full((16, LRU_W)), full((8, 64, 64)), full((8, 64, 64)), full((4, 128, 128)),
                  full((D_MODEL, D_MODEL)), full((1, D_MODEL))] + [ANY] * n_s,
        out_specs=[row(D_MODEL), row(LRU_W), row(D_MODEL), row(D_MODEL), row(4 * LRU_W)] + [ANY] * n_s,
        out_shape=[_sds((T, D_MODEL), BF16), _sds((T, LRU_W), F32), _sds((T, D_MODEL), F32), _sds((T, D_MODEL), BF16),
                   _sds((T, 4 * LRU_W), F32)] + [_sds((N_DEV,) + a.shape, a.dtype) for a in shards],
        scratch_shapes=[pltpu.VMEM((HALO + tm, LRU_W), F32), pltpu.VMEM((HALO + tm, POOL_W), F32),
                        pltpu.VMEM((tm, LRU_W), F32), pltpu.VMEM((tm, LRU_W), F32), pltpu.VMEM((8, LRU_W), F32)]
        + [pltpu.VMEM((2, 256, 256), BF16)] * 3 + (_Gather.scratch(n_s) if n_s else []),
        compiler_params=pltpu.CompilerParams(dimension_semantics=("arbitrary",)),
    )(u, x, pv, wa, wx, wp, w_out_b, g_ffn, *shards)
    return outs[0], outs[1], outs[2], outs[3], outs[4], list(outs[5:])


def _ffn_fwd(hres, h2, w1_b, w3_b, w2_b, g_fin, tgt, tm):
    T = hres.shape[0]

    def body(hres_ref, h2_ref, w1_ref, w3_ref, w2_ref, gfin_ref, tgt_ref,
             g_ref, v_ref, ff_ref, d3_ref, loss_ref, dgfin_ref):
        @pl.when(pl.program_id(0) == 0)
        def _():
            loss_ref[...] = jnp.zeros((8, 128), F32)
            dgfin_ref[...] = jnp.zeros((1, D_MODEL), F32)

        h2 = h2_ref[...]
        h3 = hres_ref[...]
        for lo, hi in FF_CHUNKS:
            g = lax.dot_general(h2, w1_ref[lo:hi, :], NT, preferred_element_type=F32)
            v = lax.dot_general(h2, w3_ref[lo:hi, :], NT, preferred_element_type=F32)
            g_ref[:, lo:hi] = g.astype(BF16)
            v_ref[:, lo:hi] = v.astype(BF16)
            ff = ((g * _sigmoid(g)) * v).astype(BF16)
            ff_ref[:, lo:hi] = ff
            h3 = h3 + jnp.dot(ff, w2_ref[lo:hi, :], preferred_element_type=F32)

        rstd = _rstd(h3)
        xh = h3 * rstd
        gfin = gfin_ref[...]
        err = xh * gfin - tgt_ref[...]
        loss_ref[...] += 0.5 * jnp.sum(jnp.mean(err * err, axis=-1, keepdims=True))
        dx, dgain = _rms_bwd(err * (1.0 / D_MODEL), xh, rstd, gfin)
        d3_ref[...] = dx
        dgfin_ref[...] += dgain

    row = lambda w: pl.BlockSpec((tm, w), lambda i: (i, 0))
    const = lambda shape: pl.BlockSpec(shape, lambda i: (0,) * len(shape))
    weight = pl.BlockSpec((D_FF, D_MODEL), lambda i: (0, 0), pipeline_mode=pl.Buffered(1))
    return pl.pallas_call(
        body, name="ffn_fwd", grid=(T // tm,),
        in_specs=[row(D_MODEL), row(D_MODEL), weight, weight, weight, const((1, D_MODEL)), row(D_MODEL)],
        out_specs=[row(D_FF), row(D_FF), row(D_FF), row(D_MODEL), const((8, 128)), const((1, D_MODEL))],
        out_shape=[_sds((T, D_FF), BF16), _sds((T, D_FF), BF16), _sds((T, D_FF), BF16),
                   _sds((T, D_MODEL), F32), _sds((8, 128), F32), _sds((1, D_MODEL), F32)],
        compiler_params=pltpu.CompilerParams(dimension_semantics=("arbitrary",)),
    )(hres, h2, w1_b, w3_b, w2_b, g_fin, tgt)


def _ffn_bwd(d3, g, v, w1_b, w3_b, w2_b, hres, g_ffn, tm):
    T = d3.shape[0]

    def body(d3_ref, g_ref, v_ref, w1_ref, w3_ref, w2_ref, hres_ref, gf_ref, dg_ref, dv_ref, d2_ref, dgffn_ref):
        @pl.when(pl.program_id(0) == 0)
        def _():
            dgffn_ref[...] = jnp.zeros((1, D_MODEL), F32)

        d3 = d3_ref[...]
        d3b = d3.astype(BF16)
        dh2 = jnp.zeros((tm, D_MODEL), F32)
        for lo, hi in FF_CHUNKS:
            dff = lax.dot_general(d3b, w2_ref[lo:hi, :], NT, preferred_element_type=F32)
            gv = g_ref[:, lo:hi].astype(F32)
            vv = v_ref[:, lo:hi].astype(F32)
            sg = _sigmoid(gv)
            sl = gv * sg
            dgb = (dff * vv * (sg * (1.0 + gv * (1.0 - sg)))).astype(BF16)
            dvb = (dff * sl).astype(BF16)
            dg_ref[:, lo:hi] = dgb
            dv_ref[:, lo:hi] = dvb
            dh2 = dh2 + (jnp.dot(dgb, w1_ref[lo:hi, :], preferred_element_type=F32)
                         + jnp.dot(dvb, w3_ref[lo:hi, :], preferred_element_type=F32))

        hr = hres_ref[...]
        rstd = _rstd(hr)
        dx, dgain = _rms_bwd(dh2, hr * rstd, rstd, gf_ref[...])
        d2_ref[...] = d3 + dx
        dgffn_ref[...] += dgain

    row = lambda w: pl.BlockSpec((tm, w), lambda i: (i, 0))
    const = lambda shape: pl.BlockSpec(shape, lambda i: (0,) * len(shape))
    weight = pl.BlockSpec((D_FF, D_MODEL), lambda i: (0, 0), pipeline_mode=pl.Buffered(1))
    return pl.pallas_call(
        body, name="ffn_bwd", grid=(T // tm,),
        in_specs=[row(D_MODEL), row(D_FF), row(D_FF), weight, weight, weight, row(D_MODEL), const((1, D_MODEL))],
        out_specs=[row(D_FF), row(D_FF), row(D_MODEL), const((1, D_MODEL))],
        out_shape=[_sds((T, D_FF), BF16), _sds((T, D_FF), BF16), _sds((T, D_MODEL), F32), _sds((1, D_MODEL), F32)],
        compiler_params=pltpu.CompilerParams(dimension_semantics=("arbitrary",)),
    )(d3, g, v, w1_b, w3_b, w2_b, hres, g_ffn)


def _at_b_pair(a, b, c_arr, name, tk, gather=(), collective_id=None):
    T, M = a.shape
    N = b.shape[1]
    hm, n_k = M // 2, T // tk
    n_g = len(gather)
    assert bool(n_g) != (collective_id is not None)

    def body(c_ref, a_ref, b_ref, *rest):
        g_in, o_ref, rest = rest[:n_g], rest[n_g], rest[n_g + 1:]
        g_out, rest = rest[:n_g], rest[n_g:]
        acc, landed, send_sem, recv_sem = rest[:4]
        ag = _Gather(g_in, g_out, *rest[4:]) if n_g else None
        ph, k = pl.program_id(0), pl.program_id(1)

        def hand_over():
            x, y, c, _ = _place()
            return pltpu.make_async_remote_copy(
                src_ref=acc.at[0], dst_ref=landed, send_sem=send_sem, recv_sem=recv_sem,
                device_id=(x, y, 1 - c), device_id_type=MESH)

        if ag:
            @pl.when((ph == 0) & (k == 0))
            def _():
                ag.start()

            @pl.when((ph == 1) & (k == 0))
            def _():
                ag.relay()
        else:
            barrier = pltpu.get_barrier_semaphore()

            @pl.when((ph == 0) & (k == 0))
            def _():
                x, y, c, _ = _place()
                pl.semaphore_signal(barrier, inc=1, device_id=(x, y, 1 - c), device_id_type=MESH)

        @pl.when(k == 0)
        def _():
            acc[ph] = jnp.zeros((hm, N), F32)

        acc[ph] += lax.dot_general(a_ref[...].astype(BF16), b_ref[...].astype(BF16), TN, preferred_element_type=F32)

        @pl.when((ph == 0) & (k == n_k - 1))
        def _():
            if not ag:
                pl.semaphore_wait(barrier, 1)
            hand_over().start()

        @pl.when((ph == 1) & (k == n_k - 1))
        def _():
            copy = hand_over()
            copy.wait_recv()
            o_ref[...] = (acc[1] + landed[...]).astype(BF16)
            copy.wait_send()
            if ag:
                ag.finish()

    outs = pl.pallas_call(
        body, name=name,
        grid_spec=pltpu.PrefetchScalarGridSpec(
            num_scalar_prefetch=1, grid=(2, n_k),
            in_specs=[pl.BlockSpec((tk, hm), lambda ph, k, c_ref: (k, (ph + 1 - c_ref[0]) % 2)),
                      pl.BlockSpec((tk, N), lambda ph, k, c_ref: (k, 0))] + [ANY] * n_g,
            out_specs=[pl.BlockSpec((hm, N), lambda ph, k, c_ref: (0, 0))] + [ANY] * n_g,
            scratch_shapes=[pltpu.VMEM((2, hm, N), F32), pltpu.VMEM((hm, N), F32),
                            pltpu.SemaphoreType.DMA, pltpu.SemaphoreType.DMA] + (_Gather.scratch(n_g) if n_g else [])),
        out_shape=[_sds((hm, N), BF16)] + [_sds((N_DEV,) + g.shape, g.dtype) for g in gather],
        compiler_params=pltpu.CompilerParams(dimension_semantics=("arbitrary", "arbitrary"),
                                             collective_id=collective_id),
    )(c_arr, a, b, *gather)
    return (outs[0], list(outs[1:])) if n_g else outs[0]


def _mixer_bwd(d2, u, hs, saved, pv, wa, wx, wp, w_out_b, tm, chip_sums=()):
    T = u.shape[0]
    n_t = T // tm
    n_x = len(chip_sums)

    def body(d2_ref, u_ref, uh_ref, hs_ref, hh_ref, saved_ref, pv_ref, wa_in, wx_in, wp_in, wo_ref, *rest):
        x_in, rest = rest[:n_x], rest[n_x:]
        du_ref, sg_ref = rest[:2]
        x_out, rest = rest[2:2 + n_x], rest[2 + n_x:]
        e_pool, e_h, a_s, b_s, dh_s, mu_s, f_x, f_p, mc, cx, cp = rest[:11]
        wa_ref, wx_ref, wp_ref, vacc_ref, dwa_ref, dwx_ref, dwp_ref = rest[11:18]
        exchange = _ChipExchange(x_in, x_out, *rest[18:]) if n_x else None
        s = pl.program_id(0)
        it = n_t - 1 - s

        @pl.when(s == 0)
        def _():
            if exchange:
                exchange.start()
            mc[...] = jnp.zeros((8, LRU_W), F32)
            cx[...] = jnp.zeros((8, LRU_W), F32)
            cp[...] = jnp.zeros((HALO, POOL_W), F32)
            vacc_ref[...] = jnp.zeros((16, LRU_W), F32)
            dwa_ref[...] = jnp.zeros((2, 256, 256), F32)
            dwx_ref[...] = jnp.zeros((2, 256, 256), F32)
            dwp_ref[...] = jnp.zeros((2, 256, 256), F32)
            _fill_block_diag(wa_ref, wa_in)
            _fill_block_diag(wx_ref, wx_in)
            _fill_block_diag(wp_ref, wp_in)

        first = it == 0
        e_pool[pl.ds(0, HALO), :] = jnp.where(first, 0.0, uh_ref[...])
        e_pool[pl.ds(HALO, tm), :] = u_ref[:, 2 * LRU_W:D_IN]
        e_h[pl.ds(0, 8), :] = jnp.where(first, 0.0, hh_ref[...])
        e_h[pl.ds(8, tm), :] = hs_ref[...]
        pv = pv_ref[...]
        saved = lambda q: saved_ref[:, LRU_W * q:LRU_W * (q + 1)]

        dyn = lax.dot_general(d2_ref[...].astype(BF16), wo_ref[...], NT, preferred_element_type=F32)
        dyn = jnp.concatenate([dyn[:, 128 * _y_pos(b):128 * (_y_pos(b) + 1)] for b in range(N_DEV)], axis=1)

        h = hs_ref[...]
        ug = u_ref[:, LRU_W:2 * LRU_W]
        gl, dgl = _gelu_parts(ug)
        y_lru = h * gl
        rstd_l = _rstd(y_lru)
        dy_lru, d_gain_l = _rms_bwd(dyn[:, 0:LRU_W], y_lru * rstd_l, rstd_l, pv[ROW_GL:ROW_GL + 1, :])
        dh = dy_lru * gl
        du_ref[:, LRU_W:2 * LRU_W] = (dy_lru * h * dgl).astype(BF16)
        a_s[...] = jnp.exp(saved(3))
        b_s[...] = a_s[...] * dh
        dh_s[...] = dh
        mu_s[pl.ds(tm, 8), :] = mc[...]
        mc[...] = _scan_tile(a_s, b_s, mu_s, mc[...], tm, reverse=True)
        xc, r, ig = saved(0), saved(1), saved(2)
        a, om, mult, rmult = _lru_decay(saved(3))
        lam_t = dh_s[...] + mu_s[pl.ds(1, tm), :]
        da = lam_t * e_h[pl.ds(7, tm), :]
        dmult = lam_t * (ig * xc)
        di = lam_t * (mult * xc)
        dxc = lam_t * (mult * ig)
        dla = da * a - jnp.where(om > 1e-12, dmult * ((a * a) * rmult), 0.0)
        dra = (dla * (-LRU_C * _softplus_neg_lambda(pv))) * (r * (1.0 - r))
        dia = di * (ig * (1.0 - ig))
        drab = dra.astype(BF16)
        diab = dia.astype(BF16)
        xcb = xc.astype(BF16)
        dxc = dxc + _bd_t(drab, wa_ref) + _bd_t(diab, wx_ref)
        dwa_ref[...] += _bd_grad(xcb, drab)
        dwx_ref[...] += _bd_grad(xcb, diab)
        sig_neg_lam = _sigmoid(-pv[ROW_LAM:ROW_LAM + 1, :])
        d_lam = jnp.sum(dla * r, axis=0, keepdims=True) * (LRU_C * sig_neg_lam)

        f_x[pl.ds(0, tm), :] = dxc
        f_x[pl.ds(tm, 8), :] = cx[...]
        du_lru = jnp.zeros((tm, LRU_W), F32)
        u_lru = u_ref[:, 0:LRU_W]
        d_cw = []
        for k in range(4):
            later = f_x[pl.ds(3 - k, tm), :]
            du_lru = du_lru + later * pv[ROW_CW + k:ROW_CW + k + 1, :]
            d_cw.append(jnp.sum(later * u_lru, axis=0, keepdims=True))
        du_ref[:, 0:LRU_W] = du_lru.astype(BF16)
        cx[...] = f_x[pl.ds(0, 8), :]

        pooled_b, zp, inv_cnts = _pool_pre(e_pool, pv, wp_ref, tm, it * tm)
        ps = pv[ROW_PS:ROW_PS + 1, :]
        y_pool = zp * ps
        rstd_p = _rstd(y_pool)
        dy_pool, d_gain_p = _rms_bwd(dyn[:, LRU_W:D_MODEL], y_pool * rstd_p, rstd_p, pv[ROW_GP:ROW_GP + 1, :])
        dz = dy_pool * ps
        dzb = dz.astype(BF16)
        dwp_ref[...] += _bd_grad(pooled_b, dzb)
        dpooled = _bd_t(dzb, wp_ref)
        for g, w in enumerate(POOL_WINDOWS):
            f_p[pl.ds(0, tm), pl.ds(128 * g, 128)] = _over_count(dpooled[:, 128 * g:128 * (g + 1)], w, inv_cnts[g])
        f_p[pl.ds(tm, HALO), :] = cp[...]
        for g, w in enumerate(POOL_WINDOWS):
            acc = _window_sum(f_p[:, pl.ds(128 * g, 128)], w, back=False)[0:tm, :]
            du_ref[:, 2 * LRU_W + 128 * g:2 * LRU_W + 128 * (g + 1)] = (
                acc - dpooled[:, 128 * g:128 * (g + 1)]).astype(BF16)
        cp[...] = f_p[pl.ds(0, HALO), :]

        rows = d_cw + [
            jnp.sum(dxc, axis=0, keepdims=True),
            jnp.sum(dra, axis=0, keepdims=True),
            jnp.sum(dia, axis=0, keepdims=True),
            d_lam,
            jnp.sum(dz, axis=0, keepdims=True),
            jnp.sum(dy_pool * zp, axis=0, keepdims=True),
            d_gain_l, d_gain_p,
            jnp.zeros((4, LRU_W), F32),
        ]
        vacc_ref[...] += jnp.concatenate(rows, axis=0)

        @pl.when(s == n_t - 1)
        def _():
            sg_ref[SG_VEC:SG_VEC + 16, :] = vacc_ref[:, 0:256]
            sg_ref[SG_VEC + 16:SG_VEC + 32, :] = vacc_ref[:, 256:512]
            for half in range(2):
                sg_ref[SG_WA + 64 * half:SG_WA + 64 * (half + 1), :] = _diag_pack(dwa_ref[half], 64)
                sg_ref[SG_WX + 64 * half:SG_WX + 64 * (half + 1), :] = _diag_pack(dwx_ref[half], 64)
                sg_ref[SG_WP + 128 * half:SG_WP + 128 * (half + 1), :] = _diag_pack(dwp_ref[half], 128)
            if exchange:
                exchange.finish()

    rev = lambda w: pl.BlockSpec((tm, w), lambda s: (n_t - 1 - s, 0))
    full = lambda shape: pl.BlockSpec(shape, lambda s: (0,) * len(shape))
    outs = pl.pallas_call(
        body, name="mixer_bwd", grid=(n_t,),
        in_specs=[rev(D_MODEL), rev(D_IN),
                  pl.BlockSpec((HALO, POOL_W), lambda s: (jnp.maximum((n_t - 1 - s) * (tm // HALO) - 1, 0), 2)),
                  rev(LRU_W),
                  pl.BlockSpec((8, LRU_W), lambda s: (jnp.maximum((n_t - 1 - s) * (tm // 8) - 1, 0), 0)),
                  rev(4 * LRU_W), full((16, LRU_W)), full((8, 64, 64)), full((8, 64, 64)), full((4, 128, 128)),
                  full((D_MODEL, D_MODEL))] + [ANY] * n_x,
        out_specs=[rev(D_IN), full((SG_ROWS, 256))] + [ANY] * n_x,
        out_shape=[_sds((T, D_IN), BF16), _sds((SG_ROWS, 256), F32)] + [_sds(a.shape, a.dtype) for a in chip_sums],
        scratch_shapes=[pltpu.VMEM((HALO + tm, POOL_W), F32),
                        pltpu.VMEM((8 + tm, LRU_W), F32)] + [pltpu.VMEM((tm, LRU_W), F32)] * 3 + [
                        pltpu.VMEM((tm + 8, LRU_W), F32), pltpu.VMEM((tm + 8, LRU_W), F32),
                        pltpu.VMEM((tm + HALO, POOL_W), F32), pltpu.VMEM((8, LRU_W), F32),
                        pltpu.VMEM((8, LRU_W), F32), pltpu.VMEM((HALO, POOL_W), F32)]
        + [pltpu.VMEM((2, 256, 256), BF16)] * 3 + [pltpu.VMEM((16, LRU_W), F32)] + [pltpu.VMEM((2, 256, 256), F32)] * 3
        + (_ChipExchange.scratch(n_x) if n_x else []),
        compiler_params=pltpu.CompilerParams(dimension_semantics=("arbitrary",)),
    )(d2, u, u, hs, hs, saved, pv, wa, wx, wp, w_out_b, *chip_sums)
    return outs[0], outs[1], list(outs[2:])


def _mix_in_bwd(du, x, d2, w_in_t, g_mix, tm):
    T = x.shape[0]

    def body(du_ref, x_ref, d2_ref, w_ref, g_ref, dx_ref, dg_ref):
        @pl.when(pl.program_id(0) == 0)
        def _():
            dg_ref[...] = jnp.zeros((1, D_MODEL), F32)

        dh = jnp.dot(du_ref[...], w_ref[...], preferred_element_type=F32)
        xv = x_ref[...]
        rstd = _rstd(xv)
        dx, dgain = _rms_bwd(dh, xv * rstd, rstd, g_ref[...])
        dx_ref[...] = d2_ref[...] + dx
        dg_ref[...] += dgain

    row = lambda w: pl.BlockSpec((tm, w), lambda i: (i, 0))
    const = lambda shape: pl.BlockSpec(shape, lambda i: (0,) * len(shape))
    return pl.pallas_call(
        body, name="mix_in_bwd", grid=(T // tm,),
        in_specs=[row(D_IN), row(D_MODEL), row(D_MODEL), const((D_IN, D_MODEL)), const((1, D_MODEL))],
        out_specs=[row(D_MODEL), const((1, D_MODEL))],
        out_shape=[_sds((T, D_MODEL), F32), _sds((1, D_MODEL), F32)],
        compiler_params=pltpu.CompilerParams(dimension_semantics=("arbitrary",)),
    )(du, x, d2, w_in_t, g_mix)


def _adamw(w, g, m, v):
    m = ADAM_B1 * m + (1.0 - ADAM_B1) * g
    v = ADAM_B2 * v + (1.0 - ADAM_B2) * (g * g)
    m_hat = m / (1.0 - ADAM_B1 ** ADAM_STEP)
    v_hat = v / (1.0 - ADAM_B2 ** ADAM_STEP)
    delta = -ADAM_LR * (m_hat / (jnp.sqrt(v_hat) + ADAM_EPS) + ADAM_WD * w)
    return delta, m, v


def _adam_shards(ws, ms, vs, parts):
    n = len(ws)
    n_blk = [w.shape[0] // ADAM_ROWS for w in ws]

    def body(*refs):
        w_refs, m_refs, v_refs, p_refs, outs = (refs[:n], refs[n:2 * n], refs[2 * n:3 * n], refs[3 * n:4 * n],
                                                refs[4 * n:])
        i = pl.program_id(0)
        for a in range(n):
            @pl.when(i < n_blk[a])
            def _(a=a):
                g = p_refs[a][0].astype(F32)
                for j in range(1, 4):
                    g = g + p_refs[a][j].astype(F32)
                delta, new_m, new_v = _adamw(w_refs[a][...], g, m_refs[a][...], v_refs[a][...])
                for kind, val in enumerate((g, delta, new_m, new_v)):
                    outs[4 * a + kind][...] = val

    blk = lambda a: pl.BlockSpec((ADAM_ROWS, D_MODEL), lambda i: (jnp.minimum(i, n_blk[a] - 1), 0))
    part_blk = lambda a: pl.BlockSpec((4, ADAM_ROWS, D_MODEL), lambda i: (0, jnp.minimum(i, n_blk[a] - 1), 0))
    res = pl.pallas_call(
        body, name="adam_shards", grid=(max(n_blk),),
        in_specs=[blk(a) for a in range(n)] * 3 + [part_blk(a) for a in range(n)],
        out_specs=[blk(a) for a in range(n) for _ in range(4)],
        out_shape=[_sds(w.shape, F32) for w in ws for _ in range(4)],
        compiler_params=pltpu.CompilerParams(dimension_semantics=("arbitrary",)),
    )(*ws, *ms, *vs, *parts)
    return [tuple(res[4 * a:4 * a + 4]) for a in range(n)]


SMALL_PARAMS = [("norm_mix_g", (1, D_MODEL)), ("conv_w", (1, 4, 64)), ("conv_b", (1, LRU_W)),
                ("gate_a_w", (1, 8, 64, 64)), ("gate_a_b", (1, LRU_W)), ("gate_x_w", (1, 8, 64, 64)),
                ("gate_x_b", (1, LRU_W)), ("lru_lambda", (1, LRU_W)), ("pool_w", (1, 4, 128, 128)),
                ("pool_b", (1, POOL_W)), ("pool_scale", (1, POOL_W)), ("norm_lru_g", (1, LRU_W)),
                ("norm_pool_g", (1, POOL_W)), ("norm_ffn_g", (1, D_MODEL)), ("final_norm_g", (1, D_MODEL))]
VEC_ROW = dict(conv_b=ROW_CB, gate_a_b=ROW_BA, gate_x_b=ROW_BX, lru_lambda=ROW_LAM, pool_b=ROW_PB, pool_scale=ROW_PS,
               norm_lru_g=ROW_GL, norm_pool_g=ROW_GP)
WHOLE = (Ellipsis,)


def _unpack_mixer_grads(sg, dev):
    vec = jnp.concatenate([sg[SG_VEC:SG_VEC + 16], sg[SG_VEC + 16:SG_VEC + 32]], axis=1)
    out = {nm: [(WHOLE, vec[r:r + 1])] for nm, r in VEC_ROW.items()}
    own = jnp.zeros((4, 64), F32)
    for d in range(N_DEV):
        own = jnp.where(dev == d, vec[ROW_CW:ROW_CW + 4, 64 * d:64 * (d + 1)], own)
    out["conv_w"] = [((0,), own)]
    for nm, row0 in (("gate_a_w", SG_WA), ("gate_x_w", SG_WX)):
        out[nm] = [((0, b), sg[row0 + 64 * (b // 4):row0 + 64 * (b // 4 + 1), 64 * (b % 4):64 * (b % 4 + 1)])
                   for b in range(8)]
    out["pool_w"] = [((0, b), sg[SG_WP + 128 * (b // 2):SG_WP + 128 * (b // 2 + 1), 128 * (b % 2):128 * (b % 2 + 1)])
                     for b in range(4)]
    return out


def _adam_small(parts, w, m, v):
    names = [nm for nm, _ in SMALL_PARAMS]
    n = len(names)

    def body(sg_ref, gm_ref, gf_ref, gn_ref, ls_ref, *rest):
        w_refs, m_refs, v_refs, outs = rest[:n], rest[n:2 * n], rest[2 * n:3 * n], rest[3 * n:]
        dev = 4 * lax.axis_index("x") + 2 * lax.axis_index("y") + lax.axis_index("c")

        def total(ref):
            acc = ref[0]
            for d in range(1, N_DEV):
                acc = acc + ref[d]
            return acc

        pieces = _unpack_mixer_grads(total(sg_ref), dev)
        pieces["norm_mix_g"] = [(WHOLE, total(gm_ref))]
        pieces["norm_ffn_g"] = [(WHOLE, total(gf_ref))]
        pieces["final_norm_g"] = [(WHOLE, total(gn_ref))]
        for i, nm in enumerate(names):
            for idx, g in pieces[nm]:
                delta, new_m, new_v = _adamw(w_refs[i][idx], g, m_refs[i][idx], v_refs[i][idx])
                for kind, val in enumerate((g, delta, new_m, new_v)):
                    outs[4 * i + kind][idx] = val
        outs[4 * n][...] = total(ls_ref)

    shapes = [_sds(shape, F32) for _, shape in SMALL_PARAMS for _ in range(4)] + [_sds((8, 128), F32)]
    res = pl.pallas_call(body, name="adam_small", out_shape=shapes)(
        *parts, *[w[nm] for nm in names], *[m[nm] for nm in names], *[v[nm] for nm in names])
    return {nm: tuple(res[4 * i:4 * i + 4]) for i, nm in enumerate(names)}, res[4 * n][0, 0]


def _vec_rows(conv_w_full, conv_b, ba, bx, lam, pb, ps, gl, gp):
    return jnp.concatenate([conv_w_full, conv_b, ba, bx, lam, pb, ps, gl, gp, jnp.zeros((4, LRU_W), F32)], axis=0)


WEIGHT_ORDER = ['norm_mix_g', 'w_in', 'conv_w', 'conv_b', 'gate_a_w', 'gate_a_b', 'gate_x_w', 'gate_x_b', 'lru_lambda',
                'pool_w', 'pool_b', 'pool_scale', 'norm_lru_g', 'norm_pool_g', 'w_out', 'norm_ffn_g', 'ffn_w1', 'ffn_w3',
                'ffn_w2', 'final_norm_g']


def kernel(x, norm_mix_g, w_in, conv_w, conv_b, gate_a_w, gate_a_b, gate_x_w, gate_x_b, lru_lambda, pool_w, pool_b, pool_scale, norm_lru_g, norm_pool_g, w_out, norm_ffn_g, ffn_w1, ffn_w3, ffn_w2, final_norm_g, loss_target, m_norm_mix_g, m_w_in, m_conv_w, m_conv_b, m_gate_a_w, m_gate_a_b, m_gate_x_w, m_gate_x_b, m_lru_lambda, m_pool_w, m_pool_b, m_pool_scale, m_norm_lru_g, m_norm_pool_g, m_w_out, m_norm_ffn_g, m_ffn_w1, m_ffn_w3, m_ffn_w2, m_final_norm_g, v_norm_mix_g, v_w_in, v_conv_w, v_conv_b, v_gate_a_w, v_gate_a_b, v_gate_x_w, v_gate_x_b, v_lru_lambda, v_pool_w, v_pool_b, v_pool_scale, v_norm_lru_g, v_norm_pool_g, v_w_out, v_norm_ffn_g, v_ffn_w1, v_ffn_w3, v_ffn_w2, v_final_norm_g):
    ac = lax.axis_index("c")
    tm, tmx, tm_in, tk = 512, 512, 1024, 2048
    xs, tgt = x[0], loss_target[0]
    g_fin = final_norm_g.reshape(1, D_MODEL)
    c_arr = jnp.reshape(ac, (1,)).astype(jnp.int32)

    tr = lambda w: jnp.swapaxes(w[0], 0, 1)
    own = lambda w: w[0]
    bf = lambda a: a.astype(BF16)

    h1, (g_in, g_conv) = _norm_in(xs, norm_mix_g, [bf(tr(w_in)), conv_w[0]], tm_in)
    w_in_t = g_in.reshape(D_IN, D_MODEL)
    u, (g_out,) = _mix_in(h1, w_in_t, tm_in, shards=[bf(own(w_out))])
    conv_w_full = g_conv.transpose(1, 0, 2).reshape(4, LRU_W)
    pv = _vec_rows(conv_w_full, conv_b, gate_a_b, gate_x_b, lru_lambda, pool_b, pool_scale, norm_lru_g, norm_pool_g)
    wa, wx, wp = gate_a_w[0], gate_x_w[0], pool_w[0]
    w_out_b = g_out.reshape(D_MODEL, D_MODEL)
    y, hs, hres, h2, saved, (g_w1, g_w3, g_w2) = _mixer_fwd(
        u, xs, pv, wa, wx, wp, w_out_b, norm_ffn_g, tmx, shards=[bf(tr(ffn_w1)), bf(tr(ffn_w3)), bf(own(ffn_w2))])
    w1_t, w3_t, w2_b = g_w1.reshape(D_FF, D_MODEL), g_w3.reshape(D_FF, D_MODEL), g_w2.reshape(D_FF, D_MODEL)
    g, v, ff, d3, loss_acc, d_gfin = _ffn_fwd(hres, h2, w1_t, w3_t, w2_b, g_fin, tgt, tm)

    dg, dv, d2, d_gffn = _ffn_bwd(d3, g, v, w1_t, w3_t, w2_b, hres, norm_ffn_g, tm)
    chips = lambda a: a.reshape(4, a.shape[0] // 4, a.shape[1])
    early = [(y, d2, "grad_w_out"), (dg, h2, "grad_w1"), (dv, h2, "grad_w3"), (ff, d3, "grad_w2")]
    early_sums = [chips(_at_b_pair(a, b, c_arr, name, tk, collective_id=j)) for j, (a, b, name) in enumerate(early)]
    du, d_mixer, early_parts = _mixer_bwd(d2, u, hs, saved, pv, wa, wx, wp, w_out_b, tmx, chip_sums=early_sums)
    grad_x, d_gmix = _mix_in_bwd(du, xs, d2, w_in_t, norm_mix_g, tm_in)
    d_win, small_parts = _at_b_pair(du, h1, c_arr, "grad_w_in", tk,
                                    gather=[d_mixer, d_gmix, d_gffn, d_gfin, loss_acc])
    parts = [_half_exchange(chips(d_win), "grads_to_chips_w_in")] + list(early_parts)

    res = {}
    shard_w = dict(w_in=(w_in, m_w_in, v_w_in, tr), w_out=(w_out, m_w_out, v_w_out, own),
                   ffn_w1=(ffn_w1, m_ffn_w1, v_ffn_w1, tr), ffn_w3=(ffn_w3, m_ffn_w3, v_ffn_w3, tr),
                   ffn_w2=(ffn_w2, m_ffn_w2, v_ffn_w2, own))
    shard_res = _adam_shards([view(w) for w, _, _, view in shard_w.values()],
                             [view(m) for _, m, _, view in shard_w.values()],
                             [view(v) for _, _, v, view in shard_w.values()], parts)
    for (nm, (_, _, _, view)), outs in zip(shard_w.items(), shard_res):
        res[nm] = [(jnp.swapaxes(o, 0, 1) if view is tr else o)[None] for o in outs]

    row = lambda a: a.reshape(1, D_MODEL)
    small = lambda gm, cw, cb, wa_, ba, wx_, bx, lam, pw, pb, ps, gl, gp, gf, gn: dict(
        norm_mix_g=gm, conv_w=cw, conv_b=cb, gate_a_w=wa_, gate_a_b=ba, gate_x_w=wx_, gate_x_b=bx, lru_lambda=lam,
        pool_w=pw, pool_b=pb, pool_scale=ps, norm_lru_g=gl, norm_pool_g=gp, norm_ffn_g=gf, final_norm_g=row(gn))
    small_res, loss = _adam_small(
        small_parts,
        small(norm_mix_g, conv_w, conv_b, gate_a_w, gate_a_b, gate_x_w, gate_x_b, lru_lambda, pool_w, pool_b,
              pool_scale, norm_lru_g, norm_pool_g, norm_ffn_g, final_norm_g),
        small(m_norm_mix_g, m_conv_w, m_conv_b, m_gate_a_w, m_gate_a_b, m_gate_x_w, m_gate_x_b, m_lru_lambda, m_pool_w,
              m_pool_b, m_pool_scale, m_norm_lru_g, m_norm_pool_g, m_norm_ffn_g, m_final_norm_g),
        small(v_norm_mix_g, v_conv_w, v_conv_b, v_gate_a_w, v_gate_a_b, v_gate_x_w, v_gate_x_b, v_lru_lambda, v_pool_w,
              v_pool_b, v_pool_scale, v_norm_lru_g, v_norm_pool_g, v_norm_ffn_g, v_final_norm_g))
    for nm, outs in small_res.items():
        res[nm] = [o.reshape(D_MODEL) for o in outs] if nm == "final_norm_g" else list(outs)

    out = [loss, grad_x[None]]
    for kind in range(4):
        out += [res[nm][kind] for nm in WEIGHT_ORDER]
    return tuple(out)
```

```python
import functools

import jax
import jax.numpy as jnp
from jax import lax
from jax.experimental import pallas as pl
from jax.experimental.pallas import tpu as pltpu

F32 = jnp.float32
BF16 = jnp.bfloat16

D_MODEL = 1024
LRU_W = 512
POOL_W = 512
D_IN = 1536
D_FF = 2816
POOL_WINDOWS = (2, 4, 8, 16)
EPS = 1e-6
LRU_C = 8.0
N_DEV = 8
HALO = 16
SCAN_UNROLL = 8
ADAM_ROWS = 32
FF_CHUNKS = ((0, 1536), (1536, 2816))
RELAY_SPLIT_ROWS = 32
COLLECTIVE_IDS = {name: j for j, name in enumerate(
    ("grad_w_out", "grad_w1", "grad_w3", "grad_w2", "norm_in", "mix_in", "mixer_fwd", "mixer_bwd", "grad_w_in"))}

ADAM_LR = 0.001
ADAM_B1 = 0.9
ADAM_B2 = 0.999
ADAM_EPS = 1e-08
ADAM_WD = 0.01
ADAM_STEP = 10

ROW_CW, ROW_CB, ROW_BA, ROW_BX, ROW_LAM, ROW_PB, ROW_PS, ROW_GL, ROW_GP = 0, 4, 5, 6, 7, 8, 9, 10, 11
SG_VEC, SG_WA, SG_WX, SG_WP, SG_ROWS = 0, 32, 160, 288, 544

NT = (((1,), (1,)), ((), ()))
TN = (((0,), (0,)), ((), ()))


def _sds(shape, dtype):
    return jax.ShapeDtypeStruct(shape, dtype)


def _sigmoid(x):
    return 0.5 * jnp.tanh(0.5 * x) + 0.5


def _gelu_parts(x):
    c = 0.7978845608028654
    inner = c * (x + 0.044715 * (x * x * x))
    th = jnp.tanh(inner)
    g = 0.5 * x * (1.0 + th)
    dg = 0.5 * (1.0 + th) + 0.5 * x * (1.0 - th * th) * (c * (1.0 + 3.0 * 0.044715 * (x * x)))
    return g, dg


def _window_sum(ext, w, back):
    n = ext.shape[0]
    s, k = ext, 1
    while k < w:
        s = s + pltpu.roll(s, k if back else n - k, 0)
        k *= 2
    return s


def _rstd(x):
    return lax.rsqrt(jnp.mean(x * x, axis=-1, keepdims=True) + EPS)


def _rms_bwd(dy, xhat, rstd, gain):
    dxh = dy * gain
    dx = rstd * (dxh - xhat * jnp.mean(dxh * xhat, axis=-1, keepdims=True))
    return dx, jnp.sum(dy * xhat, axis=0, keepdims=True)


def _bd(xb, w_ref):
    return jnp.concatenate(
        [jnp.dot(xb[:, :256], w_ref[0], preferred_element_type=F32),
         jnp.dot(xb[:, 256:], w_ref[1], preferred_element_type=F32)], axis=1)


def _bd_t(xb, w_ref):
    return jnp.concatenate(
        [lax.dot_general(xb[:, :256], w_ref[0], NT, preferred_element_type=F32),
         lax.dot_general(xb[:, 256:], w_ref[1], NT, preferred_element_type=F32)], axis=1)


def _bd_grad(xb, db):
    return jnp.stack(
        [lax.dot_general(xb[:, :256], db[:, :256], TN, preferred_element_type=F32),
         lax.dot_general(xb[:, 256:], db[:, 256:], TN, preferred_element_type=F32)], axis=0)


def _fill_block_diag(dst, src_ref):
    n, k, _ = src_ref.shape
    dst[...] = jnp.zeros(dst.shape, BF16)
    for b in range(n):
        p, q = divmod(b, 256 // k)
        dst[p, q * k:(q + 1) * k, q * k:(q + 1) * k] = src_ref[b].astype(BF16)


def _diag_pack(w, k):
    lane = lax.broadcasted_iota(jnp.int32, (k, 256), 1)
    out = w[0:k]
    for q in range(1, 256 // k):
        out = jnp.where(lane >= q * k, w[q * k:(q + 1) * k], out)
    return out


def _y_pos(b):
    return 4 * (b % 2) + b // 2


def _softplus_neg_lambda(pv):
    z = -pv[ROW_LAM:ROW_LAM + 1, :]
    return jnp.maximum(z, 0.0) + jnp.log(1.0 + jnp.exp(-jnp.abs(z)))


def _lru_gates(e_lru, pv, wa_ref, wx_ref, tm):
    xc = pv[ROW_CB:ROW_CB + 1, :]
    for k in range(4):
        xc = xc + e_lru[pl.ds(HALO - 3 + k, tm), :] * pv[ROW_CW + k:ROW_CW + k + 1, :]
    xcb = xc.astype(BF16)
    r = _sigmoid(_bd(xcb, wa_ref) + pv[ROW_BA:ROW_BA + 1, :])
    ig = _sigmoid(_bd(xcb, wx_ref) + pv[ROW_BX:ROW_BX + 1, :])
    return xc, r, ig, (-LRU_C * r) * _softplus_neg_lambda(pv)


def _lru_decay(la):
    a = jnp.exp(la)
    om = -jnp.tanh(la) * (1.0 + a * a)
    omc = jnp.maximum(om, 1e-12)
    rmult = lax.rsqrt(omc)
    return a, om, omc * rmult, rmult


def _over_count(v, w, inv_head):
    return jnp.concatenate([v[0:HALO] * inv_head, v[HALO:] * (1.0 / w)], axis=0)


def _pool_pre(e_pool, pv, wp_ref, tm, t0):
    t_head = t0 + lax.broadcasted_iota(jnp.int32, (HALO, 1), 0)
    parts, inv_heads = [], []
    for g, w in enumerate(POOL_WINDOWS):
        ext = e_pool[:, pl.ds(128 * g, 128)]
        s = _window_sum(ext, w, back=True)[HALO:, :]
        inv_head = 1.0 / jnp.minimum(t_head + 1, w).astype(F32)
        inv_heads.append(inv_head)
        parts.append(_over_count(s, w, inv_head) - ext[HALO:, :])
    pooled = jnp.concatenate(parts, axis=1)
    pooled_b = pooled.astype(BF16)
    zp = _bd(pooled_b, wp_ref) + pv[ROW_PB:ROW_PB + 1, :]
    return pooled_b, zp, inv_heads


def _scan_tile(a_ref, b_ref, out_ref, carry, tm, reverse):
    row = lax.broadcasted_iota(jnp.int32, (8, LRU_W), 0)
    nblk = tm // 8

    def local_scan(blk):
        r0 = pl.multiple_of(blk * 8, 8)
        av = a_ref[pl.ds(r0, 8), :]
        bv = b_ref[pl.ds(r0, 8), :]
        for d in (1, 2, 4):
            sh = (8 - d) if reverse else d
            a_s = pltpu.roll(av, sh, 0)
            b_s = pltpu.roll(bv, sh, 0)
            m = (row < 8 - d) if reverse else (row >= d)
            bv = jnp.where(m, av * b_s + bv, bv)
            av = jnp.where(m, av * a_s, av)
        return r0, av, bv

    def step(i, hin):
        local = [local_scan((nblk - 1 - (i * SCAN_UNROLL + j)) if reverse else (i * SCAN_UNROLL + j))
                 for j in range(SCAN_UNROLL)]
        for r0, av, bv in local:
            hv = av * hin + bv
            out_ref[pl.ds(r0, 8), :] = hv
            hin = jnp.broadcast_to(hv[0:1, :] if reverse else hv[7:8, :], (8, LRU_W))
        return hin

    return lax.fori_loop(0, nblk // SCAN_UNROLL, step, carry)


MESH = pl.DeviceIdType.MESH
ANY = pl.BlockSpec(memory_space=pl.ANY)


def _place():
    x, y, c = lax.axis_index("x"), lax.axis_index("y"), lax.axis_index("c")
    chips = [(1 - x, y), (x, 1 - y), (1 - x, 1 - y)]
    return x, y, c, chips


def _meet(peers):
    barrier = pltpu.get_barrier_semaphore()
    for peer in peers:
        pl.semaphore_signal(barrier, inc=1, device_id=peer, device_id_type=MESH)
    pl.semaphore_wait(barrier, len(peers))


class _Gather:
    def __init__(self, ins, outs, send_sems, recv_sems, local_sems, core_major=False):
        self.ins, self.outs, self.n = ins, outs, len(ins)
        self.send_sems, self.recv_sems, self.local_sems = send_sems, recv_sems, local_sems
        self.core_major = core_major

    @staticmethod
    def scratch(n):
        return [pltpu.SemaphoreType.DMA((8, n)), pltpu.SemaphoreType.DMA((8, n)), pltpu.SemaphoreType.DMA((n,))]

    def _slot(self, a, px, py, pc):
        return self.outs[a].at[4 * pc + 2 * px + py if self.core_major else 4 * px + 2 * py + pc]

    def _half(self, a, h):
        rows = self.ins[a].shape[0]
        if rows % RELAY_SPLIT_ROWS:
            return None if h else (0, rows)
        return (h * (rows // 2), rows // 2)

    def _copy(self, a, k, block, to, src=None, rows=None):
        dst = self._slot(a, *block)
        src = dst if src is None else src
        if rows is not None:
            src, dst = src.at[pl.ds(*rows)], dst.at[pl.ds(*rows)]
        return pltpu.make_async_remote_copy(
            src_ref=src, dst_ref=dst, send_sem=self.send_sems.at[k, a], recv_sem=self.recv_sems.at[k, a],
            device_id=to, device_id_type=MESH)

    def _mine(self, a):
        x, y, c, _ = _place()
        return pltpu.make_async_copy(self.ins[a], self._slot(a, x, y, c), self.local_sems.at[a])

    def _first(self, a):
        x, y, c, chips = _place()
        me = (x, y, c)
        return ([self._copy(a, 0, me, (x, y, 1 - c), src=self.ins[a])]
                + [self._copy(a, 1 + j, me, (*chip, c), src=self.ins[a]) for j, chip in enumerate(chips[:2])])

    def _passed_on(self, a, h):
        x, y, c, chips = _place()
        block = (*chips[h], c)
        out = [self._copy(a, 4 + h, block, (x, y, 1 - c))]
        if self._half(a, h) is not None:
            out.append(self._copy(a, (3, 7)[h], block, (*chips[1 - h], c), rows=self._half(a, h)))
        return out

    def start(self):
        x, y, c, chips = _place()
        _meet([(x, y, 1 - c), (*chips[0], c), (*chips[1], c)])
        for a in range(self.n):
            self._mine(a).start()
        for a in range(self.n):
            for cp in self._first(a):
                cp.start()

    def relay(self):
        x, y, c, chips = _place()
        for h in range(2):
            for a in range(self.n):
                self._copy(a, 1 + h, (*chips[h], c), (x, y, c)).wait_recv()
                for cp in self._passed_on(a, h):
                    cp.start()

    def finish(self):
        x, y, c, chips = _place()
        me, sibling = (x, y, c), (x, y, 1 - c)
        passed = []
        for a in range(self.n):
            for h in range(2):
                if self._half(a, h) is not None:
                    self._copy(a, (3, 7)[h], (*chips[2], c), me, rows=self._half(a, h)).wait_recv()
            fwd = self._copy(a, 6, (*chips[2], c), sibling)
            fwd.start()
            passed.append(fwd)
        for a in range(self.n):
            self._copy(a, 0, (x, y, 1 - c), me).wait_recv()
            for j, chip in enumerate(chips):
                self._copy(a, 4 + j, (*chip, 1 - c), me).wait_recv()
        for a in range(self.n):
            for cp in self._first(a) + self._passed_on(a, 0) + self._passed_on(a, 1):
                cp.wait_send()
        for cp in passed:
            cp.wait_send()
        for a in range(self.n):
            self._mine(a).wait()


def _half_exchange(arr, name):
    def body(in_ref, out_ref, send_sems, recv_sems, local_sem):
        x, y, c, _ = _place()
        my_chip = 2 * x + y

        def send(j, wait):
            to_me = (c == x) & (y == j // 2) & (c == j % 2)

            @pl.when(to_me)
            def _():
                local = pltpu.make_async_copy(in_ref.at[j], out_ref.at[my_chip], local_sem)
                local.wait() if wait else local.start()

            @pl.when(jnp.logical_not(to_me))
            def _():
                remote = pltpu.make_async_remote_copy(
                    src_ref=in_ref.at[j], dst_ref=out_ref.at[my_chip], send_sem=send_sems.at[j],
                    recv_sem=recv_sems.at[my_chip], device_id=(c, j // 2, j % 2), device_id_type=MESH)
                remote.wait_send() if wait else remote.start()

        for j in range(4):
            send(j, wait=False)
        for j in range(4):
            send(j, wait=True)
        for k in range(4):
            from_me = (k // 2 == x) & (k % 2 == y) & (c == x)

            @pl.when(jnp.logical_not(from_me))
            def _():
                pltpu.make_async_remote_copy(
                    src_ref=in_ref.at[0], dst_ref=out_ref.at[k], send_sem=send_sems.at[0], recv_sem=recv_sems.at[k],
                    device_id=(k // 2, k % 2, x), device_id_type=MESH).wait_recv()

    return pl.pallas_call(
        body, name=name, out_shape=_sds(arr.shape, arr.dtype), in_specs=[ANY], out_specs=ANY,
        scratch_shapes=[pltpu.SemaphoreType.DMA((4,)), pltpu.SemaphoreType.DMA((4,)), pltpu.SemaphoreType.DMA],
    )(arr)


class _ChipExchange:
    def __init__(self, ins, outs, send_sems, recv_sems, local_sems):
        self.ins, self.outs, self.n = ins, outs, len(ins)
        self.send_sems, self.recv_sems, self.local_sems = send_sems, recv_sems, local_sems

    @staticmethod
    def scratch(n):
        return [pltpu.SemaphoreType.DMA((3, n)), pltpu.SemaphoreType.DMA((3, n)), pltpu.SemaphoreType.DMA((n,))]

    def _local(self, a):
        x, y, _, _ = _place()
        me = 2 * x + y
        return pltpu.make_async_copy(self.ins[a].at[me], self.outs[a].at[me], self.local_sems.at[a])

    def _copies(self, a):
        x, y, c, chips = _place()
        me = 2 * x + y
        return [(pltpu.make_async_remote_copy(
                     src_ref=self.ins[a].at[2 * px + py], dst_ref=self.outs[a].at[me],
                     send_sem=self.send_sems.at[k, a], recv_sem=self.recv_sems.at[k, a],
                     device_id=(px, py, c), device_id_type=MESH),
                 pltpu.make_async_remote_copy(
                     src_ref=self.ins[a].at[me], dst_ref=self.outs[a].at[2 * px + py],
                     send_sem=self.send_sems.at[k, a], recv_sem=self.recv_sems.at[k, a],
                     device_id=(px, py, c), device_id_type=MESH))
                for k, (px, py) in enumerate(chips)]

    def start(self):
        _, _, c, chips = _place()
        _meet([(*chip, c) for chip in chips])
        for a in range(self.n):
            self._local(a).start()
        for a in range(self.n):
            for send, _ in self._copies(a):
                send.start()

    def finish(self):
        for a in range(self.n):
            for send, recv in self._copies(a):
                send.wait_send()
                recv.wait_recv()
        for a in range(self.n):
            self._local(a).wait()


def _gathering(body, n_steps, n_s, core_major):
    def wrapped(*refs, n_in, n_out):
        ins, sh_in = refs[:n_in], refs[n_in:n_in + n_s]
        outs, sh_out = refs[n_in + n_s:n_in + n_s + n_out], refs[n_in + n_s + n_out:n_in + 2 * n_s + n_out]
        rest = refs[n_in + 2 * n_s + n_out:]
        gather = _Gather(sh_in, sh_out, *rest[len(rest) - 3:], core_major=core_major)
        i = pl.program_id(0)

        @pl.when(i == 0)
        def _():
            gather.start()

        @pl.when(i == n_steps // 2)
        def _():
            gather.relay()

        body(*ins, *outs, *rest[:len(rest) - 3])

        @pl.when(i == n_steps - 1)
        def _():
            gather.finish()

    return wrapped


def _norm_in(x, g_mix, shards, tm):
    T = x.shape[0]
    n_t = T // tm
    n_s = len(shards)

    def norm(x_ref, g_ref, h_ref):
        xv = x_ref[...]
        h_ref[...] = (xv * _rstd(xv) * g_ref[...]).astype(BF16)

    outs = pl.pallas_call(
        functools.partial(_gathering(norm, n_t, n_s, core_major=False), n_in=2, n_out=1), name="norm_in", grid=(n_t,),
        in_specs=[pl.BlockSpec((tm, D_MODEL), lambda i: (i, 0)), pl.BlockSpec((1, D_MODEL), lambda i: (0, 0))]
        + [ANY] * n_s,
        out_specs=[pl.BlockSpec((tm, D_MODEL), lambda i: (i, 0))] + [ANY] * n_s,
        out_shape=[_sds((T, D_MODEL), BF16)] + [_sds((N_DEV,) + a.shape, a.dtype) for a in shards],
        scratch_shapes=_Gather.scratch(n_s),
        compiler_params=pltpu.CompilerParams(dimension_semantics=("arbitrary",), collective_id=COLLECTIVE_IDS["norm_in"]),
    )(x, g_mix, *shards)
    return outs[0], list(outs[1:])


def _mix_in(h1, w_in_t, tm, shards):
    T = h1.shape[0]
    n_t = T // tm
    n_s = len(shards)

    def project(h_ref, w_ref, u_ref):
        u_ref[...] = lax.dot_general(h_ref[...], w_ref[...], NT, preferred_element_type=F32)

    outs = pl.pallas_call(
        functools.partial(_gathering(project, n_t, n_s, core_major=True), n_in=2, n_out=1), name="mix_in", grid=(n_t,),
        in_specs=[pl.BlockSpec((tm, D_MODEL), lambda i: (i, 0)), pl.BlockSpec((D_IN, D_MODEL), lambda i: (0, 0))]
        + [ANY] * n_s,
        out_specs=[pl.BlockSpec((tm, D_IN), lambda i: (i, 0))] + [ANY] * n_s,
        out_shape=[_sds((T, D_IN), F32)] + [_sds((N_DEV,) + a.shape, a.dtype) for a in shards],
        scratch_shapes=_Gather.scratch(n_s),
        compiler_params=pltpu.CompilerParams(dimension_semantics=("arbitrary",), collective_id=COLLECTIVE_IDS["mix_in"]),
    )(h1, w_in_t, *shards)
    return outs[0], list(outs[1:])


def _mixer_fwd(u, x, pv, wa, wx, wp, w_out_b, g_ffn, tm, shards=()):
    T = u.shape[0]
    n_s = len(shards)
    n_t = T // tm

    def body(u_ref, x_ref, pv_ref, wa_in, wx_in, wp_in, wo_ref, gf_ref, *rest):
        sh_in, rest = rest[:n_s], rest[n_s:]
        y_ref, hs_ref, hres_ref, h2_ref, saved_ref = rest[:5]
        sh_out, rest = rest[5:5 + n_s], rest[5 + n_s:]
        e_lru, e_pool, a_s, b_s, hc, wa_ref, wx_ref, wp_ref = rest[:8]
        gather = _Gather(sh_in, sh_out, *rest[8:], core_major=True) if n_s else None
        i = pl.program_id(0)

        @pl.when(i == 0)
        def _():
            if gather:
                gather.start()
            e_lru[pl.ds(0, HALO), :] = jnp.zeros((HALO, LRU_W), F32)
            e_pool[pl.ds(0, HALO), :] = jnp.zeros((HALO, POOL_W), F32)
            hc[...] = jnp.zeros((8, LRU_W), F32)
            _fill_block_diag(wa_ref, wa_in)
            _fill_block_diag(wx_ref, wx_in)
            _fill_block_diag(wp_ref, wp_in)

        if gather:
            @pl.when(i == (2 * n_t) // 3)
            def _():
                gather.relay()

        e_lru[pl.ds(HALO, tm), :] = u_ref[:, 0:LRU_W]
        e_pool[pl.ds(HALO, tm), :] = u_ref[:, 2 * LRU_W:D_IN]
        pv = pv_ref[...]
        xc, r, ig, la = _lru_gates(e_lru, pv, wa_ref, wx_ref, tm)
        for q, val in enumerate((xc, r, ig, la)):
            saved_ref[:, LRU_W * q:LRU_W * (q + 1)] = val
        a, _, mult, _ = _lru_decay(la)
        a_s[...] = a
        b_s[...] = mult * (ig * xc)
        hc[...] = _scan_tile(a_s, b_s, hs_ref, hc[...], tm, reverse=False)
        gl, _ = _gelu_parts(u_ref[:, LRU_W:2 * LRU_W])
        y_lru = hs_ref[...] * gl
        _, zp, _ = _pool_pre(e_pool, pv, wp_ref, tm, i * tm)
        y_pool = zp * pv[ROW_PS:ROW_PS + 1, :]
        yn = jnp.concatenate([y_lru * _rstd(y_lru) * pv[ROW_GL:ROW_GL + 1, :],
                              y_pool * _rstd(y_pool) * pv[ROW_GP:ROW_GP + 1, :]], axis=1).astype(BF16)
        for b in range(N_DEV):
            y_ref[:, 128 * _y_pos(b):128 * (_y_pos(b) + 1)] = yn[:, 128 * b:128 * (b + 1)]
        hr = x_ref[...] + jnp.dot(y_ref[...], wo_ref[...], preferred_element_type=F32)
        hres_ref[...] = hr
        h2_ref[...] = (hr * _rstd(hr) * gf_ref[...]).astype(BF16)
        e_lru[pl.ds(0, HALO), :] = e_lru[pl.ds(tm, HALO), :]
        e_pool[pl.ds(0, HALO), :] = e_pool[pl.ds(tm, HALO), :]

        if gather:
            @pl.when(i == n_t - 1)
            def _():
                gather.finish()

    full = lambda shape: pl.BlockSpec(shape, lambda i: (0,) * len(shape))
    row = lambda w: pl.BlockSpec((tm, w), lambda i: (i, 0))
    outs = pl.pallas_call(
        body, name="mixer_fwd", grid=(n_t,),
        in_specs=[row(D_IN), row(D_MODEL), full((16, LRU_W)), full((8, 64, 64)), full((8, 64, 64)), full((4, 128, 128)),
                  full((D_MODEL, D_MODEL)), full((1, D_MODEL))] + [ANY] * n_s,
        out_specs=[row(D_MODEL), row(LRU_W), row(D_MODEL), row(D_MODEL), row(4 * LRU_W)] + [ANY] * n_s,
        out_shape=[_sds((T, D_MODEL), BF16), _sds((T, LRU_W), F32), _sds((T, D_MODEL), F32), _sds((T, D_MODEL), BF16),
                   _sds((T, 4 * LRU_W), F32)] + [_sds((N_DEV,) + a.shape, a.dtype) for a in shards],
        scratch_shapes=[pltpu.VMEM((HALO + tm, LRU_W), F32), pltpu.VMEM((HALO + tm, POOL_W), F32),
                        pltpu.VMEM((tm, LRU_W), F32), pltpu.VMEM((tm, LRU_W), F32), pltpu.VMEM((8, LRU_W), F32)]
        + [pltpu.VMEM((2, 256, 256), BF16)] * 3 + (_Gather.scratch(n_s) if n_s else []),
        compiler_params=pltpu.CompilerParams(dimension_semantics=("arbitrary",),
                                             collective_id=COLLECTIVE_IDS["mixer_fwd"] if n_s else None),
    )(u, x, pv, wa, wx, wp, w_out_b, g_ffn, *shards)
    return outs[0], outs[1], outs[2], outs[3], outs[4], list(outs[5:])


def _ffn_fwd(hres, h2, w1_b, w3_b, w2_b, g_fin, tgt, tm):
    T = hres.shape[0]

    def body(hres_ref, h2_ref, w1_ref, w3_ref, w2_ref, gfin_ref, tgt_ref,
             g_ref, v_ref, ff_ref, d3_ref, loss_ref, dgfin_ref):
        @pl.when(pl.program_id(0) == 0)
        def _():
            loss_ref[...] = jnp.zeros((8, 128), F32)
            dgfin_ref[...] = jnp.zeros((1, D_MODEL), F32)

        h2 = h2_ref[...]
        h3 = hres_ref[...]
        for lo, hi in FF_CHUNKS:
            g = lax.dot_general(h2, w1_ref[lo:hi, :], NT, preferred_element_type=F32)
            v = lax.dot_general(h2, w3_ref[lo:hi, :], NT, preferred_element_type=F32)
            g_ref[:, lo:hi] = g.astype(BF16)
            v_ref[:, lo:hi] = v.astype(BF16)
            ff = ((g * _sigmoid(g)) * v).astype(BF16)
            ff_ref[:, lo:hi] = ff
            h3 = h3 + jnp.dot(ff, w2_ref[lo:hi, :], preferred_element_type=F32)

        rstd = _rstd(h3)
        xh = h3 * rstd
        gfin = gfin_ref[...]
        err = xh * gfin - tgt_ref[...]
        loss_ref[...] += 0.5 * jnp.sum(jnp.mean(err * err, axis=-1, keepdims=True))
        dx, dgain = _rms_bwd(err * (1.0 / D_MODEL), xh, rstd, gfin)
        d3_ref[...] = dx
        dgfin_ref[...] += dgain

    row = lambda w: pl.BlockSpec((tm, w), lambda i: (i, 0))
    const = lambda shape: pl.BlockSpec(shape, lambda i: (0,) * len(shape))
    weight = pl.BlockSpec((D_FF, D_MODEL), lambda i: (0, 0), pipeline_mode=pl.Buffered(1))
    return pl.pallas_call(
        body, name="ffn_fwd", grid=(T // tm,),
        in_specs=[row(D_MODEL), row(D_MODEL), weight, weight, weight, const((1, D_MODEL)), row(D_MODEL)],
        out_specs=[row(D_FF), row(D_FF), row(D_FF), row(D_MODEL), const((8, 128)), const((1, D_MODEL))],
        out_shape=[_sds((T, D_FF), BF16), _sds((T, D_FF), BF16), _sds((T, D_FF), BF16),
                   _sds((T, D_MODEL), F32), _sds((8, 128), F32), _sds((1, D_MODEL), F32)],
        compiler_params=pltpu.CompilerParams(dimension_semantics=("arbitrary",)),
    )(hres, h2, w1_b, w3_b, w2_b, g_fin, tgt)


def _ffn_bwd(d3, g, v, w1_b, w3_b, w2_b, hres, g_ffn, tm):
    T = d3.shape[0]

    def body(d3_ref, g_ref, v_ref, w1_ref, w3_ref, w2_ref, hres_ref, gf_ref, dg_ref, dv_ref, d2_ref, dgffn_ref):
        @pl.when(pl.program_id(0) == 0)
        def _():
            dgffn_ref[...] = jnp.zeros((1, D_MODEL), F32)

        d3 = d3_ref[...]
        d3b = d3.astype(BF16)
        dh2 = jnp.zeros((tm, D_MODEL), F32)
        for lo, hi in FF_CHUNKS:
            dff = lax.dot_general(d3b, w2_ref[lo:hi, :], NT, preferred_element_type=F32)
            gv = g_ref[:, lo:hi].astype(F32)
            vv = v_ref[:, lo:hi].astype(F32)
            sg = _sigmoid(gv)
            sl = gv * sg
            dgb = (dff * vv * (sg * (1.0 + gv * (1.0 - sg)))).astype(BF16)
            dvb = (dff * sl).astype(BF16)
            dg_ref[:, lo:hi] = dgb
            dv_ref[:, lo:hi] = dvb
            dh2 = dh2 + (jnp.dot(dgb, w1_ref[lo:hi, :], preferred_element_type=F32)
                         + jnp.dot(dvb, w3_ref[lo:hi, :], preferred_element_type=F32))

        hr = hres_ref[...]
        rstd = _rstd(hr)
        dx, dgain = _rms_bwd(dh2, hr * rstd, rstd, gf_ref[...])
        d2_ref[...] = d3 + dx
        dgffn_ref[...] += dgain

    row = lambda w: pl.BlockSpec((tm, w), lambda i: (i, 0))
    const = lambda shape: pl.BlockSpec(shape, lambda i: (0,) * len(shape))
    weight = pl.BlockSpec((D_FF, D_MODEL), lambda i: (0, 0), pipeline_mode=pl.Buffered(1))
    return pl.pallas_call(
        body, name="ffn_bwd", grid=(T // tm,),
        in_specs=[row(D_MODEL), row(D_FF), row(D_FF), weight, weight, weight, row(D_MODEL), const((1, D_MODEL))],
        out_specs=[row(D_FF), row(D_FF), row(D_MODEL), const((1, D_MODEL))],
        out_shape=[_sds((T, D_FF), BF16), _sds((T, D_FF), BF16), _sds((T, D_MODEL), F32), _sds((1, D_MODEL), F32)],
        compiler_params=pltpu.CompilerParams(dimension_semantics=("arbitrary",)),
    )(d3, g, v, w1_b, w3_b, w2_b, hres, g_ffn)


def _at_b_pair(a, b, c_arr, name, tk, gather=()):
    T, M = a.shape
    N = b.shape[1]
    hm, n_k = M // 2, T // tk
    n_g = len(gather)

    def body(c_ref, a_ref, b_ref, *rest):
        g_in, o_ref, rest = rest[:n_g], rest[n_g], rest[n_g + 1:]
        g_out, rest = rest[:n_g], rest[n_g:]
        acc, landed, send_sem, recv_sem = rest[:4]
        ag = _Gather(g_in, g_out, *rest[4:]) if n_g else None
        ph, k = pl.program_id(0), pl.program_id(1)

        def hand_over():
            x, y, c, _ = _place()
            return pltpu.make_async_remote_copy(
                src_ref=acc.at[0], dst_ref=landed, send_sem=send_sem, recv_sem=recv_sem,
                device_id=(x, y, 1 - c), device_id_type=MESH)

        if ag:
            @pl.when((ph == 0) & (k == 0))
            def _():
                ag.start()

            @pl.when((ph == 1) & (k == 0))
            def _():
                ag.relay()
        else:
            barrier = pltpu.get_barrier_semaphore()

            @pl.when((ph == 0) & (k == 0))
            def _():
                x, y, c, _ = _place()
                pl.semaphore_signal(barrier, inc=1, device_id=(x, y, 1 - c), device_id_type=MESH)

        @pl.when(k == 0)
        def _():
            acc[ph] = jnp.zeros((hm, N), F32)

        acc[ph] += lax.dot_general(a_ref[...].astype(BF16), b_ref[...].astype(BF16), TN, preferred_element_type=F32)

        @pl.when((ph == 0) & (k == n_k - 1))
        def _():
            if not ag:
                pl.semaphore_wait(barrier, 1)
            hand_over().start()

        @pl.when((ph == 1) & (k == n_k - 1))
        def _():
            copy = hand_over()
            copy.wait_recv()
            o_ref[...] = (acc[1] + landed[...]).astype(BF16)
            copy.wait_send()
            if ag:
                ag.finish()

    outs = pl.pallas_call(
        body, name=name,
        grid_spec=pltpu.PrefetchScalarGridSpec(
            num_scalar_prefetch=1, grid=(2, n_k),
            in_specs=[pl.BlockSpec((tk, hm), lambda ph, k, c_ref: (k, (ph + 1 - c_ref[0]) % 2)),
                      pl.BlockSpec((tk, N), lambda ph, k, c_ref: (k, 0))] + [ANY] * n_g,
            out_specs=[pl.BlockSpec((hm, N), lambda ph, k, c_ref: (0, 0))] + [ANY] * n_g,
            scratch_shapes=[pltpu.VMEM((2, hm, N), F32), pltpu.VMEM((hm, N), F32),
                            pltpu.SemaphoreType.DMA, pltpu.SemaphoreType.DMA] + (_Gather.scratch(n_g) if n_g else [])),
        out_shape=[_sds((hm, N), BF16)] + [_sds((N_DEV,) + g.shape, g.dtype) for g in gather],
        compiler_params=pltpu.CompilerParams(dimension_semantics=("arbitrary", "arbitrary"),
                                             collective_id=COLLECTIVE_IDS[name]),
    )(c_arr, a, b, *gather)
    return (outs[0], list(outs[1:])) if n_g else outs[0]


def _mixer_bwd(d2, u, hs, saved, pv, wa, wx, wp, w_out_b, tm, chip_sums=()):
    T = u.shape[0]
    n_t = T // tm
    n_x = len(chip_sums)

    def body(d2_ref, u_ref, uh_ref, hs_ref, hh_ref, saved_ref, pv_ref, wa_in, wx_in, wp_in, wo_ref, *rest):
        x_in, rest = rest[:n_x], rest[n_x:]
        du_ref, sg_ref = rest[:2]
        x_out, rest = rest[2:2 + n_x], rest[2 + n_x:]
        e_pool, e_h, a_s, b_s, dh_s, mu_s, f_x, f_p, mc, cx, cp = rest[:11]
        wa_ref, wx_ref, wp_ref, vacc_ref, dwa_ref, dwx_ref, dwp_ref = rest[11:18]
        exchange = _ChipExchange(x_in, x_out, *rest[18:]) if n_x else None
        s = pl.program_id(0)
        it = n_t - 1 - s

        @pl.when(s == 0)
        def _():
            if exchange:
                exchange.start()
            mc[...] = jnp.zeros((8, LRU_W), F32)
            cx[...] = jnp.zeros((8, LRU_W), F32)
            cp[...] = jnp.zeros((HALO, POOL_W), F32)
            vacc_ref[...] = jnp.zeros((16, LRU_W), F32)
            dwa_ref[...] = jnp.zeros((2, 256, 256), F32)
            dwx_ref[...] = jnp.zeros((2, 256, 256), F32)
            dwp_ref[...] = jnp.zeros((2, 256, 256), F32)
            _fill_block_diag(wa_ref, wa_in)
            _fill_block_diag(wx_ref, wx_in)
            _fill_block_diag(wp_ref, wp_in)

        first = it == 0
        e_pool[pl.ds(0, HALO), :] = jnp.where(first, 0.0, uh_ref[...])
        e_pool[pl.ds(HALO, tm), :] = u_ref[:, 2 * LRU_W:D_IN]
        e_h[pl.ds(0, 8), :] = jnp.where(first, 0.0, hh_ref[...])
        e_h[pl.ds(8, tm), :] = hs_ref[...]
        pv = pv_ref[...]
        saved = lambda q: saved_ref[:, LRU_W * q:LRU_W * (q + 1)]

        dyn = lax.dot_general(d2_ref[...].astype(BF16), wo_ref[...], NT, preferred_element_type=F32)
        dyn = jnp.concatenate([dyn[:, 128 * _y_pos(b):128 * (_y_pos(b) + 1)] for b in range(N_DEV)], axis=1)

        h = hs_ref[...]
        ug = u_ref[:, LRU_W:2 * LRU_W]
        gl, dgl = _gelu_parts(ug)
        y_lru = h * gl
        rstd_l = _rstd(y_lru)
        dy_lru, d_gain_l = _rms_bwd(dyn[:, 0:LRU_W], y_lru * rstd_l, rstd_l, pv[ROW_GL:ROW_GL + 1, :])
        dh = dy_lru * gl
        du_ref[:, LRU_W:2 * LRU_W] = (dy_lru * h * dgl).astype(BF16)
        a_s[...] = jnp.exp(saved(3))
        b_s[...] = a_s[...] * dh
        dh_s[...] = dh
        mu_s[pl.ds(tm, 8), :] = mc[...]
        mc[...] = _scan_tile(a_s, b_s, mu_s, mc[...], tm, reverse=True)
        xc, r, ig = saved(0), saved(1), saved(2)
        a, om, mult, rmult = _lru_decay(saved(3))
        lam_t = dh_s[...] + mu_s[pl.ds(1, tm), :]
        da = lam_t * e_h[pl.ds(7, tm), :]
        dmult = lam_t * (ig * xc)
        di = lam_t * (mult * xc)
        dxc = lam_t * (mult * ig)
        dla = da * a - jnp.where(om > 1e-12, dmult * ((a * a) * rmult), 0.0)
        dra = (dla * (-LRU_C * _softplus_neg_lambda(pv))) * (r * (1.0 - r))
        dia = di * (ig * (1.0 - ig))
        drab = dra.astype(BF16)
        diab = dia.astype(BF16)
        xcb = xc.astype(BF16)
        dxc = dxc + _bd_t(drab, wa_ref) + _bd_t(diab, wx_ref)
        dwa_ref[...] += _bd_grad(xcb, drab)
        dwx_ref[...] += _bd_grad(xcb, diab)
        sig_neg_lam = _sigmoid(-pv[ROW_LAM:ROW_LAM + 1, :])
        d_lam = jnp.sum(dla * r, axis=0, keepdims=True) * (LRU_C * sig_neg_lam)

        f_x[pl.ds(0, tm), :] = dxc
        f_x[pl.ds(tm, 8), :] = cx[...]
        du_lru = jnp.zeros((tm, LRU_W), F32)
        u_lru = u_ref[:, 0:LRU_W]
        d_cw = []
        for k in range(4):
            later = f_x[pl.ds(3 - k, tm), :]
            du_lru = du_lru + later * pv[ROW_CW + k:ROW_CW + k + 1, :]
            d_cw.append(jnp.sum(later * u_lru, axis=0, keepdims=True))
        du_ref[:, 0:LRU_W] = du_lru.astype(BF16)
        cx[...] = f_x[pl.ds(0, 8), :]

        pooled_b, zp, inv_cnts = _pool_pre(e_pool, pv, wp_ref, tm, it * tm)
        ps = pv[ROW_PS:ROW_PS + 1, :]
        y_pool = zp * ps
        rstd_p = _rstd(y_pool)
        dy_pool, d_gain_p = _rms_bwd(dyn[:, LRU_W:D_MODEL], y_pool * rstd_p, rstd_p, pv[ROW_GP:ROW_GP + 1, :])
        dz = dy_pool * ps
        dzb = dz.astype(BF16)
        dwp_ref[...] += _bd_grad(pooled_b, dzb)
        dpooled = _bd_t(dzb, wp_ref)
        for g, w in enumerate(POOL_WINDOWS):
            f_p[pl.ds(0, tm), pl.ds(128 * g, 128)] = _over_count(dpooled[:, 128 * g:128 * (g + 1)], w, inv_cnts[g])
        f_p[pl.ds(tm, HALO), :] = cp[...]
        for g, w in enumerate(POOL_WINDOWS):
            acc = _window_sum(f_p[:, pl.ds(128 * g, 128)], w, back=False)[0:tm, :]
            du_ref[:, 2 * LRU_W + 128 * g:2 * LRU_W + 128 * (g + 1)] = (
                acc - dpooled[:, 128 * g:128 * (g + 1)]).astype(BF16)
        cp[...] = f_p[pl.ds(0, HALO), :]

        rows = d_cw + [
            jnp.sum(dxc, axis=0, keepdims=True),
            jnp.sum(dra, axis=0, keepdims=True),
            jnp.sum(dia, axis=0, keepdims=True),
            d_lam,
            jnp.sum(dz, axis=0, keepdims=True),
            jnp.sum(dy_pool * zp, axis=0, keepdims=True),
            d_gain_l, d_gain_p,
            jnp.zeros((4, LRU_W), F32),
        ]
        vacc_ref[...] += jnp.concatenate(rows, axis=0)

        @pl.when(s == n_t - 1)
        def _():
            sg_ref[SG_VEC:SG_VEC + 16, :] = vacc_ref[:, 0:256]
            sg_ref[SG_VEC + 16:SG_VEC + 32, :] = vacc_ref[:, 256:512]
            for half in range(2):
                sg_ref[SG_WA + 64 * half:SG_WA + 64 * (half + 1), :] = _diag_pack(dwa_ref[half], 64)
                sg_ref[SG_WX + 64 * half:SG_WX + 64 * (half + 1), :] = _diag_pack(dwx_ref[half], 64)
                sg_ref[SG_WP + 128 * half:SG_WP + 128 * (half + 1), :] = _diag_pack(dwp_ref[half], 128)
            if exchange:
                exchange.finish()

    rev = lambda w: pl.BlockSpec((tm, w), lambda s: (n_t - 1 - s, 0))
    full = lambda shape: pl.BlockSpec(shape, lambda s: (0,) * len(shape))
    outs = pl.pallas_call(
        body, name="mixer_bwd", grid=(n_t,),
        in_specs=[rev(D_MODEL), rev(D_IN),
                  pl.BlockSpec((HALO, POOL_W), lambda s: (jnp.maximum((n_t - 1 - s) * (tm // HALO) - 1, 0), 2)),
                  rev(LRU_W),
                  pl.BlockSpec((8, LRU_W), lambda s: (jnp.maximum((n_t - 1 - s) * (tm // 8) - 1, 0), 0)),
                  rev(4 * LRU_W), full((16, LRU_W)), full((8, 64, 64)), full((8, 64, 64)), full((4, 128, 128)),
                  full((D_MODEL, D_MODEL))] + [ANY] * n_x,
        out_specs=[rev(D_IN), full((SG_ROWS, 256))] + [ANY] * n_x,
        out_shape=[_sds((T, D_IN), BF16), _sds((SG_ROWS, 256), F32)] + [_sds(a.shape, a.dtype) for a in chip_sums],
        scratch_shapes=[pltpu.VMEM((HALO + tm, POOL_W), F32),
                        pltpu.VMEM((8 + tm, LRU_W), F32)] + [pltpu.VMEM((tm, LRU_W), F32)] * 3 + [
                        pltpu.VMEM((tm + 8, LRU_W), F32), pltpu.VMEM((tm + 8, LRU_W), F32),
                        pltpu.VMEM((tm + HALO, POOL_W), F32), pltpu.VMEM((8, LRU_W), F32),
                        pltpu.VMEM((8, LRU_W), F32), pltpu.VMEM((HALO, POOL_W), F32)]
        + [pltpu.VMEM((2, 256, 256), BF16)] * 3 + [pltpu.VMEM((16, LRU_W), F32)] + [pltpu.VMEM((2, 256, 256), F32)] * 3
        + (_ChipExchange.scratch(n_x) if n_x else []),
        compiler_params=pltpu.CompilerParams(dimension_semantics=("arbitrary",),
                                             collective_id=COLLECTIVE_IDS["mixer_bwd"] if n_x else None),
    )(d2, u, u, hs, hs, saved, pv, wa, wx, wp, w_out_b, *chip_sums)
    return outs[0], outs[1], list(outs[2:])


def _mix_in_bwd(du, x, d2, w_in_t, g_mix, tm):
    T = x.shape[0]

    def body(du_ref, x_ref, d2_ref, w_ref, g_ref, dx_ref, dg_ref):
        @pl.when(pl.program_id(0) == 0)
        def _():
            dg_ref[...] = jnp.zeros((1, D_MODEL), F32)

        dh = jnp.dot(du_ref[...], w_ref[...], preferred_element_type=F32)
        xv = x_ref[...]
        rstd = _rstd(xv)
        dx, dgain = _rms_bwd(dh, xv * rstd, rstd, g_ref[...])
        dx_ref[...] = d2_ref[...] + dx
        dg_ref[...] += dgain

    row = lambda w: pl.BlockSpec((tm, w), lambda i: (i, 0))
    const = lambda shape: pl.BlockSpec(shape, lambda i: (0,) * len(shape))
    return pl.pallas_call(
        body, name="mix_in_bwd", grid=(T // tm,),
        in_specs=[row(D_IN), row(D_MODEL), row(D_MODEL), const((D_IN, D_MODEL)), const((1, D_MODEL))],
        out_specs=[row(D_MODEL), const((1, D_MODEL))],
        out_shape=[_sds((T, D_MODEL), F32), _sds((1, D_MODEL), F32)],
        compiler_params=pltpu.CompilerParams(dimension_semantics=("arbitrary",)),
    )(du, x, d2, w_in_t, g_mix)


def _adamw(w, g, m, v):
    m = ADAM_B1 * m + (1.0 - ADAM_B1) * g
    v = ADAM_B2 * v + (1.0 - ADAM_B2) * (g * g)
    m_hat = m / (1.0 - ADAM_B1 ** ADAM_STEP)
    v_hat = v / (1.0 - ADAM_B2 ** ADAM_STEP)
    delta = -ADAM_LR * (m_hat / (jnp.sqrt(v_hat) + ADAM_EPS) + ADAM_WD * w)
    return delta, m, v


def _adam_shards(ws, ms, vs, parts):
    n = len(ws)
    n_blk = [w.shape[0] // ADAM_ROWS for w in ws]

    def body(*refs):
        w_refs, m_refs, v_refs, p_refs, outs = (refs[:n], refs[n:2 * n], refs[2 * n:3 * n], refs[3 * n:4 * n],
                                                refs[4 * n:])
        i = pl.program_id(0)
        for a in range(n):
            @pl.when(i < n_blk[a])
            def _(a=a):
                g = p_refs[a][0].astype(F32)
                for j in range(1, 4):
                    g = g + p_refs[a][j].astype(F32)
                delta, new_m, new_v = _adamw(w_refs[a][...], g, m_refs[a][...], v_refs[a][...])
                for kind, val in enumerate((g, delta, new_m, new_v)):
                    outs[4 * a + kind][...] = val

    blk = lambda a: pl.BlockSpec((ADAM_ROWS, D_MODEL), lambda i: (jnp.minimum(i, n_blk[a] - 1), 0))
    part_blk = lambda a: pl.BlockSpec((4, ADAM_ROWS, D_MODEL), lambda i: (0, jnp.minimum(i, n_blk[a] - 1), 0))
    res = pl.pallas_call(
        body, name="adam_shards", grid=(max(n_blk),),
        in_specs=[blk(a) for a in range(n)] * 3 + [part_blk(a) for a in range(n)],
        out_specs=[blk(a) for a in range(n) for _ in range(4)],
        out_shape=[_sds(w.shape, F32) for w in ws for _ in range(4)],
        compiler_params=pltpu.CompilerParams(dimension_semantics=("arbitrary",)),
    )(*ws, *ms, *vs, *parts)
    return [tuple(res[4 * a:4 * a + 4]) for a in range(n)]


SMALL_PARAMS = [("norm_mix_g", (1, D_MODEL)), ("conv_w", (1, 4, 64)), ("conv_b", (1, LRU_W)),
                ("gate_a_w", (1, 8, 64, 64)), ("gate_a_b", (1, LRU_W)), ("gate_x_w", (1, 8, 64, 64)),
                ("gate_x_b", (1, LRU_W)), ("lru_lambda", (1, LRU_W)), ("pool_w", (1, 4, 128, 128)),
                ("pool_b", (1, POOL_W)), ("pool_scale", (1, POOL_W)), ("norm_lru_g", (1, LRU_W)),
                ("norm_pool_g", (1, POOL_W)), ("norm_ffn_g", (1, D_MODEL)), ("final_norm_g", (1, D_MODEL))]
VEC_ROW = dict(conv_b=ROW_CB, gate_a_b=ROW_BA, gate_x_b=ROW_BX, lru_lambda=ROW_LAM, pool_b=ROW_PB, pool_scale=ROW_PS,
               norm_lru_g=ROW_GL, norm_pool_g=ROW_GP)
WHOLE = (Ellipsis,)


def _unpack_mixer_grads(sg, dev):
    vec = jnp.concatenate([sg[SG_VEC:SG_VEC + 16], sg[SG_VEC + 16:SG_VEC + 32]], axis=1)
    out = {nm: [(WHOLE, vec[r:r + 1])] for nm, r in VEC_ROW.items()}
    own = jnp.zeros((4, 64), F32)
    for d in range(N_DEV):
        own = jnp.where(dev == d, vec[ROW_CW:ROW_CW + 4, 64 * d:64 * (d + 1)], own)
    out["conv_w"] = [((0,), own)]
    for nm, row0 in (("gate_a_w", SG_WA), ("gate_x_w", SG_WX)):
        out[nm] = [((0, b), sg[row0 + 64 * (b // 4):row0 + 64 * (b // 4 + 1), 64 * (b % 4):64 * (b % 4 + 1)])
                   for b in range(8)]
    out["pool_w"] = [((0, b), sg[SG_WP + 128 * (b // 2):SG_WP + 128 * (b // 2 + 1), 128 * (b % 2):128 * (b % 2 + 1)])
                     for b in range(4)]
    return out


def _adam_small(parts, w, m, v):
    names = [nm for nm, _ in SMALL_PARAMS]
    n = len(names)

    def body(sg_ref, gm_ref, gf_ref, gn_ref, ls_ref, *rest):
        w_refs, m_refs, v_refs, outs = rest[:n], rest[n:2 * n], rest[2 * n:3 * n], rest[3 * n:]
        dev = 4 * lax.axis_index("x") + 2 * lax.axis_index("y") + lax.axis_index("c")

        def total(ref):
            acc = ref[0]
            for d in range(1, N_DEV):
                acc = acc + ref[d]
            return acc

        pieces = _unpack_mixer_grads(total(sg_ref), dev)
        pieces["norm_mix_g"] = [(WHOLE, total(gm_ref))]
        pieces["norm_ffn_g"] = [(WHOLE, total(gf_ref))]
        pieces["final_norm_g"] = [(WHOLE, total(gn_ref))]
        for i, nm in enumerate(names):
            for idx, g in pieces[nm]:
                delta, new_m, new_v = _adamw(w_refs[i][idx], g, m_refs[i][idx], v_refs[i][idx])
                for kind, val in enumerate((g, delta, new_m, new_v)):
                    outs[4 * i + kind][idx] = val
        outs[4 * n][...] = total(ls_ref)

    shapes = [_sds(shape, F32) for _, shape in SMALL_PARAMS for _ in range(4)] + [_sds((8, 128), F32)]
    res = pl.pallas_call(body, name="adam_small", out_shape=shapes)(
        *parts, *[w[nm] for nm in names], *[m[nm] for nm in names], *[v[nm] for nm in names])
    return {nm: tuple(res[4 * i:4 * i + 4]) for i, nm in enumerate(names)}, res[4 * n][0, 0]


def _vec_rows(conv_w_full, conv_b, ba, bx, lam, pb, ps, gl, gp):
    return jnp.concatenate([conv_w_full, conv_b, ba, bx, lam, pb, ps, gl, gp, jnp.zeros((4, LRU_W), F32)], axis=0)


WEIGHT_ORDER = ['norm_mix_g', 'w_in', 'conv_w', 'conv_b', 'gate_a_w', 'gate_a_b', 'gate_x_w', 'gate_x_b', 'lru_lambda',
                'pool_w', 'pool_b', 'pool_scale', 'norm_lru_g', 'norm_pool_g', 'w_out', 'norm_ffn_g', 'ffn_w1', 'ffn_w3',
                'ffn_w2', 'final_norm_g']


def kernel(x, norm_mix_g, w_in, conv_w, conv_b, gate_a_w, gate_a_b, gate_x_w, gate_x_b, lru_lambda, pool_w, pool_b, pool_scale, norm_lru_g, norm_pool_g, w_out, norm_ffn_g, ffn_w1, ffn_w3, ffn_w2, final_norm_g, loss_target, m_norm_mix_g, m_w_in, m_conv_w, m_conv_b, m_gate_a_w, m_gate_a_b, m_gate_x_w, m_gate_x_b, m_lru_lambda, m_pool_w, m_pool_b, m_pool_scale, m_norm_lru_g, m_norm_pool_g, m_w_out, m_norm_ffn_g, m_ffn_w1, m_ffn_w3, m_ffn_w2, m_final_norm_g, v_norm_mix_g, v_w_in, v_conv_w, v_conv_b, v_gate_a_w, v_gate_a_b, v_gate_x_w, v_gate_x_b, v_lru_lambda, v_pool_w, v_pool_b, v_pool_scale, v_norm_lru_g, v_norm_pool_g, v_w_out, v_norm_ffn_g, v_ffn_w1, v_ffn_w3, v_ffn_w2, v_final_norm_g):
    ac = lax.axis_index("c")
    tm, tmx, tm_in, tk = 512, 512, 1024, 2048
    xs, tgt = x[0], loss_target[0]
    g_fin = final_norm_g.reshape(1, D_MODEL)
    c_arr = jnp.reshape(ac, (1,)).astype(jnp.int32)

    tr = lambda w: jnp.swapaxes(w[0], 0, 1)
    own = lambda w: w[0]
    bf = lambda a: a.astype(BF16)

    h1, (g_in, g_conv) = _norm_in(xs, norm_mix_g, [bf(tr(w_in)), conv_w[0]], tm_in)
    w_in_t = g_in.reshape(D_IN, D_MODEL)
    u, (g_out,) = _mix_in(h1, w_in_t, tm_in, shards=[bf(own(w_out))])
    conv_w_full = g_conv.transpose(1, 0, 2).reshape(4, LRU_W)
    pv = _vec_rows(conv_w_full, conv_b, gate_a_b, gate_x_b, lru_lambda, pool_b, pool_scale, norm_lru_g, norm_pool_g)
    wa, wx, wp = gate_a_w[0], gate_x_w[0], pool_w[0]
    w_out_b = g_out.reshape(D_MODEL, D_MODEL)
    y, hs, hres, h2, saved, (g_w1, g_w3, g_w2) = _mixer_fwd(
        u, xs, pv, wa, wx, wp, w_out_b, norm_ffn_g, tmx, shards=[bf(tr(ffn_w1)), bf(tr(ffn_w3)), bf(own(ffn_w2))])
    w1_t, w3_t, w2_b = g_w1.reshape(D_FF, D_MODEL), g_w3.reshape(D_FF, D_MODEL), g_w2.reshape(D_FF, D_MODEL)
    g, v, ff, d3, loss_acc, d_gfin = _ffn_fwd(hres, h2, w1_t, w3_t, w2_b, g_fin, tgt, tm)

    dg, dv, d2, d_gffn = _ffn_bwd(d3, g, v, w1_t, w3_t, w2_b, hres, norm_ffn_g, tm)
    chips = lambda a: a.reshape(4, a.shape[0] // 4, a.shape[1])
    early = [(y, d2, "grad_w_out"), (dg, h2, "grad_w1"), (dv, h2, "grad_w3"), (ff, d3, "grad_w2")]
    early_sums = [chips(_at_b_pair(a, b, c_arr, name, tk)) for a, b, name in early]
    du, d_mixer, early_parts = _mixer_bwd(d2, u, hs, saved, pv, wa, wx, wp, w_out_b, tmx, chip_sums=early_sums)
    grad_x, d_gmix = _mix_in_bwd(du, xs, d2, w_in_t, norm_mix_g, tm_in)
    d_win, small_parts = _at_b_pair(du, h1, c_arr, "grad_w_in", tk,
                                    gather=[d_mixer, d_gmix, d_gffn, d_gfin, loss_acc])
    parts = [_half_exchange(chips(d_win), "grads_to_chips_w_in")] + list(early_parts)

    res = {}
    shard_w = dict(w_in=(w_in, m_w_in, v_w_in, tr), w_out=(w_out, m_w_out, v_w_out, own),
                   ffn_w1=(ffn_w1, m_ffn_w1, v_ffn_w1, tr), ffn_w3=(ffn_w3, m_ffn_w3, v_ffn_w3, tr),
                   ffn_w2=(ffn_w2, m_ffn_w2, v_ffn_w2, own))
    shard_res = _adam_shards([view(w) for w, _, _, view in shard_w.values()],
                             [view(m) for _, m, _, view in shard_w.values()],
                             [view(v) for _, _, v, view in shard_w.values()], parts)
    for (nm, (_, _, _, view)), outs in zip(shard_w.items(), shard_res):
        res[nm] = [(jnp.swapaxes(o, 0, 1) if view is tr else o)[None] for o in outs]

    row = lambda a: a.reshape(1, D_MODEL)
    small = lambda gm, cw, cb, wa_, ba, wx_, bx, lam, pw, pb, ps, gl, gp, gf, gn: dict(
        norm_mix_g=gm, conv_w=cw, conv_b=cb, gate_a_w=wa_, gate_a_b=ba, gate_x_w=wx_, gate_x_b=bx, lru_lambda=lam,
        pool_w=pw, pool_b=pb, pool_scale=ps, norm_lru_g=gl, norm_pool_g=gp, norm_ffn_g=gf, final_norm_g=row(gn))
    small_res, loss = _adam_small(
        small_parts,
        small(norm_mix_g, conv_w, conv_b, gate_a_w, gate_a_b, gate_x_w, gate_x_b, lru_lambda, pool_w, pool_b,
              pool_scale, norm_lru_g, norm_pool_g, norm_ffn_g, final_norm_g),
        small(m_norm_mix_g, m_conv_w, m_conv_b, m_gate_a_w, m_gate_a_b, m_gate_x_w, m_gate_x_b, m_lru_lambda, m_pool_w,
              m_pool_b, m_pool_scale, m_norm_lru_g, m_norm_pool_g, m_norm_ffn_g, m_final_norm_g),
        small(v_norm_mix_g, v_conv_w, v_conv_b, v_gate_a_w, v_gate_a_b, v_gate_x_w, v_gate_x_b, v_lru_lambda, v_pool_w,
              v_pool_b, v_pool_scale, v_norm_lru_g, v_norm_pool_g, v_norm_ffn_g, v_final_norm_g))
    for nm, outs in small_res.items():
        res[nm] = [o.reshape(D_MODEL) for o in outs] if nm == "final_norm_g" else list(outs)

    out = [loss, grad_x[None]]
    for kind in range(4):
        out += [res[nm][kind] for nm in WEIGHT_ORDER]
    return tuple(out)
```

```python
import functools

import jax
import jax.numpy as jnp
from jax import lax
from jax.experimental import pallas as pl
from jax.experimental.pallas import tpu as pltpu

F32 = jnp.float32
BF16 = jnp.bfloat16

D_MODEL = 1024
LRU_W = 512
POOL_W = 512
D_IN = 1536
D_FF = 2816
POOL_WINDOWS = (2, 4, 8, 16)
EPS = 1e-6
LRU_C = 8.0
N_DEV = 8
HALO = 16
SCAN_UNROLL = 8
ADAM_ROWS = 32
FF_CHUNKS = ((0, 1536), (1536, 2816))
RELAY_SPLIT_ROWS = 32
COLLECTIVE_IDS = {name: j for j, name in enumerate(
    ("grad_w_out", "grad_w1", "grad_w3", "grad_w2", "norm_in", "mix_in", "mixer_fwd", "mixer_bwd", "grad_w_in",
     "grads_to_chips_w_in"))}

ADAM_LR = 0.001
ADAM_B1 = 0.9
ADAM_B2 = 0.999
ADAM_EPS = 1e-08
ADAM_WD = 0.01
ADAM_STEP = 10

ROW_CW, ROW_CB, ROW_BA, ROW_BX, ROW_LAM, ROW_PB, ROW_PS, ROW_GL, ROW_GP = 0, 4, 5, 6, 7, 8, 9, 10, 11
SG_VEC, SG_WA, SG_WX, SG_WP, SG_ROWS = 0, 32, 160, 288, 544

NT = (((1,), (1,)), ((), ()))
TN = (((0,), (0,)), ((), ()))


def _sds(shape, dtype):
    return jax.ShapeDtypeStruct(shape, dtype)


def _sigmoid(x):
    return 0.5 * jnp.tanh(0.5 * x) + 0.5


def _gelu_parts(x):
    c = 0.7978845608028654
    inner = c * (x + 0.044715 * (x * x * x))
    th = jnp.tanh(inner)
    g = 0.5 * x * (1.0 + th)
    dg = 0.5 * (1.0 + th) + 0.5 * x * (1.0 - th * th) * (c * (1.0 + 3.0 * 0.044715 * (x * x)))
    return g, dg


def _window_sum(ext, w, back):
    n = ext.shape[0]
    s, k = ext, 1
    while k < w:
        s = s + pltpu.roll(s, k if back else n - k, 0)
        k *= 2
    return s


def _rstd(x):
    return lax.rsqrt(jnp.mean(x * x, axis=-1, keepdims=True) + EPS)


def _rms_bwd(dy, xhat, rstd, gain):
    dxh = dy * gain
    dx = rstd * (dxh - xhat * jnp.mean(dxh * xhat, axis=-1, keepdims=True))
    return dx, jnp.sum(dy * xhat, axis=0, keepdims=True)


def _bd(xb, w_ref):
    return jnp.concatenate(
        [jnp.dot(xb[:, :256], w_ref[0], preferred_element_type=F32),
         jnp.dot(xb[:, 256:], w_ref[1], preferred_element_type=F32)], axis=1)


def _bd_t(xb, w_ref):
    return jnp.concatenate(
        [lax.dot_general(xb[:, :256], w_ref[0], NT, preferred_element_type=F32),
         lax.dot_general(xb[:, 256:], w_ref[1], NT, preferred_element_type=F32)], axis=1)


def _bd_grad(xb, db):
    return jnp.stack(
        [lax.dot_general(xb[:, :256], db[:, :256], TN, preferred_element_type=F32),
         lax.dot_general(xb[:, 256:], db[:, 256:], TN, preferred_element_type=F32)], axis=0)


def _fill_block_diag(dst, src_ref):
    n, k, _ = src_ref.shape
    dst[...] = jnp.zeros(dst.shape, BF16)
    for b in range(n):
        p, q = divmod(b, 256 // k)
        dst[p, q * k:(q + 1) * k, q * k:(q + 1) * k] = src_ref[b].astype(BF16)


def _diag_pack(w, k):
    lane = lax.broadcasted_iota(jnp.int32, (k, 256), 1)
    out = w[0:k]
    for q in range(1, 256 // k):
        out = jnp.where(lane >= q * k, w[q * k:(q + 1) * k], out)
    return out


def _y_pos(b):
    return 4 * (b % 2) + b // 2


def _softplus_neg_lambda(pv):
    z = -pv[ROW_LAM:ROW_LAM + 1, :]
    return jnp.maximum(z, 0.0) + jnp.log(1.0 + jnp.exp(-jnp.abs(z)))


def _lru_gates(e_lru, pv, wa_ref, wx_ref, tm):
    xc = pv[ROW_CB:ROW_CB + 1, :]
    for k in range(4):
        xc = xc + e_lru[pl.ds(HALO - 3 + k, tm), :] * pv[ROW_CW + k:ROW_CW + k + 1, :]
    xcb = xc.astype(BF16)
    r = _sigmoid(_bd(xcb, wa_ref) + pv[ROW_BA:ROW_BA + 1, :])
    ig = _sigmoid(_bd(xcb, wx_ref) + pv[ROW_BX:ROW_BX + 1, :])
    return xc, r, ig, (-LRU_C * r) * _softplus_neg_lambda(pv)


def _lru_decay(la):
    a = jnp.exp(la)
    om = -jnp.tanh(la) * (1.0 + a * a)
    omc = jnp.maximum(om, 1e-12)
    rmult = lax.rsqrt(omc)
    return a, om, omc * rmult, rmult


def _over_count(v, w, inv_head):
    return jnp.concatenate([v[0:HALO] * inv_head, v[HALO:] * (1.0 / w)], axis=0)


def _pool_pre(e_pool, pv, wp_ref, tm, t0):
    t_head = t0 + lax.broadcasted_iota(jnp.int32, (HALO, 1), 0)
    parts, inv_heads = [], []
    for g, w in enumerate(POOL_WINDOWS):
        ext = e_pool[:, pl.ds(128 * g, 128)]
        s = _window_sum(ext, w, back=True)[HALO:, :]
        inv_head = 1.0 / jnp.minimum(t_head + 1, w).astype(F32)
        inv_heads.append(inv_head)
        parts.append(_over_count(s, w, inv_head) - ext[HALO:, :])
    pooled = jnp.concatenate(parts, axis=1)
    pooled_b = pooled.astype(BF16)
    zp = _bd(pooled_b, wp_ref) + pv[ROW_PB:ROW_PB + 1, :]
    return pooled_b, zp, inv_heads


def _scan_tile(a_ref, b_ref, out_ref, carry, tm, reverse):
    row = lax.broadcasted_iota(jnp.int32, (8, LRU_W), 0)
    nblk = tm // 8

    def local_scan(blk):
        r0 = pl.multiple_of(blk * 8, 8)
        av = a_ref[pl.ds(r0, 8), :]
        bv = b_ref[pl.ds(r0, 8), :]
        for d in (1, 2, 4):
            sh = (8 - d) if reverse else d
            a_s = pltpu.roll(av, sh, 0)
            b_s = pltpu.roll(bv, sh, 0)
            m = (row < 8 - d) if reverse else (row >= d)
            bv = jnp.where(m, av * b_s + bv, bv)
            av = jnp.where(m, av * a_s, av)
        return r0, av, bv

    def step(i, hin):
        local = [local_scan((nblk - 1 - (i * SCAN_UNROLL + j)) if reverse else (i * SCAN_UNROLL + j))
                 for j in range(SCAN_UNROLL)]
        for r0, av, bv in local:
            hv = av * hin + bv
            out_ref[pl.ds(r0, 8), :] = hv
            hin = jnp.broadcast_to(hv[0:1, :] if reverse else hv[7:8, :], (8, LRU_W))
        return hin

    return lax.fori_loop(0, nblk // SCAN_UNROLL, step, carry)


MESH = pl.DeviceIdType.MESH
ANY = pl.BlockSpec(memory_space=pl.ANY)


def _place():
    x, y, c = lax.axis_index("x"), lax.axis_index("y"), lax.axis_index("c")
    chips = [(1 - x, y), (x, 1 - y), (1 - x, 1 - y)]
    return x, y, c, chips


def _meet(peers):
    _announce(peers)
    _await(peers)


def _announce(peers):
    for peer in peers:
        pl.semaphore_signal(pltpu.get_barrier_semaphore(), inc=1, device_id=peer, device_id_type=MESH)


def _await(peers):
    pl.semaphore_wait(pltpu.get_barrier_semaphore(), len(peers))


class _Gather:
    def __init__(self, ins, outs, send_sems, recv_sems, local_sems, core_major=False):
        self.ins, self.outs, self.n = ins, outs, len(ins)
        self.send_sems, self.recv_sems, self.local_sems = send_sems, recv_sems, local_sems
        self.core_major = core_major

    @staticmethod
    def scratch(n):
        return [pltpu.SemaphoreType.DMA((8, n)), pltpu.SemaphoreType.DMA((8, n)), pltpu.SemaphoreType.DMA((n,))]

    def _slot(self, a, px, py, pc):
        return self.outs[a].at[4 * pc + 2 * px + py if self.core_major else 4 * px + 2 * py + pc]

    def _half(self, a, h):
        rows = self.ins[a].shape[0]
        if rows % RELAY_SPLIT_ROWS:
            return None if h else (0, rows)
        return (h * (rows // 2), rows // 2)

    def _copy(self, a, k, block, to, src=None, rows=None):
        dst = self._slot(a, *block)
        src = dst if src is None else src
        if rows is not None:
            src, dst = src.at[pl.ds(*rows)], dst.at[pl.ds(*rows)]
        return pltpu.make_async_remote_copy(
            src_ref=src, dst_ref=dst, send_sem=self.send_sems.at[k, a], recv_sem=self.recv_sems.at[k, a],
            device_id=to, device_id_type=MESH)

    def _mine(self, a):
        x, y, c, _ = _place()
        return pltpu.make_async_copy(self.ins[a], self._slot(a, x, y, c), self.local_sems.at[a])

    def _first(self, a):
        x, y, c, chips = _place()
        me = (x, y, c)
        return ([self._copy(a, 0, me, (x, y, 1 - c), src=self.ins[a])]
                + [self._copy(a, 1 + j, me, (*chip, c), src=self.ins[a]) for j, chip in enumerate(chips[:2])])

    def _passed_on(self, a, h):
        x, y, c, chips = _place()
        block = (*chips[h], c)
        out = [self._copy(a, 4 + h, block, (x, y, 1 - c))]
        if self._half(a, h) is not None:
            out.append(self._copy(a, (3, 7)[h], block, (*chips[1 - h], c), rows=self._half(a, h)))
        return out

    def _peers(self):
        x, y, c, chips = _place()
        return [(x, y, 1 - c), (*chips[0], c), (*chips[1], c)]

    def announce(self):
        _announce(self._peers())

    def start(self, announced=False):
        (_await if announced else _meet)(self._peers())
        for a in range(self.n):
            self._mine(a).start()
        for a in range(self.n):
            for cp in self._first(a):
                cp.start()

    def relay(self):
        x, y, c, chips = _place()
        for h in range(2):
            for a in range(self.n):
                self._copy(a, 1 + h, (*chips[h], c), (x, y, c)).wait_recv()
                for cp in self._passed_on(a, h):
                    cp.start()

    def finish(self):
        x, y, c, chips = _place()
        me, sibling = (x, y, c), (x, y, 1 - c)
        passed = []
        for a in range(self.n):
            for h in range(2):
                if self._half(a, h) is not None:
                    self._copy(a, (3, 7)[h], (*chips[2], c), me, rows=self._half(a, h)).wait_recv()
            fwd = self._copy(a, 6, (*chips[2], c), sibling)
            fwd.start()
            passed.append(fwd)
        for a in range(self.n):
            self._copy(a, 0, (x, y, 1 - c), me).wait_recv()
            for j, chip in enumerate(chips):
                self._copy(a, 4 + j, (*chip, 1 - c), me).wait_recv()
        for a in range(self.n):
            for cp in self._first(a) + self._passed_on(a, 0) + self._passed_on(a, 1):
                cp.wait_send()
        for cp in passed:
            cp.wait_send()
        for a in range(self.n):
            self._mine(a).wait()


def _half_exchange(arr, name):
    def body(in_ref, out_ref, send_sems, recv_sems, local_sem):
        x, y, c, _ = _place()
        my_chip = 2 * x + y
        _meet([((x + dx) % 2, (y + dy) % 2, (c + dc) % 2)
               for dx in range(2) for dy in range(2) for dc in range(2) if dx + dy + dc])

        def send(j, wait):
            to_me = (c == x) & (y == j // 2) & (c == j % 2)

            @pl.when(to_me)
            def _():
                local = pltpu.make_async_copy(in_ref.at[j], out_ref.at[my_chip], local_sem)
                local.wait() if wait else local.start()

            @pl.when(jnp.logical_not(to_me))
            def _():
                remote = pltpu.make_async_remote_copy(
                    src_ref=in_ref.at[j], dst_ref=out_ref.at[my_chip], send_sem=send_sems.at[j],
                    recv_sem=recv_sems.at[my_chip], device_id=(c, j // 2, j % 2), device_id_type=MESH)
                remote.wait_send() if wait else remote.start()

        for j in range(4):
            send(j, wait=False)
        for j in range(4):
            send(j, wait=True)
        for k in range(4):
            from_me = (k // 2 == x) & (k % 2 == y) & (c == x)

            @pl.when(jnp.logical_not(from_me))
            def _():
                pltpu.make_async_remote_copy(
                    src_ref=in_ref.at[0], dst_ref=out_ref.at[k], send_sem=send_sems.at[0], recv_sem=recv_sems.at[k],
                    device_id=(k // 2, k % 2, x), device_id_type=MESH).wait_recv()

    return pl.pallas_call(
        body, name=name, out_shape=_sds(arr.shape, arr.dtype), in_specs=[ANY], out_specs=ANY,
        scratch_shapes=[pltpu.SemaphoreType.DMA((4,)), pltpu.SemaphoreType.DMA((4,)), pltpu.SemaphoreType.DMA],
        compiler_params=pltpu.CompilerParams(collective_id=COLLECTIVE_IDS[name]),
    )(arr)


class _ChipExchange:
    def __init__(self, ins, outs, send_sems, recv_sems, local_sems):
        self.ins, self.outs, self.n = ins, outs, len(ins)
        self.send_sems, self.recv_sems, self.local_sems = send_sems, recv_sems, local_sems

    @staticmethod
    def scratch(n):
        return [pltpu.SemaphoreType.DMA((3, n)), pltpu.SemaphoreType.DMA((3, n)), pltpu.SemaphoreType.DMA((n,))]

    def _local(self, a):
        x, y, _, _ = _place()
        me = 2 * x + y
        return pltpu.make_async_copy(self.ins[a].at[me], self.outs[a].at[me], self.local_sems.at[a])

    def _copies(self, a):
        x, y, c, chips = _place()
        me = 2 * x + y
        return [(pltpu.make_async_remote_copy(
                     src_ref=self.ins[a].at[2 * px + py], dst_ref=self.outs[a].at[me],
                     send_sem=self.send_sems.at[k, a], recv_sem=self.recv_sems.at[k, a],
                     device_id=(px, py, c), device_id_type=MESH),
                 pltpu.make_async_remote_copy(
                     src_ref=self.ins[a].at[me], dst_ref=self.outs[a].at[2 * px + py],
                     send_sem=self.send_sems.at[k, a], recv_sem=self.recv_sems.at[k, a],
                     device_id=(px, py, c), device_id_type=MESH))
                for k, (px, py) in enumerate(chips)]

    def _peers(self):
        _, _, c, chips = _place()
        return [(*chip, c) for chip in chips]

    def announce(self):
        _announce(self._peers())

    def start(self):
        _await(self._peers())
        for a in range(self.n):
            self._local(a).start()
        for a in range(self.n):
            for send, _ in self._copies(a):
                send.start()

    def finish(self):
        for a in range(self.n):
            for send, recv in self._copies(a):
                send.wait_send()
                recv.wait_recv()
        for a in range(self.n):
            self._local(a).wait()


def _gathering(body, n_steps, n_s, core_major, start_at=0):
    def wrapped(*refs, n_in, n_out):
        ins, sh_in = refs[:n_in], refs[n_in:n_in + n_s]
        outs, sh_out = refs[n_in + n_s:n_in + n_s + n_out], refs[n_in + n_s + n_out:n_in + 2 * n_s + n_out]
        rest = refs[n_in + 2 * n_s + n_out:]
        gather = _Gather(sh_in, sh_out, *rest[len(rest) - 3:], core_major=core_major)
        i = pl.program_id(0)

        if start_at:
            @pl.when(i == 0)
            def _():
                gather.announce()

        @pl.when(i == start_at)
        def _():
            gather.start(announced=start_at > 0)

        @pl.when(i == n_steps // 2)
        def _():
            gather.relay()

        body(*ins, *outs, *rest[:len(rest) - 3])

        @pl.when(i == n_steps - 1)
        def _():
            gather.finish()

    return wrapped


def _norm_in(x, g_mix, shards, tm):
    T = x.shape[0]
    n_t = T // tm
    n_s = len(shards)

    def norm(x_ref, g_ref, h_ref):
        xv = x_ref[...]
        h_ref[...] = (xv * _rstd(xv) * g_ref[...]).astype(BF16)

    outs = pl.pallas_call(
        functools.partial(_gathering(norm, n_t, n_s, core_major=False), n_in=2, n_out=1), name="norm_in", grid=(n_t,),
        in_specs=[pl.BlockSpec((tm, D_MODEL), lambda i: (i, 0)), pl.BlockSpec((1, D_MODEL), lambda i: (0, 0))]
        + [ANY] * n_s,
        out_specs=[pl.BlockSpec((tm, D_MODEL), lambda i: (i, 0))] + [ANY] * n_s,
        out_shape=[_sds((T, D_MODEL), BF16)] + [_sds((N_DEV,) + a.shape, a.dtype) for a in shards],
        scratch_shapes=_Gather.scratch(n_s),
        compiler_params=pltpu.CompilerParams(dimension_semantics=("arbitrary",), collective_id=COLLECTIVE_IDS["norm_in"]),
    )(x, g_mix, *shards)
    return outs[0], list(outs[1:])


def _mix_in(h1, w_in_t, tm, shards):
    T = h1.shape[0]
    n_t = T // tm
    n_s = len(shards)

    def project(h_ref, w_ref, u_ref):
        u_ref[...] = lax.dot_general(h_ref[...], w_ref[...], NT, preferred_element_type=F32)

    outs = pl.pallas_call(
        functools.partial(_gathering(project, n_t, n_s, core_major=True, start_at=1), n_in=2, n_out=1), name="mix_in",
        grid=(n_t,),
        in_specs=[pl.BlockSpec((tm, D_MODEL), lambda i: (i, 0)), pl.BlockSpec((D_IN, D_MODEL), lambda i: (0, 0))]
        + [ANY] * n_s,
        out_specs=[pl.BlockSpec((tm, D_IN), lambda i: (i, 0))] + [ANY] * n_s,
        out_shape=[_sds((T, D_IN), F32)] + [_sds((N_DEV,) + a.shape, a.dtype) for a in shards],
        scratch_shapes=_Gather.scratch(n_s),
        compiler_params=pltpu.CompilerParams(dimension_semantics=("arbitrary",), collective_id=COLLECTIVE_IDS["mix_in"]),
    )(h1, w_in_t, *shards)
    return outs[0], list(outs[1:])


def _mixer_fwd(u, x, pv, wa, wx, wp, w_out_b, g_ffn, tm, shards=()):
    T = u.shape[0]
    n_s = len(shards)
    n_t = T // tm

    def body(u_ref, x_ref, pv_ref, wa_in, wx_in, wp_in, wo_ref, gf_ref, *rest):
        sh_in, rest = rest[:n_s], rest[n_s:]
        y_ref, hs_ref, hres_ref, h2_ref, saved_ref = rest[:5]
        sh_out, rest = rest[5:5 + n_s], rest[5 + n_s:]
        e_lru, e_pool, a_s, b_s, hc, wa_ref, wx_ref, wp_ref = rest[:8]
        gather = _Gather(sh_in, sh_out, *rest[8:], core_major=True) if n_s else None
        i = pl.program_id(0)

        @pl.when(i == 0)
        def _():
            if gather:
                gather.start()
            e_lru[pl.ds(0, HALO), :] = jnp.zeros((HALO, LRU_W), F32)
            e_pool[pl.ds(0, HALO), :] = jnp.zeros((HALO, POOL_W), F32)
            hc[...] = jnp.zeros((8, LRU_W), F32)
            _fill_block_diag(wa_ref, wa_in)
            _fill_block_diag(wx_ref, wx_in)
            _fill_block_diag(wp_ref, wp_in)

        if gather:
            @pl.when(i == (2 * n_t) // 3)
            def _():
                gather.relay()

        e_lru[pl.ds(HALO, tm), :] = u_ref[:, 0:LRU_W]
        e_pool[pl.ds(HALO, tm), :] = u_ref[:, 2 * LRU_W:D_IN]
        pv = pv_ref[...]
        xc, r, ig, la = _lru_gates(e_lru, pv, wa_ref, wx_ref, tm)
        for q, val in enumerate((xc, r, ig, la)):
            saved_ref[:, LRU_W * q:LRU_W * (q + 1)] = val
        a, _, mult, _ = _lru_decay(la)
        a_s[...] = a
        b_s[...] = mult * (ig * xc)
        hc[...] = _scan_tile(a_s, b_s, hs_ref, hc[...], tm, reverse=False)
        gl, _ = _gelu_parts(u_ref[:, LRU_W:2 * LRU_W])
        y_lru = hs_ref[...] * gl
        _, zp, _ = _pool_pre(e_pool, pv, wp_ref, tm, i * tm)
        y_pool = zp * pv[ROW_PS:ROW_PS + 1, :]
        yn = jnp.concatenate([y_lru * _rstd(y_lru) * pv[ROW_GL:ROW_GL + 1, :],
                              y_pool * _rstd(y_pool) * pv[ROW_GP:ROW_GP + 1, :]], axis=1).astype(BF16)
        for b in range(N_DEV):
            y_ref[:, 128 * _y_pos(b):128 * (_y_pos(b) + 1)] = yn[:, 128 * b:128 * (b + 1)]
        hr = x_ref[...] + jnp.dot(y_ref[...], wo_ref[...], preferred_element_type=F32)
        hres_ref[...] = hr
        h2_ref[...] = (hr * _rstd(hr) * gf_ref[...]).astype(BF16)
        e_lru[pl.ds(0, HALO), :] = e_lru[pl.ds(tm, HALO), :]
        e_pool[pl.ds(0, HALO), :] = e_pool[pl.ds(tm, HALO), :]

        if gather:
            @pl.when(i == n_t - 1)
            def _():
                gather.finish()

    full = lambda shape: pl.BlockSpec(shape, lambda i: (0,) * len(shape))
    row = lambda w: pl.BlockSpec((tm, w), lambda i: (i, 0))
    outs = pl.pallas_call(
        body, name="mixer_fwd", grid=(n_t,),
        in_specs=[row(D_IN), row(D_MODEL), full((16, LRU_W)), full((8, 64, 64)), full((8, 64, 64)), full((4, 128, 128)),
                  full((D_MODEL, D_MODEL)), full((1, D_MODEL))] + [ANY] * n_s,
        out_specs=[row(D_MODEL), row(LRU_W), row(D_MODEL), row(D_MODEL), row(4 * LRU_W)] + [ANY] * n_s,
        out_shape=[_sds((T, D_MODEL), BF16), _sds((T, LRU_W), F32), _sds((T, D_MODEL), F32), _sds((T, D_MODEL), BF16),
                   _sds((T, 4 * LRU_W), F32)] + [_sds((N_DEV,) + a.shape, a.dtype) for a in shards],
        scratch_shapes=[pltpu.VMEM((HALO + tm, LRU_W), F32), pltpu.VMEM((HALO + tm, POOL_W), F32),
                        pltpu.VMEM((tm, LRU_W), F32), pltpu.VMEM((tm, LRU_W), F32), pltpu.VMEM((8, LRU_W), F32)]
        + [pltpu.VMEM((2, 256, 256), BF16)] * 3 + (_Gather.scratch(n_s) if n_s else []),
        compiler_params=pltpu.CompilerParams(dimension_semantics=("arbitrary",),
                                             collective_id=COLLECTIVE_IDS["mixer_fwd"] if n_s else None),
    )(u, x, pv, wa, wx, wp, w_out_b, g_ffn, *shards)
    return outs[0], outs[1], outs[2], outs[3], outs[4], list(outs[5:])


def _ffn_fwd(hres, h2, w1_b, w3_b, w2_b, g_fin, tgt, tm):
    T = hres.shape[0]

    def body(hres_ref, h2_ref, w1_ref, w3_ref, w2_ref, gfin_ref, tgt_ref,
             g_ref, v_ref, ff_ref, d3_ref, loss_ref, dgfin_ref):
        @pl.when(pl.program_id(0) == 0)
        def _():
            loss_ref[...] = jnp.zeros((8, 128), F32)
            dgfin_ref[...] = jnp.zeros((1, D_MODEL), F32)

        h2 = h2_ref[...]
        h3 = hres_ref[...]
        for lo, hi in FF_CHUNKS:
            g = lax.dot_general(h2, w1_ref[lo:hi, :], NT, preferred_element_type=F32)
            v = lax.dot_general(h2, w3_ref[lo:hi, :], NT, preferred_element_type=F32)
            g_ref[:, lo:hi] = g.astype(BF16)
            v_ref[:, lo:hi] = v.astype(BF16)
            ff = ((g * _sigmoid(g)) * v).astype(BF16)
            ff_ref[:, lo:hi] = ff
            h3 = h3 + jnp.dot(ff, w2_ref[lo:hi, :], preferred_element_type=F32)

        rstd = _rstd(h3)
        xh = h3 * rstd
        gfin = gfin_ref[...]
        err = xh * gfin - tgt_ref[...]
        loss_ref[...] += 0.5 * jnp.sum(jnp.mean(err * err, axis=-1, keepdims=True))
        dx, dgain = _rms_bwd(err * (1.0 / D_MODEL), xh, rstd, gfin)
        d3_ref[...] = dx
        dgfin_ref[...] += dgain

    row = lambda w: pl.BlockSpec((tm, w), lambda i: (i, 0))
    const = lambda shape: pl.BlockSpec(shape, lambda i: (0,) * len(shape))
    weight = pl.BlockSpec((D_FF, D_MODEL), lambda i: (0, 0), pipeline_mode=pl.Buffered(1))
    return pl.pallas_call(
        body, name="ffn_fwd", grid=(T // tm,),
        in_specs=[row(D_MODEL), row(D_MODEL), weight, weight, weight, const((1, D_MODEL)), row(D_MODEL)],
        out_specs=[row(D_FF), row(D_FF), row(D_FF), row(D_MODEL), const((8, 128)), const((1, D_MODEL))],
        out_shape=[_sds((T, D_FF), BF16), _sds((T, D_FF), BF16), _sds((T, D_FF), BF16),
                   _sds((T, D_MODEL), F32), _sds((8, 128), F32), _sds((1, D_MODEL), F32)],
        compiler_params=pltpu.CompilerParams(dimension_semantics=("arbitrary",)),
    )(hres, h2, w1_b, w3_b, w2_b, g_fin, tgt)


def _ffn_bwd(d3, g, v, w1_b, w3_b, w2_b, hres, g_ffn, tm):
    T = d3.shape[0]

    def body(d3_ref, g_ref, v_ref, w1_ref, w3_ref, w2_ref, hres_ref, gf_ref, dg_ref, dv_ref, d2_ref, dgffn_ref):
        @pl.when(pl.program_id(0) == 0)
        def _():
            dgffn_ref[...] = jnp.zeros((1, D_MODEL), F32)

        d3 = d3_ref[...]
        d3b = d3.astype(BF16)
        dh2 = jnp.zeros((tm, D_MODEL), F32)
        for lo, hi in FF_CHUNKS:
            dff = lax.dot_general(d3b, w2_ref[lo:hi, :], NT, preferred_element_type=F32)
            gv = g_ref[:, lo:hi].astype(F32)
            vv = v_ref[:, lo:hi].astype(F32)
            sg = _sigmoid(gv)
            sl = gv * sg
            dgb = (dff * vv * (sg * (1.0 + gv * (1.0 - sg)))).astype(BF16)
            dvb = (dff * sl).astype(BF16)
            dg_ref[:, lo:hi] = dgb
            dv_ref[:, lo:hi] = dvb
            dh2 = dh2 + (jnp.dot(dgb, w1_ref[lo:hi, :], preferred_element_type=F32)
                         + jnp.dot(dvb, w3_ref[lo:hi, :], preferred_element_type=F32))

        hr = hres_ref[...]
        rstd = _rstd(hr)
        dx, dgain = _rms_bwd(dh2, hr * rstd, rstd, gf_ref[...])
        d2_ref[...] = d3 + dx
        dgffn_ref[...] += dgain

    row = lambda w: pl.BlockSpec((tm, w), lambda i: (i, 0))
    const = lambda shape: pl.BlockSpec(shape, lambda i: (0,) * len(shape))
    weight = pl.BlockSpec((D_FF, D_MODEL), lambda i: (0, 0), pipeline_mode=pl.Buffered(1))
    return pl.pallas_call(
        body, name="ffn_bwd", grid=(T // tm,),
        in_specs=[row(D_MODEL), row(D_FF), row(D_FF), weight, weight, weight, row(D_MODEL), const((1, D_MODEL))],
        out_specs=[row(D_FF), row(D_FF), row(D_MODEL), const((1, D_MODEL))],
        out_shape=[_sds((T, D_FF), BF16), _sds((T, D_FF), BF16), _sds((T, D_MODEL), F32), _sds((1, D_MODEL), F32)],
        compiler_params=pltpu.CompilerParams(dimension_semantics=("arbitrary",)),
    )(d3, g, v, w1_b, w3_b, w2_b, hres, g_ffn)


def _at_b_pair(a, b, c_arr, name, tk, gather=()):
    T, M = a.shape
    N = b.shape[1]
    hm, n_k = M // 2, T // tk
    n_g = len(gather)

    def body(c_ref, a_ref, b_ref, *rest):
        g_in, o_ref, rest = rest[:n_g], rest[n_g], rest[n_g + 1:]
        g_out, rest = rest[:n_g], rest[n_g:]
        acc, landed, send_sem, recv_sem = rest[:4]
        ag = _Gather(g_in, g_out, *rest[4:]) if n_g else None
        ph, k = pl.program_id(0), pl.program_id(1)

        def hand_over():
            x, y, c, _ = _place()
            return pltpu.make_async_remote_copy(
                src_ref=acc.at[0], dst_ref=landed, send_sem=send_sem, recv_sem=recv_sem,
                device_id=(x, y, 1 - c), device_id_type=MESH)

        if ag:
            @pl.when((ph == 0) & (k == 0))
            def _():
                ag.start()

            @pl.when((ph == 1) & (k == 0))
            def _():
                ag.relay()
        else:
            barrier = pltpu.get_barrier_semaphore()

            @pl.when((ph == 0) & (k == 0))
            def _():
                x, y, c, _ = _place()
                pl.semaphore_signal(barrier, inc=1, device_id=(x, y, 1 - c), device_id_type=MESH)

        @pl.when(k == 0)
        def _():
            acc[ph] = jnp.zeros((hm, N), F32)

        acc[ph] += lax.dot_general(a_ref[...].astype(BF16), b_ref[...].astype(BF16), TN, preferred_element_type=F32)

        @pl.when((ph == 0) & (k == n_k - 1))
        def _():
            if not ag:
                pl.semaphore_wait(barrier, 1)
            hand_over().start()

        @pl.when((ph == 1) & (k == n_k - 1))
        def _():
            copy = hand_over()
            copy.wait_recv()
            o_ref[...] = (acc[1] + landed[...]).astype(BF16)
            copy.wait_send()
            if ag:
                ag.finish()

    outs = pl.pallas_call(
        body, name=name,
        grid_spec=pltpu.PrefetchScalarGridSpec(
            num_scalar_prefetch=1, grid=(2, n_k),
            in_specs=[pl.BlockSpec((tk, hm), lambda ph, k, c_ref: (k, (ph + 1 - c_ref[0]) % 2)),
                      pl.BlockSpec((tk, N), lambda ph, k, c_ref: (k, 0))] + [ANY] * n_g,
            out_specs=[pl.BlockSpec((hm, N), lambda ph, k, c_ref: (0, 0))] + [ANY] * n_g,
            scratch_shapes=[pltpu.VMEM((2, hm, N), F32), pltpu.VMEM((hm, N), F32),
                            pltpu.SemaphoreType.DMA, pltpu.SemaphoreType.DMA] + (_Gather.scratch(n_g) if n_g else [])),
        out_shape=[_sds((hm, N), BF16)] + [_sds((N_DEV,) + g.shape, g.dtype) for g in gather],
        compiler_params=pltpu.CompilerParams(dimension_semantics=("arbitrary", "arbitrary"),
                                             collective_id=COLLECTIVE_IDS[name]),
    )(c_arr, a, b, *gather)
    return (outs[0], list(outs[1:])) if n_g else outs[0]


def _mixer_bwd(d2, u, hs, saved, pv, wa, wx, wp, w_out_b, tm, chip_sums=()):
    T = u.shape[0]
    n_t = T // tm
    n_x = len(chip_sums)

    def body(d2_ref, u_ref, uh_ref, hs_ref, hh_ref, saved_ref, pv_ref, wa_in, wx_in, wp_in, wo_ref, *rest):
        x_in, rest = rest[:n_x], rest[n_x:]
        du_ref, sg_ref = rest[:2]
        x_out, rest = rest[2:2 + n_x], rest[2 + n_x:]
        e_pool, e_h, a_s, b_s, dh_s, mu_s, f_x, f_p, mc, cx, cp = rest[:11]
        wa_ref, wx_ref, wp_ref, vacc_ref, dwa_ref, dwx_ref, dwp_ref = rest[11:18]
        exchange = _ChipExchange(x_in, x_out, *rest[18:]) if n_x else None
        s = pl.program_id(0)
        it = n_t - 1 - s

        @pl.when(s == 0)
        def _():
            if exchange:
                exchange.announce()
            mc[...] = jnp.zeros((8, LRU_W), F32)
            cx[...] = jnp.zeros((8, LRU_W), F32)
            cp[...] = jnp.zeros((HALO, POOL_W), F32)
            vacc_ref[...] = jnp.zeros((16, LRU_W), F32)
            dwa_ref[...] = jnp.zeros((2, 256, 256), F32)
            dwx_ref[...] = jnp.zeros((2, 256, 256), F32)
            dwp_ref[...] = jnp.zeros((2, 256, 256), F32)
            _fill_block_diag(wa_ref, wa_in)
            _fill_block_diag(wx_ref, wx_in)
            _fill_block_diag(wp_ref, wp_in)

        if exchange:
            @pl.when(s == 1)
            def _():
                exchange.start()

        first = it == 0
        e_pool[pl.ds(0, HALO), :] = jnp.where(first, 0.0, uh_ref[...])
        e_pool[pl.ds(HALO, tm), :] = u_ref[:, 2 * LRU_W:D_IN]
        e_h[pl.ds(0, 8), :] = jnp.where(first, 0.0, hh_ref[...])
        e_h[pl.ds(8, tm), :] = hs_ref[...]
        pv = pv_ref[...]
        saved = lambda q: saved_ref[:, LRU_W * q:LRU_W * (q + 1)]

        dyn = lax.dot_general(d2_ref[...].astype(BF16), wo_ref[...], NT, preferred_element_type=F32)
        dyn = jnp.concatenate([dyn[:, 128 * _y_pos(b):128 * (_y_pos(b) + 1)] for b in range(N_DEV)], axis=1)

        h = hs_ref[...]
        ug = u_ref[:, LRU_W:2 * LRU_W]
        gl, dgl = _gelu_parts(ug)
        y_lru = h * gl
        rstd_l = _rstd(y_lru)
        dy_lru, d_gain_l = _rms_bwd(dyn[:, 0:LRU_W], y_lru * rstd_l, rstd_l, pv[ROW_GL:ROW_GL + 1, :])
        dh = dy_lru * gl
        du_ref[:, LRU_W:2 * LRU_W] = (dy_lru * h * dgl).astype(BF16)
        a_s[...] = jnp.exp(saved(3))
        b_s[...] = a_s[...] * dh
        dh_s[...] = dh
        mu_s[pl.ds(tm, 8), :] = mc[...]
        mc[...] = _scan_tile(a_s, b_s, mu_s, mc[...], tm, reverse=True)
        xc, r, ig = saved(0), saved(1), saved(2)
        a, om, mult, rmult = _lru_decay(saved(3))
        lam_t = dh_s[...] + mu_s[pl.ds(1, tm), :]
        da = lam_t * e_h[pl.ds(7, tm), :]
        dmult = lam_t * (ig * xc)
        di = lam_t * (mult * xc)
        dxc = lam_t * (mult * ig)
        dla = da * a - jnp.where(om > 1e-12, dmult * ((a * a) * rmult), 0.0)
        dra = (dla * (-LRU_C * _softplus_neg_lambda(pv))) * (r * (1.0 - r))
        dia = di * (ig * (1.0 - ig))
        drab = dra.astype(BF16)
        diab = dia.astype(BF16)
        xcb = xc.astype(BF16)
        dxc = dxc + _bd_t(drab, wa_ref) + _bd_t(diab, wx_ref)
        dwa_ref[...] += _bd_grad(xcb, drab)
        dwx_ref[...] += _bd_grad(xcb, diab)
        sig_neg_lam = _sigmoid(-pv[ROW_LAM:ROW_LAM + 1, :])
        d_lam = jnp.sum(dla * r, axis=0, keepdims=True) * (LRU_C * sig_neg_lam)

        f_x[pl.ds(0, tm), :] = dxc
        f_x[pl.ds(tm, 8), :] = cx[...]
        du_lru = jnp.zeros((tm, LRU_W), F32)
        u_lru = u_ref[:, 0:LRU_W]
        d_cw = []
        for k in range(4):
            later = f_x[pl.ds(3 - k, tm), :]
            du_lru = du_lru + later * pv[ROW_CW + k:ROW_CW + k + 1, :]
            d_cw.append(jnp.sum(later * u_lru, axis=0, keepdims=True))
        du_ref[:, 0:LRU_W] = du_lru.astype(BF16)
        cx[...] = f_x[pl.ds(0, 8), :]

        pooled_b, zp, inv_cnts = _pool_pre(e_pool, pv, wp_ref, tm, it * tm)
        ps = pv[ROW_PS:ROW_PS + 1, :]
        y_pool = zp * ps
        rstd_p = _rstd(y_pool)
        dy_pool, d_gain_p = _rms_bwd(dyn[:, LRU_W:D_MODEL], y_pool * rstd_p, rstd_p, pv[ROW_GP:ROW_GP + 1, :])
        dz = dy_pool * ps
        dzb = dz.astype(BF16)
        dwp_ref[...] += _bd_grad(pooled_b, dzb)
        dpooled = _bd_t(dzb, wp_ref)
        for g, w in enumerate(POOL_WINDOWS):
            f_p[pl.ds(0, tm), pl.ds(128 * g, 128)] = _over_count(dpooled[:, 128 * g:128 * (g + 1)], w, inv_cnts[g])
        f_p[pl.ds(tm, HALO), :] = cp[...]
        for g, w in enumerate(POOL_WINDOWS):
            acc = _window_sum(f_p[:, pl.ds(128 * g, 128)], w, back=False)[0:tm, :]
            du_ref[:, 2 * LRU_W + 128 * g:2 * LRU_W + 128 * (g + 1)] = (
                acc - dpooled[:, 128 * g:128 * (g + 1)]).astype(BF16)
        cp[...] = f_p[pl.ds(0, HALO), :]

        rows = d_cw + [
            jnp.sum(dxc, axis=0, keepdims=True),
            jnp.sum(dra, axis=0, keepdims=True),
            jnp.sum(dia, axis=0, keepdims=True),
            d_lam,
            jnp.sum(dz, axis=0, keepdims=True),
            jnp.sum(dy_pool * zp, axis=0, keepdims=True),
            d_gain_l, d_gain_p,
            jnp.zeros((4, LRU_W), F32),
        ]
        vacc_ref[...] += jnp.concatenate(rows, axis=0)

        @pl.when(s == n_t - 1)
        def _():
            sg_ref[SG_VEC:SG_VEC + 16, :] = vacc_ref[:, 0:256]
            sg_ref[SG_VEC + 16:SG_VEC + 32, :] = vacc_ref[:, 256:512]
            for half in range(2):
                sg_ref[SG_WA + 64 * half:SG_WA + 64 * (half + 1), :] = _diag_pack(dwa_ref[half], 64)
                sg_ref[SG_WX + 64 * half:SG_WX + 64 * (half + 1), :] = _diag_pack(dwx_ref[half], 64)
                sg_ref[SG_WP + 128 * half:SG_WP + 128 * (half + 1), :] = _diag_pack(dwp_ref[half], 128)
            if exchange:
                exchange.finish()

    rev = lambda w: pl.BlockSpec((tm, w), lambda s: (n_t - 1 - s, 0))
    full = lambda shape: pl.BlockSpec(shape, lambda s: (0,) * len(shape))
    outs = pl.pallas_call(
        body, name="mixer_bwd", grid=(n_t,),
        in_specs=[rev(D_MODEL), rev(D_IN),
                  pl.BlockSpec((HALO, POOL_W), lambda s: (jnp.maximum((n_t - 1 - s) * (tm // HALO) - 1, 0), 2)),
                  rev(LRU_W),
                  pl.BlockSpec((8, LRU_W), lambda s: (jnp.maximum((n_t - 1 - s) * (tm // 8) - 1, 0), 0)),
                  rev(4 * LRU_W), full((16, LRU_W)), full((8, 64, 64)), full((8, 64, 64)), full((4, 128, 128)),
                  full((D_MODEL, D_MODEL))] + [ANY] * n_x,
        out_specs=[rev(D_IN), full((SG_ROWS, 256))] + [ANY] * n_x,
        out_shape=[_sds((T, D_IN), BF16), _sds((SG_ROWS, 256), F32)] + [_sds(a.shape, a.dtype) for a in chip_sums],
        scratch_shapes=[pltpu.VMEM((HALO + tm, POOL_W), F32),
                        pltpu.VMEM((8 + tm, LRU_W), F32)] + [pltpu.VMEM((tm, LRU_W), F32)] * 3 + [
                        pltpu.VMEM((tm + 8, LRU_W), F32), pltpu.VMEM((tm + 8, LRU_W), F32),
                        pltpu.VMEM((tm + HALO, POOL_W), F32), pltpu.VMEM((8, LRU_W), F32),
                        pltpu.VMEM((8, LRU_W), F32), pltpu.VMEM((HALO, POOL_W), F32)]
        + [pltpu.VMEM((2, 256, 256), BF16)] * 3 + [pltpu.VMEM((16, LRU_W), F32)] + [pltpu.VMEM((2, 256, 256), F32)] * 3
        + (_ChipExchange.scratch(n_x) if n_x else []),
        compiler_params=pltpu.CompilerParams(dimension_semantics=("arbitrary",),
                                             collective_id=COLLECTIVE_IDS["mixer_bwd"] if n_x else None),
    )(d2, u, u, hs, hs, saved, pv, wa, wx, wp, w_out_b, *chip_sums)
    return outs[0], outs[1], list(outs[2:])


def _mix_in_bwd(du, x, d2, w_in_t, g_mix, tm):
    T = x.shape[0]

    def body(du_ref, x_ref, d2_ref, w_ref, g_ref, dx_ref, dg_ref):
        @pl.when(pl.program_id(0) == 0)
        def _():
            dg_ref[...] = jnp.zeros((1, D_MODEL), F32)

        dh = jnp.dot(du_ref[...], w_ref[...], preferred_element_type=F32)
        xv = x_ref[...]
        rstd = _rstd(xv)
        dx, dgain = _rms_bwd(dh, xv * rstd, rstd, g_ref[...])
        dx_ref[...] = d2_ref[...] + dx
        dg_ref[...] += dgain

    row = lambda w: pl.BlockSpec((tm, w), lambda i: (i, 0))
    const = lambda shape: pl.BlockSpec(shape, lambda i: (0,) * len(shape))
    return pl.pallas_call(
        body, name="mix_in_bwd", grid=(T // tm,),
        in_specs=[row(D_IN), row(D_MODEL), row(D_MODEL), const((D_IN, D_MODEL)), const((1, D_MODEL))],
        out_specs=[row(D_MODEL), const((1, D_MODEL))],
        out_shape=[_sds((T, D_MODEL), F32), _sds((1, D_MODEL), F32)],
        compiler_params=pltpu.CompilerParams(dimension_semantics=("arbitrary",)),
    )(du, x, d2, w_in_t, g_mix)


def _adamw(w, g, m, v):
    m = ADAM_B1 * m + (1.0 - ADAM_B1) * g
    v = ADAM_B2 * v + (1.0 - ADAM_B2) * (g * g)
    m_hat = m / (1.0 - ADAM_B1 ** ADAM_STEP)
    v_hat = v / (1.0 - ADAM_B2 ** ADAM_STEP)
    delta = -ADAM_LR * (m_hat / (jnp.sqrt(v_hat) + ADAM_EPS) + ADAM_WD * w)
    return delta, m, v


def _adam_shards(ws, ms, vs, parts):
    n = len(ws)
    n_blk = [w.shape[0] // ADAM_ROWS for w in ws]

    def body(*refs):
        w_refs, m_refs, v_refs, p_refs, outs = (refs[:n], refs[n:2 * n], refs[2 * n:3 * n], refs[3 * n:4 * n],
                                                refs[4 * n:])
        i = pl.program_id(0)
        for a in range(n):
            @pl.when(i < n_blk[a])
            def _(a=a):
                g = p_refs[a][0].astype(F32)
                for j in range(1, 4):
                    g = g + p_refs[a][j].astype(F32)
                delta, new_m, new_v = _adamw(w_refs[a][...], g, m_refs[a][...], v_refs[a][...])
                for kind, val in enumerate((g, delta, new_m, new_v)):
                    outs[4 * a + kind][...] = val

    blk = lambda a: pl.BlockSpec((ADAM_ROWS, D_MODEL), lambda i: (jnp.minimum(i, n_blk[a] - 1), 0))
    part_blk = lambda a: pl.BlockSpec((4, ADAM_ROWS, D_MODEL), lambda i: (0, jnp.minimum(i, n_blk[a] - 1), 0))
    res = pl.pallas_call(
        body, name="adam_shards", grid=(max(n_blk),),
        in_specs=[blk(a) for a in range(n)] * 3 + [part_blk(a) for a in range(n)],
        out_specs=[blk(a) for a in range(n) for _ in range(4)],
        out_shape=[_sds(w.shape, F32) for w in ws for _ in range(4)],
        compiler_params=pltpu.CompilerParams(dimension_semantics=("arbitrary",)),
    )(*ws, *ms, *vs, *parts)
    return [tuple(res[4 * a:4 * a + 4]) for a in range(n)]


SMALL_PARAMS = [("norm_mix_g", (1, D_MODEL)), ("conv_w", (1, 4, 64)), ("conv_b", (1, LRU_W)),
                ("gate_a_w", (1, 8, 64, 64)), ("gate_a_b", (1, LRU_W)), ("gate_x_w", (1, 8, 64, 64)),
                ("gate_x_b", (1, LRU_W)), ("lru_lambda", (1, LRU_W)), ("pool_w", (1, 4, 128, 128)),
                ("pool_b", (1, POOL_W)), ("pool_scale", (1, POOL_W)), ("norm_lru_g", (1, LRU_W)),
                ("norm_pool_g", (1, POOL_W)), ("norm_ffn_g", (1, D_MODEL)), ("final_norm_g", (1, D_MODEL))]
VEC_ROW = dict(conv_b=ROW_CB, gate_a_b=ROW_BA, gate_x_b=ROW_BX, lru_lambda=ROW_LAM, pool_b=ROW_PB, pool_scale=ROW_PS,
               norm_lru_g=ROW_GL, norm_pool_g=ROW_GP)
WHOLE = (Ellipsis,)


def _unpack_mixer_grads(sg, dev):
    vec = jnp.concatenate([sg[SG_VEC:SG_VEC + 16], sg[SG_VEC + 16:SG_VEC + 32]], axis=1)
    out = {nm: [(WHOLE, vec[r:r + 1])] for nm, r in VEC_ROW.items()}
    own = jnp.zeros((4, 64), F32)
    for d in range(N_DEV):
        own = jnp.where(dev == d, vec[ROW_CW:ROW_CW + 4, 64 * d:64 * (d + 1)], own)
    out["conv_w"] = [((0,), own)]
    for nm, row0 in (("gate_a_w", SG_WA), ("gate_x_w", SG_WX)):
        out[nm] = [((0, b), sg[row0 + 64 * (b // 4):row0 + 64 * (b // 4 + 1), 64 * (b % 4):64 * (b % 4 + 1)])
                   for b in range(8)]
    out["pool_w"] = [((0, b), sg[SG_WP + 128 * (b // 2):SG_WP + 128 * (b // 2 + 1), 128 * (b % 2):128 * (b % 2 + 1)])
                     for b in range(4)]
    return out


def _adam_small(parts, w, m, v):
    names = [nm for nm, _ in SMALL_PARAMS]
    n = len(names)

    def body(sg_ref, gm_ref, gf_ref, gn_ref, ls_ref, *rest):
        w_refs, m_refs, v_refs, outs = rest[:n], rest[n:2 * n], rest[2 * n:3 * n], rest[3 * n:]
        dev = 4 * lax.axis_index("x") + 2 * lax.axis_index("y") + lax.axis_index("c")

        def total(ref):
            acc = ref[0]
            for d in range(1, N_DEV):
                acc = acc + ref[d]
            return acc

        pieces = _unpack_mixer_grads(total(sg_ref), dev)
        pieces["norm_mix_g"] = [(WHOLE, total(gm_ref))]
        pieces["norm_ffn_g"] = [(WHOLE, total(gf_ref))]
        pieces["final_norm_g"] = [(WHOLE, total(gn_ref))]
        for i, nm in enumerate(names):
            for idx, g in pieces[nm]:
                delta, new_m, new_v = _adamw(w_refs[i][idx], g, m_refs[i][idx], v_refs[i][idx])
                for kind, val in enumerate((g, delta, new_m, new_v)):
                    outs[4 * i + kind][idx] = val
        outs[4 * n][...] = total(ls_ref)

    shapes = [_sds(shape, F32) for _, shape in SMALL_PARAMS for _ in range(4)] + [_sds((8, 128), F32)]
    res = pl.pallas_call(body, name="adam_small", out_shape=shapes)(
        *parts, *[w[nm] for nm in names], *[m[nm] for nm in names], *[v[nm] for nm in names])
    return {nm: tuple(res[4 * i:4 * i + 4]) for i, nm in enumerate(names)}, res[4 * n][0, 0]


def _vec_rows(conv_w_full, conv_b, ba, bx, lam, pb, ps, gl, gp):
    return jnp.concatenate([conv_w_full, conv_b, ba, bx, lam, pb, ps, gl, gp, jnp.zeros((4, LRU_W), F32)], axis=0)


WEIGHT_ORDER = ['norm_mix_g', 'w_in', 'conv_w', 'conv_b', 'gate_a_w', 'gate_a_b', 'gate_x_w', 'gate_x_b', 'lru_lambda',
                'pool_w', 'pool_b', 'pool_scale', 'norm_lru_g', 'norm_pool_g', 'w_out', 'norm_ffn_g', 'ffn_w1', 'ffn_w3',
                'ffn_w2', 'final_norm_g']


def kernel(x, norm_mix_g, w_in, conv_w, conv_b, gate_a_w, gate_a_b, gate_x_w, gate_x_b, lru_lambda, pool_w, pool_b, pool_scale, norm_lru_g, norm_pool_g, w_out, norm_ffn_g, ffn_w1, ffn_w3, ffn_w2, final_norm_g, loss_target, m_norm_mix_g, m_w_in, m_conv_w, m_conv_b, m_gate_a_w, m_gate_a_b, m_gate_x_w, m_gate_x_b, m_lru_lambda, m_pool_w, m_pool_b, m_pool_scale, m_norm_lru_g, m_norm_pool_g, m_w_out, m_norm_ffn_g, m_ffn_w1, m_ffn_w3, m_ffn_w2, m_final_norm_g, v_norm_mix_g, v_w_in, v_conv_w, v_conv_b, v_gate_a_w, v_gate_a_b, v_gate_x_w, v_gate_x_b, v_lru_lambda, v_pool_w, v_pool_b, v_pool_scale, v_norm_lru_g, v_norm_pool_g, v_w_out, v_norm_ffn_g, v_ffn_w1, v_ffn_w3, v_ffn_w2, v_final_norm_g):
    ac = lax.axis_index("c")
    tm, tmx, tm_in, tk = 512, 512, 1024, 2048
    xs, tgt = x[0], loss_target[0]
    g_fin = final_norm_g.reshape(1, D_MODEL)
    c_arr = jnp.reshape(ac, (1,)).astype(jnp.int32)

    tr = lambda w: jnp.swapaxes(w[0], 0, 1)
    own = lambda w: w[0]
    bf = lambda a: a.astype(BF16)

    h1, (g_in, g_conv) = _norm_in(xs, norm_mix_g, [bf(tr(w_in)), conv_w[0]], tm_in)
    w_in_t = g_in.reshape(D_IN, D_MODEL)
    u, (g_out,) = _mix_in(h1, w_in_t, tm_in, shards=[bf(own(w_out))])
    conv_w_full = g_conv.transpose(1, 0, 2).reshape(4, LRU_W)
    pv = _vec_rows(conv_w_full, conv_b, gate_a_b, gate_x_b, lru_lambda, pool_b, pool_scale, norm_lru_g, norm_pool_g)
    wa, wx, wp = gate_a_w[0], gate_x_w[0], pool_w[0]
    w_out_b = g_out.reshape(D_MODEL, D_MODEL)
    y, hs, hres, h2, saved, (g_w1, g_w3, g_w2) = _mixer_fwd(
        u, xs, pv, wa, wx, wp, w_out_b, norm_ffn_g, tmx, shards=[bf(tr(ffn_w1)), bf(tr(ffn_w3)), bf(own(ffn_w2))])
    w1_t, w3_t, w2_b = g_w1.reshape(D_FF, D_MODEL), g_w3.reshape(D_FF, D_MODEL), g_w2.reshape(D_FF, D_MODEL)
    g, v, ff, d3, loss_acc, d_gfin = _ffn_fwd(hres, h2, w1_t, w3_t, w2_b, g_fin, tgt, tm)

    dg, dv, d2, d_gffn = _ffn_bwd(d3, g, v, w1_t, w3_t, w2_b, hres, norm_ffn_g, tm)
    chips = lambda a: a.reshape(4, a.shape[0] // 4, a.shape[1])
    early = [(y, d2, "grad_w_out"), (dg, h2, "grad_w1"), (dv, h2, "grad_w3"), (ff, d3, "grad_w2")]
    early_sums = [chips(_at_b_pair(a, b, c_arr, name, tk)) for a, b, name in early]
    du, d_mixer, early_parts = _mixer_bwd(d2, u, hs, saved, pv, wa, wx, wp, w_out_b, tmx, chip_sums=early_sums)
    grad_x, d_gmix = _mix_in_bwd(du, xs, d2, w_in_t, norm_mix_g, tm_in)
    d_win, small_parts = _at_b_pair(du, h1, c_arr, "grad_w_in", tk,
                                    gather=[d_mixer, d_gmix, d_gffn, d_gfin, loss_acc])
    parts = [_half_exchange(chips(d_win), "grads_to_chips_w_in")] + list(early_parts)

    res = {}
    shard_w = dict(w_in=(w_in, m_w_in, v_w_in, tr), w_out=(w_out, m_w_out, v_w_out, own),
                   ffn_w1=(ffn_w1, m_ffn_w1, v_ffn_w1, tr), ffn_w3=(ffn_w3, m_ffn_w3, v_ffn_w3, tr),
                   ffn_w2=(ffn_w2, m_ffn_w2, v_ffn_w2, own))
    shard_res = _adam_shards([view(w) for w, _, _, view in shard_w.values()],
                             [view(m) for _, m, _, view in shard_w.values()],
                             [view(v) for _, _, v, view in shard_w.values()], parts)
    for (nm, (_, _, _, view)), outs in zip(shard_w.items(), shard_res):
        res[nm] = [(jnp.swapaxes(o, 0, 1) if view is tr else o)[None] for o in outs]

    row = lambda a: a.reshape(1, D_MODEL)
    small = lambda gm, cw, cb, wa_, ba, wx_, bx, lam, pw, pb, ps, gl, gp, gf, gn: dict(
        norm_mix_g=gm, conv_w=cw, conv_b=cb, gate_a_w=wa_, gate_a_b=ba, gate_x_w=wx_, gate_x_b=bx, lru_lambda=lam,
        pool_w=pw, pool_b=pb, pool_scale=ps, norm_lru_g=gl, norm_pool_g=gp, norm_ffn_g=gf, final_norm_g=row(gn))
    small_res, loss = _adam_small(
        small_parts,
        small(norm_mix_g, conv_w, conv_b, gate_a_w, gate_a_b, gate_x_w, gate_x_b, lru_lambda, pool_w, pool_b,
              pool_scale, norm_lru_g, norm_pool_g, norm_ffn_g, final_norm_g),
        small(m_norm_mix_g, m_conv_w, m_conv_b, m_gate_a_w, m_gate_a_b, m_gate_x_w, m_gate_x_b, m_lru_lambda, m_pool_w,
              m_pool_b, m_pool_scale, m_norm_lru_g, m_norm_pool_g, m_norm_ffn_g, m_final_norm_g),
        small(v_norm_mix_g, v_conv_w, v_conv_b, v_gate_a_w, v_gate_a_b, v_gate_x_w, v_gate_x_b, v_lru_lambda, v_pool_w,
              v_pool_b, v_pool_scale, v_norm_lru_g, v_norm_pool_g, v_norm_ffn_g, v_final_norm_g))
    for nm, outs in small_res.items():
        res[nm] = [o.reshape(D_MODEL) for o in outs] if nm == "final_norm_g" else list(outs)

    out = [loss, grad_x[None]]
    for kind in range(4):
        out += [res[nm][kind] for nm in WEIGHT_ORDER]
    return tuple(out)
```

```python
import functools

import jax
import jax.numpy as jnp
from jax import lax
from jax.experimental import pallas as pl
from jax.experimental.pallas import tpu as pltpu

F32 = jnp.float32
BF16 = jnp.bfloat16

D_MODEL = 1024
LRU_W = 512
POOL_W = 512
D_IN = 1536
D_FF = 2816
POOL_WINDOWS = (2, 4, 8, 16)
EPS = 1e-6
LRU_C = 8.0
N_DEV = 8
HALO = 16
SCAN_UNROLL = 8
ADAM_ROWS = 32
FF_CHUNKS = ((0, 1536), (1536, 2816))
RELAY_SPLIT_ROWS = 32
COLLECTIVE_IDS = {name: j for j, name in enumerate(
    ("grad_w_out", "grad_w1", "grad_w3", "grad_w2", "norm_in", "mix_in", "mixer_fwd", "mixer_bwd", "grad_w_in",
     "grads_to_chips_w_in"))}

ADAM_LR = 0.001
ADAM_B1 = 0.9
ADAM_B2 = 0.999
ADAM_EPS = 1e-08
ADAM_WD = 0.01
ADAM_STEP = 10

ROW_CW, ROW_CB, ROW_BA, ROW_BX, ROW_LAM, ROW_PB, ROW_PS, ROW_GL, ROW_GP = 0, 4, 5, 6, 7, 8, 9, 10, 11
SG_VEC, SG_WA, SG_WX, SG_WP, SG_ROWS = 0, 32, 160, 288, 544

NT = (((1,), (1,)), ((), ()))
TN = (((0,), (0,)), ((), ()))


def _sds(shape, dtype):
    return jax.ShapeDtypeStruct(shape, dtype)


def _sigmoid(x):
    return 0.5 * jnp.tanh(0.5 * x) + 0.5


def _gelu_parts(x):
    c = 0.7978845608028654
    inner = c * (x + 0.044715 * (x * x * x))
    th = jnp.tanh(inner)
    g = 0.5 * x * (1.0 + th)
    dg = 0.5 * (1.0 + th) + 0.5 * x * (1.0 - th * th) * (c * (1.0 + 3.0 * 0.044715 * (x * x)))
    return g, dg


def _window_sum(ext, w, back):
    n = ext.shape[0]
    s, k = ext, 1
    while k < w:
        s = s + pltpu.roll(s, k if back else n - k, 0)
        k *= 2
    return s


def _rstd(x):
    return lax.rsqrt(jnp.mean(x * x, axis=-1, keepdims=True) + EPS)


def _rms_bwd(dy, xhat, rstd, gain):
    dxh = dy * gain
    dx = rstd * (dxh - xhat * jnp.mean(dxh * xhat, axis=-1, keepdims=True))
    return dx, jnp.sum(dy * xhat, axis=0, keepdims=True)


def _bd(xb, w_ref):
    return jnp.concatenate(
        [jnp.dot(xb[:, :256], w_ref[0], preferred_element_type=F32),
         jnp.dot(xb[:, 256:], w_ref[1], preferred_element_type=F32)], axis=1)


def _bd_t(xb, w_ref):
    return jnp.concatenate(
        [lax.dot_general(xb[:, :256], w_ref[0], NT, preferred_element_type=F32),
         lax.dot_general(xb[:, 256:], w_ref[1], NT, preferred_element_type=F32)], axis=1)


def _bd_grad(xb, db):
    return jnp.stack(
        [lax.dot_general(xb[:, :256], db[:, :256], TN, preferred_element_type=F32),
         lax.dot_general(xb[:, 256:], db[:, 256:], TN, preferred_element_type=F32)], axis=0)


def _fill_block_diag(dst, src_ref):
    n, k, _ = src_ref.shape
    dst[...] = jnp.zeros(dst.shape, BF16)
    for b in range(n):
        p, q = divmod(b, 256 // k)
        dst[p, q * k:(q + 1) * k, q * k:(q + 1) * k] = src_ref[b].astype(BF16)


def _diag_pack(w, k):
    lane = lax.broadcasted_iota(jnp.int32, (k, 256), 1)
    out = w[0:k]
    for q in range(1, 256 // k):
        out = jnp.where(lane >= q * k, w[q * k:(q + 1) * k], out)
    return out


def _y_pos(b):
    return 4 * (b % 2) + b // 2


def _softplus_neg_lambda(pv):
    z = -pv[ROW_LAM:ROW_LAM + 1, :]
    return jnp.maximum(z, 0.0) + jnp.log(1.0 + jnp.exp(-jnp.abs(z)))


def _lru_gates(e_lru, pv, wa_ref, wx_ref, tm):
    xc = pv[ROW_CB:ROW_CB + 1, :]
    for k in range(4):
        xc = xc + e_lru[pl.ds(HALO - 3 + k, tm), :] * pv[ROW_CW + k:ROW_CW + k + 1, :]
    xcb = xc.astype(BF16)
    r = _sigmoid(_bd(xcb, wa_ref) + pv[ROW_BA:ROW_BA + 1, :])
    ig = _sigmoid(_bd(xcb, wx_ref) + pv[ROW_BX:ROW_BX + 1, :])
    return xc, r, ig, (-LRU_C * r) * _softplus_neg_lambda(pv)


def _lru_decay(la):
    a = jnp.exp(la)
    om = -jnp.tanh(la) * (1.0 + a * a)
    omc = jnp.maximum(om, 1e-12)
    rmult = lax.rsqrt(omc)
    return a, om, omc * rmult, rmult


def _over_count(v, w, inv_head):
    return jnp.concatenate([v[0:HALO] * inv_head, v[HALO:] * (1.0 / w)], axis=0)


def _pool_pre(e_pool, pv, wp_ref, tm, t0):
    t_head = t0 + lax.broadcasted_iota(jnp.int32, (HALO, 1), 0)
    parts, inv_heads = [], []
    for g, w in enumerate(POOL_WINDOWS):
        ext = e_pool[:, pl.ds(128 * g, 128)]
        s = _window_sum(ext, w, back=True)[HALO:, :]
        inv_head = 1.0 / jnp.minimum(t_head + 1, w).astype(F32)
        inv_heads.append(inv_head)
        parts.append(_over_count(s, w, inv_head) - ext[HALO:, :])
    pooled = jnp.concatenate(parts, axis=1)
    pooled_b = pooled.astype(BF16)
    zp = _bd(pooled_b, wp_ref) + pv[ROW_PB:ROW_PB + 1, :]
    return pooled_b, zp, inv_heads


def _scan_tile(a_ref, b_ref, out_ref, carry, tm, reverse):
    row = lax.broadcasted_iota(jnp.int32, (8, LRU_W), 0)
    nblk = tm // 8

    def local_scan(blk):
        r0 = pl.multiple_of(blk * 8, 8)
        av = a_ref[pl.ds(r0, 8), :]
        bv = b_ref[pl.ds(r0, 8), :]
        for d in (1, 2, 4):
            sh = (8 - d) if reverse else d
            a_s = pltpu.roll(av, sh, 0)
            b_s = pltpu.roll(bv, sh, 0)
            m = (row < 8 - d) if reverse else (row >= d)
            bv = jnp.where(m, av * b_s + bv, bv)
            av = jnp.where(m, av * a_s, av)
        return r0, av, bv

    def step(i, hin):
        local = [local_scan((nblk - 1 - (i * SCAN_UNROLL + j)) if reverse else (i * SCAN_UNROLL + j))
                 for j in range(SCAN_UNROLL)]
        for r0, av, bv in local:
            hv = av * hin + bv
            out_ref[pl.ds(r0, 8), :] = hv
            hin = jnp.broadcast_to(hv[0:1, :] if reverse else hv[7:8, :], (8, LRU_W))
        return hin

    return lax.fori_loop(0, nblk // SCAN_UNROLL, step, carry)


MESH = pl.DeviceIdType.MESH
ANY = pl.BlockSpec(memory_space=pl.ANY)


def _place():
    x, y, c = lax.axis_index("x"), lax.axis_index("y"), lax.axis_index("c")
    chips = [(1 - x, y), (x, 1 - y), (1 - x, 1 - y)]
    return x, y, c, chips


def _meet(peers):
    _announce(peers)
    _await(peers)


def _announce(peers):
    for peer in peers:
        pl.semaphore_signal(pltpu.get_barrier_semaphore(), inc=1, device_id=peer, device_id_type=MESH)


def _await(peers):
    pl.semaphore_wait(pltpu.get_barrier_semaphore(), len(peers))


class _Gather:
    def __init__(self, ins, outs, send_sems, recv_sems, local_sems, core_major=False):
        self.ins, self.outs, self.n = ins, outs, len(ins)
        self.send_sems, self.recv_sems, self.local_sems = send_sems, recv_sems, local_sems
        self.core_major = core_major

    @staticmethod
    def scratch(n):
        return [pltpu.SemaphoreType.DMA((8, n)), pltpu.SemaphoreType.DMA((8, n)), pltpu.SemaphoreType.DMA((n,))]

    def _slot(self, a, px, py, pc):
        return self.outs[a].at[4 * pc + 2 * px + py if self.core_major else 4 * px + 2 * py + pc]

    def _half(self, a, h):
        rows = self.ins[a].shape[0]
        if rows % RELAY_SPLIT_ROWS:
            return None if h else (0, rows)
        return (h * (rows // 2), rows // 2)

    def _copy(self, a, k, block, to, src=None, rows=None):
        dst = self._slot(a, *block)
        src = dst if src is None else src
        if rows is not None:
            src, dst = src.at[pl.ds(*rows)], dst.at[pl.ds(*rows)]
        return pltpu.make_async_remote_copy(
            src_ref=src, dst_ref=dst, send_sem=self.send_sems.at[k, a], recv_sem=self.recv_sems.at[k, a],
            device_id=to, device_id_type=MESH)

    def _mine(self, a):
        x, y, c, _ = _place()
        return pltpu.make_async_copy(self.ins[a], self._slot(a, x, y, c), self.local_sems.at[a])

    def _first(self, a):
        x, y, c, chips = _place()
        me = (x, y, c)
        return ([self._copy(a, 0, me, (x, y, 1 - c), src=self.ins[a])]
                + [self._copy(a, 1 + j, me, (*chip, c), src=self.ins[a]) for j, chip in enumerate(chips[:2])])

    def _passed_on(self, a, h):
        x, y, c, chips = _place()
        block = (*chips[h], c)
        out = [self._copy(a, 4 + h, block, (x, y, 1 - c))]
        if self._half(a, h) is not None:
            out.append(self._copy(a, (3, 7)[h], block, (*chips[1 - h], c), rows=self._half(a, h)))
        return out

    def _peers(self):
        x, y, c, chips = _place()
        return [(x, y, 1 - c), (*chips[0], c), (*chips[1], c)]

    def announce(self):
        _announce(self._peers())

    def start(self, announced=False):
        (_await if announced else _meet)(self._peers())
        for a in range(self.n):
            self._mine(a).start()
        for a in range(self.n):
            for cp in self._first(a):
                cp.start()

    def relay(self):
        x, y, c, chips = _place()
        for h in range(2):
            for a in range(self.n):
                self._copy(a, 1 + h, (*chips[h], c), (x, y, c)).wait_recv()
                for cp in self._passed_on(a, h):
                    cp.start()

    def finish(self):
        x, y, c, chips = _place()
        me, sibling = (x, y, c), (x, y, 1 - c)
        passed = []
        for a in range(self.n):
            for h in range(2):
                if self._half(a, h) is not None:
                    self._copy(a, (3, 7)[h], (*chips[2], c), me, rows=self._half(a, h)).wait_recv()
            fwd = self._copy(a, 6, (*chips[2], c), sibling)
            fwd.start()
            passed.append(fwd)
        for a in range(self.n):
            self._copy(a, 0, (x, y, 1 - c), me).wait_recv()
            for j, chip in enumerate(chips):
                self._copy(a, 4 + j, (*chip, 1 - c), me).wait_recv()
        for a in range(self.n):
            for cp in self._first(a) + self._passed_on(a, 0) + self._passed_on(a, 1):
                cp.wait_send()
        for cp in passed:
            cp.wait_send()
        for a in range(self.n):
            self._mine(a).wait()


def _half_exchange(arr, name):
    def body(in_ref, out_ref, send_sems, recv_sems, local_sem):
        x, y, c, _ = _place()
        my_chip = 2 * x + y
        _meet([((x + dx) % 2, (y + dy) % 2, (c + dc) % 2)
               for dx in range(2) for dy in range(2) for dc in range(2) if dx + dy + dc])

        def send(j, wait):
            to_me = (c == x) & (y == j // 2) & (c == j % 2)

            @pl.when(to_me)
            def _():
                local = pltpu.make_async_copy(in_ref.at[j], out_ref.at[my_chip], local_sem)
                local.wait() if wait else local.start()

            @pl.when(jnp.logical_not(to_me))
            def _():
                remote = pltpu.make_async_remote_copy(
                    src_ref=in_ref.at[j], dst_ref=out_ref.at[my_chip], send_sem=send_sems.at[j],
                    recv_sem=recv_sems.at[my_chip], device_id=(c, j // 2, j % 2), device_id_type=MESH)
                remote.wait_send() if wait else remote.start()

        for j in range(4):
            send(j, wait=False)
        for j in range(4):
            send(j, wait=True)
        for k in range(4):
            from_me = (k // 2 == x) & (k % 2 == y) & (c == x)

            @pl.when(jnp.logical_not(from_me))
            def _():
                pltpu.make_async_remote_copy(
                    src_ref=in_ref.at[0], dst_ref=out_ref.at[k], send_sem=send_sems.at[0], recv_sem=recv_sems.at[k],
                    device_id=(k // 2, k % 2, x), device_id_type=MESH).wait_recv()

    return pl.pallas_call(
        body, name=name, out_shape=_sds(arr.shape, arr.dtype), in_specs=[ANY], out_specs=ANY,
        scratch_shapes=[pltpu.SemaphoreType.DMA((4,)), pltpu.SemaphoreType.DMA((4,)), pltpu.SemaphoreType.DMA],
        compiler_params=pltpu.CompilerParams(collective_id=COLLECTIVE_IDS[name]),
    )(arr)


class _ChipExchange:
    def __init__(self, ins, outs, send_sems, recv_sems, local_sems):
        self.ins, self.outs, self.n = ins, outs, len(ins)
        self.send_sems, self.recv_sems, self.local_sems = send_sems, recv_sems, local_sems

    @staticmethod
    def scratch(n):
        return [pltpu.SemaphoreType.DMA((3, n)), pltpu.SemaphoreType.DMA((3, n)), pltpu.SemaphoreType.DMA((n,))]

    def _local(self, a):
        x, y, _, _ = _place()
        me = 2 * x + y
        return pltpu.make_async_copy(self.ins[a].at[me], self.outs[a].at[me], self.local_sems.at[a])

    def _copies(self, a):
        x, y, c, chips = _place()
        me = 2 * x + y
        return [(pltpu.make_async_remote_copy(
                     src_ref=self.ins[a].at[2 * px + py], dst_ref=self.outs[a].at[me],
                     send_sem=self.send_sems.at[k, a], recv_sem=self.recv_sems.at[k, a],
                     device_id=(px, py, c), device_id_type=MESH),
                 pltpu.make_async_remote_copy(
                     src_ref=self.ins[a].at[me], dst_ref=self.outs[a].at[2 * px + py],
                     send_sem=self.send_sems.at[k, a], recv_sem=self.recv_sems.at[k, a],
                     device_id=(px, py, c), device_id_type=MESH))
                for k, (px, py) in enumerate(chips)]

    def _peers(self):
        _, _, c, chips = _place()
        return [(*chip, c) for chip in chips]

    def announce(self):
        _announce(self._peers())

    def start(self):
        _await(self._peers())
        for a in range(self.n):
            self._local(a).start()
        for a in range(self.n):
            for send, _ in self._copies(a):
                send.start()

    def finish(self):
        for a in range(self.n):
            for send, recv in self._copies(a):
                send.wait_send()
                recv.wait_recv()
        for a in range(self.n):
            self._local(a).wait()


def _gathering(body, n_steps, n_s, core_major, start_at=0):
    def wrapped(*refs, n_in, n_out):
        ins, sh_in = refs[:n_in], refs[n_in:n_in + n_s]
        outs, sh_out = refs[n_in + n_s:n_in + n_s + n_out], refs[n_in + n_s + n_out:n_in + 2 * n_s + n_out]
        rest = refs[n_in + 2 * n_s + n_out:]
        gather = _Gather(sh_in, sh_out, *rest[len(rest) - 3:], core_major=core_major)
        i = pl.program_id(0)

        if start_at:
            @pl.when(i == 0)
            def _():
                gather.announce()

        @pl.when(i == start_at)
        def _():
            gather.start(announced=start_at > 0)

        @pl.when(i == n_steps // 2)
        def _():
            gather.relay()

        body(*ins, *outs, *rest[:len(rest) - 3])

        @pl.when(i == n_steps - 1)
        def _():
            gather.finish()

    return wrapped


def _norm_in(x, g_mix, shards, tm):
    T = x.shape[0]
    n_t = T // tm
    n_s = len(shards)

    def norm(x_ref, g_ref, h_ref):
        xv = x_ref[...]
        h_ref[...] = (xv * _rstd(xv) * g_ref[...]).astype(BF16)

    outs = pl.pallas_call(
        functools.partial(_gathering(norm, n_t, n_s, core_major=False), n_in=2, n_out=1), name="norm_in", grid=(n_t,),
        in_specs=[pl.BlockSpec((tm, D_MODEL), lambda i: (i, 0)), pl.BlockSpec((1, D_MODEL), lambda i: (0, 0))]
        + [ANY] * n_s,
        out_specs=[pl.BlockSpec((tm, D_MODEL), lambda i: (i, 0))] + [ANY] * n_s,
        out_shape=[_sds((T, D_MODEL), BF16)] + [_sds((N_DEV,) + a.shape, a.dtype) for a in shards],
        scratch_shapes=_Gather.scratch(n_s),
        compiler_params=pltpu.CompilerParams(dimension_semantics=("arbitrary",), collective_id=COLLECTIVE_IDS["norm_in"]),
    )(x, g_mix, *shards)
    return outs[0], list(outs[1:])


def _mix_in(h1, w_in_t, tm, shards):
    T = h1.shape[0]
    n_t = T // tm
    n_s = len(shards)

    def project(h_ref, w_ref, u_ref):
        u_ref[...] = lax.dot_general(h_ref[...], w_ref[...], NT, preferred_element_type=F32)

    outs = pl.pallas_call(
        functools.partial(_gathering(project, n_t, n_s, core_major=True, start_at=1), n_in=2, n_out=1), name="mix_in",
        grid=(n_t,),
        in_specs=[pl.BlockSpec((tm, D_MODEL), lambda i: (i, 0)), pl.BlockSpec((D_IN, D_MODEL), lambda i: (0, 0))]
        + [ANY] * n_s,
        out_specs=[pl.BlockSpec((tm, D_IN), lambda i: (i, 0))] + [ANY] * n_s,
        out_shape=[_sds((T, D_IN), F32)] + [_sds((N_DEV,) + a.shape, a.dtype) for a in shards],
        scratch_shapes=_Gather.scratch(n_s),
        compiler_params=pltpu.CompilerParams(dimension_semantics=("arbitrary",), collective_id=COLLECTIVE_IDS["mix_in"]),
    )(h1, w_in_t, *shards)
    return outs[0], list(outs[1:])


def _mixer_fwd(u, x, pv, wa, wx, wp, w_out_b, g_ffn, tm, shards=()):
    T = u.shape[0]
    n_s = len(shards)
    n_t = T // tm

    def body(u_ref, x_ref, pv_ref, wa_in, wx_in, wp_in, wo_ref, gf_ref, *rest):
        sh_in, rest = rest[:n_s], rest[n_s:]
        y_ref, hs_ref, hres_ref, h2_ref, saved_ref = rest[:5]
        sh_out, rest = rest[5:5 + n_s], rest[5 + n_s:]
        e_lru, e_pool, a_s, b_s, hc, wa_ref, wx_ref, wp_ref = rest[:8]
        gather = _Gather(sh_in, sh_out, *rest[8:], core_major=True) if n_s else None
        i = pl.program_id(0)

        @pl.when(i == 0)
        def _():
            if gather:
                gather.start()
            e_lru[pl.ds(0, HALO), :] = jnp.zeros((HALO, LRU_W), F32)
            e_pool[pl.ds(0, HALO), :] = jnp.zeros((HALO, POOL_W), F32)
            hc[...] = jnp.zeros((8, LRU_W), F32)
            _fill_block_diag(wa_ref, wa_in)
            _fill_block_diag(wx_ref, wx_in)
            _fill_block_diag(wp_ref, wp_in)

        if gather:
            @pl.when(i == (2 * n_t) // 3)
            def _():
                gather.relay()

        e_lru[pl.ds(HALO, tm), :] = u_ref[:, 0:LRU_W]
        e_pool[pl.ds(HALO, tm), :] = u_ref[:, 2 * LRU_W:D_IN]
        pv = pv_ref[...]
        xc, r, ig, la = _lru_gates(e_lru, pv, wa_ref, wx_ref, tm)
        for q, val in enumerate((xc, r, ig, la)):
            saved_ref[:, LRU_W * q:LRU_W * (q + 1)] = val
        a, _, mult, _ = _lru_decay(la)
        a_s[...] = a
        b_s[...] = mult * (ig * xc)
        hc[...] = _scan_tile(a_s, b_s, hs_ref, hc[...], tm, reverse=False)
        gl, _ = _gelu_parts(u_ref[:, LRU_W:2 * LRU_W])
        y_lru = hs_ref[...] * gl
        _, zp, _ = _pool_pre(e_pool, pv, wp_ref, tm, i * tm)
        y_pool = zp * pv[ROW_PS:ROW_PS + 1, :]
        yn = jnp.concatenate([y_lru * _rstd(y_lru) * pv[ROW_GL:ROW_GL + 1, :],
                              y_pool * _rstd(y_pool) * pv[ROW_GP:ROW_GP + 1, :]], axis=1).astype(BF16)
        for b in range(N_DEV):
            y_ref[:, 128 * _y_pos(b):128 * (_y_pos(b) + 1)] = yn[:, 128 * b:128 * (b + 1)]
        hr = x_ref[...] + jnp.dot(y_ref[...], wo_ref[...], preferred_element_type=F32)
        hres_ref[...] = hr
        h2_ref[...] = (hr * _rstd(hr) * gf_ref[...]).astype(BF16)
        e_lru[pl.ds(0, HALO), :] = e_lru[pl.ds(tm, HALO), :]
        e_pool[pl.ds(0, HALO), :] = e_pool[pl.ds(tm, HALO), :]

        if gather:
            @pl.when(i == n_t - 1)
            def _():
                gather.finish()

    full = lambda shape: pl.BlockSpec(shape, lambda i: (0,) * len(shape))
    row = lambda w: pl.BlockSpec((tm, w), lambda i: (i, 0))
    outs = pl.pallas_call(
        body, name="mixer_fwd", grid=(n_t,),
        in_specs=[row(D_IN), row(D_MODEL), full((16, LRU_W)), full((8, 64, 64)), full((8, 64, 64)), full((4, 128, 128)),
                  full((D_MODEL, D_MODEL)), full((1, D_MODEL))] + [ANY] * n_s,
        out_specs=[row(D_MODEL), row(LRU_W), row(D_MODEL), row(D_MODEL), row(4 * LRU_W)] + [ANY] * n_s,
        out_shape=[_sds((T, D_MODEL), BF16), _sds((T, LRU_W), F32), _sds((T, D_MODEL), F32), _sds((T, D_MODEL), BF16),
                   _sds((T, 4 * LRU_W), F32)] + [_sds((N_DEV,) + a.shape, a.dtype) for a in shards],
        scratch_shapes=[pltpu.VMEM((HALO + tm, LRU_W), F32), pltpu.VMEM((HALO + tm, POOL_W), F32),
                        pltpu.VMEM((tm, LRU_W), F32), pltpu.VMEM((tm, LRU_W), F32), pltpu.VMEM((8, LRU_W), F32)]
        + [pltpu.VMEM((2, 256, 256), BF16)] * 3 + (_Gather.scratch(n_s) if n_s else []),
        compiler_params=pltpu.CompilerParams(dimension_semantics=("arbitrary",),
                                             collective_id=COLLECTIVE_IDS["mixer_fwd"] if n_s else None),
    )(u, x, pv, wa, wx, wp, w_out_b, g_ffn, *shards)
    return outs[0], outs[1], outs[2], outs[3], outs[4], list(outs[5:])


def _ffn_fwd(hres, h2, w1_b, w3_b, w2_b, g_fin, tgt, tm):
    T = hres.shape[0]

    def body(hres_ref, h2_ref, w1_ref, w3_ref, w2_ref, gfin_ref, tgt_ref,
             g_ref, v_ref, ff_ref, d3_ref, loss_ref, dgfin_ref):
        @pl.when(pl.program_id(0) == 0)
        def _():
            loss_ref[...] = jnp.zeros((8, 128), F32)
            dgfin_ref[...] = jnp.zeros((1, D_MODEL), F32)

        h2 = h2_ref[...]
        h3 = hres_ref[...]
        for lo, hi in FF_CHUNKS:
            g = lax.dot_general(h2, w1_ref[lo:hi, :], NT, preferred_element_type=F32)
            v = lax.dot_general(h2, w3_ref[lo:hi, :], NT, preferred_element_type=F32)
            g_ref[:, lo:hi] = g.astype(BF16)
            v_ref[:, lo:hi] = v.astype(BF16)
            ff = ((g * _sigmoid(g)) * v).astype(BF16)
            ff_ref[:, lo:hi] = ff
            h3 = h3 + jnp.dot(ff, w2_ref[lo:hi, :], preferred_element_type=F32)

        rstd = _rstd(h3)
        xh = h3 * rstd
        gfin = gfin_ref[...]
        err = xh * gfin - tgt_ref[...]
        loss_ref[...] += 0.5 * jnp.sum(jnp.mean(err * err, axis=-1, keepdims=True))
        dx, dgain = _rms_bwd(err * (1.0 / D_MODEL), xh, rstd, gfin)
        d3_ref[...] = dx
        dgfin_ref[...] += dgain

    row = lambda w: pl.BlockSpec((tm, w), lambda i: (i, 0))
    const = lambda shape: pl.BlockSpec(shape, lambda i: (0,) * len(shape))
    weight = pl.BlockSpec((D_FF, D_MODEL), lambda i: (0, 0), pipeline_mode=pl.Buffered(1))
    return pl.pallas_call(
        body, name="ffn_fwd", grid=(T // tm,),
        in_specs=[row(D_MODEL), row(D_MODEL), weight, weight, weight, const((1, D_MODEL)), row(D_MODEL)],
        out_specs=[row(D_FF), row(D_FF), row(D_FF), row(D_MODEL), const((8, 128)), const((1, D_MODEL))],
        out_shape=[_sds((T, D_FF), BF16), _sds((T, D_FF), BF16), _sds((T, D_FF), BF16),
                   _sds((T, D_MODEL), F32), _sds((8, 128), F32), _sds((1, D_MODEL), F32)],
        compiler_params=pltpu.CompilerParams(dimension_semantics=("arbitrary",)),
    )(hres, h2, w1_b, w3_b, w2_b, g_fin, tgt)


def _ffn_bwd(d3, g, v, w1_b, w3_b, w2_b, hres, g_ffn, tm):
    T = d3.shape[0]

    def body(d3_ref, g_ref, v_ref, w1_ref, w3_ref, w2_ref, hres_ref, gf_ref, dg_ref, dv_ref, d2_ref, dgffn_ref):
        @pl.when(pl.program_id(0) == 0)
        def _():
            dgffn_ref[...] = jnp.zeros((1, D_MODEL), F32)

        d3 = d3_ref[...]
        d3b = d3.astype(BF16)
        dh2 = jnp.zeros((tm, D_MODEL), F32)
        for lo, hi in FF_CHUNKS:
            dff = lax.dot_general(d3b, w2_ref[lo:hi, :], NT, preferred_element_type=F32)
            gv = g_ref[:, lo:hi].astype(F32)
            vv = v_ref[:, lo:hi].astype(F32)
            sg = _sigmoid(gv)
            sl = gv * sg
            dgb = (dff * vv * (sg * (1.0 + gv * (1.0 - sg)))).astype(BF16)
            dvb = (dff * sl).astype(BF16)
            dg_ref[:, lo:hi] = dgb
            dv_ref[:, lo:hi] = dvb
            dh2 = dh2 + (jnp.dot(dgb, w1_ref[lo:hi, :], preferred_element_type=F32)
                         + jnp.dot(dvb, w3_ref[lo:hi, :], preferred_element_type=F32))

        hr = hres_ref[...]
        rstd = _rstd(hr)
        dx, dgain = _rms_bwd(dh2, hr * rstd, rstd, gf_ref[...])
        d2_ref[...] = d3 + dx
        dgffn_ref[...] += dgain

    row = lambda w: pl.BlockSpec((tm, w), lambda i: (i, 0))
    const = lambda shape: pl.BlockSpec(shape, lambda i: (0,) * len(shape))
    weight = pl.BlockSpec((D_FF, D_MODEL), lambda i: (0, 0), pipeline_mode=pl.Buffered(1))
    return pl.pallas_call(
        body, name="ffn_bwd", grid=(T // tm,),
        in_specs=[row(D_MODEL), row(D_FF), row(D_FF), weight, weight, weight, row(D_MODEL), const((1, D_MODEL))],
        out_specs=[row(D_FF), row(D_FF), row(D_MODEL), const((1, D_MODEL))],
        out_shape=[_sds((T, D_FF), BF16), _sds((T, D_FF), BF16), _sds((T, D_MODEL), F32), _sds((1, D_MODEL), F32)],
        compiler_params=pltpu.CompilerParams(dimension_semantics=("arbitrary",)),
    )(d3, g, v, w1_b, w3_b, w2_b, hres, g_ffn)


def _at_b_pair(a, b, c_arr, name, tk, gather=()):
    T, M = a.shape
    N = b.shape[1]
    hm, n_k = M // 2, T // tk
    n_g = len(gather)

    def body(c_ref, a_ref, b_ref, *rest):
        g_in, o_ref, rest = rest[:n_g], rest[n_g], rest[n_g + 1:]
        g_out, rest = rest[:n_g], rest[n_g:]
        acc, landed, send_sem, recv_sem = rest[:4]
        ag = _Gather(g_in, g_out, *rest[4:]) if n_g else None
        ph, k = pl.program_id(0), pl.program_id(1)

        def hand_over():
            x, y, c, _ = _place()
            return pltpu.make_async_remote_copy(
                src_ref=acc.at[0], dst_ref=landed, send_sem=send_sem, recv_sem=recv_sem,
                device_id=(x, y, 1 - c), device_id_type=MESH)

        if ag:
            start_at = min(1, n_k - 1)

            if start_at:
                @pl.when((ph == 0) & (k == 0))
                def _():
                    ag.announce()

            @pl.when((ph == 0) & (k == start_at))
            def _():
                ag.start(announced=start_at > 0)

            @pl.when((ph == 1) & (k == 0))
            def _():
                ag.relay()
        else:
            barrier = pltpu.get_barrier_semaphore()

            @pl.when((ph == 0) & (k == 0))
            def _():
                x, y, c, _ = _place()
                pl.semaphore_signal(barrier, inc=1, device_id=(x, y, 1 - c), device_id_type=MESH)

        @pl.when(k == 0)
        def _():
            acc[ph] = jnp.zeros((hm, N), F32)

        acc[ph] += lax.dot_general(a_ref[...].astype(BF16), b_ref[...].astype(BF16), TN, preferred_element_type=F32)

        @pl.when((ph == 0) & (k == n_k - 1))
        def _():
            if not ag:
                pl.semaphore_wait(barrier, 1)
            hand_over().start()

        @pl.when((ph == 1) & (k == n_k - 1))
        def _():
            copy = hand_over()
            copy.wait_recv()
            o_ref[...] = (acc[1] + landed[...]).astype(BF16)
            copy.wait_send()
            if ag:
                ag.finish()

    outs = pl.pallas_call(
        body, name=name,
        grid_spec=pltpu.PrefetchScalarGridSpec(
            num_scalar_prefetch=1, grid=(2, n_k),
            in_specs=[pl.BlockSpec((tk, hm), lambda ph, k, c_ref: (k, (ph + 1 - c_ref[0]) % 2)),
                      pl.BlockSpec((tk, N), lambda ph, k, c_ref: (k, 0))] + [ANY] * n_g,
            out_specs=[pl.BlockSpec((hm, N), lambda ph, k, c_ref: (0, 0))] + [ANY] * n_g,
            scratch_shapes=[pltpu.VMEM((2, hm, N), F32), pltpu.VMEM((hm, N), F32),
                            pltpu.SemaphoreType.DMA, pltpu.SemaphoreType.DMA] + (_Gather.scratch(n_g) if n_g else [])),
        out_shape=[_sds((hm, N), BF16)] + [_sds((N_DEV,) + g.shape, g.dtype) for g in gather],
        compiler_params=pltpu.CompilerParams(dimension_semantics=("arbitrary", "arbitrary"),
                                             collective_id=COLLECTIVE_IDS[name]),
    )(c_arr, a, b, *gather)
    return (outs[0], list(outs[1:])) if n_g else outs[0]


def _mixer_bwd(d2, u, hs, saved, pv, wa, wx, wp, w_out_b, tm, chip_sums=()):
    T = u.shape[0]
    n_t = T // tm
    n_x = len(chip_sums)

    def body(d2_ref, u_ref, uh_ref, hs_ref, hh_ref, saved_ref, pv_ref, wa_in, wx_in, wp_in, wo_ref, *rest):
        x_in, rest = rest[:n_x], rest[n_x:]
        du_ref, sg_ref = rest[:2]
        x_out, rest = rest[2:2 + n_x], rest[2 + n_x:]
        e_pool, e_h, a_s, b_s, dh_s, mu_s, f_x, f_p, mc, cx, cp = rest[:11]
        wa_ref, wx_ref, wp_ref, vacc_ref, dwa_ref, dwx_ref, dwp_ref = rest[11:18]
        exchange = _ChipExchange(x_in, x_out, *rest[18:]) if n_x else None
        s = pl.program_id(0)
        it = n_t - 1 - s

        @pl.when(s == 0)
        def _():
            if exchange:
                exchange.announce()
            mc[...] = jnp.zeros((8, LRU_W), F32)
            cx[...] = jnp.zeros((8, LRU_W), F32)
            cp[...] = jnp.zeros((HALO, POOL_W), F32)
            vacc_ref[...] = jnp.zeros((16, LRU_W), F32)
            dwa_ref[...] = jnp.zeros((2, 256, 256), F32)
            dwx_ref[...] = jnp.zeros((2, 256, 256), F32)
            dwp_ref[...] = jnp.zeros((2, 256, 256), F32)
            _fill_block_diag(wa_ref, wa_in)
            _fill_block_diag(wx_ref, wx_in)
            _fill_block_diag(wp_ref, wp_in)

        if exchange:
            @pl.when(s == 1)
            def _():
                exchange.start()

        first = it == 0
        e_pool[pl.ds(0, HALO), :] = jnp.where(first, 0.0, uh_ref[...])
        e_pool[pl.ds(HALO, tm), :] = u_ref[:, 2 * LRU_W:D_IN]
        e_h[pl.ds(0, 8), :] = jnp.where(first, 0.0, hh_ref[...])
        e_h[pl.ds(8, tm), :] = hs_ref[...]
        pv = pv_ref[...]
        saved = lambda q: saved_ref[:, LRU_W * q:LRU_W * (q + 1)]

        dyn = lax.dot_general(d2_ref[...].astype(BF16), wo_ref[...], NT, preferred_element_type=F32)
        dyn = jnp.concatenate([dyn[:, 128 * _y_pos(b):128 * (_y_pos(b) + 1)] for b in range(N_DEV)], axis=1)

        h = hs_ref[...]
        ug = u_ref[:, LRU_W:2 * LRU_W]
        gl, dgl = _gelu_parts(ug)
        y_lru = h * gl
        rstd_l = _rstd(y_lru)
        dy_lru, d_gain_l = _rms_bwd(dyn[:, 0:LRU_W], y_lru * rstd_l, rstd_l, pv[ROW_GL:ROW_GL + 1, :])
        dh = dy_lru * gl
        du_ref[:, LRU_W:2 * LRU_W] = (dy_lru * h * dgl).astype(BF16)
        a_s[...] = jnp.exp(saved(3))
        b_s[...] = a_s[...] * dh
        dh_s[...] = dh
        mu_s[pl.ds(tm, 8), :] = mc[...]
        mc[...] = _scan_tile(a_s, b_s, mu_s, mc[...], tm, reverse=True)
        xc, r, ig = saved(0), saved(1), saved(2)
        a, om, mult, rmult = _lru_decay(saved(3))
        lam_t = dh_s[...] + mu_s[pl.ds(1, tm), :]
        da = lam_t * e_h[pl.ds(7, tm), :]
        dmult = lam_t * (ig * xc)
        di = lam_t * (mult * xc)
        dxc = lam_t * (mult * ig)
        dla = da * a - jnp.where(om > 1e-12, dmult * ((a * a) * rmult), 0.0)
        dra = (dla * (-LRU_C * _softplus_neg_lambda(pv))) * (r * (1.0 - r))
        dia = di * (ig * (1.0 - ig))
        drab = dra.astype(BF16)
        diab = dia.astype(BF16)
        xcb = xc.astype(BF16)
        dxc = dxc + _bd_t(drab, wa_ref) + _bd_t(diab, wx_ref)
        dwa_ref[...] += _bd_grad(xcb, drab)
        dwx_ref[...] += _bd_grad(xcb, diab)
        sig_neg_lam = _sigmoid(-pv[ROW_LAM:ROW_LAM + 1, :])
        d_lam = jnp.sum(dla * r, axis=0, keepdims=True) * (LRU_C * sig_neg_lam)

        f_x[pl.ds(0, tm), :] = dxc
        f_x[pl.ds(tm, 8), :] = cx[...]
        du_lru = jnp.zeros((tm, LRU_W), F32)
        u_lru = u_ref[:, 0:LRU_W]
        d_cw = []
        for k in range(4):
            later = f_x[pl.ds(3 - k, tm), :]
            du_lru = du_lru + later * pv[ROW_CW + k:ROW_CW + k + 1, :]
            d_cw.append(jnp.sum(later * u_lru, axis=0, keepdims=True))
        du_ref[:, 0:LRU_W] = du_lru.astype(BF16)
        cx[...] = f_x[pl.ds(0, 8), :]

        pooled_b, zp, inv_cnts = _pool_pre(e_pool, pv, wp_ref, tm, it * tm)
        ps = pv[ROW_PS:ROW_PS + 1, :]
        y_pool = zp * ps
        rstd_p = _rstd(y_pool)
        dy_pool, d_gain_p = _rms_bwd(dyn[:, LRU_W:D_MODEL], y_pool * rstd_p, rstd_p, pv[ROW_GP:ROW_GP + 1, :])
        dz = dy_pool * ps
        dzb = dz.astype(BF16)
        dwp_ref[...] += _bd_grad(pooled_b, dzb)
        dpooled = _bd_t(dzb, wp_ref)
        for g, w in enumerate(POOL_WINDOWS):
            f_p[pl.ds(0, tm), pl.ds(128 * g, 128)] = _over_count(dpooled[:, 128 * g:128 * (g + 1)], w, inv_cnts[g])
        f_p[pl.ds(tm, HALO), :] = cp[...]
        for g, w in enumerate(POOL_WINDOWS):
            acc = _window_sum(f_p[:, pl.ds(128 * g, 128)], w, back=False)[0:tm, :]
            du_ref[:, 2 * LRU_W + 128 * g:2 * LRU_W + 128 * (g + 1)] = (
                acc - dpooled[:, 128 * g:128 * (g + 1)]).astype(BF16)
        cp[...] = f_p[pl.ds(0, HALO), :]

        rows = d_cw + [
            jnp.sum(dxc, axis=0, keepdims=True),
            jnp.sum(dra, axis=0, keepdims=True),
            jnp.sum(dia, axis=0, keepdims=True),
            d_lam,
            jnp.sum(dz, axis=0, keepdims=True),
            jnp.sum(dy_pool * zp, axis=0, keepdims=True),
            d_gain_l, d_gain_p,
            jnp.zeros((4, LRU_W), F32),
        ]
        vacc_ref[...] += jnp.concatenate(rows, axis=0)

        @pl.when(s == n_t - 1)
        def _():
            sg_ref[SG_VEC:SG_VEC + 16, :] = vacc_ref[:, 0:256]
            sg_ref[SG_VEC + 16:SG_VEC + 32, :] = vacc_ref[:, 256:512]
            for half in range(2):
                sg_ref[SG_WA + 64 * half:SG_WA + 64 * (half + 1), :] = _diag_pack(dwa_ref[half], 64)
                sg_ref[SG_WX + 64 * half:SG_WX + 64 * (half + 1), :] = _diag_pack(dwx_ref[half], 64)
                sg_ref[SG_WP + 128 * half:SG_WP + 128 * (half + 1), :] = _diag_pack(dwp_ref[half], 128)
            if exchange:
                exchange.finish()

    rev = lambda w: pl.BlockSpec((tm, w), lambda s: (n_t - 1 - s, 0))
    full = lambda shape: pl.BlockSpec(shape, lambda s: (0,) * len(shape))
    outs = pl.pallas_call(
        body, name="mixer_bwd", grid=(n_t,),
        in_specs=[rev(D_MODEL), rev(D_IN),
                  pl.BlockSpec((HALO, POOL_W), lambda s: (jnp.maximum((n_t - 1 - s) * (tm // HALO) - 1, 0), 2)),
                  rev(LRU_W),
                  pl.BlockSpec((8, LRU_W), lambda s: (jnp.maximum((n_t - 1 - s) * (tm // 8) - 1, 0), 0)),
                  rev(4 * LRU_W), full((16, LRU_W)), full((8, 64, 64)), full((8, 64, 64)), full((4, 128, 128)),
                  full((D_MODEL, D_MODEL))] + [ANY] * n_x,
        out_specs=[rev(D_IN), full((SG_ROWS, 256))] + [ANY] * n_x,
        out_shape=[_sds((T, D_IN), BF16), _sds((SG_ROWS, 256), F32)] + [_sds(a.shape, a.dtype) for a in chip_sums],
        scratch_shapes=[pltpu.VMEM((HALO + tm, POOL_W), F32),
                        pltpu.VMEM((8 + tm, LRU_W), F32)] + [pltpu.VMEM((tm, LRU_W), F32)] * 3 + [
                        pltpu.VMEM((tm + 8, LRU_W), F32), pltpu.VMEM((tm + 8, LRU_W), F32),
                        pltpu.VMEM((tm + HALO, POOL_W), F32), pltpu.VMEM((8, LRU_W), F32),
                        pltpu.VMEM((8, LRU_W), F32), pltpu.VMEM((HALO, POOL_W), F32)]
        + [pltpu.VMEM((2, 256, 256), BF16)] * 3 + [pltpu.VMEM((16, LRU_W), F32)] + [pltpu.VMEM((2, 256, 256), F32)] * 3
        + (_ChipExchange.scratch(n_x) if n_x else []),
        compiler_params=pltpu.CompilerParams(dimension_semantics=("arbitrary",),
                                             collective_id=COLLECTIVE_IDS["mixer_bwd"] if n_x else None),
    )(d2, u, u, hs, hs, saved, pv, wa, wx, wp, w_out_b, *chip_sums)
    return outs[0], outs[1], list(outs[2:])


def _mix_in_bwd(du, x, d2, w_in_t, g_mix, tm):
    T = x.shape[0]

    def body(du_ref, x_ref, d2_ref, w_ref, g_ref, dx_ref, dg_ref):
        @pl.when(pl.program_id(0) == 0)
        def _():
            dg_ref[...] = jnp.zeros((1, D_MODEL), F32)

        dh = jnp.dot(du_ref[...], w_ref[...], preferred_element_type=F32)
        xv = x_ref[...]
        rstd = _rstd(xv)
        dx, dgain = _rms_bwd(dh, xv * rstd, rstd, g_ref[...])
        dx_ref[...] = d2_ref[...] + dx
        dg_ref[...] += dgain

    row = lambda w: pl.BlockSpec((tm, w), lambda i: (i, 0))
    const = lambda shape: pl.BlockSpec(shape, lambda i: (0,) * len(shape))
    return pl.pallas_call(
        body, name="mix_in_bwd", grid=(T // tm,),
        in_specs=[row(D_IN), row(D_MODEL), row(D_MODEL), const((D_IN, D_MODEL)), const((1, D_MODEL))],
        out_specs=[row(D_MODEL), const((1, D_MODEL))],
        out_shape=[_sds((T, D_MODEL), F32), _sds((1, D_MODEL), F32)],
        compiler_params=pltpu.CompilerParams(dimension_semantics=("arbitrary",)),
    )(du, x, d2, w_in_t, g_mix)


def _adamw(w, g, m, v):
    m = ADAM_B1 * m + (1.0 - ADAM_B1) * g
    v = ADAM_B2 * v + (1.0 - ADAM_B2) * (g * g)
    m_hat = m / (1.0 - ADAM_B1 ** ADAM_STEP)
    v_hat = v / (1.0 - ADAM_B2 ** ADAM_STEP)
    delta = -ADAM_LR * (m_hat / (jnp.sqrt(v_hat) + ADAM_EPS) + ADAM_WD * w)
    return delta, m, v


def _adam_shards(ws, ms, vs, parts):
    n = len(ws)
    n_blk = [w.shape[0] // ADAM_ROWS for w in ws]

    def body(*refs):
        w_refs, m_refs, v_refs, p_refs, outs = (refs[:n], refs[n:2 * n], refs[2 * n:3 * n], refs[3 * n:4 * n],
                                                refs[4 * n:])
        i = pl.program_id(0)
        for a in range(n):
            @pl.when(i < n_blk[a])
            def _(a=a):
                g = p_refs[a][0].astype(F32)
                for j in range(1, 4):
                    g = g + p_refs[a][j].astype(F32)
                delta, new_m, new_v = _adamw(w_refs[a][...], g, m_refs[a][...], v_refs[a][...])
                for kind, val in enumerate((g, delta, new_m, new_v)):
                    outs[4 * a + kind][...] = val

    blk = lambda a: pl.BlockSpec((ADAM_ROWS, D_MODEL), lambda i: (jnp.minimum(i, n_blk[a] - 1), 0))
    part_blk = lambda a: pl.BlockSpec((4, ADAM_ROWS, D_MODEL), lambda i: (0, jnp.minimum(i, n_blk[a] - 1), 0))
    res = pl.pallas_call(
        body, name="adam_shards", grid=(max(n_blk),),
        in_specs=[blk(a) for a in range(n)] * 3 + [part_blk(a) for a in range(n)],
        out_specs=[blk(a) for a in range(n) for _ in range(4)],
        out_shape=[_sds(w.shape, F32) for w in ws for _ in range(4)],
        compiler_params=pltpu.CompilerParams(dimension_semantics=("arbitrary",)),
    )(*ws, *ms, *vs, *parts)
    return [tuple(res[4 * a:4 * a + 4]) for a in range(n)]


SMALL_PARAMS = [("norm_mix_g", (1, D_MODEL)), ("conv_w", (1, 4, 64)), ("conv_b", (1, LRU_W)),
                ("gate_a_w", (1, 8, 64, 64)), ("gate_a_b", (1, LRU_W)), ("gate_x_w", (1, 8, 64, 64)),
                ("gate_x_b", (1, LRU_W)), ("lru_lambda", (1, LRU_W)), ("pool_w", (1, 4, 128, 128)),
                ("pool_b", (1, POOL_W)), ("pool_scale", (1, POOL_W)), ("norm_lru_g", (1, LRU_W)),
                ("norm_pool_g", (1, POOL_W)), ("norm_ffn_g", (1, D_MODEL)), ("final_norm_g", (1, D_MODEL))]
VEC_ROW = dict(conv_b=ROW_CB, gate_a_b=ROW_BA, gate_x_b=ROW_BX, lru_lambda=ROW_LAM, pool_b=ROW_PB, pool_scale=ROW_PS,
               norm_lru_g=ROW_GL, norm_pool_g=ROW_GP)
WHOLE = (Ellipsis,)


def _unpack_mixer_grads(sg, dev):
    vec = jnp.concatenate([sg[SG_VEC:SG_VEC + 16], sg[SG_VEC + 16:SG_VEC + 32]], axis=1)
    out = {nm: [(WHOLE, vec[r:r + 1])] for nm, r in VEC_ROW.items()}
    own = jnp.zeros((4, 64), F32)
    for d in range(N_DEV):
        own = jnp.where(dev == d, vec[ROW_CW:ROW_CW + 4, 64 * d:64 * (d + 1)], own)
    out["conv_w"] = [((0,), own)]
    for nm, row0 in (("gate_a_w", SG_WA), ("gate_x_w", SG_WX)):
        out[nm] = [((0, b), sg[row0 + 64 * (b // 4):row0 + 64 * (b // 4 + 1), 64 * (b % 4):64 * (b % 4 + 1)])
                   for b in range(8)]
    out["pool_w"] = [((0, b), sg[SG_WP + 128 * (b // 2):SG_WP + 128 * (b // 2 + 1), 128 * (b % 2):128 * (b % 2 + 1)])
                     for b in range(4)]
    return out


def _adam_small(parts, w, m, v):
    names = [nm for nm, _ in SMALL_PARAMS]
    n = len(names)

    def body(sg_ref, gm_ref, gf_ref, gn_ref, ls_ref, *rest):
        w_refs, m_refs, v_refs, outs = rest[:n], rest[n:2 * n], rest[2 * n:3 * n], rest[3 * n:]
        dev = 4 * lax.axis_index("x") + 2 * lax.axis_index("y") + lax.axis_index("c")

        def total(ref):
            acc = ref[0]
            for d in range(1, N_DEV):
                acc = acc + ref[d]
            return acc

        pieces = _unpack_mixer_grads(total(sg_ref), dev)
        pieces["norm_mix_g"] = [(WHOLE, total(gm_ref))]
        pieces["norm_ffn_g"] = [(WHOLE, total(gf_ref))]
        pieces["final_norm_g"] = [(WHOLE, total(gn_ref))]
        for i, nm in enumerate(names):
            for idx, g in pieces[nm]:
                delta, new_m, new_v = _adamw(w_refs[i][idx], g, m_refs[i][idx], v_refs[i][idx])
                for kind, val in enumerate((g, delta, new_m, new_v)):
                    outs[4 * i + kind][idx] = val
        outs[4 * n][...] = total(ls_ref)

    shapes = [_sds(shape, F32) for _, shape in SMALL_PARAMS for _ in range(4)] + [_sds((8, 128), F32)]
    res = pl.pallas_call(body, name="adam_small", out_shape=shapes)(
        *parts, *[w[nm] for nm in names], *[m[nm] for nm in names], *[v[nm] for nm in names])
    return {nm: tuple(res[4 * i:4 * i + 4]) for i, nm in enumerate(names)}, res[4 * n][0, 0]


def _vec_rows(conv_w_full, conv_b, ba, bx, lam, pb, ps, gl, gp):
    return jnp.concatenate([conv_w_full, conv_b, ba, bx, lam, pb, ps, gl, gp, jnp.zeros((4, LRU_W), F32)], axis=0)


WEIGHT_ORDER = ['norm_mix_g', 'w_in', 'conv_w', 'conv_b', 'gate_a_w', 'gate_a_b', 'gate_x_w', 'gate_x_b', 'lru_lambda',
                'pool_w', 'pool_b', 'pool_scale', 'norm_lru_g', 'norm_pool_g', 'w_out', 'norm_ffn_g', 'ffn_w1', 'ffn_w3',
                'ffn_w2', 'final_norm_g']


def kernel(x, norm_mix_g, w_in, conv_w, conv_b, gate_a_w, gate_a_b, gate_x_w, gate_x_b, lru_lambda, pool_w, pool_b, pool_scale, norm_lru_g, norm_pool_g, w_out, norm_ffn_g, ffn_w1, ffn_w3, ffn_w2, final_norm_g, loss_target, m_norm_mix_g, m_w_in, m_conv_w, m_conv_b, m_gate_a_w, m_gate_a_b, m_gate_x_w, m_gate_x_b, m_lru_lambda, m_pool_w, m_pool_b, m_pool_scale, m_norm_lru_g, m_norm_pool_g, m_w_out, m_norm_ffn_g, m_ffn_w1, m_ffn_w3, m_ffn_w2, m_final_norm_g, v_norm_mix_g, v_w_in, v_conv_w, v_conv_b, v_gate_a_w, v_gate_a_b, v_gate_x_w, v_gate_x_b, v_lru_lambda, v_pool_w, v_pool_b, v_pool_scale, v_norm_lru_g, v_norm_pool_g, v_w_out, v_norm_ffn_g, v_ffn_w1, v_ffn_w3, v_ffn_w2, v_final_norm_g):
    ac = lax.axis_index("c")
    tm, tmx, tm_in, tk = 512, 512, 1024, 2048
    xs, tgt = x[0], loss_target[0]
    g_fin = final_norm_g.reshape(1, D_MODEL)
    c_arr = jnp.reshape(ac, (1,)).astype(jnp.int32)

    tr = lambda w: jnp.swapaxes(w[0], 0, 1)
    own = lambda w: w[0]
    bf = lambda a: a.astype(BF16)

    h1, (g_in, g_conv) = _norm_in(xs, norm_mix_g, [bf(tr(w_in)), conv_w[0]], tm_in)
    w_in_t = g_in.reshape(D_IN, D_MODEL)
    u, (g_out,) = _mix_in(h1, w_in_t, tm_in, shards=[bf(own(w_out))])
    conv_w_full = g_conv.transpose(1, 0, 2).reshape(4, LRU_W)
    pv = _vec_rows(conv_w_full, conv_b, gate_a_b, gate_x_b, lru_lambda, pool_b, pool_scale, norm_lru_g, norm_pool_g)
    wa, wx, wp = gate_a_w[0], gate_x_w[0], pool_w[0]
    w_out_b = g_out.reshape(D_MODEL, D_MODEL)
    y, hs, hres, h2, saved, (g_w1, g_w3, g_w2) = _mixer_fwd(
        u, xs, pv, wa, wx, wp, w_out_b, norm_ffn_g, tmx, shards=[bf(tr(ffn_w1)), bf(tr(ffn_w3)), bf(own(ffn_w2))])
    w1_t, w3_t, w2_b = g_w1.reshape(D_FF, D_MODEL), g_w3.reshape(D_FF, D_MODEL), g_w2.reshape(D_FF, D_MODEL)
    g, v, ff, d3, loss_acc, d_gfin = _ffn_fwd(hres, h2, w1_t, w3_t, w2_b, g_fin, tgt, tm)

    dg, dv, d2, d_gffn = _ffn_bwd(d3, g, v, w1_t, w3_t, w2_b, hres, norm_ffn_g, tm)
    chips = lambda a: a.reshape(4, a.shape[0] // 4, a.shape[1])
    early = [(y, d2, "grad_w_out"), (dg, h2, "grad_w1"), (dv, h2, "grad_w3"), (ff, d3, "grad_w2")]
    early_sums = [chips(_at_b_pair(a, b, c_arr, name, tk)) for a, b, name in early]
    du, d_mixer, early_parts = _mixer_bwd(d2, u, hs, saved, pv, wa, wx, wp, w_out_b, tmx, chip_sums=early_sums)
    grad_x, d_gmix = _mix_in_bwd(du, xs, d2, w_in_t, norm_mix_g, tm_in)
    d_win, small_parts = _at_b_pair(du, h1, c_arr, "grad_w_in", tk,
                                    gather=[d_mixer, d_gmix, d_gffn, d_gfin, loss_acc])
    parts = [_half_exchange(chips(d_win), "grads_to_chips_w_in")] + list(early_parts)

    res = {}
    shard_w = dict(w_in=(w_in, m_w_in, v_w_in, tr), w_out=(w_out, m_w_out, v_w_out, own),
                   ffn_w1=(ffn_w1, m_ffn_w1, v_ffn_w1, tr), ffn_w3=(ffn_w3, m_ffn_w3, v_ffn_w3, tr),
                   ffn_w2=(ffn_w2, m_ffn_w2, v_ffn_w2, own))
    shard_res = _adam_shards([view(w) for w, _, _, view in shard_w.values()],
                             [view(m) for _, m, _, view in shard_w.values()],
                             [view(v) for _, _, v, view in shard_w.values()], parts)
    for (nm, (_, _, _, view)), outs in zip(shard_w.items(), shard_res):
        res[nm] = [(jnp.swapaxes(o, 0, 1) if view is tr else o)[None] for o in outs]

    row = lambda a: a.reshape(1, D_MODEL)
    small = lambda gm, cw, cb, wa_, ba, wx_, bx, lam, pw, pb, ps, gl, gp, gf, gn: dict(
        norm_mix_g=gm, conv_w=cw, conv_b=cb, gate_a_w=wa_, gate_a_b=ba, gate_x_w=wx_, gate_x_b=bx, lru_lambda=lam,
        pool_w=pw, pool_b=pb, pool_scale=ps, norm_lru_g=gl, norm_pool_g=gp, norm_ffn_g=gf, final_norm_g=row(gn))
    small_res, loss = _adam_small(
        small_parts,
        small(norm_mix_g, conv_w, conv_b, gate_a_w, gate_a_b, gate_x_w, gate_x_b, lru_lambda, pool_w, pool_b,
              pool_scale, norm_lru_g, norm_pool_g, norm_ffn_g, final_norm_g),
        small(m_norm_mix_g, m_conv_w, m_conv_b, m_gate_a_w, m_gate_a_b, m_gate_x_w, m_gate_x_b, m_lru_lambda, m_pool_w,
              m_pool_b, m_pool_scale, m_norm_lru_g, m_norm_pool_g, m_norm_ffn_g, m_final_norm_g),
        small(v_norm_mix_g, v_conv_w, v_conv_b, v_gate_a_w, v_gate_a_b, v_gate_x_w, v_gate_x_b, v_lru_lambda, v_pool_w,
              v_pool_b, v_pool_scale, v_norm_lru_g, v_norm_pool_g, v_norm_ffn_g, v_final_norm_g))
    for nm, outs in small_res.items():
        res[nm] = [o.reshape(D_MODEL) for o in outs] if nm == "final_norm_g" else list(outs)

    out = [loss, grad_x[None]]
    for kind in range(4):
        out += [res[nm][kind] for nm in WEIGHT_ORDER]
    return tuple(out)
```

```python
import functools

import jax
import jax.numpy as jnp
from jax import lax
from jax.experimental import pallas as pl
from jax.experimental.pallas import tpu as pltpu

F32 = jnp.float32
BF16 = jnp.bfloat16

D_MODEL = 1024
LRU_W = 512
POOL_W = 512
D_IN = 1536
D_FF = 2816
POOL_WINDOWS = (2, 4, 8, 16)
EPS = 1e-6
LRU_C = 8.0
N_DEV = 8
HALO = 16
SCAN_UNROLL = 8
ADAM_ROWS = 32
FF_CHUNKS = ((0, 1536), (1536, 2816))
RELAY_SPLIT_ROWS = 32
COLLECTIVE_IDS = {name: j for j, name in enumerate(
    ("grad_w_out", "grad_w1", "grad_w3", "grad_w2", "norm_in", "mix_in", "mixer_fwd", "mixer_bwd", "grad_w_in",
     "grads_to_chips_w_in"))}

ADAM_LR = 0.001
ADAM_B1 = 0.9
ADAM_B2 = 0.999
ADAM_EPS = 1e-08
ADAM_WD = 0.01
ADAM_STEP = 10

ROW_CW, ROW_CB, ROW_BA, ROW_BX, ROW_LAM, ROW_PB, ROW_PS, ROW_GL, ROW_GP = 0, 4, 5, 6, 7, 8, 9, 10, 11
SG_VEC, SG_WA, SG_WX, SG_WP, SG_ROWS = 0, 32, 160, 288, 544

NT = (((1,), (1,)), ((), ()))
TN = (((0,), (0,)), ((), ()))


def _sds(shape, dtype):
    return jax.ShapeDtypeStruct(shape, dtype)


def _sigmoid(x):
    return 0.5 * jnp.tanh(0.5 * x) + 0.5


def _gelu_parts(x):
    c = 0.7978845608028654
    inner = c * (x + 0.044715 * (x * x * x))
    th = jnp.tanh(inner)
    g = 0.5 * x * (1.0 + th)
    dg = 0.5 * (1.0 + th) + 0.5 * x * (1.0 - th * th) * (c * (1.0 + 3.0 * 0.044715 * (x * x)))
    return g, dg


def _window_sum(ext, w, back):
    n = ext.shape[0]
    s, k = ext, 1
    while k < w:
        s = s + pltpu.roll(s, k if back else n - k, 0)
        k *= 2
    return s


def _rstd(x):
    return lax.rsqrt(jnp.mean(x * x, axis=-1, keepdims=True) + EPS)


def _rms_bwd(dy, xhat, rstd, gain):
    dxh = dy * gain
    dx = rstd * (dxh - xhat * jnp.mean(dxh * xhat, axis=-1, keepdims=True))
    return dx, jnp.sum(dy * xhat, axis=0, keepdims=True)


def _bd(xb, w_ref):
    return jnp.concatenate(
        [jnp.dot(xb[:, :256], w_ref[0], preferred_element_type=F32),
         jnp.dot(xb[:, 256:], w_ref[1], preferred_element_type=F32)], axis=1)


def _bd_t(xb, w_ref):
    return jnp.concatenate(
        [lax.dot_general(xb[:, :256], w_ref[0], NT, preferred_element_type=F32),
         lax.dot_general(xb[:, 256:], w_ref[1], NT, preferred_element_type=F32)], axis=1)


def _bd_grad(xb, db):
    return jnp.stack(
        [lax.dot_general(xb[:, :256], db[:, :256], TN, preferred_element_type=F32),
         lax.dot_general(xb[:, 256:], db[:, 256:], TN, preferred_element_type=F32)], axis=0)


def _fill_block_diag(dst, src_ref):
    n, k, _ = src_ref.shape
    dst[...] = jnp.zeros(dst.shape, BF16)
    for b in range(n):
        p, q = divmod(b, 256 // k)
        dst[p, q * k:(q + 1) * k, q * k:(q + 1) * k] = src_ref[b].astype(BF16)


def _diag_pack(w, k):
    lane = lax.broadcasted_iota(jnp.int32, (k, 256), 1)
    out = w[0:k]
    for q in range(1, 256 // k):
        out = jnp.where(lane >= q * k, w[q * k:(q + 1) * k], out)
    return out


def _y_pos(b):
    return 4 * (b % 2) + b // 2


def _softplus_neg_lambda(pv):
    z = -pv[ROW_LAM:ROW_LAM + 1, :]
    return jnp.maximum(z, 0.0) + jnp.log(1.0 + jnp.exp(-jnp.abs(z)))


def _lru_gates(e_lru, pv, wa_ref, wx_ref, tm):
    xc = pv[ROW_CB:ROW_CB + 1, :]
    for k in range(4):
        xc = xc + e_lru[pl.ds(HALO - 3 + k, tm), :] * pv[ROW_CW + k:ROW_CW + k + 1, :]
    xcb = xc.astype(BF16)
    r = _sigmoid(_bd(xcb, wa_ref) + pv[ROW_BA:ROW_BA + 1, :])
    ig = _sigmoid(_bd(xcb, wx_ref) + pv[ROW_BX:ROW_BX + 1, :])
    return xc, r, ig, (-LRU_C * r) * _softplus_neg_lambda(pv)


def _lru_decay(la):
    a = jnp.exp(la)
    om = -jnp.tanh(la) * (1.0 + a * a)
    omc = jnp.maximum(om, 1e-12)
    rmult = lax.rsqrt(omc)
    return a, om, omc * rmult, rmult


def _over_count(v, w, inv_head):
    return jnp.concatenate([v[0:HALO] * inv_head, v[HALO:] * (1.0 / w)], axis=0)


def _pool_pre(e_pool, pv, wp_ref, tm, t0):
    t_head = t0 + lax.broadcasted_iota(jnp.int32, (HALO, 1), 0)
    parts, inv_heads = [], []
    for g, w in enumerate(POOL_WINDOWS):
        ext = e_pool[:, pl.ds(128 * g, 128)]
        s = _window_sum(ext, w, back=True)[HALO:, :]
        inv_head = 1.0 / jnp.minimum(t_head + 1, w).astype(F32)
        inv_heads.append(inv_head)
        parts.append(_over_count(s, w, inv_head) - ext[HALO:, :])
    pooled = jnp.concatenate(parts, axis=1)
    pooled_b = pooled.astype(BF16)
    zp = _bd(pooled_b, wp_ref) + pv[ROW_PB:ROW_PB + 1, :]
    return pooled_b, zp, inv_heads


def _scan_tile(a_ref, b_ref, out_ref, carry, tm, reverse):
    row = lax.broadcasted_iota(jnp.int32, (8, LRU_W), 0)
    nblk = tm // 8

    def local_scan(blk):
        r0 = pl.multiple_of(blk * 8, 8)
        av = a_ref[pl.ds(r0, 8), :]
        bv = b_ref[pl.ds(r0, 8), :]
        for d in (1, 2, 4):
            sh = (8 - d) if reverse else d
            a_s = pltpu.roll(av, sh, 0)
            b_s = pltpu.roll(bv, sh, 0)
            m = (row < 8 - d) if reverse else (row >= d)
            bv = jnp.where(m, av * b_s + bv, bv)
            av = jnp.where(m, av * a_s, av)
        return r0, av, bv

    def step(i, hin):
        local = [local_scan((nblk - 1 - (i * SCAN_UNROLL + j)) if reverse else (i * SCAN_UNROLL + j))
                 for j in range(SCAN_UNROLL)]
        for r0, av, bv in local:
            hv = av * hin + bv
            out_ref[pl.ds(r0, 8), :] = hv
            hin = jnp.broadcast_to(hv[0:1, :] if reverse else hv[7:8, :], (8, LRU_W))
        return hin

    return lax.fori_loop(0, nblk // SCAN_UNROLL, step, carry)


MESH = pl.DeviceIdType.MESH
ANY = pl.BlockSpec(memory_space=pl.ANY)


def _place():
    x, y, c = lax.axis_index("x"), lax.axis_index("y"), lax.axis_index("c")
    chips = [(1 - x, y), (x, 1 - y), (1 - x, 1 - y)]
    return x, y, c, chips


def _meet(peers):
    _announce(peers)
    _await(peers)


def _announce(peers):
    for peer in peers:
        pl.semaphore_signal(pltpu.get_barrier_semaphore(), inc=1, device_id=peer, device_id_type=MESH)


def _await(peers):
    pl.semaphore_wait(pltpu.get_barrier_semaphore(), len(peers))


class _Gather:
    def __init__(self, ins, outs, send_sems, recv_sems, local_sems, core_major=False):
        self.ins, self.outs, self.n = ins, outs, len(ins)
        self.send_sems, self.recv_sems, self.local_sems = send_sems, recv_sems, local_sems
        self.core_major = core_major

    @staticmethod
    def scratch(n):
        return [pltpu.SemaphoreType.DMA((8, n)), pltpu.SemaphoreType.DMA((8, n)), pltpu.SemaphoreType.DMA((n,))]

    def _slot(self, a, px, py, pc):
        return self.outs[a].at[4 * pc + 2 * px + py if self.core_major else 4 * px + 2 * py + pc]

    def _half(self, a, h):
        rows = self.ins[a].shape[0]
        if rows % RELAY_SPLIT_ROWS:
            return None if h else (0, rows)
        return (h * (rows // 2), rows // 2)

    def _copy(self, a, k, block, to, src=None, rows=None):
        dst = self._slot(a, *block)
        src = dst if src is None else src
        if rows is not None:
            src, dst = src.at[pl.ds(*rows)], dst.at[pl.ds(*rows)]
        return pltpu.make_async_remote_copy(
            src_ref=src, dst_ref=dst, send_sem=self.send_sems.at[k, a], recv_sem=self.recv_sems.at[k, a],
            device_id=to, device_id_type=MESH)

    def _mine(self, a):
        x, y, c, _ = _place()
        return pltpu.make_async_copy(self.ins[a], self._slot(a, x, y, c), self.local_sems.at[a])

    def _first(self, a):
        x, y, c, chips = _place()
        me = (x, y, c)
        return ([self._copy(a, 0, me, (x, y, 1 - c), src=self.ins[a])]
                + [self._copy(a, 1 + j, me, (*chip, c), src=self.ins[a]) for j, chip in enumerate(chips[:2])])

    def _passed_on(self, a, h):
        x, y, c, chips = _place()
        block = (*chips[h], c)
        out = [self._copy(a, 4 + h, block, (x, y, 1 - c))]
        if self._half(a, h) is not None:
            out.append(self._copy(a, (3, 7)[h], block, (*chips[1 - h], c), rows=self._half(a, h)))
        return out

    def _peers(self):
        x, y, c, chips = _place()
        return [(x, y, 1 - c), (*chips[0], c), (*chips[1], c)]

    def announce(self):
        _announce(self._peers())

    def start(self, announced=False):
        (_await if announced else _meet)(self._peers())
        for a in range(self.n):
            self._mine(a).start()
        for a in range(self.n):
            for cp in self._first(a):
                cp.start()

    def relay(self):
        x, y, c, chips = _place()
        for h in range(2):
            for a in range(self.n):
                self._copy(a, 1 + h, (*chips[h], c), (x, y, c)).wait_recv()
                for cp in self._passed_on(a, h):
                    cp.start()

    def finish(self):
        x, y, c, chips = _place()
        me, sibling = (x, y, c), (x, y, 1 - c)
        passed = []
        for a in range(self.n):
            for h in range(2):
                if self._half(a, h) is not None:
                    self._copy(a, (3, 7)[h], (*chips[2], c), me, rows=self._half(a, h)).wait_recv()
            fwd = self._copy(a, 6, (*chips[2], c), sibling)
            fwd.start()
            passed.append(fwd)
        for a in range(self.n):
            self._copy(a, 0, (x, y, 1 - c), me).wait_recv()
            for j, chip in enumerate(chips):
                self._copy(a, 4 + j, (*chip, 1 - c), me).wait_recv()
        for a in range(self.n):
            for cp in self._first(a) + self._passed_on(a, 0) + self._passed_on(a, 1):
                cp.wait_send()
        for cp in passed:
            cp.wait_send()
        for a in range(self.n):
            self._mine(a).wait()


def _half_exchange(arr, name):
    def body(in_ref, out_ref, send_sems, recv_sems, local_sem):
        x, y, c, _ = _place()
        my_chip = 2 * x + y
        _meet([((x + dx) % 2, (y + dy) % 2, (c + dc) % 2)
               for dx in range(2) for dy in range(2) for dc in range(2) if dx + dy + dc])

        def send(j, wait):
            to_me = (c == x) & (y == j // 2) & (c == j % 2)

            @pl.when(to_me)
            def _():
                local = pltpu.make_async_copy(in_ref.at[j], out_ref.at[my_chip], local_sem)
                local.wait() if wait else local.start()

            @pl.when(jnp.logical_not(to_me))
            def _():
                remote = pltpu.make_async_remote_copy(
                    src_ref=in_ref.at[j], dst_ref=out_ref.at[my_chip], send_sem=send_sems.at[j],
                    recv_sem=recv_sems.at[my_chip], device_id=(c, j // 2, j % 2), device_id_type=MESH)
                remote.wait_send() if wait else remote.start()

        for j in range(4):
            send(j, wait=False)
        for j in range(4):
            send(j, wait=True)
        for k in range(4):
            from_me = (k // 2 == x) & (k % 2 == y) & (c == x)

            @pl.when(jnp.logical_not(from_me))
            def _():
                pltpu.make_async_remote_copy(
                    src_ref=in_ref.at[0], dst_ref=out_ref.at[k], send_sem=send_sems.at[0], recv_sem=recv_sems.at[k],
                    device_id=(k // 2, k % 2, x), device_id_type=MESH).wait_recv()

    return pl.pallas_call(
        body, name=name, out_shape=_sds(arr.shape, arr.dtype), in_specs=[ANY], out_specs=ANY,
        scratch_shapes=[pltpu.SemaphoreType.DMA((4,)), pltpu.SemaphoreType.DMA((4,)), pltpu.SemaphoreType.DMA],
        compiler_params=pltpu.CompilerParams(collective_id=COLLECTIVE_IDS[name]),
    )(arr)


class _ChipExchange:
    def __init__(self, ins, outs, send_sems, recv_sems, local_sems):
        self.ins, self.outs, self.n = ins, outs, len(ins)
        self.send_sems, self.recv_sems, self.local_sems = send_sems, recv_sems, local_sems

    @staticmethod
    def scratch(n):
        return [pltpu.SemaphoreType.DMA((3, n)), pltpu.SemaphoreType.DMA((3, n)), pltpu.SemaphoreType.DMA((n,))]

    def _local(self, a):
        x, y, _, _ = _place()
        me = 2 * x + y
        return pltpu.make_async_copy(self.ins[a].at[me], self.outs[a].at[me], self.local_sems.at[a])

    def _copies(self, a):
        x, y, c, chips = _place()
        me = 2 * x + y
        return [(pltpu.make_async_remote_copy(
                     src_ref=self.ins[a].at[2 * px + py], dst_ref=self.outs[a].at[me],
                     send_sem=self.send_sems.at[k, a], recv_sem=self.recv_sems.at[k, a],
                     device_id=(px, py, c), device_id_type=MESH),
                 pltpu.make_async_remote_copy(
                     src_ref=self.ins[a].at[me], dst_ref=self.outs[a].at[2 * px + py],
                     send_sem=self.send_sems.at[k, a], recv_sem=self.recv_sems.at[k, a],
                     device_id=(px, py, c), device_id_type=MESH))
                for k, (px, py) in enumerate(chips)]

    def _peers(self):
        _, _, c, chips = _place()
        return [(*chip, c) for chip in chips]

    def announce(self):
        _announce(self._peers())

    def start(self):
        _await(self._peers())
        for a in range(self.n):
            self._local(a).start()
        for a in range(self.n):
            for send, _ in self._copies(a):
                send.start()

    def finish(self):
        for a in range(self.n):
            for send, recv in self._copies(a):
                send.wait_send()
                recv.wait_recv()
        for a in range(self.n):
            self._local(a).wait()


def _gathering(body, n_steps, n_s, core_major, start_at=0):
    def wrapped(*refs, n_in, n_out):
        ins, sh_in = refs[:n_in], refs[n_in:n_in + n_s]
        outs, sh_out = refs[n_in + n_s:n_in + n_s + n_out], refs[n_in + n_s + n_out:n_in + 2 * n_s + n_out]
        rest = refs[n_in + 2 * n_s + n_out:]
        gather = _Gather(sh_in, sh_out, *rest[len(rest) - 3:], core_major=core_major)
        i = pl.program_id(0)

        if start_at:
            @pl.when(i == 0)
            def _():
                gather.announce()

        @pl.when(i == start_at)
        def _():
            gather.start(announced=start_at > 0)

        @pl.when(i == n_steps // 2)
        def _():
            gather.relay()

        body(*ins, *outs, *rest[:len(rest) - 3])

        @pl.when(i == n_steps - 1)
        def _():
            gather.finish()

    return wrapped


def _norm_in(x, g_mix, shards, tm):
    T = x.shape[0]
    n_t = T // tm
    n_s = len(shards)

    def norm(x_ref, g_ref, h_ref):
        xv = x_ref[...]
        h_ref[...] = (xv * _rstd(xv) * g_ref[...]).astype(BF16)

    outs = pl.pallas_call(
        functools.partial(_gathering(norm, n_t, n_s, core_major=False), n_in=2, n_out=1), name="norm_in", grid=(n_t,),
        in_specs=[pl.BlockSpec((tm, D_MODEL), lambda i: (i, 0)), pl.BlockSpec((1, D_MODEL), lambda i: (0, 0))]
        + [ANY] * n_s,
        out_specs=[pl.BlockSpec((tm, D_MODEL), lambda i: (i, 0))] + [ANY] * n_s,
        out_shape=[_sds((T, D_MODEL), BF16)] + [_sds((N_DEV,) + a.shape, a.dtype) for a in shards],
        scratch_shapes=_Gather.scratch(n_s),
        compiler_params=pltpu.CompilerParams(dimension_semantics=("arbitrary",), collective_id=COLLECTIVE_IDS["norm_in"]),
    )(x, g_mix, *shards)
    return outs[0], list(outs[1:])


def _mix_in(h1, w_in_t, tm, shards):
    T = h1.shape[0]
    n_t = T // tm
    n_s = len(shards)

    def project(h_ref, w_ref, u_ref):
        u_ref[...] = lax.dot_general(h_ref[...], w_ref[...], NT, preferred_element_type=F32)

    outs = pl.pallas_call(
        functools.partial(_gathering(project, n_t, n_s, core_major=True, start_at=1), n_in=2, n_out=1), name="mix_in",
        grid=(n_t,),
        in_specs=[pl.BlockSpec((tm, D_MODEL), lambda i: (i, 0)), pl.BlockSpec((D_IN, D_MODEL), lambda i: (0, 0))]
        + [ANY] * n_s,
        out_specs=[pl.BlockSpec((tm, D_IN), lambda i: (i, 0))] + [ANY] * n_s,
        out_shape=[_sds((T, D_IN), F32)] + [_sds((N_DEV,) + a.shape, a.dtype) for a in shards],
        scratch_shapes=_Gather.scratch(n_s),
        compiler_params=pltpu.CompilerParams(dimension_semantics=("arbitrary",), collective_id=COLLECTIVE_IDS["mix_in"]),
    )(h1, w_in_t, *shards)
    return outs[0], list(outs[1:])


def _mixer_fwd(u, x, pv, wa, wx, wp, w_out_b, g_ffn, tm, shards=()):
    T = u.shape[0]
    n_s = len(shards)
    n_t = T // tm

    def body(u_ref, x_ref, pv_ref, wa_in, wx_in, wp_in, wo_ref, gf_ref, *rest):
        sh_in, rest = rest[:n_s], rest[n_s:]
        y_ref, hs_ref, hres_ref, h2_ref, saved_ref = rest[:5]
        sh_out, rest = rest[5:5 + n_s], rest[5 + n_s:]
        e_lru, e_pool, a_s, b_s, hc, wa_ref, wx_ref, wp_ref = rest[:8]
        gather = _Gather(sh_in, sh_out, *rest[8:], core_major=True) if n_s else None
        i = pl.program_id(0)

        @pl.when(i == 0)
        def _():
            if gather:
                gather.start()
            e_lru[pl.ds(0, HALO), :] = jnp.zeros((HALO, LRU_W), F32)
            e_pool[pl.ds(0, HALO), :] = jnp.zeros((HALO, POOL_W), F32)
            hc[...] = jnp.zeros((8, LRU_W), F32)
            _fill_block_diag(wa_ref, wa_in)
            _fill_block_diag(wx_ref, wx_in)
            _fill_block_diag(wp_ref, wp_in)

        if gather:
            @pl.when(i == (2 * n_t) // 3)
            def _():
                gather.relay()

        e_lru[pl.ds(HALO, tm), :] = u_ref[:, 0:LRU_W]
        e_pool[pl.ds(HALO, tm), :] = u_ref[:, 2 * LRU_W:D_IN]
        pv = pv_ref[...]
        xc, r, ig, la = _lru_gates(e_lru, pv, wa_ref, wx_ref, tm)
        for q, val in enumerate((xc, r, ig, la)):
            saved_ref[:, LRU_W * q:LRU_W * (q + 1)] = val
        a, _, mult, _ = _lru_decay(la)
        a_s[...] = a
        b_s[...] = mult * (ig * xc)
        hc[...] = _scan_tile(a_s, b_s, hs_ref, hc[...], tm, reverse=False)
        gl, _ = _gelu_parts(u_ref[:, LRU_W:2 * LRU_W])
        y_lru = hs_ref[...] * gl
        _, zp, _ = _pool_pre(e_pool, pv, wp_ref, tm, i * tm)
        y_pool = zp * pv[ROW_PS:ROW_PS + 1, :]
        yn = jnp.concatenate([y_lru * _rstd(y_lru) * pv[ROW_GL:ROW_GL + 1, :],
                              y_pool * _rstd(y_pool) * pv[ROW_GP:ROW_GP + 1, :]], axis=1).astype(BF16)
        for b in range(N_DEV):
            y_ref[:, 128 * _y_pos(b):128 * (_y_pos(b) + 1)] = yn[:, 128 * b:128 * (b + 1)]
        hr = x_ref[...] + jnp.dot(y_ref[...], wo_ref[...], preferred_element_type=F32)
        hres_ref[...] = hr
        h2_ref[...] = (hr * _rstd(hr) * gf_ref[...]).astype(BF16)
        e_lru[pl.ds(0, HALO), :] = e_lru[pl.ds(tm, HALO), :]
        e_pool[pl.ds(0, HALO), :] = e_pool[pl.ds(tm, HALO), :]

        if gather:
            @pl.when(i == n_t - 1)
            def _():
                gather.finish()

    full = lambda shape: pl.BlockSpec(shape, lambda i: (0,) * len(shape))
    row = lambda w: pl.BlockSpec((tm, w), lambda i: (i, 0))
    outs = pl.pallas_call(
        body, name="mixer_fwd", grid=(n_t,),
        in_specs=[row(D_IN), row(D_MODEL), full((16, LRU_W)), full((8, 64, 64)), full((8, 64, 64)), full((4, 128, 128)),
                  full((D_MODEL, D_MODEL)), full((1, D_MODEL))] + [ANY] * n_s,
        out_specs=[row(D_MODEL), row(LRU_W), row(D_MODEL), row(D_MODEL), row(4 * LRU_W)] + [ANY] * n_s,
        out_shape=[_sds((T, D_MODEL), BF16), _sds((T, LRU_W), F32), _sds((T, D_MODEL), F32), _sds((T, D_MODEL), BF16),
                   _sds((T, 4 * LRU_W), F32)] + [_sds((N_DEV,) + a.shape, a.dtype) for a in shards],
        scratch_shapes=[pltpu.VMEM((HALO + tm, LRU_W), F32), pltpu.VMEM((HALO + tm, POOL_W), F32),
                        pltpu.VMEM((tm, LRU_W), F32), pltpu.VMEM((tm, LRU_W), F32), pltpu.VMEM((8, LRU_W), F32)]
        + [pltpu.VMEM((2, 256, 256), BF16)] * 3 + (_Gather.scratch(n_s) if n_s else []),
        compiler_params=pltpu.CompilerParams(dimension_semantics=("arbitrary",),
                                             collective_id=COLLECTIVE_IDS["mixer_fwd"] if n_s else None),
    )(u, x, pv, wa, wx, wp, w_out_b, g_ffn, *shards)
    return outs[0], outs[1], outs[2], outs[3], outs[4], list(outs[5:])


def _ffn_fwd(hres, h2, w1_b, w3_b, w2_b, g_fin, tgt, tm):
    T = hres.shape[0]

    def body(hres_ref, h2_ref, w1_in, w3_in, w2_in, gfin_ref, tgt_ref,
             g_ref, v_ref, ff_ref, d3_ref, loss_ref, dgfin_ref, w1_ref, w3_ref, w2_ref, sems):
        first_step = pl.program_id(0) == 0
        loads = [pltpu.make_async_copy(src.at[pl.ds(lo, hi - lo)], dst.at[pl.ds(lo, hi - lo)], sems.at[w, q])
                 for q, (lo, hi) in enumerate(FF_CHUNKS)
                 for w, (src, dst) in enumerate(((w1_in, w1_ref), (w3_in, w3_ref), (w2_in, w2_ref)))]

        def tile(first):
            if first:
                for load in loads:
                    load.start()
                loss_ref[...] = jnp.zeros((8, 128), F32)
                dgfin_ref[...] = jnp.zeros((1, D_MODEL), F32)
                for load in loads[:2]:
                    load.wait()
            h2 = h2_ref[...]
            h3 = hres_ref[...]
            for q, (lo, hi) in enumerate(FF_CHUNKS):
                g = lax.dot_general(h2, w1_ref[lo:hi, :], NT, preferred_element_type=F32)
                v = lax.dot_general(h2, w3_ref[lo:hi, :], NT, preferred_element_type=F32)
                g_ref[:, lo:hi] = g.astype(BF16)
                v_ref[:, lo:hi] = v.astype(BF16)
                ff = ((g * _sigmoid(g)) * v).astype(BF16)
                ff_ref[:, lo:hi] = ff
                if first and q == 0:
                    for load in loads[2:]:
                        load.wait()
                h3 = h3 + jnp.dot(ff, w2_ref[lo:hi, :], preferred_element_type=F32)

            rstd = _rstd(h3)
            xh = h3 * rstd
            gfin = gfin_ref[...]
            err = xh * gfin - tgt_ref[...]
            loss_ref[...] += 0.5 * jnp.sum(jnp.mean(err * err, axis=-1, keepdims=True))
            dx, dgain = _rms_bwd(err * (1.0 / D_MODEL), xh, rstd, gfin)
            d3_ref[...] = dx
            dgfin_ref[...] += dgain

        pl.when(first_step)(functools.partial(tile, True))
        pl.when(jnp.logical_not(first_step))(functools.partial(tile, False))

    row = lambda w: pl.BlockSpec((tm, w), lambda i: (i, 0))
    const = lambda shape: pl.BlockSpec(shape, lambda i: (0,) * len(shape))
    return pl.pallas_call(
        body, name="ffn_fwd", grid=(T // tm,),
        in_specs=[row(D_MODEL), row(D_MODEL), ANY, ANY, ANY, const((1, D_MODEL)), row(D_MODEL)],
        out_specs=[row(D_FF), row(D_FF), row(D_FF), row(D_MODEL), const((8, 128)), const((1, D_MODEL))],
        out_shape=[_sds((T, D_FF), BF16), _sds((T, D_FF), BF16), _sds((T, D_FF), BF16),
                   _sds((T, D_MODEL), F32), _sds((8, 128), F32), _sds((1, D_MODEL), F32)],
        scratch_shapes=[pltpu.VMEM((D_FF, D_MODEL), BF16)] * 3 + [pltpu.SemaphoreType.DMA((3, len(FF_CHUNKS)))],
        compiler_params=pltpu.CompilerParams(dimension_semantics=("arbitrary",)),
    )(hres, h2, w1_b, w3_b, w2_b, g_fin, tgt)


def _ffn_bwd(d3, g, v, w1_b, w3_b, w2_b, hres, g_ffn, tm):
    T = d3.shape[0]

    def body(d3_ref, g_ref, v_ref, w1_ref, w3_ref, w2_ref, hres_ref, gf_ref, dg_ref, dv_ref, d2_ref, dgffn_ref):
        @pl.when(pl.program_id(0) == 0)
        def _():
            dgffn_ref[...] = jnp.zeros((1, D_MODEL), F32)

        d3 = d3_ref[...]
        d3b = d3.astype(BF16)
        dh2 = jnp.zeros((tm, D_MODEL), F32)
        for lo, hi in FF_CHUNKS:
            dff = lax.dot_general(d3b, w2_ref[lo:hi, :], NT, preferred_element_type=F32)
            gv = g_ref[:, lo:hi].astype(F32)
            vv = v_ref[:, lo:hi].astype(F32)
            sg = _sigmoid(gv)
            sl = gv * sg
            dgb = (dff * vv * (sg * (1.0 + gv * (1.0 - sg)))).astype(BF16)
            dvb = (dff * sl).astype(BF16)
            dg_ref[:, lo:hi] = dgb
            dv_ref[:, lo:hi] = dvb
            dh2 = dh2 + (jnp.dot(dgb, w1_ref[lo:hi, :], preferred_element_type=F32)
                         + jnp.dot(dvb, w3_ref[lo:hi, :], preferred_element_type=F32))

        hr = hres_ref[...]
        rstd = _rstd(hr)
        dx, dgain = _rms_bwd(dh2, hr * rstd, rstd, gf_ref[...])
        d2_ref[...] = d3 + dx
        dgffn_ref[...] += dgain

    row = lambda w: pl.BlockSpec((tm, w), lambda i: (i, 0))
    const = lambda shape: pl.BlockSpec(shape, lambda i: (0,) * len(shape))
    weight = pl.BlockSpec((D_FF, D_MODEL), lambda i: (0, 0), pipeline_mode=pl.Buffered(1))
    return pl.pallas_call(
        body, name="ffn_bwd", grid=(T // tm,),
        in_specs=[row(D_MODEL), row(D_FF), row(D_FF), weight, weight, weight, row(D_MODEL), const((1, D_MODEL))],
        out_specs=[row(D_FF), row(D_FF), row(D_MODEL), const((1, D_MODEL))],
        out_shape=[_sds((T, D_FF), BF16), _sds((T, D_FF), BF16), _sds((T, D_MODEL), F32), _sds((1, D_MODEL), F32)],
        compiler_params=pltpu.CompilerParams(dimension_semantics=("arbitrary",)),
    )(d3, g, v, w1_b, w3_b, w2_b, hres, g_ffn)


def _at_b_pair(a, b, c_arr, name, tk, gather=()):
    T, M = a.shape
    N = b.shape[1]
    hm, n_k = M // 2, T // tk
    n_g = len(gather)

    def body(c_ref, a_ref, b_ref, *rest):
        g_in, o_ref, rest = rest[:n_g], rest[n_g], rest[n_g + 1:]
        g_out, rest = rest[:n_g], rest[n_g:]
        acc, landed, send_sem, recv_sem = rest[:4]
        ag = _Gather(g_in, g_out, *rest[4:]) if n_g else None
        ph, k = pl.program_id(0), pl.program_id(1)

        def hand_over():
            x, y, c, _ = _place()
            return pltpu.make_async_remote_copy(
                src_ref=acc.at[0], dst_ref=landed, send_sem=send_sem, recv_sem=recv_sem,
                device_id=(x, y, 1 - c), device_id_type=MESH)

        if ag:
            @pl.when((ph == 0) & (k == 0))
            def _():
                ag.start()

            @pl.when((ph == 1) & (k == 0))
            def _():
                ag.relay()
        else:
            barrier = pltpu.get_barrier_semaphore()

            @pl.when((ph == 0) & (k == 0))
            def _():
                x, y, c, _ = _place()
                pl.semaphore_signal(barrier, inc=1, device_id=(x, y, 1 - c), device_id_type=MESH)

        @pl.when(k == 0)
        def _():
            acc[ph] = jnp.zeros((hm, N), F32)

        acc[ph] += lax.dot_general(a_ref[...].astype(BF16), b_ref[...].astype(BF16), TN, preferred_element_type=F32)

        @pl.when((ph == 0) & (k == n_k - 1))
        def _():
            if not ag:
                pl.semaphore_wait(barrier, 1)
            hand_over().start()

        @pl.when((ph == 1) & (k == n_k - 1))
        def _():
            copy = hand_over()
            copy.wait_recv()
            o_ref[...] = (acc[1] + landed[...]).astype(BF16)
            copy.wait_send()
            if ag:
                ag.finish()

    outs = pl.pallas_call(
        body, name=name,
        grid_spec=pltpu.PrefetchScalarGridSpec(
            num_scalar_prefetch=1, grid=(2, n_k),
            in_specs=[pl.BlockSpec((tk, hm), lambda ph, k, c_ref: (k, (ph + 1 - c_ref[0]) % 2)),
                      pl.BlockSpec((tk, N), lambda ph, k, c_ref: (k, 0))] + [ANY] * n_g,
            out_specs=[pl.BlockSpec((hm, N), lambda ph, k, c_ref: (0, 0))] + [ANY] * n_g,
            scratch_shapes=[pltpu.VMEM((2, hm, N), F32), pltpu.VMEM((hm, N), F32),
                            pltpu.SemaphoreType.DMA, pltpu.SemaphoreType.DMA] + (_Gather.scratch(n_g) if n_g else [])),
        out_shape=[_sds((hm, N), BF16)] + [_sds((N_DEV,) + g.shape, g.dtype) for g in gather],
        compiler_params=pltpu.CompilerParams(dimension_semantics=("arbitrary", "arbitrary"),
                                             collective_id=COLLECTIVE_IDS[name]),
    )(c_arr, a, b, *gather)
    return (outs[0], list(outs[1:])) if n_g else outs[0]


def _mixer_bwd(d2, u, hs, saved, pv, wa, wx, wp, w_out_b, tm, chip_sums=()):
    T = u.shape[0]
    n_t = T // tm
    n_x = len(chip_sums)

    def body(d2_ref, u_ref, uh_ref, hs_ref, hh_ref, saved_ref, pv_ref, wa_in, wx_in, wp_in, wo_ref, *rest):
        x_in, rest = rest[:n_x], rest[n_x:]
        du_ref, sg_ref = rest[:2]
        x_out, rest = rest[2:2 + n_x], rest[2 + n_x:]
        e_pool, e_h, a_s, b_s, dh_s, mu_s, f_x, f_p, mc, cx, cp = rest[:11]
        wa_ref, wx_ref, wp_ref, vacc_ref, dwa_ref, dwx_ref, dwp_ref = rest[11:18]
        exchange = _ChipExchange(x_in, x_out, *rest[18:]) if n_x else None
        s = pl.program_id(0)
        it = n_t - 1 - s

        @pl.when(s == 0)
        def _():
            if exchange:
                exchange.announce()
            mc[...] = jnp.zeros((8, LRU_W), F32)
            cx[...] = jnp.zeros((8, LRU_W), F32)
            cp[...] = jnp.zeros((HALO, POOL_W), F32)
            vacc_ref[...] = jnp.zeros((16, LRU_W), F32)
            dwa_ref[...] = jnp.zeros((2, 256, 256), F32)
            dwx_ref[...] = jnp.zeros((2, 256, 256), F32)
            dwp_ref[...] = jnp.zeros((2, 256, 256), F32)
            _fill_block_diag(wa_ref, wa_in)
            _fill_block_diag(wx_ref, wx_in)
            _fill_block_diag(wp_ref, wp_in)

        if exchange:
            @pl.when(s == 1)
            def _():
                exchange.start()

        first = it == 0
        e_pool[pl.ds(0, HALO), :] = jnp.where(first, 0.0, uh_ref[...])
        e_pool[pl.ds(HALO, tm), :] = u_ref[:, 2 * LRU_W:D_IN]
        e_h[pl.ds(0, 8), :] = jnp.where(first, 0.0, hh_ref[...])
        e_h[pl.ds(8, tm), :] = hs_ref[...]
        pv = pv_ref[...]
        saved = lambda q: saved_ref[:, LRU_W * q:LRU_W * (q + 1)]

        dyn = lax.dot_general(d2_ref[...].astype(BF16), wo_ref[...], NT, preferred_element_type=F32)
        dyn = jnp.concatenate([dyn[:, 128 * _y_pos(b):128 * (_y_pos(b) + 1)] for b in range(N_DEV)], axis=1)

        h = hs_ref[...]
        ug = u_ref[:, LRU_W:2 * LRU_W]
        gl, dgl = _gelu_parts(ug)
        y_lru = h * gl
        rstd_l = _rstd(y_lru)
        dy_lru, d_gain_l = _rms_bwd(dyn[:, 0:LRU_W], y_lru * rstd_l, rstd_l, pv[ROW_GL:ROW_GL + 1, :])
        dh = dy_lru * gl
        du_ref[:, LRU_W:2 * LRU_W] = (dy_lru * h * dgl).astype(BF16)
        a_s[...] = jnp.exp(saved(3))
        b_s[...] = a_s[...] * dh
        dh_s[...] = dh
        mu_s[pl.ds(tm, 8), :] = mc[...]
        mc[...] = _scan_tile(a_s, b_s, mu_s, mc[...], tm, reverse=True)
        xc, r, ig = saved(0), saved(1), saved(2)
        a, om, mult, rmult = _lru_decay(saved(3))
        lam_t = dh_s[...] + mu_s[pl.ds(1, tm), :]
        da = lam_t * e_h[pl.ds(7, tm), :]
        dmult = lam_t * (ig * xc)
        di = lam_t * (mult * xc)
        dxc = lam_t * (mult * ig)
        dla = da * a - jnp.where(om > 1e-12, dmult * ((a * a) * rmult), 0.0)
        dra = (dla * (-LRU_C * _softplus_neg_lambda(pv))) * (r * (1.0 - r))
        dia = di * (ig * (1.0 - ig))
        drab = dra.astype(BF16)
        diab = dia.astype(BF16)
        xcb = xc.astype(BF16)
        dxc = dxc + _bd_t(drab, wa_ref) + _bd_t(diab, wx_ref)
        dwa_ref[...] += _bd_grad(xcb, drab)
        dwx_ref[...] += _bd_grad(xcb, diab)
        sig_neg_lam = _sigmoid(-pv[ROW_LAM:ROW_LAM + 1, :])
        d_lam = jnp.sum(dla * r, axis=0, keepdims=True) * (LRU_C * sig_neg_lam)

        f_x[pl.ds(0, tm), :] = dxc
        f_x[pl.ds(tm, 8), :] = cx[...]
        du_lru = jnp.zeros((tm, LRU_W), F32)
        u_lru = u_ref[:, 0:LRU_W]
        d_cw = []
        for k in range(4):
            later = f_x[pl.ds(3 - k, tm), :]
            du_lru = du_lru + later * pv[ROW_CW + k:ROW_CW + k + 1, :]
            d_cw.append(jnp.sum(later * u_lru, axis=0, keepdims=True))
        du_ref[:, 0:LRU_W] = du_lru.astype(BF16)
        cx[...] = f_x[pl.ds(0, 8), :]

        pooled_b, zp, inv_cnts = _pool_pre(e_pool, pv, wp_ref, tm, it * tm)
        ps = pv[ROW_PS:ROW_PS + 1, :]
        y_pool = zp * ps
        rstd_p = _rstd(y_pool)
        dy_pool, d_gain_p = _rms_bwd(dyn[:, LRU_W:D_MODEL], y_pool * rstd_p, rstd_p, pv[ROW_GP:ROW_GP + 1, :])
        dz = dy_pool * ps
        dzb = dz.astype(BF16)
        dwp_ref[...] += _bd_grad(pooled_b, dzb)
        dpooled = _bd_t(dzb, wp_ref)
        for g, w in enumerate(POOL_WINDOWS):
            f_p[pl.ds(0, tm), pl.ds(128 * g, 128)] = _over_count(dpooled[:, 128 * g:128 * (g + 1)], w, inv_cnts[g])
        f_p[pl.ds(tm, HALO), :] = cp[...]
        for g, w in enumerate(POOL_WINDOWS):
            acc = _window_sum(f_p[:, pl.ds(128 * g, 128)], w, back=False)[0:tm, :]
            du_ref[:, 2 * LRU_W + 128 * g:2 * LRU_W + 128 * (g + 1)] = (
                acc - dpooled[:, 128 * g:128 * (g + 1)]).astype(BF16)
        cp[...] = f_p[pl.ds(0, HALO), :]

        rows = d_cw + [
            jnp.sum(dxc, axis=0, keepdims=True),
            jnp.sum(dra, axis=0, keepdims=True),
            jnp.sum(dia, axis=0, keepdims=True),
            d_lam,
            jnp.sum(dz, axis=0, keepdims=True),
            jnp.sum(dy_pool * zp, axis=0, keepdims=True),
            d_gain_l, d_gain_p,
            jnp.zeros((4, LRU_W), F32),
        ]
        vacc_ref[...] += jnp.concatenate(rows, axis=0)

        @pl.when(s == n_t - 1)
        def _():
            sg_ref[SG_VEC:SG_VEC + 16, :] = vacc_ref[:, 0:256]
            sg_ref[SG_VEC + 16:SG_VEC + 32, :] = vacc_ref[:, 256:512]
            for half in range(2):
                sg_ref[SG_WA + 64 * half:SG_WA + 64 * (half + 1), :] = _diag_pack(dwa_ref[half], 64)
                sg_ref[SG_WX + 64 * half:SG_WX + 64 * (half + 1), :] = _diag_pack(dwx_ref[half], 64)
                sg_ref[SG_WP + 128 * half:SG_WP + 128 * (half + 1), :] = _diag_pack(dwp_ref[half], 128)
            if exchange:
                exchange.finish()

    rev = lambda w: pl.BlockSpec((tm, w), lambda s: (n_t - 1 - s, 0))
    full = lambda shape: pl.BlockSpec(shape, lambda s: (0,) * len(shape))
    outs = pl.pallas_call(
        body, name="mixer_bwd", grid=(n_t,),
        in_specs=[rev(D_MODEL), rev(D_IN),
                  pl.BlockSpec((HALO, POOL_W), lambda s: (jnp.maximum((n_t - 1 - s) * (tm // HALO) - 1, 0), 2)),
                  rev(LRU_W),
                  pl.BlockSpec((8, LRU_W), lambda s: (jnp.maximum((n_t - 1 - s) * (tm // 8) - 1, 0), 0)),
                  rev(4 * LRU_W), full((16, LRU_W)), full((8, 64, 64)), full((8, 64, 64)), full((4, 128, 128)),
                  full((D_MODEL, D_MODEL))] + [ANY] * n_x,
        out_specs=[rev(D_IN), full((SG_ROWS, 256))] + [ANY] * n_x,
        out_shape=[_sds((T, D_IN), BF16), _sds((SG_ROWS, 256), F32)] + [_sds(a.shape, a.dtype) for a in chip_sums],
        scratch_shapes=[pltpu.VMEM((HALO + tm, POOL_W), F32),
                        pltpu.VMEM((8 + tm, LRU_W), F32)] + [pltpu.VMEM((tm, LRU_W), F32)] * 3 + [
                        pltpu.VMEM((tm + 8, LRU_W), F32), pltpu.VMEM((tm + 8, LRU_W), F32),
                        pltpu.VMEM((tm + HALO, POOL_W), F32), pltpu.VMEM((8, LRU_W), F32),
                        pltpu.VMEM((8, LRU_W), F32), pltpu.VMEM((HALO, POOL_W), F32)]
        + [pltpu.VMEM((2, 256, 256), BF16)] * 3 + [pltpu.VMEM((16, LRU_W), F32)] + [pltpu.VMEM((2, 256, 256), F32)] * 3
        + (_ChipExchange.scratch(n_x) if n_x else []),
        compiler_params=pltpu.CompilerParams(dimension_semantics=("arbitrary",),
                                             collective_id=COLLECTIVE_IDS["mixer_bwd"] if n_x else None),
    )(d2, u, u, hs, hs, saved, pv, wa, wx, wp, w_out_b, *chip_sums)
    return outs[0], outs[1], list(outs[2:])


def _mix_in_bwd(du, x, d2, w_in_t, g_mix, tm):
    T = x.shape[0]

    def body(du_ref, x_ref, d2_ref, w_ref, g_ref, dx_ref, dg_ref):
        @pl.when(pl.program_id(0) == 0)
        def _():
            dg_ref[...] = jnp.zeros((1, D_MODEL), F32)

        dh = jnp.dot(du_ref[...], w_ref[...], preferred_element_type=F32)
        xv = x_ref[...]
        rstd = _rstd(xv)
        dx, dgain = _rms_bwd(dh, xv * rstd, rstd, g_ref[...])
        dx_ref[...] = d2_ref[...] + dx
        dg_ref[...] += dgain

    row = lambda w: pl.BlockSpec((tm, w), lambda i: (i, 0))
    const = lambda shape: pl.BlockSpec(shape, lambda i: (0,) * len(shape))
    return pl.pallas_call(
        body, name="mix_in_bwd", grid=(T // tm,),
        in_specs=[row(D_IN), row(D_MODEL), row(D_MODEL), const((D_IN, D_MODEL)), const((1, D_MODEL))],
        out_specs=[row(D_MODEL), const((1, D_MODEL))],
        out_shape=[_sds((T, D_MODEL), F32), _sds((1, D_MODEL), F32)],
        compiler_params=pltpu.CompilerParams(dimension_semantics=("arbitrary",)),
    )(du, x, d2, w_in_t, g_mix)


def _adamw(w, g, m, v):
    m = ADAM_B1 * m + (1.0 - ADAM_B1) * g
    v = ADAM_B2 * v + (1.0 - ADAM_B2) * (g * g)
    m_hat = m / (1.0 - ADAM_B1 ** ADAM_STEP)
    v_hat = v / (1.0 - ADAM_B2 ** ADAM_STEP)
    delta = -ADAM_LR * (m_hat / (jnp.sqrt(v_hat) + ADAM_EPS) + ADAM_WD * w)
    return delta, m, v


def _adam_shards(ws, ms, vs, parts):
    n = len(ws)
    n_blk = [w.shape[0] // ADAM_ROWS for w in ws]

    def body(*refs):
        w_refs, m_refs, v_refs, p_refs, outs = (refs[:n], refs[n:2 * n], refs[2 * n:3 * n], refs[3 * n:4 * n],
                                                refs[4 * n:])
        i = pl.program_id(0)
        for a in range(n):
            @pl.when(i < n_blk[a])
            def _(a=a):
                g = p_refs[a][0].astype(F32)
                for j in range(1, 4):
                    g = g + p_refs[a][j].astype(F32)
                delta, new_m, new_v = _adamw(w_refs[a][...], g, m_refs[a][...], v_refs[a][...])
                for kind, val in enumerate((g, delta, new_m, new_v)):
                    outs[4 * a + kind][...] = val

    blk = lambda a: pl.BlockSpec((ADAM_ROWS, D_MODEL), lambda i: (jnp.minimum(i, n_blk[a] - 1), 0))
    part_blk = lambda a: pl.BlockSpec((4, ADAM_ROWS, D_MODEL), lambda i: (0, jnp.minimum(i, n_blk[a] - 1), 0))
    res = pl.pallas_call(
        body, name="adam_shards", grid=(max(n_blk),),
        in_specs=[blk(a) for a in range(n)] * 3 + [part_blk(a) for a in range(n)],
        out_specs=[blk(a) for a in range(n) for _ in range(4)],
        out_shape=[_sds(w.shape, F32) for w in ws for _ in range(4)],
        compiler_params=pltpu.CompilerParams(dimension_semantics=("arbitrary",)),
    )(*ws, *ms, *vs, *parts)
    return [tuple(res[4 * a:4 * a + 4]) for a in range(n)]


SMALL_PARAMS = [("norm_mix_g", (1, D_MODEL)), ("conv_w", (1, 4, 64)), ("conv_b", (1, LRU_W)),
                ("gate_a_w", (1, 8, 64, 64)), ("gate_a_b", (1, LRU_W)), ("gate_x_w", (1, 8, 64, 64)),
                ("gate_x_b", (1, LRU_W)), ("lru_lambda", (1, LRU_W)), ("pool_w", (1, 4, 128, 128)),
                ("pool_b", (1, POOL_W)), ("pool_scale", (1, POOL_W)), ("norm_lru_g", (1, LRU_W)),
                ("norm_pool_g", (1, POOL_W)), ("norm_ffn_g", (1, D_MODEL)), ("final_norm_g", (1, D_MODEL))]
VEC_ROW = dict(conv_b=ROW_CB, gate_a_b=ROW_BA, gate_x_b=ROW_BX, lru_lambda=ROW_LAM, pool_b=ROW_PB, pool_scale=ROW_PS,
               norm_lru_g=ROW_GL, norm_pool_g=ROW_GP)
WHOLE = (Ellipsis,)


def _unpack_mixer_grads(sg, dev):
    vec = jnp.concatenate([sg[SG_VEC:SG_VEC + 16], sg[SG_VEC + 16:SG_VEC + 32]], axis=1)
    out = {nm: [(WHOLE, vec[r:r + 1])] for nm, r in VEC_ROW.items()}
    own = jnp.zeros((4, 64), F32)
    for d in range(N_DEV):
        own = jnp.where(dev == d, vec[ROW_CW:ROW_CW + 4, 64 * d:64 * (d + 1)], own)
    out["conv_w"] = [((0,), own)]
    for nm, row0 in (("gate_a_w", SG_WA), ("gate_x_w", SG_WX)):
        out[nm] = [((0, b), sg[row0 + 64 * (b // 4):row0 + 64 * (b // 4 + 1), 64 * (b % 4):64 * (b % 4 + 1)])
                   for b in range(8)]
    out["pool_w"] = [((0, b), sg[SG_WP + 128 * (b // 2):SG_WP + 128 * (b // 2 + 1), 128 * (b % 2):128 * (b % 2 + 1)])
                     for b in range(4)]
    return out


def _adam_small(parts, w, m, v):
    names = [nm for nm, _ in SMALL_PARAMS]
    n = len(names)

    def body(sg_ref, gm_ref, gf_ref, gn_ref, ls_ref, *rest):
        w_refs, m_refs, v_refs, outs = rest[:n], rest[n:2 * n], rest[2 * n:3 * n], rest[3 * n:]
        dev = 4 * lax.axis_index("x") + 2 * lax.axis_index("y") + lax.axis_index("c")

        def total(ref):
            acc = ref[0]
            for d in range(1, N_DEV):
                acc = acc + ref[d]
            return acc

        pieces = _unpack_mixer_grads(total(sg_ref), dev)
        pieces["norm_mix_g"] = [(WHOLE, total(gm_ref))]
        pieces["norm_ffn_g"] = [(WHOLE, total(gf_ref))]
        pieces["final_norm_g"] = [(WHOLE, total(gn_ref))]
        for i, nm in enumerate(names):
            for idx, g in pieces[nm]:
                delta, new_m, new_v = _adamw(w_refs[i][idx], g, m_refs[i][idx], v_refs[i][idx])
                for kind, val in enumerate((g, delta, new_m, new_v)):
                    outs[4 * i + kind][idx] = val
        outs[4 * n][...] = total(ls_ref)

    shapes = [_sds(shape, F32) for _, shape in SMALL_PARAMS for _ in range(4)] + [_sds((8, 128), F32)]
    res = pl.pallas_call(body, name="adam_small", out_shape=shapes)(
        *parts, *[w[nm] for nm in names], *[m[nm] for nm in names], *[v[nm] for nm in names])
    return {nm: tuple(res[4 * i:4 * i + 4]) for i, nm in enumerate(names)}, res[4 * n][0, 0]


def _vec_rows(conv_w_full, conv_b, ba, bx, lam, pb, ps, gl, gp):
    return jnp.concatenate([conv_w_full, conv_b, ba, bx, lam, pb, ps, gl, gp, jnp.zeros((4, LRU_W), F32)], axis=0)


WEIGHT_ORDER = ['norm_mix_g', 'w_in', 'conv_w', 'conv_b', 'gate_a_w', 'gate_a_b', 'gate_x_w', 'gate_x_b', 'lru_lambda',
                'pool_w', 'pool_b', 'pool_scale', 'norm_lru_g', 'norm_pool_g', 'w_out', 'norm_ffn_g', 'ffn_w1', 'ffn_w3',
                'ffn_w2', 'final_norm_g']


def kernel(x, norm_mix_g, w_in, conv_w, conv_b, gate_a_w, gate_a_b, gate_x_w, gate_x_b, lru_lambda, pool_w, pool_b, pool_scale, norm_lru_g, norm_pool_g, w_out, norm_ffn_g, ffn_w1, ffn_w3, ffn_w2, final_norm_g, loss_target, m_norm_mix_g, m_w_in, m_conv_w, m_conv_b, m_gate_a_w, m_gate_a_b, m_gate_x_w, m_gate_x_b, m_lru_lambda, m_pool_w, m_pool_b, m_pool_scale, m_norm_lru_g, m_norm_pool_g, m_w_out, m_norm_ffn_g, m_ffn_w1, m_ffn_w3, m_ffn_w2, m_final_norm_g, v_norm_mix_g, v_w_in, v_conv_w, v_conv_b, v_gate_a_w, v_gate_a_b, v_gate_x_w, v_gate_x_b, v_lru_lambda, v_pool_w, v_pool_b, v_pool_scale, v_norm_lru_g, v_norm_pool_g, v_w_out, v_norm_ffn_g, v_ffn_w1, v_ffn_w3, v_ffn_w2, v_final_norm_g):
    ac = lax.axis_index("c")
    tm, tmx, tm_in, tk = 512, 512, 1024, 2048
    xs, tgt = x[0], loss_target[0]
    g_fin = final_norm_g.reshape(1, D_MODEL)
    c_arr = jnp.reshape(ac, (1,)).astype(jnp.int32)

    tr = lambda w: jnp.swapaxes(w[0], 0, 1)
    own = lambda w: w[0]
    bf = lambda a: a.astype(BF16)

    h1, (g_in, g_conv) = _norm_in(xs, norm_mix_g, [bf(tr(w_in)), conv_w[0]], tm_in)
    w_in_t = g_in.reshape(D_IN, D_MODEL)
    u, (g_out,) = _mix_in(h1, w_in_t, tm_in, shards=[bf(own(w_out))])
    conv_w_full = g_conv.transpose(1, 0, 2).reshape(4, LRU_W)
    pv = _vec_rows(conv_w_full, conv_b, gate_a_b, gate_x_b, lru_lambda, pool_b, pool_scale, norm_lru_g, norm_pool_g)
    wa, wx, wp = gate_a_w[0], gate_x_w[0], pool_w[0]
    w_out_b = g_out.reshape(D_MODEL, D_MODEL)
    y, hs, hres, h2, saved, (g_w1, g_w3, g_w2) = _mixer_fwd(
        u, xs, pv, wa, wx, wp, w_out_b, norm_ffn_g, tmx, shards=[bf(tr(ffn_w1)), bf(tr(ffn_w3)), bf(own(ffn_w2))])
    w1_t, w3_t, w2_b = g_w1.reshape(D_FF, D_MODEL), g_w3.reshape(D_FF, D_MODEL), g_w2.reshape(D_FF, D_MODEL)
    g, v, ff, d3, loss_acc, d_gfin = _ffn_fwd(hres, h2, w1_t, w3_t, w2_b, g_fin, tgt, tm)

    dg, dv, d2, d_gffn = _ffn_bwd(d3, g, v, w1_t, w3_t, w2_b, hres, norm_ffn_g, tm)
    chips = lambda a: a.reshape(4, a.shape[0] // 4, a.shape[1])
    early = [(y, d2, "grad_w_out"), (dg, h2, "grad_w1"), (dv, h2, "grad_w3"), (ff, d3, "grad_w2")]
    early_sums = [chips(_at_b_pair(a, b, c_arr, name, tk)) for a, b, name in early]
    du, d_mixer, early_parts = _mixer_bwd(d2, u, hs, saved, pv, wa, wx, wp, w_out_b, tmx, chip_sums=early_sums)
    grad_x, d_gmix = _mix_in_bwd(du, xs, d2, w_in_t, norm_mix_g, tm_in)
    d_win, small_parts = _at_b_pair(du, h1, c_arr, "grad_w_in", tk,
                                    gather=[d_mixer, d_gmix, d_gffn, d_gfin, loss_acc])
    parts = [_half_exchange(chips(d_win), "grads_to_chips_w_in")] + list(early_parts)

    res = {}
    shard_w = dict(w_in=(w_in, m_w_in, v_w_in, tr), w_out=(w_out, m_w_out, v_w_out, own),
                   ffn_w1=(ffn_w1, m_ffn_w1, v_ffn_w1, tr), ffn_w3=(ffn_w3, m_ffn_w3, v_ffn_w3, tr),
                   ffn_w2=(ffn_w2, m_ffn_w2, v_ffn_w2, own))
    shard_res = _adam_shards([view(w) for w, _, _, view in shard_w.values()],
                             [view(m) for _, m, _, view in shard_w.values()],
                             [view(v) for _, _, v, view in shard_w.values()], parts)
    for (nm, (_, _, _, view)), outs in zip(shard_w.items(), shard_res):
        res[nm] = [(jnp.swapaxes(o, 0, 1) if view is tr else o)[None] for o in outs]

    row = lambda a: a.reshape(1, D_MODEL)
    small = lambda gm, cw, cb, wa_, ba, wx_, bx, lam, pw, pb, ps, gl, gp, gf, gn: dict(
        norm_mix_g=gm, conv_w=cw, conv_b=cb, gate_a_w=wa_, gate_a_b=ba, gate_x_w=wx_, gate_x_b=bx, lru_lambda=lam,
        pool_w=pw, pool_b=pb, pool_scale=ps, norm_lru_g=gl, norm_pool_g=gp, norm_ffn_g=gf, final_norm_g=row(gn))
    small_res, loss = _adam_small(
        small_parts,
        small(norm_mix_g, conv_w, conv_b, gate_a_w, gate_a_b, gate_x_w, gate_x_b, lru_lambda, pool_w, pool_b,
              pool_scale, norm_lru_g, norm_pool_g, norm_ffn_g, final_norm_g),
        small(m_norm_mix_g, m_conv_w, m_conv_b, m_gate_a_w, m_gate_a_b, m_gate_x_w, m_gate_x_b, m_lru_lambda, m_pool_w,
              m_pool_b, m_pool_scale, m_norm_lru_g, m_norm_pool_g, m_norm_ffn_g, m_final_norm_g),
        small(v_norm_mix_g, v_conv_w, v_conv_b, v_gate_a_w, v_gate_a_b, v_gate_x_w, v_gate_x_b, v_lru_lambda, v_pool_w,
              v_pool_b, v_pool_scale, v_norm_lru_g, v_norm_pool_g, v_norm_ffn_g, v_final_norm_g))
    for nm, outs in small_res.items():
        res[nm] = [o.reshape(D_MODEL) for o in outs] if nm == "final_norm_g" else list(outs)

    out = [loss, grad_x[None]]
    for kind in range(4):
        out += [res[nm][kind] for nm in WEIGHT_ORDER]
    return tuple(out)
```

```python
import functools

import jax
import jax.numpy as jnp
from jax import lax
from jax.experimental import pallas as pl
from jax.experimental.pallas import tpu as pltpu

F32 = jnp.float32
BF16 = jnp.bfloat16

D_MODEL = 1024
LRU_W = 512
POOL_W = 512
D_IN = 1536
D_FF = 2816
POOL_WINDOWS = (2, 4, 8, 16)
EPS = 1e-6
LRU_C = 8.0
N_DEV = 8
HALO = 16
SCAN_UNROLL = 8
ADAM_ROWS = 32
FF_CHUNKS = ((0, 1536), (1536, 2816))
RELAY_SPLIT_ROWS = 32
COLLECTIVE_IDS = {name: j for j, name in enumerate(
    ("grad_w_out", "grad_w1", "grad_w3", "grad_w2", "norm_in", "mix_in", "mixer_fwd", "mixer_bwd", "grad_w_in",
     "grads_to_chips_w_in"))}

ADAM_LR = 0.001
ADAM_B1 = 0.9
ADAM_B2 = 0.999
ADAM_EPS = 1e-08
ADAM_WD = 0.01
ADAM_STEP = 10

ROW_CW, ROW_CB, ROW_BA, ROW_BX, ROW_LAM, ROW_PB, ROW_PS, ROW_GL, ROW_GP = 0, 4, 5, 6, 7, 8, 9, 10, 11
SG_VEC, SG_WA, SG_WX, SG_WP, SG_ROWS = 0, 32, 160, 288, 544

NT = (((1,), (1,)), ((), ()))
TN = (((0,), (0,)), ((), ()))


def _sds(shape, dtype):
    return jax.ShapeDtypeStruct(shape, dtype)


def _sigmoid(x):
    return 0.5 * jnp.tanh(0.5 * x) + 0.5


def _gelu_parts(x):
    c = 0.7978845608028654
    inner = c * (x + 0.044715 * (x * x * x))
    th = jnp.tanh(inner)
    g = 0.5 * x * (1.0 + th)
    dg = 0.5 * (1.0 + th) + 0.5 * x * (1.0 - th * th) * (c * (1.0 + 3.0 * 0.044715 * (x * x)))
    return g, dg


def _window_sum(ext, w, back):
    n = ext.shape[0]
    s, k = ext, 1
    while k < w:
        s = s + pltpu.roll(s, k if back else n - k, 0)
        k *= 2
    return s


def _rstd(x):
    return lax.rsqrt(jnp.mean(x * x, axis=-1, keepdims=True) + EPS)


def _rms_bwd(dy, xhat, rstd, gain):
    dxh = dy * gain
    dx = rstd * (dxh - xhat * jnp.mean(dxh * xhat, axis=-1, keepdims=True))
    return dx, jnp.sum(dy * xhat, axis=0, keepdims=True)


def _bd(xb, w_ref):
    return jnp.concatenate(
        [jnp.dot(xb[:, :256], w_ref[0], preferred_element_type=F32),
         jnp.dot(xb[:, 256:], w_ref[1], preferred_element_type=F32)], axis=1)


def _bd_t(xb, w_ref):
    return jnp.concatenate(
        [lax.dot_general(xb[:, :256], w_ref[0], NT, preferred_element_type=F32),
         lax.dot_general(xb[:, 256:], w_ref[1], NT, preferred_element_type=F32)], axis=1)


def _bd_grad(xb, db):
    return jnp.stack(
        [lax.dot_general(xb[:, :256], db[:, :256], TN, preferred_element_type=F32),
         lax.dot_general(xb[:, 256:], db[:, 256:], TN, preferred_element_type=F32)], axis=0)


def _fill_block_diag(dst, src_ref):
    n, k, _ = src_ref.shape
    dst[...] = jnp.zeros(dst.shape, BF16)
    for b in range(n):
        p, q = divmod(b, 256 // k)
        dst[p, q * k:(q + 1) * k, q * k:(q + 1) * k] = src_ref[b].astype(BF16)


def _diag_pack(w, k):
    lane = lax.broadcasted_iota(jnp.int32, (k, 256), 1)
    out = w[0:k]
    for q in range(1, 256 // k):
        out = jnp.where(lane >= q * k, w[q * k:(q + 1) * k], out)
    return out


def _y_pos(b):
    return 4 * (b % 2) + b // 2


def _softplus_neg_lambda(pv):
    z = -pv[ROW_LAM:ROW_LAM + 1, :]
    return jnp.maximum(z, 0.0) + jnp.log(1.0 + jnp.exp(-jnp.abs(z)))


def _lru_gates(e_lru, pv, wa_ref, wx_ref, tm):
    xc = pv[ROW_CB:ROW_CB + 1, :]
    for k in range(4):
        xc = xc + e_lru[pl.ds(HALO - 3 + k, tm), :] * pv[ROW_CW + k:ROW_CW + k + 1, :]
    xcb = xc.astype(BF16)
    r = _sigmoid(_bd(xcb, wa_ref) + pv[ROW_BA:ROW_BA + 1, :])
    ig = _sigmoid(_bd(xcb, wx_ref) + pv[ROW_BX:ROW_BX + 1, :])
    return xc, r, ig, (-LRU_C * r) * _softplus_neg_lambda(pv)


def _lru_decay(la):
    a = jnp.exp(la)
    om = -jnp.tanh(la) * (1.0 + a * a)
    omc = jnp.maximum(om, 1e-12)
    rmult = lax.rsqrt(omc)
    return a, om, omc * rmult, rmult


def _over_count(v, w, inv_head):
    return jnp.concatenate([v[0:HALO] * inv_head, v[HALO:] * (1.0 / w)], axis=0)


def _pool_pre(e_pool, pv, wp_ref, tm, t0):
    t_head = t0 + lax.broadcasted_iota(jnp.int32, (HALO, 1), 0)
    parts, inv_heads = [], []
    for g, w in enumerate(POOL_WINDOWS):
        ext = e_pool[:, pl.ds(128 * g, 128)]
        s = _window_sum(ext, w, back=True)[HALO:, :]
        inv_head = 1.0 / jnp.minimum(t_head + 1, w).astype(F32)
        inv_heads.append(inv_head)
        parts.append(_over_count(s, w, inv_head) - ext[HALO:, :])
    pooled = jnp.concatenate(parts, axis=1)
    pooled_b = pooled.astype(BF16)
    zp = _bd(pooled_b, wp_ref) + pv[ROW_PB:ROW_PB + 1, :]
    return pooled_b, zp, inv_heads


def _scan_tile(a_ref, b_ref, out_ref, carry, tm, reverse):
    row = lax.broadcasted_iota(jnp.int32, (8, LRU_W), 0)
    nblk = tm // 8

    def local_scan(blk):
        r0 = pl.multiple_of(blk * 8, 8)
        av = a_ref[pl.ds(r0, 8), :]
        bv = b_ref[pl.ds(r0, 8), :]
        for d in (1, 2, 4):
            sh = (8 - d) if reverse else d
            a_s = pltpu.roll(av, sh, 0)
            b_s = pltpu.roll(bv, sh, 0)
            m = (row < 8 - d) if reverse else (row >= d)
            bv = jnp.where(m, av * b_s + bv, bv)
            av = jnp.where(m, av * a_s, av)
        return r0, av, bv

    def step(i, hin):
        local = [local_scan((nblk - 1 - (i * SCAN_UNROLL + j)) if reverse else (i * SCAN_UNROLL + j))
                 for j in range(SCAN_UNROLL)]
        for r0, av, bv in local:
            hv = av * hin + bv
            out_ref[pl.ds(r0, 8), :] = hv
            hin = jnp.broadcast_to(hv[0:1, :] if reverse else hv[7:8, :], (8, LRU_W))
        return hin

    return lax.fori_loop(0, nblk // SCAN_UNROLL, step, carry)


MESH = pl.DeviceIdType.MESH
ANY = pl.BlockSpec(memory_space=pl.ANY)


def _place():
    x, y, c = lax.axis_index("x"), lax.axis_index("y"), lax.axis_index("c")
    chips = [(1 - x, y), (x, 1 - y), (1 - x, 1 - y)]
    return x, y, c, chips


def _meet(peers):
    _announce(peers)
    _await(peers)


def _announce(peers):
    for peer in peers:
        pl.semaphore_signal(pltpu.get_barrier_semaphore(), inc=1, device_id=peer, device_id_type=MESH)


def _await(peers):
    pl.semaphore_wait(pltpu.get_barrier_semaphore(), len(peers))


class _Gather:
    def __init__(self, ins, outs, send_sems, recv_sems, local_sems, core_major=False):
        self.ins, self.outs, self.n = ins, outs, len(ins)
        self.send_sems, self.recv_sems, self.local_sems = send_sems, recv_sems, local_sems
        self.core_major = core_major

    @staticmethod
    def scratch(n):
        return [pltpu.SemaphoreType.DMA((8, n)), pltpu.SemaphoreType.DMA((8, n)), pltpu.SemaphoreType.DMA((n,))]

    def _slot(self, a, px, py, pc):
        return self.outs[a].at[4 * pc + 2 * px + py if self.core_major else 4 * px + 2 * py + pc]

    def _half(self, a, h):
        rows = self.ins[a].shape[0]
        if rows % RELAY_SPLIT_ROWS:
            return None if h else (0, rows)
        return (h * (rows // 2), rows // 2)

    def _copy(self, a, k, block, to, src=None, rows=None):
        dst = self._slot(a, *block)
        src = dst if src is None else src
        if rows is not None:
            src, dst = src.at[pl.ds(*rows)], dst.at[pl.ds(*rows)]
        return pltpu.make_async_remote_copy(
            src_ref=src, dst_ref=dst, send_sem=self.send_sems.at[k, a], recv_sem=self.recv_sems.at[k, a],
            device_id=to, device_id_type=MESH)

    def _mine(self, a):
        x, y, c, _ = _place()
        return pltpu.make_async_copy(self.ins[a], self._slot(a, x, y, c), self.local_sems.at[a])

    def _first(self, a):
        x, y, c, chips = _place()
        me = (x, y, c)
        return ([self._copy(a, 0, me, (x, y, 1 - c), src=self.ins[a])]
                + [self._copy(a, 1 + j, me, (*chip, c), src=self.ins[a]) for j, chip in enumerate(chips[:2])])

    def _passed_on(self, a, h):
        x, y, c, chips = _place()
        block = (*chips[h], c)
        out = [self._copy(a, 4 + h, block, (x, y, 1 - c))]
        if self._half(a, h) is not None:
            out.append(self._copy(a, (3, 7)[h], block, (*chips[1 - h], c), rows=self._half(a, h)))
        return out

    def _peers(self):
        x, y, c, chips = _place()
        return [(x, y, 1 - c), (*chips[0], c), (*chips[1], c)]

    def announce(self):
        _announce(self._peers())

    def start(self, announced=False):
        (_await if announced else _meet)(self._peers())
        for a in range(self.n):
            self._mine(a).start()
        for a in range(self.n):
            for cp in self._first(a):
                cp.start()

    def relay(self):
        x, y, c, chips = _place()
        for h in range(2):
            for a in range(self.n):
                self._copy(a, 1 + h, (*chips[h], c), (x, y, c)).wait_recv()
                for cp in self._passed_on(a, h):
                    cp.start()

    def finish(self):
        x, y, c, chips = _place()
        me, sibling = (x, y, c), (x, y, 1 - c)
        passed = []
        for a in range(self.n):
            for h in range(2):
                if self._half(a, h) is not None:
                    self._copy(a, (3, 7)[h], (*chips[2], c), me, rows=self._half(a, h)).wait_recv()
            fwd = self._copy(a, 6, (*chips[2], c), sibling)
            fwd.start()
            passed.append(fwd)
        for a in range(self.n):
            self._copy(a, 0, (x, y, 1 - c), me).wait_recv()
            for j, chip in enumerate(chips):
                self._copy(a, 4 + j, (*chip, 1 - c), me).wait_recv()
        for a in range(self.n):
            for cp in self._first(a) + self._passed_on(a, 0) + self._passed_on(a, 1):
                cp.wait_send()
        for cp in passed:
            cp.wait_send()
        for a in range(self.n):
            self._mine(a).wait()


def _half_exchange(arr, name):
    def body(in_ref, out_ref, send_sems, recv_sems, local_sem):
        x, y, c, _ = _place()
        my_chip = 2 * x + y
        _meet([((x + dx) % 2, (y + dy) % 2, (c + dc) % 2)
               for dx in range(2) for dy in range(2) for dc in range(2) if dx + dy + dc])

        def send(j, wait):
            to_me = (c == x) & (y == j // 2) & (c == j % 2)

            @pl.when(to_me)
            def _():
                local = pltpu.make_async_copy(in_ref.at[j], out_ref.at[my_chip], local_sem)
                local.wait() if wait else local.start()

            @pl.when(jnp.logical_not(to_me))
            def _():
                remote = pltpu.make_async_remote_copy(
                    src_ref=in_ref.at[j], dst_ref=out_ref.at[my_chip], send_sem=send_sems.at[j],
                    recv_sem=recv_sems.at[my_chip], device_id=(c, j // 2, j % 2), device_id_type=MESH)
                remote.wait_send() if wait else remote.start()

        for j in range(4):
            send(j, wait=False)
        for j in range(4):
            send(j, wait=True)
        for k in range(4):
            from_me = (k // 2 == x) & (k % 2 == y) & (c == x)

            @pl.when(jnp.logical_not(from_me))
            def _():
                pltpu.make_async_remote_copy(
                    src_ref=in_ref.at[0], dst_ref=out_ref.at[k], send_sem=send_sems.at[0], recv_sem=recv_sems.at[k],
                    device_id=(k // 2, k % 2, x), device_id_type=MESH).wait_recv()

    return pl.pallas_call(
        body, name=name, out_shape=_sds(arr.shape, arr.dtype), in_specs=[ANY], out_specs=ANY,
        scratch_shapes=[pltpu.SemaphoreType.DMA((4,)), pltpu.SemaphoreType.DMA((4,)), pltpu.SemaphoreType.DMA],
        compiler_params=pltpu.CompilerParams(collective_id=COLLECTIVE_IDS[name]),
    )(arr)


class _ChipExchange:
    def __init__(self, ins, outs, send_sems, recv_sems, local_sems):
        self.ins, self.outs, self.n = ins, outs, len(ins)
        self.send_sems, self.recv_sems, self.local_sems = send_sems, recv_sems, local_sems

    @staticmethod
    def scratch(n):
        return [pltpu.SemaphoreType.DMA((3, n)), pltpu.SemaphoreType.DMA((3, n)), pltpu.SemaphoreType.DMA((n,))]

    def _local(self, a):
        x, y, _, _ = _place()
        me = 2 * x + y
        return pltpu.make_async_copy(self.ins[a].at[me], self.outs[a].at[me], self.local_sems.at[a])

    def _copies(self, a):
        x, y, c, chips = _place()
        me = 2 * x + y
        return [(pltpu.make_async_remote_copy(
                     src_ref=self.ins[a].at[2 * px + py], dst_ref=self.outs[a].at[me],
                     send_sem=self.send_sems.at[k, a], recv_sem=self.recv_sems.at[k, a],
                     device_id=(px, py, c), device_id_type=MESH),
                 pltpu.make_async_remote_copy(
                     src_ref=self.ins[a].at[me], dst_ref=self.outs[a].at[2 * px + py],
                     send_sem=self.send_sems.at[k, a], recv_sem=self.recv_sems.at[k, a],
                     device_id=(px, py, c), device_id_type=MESH))
                for k, (px, py) in enumerate(chips)]

    def _peers(self):
        _, _, c, chips = _place()
        return [(*chip, c) for chip in chips]

    def announce(self):
        _announce(self._peers())

    def start(self):
        _await(self._peers())
        for a in range(self.n):
            self._local(a).start()
        for a in range(self.n):
            for send, _ in self._copies(a):
                send.start()

    def finish(self):
        for a in range(self.n):
            for send, recv in self._copies(a):
                send.wait_send()
                recv.wait_recv()
        for a in range(self.n):
            self._local(a).wait()


def _gathering(body, n_steps, n_s, core_major, start_at=0):
    def wrapped(*refs, n_in, n_out):
        ins, sh_in = refs[:n_in], refs[n_in:n_in + n_s]
        outs, sh_out = refs[n_in + n_s:n_in + n_s + n_out], refs[n_in + n_s + n_out:n_in + 2 * n_s + n_out]
        rest = refs[n_in + 2 * n_s + n_out:]
        gather = _Gather(sh_in, sh_out, *rest[len(rest) - 3:], core_major=core_major)
        i = pl.program_id(0)

        if start_at:
            @pl.when(i == 0)
            def _():
                gather.announce()

        @pl.when(i == start_at)
        def _():
            gather.start(announced=start_at > 0)

        @pl.when(i == n_steps // 2)
        def _():
            gather.relay()

        body(*ins, *outs, *rest[:len(rest) - 3])

        @pl.when(i == n_steps - 1)
        def _():
            gather.finish()

    return wrapped


def _norm_in(x, g_mix, shards, tm):
    T = x.shape[0]
    n_t = T // tm
    n_s = len(shards)

    def norm(x_ref, g_ref, h_ref):
        xv = x_ref[...]
        h_ref[...] = (xv * _rstd(xv) * g_ref[...]).astype(BF16)

    outs = pl.pallas_call(
        functools.partial(_gathering(norm, n_t, n_s, core_major=False), n_in=2, n_out=1), name="norm_in", grid=(n_t,),
        in_specs=[pl.BlockSpec((tm, D_MODEL), lambda i: (i, 0)), pl.BlockSpec((1, D_MODEL), lambda i: (0, 0))]
        + [ANY] * n_s,
        out_specs=[pl.BlockSpec((tm, D_MODEL), lambda i: (i, 0))] + [ANY] * n_s,
        out_shape=[_sds((T, D_MODEL), BF16)] + [_sds((N_DEV,) + a.shape, a.dtype) for a in shards],
        scratch_shapes=_Gather.scratch(n_s),
        compiler_params=pltpu.CompilerParams(dimension_semantics=("arbitrary",), collective_id=COLLECTIVE_IDS["norm_in"]),
    )(x, g_mix, *shards)
    return outs[0], list(outs[1:])


def _mix_in(h1, w_in_t, tm, shards):
    T = h1.shape[0]
    n_t = T // tm
    n_s = len(shards)

    def project(h_ref, w_ref, u_ref):
        u_ref[...] = lax.dot_general(h_ref[...], w_ref[...], NT, preferred_element_type=F32)

    outs = pl.pallas_call(
        functools.partial(_gathering(project, n_t, n_s, core_major=True, start_at=1), n_in=2, n_out=1), name="mix_in",
        grid=(n_t,),
        in_specs=[pl.BlockSpec((tm, D_MODEL), lambda i: (i, 0)), pl.BlockSpec((D_IN, D_MODEL), lambda i: (0, 0))]
        + [ANY] * n_s,
        out_specs=[pl.BlockSpec((tm, D_IN), lambda i: (i, 0))] + [ANY] * n_s,
        out_shape=[_sds((T, D_IN), F32)] + [_sds((N_DEV,) + a.shape, a.dtype) for a in shards],
        scratch_shapes=_Gather.scratch(n_s),
        compiler_params=pltpu.CompilerParams(dimension_semantics=("arbitrary",), collective_id=COLLECTIVE_IDS["mix_in"]),
    )(h1, w_in_t, *shards)
    return outs[0], list(outs[1:])


def _mixer_fwd(u, x, pv, wa, wx, wp, w_out_b, g_ffn, tm, shards=()):
    T = u.shape[0]
    n_s = len(shards)
    n_t = T // tm

    def body(u_ref, x_ref, pv_ref, wa_in, wx_in, wp_in, wo_ref, gf_ref, *rest):
        sh_in, rest = rest[:n_s], rest[n_s:]
        y_ref, hs_ref, hres_ref, h2_ref, saved_ref = rest[:5]
        sh_out, rest = rest[5:5 + n_s], rest[5 + n_s:]
        e_lru, e_pool, a_s, b_s, hc, wa_ref, wx_ref, wp_ref = rest[:8]
        gather = _Gather(sh_in, sh_out, *rest[8:], core_major=True) if n_s else None
        i = pl.program_id(0)

        @pl.when(i == 0)
        def _():
            if gather:
                gather.start()
            e_lru[pl.ds(0, HALO), :] = jnp.zeros((HALO, LRU_W), F32)
            e_pool[pl.ds(0, HALO), :] = jnp.zeros((HALO, POOL_W), F32)
            hc[...] = jnp.zeros((8, LRU_W), F32)
            _fill_block_diag(wa_ref, wa_in)
            _fill_block_diag(wx_ref, wx_in)
            _fill_block_diag(wp_ref, wp_in)

        if gather:
            @pl.when(i == (2 * n_t) // 3)
            def _():
                gather.relay()

        e_lru[pl.ds(HALO, tm), :] = u_ref[:, 0:LRU_W]
        e_pool[pl.ds(HALO, tm), :] = u_ref[:, 2 * LRU_W:D_IN]
        pv = pv_ref[...]
        xc, r, ig, la = _lru_gates(e_lru, pv, wa_ref, wx_ref, tm)
        for q, val in enumerate((xc, r, ig, la)):
            saved_ref[:, LRU_W * q:LRU_W * (q + 1)] = val
        a, _, mult, _ = _lru_decay(la)
        a_s[...] = a
        b_s[...] = mult * (ig * xc)
        hc[...] = _scan_tile(a_s, b_s, hs_ref, hc[...], tm, reverse=False)
        gl, _ = _gelu_parts(u_ref[:, LRU_W:2 * LRU_W])
        y_lru = hs_ref[...] * gl
        _, zp, _ = _pool_pre(e_pool, pv, wp_ref, tm, i * tm)
        y_pool = zp * pv[ROW_PS:ROW_PS + 1, :]
        yn = jnp.concatenate([y_lru * _rstd(y_lru) * pv[ROW_GL:ROW_GL + 1, :],
                              y_pool * _rstd(y_pool) * pv[ROW_GP:ROW_GP + 1, :]], axis=1).astype(BF16)
        for b in range(N_DEV):
            y_ref[:, 128 * _y_pos(b):128 * (_y_pos(b) + 1)] = yn[:, 128 * b:128 * (b + 1)]
        hr = x_ref[...] + jnp.dot(y_ref[...], wo_ref[...], preferred_element_type=F32)
        hres_ref[...] = hr
        h2_ref[...] = (hr * _rstd(hr) * gf_ref[...]).astype(BF16)
        e_lru[pl.ds(0, HALO), :] = e_lru[pl.ds(tm, HALO), :]
        e_pool[pl.ds(0, HALO), :] = e_pool[pl.ds(tm, HALO), :]

        if gather:
            @pl.when(i == n_t - 1)
            def _():
                gather.finish()

    full = lambda shape: pl.BlockSpec(shape, lambda i: (0,) * len(shape))
    row = lambda w: pl.BlockSpec((tm, w), lambda i: (i, 0))
    outs = pl.pallas_call(
        body, name="mixer_fwd", grid=(n_t,),
        in_specs=[row(D_IN), row(D_MODEL), full((16, LRU_W)), full((8, 64, 64)), full((8, 64, 64)), full((4, 128, 128)),
                  full((D_MODEL, D_MODEL)), full((1, D_MODEL))] + [ANY] * n_s,
        out_specs=[row(D_MODEL), row(LRU_W), row(D_MODEL), row(D_MODEL), row(4 * LRU_W)] + [ANY] * n_s,
        out_shape=[_sds((T, D_MODEL), BF16), _sds((T, LRU_W), F32), _sds((T, D_MODEL), F32), _sds((T, D_MODEL), BF16),
                   _sds((T, 4 * LRU_W), F32)] + [_sds((N_DEV,) + a.shape, a.dtype) for a in shards],
        scratch_shapes=[pltpu.VMEM((HALO + tm, LRU_W), F32), pltpu.VMEM((HALO + tm, POOL_W), F32),
                        pltpu.VMEM((tm, LRU_W), F32), pltpu.VMEM((tm, LRU_W), F32), pltpu.VMEM((8, LRU_W), F32)]
        + [pltpu.VMEM((2, 256, 256), BF16)] * 3 + (_Gather.scratch(n_s) if n_s else []),
        compiler_params=pltpu.CompilerParams(dimension_semantics=("arbitrary",),
                                             collective_id=COLLECTIVE_IDS["mixer_fwd"] if n_s else None),
    )(u, x, pv, wa, wx, wp, w_out_b, g_ffn, *shards)
    return outs[0], outs[1], outs[2], outs[3], outs[4], list(outs[5:])


def _ffn_fwd(hres, h2, w1_b, w3_b, w2_b, g_fin, tgt, tm):
    T = hres.shape[0]

    def body(hres_ref, h2_ref, w1_ref, w3_ref, w2_ref, gfin_ref, tgt_ref,
             g_ref, v_ref, ff_ref, d3_ref, d3b_ref, loss_ref, dgfin_ref):
        @pl.when(pl.program_id(0) == 0)
        def _():
            loss_ref[...] = jnp.zeros((8, 128), F32)
            dgfin_ref[...] = jnp.zeros((1, D_MODEL), F32)

        h2 = h2_ref[...]
        h3 = hres_ref[...]
        for lo, hi in FF_CHUNKS:
            g = lax.dot_general(h2, w1_ref[lo:hi, :], NT, preferred_element_type=F32)
            v = lax.dot_general(h2, w3_ref[lo:hi, :], NT, preferred_element_type=F32)
            g_ref[:, lo:hi] = g.astype(BF16)
            v_ref[:, lo:hi] = v.astype(BF16)
            ff = ((g * _sigmoid(g)) * v).astype(BF16)
            ff_ref[:, lo:hi] = ff
            h3 = h3 + jnp.dot(ff, w2_ref[lo:hi, :], preferred_element_type=F32)

        rstd = _rstd(h3)
        xh = h3 * rstd
        gfin = gfin_ref[...]
        err = xh * gfin - tgt_ref[...]
        loss_ref[...] += 0.5 * jnp.sum(jnp.mean(err * err, axis=-1, keepdims=True))
        dx, dgain = _rms_bwd(err * (1.0 / D_MODEL), xh, rstd, gfin)
        d3_ref[...] = dx
        d3b_ref[...] = dx.astype(BF16)
        dgfin_ref[...] += dgain

    row = lambda w: pl.BlockSpec((tm, w), lambda i: (i, 0))
    const = lambda shape: pl.BlockSpec(shape, lambda i: (0,) * len(shape))
    weight = pl.BlockSpec((D_FF, D_MODEL), lambda i: (0, 0), pipeline_mode=pl.Buffered(1))
    return pl.pallas_call(
        body, name="ffn_fwd", grid=(T // tm,),
        in_specs=[row(D_MODEL), row(D_MODEL), weight, weight, weight, const((1, D_MODEL)), row(D_MODEL)],
        out_specs=[row(D_FF), row(D_FF), row(D_FF), row(D_MODEL), row(D_MODEL), const((8, 128)), const((1, D_MODEL))],
        out_shape=[_sds((T, D_FF), BF16), _sds((T, D_FF), BF16), _sds((T, D_FF), BF16),
                   _sds((T, D_MODEL), F32), _sds((T, D_MODEL), BF16), _sds((8, 128), F32), _sds((1, D_MODEL), F32)],
        compiler_params=pltpu.CompilerParams(dimension_semantics=("arbitrary",)),
    )(hres, h2, w1_b, w3_b, w2_b, g_fin, tgt)


def _ffn_bwd(d3, g, v, w1_b, w3_b, w2_b, hres, g_ffn, tm):
    T = d3.shape[0]

    def body(d3_ref, g_ref, v_ref, w1_ref, w3_ref, w2_ref, hres_ref, gf_ref, dg_ref, dv_ref, d2_ref, dgffn_ref):
        @pl.when(pl.program_id(0) == 0)
        def _():
            dgffn_ref[...] = jnp.zeros((1, D_MODEL), F32)

        d3 = d3_ref[...]
        d3b = d3.astype(BF16)
        dh2 = jnp.zeros((tm, D_MODEL), F32)
        for lo, hi in FF_CHUNKS:
            dff = lax.dot_general(d3b, w2_ref[lo:hi, :], NT, preferred_element_type=F32)
            gv = g_ref[:, lo:hi].astype(F32)
            vv = v_ref[:, lo:hi].astype(F32)
            sg = _sigmoid(gv)
            sl = gv * sg
            dgb = (dff * vv * (sg * (1.0 + gv * (1.0 - sg)))).astype(BF16)
            dvb = (dff * sl).astype(BF16)
            dg_ref[:, lo:hi] = dgb
            dv_ref[:, lo:hi] = dvb
            dh2 = dh2 + (jnp.dot(dgb, w1_ref[lo:hi, :], preferred_element_type=F32)
                         + jnp.dot(dvb, w3_ref[lo:hi, :], preferred_element_type=F32))

        hr = hres_ref[...]
        rstd = _rstd(hr)
        dx, dgain = _rms_bwd(dh2, hr * rstd, rstd, gf_ref[...])
        d2_ref[...] = d3 + dx
        dgffn_ref[...] += dgain

    row = lambda w: pl.BlockSpec((tm, w), lambda i: (i, 0))
    const = lambda shape: pl.BlockSpec(shape, lambda i: (0,) * len(shape))
    weight = pl.BlockSpec((D_FF, D_MODEL), lambda i: (0, 0), pipeline_mode=pl.Buffered(1))
    return pl.pallas_call(
        body, name="ffn_bwd", grid=(T // tm,),
        in_specs=[row(D_MODEL), row(D_FF), row(D_FF), weight, weight, weight, row(D_MODEL), const((1, D_MODEL))],
        out_specs=[row(D_FF), row(D_FF), row(D_MODEL), const((1, D_MODEL))],
        out_shape=[_sds((T, D_FF), BF16), _sds((T, D_FF), BF16), _sds((T, D_MODEL), F32), _sds((1, D_MODEL), F32)],
        compiler_params=pltpu.CompilerParams(dimension_semantics=("arbitrary",)),
    )(d3, g, v, w1_b, w3_b, w2_b, hres, g_ffn)


def _at_b_pair(a, b, c_arr, name, tk, gather=()):
    T, M = a.shape
    N = b.shape[1]
    hm, n_k = M // 2, T // tk
    n_g = len(gather)

    def body(c_ref, a_ref, b_ref, *rest):
        g_in, o_ref, rest = rest[:n_g], rest[n_g], rest[n_g + 1:]
        g_out, rest = rest[:n_g], rest[n_g:]
        acc, landed, send_sem, recv_sem = rest[:4]
        ag = _Gather(g_in, g_out, *rest[4:]) if n_g else None
        ph, k = pl.program_id(0), pl.program_id(1)

        def hand_over():
            x, y, c, _ = _place()
            return pltpu.make_async_remote_copy(
                src_ref=acc.at[0], dst_ref=landed, send_sem=send_sem, recv_sem=recv_sem,
                device_id=(x, y, 1 - c), device_id_type=MESH)

        if ag:
            @pl.when((ph == 0) & (k == 0))
            def _():
                ag.start()

            @pl.when((ph == 1) & (k == 0))
            def _():
                ag.relay()
        else:
            barrier = pltpu.get_barrier_semaphore()

            @pl.when((ph == 0) & (k == 0))
            def _():
                x, y, c, _ = _place()
                pl.semaphore_signal(barrier, inc=1, device_id=(x, y, 1 - c), device_id_type=MESH)

        @pl.when(k == 0)
        def _():
            acc[ph] = jnp.zeros((hm, N), F32)

        acc[ph] += lax.dot_general(a_ref[...].astype(BF16), b_ref[...].astype(BF16), TN, preferred_element_type=F32)

        @pl.when((ph == 0) & (k == n_k - 1))
        def _():
            if not ag:
                pl.semaphore_wait(barrier, 1)
            hand_over().start()

        @pl.when((ph == 1) & (k == n_k - 1))
        def _():
            copy = hand_over()
            copy.wait_recv()
            o_ref[...] = (acc[1] + landed[...]).astype(BF16)
            copy.wait_send()
            if ag:
                ag.finish()

    outs = pl.pallas_call(
        body, name=name,
        grid_spec=pltpu.PrefetchScalarGridSpec(
            num_scalar_prefetch=1, grid=(2, n_k),
            in_specs=[pl.BlockSpec((tk, hm), lambda ph, k, c_ref: (k, (ph + 1 - c_ref[0]) % 2)),
                      pl.BlockSpec((tk, N), lambda ph, k, c_ref: (k, 0))] + [ANY] * n_g,
            out_specs=[pl.BlockSpec((hm, N), lambda ph, k, c_ref: (0, 0))] + [ANY] * n_g,
            scratch_shapes=[pltpu.VMEM((2, hm, N), F32), pltpu.VMEM((hm, N), F32),
                            pltpu.SemaphoreType.DMA, pltpu.SemaphoreType.DMA] + (_Gather.scratch(n_g) if n_g else [])),
        out_shape=[_sds((hm, N), BF16)] + [_sds((N_DEV,) + g.shape, g.dtype) for g in gather],
        compiler_params=pltpu.CompilerParams(dimension_semantics=("arbitrary", "arbitrary"),
                                             collective_id=COLLECTIVE_IDS[name]),
    )(c_arr, a, b, *gather)
    return (outs[0], list(outs[1:])) if n_g else outs[0]


def _mixer_bwd(d2, u, hs, saved, pv, wa, wx, wp, w_out_b, tm, chip_sums=()):
    T = u.shape[0]
    n_t = T // tm
    n_x = len(chip_sums)

    def body(d2_ref, u_ref, uh_ref, hs_ref, hh_ref, saved_ref, pv_ref, wa_in, wx_in, wp_in, wo_ref, *rest):
        x_in, rest = rest[:n_x], rest[n_x:]
        du_ref, sg_ref = rest[:2]
        x_out, rest = rest[2:2 + n_x], rest[2 + n_x:]
        e_pool, e_h, a_s, b_s, dh_s, mu_s, f_x, f_p, mc, cx, cp = rest[:11]
        wa_ref, wx_ref, wp_ref, vacc_ref, dwa_ref, dwx_ref, dwp_ref = rest[11:18]
        exchange = _ChipExchange(x_in, x_out, *rest[18:]) if n_x else None
        s = pl.program_id(0)
        it = n_t - 1 - s

        @pl.when(s == 0)
        def _():
            if exchange:
                exchange.announce()
            mc[...] = jnp.zeros((8, LRU_W), F32)
            cx[...] = jnp.zeros((8, LRU_W), F32)
            cp[...] = jnp.zeros((HALO, POOL_W), F32)
            vacc_ref[...] = jnp.zeros((16, LRU_W), F32)
            dwa_ref[...] = jnp.zeros((2, 256, 256), F32)
            dwx_ref[...] = jnp.zeros((2, 256, 256), F32)
            dwp_ref[...] = jnp.zeros((2, 256, 256), F32)
            _fill_block_diag(wa_ref, wa_in)
            _fill_block_diag(wx_ref, wx_in)
            _fill_block_diag(wp_ref, wp_in)

        if exchange:
            @pl.when(s == 1)
            def _():
                exchange.start()

        first = it == 0
        e_pool[pl.ds(0, HALO), :] = jnp.where(first, 0.0, uh_ref[...])
        e_pool[pl.ds(HALO, tm), :] = u_ref[:, 2 * LRU_W:D_IN]
        e_h[pl.ds(0, 8), :] = jnp.where(first, 0.0, hh_ref[...])
        e_h[pl.ds(8, tm), :] = hs_ref[...]
        pv = pv_ref[...]
        saved = lambda q: saved_ref[:, LRU_W * q:LRU_W * (q + 1)]

        dyn = lax.dot_general(d2_ref[...].astype(BF16), wo_ref[...], NT, preferred_element_type=F32)
        dyn = jnp.concatenate([dyn[:, 128 * _y_pos(b):128 * (_y_pos(b) + 1)] for b in range(N_DEV)], axis=1)

        h = hs_ref[...]
        ug = u_ref[:, LRU_W:2 * LRU_W]
        gl, dgl = _gelu_parts(ug)
        y_lru = h * gl
        rstd_l = _rstd(y_lru)
        dy_lru, d_gain_l = _rms_bwd(dyn[:, 0:LRU_W], y_lru * rstd_l, rstd_l, pv[ROW_GL:ROW_GL + 1, :])
        dh = dy_lru * gl
        du_ref[:, LRU_W:2 * LRU_W] = (dy_lru * h * dgl).astype(BF16)
        a_s[...] = jnp.exp(saved(3))
        b_s[...] = a_s[...] * dh
        dh_s[...] = dh
        mu_s[pl.ds(tm, 8), :] = mc[...]
        mc[...] = _scan_tile(a_s, b_s, mu_s, mc[...], tm, reverse=True)
        xc, r, ig = saved(0), saved(1), saved(2)
        a, om, mult, rmult = _lru_decay(saved(3))
        lam_t = dh_s[...] + mu_s[pl.ds(1, tm), :]
        da = lam_t * e_h[pl.ds(7, tm), :]
        dmult = lam_t * (ig * xc)
        di = lam_t * (mult * xc)
        dxc = lam_t * (mult * ig)
        dla = da * a - jnp.where(om > 1e-12, dmult * ((a * a) * rmult), 0.0)
        dra = (dla * (-LRU_C * _softplus_neg_lambda(pv))) * (r * (1.0 - r))
        dia = di * (ig * (1.0 - ig))
        drab = dra.astype(BF16)
        diab = dia.astype(BF16)
        xcb = xc.astype(BF16)
        dxc = dxc + _bd_t(drab, wa_ref) + _bd_t(diab, wx_ref)
        dwa_ref[...] += _bd_grad(xcb, drab)
        dwx_ref[...] += _bd_grad(xcb, diab)
        sig_neg_lam = _sigmoid(-pv[ROW_LAM:ROW_LAM + 1, :])
        d_lam = jnp.sum(dla * r, axis=0, keepdims=True) * (LRU_C * sig_neg_lam)

        f_x[pl.ds(0, tm), :] = dxc
        f_x[pl.ds(tm, 8), :] = cx[...]
        du_lru = jnp.zeros((tm, LRU_W), F32)
        u_lru = u_ref[:, 0:LRU_W]
        d_cw = []
        for k in range(4):
            later = f_x[pl.ds(3 - k, tm), :]
            du_lru = du_lru + later * pv[ROW_CW + k:ROW_CW + k + 1, :]
            d_cw.append(jnp.sum(later * u_lru, axis=0, keepdims=True))
        du_ref[:, 0:LRU_W] = du_lru.astype(BF16)
        cx[...] = f_x[pl.ds(0, 8), :]

        pooled_b, zp, inv_cnts = _pool_pre(e_pool, pv, wp_ref, tm, it * tm)
        ps = pv[ROW_PS:ROW_PS + 1, :]
        y_pool = zp * ps
        rstd_p = _rstd(y_pool)
        dy_pool, d_gain_p = _rms_bwd(dyn[:, LRU_W:D_MODEL], y_pool * rstd_p, rstd_p, pv[ROW_GP:ROW_GP + 1, :])
        dz = dy_pool * ps
        dzb = dz.astype(BF16)
        dwp_ref[...] += _bd_grad(pooled_b, dzb)
        dpooled = _bd_t(dzb, wp_ref)
        for g, w in enumerate(POOL_WINDOWS):
            f_p[pl.ds(0, tm), pl.ds(128 * g, 128)] = _over_count(dpooled[:, 128 * g:128 * (g + 1)], w, inv_cnts[g])
        f_p[pl.ds(tm, HALO), :] = cp[...]
        for g, w in enumerate(POOL_WINDOWS):
            acc = _window_sum(f_p[:, pl.ds(128 * g, 128)], w, back=False)[0:tm, :]
            du_ref[:, 2 * LRU_W + 128 * g:2 * LRU_W + 128 * (g + 1)] = (
                acc - dpooled[:, 128 * g:128 * (g + 1)]).astype(BF16)
        cp[...] = f_p[pl.ds(0, HALO), :]

        rows = d_cw + [
            jnp.sum(dxc, axis=0, keepdims=True),
            jnp.sum(dra, axis=0, keepdims=True),
            jnp.sum(dia, axis=0, keepdims=True),
            d_lam,
            jnp.sum(dz, axis=0, keepdims=True),
            jnp.sum(dy_pool * zp, axis=0, keepdims=True),
            d_gain_l, d_gain_p,
            jnp.zeros((4, LRU_W), F32),
        ]
        vacc_ref[...] += jnp.concatenate(rows, axis=0)

        @pl.when(s == n_t - 1)
        def _():
            sg_ref[SG_VEC:SG_VEC + 16, :] = vacc_ref[:, 0:256]
            sg_ref[SG_VEC + 16:SG_VEC + 32, :] = vacc_ref[:, 256:512]
            for half in range(2):
                sg_ref[SG_WA + 64 * half:SG_WA + 64 * (half + 1), :] = _diag_pack(dwa_ref[half], 64)
                sg_ref[SG_WX + 64 * half:SG_WX + 64 * (half + 1), :] = _diag_pack(dwx_ref[half], 64)
                sg_ref[SG_WP + 128 * half:SG_WP + 128 * (half + 1), :] = _diag_pack(dwp_ref[half], 128)
            if exchange:
                exchange.finish()

    rev = lambda w: pl.BlockSpec((tm, w), lambda s: (n_t - 1 - s, 0))
    full = lambda shape: pl.BlockSpec(shape, lambda s: (0,) * len(shape))
    outs = pl.pallas_call(
        body, name="mixer_bwd", grid=(n_t,),
        in_specs=[rev(D_MODEL), rev(D_IN),
                  pl.BlockSpec((HALO, POOL_W), lambda s: (jnp.maximum((n_t - 1 - s) * (tm // HALO) - 1, 0), 2)),
                  rev(LRU_W),
                  pl.BlockSpec((8, LRU_W), lambda s: (jnp.maximum((n_t - 1 - s) * (tm // 8) - 1, 0), 0)),
                  rev(4 * LRU_W), full((16, LRU_W)), full((8, 64, 64)), full((8, 64, 64)), full((4, 128, 128)),
                  full((D_MODEL, D_MODEL))] + [ANY] * n_x,
        out_specs=[rev(D_IN), full((SG_ROWS, 256))] + [ANY] * n_x,
        out_shape=[_sds((T, D_IN), BF16), _sds((SG_ROWS, 256), F32)] + [_sds(a.shape, a.dtype) for a in chip_sums],
        scratch_shapes=[pltpu.VMEM((HALO + tm, POOL_W), F32),
                        pltpu.VMEM((8 + tm, LRU_W), F32)] + [pltpu.VMEM((tm, LRU_W), F32)] * 3 + [
                        pltpu.VMEM((tm + 8, LRU_W), F32), pltpu.VMEM((tm + 8, LRU_W), F32),
                        pltpu.VMEM((tm + HALO, POOL_W), F32), pltpu.VMEM((8, LRU_W), F32),
                        pltpu.VMEM((8, LRU_W), F32), pltpu.VMEM((HALO, POOL_W), F32)]
        + [pltpu.VMEM((2, 256, 256), BF16)] * 3 + [pltpu.VMEM((16, LRU_W), F32)] + [pltpu.VMEM((2, 256, 256), F32)] * 3
        + (_ChipExchange.scratch(n_x) if n_x else []),
        compiler_params=pltpu.CompilerParams(dimension_semantics=("arbitrary",),
                                             collective_id=COLLECTIVE_IDS["mixer_bwd"] if n_x else None),
    )(d2, u, u, hs, hs, saved, pv, wa, wx, wp, w_out_b, *chip_sums)
    return outs[0], outs[1], list(outs[2:])


def _mix_in_bwd(du, x, d2, w_in_t, g_mix, tm):
    T = x.shape[0]

    def body(du_ref, x_ref, d2_ref, w_ref, g_ref, dx_ref, dg_ref):
        @pl.when(pl.program_id(0) == 0)
        def _():
            dg_ref[...] = jnp.zeros((1, D_MODEL), F32)

        dh = jnp.dot(du_ref[...], w_ref[...], preferred_element_type=F32)
        xv = x_ref[...]
        rstd = _rstd(xv)
        dx, dgain = _rms_bwd(dh, xv * rstd, rstd, g_ref[...])
        dx_ref[...] = d2_ref[...] + dx
        dg_ref[...] += dgain

    row = lambda w: pl.BlockSpec((tm, w), lambda i: (i, 0))
    const = lambda shape: pl.BlockSpec(shape, lambda i: (0,) * len(shape))
    return pl.pallas_call(
        body, name="mix_in_bwd", grid=(T // tm,),
        in_specs=[row(D_IN), row(D_MODEL), row(D_MODEL), const((D_IN, D_MODEL)), const((1, D_MODEL))],
        out_specs=[row(D_MODEL), const((1, D_MODEL))],
        out_shape=[_sds((T, D_MODEL), F32), _sds((1, D_MODEL), F32)],
        compiler_params=pltpu.CompilerParams(dimension_semantics=("arbitrary",)),
    )(du, x, d2, w_in_t, g_mix)


def _adamw(w, g, m, v):
    m = ADAM_B1 * m + (1.0 - ADAM_B1) * g
    v = ADAM_B2 * v + (1.0 - ADAM_B2) * (g * g)
    m_hat = m / (1.0 - ADAM_B1 ** ADAM_STEP)
    v_hat = v / (1.0 - ADAM_B2 ** ADAM_STEP)
    delta = -ADAM_LR * (m_hat / (jnp.sqrt(v_hat) + ADAM_EPS) + ADAM_WD * w)
    return delta, m, v


def _adam_shards(ws, ms, vs, parts):
    n = len(ws)
    n_blk = [w.shape[0] // ADAM_ROWS for w in ws]

    def body(*refs):
        w_refs, m_refs, v_refs, p_refs, outs = (refs[:n], refs[n:2 * n], refs[2 * n:3 * n], refs[3 * n:4 * n],
                                                refs[4 * n:])
        i = pl.program_id(0)
        for a in range(n):
            @pl.when(i < n_blk[a])
            def _(a=a):
                g = p_refs[a][0].astype(F32)
                for j in range(1, 4):
                    g = g + p_refs[a][j].astype(F32)
                delta, new_m, new_v = _adamw(w_refs[a][...], g, m_refs[a][...], v_refs[a][...])
                for kind, val in enumerate((g, delta, new_m, new_v)):
                    outs[4 * a + kind][...] = val

    blk = lambda a: pl.BlockSpec((ADAM_ROWS, D_MODEL), lambda i: (jnp.minimum(i, n_blk[a] - 1), 0))
    part_blk = lambda a: pl.BlockSpec((4, ADAM_ROWS, D_MODEL), lambda i: (0, jnp.minimum(i, n_blk[a] - 1), 0))
    res = pl.pallas_call(
        body, name="adam_shards", grid=(max(n_blk),),
        in_specs=[blk(a) for a in range(n)] * 3 + [part_blk(a) for a in range(n)],
        out_specs=[blk(a) for a in range(n) for _ in range(4)],
        out_shape=[_sds(w.shape, F32) for w in ws for _ in range(4)],
        compiler_params=pltpu.CompilerParams(dimension_semantics=("arbitrary",)),
    )(*ws, *ms, *vs, *parts)
    return [tuple(res[4 * a:4 * a + 4]) for a in range(n)]


SMALL_PARAMS = [("norm_mix_g", (1, D_MODEL)), ("conv_w", (1, 4, 64)), ("conv_b", (1, LRU_W)),
                ("gate_a_w", (1, 8, 64, 64)), ("gate_a_b", (1, LRU_W)), ("gate_x_w", (1, 8, 64, 64)),
                ("gate_x_b", (1, LRU_W)), ("lru_lambda", (1, LRU_W)), ("pool_w", (1, 4, 128, 128)),
                ("pool_b", (1, POOL_W)), ("pool_scale", (1, POOL_W)), ("norm_lru_g", (1, LRU_W)),
                ("norm_pool_g", (1, POOL_W)), ("norm_ffn_g", (1, D_MODEL)), ("final_norm_g", (1, D_MODEL))]
VEC_ROW = dict(conv_b=ROW_CB, gate_a_b=ROW_BA, gate_x_b=ROW_BX, lru_lambda=ROW_LAM, pool_b=ROW_PB, pool_scale=ROW_PS,
               norm_lru_g=ROW_GL, norm_pool_g=ROW_GP)
WHOLE = (Ellipsis,)


def _unpack_mixer_grads(sg, dev):
    vec = jnp.concatenate([sg[SG_VEC:SG_VEC + 16], sg[SG_VEC + 16:SG_VEC + 32]], axis=1)
    out = {nm: [(WHOLE, vec[r:r + 1])] for nm, r in VEC_ROW.items()}
    own = jnp.zeros((4, 64), F32)
    for d in range(N_DEV):
        own = jnp.where(dev == d, vec[ROW_CW:ROW_CW + 4, 64 * d:64 * (d + 1)], own)
    out["conv_w"] = [((0,), own)]
    for nm, row0 in (("gate_a_w", SG_WA), ("gate_x_w", SG_WX)):
        out[nm] = [((0, b), sg[row0 + 64 * (b // 4):row0 + 64 * (b // 4 + 1), 64 * (b % 4):64 * (b % 4 + 1)])
                   for b in range(8)]
    out["pool_w"] = [((0, b), sg[SG_WP + 128 * (b // 2):SG_WP + 128 * (b // 2 + 1), 128 * (b % 2):128 * (b % 2 + 1)])
                     for b in range(4)]
    return out


def _adam_small(parts, w, m, v):
    names = [nm for nm, _ in SMALL_PARAMS]
    n = len(names)

    def body(sg_ref, gm_ref, gf_ref, gn_ref, ls_ref, *rest):
        w_refs, m_refs, v_refs, outs = rest[:n], rest[n:2 * n], rest[2 * n:3 * n], rest[3 * n:]
        dev = 4 * lax.axis_index("x") + 2 * lax.axis_index("y") + lax.axis_index("c")

        def total(ref):
            acc = ref[0]
            for d in range(1, N_DEV):
                acc = acc + ref[d]
            return acc

        pieces = _unpack_mixer_grads(total(sg_ref), dev)
        pieces["norm_mix_g"] = [(WHOLE, total(gm_ref))]
        pieces["norm_ffn_g"] = [(WHOLE, total(gf_ref))]
        pieces["final_norm_g"] = [(WHOLE, total(gn_ref))]
        for i, nm in enumerate(names):
            for idx, g in pieces[nm]:
                delta, new_m, new_v = _adamw(w_refs[i][idx], g, m_refs[i][idx], v_refs[i][idx])
                for kind, val in enumerate((g, delta, new_m, new_v)):
                    outs[4 * i + kind][idx] = val
        outs[4 * n][...] = total(ls_ref)

    shapes = [_sds(shape, F32) for _, shape in SMALL_PARAMS for _ in range(4)] + [_sds((8, 128), F32)]
    res = pl.pallas_call(body, name="adam_small", out_shape=shapes)(
        *parts, *[w[nm] for nm in names], *[m[nm] for nm in names], *[v[nm] for nm in names])
    return {nm: tuple(res[4 * i:4 * i + 4]) for i, nm in enumerate(names)}, res[4 * n][0, 0]


def _vec_rows(conv_w_full, conv_b, ba, bx, lam, pb, ps, gl, gp):
    return jnp.concatenate([conv_w_full, conv_b, ba, bx, lam, pb, ps, gl, gp, jnp.zeros((4, LRU_W), F32)], axis=0)


WEIGHT_ORDER = ['norm_mix_g', 'w_in', 'conv_w', 'conv_b', 'gate_a_w', 'gate_a_b', 'gate_x_w', 'gate_x_b', 'lru_lambda',
                'pool_w', 'pool_b', 'pool_scale', 'norm_lru_g', 'norm_pool_g', 'w_out', 'norm_ffn_g', 'ffn_w1', 'ffn_w3',
                'ffn_w2', 'final_norm_g']


def kernel(x, norm_mix_g, w_in, conv_w, conv_b, gate_a_w, gate_a_b, gate_x_w, gate_x_b, lru_lambda, pool_w, pool_b, pool_scale, norm_lru_g, norm_pool_g, w_out, norm_ffn_g, ffn_w1, ffn_w3, ffn_w2, final_norm_g, loss_target, m_norm_mix_g, m_w_in, m_conv_w, m_conv_b, m_gate_a_w, m_gate_a_b, m_gate_x_w, m_gate_x_b, m_lru_lambda, m_pool_w, m_pool_b, m_pool_scale, m_norm_lru_g, m_norm_pool_g, m_w_out, m_norm_ffn_g, m_ffn_w1, m_ffn_w3, m_ffn_w2, m_final_norm_g, v_norm_mix_g, v_w_in, v_conv_w, v_conv_b, v_gate_a_w, v_gate_a_b, v_gate_x_w, v_gate_x_b, v_lru_lambda, v_pool_w, v_pool_b, v_pool_scale, v_norm_lru_g, v_norm_pool_g, v_w_out, v_norm_ffn_g, v_ffn_w1, v_ffn_w3, v_ffn_w2, v_final_norm_g):
    ac = lax.axis_index("c")
    tm, tmx, tm_in, tk = 512, 512, 1024, 2048
    xs, tgt = x[0], loss_target[0]
    g_fin = final_norm_g.reshape(1, D_MODEL)
    c_arr = jnp.reshape(ac, (1,)).astype(jnp.int32)

    tr = lambda w: jnp.swapaxes(w[0], 0, 1)
    own = lambda w: w[0]
    bf = lambda a: a.astype(BF16)

    h1, (g_in, g_conv) = _norm_in(xs, norm_mix_g, [bf(tr(w_in)), conv_w[0]], tm_in)
    w_in_t = g_in.reshape(D_IN, D_MODEL)
    u, (g_out,) = _mix_in(h1, w_in_t, tm_in, shards=[bf(own(w_out))])
    conv_w_full = g_conv.transpose(1, 0, 2).reshape(4, LRU_W)
    pv = _vec_rows(conv_w_full, conv_b, gate_a_b, gate_x_b, lru_lambda, pool_b, pool_scale, norm_lru_g, norm_pool_g)
    wa, wx, wp = gate_a_w[0], gate_x_w[0], pool_w[0]
    w_out_b = g_out.reshape(D_MODEL, D_MODEL)
    y, hs, hres, h2, saved, (g_w1, g_w3, g_w2) = _mixer_fwd(
        u, xs, pv, wa, wx, wp, w_out_b, norm_ffn_g, tmx, shards=[bf(tr(ffn_w1)), bf(tr(ffn_w3)), bf(own(ffn_w2))])
    w1_t, w3_t, w2_b = g_w1.reshape(D_FF, D_MODEL), g_w3.reshape(D_FF, D_MODEL), g_w2.reshape(D_FF, D_MODEL)
    g, v, ff, d3, d3b, loss_acc, d_gfin = _ffn_fwd(hres, h2, w1_t, w3_t, w2_b, g_fin, tgt, tm)

    dg, dv, d2, d_gffn = _ffn_bwd(d3, g, v, w1_t, w3_t, w2_b, hres, norm_ffn_g, tm)
    chips = lambda a: a.reshape(4, a.shape[0] // 4, a.shape[1])
    early = [(y, d2, "grad_w_out"), (dg, h2, "grad_w1"), (dv, h2, "grad_w3"), (ff, d3b, "grad_w2")]
    early_sums = [chips(_at_b_pair(a, b, c_arr, name, tk)) for a, b, name in early]
    du, d_mixer, early_parts = _mixer_bwd(d2, u, hs, saved, pv, wa, wx, wp, w_out_b, tmx, chip_sums=early_sums)
    grad_x, d_gmix = _mix_in_bwd(du, xs, d2, w_in_t, norm_mix_g, tm_in)
    d_win, small_parts = _at_b_pair(du, h1, c_arr, "grad_w_in", tk,
                                    gather=[d_mixer, d_gmix, d_gffn, d_gfin, loss_acc])
    parts = [_half_exchange(chips(d_win), "grads_to_chips_w_in")] + list(early_parts)

    res = {}
    shard_w = dict(w_in=(w_in, m_w_in, v_w_in, tr), w_out=(w_out, m_w_out, v_w_out, own),
                   ffn_w1=(ffn_w1, m_ffn_w1, v_ffn_w1, tr), ffn_w3=(ffn_w3, m_ffn_w3, v_ffn_w3, tr),
                   ffn_w2=(ffn_w2, m_ffn_w2, v_ffn_w2, own))
    shard_res = _adam_shards([view(w) for w, _, _, view in shard_w.values()],
                             [view(m) for _, m, _, view in shard_w.values()],
                             [view(v) for _, _, v, view in shard_w.values()], parts)
    for (nm, (_, _, _, view)), outs in zip(shard_w.items(), shard_res):
        res[nm] = [(jnp.swapaxes(o, 0, 1) if view is tr else o)[None] for o in outs]

    row = lambda a: a.reshape(1, D_MODEL)
    small = lambda gm, cw, cb, wa_, ba, wx_, bx, lam, pw, pb, ps, gl, gp, gf, gn: dict(
        norm_mix_g=gm, conv_w=cw, conv_b=cb, gate_a_w=wa_, gate_a_b=ba, gate_x_w=wx_, gate_x_b=bx, lru_lambda=lam,
        pool_w=pw, pool_b=pb, pool_scale=ps, norm_lru_g=gl, norm_pool_g=gp, norm_ffn_g=gf, final_norm_g=row(gn))
    small_res, loss = _adam_small(
        small_parts,
        small(norm_mix_g, conv_w, conv_b, gate_a_w, gate_a_b, gate_x_w, gate_x_b, lru_lambda, pool_w, pool_b,
              pool_scale, norm_lru_g, norm_pool_g, norm_ffn_g, final_norm_g),
        small(m_norm_mix_g, m_conv_w, m_conv_b, m_gate_a_w, m_gate_a_b, m_gate_x_w, m_gate_x_b, m_lru_lambda, m_pool_w,
              m_pool_b, m_pool_scale, m_norm_lru_g, m_norm_pool_g, m_norm_ffn_g, m_final_norm_g),
        small(v_norm_mix_g, v_conv_w, v_conv_b, v_gate_a_w, v_gate_a_b, v_gate_x_w, v_gate_x_b, v_lru_lambda, v_pool_w,
              v_pool_b, v_pool_scale, v_norm_lru_g, v_norm_pool_g, v_norm_ffn_g, v_final_norm_g))
    for nm, outs in small_res.items():
        res[nm] = [o.reshape(D_MODEL) for o in outs] if nm == "final_norm_g" else list(outs)

    out = [loss, grad_x[None]]
    for kind in range(4):
        out += [res[nm][kind] for nm in WEIGHT_ORDER]
    return tuple(out)
```

```python
import functools

import jax
import jax.numpy as jnp
from jax import lax
from jax.experimental import pallas as pl
from jax.experimental.pallas import tpu as pltpu

F32 = jnp.float32
BF16 = jnp.bfloat16

D_MODEL = 1024
LRU_W = 512
POOL_W = 512
D_IN = 1536
D_FF = 2816
POOL_WINDOWS = (2, 4, 8, 16)
EPS = 1e-6
LRU_C = 8.0
N_DEV = 8
HALO = 16
SCAN_UNROLL = 8
ADAM_ROWS = 32
FF_CHUNKS = ((0, 1536), (1536, 2816))
RELAY_SPLIT_ROWS = 32
COLLECTIVE_IDS = {name: j for j, name in enumerate(
    ("grad_w_out", "grad_w1", "grad_w3", "grad_w2", "norm_in", "mix_in", "mixer_fwd", "mixer_bwd", "grad_w_in",
     "grads_to_chips_w_in"))}

ADAM_LR = 0.001
ADAM_B1 = 0.9
ADAM_B2 = 0.999
ADAM_EPS = 1e-08
ADAM_WD = 0.01
ADAM_STEP = 10

ROW_CW, ROW_CB, ROW_BA, ROW_BX, ROW_LAM, ROW_PB, ROW_PS, ROW_GL, ROW_GP = 0, 4, 5, 6, 7, 8, 9, 10, 11
SG_VEC, SG_WA, SG_WX, SG_WP, SG_ROWS = 0, 32, 160, 288, 544

NT = (((1,), (1,)), ((), ()))
TN = (((0,), (0,)), ((), ()))


def _sds(shape, dtype):
    return jax.ShapeDtypeStruct(shape, dtype)


def _sigmoid(x):
    return 0.5 * jnp.tanh(0.5 * x) + 0.5


def _gelu_parts(x):
    c = 0.7978845608028654
    inner = c * (x + 0.044715 * (x * x * x))
    th = jnp.tanh(inner)
    g = 0.5 * x * (1.0 + th)
    dg = 0.5 * (1.0 + th) + 0.5 * x * (1.0 - th * th) * (c * (1.0 + 3.0 * 0.044715 * (x * x)))
    return g, dg


def _window_sum(ext, w, back):
    n = ext.shape[0]
    s, k = ext, 1
    while k < w:
        s = s + pltpu.roll(s, k if back else n - k, 0)
        k *= 2
    return s


def _rstd(x):
    return lax.rsqrt(jnp.mean(x * x, axis=-1, keepdims=True) + EPS)


def _rms_bwd(dy, xhat, rstd, gain):
    dxh = dy * gain
    dx = rstd * (dxh - xhat * jnp.mean(dxh * xhat, axis=-1, keepdims=True))
    return dx, jnp.sum(dy * xhat, axis=0, keepdims=True)


def _bd(xb, w_ref):
    return jnp.concatenate(
        [jnp.dot(xb[:, :256], w_ref[0], preferred_element_type=F32),
         jnp.dot(xb[:, 256:], w_ref[1], preferred_element_type=F32)], axis=1)


def _bd_t(xb, w_ref):
    return jnp.concatenate(
        [lax.dot_general(xb[:, :256], w_ref[0], NT, preferred_element_type=F32),
         lax.dot_general(xb[:, 256:], w_ref[1], NT, preferred_element_type=F32)], axis=1)


def _bd_grad(xb, db):
    return jnp.stack(
        [lax.dot_general(xb[:, :256], db[:, :256], TN, preferred_element_type=F32),
         lax.dot_general(xb[:, 256:], db[:, 256:], TN, preferred_element_type=F32)], axis=0)


def _fill_block_diag(dst, src_ref):
    n, k, _ = src_ref.shape
    dst[...] = jnp.zeros(dst.shape, BF16)
    for b in range(n):
        p, q = divmod(b, 256 // k)
        dst[p, q * k:(q + 1) * k, q * k:(q + 1) * k] = src_ref[b].astype(BF16)


def _diag_pack(w, k):
    lane = lax.broadcasted_iota(jnp.int32, (k, 256), 1)
    out = w[0:k]
    for q in range(1, 256 // k):
        out = jnp.where(lane >= q * k, w[q * k:(q + 1) * k], out)
    return out


def _y_pos(b):
    return 4 * (b % 2) + b // 2


def _softplus_neg_lambda(pv):
    z = -pv[ROW_LAM:ROW_LAM + 1, :]
    return jnp.maximum(z, 0.0) + jnp.log(1.0 + jnp.exp(-jnp.abs(z)))


def _lru_gates(e_lru, pv, wa_ref, wx_ref, tm):
    xc = pv[ROW_CB:ROW_CB + 1, :]
    for k in range(4):
        xc = xc + e_lru[pl.ds(HALO - 3 + k, tm), :] * pv[ROW_CW + k:ROW_CW + k + 1, :]
    xcb = xc.astype(BF16)
    r = _sigmoid(_bd(xcb, wa_ref) + pv[ROW_BA:ROW_BA + 1, :])
    ig = _sigmoid(_bd(xcb, wx_ref) + pv[ROW_BX:ROW_BX + 1, :])
    return xc, r, ig, (-LRU_C * r) * _softplus_neg_lambda(pv)


def _lru_decay(la):
    a = jnp.exp(la)
    om = -jnp.tanh(la) * (1.0 + a * a)
    omc = jnp.maximum(om, 1e-12)
    rmult = lax.rsqrt(omc)
    return a, om, omc * rmult, rmult


def _over_count(v, w, inv_head):
    return jnp.concatenate([v[0:HALO] * inv_head, v[HALO:] * (1.0 / w)], axis=0)


def _pool_pre(e_pool, pv, wp_ref, tm, t0):
    t_head = t0 + lax.broadcasted_iota(jnp.int32, (HALO, 1), 0)
    parts, inv_heads = [], []
    for g, w in enumerate(POOL_WINDOWS):
        ext = e_pool[:, pl.ds(128 * g, 128)]
        s = _window_sum(ext, w, back=True)[HALO:, :]
        inv_head = 1.0 / jnp.minimum(t_head + 1, w).astype(F32)
        inv_heads.append(inv_head)
        parts.append(_over_count(s, w, inv_head) - ext[HALO:, :])
    pooled = jnp.concatenate(parts, axis=1)
    pooled_b = pooled.astype(BF16)
    zp = _bd(pooled_b, wp_ref) + pv[ROW_PB:ROW_PB + 1, :]
    return pooled_b, zp, inv_heads


def _scan_tile(a_ref, b_ref, out_ref, carry, tm, reverse):
    row = lax.broadcasted_iota(jnp.int32, (8, LRU_W), 0)
    nblk = tm // 8

    def local_scan(blk):
        r0 = pl.multiple_of(blk * 8, 8)
        av = a_ref[pl.ds(r0, 8), :]
        bv = b_ref[pl.ds(r0, 8), :]
        for d in (1, 2, 4):
            sh = (8 - d) if reverse else d
            a_s = pltpu.roll(av, sh, 0)
            b_s = pltpu.roll(bv, sh, 0)
            m = (row < 8 - d) if reverse else (row >= d)
            bv = jnp.where(m, av * b_s + bv, bv)
            av = jnp.where(m, av * a_s, av)
        return r0, av, bv

    def step(i, hin):
        local = [local_scan((nblk - 1 - (i * SCAN_UNROLL + j)) if reverse else (i * SCAN_UNROLL + j))
                 for j in range(SCAN_UNROLL)]
        for r0, av, bv in local:
            hv = av * hin + bv
            out_ref[pl.ds(r0, 8), :] = hv
            hin = jnp.broadcast_to(hv[0:1, :] if reverse else hv[7:8, :], (8, LRU_W))
        return hin

    return lax.fori_loop(0, nblk // SCAN_UNROLL, step, carry)


MESH = pl.DeviceIdType.MESH
ANY = pl.BlockSpec(memory_space=pl.ANY)


def _place():
    x, y, c = lax.axis_index("x"), lax.axis_index("y"), lax.axis_index("c")
    chips = [(1 - x, y), (x, 1 - y), (1 - x, 1 - y)]
    return x, y, c, chips


def _meet(peers):
    _announce(peers)
    _await(peers)


def _announce(peers):
    for peer in peers:
        pl.semaphore_signal(pltpu.get_barrier_semaphore(), inc=1, device_id=peer, device_id_type=MESH)


def _await(peers):
    pl.semaphore_wait(pltpu.get_barrier_semaphore(), len(peers))


class _Gather:
    def __init__(self, ins, outs, send_sems, recv_sems, local_sems, core_major=False):
        self.ins, self.outs, self.n = ins, outs, len(ins)
        self.send_sems, self.recv_sems, self.local_sems = send_sems, recv_sems, local_sems
        self.core_major = core_major

    @staticmethod
    def scratch(n):
        return [pltpu.SemaphoreType.DMA((8, n)), pltpu.SemaphoreType.DMA((8, n)), pltpu.SemaphoreType.DMA((n,))]

    def _slot(self, a, px, py, pc):
        return self.outs[a].at[4 * pc + 2 * px + py if self.core_major else 4 * px + 2 * py + pc]

    def _half(self, a, h):
        rows = self.ins[a].shape[0]
        if rows % RELAY_SPLIT_ROWS:
            return None if h else (0, rows)
        return (h * (rows // 2), rows // 2)

    def _copy(self, a, k, block, to, src=None, rows=None):
        dst = self._slot(a, *block)
        src = dst if src is None else src
        if rows is not None:
            src, dst = src.at[pl.ds(*rows)], dst.at[pl.ds(*rows)]
        return pltpu.make_async_remote_copy(
            src_ref=src, dst_ref=dst, send_sem=self.send_sems.at[k, a], recv_sem=self.recv_sems.at[k, a],
            device_id=to, device_id_type=MESH)

    def _mine(self, a):
        x, y, c, _ = _place()
        return pltpu.make_async_copy(self.ins[a], self._slot(a, x, y, c), self.local_sems.at[a])

    def _first(self, a):
        x, y, c, chips = _place()
        me = (x, y, c)
        return ([self._copy(a, 0, me, (x, y, 1 - c), src=self.ins[a])]
                + [self._copy(a, 1 + j, me, (*chip, c), src=self.ins[a]) for j, chip in enumerate(chips[:2])])

    def _passed_on(self, a, h):
        x, y, c, chips = _place()
        block = (*chips[h], c)
        out = [self._copy(a, 4 + h, block, (x, y, 1 - c))]
        if self._half(a, h) is not None:
            out.append(self._copy(a, (3, 7)[h], block, (*chips[1 - h], c), rows=self._half(a, h)))
        return out

    def _peers(self):
        x, y, c, chips = _place()
        return [(x, y, 1 - c), (*chips[0], c), (*chips[1], c)]

    def announce(self):
        _announce(self._peers())

    def start(self, announced=False):
        (_await if announced else _meet)(self._peers())
        for a in range(self.n):
            self._mine(a).start()
        for a in range(self.n):
            for cp in self._first(a):
                cp.start()

    def relay(self, arrays=None):
        x, y, c, chips = _place()
        for h in range(2):
            for a in range(self.n) if arrays is None else arrays:
                self._copy(a, 1 + h, (*chips[h], c), (x, y, c)).wait_recv()
                for cp in self._passed_on(a, h):
                    cp.start()

    def finish(self):
        x, y, c, chips = _place()
        me, sibling = (x, y, c), (x, y, 1 - c)
        passed = []
        for a in range(self.n):
            for h in range(2):
                if self._half(a, h) is not None:
                    self._copy(a, (3, 7)[h], (*chips[2], c), me, rows=self._half(a, h)).wait_recv()
            fwd = self._copy(a, 6, (*chips[2], c), sibling)
            fwd.start()
            passed.append(fwd)
        for a in range(self.n):
            self._copy(a, 0, (x, y, 1 - c), me).wait_recv()
            for j, chip in enumerate(chips):
                self._copy(a, 4 + j, (*chip, 1 - c), me).wait_recv()
        for a in range(self.n):
            for cp in self._first(a) + self._passed_on(a, 0) + self._passed_on(a, 1):
                cp.wait_send()
        for cp in passed:
            cp.wait_send()
        for a in range(self.n):
            self._mine(a).wait()


def _half_exchange(arr, name):
    def body(in_ref, out_ref, send_sems, recv_sems, local_sem):
        x, y, c, _ = _place()
        my_chip = 2 * x + y
        _meet([((x + dx) % 2, (y + dy) % 2, (c + dc) % 2)
               for dx in range(2) for dy in range(2) for dc in range(2) if dx + dy + dc])

        def send(j, wait):
            to_me = (c == x) & (y == j // 2) & (c == j % 2)

            @pl.when(to_me)
            def _():
                local = pltpu.make_async_copy(in_ref.at[j], out_ref.at[my_chip], local_sem)
                local.wait() if wait else local.start()

            @pl.when(jnp.logical_not(to_me))
            def _():
                remote = pltpu.make_async_remote_copy(
                    src_ref=in_ref.at[j], dst_ref=out_ref.at[my_chip], send_sem=send_sems.at[j],
                    recv_sem=recv_sems.at[my_chip], device_id=(c, j // 2, j % 2), device_id_type=MESH)
                remote.wait_send() if wait else remote.start()

        for j in range(4):
            send(j, wait=False)
        for j in range(4):
            send(j, wait=True)
        for k in range(4):
            from_me = (k // 2 == x) & (k % 2 == y) & (c == x)

            @pl.when(jnp.logical_not(from_me))
            def _():
                pltpu.make_async_remote_copy(
                    src_ref=in_ref.at[0], dst_ref=out_ref.at[k], send_sem=send_sems.at[0], recv_sem=recv_sems.at[k],
                    device_id=(k // 2, k % 2, x), device_id_type=MESH).wait_recv()

    return pl.pallas_call(
        body, name=name, out_shape=_sds(arr.shape, arr.dtype), in_specs=[ANY], out_specs=ANY,
        scratch_shapes=[pltpu.SemaphoreType.DMA((4,)), pltpu.SemaphoreType.DMA((4,)), pltpu.SemaphoreType.DMA],
        compiler_params=pltpu.CompilerParams(collective_id=COLLECTIVE_IDS[name]),
    )(arr)


class _ChipExchange:
    def __init__(self, ins, outs, send_sems, recv_sems, local_sems):
        self.ins, self.outs, self.n = ins, outs, len(ins)
        self.send_sems, self.recv_sems, self.local_sems = send_sems, recv_sems, local_sems

    @staticmethod
    def scratch(n):
        return [pltpu.SemaphoreType.DMA((3, n)), pltpu.SemaphoreType.DMA((3, n)), pltpu.SemaphoreType.DMA((n,))]

    def _local(self, a):
        x, y, _, _ = _place()
        me = 2 * x + y
        return pltpu.make_async_copy(self.ins[a].at[me], self.outs[a].at[me], self.local_sems.at[a])

    def _copies(self, a):
        x, y, c, chips = _place()
        me = 2 * x + y
        return [(pltpu.make_async_remote_copy(
                     src_ref=self.ins[a].at[2 * px + py], dst_ref=self.outs[a].at[me],
                     send_sem=self.send_sems.at[k, a], recv_sem=self.recv_sems.at[k, a],
                     device_id=(px, py, c), device_id_type=MESH),
                 pltpu.make_async_remote_copy(
                     src_ref=self.ins[a].at[me], dst_ref=self.outs[a].at[2 * px + py],
                     send_sem=self.send_sems.at[k, a], recv_sem=self.recv_sems.at[k, a],
                     device_id=(px, py, c), device_id_type=MESH))
                for k, (px, py) in enumerate(chips)]

    def _peers(self):
        _, _, c, chips = _place()
        return [(*chip, c) for chip in chips]

    def announce(self):
        _announce(self._peers())

    def start(self):
        _await(self._peers())
        for a in range(self.n):
            self._local(a).start()
        for a in range(self.n):
            for send, _ in self._copies(a):
                send.start()

    def finish(self):
        for a in range(self.n):
            for send, recv in self._copies(a):
                send.wait_send()
                recv.wait_recv()
        for a in range(self.n):
            self._local(a).wait()


def _gathering(body, n_steps, n_s, core_major, start_at=0):
    def wrapped(*refs, n_in, n_out):
        ins, sh_in = refs[:n_in], refs[n_in:n_in + n_s]
        outs, sh_out = refs[n_in + n_s:n_in + n_s + n_out], refs[n_in + n_s + n_out:n_in + 2 * n_s + n_out]
        rest = refs[n_in + 2 * n_s + n_out:]
        gather = _Gather(sh_in, sh_out, *rest[len(rest) - 3:], core_major=core_major)
        i = pl.program_id(0)

        if start_at:
            @pl.when(i == 0)
            def _():
                gather.announce()

        @pl.when(i == start_at)
        def _():
            gather.start(announced=start_at > 0)

        @pl.when(i == n_steps // 2)
        def _():
            gather.relay()

        body(*ins, *outs, *rest[:len(rest) - 3])

        @pl.when(i == n_steps - 1)
        def _():
            gather.finish()

    return wrapped


def _norm_in(x, g_mix, shards, tm):
    T = x.shape[0]
    n_t = T // tm
    n_s = len(shards)

    def norm(x_ref, g_ref, h_ref):
        xv = x_ref[...]
        h_ref[...] = (xv * _rstd(xv) * g_ref[...]).astype(BF16)

    outs = pl.pallas_call(
        functools.partial(_gathering(norm, n_t, n_s, core_major=False), n_in=2, n_out=1), name="norm_in", grid=(n_t,),
        in_specs=[pl.BlockSpec((tm, D_MODEL), lambda i: (i, 0)), pl.BlockSpec((1, D_MODEL), lambda i: (0, 0))]
        + [ANY] * n_s,
        out_specs=[pl.BlockSpec((tm, D_MODEL), lambda i: (i, 0))] + [ANY] * n_s,
        out_shape=[_sds((T, D_MODEL), BF16)] + [_sds((N_DEV,) + a.shape, a.dtype) for a in shards],
        scratch_shapes=_Gather.scratch(n_s),
        compiler_params=pltpu.CompilerParams(dimension_semantics=("arbitrary",), collective_id=COLLECTIVE_IDS["norm_in"]),
    )(x, g_mix, *shards)
    return outs[0], list(outs[1:])


def _mix_in(h1, w_in_t, tm, shards):
    T = h1.shape[0]
    n_t = T // tm
    n_s = len(shards)

    def project(h_ref, w_ref, u_ref):
        u_ref[...] = lax.dot_general(h_ref[...], w_ref[...], NT, preferred_element_type=F32)

    outs = pl.pallas_call(
        functools.partial(_gathering(project, n_t, n_s, core_major=True, start_at=1), n_in=2, n_out=1), name="mix_in",
        grid=(n_t,),
        in_specs=[pl.BlockSpec((tm, D_MODEL), lambda i: (i, 0)), pl.BlockSpec((D_IN, D_MODEL), lambda i: (0, 0))]
        + [ANY] * n_s,
        out_specs=[pl.BlockSpec((tm, D_IN), lambda i: (i, 0))] + [ANY] * n_s,
        out_shape=[_sds((T, D_IN), F32)] + [_sds((N_DEV,) + a.shape, a.dtype) for a in shards],
        scratch_shapes=_Gather.scratch(n_s),
        compiler_params=pltpu.CompilerParams(dimension_semantics=("arbitrary",), collective_id=COLLECTIVE_IDS["mix_in"]),
    )(h1, w_in_t, *shards)
    return outs[0], list(outs[1:])


def _mixer_fwd(u, x, pv, wa, wx, wp, w_out_b, g_ffn, tm, shards=()):
    T = u.shape[0]
    n_s = len(shards)
    n_t = T // tm

    def body(u_ref, x_ref, pv_ref, wa_in, wx_in, wp_in, wo_ref, gf_ref, *rest):
        sh_in, rest = rest[:n_s], rest[n_s:]
        y_ref, hs_ref, hres_ref, h2_ref, saved_ref = rest[:5]
        sh_out, rest = rest[5:5 + n_s], rest[5 + n_s:]
        e_lru, e_pool, a_s, b_s, hc, wa_ref, wx_ref, wp_ref = rest[:8]
        gather = _Gather(sh_in, sh_out, *rest[8:], core_major=True) if n_s else None
        i = pl.program_id(0)

        @pl.when(i == 0)
        def _():
            if gather:
                gather.start()
            e_lru[pl.ds(0, HALO), :] = jnp.zeros((HALO, LRU_W), F32)
            e_pool[pl.ds(0, HALO), :] = jnp.zeros((HALO, POOL_W), F32)
            hc[...] = jnp.zeros((8, LRU_W), F32)
            _fill_block_diag(wa_ref, wa_in)
            _fill_block_diag(wx_ref, wx_in)
            _fill_block_diag(wp_ref, wp_in)

        if gather:
            for a in range(n_s):
                @pl.when(i == ((5 + 3 * a) * n_t) // 16)
                def _(a=a):
                    gather.relay([a])

        e_lru[pl.ds(HALO, tm), :] = u_ref[:, 0:LRU_W]
        e_pool[pl.ds(HALO, tm), :] = u_ref[:, 2 * LRU_W:D_IN]
        pv = pv_ref[...]
        xc, r, ig, la = _lru_gates(e_lru, pv, wa_ref, wx_ref, tm)
        for q, val in enumerate((xc, r, ig, la)):
            saved_ref[:, LRU_W * q:LRU_W * (q + 1)] = val
        a, _, mult, _ = _lru_decay(la)
        a_s[...] = a
        b_s[...] = mult * (ig * xc)
        hc[...] = _scan_tile(a_s, b_s, hs_ref, hc[...], tm, reverse=False)
        gl, _ = _gelu_parts(u_ref[:, LRU_W:2 * LRU_W])
        y_lru = hs_ref[...] * gl
        _, zp, _ = _pool_pre(e_pool, pv, wp_ref, tm, i * tm)
        y_pool = zp * pv[ROW_PS:ROW_PS + 1, :]
        yn = jnp.concatenate([y_lru * _rstd(y_lru) * pv[ROW_GL:ROW_GL + 1, :],
                              y_pool * _rstd(y_pool) * pv[ROW_GP:ROW_GP + 1, :]], axis=1).astype(BF16)
        for b in range(N_DEV):
            y_ref[:, 128 * _y_pos(b):128 * (_y_pos(b) + 1)] = yn[:, 128 * b:128 * (b + 1)]
        hr = x_ref[...] + jnp.dot(y_ref[...], wo_ref[...], preferred_element_type=F32)
        hres_ref[...] = hr
        h2_ref[...] = (hr * _rstd(hr) * gf_ref[...]).astype(BF16)
        e_lru[pl.ds(0, HALO), :] = e_lru[pl.ds(tm, HALO), :]
        e_pool[pl.ds(0, HALO), :] = e_pool[pl.ds(tm, HALO), :]

        if gather:
            @pl.when(i == n_t - 1)
            def _():
                gather.finish()

    full = lambda shape: pl.BlockSpec(shape, lambda i: (0,) * len(shape))
    row = lambda w: pl.BlockSpec((tm, w), lambda i: (i, 0))
    outs = pl.pallas_call(
        body, name="mixer_fwd", grid=(n_t,),
        in_specs=[row(D_IN), row(D_MODEL), full((16, LRU_W)), full((8, 64, 64)), full((8, 64, 64)), full((4, 128, 128)),
                  full((D_MODEL, D_MODEL)), full((1, D_MODEL))] + [ANY] * n_s,
        out_specs=[row(D_MODEL), row(LRU_W), row(D_MODEL), row(D_MODEL), row(4 * LRU_W)] + [ANY] * n_s,
        out_shape=[_sds((T, D_MODEL), BF16), _sds((T, LRU_W), F32), _sds((T, D_MODEL), F32), _sds((T, D_MODEL), BF16),
                   _sds((T, 4 * LRU_W), F32)] + [_sds((N_DEV,) + a.shape, a.dtype) for a in shards],
        scratch_shapes=[pltpu.VMEM((HALO + tm, LRU_W), F32), pltpu.VMEM((HALO + tm, POOL_W), F32),
                        pltpu.VMEM((tm, LRU_W), F32), pltpu.VMEM((tm, LRU_W), F32), pltpu.VMEM((8, LRU_W), F32)]
        + [pltpu.VMEM((2, 256, 256), BF16)] * 3 + (_Gather.scratch(n_s) if n_s else []),
        compiler_params=pltpu.CompilerParams(dimension_semantics=("arbitrary",),
                                             collective_id=COLLECTIVE_IDS["mixer_fwd"] if n_s else None),
    )(u, x, pv, wa, wx, wp, w_out_b, g_ffn, *shards)
    return outs[0], outs[1], outs[2], outs[3], outs[4], list(outs[5:])


def _ffn_fwd(hres, h2, w1_b, w3_b, w2_b, g_fin, tgt, tm):
    T = hres.shape[0]

    def body(hres_ref, h2_ref, w1_ref, w3_ref, w2_ref, gfin_ref, tgt_ref,
             g_ref, v_ref, ff_ref, d3_ref, d3b_ref, loss_ref, dgfin_ref):
        @pl.when(pl.program_id(0) == 0)
        def _():
            loss_ref[...] = jnp.zeros((8, 128), F32)
            dgfin_ref[...] = jnp.zeros((1, D_MODEL), F32)

        h2 = h2_ref[...]
        h3 = hres_ref[...]
        for lo, hi in FF_CHUNKS:
            g = lax.dot_general(h2, w1_ref[lo:hi, :], NT, preferred_element_type=F32)
            v = lax.dot_general(h2, w3_ref[lo:hi, :], NT, preferred_element_type=F32)
            g_ref[:, lo:hi] = g.astype(BF16)
            v_ref[:, lo:hi] = v.astype(BF16)
            ff = ((g * _sigmoid(g)) * v).astype(BF16)
            ff_ref[:, lo:hi] = ff
            h3 = h3 + jnp.dot(ff, w2_ref[lo:hi, :], preferred_element_type=F32)

        rstd = _rstd(h3)
        xh = h3 * rstd
        gfin = gfin_ref[...]
        err = xh * gfin - tgt_ref[...]
        loss_ref[...] += 0.5 * jnp.sum(jnp.mean(err * err, axis=-1, keepdims=True))
        dx, dgain = _rms_bwd(err * (1.0 / D_MODEL), xh, rstd, gfin)
        d3_ref[...] = dx
        d3b_ref[...] = dx.astype(BF16)
        dgfin_ref[...] += dgain

    row = lambda w: pl.BlockSpec((tm, w), lambda i: (i, 0))
    const = lambda shape: pl.BlockSpec(shape, lambda i: (0,) * len(shape))
    weight = pl.BlockSpec((D_FF, D_MODEL), lambda i: (0, 0), pipeline_mode=pl.Buffered(1))
    return pl.pallas_call(
        body, name="ffn_fwd", grid=(T // tm,),
        in_specs=[row(D_MODEL), row(D_MODEL), weight, weight, weight, const((1, D_MODEL)), row(D_MODEL)],
        out_specs=[row(D_FF), row(D_FF), row(D_FF), row(D_MODEL), row(D_MODEL), const((8, 128)), const((1, D_MODEL))],
        out_shape=[_sds((T, D_FF), BF16), _sds((T, D_FF), BF16), _sds((T, D_FF), BF16),
                   _sds((T, D_MODEL), F32), _sds((T, D_MODEL), BF16), _sds((8, 128), F32), _sds((1, D_MODEL), F32)],
        compiler_params=pltpu.CompilerParams(dimension_semantics=("arbitrary",)),
    )(hres, h2, w1_b, w3_b, w2_b, g_fin, tgt)


def _ffn_bwd(d3, g, v, w1_b, w3_b, w2_b, hres, g_ffn, tm):
    T = d3.shape[0]

    def body(d3_ref, g_ref, v_ref, w1_ref, w3_ref, w2_ref, hres_ref, gf_ref, dg_ref, dv_ref, d2_ref, dgffn_ref):
        @pl.when(pl.program_id(0) == 0)
        def _():
            dgffn_ref[...] = jnp.zeros((1, D_MODEL), F32)

        d3 = d3_ref[...]
        d3b = d3.astype(BF16)
        dh2 = jnp.zeros((tm, D_MODEL), F32)
        for lo, hi in FF_CHUNKS:
            dff = lax.dot_general(d3b, w2_ref[lo:hi, :], NT, preferred_element_type=F32)
            gv = g_ref[:, lo:hi].astype(F32)
            vv = v_ref[:, lo:hi].astype(F32)
            sg = _sigmoid(gv)
            sl = gv * sg
            dgb = (dff * vv * (sg * (1.0 + gv * (1.0 - sg)))).astype(BF16)
            dvb = (dff * sl).astype(BF16)
            dg_ref[:, lo:hi] = dgb
            dv_ref[:, lo:hi] = dvb
            dh2 = dh2 + (jnp.dot(dgb, w1_ref[lo:hi, :], preferred_element_type=F32)
                         + jnp.dot(dvb, w3_ref[lo:hi, :], preferred_element_type=F32))

        hr = hres_ref[...]
        rstd = _rstd(hr)
        dx, dgain = _rms_bwd(dh2, hr * rstd, rstd, gf_ref[...])
        d2_ref[...] = d3 + dx
        dgffn_ref[...] += dgain

    row = lambda w: pl.BlockSpec((tm, w), lambda i: (i, 0))
    const = lambda shape: pl.BlockSpec(shape, lambda i: (0,) * len(shape))
    weight = pl.BlockSpec((D_FF, D_MODEL), lambda i: (0, 0), pipeline_mode=pl.Buffered(1))
    return pl.pallas_call(
        body, name="ffn_bwd", grid=(T // tm,),
        in_specs=[row(D_MODEL), row(D_FF), row(D_FF), weight, weight, weight, row(D_MODEL), const((1, D_MODEL))],
        out_specs=[row(D_FF), row(D_FF), row(D_MODEL), const((1, D_MODEL))],
        out_shape=[_sds((T, D_FF), BF16), _sds((T, D_FF), BF16), _sds((T, D_MODEL), F32), _sds((1, D_MODEL), F32)],
        compiler_params=pltpu.CompilerParams(dimension_semantics=("arbitrary",)),
    )(d3, g, v, w1_b, w3_b, w2_b, hres, g_ffn)


def _at_b_pair(a, b, c_arr, name, tk, gather=()):
    T, M = a.shape
    N = b.shape[1]
    hm, n_k = M // 2, T // tk
    n_g = len(gather)

    def body(c_ref, a_ref, b_ref, *rest):
        g_in, o_ref, rest = rest[:n_g], rest[n_g], rest[n_g + 1:]
        g_out, rest = rest[:n_g], rest[n_g:]
        acc, landed, send_sem, recv_sem = rest[:4]
        ag = _Gather(g_in, g_out, *rest[4:]) if n_g else None
        ph, k = pl.program_id(0), pl.program_id(1)

        def hand_over():
            x, y, c, _ = _place()
            return pltpu.make_async_remote_copy(
                src_ref=acc.at[0], dst_ref=landed, send_sem=send_sem, recv_sem=recv_sem,
                device_id=(x, y, 1 - c), device_id_type=MESH)

        if ag:
            @pl.when((ph == 0) & (k == 0))
            def _():
                ag.start()

            @pl.when((ph == 1) & (k == 0))
            def _():
                ag.relay()
        else:
            barrier = pltpu.get_barrier_semaphore()

            @pl.when((ph == 0) & (k == 0))
            def _():
                x, y, c, _ = _place()
                pl.semaphore_signal(barrier, inc=1, device_id=(x, y, 1 - c), device_id_type=MESH)

        @pl.when(k == 0)
        def _():
            acc[ph] = jnp.zeros((hm, N), F32)

        acc[ph] += lax.dot_general(a_ref[...].astype(BF16), b_ref[...].astype(BF16), TN, preferred_element_type=F32)

        @pl.when((ph == 0) & (k == n_k - 1))
        def _():
            if not ag:
                pl.semaphore_wait(barrier, 1)
            hand_over().start()

        @pl.when((ph == 1) & (k == n_k - 1))
        def _():
            copy = hand_over()
            copy.wait_recv()
            o_ref[...] = (acc[1] + landed[...]).astype(BF16)
            copy.wait_send()
            if ag:
                ag.finish()

    outs = pl.pallas_call(
        body, name=name,
        grid_spec=pltpu.PrefetchScalarGridSpec(
            num_scalar_prefetch=1, grid=(2, n_k),
            in_specs=[pl.BlockSpec((tk, hm), lambda ph, k, c_ref: (k, (ph + 1 - c_ref[0]) % 2)),
                      pl.BlockSpec((tk, N), lambda ph, k, c_ref: (k, 0))] + [ANY] * n_g,
            out_specs=[pl.BlockSpec((hm, N), lambda ph, k, c_ref: (0, 0))] + [ANY] * n_g,
            scratch_shapes=[pltpu.VMEM((2, hm, N), F32), pltpu.VMEM((hm, N), F32),
                            pltpu.SemaphoreType.DMA, pltpu.SemaphoreType.DMA] + (_Gather.scratch(n_g) if n_g else [])),
        out_shape=[_sds((hm, N), BF16)] + [_sds((N_DEV,) + g.shape, g.dtype) for g in gather],
        compiler_params=pltpu.CompilerParams(dimension_semantics=("arbitrary", "arbitrary"),
                                             collective_id=COLLECTIVE_IDS[name]),
    )(c_arr, a, b, *gather)
    return (outs[0], list(outs[1:])) if n_g else outs[0]


def _mixer_bwd(d2, u, hs, saved, pv, wa, wx, wp, w_out_b, tm, chip_sums=()):
    T = u.shape[0]
    n_t = T // tm
    n_x = len(chip_sums)

    def body(d2_ref, u_ref, uh_ref, hs_ref, hh_ref, saved_ref, pv_ref, wa_in, wx_in, wp_in, wo_ref, *rest):
        x_in, rest = rest[:n_x], rest[n_x:]
        du_ref, sg_ref = rest[:2]
        x_out, rest = rest[2:2 + n_x], rest[2 + n_x:]
        e_pool, e_h, a_s, b_s, dh_s, mu_s, f_x, f_p, mc, cx, cp = rest[:11]
        wa_ref, wx_ref, wp_ref, vacc_ref, dwa_ref, dwx_ref, dwp_ref = rest[11:18]
        exchange = _ChipExchange(x_in, x_out, *rest[18:]) if n_x else None
        s = pl.program_id(0)
        it = n_t - 1 - s

        @pl.when(s == 0)
        def _():
            if exchange:
                exchange.announce()
            mc[...] = jnp.zeros((8, LRU_W), F32)
            cx[...] = jnp.zeros((8, LRU_W), F32)
            cp[...] = jnp.zeros((HALO, POOL_W), F32)
            vacc_ref[...] = jnp.zeros((16, LRU_W), F32)
            dwa_ref[...] = jnp.zeros((2, 256, 256), F32)
            dwx_ref[...] = jnp.zeros((2, 256, 256), F32)
            dwp_ref[...] = jnp.zeros((2, 256, 256), F32)
            _fill_block_diag(wa_ref, wa_in)
            _fill_block_diag(wx_ref, wx_in)
            _fill_block_diag(wp_ref, wp_in)

        if exchange:
            @pl.when(s == 1)
            def _():
                exchange.start()

        first = it == 0
        e_pool[pl.ds(0, HALO), :] = jnp.where(first, 0.0, uh_ref[...])
        e_pool[pl.ds(HALO, tm), :] = u_ref[:, 2 * LRU_W:D_IN]
        e_h[pl.ds(0, 8), :] = jnp.where(first, 0.0, hh_ref[...])
        e_h[pl.ds(8, tm), :] = hs_ref[...]
        pv = pv_ref[...]
        saved = lambda q: saved_ref[:, LRU_W * q:LRU_W * (q + 1)]

        dyn = lax.dot_general(d2_ref[...].astype(BF16), wo_ref[...], NT, preferred_element_type=F32)
        dyn = jnp.concatenate([dyn[:, 128 * _y_pos(b):128 * (_y_pos(b) + 1)] for b in range(N_DEV)], axis=1)

        h = hs_ref[...]
        ug = u_ref[:, LRU_W:2 * LRU_W]
        gl, dgl = _gelu_parts(ug)
        y_lru = h * gl
        rstd_l = _rstd(y_lru)
        dy_lru, d_gain_l = _rms_bwd(dyn[:, 0:LRU_W], y_lru * rstd_l, rstd_l, pv[ROW_GL:ROW_GL + 1, :])
        dh = dy_lru * gl
        du_ref[:, LRU_W:2 * LRU_W] = (dy_lru * h * dgl).astype(BF16)
        a_s[...] = jnp.exp(saved(3))
        b_s[...] = a_s[...] * dh
        dh_s[...] = dh
        mu_s[pl.ds(tm, 8), :] = mc[...]
        mc[...] = _scan_tile(a_s, b_s, mu_s, mc[...], tm, reverse=True)
        xc, r, ig = saved(0), saved(1), saved(2)
        a, om, mult, rmult = _lru_decay(saved(3))
        lam_t = dh_s[...] + mu_s[pl.ds(1, tm), :]
        da = lam_t * e_h[pl.ds(7, tm), :]
        dmult = lam_t * (ig * xc)
        di = lam_t * (mult * xc)
        dxc = lam_t * (mult * ig)
        dla = da * a - jnp.where(om > 1e-12, dmult * ((a * a) * rmult), 0.0)
        dra = (dla * (-LRU_C * _softplus_neg_lambda(pv))) * (r * (1.0 - r))
        dia = di * (ig * (1.0 - ig))
        drab = dra.astype(BF16)
        diab = dia.astype(BF16)
        xcb = xc.astype(BF16)
        dxc = dxc + _bd_t(drab, wa_ref) + _bd_t(diab, wx_ref)
        dwa_ref[...] += _bd_grad(xcb, drab)
        dwx_ref[...] += _bd_grad(xcb, diab)
        sig_neg_lam = _sigmoid(-pv[ROW_LAM:ROW_LAM + 1, :])
        d_lam = jnp.sum(dla * r, axis=0, keepdims=True) * (LRU_C * sig_neg_lam)

        f_x[pl.ds(0, tm), :] = dxc
        f_x[pl.ds(tm, 8), :] = cx[...]
        du_lru = jnp.zeros((tm, LRU_W), F32)
        u_lru = u_ref[:, 0:LRU_W]
        d_cw = []
        for k in range(4):
            later = f_x[pl.ds(3 - k, tm), :]
            du_lru = du_lru + later * pv[ROW_CW + k:ROW_CW + k + 1, :]
            d_cw.append(jnp.sum(later * u_lru, axis=0, keepdims=True))
        du_ref[:, 0:LRU_W] = du_lru.astype(BF16)
        cx[...] = f_x[pl.ds(0, 8), :]

        pooled_b, zp, inv_cnts = _pool_pre(e_pool, pv, wp_ref, tm, it * tm)
        ps = pv[ROW_PS:ROW_PS + 1, :]
        y_pool = zp * ps
        rstd_p = _rstd(y_pool)
        dy_pool, d_gain_p = _rms_bwd(dyn[:, LRU_W:D_MODEL], y_pool * rstd_p, rstd_p, pv[ROW_GP:ROW_GP + 1, :])
        dz = dy_pool * ps
        dzb = dz.astype(BF16)
        dwp_ref[...] += _bd_grad(pooled_b, dzb)
        dpooled = _bd_t(dzb, wp_ref)
        for g, w in enumerate(POOL_WINDOWS):
            f_p[pl.ds(0, tm), pl.ds(128 * g, 128)] = _over_count(dpooled[:, 128 * g:128 * (g + 1)], w, inv_cnts[g])
        f_p[pl.ds(tm, HALO), :] = cp[...]
        for g, w in enumerate(POOL_WINDOWS):
            acc = _window_sum(f_p[:, pl.ds(128 * g, 128)], w, back=False)[0:tm, :]
            du_ref[:, 2 * LRU_W + 128 * g:2 * LRU_W + 128 * (g + 1)] = (
                acc - dpooled[:, 128 * g:128 * (g + 1)]).astype(BF16)
        cp[...] = f_p[pl.ds(0, HALO), :]

        rows = d_cw + [
            jnp.sum(dxc, axis=0, keepdims=True),
            jnp.sum(dra, axis=0, keepdims=True),
            jnp.sum(dia, axis=0, keepdims=True),
            d_lam,
            jnp.sum(dz, axis=0, keepdims=True),
            jnp.sum(dy_pool * zp, axis=0, keepdims=True),
            d_gain_l, d_gain_p,
            jnp.zeros((4, LRU_W), F32),
        ]
        vacc_ref[...] += jnp.concatenate(rows, axis=0)

        @pl.when(s == n_t - 1)
        def _():
            sg_ref[SG_VEC:SG_VEC + 16, :] = vacc_ref[:, 0:256]
            sg_ref[SG_VEC + 16:SG_VEC + 32, :] = vacc_ref[:, 256:512]
            for half in range(2):
                sg_ref[SG_WA + 64 * half:SG_WA + 64 * (half + 1), :] = _diag_pack(dwa_ref[half], 64)
                sg_ref[SG_WX + 64 * half:SG_WX + 64 * (half + 1), :] = _diag_pack(dwx_ref[half], 64)
                sg_ref[SG_WP + 128 * half:SG_WP + 128 * (half + 1), :] = _diag_pack(dwp_ref[half], 128)
            if exchange:
                exchange.finish()

    rev = lambda w: pl.BlockSpec((tm, w), lambda s: (n_t - 1 - s, 0))
    full = lambda shape: pl.BlockSpec(shape, lambda s: (0,) * len(shape))
    outs = pl.pallas_call(
        body, name="mixer_bwd", grid=(n_t,),
        in_specs=[rev(D_MODEL), rev(D_IN),
                  pl.BlockSpec((HALO, POOL_W), lambda s: (jnp.maximum((n_t - 1 - s) * (tm // HALO) - 1, 0), 2)),
                  rev(LRU_W),
                  pl.BlockSpec((8, LRU_W), lambda s: (jnp.maximum((n_t - 1 - s) * (tm // 8) - 1, 0), 0)),
                  rev(4 * LRU_W), full((16, LRU_W)), full((8, 64, 64)), full((8, 64, 64)), full((4, 128, 128)),
                  full((D_MODEL, D_MODEL))] + [ANY] * n_x,
        out_specs=[rev(D_IN), full((SG_ROWS, 256))] + [ANY] * n_x,
        out_shape=[_sds((T, D_IN), BF16), _sds((SG_ROWS, 256), F32)] + [_sds(a.shape, a.dtype) for a in chip_sums],
        scratch_shapes=[pltpu.VMEM((HALO + tm, POOL_W), F32),
                        pltpu.VMEM((8 + tm, LRU_W), F32)] + [pltpu.VMEM((tm, LRU_W), F32)] * 3 + [
                        pltpu.VMEM((tm + 8, LRU_W), F32), pltpu.VMEM((tm + 8, LRU_W), F32),
                        pltpu.VMEM((tm + HALO, POOL_W), F32), pltpu.VMEM((8, LRU_W), F32),
                        pltpu.VMEM((8, LRU_W), F32), pltpu.VMEM((HALO, POOL_W), F32)]
        + [pltpu.VMEM((2, 256, 256), BF16)] * 3 + [pltpu.VMEM((16, LRU_W), F32)] + [pltpu.VMEM((2, 256, 256), F32)] * 3
        + (_ChipExchange.scratch(n_x) if n_x else []),
        compiler_params=pltpu.CompilerParams(dimension_semantics=("arbitrary",),
                                             collective_id=COLLECTIVE_IDS["mixer_bwd"] if n_x else None),
    )(d2, u, u, hs, hs, saved, pv, wa, wx, wp, w_out_b, *chip_sums)
    return outs[0], outs[1], list(outs[2:])


def _mix_in_bwd(du, x, d2, w_in_t, g_mix, tm):
    T = x.shape[0]

    def body(du_ref, x_ref, d2_ref, w_ref, g_ref, dx_ref, dg_ref):
        @pl.when(pl.program_id(0) == 0)
        def _():
            dg_ref[...] = jnp.zeros((1, D_MODEL), F32)

        dh = jnp.dot(du_ref[...], w_ref[...], preferred_element_type=F32)
        xv = x_ref[...]
        rstd = _rstd(xv)
        dx, dgain = _rms_bwd(dh, xv * rstd, rstd, g_ref[...])
        dx_ref[...] = d2_ref[...] + dx
        dg_ref[...] += dgain

    row = lambda w: pl.BlockSpec((tm, w), lambda i: (i, 0))
    const = lambda shape: pl.BlockSpec(shape, lambda i: (0,) * len(shape))
    return pl.pallas_call(
        body, name="mix_in_bwd", grid=(T // tm,),
        in_specs=[row(D_IN), row(D_MODEL), row(D_MODEL), const((D_IN, D_MODEL)), const((1, D_MODEL))],
        out_specs=[row(D_MODEL), const((1, D_MODEL))],
        out_shape=[_sds((T, D_MODEL), F32), _sds((1, D_MODEL), F32)],
        compiler_params=pltpu.CompilerParams(dimension_semantics=("arbitrary",)),
    )(du, x, d2, w_in_t, g_mix)


def _adamw(w, g, m, v):
    m = ADAM_B1 * m + (1.0 - ADAM_B1) * g
    v = ADAM_B2 * v + (1.0 - ADAM_B2) * (g * g)
    m_hat = m / (1.0 - ADAM_B1 ** ADAM_STEP)
    v_hat = v / (1.0 - ADAM_B2 ** ADAM_STEP)
    delta = -ADAM_LR * (m_hat / (jnp.sqrt(v_hat) + ADAM_EPS) + ADAM_WD * w)
    return delta, m, v


def _adam_shards(ws, ms, vs, parts):
    n = len(ws)
    n_blk = [w.shape[0] // ADAM_ROWS for w in ws]

    def body(*refs):
        w_refs, m_refs, v_refs, p_refs, outs = (refs[:n], refs[n:2 * n], refs[2 * n:3 * n], refs[3 * n:4 * n],
                                                refs[4 * n:])
        i = pl.program_id(0)
        for a in range(n):
            @pl.when(i < n_blk[a])
            def _(a=a):
                g = p_refs[a][0].astype(F32)
                for j in range(1, 4):
                    g = g + p_refs[a][j].astype(F32)
                delta, new_m, new_v = _adamw(w_refs[a][...], g, m_refs[a][...], v_refs[a][...])
                for kind, val in enumerate((g, delta, new_m, new_v)):
                    outs[4 * a + kind][...] = val

    blk = lambda a: pl.BlockSpec((ADAM_ROWS, D_MODEL), lambda i: (jnp.minimum(i, n_blk[a] - 1), 0))
    part_blk = lambda a: pl.BlockSpec((4, ADAM_ROWS, D_MODEL), lambda i: (0, jnp.minimum(i, n_blk[a] - 1), 0))
    res = pl.pallas_call(
        body, name="adam_shards", grid=(max(n_blk),),
        in_specs=[blk(a) for a in range(n)] * 3 + [part_blk(a) for a in range(n)],
        out_specs=[blk(a) for a in range(n) for _ in range(4)],
        out_shape=[_sds(w.shape, F32) for w in ws for _ in range(4)],
        compiler_params=pltpu.CompilerParams(dimension_semantics=("arbitrary",)),
    )(*ws, *ms, *vs, *parts)
    return [tuple(res[4 * a:4 * a + 4]) for a in range(n)]


SMALL_PARAMS = [("norm_mix_g", (1, D_MODEL)), ("conv_w", (1, 4, 64)), ("conv_b", (1, LRU_W)),
                ("gate_a_w", (1, 8, 64, 64)), ("gate_a_b", (1, LRU_W)), ("gate_x_w", (1, 8, 64, 64)),
                ("gate_x_b", (1, LRU_W)), ("lru_lambda", (1, LRU_W)), ("pool_w", (1, 4, 128, 128)),
                ("pool_b", (1, POOL_W)), ("pool_scale", (1, POOL_W)), ("norm_lru_g", (1, LRU_W)),
                ("norm_pool_g", (1, POOL_W)), ("norm_ffn_g", (1, D_MODEL)), ("final_norm_g", (1, D_MODEL))]
VEC_ROW = dict(conv_b=ROW_CB, gate_a_b=ROW_BA, gate_x_b=ROW_BX, lru_lambda=ROW_LAM, pool_b=ROW_PB, pool_scale=ROW_PS,
               norm_lru_g=ROW_GL, norm_pool_g=ROW_GP)
WHOLE = (Ellipsis,)


def _unpack_mixer_grads(sg, dev):
    vec = jnp.concatenate([sg[SG_VEC:SG_VEC + 16], sg[SG_VEC + 16:SG_VEC + 32]], axis=1)
    out = {nm: [(WHOLE, vec[r:r + 1])] for nm, r in VEC_ROW.items()}
    own = jnp.zeros((4, 64), F32)
    for d in range(N_DEV):
        own = jnp.where(dev == d, vec[ROW_CW:ROW_CW + 4, 64 * d:64 * (d + 1)], own)
    out["conv_w"] = [((0,), own)]
    for nm, row0 in (("gate_a_w", SG_WA), ("gate_x_w", SG_WX)):
        out[nm] = [((0, b), sg[row0 + 64 * (b // 4):row0 + 64 * (b // 4 + 1), 64 * (b % 4):64 * (b % 4 + 1)])
                   for b in range(8)]
    out["pool_w"] = [((0, b), sg[SG_WP + 128 * (b // 2):SG_WP + 128 * (b // 2 + 1), 128 * (b % 2):128 * (b % 2 + 1)])
                     for b in range(4)]
    return out


def _adam_small(parts, w, m, v):
    names = [nm for nm, _ in SMALL_PARAMS]
    n = len(names)

    def body(sg_ref, gm_ref, gf_ref, gn_ref, ls_ref, *rest):
        w_refs, m_refs, v_refs, outs = rest[:n], rest[n:2 * n], rest[2 * n:3 * n], rest[3 * n:]
        dev = 4 * lax.axis_index("x") + 2 * lax.axis_index("y") + lax.axis_index("c")

        def total(ref):
            acc = ref[0]
            for d in range(1, N_DEV):
                acc = acc + ref[d]
            return acc

        pieces = _unpack_mixer_grads(total(sg_ref), dev)
        pieces["norm_mix_g"] = [(WHOLE, total(gm_ref))]
        pieces["norm_ffn_g"] = [(WHOLE, total(gf_ref))]
        pieces["final_norm_g"] = [(WHOLE, total(gn_ref))]
        for i, nm in enumerate(names):
            for idx, g in pieces[nm]:
                delta, new_m, new_v = _adamw(w_refs[i][idx], g, m_refs[i][idx], v_refs[i][idx])
                for kind, val in enumerate((g, delta, new_m, new_v)):
                    outs[4 * i + kind][idx] = val
        outs[4 * n][...] = total(ls_ref)

    shapes = [_sds(shape, F32) for _, shape in SMALL_PARAMS for _ in range(4)] + [_sds((8, 128), F32)]
    res = pl.pallas_call(body, name="adam_small", out_shape=shapes)(
        *parts, *[w[nm] for nm in names], *[m[nm] for nm in names], *[v[nm] for nm in names])
    return {nm: tuple(res[4 * i:4 * i + 4]) for i, nm in enumerate(names)}, res[4 * n][0, 0]


def _vec_rows(conv_w_full, conv_b, ba, bx, lam, pb, ps, gl, gp):
    return jnp.concatenate([conv_w_full, conv_b, ba, bx, lam, pb, ps, gl, gp, jnp.zeros((4, LRU_W), F32)], axis=0)


WEIGHT_ORDER = ['norm_mix_g', 'w_in', 'conv_w', 'conv_b', 'gate_a_w', 'gate_a_b', 'gate_x_w', 'gate_x_b', 'lru_lambda',
                'pool_w', 'pool_b', 'pool_scale', 'norm_lru_g', 'norm_pool_g', 'w_out', 'norm_ffn_g', 'ffn_w1', 'ffn_w3',
                'ffn_w2', 'final_norm_g']


def kernel(x, norm_mix_g, w_in, conv_w, conv_b, gate_a_w, gate_a_b, gate_x_w, gate_x_b, lru_lambda, pool_w, pool_b, pool_scale, norm_lru_g, norm_pool_g, w_out, norm_ffn_g, ffn_w1, ffn_w3, ffn_w2, final_norm_g, loss_target, m_norm_mix_g, m_w_in, m_conv_w, m_conv_b, m_gate_a_w, m_gate_a_b, m_gate_x_w, m_gate_x_b, m_lru_lambda, m_pool_w, m_pool_b, m_pool_scale, m_norm_lru_g, m_norm_pool_g, m_w_out, m_norm_ffn_g, m_ffn_w1, m_ffn_w3, m_ffn_w2, m_final_norm_g, v_norm_mix_g, v_w_in, v_conv_w, v_conv_b, v_gate_a_w, v_gate_a_b, v_gate_x_w, v_gate_x_b, v_lru_lambda, v_pool_w, v_pool_b, v_pool_scale, v_norm_lru_g, v_norm_pool_g, v_w_out, v_norm_ffn_g, v_ffn_w1, v_ffn_w3, v_ffn_w2, v_final_norm_g):
    ac = lax.axis_index("c")
    tm, tmx, tm_in, tk = 512, 512, 1024, 2048
    xs, tgt = x[0], loss_target[0]
    g_fin = final_norm_g.reshape(1, D_MODEL)
    c_arr = jnp.reshape(ac, (1,)).astype(jnp.int32)

    tr = lambda w: jnp.swapaxes(w[0], 0, 1)
    own = lambda w: w[0]
    bf = lambda a: a.astype(BF16)

    h1, (g_in, g_conv) = _norm_in(xs, norm_mix_g, [bf(tr(w_in)), conv_w[0]], tm_in)
    w_in_t = g_in.reshape(D_IN, D_MODEL)
    u, (g_out,) = _mix_in(h1, w_in_t, tm_in, shards=[bf(own(w_out))])
    conv_w_full = g_conv.transpose(1, 0, 2).reshape(4, LRU_W)
    pv = _vec_rows(conv_w_full, conv_b, gate_a_b, gate_x_b, lru_lambda, pool_b, pool_scale, norm_lru_g, norm_pool_g)
    wa, wx, wp = gate_a_w[0], gate_x_w[0], pool_w[0]
    w_out_b = g_out.reshape(D_MODEL, D_MODEL)
    y, hs, hres, h2, saved, (g_w1, g_w3, g_w2) = _mixer_fwd(
        u, xs, pv, wa, wx, wp, w_out_b, norm_ffn_g, tmx, shards=[bf(tr(ffn_w1)), bf(tr(ffn_w3)), bf(own(ffn_w2))])
    w1_t, w3_t, w2_b = g_w1.reshape(D_FF, D_MODEL), g_w3.reshape(D_FF, D_MODEL), g_w2.reshape(D_FF, D_MODEL)
    g, v, ff, d3, d3b, loss_acc, d_gfin = _ffn_fwd(hres, h2, w1_t, w3_t, w2_b, g_fin, tgt, tm)

    dg, dv, d2, d_gffn = _ffn_bwd(d3, g, v, w1_t, w3_t, w2_b, hres, norm_ffn_g, tm)
    chips = lambda a: a.reshape(4, a.shape[0] // 4, a.shape[1])
    early = [(y, d2, "grad_w_out"), (dg, h2, "grad_w1"), (dv, h2, "grad_w3"), (ff, d3b, "grad_w2")]
    early_sums = [chips(_at_b_pair(a, b, c_arr, name, tk)) for a, b, name in early]
    du, d_mixer, early_parts = _mixer_bwd(d2, u, hs, saved, pv, wa, wx, wp, w_out_b, tmx, chip_sums=early_sums)
    grad_x, d_gmix = _mix_in_bwd(du, xs, d2, w_in_t, norm_mix_g, tm_in)
    d_win, small_parts = _at_b_pair(du, h1, c_arr, "grad_w_in", tk,
                                    gather=[d_mixer, d_gmix, d_gffn, d_gfin, loss_acc])
    parts = [_half_exchange(chips(d_win), "grads_to_chips_w_in")] + list(early_parts)

    res = {}
    shard_w = dict(w_in=(w_in, m_w_in, v_w_in, tr), w_out=(w_out, m_w_out, v_w_out, own),
                   ffn_w1=(ffn_w1, m_ffn_w1, v_ffn_w1, tr), ffn_w3=(ffn_w3, m_ffn_w3, v_ffn_w3, tr),
                   ffn_w2=(ffn_w2, m_ffn_w2, v_ffn_w2, own))
    shard_res = _adam_shards([view(w) for w, _, _, view in shard_w.values()],
                             [view(m) for _, m, _, view in shard_w.values()],
                             [view(v) for _, _, v, view in shard_w.values()], parts)
    for (nm, (_, _, _, view)), outs in zip(shard_w.items(), shard_res):
        res[nm] = [(jnp.swapaxes(o, 0, 1) if view is tr else o)[None] for o in outs]

    row = lambda a: a.reshape(1, D_MODEL)
    small = lambda gm, cw, cb, wa_, ba, wx_, bx, lam, pw, pb, ps, gl, gp, gf, gn: dict(
        norm_mix_g=gm, conv_w=cw, conv_b=cb, gate_a_w=wa_, gate_a_b=ba, gate_x_w=wx_, gate_x_b=bx, lru_lambda=lam,
        pool_w=pw, pool_b=pb, pool_scale=ps, norm_lru_g=gl, norm_pool_g=gp, norm_ffn_g=gf, final_norm_g=row(gn))
    small_res, loss = _adam_small(
        small_parts,
        small(norm_mix_g, conv_w, conv_b, gate_a_w, gate_a_b, gate_x_w, gate_x_b, lru_lambda, pool_w, pool_b,
              pool_scale, norm_lru_g, norm_pool_g, norm_ffn_g, final_norm_g),
        small(m_norm_mix_g, m_conv_w, m_conv_b, m_gate_a_w, m_gate_a_b, m_gate_x_w, m_gate_x_b, m_lru_lambda, m_pool_w,
              m_pool_b, m_pool_scale, m_norm_lru_g, m_norm_pool_g, m_norm_ffn_g, m_final_norm_g),
        small(v_norm_mix_g, v_conv_w, v_conv_b, v_gate_a_w, v_gate_a_b, v_gate_x_w, v_gate_x_b, v_lru_lambda, v_pool_w,
              v_pool_b, v_pool_scale, v_norm_lru_g, v_norm_pool_g, v_norm_ffn_g, v_final_norm_g))
    for nm, outs in small_res.items():
        res[nm] = [o.reshape(D_MODEL) for o in outs] if nm == "final_norm_g" else list(outs)

    out = [loss, grad_x[None]]
    for kind in range(4):
        out += [res[nm][kind] for nm in WEIGHT_ORDER]
    return tuple(out)
```
